```python
import jax, jax.numpy as jnp
from jax import lax
import numpy as np

D_MODEL = 1024
BATCH = 16
SEQ = 2048
DEPTH = 2

CHUNK = 64
RW_HEAD_DIM = 64
RW_WIDTH = D_MODEL // 2
RW_HEADS = RW_WIDTH // RW_HEAD_DIM
DECAY_LORA = 64
AAA_LORA = 64
AT_HEAD_DIM = 64
AT_WIDTH = D_MODEL // 2
AT_HEADS = AT_WIDTH // AT_HEAD_DIM
LEFT_CHUNKS = 8
BAND_CHUNKS = LEFT_CHUNKS + 1
REL_CLIP = 2 * CHUNK
SG_CHUNK = 128
SG_WIDTH = D_MODEL
SG_GROUPS = 8
SG_GROUP_DIM = SG_WIDTH // SG_GROUPS
SHIFT_WIDTH = 3 * RW_WIDTH + DECAY_LORA + AAA_LORA
EVEN_IN = SHIFT_WIDTH + RW_WIDTH + 4 * AT_WIDTH
EVEN_MIX = RW_WIDTH + AT_WIDTH
ODD_IN = 3 * SG_WIDTH
RMS_EPS = 1e-6
LN_EPS = 1e-5
GN_EPS = 64e-5
NEG_INF = -1e30

kernel_name = "hybrid_rwkv7_chunkattn_gmlp_encoder"


def rmsnorm(x, g):
    x32 = x.astype(jnp.float32)
    y = x32 * lax.rsqrt(jnp.mean(x32 * x32, axis=-1, keepdims=True) + RMS_EPS)
    return (y * g.astype(jnp.float32)).astype(x.dtype)


def token_shift(p, mu):
    p_prev = jnp.pad(p, ((0, 0), (1, 0), (0, 0)))[:, :-1]
    return p + (p_prev - p) * mu


def rwkv7_mix(p_r, p_k, p_v, p_wd, p_ad, w0, w2, a0, a2, k_k, k_a, r_k, lnx_g, lnx_b):
    B, S, W = p_r.shape
    H, N = RW_HEADS, RW_HEAD_DIM
    f32 = jnp.float32
    r = p_r.astype(f32)
    k = p_k.astype(f32)
    v = p_v.astype(f32)
    w = -jax.nn.softplus(-(w0.astype(f32) + jnp.tanh(p_wd.astype(f32)) @ w2.astype(f32))) - 0.5
    decay = jnp.exp(-jnp.exp(w))
    a = jax.nn.sigmoid(a0.astype(f32) + p_ad.astype(f32) @ a2.astype(f32))
    heads = lambda t: t.reshape(B, S, H, N)
    kk = heads(k * k_k.astype(f32))
    kk = kk / jnp.maximum(jnp.sqrt(jnp.sum(kk * kk, axis=-1, keepdims=True)), 1e-12)
    k = k * (1.0 + (a - 1.0) * k_a.astype(f32))
    r_h, k_h, v_h, a_h = heads(r), heads(k), heads(v), heads(a)
    tm = lambda t: jnp.moveaxis(t, 1, 0)

    def step(state, inp):
        r_t, w_t, k_t, v_t, a_t, b_t = inp
        sa = jnp.einsum('bhvk,bhk->bhv', state, a_t)
        state = (state * w_t[:, :, None, :]
                 + sa[..., None] * b_t[:, :, None, :]
                 + v_t[..., None] * k_t[:, :, None, :])
        y_t = jnp.einsum('bhvk,bhk->bhv', state, r_t)
        return state, y_t

    s0 = jnp.zeros((B, H, N, N), f32)
    _, y = lax.scan(step, s0, (tm(r_h), tm(heads(decay)), tm(k_h), tm(v_h), tm(-kk), tm(kk * a_h)))
    y = jnp.moveaxis(y, 0, 1)
    mu = jnp.mean(y, axis=-1, keepdims=True)
    var = jnp.mean(jnp.square(y - mu), axis=-1, keepdims=True)
    y = ((y - mu) * lax.rsqrt(var + GN_EPS)).reshape(B, S, W)
    y = y * lnx_g.astype(f32) + lnx_b.astype(f32)
    bonus = jnp.sum(r_h * k_h * r_k.astype(f32), axis=-1, keepdims=True) * v_h
    y = y + bonus.reshape(B, S, W)
    return y.astype(p_r.dtype)


def chunk_attention(q, k, v, bias_table):
    B, S, W = q.shape
    H, Dh, L = AT_HEADS, AT_HEAD_DIM, CHUNK
    NC = S // L
    to_chunks = lambda t: t.reshape(B, NC, L, H, Dh).transpose(0, 3, 1, 2, 4)
    pad = ((0, 0), (0, 0), (LEFT_CHUNKS, 0), (0, 0), (0, 0))
    qc = to_chunks(q)
    kp = jnp.pad(to_chunks(k), pad)
    vp = jnp.pad(to_chunks(v), pad)
    qi = jnp.arange(L)
    kj = jnp.arange(BAND_CHUNKS * L)
    rel = LEFT_CHUNKS * L + qi[:, None] - kj[None, :]
    idx = jnp.clip(rel, -REL_CLIP, REL_CLIP) + REL_CLIP
    bias = bias_table[:, idx].astype(jnp.float32)
    scale = 1.0 / np.sqrt(Dh)

    def one_chunk(args):
        q_blk, c = args
        kb = lax.dynamic_slice_in_dim(kp, c, BAND_CHUNKS, axis=2).reshape(B, H, BAND_CHUNKS * L, Dh)
        vb = lax.dynamic_slice_in_dim(vp, c, BAND_CHUNKS, axis=2).reshape(B, H, BAND_CHUNKS * L, Dh)
        s = jnp.einsum('bhqd,bhkd->bhqk', q_blk, kb).astype(jnp.float32) * scale + bias
        valid = kj >= (LEFT_CHUNKS - c) * L
        s = jnp.where(valid[None, None, None, :], s, NEG_INF)
        p = jax.nn.softmax(s, axis=-1)
        return jnp.einsum('bhqk,bhkd->bhqd', p.astype(vb.dtype), vb)

    out = lax.map(one_chunk, (jnp.moveaxis(qc, 2, 0), jnp.arange(NC)))
    return out.transpose(1, 0, 3, 2, 4).reshape(B, S, W)


def spatial_gating(u, v, ln_g, ln_b, sg_w, sg_b):
    B, S, W = u.shape
    NB = S // SG_CHUNK
    v32 = v.astype(jnp.float32)
    mu = jnp.mean(v32, axis=-1, keepdims=True)
    var = jnp.mean(jnp.square(v32 - mu), axis=-1, keepdims=True)
    v = ((v32 - mu) * lax.rsqrt(var + LN_EPS) * ln_g.astype(jnp.float32)
         + ln_b.astype(jnp.float32)).astype(u.dtype)
    pos = jnp.arange(SG_CHUNK)
    mask = (pos[None, :] // CHUNK) <= (pos[:, None] // CHUNK)
    w = sg_w * mask[None].astype(sg_w.dtype)
    vb = v.reshape(B, NB, SG_CHUNK, SG_GROUPS, SG_GROUP_DIM)
    sv = jnp.einsum('gij,bnjgc->bnigc', w, vb) + sg_b.T[None, None, :, :, None]
    return u * sv.reshape(B, S, W)


def even_layer(h, g, w_in, shift_mu, w0, w2, a0, a2, k_k, k_a, r_k, lnx_g, lnx_b, att_bias, w_out):
    p = rmsnorm(h, g) @ w_in
    p_shift = token_shift(p[..., :SHIFT_WIDTH], shift_mu)
    p_r, p_k, p_v, p_wd, p_ad = jnp.split(
        p_shift, [RW_WIDTH, 2 * RW_WIDTH, 3 * RW_WIDTH, 3 * RW_WIDTH + DECAY_LORA], axis=-1)
    rest = p[..., SHIFT_WIDTH:]
    gate_a, q_b, k_b, v_b, gate_b = jnp.split(
        rest, [RW_WIDTH, RW_WIDTH + AT_WIDTH, RW_WIDTH + 2 * AT_WIDTH, RW_WIDTH + 3 * AT_WIDTH], axis=-1)
    y_a = rwkv7_mix(p_r, p_k, p_v, p_wd, p_ad, w0, w2, a0, a2, k_k, k_a, r_k, lnx_g, lnx_b)
    y_a = y_a * jax.nn.silu(gate_a)
    y_b = chunk_attention(q_b, k_b, v_b, att_bias) * jax.nn.silu(gate_b)
    return h + jnp.concatenate([y_a, y_b], axis=-1) @ w_out


def odd_layer(h, g, w_in, ln_g, ln_b, sg_w, sg_b, w_out):
    p = rmsnorm(h, g) @ w_in
    u, v, gate = jnp.split(p, [SG_WIDTH, 2 * SG_WIDTH], axis=-1)
    y = spatial_gating(jax.nn.gelu(u), jax.nn.gelu(v), ln_g, ln_b, sg_w, sg_b)
    return h + (y * jax.nn.silu(gate)) @ w_out


def _fwd_setup_inputs(seed: int = 0) -> dict:
    key = jax.random.key(seed)
    ks = jax.random.split(key, 24)
    ne = (DEPTH + 1) // 2
    no = DEPTH // 2
    nrm = lambda k, shape, s: jax.random.normal(k, shape, jnp.float32) * s
    return {
        "x": nrm(ks[0], (BATCH, SEQ, D_MODEL), 1.0),
        "norm_g": 1.0 + nrm(ks[1], (DEPTH, D_MODEL), 0.01),
        "w_in_e": nrm(ks[2], (ne, D_MODEL, EVEN_IN), D_MODEL ** -0.5),
        "shift_mu": jax.random.uniform(ks[3], (ne, SHIFT_WIDTH), jnp.float32),
        "rw_w0": jax.random.uniform(ks[4], (ne, RW_WIDTH), jnp.float32, -4.0, 1.0),
        "rw_w2": nrm(ks[5], (ne, DECAY_LORA, RW_WIDTH), 0.5 * DECAY_LORA ** -0.5),
        "rw_a0": nrm(ks[6], (ne, RW_WIDTH), 0.1),
        "rw_a2": nrm(ks[7], (ne, AAA_LORA, RW_WIDTH), 0.5 * AAA_LORA ** -0.5),
        "rw_kk": 0.85 + nrm(ks[8], (ne, RW_WIDTH), 0.02),
        "rw_ka": 1.0 + nrm(ks[9], (ne, RW_WIDTH), 0.02),
        "rw_rk": nrm(ks[10], (ne, RW_HEADS, RW_HEAD_DIM), 0.1),
        "rw_lnx_g": 1.0 + nrm(ks[11], (ne, RW_WIDTH), 0.01),
        "rw_lnx_b": nrm(ks[12], (ne, RW_WIDTH), 0.01),
        "att_bias": nrm(ks[13], (ne, AT_HEADS, 2 * REL_CLIP + 1), 0.1),
        "w_out_e": nrm(ks[14], (ne, EVEN_MIX, D_MODEL), 0.5 * EVEN_MIX ** -0.5),
        "w_in_o": nrm(ks[15], (no, D_MODEL, ODD_IN), D_MODEL ** -0.5),
        "sg_ln_g": 1.0 + nrm(ks[16], (no, SG_WIDTH), 0.01),
        "sg_ln_b": nrm(ks[17], (no, SG_WIDTH), 0.01),
        "sg_w": nrm(ks[18], (no, SG_GROUPS, SG_CHUNK, SG_CHUNK), SG_CHUNK ** -0.5),
        "sg_b": 1.0 + nrm(ks[19], (no, SG_GROUPS, SG_CHUNK), 0.01),
        "w_out_o": nrm(ks[20], (no, SG_WIDTH, D_MODEL), 0.5 * SG_WIDTH ** -0.5),
        "final_g": 1.0 + nrm(ks[21], (D_MODEL,), 0.01),
    }


def _fwd_reference(x, norm_g, w_in_e, shift_mu, rw_w0, rw_w2, rw_a0, rw_a2, rw_kk, rw_ka, rw_rk,
              rw_lnx_g, rw_lnx_b, att_bias, w_out_e, w_in_o, sg_ln_g, sg_ln_b, sg_w, sg_b,
              w_out_o, final_g):
    h = x
    for layer in range(DEPTH):
        i = layer // 2
        if layer % 2 == 0:
            h = even_layer(h, norm_g[layer], w_in_e[i], shift_mu[i], rw_w0[i], rw_w2[i], rw_a0[i],
                           rw_a2[i], rw_kk[i], rw_ka[i], rw_rk[i], rw_lnx_g[i], rw_lnx_b[i],
                           att_bias[i], w_out_e[i])
        else:
            h = odd_layer(h, norm_g[layer], w_in_o[i], sg_ln_g[i], sg_ln_b[i], sg_w[i], sg_b[i],
                          w_out_o[i])
    return rmsnorm(h, final_g)


import jax as _jax
import jax.numpy as _jnp

TWIN_FORMAT = 'train_step'
FWD_PARAMS = ['x', 'norm_g', 'w_in_e', 'shift_mu', 'rw_w0', 'rw_w2', 'rw_a0', 'rw_a2', 'rw_kk', 'rw_ka', 'rw_rk', 'rw_lnx_g', 'rw_lnx_b', 'att_bias', 'w_out_e', 'w_in_o', 'sg_ln_g', 'sg_ln_b', 'sg_w', 'sg_b', 'w_out_o', 'final_g']
TWIN_WEIGHTS = ['norm_g', 'w_in_e', 'shift_mu', 'rw_w0', 'rw_w2', 'rw_a0', 'rw_a2', 'rw_kk', 'rw_ka', 'rw_rk', 'rw_lnx_g', 'rw_lnx_b', 'att_bias', 'w_out_e', 'w_in_o', 'sg_ln_g', 'sg_ln_b', 'sg_w', 'sg_b', 'w_out_o', 'final_g']
TWIN_DIFF_INPUT = 'x'
TWIN_INPUTS = ['x', 'norm_g', 'w_in_e', 'shift_mu', 'rw_w0', 'rw_w2', 'rw_a0', 'rw_a2', 'rw_kk', 'rw_ka', 'rw_rk', 'rw_lnx_g', 'rw_lnx_b', 'att_bias', 'w_out_e', 'w_in_o', 'sg_ln_g', 'sg_ln_b', 'sg_w', 'sg_b', 'w_out_o', 'final_g', 'loss_target', 'm_norm_g', 'm_w_in_e', 'm_shift_mu', 'm_rw_w0', 'm_rw_w2', 'm_rw_a0', 'm_rw_a2', 'm_rw_kk', 'm_rw_ka', 'm_rw_rk', 'm_rw_lnx_g', 'm_rw_lnx_b', 'm_att_bias', 'm_w_out_e', 'm_w_in_o', 'm_sg_ln_g', 'm_sg_ln_b', 'm_sg_w', 'm_sg_b', 'm_w_out_o', 'm_final_g', 'v_norm_g', 'v_w_in_e', 'v_shift_mu', 'v_rw_w0', 'v_rw_w2', 'v_rw_a0', 'v_rw_a2', 'v_rw_kk', 'v_rw_ka', 'v_rw_rk', 'v_rw_lnx_g', 'v_rw_lnx_b', 'v_att_bias', 'v_w_out_e', 'v_w_in_o', 'v_sg_ln_g', 'v_sg_ln_b', 'v_sg_w', 'v_sg_b', 'v_w_out_o', 'v_final_g']
TWIN_OUTPUTS = ['loss', 'grad_x', 'grad_norm_g', 'grad_w_in_e', 'grad_shift_mu', 'grad_rw_w0', 'grad_rw_w2', 'grad_rw_a0', 'grad_rw_a2', 'grad_rw_kk', 'grad_rw_ka', 'grad_rw_rk', 'grad_rw_lnx_g', 'grad_rw_lnx_b', 'grad_att_bias', 'grad_w_out_e', 'grad_w_in_o', 'grad_sg_ln_g', 'grad_sg_ln_b', 'grad_sg_w', 'grad_sg_b', 'grad_w_out_o', 'grad_final_g', 'delta_norm_g', 'delta_w_in_e', 'delta_shift_mu', 'delta_rw_w0', 'delta_rw_w2', 'delta_rw_a0', 'delta_rw_a2', 'delta_rw_kk', 'delta_rw_ka', 'delta_rw_rk', 'delta_rw_lnx_g', 'delta_rw_lnx_b', 'delta_att_bias', 'delta_w_out_e', 'delta_w_in_o', 'delta_sg_ln_g', 'delta_sg_ln_b', 'delta_sg_w', 'delta_sg_b', 'delta_w_out_o', 'delta_final_g', 'new_m_norm_g', 'new_m_w_in_e', 'new_m_shift_mu', 'new_m_rw_w0', 'new_m_rw_w2', 'new_m_rw_a0', 'new_m_rw_a2', 'new_m_rw_kk', 'new_m_rw_ka', 'new_m_rw_rk', 'new_m_rw_lnx_g', 'new_m_rw_lnx_b', 'new_m_att_bias', 'new_m_w_out_e', 'new_m_w_in_o', 'new_m_sg_ln_g', 'new_m_sg_ln_b', 'new_m_sg_w', 'new_m_sg_b', 'new_m_w_out_o', 'new_m_final_g', 'new_v_norm_g', 'new_v_w_in_e', 'new_v_shift_mu', 'new_v_rw_w0', 'new_v_rw_w2', 'new_v_rw_a0', 'new_v_rw_a2', 'new_v_rw_kk', 'new_v_rw_ka', 'new_v_rw_rk', 'new_v_rw_lnx_g', 'new_v_rw_lnx_b', 'new_v_att_bias', 'new_v_w_out_e', 'new_v_w_in_o', 'new_v_sg_ln_g', 'new_v_sg_ln_b', 'new_v_sg_w', 'new_v_sg_b', 'new_v_w_out_o', 'new_v_final_g']
TWIN_LEAF_KINDS = {'loss': 'loss', 'grad_x': 'grad_x', 'grad_norm_g': 'grad_w', 'grad_w_in_e': 'grad_w', 'grad_shift_mu': 'grad_w', 'grad_rw_w0': 'grad_w', 'grad_rw_w2': 'grad_w', 'grad_rw_a0': 'grad_w', 'grad_rw_a2': 'grad_w', 'grad_rw_kk': 'grad_w', 'grad_rw_ka': 'grad_w', 'grad_rw_rk': 'grad_w', 'grad_rw_lnx_g': 'grad_w', 'grad_rw_lnx_b': 'grad_w', 'grad_att_bias': 'grad_w', 'grad_w_out_e': 'grad_w', 'grad_w_in_o': 'grad_w', 'grad_sg_ln_g': 'grad_w', 'grad_sg_ln_b': 'grad_w', 'grad_sg_w': 'grad_w', 'grad_sg_b': 'grad_w', 'grad_w_out_o': 'grad_w', 'grad_final_g': 'grad_w', 'delta_norm_g': 'delta_w', 'delta_w_in_e': 'delta_w', 'delta_shift_mu': 'delta_w', 'delta_rw_w0': 'delta_w', 'delta_rw_w2': 'delta_w', 'delta_rw_a0': 'delta_w', 'delta_rw_a2': 'delta_w', 'delta_rw_kk': 'delta_w', 'delta_rw_ka': 'delta_w', 'delta_rw_rk': 'delta_w', 'delta_rw_lnx_g': 'delta_w', 'delta_rw_lnx_b': 'delta_w', 'delta_att_bias': 'delta_w', 'delta_w_out_e': 'delta_w', 'delta_w_in_o': 'delta_w', 'delta_sg_ln_g': 'delta_w', 'delta_sg_ln_b': 'delta_w', 'delta_sg_w': 'delta_w', 'delta_sg_b': 'delta_w', 'delta_w_out_o': 'delta_w', 'delta_final_g': 'delta_w', 'new_m_norm_g': 'new_m', 'new_m_w_in_e': 'new_m', 'new_m_shift_mu': 'new_m', 'new_m_rw_w0': 'new_m', 'new_m_rw_w2': 'new_m', 'new_m_rw_a0': 'new_m', 'new_m_rw_a2': 'new_m', 'new_m_rw_kk': 'new_m', 'new_m_rw_ka': 'new_m', 'new_m_rw_rk': 'new_m', 'new_m_rw_lnx_g': 'new_m', 'new_m_rw_lnx_b': 'new_m', 'new_m_att_bias': 'new_m', 'new_m_w_out_e': 'new_m', 'new_m_w_in_o': 'new_m', 'new_m_sg_ln_g': 'new_m', 'new_m_sg_ln_b': 'new_m', 'new_m_sg_w': 'new_m', 'new_m_sg_b': 'new_m', 'new_m_w_out_o': 'new_m', 'new_m_final_g': 'new_m', 'new_v_norm_g': 'new_v', 'new_v_w_in_e': 'new_v', 'new_v_shift_mu': 'new_v', 'new_v_rw_w0': 'new_v', 'new_v_rw_w2': 'new_v', 'new_v_rw_a0': 'new_v', 'new_v_rw_a2': 'new_v', 'new_v_rw_kk': 'new_v', 'new_v_rw_ka': 'new_v', 'new_v_rw_rk': 'new_v', 'new_v_rw_lnx_g': 'new_v', 'new_v_rw_lnx_b': 'new_v', 'new_v_att_bias': 'new_v', 'new_v_w_out_e': 'new_v', 'new_v_w_in_o': 'new_v', 'new_v_sg_ln_g': 'new_v', 'new_v_sg_ln_b': 'new_v', 'new_v_sg_w': 'new_v', 'new_v_sg_b': 'new_v', 'new_v_w_out_o': 'new_v', 'new_v_final_g': 'new_v'}


def _forward(args):
    return _fwd_reference(*[args[k] for k in FWD_PARAMS])


def _output_shape():
    out = _jax.eval_shape(lambda: _forward(_fwd_setup_inputs(0)))
    return out.shape, out.dtype

N_MICROBATCH = 1
ADAM_LR = 0.001
ADAM_B1 = 0.9
ADAM_B2 = 0.999
ADAM_EPS = 1e-08
ADAM_WD = 0.01
ADAM_STEP = 10
PER_EXAMPLE_BATCH_AXIS = {'x': 0, 'loss_target': 0}
SHARED_INPUTS = []
_WEIGHT_DTYPES = {'norm_g': _jnp.float32, 'w_in_e': _jnp.float32, 'shift_mu': _jnp.float32, 'rw_w0': _jnp.float32, 'rw_w2': _jnp.float32, 'rw_a0': _jnp.float32, 'rw_a2': _jnp.float32, 'rw_kk': _jnp.float32, 'rw_ka': _jnp.float32, 'rw_rk': _jnp.float32, 'rw_lnx_g': _jnp.float32, 'rw_lnx_b': _jnp.float32, 'att_bias': _jnp.float32, 'w_out_e': _jnp.float32, 'w_in_o': _jnp.float32, 'sg_ln_g': _jnp.float32, 'sg_ln_b': _jnp.float32, 'sg_w': _jnp.float32, 'sg_b': _jnp.float32, 'w_out_o': _jnp.float32, 'final_g': _jnp.float32}
MOMENT_SCALE = {'norm_g': 7.876537e-02, 'w_in_e': 4.141719e-02, 'shift_mu': 9.064784e-02, 'rw_w0': 3.160609e-02, 'rw_w2': 4.794476e-03, 'rw_a0': 2.682771e-02, 'rw_a2': 2.220353e-02, 'rw_kk': 4.063884e-02, 'rw_ka': 6.303855e-02, 'rw_rk': 1.259435e-01, 'rw_lnx_g': 5.518582e-02, 'rw_lnx_b': 5.338995e-02, 'att_bias': 3.651647e-03, 'w_out_e': 7.936407e-02, 'w_in_o': 4.113924e-02, 'sg_ln_g': 2.759599e-02, 'sg_ln_b': 2.820147e-02, 'sg_w': 2.860613e-02, 'sg_b': 3.262870e-02, 'w_out_o': 8.677196e-02, 'final_g': 3.196772e+01}


def _to_microbatches(a, axis):
    t = _jnp.moveaxis(a, axis, 0)
    t = t.reshape((N_MICROBATCH, t.shape[0] // N_MICROBATCH) + t.shape[1:])
    return _jnp.moveaxis(t, 1, axis + 1)


def setup_inputs(seed: int = 0) -> dict:
    inp = _fwd_setup_inputs(seed)
    key = _jax.random.fold_in(_jax.random.key(seed), 7919)
    shape, _ = _output_shape()
    out = dict(inp)
    out["loss_target"] = _jax.random.normal(_jax.random.fold_in(key, 0), shape, _jnp.float32)
    for i, name in enumerate(TWIN_WEIGHTS):
        w = inp[name].astype(_jnp.float32)
        if MOMENT_SCALE is None:
            s = _jnp.sqrt(_jnp.mean(_jnp.square(w)) + 1e-30)
        else:
            s = MOMENT_SCALE[name]
        km, kv = _jax.random.split(_jax.random.fold_in(key, i + 1))
        out[name] = w
        out["m_" + name] = s * _jax.random.normal(km, w.shape, _jnp.float32)
        out["v_" + name] = (s * s) * _jax.random.uniform(kv, w.shape, _jnp.float32, 0.5, 1.5)
    if N_MICROBATCH > 1:
        for name, axis in PER_EXAMPLE_BATCH_AXIS.items():
            out[name] = _to_microbatches(out[name], axis)
    return {'x': out['x'], 'norm_g': out['norm_g'], 'w_in_e': out['w_in_e'], 'shift_mu': out['shift_mu'], 'rw_w0': out['rw_w0'], 'rw_w2': out['rw_w2'], 'rw_a0': out['rw_a0'], 'rw_a2': out['rw_a2'], 'rw_kk': out['rw_kk'], 'rw_ka': out['rw_ka'], 'rw_rk': out['rw_rk'], 'rw_lnx_g': out['rw_lnx_g'], 'rw_lnx_b': out['rw_lnx_b'], 'att_bias': out['att_bias'], 'w_out_e': out['w_out_e'], 'w_in_o': out['w_in_o'], 'sg_ln_g': out['sg_ln_g'], 'sg_ln_b': out['sg_ln_b'], 'sg_w': out['sg_w'], 'sg_b': out['sg_b'], 'w_out_o': out['w_out_o'], 'final_g': out['final_g'], 'loss_target': out['loss_target'], 'm_norm_g': out['m_norm_g'], 'm_w_in_e': out['m_w_in_e'], 'm_shift_mu': out['m_shift_mu'], 'm_rw_w0': out['m_rw_w0'], 'm_rw_w2': out['m_rw_w2'], 'm_rw_a0': out['m_rw_a0'], 'm_rw_a2': out['m_rw_a2'], 'm_rw_kk': out['m_rw_kk'], 'm_rw_ka': out['m_rw_ka'], 'm_rw_rk': out['m_rw_rk'], 'm_rw_lnx_g': out['m_rw_lnx_g'], 'm_rw_lnx_b': out['m_rw_lnx_b'], 'm_att_bias': out['m_att_bias'], 'm_w_out_e': out['m_w_out_e'], 'm_w_in_o': out['m_w_in_o'], 'm_sg_ln_g': out['m_sg_ln_g'], 'm_sg_ln_b': out['m_sg_ln_b'], 'm_sg_w': out['m_sg_w'], 'm_sg_b': out['m_sg_b'], 'm_w_out_o': out['m_w_out_o'], 'm_final_g': out['m_final_g'], 'v_norm_g': out['v_norm_g'], 'v_w_in_e': out['v_w_in_e'], 'v_shift_mu': out['v_shift_mu'], 'v_rw_w0': out['v_rw_w0'], 'v_rw_w2': out['v_rw_w2'], 'v_rw_a0': out['v_rw_a0'], 'v_rw_a2': out['v_rw_a2'], 'v_rw_kk': out['v_rw_kk'], 'v_rw_ka': out['v_rw_ka'], 'v_rw_rk': out['v_rw_rk'], 'v_rw_lnx_g': out['v_rw_lnx_g'], 'v_rw_lnx_b': out['v_rw_lnx_b'], 'v_att_bias': out['v_att_bias'], 'v_w_out_e': out['v_w_out_e'], 'v_w_in_o': out['v_w_in_o'], 'v_sg_ln_g': out['v_sg_ln_g'], 'v_sg_ln_b': out['v_sg_ln_b'], 'v_sg_w': out['v_sg_w'], 'v_sg_b': out['v_sg_b'], 'v_w_out_o': out['v_w_out_o'], 'v_final_g': out['v_final_g']}


def _loss(weights, diff, rest, loss_target):
    with _jax.named_scope("forward"):
        args = {**rest, TWIN_DIFF_INPUT: diff, **{k: w.astype(_WEIGHT_DTYPES[k]) for k, w in weights.items()}}
        y = _forward(args)
    with _jax.named_scope("loss_head"):
        err = _jnp.square(y.astype(_jnp.float32) - loss_target)
        return 0.5 * _jnp.sum(_jnp.mean(err, axis=-1)) if err.ndim else 0.5 * err


def _adamw(w, g, m, v):
    m = ADAM_B1 * m + (1.0 - ADAM_B1) * g
    v = ADAM_B2 * v + (1.0 - ADAM_B2) * _jnp.square(g)
    m_hat = m / (1.0 - ADAM_B1 ** ADAM_STEP)
    v_hat = v / (1.0 - ADAM_B2 ** ADAM_STEP)
    delta = -ADAM_LR * (m_hat / (_jnp.sqrt(v_hat) + ADAM_EPS) + ADAM_WD * w)
    return delta, m, v


def reference(x, norm_g, w_in_e, shift_mu, rw_w0, rw_w2, rw_a0, rw_a2, rw_kk, rw_ka, rw_rk, rw_lnx_g, rw_lnx_b, att_bias, w_out_e, w_in_o, sg_ln_g, sg_ln_b, sg_w, sg_b, w_out_o, final_g, loss_target, m_norm_g, m_w_in_e, m_shift_mu, m_rw_w0, m_rw_w2, m_rw_a0, m_rw_a2, m_rw_kk, m_rw_ka, m_rw_rk, m_rw_lnx_g, m_rw_lnx_b, m_att_bias, m_w_out_e, m_w_in_o, m_sg_ln_g, m_sg_ln_b, m_sg_w, m_sg_b, m_w_out_o, m_final_g, v_norm_g, v_w_in_e, v_shift_mu, v_rw_w0, v_rw_w2, v_rw_a0, v_rw_a2, v_rw_kk, v_rw_ka, v_rw_rk, v_rw_lnx_g, v_rw_lnx_b, v_att_bias, v_w_out_e, v_w_in_o, v_sg_ln_g, v_sg_ln_b, v_sg_w, v_sg_b, v_w_out_o, v_final_g):
    given = dict(x=x, norm_g=norm_g, w_in_e=w_in_e, shift_mu=shift_mu, rw_w0=rw_w0, rw_w2=rw_w2, rw_a0=rw_a0, rw_a2=rw_a2, rw_kk=rw_kk, rw_ka=rw_ka, rw_rk=rw_rk, rw_lnx_g=rw_lnx_g, rw_lnx_b=rw_lnx_b, att_bias=att_bias, w_out_e=w_out_e, w_in_o=w_in_o, sg_ln_g=sg_ln_g, sg_ln_b=sg_ln_b, sg_w=sg_w, sg_b=sg_b, w_out_o=w_out_o, final_g=final_g, loss_target=loss_target, m_norm_g=m_norm_g, m_w_in_e=m_w_in_e, m_shift_mu=m_shift_mu, m_rw_w0=m_rw_w0, m_rw_w2=m_rw_w2, m_rw_a0=m_rw_a0, m_rw_a2=m_rw_a2, m_rw_kk=m_rw_kk, m_rw_ka=m_rw_ka, m_rw_rk=m_rw_rk, m_rw_lnx_g=m_rw_lnx_g, m_rw_lnx_b=m_rw_lnx_b, m_att_bias=m_att_bias, m_w_out_e=m_w_out_e, m_w_in_o=m_w_in_o, m_sg_ln_g=m_sg_ln_g, m_sg_ln_b=m_sg_ln_b, m_sg_w=m_sg_w, m_sg_b=m_sg_b, m_w_out_o=m_w_out_o, m_final_g=m_final_g, v_norm_g=v_norm_g, v_w_in_e=v_w_in_e, v_shift_mu=v_shift_mu, v_rw_w0=v_rw_w0, v_rw_w2=v_rw_w2, v_rw_a0=v_rw_a0, v_rw_a2=v_rw_a2, v_rw_kk=v_rw_kk, v_rw_ka=v_rw_ka, v_rw_rk=v_rw_rk, v_rw_lnx_g=v_rw_lnx_g, v_rw_lnx_b=v_rw_lnx_b, v_att_bias=v_att_bias, v_w_out_e=v_w_out_e, v_w_in_o=v_w_in_o, v_sg_ln_g=v_sg_ln_g, v_sg_ln_b=v_sg_ln_b, v_sg_w=v_sg_w, v_sg_b=v_sg_b, v_w_out_o=v_w_out_o, v_final_g=v_final_g)
    weights = {n: given[n] for n in TWIN_WEIGHTS}
    shared = {n: given[n] for n in SHARED_INPUTS}
    per_example = {n: given[n] for n in ['x']}
    grad_fn = _jax.value_and_grad(_loss, argnums=(0, 1))

    def one_microbatch(ex, loss_target):
        ex = dict(ex)
        diff = ex.pop(TWIN_DIFF_INPUT)
        return grad_fn(weights, diff, {**shared, **ex}, loss_target)

    if N_MICROBATCH == 1:
        loss, (grad_w, grad_x) = one_microbatch(per_example, given["loss_target"])
    else:
        def body(carry, xs):
            loss_sum, grad_sum = carry
            l_k, (gw_k, gx_k) = one_microbatch(xs[0], xs[1])
            with _jax.named_scope("update"):
                return (loss_sum + l_k, _jax.tree.map(_jnp.add, grad_sum, gw_k)), gx_k

        init = (_jnp.zeros((), _jnp.float32), _jax.tree.map(_jnp.zeros_like, weights))
        (loss, grad_w), grad_x = _jax.lax.scan(body, init, (per_example, given["loss_target"]))
    with _jax.named_scope("update"):
        delta_w, new_m, new_v = {}, {}, {}
        for n in TWIN_WEIGHTS:
            delta_w[n], new_m[n], new_v[n] = _adamw(weights[n], grad_w[n], given["m_" + n], given["v_" + n])
    return (loss, grad_x, *[grad_w[n] for n in TWIN_WEIGHTS], *[delta_w[n] for n in TWIN_WEIGHTS],
            *[new_m[n] for n in TWIN_WEIGHTS], *[new_v[n] for n in TWIN_WEIGHTS])
```

```python
import functools
import math

import jax
import jax.numpy as jnp
import numpy as np
from jax import lax
from jax.experimental import pallas as pl
from jax.experimental.pallas import tpu as pltpu

F32 = jnp.float32
BF16 = jnp.bfloat16
HI = lax.Precision.HIGHEST

D = 1024
SEQ = 2048
NSEQ = 2
T = NSEQ * SEQ
HD = 64
NH = 8
W = 512
SHIFT = 1664
LORA = 64
EVEN_IN = 4224
ODD_IN = 3072
L = 64
NC = SEQ // L
LEFT = 8
BAND = (LEFT + 1) * L
CLIP = 128
SGC = 128
NG = 8
RMS_EPS = 1e-6
LN_EPS = 1e-5
GN_EPS = 64e-5
NEG = -1e30
VMEM_BIG = 56 * 1024 * 1024

ADAM_LR = 0.001
ADAM_B1 = 0.9
ADAM_B2 = 0.999
ADAM_EPS = 1e-08
ADAM_WD = 0.01
ADAM_STEP = 10

MESH = pl.DeviceIdType.MESH


def _bdot(a, b):
    return jnp.dot(a.astype(BF16), b.astype(BF16), preferred_element_type=F32)


def _bdot_nt(a, b):
    return lax.dot_general(a.astype(BF16), b.astype(BF16), (((1,), (1,)), ((), ())), preferred_element_type=F32)


def _bdot_tn(a, b):
    return lax.dot_general(a.astype(BF16), b.astype(BF16), (((0,), (0,)), ((), ())), preferred_element_type=F32)


def _hdot(a, b):
    return jnp.dot(a, b, precision=HI, preferred_element_type=F32)


def _hdot_nt(a, b):
    return lax.dot_general(a, b, (((1,), (1,)), ((), ())), precision=HI, preferred_element_type=F32)


def _hdot_tn(a, b):
    return lax.dot_general(a, b, (((0,), (0,)), ((), ())), precision=HI, preferred_element_type=F32)


def _iota2(shape, dim):
    return lax.broadcasted_iota(jnp.int32, shape, dim)


def _head_blockdiag():
    r = _iota2((W, W), 0) // HD
    c = _iota2((W, W), 1) // HD
    return (r == c).astype(F32)


def _silu(x):
    return x * jax.nn.sigmoid(x)


def _dsilu(x):
    s = jax.nn.sigmoid(x)
    return s * (1.0 + x * (1.0 - s))


_GELU_C = math.sqrt(2.0 / math.pi)


def _gelu(x):
    return 0.5 * x * (1.0 + jnp.tanh(_GELU_C * (x + 0.044715 * (x * x * x))))


def _dgelu(x):
    t = jnp.tanh(_GELU_C * (x + 0.044715 * (x * x * x)))
    return 0.5 * (1.0 + t) + 0.5 * x * (1.0 - t * t) * _GELU_C * (1.0 + 3.0 * 0.044715 * x * x)


def _softplus(x):
    return jnp.maximum(x, 0.0) + jnp.log(1.0 + jnp.exp(-jnp.abs(x)))


def _cparams(sem, vmem=None):
    return pltpu.CompilerParams(dimension_semantics=sem, vmem_limit_bytes=vmem)


def _row_spec(tm, width):
    return pl.BlockSpec((tm, width), lambda i: (i, 0))


def _const_spec(shape):
    nd = len(shape)
    return pl.BlockSpec(shape, lambda *_: (0,) * nd)


def ln_in_proj(x, g, w_bf, splits, name):
    n = w_bf.shape[1]
    tm = 256
    spans = []
    o = 0
    for s in splits:
        spans.append((o, o + s))
        o += s
    assert o == n

    def body(x_ref, g_ref, w_ref, xn_ref, *outs):
        xv = x_ref[...]
        rstd = lax.rsqrt(jnp.mean(xv * xv, axis=-1, keepdims=True) + RMS_EPS)
        xn = (xv * rstd * g_ref[...]).astype(BF16)
        xn_ref[...] = xn
        p = jnp.dot(xn, w_ref[...], preferred_element_type=F32)
        for o_ref, (a, b) in zip(outs, spans):
            o_ref[...] = p[:, a:b]

    return pl.pallas_call(
        body, grid=(T // tm,), name=name,
        in_specs=[_row_spec(tm, D), _const_spec((1, D)), _const_spec((D, n))],
        out_specs=[_row_spec(tm, D)] + [_row_spec(tm, s) for s in splits],
        out_shape=[jax.ShapeDtypeStruct((T, D), BF16)] + [jax.ShapeDtypeStruct((T, s), F32) for s in splits],
        compiler_params=_cparams(("parallel",), VMEM_BIG),
    )(x, g, w_bf)


def in_proj_bwd_x(x, g, w_bf, dps, dres, name):
    n = w_bf.shape[1]
    tm = 256
    widths = [d.shape[1] for d in dps]

    def body(x_ref, g_ref, w_ref, dres_ref, *rest):
        dp_refs = rest[:len(widths)]
        dx_ref, dg_ref, dpc_ref = rest[len(widths):]
        dp = jnp.concatenate([r[...].astype(BF16) for r in dp_refs], axis=-1)
        dpc_ref[...] = dp
        dxn = lax.dot_general(dp, w_ref[...], (((1,), (1,)), ((), ())), preferred_element_type=F32)
        xv = x_ref[...]
        rstd = lax.rsqrt(jnp.mean(xv * xv, axis=-1, keepdims=True) + RMS_EPS)
        xhat = xv * rstd
        dgp = jnp.sum(dxn * xhat, axis=0, keepdims=True)

        @pl.when(pl.program_id(0) == 0)
        def _():
            dg_ref[...] = jnp.zeros_like(dg_ref)

        dg_ref[...] += dgp
        dxh = dxn * g_ref[...]
        dx_ref[...] = dres_ref[...] + rstd * (dxh - xhat * jnp.mean(dxh * xhat, axis=-1, keepdims=True))

    return pl.pallas_call(
        body, grid=(T // tm,), name=name,
        in_specs=[_row_spec(tm, D), _const_spec((1, D)), _const_spec((D, n)), _row_spec(tm, D)]
        + [_row_spec(tm, s) for s in widths],
        out_specs=[_row_spec(tm, D), _const_spec((1, D)), _row_spec(tm, n)],
        out_shape=[jax.ShapeDtypeStruct((T, D), F32), jax.ShapeDtypeStruct((1, D), F32),
                   jax.ShapeDtypeStruct((T, n), BF16)],
        compiler_params=_cparams(("arbitrary",), VMEM_BIG),
    )(x, g, w_bf, dres, *dps)


def matmul_tn_acc(a_bf, b_bf, tn, name):
    k = a_bf.shape[1]
    n = b_bf.shape[1]
    tm = 512
    nt = T // tm

    def body(a_ref, b_ref, o_ref):
        @pl.when(pl.program_id(1) == 0)
        def _():
            o_ref[...] = jnp.zeros_like(o_ref)

        o_ref[...] += lax.dot_general(a_ref[...], b_ref[...], (((0,), (0,)), ((), ())), preferred_element_type=F32)

    return pl.pallas_call(
        body, grid=(n // tn, nt), name=name,
        in_specs=[pl.BlockSpec((tm, k), lambda j, i: (i, 0)), pl.BlockSpec((tm, tn), lambda j, i: (i, j))],
        out_specs=pl.BlockSpec((k, tn), lambda j, i: (0, j)),
        out_shape=jax.ShapeDtypeStruct((k, n), F32),
        compiler_params=_cparams(("parallel", "arbitrary"), VMEM_BIG),
    )(a_bf, b_bf)


def out_proj(h, z_bf, w_bf, name):
    tm = 256

    def body(h_ref, z_ref, w_ref, o_ref):
        o_ref[...] = h_ref[...] + jnp.dot(z_ref[...], w_ref[...], preferred_element_type=F32)

    return pl.pallas_call(
        body, grid=(T // tm,), name=name,
        in_specs=[_row_spec(tm, D), _row_spec(tm, D), _const_spec((D, D))],
        out_specs=_row_spec(tm, D), out_shape=jax.ShapeDtypeStruct((T, D), F32),
        compiler_params=_cparams(("parallel",)),
    )(h, z_bf, w_bf)


def out_proj_bwd(dh, z_bf, w_bf, name):
    tm = 256

    def body(dh_ref, z_ref, w_ref, dz_ref, dw_ref):
        dhb = dh_ref[...].astype(BF16)
        dz_ref[...] = lax.dot_general(dhb, w_ref[...], (((1,), (1,)), ((), ())), preferred_element_type=F32)

        @pl.when(pl.program_id(0) == 0)
        def _():
            dw_ref[...] = jnp.zeros_like(dw_ref)

        dw_ref[...] += lax.dot_general(z_ref[...], dhb, (((0,), (0,)), ((), ())), preferred_element_type=F32)

    return pl.pallas_call(
        body, grid=(T // tm,), name=name,
        in_specs=[_row_spec(tm, D), _row_spec(tm, D), _const_spec((D, D))],
        out_specs=[_row_spec(tm, D), _const_spec((D, D))],
        out_shape=[jax.ShapeDtypeStruct((T, D), F32), jax.ShapeDtypeStruct((D, D), F32)],
        compiler_params=_cparams(("arbitrary",)),
    )(dh, z_bf, w_bf)


def final_loss(h, g, target):
    tm = 256

    def body(h_ref, g_ref, t_ref, dh_ref, loss_ref, dg_ref):
        xv = h_ref[...]
        rstd = lax.rsqrt(jnp.mean(xv * xv, axis=-1, keepdims=True) + RMS_EPS)
        xhat = xv * rstd
        err = xhat * g_ref[...] - t_ref[...]
        part = 0.5 * jnp.sum(jnp.mean(err * err, axis=-1, keepdims=True), axis=0, keepdims=True)
        dout = err * (1.0 / D)

        @pl.when(pl.program_id(0) == 0)
        def _():
            loss_ref[...] = jnp.zeros_like(loss_ref)
            dg_ref[...] = jnp.zeros_like(dg_ref)

        loss_ref[...] += jnp.broadcast_to(part, loss_ref.shape)
        dg_ref[...] += jnp.sum(dout * xhat, axis=0, keepdims=True)
        dxh = dout * g_ref[...]
        dh_ref[...] = rstd * (dxh - xhat * jnp.mean(dxh * xhat, axis=-1, keepdims=True))

    return pl.pallas_call(
        body, grid=(T // tm,), name="final_loss",
        in_specs=[_row_spec(tm, D), _const_spec((1, D)), _row_spec(tm, D)],
        out_specs=[_row_spec(tm, D), _const_spec((8, 128)), _const_spec((1, D))],
        out_shape=[jax.ShapeDtypeStruct((T, D), F32), jax.ShapeDtypeStruct((8, 128), F32),
                   jax.ShapeDtypeStruct((1, D), F32)],
        compiler_params=_cparams(("arbitrary",)),
    )(h, g, target)


PREP_TM = 256
PREP_NB = SEQ // PREP_TM


def _prep_elem(k, wl, apre, kkw, kaw, bd):
    wraw = -_softplus(-wl) - 0.5
    lw = -jnp.exp(wraw)
    asig = jax.nn.sigmoid(apre)
    kkr = k * kkw
    nrm = jnp.maximum(jnp.sqrt(_hdot(kkr * kkr, bd)), 1e-12)
    kk = kkr / nrm
    k2 = k * (1.0 + (asig - 1.0) * kaw)
    return lw, k2, -kk, kk * asig


def _shifted(ps_ref, prev_ref, mu, blk):
    p = ps_ref[...]
    first = (blk % PREP_NB) == 0
    prev_row = jnp.where(first, 0.0, prev_ref[7:8, :])
    rolled = pltpu.roll(p, 1, 0)
    p_prev = jnp.where(_iota2(p.shape, 0) == 0, prev_row, rolled)
    return p, p_prev, p + (p_prev - p) * mu


def _prev_spec(width, blk_of):
    return pl.BlockSpec((8, width), lambda i: (jnp.maximum(blk_of(i) * (PREP_TM // 8) - 1, 0), 0))


def even_prep(ps, mu, w0, w2x, a0, a2x, kkw, kaw):
    tm = PREP_TM

    def body(ps_ref, prev_ref, mu_ref, w0_ref, w2_ref, a0_ref, a2_ref, kk_ref, ka_ref,
             r_ref, lw_ref, k2_ref, v_ref, aa_ref, bb_ref):
        _, _, s = _shifted(ps_ref, prev_ref, mu_ref[...], pl.program_id(0))
        wa = s[:, 3 * W:]
        wl = w0_ref[...] + _bdot(jnp.tanh(wa), w2_ref[...])
        apre = a0_ref[...] + _bdot(wa, a2_ref[...])
        lw, k2, aa, bb = _prep_elem(s[:, W:2 * W], wl, apre, kk_ref[...], ka_ref[...], _head_blockdiag())
        r_ref[...] = s[:, 0:W]
        v_ref[...] = s[:, 2 * W:3 * W]
        lw_ref[...] = lw
        k2_ref[...] = k2
        aa_ref[...] = aa
        bb_ref[...] = bb

    vec = _const_spec((1, W))
    return pl.pallas_call(
        body, grid=(T // tm,), name="even_prep",
        in_specs=[_row_spec(tm, SHIFT), _prev_spec(SHIFT, lambda i: i), _const_spec((1, SHIFT)), vec,
                  _const_spec((2 * LORA, W)), vec, _const_spec((2 * LORA, W)), vec, vec],
        out_specs=[_row_spec(tm, W)] * 6,
        out_shape=[jax.ShapeDtypeStruct((T, W), F32)] * 6,
        compiler_params=_cparams(("parallel",)),
    )(ps, ps, mu, w0, w2x, a0, a2x, kkw, kaw)


def even_prep_bwd(ps, mu, w0, w2x, a0, a2x, kkw, kaw, dr, dlw, dk2, dv, daa, dbb, dr2, dk22, dv2):
    tm = PREP_TM
    nb = T // tm
    rev = lambda i: nb - 1 - i

    def body(ps_ref, prev_ref, mu_ref, w0_ref, w2_ref, a0_ref, a2_ref, kk_ref, ka_ref,
             dr_ref, dlw_ref, dk2_ref, dv_ref, daa_ref, dbb_ref, dr2_ref, dk22_ref, dv2_ref,
             dps_ref, dmu_ref, dw0_ref, dw2_ref, da0_ref, da2_ref, dkk_ref, dka_ref, carry):
        i = pl.program_id(0)
        blk = rev(i)
        mu_v = mu_ref[...]
        p, p_prev, s = _shifted(ps_ref, prev_ref, mu_v, blk)
        wa = s[:, 3 * W:]
        th = jnp.tanh(wa)
        wl = w0_ref[...] + _bdot(th, w2_ref[...])
        apre = a0_ref[...] + _bdot(wa, a2_ref[...])
        bd = _head_blockdiag()
        k = s[:, W:2 * W]
        _, vjp = jax.vjp(lambda k_, wl_, ap_, kkw_, kaw_: _prep_elem(k_, wl_, ap_, kkw_, kaw_, bd),
                         k, wl, apre, kk_ref[...], ka_ref[...])
        dk, dwl, dap, dkkw, dkaw = vjp((dlw_ref[...], dk2_ref[...] + dk22_ref[...], daa_ref[...], dbb_ref[...]))
        dwa = _bdot_nt(dwl, w2_ref[...]) * (1.0 - th * th) + _bdot_nt(dap, a2_ref[...])
        ds = jnp.concatenate([dr_ref[...] + dr2_ref[...], dk, dv_ref[...] + dv2_ref[...], dwa], axis=-1)

        @pl.when(i == 0)
        def _():
            for ref in (dmu_ref, dw0_ref, dw2_ref, da0_ref, da2_ref, dkk_ref, dka_ref, carry):
                ref[...] = jnp.zeros_like(ref)

        dmu_ref[...] += jnp.sum(ds * (p_prev - p), axis=0, keepdims=True)
        dw0_ref[...] += jnp.sum(dwl, axis=0, keepdims=True)
        da0_ref[...] += jnp.sum(dap, axis=0, keepdims=True)
        dw2_ref[...] += _bdot_tn(th, dwl)
        da2_ref[...] += _bdot_tn(wa, dap)
        dkk_ref[...] += dkkw
        dka_ref[...] += dkaw
        dsm = ds * mu_v
        last = (blk % PREP_NB) == PREP_NB - 1
        nxt = jnp.where(last, 0.0, carry[0:1, :])
        up = pltpu.roll(dsm, tm - 1, 0)
        up = jnp.where(_iota2(up.shape, 0) == tm - 1, nxt, up)
        dps_ref[...] = ds - dsm + up
        carry[0:1, :] = dsm[0:1, :]

    vec = _const_spec((1, W))
    rrow = lambda width: pl.BlockSpec((tm, width), lambda i: (rev(i), 0))
    return pl.pallas_call(
        body, grid=(nb,), name="even_prep_bwd",
        in_specs=[rrow(SHIFT), _prev_spec(SHIFT, rev), _const_spec((1, SHIFT)), vec,
                  _const_spec((2 * LORA, W)), vec, _const_spec((2 * LORA, W)), vec, vec] + [rrow(W)] * 9,
        out_specs=[rrow(SHIFT), _const_spec((1, SHIFT)), vec, _const_spec((2 * LORA, W)), vec,
                   _const_spec((2 * LORA, W)), vec, vec],
        out_shape=[jax.ShapeDtypeStruct((T, SHIFT), F32), jax.ShapeDtypeStruct((1, SHIFT), F32),
                   jax.ShapeDtypeStruct((1, W), F32), jax.ShapeDtypeStruct((2 * LORA, W), F32),
                   jax.ShapeDtypeStruct((1, W), F32), jax.ShapeDtypeStruct((2 * LORA, W), F32),
                   jax.ShapeDtypeStruct((1, W), F32), jax.ShapeDtypeStruct((1, W), F32)],
        scratch_shapes=[pltpu.VMEM((8, SHIFT), F32)],
        compiler_params=_cparams(("arbitrary",), VMEM_BIG),
    )(ps, ps, mu, w0, w2x, a0, a2x, kkw, kaw, dr, dlw, dk2, dv, daa, dbb, dr2, dk22, dv2)


def _chunk_consts():
    row = _iota2((L, L), 0)
    col = _iota2((L, L), 1)
    return (row >= col).astype(F32), (row > col).astype(F32), (row == col).astype(F32)


def _chunk_fwd(r, lw, k2, v, aa, bb, h0, consts):
    tri, stri, eye = consts
    g = _hdot(tri, lw)
    eg = jnp.exp(g)
    eng = jnp.exp(-g)
    at = aa * jnp.exp(g - lw)
    rt = r * eg
    bt = bb * eng
    kt = k2 * eng
    aab = _hdot_nt(at, bt) * stri
    aak = _hdot_nt(at, kt) * stri
    arb = _hdot_nt(rt, bt) * tri
    ark = _hdot_nt(rt, kt) * tri
    tinv = eye + aab
    m = aab
    for _ in range(5):
        m = _hdot(m, m)
        tinv = tinv + _hdot(tinv, m)
    u = _hdot(tinv, _hdot(at, h0) + _hdot(aak, v))
    y = _hdot(rt, h0) + _hdot(arb, u) + _hdot(ark, v)
    egl = jnp.sum(eye * eg[L - 1:L, :], axis=-1, keepdims=True)
    hn = egl * (h0 + _hdot_tn(bt, u) + _hdot_tn(kt, v))
    return y, hn, (g, eg, eng, at, rt, bt, kt, aab, aak, arb, ark, tinv, u, egl)


def _chunk_bwd(r, lw, k2, v, aa, bb, h0, dy, dhn, consts):
    tri, stri, eye = consts
    _, hn, (g, eg, eng, at, rt, bt, kt, aab, aak, arb, ark, tinv, u, egl) = _chunk_fwd(r, lw, k2, v, aa, bb, h0, consts)
    dz = egl * dhn
    dgl = jnp.sum(dhn * hn, axis=-1, keepdims=True)
    dgl_row = jnp.sum(eye * dgl, axis=0, keepdims=True)
    du = _hdot_tn(arb, dy) + _hdot(bt, dz)
    drhs = _hdot_tn(tinv, du)
    dv = _hdot_tn(ark, dy) + _hdot(kt, dz) + _hdot_tn(aak, drhs)
    dh0 = dz + _hdot_tn(rt, dy) + _hdot_tn(at, drhs)
    daab = _hdot_nt(drhs, u) * stri
    daak = _hdot_nt(drhs, v) * stri
    darb = _hdot_nt(dy, u) * tri
    dark = _hdot_nt(dy, v) * tri
    drt = _hdot_nt(dy, h0) + _hdot(darb, bt) + _hdot(dark, kt)
    dat = _hdot_nt(drhs, h0) + _hdot(daab, bt) + _hdot(daak, kt)
    dbt = _hdot_tn(daab, at) + _hdot_tn(darb, rt) + _hdot_nt(u, dz)
    dkt = _hdot_tn(daak, at) + _hdot_tn(dark, rt) + _hdot_nt(v, dz)
    dg = drt * rt - dbt * bt - dkt * kt
    dg = dg + jnp.where(_iota2(dg.shape, 0) == L - 1, dgl_row, 0.0)
    dgp = dat * at
    dlw = _hdot_tn(tri, dg + dgp) - dgp
    return drt * eg, dlw, dkt * eng, dv, dat * jnp.exp(g - lw), dbt * eng, dh0


def _head_cols(h):
    return slice(h * HD, (h + 1) * HD)


def rwkv_fwd(r, lw, k2, v, aa, bb):
    def body(r_ref, lw_ref, k2_ref, v_ref, aa_ref, bb_ref, y_ref, hs_ref, state):
        @pl.when(pl.program_id(1) == 0)
        def _():
            state[...] = jnp.zeros_like(state)

        hs_ref[0] = state[...]
        consts = _chunk_consts()
        for h in range(NH):
            c = _head_cols(h)
            y, hn, _ = _chunk_fwd(r_ref[:, c], lw_ref[:, c], k2_ref[:, c], v_ref[:, c], aa_ref[:, c], bb_ref[:, c],
                                  state[c, :], consts)
            y_ref[:, c] = y
            state[c, :] = hn

    blk = pl.BlockSpec((L, W), lambda b, c: (b * NC + c, 0))
    return pl.pallas_call(
        body, grid=(NSEQ, NC), name="rwkv_fwd",
        in_specs=[blk] * 6,
        out_specs=[blk, pl.BlockSpec((1, W, HD), lambda b, c: (b * NC + c, 0, 0))],
        out_shape=[jax.ShapeDtypeStruct((T, W), F32), jax.ShapeDtypeStruct((NSEQ * NC, W, HD), F32)],
        scratch_shapes=[pltpu.VMEM((W, HD), F32)],
        compiler_params=_cparams(("parallel", "arbitrary")),
    )(r, lw, k2, v, aa, bb)


def rwkv_bwd(r, lw, k2, v, aa, bb, hs, dy):
    def body(r_ref, lw_ref, k2_ref, v_ref, aa_ref, bb_ref, hs_ref, dy_ref,
             dr_ref, dlw_ref, dk2_ref, dv_ref, daa_ref, dbb_ref, dstate):
        @pl.when(pl.program_id(1) == 0)
        def _():
            dstate[...] = jnp.zeros_like(dstate)

        consts = _chunk_consts()
        for h in range(NH):
            c = _head_cols(h)
            dr, dlw, dk2, dv, daa, dbb, dh0 = _chunk_bwd(
                r_ref[:, c], lw_ref[:, c], k2_ref[:, c], v_ref[:, c], aa_ref[:, c], bb_ref[:, c],
                hs_ref[0, c, :], dy_ref[:, c], dstate[c, :], consts)
            dr_ref[:, c] = dr
            dlw_ref[:, c] = dlw
            dk2_ref[:, c] = dk2
            dv_ref[:, c] = dv
            daa_ref[:, c] = daa
            dbb_ref[:, c] = dbb
            dstate[c, :] = dh0

    blk = pl.BlockSpec((L, W), lambda b, c: (b * NC + NC - 1 - c, 0))
    return pl.pallas_call(
        body, grid=(NSEQ, NC), name="rwkv_bwd",
        in_specs=[blk] * 6 + [pl.BlockSpec((1, W, HD), lambda b, c: (b * NC + NC - 1 - c, 0, 0)), blk],
        out_specs=[blk] * 6,
        out_shape=[jax.ShapeDtypeStruct((T, W), F32)] * 6,
        scratch_shapes=[pltpu.VMEM((W, HD), F32)],
        compiler_params=_cparams(("parallel", "arbitrary")),
    )(r, lw, k2, v, aa, bb, hs, dy)


def _post_math(y, r, k2, v, ga, o, gb, lng, lnb, rk, bd):
    mu = _hdot(y, bd) * (1.0 / HD)
    yc = y - mu
    var = _hdot(yc * yc, bd) * (1.0 / HD)
    yn = yc * lax.rsqrt(var + GN_EPS) * lng + lnb
    bonus = _hdot(r * k2 * rk, bd) * v
    return (yn + bonus) * _silu(ga), o * _silu(gb)


def even_post(y, r, k2, v, ga, o, gb, lng, lnb, rk):
    tm = 256

    def body(y_ref, r_ref, k2_ref, v_ref, ga_ref, o_ref, gb_ref, lng_ref, lnb_ref, rk_ref, z_ref):
        ya, yb = _post_math(y_ref[...], r_ref[...], k2_ref[...], v_ref[...], ga_ref[...], o_ref[...], gb_ref[...],
                            lng_ref[...], lnb_ref[...], rk_ref[...], _head_blockdiag())
        z_ref[:, 0:W] = ya.astype(BF16)
        z_ref[:, W:2 * W] = yb.astype(BF16)

    vec = _const_spec((1, W))
    return pl.pallas_call(
        body, grid=(T // tm,), name="even_post",
        in_specs=[_row_spec(tm, W)] * 7 + [vec] * 3,
        out_specs=_row_spec(tm, D), out_shape=jax.ShapeDtypeStruct((T, D), BF16),
        compiler_params=_cparams(("parallel",)),
    )(y, r, k2, v, ga, o, gb, lng, lnb, rk)


def even_post_bwd(y, r, k2, v, ga, o, gb, lng, lnb, rk, dz):
    tm = 256

    def body(y_ref, r_ref, k2_ref, v_ref, ga_ref, o_ref, gb_ref, lng_ref, lnb_ref, rk_ref, dz_ref,
             dy_ref, dr_ref, dk2_ref, dv_ref, dga_ref, do_ref, dgb_ref, dlng_ref, dlnb_ref, drk_ref):
        bd = _head_blockdiag()
        _, vjp = jax.vjp(lambda *a: _post_math(*a, bd), y_ref[...], r_ref[...], k2_ref[...], v_ref[...], ga_ref[...],
                         o_ref[...], gb_ref[...], lng_ref[...], lnb_ref[...], rk_ref[...])
        dzv = dz_ref[...]
        dy, dr, dk2, dv, dga, do, dgb, dlng, dlnb, drk = vjp((dzv[:, 0:W], dzv[:, W:2 * W]))
        for ref, val in ((dy_ref, dy), (dr_ref, dr), (dk2_ref, dk2), (dv_ref, dv), (dga_ref, dga), (do_ref, do),
                         (dgb_ref, dgb)):
            ref[...] = val

        @pl.when(pl.program_id(0) == 0)
        def _():
            for ref in (dlng_ref, dlnb_ref, drk_ref):
                ref[...] = jnp.zeros_like(ref)

        dlng_ref[...] += dlng
        dlnb_ref[...] += dlnb
        drk_ref[...] += drk

    vec = _const_spec((1, W))
    return pl.pallas_call(
        body, grid=(T // tm,), name="even_post_bwd",
        in_specs=[_row_spec(tm, W)] * 7 + [vec] * 3 + [_row_spec(tm, D)],
        out_specs=[_row_spec(tm, W)] * 7 + [vec] * 3,
        out_shape=[jax.ShapeDtypeStruct((T, W), F32)] * 7 + [jax.ShapeDtypeStruct((1, W), F32)] * 3,
        compiler_params=_cparams(("arbitrary",)),
    )(y, r, k2, v, ga, o, gb, lng, lnb, rk, dz)


PADSEQ = SEQ + LEFT * L
ATT_SCALE = 1.0 / math.sqrt(HD)


def _att_probs(qh, kwh, bias_h, c):
    s = _bdot_nt(qh, kwh) * ATT_SCALE + bias_h
    valid = _iota2((1, BAND), 1) >= (LEFT - c) * L
    s = jnp.where(valid, s, NEG)
    e = jnp.exp(s - jnp.max(s, axis=-1, keepdims=True))
    return e / jnp.sum(e, axis=-1, keepdims=True)


def attention_fwd(q, kpad, vpad, bias):
    def body(q_ref, k_ref, v_ref, b_ref, o_ref):
        c = pl.program_id(1)
        start = pl.multiple_of(c * L, L)
        kw = k_ref[pl.ds(start, BAND), :]
        vw = v_ref[pl.ds(start, BAND), :]
        for h in range(NH):
            cs = _head_cols(h)
            p = _att_probs(q_ref[:, cs], kw[:, cs], b_ref[h], c)
            o_ref[:, cs] = _bdot(p, vw[:, cs])

    qblk = pl.BlockSpec((L, W), lambda b, c: (b * NC + c, 0))
    kblk = pl.BlockSpec((PADSEQ, W), lambda b, c: (b, 0))
    return pl.pallas_call(
        body, grid=(NSEQ, NC), name="attention_fwd",
        in_specs=[qblk, kblk, kblk, _const_spec((NH, L, BAND))],
        out_specs=qblk, out_shape=jax.ShapeDtypeStruct((T, W), F32),
        compiler_params=_cparams(("parallel", "arbitrary")),
    )(q, kpad, vpad, bias)


def attention_bwd(q, kpad, vpad, bias, do):
    def body(q_ref, k_ref, v_ref, b_ref, do_ref, dq_ref, dk_ref, dv_ref, db_ref):
        b = pl.program_id(0)
        c = pl.program_id(1)

        @pl.when(c == 0)
        def _():
            dk_ref[...] = jnp.zeros_like(dk_ref)
            dv_ref[...] = jnp.zeros_like(dv_ref)

        @pl.when((c == 0) & (b == 0))
        def _():
            db_ref[...] = jnp.zeros_like(db_ref)

        start = pl.multiple_of(c * L, L)
        kw = k_ref[pl.ds(start, BAND), :]
        vw = v_ref[pl.ds(start, BAND), :]
        for h in range(NH):
            cs = _head_cols(h)
            qh = q_ref[:, cs]
            doh = do_ref[:, cs]
            p = _att_probs(qh, kw[:, cs], b_ref[h], c)
            dp = _bdot_nt(doh, vw[:, cs])
            ds = p * (dp - jnp.sum(dp * p, axis=-1, keepdims=True))
            db_ref[h] += ds
            dss = ds * ATT_SCALE
            dq_ref[:, cs] = _bdot(dss, kw[:, cs])
            dk_ref[pl.ds(start, BAND), cs] += _bdot_tn(dss, qh)
            dv_ref[pl.ds(start, BAND), cs] += _bdot_tn(p, doh)

    qblk = pl.BlockSpec((L, W), lambda b, c: (b * NC + c, 0))
    kblk = pl.BlockSpec((PADSEQ, W), lambda b, c: (b, 0))
    bblk = _const_spec((NH, L, BAND))
    return pl.pallas_call(
        body, grid=(NSEQ, NC), name="attention_bwd",
        in_specs=[qblk, kblk, kblk, bblk, qblk],
        out_specs=[qblk, kblk, kblk, bblk],
        out_shape=[jax.ShapeDtypeStruct((T, W), F32), jax.ShapeDtypeStruct((NSEQ * PADSEQ, W), F32),
                   jax.ShapeDtypeStruct((NSEQ * PADSEQ, W), F32), jax.ShapeDtypeStruct((NH, L, BAND), F32)],
        compiler_params=_cparams(("arbitrary", "arbitrary"), VMEM_BIG),
    )(q, kpad, vpad, bias, do)


def attention_bias(table):
    n = np.arange(BAND + L - 1)
    idx = np.clip(LEFT * L + (L - 1) - n, -CLIP, CLIP) + CLIP
    lo = int(idx.min())
    rev = jnp.flip(table[:, lo:], axis=1)
    n_top = int((idx == 2 * CLIP).sum())
    ext = jnp.concatenate([jnp.broadcast_to(table[:, 2 * CLIP:], (NH, n_top - 1)), rev], axis=1)
    return jnp.stack([ext[:, L - 1 - i:L - 1 - i + BAND] for i in range(L)], axis=1)


def _group_cols(g):
    return slice(g * SGC, (g + 1) * SGC)


def _sg_norm(v, lng, lnb):
    gv = _gelu(v)
    gc = gv - jnp.mean(gv, axis=-1, keepdims=True)
    rstd = lax.rsqrt(jnp.mean(gc * gc, axis=-1, keepdims=True) + LN_EPS)
    xhat = gc * rstd
    return xhat, rstd, xhat * lng + lnb


def gmlp_fwd(u, v, gate, lng, lnb, wm_bf, sgb_t):
    def body(u_ref, v_ref, gt_ref, lng_ref, lnb_ref, wm_ref, sb_ref, z_ref):
        _, _, vln = _sg_norm(v_ref[...], lng_ref[...], lnb_ref[...])
        vlb = vln.astype(BF16)
        for g in range(NG):
            cs = _group_cols(g)
            sv = jnp.dot(wm_ref[g], vlb[:, cs], preferred_element_type=F32) + sb_ref[:, g:g + 1]
            z_ref[:, cs] = (_gelu(u_ref[:, cs]) * sv * _silu(gt_ref[:, cs])).astype(BF16)

    return pl.pallas_call(
        body, grid=(T // SGC,), name="gmlp_fwd",
        in_specs=[_row_spec(SGC, D)] * 3 + [_const_spec((1, D))] * 2 + [_const_spec((NG, SGC, SGC)),
                                                                      _const_spec((SGC, NG))],
        out_specs=_row_spec(SGC, D), out_shape=jax.ShapeDtypeStruct((T, D), BF16),
        compiler_params=_cparams(("parallel",)),
    )(u, v, gate, lng, lnb, wm_bf, sgb_t)


def gmlp_bwd(u, v, gate, lng, lnb, wm_bf, sgb_t, dz):
    def body(u_ref, v_ref, gt_ref, lng_ref, lnb_ref, wm_ref, sb_ref, dz_ref,
             du_ref, dv_ref, dgt_ref, dlng_ref, dlnb_ref, dwm_ref, dsb_ref):
        @pl.when(pl.program_id(0) == 0)
        def _():
            for ref in (dlng_ref, dlnb_ref, dwm_ref, dsb_ref):
                ref[...] = jnp.zeros_like(ref)

        vv = v_ref[...]
        xhat, rstd, vln = _sg_norm(vv, lng_ref[...], lnb_ref[...])
        vlb = vln.astype(BF16)
        dvln = []
        dsv_all = []
        for g in range(NG):
            cs = _group_cols(g)
            uu = u_ref[:, cs]
            gg = gt_ref[:, cs]
            dzz = dz_ref[:, cs]
            sv = jnp.dot(wm_ref[g], vlb[:, cs], preferred_element_type=F32) + sb_ref[:, g:g + 1]
            gu = _gelu(uu)
            sg = _silu(gg)
            dsv = dzz * gu * sg
            dgt_ref[:, cs] = dzz * gu * sv * _dsilu(gg)
            du_ref[:, cs] = dzz * sv * sg * _dgelu(uu)
            dsb16 = dsv.astype(BF16)
            dvln.append(lax.dot_general(wm_ref[g], dsb16, (((0,), (0,)), ((), ())), preferred_element_type=F32))
            dwm_ref[g] += lax.dot_general(dsb16, vlb[:, cs], (((1,), (1,)), ((), ())), preferred_element_type=F32)
            dsv_all.append(dsv)
        dvl = jnp.concatenate(dvln, axis=-1)
        dsv_cat = jnp.concatenate(dsv_all, axis=-1)
        sel = (_iota2((D, NG), 0) // SGC == _iota2((D, NG), 1)).astype(F32)
        dsb_ref[...] += _hdot(dsv_cat, sel)
        dlng_ref[...] += jnp.sum(dvl * xhat, axis=0, keepdims=True)
        dlnb_ref[...] += jnp.sum(dvl, axis=0, keepdims=True)
        dxh = dvl * lng_ref[...]
        dgv = rstd * (dxh - jnp.mean(dxh, axis=-1, keepdims=True)
                      - xhat * jnp.mean(dxh * xhat, axis=-1, keepdims=True))
        dv_ref[...] = dgv * _dgelu(vv)

    return pl.pallas_call(
        body, grid=(T // SGC,), name="gmlp_bwd",
        in_specs=[_row_spec(SGC, D)] * 3 + [_const_spec((1, D))] * 2
        + [_const_spec((NG, SGC, SGC)), _const_spec((SGC, NG)), _row_spec(SGC, D)],
        out_specs=[_row_spec(SGC, D)] * 3 + [_const_spec((1, D))] * 2 + [_const_spec((NG, SGC, SGC)),
                                                                       _const_spec((SGC, NG))],
        out_shape=[jax.ShapeDtypeStruct((T, D), F32)] * 3 + [jax.ShapeDtypeStruct((1, D), F32)] * 2
        + [jax.ShapeDtypeStruct((NG, SGC, SGC), F32), jax.ShapeDtypeStruct((SGC, NG), F32)],
        compiler_params=_cparams(("arbitrary",)),
    )(u, v, gate, lng, lnb, wm_bf, sgb_t, dz)


NCHIP = 4
NDEV = 8
ANY = pl.BlockSpec(memory_space=pl.ANY)


def exchange_xy(arrs, scatter, name):
    n = len(arrs)

    def body(*refs):
        ins, outs = refs[:n], refs[n:2 * n]
        send, recv, loc = refs[2 * n:]
        x, y, c = lax.axis_index("x"), lax.axis_index("y"), lax.axis_index("c")
        me = 2 * x + y
        peers = [(1 - x, y), (x, 1 - y), (1 - x, 1 - y)]

        def src(t, chip):
            return ins[t].at[chip] if scatter else ins[t]

        local = [pltpu.make_async_copy(src(t, me), outs[t].at[me], loc.at[t]) for t in range(n)]
        for cp in local:
            cp.start()
        remote = []
        for t in range(n):
            for j, (px, py) in enumerate(peers):
                remote.append((pltpu.make_async_remote_copy(
                    src_ref=src(t, 2 * px + py), dst_ref=outs[t].at[me], send_sem=send.at[t, j], recv_sem=recv.at[t, j],
                    device_id=(px, py, c), device_id_type=MESH), t, j, 2 * px + py))
        for cp, _, _, _ in remote:
            cp.start()
        for cp, t, j, chip in remote:
            cp.wait_send()
            pltpu.make_async_remote_copy(
                src_ref=src(t, chip), dst_ref=outs[t].at[chip], send_sem=send.at[t, j], recv_sem=recv.at[t, j],
                device_id=(x, y, c), device_id_type=MESH).wait_recv()
        for cp in local:
            cp.wait()

    def out_sds(a):
        shape = a.shape if scatter else (NCHIP,) + a.shape
        return jax.ShapeDtypeStruct(shape, a.dtype)

    return pl.pallas_call(
        body, name=name, in_specs=[ANY] * n, out_specs=[ANY] * n, out_shape=[out_sds(a) for a in arrs],
        scratch_shapes=[pltpu.SemaphoreType.DMA((n, 3)), pltpu.SemaphoreType.DMA((n, 3)), pltpu.SemaphoreType.DMA((n,))],
    )(*arrs)


def exchange_c(arrs, name):
    n = len(arrs)

    def body(*refs):
        ins, outs = refs[:n], refs[n:2 * n]
        send, recv = refs[2 * n:]
        sibling = (lax.axis_index("x"), lax.axis_index("y"), 1 - lax.axis_index("c"))
        copies = [pltpu.make_async_remote_copy(src_ref=ins[t], dst_ref=outs[t], send_sem=send.at[t], recv_sem=recv.at[t],
                                               device_id=sibling, device_id_type=MESH) for t in range(n)]
        for cp in copies:
            cp.start()
        for cp in copies:
            cp.wait()

    return pl.pallas_call(
        body, name=name, in_specs=[ANY] * n, out_specs=[ANY] * n,
        out_shape=[jax.ShapeDtypeStruct(a.shape, a.dtype) for a in arrs],
        scratch_shapes=[pltpu.SemaphoreType.DMA((n,)), pltpu.SemaphoreType.DMA((n,))],
    )(*arrs)


def allgather_all(buf, name):
    def body(in_ref, out_ref, send, recv, loc):
        x, y, c = lax.axis_index("x"), lax.axis_index("y"), lax.axis_index("c")
        me = 4 * x + 2 * y + c
        mine = pltpu.make_async_copy(in_ref, out_ref.at[me], loc)
        mine.start()
        copies = []
        for j in range(1, NDEV):
            px, py, pc = x ^ (j >> 2), y ^ ((j >> 1) & 1), c ^ (j & 1)
            copies.append((pltpu.make_async_remote_copy(
                src_ref=in_ref, dst_ref=out_ref.at[me], send_sem=send.at[j], recv_sem=recv.at[j],
                device_id=(px, py, pc), device_id_type=MESH), j, 4 * px + 2 * py + pc))
        for cp, _, _ in copies:
            cp.start()
        for cp, j, peer in copies:
            cp.wait_send()
            pltpu.make_async_remote_copy(
                src_ref=in_ref, dst_ref=out_ref.at[peer], send_sem=send.at[j], recv_sem=recv.at[j],
                device_id=(x, y, c), device_id_type=MESH).wait_recv()
        mine.wait()

    return pl.pallas_call(
        body, name=name, in_specs=[ANY], out_specs=ANY,
        out_shape=jax.ShapeDtypeStruct((NDEV,) + buf.shape, buf.dtype),
        scratch_shapes=[pltpu.SemaphoreType.DMA((NDEV,)), pltpu.SemaphoreType.DMA((NDEV,)), pltpu.SemaphoreType.DMA],
    )(buf)


def _adam_math(g, w, m, v):
    m = ADAM_B1 * m + (1.0 - ADAM_B1) * g
    v = ADAM_B2 * v + (1.0 - ADAM_B2) * (g * g)
    m_hat = m / (1.0 - ADAM_B1 ** ADAM_STEP)
    v_hat = v / (1.0 - ADAM_B2 ** ADAM_STEP)
    delta = -ADAM_LR * (m_hat / (jnp.sqrt(v_hat) + ADAM_EPS) + ADAM_WD * w)
    return delta, m, v


def _rows_tile(rows):
    return rows if rows <= 256 else 256


def sum_chips(parts, name):
    _, rows, cols = parts.shape
    tr = _rows_tile(rows)

    def body(p_ref, o_ref):
        o_ref[...] = ((p_ref[0] + p_ref[1]) + p_ref[2]) + p_ref[3]

    return pl.pallas_call(
        body, grid=(rows // tr,), name=name,
        in_specs=[pl.BlockSpec((NCHIP, tr, cols), lambda i: (0, i, 0))],
        out_specs=pl.BlockSpec((tr, cols), lambda i: (i, 0)),
        out_shape=jax.ShapeDtypeStruct((rows, cols), F32),
        compiler_params=_cparams(("parallel",)),
    )(parts)


def adam_shard(p_mine, p_sib, w, m, v, name):
    rows, cols = w.shape
    tr = _rows_tile(rows)

    def body(a_ref, b_ref, w_ref, m_ref, v_ref, g_ref, d_ref, mo_ref, vo_ref):
        g = a_ref[...] + b_ref[...]
        g_ref[...] = g
        d_ref[...], mo_ref[...], vo_ref[...] = _adam_math(g, w_ref[...], m_ref[...], v_ref[...])

    spec = pl.BlockSpec((tr, cols), lambda i: (i, 0))
    return pl.pallas_call(
        body, grid=(rows // tr,), name=name, in_specs=[spec] * 5, out_specs=[spec] * 4,
        out_shape=[jax.ShapeDtypeStruct((rows, cols), F32)] * 4,
        compiler_params=_cparams(("parallel",)),
    )(p_mine, p_sib, w, m, v)


def adam_replicated(parts, w, m, v):
    rows = w.shape[0]

    def body(p_ref, w_ref, m_ref, v_ref, g_ref, d_ref, mo_ref, vo_ref):
        g = p_ref[0]
        for d in range(1, NDEV):
            g = g + p_ref[d]
        g_ref[...] = g
        d_ref[...], mo_ref[...], vo_ref[...] = _adam_math(g, w_ref[...], m_ref[...], v_ref[...])

    return pl.pallas_call(
        body, name="adam_replicated", out_shape=[jax.ShapeDtypeStruct((rows, 128), F32)] * 4,
    )(parts, w, m, v)


def _pack(arrs):
    pieces = []
    for a in arrs:
        flat = a.reshape(-1)
        pad = (-flat.shape[0]) % 128
        pieces.append(jnp.pad(flat, (0, pad)) if pad else flat)
    flat = jnp.concatenate(pieces)
    pad = (-flat.shape[0]) % 1024
    return jnp.pad(flat, (0, pad)).reshape(-1, 128)


def _unpack(buf, shapes):
    flat = buf.reshape(-1)
    out = []
    o = 0
    for s in shapes:
        n = int(np.prod(s))
        out.append(flat[o:o + n].reshape(s))
        o += n + (-n) % 128
    return out


EVEN_SPLITS = (SHIFT, W, W, W, W, W)
ODD_SPLITS = (D, D, D)


def _cols_to_chips(a):
    rows, cols = a.shape
    return a.reshape(rows, NCHIP, cols // NCHIP).transpose(1, 0, 2)


def _chips_to_cols(a):
    _, rows, n = a.shape
    return a.transpose(1, 0, 2).reshape(rows, NCHIP * n)


def kernel(x, norm_g, w_in_e, shift_mu, rw_w0, rw_w2, rw_a0, rw_a2, rw_kk, rw_ka, rw_rk, rw_lnx_g, rw_lnx_b, att_bias, w_out_e, w_in_o, sg_ln_g, sg_ln_b, sg_w, sg_b, w_out_o, final_g, loss_target, m_norm_g, m_w_in_e, m_shift_mu, m_rw_w0, m_rw_w2, m_rw_a0, m_rw_a2, m_rw_kk, m_rw_ka, m_rw_rk, m_rw_lnx_g, m_rw_lnx_b, m_att_bias, m_w_out_e, m_w_in_o, m_sg_ln_g, m_sg_ln_b, m_sg_w, m_sg_b, m_w_out_o, m_final_g, v_norm_g, v_w_in_e, v_shift_mu, v_rw_w0, v_rw_w2, v_rw_a0, v_rw_a2, v_rw_kk, v_rw_ka, v_rw_rk, v_rw_lnx_g, v_rw_lnx_b, v_att_bias, v_w_out_e, v_w_in_o, v_sg_ln_g, v_sg_ln_b, v_sg_w, v_sg_b, v_w_out_o, v_final_g):
    x2 = x.reshape(T, D)
    tgt = loss_target.reshape(T, D)

    gathered = exchange_xy(
        [w_in_e[0].astype(BF16), w_out_e[0].astype(BF16), w_in_o[0].astype(BF16), w_out_o[0].astype(BF16),
         rw_w2[0], rw_a2[0], sg_ln_g, sg_ln_b], False, "gather_weights")
    wie = _chips_to_cols(gathered[0])
    woe = gathered[1].reshape(D, D)
    wio = _chips_to_cols(gathered[2])
    woo = gathered[3].reshape(D, D)
    w2 = _chips_to_cols(gathered[4])
    a2 = _chips_to_cols(gathered[5])
    sglg = _chips_to_cols(gathered[6])
    sglb = _chips_to_cols(gathered[7])
    loss_part, dx, big_g, rep_g = _local_step(
        x2, tgt, wie, woe, wio, woo, w2, a2, sglg, sglb, norm_g, shift_mu, rw_w0, rw_a0, rw_kk, rw_ka, rw_rk,
        rw_lnx_g, rw_lnx_b, att_bias, sg_w, sg_b, final_g)
    loss = lax.psum(loss_part, ("x", "y", "c"))
    d_wie, d_woe, d_wio, d_woo, d_w2, d_a2, d_sglg, d_sglb = big_g

    local = [_cols_to_chips(d_wie), d_woe.reshape(NCHIP, D // NCHIP, D), _cols_to_chips(d_wio),
             d_woo.reshape(NCHIP, D // NCHIP, D), _cols_to_chips(d_w2), _cols_to_chips(d_a2),
             _cols_to_chips(d_sglg), _cols_to_chips(d_sglb)]
    names = ["w_in_e", "w_out_e", "w_in_o", "w_out_o", "rw_w2", "rw_a2", "sg_ln_g", "sg_ln_b"]
    received = exchange_xy(local, True, "scatter_grads")
    partial = [sum_chips(p, "sum_" + nm) for p, nm in zip(received, names)]
    from_sibling = exchange_c(partial, "swap_partials")
    sharded = {}
    wmv = {"w_in_e": (w_in_e[0], m_w_in_e[0], v_w_in_e[0]), "w_out_e": (w_out_e[0], m_w_out_e[0], v_w_out_e[0]),
           "w_in_o": (w_in_o[0], m_w_in_o[0], v_w_in_o[0]), "w_out_o": (w_out_o[0], m_w_out_o[0], v_w_out_o[0]),
           "rw_w2": (rw_w2[0], m_rw_w2[0], v_rw_w2[0]), "rw_a2": (rw_a2[0], m_rw_a2[0], v_rw_a2[0]),
           "sg_ln_g": (sg_ln_g, m_sg_ln_g, v_sg_ln_g), "sg_ln_b": (sg_ln_b, m_sg_ln_b, v_sg_ln_b)}
    for nm, mine, sib in zip(names, partial, from_sibling):
        w_, m_, v_ = wmv[nm]
        res = adam_shard(mine, sib, w_, m_, v_, "adam_" + nm)
        lead = nm not in ("sg_ln_g", "sg_ln_b")
        sharded[nm] = [a[None] if lead else a for a in res]

    rep_names = ["norm_g", "shift_mu", "rw_w0", "rw_a0", "rw_kk", "rw_ka", "rw_rk", "rw_lnx_g", "rw_lnx_b",
                 "att_bias", "sg_w", "sg_b", "final_g"]
    rep_w = [norm_g, shift_mu, rw_w0, rw_a0, rw_kk, rw_ka, rw_rk, rw_lnx_g, rw_lnx_b, att_bias, sg_w, sg_b, final_g]
    rep_m = [m_norm_g, m_shift_mu, m_rw_w0, m_rw_a0, m_rw_kk, m_rw_ka, m_rw_rk, m_rw_lnx_g, m_rw_lnx_b, m_att_bias,
             m_sg_w, m_sg_b, m_final_g]
    rep_v = [v_norm_g, v_shift_mu, v_rw_w0, v_rw_a0, v_rw_kk, v_rw_ka, v_rw_rk, v_rw_lnx_g, v_rw_lnx_b, v_att_bias,
             v_sg_w, v_sg_b, v_final_g]
    shapes = [w_.shape for w_ in rep_w]
    gathered_g = allgather_all(_pack(rep_g), "gather_small_grads")
    rep_out = adam_replicated(gathered_g, _pack(rep_w), _pack(rep_m), _pack(rep_v))
    rep = {nm: [] for nm in rep_names}
    for buf in rep_out:
        for nm, a in zip(rep_names, _unpack(buf, shapes)):
            rep[nm].append(a)

    order = ["norm_g", "w_in_e", "shift_mu", "rw_w0", "rw_w2", "rw_a0", "rw_a2", "rw_kk", "rw_ka", "rw_rk",
             "rw_lnx_g", "rw_lnx_b", "att_bias", "w_out_e", "w_in_o", "sg_ln_g", "sg_ln_b", "sg_w", "sg_b",
             "w_out_o", "final_g"]
    results = {**sharded, **rep}
    outs = [loss, dx.reshape(NSEQ, SEQ, D)]
    for kind in range(4):
        outs += [results[nm][kind] for nm in order]
    return tuple(outs)


def _local_step(x2, tgt, wie, woe, wio, woo, w2, a2, sglg, sglb, norm_g, shift_mu, rw_w0, rw_a0, rw_kk, rw_ka, rw_rk,
                rw_lnx_g, rw_lnx_b, att_bias, sg_w, sg_b, final_g):
    zl = jnp.zeros((LORA, W), F32)
    w2x = jnp.concatenate([w2, zl], axis=0)
    a2x = jnp.concatenate([zl, a2], axis=0)
    rk = rw_rk.reshape(1, W)
    pos = np.arange(SGC)
    sg_mask = jnp.asarray(((pos[None, :] // L) <= (pos[:, None] // L)).astype(np.float32))
    wm = (sg_w[0] * sg_mask[None]).astype(BF16)
    sgb_t = sg_b[0].T

    xn0, ps, ga, q, kb, vb, gb = ln_in_proj(x2, norm_g[0:1], wie, EVEN_SPLITS, "in_proj_even")
    r, lw, k2, v, aa, bb = even_prep(ps, shift_mu, rw_w0, w2x, rw_a0, a2x, rw_kk, rw_ka)
    y, hs = rwkv_fwd(r, lw, k2, v, aa, bb)
    bias, bias_vjp = jax.vjp(attention_bias, att_bias[0])

    def padded(a):
        return jnp.pad(a.astype(BF16).reshape(NSEQ, SEQ, W), ((0, 0), (LEFT * L, 0), (0, 0))).reshape(NSEQ * PADSEQ, W)

    kpad, vpad = padded(kb), padded(vb)
    o = attention_fwd(q, kpad, vpad, bias)
    z = even_post(y, r, k2, v, ga, o, gb, rw_lnx_g, rw_lnx_b, rk)
    h1 = out_proj(x2, z, woe, "out_proj_even")
    xn1, u, vv, gt = ln_in_proj(h1, norm_g[1:2], wio, ODD_SPLITS, "in_proj_odd")
    z2 = gmlp_fwd(u, vv, gt, sglg, sglb, wm, sgb_t)
    h2 = out_proj(h1, z2, woo, "out_proj_odd")
    dh2, loss_part, d_final_g = final_loss(h2, final_g[None], tgt)

    dz2, d_woo = out_proj_bwd(dh2, z2, woo, "out_proj_odd_bwd")
    du, dvv, dgt, d_sglg, d_sglb, d_wm, d_sgb_t = gmlp_bwd(u, vv, gt, sglg, sglb, wm, sgb_t, dz2)
    dh1, d_g1, dp2 = in_proj_bwd_x(h1, norm_g[1:2], wio, [du, dvv, dgt], dh2, "in_proj_odd_bwd")
    d_wio = matmul_tn_acc(xn1, dp2, 512, "in_proj_odd_dw")
    dz, d_woe = out_proj_bwd(dh1, z, woe, "out_proj_even_bwd")
    dy, dr2, dk22, dv2, dga, do, dgb, d_lng, d_lnb, d_rk = even_post_bwd(
        y, r, k2, v, ga, o, gb, rw_lnx_g, rw_lnx_b, rk, dz)
    dq, dkpad, dvpad, dbias = attention_bwd(q, kpad, vpad, bias, do)

    def unpadded(a):
        return a.reshape(NSEQ, PADSEQ, W)[:, LEFT * L:].reshape(T, W)

    (d_att_bias,) = bias_vjp(dbias)
    dr, dlw, dk2, dv, daa, dbb = rwkv_bwd(r, lw, k2, v, aa, bb, hs, dy)
    dps, d_mu, d_w0, d_w2x, d_a0, d_a2x, d_kk, d_ka = even_prep_bwd(
        ps, shift_mu, rw_w0, w2x, rw_a0, a2x, rw_kk, rw_ka, dr, dlw, dk2, dv, daa, dbb, dr2, dk22, dv2)
    dx, d_g0, dp = in_proj_bwd_x(x2, norm_g[0:1], wie, [dps, dga, dq, unpadded(dkpad), unpadded(dvpad), dgb], dh1,
                                 "in_proj_even_bwd")
    d_wie = matmul_tn_acc(xn0, dp, 384, "in_proj_even_dw")

    big_g = (d_wie, d_woe, d_wio, d_woo, d_w2x[:LORA], d_a2x[LORA:], d_sglg, d_sglb)
    rep_g = [jnp.concatenate([d_g0, d_g1], axis=0), d_mu, d_w0, d_a0, d_kk, d_ka, d_rk, d_lng, d_lnb, d_att_bias,
             d_wm * sg_mask[None], d_sgb_t.T, d_final_g]
    return loss_part[0, 0], dx, big_g, rep_g
```

```python
import functools
import math

import jax
import jax.numpy as jnp
import numpy as np
from jax import lax
from jax.experimental import pallas as pl
from jax.experimental.pallas import tpu as pltpu

F32 = jnp.float32
BF16 = jnp.bfloat16
HI = lax.Precision.HIGHEST

D = 1024
SEQ = 2048
NSEQ = 2
T = NSEQ * SEQ
HD = 64
NH = 8
W = 512
SHIFT = 1664
LORA = 64
EVEN_IN = 4224
ODD_IN = 3072
L = 64
NC = SEQ // L
LEFT = 8
BAND = (LEFT + 1) * L
CLIP = 128
SGC = 128
NG = 8
RMS_EPS = 1e-6
LN_EPS = 1e-5
GN_EPS = 64e-5
NEG = -1e30
VMEM_BIG = 56 * 1024 * 1024

ADAM_LR = 0.001
ADAM_B1 = 0.9
ADAM_B2 = 0.999
ADAM_EPS = 1e-08
ADAM_WD = 0.01
ADAM_STEP = 10

MESH = pl.DeviceIdType.MESH


def _bdot(a, b):
    return jnp.dot(a.astype(BF16), b.astype(BF16), preferred_element_type=F32)


def _bdot_nt(a, b):
    return lax.dot_general(a.astype(BF16), b.astype(BF16), (((1,), (1,)), ((), ())), preferred_element_type=F32)


def _bdot_tn(a, b):
    return lax.dot_general(a.astype(BF16), b.astype(BF16), (((0,), (0,)), ((), ())), preferred_element_type=F32)


def _hdot(a, b):
    return jnp.dot(a, b, precision=HI, preferred_element_type=F32)


def _hdot_nt(a, b):
    return lax.dot_general(a, b, (((1,), (1,)), ((), ())), precision=HI, preferred_element_type=F32)


def _hdot_tn(a, b):
    return lax.dot_general(a, b, (((0,), (0,)), ((), ())), precision=HI, preferred_element_type=F32)


def _iota2(shape, dim):
    return lax.broadcasted_iota(jnp.int32, shape, dim)


def _head_blockdiag():
    r = _iota2((W, W), 0) // HD
    c = _iota2((W, W), 1) // HD
    return (r == c).astype(F32)


def _silu(x):
    return x * jax.nn.sigmoid(x)


def _dsilu(x):
    s = jax.nn.sigmoid(x)
    return s * (1.0 + x * (1.0 - s))


_GELU_C = math.sqrt(2.0 / math.pi)


def _gelu(x):
    return 0.5 * x * (1.0 + jnp.tanh(_GELU_C * (x + 0.044715 * (x * x * x))))


def _dgelu(x):
    t = jnp.tanh(_GELU_C * (x + 0.044715 * (x * x * x)))
    return 0.5 * (1.0 + t) + 0.5 * x * (1.0 - t * t) * _GELU_C * (1.0 + 3.0 * 0.044715 * x * x)


def _softplus(x):
    return jnp.maximum(x, 0.0) + jnp.log(1.0 + jnp.exp(-jnp.abs(x)))


def _cparams(sem, vmem=None):
    return pltpu.CompilerParams(dimension_semantics=sem, vmem_limit_bytes=vmem)


def _row_spec(tm, width):
    return pl.BlockSpec((tm, width), lambda i: (i, 0))


def _const_spec(shape):
    nd = len(shape)
    return pl.BlockSpec(shape, lambda *_: (0,) * nd)


def ln_in_proj(x, g, w_bf, splits, name):
    n = w_bf.shape[1]
    tm = 256
    spans = []
    o = 0
    for s in splits:
        spans.append((o, o + s))
        o += s
    assert o == n

    def body(x_ref, g_ref, w_ref, xn_ref, *outs):
        xv = x_ref[...]
        rstd = lax.rsqrt(jnp.mean(xv * xv, axis=-1, keepdims=True) + RMS_EPS)
        xn = (xv * rstd * g_ref[...]).astype(BF16)
        xn_ref[...] = xn
        p = jnp.dot(xn, w_ref[...], preferred_element_type=F32)
        for o_ref, (a, b) in zip(outs, spans):
            o_ref[...] = p[:, a:b]

    return pl.pallas_call(
        body, grid=(T // tm,), name=name,
        in_specs=[_row_spec(tm, D), _const_spec((1, D)), _const_spec((D, n))],
        out_specs=[_row_spec(tm, D)] + [_row_spec(tm, s) for s in splits],
        out_shape=[jax.ShapeDtypeStruct((T, D), BF16)] + [jax.ShapeDtypeStruct((T, s), F32) for s in splits],
        compiler_params=_cparams(("parallel",), VMEM_BIG),
    )(x, g, w_bf)


def in_proj_bwd_x(x, g, w_bf, dps, dres, name):
    n = w_bf.shape[1]
    tm = 256
    widths = [d.shape[1] for d in dps]

    def body(x_ref, g_ref, w_ref, dres_ref, *rest):
        dp_refs = rest[:len(widths)]
        dx_ref, dg_ref, dpc_ref = rest[len(widths):]
        dp = jnp.concatenate([r[...].astype(BF16) for r in dp_refs], axis=-1)
        dpc_ref[...] = dp
        dxn = lax.dot_general(dp, w_ref[...], (((1,), (1,)), ((), ())), preferred_element_type=F32)
        xv = x_ref[...]
        rstd = lax.rsqrt(jnp.mean(xv * xv, axis=-1, keepdims=True) + RMS_EPS)
        xhat = xv * rstd
        dgp = jnp.sum(dxn * xhat, axis=0, keepdims=True)

        @pl.when(pl.program_id(0) == 0)
        def _():
            dg_ref[...] = jnp.zeros_like(dg_ref)

        dg_ref[...] += dgp
        dxh = dxn * g_ref[...]
        dx_ref[...] = dres_ref[...] + rstd * (dxh - xhat * jnp.mean(dxh * xhat, axis=-1, keepdims=True))

    return pl.pallas_call(
        body, grid=(T // tm,), name=name,
        in_specs=[_row_spec(tm, D), _const_spec((1, D)), _const_spec((D, n)), _row_spec(tm, D)]
        + [_row_spec(tm, s) for s in widths],
        out_specs=[_row_spec(tm, D), _const_spec((1, D)), _row_spec(tm, n)],
        out_shape=[jax.ShapeDtypeStruct((T, D), F32), jax.ShapeDtypeStruct((1, D), F32),
                   jax.ShapeDtypeStruct((T, n), BF16)],
        compiler_params=_cparams(("arbitrary",), VMEM_BIG),
    )(x, g, w_bf, dres, *dps)


def matmul_tn_acc(a_bf, b_bf, tn, name):
    k = a_bf.shape[1]
    n = b_bf.shape[1]
    tm = 512
    nt = T // tm

    def body(a_ref, b_ref, o_ref):
        @pl.when(pl.program_id(1) == 0)
        def _():
            o_ref[...] = jnp.zeros_like(o_ref)

        o_ref[...] += lax.dot_general(a_ref[...], b_ref[...], (((0,), (0,)), ((), ())), preferred_element_type=F32)

    return pl.pallas_call(
        body, grid=(n // tn, nt), name=name,
        in_specs=[pl.BlockSpec((tm, k), lambda j, i: (i, 0)), pl.BlockSpec((tm, tn), lambda j, i: (i, j))],
        out_specs=pl.BlockSpec((k, tn), lambda j, i: (0, j)),
        out_shape=jax.ShapeDtypeStruct((k, n), F32),
        compiler_params=_cparams(("parallel", "arbitrary"), VMEM_BIG),
    )(a_bf, b_bf)


def out_proj(h, z_bf, w_bf, name):
    tm = 256

    def body(h_ref, z_ref, w_ref, o_ref):
        o_ref[...] = h_ref[...] + jnp.dot(z_ref[...], w_ref[...], preferred_element_type=F32)

    return pl.pallas_call(
        body, grid=(T // tm,), name=name,
        in_specs=[_row_spec(tm, D), _row_spec(tm, D), _const_spec((D, D))],
        out_specs=_row_spec(tm, D), out_shape=jax.ShapeDtypeStruct((T, D), F32),
        compiler_params=_cparams(("parallel",)),
    )(h, z_bf, w_bf)


def out_proj_bwd(dh, z_bf, w_bf, name):
    tm = 256

    def body(dh_ref, z_ref, w_ref, dz_ref, dw_ref):
        dhb = dh_ref[...].astype(BF16)
        dz_ref[...] = lax.dot_general(dhb, w_ref[...], (((1,), (1,)), ((), ())), preferred_element_type=F32)

        @pl.when(pl.program_id(0) == 0)
        def _():
            dw_ref[...] = jnp.zeros_like(dw_ref)

        dw_ref[...] += lax.dot_general(z_ref[...], dhb, (((0,), (0,)), ((), ())), preferred_element_type=F32)

    return pl.pallas_call(
        body, grid=(T // tm,), name=name,
        in_specs=[_row_spec(tm, D), _row_spec(tm, D), _const_spec((D, D))],
        out_specs=[_row_spec(tm, D), _const_spec((D, D))],
        out_shape=[jax.ShapeDtypeStruct((T, D), F32), jax.ShapeDtypeStruct((D, D), F32)],
        compiler_params=_cparams(("arbitrary",)),
    )(dh, z_bf, w_bf)


def final_loss(h, g, target):
    tm = 256

    def body(h_ref, g_ref, t_ref, dh_ref, loss_ref, dg_ref):
        xv = h_ref[...]
        rstd = lax.rsqrt(jnp.mean(xv * xv, axis=-1, keepdims=True) + RMS_EPS)
        xhat = xv * rstd
        err = xhat * g_ref[...] - t_ref[...]
        part = 0.5 * jnp.sum(jnp.mean(err * err, axis=-1, keepdims=True), axis=0, keepdims=True)
        dout = err * (1.0 / D)

        @pl.when(pl.program_id(0) == 0)
        def _():
            loss_ref[...] = jnp.zeros_like(loss_ref)
            dg_ref[...] = jnp.zeros_like(dg_ref)

        loss_ref[...] += jnp.broadcast_to(part, loss_ref.shape)
        dg_ref[...] += jnp.sum(dout * xhat, axis=0, keepdims=True)
        dxh = dout * g_ref[...]
        dh_ref[...] = rstd * (dxh - xhat * jnp.mean(dxh * xhat, axis=-1, keepdims=True))

    return pl.pallas_call(
        body, grid=(T // tm,), name="final_loss",
        in_specs=[_row_spec(tm, D), _const_spec((1, D)), _row_spec(tm, D)],
        out_specs=[_row_spec(tm, D), _const_spec((8, 128)), _const_spec((1, D))],
        out_shape=[jax.ShapeDtypeStruct((T, D), F32), jax.ShapeDtypeStruct((8, 128), F32),
                   jax.ShapeDtypeStruct((1, D), F32)],
        compiler_params=_cparams(("arbitrary",)),
    )(h, g, target)


PREP_TM = 256
PREP_NB = SEQ // PREP_TM


def _prep_elem(k, wl, apre, kkw, kaw, bd):
    wraw = -_softplus(-wl) - 0.5
    lw = -jnp.exp(wraw)
    asig = jax.nn.sigmoid(apre)
    kkr = k * kkw
    nrm = jnp.maximum(jnp.sqrt(_hdot(kkr * kkr, bd)), 1e-12)
    kk = kkr / nrm
    k2 = k * (1.0 + (asig - 1.0) * kaw)
    return lw, k2, -kk, kk * asig


def _shifted(ps_ref, prev_ref, mu, blk):
    p = ps_ref[...]
    first = (blk % PREP_NB) == 0
    prev_row = jnp.where(first, 0.0, prev_ref[7:8, :])
    rolled = pltpu.roll(p, 1, 0)
    p_prev = jnp.where(_iota2(p.shape, 0) == 0, prev_row, rolled)
    return p, p_prev, p + (p_prev - p) * mu


def _prev_spec(width, blk_of):
    return pl.BlockSpec((8, width), lambda i: (jnp.maximum(blk_of(i) * (PREP_TM // 8) - 1, 0), 0))


def even_prep(ps, mu, w0, w2x, a0, a2x, kkw, kaw):
    tm = PREP_TM

    def body(ps_ref, prev_ref, mu_ref, w0_ref, w2_ref, a0_ref, a2_ref, kk_ref, ka_ref,
             r_ref, lw_ref, k2_ref, v_ref, aa_ref, bb_ref):
        _, _, s = _shifted(ps_ref, prev_ref, mu_ref[...], pl.program_id(0))
        wa = s[:, 3 * W:]
        wl = w0_ref[...] + _bdot(jnp.tanh(wa), w2_ref[...])
        apre = a0_ref[...] + _bdot(wa, a2_ref[...])
        lw, k2, aa, bb = _prep_elem(s[:, W:2 * W], wl, apre, kk_ref[...], ka_ref[...], _head_blockdiag())
        r_ref[...] = s[:, 0:W]
        v_ref[...] = s[:, 2 * W:3 * W]
        lw_ref[...] = lw
        k2_ref[...] = k2
        aa_ref[...] = aa
        bb_ref[...] = bb

    vec = _const_spec((1, W))
    return pl.pallas_call(
        body, grid=(T // tm,), name="even_prep",
        in_specs=[_row_spec(tm, SHIFT), _prev_spec(SHIFT, lambda i: i), _const_spec((1, SHIFT)), vec,
                  _const_spec((2 * LORA, W)), vec, _const_spec((2 * LORA, W)), vec, vec],
        out_specs=[_row_spec(tm, W)] * 6,
        out_shape=[jax.ShapeDtypeStruct((T, W), F32)] * 6,
        compiler_params=_cparams(("parallel",)),
    )(ps, ps, mu, w0, w2x, a0, a2x, kkw, kaw)


def even_prep_bwd(ps, mu, w0, w2x, a0, a2x, kkw, kaw, dr, dlw, dk2, dv, daa, dbb, dr2, dk22, dv2):
    tm = PREP_TM
    nb = T // tm
    rev = lambda i: nb - 1 - i

    def body(ps_ref, prev_ref, mu_ref, w0_ref, w2_ref, a0_ref, a2_ref, kk_ref, ka_ref,
             dr_ref, dlw_ref, dk2_ref, dv_ref, daa_ref, dbb_ref, dr2_ref, dk22_ref, dv2_ref,
             dps_ref, dmu_ref, dw0_ref, dw2_ref, da0_ref, da2_ref, dkk_ref, dka_ref, carry):
        i = pl.program_id(0)
        blk = rev(i)
        mu_v = mu_ref[...]
        p, p_prev, s = _shifted(ps_ref, prev_ref, mu_v, blk)
        wa = s[:, 3 * W:]
        th = jnp.tanh(wa)
        wl = w0_ref[...] + _bdot(th, w2_ref[...])
        apre = a0_ref[...] + _bdot(wa, a2_ref[...])
        bd = _head_blockdiag()
        k = s[:, W:2 * W]
        _, vjp = jax.vjp(lambda k_, wl_, ap_, kkw_, kaw_: _prep_elem(k_, wl_, ap_, kkw_, kaw_, bd),
                         k, wl, apre, kk_ref[...], ka_ref[...])
        dk, dwl, dap, dkkw, dkaw = vjp((dlw_ref[...], dk2_ref[...] + dk22_ref[...], daa_ref[...], dbb_ref[...]))
        dwa = _bdot_nt(dwl, w2_ref[...]) * (1.0 - th * th) + _bdot_nt(dap, a2_ref[...])
        ds = jnp.concatenate([dr_ref[...] + dr2_ref[...], dk, dv_ref[...] + dv2_ref[...], dwa], axis=-1)

        @pl.when(i == 0)
        def _():
            for ref in (dmu_ref, dw0_ref, dw2_ref, da0_ref, da2_ref, dkk_ref, dka_ref, carry):
                ref[...] = jnp.zeros_like(ref)

        dmu_ref[...] += jnp.sum(ds * (p_prev - p), axis=0, keepdims=True)
        dw0_ref[...] += jnp.sum(dwl, axis=0, keepdims=True)
        da0_ref[...] += jnp.sum(dap, axis=0, keepdims=True)
        dw2_ref[...] += _bdot_tn(th, dwl)
        da2_ref[...] += _bdot_tn(wa, dap)
        dkk_ref[...] += dkkw
        dka_ref[...] += dkaw
        dsm = ds * mu_v
        last = (blk % PREP_NB) == PREP_NB - 1
        nxt = jnp.where(last, 0.0, carry[0:1, :])
        up = pltpu.roll(dsm, tm - 1, 0)
        up = jnp.where(_iota2(up.shape, 0) == tm - 1, nxt, up)
        dps_ref[...] = ds - dsm + up
        carry[0:1, :] = dsm[0:1, :]

    vec = _const_spec((1, W))
    rrow = lambda width: pl.BlockSpec((tm, width), lambda i: (rev(i), 0))
    return pl.pallas_call(
        body, grid=(nb,), name="even_prep_bwd",
        in_specs=[rrow(SHIFT), _prev_spec(SHIFT, rev), _const_spec((1, SHIFT)), vec,
                  _const_spec((2 * LORA, W)), vec, _const_spec((2 * LORA, W)), vec, vec] + [rrow(W)] * 9,
        out_specs=[rrow(SHIFT), _const_spec((1, SHIFT)), vec, _const_spec((2 * LORA, W)), vec,
                   _const_spec((2 * LORA, W)), vec, vec],
        out_shape=[jax.ShapeDtypeStruct((T, SHIFT), F32), jax.ShapeDtypeStruct((1, SHIFT), F32),
                   jax.ShapeDtypeStruct((1, W), F32), jax.ShapeDtypeStruct((2 * LORA, W), F32),
                   jax.ShapeDtypeStruct((1, W), F32), jax.ShapeDtypeStruct((2 * LORA, W), F32),
                   jax.ShapeDtypeStruct((1, W), F32), jax.ShapeDtypeStruct((1, W), F32)],
        scratch_shapes=[pltpu.VMEM((8, SHIFT), F32)],
        compiler_params=_cparams(("arbitrary",), VMEM_BIG),
    )(ps, ps, mu, w0, w2x, a0, a2x, kkw, kaw, dr, dlw, dk2, dv, daa, dbb, dr2, dk22, dv2)


def _chunk_masks():
    row = _iota2((2 * L, 2 * L), 0)
    col = _iota2((2 * L, 2 * L), 1)
    step = col & (L - 1)
    keep = ((row < L) & (row > step)) | ((row >= L) & (row - L >= step))
    r1 = _iota2((L, L), 0)
    c1 = _iota2((L, L), 1)
    return keep.astype(F32), (r1 >= c1).astype(F32), (r1 == c1).astype(F32)


def _scaled(r, lw, k2, aa, bb, tri):
    g = _hdot(tri, lw)
    eg = jnp.exp(g)
    eng = jnp.exp(-g)
    egp = jnp.exp(g - lw)
    return eg, eng, egp, aa * egp, r * eg, bb * eng, k2 * eng


def _head_fwd(at, rt, bt, kt, v, s0, egl, keep, eye):
    x = jnp.concatenate([at, rt], axis=0)
    yk = jnp.concatenate([bt, kt], axis=0)
    m = _bdot_nt(x, yk) * keep
    xh = _bdot_nt(x, s0)
    aab = m[:L, :L]
    tinv = eye + aab
    p = aab
    for _ in range(5):
        p = _bdot(p, p)
        tinv = tinv + _bdot(tinv, p)
    u = _bdot(tinv, xh[:L] + _bdot(m[:L, L:], v))
    uv = jnp.concatenate([u, v], axis=0)
    y = xh[L:] + _bdot(m[L:], uv)
    sn = egl * (s0 + _bdot_tn(uv, yk))
    return y, sn, (x, yk, m, tinv, uv)


def _head_bwd(at, rt, bt, kt, v, s0, egl, dy, dsn, keep, eye):
    _, sn, (x, yk, m, tinv, uv) = _head_fwd(at, rt, bt, kt, v, s0, egl, keep, eye)
    dzs = dsn * egl
    dgl = jnp.sum(dsn * sn, axis=0, keepdims=True)
    t1 = _bdot_tn(m[L:], dy)
    t2 = _bdot_nt(yk, dzs)
    drhs = _bdot_tn(tinv, t1[:L] + t2[:L])
    dv = t1[L:] + t2[L:] + _bdot_tn(m[:L, L:], drhs)
    gg = jnp.concatenate([drhs, dy], axis=0)
    ds0 = dzs + _bdot_tn(gg, x)
    dm = _bdot_nt(gg, uv) * keep
    dx = _bdot(gg, s0) + _bdot(dm, yk)
    dyk = _bdot_tn(dm, x) + _bdot(uv, dzs)
    return dx[:L], dx[L:], dyk[:L], dyk[L:], dv, dgl, ds0


def _head_cols(h):
    return slice(h * HD, (h + 1) * HD)


def rwkv_fwd(r, lw, k2, v, aa, bb):
    def body(r_ref, lw_ref, k2_ref, v_ref, aa_ref, bb_ref, y_ref, hs_ref, state):
        @pl.when(pl.program_id(1) == 0)
        def _():
            state[...] = jnp.zeros_like(state)

        hs_ref[0] = state[...]
        keep, tri, eye = _chunk_masks()
        eg, _, _, at, rt, bt, kt = _scaled(r_ref[...], lw_ref[...], k2_ref[...], aa_ref[...], bb_ref[...], tri)
        for h in range(NH):
            c = _head_cols(h)
            y, sn, _ = _head_fwd(at[:, c], rt[:, c], bt[:, c], kt[:, c], v_ref[:, c], state[c, :], eg[L - 1:L, c],
                                 keep, eye)
            y_ref[:, c] = y
            state[c, :] = sn

    blk = pl.BlockSpec((L, W), lambda b, c: (b * NC + c, 0))
    return pl.pallas_call(
        body, grid=(NSEQ, NC), name="rwkv_fwd",
        in_specs=[blk] * 6,
        out_specs=[blk, pl.BlockSpec((1, W, HD), lambda b, c: (b * NC + c, 0, 0))],
        out_shape=[jax.ShapeDtypeStruct((T, W), F32), jax.ShapeDtypeStruct((NSEQ * NC, W, HD), F32)],
        scratch_shapes=[pltpu.VMEM((W, HD), F32)],
        compiler_params=_cparams(("parallel", "arbitrary")),
    )(r, lw, k2, v, aa, bb)


def rwkv_bwd(r, lw, k2, v, aa, bb, hs, dy):
    def body(r_ref, lw_ref, k2_ref, v_ref, aa_ref, bb_ref, hs_ref, dy_ref,
             dr_ref, dlw_ref, dk2_ref, dv_ref, daa_ref, dbb_ref, dstate):
        @pl.when(pl.program_id(1) == 0)
        def _():
            dstate[...] = jnp.zeros_like(dstate)

        keep, tri, eye = _chunk_masks()
        eg, eng, egp, at, rt, bt, kt = _scaled(r_ref[...], lw_ref[...], k2_ref[...], aa_ref[...], bb_ref[...], tri)
        for h in range(NH):
            c = _head_cols(h)
            dat, drt, dbt, dkt, dv, dgl, ds0 = _head_bwd(
                at[:, c], rt[:, c], bt[:, c], kt[:, c], v_ref[:, c], hs_ref[0, c, :], eg[L - 1:L, c],
                dy_ref[:, c], dstate[c, :], keep, eye)
            daa_ref[:, c] = dat
            dr_ref[:, c] = drt
            dbb_ref[:, c] = dbt
            dk2_ref[:, c] = dkt
            dv_ref[:, c] = dv
            dlw_ref[L - 1:L, c] = dgl
            dstate[c, :] = ds0
        dat, drt, dbt, dkt = daa_ref[...], dr_ref[...], dbb_ref[...], dk2_ref[...]
        dg = drt * rt - dbt * bt - dkt * kt
        dg = dg + jnp.where(_iota2(dg.shape, 0) == L - 1, dlw_ref[L - 1:L, :], 0.0)
        dgp = dat * at
        dlw_ref[...] = _hdot_tn(tri, dg + dgp) - dgp
        dr_ref[...] = drt * eg
        daa_ref[...] = dat * egp
        dbb_ref[...] = dbt * eng
        dk2_ref[...] = dkt * eng

    blk = pl.BlockSpec((L, W), lambda b, c: (b * NC + NC - 1 - c, 0))
    return pl.pallas_call(
        body, grid=(NSEQ, NC), name="rwkv_bwd",
        in_specs=[blk] * 6 + [pl.BlockSpec((1, W, HD), lambda b, c: (b * NC + NC - 1 - c, 0, 0)), blk],
        out_specs=[blk] * 6,
        out_shape=[jax.ShapeDtypeStruct((T, W), F32)] * 6,
        scratch_shapes=[pltpu.VMEM((W, HD), F32)],
        compiler_params=_cparams(("parallel", "arbitrary")),
    )(r, lw, k2, v, aa, bb, hs, dy)


def _post_math(y, r, k2, v, ga, o, gb, lng, lnb, rk, bd):
    mu = _hdot(y, bd) * (1.0 / HD)
    yc = y - mu
    var = _hdot(yc * yc, bd) * (1.0 / HD)
    yn = yc * lax.rsqrt(var + GN_EPS) * lng + lnb
    bonus = _hdot(r * k2 * rk, bd) * v
    return (yn + bonus) * _silu(ga), o * _silu(gb)


def even_post(y, r, k2, v, ga, o, gb, lng, lnb, rk):
    tm = 256

    def body(y_ref, r_ref, k2_ref, v_ref, ga_ref, o_ref, gb_ref, lng_ref, lnb_ref, rk_ref, z_ref):
        ya, yb = _post_math(y_ref[...], r_ref[...], k2_ref[...], v_ref[...], ga_ref[...], o_ref[...], gb_ref[...],
                            lng_ref[...], lnb_ref[...], rk_ref[...], _head_blockdiag())
        z_ref[:, 0:W] = ya.astype(BF16)
        z_ref[:, W:2 * W] = yb.astype(BF16)

    vec = _const_spec((1, W))
    return pl.pallas_call(
        body, grid=(T // tm,), name="even_post",
        in_specs=[_row_spec(tm, W)] * 7 + [vec] * 3,
        out_specs=_row_spec(tm, D), out_shape=jax.ShapeDtypeStruct((T, D), BF16),
        compiler_params=_cparams(("parallel",)),
    )(y, r, k2, v, ga, o, gb, lng, lnb, rk)


def even_post_bwd(y, r, k2, v, ga, o, gb, lng, lnb, rk, dz):
    tm = 256

    def body(y_ref, r_ref, k2_ref, v_ref, ga_ref, o_ref, gb_ref, lng_ref, lnb_ref, rk_ref, dz_ref,
             dy_ref, dr_ref, dk2_ref, dv_ref, dga_ref, do_ref, dgb_ref, dlng_ref, dlnb_ref, drk_ref):
        bd = _head_blockdiag()
        _, vjp = jax.vjp(lambda *a: _post_math(*a, bd), y_ref[...], r_ref[...], k2_ref[...], v_ref[...], ga_ref[...],
                         o_ref[...], gb_ref[...], lng_ref[...], lnb_ref[...], rk_ref[...])
        dzv = dz_ref[...]
        dy, dr, dk2, dv, dga, do, dgb, dlng, dlnb, drk = vjp((dzv[:, 0:W], dzv[:, W:2 * W]))
        for ref, val in ((dy_ref, dy), (dr_ref, dr), (dk2_ref, dk2), (dv_ref, dv), (dga_ref, dga), (do_ref, do),
                         (dgb_ref, dgb)):
            ref[...] = val

        @pl.when(pl.program_id(0) == 0)
        def _():
            for ref in (dlng_ref, dlnb_ref, drk_ref):
                ref[...] = jnp.zeros_like(ref)

        dlng_ref[...] += dlng
        dlnb_ref[...] += dlnb
        drk_ref[...] += drk

    vec = _const_spec((1, W))
    return pl.pallas_call(
        body, grid=(T // tm,), name="even_post_bwd",
        in_specs=[_row_spec(tm, W)] * 7 + [vec] * 3 + [_row_spec(tm, D)],
        out_specs=[_row_spec(tm, W)] * 7 + [vec] * 3,
        out_shape=[jax.ShapeDtypeStruct((T, W), F32)] * 7 + [jax.ShapeDtypeStruct((1, W), F32)] * 3,
        compiler_params=_cparams(("arbitrary",)),
    )(y, r, k2, v, ga, o, gb, lng, lnb, rk, dz)


PADSEQ = SEQ + LEFT * L
ATT_SCALE = 1.0 / math.sqrt(HD)


def _att_probs(qh, kwh, bias_h, c):
    s = _bdot_nt(qh, kwh) * ATT_SCALE + bias_h
    valid = _iota2((1, BAND), 1) >= (LEFT - c) * L
    s = jnp.where(valid, s, NEG)
    e = jnp.exp(s - jnp.max(s, axis=-1, keepdims=True))
    return e / jnp.sum(e, axis=-1, keepdims=True)


def attention_fwd(q, kpad, vpad, bias):
    def body(q_ref, k_ref, v_ref, b_ref, o_ref):
        c = pl.program_id(1)
        start = pl.multiple_of(c * L, L)
        kw = k_ref[pl.ds(start, BAND), :]
        vw = v_ref[pl.ds(start, BAND), :]
        for h in range(NH):
            cs = _head_cols(h)
            p = _att_probs(q_ref[:, cs], kw[:, cs], b_ref[h], c)
            o_ref[:, cs] = _bdot(p, vw[:, cs])

    qblk = pl.BlockSpec((L, W), lambda b, c: (b * NC + c, 0))
    kblk = pl.BlockSpec((PADSEQ, W), lambda b, c: (b, 0))
    return pl.pallas_call(
        body, grid=(NSEQ, NC), name="attention_fwd",
        in_specs=[qblk, kblk, kblk, _const_spec((NH, L, BAND))],
        out_specs=qblk, out_shape=jax.ShapeDtypeStruct((T, W), F32),
        compiler_params=_cparams(("parallel", "arbitrary")),
    )(q, kpad, vpad, bias)


def attention_bwd(q, kpad, vpad, bias, do):
    def body(q_ref, k_ref, v_ref, b_ref, do_ref, dq_ref, dk_ref, dv_ref, db_ref):
        b = pl.program_id(0)
        c = pl.program_id(1)

        @pl.when(c == 0)
        def _():
            dk_ref[...] = jnp.zeros_like(dk_ref)
            dv_ref[...] = jnp.zeros_like(dv_ref)

        @pl.when((c == 0) & (b == 0))
        def _():
            db_ref[...] = jnp.zeros_like(db_ref)

        start = pl.multiple_of(c * L, L)
        kw = k_ref[pl.ds(start, BAND), :]
        vw = v_ref[pl.ds(start, BAND), :]
        for h in range(NH):
            cs = _head_cols(h)
            qh = q_ref[:, cs]
            doh = do_ref[:, cs]
            p = _att_probs(qh, kw[:, cs], b_ref[h], c)
            dp = _bdot_nt(doh, vw[:, cs])
            ds = p * (dp - jnp.sum(dp * p, axis=-1, keepdims=True))
            db_ref[h] += ds
            dss = ds * ATT_SCALE
            dq_ref[:, cs] = _bdot(dss, kw[:, cs])
            dk_ref[pl.ds(start, BAND), cs] += _bdot_tn(dss, qh)
            dv_ref[pl.ds(start, BAND), cs] += _bdot_tn(p, doh)

    qblk = pl.BlockSpec((L, W), lambda b, c: (b * NC + c, 0))
    kblk = pl.BlockSpec((PADSEQ, W), lambda b, c: (b, 0))
    bblk = _const_spec((NH, L, BAND))
    return pl.pallas_call(
        body, grid=(NSEQ, NC), name="attention_bwd",
        in_specs=[qblk, kblk, kblk, bblk, qblk],
        out_specs=[qblk, kblk, kblk, bblk],
        out_shape=[jax.ShapeDtypeStruct((T, W), F32), jax.ShapeDtypeStruct((NSEQ * PADSEQ, W), F32),
                   jax.ShapeDtypeStruct((NSEQ * PADSEQ, W), F32), jax.ShapeDtypeStruct((NH, L, BAND), F32)],
        compiler_params=_cparams(("arbitrary", "arbitrary"), VMEM_BIG),
    )(q, kpad, vpad, bias, do)


def attention_bias(table):
    n = np.arange(BAND + L - 1)
    idx = np.clip(LEFT * L + (L - 1) - n, -CLIP, CLIP) + CLIP
    lo = int(idx.min())
    rev = jnp.flip(table[:, lo:], axis=1)
    n_top = int((idx == 2 * CLIP).sum())
    ext = jnp.concatenate([jnp.broadcast_to(table[:, 2 * CLIP:], (NH, n_top - 1)), rev], axis=1)
    return jnp.stack([ext[:, L - 1 - i:L - 1 - i + BAND] for i in range(L)], axis=1)


def _group_cols(g):
    return slice(g * SGC, (g + 1) * SGC)


def _sg_norm(v, lng, lnb):
    gv = _gelu(v)
    gc = gv - jnp.mean(gv, axis=-1, keepdims=True)
    rstd = lax.rsqrt(jnp.mean(gc * gc, axis=-1, keepdims=True) + LN_EPS)
    xhat = gc * rstd
    return xhat, rstd, xhat * lng + lnb


def gmlp_fwd(u, v, gate, lng, lnb, wm_bf, sgb_t):
    def body(u_ref, v_ref, gt_ref, lng_ref, lnb_ref, wm_ref, sb_ref, z_ref):
        _, _, vln = _sg_norm(v_ref[...], lng_ref[...], lnb_ref[...])
        vlb = vln.astype(BF16)
        for g in range(NG):
            cs = _group_cols(g)
            sv = jnp.dot(wm_ref[g], vlb[:, cs], preferred_element_type=F32) + sb_ref[:, g:g + 1]
            z_ref[:, cs] = (_gelu(u_ref[:, cs]) * sv * _silu(gt_ref[:, cs])).astype(BF16)

    return pl.pallas_call(
        body, grid=(T // SGC,), name="gmlp_fwd",
        in_specs=[_row_spec(SGC, D)] * 3 + [_const_spec((1, D))] * 2 + [_const_spec((NG, SGC, SGC)),
                                                                      _const_spec((SGC, NG))],
        out_specs=_row_spec(SGC, D), out_shape=jax.ShapeDtypeStruct((T, D), BF16),
        compiler_params=_cparams(("parallel",)),
    )(u, v, gate, lng, lnb, wm_bf, sgb_t)


def gmlp_bwd(u, v, gate, lng, lnb, wm_bf, sgb_t, dz):
    def body(u_ref, v_ref, gt_ref, lng_ref, lnb_ref, wm_ref, sb_ref, dz_ref,
             du_ref, dv_ref, dgt_ref, dlng_ref, dlnb_ref, dwm_ref, dsb_ref):
        @pl.when(pl.program_id(0) == 0)
        def _():
            for ref in (dlng_ref, dlnb_ref, dwm_ref, dsb_ref):
                ref[...] = jnp.zeros_like(ref)

        vv = v_ref[...]
        xhat, rstd, vln = _sg_norm(vv, lng_ref[...], lnb_ref[...])
        vlb = vln.astype(BF16)
        dvln = []
        dsv_all = []
        for g in range(NG):
            cs = _group_cols(g)
            uu = u_ref[:, cs]
            gg = gt_ref[:, cs]
            dzz = dz_ref[:, cs]
            sv = jnp.dot(wm_ref[g], vlb[:, cs], preferred_element_type=F32) + sb_ref[:, g:g + 1]
            gu = _gelu(uu)
            sg = _silu(gg)
            dsv = dzz * gu * sg
            dgt_ref[:, cs] = dzz * gu * sv * _dsilu(gg)
            du_ref[:, cs] = dzz * sv * sg * _dgelu(uu)
            dsb16 = dsv.astype(BF16)
            dvln.append(lax.dot_general(wm_ref[g], dsb16, (((0,), (0,)), ((), ())), preferred_element_type=F32))
            dwm_ref[g] += lax.dot_general(dsb16, vlb[:, cs], (((1,), (1,)), ((), ())), preferred_element_type=F32)
            dsv_all.append(dsv)
        dvl = jnp.concatenate(dvln, axis=-1)
        dsv_cat = jnp.concatenate(dsv_all, axis=-1)
        sel = (_iota2((D, NG), 0) // SGC == _iota2((D, NG), 1)).astype(F32)
        dsb_ref[...] += _hdot(dsv_cat, sel)
        dlng_ref[...] += jnp.sum(dvl * xhat, axis=0, keepdims=True)
        dlnb_ref[...] += jnp.sum(dvl, axis=0, keepdims=True)
        dxh = dvl * lng_ref[...]
        dgv = rstd * (dxh - jnp.mean(dxh, axis=-1, keepdims=True)
                      - xhat * jnp.mean(dxh * xhat, axis=-1, keepdims=True))
        dv_ref[...] = dgv * _dgelu(vv)

    return pl.pallas_call(
        body, grid=(T // SGC,), name="gmlp_bwd",
        in_specs=[_row_spec(SGC, D)] * 3 + [_const_spec((1, D))] * 2
        + [_const_spec((NG, SGC, SGC)), _const_spec((SGC, NG)), _row_spec(SGC, D)],
        out_specs=[_row_spec(SGC, D)] * 3 + [_const_spec((1, D))] * 2 + [_const_spec((NG, SGC, SGC)),
                                                                       _const_spec((SGC, NG))],
        out_shape=[jax.ShapeDtypeStruct((T, D), F32)] * 3 + [jax.ShapeDtypeStruct((1, D), F32)] * 2
        + [jax.ShapeDtypeStruct((NG, SGC, SGC), F32), jax.ShapeDtypeStruct((SGC, NG), F32)],
        compiler_params=_cparams(("arbitrary",)),
    )(u, v, gate, lng, lnb, wm_bf, sgb_t, dz)


NCHIP = 4
NDEV = 8
ANY = pl.BlockSpec(memory_space=pl.ANY)


def exchange_xy(arrs, name):
    n = len(arrs)

    def body(*refs):
        ins, outs = refs[:n], refs[n:2 * n]
        send, recv, loc = refs[2 * n:]
        x, y, c = lax.axis_index("x"), lax.axis_index("y"), lax.axis_index("c")
        me = 2 * x + y
        peers = [(1 - x, y), (x, 1 - y), (1 - x, 1 - y)]
        local = [pltpu.make_async_copy(ins[t].at[me], outs[t].at[me], loc.at[t]) for t in range(n)]
        for cp in local:
            cp.start()
        remote = []
        for t in range(n):
            for j, (px, py) in enumerate(peers):
                remote.append((pltpu.make_async_remote_copy(
                    src_ref=ins[t].at[2 * px + py], dst_ref=outs[t].at[me], send_sem=send.at[t, j],
                    recv_sem=recv.at[t, j], device_id=(px, py, c), device_id_type=MESH), t, j, 2 * px + py))
        for cp, _, _, _ in remote:
            cp.start()
        for cp, t, j, chip in remote:
            cp.wait_send()
            pltpu.make_async_remote_copy(
                src_ref=ins[t].at[chip], dst_ref=outs[t].at[chip], send_sem=send.at[t, j], recv_sem=recv.at[t, j],
                device_id=(x, y, c), device_id_type=MESH).wait_recv()
        for cp in local:
            cp.wait()

    return pl.pallas_call(
        body, name=name, in_specs=[ANY] * n, out_specs=[ANY] * n,
        out_shape=[jax.ShapeDtypeStruct(a.shape, a.dtype) for a in arrs],
        scratch_shapes=[pltpu.SemaphoreType.DMA((n, 3)), pltpu.SemaphoreType.DMA((n, 3)), pltpu.SemaphoreType.DMA((n,))],
    )(*arrs)


def exchange_c(arrs, name):
    n = len(arrs)

    def body(*refs):
        ins, outs = refs[:n], refs[n:2 * n]
        send, recv = refs[2 * n:]
        sibling = (lax.axis_index("x"), lax.axis_index("y"), 1 - lax.axis_index("c"))
        copies = [pltpu.make_async_remote_copy(src_ref=ins[t], dst_ref=outs[t], send_sem=send.at[t], recv_sem=recv.at[t],
                                               device_id=sibling, device_id_type=MESH) for t in range(n)]
        for cp in copies:
            cp.start()
        for cp in copies:
            cp.wait()

    return pl.pallas_call(
        body, name=name, in_specs=[ANY] * n, out_specs=[ANY] * n,
        out_shape=[jax.ShapeDtypeStruct(a.shape, a.dtype) for a in arrs],
        scratch_shapes=[pltpu.SemaphoreType.DMA((n,)), pltpu.SemaphoreType.DMA((n,))],
    )(*arrs)


def gather_weights(arrs, split):
    n = len(arrs)

    def body(*refs):
        ins, outs = refs[:n], refs[n:2 * n]
        send1, recv1, send2, recv2, loc = refs[2 * n:]
        x, y, c = lax.axis_index("x"), lax.axis_index("y"), lax.axis_index("c")
        me = 2 * x + y
        sibling = (x, y, 1 - c)
        peers = [(1 - x, y), (x, 1 - y), (1 - x, 1 - y)]

        def rows_of(t, core):
            half = arrs[t].shape[0] // 2
            return pl.ds(core * half, half)

        def part(ref, t, core):
            return ref.at[rows_of(t, core)] if split[t] else ref

        local = [pltpu.make_async_copy(ins[t], outs[t].at[me], loc.at[t]) for t in range(n)]
        for cp in local:
            cp.start()
        first = []
        for t in range(n):
            for j, (px, py) in enumerate(peers):
                first.append(pltpu.make_async_remote_copy(
                    src_ref=part(ins[t], t, c), dst_ref=part(outs[t].at[me], t, c), send_sem=send1.at[t, j],
                    recv_sem=recv1.at[t, j], device_id=(px, py, c), device_id_type=MESH))
        for cp in first:
            cp.start()
        passed = []
        for t in range(n):
            for j, (px, py) in enumerate(peers):
                landed = part(outs[t].at[2 * px + py], t, c)
                pltpu.make_async_remote_copy(
                    src_ref=landed, dst_ref=landed, send_sem=send1.at[t, j], recv_sem=recv1.at[t, j],
                    device_id=(x, y, c), device_id_type=MESH).wait_recv()
                if split[t]:
                    cp = pltpu.make_async_remote_copy(
                        src_ref=landed, dst_ref=landed, send_sem=send2.at[t, j], recv_sem=recv2.at[t, j],
                        device_id=sibling, device_id_type=MESH)
                    cp.start()
                    passed.append(cp)
        for t in range(n):
            for j, (px, py) in enumerate(peers):
                if split[t]:
                    other = part(outs[t].at[2 * px + py], t, 1 - c)
                    pltpu.make_async_remote_copy(
                        src_ref=other, dst_ref=other, send_sem=send2.at[t, j], recv_sem=recv2.at[t, j],
                        device_id=(x, y, c), device_id_type=MESH).wait_recv()
        for cp in first + passed:
            cp.wait_send()
        for cp in local:
            cp.wait()

    return pl.pallas_call(
        body, name="gather_weights", in_specs=[ANY] * n, out_specs=[ANY] * n,
        out_shape=[jax.ShapeDtypeStruct((NCHIP,) + a.shape, a.dtype) for a in arrs],
        scratch_shapes=[pltpu.SemaphoreType.DMA((n, 3))] * 4 + [pltpu.SemaphoreType.DMA((n,))],
    )(*arrs)


def allgather_all(buf, name):
    def body(in_ref, out_ref, send, recv, loc):
        x, y, c = lax.axis_index("x"), lax.axis_index("y"), lax.axis_index("c")
        me = 4 * x + 2 * y + c
        mine = pltpu.make_async_copy(in_ref, out_ref.at[me], loc)
        mine.start()
        copies = []
        for j in range(1, NDEV):
            px, py, pc = x ^ (j >> 2), y ^ ((j >> 1) & 1), c ^ (j & 1)
            copies.append((pltpu.make_async_remote_copy(
                src_ref=in_ref, dst_ref=out_ref.at[me], send_sem=send.at[j], recv_sem=recv.at[j],
                device_id=(px, py, pc), device_id_type=MESH), j, 4 * px + 2 * py + pc))
        for cp, _, _ in copies:
            cp.start()
        for cp, j, peer in copies:
            cp.wait_send()
            pltpu.make_async_remote_copy(
                src_ref=in_ref, dst_ref=out_ref.at[peer], send_sem=send.at[j], recv_sem=recv.at[j],
                device_id=(x, y, c), device_id_type=MESH).wait_recv()
        mine.wait()

    return pl.pallas_call(
        body, name=name, in_specs=[ANY], out_specs=ANY,
        out_shape=jax.ShapeDtypeStruct((NDEV,) + buf.shape, buf.dtype),
        scratch_shapes=[pltpu.SemaphoreType.DMA((NDEV,)), pltpu.SemaphoreType.DMA((NDEV,)), pltpu.SemaphoreType.DMA],
    )(buf)


def _adam_math(g, w, m, v):
    m = ADAM_B1 * m + (1.0 - ADAM_B1) * g
    v = ADAM_B2 * v + (1.0 - ADAM_B2) * (g * g)
    m_hat = m / (1.0 - ADAM_B1 ** ADAM_STEP)
    v_hat = v / (1.0 - ADAM_B2 ** ADAM_STEP)
    delta = -ADAM_LR * (m_hat / (jnp.sqrt(v_hat) + ADAM_EPS) + ADAM_WD * w)
    return delta, m, v


def _rows_tile(rows):
    return rows if rows <= 256 else 256


def sum_chips(parts, name):
    _, rows, cols = parts.shape
    tr = _rows_tile(rows)

    def body(p_ref, o_ref):
        p = [p_ref[s].astype(F32) for s in range(NCHIP)]
        o_ref[...] = ((p[0] + p[1]) + p[2]) + p[3]

    return pl.pallas_call(
        body, grid=(rows // tr,), name=name,
        in_specs=[pl.BlockSpec((NCHIP, tr, cols), lambda i: (0, i, 0))],
        out_specs=pl.BlockSpec((tr, cols), lambda i: (i, 0)),
        out_shape=jax.ShapeDtypeStruct((rows, cols), F32),
        compiler_params=_cparams(("parallel",)),
    )(parts)


def adam_shard(p_mine, p_sib, w, m, v, name):
    rows, cols = w.shape
    tr = _rows_tile(rows)

    def body(a_ref, b_ref, w_ref, m_ref, v_ref, g_ref, d_ref, mo_ref, vo_ref):
        g = a_ref[...] + b_ref[...]
        g_ref[...] = g
        d_ref[...], mo_ref[...], vo_ref[...] = _adam_math(g, w_ref[...], m_ref[...], v_ref[...])

    spec = pl.BlockSpec((tr, cols), lambda i: (i, 0))
    return pl.pallas_call(
        body, grid=(rows // tr,), name=name, in_specs=[spec] * 5, out_specs=[spec] * 4,
        out_shape=[jax.ShapeDtypeStruct((rows, cols), F32)] * 4,
        compiler_params=_cparams(("parallel",)),
    )(p_mine, p_sib, w, m, v)


def adam_replicated(parts, w, m, v):
    rows = w.shape[0]

    def body(p_ref, w_ref, m_ref, v_ref, g_ref, d_ref, mo_ref, vo_ref):
        g = p_ref[0]
        for d in range(1, NDEV):
            g = g + p_ref[d]
        g_ref[...] = g
        d_ref[...], mo_ref[...], vo_ref[...] = _adam_math(g, w_ref[...], m_ref[...], v_ref[...])

    return pl.pallas_call(
        body, name="adam_replicated", out_shape=[jax.ShapeDtypeStruct((rows, 128), F32)] * 4,
    )(parts, w, m, v)


def _pack(arrs):
    pieces = []
    for a in arrs:
        flat = a.reshape(-1)
        pad = (-flat.shape[0]) % 128
        pieces.append(jnp.pad(flat, (0, pad)) if pad else flat)
    flat = jnp.concatenate(pieces)
    pad = (-flat.shape[0]) % 1024
    return jnp.pad(flat, (0, pad)).reshape(-1, 128)


def _unpack(buf, shapes):
    flat = buf.reshape(-1)
    out = []
    o = 0
    for s in shapes:
        n = int(np.prod(s))
        out.append(flat[o:o + n].reshape(s))
        o += n + (-n) % 128
    return out


EVEN_SPLITS = (SHIFT, W, W, W, W, W)
ODD_SPLITS = (D, D, D)


def _cols_to_chips(a):
    rows, cols = a.shape
    return a.reshape(rows, NCHIP, cols // NCHIP).transpose(1, 0, 2)


def _chips_to_cols(a):
    _, rows, n = a.shape
    return a.transpose(1, 0, 2).reshape(rows, NCHIP * n)


def kernel(x, norm_g, w_in_e, shift_mu, rw_w0, rw_w2, rw_a0, rw_a2, rw_kk, rw_ka, rw_rk, rw_lnx_g, rw_lnx_b, att_bias, w_out_e, w_in_o, sg_ln_g, sg_ln_b, sg_w, sg_b, w_out_o, final_g, loss_target, m_norm_g, m_w_in_e, m_shift_mu, m_rw_w0, m_rw_w2, m_rw_a0, m_rw_a2, m_rw_kk, m_rw_ka, m_rw_rk, m_rw_lnx_g, m_rw_lnx_b, m_att_bias, m_w_out_e, m_w_in_o, m_sg_ln_g, m_sg_ln_b, m_sg_w, m_sg_b, m_w_out_o, m_final_g, v_norm_g, v_w_in_e, v_shift_mu, v_rw_w0, v_rw_w2, v_rw_a0, v_rw_a2, v_rw_kk, v_rw_ka, v_rw_rk, v_rw_lnx_g, v_rw_lnx_b, v_att_bias, v_w_out_e, v_w_in_o, v_sg_ln_g, v_sg_ln_b, v_sg_w, v_sg_b, v_w_out_o, v_final_g):
    x2 = x.reshape(T, D)
    tgt = loss_target.reshape(T, D)

    gathered = gather_weights(
        [w_in_e[0].astype(BF16), w_out_e[0].astype(BF16), w_in_o[0].astype(BF16), w_out_o[0].astype(BF16),
         jnp.concatenate([rw_w2[0], rw_a2[0]], axis=0), jnp.concatenate([sg_ln_g, sg_ln_b], axis=0)],
        [True, True, True, True, True, False])
    wie = _chips_to_cols(gathered[0])
    woe = gathered[1].reshape(D, D)
    wio = _chips_to_cols(gathered[2])
    woo = gathered[3].reshape(D, D)
    w2 = _chips_to_cols(gathered[4][:, :LORA])
    a2 = _chips_to_cols(gathered[4][:, LORA:])
    sglg = _chips_to_cols(gathered[5][:, 0:1])
    sglb = _chips_to_cols(gathered[5][:, 1:2])
    loss_part, dx, big_g, rep_g = _local_step(
        x2, tgt, wie, woe, wio, woo, w2, a2, sglg, sglb, norm_g, shift_mu, rw_w0, rw_a0, rw_kk, rw_ka, rw_rk,
        rw_lnx_g, rw_lnx_b, att_bias, sg_w, sg_b, final_g)
    loss = lax.psum(loss_part, ("x", "y", "c"))
    d_wie, d_woe, d_wio, d_woo, d_w2, d_a2, d_sglg, d_sglb = big_g

    local = [_cols_to_chips(d_wie), d_woe.reshape(NCHIP, D // NCHIP, D), _cols_to_chips(d_wio),
             d_woo.reshape(NCHIP, D // NCHIP, D), _cols_to_chips(d_w2), _cols_to_chips(d_a2),
             _cols_to_chips(d_sglg), _cols_to_chips(d_sglb)]
    names = ["w_in_e", "w_out_e", "w_in_o", "w_out_o", "rw_w2", "rw_a2", "sg_ln_g", "sg_ln_b"]
    local = [a.astype(BF16) for a in local[:4]] + local[4:]
    received = exchange_xy(local, "scatter_grads")
    partial = [sum_chips(p, "sum_" + nm) for p, nm in zip(received, names)]
    from_sibling = exchange_c(partial, "swap_partials")
    sharded = {}
    wmv = {"w_in_e": (w_in_e[0], m_w_in_e[0], v_w_in_e[0]), "w_out_e": (w_out_e[0], m_w_out_e[0], v_w_out_e[0]),
           "w_in_o": (w_in_o[0], m_w_in_o[0], v_w_in_o[0]), "w_out_o": (w_out_o[0], m_w_out_o[0], v_w_out_o[0]),
           "rw_w2": (rw_w2[0], m_rw_w2[0], v_rw_w2[0]), "rw_a2": (rw_a2[0], m_rw_a2[0], v_rw_a2[0]),
           "sg_ln_g": (sg_ln_g, m_sg_ln_g, v_sg_ln_g), "sg_ln_b": (sg_ln_b, m_sg_ln_b, v_sg_ln_b)}
    for nm, mine, sib in zip(names, partial, from_sibling):
        w_, m_, v_ = wmv[nm]
        res = adam_shard(mine, sib, w_, m_, v_, "adam_" + nm)
        lead = nm not in ("sg_ln_g", "sg_ln_b")
        sharded[nm] = [a[None] if lead else a for a in res]

    rep_names = ["norm_g", "shift_mu", "rw_w0", "rw_a0", "rw_kk", "rw_ka", "rw_rk", "rw_lnx_g", "rw_lnx_b",
                 "att_bias", "sg_w", "sg_b", "final_g"]
    rep_w = [norm_g, shift_mu, rw_w0, rw_a0, rw_kk, rw_ka, rw_rk, rw_lnx_g, rw_lnx_b, att_bias, sg_w, sg_b, final_g]
    rep_m = [m_norm_g, m_shift_mu, m_rw_w0, m_rw_a0, m_rw_kk, m_rw_ka, m_rw_rk, m_rw_lnx_g, m_rw_lnx_b, m_att_bias,
             m_sg_w, m_sg_b, m_final_g]
    rep_v = [v_norm_g, v_shift_mu, v_rw_w0, v_rw_a0, v_rw_kk, v_rw_ka, v_rw_rk, v_rw_lnx_g, v_rw_lnx_b, v_att_bias,
             v_sg_w, v_sg_b, v_final_g]
    shapes = [w_.shape for w_ in rep_w]
    gathered_g = allgather_all(_pack(rep_g), "gather_small_grads")
    rep_out = adam_replicated(gathered_g, _pack(rep_w), _pack(rep_m), _pack(rep_v))
    rep = {nm: [] for nm in rep_names}
    for buf in rep_out:
        for nm, a in zip(rep_names, _unpack(buf, shapes)):
            rep[nm].append(a)

    order = ["norm_g", "w_in_e", "shift_mu", "rw_w0", "rw_w2", "rw_a0", "rw_a2", "rw_kk", "rw_ka", "rw_rk",
             "rw_lnx_g", "rw_lnx_b", "att_bias", "w_out_e", "w_in_o", "sg_ln_g", "sg_ln_b", "sg_w", "sg_b",
             "w_out_o", "final_g"]
    results = {**sharded, **rep}
    outs = [loss, dx.reshape(NSEQ, SEQ, D)]
    for kind in range(4):
        outs += [results[nm][kind] for nm in order]
    return tuple(outs)


def _local_step(x2, tgt, wie, woe, wio, woo, w2, a2, sglg, sglb, norm_g, shift_mu, rw_w0, rw_a0, rw_kk, rw_ka, rw_rk,
                rw_lnx_g, rw_lnx_b, att_bias, sg_w, sg_b, final_g):
    zl = jnp.zeros((LORA, W), F32)
    w2x = jnp.concatenate([w2, zl], axis=0)
    a2x = jnp.concatenate([zl, a2], axis=0)
    rk = rw_rk.reshape(1, W)
    pos = np.arange(SGC)
    sg_mask = jnp.asarray(((pos[None, :] // L) <= (pos[:, None] // L)).astype(np.float32))
    wm = (sg_w[0] * sg_mask[None]).astype(BF16)
    sgb_t = sg_b[0].T

    xn0, ps, ga, q, kb, vb, gb = ln_in_proj(x2, norm_g[0:1], wie, EVEN_SPLITS, "in_proj_even")
    r, lw, k2, v, aa, bb = even_prep(ps, shift_mu, rw_w0, w2x, rw_a0, a2x, rw_kk, rw_ka)
    y, hs = rwkv_fwd(r, lw, k2, v, aa, bb)
    bias, bias_vjp = jax.vjp(attention_bias, att_bias[0])

    def padded(a):
        return jnp.pad(a.astype(BF16).reshape(NSEQ, SEQ, W), ((0, 0), (LEFT * L, 0), (0, 0))).reshape(NSEQ * PADSEQ, W)

    kpad, vpad = padded(kb), padded(vb)
    o = attention_fwd(q, kpad, vpad, bias)
    z = even_post(y, r, k2, v, ga, o, gb, rw_lnx_g, rw_lnx_b, rk)
    h1 = out_proj(x2, z, woe, "out_proj_even")
    xn1, u, vv, gt = ln_in_proj(h1, norm_g[1:2], wio, ODD_SPLITS, "in_proj_odd")
    z2 = gmlp_fwd(u, vv, gt, sglg, sglb, wm, sgb_t)
    h2 = out_proj(h1, z2, woo, "out_proj_odd")
    dh2, loss_part, d_final_g = final_loss(h2, final_g[None], tgt)

    dz2, d_woo = out_proj_bwd(dh2, z2, woo, "out_proj_odd_bwd")
    du, dvv, dgt, d_sglg, d_sglb, d_wm, d_sgb_t = gmlp_bwd(u, vv, gt, sglg, sglb, wm, sgb_t, dz2)
    dh1, d_g1, dp2 = in_proj_bwd_x(h1, norm_g[1:2], wio, [du, dvv, dgt], dh2, "in_proj_odd_bwd")
    d_wio = matmul_tn_acc(xn1, dp2, 512, "in_proj_odd_dw")
    dz, d_woe = out_proj_bwd(dh1, z, woe, "out_proj_even_bwd")
    dy, dr2, dk22, dv2, dga, do, dgb, d_lng, d_lnb, d_rk = even_post_bwd(
        y, r, k2, v, ga, o, gb, rw_lnx_g, rw_lnx_b, rk, dz)
    dq, dkpad, dvpad, dbias = attention_bwd(q, kpad, vpad, bias, do)

    def unpadded(a):
        return a.reshape(NSEQ, PADSEQ, W)[:, LEFT * L:].reshape(T, W)

    (d_att_bias,) = bias_vjp(dbias)
    dr, dlw, dk2, dv, daa, dbb = rwkv_bwd(r, lw, k2, v, aa, bb, hs, dy)
    dps, d_mu, d_w0, d_w2x, d_a0, d_a2x, d_kk, d_ka = even_prep_bwd(
        ps, shift_mu, rw_w0, w2x, rw_a0, a2x, rw_kk, rw_ka, dr, dlw, dk2, dv, daa, dbb, dr2, dk22, dv2)
    dx, d_g0, dp = in_proj_bwd_x(x2, norm_g[0:1], wie, [dps, dga, dq, unpadded(dkpad), unpadded(dvpad), dgb], dh1,
                                 "in_proj_even_bwd")
    d_wie = matmul_tn_acc(xn0, dp, 384, "in_proj_even_dw")

    big_g = (d_wie, d_woe, d_wio, d_woo, d_w2x[:LORA], d_a2x[LORA:], d_sglg, d_sglb)
    rep_g = [jnp.concatenate([d_g0, d_g1], axis=0), d_mu, d_w0, d_a0, d_kk, d_ka, d_rk, d_lng, d_lnb, d_att_bias,
             d_wm * sg_mask[None], d_sgb_t.T, d_final_g]
    return loss_part[0, 0], dx, big_g, rep_g
```

```python
import functools
import math

import jax
import jax.numpy as jnp
import numpy as np
from jax import lax
from jax.experimental import pallas as pl
from jax.experimental.pallas import tpu as pltpu

F32 = jnp.float32
BF16 = jnp.bfloat16
HI = lax.Precision.HIGHEST

D = 1024
SEQ = 2048
NSEQ = 2
T = NSEQ * SEQ
HD = 64
NH = 8
W = 512
SHIFT = 1664
LORA = 64
EVEN_IN = 4224
ODD_IN = 3072
L = 64
NC = SEQ // L
LEFT = 8
BAND = (LEFT + 1) * L
CLIP = 128
SGC = 128
NG = 8
RMS_EPS = 1e-6
LN_EPS = 1e-5
GN_EPS = 64e-5
NEG = -1e30
VMEM_BIG = 56 * 1024 * 1024

ADAM_LR = 0.001
ADAM_B1 = 0.9
ADAM_B2 = 0.999
ADAM_EPS = 1e-08
ADAM_WD = 0.01
ADAM_STEP = 10

MESH = pl.DeviceIdType.MESH


def _bdot(a, b):
    return jnp.dot(a.astype(BF16), b.astype(BF16), preferred_element_type=F32)


def _bdot_nt(a, b):
    return lax.dot_general(a.astype(BF16), b.astype(BF16), (((1,), (1,)), ((), ())), preferred_element_type=F32)


def _bdot_tn(a, b):
    return lax.dot_general(a.astype(BF16), b.astype(BF16), (((0,), (0,)), ((), ())), preferred_element_type=F32)


def _hdot(a, b):
    return jnp.dot(a, b, precision=HI, preferred_element_type=F32)


def _hdot_nt(a, b):
    return lax.dot_general(a, b, (((1,), (1,)), ((), ())), precision=HI, preferred_element_type=F32)


def _hdot_tn(a, b):
    return lax.dot_general(a, b, (((0,), (0,)), ((), ())), precision=HI, preferred_element_type=F32)


def _iota2(shape, dim):
    return lax.broadcasted_iota(jnp.int32, shape, dim)


def _head_blockdiag():
    r = _iota2((W, W), 0) // HD
    c = _iota2((W, W), 1) // HD
    return (r == c).astype(BF16)


def _headsum_impl(x, bd):
    hi = x.astype(BF16)
    mid = (x - hi.astype(F32)).astype(BF16)
    return jnp.dot(hi, bd, preferred_element_type=F32) + jnp.dot(mid, bd, preferred_element_type=F32)


@jax.custom_vjp
def _headsum(x, bd):
    return _headsum_impl(x, bd)


def _headsum_fwd(x, bd):
    return _headsum_impl(x, bd), bd


def _headsum_bwd(bd, ct):
    return _headsum_impl(ct, bd), None


_headsum.defvjp(_headsum_fwd, _headsum_bwd)


def _silu(x):
    return x * jax.nn.sigmoid(x)


def _dsilu(x):
    s = jax.nn.sigmoid(x)
    return s * (1.0 + x * (1.0 - s))


_GELU_C = math.sqrt(2.0 / math.pi)


def _gelu(x):
    return 0.5 * x * (1.0 + jnp.tanh(_GELU_C * (x + 0.044715 * (x * x * x))))


def _dgelu(x):
    t = jnp.tanh(_GELU_C * (x + 0.044715 * (x * x * x)))
    return 0.5 * (1.0 + t) + 0.5 * x * (1.0 - t * t) * _GELU_C * (1.0 + 3.0 * 0.044715 * x * x)


def _softplus(x):
    return jnp.maximum(x, 0.0) + jnp.log(1.0 + jnp.exp(-jnp.abs(x)))


def _cparams(sem, vmem=None):
    return pltpu.CompilerParams(dimension_semantics=sem, vmem_limit_bytes=vmem)


def _row_spec(tm, width):
    return pl.BlockSpec((tm, width), lambda i: (i, 0))


def _const_spec(shape):
    nd = len(shape)
    return pl.BlockSpec(shape, lambda *_: (0,) * nd)


def ln_in_proj(x, g, w_bf, splits, name):
    n = w_bf.shape[1]
    tm = 256
    spans = []
    o = 0
    for s in splits:
        spans.append((o, o + s))
        o += s
    assert o == n

    def body(x_ref, g_ref, w_ref, xn_ref, *outs):
        xv = x_ref[...]
        rstd = lax.rsqrt(jnp.mean(xv * xv, axis=-1, keepdims=True) + RMS_EPS)
        xn = (xv * rstd * g_ref[...]).astype(BF16)
        xn_ref[...] = xn
        p = jnp.dot(xn, w_ref[...], preferred_element_type=F32)
        for o_ref, (a, b) in zip(outs, spans):
            o_ref[...] = p[:, a:b]

    return pl.pallas_call(
        body, grid=(T // tm,), name=name,
        in_specs=[_row_spec(tm, D), _const_spec((1, D)), _const_spec((D, n))],
        out_specs=[_row_spec(tm, D)] + [_row_spec(tm, s) for s in splits],
        out_shape=[jax.ShapeDtypeStruct((T, D), BF16)] + [jax.ShapeDtypeStruct((T, s), F32) for s in splits],
        compiler_params=_cparams(("parallel",), VMEM_BIG),
    )(x, g, w_bf)


def in_proj_bwd_x(x, g, w_bf, dps, dres, name):
    n = w_bf.shape[1]
    tm = 256
    widths = [d.shape[1] for d in dps]

    def body(x_ref, g_ref, w_ref, dres_ref, *rest):
        dp_refs = rest[:len(widths)]
        dx_ref, dg_ref, dpc_ref = rest[len(widths):]
        dp = jnp.concatenate([r[...].astype(BF16) for r in dp_refs], axis=-1)
        dpc_ref[...] = dp
        dxn = lax.dot_general(dp, w_ref[...], (((1,), (1,)), ((), ())), preferred_element_type=F32)
        xv = x_ref[...]
        rstd = lax.rsqrt(jnp.mean(xv * xv, axis=-1, keepdims=True) + RMS_EPS)
        xhat = xv * rstd
        dgp = jnp.sum(dxn * xhat, axis=0, keepdims=True)

        @pl.when(pl.program_id(0) == 0)
        def _():
            dg_ref[...] = jnp.zeros_like(dg_ref)

        dg_ref[...] += dgp
        dxh = dxn * g_ref[...]
        dx_ref[...] = dres_ref[...] + rstd * (dxh - xhat * jnp.mean(dxh * xhat, axis=-1, keepdims=True))

    return pl.pallas_call(
        body, grid=(T // tm,), name=name,
        in_specs=[_row_spec(tm, D), _const_spec((1, D)), _const_spec((D, n)), _row_spec(tm, D)]
        + [_row_spec(tm, s) for s in widths],
        out_specs=[_row_spec(tm, D), _const_spec((1, D)), _row_spec(tm, n)],
        out_shape=[jax.ShapeDtypeStruct((T, D), F32), jax.ShapeDtypeStruct((1, D), F32),
                   jax.ShapeDtypeStruct((T, n), BF16)],
        compiler_params=_cparams(("arbitrary",), VMEM_BIG),
    )(x, g, w_bf, dres, *dps)


def matmul_tn_acc(a_bf, b_bf, tn, name):
    k = a_bf.shape[1]
    n = b_bf.shape[1]
    tm = 512
    nt = T // tm

    def body(a_ref, b_ref, o_ref):
        @pl.when(pl.program_id(1) == 0)
        def _():
            o_ref[...] = jnp.zeros_like(o_ref)

        o_ref[...] += lax.dot_general(a_ref[...], b_ref[...], (((0,), (0,)), ((), ())), preferred_element_type=F32)

    return pl.pallas_call(
        body, grid=(n // tn, nt), name=name,
        in_specs=[pl.BlockSpec((tm, k), lambda j, i: (i, 0)), pl.BlockSpec((tm, tn), lambda j, i: (i, j))],
        out_specs=pl.BlockSpec((k, tn), lambda j, i: (0, j)),
        out_shape=jax.ShapeDtypeStruct((k, n), F32),
        compiler_params=_cparams(("parallel", "arbitrary"), VMEM_BIG),
    )(a_bf, b_bf)


def out_proj(h, z_bf, w_bf, name):
    tm = 256

    def body(h_ref, z_ref, w_ref, o_ref):
        o_ref[...] = h_ref[...] + jnp.dot(z_ref[...], w_ref[...], preferred_element_type=F32)

    return pl.pallas_call(
        body, grid=(T // tm,), name=name,
        in_specs=[_row_spec(tm, D), _row_spec(tm, D), _const_spec((D, D))],
        out_specs=_row_spec(tm, D), out_shape=jax.ShapeDtypeStruct((T, D), F32),
        compiler_params=_cparams(("parallel",)),
    )(h, z_bf, w_bf)


def out_proj_bwd(dh, z_bf, w_bf, name):
    tm = 256

    def body(dh_ref, z_ref, w_ref, dz_ref, dw_ref):
        dhb = dh_ref[...].astype(BF16)
        dz_ref[...] = lax.dot_general(dhb, w_ref[...], (((1,), (1,)), ((), ())), preferred_element_type=F32)

        @pl.when(pl.program_id(0) == 0)
        def _():
            dw_ref[...] = jnp.zeros_like(dw_ref)

        dw_ref[...] += lax.dot_general(z_ref[...], dhb, (((0,), (0,)), ((), ())), preferred_element_type=F32)

    return pl.pallas_call(
        body, grid=(T // tm,), name=name,
        in_specs=[_row_spec(tm, D), _row_spec(tm, D), _const_spec((D, D))],
        out_specs=[_row_spec(tm, D), _const_spec((D, D))],
        out_shape=[jax.ShapeDtypeStruct((T, D), F32), jax.ShapeDtypeStruct((D, D), F32)],
        compiler_params=_cparams(("arbitrary",)),
    )(dh, z_bf, w_bf)


def final_loss(h, g, target):
    tm = 256

    def body(h_ref, g_ref, t_ref, dh_ref, loss_ref, dg_ref):
        xv = h_ref[...]
        rstd = lax.rsqrt(jnp.mean(xv * xv, axis=-1, keepdims=True) + RMS_EPS)
        xhat = xv * rstd
        err = xhat * g_ref[...] - t_ref[...]
        part = 0.5 * jnp.sum(jnp.mean(err * err, axis=-1, keepdims=True), axis=0, keepdims=True)
        dout = err * (1.0 / D)

        @pl.when(pl.program_id(0) == 0)
        def _():
            loss_ref[...] = jnp.zeros_like(loss_ref)
            dg_ref[...] = jnp.zeros_like(dg_ref)

        loss_ref[...] += jnp.broadcast_to(part, loss_ref.shape)
        dg_ref[...] += jnp.sum(dout * xhat, axis=0, keepdims=True)
        dxh = dout * g_ref[...]
        dh_ref[...] = rstd * (dxh - xhat * jnp.mean(dxh * xhat, axis=-1, keepdims=True))

    return pl.pallas_call(
        body, grid=(T // tm,), name="final_loss",
        in_specs=[_row_spec(tm, D), _const_spec((1, D)), _row_spec(tm, D)],
        out_specs=[_row_spec(tm, D), _const_spec((8, 128)), _const_spec((1, D))],
        out_shape=[jax.ShapeDtypeStruct((T, D), F32), jax.ShapeDtypeStruct((8, 128), F32),
                   jax.ShapeDtypeStruct((1, D), F32)],
        compiler_params=_cparams(("arbitrary",)),
    )(h, g, target)


PREP_TM = 256
PREP_NB = SEQ // PREP_TM


def _prep_elem(k, wl, apre, kkw, kaw, bd):
    wraw = -_softplus(-wl) - 0.5
    lw = -jnp.exp(wraw)
    asig = jax.nn.sigmoid(apre)
    kkr = k * kkw
    nrm = jnp.maximum(jnp.sqrt(_headsum(kkr * kkr, bd)), 1e-12)
    kk = kkr / nrm
    k2 = k * (1.0 + (asig - 1.0) * kaw)
    return lw, k2, -kk, kk * asig


def _shifted(ps_ref, prev_ref, mu, blk):
    p = ps_ref[...]
    first = (blk % PREP_NB) == 0
    prev_row = jnp.where(first, 0.0, prev_ref[7:8, :])
    rolled = pltpu.roll(p, 1, 0)
    p_prev = jnp.where(_iota2(p.shape, 0) == 0, prev_row, rolled)
    return p, p_prev, p + (p_prev - p) * mu


def _prev_spec(width, blk_of):
    return pl.BlockSpec((8, width), lambda i: (jnp.maximum(blk_of(i) * (PREP_TM // 8) - 1, 0), 0))


def even_prep(ps, mu, w0, w2x, a0, a2x, kkw, kaw):
    tm = PREP_TM

    def body(ps_ref, prev_ref, mu_ref, w0_ref, w2_ref, a0_ref, a2_ref, kk_ref, ka_ref,
             r_ref, lw_ref, k2_ref, v_ref, aa_ref, bb_ref):
        _, _, s = _shifted(ps_ref, prev_ref, mu_ref[...], pl.program_id(0))
        wa = s[:, 3 * W:]
        wl = w0_ref[...] + _bdot(jnp.tanh(wa), w2_ref[...])
        apre = a0_ref[...] + _bdot(wa, a2_ref[...])
        lw, k2, aa, bb = _prep_elem(s[:, W:2 * W], wl, apre, kk_ref[...], ka_ref[...], _head_blockdiag())
        r_ref[...] = s[:, 0:W]
        v_ref[...] = s[:, 2 * W:3 * W]
        lw_ref[...] = lw
        k2_ref[...] = k2
        aa_ref[...] = aa
        bb_ref[...] = bb

    vec = _const_spec((1, W))
    return pl.pallas_call(
        body, grid=(T // tm,), name="even_prep",
        in_specs=[_row_spec(tm, SHIFT), _prev_spec(SHIFT, lambda i: i), _const_spec((1, SHIFT)), vec,
                  _const_spec((2 * LORA, W)), vec, _const_spec((2 * LORA, W)), vec, vec],
        out_specs=[_row_spec(tm, W)] * 6,
        out_shape=[jax.ShapeDtypeStruct((T, W), F32)] * 6,
        compiler_params=_cparams(("parallel",)),
    )(ps, ps, mu, w0, w2x, a0, a2x, kkw, kaw)


def even_prep_bwd(ps, mu, w0, w2x, a0, a2x, kkw, kaw, dr, dlw, dk2, dv, daa, dbb, dr2, dk22, dv2):
    tm = PREP_TM
    nb = T // tm
    rev = lambda i: nb - 1 - i

    def body(ps_ref, prev_ref, mu_ref, w0_ref, w2_ref, a0_ref, a2_ref, kk_ref, ka_ref,
             dr_ref, dlw_ref, dk2_ref, dv_ref, daa_ref, dbb_ref, dr2_ref, dk22_ref, dv2_ref,
             dps_ref, dmu_ref, dw0_ref, dw2_ref, da0_ref, da2_ref, dkk_ref, dka_ref, carry):
        i = pl.program_id(0)
        blk = rev(i)
        mu_v = mu_ref[...]
        p, p_prev, s = _shifted(ps_ref, prev_ref, mu_v, blk)
        wa = s[:, 3 * W:]
        th = jnp.tanh(wa)
        wl = w0_ref[...] + _bdot(th, w2_ref[...])
        apre = a0_ref[...] + _bdot(wa, a2_ref[...])
        bd = _head_blockdiag()
        k = s[:, W:2 * W]
        _, vjp = jax.vjp(lambda k_, wl_, ap_, kkw_, kaw_: _prep_elem(k_, wl_, ap_, kkw_, kaw_, bd),
                         k, wl, apre, kk_ref[...], ka_ref[...])
        dk, dwl, dap, dkkw, dkaw = vjp((dlw_ref[...], dk2_ref[...] + dk22_ref[...], daa_ref[...], dbb_ref[...]))
        dwa = _bdot_nt(dwl, w2_ref[...]) * (1.0 - th * th) + _bdot_nt(dap, a2_ref[...])
        ds = jnp.concatenate([dr_ref[...] + dr2_ref[...], dk, dv_ref[...] + dv2_ref[...], dwa], axis=-1)

        @pl.when(i == 0)
        def _():
            for ref in (dmu_ref, dw0_ref, dw2_ref, da0_ref, da2_ref, dkk_ref, dka_ref, carry):
                ref[...] = jnp.zeros_like(ref)

        dmu_ref[...] += jnp.sum(ds * (p_prev - p), axis=0, keepdims=True)
        dw0_ref[...] += jnp.sum(dwl, axis=0, keepdims=True)
        da0_ref[...] += jnp.sum(dap, axis=0, keepdims=True)
        dw2_ref[...] += _bdot_tn(th, dwl)
        da2_ref[...] += _bdot_tn(wa, dap)
        dkk_ref[...] += dkkw
        dka_ref[...] += dkaw
        dsm = ds * mu_v
        last = (blk % PREP_NB) == PREP_NB - 1
        nxt = jnp.where(last, 0.0, carry[0:1, :])
        up = pltpu.roll(dsm, tm - 1, 0)
        up = jnp.where(_iota2(up.shape, 0) == tm - 1, nxt, up)
        dps_ref[...] = ds - dsm + up
        carry[0:1, :] = dsm[0:1, :]

    vec = _const_spec((1, W))
    rrow = lambda width: pl.BlockSpec((tm, width), lambda i: (rev(i), 0))
    return pl.pallas_call(
        body, grid=(nb,), name="even_prep_bwd",
        in_specs=[rrow(SHIFT), _prev_spec(SHIFT, rev), _const_spec((1, SHIFT)), vec,
                  _const_spec((2 * LORA, W)), vec, _const_spec((2 * LORA, W)), vec, vec] + [rrow(W)] * 9,
        out_specs=[rrow(SHIFT), _const_spec((1, SHIFT)), vec, _const_spec((2 * LORA, W)), vec,
                   _const_spec((2 * LORA, W)), vec, vec],
        out_shape=[jax.ShapeDtypeStruct((T, SHIFT), F32), jax.ShapeDtypeStruct((1, SHIFT), F32),
                   jax.ShapeDtypeStruct((1, W), F32), jax.ShapeDtypeStruct((2 * LORA, W), F32),
                   jax.ShapeDtypeStruct((1, W), F32), jax.ShapeDtypeStruct((2 * LORA, W), F32),
                   jax.ShapeDtypeStruct((1, W), F32), jax.ShapeDtypeStruct((1, W), F32)],
        scratch_shapes=[pltpu.VMEM((8, SHIFT), F32)],
        compiler_params=_cparams(("arbitrary",), VMEM_BIG),
    )(ps, ps, mu, w0, w2x, a0, a2x, kkw, kaw, dr, dlw, dk2, dv, daa, dbb, dr2, dk22, dv2)


def _chunk_masks():
    row = _iota2((2 * L, 2 * L), 0)
    col = _iota2((2 * L, 2 * L), 1)
    step = col & (L - 1)
    keep = ((row < L) & (row > step)) | ((row >= L) & (row - L >= step))
    r1 = _iota2((L, L), 0)
    c1 = _iota2((L, L), 1)
    return keep.astype(F32), (r1 >= c1).astype(F32), (r1 == c1).astype(F32)


def _scaled(r, lw, k2, aa, bb, tri):
    g = _hdot(tri, lw)
    eg = jnp.exp(g)
    eng = jnp.exp(-g)
    egp = jnp.exp(g - lw)
    return eg, eng, egp, aa * egp, r * eg, bb * eng, k2 * eng


def _head_cols(h):
    return slice(h * HD, (h + 1) * HD)


def _per_head(a):
    return [a[:, _head_cols(h)] for h in range(NH)]


def _heads_fwd(at, rt, bt, kt, v, s0, egl, keep, eye):
    x = [jnp.concatenate([a, r], axis=0).astype(BF16) for a, r in zip(at, rt)]
    yk = [jnp.concatenate([b, k], axis=0).astype(BF16) for b, k in zip(bt, kt)]
    vb = [a.astype(BF16) for a in v]
    m = [_bdot_nt(a, b) * keep for a, b in zip(x, yk)]
    xh = [_bdot_nt(a, s) for a, s in zip(x, s0)]
    p = [a[:L, :L] for a in m]
    tinv = [eye + a for a in p]
    for _ in range(5):
        p = [_bdot(a, a) for a in p]
        tinv = [t + _bdot(t, a) for t, a in zip(tinv, p)]
    u = [_bdot(t, h[:L] + _bdot(a[:L, L:], w)) for t, h, a, w in zip(tinv, xh, m, vb)]
    uv = [jnp.concatenate([a, w], axis=0).astype(BF16) for a, w in zip(u, v)]
    y = [h[L:] + _bdot(a[L:], w) for h, a, w in zip(xh, m, uv)]
    sn = [e * (s + _bdot_tn(w, b)) for e, s, w, b in zip(egl, s0, uv, yk)]
    return y, sn, (x, yk, m, tinv, uv)


def _heads_bwd(at, rt, bt, kt, v, s0, egl, dy, dsn, keep, eye):
    _, sn, (x, yk, m, tinv, uv) = _heads_fwd(at, rt, bt, kt, v, s0, egl, keep, eye)
    dzs = [d * e for d, e in zip(dsn, egl)]
    dgl = [jnp.sum(d * s, axis=0, keepdims=True) for d, s in zip(dsn, sn)]
    dyb = [a.astype(BF16) for a in dy]
    t1 = [_bdot_tn(a[L:], d) for a, d in zip(m, dyb)]
    t2 = [_bdot_nt(b, d) for b, d in zip(yk, dzs)]
    drhs = [_bdot_tn(t, a[:L] + b[:L]) for t, a, b in zip(tinv, t1, t2)]
    dv = [a[L:] + b[L:] + _bdot_tn(c[:L, L:], d) for a, b, c, d in zip(t1, t2, m, drhs)]
    gg = [jnp.concatenate([a, b], axis=0).astype(BF16) for a, b in zip(drhs, dy)]
    ds0 = [d + _bdot_tn(g, a) for d, g, a in zip(dzs, gg, x)]
    dm = [_bdot_nt(g, w) * keep for g, w in zip(gg, uv)]
    dx = [_bdot(g, s) + _bdot(d, b) for g, s, d, b in zip(gg, s0, dm, yk)]
    dyk = [_bdot_tn(d, a) + _bdot(w, z) for d, a, w, z in zip(dm, x, uv, dzs)]
    return ([a[:L] for a in dx], [a[L:] for a in dx], [a[:L] for a in dyk], [a[L:] for a in dyk], dv, dgl, ds0)


def rwkv_fwd(r, lw, k2, v, aa, bb):
    def body(r_ref, lw_ref, k2_ref, v_ref, aa_ref, bb_ref, y_ref, hs_ref, state):
        @pl.when(pl.program_id(1) == 0)
        def _():
            state[...] = jnp.zeros_like(state)

        s_all = state[...]
        hs_ref[0] = s_all
        keep, tri, eye = _chunk_masks()
        eg, _, _, at, rt, bt, kt = _scaled(r_ref[...], lw_ref[...], k2_ref[...], aa_ref[...], bb_ref[...], tri)
        s0 = [s_all[_head_cols(h), :] for h in range(NH)]
        y, sn, _ = _heads_fwd(_per_head(at), _per_head(rt), _per_head(bt), _per_head(kt), _per_head(v_ref[...]), s0,
                              _per_head(eg[L - 1:L, :]), keep, eye)
        y_ref[...] = jnp.concatenate(y, axis=-1)
        state[...] = jnp.concatenate(sn, axis=0)

    blk = pl.BlockSpec((L, W), lambda b, c: (b * NC + c, 0))
    return pl.pallas_call(
        body, grid=(NSEQ, NC), name="rwkv_fwd",
        in_specs=[blk] * 6,
        out_specs=[blk, pl.BlockSpec((1, W, HD), lambda b, c: (b * NC + c, 0, 0))],
        out_shape=[jax.ShapeDtypeStruct((T, W), F32), jax.ShapeDtypeStruct((NSEQ * NC, W, HD), F32)],
        scratch_shapes=[pltpu.VMEM((W, HD), F32)],
        compiler_params=_cparams(("parallel", "arbitrary")),
    )(r, lw, k2, v, aa, bb)


def rwkv_bwd(r, lw, k2, v, aa, bb, hs, dy):
    def body(r_ref, lw_ref, k2_ref, v_ref, aa_ref, bb_ref, hs_ref, dy_ref,
             dr_ref, dlw_ref, dk2_ref, dv_ref, daa_ref, dbb_ref, dstate):
        @pl.when(pl.program_id(1) == 0)
        def _():
            dstate[...] = jnp.zeros_like(dstate)

        keep, tri, eye = _chunk_masks()
        eg, eng, egp, at, rt, bt, kt = _scaled(r_ref[...], lw_ref[...], k2_ref[...], aa_ref[...], bb_ref[...], tri)
        s_all = hs_ref[0]
        ds_all = dstate[...]
        s0 = [s_all[_head_cols(h), :] for h in range(NH)]
        dsn = [ds_all[_head_cols(h), :] for h in range(NH)]
        dat, drt, dbt, dkt, dv, dgl, ds0 = _heads_bwd(
            _per_head(at), _per_head(rt), _per_head(bt), _per_head(kt), _per_head(v_ref[...]), s0,
            _per_head(eg[L - 1:L, :]), _per_head(dy_ref[...]), dsn, keep, eye)
        dstate[...] = jnp.concatenate(ds0, axis=0)
        dv_ref[...] = jnp.concatenate(dv, axis=-1)
        dat, drt, dbt, dkt, dgl = (jnp.concatenate(a, axis=-1) for a in (dat, drt, dbt, dkt, dgl))
        dg = drt * rt - dbt * bt - dkt * kt
        dg = dg + jnp.where(_iota2(dg.shape, 0) == L - 1, dgl, 0.0)
        dgp = dat * at
        dlw_ref[...] = _hdot_tn(tri, dg + dgp) - dgp
        dr_ref[...] = drt * eg
        daa_ref[...] = dat * egp
        dbb_ref[...] = dbt * eng
        dk2_ref[...] = dkt * eng

    blk = pl.BlockSpec((L, W), lambda b, c: (b * NC + NC - 1 - c, 0))
    return pl.pallas_call(
        body, grid=(NSEQ, NC), name="rwkv_bwd",
        in_specs=[blk] * 6 + [pl.BlockSpec((1, W, HD), lambda b, c: (b * NC + NC - 1 - c, 0, 0)), blk],
        out_specs=[blk] * 6,
        out_shape=[jax.ShapeDtypeStruct((T, W), F32)] * 6,
        scratch_shapes=[pltpu.VMEM((W, HD), F32)],
        compiler_params=_cparams(("parallel", "arbitrary")),
    )(r, lw, k2, v, aa, bb, hs, dy)


def _post_math(y, r, k2, v, ga, o, gb, lng, lnb, rk, bd):
    mu = _headsum(y, bd) * (1.0 / HD)
    yc = y - mu
    var = _headsum(yc * yc, bd) * (1.0 / HD)
    yn = yc * lax.rsqrt(var + GN_EPS) * lng + lnb
    bonus = _headsum(r * k2 * rk, bd) * v
    return (yn + bonus) * _silu(ga), o * _silu(gb)


def even_post(y, r, k2, v, ga, o, gb, lng, lnb, rk):
    tm = 256

    def body(y_ref, r_ref, k2_ref, v_ref, ga_ref, o_ref, gb_ref, lng_ref, lnb_ref, rk_ref, z_ref):
        ya, yb = _post_math(y_ref[...], r_ref[...], k2_ref[...], v_ref[...], ga_ref[...], o_ref[...], gb_ref[...],
                            lng_ref[...], lnb_ref[...], rk_ref[...], _head_blockdiag())
        z_ref[:, 0:W] = ya.astype(BF16)
        z_ref[:, W:2 * W] = yb.astype(BF16)

    vec = _const_spec((1, W))
    return pl.pallas_call(
        body, grid=(T // tm,), name="even_post",
        in_specs=[_row_spec(tm, W)] * 7 + [vec] * 3,
        out_specs=_row_spec(tm, D), out_shape=jax.ShapeDtypeStruct((T, D), BF16),
        compiler_params=_cparams(("parallel",)),
    )(y, r, k2, v, ga, o, gb, lng, lnb, rk)


def even_post_bwd(y, r, k2, v, ga, o, gb, lng, lnb, rk, dz):
    tm = 256

    def body(y_ref, r_ref, k2_ref, v_ref, ga_ref, o_ref, gb_ref, lng_ref, lnb_ref, rk_ref, dz_ref,
             dy_ref, dr_ref, dk2_ref, dv_ref, dga_ref, do_ref, dgb_ref, dlng_ref, dlnb_ref, drk_ref):
        bd = _head_blockdiag()
        _, vjp = jax.vjp(lambda *a: _post_math(*a, bd), y_ref[...], r_ref[...], k2_ref[...], v_ref[...], ga_ref[...],
                         o_ref[...], gb_ref[...], lng_ref[...], lnb_ref[...], rk_ref[...])
        dzv = dz_ref[...]
        dy, dr, dk2, dv, dga, do, dgb, dlng, dlnb, drk = vjp((dzv[:, 0:W], dzv[:, W:2 * W]))
        for ref, val in ((dy_ref, dy), (dr_ref, dr), (dk2_ref, dk2), (dv_ref, dv), (dga_ref, dga), (do_ref, do),
                         (dgb_ref, dgb)):
            ref[...] = val

        @pl.when(pl.program_id(0) == 0)
        def _():
            for ref in (dlng_ref, dlnb_ref, drk_ref):
                ref[...] = jnp.zeros_like(ref)

        dlng_ref[...] += dlng
        dlnb_ref[...] += dlnb
        drk_ref[...] += drk

    vec = _const_spec((1, W))
    return pl.pallas_call(
        body, grid=(T // tm,), name="even_post_bwd",
        in_specs=[_row_spec(tm, W)] * 7 + [vec] * 3 + [_row_spec(tm, D)],
        out_specs=[_row_spec(tm, W)] * 7 + [vec] * 3,
        out_shape=[jax.ShapeDtypeStruct((T, W), F32)] * 7 + [jax.ShapeDtypeStruct((1, W), F32)] * 3,
        compiler_params=_cparams(("arbitrary",)),
    )(y, r, k2, v, ga, o, gb, lng, lnb, rk, dz)


PADSEQ = SEQ + LEFT * L
ATT_SCALE = 1.0 / math.sqrt(HD)


def _att_probs(q, kw, bias, c):
    valid = _iota2((1, BAND), 1) >= (LEFT - c) * L
    s = [jnp.where(valid, _bdot_nt(a, b) * ATT_SCALE + bias[h], NEG) for h, (a, b) in enumerate(zip(q, kw))]
    e = [jnp.exp(a - jnp.max(a, axis=-1, keepdims=True)) for a in s]
    return [a / jnp.sum(a, axis=-1, keepdims=True) for a in e]


def attention_fwd(q, kpad, vpad, bias):
    def body(q_ref, k_ref, v_ref, b_ref, o_ref):
        c = pl.program_id(1)
        start = pl.multiple_of(c * L, L)
        kw = _per_head(k_ref[pl.ds(start, BAND), :])
        vw = _per_head(v_ref[pl.ds(start, BAND), :])
        p = _att_probs(_per_head(q_ref[...].astype(BF16)), kw, b_ref[...], c)
        o_ref[...] = jnp.concatenate([_bdot(a, b) for a, b in zip(p, vw)], axis=-1)

    qblk = pl.BlockSpec((L, W), lambda b, c: (b * NC + c, 0))
    kblk = pl.BlockSpec((PADSEQ, W), lambda b, c: (b, 0))
    return pl.pallas_call(
        body, grid=(NSEQ, NC), name="attention_fwd",
        in_specs=[qblk, kblk, kblk, _const_spec((NH, L, BAND))],
        out_specs=qblk, out_shape=jax.ShapeDtypeStruct((T, W), F32),
        compiler_params=_cparams(("parallel", "arbitrary")),
    )(q, kpad, vpad, bias)


def attention_bwd(q, kpad, vpad, bias, do):
    def body(q_ref, k_ref, v_ref, b_ref, do_ref, dq_ref, dk_ref, dv_ref, db_ref):
        b = pl.program_id(0)
        c = pl.program_id(1)

        @pl.when(c == 0)
        def _():
            dk_ref[...] = jnp.zeros_like(dk_ref)
            dv_ref[...] = jnp.zeros_like(dv_ref)

        @pl.when((c == 0) & (b == 0))
        def _():
            db_ref[...] = jnp.zeros_like(db_ref)

        start = pl.multiple_of(c * L, L)
        kw = _per_head(k_ref[pl.ds(start, BAND), :])
        vw = _per_head(v_ref[pl.ds(start, BAND), :])
        qs = _per_head(q_ref[...].astype(BF16))
        dos = _per_head(do_ref[...].astype(BF16))
        p = _att_probs(qs, kw, b_ref[...], c)
        dp = [_bdot_nt(a, b) for a, b in zip(dos, vw)]
        ds = [a * (d - jnp.sum(d * a, axis=-1, keepdims=True)) for a, d in zip(p, dp)]
        dss = [(a * ATT_SCALE).astype(BF16) for a in ds]
        dq_ref[...] = jnp.concatenate([_bdot(a, b) for a, b in zip(dss, kw)], axis=-1)
        dk_ref[pl.ds(start, BAND), :] += jnp.concatenate([_bdot_tn(a, b) for a, b in zip(dss, qs)], axis=-1)
        dv_ref[pl.ds(start, BAND), :] += jnp.concatenate([_bdot_tn(a, b) for a, b in zip(p, dos)], axis=-1)
        for h in range(NH):
            db_ref[h] += ds[h]

    qblk = pl.BlockSpec((L, W), lambda b, c: (b * NC + c, 0))
    kblk = pl.BlockSpec((PADSEQ, W), lambda b, c: (b, 0))
    bblk = _const_spec((NH, L, BAND))
    return pl.pallas_call(
        body, grid=(NSEQ, NC), name="attention_bwd",
        in_specs=[qblk, kblk, kblk, bblk, qblk],
        out_specs=[qblk, kblk, kblk, bblk],
        out_shape=[jax.ShapeDtypeStruct((T, W), F32), jax.ShapeDtypeStruct((NSEQ * PADSEQ, W), F32),
                   jax.ShapeDtypeStruct((NSEQ * PADSEQ, W), F32), jax.ShapeDtypeStruct((NH, L, BAND), F32)],
        compiler_params=_cparams(("arbitrary", "arbitrary"), VMEM_BIG),
    )(q, kpad, vpad, bias, do)


def attention_bias(table):
    n = np.arange(BAND + L - 1)
    idx = np.clip(LEFT * L + (L - 1) - n, -CLIP, CLIP) + CLIP
    lo = int(idx.min())
    rev = jnp.flip(table[:, lo:], axis=1)
    n_top = int((idx == 2 * CLIP).sum())
    ext = jnp.concatenate([jnp.broadcast_to(table[:, 2 * CLIP:], (NH, n_top - 1)), rev], axis=1)
    return jnp.stack([ext[:, L - 1 - i:L - 1 - i + BAND] for i in range(L)], axis=1)


def _group_cols(g):
    return slice(g * SGC, (g + 1) * SGC)


def _sg_norm(v, lng, lnb):
    gv = _gelu(v)
    gc = gv - jnp.mean(gv, axis=-1, keepdims=True)
    rstd = lax.rsqrt(jnp.mean(gc * gc, axis=-1, keepdims=True) + LN_EPS)
    xhat = gc * rstd
    return xhat, rstd, xhat * lng + lnb


def gmlp_fwd(u, v, gate, lng, lnb, wm_bf, sgb_t):
    def body(u_ref, v_ref, gt_ref, lng_ref, lnb_ref, wm_ref, sb_ref, z_ref):
        _, _, vln = _sg_norm(v_ref[...], lng_ref[...], lnb_ref[...])
        vlb = vln.astype(BF16)
        for g in range(NG):
            cs = _group_cols(g)
            sv = jnp.dot(wm_ref[g], vlb[:, cs], preferred_element_type=F32) + sb_ref[:, g:g + 1]
            z_ref[:, cs] = (_gelu(u_ref[:, cs]) * sv * _silu(gt_ref[:, cs])).astype(BF16)

    return pl.pallas_call(
        body, grid=(T // SGC,), name="gmlp_fwd",
        in_specs=[_row_spec(SGC, D)] * 3 + [_const_spec((1, D))] * 2 + [_const_spec((NG, SGC, SGC)),
                                                                      _const_spec((SGC, NG))],
        out_specs=_row_spec(SGC, D), out_shape=jax.ShapeDtypeStruct((T, D), BF16),
        compiler_params=_cparams(("parallel",)),
    )(u, v, gate, lng, lnb, wm_bf, sgb_t)


def gmlp_bwd(u, v, gate, lng, lnb, wm_bf, sgb_t, dz):
    def body(u_ref, v_ref, gt_ref, lng_ref, lnb_ref, wm_ref, sb_ref, dz_ref,
             du_ref, dv_ref, dgt_ref, dlng_ref, dlnb_ref, dwm_ref, dsb_ref):
        @pl.when(pl.program_id(0) == 0)
        def _():
            for ref in (dlng_ref, dlnb_ref, dwm_ref, dsb_ref):
                ref[...] = jnp.zeros_like(ref)

        vv = v_ref[...]
        xhat, rstd, vln = _sg_norm(vv, lng_ref[...], lnb_ref[...])
        vlb = vln.astype(BF16)
        dvln = []
        dsv_all = []
        for g in range(NG):
            cs = _group_cols(g)
            uu = u_ref[:, cs]
            gg = gt_ref[:, cs]
            dzz = dz_ref[:, cs]
            sv = jnp.dot(wm_ref[g], vlb[:, cs], preferred_element_type=F32) + sb_ref[:, g:g + 1]
            gu = _gelu(uu)
            sg = _silu(gg)
            dsv = dzz * gu * sg
            dgt_ref[:, cs] = dzz * gu * sv * _dsilu(gg)
            du_ref[:, cs] = dzz * sv * sg * _dgelu(uu)
            dsb16 = dsv.astype(BF16)
            dvln.append(lax.dot_general(wm_ref[g], dsb16, (((0,), (0,)), ((), ())), preferred_element_type=F32))
            dwm_ref[g] += lax.dot_general(dsb16, vlb[:, cs], (((1,), (1,)), ((), ())), preferred_element_type=F32)
            dsv_all.append(dsv)
        dvl = jnp.concatenate(dvln, axis=-1)
        dsv_cat = jnp.concatenate(dsv_all, axis=-1)
        sel = (_iota2((D, NG), 0) // SGC == _iota2((D, NG), 1)).astype(F32)
        dsb_ref[...] += _hdot(dsv_cat, sel)
        dlng_ref[...] += jnp.sum(dvl * xhat, axis=0, keepdims=True)
        dlnb_ref[...] += jnp.sum(dvl, axis=0, keepdims=True)
        dxh = dvl * lng_ref[...]
        dgv = rstd * (dxh - jnp.mean(dxh, axis=-1, keepdims=True)
                      - xhat * jnp.mean(dxh * xhat, axis=-1, keepdims=True))
        dv_ref[...] = dgv * _dgelu(vv)

    return pl.pallas_call(
        body, grid=(T // SGC,), name="gmlp_bwd",
        in_specs=[_row_spec(SGC, D)] * 3 + [_const_spec((1, D))] * 2
        + [_const_spec((NG, SGC, SGC)), _const_spec((SGC, NG)), _row_spec(SGC, D)],
        out_specs=[_row_spec(SGC, D)] * 3 + [_const_spec((1, D))] * 2 + [_const_spec((NG, SGC, SGC)),
                                                                       _const_spec((SGC, NG))],
        out_shape=[jax.ShapeDtypeStruct((T, D), F32)] * 3 + [jax.ShapeDtypeStruct((1, D), F32)] * 2
        + [jax.ShapeDtypeStruct((NG, SGC, SGC), F32), jax.ShapeDtypeStruct((SGC, NG), F32)],
        compiler_params=_cparams(("arbitrary",)),
    )(u, v, gate, lng, lnb, wm_bf, sgb_t, dz)


NCHIP = 4
NDEV = 8
ANY = pl.BlockSpec(memory_space=pl.ANY)


def exchange_xy(arrs, name):
    n = len(arrs)

    def body(*refs):
        ins, outs = refs[:n], refs[n:2 * n]
        send, recv, loc = refs[2 * n:]
        x, y, c = lax.axis_index("x"), lax.axis_index("y"), lax.axis_index("c")
        me = 2 * x + y
        peers = [(1 - x, y), (x, 1 - y), (1 - x, 1 - y)]
        local = [pltpu.make_async_copy(ins[t].at[me], outs[t].at[me], loc.at[t]) for t in range(n)]
        for cp in local:
            cp.start()
        remote = []
        for t in range(n):
            for j, (px, py) in enumerate(peers):
                remote.append((pltpu.make_async_remote_copy(
                    src_ref=ins[t].at[2 * px + py], dst_ref=outs[t].at[me], send_sem=send.at[t, j],
                    recv_sem=recv.at[t, j], device_id=(px, py, c), device_id_type=MESH), t, j, 2 * px + py))
        for cp, _, _, _ in remote:
            cp.start()
        for cp, t, j, chip in remote:
            cp.wait_send()
            pltpu.make_async_remote_copy(
                src_ref=ins[t].at[chip], dst_ref=outs[t].at[chip], send_sem=send.at[t, j], recv_sem=recv.at[t, j],
                device_id=(x, y, c), device_id_type=MESH).wait_recv()
        for cp in local:
            cp.wait()

    return pl.pallas_call(
        body, name=name, in_specs=[ANY] * n, out_specs=[ANY] * n,
        out_shape=[jax.ShapeDtypeStruct(a.shape, a.dtype) for a in arrs],
        scratch_shapes=[pltpu.SemaphoreType.DMA((n, 3)), pltpu.SemaphoreType.DMA((n, 3)), pltpu.SemaphoreType.DMA((n,))],
    )(*arrs)


def exchange_c(arrs, name):
    n = len(arrs)

    def body(*refs):
        ins, outs = refs[:n], refs[n:2 * n]
        send, recv = refs[2 * n:]
        sibling = (lax.axis_index("x"), lax.axis_index("y"), 1 - lax.axis_index("c"))
        copies = [pltpu.make_async_remote_copy(src_ref=ins[t], dst_ref=outs[t], send_sem=send.at[t], recv_sem=recv.at[t],
                                               device_id=sibling, device_id_type=MESH) for t in range(n)]
        for cp in copies:
            cp.start()
        for cp in copies:
            cp.wait()

    return pl.pallas_call(
        body, name=name, in_specs=[ANY] * n, out_specs=[ANY] * n,
        out_shape=[jax.ShapeDtypeStruct(a.shape, a.dtype) for a in arrs],
        scratch_shapes=[pltpu.SemaphoreType.DMA((n,)), pltpu.SemaphoreType.DMA((n,))],
    )(*arrs)


def gather_weights(arrs, split):
    n = len(arrs)

    def body(*refs):
        ins, outs = refs[:n], refs[n:2 * n]
        send1, recv1, send2, recv2, loc = refs[2 * n:]
        x, y, c = lax.axis_index("x"), lax.axis_index("y"), lax.axis_index("c")
        me = 2 * x + y
        sibling = (x, y, 1 - c)
        peers = [(1 - x, y), (x, 1 - y), (1 - x, 1 - y)]

        def rows_of(t, core):
            half = arrs[t].shape[0] // 2
            return pl.ds(core * half, half)

        def part(ref, t, core):
            return ref.at[rows_of(t, core)] if split[t] else ref

        local = [pltpu.make_async_copy(ins[t], outs[t].at[me], loc.at[t]) for t in range(n)]
        for cp in local:
            cp.start()
        first = []
        for t in range(n):
            for j, (px, py) in enumerate(peers):
                first.append(pltpu.make_async_remote_copy(
                    src_ref=part(ins[t], t, c), dst_ref=part(outs[t].at[me], t, c), send_sem=send1.at[t, j],
                    recv_sem=recv1.at[t, j], device_id=(px, py, c), device_id_type=MESH))
        for cp in first:
            cp.start()
        passed = []
        for t in range(n):
            for j, (px, py) in enumerate(peers):
                landed = part(outs[t].at[2 * px + py], t, c)
                pltpu.make_async_remote_copy(
                    src_ref=landed, dst_ref=landed, send_sem=send1.at[t, j], recv_sem=recv1.at[t, j],
                    device_id=(x, y, c), device_id_type=MESH).wait_recv()
                if split[t]:
                    cp = pltpu.make_async_remote_copy(
                        src_ref=landed, dst_ref=landed, send_sem=send2.at[t, j], recv_sem=recv2.at[t, j],
                        device_id=sibling, device_id_type=MESH)
                    cp.start()
                    passed.append(cp)
        for t in range(n):
            for j, (px, py) in enumerate(peers):
                if split[t]:
                    other = part(outs[t].at[2 * px + py], t, 1 - c)
                    pltpu.make_async_remote_copy(
                        src_ref=other, dst_ref=other, send_sem=send2.at[t, j], recv_sem=recv2.at[t, j],
                        device_id=(x, y, c), device_id_type=MESH).wait_recv()
        for cp in first + passed:
            cp.wait_send()
        for cp in local:
            cp.wait()

    return pl.pallas_call(
        body, name="gather_weights", in_specs=[ANY] * n, out_specs=[ANY] * n,
        out_shape=[jax.ShapeDtypeStruct((NCHIP,) + a.shape, a.dtype) for a in arrs],
        scratch_shapes=[pltpu.SemaphoreType.DMA((n, 3))] * 4 + [pltpu.SemaphoreType.DMA((n,))],
    )(*arrs)


def allgather_all(buf, name):
    def body(in_ref, out_ref, send, recv, loc):
        x, y, c = lax.axis_index("x"), lax.axis_index("y"), lax.axis_index("c")
        me = 4 * x + 2 * y + c
        mine = pltpu.make_async_copy(in_ref, out_ref.at[me], loc)
        mine.start()
        copies = []
        for j in range(1, NDEV):
            px, py, pc = x ^ (j >> 2), y ^ ((j >> 1) & 1), c ^ (j & 1)
            copies.append((pltpu.make_async_remote_copy(
                src_ref=in_ref, dst_ref=out_ref.at[me], send_sem=send.at[j], recv_sem=recv.at[j],
                device_id=(px, py, pc), device_id_type=MESH), j, 4 * px + 2 * py + pc))
        for cp, _, _ in copies:
            cp.start()
        for cp, j, peer in copies:
            cp.wait_send()
            pltpu.make_async_remote_copy(
                src_ref=in_ref, dst_ref=out_ref.at[peer], send_sem=send.at[j], recv_sem=recv.at[j],
                device_id=(x, y, c), device_id_type=MESH).wait_recv()
        mine.wait()

    return pl.pallas_call(
        body, name=name, in_specs=[ANY], out_specs=ANY,
        out_shape=jax.ShapeDtypeStruct((NDEV,) + buf.shape, buf.dtype),
        scratch_shapes=[pltpu.SemaphoreType.DMA((NDEV,)), pltpu.SemaphoreType.DMA((NDEV,)), pltpu.SemaphoreType.DMA],
    )(buf)


def _adam_math(g, w, m, v):
    m = ADAM_B1 * m + (1.0 - ADAM_B1) * g
    v = ADAM_B2 * v + (1.0 - ADAM_B2) * (g * g)
    m_hat = m / (1.0 - ADAM_B1 ** ADAM_STEP)
    v_hat = v / (1.0 - ADAM_B2 ** ADAM_STEP)
    delta = -ADAM_LR * (m_hat / (jnp.sqrt(v_hat) + ADAM_EPS) + ADAM_WD * w)
    return delta, m, v


def _rows_tile(rows):
    return rows if rows <= 256 else 256


def sum_chips(parts, name):
    _, rows, cols = parts.shape
    tr = _rows_tile(rows)

    def body(p_ref, o_ref):
        p = [p_ref[s].astype(F32) for s in range(NCHIP)]
        o_ref[...] = ((p[0] + p[1]) + p[2]) + p[3]

    return pl.pallas_call(
        body, grid=(rows // tr,), name=name,
        in_specs=[pl.BlockSpec((NCHIP, tr, cols), lambda i: (0, i, 0))],
        out_specs=pl.BlockSpec((tr, cols), lambda i: (i, 0)),
        out_shape=jax.ShapeDtypeStruct((rows, cols), F32),
        compiler_params=_cparams(("parallel",)),
    )(parts)


def adam_shard(p_mine, p_sib, w, m, v, name):
    rows, cols = w.shape
    tr = _rows_tile(rows)

    def body(a_ref, b_ref, w_ref, m_ref, v_ref, g_ref, d_ref, mo_ref, vo_ref):
        g = a_ref[...] + b_ref[...]
        g_ref[...] = g
        d_ref[...], mo_ref[...], vo_ref[...] = _adam_math(g, w_ref[...], m_ref[...], v_ref[...])

    spec = pl.BlockSpec((tr, cols), lambda i: (i, 0))
    return pl.pallas_call(
        body, grid=(rows // tr,), name=name, in_specs=[spec] * 5, out_specs=[spec] * 4,
        out_shape=[jax.ShapeDtypeStruct((rows, cols), F32)] * 4,
        compiler_params=_cparams(("parallel",)),
    )(p_mine, p_sib, w, m, v)


def adam_replicated(parts, w, m, v):
    rows = w.shape[0]

    def body(p_ref, w_ref, m_ref, v_ref, g_ref, d_ref, mo_ref, vo_ref):
        g = p_ref[0]
        for d in range(1, NDEV):
            g = g + p_ref[d]
        g_ref[...] = g
        d_ref[...], mo_ref[...], vo_ref[...] = _adam_math(g, w_ref[...], m_ref[...], v_ref[...])

    return pl.pallas_call(
        body, name="adam_replicated", out_shape=[jax.ShapeDtypeStruct((rows, 128), F32)] * 4,
    )(parts, w, m, v)


def _pack(arrs):
    pieces = []
    for a in arrs:
        flat = a.reshape(-1)
        pad = (-flat.shape[0]) % 128
        pieces.append(jnp.pad(flat, (0, pad)) if pad else flat)
    flat = jnp.concatenate(pieces)
    pad = (-flat.shape[0]) % 1024
    return jnp.pad(flat, (0, pad)).reshape(-1, 128)


def _unpack(buf, shapes):
    flat = buf.reshape(-1)
    out = []
    o = 0
    for s in shapes:
        n = int(np.prod(s))
        out.append(flat[o:o + n].reshape(s))
        o += n + (-n) % 128
    return out


EVEN_SPLITS = (SHIFT, W, W, W, W, W)
ODD_SPLITS = (D, D, D)


def _cols_to_chips(a):
    rows, cols = a.shape
    return a.reshape(rows, NCHIP, cols // NCHIP).transpose(1, 0, 2)


def _chips_to_cols(a):
    _, rows, n = a.shape
    return a.transpose(1, 0, 2).reshape(rows, NCHIP * n)


def kernel(x, norm_g, w_in_e, shift_mu, rw_w0, rw_w2, rw_a0, rw_a2, rw_kk, rw_ka, rw_rk, rw_lnx_g, rw_lnx_b, att_bias, w_out_e, w_in_o, sg_ln_g, sg_ln_b, sg_w, sg_b, w_out_o, final_g, loss_target, m_norm_g, m_w_in_e, m_shift_mu, m_rw_w0, m_rw_w2, m_rw_a0, m_rw_a2, m_rw_kk, m_rw_ka, m_rw_rk, m_rw_lnx_g, m_rw_lnx_b, m_att_bias, m_w_out_e, m_w_in_o, m_sg_ln_g, m_sg_ln_b, m_sg_w, m_sg_b, m_w_out_o, m_final_g, v_norm_g, v_w_in_e, v_shift_mu, v_rw_w0, v_rw_w2, v_rw_a0, v_rw_a2, v_rw_kk, v_rw_ka, v_rw_rk, v_rw_lnx_g, v_rw_lnx_b, v_att_bias, v_w_out_e, v_w_in_o, v_sg_ln_g, v_sg_ln_b, v_sg_w, v_sg_b, v_w_out_o, v_final_g):
    x2 = x.reshape(T, D)
    tgt = loss_target.reshape(T, D)

    gathered = gather_weights(
        [w_in_e[0].astype(BF16), w_out_e[0].astype(BF16), w_in_o[0].astype(BF16), w_out_o[0].astype(BF16),
         jnp.concatenate([rw_w2[0], rw_a2[0]], axis=0), jnp.concatenate([sg_ln_g, sg_ln_b], axis=0)],
        [True, True, True, True, True, False])
    wie = _chips_to_cols(gathered[0])
    woe = gathered[1].reshape(D, D)
    wio = _chips_to_cols(gathered[2])
    woo = gathered[3].reshape(D, D)
    w2 = _chips_to_cols(gathered[4][:, :LORA])
    a2 = _chips_to_cols(gathered[4][:, LORA:])
    sglg = _chips_to_cols(gathered[5][:, 0:1])
    sglb = _chips_to_cols(gathered[5][:, 1:2])
    loss_part, dx, big_g, rep_g = _local_step(
        x2, tgt, wie, woe, wio, woo, w2, a2, sglg, sglb, norm_g, shift_mu, rw_w0, rw_a0, rw_kk, rw_ka, rw_rk,
        rw_lnx_g, rw_lnx_b, att_bias, sg_w, sg_b, final_g)
    loss = lax.psum(loss_part, ("x", "y", "c"))
    d_wie, d_woe, d_wio, d_woo, d_w2, d_a2, d_sglg, d_sglb = big_g

    local = [_cols_to_chips(d_wie), d_woe.reshape(NCHIP, D // NCHIP, D), _cols_to_chips(d_wio),
             d_woo.reshape(NCHIP, D // NCHIP, D), _cols_to_chips(d_w2), _cols_to_chips(d_a2),
             _cols_to_chips(d_sglg), _cols_to_chips(d_sglb)]
    names = ["w_in_e", "w_out_e", "w_in_o", "w_out_o", "rw_w2", "rw_a2", "sg_ln_g", "sg_ln_b"]
    local = [a.astype(BF16) for a in local[:4]] + local[4:]
    received = exchange_xy(local, "scatter_grads")
    partial = [sum_chips(p, "sum_" + nm) for p, nm in zip(received, names)]
    from_sibling = exchange_c(partial, "swap_partials")
    sharded = {}
    wmv = {"w_in_e": (w_in_e[0], m_w_in_e[0], v_w_in_e[0]), "w_out_e": (w_out_e[0], m_w_out_e[0], v_w_out_e[0]),
           "w_in_o": (w_in_o[0], m_w_in_o[0], v_w_in_o[0]), "w_out_o": (w_out_o[0], m_w_out_o[0], v_w_out_o[0]),
           "rw_w2": (rw_w2[0], m_rw_w2[0], v_rw_w2[0]), "rw_a2": (rw_a2[0], m_rw_a2[0], v_rw_a2[0]),
           "sg_ln_g": (sg_ln_g, m_sg_ln_g, v_sg_ln_g), "sg_ln_b": (sg_ln_b, m_sg_ln_b, v_sg_ln_b)}
    for nm, mine, sib in zip(names, partial, from_sibling):
        w_, m_, v_ = wmv[nm]
        res = adam_shard(mine, sib, w_, m_, v_, "adam_" + nm)
        lead = nm not in ("sg_ln_g", "sg_ln_b")
        sharded[nm] = [a[None] if lead else a for a in res]

    rep_names = ["norm_g", "shift_mu", "rw_w0", "rw_a0", "rw_kk", "rw_ka", "rw_rk", "rw_lnx_g", "rw_lnx_b",
                 "att_bias", "sg_w", "sg_b", "final_g"]
    rep_w = [norm_g, shift_mu, rw_w0, rw_a0, rw_kk, rw_ka, rw_rk, rw_lnx_g, rw_lnx_b, att_bias, sg_w, sg_b, final_g]
    rep_m = [m_norm_g, m_shift_mu, m_rw_w0, m_rw_a0, m_rw_kk, m_rw_ka, m_rw_rk, m_rw_lnx_g, m_rw_lnx_b, m_att_bias,
             m_sg_w, m_sg_b, m_final_g]
    rep_v = [v_norm_g, v_shift_mu, v_rw_w0, v_rw_a0, v_rw_kk, v_rw_ka, v_rw_rk, v_rw_lnx_g, v_rw_lnx_b, v_att_bias,
             v_sg_w, v_sg_b, v_final_g]
    shapes = [w_.shape for w_ in rep_w]
    gathered_g = allgather_all(_pack(rep_g), "gather_small_grads")
    rep_out = adam_replicated(gathered_g, _pack(rep_w), _pack(rep_m), _pack(rep_v))
    rep = {nm: [] for nm in rep_names}
    for buf in rep_out:
        for nm, a in zip(rep_names, _unpack(buf, shapes)):
            rep[nm].append(a)

    order = ["norm_g", "w_in_e", "shift_mu", "rw_w0", "rw_w2", "rw_a0", "rw_a2", "rw_kk", "rw_ka", "rw_rk",
             "rw_lnx_g", "rw_lnx_b", "att_bias", "w_out_e", "w_in_o", "sg_ln_g", "sg_ln_b", "sg_w", "sg_b",
             "w_out_o", "final_g"]
    results = {**sharded, **rep}
    outs = [loss, dx.reshape(NSEQ, SEQ, D)]
    for kind in range(4):
        outs += [results[nm][kind] for nm in order]
    return tuple(outs)


def _local_step(x2, tgt, wie, woe, wio, woo, w2, a2, sglg, sglb, norm_g, shift_mu, rw_w0, rw_a0, rw_kk, rw_ka, rw_rk,
                rw_lnx_g, rw_lnx_b, att_bias, sg_w, sg_b, final_g):
    zl = jnp.zeros((LORA, W), F32)
    w2x = jnp.concatenate([w2, zl], axis=0)
    a2x = jnp.concatenate([zl, a2], axis=0)
    rk = rw_rk.reshape(1, W)
    pos = np.arange(SGC)
    sg_mask = jnp.asarray(((pos[None, :] // L) <= (pos[:, None] // L)).astype(np.float32))
    wm = (sg_w[0] * sg_mask[None]).astype(BF16)
    sgb_t = sg_b[0].T

    xn0, ps, ga, q, kb, vb, gb = ln_in_proj(x2, norm_g[0:1], wie, EVEN_SPLITS, "in_proj_even")
    r, lw, k2, v, aa, bb = even_prep(ps, shift_mu, rw_w0, w2x, rw_a0, a2x, rw_kk, rw_ka)
    y, hs = rwkv_fwd(r, lw, k2, v, aa, bb)
    bias, bias_vjp = jax.vjp(attention_bias, att_bias[0])

    def padded(a):
        return jnp.pad(a.astype(BF16).reshape(NSEQ, SEQ, W), ((0, 0), (LEFT * L, 0), (0, 0))).reshape(NSEQ * PADSEQ, W)

    kpad, vpad = padded(kb), padded(vb)
    o = attention_fwd(q, kpad, vpad, bias)
    z = even_post(y, r, k2, v, ga, o, gb, rw_lnx_g, rw_lnx_b, rk)
    h1 = out_proj(x2, z, woe, "out_proj_even")
    xn1, u, vv, gt = ln_in_proj(h1, norm_g[1:2], wio, ODD_SPLITS, "in_proj_odd")
    z2 = gmlp_fwd(u, vv, gt, sglg, sglb, wm, sgb_t)
    h2 = out_proj(h1, z2, woo, "out_proj_odd")
    dh2, loss_part, d_final_g = final_loss(h2, final_g[None], tgt)

    dz2, d_woo = out_proj_bwd(dh2, z2, woo, "out_proj_odd_bwd")
    du, dvv, dgt, d_sglg, d_sglb, d_wm, d_sgb_t = gmlp_bwd(u, vv, gt, sglg, sglb, wm, sgb_t, dz2)
    dh1, d_g1, dp2 = in_proj_bwd_x(h1, norm_g[1:2], wio, [du, dvv, dgt], dh2, "in_proj_odd_bwd")
    d_wio = matmul_tn_acc(xn1, dp2, 512, "in_proj_odd_dw")
    dz, d_woe = out_proj_bwd(dh1, z, woe, "out_proj_even_bwd")
    dy, dr2, dk22, dv2, dga, do, dgb, d_lng, d_lnb, d_rk = even_post_bwd(
        y, r, k2, v, ga, o, gb, rw_lnx_g, rw_lnx_b, rk, dz)
    dq, dkpad, dvpad, dbias = attention_bwd(q, kpad, vpad, bias, do)

    def unpadded(a):
        return a.reshape(NSEQ, PADSEQ, W)[:, LEFT * L:].reshape(T, W)

    (d_att_bias,) = bias_vjp(dbias)
    dr, dlw, dk2, dv, daa, dbb = rwkv_bwd(r, lw, k2, v, aa, bb, hs, dy)
    dps, d_mu, d_w0, d_w2x, d_a0, d_a2x, d_kk, d_ka = even_prep_bwd(
        ps, shift_mu, rw_w0, w2x, rw_a0, a2x, rw_kk, rw_ka, dr, dlw, dk2, dv, daa, dbb, dr2, dk22, dv2)
    dx, d_g0, dp = in_proj_bwd_x(x2, norm_g[0:1], wie, [dps, dga, dq, unpadded(dkpad), unpadded(dvpad), dgb], dh1,
                                 "in_proj_even_bwd")
    d_wie = matmul_tn_acc(xn0, dp, 384, "in_proj_even_dw")

    big_g = (d_wie, d_woe, d_wio, d_woo, d_w2x[:LORA], d_a2x[LORA:], d_sglg, d_sglb)
    rep_g = [jnp.concatenate([d_g0, d_g1], axis=0), d_mu, d_w0, d_a0, d_kk, d_ka, d_rk, d_lng, d_lnb, d_att_bias,
             d_wm * sg_mask[None], d_sgb_t.T, d_final_g]
    return loss_part[0, 0], dx, big_g, rep_g
```

```python
import functools
import math

import jax
import jax.numpy as jnp
import numpy as np
from jax import lax
from jax.experimental import pallas as pl
from jax.experimental.pallas import tpu as pltpu

F32 = jnp.float32
BF16 = jnp.bfloat16
HI = lax.Precision.HIGHEST

D = 1024
SEQ = 2048
NSEQ = 2
T = NSEQ * SEQ
HD = 64
NH = 8
W = 512
SHIFT = 1664
LORA = 64
EVEN_IN = 4224
ODD_IN = 3072
L = 64
NC = SEQ // L
LEFT = 8
BAND = (LEFT + 1) * L
CLIP = 128
SGC = 128
NG = 8
RMS_EPS = 1e-6
LN_EPS = 1e-5
GN_EPS = 64e-5
NEG = -1e30
VMEM_BIG = 56 * 1024 * 1024

ADAM_LR = 0.001
ADAM_B1 = 0.9
ADAM_B2 = 0.999
ADAM_EPS = 1e-08
ADAM_WD = 0.01
ADAM_STEP = 10

MESH = pl.DeviceIdType.MESH


def _bdot(a, b):
    return jnp.dot(a.astype(BF16), b.astype(BF16), preferred_element_type=F32)


def _bdot_nt(a, b):
    return lax.dot_general(a.astype(BF16), b.astype(BF16), (((1,), (1,)), ((), ())), preferred_element_type=F32)


def _bdot_tn(a, b):
    return lax.dot_general(a.astype(BF16), b.astype(BF16), (((0,), (0,)), ((), ())), preferred_element_type=F32)


def _hdot(a, b):
    return jnp.dot(a, b, precision=HI, preferred_element_type=F32)


def _hdot_nt(a, b):
    return lax.dot_general(a, b, (((1,), (1,)), ((), ())), precision=HI, preferred_element_type=F32)


def _hdot_tn(a, b):
    return lax.dot_general(a, b, (((0,), (0,)), ((), ())), precision=HI, preferred_element_type=F32)


def _iota2(shape, dim):
    return lax.broadcasted_iota(jnp.int32, shape, dim)


def _head_blockdiag():
    r = _iota2((W, W), 0) // HD
    c = _iota2((W, W), 1) // HD
    return (r == c).astype(BF16)


def _headsum_impl(x, bd):
    hi = x.astype(BF16)
    mid = (x - hi.astype(F32)).astype(BF16)
    return jnp.dot(hi, bd, preferred_element_type=F32) + jnp.dot(mid, bd, preferred_element_type=F32)


@jax.custom_vjp
def _headsum(x, bd):
    return _headsum_impl(x, bd)


def _headsum_fwd(x, bd):
    return _headsum_impl(x, bd), bd


def _headsum_bwd(bd, ct):
    return _headsum_impl(ct, bd), None


_headsum.defvjp(_headsum_fwd, _headsum_bwd)


def _silu(x):
    return x * jax.nn.sigmoid(x)


def _dsilu(x):
    s = jax.nn.sigmoid(x)
    return s * (1.0 + x * (1.0 - s))


_GELU_C = math.sqrt(2.0 / math.pi)


def _gelu(x):
    return 0.5 * x * (1.0 + jnp.tanh(_GELU_C * (x + 0.044715 * (x * x * x))))


def _dgelu(x):
    t = jnp.tanh(_GELU_C * (x + 0.044715 * (x * x * x)))
    return 0.5 * (1.0 + t) + 0.5 * x * (1.0 - t * t) * _GELU_C * (1.0 + 3.0 * 0.044715 * x * x)


def _softplus(x):
    return jnp.maximum(x, 0.0) + jnp.log(1.0 + jnp.exp(-jnp.abs(x)))


def _cparams(sem, vmem=None):
    return pltpu.CompilerParams(dimension_semantics=sem, vmem_limit_bytes=vmem)


def _row_spec(tm, width):
    return pl.BlockSpec((tm, width), lambda i: (i, 0))


def _col_spec(height, tm):
    return pl.BlockSpec((height, tm), lambda i: (0, i))


def _const_spec(shape):
    nd = len(shape)
    return pl.BlockSpec(shape, lambda *_: (0,) * nd)


def ln_in_proj(x, g, w_bf, splits, name):
    n = w_bf.shape[1]
    tm = 256
    spans = []
    o = 0
    for s in splits:
        spans.append((o, o + s))
        o += s
    assert o == n

    def body(x_ref, g_ref, w_ref, xn_ref, *outs):
        xv = x_ref[...]
        rstd = lax.rsqrt(jnp.mean(xv * xv, axis=-1, keepdims=True) + RMS_EPS)
        xn = (xv * rstd * g_ref[...]).astype(BF16)
        xn_ref[...] = xn.T
        p = jnp.dot(xn, w_ref[...], preferred_element_type=F32)
        for o_ref, (a, b) in zip(outs, spans):
            o_ref[...] = p[:, a:b]

    return pl.pallas_call(
        body, grid=(T // tm,), name=name,
        in_specs=[_row_spec(tm, D), _const_spec((1, D)), _const_spec((D, n))],
        out_specs=[_col_spec(D, tm)] + [_row_spec(tm, s) for s in splits],
        out_shape=[jax.ShapeDtypeStruct((D, T), BF16)] + [jax.ShapeDtypeStruct((T, s), F32) for s in splits],
        compiler_params=_cparams(("parallel",), VMEM_BIG),
    )(x, g, w_bf)


def in_proj_bwd_x(x, g, w_bf, dps, dres, name):
    n = w_bf.shape[1]
    tm = 256
    widths = [d.shape[1] for d in dps]

    def body(x_ref, g_ref, w_ref, dres_ref, *rest):
        dp_refs = rest[:len(widths)]
        dx_ref, dg_ref, dpc_ref = rest[len(widths):]
        dp = jnp.concatenate([r[...].astype(BF16) for r in dp_refs], axis=-1)
        dpc_ref[...] = dp
        dxn = lax.dot_general(dp, w_ref[...], (((1,), (1,)), ((), ())), preferred_element_type=F32)
        xv = x_ref[...]
        rstd = lax.rsqrt(jnp.mean(xv * xv, axis=-1, keepdims=True) + RMS_EPS)
        xhat = xv * rstd
        dgp = jnp.sum(dxn * xhat, axis=0, keepdims=True)

        @pl.when(pl.program_id(0) == 0)
        def _():
            dg_ref[...] = jnp.zeros_like(dg_ref)

        dg_ref[...] += dgp
        dxh = dxn * g_ref[...]
        dx_ref[...] = dres_ref[...] + rstd * (dxh - xhat * jnp.mean(dxh * xhat, axis=-1, keepdims=True))

    return pl.pallas_call(
        body, grid=(T // tm,), name=name,
        in_specs=[_row_spec(tm, D), _const_spec((1, D)), _const_spec((D, n)), _row_spec(tm, D)]
        + [_row_spec(tm, s) for s in widths],
        out_specs=[_row_spec(tm, D), _const_spec((1, D)), _row_spec(tm, n)],
        out_shape=[jax.ShapeDtypeStruct((T, D), F32), jax.ShapeDtypeStruct((1, D), F32),
                   jax.ShapeDtypeStruct((T, n), BF16)],
        compiler_params=_cparams(("arbitrary",), VMEM_BIG),
    )(x, g, w_bf, dres, *dps)


def matmul_acc(at_bf, b_bf, tn, name):
    k = at_bf.shape[0]
    n = b_bf.shape[1]
    tm = 512
    nt = T // tm

    def body(a_ref, b_ref, o_ref):
        @pl.when(pl.program_id(1) == 0)
        def _():
            o_ref[...] = jnp.zeros_like(o_ref)

        o_ref[...] += jnp.dot(a_ref[...], b_ref[...], preferred_element_type=F32)

    return pl.pallas_call(
        body, grid=(n // tn, nt), name=name,
        in_specs=[pl.BlockSpec((k, tm), lambda j, i: (0, i)), pl.BlockSpec((tm, tn), lambda j, i: (i, j))],
        out_specs=pl.BlockSpec((k, tn), lambda j, i: (0, j)),
        out_shape=jax.ShapeDtypeStruct((k, n), F32),
        compiler_params=_cparams(("parallel", "arbitrary"), VMEM_BIG),
    )(at_bf, b_bf)


def out_proj(h, z_bf, w_bf, name):
    tm = 256

    def body(h_ref, z_ref, w_ref, o_ref):
        o_ref[...] = h_ref[...] + jnp.dot(z_ref[...], w_ref[...], preferred_element_type=F32)

    return pl.pallas_call(
        body, grid=(T // tm,), name=name,
        in_specs=[_row_spec(tm, D), _row_spec(tm, D), _const_spec((D, D))],
        out_specs=_row_spec(tm, D), out_shape=jax.ShapeDtypeStruct((T, D), F32),
        compiler_params=_cparams(("parallel",)),
    )(h, z_bf, w_bf)


def out_proj_bwd(dh, zt_bf, w_bf, name):
    tm = 256

    def body(dh_ref, zt_ref, w_ref, dz_ref, dw_ref):
        dhb = dh_ref[...].astype(BF16)
        dz_ref[...] = lax.dot_general(dhb, w_ref[...], (((1,), (1,)), ((), ())), preferred_element_type=F32)

        @pl.when(pl.program_id(0) == 0)
        def _():
            dw_ref[...] = jnp.zeros_like(dw_ref)

        dw_ref[...] += jnp.dot(zt_ref[...], dhb, preferred_element_type=F32)

    return pl.pallas_call(
        body, grid=(T // tm,), name=name,
        in_specs=[_row_spec(tm, D), _col_spec(D, tm), _const_spec((D, D))],
        out_specs=[_row_spec(tm, D), _const_spec((D, D))],
        out_shape=[jax.ShapeDtypeStruct((T, D), F32), jax.ShapeDtypeStruct((D, D), F32)],
        compiler_params=_cparams(("arbitrary",)),
    )(dh, zt_bf, w_bf)


def final_loss(h, g, target):
    tm = 256

    def body(h_ref, g_ref, t_ref, dh_ref, loss_ref, dg_ref):
        xv = h_ref[...]
        rstd = lax.rsqrt(jnp.mean(xv * xv, axis=-1, keepdims=True) + RMS_EPS)
        xhat = xv * rstd
        err = xhat * g_ref[...] - t_ref[...]
        part = 0.5 * jnp.sum(jnp.mean(err * err, axis=-1, keepdims=True), axis=0, keepdims=True)
        dout = err * (1.0 / D)

        @pl.when(pl.program_id(0) == 0)
        def _():
            loss_ref[...] = jnp.zeros_like(loss_ref)
            dg_ref[...] = jnp.zeros_like(dg_ref)

        loss_ref[...] += jnp.broadcast_to(part, loss_ref.shape)
        dg_ref[...] += jnp.sum(dout * xhat, axis=0, keepdims=True)
        dxh = dout * g_ref[...]
        dh_ref[...] = rstd * (dxh - xhat * jnp.mean(dxh * xhat, axis=-1, keepdims=True))

    return pl.pallas_call(
        body, grid=(T // tm,), name="final_loss",
        in_specs=[_row_spec(tm, D), _const_spec((1, D)), _row_spec(tm, D)],
        out_specs=[_row_spec(tm, D), _const_spec((8, 128)), _const_spec((1, D))],
        out_shape=[jax.ShapeDtypeStruct((T, D), F32), jax.ShapeDtypeStruct((8, 128), F32),
                   jax.ShapeDtypeStruct((1, D), F32)],
        compiler_params=_cparams(("arbitrary",)),
    )(h, g, target)


PREP_TM = 256
PREP_NB = SEQ // PREP_TM


def _prep_elem(k, wl, apre, kkw, kaw, bd):
    wraw = -_softplus(-wl) - 0.5
    lw = -jnp.exp(wraw)
    asig = jax.nn.sigmoid(apre)
    kkr = k * kkw
    nrm = jnp.maximum(jnp.sqrt(_headsum(kkr * kkr, bd)), 1e-12)
    kk = kkr / nrm
    k2 = k * (1.0 + (asig - 1.0) * kaw)
    return lw, k2, -kk, kk * asig


def _shifted(ps_ref, prev_ref, mu, blk):
    p = ps_ref[...]
    first = (blk % PREP_NB) == 0
    prev_row = jnp.where(first, 0.0, prev_ref[7:8, :])
    rolled = pltpu.roll(p, 1, 0)
    p_prev = jnp.where(_iota2(p.shape, 0) == 0, prev_row, rolled)
    return p, p_prev, p + (p_prev - p) * mu


def _prev_spec(width, blk_of):
    return pl.BlockSpec((8, width), lambda i: (jnp.maximum(blk_of(i) * (PREP_TM // 8) - 1, 0), 0))


def even_prep(ps, mu, w0, w2x, a0, a2x, kkw, kaw):
    tm = PREP_TM

    def body(ps_ref, prev_ref, mu_ref, w0_ref, w2_ref, a0_ref, a2_ref, kk_ref, ka_ref,
             r_ref, lw_ref, k2_ref, v_ref, aa_ref, bb_ref):
        _, _, s = _shifted(ps_ref, prev_ref, mu_ref[...], pl.program_id(0))
        wa = s[:, 3 * W:]
        wl = w0_ref[...] + _bdot(jnp.tanh(wa), w2_ref[...])
        apre = a0_ref[...] + _bdot(wa, a2_ref[...])
        lw, k2, aa, bb = _prep_elem(s[:, W:2 * W], wl, apre, kk_ref[...], ka_ref[...], _head_blockdiag())
        r_ref[...] = s[:, 0:W]
        v_ref[...] = s[:, 2 * W:3 * W]
        lw_ref[...] = lw
        k2_ref[...] = k2
        aa_ref[...] = aa
        bb_ref[...] = bb

    vec = _const_spec((1, W))
    return pl.pallas_call(
        body, grid=(T // tm,), name="even_prep",
        in_specs=[_row_spec(tm, SHIFT), _prev_spec(SHIFT, lambda i: i), _const_spec((1, SHIFT)), vec,
                  _const_spec((2 * LORA, W)), vec, _const_spec((2 * LORA, W)), vec, vec],
        out_specs=[_row_spec(tm, W)] * 6,
        out_shape=[jax.ShapeDtypeStruct((T, W), F32)] * 6,
        compiler_params=_cparams(("parallel",)),
    )(ps, ps, mu, w0, w2x, a0, a2x, kkw, kaw)


def even_prep_bwd(ps, mu, w0, w2x, a0, a2x, kkw, kaw, dr, dlw, dk2, dv, daa, dbb, dr2, dk22, dv2):
    tm = PREP_TM
    nb = T // tm
    rev = lambda i: nb - 1 - i

    def body(ps_ref, prev_ref, mu_ref, w0_ref, w2_ref, a0_ref, a2_ref, kk_ref, ka_ref,
             dr_ref, dlw_ref, dk2_ref, dv_ref, daa_ref, dbb_ref, dr2_ref, dk22_ref, dv2_ref,
             dps_ref, dmu_ref, dw0_ref, dw2_ref, da0_ref, da2_ref, dkk_ref, dka_ref, carry):
        i = pl.program_id(0)
        blk = rev(i)
        mu_v = mu_ref[...]
        p, p_prev, s = _shifted(ps_ref, prev_ref, mu_v, blk)
        wa = s[:, 3 * W:]
        th = jnp.tanh(wa)
        wl = w0_ref[...] + _bdot(th, w2_ref[...])
        apre = a0_ref[...] + _bdot(wa, a2_ref[...])
        bd = _head_blockdiag()
        k = s[:, W:2 * W]
        _, vjp = jax.vjp(lambda k_, wl_, ap_, kkw_, kaw_: _prep_elem(k_, wl_, ap_, kkw_, kaw_, bd),
                         k, wl, apre, kk_ref[...], ka_ref[...])
        dk, dwl, dap, dkkw, dkaw = vjp((dlw_ref[...], dk2_ref[...] + dk22_ref[...], daa_ref[...], dbb_ref[...]))
        dwa = _bdot_nt(dwl, w2_ref[...]) * (1.0 - th * th) + _bdot_nt(dap, a2_ref[...])
        ds = jnp.concatenate([dr_ref[...] + dr2_ref[...], dk, dv_ref[...] + dv2_ref[...], dwa], axis=-1)

        @pl.when(i == 0)
        def _():
            for ref in (dmu_ref, dw0_ref, dw2_ref, da0_ref, da2_ref, dkk_ref, dka_ref, carry):
                ref[...] = jnp.zeros_like(ref)

        dmu_ref[...] += jnp.sum(ds * (p_prev - p), axis=0, keepdims=True)
        dw0_ref[...] += jnp.sum(dwl, axis=0, keepdims=True)
        da0_ref[...] += jnp.sum(dap, axis=0, keepdims=True)
        dw2_ref[...] += _bdot_tn(th, dwl)
        da2_ref[...] += _bdot_tn(wa, dap)
        dkk_ref[...] += dkkw
        dka_ref[...] += dkaw
        dsm = ds * mu_v
        last = (blk % PREP_NB) == PREP_NB - 1
        nxt = jnp.where(last, 0.0, carry[0:1, :])
        up = pltpu.roll(dsm, tm - 1, 0)
        up = jnp.where(_iota2(up.shape, 0) == tm - 1, nxt, up)
        dps_ref[...] = ds - dsm + up
        carry[0:1, :] = dsm[0:1, :]

    vec = _const_spec((1, W))
    rrow = lambda width: pl.BlockSpec((tm, width), lambda i: (rev(i), 0))
    return pl.pallas_call(
        body, grid=(nb,), name="even_prep_bwd",
        in_specs=[rrow(SHIFT), _prev_spec(SHIFT, rev), _const_spec((1, SHIFT)), vec,
                  _const_spec((2 * LORA, W)), vec, _const_spec((2 * LORA, W)), vec, vec] + [rrow(W)] * 9,
        out_specs=[rrow(SHIFT), _const_spec((1, SHIFT)), vec, _const_spec((2 * LORA, W)), vec,
                   _const_spec((2 * LORA, W)), vec, vec],
        out_shape=[jax.ShapeDtypeStruct((T, SHIFT), F32), jax.ShapeDtypeStruct((1, SHIFT), F32),
                   jax.ShapeDtypeStruct((1, W), F32), jax.ShapeDtypeStruct((2 * LORA, W), F32),
                   jax.ShapeDtypeStruct((1, W), F32), jax.ShapeDtypeStruct((2 * LORA, W), F32),
                   jax.ShapeDtypeStruct((1, W), F32), jax.ShapeDtypeStruct((1, W), F32)],
        scratch_shapes=[pltpu.VMEM((8, SHIFT), F32)],
        compiler_params=_cparams(("arbitrary",), VMEM_BIG),
    )(ps, ps, mu, w0, w2x, a0, a2x, kkw, kaw, dr, dlw, dk2, dv, daa, dbb, dr2, dk22, dv2)


def _chunk_masks():
    row = _iota2((2 * L, 2 * L), 0)
    col = _iota2((2 * L, 2 * L), 1)
    step = col & (L - 1)
    keep = ((row < L) & (row > step)) | ((row >= L) & (row - L >= step))
    r1 = _iota2((L, L), 0)
    c1 = _iota2((L, L), 1)
    return keep.astype(F32), (r1 >= c1).astype(F32), (r1 == c1).astype(F32)


def _scaled(r, lw, k2, aa, bb, tri):
    g = _hdot(tri, lw)
    eg = jnp.exp(g)
    eng = jnp.exp(-g)
    egp = jnp.exp(g - lw)
    return eg, eng, egp, aa * egp, r * eg, bb * eng, k2 * eng


def _head_cols(h):
    return slice(h * HD, (h + 1) * HD)


def _per_head(a):
    return [a[:, _head_cols(h)] for h in range(NH)]


def _heads_operands(at, rt, bt, kt):
    x = [jnp.concatenate([a, r], axis=0).astype(BF16) for a, r in zip(_per_head(at), _per_head(rt))]
    yk = [jnp.concatenate([b, k], axis=0).astype(BF16) for b, k in zip(_per_head(bt), _per_head(kt))]
    return x, yk


def _heads_matrices(x, yk, keep, eye):
    m = [_bdot_nt(a, b) * keep for a, b in zip(x, yk)]
    p = [a[:L, :L] for a in m]
    tinv = [eye + a for a in p]
    for _ in range(5):
        p = [_bdot(a, a) for a in p]
        tinv = [t + _bdot(t, a) for t, a in zip(tinv, p)]
    return [a.astype(BF16) for a in m], [a.astype(BF16) for a in tinv]


def _heads_fwd(x, yk, m, tinv, v, s0, egl):
    xh = [_bdot_nt(a, s) for a, s in zip(x, s0)]
    u = [_bdot(t, h[:L] + _bdot(a[:L, L:], w)) for t, h, a, w in zip(tinv, xh, m, v)]
    uv = [jnp.concatenate([a, w], axis=0).astype(BF16) for a, w in zip(u, v)]
    y = [h[L:] + _bdot(a[L:], w) for h, a, w in zip(xh, m, uv)]
    sn = [e * (s + _bdot_tn(w, b)) for e, s, w, b in zip(egl, s0, uv, yk)]
    return y, sn, uv


def _heads_bwd(x, yk, m, tinv, v, s0, egl, dy, dsn, keep):
    _, sn, uv = _heads_fwd(x, yk, m, tinv, v, s0, egl)
    dzs = [d * e for d, e in zip(dsn, egl)]
    dgl = [jnp.sum(d * s, axis=0, keepdims=True) for d, s in zip(dsn, sn)]
    dyb = [a.astype(BF16) for a in dy]
    t1 = [_bdot_tn(a[L:], d) for a, d in zip(m, dyb)]
    t2 = [_bdot_nt(b, d) for b, d in zip(yk, dzs)]
    drhs = [_bdot_tn(t, a[:L] + b[:L]) for t, a, b in zip(tinv, t1, t2)]
    dv = [a[L:] + b[L:] + _bdot_tn(c[:L, L:], d) for a, b, c, d in zip(t1, t2, m, drhs)]
    gg = [jnp.concatenate([a, b], axis=0).astype(BF16) for a, b in zip(drhs, dy)]
    ds0 = [d + _bdot_tn(g, a) for d, g, a in zip(dzs, gg, x)]
    dm = [_bdot_nt(g, w) * keep for g, w in zip(gg, uv)]
    dx = [_bdot(g, s) + _bdot(d, b) for g, s, d, b in zip(gg, s0, dm, yk)]
    dyk = [_bdot_tn(d, a) + _bdot(w, z) for d, a, w, z in zip(dm, x, uv, dzs)]
    return ([a[:L] for a in dx], [a[L:] for a in dx], [a[:L] for a in dyk], [a[L:] for a in dyk], dv, dgl, ds0)


def rwkv_fwd(r, lw, k2, v, aa, bb):
    def body(r_ref, lw_ref, k2_ref, v_ref, aa_ref, bb_ref, y_ref, hs_ref, m_ref, t_ref, state):
        @pl.when(pl.program_id(1) == 0)
        def _():
            state[...] = jnp.zeros_like(state)

        s_all = state[...]
        hs_ref[0] = s_all
        keep, tri, eye = _chunk_masks()
        eg, _, _, at, rt, bt, kt = _scaled(r_ref[...], lw_ref[...], k2_ref[...], aa_ref[...], bb_ref[...], tri)
        x, yk = _heads_operands(at, rt, bt, kt)
        m, tinv = _heads_matrices(x, yk, keep, eye)
        s0 = [s_all[_head_cols(h), :] for h in range(NH)]
        y, sn, _ = _heads_fwd(x, yk, m, tinv, _per_head(v_ref[...]), s0, _per_head(eg[L - 1:L, :]))
        y_ref[...] = jnp.concatenate(y, axis=-1)
        m_ref[0] = jnp.concatenate(m, axis=-1)
        t_ref[0] = jnp.concatenate(tinv, axis=-1)
        state[...] = jnp.concatenate(sn, axis=0)

    blk = pl.BlockSpec((L, W), lambda b, c: (b * NC + c, 0))
    per_chunk = lambda rows, cols: pl.BlockSpec((1, rows, cols), lambda b, c: (b * NC + c, 0, 0))
    return pl.pallas_call(
        body, grid=(NSEQ, NC), name="rwkv_fwd",
        in_specs=[blk] * 6,
        out_specs=[blk, per_chunk(W, HD), per_chunk(2 * L, NH * 2 * L), per_chunk(L, NH * L)],
        out_shape=[jax.ShapeDtypeStruct((T, W), F32), jax.ShapeDtypeStruct((NSEQ * NC, W, HD), F32),
                   jax.ShapeDtypeStruct((NSEQ * NC, 2 * L, NH * 2 * L), BF16),
                   jax.ShapeDtypeStruct((NSEQ * NC, L, NH * L), BF16)],
        scratch_shapes=[pltpu.VMEM((W, HD), F32)],
        compiler_params=_cparams(("parallel", "arbitrary")),
    )(r, lw, k2, v, aa, bb)


def rwkv_bwd(r, lw, k2, v, aa, bb, hs, ms, ts, dy):
    def body(r_ref, lw_ref, k2_ref, v_ref, aa_ref, bb_ref, hs_ref, m_ref, t_ref, dy_ref,
             dr_ref, dlw_ref, dk2_ref, dv_ref, daa_ref, dbb_ref, dstate):
        @pl.when(pl.program_id(1) == 0)
        def _():
            dstate[...] = jnp.zeros_like(dstate)

        keep, tri, _ = _chunk_masks()
        eg, eng, egp, at, rt, bt, kt = _scaled(r_ref[...], lw_ref[...], k2_ref[...], aa_ref[...], bb_ref[...], tri)
        x, yk = _heads_operands(at, rt, bt, kt)
        m_all = m_ref[0]
        t_all = t_ref[0]
        m = [m_all[:, h * 2 * L:(h + 1) * 2 * L] for h in range(NH)]
        tinv = [t_all[:, h * L:(h + 1) * L] for h in range(NH)]
        s_all = hs_ref[0]
        ds_all = dstate[...]
        s0 = [s_all[_head_cols(h), :] for h in range(NH)]
        dsn = [ds_all[_head_cols(h), :] for h in range(NH)]
        dat, drt, dbt, dkt, dv, dgl, ds0 = _heads_bwd(
            x, yk, m, tinv, _per_head(v_ref[...]), s0, _per_head(eg[L - 1:L, :]), _per_head(dy_ref[...]), dsn, keep)
        dstate[...] = jnp.concatenate(ds0, axis=0)
        dv_ref[...] = jnp.concatenate(dv, axis=-1)
        dat, drt, dbt, dkt, dgl = (jnp.concatenate(a, axis=-1) for a in (dat, drt, dbt, dkt, dgl))
        dg = drt * rt - dbt * bt - dkt * kt
        dg = dg + jnp.where(_iota2(dg.shape, 0) == L - 1, dgl, 0.0)
        dgp = dat * at
        dlw_ref[...] = _hdot_tn(tri, dg + dgp) - dgp
        dr_ref[...] = drt * eg
        daa_ref[...] = dat * egp
        dbb_ref[...] = dbt * eng
        dk2_ref[...] = dkt * eng

    blk = pl.BlockSpec((L, W), lambda b, c: (b * NC + NC - 1 - c, 0))
    per_chunk = lambda rows, cols: pl.BlockSpec((1, rows, cols), lambda b, c: (b * NC + NC - 1 - c, 0, 0))
    return pl.pallas_call(
        body, grid=(NSEQ, NC), name="rwkv_bwd",
        in_specs=[blk] * 6 + [per_chunk(W, HD), per_chunk(2 * L, NH * 2 * L), per_chunk(L, NH * L), blk],
        out_specs=[blk] * 6,
        out_shape=[jax.ShapeDtypeStruct((T, W), F32)] * 6,
        scratch_shapes=[pltpu.VMEM((W, HD), F32)],
        compiler_params=_cparams(("parallel", "arbitrary")),
    )(r, lw, k2, v, aa, bb, hs, ms, ts, dy)


def _post_math(y, r, k2, v, ga, o, gb, lng, lnb, rk, bd):
    mu = _headsum(y, bd) * (1.0 / HD)
    yc = y - mu
    var = _headsum(yc * yc, bd) * (1.0 / HD)
    yn = yc * lax.rsqrt(var + GN_EPS) * lng + lnb
    bonus = _headsum(r * k2 * rk, bd) * v
    return (yn + bonus) * _silu(ga), o * _silu(gb)


def even_post(y, r, k2, v, ga, o, gb, lng, lnb, rk):
    tm = 256

    def body(y_ref, r_ref, k2_ref, v_ref, ga_ref, o_ref, gb_ref, lng_ref, lnb_ref, rk_ref, z_ref, zt_ref):
        ya, yb = _post_math(y_ref[...], r_ref[...], k2_ref[...], v_ref[...], ga_ref[...], o_ref[...], gb_ref[...],
                            lng_ref[...], lnb_ref[...], rk_ref[...], _head_blockdiag())
        ya, yb = ya.astype(BF16), yb.astype(BF16)
        z_ref[:, 0:W] = ya
        z_ref[:, W:2 * W] = yb
        zt_ref[0:W, :] = ya.T
        zt_ref[W:2 * W, :] = yb.T

    vec = _const_spec((1, W))
    return pl.pallas_call(
        body, grid=(T // tm,), name="even_post",
        in_specs=[_row_spec(tm, W)] * 7 + [vec] * 3,
        out_specs=[_row_spec(tm, D), _col_spec(D, tm)],
        out_shape=[jax.ShapeDtypeStruct((T, D), BF16), jax.ShapeDtypeStruct((D, T), BF16)],
        compiler_params=_cparams(("parallel",)),
    )(y, r, k2, v, ga, o, gb, lng, lnb, rk)


def even_post_bwd(y, r, k2, v, ga, o, gb, lng, lnb, rk, dz):
    tm = 256

    def body(y_ref, r_ref, k2_ref, v_ref, ga_ref, o_ref, gb_ref, lng_ref, lnb_ref, rk_ref, dz_ref,
             dy_ref, dr_ref, dk2_ref, dv_ref, dga_ref, do_ref, dgb_ref, dlng_ref, dlnb_ref, drk_ref):
        bd = _head_blockdiag()
        _, vjp = jax.vjp(lambda *a: _post_math(*a, bd), y_ref[...], r_ref[...], k2_ref[...], v_ref[...], ga_ref[...],
                         o_ref[...], gb_ref[...], lng_ref[...], lnb_ref[...], rk_ref[...])
        dzv = dz_ref[...]
        dy, dr, dk2, dv, dga, do, dgb, dlng, dlnb, drk = vjp((dzv[:, 0:W], dzv[:, W:2 * W]))
        for ref, val in ((dy_ref, dy), (dr_ref, dr), (dk2_ref, dk2), (dv_ref, dv), (dga_ref, dga), (do_ref, do),
                         (dgb_ref, dgb)):
            ref[...] = val

        @pl.when(pl.program_id(0) == 0)
        def _():
            for ref in (dlng_ref, dlnb_ref, drk_ref):
                ref[...] = jnp.zeros_like(ref)

        dlng_ref[...] += dlng
        dlnb_ref[...] += dlnb
        drk_ref[...] += drk

    vec = _const_spec((1, W))
    return pl.pallas_call(
        body, grid=(T // tm,), name="even_post_bwd",
        in_specs=[_row_spec(tm, W)] * 7 + [vec] * 3 + [_row_spec(tm, D)],
        out_specs=[_row_spec(tm, W)] * 7 + [vec] * 3,
        out_shape=[jax.ShapeDtypeStruct((T, W), F32)] * 7 + [jax.ShapeDtypeStruct((1, W), F32)] * 3,
        compiler_params=_cparams(("arbitrary",)),
    )(y, r, k2, v, ga, o, gb, lng, lnb, rk, dz)


PADSEQ = SEQ + LEFT * L
ATT_SCALE = 1.0 / math.sqrt(HD)


def _att_probs(q, kw, bias, c):
    valid = _iota2((1, BAND), 1) >= (LEFT - c) * L
    s = [jnp.where(valid, _bdot_nt(a, b) * ATT_SCALE + bias[h], NEG) for h, (a, b) in enumerate(zip(q, kw))]
    e = [jnp.exp(a - jnp.max(a, axis=-1, keepdims=True)) for a in s]
    return [a / jnp.sum(a, axis=-1, keepdims=True) for a in e]


def attention_fwd(q, kpad, vpad, bias):
    def body(q_ref, k_ref, v_ref, b_ref, o_ref):
        c = pl.program_id(1)
        start = pl.multiple_of(c * L, L)
        kw = _per_head(k_ref[pl.ds(start, BAND), :])
        vw = _per_head(v_ref[pl.ds(start, BAND), :])
        p = _att_probs(_per_head(q_ref[...].astype(BF16)), kw, b_ref[...], c)
        o_ref[...] = jnp.concatenate([_bdot(a, b) for a, b in zip(p, vw)], axis=-1)

    qblk = pl.BlockSpec((L, W), lambda b, c: (b * NC + c, 0))
    kblk = pl.BlockSpec((PADSEQ, W), lambda b, c: (b, 0))
    return pl.pallas_call(
        body, grid=(NSEQ, NC), name="attention_fwd",
        in_specs=[qblk, kblk, kblk, _const_spec((NH, L, BAND))],
        out_specs=qblk, out_shape=jax.ShapeDtypeStruct((T, W), F32),
        compiler_params=_cparams(("parallel", "arbitrary")),
    )(q, kpad, vpad, bias)


def attention_bwd(q, kpad, vpad, bias, do):
    def body(q_ref, k_ref, v_ref, b_ref, do_ref, dq_ref, dk_ref, dv_ref, db_ref):
        b = pl.program_id(0)
        c = pl.program_id(1)

        @pl.when(c == 0)
        def _():
            dk_ref[...] = jnp.zeros_like(dk_ref)
            dv_ref[...] = jnp.zeros_like(dv_ref)

        @pl.when((c == 0) & (b == 0))
        def _():
            db_ref[...] = jnp.zeros_like(db_ref)

        start = pl.multiple_of(c * L, L)
        kw = _per_head(k_ref[pl.ds(start, BAND), :])
        vw = _per_head(v_ref[pl.ds(start, BAND), :])
        qs = _per_head(q_ref[...].astype(BF16))
        dos = _per_head(do_ref[...].astype(BF16))
        p = _att_probs(qs, kw, b_ref[...], c)
        dp = [_bdot_nt(a, b) for a, b in zip(dos, vw)]
        ds = [a * (d - jnp.sum(d * a, axis=-1, keepdims=True)) for a, d in zip(p, dp)]
        dss = [(a * ATT_SCALE).astype(BF16) for a in ds]
        dq_ref[...] = jnp.concatenate([_bdot(a, b) for a, b in zip(dss, kw)], axis=-1)
        dk_ref[pl.ds(start, BAND), :] += jnp.concatenate([_bdot_tn(a, b) for a, b in zip(dss, qs)], axis=-1)
        dv_ref[pl.ds(start, BAND), :] += jnp.concatenate([_bdot_tn(a, b) for a, b in zip(p, dos)], axis=-1)
        for h in range(NH):
            db_ref[h] += ds[h]

    qblk = pl.BlockSpec((L, W), lambda b, c: (b * NC + c, 0))
    kblk = pl.BlockSpec((PADSEQ, W), lambda b, c: (b, 0))
    bblk = _const_spec((NH, L, BAND))
    return pl.pallas_call(
        body, grid=(NSEQ, NC), name="attention_bwd",
        in_specs=[qblk, kblk, kblk, bblk, qblk],
        out_specs=[qblk, kblk, kblk, bblk],
        out_shape=[jax.ShapeDtypeStruct((T, W), F32), jax.ShapeDtypeStruct((NSEQ * PADSEQ, W), F32),
                   jax.ShapeDtypeStruct((NSEQ * PADSEQ, W), F32), jax.ShapeDtypeStruct((NH, L, BAND), F32)],
        compiler_params=_cparams(("arbitrary", "arbitrary"), VMEM_BIG),
    )(q, kpad, vpad, bias, do)


def attention_bias(table):
    n = np.arange(BAND + L - 1)
    idx = np.clip(LEFT * L + (L - 1) - n, -CLIP, CLIP) + CLIP
    lo = int(idx.min())
    rev = jnp.flip(table[:, lo:], axis=1)
    n_top = int((idx == 2 * CLIP).sum())
    ext = jnp.concatenate([jnp.broadcast_to(table[:, 2 * CLIP:], (NH, n_top - 1)), rev], axis=1)
    return jnp.stack([ext[:, L - 1 - i:L - 1 - i + BAND] for i in range(L)], axis=1)


def _group_cols(g):
    return slice(g * SGC, (g + 1) * SGC)


def _sg_norm(v, lng, lnb):
    gv = _gelu(v)
    gc = gv - jnp.mean(gv, axis=-1, keepdims=True)
    rstd = lax.rsqrt(jnp.mean(gc * gc, axis=-1, keepdims=True) + LN_EPS)
    xhat = gc * rstd
    return xhat, rstd, xhat * lng + lnb


def gmlp_fwd(u, v, gate, lng, lnb, wm_bf, sgb_t):
    def body(u_ref, v_ref, gt_ref, lng_ref, lnb_ref, wm_ref, sb_ref, z_ref, zt_ref):
        _, _, vln = _sg_norm(v_ref[...], lng_ref[...], lnb_ref[...])
        vlb = vln.astype(BF16)
        for g in range(NG):
            cs = _group_cols(g)
            sv = jnp.dot(wm_ref[g], vlb[:, cs], preferred_element_type=F32) + sb_ref[:, g:g + 1]
            zg = (_gelu(u_ref[:, cs]) * sv * _silu(gt_ref[:, cs])).astype(BF16)
            z_ref[:, cs] = zg
            zt_ref[cs, :] = zg.T

    return pl.pallas_call(
        body, grid=(T // SGC,), name="gmlp_fwd",
        in_specs=[_row_spec(SGC, D)] * 3 + [_const_spec((1, D))] * 2 + [_const_spec((NG, SGC, SGC)),
                                                                      _const_spec((SGC, NG))],
        out_specs=[_row_spec(SGC, D), _col_spec(D, SGC)],
        out_shape=[jax.ShapeDtypeStruct((T, D), BF16), jax.ShapeDtypeStruct((D, T), BF16)],
        compiler_params=_cparams(("parallel",)),
    )(u, v, gate, lng, lnb, wm_bf, sgb_t)


def gmlp_bwd(u, v, gate, lng, lnb, wm_bf, sgb_t, dz):
    def body(u_ref, v_ref, gt_ref, lng_ref, lnb_ref, wm_ref, sb_ref, dz_ref,
             du_ref, dv_ref, dgt_ref, dlng_ref, dlnb_ref, dwm_ref, dsb_ref):
        @pl.when(pl.program_id(0) == 0)
        def _():
            for ref in (dlng_ref, dlnb_ref, dwm_ref, dsb_ref):
                ref[...] = jnp.zeros_like(ref)

        vv = v_ref[...]
        xhat, rstd, vln = _sg_norm(vv, lng_ref[...], lnb_ref[...])
        vlb = vln.astype(BF16)
        dvln = []
        dsv_all = []
        for g in range(NG):
            cs = _group_cols(g)
            uu = u_ref[:, cs]
            gg = gt_ref[:, cs]
            dzz = dz_ref[:, cs]
            sv = jnp.dot(wm_ref[g], vlb[:, cs], preferred_element_type=F32) + sb_ref[:, g:g + 1]
            gu = _gelu(uu)
            sg = _silu(gg)
            dsv = dzz * gu * sg
            dgt_ref[:, cs] = dzz * gu * sv * _dsilu(gg)
            du_ref[:, cs] = dzz * sv * sg * _dgelu(uu)
            dsb16 = dsv.astype(BF16)
            dvln.append(lax.dot_general(wm_ref[g], dsb16, (((0,), (0,)), ((), ())), preferred_element_type=F32))
            dwm_ref[g] += lax.dot_general(dsb16, vlb[:, cs], (((1,), (1,)), ((), ())), preferred_element_type=F32)
            dsv_all.append(dsv)
        dvl = jnp.concatenate(dvln, axis=-1)
        dsv_cat = jnp.concatenate(dsv_all, axis=-1)
        sel = (_iota2((D, NG), 0) // SGC == _iota2((D, NG), 1)).astype(F32)
        dsb_ref[...] += _hdot(dsv_cat, sel)
        dlng_ref[...] += jnp.sum(dvl * xhat, axis=0, keepdims=True)
        dlnb_ref[...] += jnp.sum(dvl, axis=0, keepdims=True)
        dxh = dvl * lng_ref[...]
        dgv = rstd * (dxh - jnp.mean(dxh, axis=-1, keepdims=True)
                      - xhat * jnp.mean(dxh * xhat, axis=-1, keepdims=True))
        dv_ref[...] = dgv * _dgelu(vv)

    return pl.pallas_call(
        body, grid=(T // SGC,), name="gmlp_bwd",
        in_specs=[_row_spec(SGC, D)] * 3 + [_const_spec((1, D))] * 2
        + [_const_spec((NG, SGC, SGC)), _const_spec((SGC, NG)), _row_spec(SGC, D)],
        out_specs=[_row_spec(SGC, D)] * 3 + [_const_spec((1, D))] * 2 + [_const_spec((NG, SGC, SGC)),
                                                                       _const_spec((SGC, NG))],
        out_shape=[jax.ShapeDtypeStruct((T, D), F32)] * 3 + [jax.ShapeDtypeStruct((1, D), F32)] * 2
        + [jax.ShapeDtypeStruct((NG, SGC, SGC), F32), jax.ShapeDtypeStruct((SGC, NG), F32)],
        compiler_params=_cparams(("arbitrary",)),
    )(u, v, gate, lng, lnb, wm_bf, sgb_t, dz)


NCHIP = 4
NDEV = 8
ANY = pl.BlockSpec(memory_space=pl.ANY)


def exchange_xy(arrs, name):
    n = len(arrs)

    def body(*refs):
        ins, outs = refs[:n], refs[n:2 * n]
        send, recv, loc = refs[2 * n:]
        x, y, c = lax.axis_index("x"), lax.axis_index("y"), lax.axis_index("c")
        me = 2 * x + y
        peers = [(1 - x, y), (x, 1 - y), (1 - x, 1 - y)]
        local = [pltpu.make_async_copy(ins[t].at[me], outs[t].at[me], loc.at[t]) for t in range(n)]
        for cp in local:
            cp.start()
        remote = []
        for t in range(n):
            for j, (px, py) in enumerate(peers):
                remote.append((pltpu.make_async_remote_copy(
                    src_ref=ins[t].at[2 * px + py], dst_ref=outs[t].at[me], send_sem=send.at[t, j],
                    recv_sem=recv.at[t, j], device_id=(px, py, c), device_id_type=MESH), t, j, 2 * px + py))
        for cp, _, _, _ in remote:
            cp.start()
        for cp, t, j, chip in remote:
            cp.wait_send()
            pltpu.make_async_remote_copy(
                src_ref=ins[t].at[chip], dst_ref=outs[t].at[chip], send_sem=send.at[t, j], recv_sem=recv.at[t, j],
                device_id=(x, y, c), device_id_type=MESH).wait_recv()
        for cp in local:
            cp.wait()

    return pl.pallas_call(
        body, name=name, in_specs=[ANY] * n, out_specs=[ANY] * n,
        out_shape=[jax.ShapeDtypeStruct(a.shape, a.dtype) for a in arrs],
        scratch_shapes=[pltpu.SemaphoreType.DMA((n, 3)), pltpu.SemaphoreType.DMA((n, 3)), pltpu.SemaphoreType.DMA((n,))],
    )(*arrs)


def exchange_c(arrs, name):
    n = len(arrs)

    def body(*refs):
        ins, outs = refs[:n], refs[n:2 * n]
        send, recv = refs[2 * n:]
        sibling = (lax.axis_index("x"), lax.axis_index("y"), 1 - lax.axis_index("c"))
        copies = [pltpu.make_async_remote_copy(src_ref=ins[t], dst_ref=outs[t], send_sem=send.at[t], recv_sem=recv.at[t],
                                               device_id=sibling, device_id_type=MESH) for t in range(n)]
        for cp in copies:
            cp.start()
        for cp in copies:
            cp.wait()

    return pl.pallas_call(
        body, name=name, in_specs=[ANY] * n, out_specs=[ANY] * n,
        out_shape=[jax.ShapeDtypeStruct(a.shape, a.dtype) for a in arrs],
        scratch_shapes=[pltpu.SemaphoreType.DMA((n,)), pltpu.SemaphoreType.DMA((n,))],
    )(*arrs)


def gather_weights(arrs, split):
    n = len(arrs)

    def body(*refs):
        ins, outs = refs[:n], refs[n:2 * n]
        send1, recv1, send2, recv2, loc = refs[2 * n:]
        x, y, c = lax.axis_index("x"), lax.axis_index("y"), lax.axis_index("c")
        me = 2 * x + y
        sibling = (x, y, 1 - c)
        peers = [(1 - x, y), (x, 1 - y), (1 - x, 1 - y)]

        def rows_of(t, core):
            half = arrs[t].shape[0] // 2
            return pl.ds(core * half, half)

        def part(ref, t, core):
            return ref.at[rows_of(t, core)] if split[t] else ref

        local = [pltpu.make_async_copy(ins[t], outs[t].at[me], loc.at[t]) for t in range(n)]
        for cp in local:
            cp.start()
        first = []
        for t in range(n):
            for j, (px, py) in enumerate(peers):
                first.append(pltpu.make_async_remote_copy(
                    src_ref=part(ins[t], t, c), dst_ref=part(outs[t].at[me], t, c), send_sem=send1.at[t, j],
                    recv_sem=recv1.at[t, j], device_id=(px, py, c), device_id_type=MESH))
        for cp in first:
            cp.start()
        passed = []
        for t in range(n):
            for j, (px, py) in enumerate(peers):
                landed = part(outs[t].at[2 * px + py], t, c)
                pltpu.make_async_remote_copy(
                    src_ref=landed, dst_ref=landed, send_sem=send1.at[t, j], recv_sem=recv1.at[t, j],
                    device_id=(x, y, c), device_id_type=MESH).wait_recv()
                if split[t]:
                    cp = pltpu.make_async_remote_copy(
                        src_ref=landed, dst_ref=landed, send_sem=send2.at[t, j], recv_sem=recv2.at[t, j],
                        device_id=sibling, device_id_type=MESH)
                    cp.start()
                    passed.append(cp)
        for t in range(n):
            for j, (px, py) in enumerate(peers):
                if split[t]:
                    other = part(outs[t].at[2 * px + py], t, 1 - c)
                    pltpu.make_async_remote_copy(
                        src_ref=other, dst_ref=other, send_sem=send2.at[t, j], recv_sem=recv2.at[t, j],
                        device_id=(x, y, c), device_id_type=MESH).wait_recv()
        for cp in first + passed:
            cp.wait_send()
        for cp in local:
            cp.wait()

    return pl.pallas_call(
        body, name="gather_weights", in_specs=[ANY] * n, out_specs=[ANY] * n,
        out_shape=[jax.ShapeDtypeStruct((NCHIP,) + a.shape, a.dtype) for a in arrs],
        scratch_shapes=[pltpu.SemaphoreType.DMA((n, 3))] * 4 + [pltpu.SemaphoreType.DMA((n,))],
    )(*arrs)


def allgather_all(buf, name):
    def body(in_ref, out_ref, send, recv, loc):
        x, y, c = lax.axis_index("x"), lax.axis_index("y"), lax.axis_index("c")
        me = 4 * x + 2 * y + c
        mine = pltpu.make_async_copy(in_ref, out_ref.at[me], loc)
        mine.start()
        copies = []
        for j in range(1, NDEV):
            px, py, pc = x ^ (j >> 2), y ^ ((j >> 1) & 1), c ^ (j & 1)
            copies.append((pltpu.make_async_remote_copy(
                src_ref=in_ref, dst_ref=out_ref.at[me], send_sem=send.at[j], recv_sem=recv.at[j],
                device_id=(px, py, pc), device_id_type=MESH), j, 4 * px + 2 * py + pc))
        for cp, _, _ in copies:
            cp.start()
        for cp, j, peer in copies:
            cp.wait_send()
            pltpu.make_async_remote_copy(
                src_ref=in_ref, dst_ref=out_ref.at[peer], send_sem=send.at[j], recv_sem=recv.at[j],
                device_id=(x, y, c), device_id_type=MESH).wait_recv()
        mine.wait()

    return pl.pallas_call(
        body, name=name, in_specs=[ANY], out_specs=ANY,
        out_shape=jax.ShapeDtypeStruct((NDEV,) + buf.shape, buf.dtype),
        scratch_shapes=[pltpu.SemaphoreType.DMA((NDEV,)), pltpu.SemaphoreType.DMA((NDEV,)), pltpu.SemaphoreType.DMA],
    )(buf)


def _adam_math(g, w, m, v):
    m = ADAM_B1 * m + (1.0 - ADAM_B1) * g
    v = ADAM_B2 * v + (1.0 - ADAM_B2) * (g * g)
    m_hat = m / (1.0 - ADAM_B1 ** ADAM_STEP)
    v_hat = v / (1.0 - ADAM_B2 ** ADAM_STEP)
    delta = -ADAM_LR * (m_hat / (jnp.sqrt(v_hat) + ADAM_EPS) + ADAM_WD * w)
    return delta, m, v


def _rows_tile(rows):
    return rows if rows <= 256 else 256


def sum_chips(parts, name):
    _, rows, cols = parts.shape
    tr = _rows_tile(rows)

    def body(p_ref, o_ref):
        p = [p_ref[s].astype(F32) for s in range(NCHIP)]
        o_ref[...] = ((p[0] + p[1]) + p[2]) + p[3]

    return pl.pallas_call(
        body, grid=(rows // tr,), name=name,
        in_specs=[pl.BlockSpec((NCHIP, tr, cols), lambda i: (0, i, 0))],
        out_specs=pl.BlockSpec((tr, cols), lambda i: (i, 0)),
        out_shape=jax.ShapeDtypeStruct((rows, cols), F32),
        compiler_params=_cparams(("parallel",)),
    )(parts)


def adam_shard(p_mine, p_sib, w, m, v, name):
    rows, cols = w.shape
    tr = _rows_tile(rows)

    def body(a_ref, b_ref, w_ref, m_ref, v_ref, g_ref, d_ref, mo_ref, vo_ref):
        g = a_ref[...] + b_ref[...]
        g_ref[...] = g
        d_ref[...], mo_ref[...], vo_ref[...] = _adam_math(g, w_ref[...], m_ref[...], v_ref[...])

    spec = pl.BlockSpec((tr, cols), lambda i: (i, 0))
    return pl.pallas_call(
        body, grid=(rows // tr,), name=name, in_specs=[spec] * 5, out_specs=[spec] * 4,
        out_shape=[jax.ShapeDtypeStruct((rows, cols), F32)] * 4,
        compiler_params=_cparams(("parallel",)),
    )(p_mine, p_sib, w, m, v)


def adam_replicated(parts, w, m, v):
    rows = w.shape[0]

    def body(p_ref, w_ref, m_ref, v_ref, g_ref, d_ref, mo_ref, vo_ref):
        g = p_ref[0]
        for d in range(1, NDEV):
            g = g + p_ref[d]
        g_ref[...] = g
        d_ref[...], mo_ref[...], vo_ref[...] = _adam_math(g, w_ref[...], m_ref[...], v_ref[...])

    return pl.pallas_call(
        body, name="adam_replicated", out_shape=[jax.ShapeDtypeStruct((rows, 128), F32)] * 4,
    )(parts, w, m, v)


def _pack(arrs):
    pieces = []
    for a in arrs:
        flat = a.reshape(-1)
        pad = (-flat.shape[0]) % 128
        pieces.append(jnp.pad(flat, (0, pad)) if pad else flat)
    flat = jnp.concatenate(pieces)
    pad = (-flat.shape[0]) % 1024
    return jnp.pad(flat, (0, pad)).reshape(-1, 128)


def _unpack(buf, shapes):
    flat = buf.reshape(-1)
    out = []
    o = 0
    for s in shapes:
        n = int(np.prod(s))
        out.append(flat[o:o + n].reshape(s))
        o += n + (-n) % 128
    return out


EVEN_SPLITS = (SHIFT, W, W, W, W, W)
ODD_SPLITS = (D, D, D)


def _cols_to_chips(a):
    rows, cols = a.shape
    return a.reshape(rows, NCHIP, cols // NCHIP).transpose(1, 0, 2)


def _chips_to_cols(a):
    _, rows, n = a.shape
    return a.transpose(1, 0, 2).reshape(rows, NCHIP * n)


def kernel(x, norm_g, w_in_e, shift_mu, rw_w0, rw_w2, rw_a0, rw_a2, rw_kk, rw_ka, rw_rk, rw_lnx_g, rw_lnx_b, att_bias, w_out_e, w_in_o, sg_ln_g, sg_ln_b, sg_w, sg_b, w_out_o, final_g, loss_target, m_norm_g, m_w_in_e, m_shift_mu, m_rw_w0, m_rw_w2, m_rw_a0, m_rw_a2, m_rw_kk, m_rw_ka, m_rw_rk, m_rw_lnx_g, m_rw_lnx_b, m_att_bias, m_w_out_e, m_w_in_o, m_sg_ln_g, m_sg_ln_b, m_sg_w, m_sg_b, m_w_out_o, m_final_g, v_norm_g, v_w_in_e, v_shift_mu, v_rw_w0, v_rw_w2, v_rw_a0, v_rw_a2, v_rw_kk, v_rw_ka, v_rw_rk, v_rw_lnx_g, v_rw_lnx_b, v_att_bias, v_w_out_e, v_w_in_o, v_sg_ln_g, v_sg_ln_b, v_sg_w, v_sg_b, v_w_out_o, v_final_g):
    x2 = x.reshape(T, D)
    tgt = loss_target.reshape(T, D)

    gathered = gather_weights(
        [w_in_e[0].astype(BF16), w_out_e[0].astype(BF16), w_in_o[0].astype(BF16), w_out_o[0].astype(BF16),
         jnp.concatenate([rw_w2[0], rw_a2[0]], axis=0), jnp.concatenate([sg_ln_g, sg_ln_b], axis=0)],
        [True, True, True, True, True, False])
    wie = _chips_to_cols(gathered[0])
    woe = gathered[1].reshape(D, D)
    wio = _chips_to_cols(gathered[2])
    woo = gathered[3].reshape(D, D)
    w2 = _chips_to_cols(gathered[4][:, :LORA])
    a2 = _chips_to_cols(gathered[4][:, LORA:])
    sglg = _chips_to_cols(gathered[5][:, 0:1])
    sglb = _chips_to_cols(gathered[5][:, 1:2])
    loss_part, dx, big_g, rep_g = _local_step(
        x2, tgt, wie, woe, wio, woo, w2, a2, sglg, sglb, norm_g, shift_mu, rw_w0, rw_a0, rw_kk, rw_ka, rw_rk,
        rw_lnx_g, rw_lnx_b, att_bias, sg_w, sg_b, final_g)
    loss = lax.psum(loss_part, ("x", "y", "c"))
    d_wie, d_woe, d_wio, d_woo, d_w2, d_a2, d_sglg, d_sglb = big_g

    local = [_cols_to_chips(d_wie), d_woe.reshape(NCHIP, D // NCHIP, D), _cols_to_chips(d_wio),
             d_woo.reshape(NCHIP, D // NCHIP, D), _cols_to_chips(d_w2), _cols_to_chips(d_a2),
             _cols_to_chips(d_sglg), _cols_to_chips(d_sglb)]
    names = ["w_in_e", "w_out_e", "w_in_o", "w_out_o", "rw_w2", "rw_a2", "sg_ln_g", "sg_ln_b"]
    local = [a.astype(BF16) for a in local[:4]] + local[4:]
    received = exchange_xy(local, "scatter_grads")
    partial = [sum_chips(p, "sum_" + nm) for p, nm in zip(received, names)]
    from_sibling = exchange_c(partial, "swap_partials")
    sharded = {}
    wmv = {"w_in_e": (w_in_e[0], m_w_in_e[0], v_w_in_e[0]), "w_out_e": (w_out_e[0], m_w_out_e[0], v_w_out_e[0]),
           "w_in_o": (w_in_o[0], m_w_in_o[0], v_w_in_o[0]), "w_out_o": (w_out_o[0], m_w_out_o[0], v_w_out_o[0]),
           "rw_w2": (rw_w2[0], m_rw_w2[0], v_rw_w2[0]), "rw_a2": (rw_a2[0], m_rw_a2[0], v_rw_a2[0]),
           "sg_ln_g": (sg_ln_g, m_sg_ln_g, v_sg_ln_g), "sg_ln_b": (sg_ln_b, m_sg_ln_b, v_sg_ln_b)}
    for nm, mine, sib in zip(names, partial, from_sibling):
        w_, m_, v_ = wmv[nm]
        res = adam_shard(mine, sib, w_, m_, v_, "adam_" + nm)
        lead = nm not in ("sg_ln_g", "sg_ln_b")
        sharded[nm] = [a[None] if lead else a for a in res]

    rep_names = ["norm_g", "shift_mu", "rw_w0", "rw_a0", "rw_kk", "rw_ka", "rw_rk", "rw_lnx_g", "rw_lnx_b",
                 "att_bias", "sg_w", "sg_b", "final_g"]
    rep_w = [norm_g, shift_mu, rw_w0, rw_a0, rw_kk, rw_ka, rw_rk, rw_lnx_g, rw_lnx_b, att_bias, sg_w, sg_b, final_g]
    rep_m = [m_norm_g, m_shift_mu, m_rw_w0, m_rw_a0, m_rw_kk, m_rw_ka, m_rw_rk, m_rw_lnx_g, m_rw_lnx_b, m_att_bias,
             m_sg_w, m_sg_b, m_final_g]
    rep_v = [v_norm_g, v_shift_mu, v_rw_w0, v_rw_a0, v_rw_kk, v_rw_ka, v_rw_rk, v_rw_lnx_g, v_rw_lnx_b, v_att_bias,
             v_sg_w, v_sg_b, v_final_g]
    shapes = [w_.shape for w_ in rep_w]
    gathered_g = allgather_all(_pack(rep_g), "gather_small_grads")
    rep_out = adam_replicated(gathered_g, _pack(rep_w), _pack(rep_m), _pack(rep_v))
    rep = {nm: [] for nm in rep_names}
    for buf in rep_out:
        for nm, a in zip(rep_names, _unpack(buf, shapes)):
            rep[nm].append(a)

    order = ["norm_g", "w_in_e", "shift_mu", "rw_w0", "rw_w2", "rw_a0", "rw_a2", "rw_kk", "rw_ka", "rw_rk",
             "rw_lnx_g", "rw_lnx_b", "att_bias", "w_out_e", "w_in_o", "sg_ln_g", "sg_ln_b", "sg_w", "sg_b",
             "w_out_o", "final_g"]
    results = {**sharded, **rep}
    outs = [loss, dx.reshape(NSEQ, SEQ, D)]
    for kind in range(4):
        outs += [results[nm][kind] for nm in order]
    return tuple(outs)


def _local_step(x2, tgt, wie, woe, wio, woo, w2, a2, sglg, sglb, norm_g, shift_mu, rw_w0, rw_a0, rw_kk, rw_ka, rw_rk,
                rw_lnx_g, rw_lnx_b, att_bias, sg_w, sg_b, final_g):
    zl = jnp.zeros((LORA, W), F32)
    w2x = jnp.concatenate([w2, zl], axis=0)
    a2x = jnp.concatenate([zl, a2], axis=0)
    rk = rw_rk.reshape(1, W)
    pos = np.arange(SGC)
    sg_mask = jnp.asarray(((pos[None, :] // L) <= (pos[:, None] // L)).astype(np.float32))
    wm = (sg_w[0] * sg_mask[None]).astype(BF16)
    sgb_t = sg_b[0].T

    xn0, ps, ga, q, kb, vb, gb = ln_in_proj(x2, norm_g[0:1], wie, EVEN_SPLITS, "in_proj_even")
    r, lw, k2, v, aa, bb = even_prep(ps, shift_mu, rw_w0, w2x, rw_a0, a2x, rw_kk, rw_ka)
    y, hs, ms, ts = rwkv_fwd(r, lw, k2, v, aa, bb)
    bias, bias_vjp = jax.vjp(attention_bias, att_bias[0])

    def padded(a):
        return jnp.pad(a.astype(BF16).reshape(NSEQ, SEQ, W), ((0, 0), (LEFT * L, 0), (0, 0))).reshape(NSEQ * PADSEQ, W)

    kpad, vpad = padded(kb), padded(vb)
    o = attention_fwd(q, kpad, vpad, bias)
    z, zt = even_post(y, r, k2, v, ga, o, gb, rw_lnx_g, rw_lnx_b, rk)
    h1 = out_proj(x2, z, woe, "out_proj_even")
    xn1, u, vv, gt = ln_in_proj(h1, norm_g[1:2], wio, ODD_SPLITS, "in_proj_odd")
    z2, z2t = gmlp_fwd(u, vv, gt, sglg, sglb, wm, sgb_t)
    h2 = out_proj(h1, z2, woo, "out_proj_odd")
    dh2, loss_part, d_final_g = final_loss(h2, final_g[None], tgt)

    dz2, d_woo = out_proj_bwd(dh2, z2t, woo, "out_proj_odd_bwd")
    du, dvv, dgt, d_sglg, d_sglb, d_wm, d_sgb_t = gmlp_bwd(u, vv, gt, sglg, sglb, wm, sgb_t, dz2)
    dh1, d_g1, dp2 = in_proj_bwd_x(h1, norm_g[1:2], wio, [du, dvv, dgt], dh2, "in_proj_odd_bwd")
    d_wio = matmul_acc(xn1, dp2, 1024, "in_proj_odd_dw")
    dz, d_woe = out_proj_bwd(dh1, zt, woe, "out_proj_even_bwd")
    dy, dr2, dk22, dv2, dga, do, dgb, d_lng, d_lnb, d_rk = even_post_bwd(
        y, r, k2, v, ga, o, gb, rw_lnx_g, rw_lnx_b, rk, dz)
    dq, dkpad, dvpad, dbias = attention_bwd(q, kpad, vpad, bias, do)

    def unpadded(a):
        return a.reshape(NSEQ, PADSEQ, W)[:, LEFT * L:].reshape(T, W)

    (d_att_bias,) = bias_vjp(dbias)
    dr, dlw, dk2, dv, daa, dbb = rwkv_bwd(r, lw, k2, v, aa, bb, hs, ms, ts, dy)
    dps, d_mu, d_w0, d_w2x, d_a0, d_a2x, d_kk, d_ka = even_prep_bwd(
        ps, shift_mu, rw_w0, w2x, rw_a0, a2x, rw_kk, rw_ka, dr, dlw, dk2, dv, daa, dbb, dr2, dk22, dv2)
    dx, d_g0, dp = in_proj_bwd_x(x2, norm_g[0:1], wie, [dps, dga, dq, unpadded(dkpad), unpadded(dvpad), dgb], dh1,
                                 "in_proj_even_bwd")
    d_wie = matmul_acc(xn0, dp, 1408, "in_proj_even_dw")

    big_g = (d_wie, d_woe, d_wio, d_woo, d_w2x[:LORA], d_a2x[LORA:], d_sglg, d_sglb)
    rep_g = [jnp.concatenate([d_g0, d_g1], axis=0), d_mu, d_w0, d_a0, d_kk, d_ka, d_rk, d_lng, d_lnb, d_att_bias,
             d_wm * sg_mask[None], d_sgb_t.T, d_final_g]
    return loss_part[0, 0], dx, big_g, rep_g
```

```python
import functools
import math

import jax
import jax.numpy as jnp
import numpy as np
from jax import lax
from jax.experimental import pallas as pl
from jax.experimental.pallas import tpu as pltpu

F32 = jnp.float32
BF16 = jnp.bfloat16
HI = lax.Precision.HIGHEST

D = 1024
SEQ = 2048
NSEQ = 2
T = NSEQ * SEQ
HD = 64
NH = 8
W = 512
SHIFT = 1664
LORA = 64
EVEN_IN = 4224
ODD_IN = 3072
L = 64
NC = SEQ // L
LEFT = 8
BAND = (LEFT + 1) * L
CLIP = 128
SGC = 128
NG = 8
RMS_EPS = 1e-6
LN_EPS = 1e-5
GN_EPS = 64e-5
NEG = -1e30
VMEM_BIG = 56 * 1024 * 1024

ADAM_LR = 0.001
ADAM_B1 = 0.9
ADAM_B2 = 0.999
ADAM_EPS = 1e-08
ADAM_WD = 0.01
ADAM_STEP = 10

MESH = pl.DeviceIdType.MESH


def _bdot(a, b):
    return jnp.dot(a.astype(BF16), b.astype(BF16), preferred_element_type=F32)


def _bdot_nt(a, b):
    return lax.dot_general(a.astype(BF16), b.astype(BF16), (((1,), (1,)), ((), ())), preferred_element_type=F32)


def _bdot_tn(a, b):
    return lax.dot_general(a.astype(BF16), b.astype(BF16), (((0,), (0,)), ((), ())), preferred_element_type=F32)


def _hdot(a, b):
    return jnp.dot(a, b, precision=HI, preferred_element_type=F32)


def _hdot_nt(a, b):
    return lax.dot_general(a, b, (((1,), (1,)), ((), ())), precision=HI, preferred_element_type=F32)


def _hdot_tn(a, b):
    return lax.dot_general(a, b, (((0,), (0,)), ((), ())), precision=HI, preferred_element_type=F32)


def _iota2(shape, dim):
    return lax.broadcasted_iota(jnp.int32, shape, dim)


def _head_blockdiag():
    r = _iota2((W, W), 0) // HD
    c = _iota2((W, W), 1) // HD
    return (r == c).astype(BF16)


def _headsum_impl(x, bd):
    hi = x.astype(BF16)
    mid = (x - hi.astype(F32)).astype(BF16)
    return jnp.dot(hi, bd, preferred_element_type=F32) + jnp.dot(mid, bd, preferred_element_type=F32)


@jax.custom_vjp
def _headsum(x, bd):
    return _headsum_impl(x, bd)


def _headsum_fwd(x, bd):
    return _headsum_impl(x, bd), bd


def _headsum_bwd(bd, ct):
    return _headsum_impl(ct, bd), None


_headsum.defvjp(_headsum_fwd, _headsum_bwd)


def _silu(x):
    return x * jax.nn.sigmoid(x)


def _dsilu(x):
    s = jax.nn.sigmoid(x)
    return s * (1.0 + x * (1.0 - s))


_GELU_C = math.sqrt(2.0 / math.pi)


def _gelu(x):
    return 0.5 * x * (1.0 + jnp.tanh(_GELU_C * (x + 0.044715 * (x * x * x))))


def _dgelu(x):
    t = jnp.tanh(_GELU_C * (x + 0.044715 * (x * x * x)))
    return 0.5 * (1.0 + t) + 0.5 * x * (1.0 - t * t) * _GELU_C * (1.0 + 3.0 * 0.044715 * x * x)


def _softplus(x):
    return jnp.maximum(x, 0.0) + jnp.log(1.0 + jnp.exp(-jnp.abs(x)))


def _cparams(sem, vmem=None):
    return pltpu.CompilerParams(dimension_semantics=sem, vmem_limit_bytes=vmem)


def _row_spec(tm, width):
    return pl.BlockSpec((tm, width), lambda i: (i, 0))


def _col_spec(height, tm):
    return pl.BlockSpec((height, tm), lambda i: (0, i))


def _const_spec(shape):
    nd = len(shape)
    return pl.BlockSpec(shape, lambda *_: (0,) * nd)


def ln_in_proj(x, g, w_bf, splits, name, after=None):
    n = w_bf.shape[1]
    tm = 256
    spans = []
    o = 0
    for s in splits:
        spans.append((o, o + s))
        o += s
    assert o == n
    extra_specs, extra = _after_operand(after)

    def body(x_ref, g_ref, w_ref, *rest):
        xn_ref, outs = rest[len(extra)], rest[len(extra) + 1:]
        xv = x_ref[...]
        rstd = lax.rsqrt(jnp.mean(xv * xv, axis=-1, keepdims=True) + RMS_EPS)
        xn = (xv * rstd * g_ref[...]).astype(BF16)
        xn_ref[...] = xn.T
        p = jnp.dot(xn, w_ref[...], preferred_element_type=F32)
        for o_ref, (a, b) in zip(outs, spans):
            o_ref[...] = p[:, a:b]

    return pl.pallas_call(
        body, grid=(T // tm,), name=name,
        in_specs=[_row_spec(tm, D), _const_spec((1, D)), _const_spec((D, n))] + extra_specs,
        out_specs=[_col_spec(D, tm)] + [_row_spec(tm, s) for s in splits],
        out_shape=[jax.ShapeDtypeStruct((D, T), BF16)] + [jax.ShapeDtypeStruct((T, s), F32) for s in splits],
        compiler_params=_cparams(("parallel",), VMEM_BIG),
    )(x, g, w_bf, *extra)


def in_proj_bwd_x(x, g, w_bf, dps, dres, name):
    n = w_bf.shape[1]
    tm = 256
    widths = [d.shape[1] for d in dps]

    def body(x_ref, g_ref, w_ref, dres_ref, *rest):
        dp_refs = rest[:len(widths)]
        dx_ref, dg_ref, dpc_ref = rest[len(widths):]
        dp = jnp.concatenate([r[...].astype(BF16) for r in dp_refs], axis=-1)
        dpc_ref[...] = dp
        dxn = lax.dot_general(dp, w_ref[...], (((1,), (1,)), ((), ())), preferred_element_type=F32)
        xv = x_ref[...]
        rstd = lax.rsqrt(jnp.mean(xv * xv, axis=-1, keepdims=True) + RMS_EPS)
        xhat = xv * rstd
        dgp = jnp.sum(dxn * xhat, axis=0, keepdims=True)

        @pl.when(pl.program_id(0) == 0)
        def _():
            dg_ref[...] = jnp.zeros_like(dg_ref)

        dg_ref[...] += dgp
        dxh = dxn * g_ref[...]
        dx_ref[...] = dres_ref[...] + rstd * (dxh - xhat * jnp.mean(dxh * xhat, axis=-1, keepdims=True))

    return pl.pallas_call(
        body, grid=(T // tm,), name=name,
        in_specs=[_row_spec(tm, D), _const_spec((1, D)), _const_spec((D, n)), _row_spec(tm, D)]
        + [_row_spec(tm, s) for s in widths],
        out_specs=[_row_spec(tm, D), _const_spec((1, D)), _row_spec(tm, n)],
        out_shape=[jax.ShapeDtypeStruct((T, D), F32), jax.ShapeDtypeStruct((1, D), F32),
                   jax.ShapeDtypeStruct((T, n), BF16)],
        compiler_params=_cparams(("arbitrary",), VMEM_BIG),
    )(x, g, w_bf, dres, *dps)


def _after_operand(after):
    return ([ANY], [after]) if after is not None else ([], [])


def matmul_acc(at_bf, b_bf, tn, name, after=None):
    k = at_bf.shape[0]
    n = b_bf.shape[1]
    tm = 512
    nt = T // tm
    extra_specs, extra = _after_operand(after)

    def body(a_ref, b_ref, *rest):
        o_ref = rest[-1]

        @pl.when(pl.program_id(1) == 0)
        def _():
            o_ref[...] = jnp.zeros_like(o_ref)

        o_ref[...] += jnp.dot(a_ref[...], b_ref[...], preferred_element_type=F32)

    return pl.pallas_call(
        body, grid=(n // tn, nt), name=name,
        in_specs=[pl.BlockSpec((k, tm), lambda j, i: (0, i)), pl.BlockSpec((tm, tn), lambda j, i: (i, j))]
        + extra_specs,
        out_specs=pl.BlockSpec((k, tn), lambda j, i: (0, j)),
        out_shape=jax.ShapeDtypeStruct((k, n), F32),
        compiler_params=_cparams(("parallel", "arbitrary"), VMEM_BIG),
    )(at_bf, b_bf, *extra)


def out_proj(h, z_bf, w_bf, name):
    tm = 256

    def body(h_ref, z_ref, w_ref, o_ref):
        o_ref[...] = h_ref[...] + jnp.dot(z_ref[...], w_ref[...], preferred_element_type=F32)

    return pl.pallas_call(
        body, grid=(T // tm,), name=name,
        in_specs=[_row_spec(tm, D), _row_spec(tm, D), _const_spec((D, D))],
        out_specs=_row_spec(tm, D), out_shape=jax.ShapeDtypeStruct((T, D), F32),
        compiler_params=_cparams(("parallel",)),
    )(h, z_bf, w_bf)


def out_proj_bwd(dh, zt_bf, w_bf, name, after=None):
    tm = 256
    extra_specs, extra = _after_operand(after)

    def body(dh_ref, zt_ref, w_ref, *rest):
        dz_ref, dw_ref = rest[-2:]
        dhb = dh_ref[...].astype(BF16)
        dz_ref[...] = lax.dot_general(dhb, w_ref[...], (((1,), (1,)), ((), ())), preferred_element_type=F32)

        @pl.when(pl.program_id(0) == 0)
        def _():
            dw_ref[...] = jnp.zeros_like(dw_ref)

        dw_ref[...] += jnp.dot(zt_ref[...], dhb, preferred_element_type=F32)

    return pl.pallas_call(
        body, grid=(T // tm,), name=name,
        in_specs=[_row_spec(tm, D), _col_spec(D, tm), _const_spec((D, D))] + extra_specs,
        out_specs=[_row_spec(tm, D), _const_spec((D, D))],
        out_shape=[jax.ShapeDtypeStruct((T, D), F32), jax.ShapeDtypeStruct((D, D), F32)],
        compiler_params=_cparams(("arbitrary",)),
    )(dh, zt_bf, w_bf, *extra)


def final_loss(h, g, target):
    tm = 256

    def body(h_ref, g_ref, t_ref, dh_ref, loss_ref, dg_ref):
        xv = h_ref[...]
        rstd = lax.rsqrt(jnp.mean(xv * xv, axis=-1, keepdims=True) + RMS_EPS)
        xhat = xv * rstd
        err = xhat * g_ref[...] - t_ref[...]
        part = 0.5 * jnp.sum(jnp.mean(err * err, axis=-1, keepdims=True), axis=0, keepdims=True)
        dout = err * (1.0 / D)

        @pl.when(pl.program_id(0) == 0)
        def _():
            loss_ref[...] = jnp.zeros_like(loss_ref)
            dg_ref[...] = jnp.zeros_like(dg_ref)

        loss_ref[...] += jnp.broadcast_to(part, loss_ref.shape)
        dg_ref[...] += jnp.sum(dout * xhat, axis=0, keepdims=True)
        dxh = dout * g_ref[...]
        dh_ref[...] = rstd * (dxh - xhat * jnp.mean(dxh * xhat, axis=-1, keepdims=True))

    return pl.pallas_call(
        body, grid=(T // tm,), name="final_loss",
        in_specs=[_row_spec(tm, D), _const_spec((1, D)), _row_spec(tm, D)],
        out_specs=[_row_spec(tm, D), _const_spec((8, 128)), _const_spec((1, D))],
        out_shape=[jax.ShapeDtypeStruct((T, D), F32), jax.ShapeDtypeStruct((8, 128), F32),
                   jax.ShapeDtypeStruct((1, D), F32)],
        compiler_params=_cparams(("arbitrary",)),
    )(h, g, target)


PREP_TM = 256
PREP_NB = SEQ // PREP_TM


def _prep_elem(k, wl, apre, kkw, kaw, bd):
    wraw = -_softplus(-wl) - 0.5
    lw = -jnp.exp(wraw)
    asig = jax.nn.sigmoid(apre)
    kkr = k * kkw
    nrm = jnp.maximum(jnp.sqrt(_headsum(kkr * kkr, bd)), 1e-12)
    kk = kkr / nrm
    k2 = k * (1.0 + (asig - 1.0) * kaw)
    return lw, k2, -kk, kk * asig


def _shifted(ps_ref, prev_ref, mu, blk):
    p = ps_ref[...]
    first = (blk % PREP_NB) == 0
    prev_row = jnp.where(first, 0.0, prev_ref[7:8, :])
    rolled = pltpu.roll(p, 1, 0)
    p_prev = jnp.where(_iota2(p.shape, 0) == 0, prev_row, rolled)
    return p, p_prev, p + (p_prev - p) * mu


def _prev_spec(width, blk_of):
    return pl.BlockSpec((8, width), lambda i: (jnp.maximum(blk_of(i) * (PREP_TM // 8) - 1, 0), 0))


def even_prep(ps, mu, w0, w2x, a0, a2x, kkw, kaw):
    tm = PREP_TM

    def body(ps_ref, prev_ref, mu_ref, w0_ref, w2_ref, a0_ref, a2_ref, kk_ref, ka_ref,
             r_ref, lw_ref, k2_ref, v_ref, aa_ref, bb_ref):
        _, _, s = _shifted(ps_ref, prev_ref, mu_ref[...], pl.program_id(0))
        wa = s[:, 3 * W:]
        wl = w0_ref[...] + _bdot(jnp.tanh(wa), w2_ref[...])
        apre = a0_ref[...] + _bdot(wa, a2_ref[...])
        lw, k2, aa, bb = _prep_elem(s[:, W:2 * W], wl, apre, kk_ref[...], ka_ref[...], _head_blockdiag())
        r_ref[...] = s[:, 0:W]
        v_ref[...] = s[:, 2 * W:3 * W]
        lw_ref[...] = lw
        k2_ref[...] = k2
        aa_ref[...] = aa
        bb_ref[...] = bb

    vec = _const_spec((1, W))
    return pl.pallas_call(
        body, grid=(T // tm,), name="even_prep",
        in_specs=[_row_spec(tm, SHIFT), _prev_spec(SHIFT, lambda i: i), _const_spec((1, SHIFT)), vec,
                  _const_spec((2 * LORA, W)), vec, _const_spec((2 * LORA, W)), vec, vec],
        out_specs=[_row_spec(tm, W)] * 6,
        out_shape=[jax.ShapeDtypeStruct((T, W), F32)] * 6,
        compiler_params=_cparams(("parallel",)),
    )(ps, ps, mu, w0, w2x, a0, a2x, kkw, kaw)


def even_prep_bwd(ps, mu, w0, w2x, a0, a2x, kkw, kaw, dr, dlw, dk2, dv, daa, dbb, dr2, dk22, dv2):
    tm = PREP_TM
    nb = T // tm
    rev = lambda i: nb - 1 - i

    def body(ps_ref, prev_ref, mu_ref, w0_ref, w2_ref, a0_ref, a2_ref, kk_ref, ka_ref,
             dr_ref, dlw_ref, dk2_ref, dv_ref, daa_ref, dbb_ref, dr2_ref, dk22_ref, dv2_ref,
             dps_ref, dmu_ref, dw0_ref, dw2_ref, da0_ref, da2_ref, dkk_ref, dka_ref, carry):
        i = pl.program_id(0)
        blk = rev(i)
        mu_v = mu_ref[...]
        p, p_prev, s = _shifted(ps_ref, prev_ref, mu_v, blk)
        wa = s[:, 3 * W:]
        th = jnp.tanh(wa)
        wl = w0_ref[...] + _bdot(th, w2_ref[...])
        apre = a0_ref[...] + _bdot(wa, a2_ref[...])
        bd = _head_blockdiag()
        k = s[:, W:2 * W]
        _, vjp = jax.vjp(lambda k_, wl_, ap_, kkw_, kaw_: _prep_elem(k_, wl_, ap_, kkw_, kaw_, bd),
                         k, wl, apre, kk_ref[...], ka_ref[...])
        dk, dwl, dap, dkkw, dkaw = vjp((dlw_ref[...], dk2_ref[...] + dk22_ref[...], daa_ref[...], dbb_ref[...]))
        dwa = _bdot_nt(dwl, w2_ref[...]) * (1.0 - th * th) + _bdot_nt(dap, a2_ref[...])
        ds = jnp.concatenate([dr_ref[...] + dr2_ref[...], dk, dv_ref[...] + dv2_ref[...], dwa], axis=-1)

        @pl.when(i == 0)
        def _():
            for ref in (dmu_ref, dw0_ref, dw2_ref, da0_ref, da2_ref, dkk_ref, dka_ref, carry):
                ref[...] = jnp.zeros_like(ref)

        dmu_ref[...] += jnp.sum(ds * (p_prev - p), axis=0, keepdims=True)
        dw0_ref[...] += jnp.sum(dwl, axis=0, keepdims=True)
        da0_ref[...] += jnp.sum(dap, axis=0, keepdims=True)
        dw2_ref[...] += _bdot_tn(th, dwl)
        da2_ref[...] += _bdot_tn(wa, dap)
        dkk_ref[...] += dkkw
        dka_ref[...] += dkaw
        dsm = ds * mu_v
        last = (blk % PREP_NB) == PREP_NB - 1
        nxt = jnp.where(last, 0.0, carry[0:1, :])
        up = pltpu.roll(dsm, tm - 1, 0)
        up = jnp.where(_iota2(up.shape, 0) == tm - 1, nxt, up)
        dps_ref[...] = ds - dsm + up
        carry[0:1, :] = dsm[0:1, :]

    vec = _const_spec((1, W))
    rrow = lambda width: pl.BlockSpec((tm, width), lambda i: (rev(i), 0))
    return pl.pallas_call(
        body, grid=(nb,), name="even_prep_bwd",
        in_specs=[rrow(SHIFT), _prev_spec(SHIFT, rev), _const_spec((1, SHIFT)), vec,
                  _const_spec((2 * LORA, W)), vec, _const_spec((2 * LORA, W)), vec, vec] + [rrow(W)] * 9,
        out_specs=[rrow(SHIFT), _const_spec((1, SHIFT)), vec, _const_spec((2 * LORA, W)), vec,
                   _const_spec((2 * LORA, W)), vec, vec],
        out_shape=[jax.ShapeDtypeStruct((T, SHIFT), F32), jax.ShapeDtypeStruct((1, SHIFT), F32),
                   jax.ShapeDtypeStruct((1, W), F32), jax.ShapeDtypeStruct((2 * LORA, W), F32),
                   jax.ShapeDtypeStruct((1, W), F32), jax.ShapeDtypeStruct((2 * LORA, W), F32),
                   jax.ShapeDtypeStruct((1, W), F32), jax.ShapeDtypeStruct((1, W), F32)],
        scratch_shapes=[pltpu.VMEM((8, SHIFT), F32)],
        compiler_params=_cparams(("arbitrary",), VMEM_BIG),
    )(ps, ps, mu, w0, w2x, a0, a2x, kkw, kaw, dr, dlw, dk2, dv, daa, dbb, dr2, dk22, dv2)


def _chunk_masks():
    row = _iota2((2 * L, 2 * L), 0)
    col = _iota2((2 * L, 2 * L), 1)
    step = col & (L - 1)
    keep = ((row < L) & (row > step)) | ((row >= L) & (row - L >= step))
    r1 = _iota2((L, L), 0)
    c1 = _iota2((L, L), 1)
    return keep.astype(F32), (r1 >= c1).astype(F32), (r1 == c1).astype(F32)


def _scaled(r, lw, k2, aa, bb, tri):
    g = _hdot(tri, lw)
    eg = jnp.exp(g)
    eng = jnp.exp(-g)
    egp = jnp.exp(g - lw)
    return eg, eng, egp, aa * egp, r * eg, bb * eng, k2 * eng


def _head_cols(h):
    return slice(h * HD, (h + 1) * HD)


def _per_head(a):
    return [a[:, _head_cols(h)] for h in range(NH)]


def _heads_operands(at, rt, bt, kt):
    x = [jnp.concatenate([a, r], axis=0).astype(BF16) for a, r in zip(_per_head(at), _per_head(rt))]
    yk = [jnp.concatenate([b, k], axis=0).astype(BF16) for b, k in zip(_per_head(bt), _per_head(kt))]
    return x, yk


def _heads_matrices(x, yk, keep, eye):
    m = [_bdot_nt(a, b) * keep for a, b in zip(x, yk)]
    p = [a[:L, :L] for a in m]
    tinv = [eye + a for a in p]
    for _ in range(5):
        p = [_bdot(a, a) for a in p]
        tinv = [t + _bdot(t, a) for t, a in zip(tinv, p)]
    return [a.astype(BF16) for a in m], [a.astype(BF16) for a in tinv]


def _heads_fwd(x, yk, m, tinv, v, s0, egl):
    xh = [_bdot_nt(a, s) for a, s in zip(x, s0)]
    u = [_bdot(t, h[:L] + _bdot(a[:L, L:], w)) for t, h, a, w in zip(tinv, xh, m, v)]
    uv = [jnp.concatenate([a, w], axis=0).astype(BF16) for a, w in zip(u, v)]
    y = [h[L:] + _bdot(a[L:], w) for h, a, w in zip(xh, m, uv)]
    sn = [e * (s + _bdot_tn(w, b)) for e, s, w, b in zip(egl, s0, uv, yk)]
    return y, sn, uv


def _heads_bwd(x, yk, m, tinv, v, s0, egl, dy, dsn, keep):
    _, sn, uv = _heads_fwd(x, yk, m, tinv, v, s0, egl)
    dzs = [d * e for d, e in zip(dsn, egl)]
    dgl = [jnp.sum(d * s, axis=0, keepdims=True) for d, s in zip(dsn, sn)]
    dyb = [a.astype(BF16) for a in dy]
    t1 = [_bdot_tn(a[L:], d) for a, d in zip(m, dyb)]
    t2 = [_bdot_nt(b, d) for b, d in zip(yk, dzs)]
    drhs = [_bdot_tn(t, a[:L] + b[:L]) for t, a, b in zip(tinv, t1, t2)]
    dv = [a[L:] + b[L:] + _bdot_tn(c[:L, L:], d) for a, b, c, d in zip(t1, t2, m, drhs)]
    gg = [jnp.concatenate([a, b], axis=0).astype(BF16) for a, b in zip(drhs, dy)]
    ds0 = [d + _bdot_tn(g, a) for d, g, a in zip(dzs, gg, x)]
    dm = [_bdot_nt(g, w) * keep for g, w in zip(gg, uv)]
    dx = [_bdot(g, s) + _bdot(d, b) for g, s, d, b in zip(gg, s0, dm, yk)]
    dyk = [_bdot_tn(d, a) + _bdot(w, z) for d, a, w, z in zip(dm, x, uv, dzs)]
    return ([a[:L] for a in dx], [a[L:] for a in dx], [a[:L] for a in dyk], [a[L:] for a in dyk], dv, dgl, ds0)


def rwkv_fwd(r, lw, k2, v, aa, bb):
    def body(r_ref, lw_ref, k2_ref, v_ref, aa_ref, bb_ref, y_ref, hs_ref, m_ref, t_ref, state):
        @pl.when(pl.program_id(1) == 0)
        def _():
            state[...] = jnp.zeros_like(state)

        s_all = state[...]
        hs_ref[0] = s_all
        keep, tri, eye = _chunk_masks()
        eg, _, _, at, rt, bt, kt = _scaled(r_ref[...], lw_ref[...], k2_ref[...], aa_ref[...], bb_ref[...], tri)
        x, yk = _heads_operands(at, rt, bt, kt)
        m, tinv = _heads_matrices(x, yk, keep, eye)
        s0 = [s_all[_head_cols(h), :] for h in range(NH)]
        y, sn, _ = _heads_fwd(x, yk, m, tinv, _per_head(v_ref[...]), s0, _per_head(eg[L - 1:L, :]))
        y_ref[...] = jnp.concatenate(y, axis=-1)
        m_ref[0] = jnp.concatenate(m, axis=-1)
        t_ref[0] = jnp.concatenate(tinv, axis=-1)
        state[...] = jnp.concatenate(sn, axis=0)

    blk = pl.BlockSpec((L, W), lambda b, c: (b * NC + c, 0))
    per_chunk = lambda rows, cols: pl.BlockSpec((1, rows, cols), lambda b, c: (b * NC + c, 0, 0))
    return pl.pallas_call(
        body, grid=(NSEQ, NC), name="rwkv_fwd",
        in_specs=[blk] * 6,
        out_specs=[blk, per_chunk(W, HD), per_chunk(2 * L, NH * 2 * L), per_chunk(L, NH * L)],
        out_shape=[jax.ShapeDtypeStruct((T, W), F32), jax.ShapeDtypeStruct((NSEQ * NC, W, HD), F32),
                   jax.ShapeDtypeStruct((NSEQ * NC, 2 * L, NH * 2 * L), BF16),
                   jax.ShapeDtypeStruct((NSEQ * NC, L, NH * L), BF16)],
        scratch_shapes=[pltpu.VMEM((W, HD), F32)],
        compiler_params=_cparams(("parallel", "arbitrary")),
    )(r, lw, k2, v, aa, bb)


def rwkv_bwd(r, lw, k2, v, aa, bb, hs, ms, ts, dy):
    def body(r_ref, lw_ref, k2_ref, v_ref, aa_ref, bb_ref, hs_ref, m_ref, t_ref, dy_ref,
             dr_ref, dlw_ref, dk2_ref, dv_ref, daa_ref, dbb_ref, dstate):
        @pl.when(pl.program_id(1) == 0)
        def _():
            dstate[...] = jnp.zeros_like(dstate)

        keep, tri, _ = _chunk_masks()
        eg, eng, egp, at, rt, bt, kt = _scaled(r_ref[...], lw_ref[...], k2_ref[...], aa_ref[...], bb_ref[...], tri)
        x, yk = _heads_operands(at, rt, bt, kt)
        m_all = m_ref[0]
        t_all = t_ref[0]
        m = [m_all[:, h * 2 * L:(h + 1) * 2 * L] for h in range(NH)]
        tinv = [t_all[:, h * L:(h + 1) * L] for h in range(NH)]
        s_all = hs_ref[0]
        ds_all = dstate[...]
        s0 = [s_all[_head_cols(h), :] for h in range(NH)]
        dsn = [ds_all[_head_cols(h), :] for h in range(NH)]
        dat, drt, dbt, dkt, dv, dgl, ds0 = _heads_bwd(
            x, yk, m, tinv, _per_head(v_ref[...]), s0, _per_head(eg[L - 1:L, :]), _per_head(dy_ref[...]), dsn, keep)
        dstate[...] = jnp.concatenate(ds0, axis=0)
        dv_ref[...] = jnp.concatenate(dv, axis=-1)
        dat, drt, dbt, dkt, dgl = (jnp.concatenate(a, axis=-1) for a in (dat, drt, dbt, dkt, dgl))
        dg = drt * rt - dbt * bt - dkt * kt
        dg = dg + jnp.where(_iota2(dg.shape, 0) == L - 1, dgl, 0.0)
        dgp = dat * at
        dlw_ref[...] = _hdot_tn(tri, dg + dgp) - dgp
        dr_ref[...] = drt * eg
        daa_ref[...] = dat * egp
        dbb_ref[...] = dbt * eng
        dk2_ref[...] = dkt * eng

    blk = pl.BlockSpec((L, W), lambda b, c: (b * NC + NC - 1 - c, 0))
    per_chunk = lambda rows, cols: pl.BlockSpec((1, rows, cols), lambda b, c: (b * NC + NC - 1 - c, 0, 0))
    return pl.pallas_call(
        body, grid=(NSEQ, NC), name="rwkv_bwd",
        in_specs=[blk] * 6 + [per_chunk(W, HD), per_chunk(2 * L, NH * 2 * L), per_chunk(L, NH * L), blk],
        out_specs=[blk] * 6,
        out_shape=[jax.ShapeDtypeStruct((T, W), F32)] * 6,
        scratch_shapes=[pltpu.VMEM((W, HD), F32)],
        compiler_params=_cparams(("parallel", "arbitrary")),
    )(r, lw, k2, v, aa, bb, hs, ms, ts, dy)


def _post_math(y, r, k2, v, ga, o, gb, lng, lnb, rk, bd):
    mu = _headsum(y, bd) * (1.0 / HD)
    yc = y - mu
    var = _headsum(yc * yc, bd) * (1.0 / HD)
    yn = yc * lax.rsqrt(var + GN_EPS) * lng + lnb
    bonus = _headsum(r * k2 * rk, bd) * v
    return (yn + bonus) * _silu(ga), o * _silu(gb)


def even_post(y, r, k2, v, ga, o, gb, lng, lnb, rk):
    tm = 256

    def body(y_ref, r_ref, k2_ref, v_ref, ga_ref, o_ref, gb_ref, lng_ref, lnb_ref, rk_ref, z_ref, zt_ref):
        ya, yb = _post_math(y_ref[...], r_ref[...], k2_ref[...], v_ref[...], ga_ref[...], o_ref[...], gb_ref[...],
                            lng_ref[...], lnb_ref[...], rk_ref[...], _head_blockdiag())
        ya, yb = ya.astype(BF16), yb.astype(BF16)
        z_ref[:, 0:W] = ya
        z_ref[:, W:2 * W] = yb
        zt_ref[0:W, :] = ya.T
        zt_ref[W:2 * W, :] = yb.T

    vec = _const_spec((1, W))
    return pl.pallas_call(
        body, grid=(T // tm,), name="even_post",
        in_specs=[_row_spec(tm, W)] * 7 + [vec] * 3,
        out_specs=[_row_spec(tm, D), _col_spec(D, tm)],
        out_shape=[jax.ShapeDtypeStruct((T, D), BF16), jax.ShapeDtypeStruct((D, T), BF16)],
        compiler_params=_cparams(("parallel",)),
    )(y, r, k2, v, ga, o, gb, lng, lnb, rk)


def even_post_bwd(y, r, k2, v, ga, o, gb, lng, lnb, rk, dz):
    tm = 256

    def body(y_ref, r_ref, k2_ref, v_ref, ga_ref, o_ref, gb_ref, lng_ref, lnb_ref, rk_ref, dz_ref,
             dy_ref, dr_ref, dk2_ref, dv_ref, dga_ref, do_ref, dgb_ref, dlng_ref, dlnb_ref, drk_ref):
        bd = _head_blockdiag()
        _, vjp = jax.vjp(lambda *a: _post_math(*a, bd), y_ref[...], r_ref[...], k2_ref[...], v_ref[...], ga_ref[...],
                         o_ref[...], gb_ref[...], lng_ref[...], lnb_ref[...], rk_ref[...])
        dzv = dz_ref[...]
        dy, dr, dk2, dv, dga, do, dgb, dlng, dlnb, drk = vjp((dzv[:, 0:W], dzv[:, W:2 * W]))
        for ref, val in ((dy_ref, dy), (dr_ref, dr), (dk2_ref, dk2), (dv_ref, dv), (dga_ref, dga), (do_ref, do),
                         (dgb_ref, dgb)):
            ref[...] = val

        @pl.when(pl.program_id(0) == 0)
        def _():
            for ref in (dlng_ref, dlnb_ref, drk_ref):
                ref[...] = jnp.zeros_like(ref)

        dlng_ref[...] += dlng
        dlnb_ref[...] += dlnb
        drk_ref[...] += drk

    vec = _const_spec((1, W))
    return pl.pallas_call(
        body, grid=(T // tm,), name="even_post_bwd",
        in_specs=[_row_spec(tm, W)] * 7 + [vec] * 3 + [_row_spec(tm, D)],
        out_specs=[_row_spec(tm, W)] * 7 + [vec] * 3,
        out_shape=[jax.ShapeDtypeStruct((T, W), F32)] * 7 + [jax.ShapeDtypeStruct((1, W), F32)] * 3,
        compiler_params=_cparams(("arbitrary",)),
    )(y, r, k2, v, ga, o, gb, lng, lnb, rk, dz)


PADSEQ = SEQ + LEFT * L
ATT_SCALE = 1.0 / math.sqrt(HD)


def _att_probs(q, kw, bias, c):
    valid = _iota2((1, BAND), 1) >= (LEFT - c) * L
    s = [jnp.where(valid, _bdot_nt(a, b) * ATT_SCALE + bias[h], NEG) for h, (a, b) in enumerate(zip(q, kw))]
    e = [jnp.exp(a - jnp.max(a, axis=-1, keepdims=True)) for a in s]
    return [a / jnp.sum(a, axis=-1, keepdims=True) for a in e]


def attention_fwd(q, kpad, vpad, bias):
    def body(q_ref, k_ref, v_ref, b_ref, o_ref):
        c = pl.program_id(1)
        start = pl.multiple_of(c * L, L)
        kw = _per_head(k_ref[pl.ds(start, BAND), :])
        vw = _per_head(v_ref[pl.ds(start, BAND), :])
        p = _att_probs(_per_head(q_ref[...].astype(BF16)), kw, b_ref[...], c)
        o_ref[...] = jnp.concatenate([_bdot(a, b) for a, b in zip(p, vw)], axis=-1)

    qblk = pl.BlockSpec((L, W), lambda b, c: (b * NC + c, 0))
    kblk = pl.BlockSpec((PADSEQ, W), lambda b, c: (b, 0))
    return pl.pallas_call(
        body, grid=(NSEQ, NC), name="attention_fwd",
        in_specs=[qblk, kblk, kblk, _const_spec((NH, L, BAND))],
        out_specs=qblk, out_shape=jax.ShapeDtypeStruct((T, W), F32),
        compiler_params=_cparams(("parallel", "arbitrary")),
    )(q, kpad, vpad, bias)


def attention_bwd(q, kpad, vpad, bias, do):
    def body(q_ref, k_ref, v_ref, b_ref, do_ref, dq_ref, dk_ref, dv_ref, db_ref):
        b = pl.program_id(0)
        c = pl.program_id(1)

        @pl.when(c == 0)
        def _():
            dk_ref[...] = jnp.zeros_like(dk_ref)
            dv_ref[...] = jnp.zeros_like(dv_ref)

        @pl.when((c == 0) & (b == 0))
        def _():
            db_ref[...] = jnp.zeros_like(db_ref)

        start = pl.multiple_of(c * L, L)
        kw = _per_head(k_ref[pl.ds(start, BAND), :])
        vw = _per_head(v_ref[pl.ds(start, BAND), :])
        qs = _per_head(q_ref[...].astype(BF16))
        dos = _per_head(do_ref[...].astype(BF16))
        p = _att_probs(qs, kw, b_ref[...], c)
        dp = [_bdot_nt(a, b) for a, b in zip(dos, vw)]
        ds = [a * (d - jnp.sum(d * a, axis=-1, keepdims=True)) for a, d in zip(p, dp)]
        dss = [(a * ATT_SCALE).astype(BF16) for a in ds]
        dq_ref[...] = jnp.concatenate([_bdot(a, b) for a, b in zip(dss, kw)], axis=-1)
        dk_ref[pl.ds(start, BAND), :] += jnp.concatenate([_bdot_tn(a, b) for a, b in zip(dss, qs)], axis=-1)
        dv_ref[pl.ds(start, BAND), :] += jnp.concatenate([_bdot_tn(a, b) for a, b in zip(p, dos)], axis=-1)
        for h in range(NH):
            db_ref[h] += ds[h]

    qblk = pl.BlockSpec((L, W), lambda b, c: (b * NC + c, 0))
    kblk = pl.BlockSpec((PADSEQ, W), lambda b, c: (b, 0))
    bblk = _const_spec((NH, L, BAND))
    return pl.pallas_call(
        body, grid=(NSEQ, NC), name="attention_bwd",
        in_specs=[qblk, kblk, kblk, bblk, qblk],
        out_specs=[qblk, kblk, kblk, bblk],
        out_shape=[jax.ShapeDtypeStruct((T, W), F32), jax.ShapeDtypeStruct((NSEQ * PADSEQ, W), F32),
                   jax.ShapeDtypeStruct((NSEQ * PADSEQ, W), F32), jax.ShapeDtypeStruct((NH, L, BAND), F32)],
        compiler_params=_cparams(("arbitrary", "arbitrary"), VMEM_BIG),
    )(q, kpad, vpad, bias, do)


def attention_bias(table):
    n = np.arange(BAND + L - 1)
    idx = np.clip(LEFT * L + (L - 1) - n, -CLIP, CLIP) + CLIP
    lo = int(idx.min())
    rev = jnp.flip(table[:, lo:], axis=1)
    n_top = int((idx == 2 * CLIP).sum())
    ext = jnp.concatenate([jnp.broadcast_to(table[:, 2 * CLIP:], (NH, n_top - 1)), rev], axis=1)
    return jnp.stack([ext[:, L - 1 - i:L - 1 - i + BAND] for i in range(L)], axis=1)


def _group_cols(g):
    return slice(g * SGC, (g + 1) * SGC)


def _sg_norm(v, lng, lnb):
    gv = _gelu(v)
    gc = gv - jnp.mean(gv, axis=-1, keepdims=True)
    rstd = lax.rsqrt(jnp.mean(gc * gc, axis=-1, keepdims=True) + LN_EPS)
    xhat = gc * rstd
    return xhat, rstd, xhat * lng + lnb


def gmlp_fwd(u, v, gate, lng, lnb, wm_bf, sgb_t):
    def body(u_ref, v_ref, gt_ref, lng_ref, lnb_ref, wm_ref, sb_ref, z_ref, zt_ref):
        _, _, vln = _sg_norm(v_ref[...], lng_ref[...], lnb_ref[...])
        vlb = vln.astype(BF16)
        for g in range(NG):
            cs = _group_cols(g)
            sv = jnp.dot(wm_ref[g], vlb[:, cs], preferred_element_type=F32) + sb_ref[:, g:g + 1]
            zg = (_gelu(u_ref[:, cs]) * sv * _silu(gt_ref[:, cs])).astype(BF16)
            z_ref[:, cs] = zg
            zt_ref[cs, :] = zg.T

    return pl.pallas_call(
        body, grid=(T // SGC,), name="gmlp_fwd",
        in_specs=[_row_spec(SGC, D)] * 3 + [_const_spec((1, D))] * 2 + [_const_spec((NG, SGC, SGC)),
                                                                      _const_spec((SGC, NG))],
        out_specs=[_row_spec(SGC, D), _col_spec(D, SGC)],
        out_shape=[jax.ShapeDtypeStruct((T, D), BF16), jax.ShapeDtypeStruct((D, T), BF16)],
        compiler_params=_cparams(("parallel",)),
    )(u, v, gate, lng, lnb, wm_bf, sgb_t)


def gmlp_bwd(u, v, gate, lng, lnb, wm_bf, sgb_t, dz):
    def body(u_ref, v_ref, gt_ref, lng_ref, lnb_ref, wm_ref, sb_ref, dz_ref,
             du_ref, dv_ref, dgt_ref, dlng_ref, dlnb_ref, dwm_ref, dsb_ref):
        @pl.when(pl.program_id(0) == 0)
        def _():
            for ref in (dlng_ref, dlnb_ref, dwm_ref, dsb_ref):
                ref[...] = jnp.zeros_like(ref)

        vv = v_ref[...]
        xhat, rstd, vln = _sg_norm(vv, lng_ref[...], lnb_ref[...])
        vlb = vln.astype(BF16)
        dvln = []
        dsv_all = []
        for g in range(NG):
            cs = _group_cols(g)
            uu = u_ref[:, cs]
            gg = gt_ref[:, cs]
            dzz = dz_ref[:, cs]
            sv = jnp.dot(wm_ref[g], vlb[:, cs], preferred_element_type=F32) + sb_ref[:, g:g + 1]
            gu = _gelu(uu)
            sg = _silu(gg)
            dsv = dzz * gu * sg
            dgt_ref[:, cs] = dzz * gu * sv * _dsilu(gg)
            du_ref[:, cs] = dzz * sv * sg * _dgelu(uu)
            dsb16 = dsv.astype(BF16)
            dvln.append(lax.dot_general(wm_ref[g], dsb16, (((0,), (0,)), ((), ())), preferred_element_type=F32))
            dwm_ref[g] += lax.dot_general(dsb16, vlb[:, cs], (((1,), (1,)), ((), ())), preferred_element_type=F32)
            dsv_all.append(dsv)
        dvl = jnp.concatenate(dvln, axis=-1)
        dsv_cat = jnp.concatenate(dsv_all, axis=-1)
        sel = (_iota2((D, NG), 0) // SGC == _iota2((D, NG), 1)).astype(F32)
        dsb_ref[...] += _hdot(dsv_cat, sel)
        dlng_ref[...] += jnp.sum(dvl * xhat, axis=0, keepdims=True)
        dlnb_ref[...] += jnp.sum(dvl, axis=0, keepdims=True)
        dxh = dvl * lng_ref[...]
        dgv = rstd * (dxh - jnp.mean(dxh, axis=-1, keepdims=True)
                      - xhat * jnp.mean(dxh * xhat, axis=-1, keepdims=True))
        dv_ref[...] = dgv * _dgelu(vv)

    return pl.pallas_call(
        body, grid=(T // SGC,), name="gmlp_bwd",
        in_specs=[_row_spec(SGC, D)] * 3 + [_const_spec((1, D))] * 2
        + [_const_spec((NG, SGC, SGC)), _const_spec((SGC, NG)), _row_spec(SGC, D)],
        out_specs=[_row_spec(SGC, D)] * 3 + [_const_spec((1, D))] * 2 + [_const_spec((NG, SGC, SGC)),
                                                                       _const_spec((SGC, NG))],
        out_shape=[jax.ShapeDtypeStruct((T, D), F32)] * 3 + [jax.ShapeDtypeStruct((1, D), F32)] * 2
        + [jax.ShapeDtypeStruct((NG, SGC, SGC), F32), jax.ShapeDtypeStruct((SGC, NG), F32)],
        compiler_params=_cparams(("arbitrary",)),
    )(u, v, gate, lng, lnb, wm_bf, sgb_t, dz)


NCHIP = 4
NDEV = 8
ANY = pl.BlockSpec(memory_space=pl.ANY)


HBM = pl.BlockSpec(memory_space=pltpu.HBM)
SEM = pl.BlockSpec(memory_space=pltpu.SEMAPHORE)
EFFECT = pltpu.SideEffectType.DATAFLOW_SIDE_EFFECTING


def _peers(whole_mesh):
    x, y, c = lax.axis_index("x"), lax.axis_index("y"), lax.axis_index("c")
    if not whole_mesh:
        return [((px, py, c), 2 * px + py) for px, py in ((1 - x, y), (x, 1 - y), (1 - x, 1 - y))], 2 * x + y
    out = []
    for j in range(1, NDEV):
        px, py, pc = x ^ (j >> 2), y ^ ((j >> 1) & 1), c ^ (j & 1)
        out.append(((px, py, pc), 4 * px + 2 * py + pc))
    return out, 4 * x + 2 * y + c


def _send_copies(src, land, send, recv, scatter, whole_mesh, starting):
    peers, me = _peers(whole_mesh)
    copies = []
    for t in range(len(src)):
        for j, (dev, slot) in enumerate(peers):
            k = t * len(peers) + j
            copies.append(pltpu.make_async_remote_copy(
                src_ref=src[t].at[slot] if scatter else src[t], dst_ref=land[t].at[me if starting else slot],
                send_sem=send.at[k], recv_sem=recv.at[k], device_id=dev, device_id_type=MESH))
    return copies


def send_start(srcs, lands, scatter, whole_mesh, name):
    n = len(srcs)
    nsem = n * (NDEV - 1 if whole_mesh else NCHIP - 1)

    def body(*refs):
        for cp in _send_copies(refs[:n], refs[n:2 * n], refs[2 * n], refs[2 * n + 1], scatter, whole_mesh, True):
            cp.start()
        refs[-1][...] = jnp.zeros_like(refs[-1])

    arrs = list(srcs) + list(lands)
    out = pl.pallas_call(
        body, name=name,
        out_shape=(pltpu.SemaphoreType.DMA((nsem,)), pltpu.SemaphoreType.DMA((nsem,)),
                   *[pltpu.HBM(a.shape, a.dtype) for a in arrs], jax.ShapeDtypeStruct((8, 128), F32)),
        in_specs=[HBM] * (2 * n), out_specs=(SEM, SEM, *[HBM] * (2 * n), pl.BlockSpec(memory_space=pltpu.VMEM)),
        input_output_aliases={i: 2 + i for i in range(2 * n)},
        compiler_params=pltpu.CompilerParams(has_side_effects=EFFECT),
    )(*[pltpu.with_memory_space_constraint(a, pltpu.HBM) for a in arrs])
    return out[0], out[1], list(out[2:2 + n]), list(out[2 + n:2 + 2 * n]), out[-1]


def send_wait(started, after, scatter, whole_mesh, name):
    send, recv, srcs, lands, _ = started
    n = len(srcs)

    def body(*refs):
        for cp in _send_copies(refs[:n], refs[n:2 * n], refs[2 * n], refs[2 * n + 1], scatter, whole_mesh, False):
            cp.wait_send()
            cp.wait_recv()

    arrs = list(srcs) + list(lands)
    out = pl.pallas_call(
        body, name=name, out_shape=tuple(pltpu.HBM(a.shape, a.dtype) for a in arrs),
        in_specs=[HBM] * (2 * n) + [SEM, SEM, ANY], out_specs=tuple([HBM] * (2 * n)),
        input_output_aliases={i: i for i in range(2 * n)},
        compiler_params=pltpu.CompilerParams(has_side_effects=EFFECT),
    )(*arrs, send, recv, after)
    return list(out[n:])


def exchange_c(arrs, name):
    n = len(arrs)

    def body(*refs):
        ins, outs = refs[:n], refs[n:2 * n]
        send, recv = refs[2 * n:]
        sibling = (lax.axis_index("x"), lax.axis_index("y"), 1 - lax.axis_index("c"))
        copies = [pltpu.make_async_remote_copy(src_ref=ins[t], dst_ref=outs[t], send_sem=send.at[t], recv_sem=recv.at[t],
                                               device_id=sibling, device_id_type=MESH) for t in range(n)]
        for cp in copies:
            cp.start()
        for cp in copies:
            cp.wait()

    return pl.pallas_call(
        body, name=name, in_specs=[ANY] * n, out_specs=[ANY] * n,
        out_shape=[jax.ShapeDtypeStruct(a.shape, a.dtype) for a in arrs],
        scratch_shapes=[pltpu.SemaphoreType.DMA((n,)), pltpu.SemaphoreType.DMA((n,))],
    )(*arrs)


def gather_weights(arrs, split):
    n = len(arrs)

    def body(*refs):
        ins, outs = refs[:n], refs[n:2 * n]
        send1, recv1, send2, recv2, loc = refs[2 * n:]
        x, y, c = lax.axis_index("x"), lax.axis_index("y"), lax.axis_index("c")
        me = 2 * x + y
        sibling = (x, y, 1 - c)
        peers = [(1 - x, y), (x, 1 - y), (1 - x, 1 - y)]

        def rows_of(t, core):
            half = arrs[t].shape[0] // 2
            return pl.ds(core * half, half)

        def part(ref, t, core):
            return ref.at[rows_of(t, core)] if split[t] else ref

        local = [pltpu.make_async_copy(ins[t], outs[t].at[me], loc.at[t]) for t in range(n)]
        for cp in local:
            cp.start()
        first = []
        for t in range(n):
            for j, (px, py) in enumerate(peers):
                first.append(pltpu.make_async_remote_copy(
                    src_ref=part(ins[t], t, c), dst_ref=part(outs[t].at[me], t, c), send_sem=send1.at[t, j],
                    recv_sem=recv1.at[t, j], device_id=(px, py, c), device_id_type=MESH))
        for cp in first:
            cp.start()
        passed = []
        for t in range(n):
            for j, (px, py) in enumerate(peers):
                landed = part(outs[t].at[2 * px + py], t, c)
                pltpu.make_async_remote_copy(
                    src_ref=landed, dst_ref=landed, send_sem=send1.at[t, j], recv_sem=recv1.at[t, j],
                    device_id=(x, y, c), device_id_type=MESH).wait_recv()
                if split[t]:
                    cp = pltpu.make_async_remote_copy(
                        src_ref=landed, dst_ref=landed, send_sem=send2.at[t, j], recv_sem=recv2.at[t, j],
                        device_id=sibling, device_id_type=MESH)
                    cp.start()
                    passed.append(cp)
        for t in range(n):
            for j, (px, py) in enumerate(peers):
                if split[t]:
                    other = part(outs[t].at[2 * px + py], t, 1 - c)
                    pltpu.make_async_remote_copy(
                        src_ref=other, dst_ref=other, send_sem=send2.at[t, j], recv_sem=recv2.at[t, j],
                        device_id=(x, y, c), device_id_type=MESH).wait_recv()
        for cp in first + passed:
            cp.wait_send()
        for cp in local:
            cp.wait()

    return pl.pallas_call(
        body, name="gather_weights", in_specs=[ANY] * n, out_specs=[ANY] * n,
        out_shape=[jax.ShapeDtypeStruct((NCHIP,) + a.shape, a.dtype) for a in arrs],
        scratch_shapes=[pltpu.SemaphoreType.DMA((n, 3))] * 4 + [pltpu.SemaphoreType.DMA((n,))],
    )(*arrs)


def _adam_math(g, w, m, v):
    m = ADAM_B1 * m + (1.0 - ADAM_B1) * g
    v = ADAM_B2 * v + (1.0 - ADAM_B2) * (g * g)
    m_hat = m / (1.0 - ADAM_B1 ** ADAM_STEP)
    v_hat = v / (1.0 - ADAM_B2 ** ADAM_STEP)
    delta = -ADAM_LR * (m_hat / (jnp.sqrt(v_hat) + ADAM_EPS) + ADAM_WD * w)
    return delta, m, v


def _rows_tile(rows):
    return rows if rows <= 256 else 256


def sum_chips(own, parts, name):
    _, rows, cols = parts.shape
    tr = _rows_tile(rows)

    def body(own_ref, p_ref, o_ref):
        acc = own_ref[...]
        for s in range(NCHIP):
            acc = acc + p_ref[s].astype(F32)
        o_ref[...] = acc

    return pl.pallas_call(
        body, grid=(rows // tr,), name=name,
        in_specs=[pl.BlockSpec((tr, cols), lambda i: (i, 0)), pl.BlockSpec((NCHIP, tr, cols), lambda i: (0, i, 0))],
        out_specs=pl.BlockSpec((tr, cols), lambda i: (i, 0)),
        out_shape=jax.ShapeDtypeStruct((rows, cols), F32),
        compiler_params=_cparams(("parallel",)),
    )(own, parts)


def adam_shard(p_mine, p_sib, w, m, v, name):
    rows, cols = w.shape
    tr = _rows_tile(rows)

    def body(a_ref, b_ref, w_ref, m_ref, v_ref, g_ref, d_ref, mo_ref, vo_ref):
        g = a_ref[...] + b_ref[...]
        g_ref[...] = g
        d_ref[...], mo_ref[...], vo_ref[...] = _adam_math(g, w_ref[...], m_ref[...], v_ref[...])

    spec = pl.BlockSpec((tr, cols), lambda i: (i, 0))
    return pl.pallas_call(
        body, grid=(rows // tr,), name=name, in_specs=[spec] * 5, out_specs=[spec] * 4,
        out_shape=[jax.ShapeDtypeStruct((rows, cols), F32)] * 4,
        compiler_params=_cparams(("parallel",)),
    )(p_mine, p_sib, w, m, v)


def adam_replicated(parts, w, m, v):
    rows = w.shape[0]

    def body(p_ref, w_ref, m_ref, v_ref, g_ref, d_ref, mo_ref, vo_ref):
        g = p_ref[0]
        for d in range(1, NDEV):
            g = g + p_ref[d]
        g_ref[...] = g
        d_ref[...], mo_ref[...], vo_ref[...] = _adam_math(g, w_ref[...], m_ref[...], v_ref[...])

    return pl.pallas_call(
        body, name="adam_replicated", out_shape=[jax.ShapeDtypeStruct((rows, 128), F32)] * 4,
    )(parts, w, m, v)


def _pack(arrs):
    pieces = []
    for a in arrs:
        flat = a.reshape(-1)
        pad = (-flat.shape[0]) % 128
        pieces.append(jnp.pad(flat, (0, pad)) if pad else flat)
    flat = jnp.concatenate(pieces)
    pad = (-flat.shape[0]) % 1024
    return jnp.pad(flat, (0, pad)).reshape(-1, 128)


def _unpack(buf, shapes):
    flat = buf.reshape(-1)
    out = []
    o = 0
    for s in shapes:
        n = int(np.prod(s))
        out.append(flat[o:o + n].reshape(s))
        o += n + (-n) % 128
    return out


EVEN_SPLITS = (SHIFT, W, W, W, W, W)
ODD_SPLITS = (D, D, D)


def _cols_to_chips(a):
    rows, cols = a.shape
    return a.reshape(rows, NCHIP, cols // NCHIP).transpose(1, 0, 2)


def _chips_to_cols(a):
    _, rows, n = a.shape
    return a.transpose(1, 0, 2).reshape(rows, NCHIP * n)


def kernel(x, norm_g, w_in_e, shift_mu, rw_w0, rw_w2, rw_a0, rw_a2, rw_kk, rw_ka, rw_rk, rw_lnx_g, rw_lnx_b, att_bias, w_out_e, w_in_o, sg_ln_g, sg_ln_b, sg_w, sg_b, w_out_o, final_g, loss_target, m_norm_g, m_w_in_e, m_shift_mu, m_rw_w0, m_rw_w2, m_rw_a0, m_rw_a2, m_rw_kk, m_rw_ka, m_rw_rk, m_rw_lnx_g, m_rw_lnx_b, m_att_bias, m_w_out_e, m_w_in_o, m_sg_ln_g, m_sg_ln_b, m_sg_w, m_sg_b, m_w_out_o, m_final_g, v_norm_g, v_w_in_e, v_shift_mu, v_rw_w0, v_rw_w2, v_rw_a0, v_rw_a2, v_rw_kk, v_rw_ka, v_rw_rk, v_rw_lnx_g, v_rw_lnx_b, v_att_bias, v_w_out_e, v_w_in_o, v_sg_ln_g, v_sg_ln_b, v_sg_w, v_sg_b, v_w_out_o, v_final_g):
    x2 = x.reshape(T, D)
    tgt = loss_target.reshape(T, D)

    my_chip = 2 * lax.axis_index("x") + lax.axis_index("y")
    gathered = gather_weights(
        [w_in_e[0].astype(BF16), jnp.concatenate([rw_w2[0], rw_a2[0]], axis=0),
         jnp.concatenate([sg_ln_g, sg_ln_b], axis=0)], [True, True, False])
    wie = _chips_to_cols(gathered[0])
    w2 = _chips_to_cols(gathered[1][:, :LORA])
    a2 = _chips_to_cols(gathered[1][:, LORA:])
    sglg = _chips_to_cols(gathered[2][:, 0:1])
    sglb = _chips_to_cols(gathered[2][:, 1:2])

    late = [w_out_e[0].astype(BF16), w_in_o[0].astype(BF16), w_out_o[0].astype(BF16)]
    late_started = send_start(late, [jnp.broadcast_to(a[None], (NCHIP,) + a.shape) for a in late], False, False,
                              "late_weights_start")

    def late_weights(after):
        woe, wio, woo = send_wait(late_started, after, False, False, "late_weights_wait")
        return woe.reshape(D, D), _chips_to_cols(wio), woo.reshape(D, D)

    def scatter_start(grads, name):
        srcs = [g_.astype(BF16) if g_.shape[-1] >= W else g_ for g_ in grads]
        return send_start(srcs, [jnp.zeros_like(s) for s in srcs], True, False, name)

    def own_block(g_):
        return lax.dynamic_index_in_dim(g_, my_chip, axis=0, keepdims=False)

    started = {}

    def on_odd_grads(d_woo, d_wio):
        blocks = [d_woo.reshape(NCHIP, D // NCHIP, D), _cols_to_chips(d_wio)]
        started["odd"] = (scatter_start(blocks, "odd_grads_start"), [own_block(b) for b in blocks])
        return started["odd"][0][-1]

    def on_small_grads(rep_g):
        mine = _pack(rep_g)
        started["small"] = send_start([mine], [jnp.broadcast_to(mine[None], (NDEV,) + mine.shape)], False, True,
                                      "small_grads_start")
        return started["small"][-1]

    loss_part, dx, big_g, _ = _local_step(
        x2, tgt, wie, late_weights, w2, a2, sglg, sglb, norm_g, shift_mu, rw_w0, rw_a0, rw_kk, rw_ka, rw_rk,
        rw_lnx_g, rw_lnx_b, att_bias, sg_w, sg_b, final_g, first_after=late_started[-1], on_odd_grads=on_odd_grads,
        on_small_grads=on_small_grads)
    loss = lax.psum(loss_part, ("x", "y", "c"))
    d_wie, d_woe, d_wio, d_woo, d_w2, d_a2, d_sglg, d_sglb = big_g
    even_blocks = [_cols_to_chips(d_wie), d_woe.reshape(NCHIP, D // NCHIP, D), _cols_to_chips(d_w2),
                   _cols_to_chips(d_a2), _cols_to_chips(d_sglg), _cols_to_chips(d_sglb)]
    even_started = scatter_start(even_blocks, "even_grads_start")
    even_own = [own_block(b) for b in even_blocks]

    wmv = {"w_in_e": (w_in_e[0], m_w_in_e[0], v_w_in_e[0]), "w_out_e": (w_out_e[0], m_w_out_e[0], v_w_out_e[0]),
           "w_in_o": (w_in_o[0], m_w_in_o[0], v_w_in_o[0]), "w_out_o": (w_out_o[0], m_w_out_o[0], v_w_out_o[0]),
           "rw_w2": (rw_w2[0], m_rw_w2[0], v_rw_w2[0]), "rw_a2": (rw_a2[0], m_rw_a2[0], v_rw_a2[0]),
           "sg_ln_g": (sg_ln_g, m_sg_ln_g, v_sg_ln_g), "sg_ln_b": (sg_ln_b, m_sg_ln_b, v_sg_ln_b)}
    sharded = {}

    def finish(names, own, landed, tag):
        partial = [sum_chips(o_, p_, "sum_" + nm) for o_, p_, nm in zip(own, landed, names)]
        from_sibling = exchange_c(partial, "swap_partials_" + tag)
        for nm, mine, sib in zip(names, partial, from_sibling):
            w_, m_, v_ = wmv[nm]
            res = adam_shard(mine, sib, w_, m_, v_, "adam_" + nm)
            lead = nm not in ("sg_ln_g", "sg_ln_b")
            sharded[nm] = [a[None] if lead else a for a in res]
        return partial[0]

    odd_started, odd_own = started["odd"]
    odd_landed = send_wait(odd_started, even_started[-1], True, False, "odd_grads_wait")
    done_odd = finish(["w_out_o", "w_in_o"], odd_own, odd_landed, "odd")

    rep_names = ["norm_g", "shift_mu", "rw_w0", "rw_a0", "rw_kk", "rw_ka", "rw_rk", "rw_lnx_g", "rw_lnx_b",
                 "att_bias", "sg_w", "sg_b", "final_g"]
    rep_w = [norm_g, shift_mu, rw_w0, rw_a0, rw_kk, rw_ka, rw_rk, rw_lnx_g, rw_lnx_b, att_bias, sg_w, sg_b, final_g]
    rep_m = [m_norm_g, m_shift_mu, m_rw_w0, m_rw_a0, m_rw_kk, m_rw_ka, m_rw_rk, m_rw_lnx_g, m_rw_lnx_b, m_att_bias,
             m_sg_w, m_sg_b, m_final_g]
    rep_v = [v_norm_g, v_shift_mu, v_rw_w0, v_rw_a0, v_rw_kk, v_rw_ka, v_rw_rk, v_rw_lnx_g, v_rw_lnx_b, v_att_bias,
             v_sg_w, v_sg_b, v_final_g]
    shapes = [w_.shape for w_ in rep_w]
    (gathered_g,) = send_wait(started["small"], done_odd, False, True, "small_grads_wait")
    rep_out = adam_replicated(gathered_g, _pack(rep_w), _pack(rep_m), _pack(rep_v))
    even_landed = send_wait(even_started, rep_out[0], True, False, "even_grads_wait")
    finish(["w_in_e", "w_out_e", "rw_w2", "rw_a2", "sg_ln_g", "sg_ln_b"], even_own, even_landed, "even")
    rep = {nm: [] for nm in rep_names}
    for buf in rep_out:
        for nm, a in zip(rep_names, _unpack(buf, shapes)):
            rep[nm].append(a)

    order = ["norm_g", "w_in_e", "shift_mu", "rw_w0", "rw_w2", "rw_a0", "rw_a2", "rw_kk", "rw_ka", "rw_rk",
             "rw_lnx_g", "rw_lnx_b", "att_bias", "w_out_e", "w_in_o", "sg_ln_g", "sg_ln_b", "sg_w", "sg_b",
             "w_out_o", "final_g"]
    results = {**sharded, **rep}
    outs = [loss, dx.reshape(NSEQ, SEQ, D)]
    for kind in range(4):
        outs += [results[nm][kind] for nm in order]
    return tuple(outs)


def _local_step(x2, tgt, wie, late_weights, w2, a2, sglg, sglb, norm_g, shift_mu, rw_w0, rw_a0, rw_kk, rw_ka, rw_rk,
                rw_lnx_g, rw_lnx_b, att_bias, sg_w, sg_b, final_g, first_after=None, on_odd_grads=None,
                on_small_grads=None):
    zl = jnp.zeros((LORA, W), F32)
    w2x = jnp.concatenate([w2, zl], axis=0)
    a2x = jnp.concatenate([zl, a2], axis=0)
    rk = rw_rk.reshape(1, W)
    pos = np.arange(SGC)
    sg_mask = jnp.asarray(((pos[None, :] // L) <= (pos[:, None] // L)).astype(np.float32))
    wm = (sg_w[0] * sg_mask[None]).astype(BF16)
    sgb_t = sg_b[0].T

    xn0, ps, ga, q, kb, vb, gb = ln_in_proj(x2, norm_g[0:1], wie, EVEN_SPLITS, "in_proj_even", after=first_after)
    r, lw, k2, v, aa, bb = even_prep(ps, shift_mu, rw_w0, w2x, rw_a0, a2x, rw_kk, rw_ka)
    y, hs, ms, ts = rwkv_fwd(r, lw, k2, v, aa, bb)
    bias, bias_vjp = jax.vjp(attention_bias, att_bias[0])

    def padded(a):
        return jnp.pad(a.astype(BF16).reshape(NSEQ, SEQ, W), ((0, 0), (LEFT * L, 0), (0, 0))).reshape(NSEQ * PADSEQ, W)

    kpad, vpad = padded(kb), padded(vb)
    o = attention_fwd(q, kpad, vpad, bias)
    z, zt = even_post(y, r, k2, v, ga, o, gb, rw_lnx_g, rw_lnx_b, rk)
    woe, wio, woo = late_weights(z)
    h1 = out_proj(x2, z, woe, "out_proj_even")
    xn1, u, vv, gt = ln_in_proj(h1, norm_g[1:2], wio, ODD_SPLITS, "in_proj_odd")
    z2, z2t = gmlp_fwd(u, vv, gt, sglg, sglb, wm, sgb_t)
    h2 = out_proj(h1, z2, woo, "out_proj_odd")
    dh2, loss_part, d_final_g = final_loss(h2, final_g[None], tgt)

    dz2, d_woo = out_proj_bwd(dh2, z2t, woo, "out_proj_odd_bwd")
    du, dvv, dgt, d_sglg, d_sglb, d_wm, d_sgb_t = gmlp_bwd(u, vv, gt, sglg, sglb, wm, sgb_t, dz2)
    dh1, d_g1, dp2 = in_proj_bwd_x(h1, norm_g[1:2], wio, [du, dvv, dgt], dh2, "in_proj_odd_bwd")
    d_wio = matmul_acc(xn1, dp2, 1024, "in_proj_odd_dw")
    token = on_odd_grads(d_woo, d_wio) if on_odd_grads else None
    dz, d_woe = out_proj_bwd(dh1, zt, woe, "out_proj_even_bwd", after=token)
    dy, dr2, dk22, dv2, dga, do, dgb, d_lng, d_lnb, d_rk = even_post_bwd(
        y, r, k2, v, ga, o, gb, rw_lnx_g, rw_lnx_b, rk, dz)
    dq, dkpad, dvpad, dbias = attention_bwd(q, kpad, vpad, bias, do)

    def unpadded(a):
        return a.reshape(NSEQ, PADSEQ, W)[:, LEFT * L:].reshape(T, W)

    (d_att_bias,) = bias_vjp(dbias)
    dr, dlw, dk2, dv, daa, dbb = rwkv_bwd(r, lw, k2, v, aa, bb, hs, ms, ts, dy)
    dps, d_mu, d_w0, d_w2x, d_a0, d_a2x, d_kk, d_ka = even_prep_bwd(
        ps, shift_mu, rw_w0, w2x, rw_a0, a2x, rw_kk, rw_ka, dr, dlw, dk2, dv, daa, dbb, dr2, dk22, dv2)
    dx, d_g0, dp = in_proj_bwd_x(x2, norm_g[0:1], wie, [dps, dga, dq, unpadded(dkpad), unpadded(dvpad), dgb], dh1,
                                 "in_proj_even_bwd")
    rep_g = [jnp.concatenate([d_g0, d_g1], axis=0), d_mu, d_w0, d_a0, d_kk, d_ka, d_rk, d_lng, d_lnb, d_att_bias,
             d_wm * sg_mask[None], d_sgb_t.T, d_final_g]
    token = on_small_grads(rep_g) if on_small_grads else None
    d_wie = matmul_acc(xn0, dp, 1408, "in_proj_even_dw", after=token)
    big_g = (d_wie, d_woe, d_wio, d_woo, d_w2x[:LORA], d_a2x[LORA:], d_sglg, d_sglb)
    return loss_part[0, 0], dx, big_g, rep_g
```

```python
import functools
import math

import jax
import jax.numpy as jnp
import numpy as np
from jax import lax
from jax.experimental import pallas as pl
from jax.experimental.pallas import tpu as pltpu

F32 = jnp.float32
BF16 = jnp.bfloat16
HI = lax.Precision.HIGHEST

D = 1024
SEQ = 2048
NSEQ = 2
T = NSEQ * SEQ
HD = 64
NH = 8
W = 512
SHIFT = 1664
LORA = 64
EVEN_IN = 4224
ODD_IN = 3072
L = 64
NC = SEQ // L
LEFT = 8
BAND = (LEFT + 1) * L
CLIP = 128
SGC = 128
NG = 8
RMS_EPS = 1e-6
LN_EPS = 1e-5
GN_EPS = 64e-5
NEG = -1e30
VMEM_BIG = 56 * 1024 * 1024

ADAM_LR = 0.001
ADAM_B1 = 0.9
ADAM_B2 = 0.999
ADAM_EPS = 1e-08
ADAM_WD = 0.01
ADAM_STEP = 10

MESH = pl.DeviceIdType.MESH


def _bdot(a, b):
    return jnp.dot(a.astype(BF16), b.astype(BF16), preferred_element_type=F32)


def _bdot_nt(a, b):
    return lax.dot_general(a.astype(BF16), b.astype(BF16), (((1,), (1,)), ((), ())), preferred_element_type=F32)


def _bdot_tn(a, b):
    return lax.dot_general(a.astype(BF16), b.astype(BF16), (((0,), (0,)), ((), ())), preferred_element_type=F32)


def _hdot(a, b):
    return jnp.dot(a, b, precision=HI, preferred_element_type=F32)


def _hdot_nt(a, b):
    return lax.dot_general(a, b, (((1,), (1,)), ((), ())), precision=HI, preferred_element_type=F32)


def _hdot_tn(a, b):
    return lax.dot_general(a, b, (((0,), (0,)), ((), ())), precision=HI, preferred_element_type=F32)


def _iota2(shape, dim):
    return lax.broadcasted_iota(jnp.int32, shape, dim)


def _head_blockdiag():
    r = _iota2((W, W), 0) // HD
    c = _iota2((W, W), 1) // HD
    return (r == c).astype(BF16)


def _headsum_impl(x, bd):
    hi = x.astype(BF16)
    mid = (x - hi.astype(F32)).astype(BF16)
    return jnp.dot(hi, bd, preferred_element_type=F32) + jnp.dot(mid, bd, preferred_element_type=F32)


@jax.custom_vjp
def _headsum(x, bd):
    return _headsum_impl(x, bd)


def _headsum_fwd(x, bd):
    return _headsum_impl(x, bd), bd


def _headsum_bwd(bd, ct):
    return _headsum_impl(ct, bd), None


_headsum.defvjp(_headsum_fwd, _headsum_bwd)


def _silu(x):
    return x * jax.nn.sigmoid(x)


def _dsilu(x):
    s = jax.nn.sigmoid(x)
    return s * (1.0 + x * (1.0 - s))


_GELU_C = math.sqrt(2.0 / math.pi)


def _gelu(x):
    return 0.5 * x * (1.0 + jnp.tanh(_GELU_C * (x + 0.044715 * (x * x * x))))


def _dgelu(x):
    t = jnp.tanh(_GELU_C * (x + 0.044715 * (x * x * x)))
    return 0.5 * (1.0 + t) + 0.5 * x * (1.0 - t * t) * _GELU_C * (1.0 + 3.0 * 0.044715 * x * x)


def _softplus(x):
    return jnp.maximum(x, 0.0) + jnp.log(1.0 + jnp.exp(-jnp.abs(x)))


def _cparams(sem, vmem=None):
    return pltpu.CompilerParams(dimension_semantics=sem, vmem_limit_bytes=vmem)


def _row_spec(tm, width):
    return pl.BlockSpec((tm, width), lambda i: (i, 0))


def _col_spec(height, tm):
    return pl.BlockSpec((height, tm), lambda i: (0, i))


def _const_spec(shape):
    nd = len(shape)
    return pl.BlockSpec(shape, lambda *_: (0,) * nd)


def ln_in_proj(x, g, w_bf, splits, name, after=None):
    n = w_bf.shape[1]
    tm = 256
    spans = []
    o = 0
    for s in splits:
        spans.append((o, o + s))
        o += s
    assert o == n
    extra_specs, extra = _after_operand(after)

    def body(x_ref, g_ref, w_ref, *rest):
        xn_ref, outs = rest[len(extra)], rest[len(extra) + 1:]
        xv = x_ref[...]
        rstd = lax.rsqrt(jnp.mean(xv * xv, axis=-1, keepdims=True) + RMS_EPS)
        xn = (xv * rstd * g_ref[...]).astype(BF16)
        xn_ref[...] = xn.T
        p = jnp.dot(xn, w_ref[...], preferred_element_type=F32)
        for o_ref, (a, b) in zip(outs, spans):
            o_ref[...] = p[:, a:b]

    return pl.pallas_call(
        body, grid=(T // tm,), name=name,
        in_specs=[_row_spec(tm, D), _const_spec((1, D)), _const_spec((D, n))] + extra_specs,
        out_specs=[_col_spec(D, tm)] + [_row_spec(tm, s) for s in splits],
        out_shape=[jax.ShapeDtypeStruct((D, T), BF16)] + [jax.ShapeDtypeStruct((T, s), F32) for s in splits],
        compiler_params=_cparams(("parallel",), VMEM_BIG),
    )(x, g, w_bf, *extra)


def in_proj_bwd_x(x, g, w_bf, dps, dres, name, after=None):
    n = w_bf.shape[1]
    tm = 256
    widths = [d.shape[1] for d in dps]
    extra_specs, extra = _after_operand(after)

    def body(x_ref, g_ref, w_ref, dres_ref, *rest):
        dp_refs = rest[:len(widths)]
        dx_ref, dg_ref = rest[-2:]
        dp = jnp.concatenate([r[...] for r in dp_refs], axis=-1)
        dxn = lax.dot_general(dp, w_ref[...], (((1,), (1,)), ((), ())), preferred_element_type=F32)
        xv = x_ref[...]
        rstd = lax.rsqrt(jnp.mean(xv * xv, axis=-1, keepdims=True) + RMS_EPS)
        xhat = xv * rstd
        dgp = jnp.sum(dxn * xhat, axis=0, keepdims=True)

        @pl.when(pl.program_id(0) == 0)
        def _():
            dg_ref[...] = jnp.zeros_like(dg_ref)

        dg_ref[...] += dgp
        dxh = dxn * g_ref[...]
        dx_ref[...] = dres_ref[...] + rstd * (dxh - xhat * jnp.mean(dxh * xhat, axis=-1, keepdims=True))

    return pl.pallas_call(
        body, grid=(T // tm,), name=name,
        in_specs=[_row_spec(tm, D), _const_spec((1, D)), _const_spec((D, n)), _row_spec(tm, D)]
        + [_row_spec(tm, s) for s in widths] + extra_specs,
        out_specs=[_row_spec(tm, D), _const_spec((1, D))],
        out_shape=[jax.ShapeDtypeStruct((T, D), F32), jax.ShapeDtypeStruct((1, D), F32)],
        compiler_params=_cparams(("arbitrary",), VMEM_BIG),
    )(x, g, w_bf, dres, *dps, *extra)


def _after_operand(after):
    return ([ANY], [after]) if after is not None else ([], [])


def matmul_acc_chips(at_bf, pieces, name, after=None):
    k = at_bf.shape[0]
    widths = [p.shape[1] for p in pieces]
    nb = sum(widths) // NCHIP
    tm = 512
    extra_specs, extra = _after_operand(after)

    def body(a_ref, *rest):
        o_ref = rest[-1]

        @pl.when(pl.program_id(0) == 0)
        def _():
            o_ref[...] = jnp.zeros_like(o_ref)

        a = a_ref[...]
        b = jnp.concatenate([r[...] for r in rest[:len(widths)]], axis=-1)
        for s in range(NCHIP):
            o_ref[s] += jnp.dot(a, b[:, s * nb:(s + 1) * nb], preferred_element_type=F32)

    return pl.pallas_call(
        body, grid=(T // tm,), name=name,
        in_specs=[_col_spec(k, tm)] + [_row_spec(tm, w_) for w_ in widths] + extra_specs,
        out_specs=_const_spec((NCHIP, k, nb)),
        out_shape=jax.ShapeDtypeStruct((NCHIP, k, nb), F32),
        compiler_params=_cparams(("arbitrary",), VMEM_BIG),
    )(at_bf, *pieces, *extra)


def out_proj(h, z_bf, w_bf, name):
    tm = 256

    def body(h_ref, z_ref, w_ref, o_ref):
        o_ref[...] = h_ref[...] + jnp.dot(z_ref[...], w_ref[...], preferred_element_type=F32)

    return pl.pallas_call(
        body, grid=(T // tm,), name=name,
        in_specs=[_row_spec(tm, D), _row_spec(tm, D), _const_spec((D, D))],
        out_specs=_row_spec(tm, D), out_shape=jax.ShapeDtypeStruct((T, D), F32),
        compiler_params=_cparams(("parallel",)),
    )(h, z_bf, w_bf)


def out_proj_bwd(dh, zt_bf, w_bf, name, after=None):
    tm = 256
    extra_specs, extra = _after_operand(after)

    def body(dh_ref, zt_ref, w_ref, *rest):
        dz_ref, dw_ref = rest[-2:]
        dhb = dh_ref[...].astype(BF16)
        dz_ref[...] = lax.dot_general(dhb, w_ref[...], (((1,), (1,)), ((), ())), preferred_element_type=F32)

        @pl.when(pl.program_id(0) == 0)
        def _():
            dw_ref[...] = jnp.zeros_like(dw_ref)

        dw_ref[...] += jnp.dot(zt_ref[...], dhb, preferred_element_type=F32)

    return pl.pallas_call(
        body, grid=(T // tm,), name=name,
        in_specs=[_row_spec(tm, D), _col_spec(D, tm), _const_spec((D, D))] + extra_specs,
        out_specs=[_row_spec(tm, D), _const_spec((D, D))],
        out_shape=[jax.ShapeDtypeStruct((T, D), F32), jax.ShapeDtypeStruct((D, D), F32)],
        compiler_params=_cparams(("arbitrary",)),
    )(dh, zt_bf, w_bf, *extra)


def final_loss(h, g, target):
    tm = 256

    def body(h_ref, g_ref, t_ref, dh_ref, loss_ref, dg_ref):
        xv = h_ref[...]
        rstd = lax.rsqrt(jnp.mean(xv * xv, axis=-1, keepdims=True) + RMS_EPS)
        xhat = xv * rstd
        err = xhat * g_ref[...] - t_ref[...]
        part = 0.5 * jnp.sum(jnp.mean(err * err, axis=-1, keepdims=True), axis=0, keepdims=True)
        dout = err * (1.0 / D)

        @pl.when(pl.program_id(0) == 0)
        def _():
            loss_ref[...] = jnp.zeros_like(loss_ref)
            dg_ref[...] = jnp.zeros_like(dg_ref)

        loss_ref[...] += jnp.broadcast_to(part, loss_ref.shape)
        dg_ref[...] += jnp.sum(dout * xhat, axis=0, keepdims=True)
        dxh = dout * g_ref[...]
        dh_ref[...] = rstd * (dxh - xhat * jnp.mean(dxh * xhat, axis=-1, keepdims=True))

    return pl.pallas_call(
        body, grid=(T // tm,), name="final_loss",
        in_specs=[_row_spec(tm, D), _const_spec((1, D)), _row_spec(tm, D)],
        out_specs=[_row_spec(tm, D), _const_spec((8, 128)), _const_spec((1, D))],
        out_shape=[jax.ShapeDtypeStruct((T, D), F32), jax.ShapeDtypeStruct((8, 128), F32),
                   jax.ShapeDtypeStruct((1, D), F32)],
        compiler_params=_cparams(("arbitrary",)),
    )(h, g, target)


PREP_TM = 256
PREP_NB = SEQ // PREP_TM


def _prep_elem(k, wl, apre, kkw, kaw, bd):
    wraw = -_softplus(-wl) - 0.5
    lw = -jnp.exp(wraw)
    asig = jax.nn.sigmoid(apre)
    kkr = k * kkw
    nrm = jnp.maximum(jnp.sqrt(_headsum(kkr * kkr, bd)), 1e-12)
    kk = kkr / nrm
    k2 = k * (1.0 + (asig - 1.0) * kaw)
    return lw, k2, -kk, kk * asig


def _shifted(ps_ref, prev_ref, mu, blk):
    p = ps_ref[...]
    first = (blk % PREP_NB) == 0
    prev_row = jnp.where(first, 0.0, prev_ref[7:8, :])
    rolled = pltpu.roll(p, 1, 0)
    p_prev = jnp.where(_iota2(p.shape, 0) == 0, prev_row, rolled)
    return p, p_prev, p + (p_prev - p) * mu


def _prev_spec(width, blk_of):
    return pl.BlockSpec((8, width), lambda i: (jnp.maximum(blk_of(i) * (PREP_TM // 8) - 1, 0), 0))


def even_prep(ps, mu, w0, w2x, a0, a2x, kkw, kaw):
    tm = PREP_TM

    def body(ps_ref, prev_ref, mu_ref, w0_ref, w2_ref, a0_ref, a2_ref, kk_ref, ka_ref,
             r_ref, lw_ref, k2_ref, v_ref, aa_ref, bb_ref):
        _, _, s = _shifted(ps_ref, prev_ref, mu_ref[...], pl.program_id(0))
        wa = s[:, 3 * W:]
        wl = w0_ref[...] + _bdot(jnp.tanh(wa), w2_ref[...])
        apre = a0_ref[...] + _bdot(wa, a2_ref[...])
        lw, k2, aa, bb = _prep_elem(s[:, W:2 * W], wl, apre, kk_ref[...], ka_ref[...], _head_blockdiag())
        r_ref[...] = s[:, 0:W]
        v_ref[...] = s[:, 2 * W:3 * W]
        lw_ref[...] = lw
        k2_ref[...] = k2
        aa_ref[...] = aa
        bb_ref[...] = bb

    vec = _const_spec((1, W))
    return pl.pallas_call(
        body, grid=(T // tm,), name="even_prep",
        in_specs=[_row_spec(tm, SHIFT), _prev_spec(SHIFT, lambda i: i), _const_spec((1, SHIFT)), vec,
                  _const_spec((2 * LORA, W)), vec, _const_spec((2 * LORA, W)), vec, vec],
        out_specs=[_row_spec(tm, W)] * 6,
        out_shape=[jax.ShapeDtypeStruct((T, W), F32)] * 6,
        compiler_params=_cparams(("parallel",)),
    )(ps, ps, mu, w0, w2x, a0, a2x, kkw, kaw)


def even_prep_bwd(ps, mu, w0, w2x, a0, a2x, kkw, kaw, dr, dlw, dk2, dv, daa, dbb, dr2, dk22, dv2):
    tm = PREP_TM
    nb = T // tm
    rev = lambda i: nb - 1 - i

    def body(ps_ref, prev_ref, mu_ref, w0_ref, w2_ref, a0_ref, a2_ref, kk_ref, ka_ref,
             dr_ref, dlw_ref, dk2_ref, dv_ref, daa_ref, dbb_ref, dr2_ref, dk22_ref, dv2_ref,
             dps_ref, dmu_ref, dw0_ref, dw2_ref, da0_ref, da2_ref, dkk_ref, dka_ref, carry):
        i = pl.program_id(0)
        blk = rev(i)
        mu_v = mu_ref[...]
        p, p_prev, s = _shifted(ps_ref, prev_ref, mu_v, blk)
        wa = s[:, 3 * W:]
        th = jnp.tanh(wa)
        wl = w0_ref[...] + _bdot(th, w2_ref[...])
        apre = a0_ref[...] + _bdot(wa, a2_ref[...])
        bd = _head_blockdiag()
        k = s[:, W:2 * W]
        _, vjp = jax.vjp(lambda k_, wl_, ap_, kkw_, kaw_: _prep_elem(k_, wl_, ap_, kkw_, kaw_, bd),
                         k, wl, apre, kk_ref[...], ka_ref[...])
        dk, dwl, dap, dkkw, dkaw = vjp((dlw_ref[...], dk2_ref[...] + dk22_ref[...], daa_ref[...], dbb_ref[...]))
        dwa = _bdot_nt(dwl, w2_ref[...]) * (1.0 - th * th) + _bdot_nt(dap, a2_ref[...])
        ds = jnp.concatenate([dr_ref[...] + dr2_ref[...], dk, dv_ref[...] + dv2_ref[...], dwa], axis=-1)

        @pl.when(i == 0)
        def _():
            for ref in (dmu_ref, dw0_ref, dw2_ref, da0_ref, da2_ref, dkk_ref, dka_ref, carry):
                ref[...] = jnp.zeros_like(ref)

        dmu_ref[...] += jnp.sum(ds * (p_prev - p), axis=0, keepdims=True)
        dw0_ref[...] += jnp.sum(dwl, axis=0, keepdims=True)
        da0_ref[...] += jnp.sum(dap, axis=0, keepdims=True)
        dw2_ref[...] += _bdot_tn(th, dwl)
        da2_ref[...] += _bdot_tn(wa, dap)
        dkk_ref[...] += dkkw
        dka_ref[...] += dkaw
        dsm = ds * mu_v
        last = (blk % PREP_NB) == PREP_NB - 1
        nxt = jnp.where(last, 0.0, carry[0:1, :])
        up = pltpu.roll(dsm, tm - 1, 0)
        up = jnp.where(_iota2(up.shape, 0) == tm - 1, nxt, up)
        dps_ref[...] = (ds - dsm + up).astype(BF16)
        carry[0:1, :] = dsm[0:1, :]

    vec = _const_spec((1, W))
    rrow = lambda width: pl.BlockSpec((tm, width), lambda i: (rev(i), 0))
    return pl.pallas_call(
        body, grid=(nb,), name="even_prep_bwd",
        in_specs=[rrow(SHIFT), _prev_spec(SHIFT, rev), _const_spec((1, SHIFT)), vec,
                  _const_spec((2 * LORA, W)), vec, _const_spec((2 * LORA, W)), vec, vec] + [rrow(W)] * 9,
        out_specs=[rrow(SHIFT), _const_spec((1, SHIFT)), vec, _const_spec((2 * LORA, W)), vec,
                   _const_spec((2 * LORA, W)), vec, vec],
        out_shape=[jax.ShapeDtypeStruct((T, SHIFT), BF16), jax.ShapeDtypeStruct((1, SHIFT), F32),
                   jax.ShapeDtypeStruct((1, W), F32), jax.ShapeDtypeStruct((2 * LORA, W), F32),
                   jax.ShapeDtypeStruct((1, W), F32), jax.ShapeDtypeStruct((2 * LORA, W), F32),
                   jax.ShapeDtypeStruct((1, W), F32), jax.ShapeDtypeStruct((1, W), F32)],
        scratch_shapes=[pltpu.VMEM((8, SHIFT), F32)],
        compiler_params=_cparams(("arbitrary",), VMEM_BIG),
    )(ps, ps, mu, w0, w2x, a0, a2x, kkw, kaw, dr, dlw, dk2, dv, daa, dbb, dr2, dk22, dv2)


def _chunk_masks():
    row = _iota2((2 * L, 2 * L), 0)
    col = _iota2((2 * L, 2 * L), 1)
    step = col & (L - 1)
    keep = ((row < L) & (row > step)) | ((row >= L) & (row - L >= step))
    r1 = _iota2((L, L), 0)
    c1 = _iota2((L, L), 1)
    return keep.astype(F32), (r1 >= c1).astype(F32), (r1 == c1).astype(F32)


def _scaled(r, lw, k2, aa, bb, tri):
    g = _hdot(tri, lw)
    eg = jnp.exp(g)
    eng = jnp.exp(-g)
    egp = jnp.exp(g - lw)
    return eg, eng, egp, aa * egp, r * eg, bb * eng, k2 * eng


def _head_cols(h):
    return slice(h * HD, (h + 1) * HD)


def _per_head(a):
    return [a[:, _head_cols(h)] for h in range(NH)]


def _heads_operands(at, rt, bt, kt):
    x = [jnp.concatenate([a, r], axis=0).astype(BF16) for a, r in zip(_per_head(at), _per_head(rt))]
    yk = [jnp.concatenate([b, k], axis=0).astype(BF16) for b, k in zip(_per_head(bt), _per_head(kt))]
    return x, yk


def _heads_matrices(x, yk, keep, eye):
    m = [_bdot_nt(a, b) * keep for a, b in zip(x, yk)]
    p = [a[:L, :L] for a in m]
    tinv = [eye + a for a in p]
    for _ in range(5):
        p = [_bdot(a, a) for a in p]
        tinv = [t + _bdot(t, a) for t, a in zip(tinv, p)]
    return [a.astype(BF16) for a in m], [a.astype(BF16) for a in tinv]


def _heads_fwd(x, yk, m, tinv, v, s0, egl):
    xh = [_bdot_nt(a, s) for a, s in zip(x, s0)]
    u = [_bdot(t, h[:L] + _bdot(a[:L, L:], w)) for t, h, a, w in zip(tinv, xh, m, v)]
    uv = [jnp.concatenate([a, w], axis=0).astype(BF16) for a, w in zip(u, v)]
    y = [h[L:] + _bdot(a[L:], w) for h, a, w in zip(xh, m, uv)]
    sn = [e * (s + _bdot_tn(w, b)) for e, s, w, b in zip(egl, s0, uv, yk)]
    return y, sn, uv


def _heads_bwd(x, yk, m, tinv, v, s0, egl, dy, dsn, keep):
    _, sn, uv = _heads_fwd(x, yk, m, tinv, v, s0, egl)
    dzs = [d * e for d, e in zip(dsn, egl)]
    dgl = [jnp.sum(d * s, axis=0, keepdims=True) for d, s in zip(dsn, sn)]
    dyb = [a.astype(BF16) for a in dy]
    t1 = [_bdot_tn(a[L:], d) for a, d in zip(m, dyb)]
    t2 = [_bdot_nt(b, d) for b, d in zip(yk, dzs)]
    drhs = [_bdot_tn(t, a[:L] + b[:L]) for t, a, b in zip(tinv, t1, t2)]
    dv = [a[L:] + b[L:] + _bdot_tn(c[:L, L:], d) for a, b, c, d in zip(t1, t2, m, drhs)]
    gg = [jnp.concatenate([a, b], axis=0).astype(BF16) for a, b in zip(drhs, dy)]
    ds0 = [d + _bdot_tn(g, a) for d, g, a in zip(dzs, gg, x)]
    dm = [_bdot_nt(g, w) * keep for g, w in zip(gg, uv)]
    dx = [_bdot(g, s) + _bdot(d, b) for g, s, d, b in zip(gg, s0, dm, yk)]
    dyk = [_bdot_tn(d, a) + _bdot(w, z) for d, a, w, z in zip(dm, x, uv, dzs)]
    return ([a[:L] for a in dx], [a[L:] for a in dx], [a[:L] for a in dyk], [a[L:] for a in dyk], dv, dgl, ds0)


def rwkv_fwd(r, lw, k2, v, aa, bb):
    def body(r_ref, lw_ref, k2_ref, v_ref, aa_ref, bb_ref, y_ref, hs_ref, m_ref, t_ref, state):
        @pl.when(pl.program_id(1) == 0)
        def _():
            state[...] = jnp.zeros_like(state)

        s_all = state[...]
        hs_ref[0] = s_all
        keep, tri, eye = _chunk_masks()
        eg, _, _, at, rt, bt, kt = _scaled(r_ref[...], lw_ref[...], k2_ref[...], aa_ref[...], bb_ref[...], tri)
        x, yk = _heads_operands(at, rt, bt, kt)
        m, tinv = _heads_matrices(x, yk, keep, eye)
        s0 = [s_all[_head_cols(h), :] for h in range(NH)]
        y, sn, _ = _heads_fwd(x, yk, m, tinv, _per_head(v_ref[...]), s0, _per_head(eg[L - 1:L, :]))
        y_ref[...] = jnp.concatenate(y, axis=-1)
        m_ref[0] = jnp.concatenate(m, axis=-1)
        t_ref[0] = jnp.concatenate(tinv, axis=-1)
        state[...] = jnp.concatenate(sn, axis=0)

    blk = pl.BlockSpec((L, W), lambda b, c: (b * NC + c, 0))
    per_chunk = lambda rows, cols: pl.BlockSpec((1, rows, cols), lambda b, c: (b * NC + c, 0, 0))
    return pl.pallas_call(
        body, grid=(NSEQ, NC), name="rwkv_fwd",
        in_specs=[blk] * 6,
        out_specs=[blk, per_chunk(W, HD), per_chunk(2 * L, NH * 2 * L), per_chunk(L, NH * L)],
        out_shape=[jax.ShapeDtypeStruct((T, W), F32), jax.ShapeDtypeStruct((NSEQ * NC, W, HD), F32),
                   jax.ShapeDtypeStruct((NSEQ * NC, 2 * L, NH * 2 * L), BF16),
                   jax.ShapeDtypeStruct((NSEQ * NC, L, NH * L), BF16)],
        scratch_shapes=[pltpu.VMEM((W, HD), F32)],
        compiler_params=_cparams(("parallel", "arbitrary")),
    )(r, lw, k2, v, aa, bb)


def rwkv_bwd(r, lw, k2, v, aa, bb, hs, ms, ts, dy):
    def body(r_ref, lw_ref, k2_ref, v_ref, aa_ref, bb_ref, hs_ref, m_ref, t_ref, dy_ref,
             dr_ref, dlw_ref, dk2_ref, dv_ref, daa_ref, dbb_ref, dstate):
        @pl.when(pl.program_id(1) == 0)
        def _():
            dstate[...] = jnp.zeros_like(dstate)

        keep, tri, _ = _chunk_masks()
        eg, eng, egp, at, rt, bt, kt = _scaled(r_ref[...], lw_ref[...], k2_ref[...], aa_ref[...], bb_ref[...], tri)
        x, yk = _heads_operands(at, rt, bt, kt)
        m_all = m_ref[0]
        t_all = t_ref[0]
        m = [m_all[:, h * 2 * L:(h + 1) * 2 * L] for h in range(NH)]
        tinv = [t_all[:, h * L:(h + 1) * L] for h in range(NH)]
        s_all = hs_ref[0]
        ds_all = dstate[...]
        s0 = [s_all[_head_cols(h), :] for h in range(NH)]
        dsn = [ds_all[_head_cols(h), :] for h in range(NH)]
        dat, drt, dbt, dkt, dv, dgl, ds0 = _heads_bwd(
            x, yk, m, tinv, _per_head(v_ref[...]), s0, _per_head(eg[L - 1:L, :]), _per_head(dy_ref[...]), dsn, keep)
        dstate[...] = jnp.concatenate(ds0, axis=0)
        dv_ref[...] = jnp.concatenate(dv, axis=-1)
        dat, drt, dbt, dkt, dgl = (jnp.concatenate(a, axis=-1) for a in (dat, drt, dbt, dkt, dgl))
        dg = drt * rt - dbt * bt - dkt * kt
        dg = dg + jnp.where(_iota2(dg.shape, 0) == L - 1, dgl, 0.0)
        dgp = dat * at
        dlw_ref[...] = _hdot_tn(tri, dg + dgp) - dgp
        dr_ref[...] = drt * eg
        daa_ref[...] = dat * egp
        dbb_ref[...] = dbt * eng
        dk2_ref[...] = dkt * eng

    blk = pl.BlockSpec((L, W), lambda b, c: (b * NC + NC - 1 - c, 0))
    per_chunk = lambda rows, cols: pl.BlockSpec((1, rows, cols), lambda b, c: (b * NC + NC - 1 - c, 0, 0))
    return pl.pallas_call(
        body, grid=(NSEQ, NC), name="rwkv_bwd",
        in_specs=[blk] * 6 + [per_chunk(W, HD), per_chunk(2 * L, NH * 2 * L), per_chunk(L, NH * L), blk],
        out_specs=[blk] * 6,
        out_shape=[jax.ShapeDtypeStruct((T, W), F32)] * 6,
        scratch_shapes=[pltpu.VMEM((W, HD), F32)],
        compiler_params=_cparams(("parallel", "arbitrary")),
    )(r, lw, k2, v, aa, bb, hs, ms, ts, dy)


def _post_math(y, r, k2, v, ga, o, gb, lng, lnb, rk, bd):
    mu = _headsum(y, bd) * (1.0 / HD)
    yc = y - mu
    var = _headsum(yc * yc, bd) * (1.0 / HD)
    yn = yc * lax.rsqrt(var + GN_EPS) * lng + lnb
    bonus = _headsum(r * k2 * rk, bd) * v
    return (yn + bonus) * _silu(ga), o * _silu(gb)


def even_post(y, r, k2, v, ga, o, gb, lng, lnb, rk):
    tm = 256

    def body(y_ref, r_ref, k2_ref, v_ref, ga_ref, o_ref, gb_ref, lng_ref, lnb_ref, rk_ref, z_ref, zt_ref):
        ya, yb = _post_math(y_ref[...], r_ref[...], k2_ref[...], v_ref[...], ga_ref[...], o_ref[...], gb_ref[...],
                            lng_ref[...], lnb_ref[...], rk_ref[...], _head_blockdiag())
        ya, yb = ya.astype(BF16), yb.astype(BF16)
        z_ref[:, 0:W] = ya
        z_ref[:, W:2 * W] = yb
        zt_ref[0:W, :] = ya.T
        zt_ref[W:2 * W, :] = yb.T

    vec = _const_spec((1, W))
    return pl.pallas_call(
        body, grid=(T // tm,), name="even_post",
        in_specs=[_row_spec(tm, W)] * 7 + [vec] * 3,
        out_specs=[_row_spec(tm, D), _col_spec(D, tm)],
        out_shape=[jax.ShapeDtypeStruct((T, D), BF16), jax.ShapeDtypeStruct((D, T), BF16)],
        compiler_params=_cparams(("parallel",)),
    )(y, r, k2, v, ga, o, gb, lng, lnb, rk)


def even_post_bwd(y, r, k2, v, ga, o, gb, lng, lnb, rk, dz):
    tm = 256

    def body(y_ref, r_ref, k2_ref, v_ref, ga_ref, o_ref, gb_ref, lng_ref, lnb_ref, rk_ref, dz_ref,
             dy_ref, dr_ref, dk2_ref, dv_ref, dga_ref, do_ref, dgb_ref, dlng_ref, dlnb_ref, drk_ref):
        bd = _head_blockdiag()
        _, vjp = jax.vjp(lambda *a: _post_math(*a, bd), y_ref[...], r_ref[...], k2_ref[...], v_ref[...], ga_ref[...],
                         o_ref[...], gb_ref[...], lng_ref[...], lnb_ref[...], rk_ref[...])
        dzv = dz_ref[...]
        dy, dr, dk2, dv, dga, do, dgb, dlng, dlnb, drk = vjp((dzv[:, 0:W], dzv[:, W:2 * W]))
        for ref, val in ((dy_ref, dy), (dr_ref, dr), (dk2_ref, dk2), (dv_ref, dv), (dga_ref, dga), (do_ref, do),
                         (dgb_ref, dgb)):
            ref[...] = val.astype(ref.dtype)

        @pl.when(pl.program_id(0) == 0)
        def _():
            for ref in (dlng_ref, dlnb_ref, drk_ref):
                ref[...] = jnp.zeros_like(ref)

        dlng_ref[...] += dlng
        dlnb_ref[...] += dlnb
        drk_ref[...] += drk

    vec = _const_spec((1, W))
    return pl.pallas_call(
        body, grid=(T // tm,), name="even_post_bwd",
        in_specs=[_row_spec(tm, W)] * 7 + [vec] * 3 + [_row_spec(tm, D)],
        out_specs=[_row_spec(tm, W)] * 7 + [vec] * 3,
        out_shape=[jax.ShapeDtypeStruct((T, W), dt) for dt in (F32, F32, F32, F32, BF16, F32, BF16)]
        + [jax.ShapeDtypeStruct((1, W), F32)] * 3,
        compiler_params=_cparams(("arbitrary",)),
    )(y, r, k2, v, ga, o, gb, lng, lnb, rk, dz)


PADSEQ = SEQ + LEFT * L
ATT_SCALE = 1.0 / math.sqrt(HD)


def _att_probs(q, kw, bias, c):
    valid = _iota2((1, BAND), 1) >= (LEFT - c) * L
    s = [jnp.where(valid, _bdot_nt(a, b) * ATT_SCALE + bias[h], NEG) for h, (a, b) in enumerate(zip(q, kw))]
    e = [jnp.exp(a - jnp.max(a, axis=-1, keepdims=True)) for a in s]
    return [a / jnp.sum(a, axis=-1, keepdims=True) for a in e]


def attention_fwd(q, kpad, vpad, bias):
    def body(q_ref, k_ref, v_ref, b_ref, o_ref):
        c = pl.program_id(1)
        start = pl.multiple_of(c * L, L)
        kw = _per_head(k_ref[pl.ds(start, BAND), :])
        vw = _per_head(v_ref[pl.ds(start, BAND), :])
        p = _att_probs(_per_head(q_ref[...].astype(BF16)), kw, b_ref[...], c)
        o_ref[...] = jnp.concatenate([_bdot(a, b) for a, b in zip(p, vw)], axis=-1)

    qblk = pl.BlockSpec((L, W), lambda b, c: (b * NC + c, 0))
    kblk = pl.BlockSpec((PADSEQ, W), lambda b, c: (b, 0))
    return pl.pallas_call(
        body, grid=(NSEQ, NC), name="attention_fwd",
        in_specs=[qblk, kblk, kblk, _const_spec((NH, L, BAND))],
        out_specs=qblk, out_shape=jax.ShapeDtypeStruct((T, W), F32),
        compiler_params=_cparams(("parallel", "arbitrary")),
    )(q, kpad, vpad, bias)


def attention_bwd(q, kpad, vpad, bias, do):
    def body(q_ref, k_ref, v_ref, b_ref, do_ref, dq_ref, dko_ref, dvo_ref, db_ref, dk_ref, dv_ref):
        b = pl.program_id(0)
        c = pl.program_id(1)

        @pl.when(c == 0)
        def _():
            dk_ref[...] = jnp.zeros_like(dk_ref)
            dv_ref[...] = jnp.zeros_like(dv_ref)

        @pl.when((c == 0) & (b == 0))
        def _():
            db_ref[...] = jnp.zeros_like(db_ref)

        start = pl.multiple_of(c * L, L)
        kw = _per_head(k_ref[pl.ds(start, BAND), :])
        vw = _per_head(v_ref[pl.ds(start, BAND), :])
        qs = _per_head(q_ref[...].astype(BF16))
        dos = _per_head(do_ref[...].astype(BF16))
        p = _att_probs(qs, kw, b_ref[...], c)
        dp = [_bdot_nt(a, b) for a, b in zip(dos, vw)]
        ds = [a * (d - jnp.sum(d * a, axis=-1, keepdims=True)) for a, d in zip(p, dp)]
        dss = [(a * ATT_SCALE).astype(BF16) for a in ds]
        dq_ref[...] = jnp.concatenate([_bdot(a, b) for a, b in zip(dss, kw)], axis=-1).astype(BF16)
        dk_ref[pl.ds(start, BAND), :] += jnp.concatenate([_bdot_tn(a, b) for a, b in zip(dss, qs)], axis=-1)
        dv_ref[pl.ds(start, BAND), :] += jnp.concatenate([_bdot_tn(a, b) for a, b in zip(p, dos)], axis=-1)
        for h in range(NH):
            db_ref[h] += ds[h]

        @pl.when(c == NC - 1)
        def _():
            dko_ref[...] = dk_ref[LEFT * L:, :].astype(BF16)
            dvo_ref[...] = dv_ref[LEFT * L:, :].astype(BF16)

    qblk = pl.BlockSpec((L, W), lambda b, c: (b * NC + c, 0))
    kblk = pl.BlockSpec((PADSEQ, W), lambda b, c: (b, 0))
    sblk = pl.BlockSpec((SEQ, W), lambda b, c: (b, 0))
    bblk = _const_spec((NH, L, BAND))
    return pl.pallas_call(
        body, grid=(NSEQ, NC), name="attention_bwd",
        in_specs=[qblk, kblk, kblk, bblk, qblk],
        out_specs=[qblk, sblk, sblk, bblk],
        out_shape=[jax.ShapeDtypeStruct((T, W), BF16), jax.ShapeDtypeStruct((T, W), BF16),
                   jax.ShapeDtypeStruct((T, W), BF16), jax.ShapeDtypeStruct((NH, L, BAND), F32)],
        scratch_shapes=[pltpu.VMEM((PADSEQ, W), F32), pltpu.VMEM((PADSEQ, W), F32)],
        compiler_params=_cparams(("arbitrary", "arbitrary"), VMEM_BIG),
    )(q, kpad, vpad, bias, do)


NTAB = 2 * CLIP + 1
EXT = BAND + L


def _ext_onehot():
    n = _iota2((EXT, NTAB), 0)
    m = _iota2((EXT, NTAB), 1)
    return (jnp.clip(BAND - 1 - n, -CLIP, CLIP) + CLIP == m).astype(F32)


def bias_expand(table):
    def body(t_ref, o_ref):
        ext = _hdot_nt(t_ref[...], _ext_onehot())
        for i in range(L):
            s = L - 1 - i
            o_ref[:, i, :] = (pltpu.roll(ext, EXT - s, 1) if s else ext)[:, :BAND]

    return pl.pallas_call(body, name="bias_expand", out_shape=jax.ShapeDtypeStruct((NH, L, BAND), F32))(table)


def bias_grad(dbias):
    def body(d_ref, o_ref):
        acc = jnp.zeros((NH, EXT), F32)
        zpad = jnp.zeros((NH, EXT - BAND), F32)
        for i in range(L):
            s = L - 1 - i
            row = jnp.concatenate([d_ref[:, i, :], zpad], axis=-1)
            acc = acc + (pltpu.roll(row, s, 1) if s else row)
        o_ref[...] = _hdot(acc, _ext_onehot())

    return pl.pallas_call(body, name="bias_grad", out_shape=jax.ShapeDtypeStruct((NH, NTAB), F32))(dbias)


def _group_cols(g):
    return slice(g * SGC, (g + 1) * SGC)


def _sg_norm(v, lng, lnb):
    gv = _gelu(v)
    gc = gv - jnp.mean(gv, axis=-1, keepdims=True)
    rstd = lax.rsqrt(jnp.mean(gc * gc, axis=-1, keepdims=True) + LN_EPS)
    xhat = gc * rstd
    return xhat, rstd, xhat * lng + lnb


def gmlp_fwd(u, v, gate, lng, lnb, wm_bf, sgb_t):
    def body(u_ref, v_ref, gt_ref, lng_ref, lnb_ref, wm_ref, sb_ref, z_ref, zt_ref):
        _, _, vln = _sg_norm(v_ref[...], lng_ref[...], lnb_ref[...])
        vlb = vln.astype(BF16)
        for g in range(NG):
            cs = _group_cols(g)
            sv = jnp.dot(wm_ref[g], vlb[:, cs], preferred_element_type=F32) + sb_ref[:, g:g + 1]
            zg = (_gelu(u_ref[:, cs]) * sv * _silu(gt_ref[:, cs])).astype(BF16)
            z_ref[:, cs] = zg
            zt_ref[cs, :] = zg.T

    return pl.pallas_call(
        body, grid=(T // SGC,), name="gmlp_fwd",
        in_specs=[_row_spec(SGC, D)] * 3 + [_const_spec((1, D))] * 2 + [_const_spec((NG, SGC, SGC)),
                                                                      _const_spec((SGC, NG))],
        out_specs=[_row_spec(SGC, D), _col_spec(D, SGC)],
        out_shape=[jax.ShapeDtypeStruct((T, D), BF16), jax.ShapeDtypeStruct((D, T), BF16)],
        compiler_params=_cparams(("parallel",)),
    )(u, v, gate, lng, lnb, wm_bf, sgb_t)


def gmlp_bwd(u, v, gate, lng, lnb, wm_bf, sgb_t, dz):
    def body(u_ref, v_ref, gt_ref, lng_ref, lnb_ref, wm_ref, sb_ref, dz_ref,
             du_ref, dv_ref, dgt_ref, dlng_ref, dlnb_ref, dwm_ref, dsb_ref):
        @pl.when(pl.program_id(0) == 0)
        def _():
            for ref in (dlng_ref, dlnb_ref, dwm_ref, dsb_ref):
                ref[...] = jnp.zeros_like(ref)

        vv = v_ref[...]
        xhat, rstd, vln = _sg_norm(vv, lng_ref[...], lnb_ref[...])
        vlb = vln.astype(BF16)
        dvln = []
        dsv_all = []
        for g in range(NG):
            cs = _group_cols(g)
            uu = u_ref[:, cs]
            gg = gt_ref[:, cs]
            dzz = dz_ref[:, cs]
            sv = jnp.dot(wm_ref[g], vlb[:, cs], preferred_element_type=F32) + sb_ref[:, g:g + 1]
            gu = _gelu(uu)
            sg = _silu(gg)
            dsv = dzz * gu * sg
            dgt_ref[:, cs] = (dzz * gu * sv * _dsilu(gg)).astype(BF16)
            du_ref[:, cs] = (dzz * sv * sg * _dgelu(uu)).astype(BF16)
            dsb16 = dsv.astype(BF16)
            dvln.append(lax.dot_general(wm_ref[g], dsb16, (((0,), (0,)), ((), ())), preferred_element_type=F32))
            dwm_ref[g] += lax.dot_general(dsb16, vlb[:, cs], (((1,), (1,)), ((), ())), preferred_element_type=F32)
            dsv_all.append(dsv)
        dvl = jnp.concatenate(dvln, axis=-1)
        dsv_cat = jnp.concatenate(dsv_all, axis=-1)
        sel = (_iota2((D, NG), 0) // SGC == _iota2((D, NG), 1)).astype(F32)
        dsb_ref[...] += _hdot(dsv_cat, sel)
        dlng_ref[...] += jnp.sum(dvl * xhat, axis=0, keepdims=True)
        dlnb_ref[...] += jnp.sum(dvl, axis=0, keepdims=True)
        dxh = dvl * lng_ref[...]
        dgv = rstd * (dxh - jnp.mean(dxh, axis=-1, keepdims=True)
                      - xhat * jnp.mean(dxh * xhat, axis=-1, keepdims=True))
        dv_ref[...] = (dgv * _dgelu(vv)).astype(BF16)

    return pl.pallas_call(
        body, grid=(T // SGC,), name="gmlp_bwd",
        in_specs=[_row_spec(SGC, D)] * 3 + [_const_spec((1, D))] * 2
        + [_const_spec((NG, SGC, SGC)), _const_spec((SGC, NG)), _row_spec(SGC, D)],
        out_specs=[_row_spec(SGC, D)] * 3 + [_const_spec((1, D))] * 2 + [_const_spec((NG, SGC, SGC)),
                                                                       _const_spec((SGC, NG))],
        out_shape=[jax.ShapeDtypeStruct((T, D), BF16)] * 3 + [jax.ShapeDtypeStruct((1, D), F32)] * 2
        + [jax.ShapeDtypeStruct((NG, SGC, SGC), F32), jax.ShapeDtypeStruct((SGC, NG), F32)],
        compiler_params=_cparams(("arbitrary",)),
    )(u, v, gate, lng, lnb, wm_bf, sgb_t, dz)


NCHIP = 4
NDEV = 8
ANY = pl.BlockSpec(memory_space=pl.ANY)


HBM = pl.BlockSpec(memory_space=pltpu.HBM)
SEM = pl.BlockSpec(memory_space=pltpu.SEMAPHORE)
EFFECT = pltpu.SideEffectType.DATAFLOW_SIDE_EFFECTING


def _peers(whole_mesh):
    x, y, c = lax.axis_index("x"), lax.axis_index("y"), lax.axis_index("c")
    if not whole_mesh:
        return [((px, py, c), 2 * px + py) for px, py in ((1 - x, y), (x, 1 - y), (1 - x, 1 - y))], 2 * x + y
    out = []
    for j in range(1, NDEV):
        px, py, pc = x ^ (j >> 2), y ^ ((j >> 1) & 1), c ^ (j & 1)
        out.append(((px, py, pc), 4 * px + 2 * py + pc))
    return out, 4 * x + 2 * y + c


def _send_copies(src, land, send, recv, scatter, whole_mesh, starting):
    peers, me = _peers(whole_mesh)
    copies = []
    for t in range(len(src)):
        for j, (dev, slot) in enumerate(peers):
            k = t * len(peers) + j
            copies.append(pltpu.make_async_remote_copy(
                src_ref=src[t].at[slot] if scatter else src[t], dst_ref=land[t].at[me if starting else slot],
                send_sem=send.at[k], recv_sem=recv.at[k], device_id=dev, device_id_type=MESH))
    return copies


def send_start(srcs, lands, scatter, whole_mesh, name):
    n = len(srcs)
    nsem = n * (NDEV - 1 if whole_mesh else NCHIP - 1)

    def body(*refs):
        for cp in _send_copies(refs[:n], refs[n:2 * n], refs[2 * n], refs[2 * n + 1], scatter, whole_mesh, True):
            cp.start()
        refs[-1][...] = jnp.zeros_like(refs[-1])

    arrs = list(srcs) + list(lands)
    out = pl.pallas_call(
        body, name=name,
        out_shape=(pltpu.SemaphoreType.DMA((nsem,)), pltpu.SemaphoreType.DMA((nsem,)),
                   *[pltpu.HBM(a.shape, a.dtype) for a in arrs], jax.ShapeDtypeStruct((8, 128), F32)),
        in_specs=[HBM] * (2 * n), out_specs=(SEM, SEM, *[HBM] * (2 * n), pl.BlockSpec(memory_space=pltpu.VMEM)),
        input_output_aliases={i: 2 + i for i in range(2 * n)},
        compiler_params=pltpu.CompilerParams(has_side_effects=EFFECT),
    )(*[pltpu.with_memory_space_constraint(a, pltpu.HBM) for a in arrs])
    return out[0], out[1], list(out[2:2 + n]), list(out[2 + n:2 + 2 * n]), out[-1]


def send_wait(started, after, scatter, whole_mesh, name):
    send, recv, srcs, lands, _ = started
    n = len(srcs)

    def body(*refs):
        for cp in _send_copies(refs[:n], refs[n:2 * n], refs[2 * n], refs[2 * n + 1], scatter, whole_mesh, False):
            cp.wait_send()
            cp.wait_recv()

    arrs = list(srcs) + list(lands)
    out = pl.pallas_call(
        body, name=name, out_shape=tuple(pltpu.HBM(a.shape, a.dtype) for a in arrs),
        in_specs=[HBM] * (2 * n) + [SEM, SEM, ANY], out_specs=tuple([HBM] * (2 * n)),
        input_output_aliases={i: i for i in range(2 * n)},
        compiler_params=pltpu.CompilerParams(has_side_effects=EFFECT),
    )(*arrs, send, recv, after)
    return list(out[n:])


def exchange_c(arrs, name):
    n = len(arrs)

    def body(*refs):
        ins, outs = refs[:n], refs[n:2 * n]
        send, recv = refs[2 * n:]
        sibling = (lax.axis_index("x"), lax.axis_index("y"), 1 - lax.axis_index("c"))
        copies = [pltpu.make_async_remote_copy(src_ref=ins[t], dst_ref=outs[t], send_sem=send.at[t], recv_sem=recv.at[t],
                                               device_id=sibling, device_id_type=MESH) for t in range(n)]
        for cp in copies:
            cp.start()
        for cp in copies:
            cp.wait()

    return pl.pallas_call(
        body, name=name, in_specs=[ANY] * n, out_specs=[ANY] * n,
        out_shape=[jax.ShapeDtypeStruct(a.shape, a.dtype) for a in arrs],
        scratch_shapes=[pltpu.SemaphoreType.DMA((n,)), pltpu.SemaphoreType.DMA((n,))],
    )(*arrs)


def gather_weights(arrs, split):
    n = len(arrs)

    def body(*refs):
        ins, outs = refs[:n], refs[n:2 * n]
        send1, recv1, send2, recv2, loc = refs[2 * n:]
        x, y, c = lax.axis_index("x"), lax.axis_index("y"), lax.axis_index("c")
        me = 2 * x + y
        sibling = (x, y, 1 - c)
        peers = [(1 - x, y), (x, 1 - y), (1 - x, 1 - y)]

        def rows_of(t, core):
            half = arrs[t].shape[0] // 2
            return pl.ds(core * half, half)

        def part(ref, t, core):
            return ref.at[rows_of(t, core)] if split[t] else ref

        local = [pltpu.make_async_copy(ins[t], outs[t].at[me], loc.at[t]) for t in range(n)]
        for cp in local:
            cp.start()
        first = []
        for t in range(n):
            for j, (px, py) in enumerate(peers):
                first.append(pltpu.make_async_remote_copy(
                    src_ref=part(ins[t], t, c), dst_ref=part(outs[t].at[me], t, c), send_sem=send1.at[t, j],
                    recv_sem=recv1.at[t, j], device_id=(px, py, c), device_id_type=MESH))
        for cp in first:
            cp.start()
        passed = []
        for t in range(n):
            for j, (px, py) in enumerate(peers):
                landed = part(outs[t].at[2 * px + py], t, c)
                pltpu.make_async_remote_copy(
                    src_ref=landed, dst_ref=landed, send_sem=send1.at[t, j], recv_sem=recv1.at[t, j],
                    device_id=(x, y, c), device_id_type=MESH).wait_recv()
                if split[t]:
                    cp = pltpu.make_async_remote_copy(
                        src_ref=landed, dst_ref=landed, send_sem=send2.at[t, j], recv_sem=recv2.at[t, j],
                        device_id=sibling, device_id_type=MESH)
                    cp.start()
                    passed.append(cp)
        for t in range(n):
            for j, (px, py) in enumerate(peers):
                if split[t]:
                    other = part(outs[t].at[2 * px + py], t, 1 - c)
                    pltpu.make_async_remote_copy(
                        src_ref=other, dst_ref=other, send_sem=send2.at[t, j], recv_sem=recv2.at[t, j],
                        device_id=(x, y, c), device_id_type=MESH).wait_recv()
        for cp in first + passed:
            cp.wait_send()
        for cp in local:
            cp.wait()

    return pl.pallas_call(
        body, name="gather_weights", in_specs=[ANY] * n, out_specs=[ANY] * n,
        out_shape=[jax.ShapeDtypeStruct((NCHIP,) + a.shape, a.dtype) for a in arrs],
        scratch_shapes=[pltpu.SemaphoreType.DMA((n, 3))] * 4 + [pltpu.SemaphoreType.DMA((n,))],
    )(*arrs)


def _adam_math(g, w, m, v):
    m = ADAM_B1 * m + (1.0 - ADAM_B1) * g
    v = ADAM_B2 * v + (1.0 - ADAM_B2) * (g * g)
    m_hat = m / (1.0 - ADAM_B1 ** ADAM_STEP)
    v_hat = v / (1.0 - ADAM_B2 ** ADAM_STEP)
    delta = -ADAM_LR * (m_hat / (jnp.sqrt(v_hat) + ADAM_EPS) + ADAM_WD * w)
    return delta, m, v


def _rows_tile(rows):
    return rows if rows <= 256 else 256


def sum_chips(own, parts, name):
    _, rows, cols = parts.shape
    tr = _rows_tile(rows)

    def body(own_ref, p_ref, o_ref):
        acc = own_ref[...]
        for s in range(NCHIP):
            acc = acc + p_ref[s].astype(F32)
        o_ref[...] = acc

    return pl.pallas_call(
        body, grid=(rows // tr,), name=name,
        in_specs=[pl.BlockSpec((tr, cols), lambda i: (i, 0)), pl.BlockSpec((NCHIP, tr, cols), lambda i: (0, i, 0))],
        out_specs=pl.BlockSpec((tr, cols), lambda i: (i, 0)),
        out_shape=jax.ShapeDtypeStruct((rows, cols), F32),
        compiler_params=_cparams(("parallel",)),
    )(own, parts)


def adam_shard(p_mine, p_sib, w, m, v, name):
    rows, cols = w.shape
    tr = _rows_tile(rows)

    def body(a_ref, b_ref, w_ref, m_ref, v_ref, g_ref, d_ref, mo_ref, vo_ref):
        g = a_ref[...] + b_ref[...]
        g_ref[...] = g
        d_ref[...], mo_ref[...], vo_ref[...] = _adam_math(g, w_ref[...], m_ref[...], v_ref[...])

    spec = pl.BlockSpec((tr, cols), lambda i: (i, 0))
    return pl.pallas_call(
        body, grid=(rows // tr,), name=name, in_specs=[spec] * 5, out_specs=[spec] * 4,
        out_shape=[jax.ShapeDtypeStruct((rows, cols), F32)] * 4,
        compiler_params=_cparams(("parallel",)),
    )(p_mine, p_sib, w, m, v)


def adam_replicated(parts, w, m, v):
    rows = w.shape[0]

    def body(p_ref, w_ref, m_ref, v_ref, g_ref, d_ref, mo_ref, vo_ref):
        g = p_ref[0]
        for d in range(1, NDEV):
            g = g + p_ref[d]
        g_ref[...] = g
        d_ref[...], mo_ref[...], vo_ref[...] = _adam_math(g, w_ref[...], m_ref[...], v_ref[...])

    return pl.pallas_call(
        body, name="adam_replicated", out_shape=[jax.ShapeDtypeStruct((rows, 128), F32)] * 4,
    )(parts, w, m, v)


def _pack(arrs):
    pieces = []
    for a in arrs:
        flat = a.reshape(-1)
        pad = (-flat.shape[0]) % 128
        pieces.append(jnp.pad(flat, (0, pad)) if pad else flat)
    flat = jnp.concatenate(pieces)
    pad = (-flat.shape[0]) % 1024
    return jnp.pad(flat, (0, pad)).reshape(-1, 128)


def _unpack(buf, shapes):
    flat = buf.reshape(-1)
    out = []
    o = 0
    for s in shapes:
        n = int(np.prod(s))
        out.append(flat[o:o + n].reshape(s))
        o += n + (-n) % 128
    return out


EVEN_SPLITS = (SHIFT, W, W, W, W, W)
ODD_SPLITS = (D, D, D)


def _cols_to_chips(a):
    rows, cols = a.shape
    return a.reshape(rows, NCHIP, cols // NCHIP).transpose(1, 0, 2)


def _chips_to_cols(a):
    _, rows, n = a.shape
    return a.transpose(1, 0, 2).reshape(rows, NCHIP * n)


def kernel(x, norm_g, w_in_e, shift_mu, rw_w0, rw_w2, rw_a0, rw_a2, rw_kk, rw_ka, rw_rk, rw_lnx_g, rw_lnx_b, att_bias, w_out_e, w_in_o, sg_ln_g, sg_ln_b, sg_w, sg_b, w_out_o, final_g, loss_target, m_norm_g, m_w_in_e, m_shift_mu, m_rw_w0, m_rw_w2, m_rw_a0, m_rw_a2, m_rw_kk, m_rw_ka, m_rw_rk, m_rw_lnx_g, m_rw_lnx_b, m_att_bias, m_w_out_e, m_w_in_o, m_sg_ln_g, m_sg_ln_b, m_sg_w, m_sg_b, m_w_out_o, m_final_g, v_norm_g, v_w_in_e, v_shift_mu, v_rw_w0, v_rw_w2, v_rw_a0, v_rw_a2, v_rw_kk, v_rw_ka, v_rw_rk, v_rw_lnx_g, v_rw_lnx_b, v_att_bias, v_w_out_e, v_w_in_o, v_sg_ln_g, v_sg_ln_b, v_sg_w, v_sg_b, v_w_out_o, v_final_g):
    x2 = x.reshape(T, D)
    tgt = loss_target.reshape(T, D)

    my_chip = 2 * lax.axis_index("x") + lax.axis_index("y")
    gathered = gather_weights(
        [w_in_e[0].astype(BF16), jnp.concatenate([rw_w2[0], rw_a2[0]], axis=0),
         jnp.concatenate([sg_ln_g, sg_ln_b], axis=0)], [True, True, False])
    wie = _chips_to_cols(gathered[0])
    w2 = _chips_to_cols(gathered[1][:, :LORA])
    a2 = _chips_to_cols(gathered[1][:, LORA:])
    sglg = _chips_to_cols(gathered[2][:, 0:1])
    sglb = _chips_to_cols(gathered[2][:, 1:2])

    late = [w_out_e[0].astype(BF16), w_in_o[0].astype(BF16), w_out_o[0].astype(BF16)]
    late_started = send_start(late, [jnp.broadcast_to(a[None], (NCHIP,) + a.shape) for a in late], False, False,
                              "late_weights_start")

    def late_weights(after):
        woe, wio, woo = send_wait(late_started, after, False, False, "late_weights_wait")
        return woe.reshape(D, D), _chips_to_cols(wio), woo.reshape(D, D)

    def scatter_start(grads, name):
        srcs = [g_.astype(BF16) if g_.shape[-1] >= W else g_ for g_ in grads]
        return send_start(srcs, [jnp.zeros_like(s) for s in srcs], True, False, name)

    def own_block(g_):
        return lax.dynamic_index_in_dim(g_, my_chip, axis=0, keepdims=False)

    started = {}

    def on_odd_grads(d_woo, d_wio):
        blocks = [d_woo.reshape(NCHIP, D // NCHIP, D), d_wio]
        started["odd"] = (scatter_start(blocks, "odd_grads_start"), [own_block(b) for b in blocks])
        return started["odd"][0][-1]

    def on_even_grads(big_g):
        d_wie, d_woe, _, _, d_w2, d_a2, d_sglg, d_sglb = big_g
        blocks = [d_wie, d_woe.reshape(NCHIP, D // NCHIP, D), _cols_to_chips(d_w2), _cols_to_chips(d_a2),
                  _cols_to_chips(d_sglg), _cols_to_chips(d_sglb)]
        started["even"] = (scatter_start(blocks, "even_grads_start"), [own_block(b) for b in blocks])
        return started["even"][0][-1]

    def on_small_grads(rep_g):
        mine = _pack(rep_g)
        started["small"] = send_start([mine], [jnp.broadcast_to(mine[None], (NDEV,) + mine.shape)], False, True,
                                      "small_grads_start")

    loss_part, dx, _, _ = _local_step(
        x2, tgt, wie, late_weights, w2, a2, sglg, sglb, norm_g, shift_mu, rw_w0, rw_a0, rw_kk, rw_ka, rw_rk,
        rw_lnx_g, rw_lnx_b, att_bias, sg_w, sg_b, final_g, first_after=late_started[-1], on_odd_grads=on_odd_grads,
        on_even_grads=on_even_grads, on_small_grads=on_small_grads)
    loss = lax.psum(loss_part, ("x", "y", "c"))
    even_started, even_own = started["even"]

    wmv = {"w_in_e": (w_in_e[0], m_w_in_e[0], v_w_in_e[0]), "w_out_e": (w_out_e[0], m_w_out_e[0], v_w_out_e[0]),
           "w_in_o": (w_in_o[0], m_w_in_o[0], v_w_in_o[0]), "w_out_o": (w_out_o[0], m_w_out_o[0], v_w_out_o[0]),
           "rw_w2": (rw_w2[0], m_rw_w2[0], v_rw_w2[0]), "rw_a2": (rw_a2[0], m_rw_a2[0], v_rw_a2[0]),
           "sg_ln_g": (sg_ln_g, m_sg_ln_g, v_sg_ln_g), "sg_ln_b": (sg_ln_b, m_sg_ln_b, v_sg_ln_b)}
    sharded = {}

    def finish(names, own, landed, tag):
        partial = [sum_chips(o_, p_, "sum_" + nm) for o_, p_, nm in zip(own, landed, names)]
        from_sibling = exchange_c(partial, "swap_partials_" + tag)
        for nm, mine, sib in zip(names, partial, from_sibling):
            w_, m_, v_ = wmv[nm]
            res = adam_shard(mine, sib, w_, m_, v_, "adam_" + nm)
            lead = nm not in ("sg_ln_g", "sg_ln_b")
            sharded[nm] = [a[None] if lead else a for a in res]
        return partial[0]

    odd_started, odd_own = started["odd"]
    odd_landed = send_wait(odd_started, started["small"][-1], True, False, "odd_grads_wait")
    done_odd = finish(["w_out_o", "w_in_o"], odd_own, odd_landed, "odd")

    rep_names = ["norm_g", "shift_mu", "rw_w0", "rw_a0", "rw_kk", "rw_ka", "rw_rk", "rw_lnx_g", "rw_lnx_b",
                 "att_bias", "sg_w", "sg_b", "final_g"]
    rep_w = [norm_g, shift_mu, rw_w0, rw_a0, rw_kk, rw_ka, rw_rk, rw_lnx_g, rw_lnx_b, att_bias, sg_w, sg_b, final_g]
    rep_m = [m_norm_g, m_shift_mu, m_rw_w0, m_rw_a0, m_rw_kk, m_rw_ka, m_rw_rk, m_rw_lnx_g, m_rw_lnx_b, m_att_bias,
             m_sg_w, m_sg_b, m_final_g]
    rep_v = [v_norm_g, v_shift_mu, v_rw_w0, v_rw_a0, v_rw_kk, v_rw_ka, v_rw_rk, v_rw_lnx_g, v_rw_lnx_b, v_att_bias,
             v_sg_w, v_sg_b, v_final_g]
    shapes = [w_.shape for w_ in rep_w]
    (gathered_g,) = send_wait(started["small"], done_odd, False, True, "small_grads_wait")
    rep_out = adam_replicated(gathered_g, _pack(rep_w), _pack(rep_m), _pack(rep_v))
    even_landed = send_wait(even_started, rep_out[0], True, False, "even_grads_wait")
    finish(["w_in_e", "w_out_e", "rw_w2", "rw_a2", "sg_ln_g", "sg_ln_b"], even_own, even_landed, "even")
    rep = {nm: [] for nm in rep_names}
    for buf in rep_out:
        for nm, a in zip(rep_names, _unpack(buf, shapes)):
            rep[nm].append(a)

    order = ["norm_g", "w_in_e", "shift_mu", "rw_w0", "rw_w2", "rw_a0", "rw_a2", "rw_kk", "rw_ka", "rw_rk",
             "rw_lnx_g", "rw_lnx_b", "att_bias", "w_out_e", "w_in_o", "sg_ln_g", "sg_ln_b", "sg_w", "sg_b",
             "w_out_o", "final_g"]
    results = {**sharded, **rep}
    outs = [loss, dx.reshape(NSEQ, SEQ, D)]
    for kind in range(4):
        outs += [results[nm][kind] for nm in order]
    return tuple(outs)


def _local_step(x2, tgt, wie, late_weights, w2, a2, sglg, sglb, norm_g, shift_mu, rw_w0, rw_a0, rw_kk, rw_ka, rw_rk,
                rw_lnx_g, rw_lnx_b, att_bias, sg_w, sg_b, final_g, first_after=None, on_odd_grads=None,
                on_even_grads=None, on_small_grads=None):
    zl = jnp.zeros((LORA, W), F32)
    w2x = jnp.concatenate([w2, zl], axis=0)
    a2x = jnp.concatenate([zl, a2], axis=0)
    rk = rw_rk.reshape(1, W)
    pos = np.arange(SGC)
    sg_mask = jnp.asarray(((pos[None, :] // L) <= (pos[:, None] // L)).astype(np.float32))
    wm = (sg_w[0] * sg_mask[None]).astype(BF16)
    sgb_t = sg_b[0].T

    xn0, ps, ga, q, kb, vb, gb = ln_in_proj(x2, norm_g[0:1], wie, EVEN_SPLITS, "in_proj_even", after=first_after)
    r, lw, k2, v, aa, bb = even_prep(ps, shift_mu, rw_w0, w2x, rw_a0, a2x, rw_kk, rw_ka)
    y, hs, ms, ts = rwkv_fwd(r, lw, k2, v, aa, bb)
    bias = bias_expand(att_bias[0])

    def padded(a):
        return jnp.pad(a.astype(BF16).reshape(NSEQ, SEQ, W), ((0, 0), (LEFT * L, 0), (0, 0))).reshape(NSEQ * PADSEQ, W)

    kpad, vpad = padded(kb), padded(vb)
    o = attention_fwd(q, kpad, vpad, bias)
    z, zt = even_post(y, r, k2, v, ga, o, gb, rw_lnx_g, rw_lnx_b, rk)
    woe, wio, woo = late_weights(z)
    h1 = out_proj(x2, z, woe, "out_proj_even")
    xn1, u, vv, gt = ln_in_proj(h1, norm_g[1:2], wio, ODD_SPLITS, "in_proj_odd")
    z2, z2t = gmlp_fwd(u, vv, gt, sglg, sglb, wm, sgb_t)
    h2 = out_proj(h1, z2, woo, "out_proj_odd")
    dh2, loss_part, d_final_g = final_loss(h2, final_g[None], tgt)

    dz2, d_woo = out_proj_bwd(dh2, z2t, woo, "out_proj_odd_bwd")
    du, dvv, dgt, d_sglg, d_sglb, d_wm, d_sgb_t = gmlp_bwd(u, vv, gt, sglg, sglb, wm, sgb_t, dz2)
    dp_odd = [du, dvv, dgt]
    d_wio = matmul_acc_chips(xn1, dp_odd, "in_proj_odd_dw")
    token = on_odd_grads(d_woo, d_wio) if on_odd_grads else None
    dh1, d_g1 = in_proj_bwd_x(h1, norm_g[1:2], wio, dp_odd, dh2, "in_proj_odd_bwd", after=token)
    dz, d_woe = out_proj_bwd(dh1, zt, woe, "out_proj_even_bwd")
    dy, dr2, dk22, dv2, dga, do, dgb, d_lng, d_lnb, d_rk = even_post_bwd(
        y, r, k2, v, ga, o, gb, rw_lnx_g, rw_lnx_b, rk, dz)
    dq, dkb, dvb, dbias = attention_bwd(q, kpad, vpad, bias, do)
    d_att_bias = bias_grad(dbias)
    dr, dlw, dk2, dv, daa, dbb = rwkv_bwd(r, lw, k2, v, aa, bb, hs, ms, ts, dy)
    dps, d_mu, d_w0, d_w2x, d_a0, d_a2x, d_kk, d_ka = even_prep_bwd(
        ps, shift_mu, rw_w0, w2x, rw_a0, a2x, rw_kk, rw_ka, dr, dlw, dk2, dv, daa, dbb, dr2, dk22, dv2)
    dp_even = [dps, dga, dq, dkb, dvb, dgb]
    d_wie = matmul_acc_chips(xn0, dp_even, "in_proj_even_dw")
    big_g = (d_wie, d_woe, d_wio, d_woo, d_w2x[:LORA], d_a2x[LORA:], d_sglg, d_sglb)
    token = on_even_grads(big_g) if on_even_grads else None
    dx, d_g0 = in_proj_bwd_x(x2, norm_g[0:1], wie, dp_even, dh1, "in_proj_even_bwd", after=token)
    rep_g = [jnp.concatenate([d_g0, d_g1], axis=0), d_mu, d_w0, d_a0, d_kk, d_ka, d_rk, d_lng, d_lnb, d_att_bias,
             d_wm * sg_mask[None], d_sgb_t.T, d_final_g]
    if on_small_grads:
        on_small_grads(rep_g)
    return loss_part[0, 0], dx, big_g, rep_g
```

```python
import functools
import math

import jax
import jax.numpy as jnp
import numpy as np
from jax import lax
from jax.experimental import pallas as pl
from jax.experimental.pallas import tpu as pltpu

F32 = jnp.float32
BF16 = jnp.bfloat16
HI = lax.Precision.HIGHEST

D = 1024
SEQ = 2048
NSEQ = 2
T = NSEQ * SEQ
HD = 64
NH = 8
W = 512
SHIFT = 1664
LORA = 64
EVEN_IN = 4224
ODD_IN = 3072
L = 64
NC = SEQ // L
LEFT = 8
BAND = (LEFT + 1) * L
CLIP = 128
SGC = 128
NG = 8
RMS_EPS = 1e-6
LN_EPS = 1e-5
GN_EPS = 64e-5
NEG = -1e30
VMEM_BIG = 56 * 1024 * 1024

ADAM_LR = 0.001
ADAM_B1 = 0.9
ADAM_B2 = 0.999
ADAM_EPS = 1e-08
ADAM_WD = 0.01
ADAM_STEP = 10

MESH = pl.DeviceIdType.MESH


def _bdot(a, b):
    return jnp.dot(a.astype(BF16), b.astype(BF16), preferred_element_type=F32)


def _bdot_nt(a, b):
    return lax.dot_general(a.astype(BF16), b.astype(BF16), (((1,), (1,)), ((), ())), preferred_element_type=F32)


def _bdot_tn(a, b):
    return lax.dot_general(a.astype(BF16), b.astype(BF16), (((0,), (0,)), ((), ())), preferred_element_type=F32)


def _hdot(a, b):
    return jnp.dot(a, b, precision=HI, preferred_element_type=F32)


def _hdot_nt(a, b):
    return lax.dot_general(a, b, (((1,), (1,)), ((), ())), precision=HI, preferred_element_type=F32)


def _hdot_tn(a, b):
    return lax.dot_general(a, b, (((0,), (0,)), ((), ())), precision=HI, preferred_element_type=F32)


def _iota2(shape, dim):
    return lax.broadcasted_iota(jnp.int32, shape, dim)


def _head_blockdiag():
    r = _iota2((W, W), 0) // HD
    c = _iota2((W, W), 1) // HD
    return (r == c).astype(BF16)


def _headsum_impl(x, bd):
    hi = x.astype(BF16)
    mid = (x - hi.astype(F32)).astype(BF16)
    return jnp.dot(hi, bd, preferred_element_type=F32) + jnp.dot(mid, bd, preferred_element_type=F32)


@jax.custom_vjp
def _headsum(x, bd):
    return _headsum_impl(x, bd)


def _headsum_fwd(x, bd):
    return _headsum_impl(x, bd), bd


def _headsum_bwd(bd, ct):
    return _headsum_impl(ct, bd), None


_headsum.defvjp(_headsum_fwd, _headsum_bwd)


def _silu(x):
    return x * jax.nn.sigmoid(x)


def _dsilu(x):
    s = jax.nn.sigmoid(x)
    return s * (1.0 + x * (1.0 - s))


_GELU_C = math.sqrt(2.0 / math.pi)


def _gelu(x):
    return 0.5 * x * (1.0 + jnp.tanh(_GELU_C * (x + 0.044715 * (x * x * x))))


def _dgelu(x):
    t = jnp.tanh(_GELU_C * (x + 0.044715 * (x * x * x)))
    return 0.5 * (1.0 + t) + 0.5 * x * (1.0 - t * t) * _GELU_C * (1.0 + 3.0 * 0.044715 * x * x)


def _softplus(x):
    return jnp.maximum(x, 0.0) + jnp.log(1.0 + jnp.exp(-jnp.abs(x)))


def _cparams(sem, vmem=None):
    return pltpu.CompilerParams(dimension_semantics=sem, vmem_limit_bytes=vmem)


def _row_spec(tm, width):
    return pl.BlockSpec((tm, width), lambda i: (i, 0))


def _col_spec(height, tm):
    return pl.BlockSpec((height, tm), lambda i: (0, i))


def _const_spec(shape):
    nd = len(shape)
    return pl.BlockSpec(shape, lambda *_: (0,) * nd)


def ln_in_proj(x, g, w_bf, splits, name, after=None):
    n = w_bf.shape[1]
    tm = 256
    spans = []
    o = 0
    for s in splits:
        spans.append((o, o + s))
        o += s
    assert o == n
    extra_specs, extra = _after_operand(after)

    def body(x_ref, g_ref, w_ref, *rest):
        xn_ref, outs = rest[len(extra)], rest[len(extra) + 1:]
        xv = x_ref[...]
        rstd = lax.rsqrt(jnp.mean(xv * xv, axis=-1, keepdims=True) + RMS_EPS)
        xn = (xv * rstd * g_ref[...]).astype(BF16)
        xn_ref[...] = xn.T
        p = jnp.dot(xn, w_ref[...], preferred_element_type=F32)
        for o_ref, (a, b) in zip(outs, spans):
            o_ref[...] = p[:, a:b]

    return pl.pallas_call(
        body, grid=(T // tm,), name=name,
        in_specs=[_row_spec(tm, D), _const_spec((1, D)), _const_spec((D, n))] + extra_specs,
        out_specs=[_col_spec(D, tm)] + [_row_spec(tm, s) for s in splits],
        out_shape=[jax.ShapeDtypeStruct((D, T), BF16)] + [jax.ShapeDtypeStruct((T, s), F32) for s in splits],
        compiler_params=_cparams(("parallel",), VMEM_BIG),
    )(x, g, w_bf, *extra)


def in_proj_bwd_x(x, g, w_bf, dps, dres, name, after=None):
    n = w_bf.shape[1]
    tm = 256
    widths = [d.shape[1] for d in dps]
    extra_specs, extra = _after_operand(after)

    def body(x_ref, g_ref, w_ref, dres_ref, *rest):
        dp_refs = rest[:len(widths)]
        dx_ref, dg_ref = rest[-2:]
        dp = jnp.concatenate([r[...] for r in dp_refs], axis=-1)
        dxn = lax.dot_general(dp, w_ref[...], (((1,), (1,)), ((), ())), preferred_element_type=F32)
        xv = x_ref[...]
        rstd = lax.rsqrt(jnp.mean(xv * xv, axis=-1, keepdims=True) + RMS_EPS)
        xhat = xv * rstd
        dgp = jnp.sum(dxn * xhat, axis=0, keepdims=True)

        @pl.when(pl.program_id(0) == 0)
        def _():
            dg_ref[...] = jnp.zeros_like(dg_ref)

        dg_ref[...] += dgp
        dxh = dxn * g_ref[...]
        dx_ref[...] = dres_ref[...] + rstd * (dxh - xhat * jnp.mean(dxh * xhat, axis=-1, keepdims=True))

    return pl.pallas_call(
        body, grid=(T // tm,), name=name,
        in_specs=[_row_spec(tm, D), _const_spec((1, D)), _const_spec((D, n)), _row_spec(tm, D)]
        + [_row_spec(tm, s) for s in widths] + extra_specs,
        out_specs=[_row_spec(tm, D), _const_spec((1, D))],
        out_shape=[jax.ShapeDtypeStruct((T, D), F32), jax.ShapeDtypeStruct((1, D), F32)],
        compiler_params=_cparams(("arbitrary",), VMEM_BIG),
    )(x, g, w_bf, dres, *dps, *extra)


def _after_operand(after):
    return ([ANY], [after]) if after is not None else ([], [])


def matmul_acc_chips(at_bf, pieces, name, after=None):
    k = at_bf.shape[0]
    widths = [p.shape[1] for p in pieces]
    nb = sum(widths) // NCHIP
    tm = 512
    extra_specs, extra = _after_operand(after)

    def body(a_ref, *rest):
        o_ref = rest[-1]

        @pl.when(pl.program_id(0) == 0)
        def _():
            o_ref[...] = jnp.zeros_like(o_ref)

        a = a_ref[...]
        b = jnp.concatenate([r[...] for r in rest[:len(widths)]], axis=-1)
        for s in range(NCHIP):
            o_ref[s] += jnp.dot(a, b[:, s * nb:(s + 1) * nb], preferred_element_type=F32)

    return pl.pallas_call(
        body, grid=(T // tm,), name=name,
        in_specs=[_col_spec(k, tm)] + [_row_spec(tm, w_) for w_ in widths] + extra_specs,
        out_specs=_const_spec((NCHIP, k, nb)),
        out_shape=jax.ShapeDtypeStruct((NCHIP, k, nb), F32),
        compiler_params=_cparams(("arbitrary",), VMEM_BIG),
    )(at_bf, *pieces, *extra)


def out_proj(h, z_bf, w_bf, name):
    tm = 256

    def body(h_ref, z_ref, w_ref, o_ref):
        o_ref[...] = h_ref[...] + jnp.dot(z_ref[...], w_ref[...], preferred_element_type=F32)

    return pl.pallas_call(
        body, grid=(T // tm,), name=name,
        in_specs=[_row_spec(tm, D), _row_spec(tm, D), _const_spec((D, D))],
        out_specs=_row_spec(tm, D), out_shape=jax.ShapeDtypeStruct((T, D), F32),
        compiler_params=_cparams(("parallel",)),
    )(h, z_bf, w_bf)


def out_proj_bwd(dh, zt_bf, w_bf, name, after=None):
    tm = 256
    extra_specs, extra = _after_operand(after)

    def body(dh_ref, zt_ref, w_ref, *rest):
        dz_ref, dw_ref = rest[-2:]
        dhb = dh_ref[...].astype(BF16)
        dz_ref[...] = lax.dot_general(dhb, w_ref[...], (((1,), (1,)), ((), ())), preferred_element_type=F32)

        @pl.when(pl.program_id(0) == 0)
        def _():
            dw_ref[...] = jnp.zeros_like(dw_ref)

        dw_ref[...] += jnp.dot(zt_ref[...], dhb, preferred_element_type=F32)

    return pl.pallas_call(
        body, grid=(T // tm,), name=name,
        in_specs=[_row_spec(tm, D), _col_spec(D, tm), _const_spec((D, D))] + extra_specs,
        out_specs=[_row_spec(tm, D), _const_spec((D, D))],
        out_shape=[jax.ShapeDtypeStruct((T, D), F32), jax.ShapeDtypeStruct((D, D), F32)],
        compiler_params=_cparams(("arbitrary",)),
    )(dh, zt_bf, w_bf, *extra)


def final_loss(h, g, target):
    tm = 256

    def body(h_ref, g_ref, t_ref, dh_ref, loss_ref, dg_ref):
        xv = h_ref[...]
        rstd = lax.rsqrt(jnp.mean(xv * xv, axis=-1, keepdims=True) + RMS_EPS)
        xhat = xv * rstd
        err = xhat * g_ref[...] - t_ref[...]
        part = 0.5 * jnp.sum(jnp.mean(err * err, axis=-1, keepdims=True), axis=0, keepdims=True)
        dout = err * (1.0 / D)

        @pl.when(pl.program_id(0) == 0)
        def _():
            loss_ref[...] = jnp.zeros_like(loss_ref)
            dg_ref[...] = jnp.zeros_like(dg_ref)

        loss_ref[...] += jnp.broadcast_to(part, loss_ref.shape)
        dg_ref[...] += jnp.sum(dout * xhat, axis=0, keepdims=True)
        dxh = dout * g_ref[...]
        dh_ref[...] = rstd * (dxh - xhat * jnp.mean(dxh * xhat, axis=-1, keepdims=True))

    return pl.pallas_call(
        body, grid=(T // tm,), name="final_loss",
        in_specs=[_row_spec(tm, D), _const_spec((1, D)), _row_spec(tm, D)],
        out_specs=[_row_spec(tm, D), _const_spec((8, 128)), _const_spec((1, D))],
        out_shape=[jax.ShapeDtypeStruct((T, D), F32), jax.ShapeDtypeStruct((8, 128), F32),
                   jax.ShapeDtypeStruct((1, D), F32)],
        compiler_params=_cparams(("arbitrary",)),
    )(h, g, target)


PREP_TM = 256
PREP_NB = SEQ // PREP_TM


def _prep_elem(k, wl, apre, kkw, kaw, bd):
    wraw = -_softplus(-wl) - 0.5
    lw = -jnp.exp(wraw)
    asig = jax.nn.sigmoid(apre)
    kkr = k * kkw
    nrm = jnp.maximum(jnp.sqrt(_headsum(kkr * kkr, bd)), 1e-12)
    kk = kkr / nrm
    k2 = k * (1.0 + (asig - 1.0) * kaw)
    return lw, k2, -kk, kk * asig


def _shifted(ps_ref, prev_ref, mu, blk):
    p = ps_ref[...]
    first = (blk % PREP_NB) == 0
    prev_row = jnp.where(first, 0.0, prev_ref[7:8, :])
    rolled = pltpu.roll(p, 1, 0)
    p_prev = jnp.where(_iota2(p.shape, 0) == 0, prev_row, rolled)
    return p, p_prev, p + (p_prev - p) * mu


def _prev_spec(width, blk_of):
    return pl.BlockSpec((8, width), lambda i: (jnp.maximum(blk_of(i) * (PREP_TM // 8) - 1, 0), 0))


def even_prep(ps, mu, w0, w2x, a0, a2x, kkw, kaw):
    tm = PREP_TM

    def body(ps_ref, prev_ref, mu_ref, w0_ref, w2_ref, a0_ref, a2_ref, kk_ref, ka_ref,
             r_ref, lw_ref, k2_ref, v_ref, aa_ref, bb_ref):
        _, _, s = _shifted(ps_ref, prev_ref, mu_ref[...], pl.program_id(0))
        wa = s[:, 3 * W:]
        wl = w0_ref[...] + _bdot(jnp.tanh(wa), w2_ref[...])
        apre = a0_ref[...] + _bdot(wa, a2_ref[...])
        lw, k2, aa, bb = _prep_elem(s[:, W:2 * W], wl, apre, kk_ref[...], ka_ref[...], _head_blockdiag())
        r_ref[...] = s[:, 0:W]
        v_ref[...] = s[:, 2 * W:3 * W]
        lw_ref[...] = lw
        k2_ref[...] = k2
        aa_ref[...] = aa
        bb_ref[...] = bb

    vec = _const_spec((1, W))
    return pl.pallas_call(
        body, grid=(T // tm,), name="even_prep",
        in_specs=[_row_spec(tm, SHIFT), _prev_spec(SHIFT, lambda i: i), _const_spec((1, SHIFT)), vec,
                  _const_spec((2 * LORA, W)), vec, _const_spec((2 * LORA, W)), vec, vec],
        out_specs=[_row_spec(tm, W)] * 6,
        out_shape=[jax.ShapeDtypeStruct((T, W), F32)] * 6,
        compiler_params=_cparams(("parallel",)),
    )(ps, ps, mu, w0, w2x, a0, a2x, kkw, kaw)


def even_prep_bwd(ps, mu, w0, w2x, a0, a2x, kkw, kaw, dr, dlw, dk2, dv, daa, dbb, dr2, dk22, dv2):
    tm = PREP_TM
    nb = T // tm
    rev = lambda i: nb - 1 - i

    def body(ps_ref, prev_ref, mu_ref, w0_ref, w2_ref, a0_ref, a2_ref, kk_ref, ka_ref,
             dr_ref, dlw_ref, dk2_ref, dv_ref, daa_ref, dbb_ref, dr2_ref, dk22_ref, dv2_ref,
             dps_ref, dmu_ref, dw0_ref, dw2_ref, da0_ref, da2_ref, dkk_ref, dka_ref, carry):
        i = pl.program_id(0)
        blk = rev(i)
        mu_v = mu_ref[...]
        p, p_prev, s = _shifted(ps_ref, prev_ref, mu_v, blk)
        wa = s[:, 3 * W:]
        th = jnp.tanh(wa)
        wl = w0_ref[...] + _bdot(th, w2_ref[...])
        apre = a0_ref[...] + _bdot(wa, a2_ref[...])
        bd = _head_blockdiag()
        k = s[:, W:2 * W]
        _, vjp = jax.vjp(lambda k_, wl_, ap_, kkw_, kaw_: _prep_elem(k_, wl_, ap_, kkw_, kaw_, bd),
                         k, wl, apre, kk_ref[...], ka_ref[...])
        dk, dwl, dap, dkkw, dkaw = vjp((dlw_ref[...], dk2_ref[...] + dk22_ref[...], daa_ref[...], dbb_ref[...]))
        dwa = _bdot_nt(dwl, w2_ref[...]) * (1.0 - th * th) + _bdot_nt(dap, a2_ref[...])
        ds = jnp.concatenate([dr_ref[...] + dr2_ref[...], dk, dv_ref[...] + dv2_ref[...], dwa], axis=-1)

        @pl.when(i == 0)
        def _():
            for ref in (dmu_ref, dw0_ref, dw2_ref, da0_ref, da2_ref, dkk_ref, dka_ref, carry):
                ref[...] = jnp.zeros_like(ref)

        dmu_ref[...] += jnp.sum(ds * (p_prev - p), axis=0, keepdims=True)
        dw0_ref[...] += jnp.sum(dwl, axis=0, keepdims=True)
        da0_ref[...] += jnp.sum(dap, axis=0, keepdims=True)
        dw2_ref[...] += _bdot_tn(th, dwl)
        da2_ref[...] += _bdot_tn(wa, dap)
        dkk_ref[...] += dkkw
        dka_ref[...] += dkaw
        dsm = ds * mu_v
        last = (blk % PREP_NB) == PREP_NB - 1
        nxt = jnp.where(last, 0.0, carry[0:1, :])
        up = pltpu.roll(dsm, tm - 1, 0)
        up = jnp.where(_iota2(up.shape, 0) == tm - 1, nxt, up)
        dps_ref[...] = (ds - dsm + up).astype(BF16)
        carry[0:1, :] = dsm[0:1, :]

    vec = _const_spec((1, W))
    rrow = lambda width: pl.BlockSpec((tm, width), lambda i: (rev(i), 0))
    return pl.pallas_call(
        body, grid=(nb,), name="even_prep_bwd",
        in_specs=[rrow(SHIFT), _prev_spec(SHIFT, rev), _const_spec((1, SHIFT)), vec,
                  _const_spec((2 * LORA, W)), vec, _const_spec((2 * LORA, W)), vec, vec] + [rrow(W)] * 9,
        out_specs=[rrow(SHIFT), _const_spec((1, SHIFT)), vec, _const_spec((2 * LORA, W)), vec,
                   _const_spec((2 * LORA, W)), vec, vec],
        out_shape=[jax.ShapeDtypeStruct((T, SHIFT), BF16), jax.ShapeDtypeStruct((1, SHIFT), F32),
                   jax.ShapeDtypeStruct((1, W), F32), jax.ShapeDtypeStruct((2 * LORA, W), F32),
                   jax.ShapeDtypeStruct((1, W), F32), jax.ShapeDtypeStruct((2 * LORA, W), F32),
                   jax.ShapeDtypeStruct((1, W), F32), jax.ShapeDtypeStruct((1, W), F32)],
        scratch_shapes=[pltpu.VMEM((8, SHIFT), F32)],
        compiler_params=_cparams(("arbitrary",), VMEM_BIG),
    )(ps, ps, mu, w0, w2x, a0, a2x, kkw, kaw, dr, dlw, dk2, dv, daa, dbb, dr2, dk22, dv2)


def _chunk_masks():
    row = _iota2((2 * L, 2 * L), 0)
    col = _iota2((2 * L, 2 * L), 1)
    step = col & (L - 1)
    keep = ((row < L) & (row > step)) | ((row >= L) & (row - L >= step))
    r1 = _iota2((L, L), 0)
    c1 = _iota2((L, L), 1)
    return keep.astype(F32), (r1 >= c1).astype(F32), (r1 == c1).astype(F32)


def _scaled(r, lw, k2, aa, bb, tri):
    g = _hdot(tri, lw)
    eg = jnp.exp(g)
    eng = jnp.exp(-g)
    egp = jnp.exp(g - lw)
    return eg, eng, egp, aa * egp, r * eg, bb * eng, k2 * eng


def _head_cols(h):
    return slice(h * HD, (h + 1) * HD)


def _per_head(a):
    return [a[:, _head_cols(h)] for h in range(NH)]


def _heads_operands(at, rt, bt, kt):
    x = [jnp.concatenate([a, r], axis=0).astype(BF16) for a, r in zip(_per_head(at), _per_head(rt))]
    yk = [jnp.concatenate([b, k], axis=0).astype(BF16) for b, k in zip(_per_head(bt), _per_head(kt))]
    return x, yk


def _heads_matrices(x, yk, keep, eye):
    m = [_bdot_nt(a, b) * keep for a, b in zip(x, yk)]
    p = [a[:L, :L] for a in m]
    tinv = [eye + a for a in p]
    for _ in range(5):
        p = [_bdot(a, a) for a in p]
        tinv = [t + _bdot(t, a) for t, a in zip(tinv, p)]
    return [a.astype(BF16) for a in m], [a.astype(BF16) for a in tinv]


def _heads_fwd(x, yk, m, tinv, v, s0, egl):
    xh = [_bdot_nt(a, s) for a, s in zip(x, s0)]
    u = [_bdot(t, h[:L] + _bdot(a[:L, L:], w)) for t, h, a, w in zip(tinv, xh, m, v)]
    uv = [jnp.concatenate([a, w], axis=0).astype(BF16) for a, w in zip(u, v)]
    y = [h[L:] + _bdot(a[L:], w) for h, a, w in zip(xh, m, uv)]
    sn = [e * (s + _bdot_tn(w, b)) for e, s, w, b in zip(egl, s0, uv, yk)]
    return y, sn, uv


def _heads_bwd(x, yk, m, tinv, v, s0, egl, dy, dsn, keep):
    _, sn, uv = _heads_fwd(x, yk, m, tinv, v, s0, egl)
    dzs = [d * e for d, e in zip(dsn, egl)]
    dgl = [jnp.sum(d * s, axis=0, keepdims=True) for d, s in zip(dsn, sn)]
    dyb = [a.astype(BF16) for a in dy]
    t1 = [_bdot_tn(a[L:], d) for a, d in zip(m, dyb)]
    t2 = [_bdot_nt(b, d) for b, d in zip(yk, dzs)]
    drhs = [_bdot_tn(t, a[:L] + b[:L]) for t, a, b in zip(tinv, t1, t2)]
    dv = [a[L:] + b[L:] + _bdot_tn(c[:L, L:], d) for a, b, c, d in zip(t1, t2, m, drhs)]
    gg = [jnp.concatenate([a, b], axis=0).astype(BF16) for a, b in zip(drhs, dy)]
    ds0 = [d + _bdot_tn(g, a) for d, g, a in zip(dzs, gg, x)]
    dm = [_bdot_nt(g, w) * keep for g, w in zip(gg, uv)]
    dx = [_bdot(g, s) + _bdot(d, b) for g, s, d, b in zip(gg, s0, dm, yk)]
    dyk = [_bdot_tn(d, a) + _bdot(w, z) for d, a, w, z in zip(dm, x, uv, dzs)]
    return ([a[:L] for a in dx], [a[L:] for a in dx], [a[:L] for a in dyk], [a[L:] for a in dyk], dv, dgl, ds0)


def rwkv_fwd(r, lw, k2, v, aa, bb):
    def body(r_ref, lw_ref, k2_ref, v_ref, aa_ref, bb_ref, y_ref, hs_ref, m_ref, t_ref, state):
        @pl.when(pl.program_id(1) == 0)
        def _():
            state[...] = jnp.zeros_like(state)

        s_all = state[...]
        hs_ref[0] = s_all
        keep, tri, eye = _chunk_masks()
        eg, _, _, at, rt, bt, kt = _scaled(r_ref[...], lw_ref[...], k2_ref[...], aa_ref[...], bb_ref[...], tri)
        x, yk = _heads_operands(at, rt, bt, kt)
        m, tinv = _heads_matrices(x, yk, keep, eye)
        s0 = [s_all[_head_cols(h), :] for h in range(NH)]
        y, sn, _ = _heads_fwd(x, yk, m, tinv, _per_head(v_ref[...]), s0, _per_head(eg[L - 1:L, :]))
        y_ref[...] = jnp.concatenate(y, axis=-1)
        m_ref[0] = jnp.concatenate(m, axis=-1)
        t_ref[0] = jnp.concatenate(tinv, axis=-1)
        state[...] = jnp.concatenate(sn, axis=0)

    blk = pl.BlockSpec((L, W), lambda b, c: (b * NC + c, 0))
    per_chunk = lambda rows, cols: pl.BlockSpec((1, rows, cols), lambda b, c: (b * NC + c, 0, 0))
    return pl.pallas_call(
        body, grid=(NSEQ, NC), name="rwkv_fwd",
        in_specs=[blk] * 6,
        out_specs=[blk, per_chunk(W, HD), per_chunk(2 * L, NH * 2 * L), per_chunk(L, NH * L)],
        out_shape=[jax.ShapeDtypeStruct((T, W), F32), jax.ShapeDtypeStruct((NSEQ * NC, W, HD), F32),
                   jax.ShapeDtypeStruct((NSEQ * NC, 2 * L, NH * 2 * L), BF16),
                   jax.ShapeDtypeStruct((NSEQ * NC, L, NH * L), BF16)],
        scratch_shapes=[pltpu.VMEM((W, HD), F32)],
        compiler_params=_cparams(("parallel", "arbitrary")),
    )(r, lw, k2, v, aa, bb)


def rwkv_bwd(r, lw, k2, v, aa, bb, hs, ms, ts, dy):
    def body(r_ref, lw_ref, k2_ref, v_ref, aa_ref, bb_ref, hs_ref, m_ref, t_ref, dy_ref,
             dr_ref, dlw_ref, dk2_ref, dv_ref, daa_ref, dbb_ref, dstate):
        @pl.when(pl.program_id(1) == 0)
        def _():
            dstate[...] = jnp.zeros_like(dstate)

        keep, tri, _ = _chunk_masks()
        eg, eng, egp, at, rt, bt, kt = _scaled(r_ref[...], lw_ref[...], k2_ref[...], aa_ref[...], bb_ref[...], tri)
        x, yk = _heads_operands(at, rt, bt, kt)
        m_all = m_ref[0]
        t_all = t_ref[0]
        m = [m_all[:, h * 2 * L:(h + 1) * 2 * L] for h in range(NH)]
        tinv = [t_all[:, h * L:(h + 1) * L] for h in range(NH)]
        s_all = hs_ref[0]
        ds_all = dstate[...]
        s0 = [s_all[_head_cols(h), :] for h in range(NH)]
        dsn = [ds_all[_head_cols(h), :] for h in range(NH)]
        dat, drt, dbt, dkt, dv, dgl, ds0 = _heads_bwd(
            x, yk, m, tinv, _per_head(v_ref[...]), s0, _per_head(eg[L - 1:L, :]), _per_head(dy_ref[...]), dsn, keep)
        dstate[...] = jnp.concatenate(ds0, axis=0)
        dv_ref[...] = jnp.concatenate(dv, axis=-1)
        dat, drt, dbt, dkt, dgl = (jnp.concatenate(a, axis=-1) for a in (dat, drt, dbt, dkt, dgl))
        dg = drt * rt - dbt * bt - dkt * kt
        dg = dg + jnp.where(_iota2(dg.shape, 0) == L - 1, dgl, 0.0)
        dgp = dat * at
        dlw_ref[...] = _hdot_tn(tri, dg + dgp) - dgp
        dr_ref[...] = drt * eg
        daa_ref[...] = dat * egp
        dbb_ref[...] = dbt * eng
        dk2_ref[...] = dkt * eng

    blk = pl.BlockSpec((L, W), lambda b, c: (b * NC + NC - 1 - c, 0))
    per_chunk = lambda rows, cols: pl.BlockSpec((1, rows, cols), lambda b, c: (b * NC + NC - 1 - c, 0, 0))
    return pl.pallas_call(
        body, grid=(NSEQ, NC), name="rwkv_bwd",
        in_specs=[blk] * 6 + [per_chunk(W, HD), per_chunk(2 * L, NH * 2 * L), per_chunk(L, NH * L), blk],
        out_specs=[blk] * 6,
        out_shape=[jax.ShapeDtypeStruct((T, W), F32)] * 6,
        scratch_shapes=[pltpu.VMEM((W, HD), F32)],
        compiler_params=_cparams(("parallel", "arbitrary")),
    )(r, lw, k2, v, aa, bb, hs, ms, ts, dy)


def _post_math(y, r, k2, v, ga, o, gb, lng, lnb, rk, bd):
    mu = _headsum(y, bd) * (1.0 / HD)
    yc = y - mu
    var = _headsum(yc * yc, bd) * (1.0 / HD)
    yn = yc * lax.rsqrt(var + GN_EPS) * lng + lnb
    bonus = _headsum(r * k2 * rk, bd) * v
    return (yn + bonus) * _silu(ga), o * _silu(gb)


def even_post(y, r, k2, v, ga, o, gb, lng, lnb, rk):
    tm = 256

    def body(y_ref, r_ref, k2_ref, v_ref, ga_ref, o_ref, gb_ref, lng_ref, lnb_ref, rk_ref, z_ref, zt_ref):
        ya, yb = _post_math(y_ref[...], r_ref[...], k2_ref[...], v_ref[...], ga_ref[...], o_ref[...], gb_ref[...],
                            lng_ref[...], lnb_ref[...], rk_ref[...], _head_blockdiag())
        ya, yb = ya.astype(BF16), yb.astype(BF16)
        z_ref[:, 0:W] = ya
        z_ref[:, W:2 * W] = yb
        zt_ref[0:W, :] = ya.T
        zt_ref[W:2 * W, :] = yb.T

    vec = _const_spec((1, W))
    return pl.pallas_call(
        body, grid=(T // tm,), name="even_post",
        in_specs=[_row_spec(tm, W)] * 7 + [vec] * 3,
        out_specs=[_row_spec(tm, D), _col_spec(D, tm)],
        out_shape=[jax.ShapeDtypeStruct((T, D), BF16), jax.ShapeDtypeStruct((D, T), BF16)],
        compiler_params=_cparams(("parallel",)),
    )(y, r, k2, v, ga, o, gb, lng, lnb, rk)


def even_post_bwd(y, r, k2, v, ga, o, gb, lng, lnb, rk, dz):
    tm = 256

    def body(y_ref, r_ref, k2_ref, v_ref, ga_ref, o_ref, gb_ref, lng_ref, lnb_ref, rk_ref, dz_ref,
             dy_ref, dr_ref, dk2_ref, dv_ref, dga_ref, do_ref, dgb_ref, dlng_ref, dlnb_ref, drk_ref):
        bd = _head_blockdiag()
        _, vjp = jax.vjp(lambda *a: _post_math(*a, bd), y_ref[...], r_ref[...], k2_ref[...], v_ref[...], ga_ref[...],
                         o_ref[...], gb_ref[...], lng_ref[...], lnb_ref[...], rk_ref[...])
        dzv = dz_ref[...]
        dy, dr, dk2, dv, dga, do, dgb, dlng, dlnb, drk = vjp((dzv[:, 0:W], dzv[:, W:2 * W]))
        for ref, val in ((dy_ref, dy), (dr_ref, dr), (dk2_ref, dk2), (dv_ref, dv), (dga_ref, dga), (do_ref, do),
                         (dgb_ref, dgb)):
            ref[...] = val.astype(ref.dtype)

        @pl.when(pl.program_id(0) == 0)
        def _():
            for ref in (dlng_ref, dlnb_ref, drk_ref):
                ref[...] = jnp.zeros_like(ref)

        dlng_ref[...] += dlng
        dlnb_ref[...] += dlnb
        drk_ref[...] += drk

    vec = _const_spec((1, W))
    return pl.pallas_call(
        body, grid=(T // tm,), name="even_post_bwd",
        in_specs=[_row_spec(tm, W)] * 7 + [vec] * 3 + [_row_spec(tm, D)],
        out_specs=[_row_spec(tm, W)] * 7 + [vec] * 3,
        out_shape=[jax.ShapeDtypeStruct((T, W), dt) for dt in (F32, F32, F32, F32, BF16, F32, BF16)]
        + [jax.ShapeDtypeStruct((1, W), F32)] * 3,
        compiler_params=_cparams(("arbitrary",)),
    )(y, r, k2, v, ga, o, gb, lng, lnb, rk, dz)


PADSEQ = SEQ + LEFT * L
ATT_SCALE = 1.0 / math.sqrt(HD)


def _att_probs(q, kw, bias, c):
    valid = _iota2((1, BAND), 1) >= (LEFT - c) * L
    s = [jnp.where(valid, _bdot_nt(a, b) * ATT_SCALE + bias[h], NEG) for h, (a, b) in enumerate(zip(q, kw))]
    e = [jnp.exp(a - jnp.max(a, axis=-1, keepdims=True)) for a in s]
    return [a / jnp.sum(a, axis=-1, keepdims=True) for a in e]


def attention_fwd(q, kpad, vpad, bias):
    def body(q_ref, k_ref, v_ref, b_ref, o_ref):
        c = pl.program_id(1)
        start = pl.multiple_of(c * L, L)
        kw = _per_head(k_ref[pl.ds(start, BAND), :])
        vw = _per_head(v_ref[pl.ds(start, BAND), :])
        p = _att_probs(_per_head(q_ref[...].astype(BF16)), kw, b_ref[...], c)
        o_ref[...] = jnp.concatenate([_bdot(a, b) for a, b in zip(p, vw)], axis=-1)

    qblk = pl.BlockSpec((L, W), lambda b, c: (b * NC + c, 0))
    kblk = pl.BlockSpec((PADSEQ, W), lambda b, c: (b, 0))
    return pl.pallas_call(
        body, grid=(NSEQ, NC), name="attention_fwd",
        in_specs=[qblk, kblk, kblk, _const_spec((NH, L, BAND))],
        out_specs=qblk, out_shape=jax.ShapeDtypeStruct((T, W), F32),
        compiler_params=_cparams(("parallel", "arbitrary")),
    )(q, kpad, vpad, bias)


def attention_bwd(q, kpad, vpad, bias, do):
    def body(q_ref, k_ref, v_ref, b_ref, do_ref, dq_ref, dko_ref, dvo_ref, db_ref, dk_ref, dv_ref):
        b = pl.program_id(0)
        c = pl.program_id(1)

        @pl.when(c == 0)
        def _():
            dk_ref[...] = jnp.zeros_like(dk_ref)
            dv_ref[...] = jnp.zeros_like(dv_ref)

        @pl.when((c == 0) & (b == 0))
        def _():
            db_ref[...] = jnp.zeros_like(db_ref)

        start = pl.multiple_of(c * L, L)
        kw = _per_head(k_ref[pl.ds(start, BAND), :])
        vw = _per_head(v_ref[pl.ds(start, BAND), :])
        qs = _per_head(q_ref[...].astype(BF16))
        dos = _per_head(do_ref[...].astype(BF16))
        p = _att_probs(qs, kw, b_ref[...], c)
        dp = [_bdot_nt(a, b) for a, b in zip(dos, vw)]
        ds = [a * (d - jnp.sum(d * a, axis=-1, keepdims=True)) for a, d in zip(p, dp)]
        dss = [(a * ATT_SCALE).astype(BF16) for a in ds]
        dq_ref[...] = jnp.concatenate([_bdot(a, b) for a, b in zip(dss, kw)], axis=-1).astype(BF16)
        dk_ref[pl.ds(start, BAND), :] += jnp.concatenate([_bdot_tn(a, b) for a, b in zip(dss, qs)], axis=-1)
        dv_ref[pl.ds(start, BAND), :] += jnp.concatenate([_bdot_tn(a, b) for a, b in zip(p, dos)], axis=-1)
        for h in range(NH):
            db_ref[h] += ds[h]

        @pl.when(c == NC - 1)
        def _():
            dko_ref[...] = dk_ref[LEFT * L:, :].astype(BF16)
            dvo_ref[...] = dv_ref[LEFT * L:, :].astype(BF16)

    qblk = pl.BlockSpec((L, W), lambda b, c: (b * NC + c, 0))
    kblk = pl.BlockSpec((PADSEQ, W), lambda b, c: (b, 0))
    sblk = pl.BlockSpec((SEQ, W), lambda b, c: (b, 0))
    bblk = _const_spec((NH, L, BAND))
    return pl.pallas_call(
        body, grid=(NSEQ, NC), name="attention_bwd",
        in_specs=[qblk, kblk, kblk, bblk, qblk],
        out_specs=[qblk, sblk, sblk, bblk],
        out_shape=[jax.ShapeDtypeStruct((T, W), BF16), jax.ShapeDtypeStruct((T, W), BF16),
                   jax.ShapeDtypeStruct((T, W), BF16), jax.ShapeDtypeStruct((NH, L, BAND), F32)],
        scratch_shapes=[pltpu.VMEM((PADSEQ, W), F32), pltpu.VMEM((PADSEQ, W), F32)],
        compiler_params=_cparams(("arbitrary", "arbitrary"), VMEM_BIG),
    )(q, kpad, vpad, bias, do)


NTAB = 2 * CLIP + 1
EXT = BAND + L


def _ext_onehot():
    n = _iota2((EXT, NTAB), 0)
    m = _iota2((EXT, NTAB), 1)
    return (jnp.clip(BAND - 1 - n, -CLIP, CLIP) + CLIP == m).astype(F32)


def bias_expand(table):
    def body(t_ref, o_ref):
        ext = _hdot_nt(t_ref[...], _ext_onehot())
        for i in range(L):
            s = L - 1 - i
            o_ref[:, i, :] = (pltpu.roll(ext, EXT - s, 1) if s else ext)[:, :BAND]

    return pl.pallas_call(body, name="bias_expand", out_shape=jax.ShapeDtypeStruct((NH, L, BAND), F32))(table)


def bias_grad(dbias):
    def body(d_ref, o_ref):
        acc = jnp.zeros((NH, EXT), F32)
        zpad = jnp.zeros((NH, EXT - BAND), F32)
        for i in range(L):
            s = L - 1 - i
            row = jnp.concatenate([d_ref[:, i, :], zpad], axis=-1)
            acc = acc + (pltpu.roll(row, s, 1) if s else row)
        o_ref[...] = _hdot(acc, _ext_onehot())

    return pl.pallas_call(body, name="bias_grad", out_shape=jax.ShapeDtypeStruct((NH, NTAB), F32))(dbias)


def _group_cols(g):
    return slice(g * SGC, (g + 1) * SGC)


def _sg_norm(v, lng, lnb):
    gv = _gelu(v)
    gc = gv - jnp.mean(gv, axis=-1, keepdims=True)
    rstd = lax.rsqrt(jnp.mean(gc * gc, axis=-1, keepdims=True) + LN_EPS)
    xhat = gc * rstd
    return xhat, rstd, xhat * lng + lnb


def gmlp_fwd(u, v, gate, lng, lnb, wm_bf, sgb_t):
    def body(u_ref, v_ref, gt_ref, lng_ref, lnb_ref, wm_ref, sb_ref, z_ref, zt_ref):
        _, _, vln = _sg_norm(v_ref[...], lng_ref[...], lnb_ref[...])
        vlb = vln.astype(BF16)
        for g in range(NG):
            cs = _group_cols(g)
            sv = jnp.dot(wm_ref[g], vlb[:, cs], preferred_element_type=F32) + sb_ref[:, g:g + 1]
            zg = (_gelu(u_ref[:, cs]) * sv * _silu(gt_ref[:, cs])).astype(BF16)
            z_ref[:, cs] = zg
            zt_ref[cs, :] = zg.T

    return pl.pallas_call(
        body, grid=(T // SGC,), name="gmlp_fwd",
        in_specs=[_row_spec(SGC, D)] * 3 + [_const_spec((1, D))] * 2 + [_const_spec((NG, SGC, SGC)),
                                                                      _const_spec((SGC, NG))],
        out_specs=[_row_spec(SGC, D), _col_spec(D, SGC)],
        out_shape=[jax.ShapeDtypeStruct((T, D), BF16), jax.ShapeDtypeStruct((D, T), BF16)],
        compiler_params=_cparams(("parallel",)),
    )(u, v, gate, lng, lnb, wm_bf, sgb_t)


def gmlp_bwd(u, v, gate, lng, lnb, wm_bf, sgb_t, dz):
    def body(u_ref, v_ref, gt_ref, lng_ref, lnb_ref, wm_ref, sb_ref, dz_ref,
             du_ref, dv_ref, dgt_ref, dlng_ref, dlnb_ref, dwm_ref, dsb_ref):
        @pl.when(pl.program_id(0) == 0)
        def _():
            for ref in (dlng_ref, dlnb_ref, dwm_ref, dsb_ref):
                ref[...] = jnp.zeros_like(ref)

        vv = v_ref[...]
        xhat, rstd, vln = _sg_norm(vv, lng_ref[...], lnb_ref[...])
        vlb = vln.astype(BF16)
        dvln = []
        dsv_all = []
        for g in range(NG):
            cs = _group_cols(g)
            uu = u_ref[:, cs]
            gg = gt_ref[:, cs]
            dzz = dz_ref[:, cs]
            sv = jnp.dot(wm_ref[g], vlb[:, cs], preferred_element_type=F32) + sb_ref[:, g:g + 1]
            gu = _gelu(uu)
            sg = _silu(gg)
            dsv = dzz * gu * sg
            dgt_ref[:, cs] = (dzz * gu * sv * _dsilu(gg)).astype(BF16)
            du_ref[:, cs] = (dzz * sv * sg * _dgelu(uu)).astype(BF16)
            dsb16 = dsv.astype(BF16)
            dvln.append(lax.dot_general(wm_ref[g], dsb16, (((0,), (0,)), ((), ())), preferred_element_type=F32))
            dwm_ref[g] += lax.dot_general(dsb16, vlb[:, cs], (((1,), (1,)), ((), ())), preferred_element_type=F32)
            dsv_all.append(dsv)
        dvl = jnp.concatenate(dvln, axis=-1)
        dsv_cat = jnp.concatenate(dsv_all, axis=-1)
        sel = (_iota2((D, NG), 0) // SGC == _iota2((D, NG), 1)).astype(F32)
        dsb_ref[...] += _hdot(dsv_cat, sel)
        dlng_ref[...] += jnp.sum(dvl * xhat, axis=0, keepdims=True)
        dlnb_ref[...] += jnp.sum(dvl, axis=0, keepdims=True)
        dxh = dvl * lng_ref[...]
        dgv = rstd * (dxh - jnp.mean(dxh, axis=-1, keepdims=True)
                      - xhat * jnp.mean(dxh * xhat, axis=-1, keepdims=True))
        dv_ref[...] = (dgv * _dgelu(vv)).astype(BF16)

    return pl.pallas_call(
        body, grid=(T // SGC,), name="gmlp_bwd",
        in_specs=[_row_spec(SGC, D)] * 3 + [_const_spec((1, D))] * 2
        + [_const_spec((NG, SGC, SGC)), _const_spec((SGC, NG)), _row_spec(SGC, D)],
        out_specs=[_row_spec(SGC, D)] * 3 + [_const_spec((1, D))] * 2 + [_const_spec((NG, SGC, SGC)),
                                                                       _const_spec((SGC, NG))],
        out_shape=[jax.ShapeDtypeStruct((T, D), BF16)] * 3 + [jax.ShapeDtypeStruct((1, D), F32)] * 2
        + [jax.ShapeDtypeStruct((NG, SGC, SGC), F32), jax.ShapeDtypeStruct((SGC, NG), F32)],
        compiler_params=_cparams(("arbitrary",)),
    )(u, v, gate, lng, lnb, wm_bf, sgb_t, dz)


NCHIP = 4
NDEV = 8
ANY = pl.BlockSpec(memory_space=pl.ANY)


HBM = pl.BlockSpec(memory_space=pltpu.HBM)
SEM = pl.BlockSpec(memory_space=pltpu.SEMAPHORE)
EFFECT = pltpu.SideEffectType.DATAFLOW_SIDE_EFFECTING


def _peers(whole_mesh):
    x, y, c = lax.axis_index("x"), lax.axis_index("y"), lax.axis_index("c")
    if not whole_mesh:
        return [((px, py, c), 2 * px + py) for px, py in ((1 - x, y), (x, 1 - y), (1 - x, 1 - y))], 2 * x + y
    out = []
    for j in range(1, NDEV):
        px, py, pc = x ^ (j >> 2), y ^ ((j >> 1) & 1), c ^ (j & 1)
        out.append(((px, py, pc), 4 * px + 2 * py + pc))
    return out, 4 * x + 2 * y + c


def _send_copies(src, land, send, recv, scatter, whole_mesh, starting):
    peers, me = _peers(whole_mesh)
    copies = []
    for t in range(len(src)):
        for j, (dev, slot) in enumerate(peers):
            k = t * len(peers) + j
            copies.append(pltpu.make_async_remote_copy(
                src_ref=src[t].at[slot] if scatter else src[t], dst_ref=land[t].at[me if starting else slot],
                send_sem=send.at[k], recv_sem=recv.at[k], device_id=dev, device_id_type=MESH))
    return copies


def send_start(srcs, lands, scatter, whole_mesh, name):
    n = len(srcs)
    nsem = n * (NDEV - 1 if whole_mesh else NCHIP - 1)

    def body(*refs):
        for cp in _send_copies(refs[:n], refs[n:2 * n], refs[2 * n], refs[2 * n + 1], scatter, whole_mesh, True):
            cp.start()
        refs[-1][...] = jnp.zeros_like(refs[-1])

    arrs = list(srcs) + list(lands)
    out = pl.pallas_call(
        body, name=name,
        out_shape=(pltpu.SemaphoreType.DMA((nsem,)), pltpu.SemaphoreType.DMA((nsem,)),
                   *[pltpu.HBM(a.shape, a.dtype) for a in arrs], jax.ShapeDtypeStruct((8, 128), F32)),
        in_specs=[HBM] * (2 * n), out_specs=(SEM, SEM, *[HBM] * (2 * n), pl.BlockSpec(memory_space=pltpu.VMEM)),
        input_output_aliases={i: 2 + i for i in range(2 * n)},
        compiler_params=pltpu.CompilerParams(has_side_effects=EFFECT),
    )(*[pltpu.with_memory_space_constraint(a, pltpu.HBM) for a in arrs])
    return out[0], out[1], list(out[2:2 + n]), list(out[2 + n:2 + 2 * n]), out[-1]


def send_wait(started, after, scatter, whole_mesh, name):
    send, recv, srcs, lands, _ = started
    n = len(srcs)

    def body(*refs):
        for cp in _send_copies(refs[:n], refs[n:2 * n], refs[2 * n], refs[2 * n + 1], scatter, whole_mesh, False):
            cp.wait_send()
            cp.wait_recv()

    arrs = list(srcs) + list(lands)
    out = pl.pallas_call(
        body, name=name, out_shape=tuple(pltpu.HBM(a.shape, a.dtype) for a in arrs),
        in_specs=[HBM] * (2 * n) + [SEM, SEM, ANY], out_specs=tuple([HBM] * (2 * n)),
        input_output_aliases={i: i for i in range(2 * n)},
        compiler_params=pltpu.CompilerParams(has_side_effects=EFFECT),
    )(*arrs, send, recv, after)
    return list(out[n:])


def exchange_c(arrs, name):
    n = len(arrs)

    def body(*refs):
        ins, outs = refs[:n], refs[n:2 * n]
        send, recv = refs[2 * n:]
        sibling = (lax.axis_index("x"), lax.axis_index("y"), 1 - lax.axis_index("c"))
        copies = [pltpu.make_async_remote_copy(src_ref=ins[t], dst_ref=outs[t], send_sem=send.at[t], recv_sem=recv.at[t],
                                               device_id=sibling, device_id_type=MESH) for t in range(n)]
        for cp in copies:
            cp.start()
        for cp in copies:
            cp.wait()

    return pl.pallas_call(
        body, name=name, in_specs=[ANY] * n, out_specs=[ANY] * n,
        out_shape=[jax.ShapeDtypeStruct(a.shape, a.dtype) for a in arrs],
        scratch_shapes=[pltpu.SemaphoreType.DMA((n,)), pltpu.SemaphoreType.DMA((n,))],
    )(*arrs)


def gather_weights(arrs, split):
    n = len(arrs)

    def body(*refs):
        ins, outs = refs[:n], refs[n:2 * n]
        send1, recv1, send2, recv2, loc = refs[2 * n:]
        x, y, c = lax.axis_index("x"), lax.axis_index("y"), lax.axis_index("c")
        me = 2 * x + y
        sibling = (x, y, 1 - c)
        peers = [(1 - x, y), (x, 1 - y), (1 - x, 1 - y)]

        def rows_of(t, core):
            half = arrs[t].shape[0] // 2
            return pl.ds(core * half, half)

        def part(ref, t, core):
            return ref.at[rows_of(t, core)] if split[t] else ref

        local = [pltpu.make_async_copy(ins[t], outs[t].at[me], loc.at[t]) for t in range(n)]
        for cp in local:
            cp.start()
        first = []
        for t in range(n):
            for j, (px, py) in enumerate(peers):
                first.append(pltpu.make_async_remote_copy(
                    src_ref=part(ins[t], t, c), dst_ref=part(outs[t].at[me], t, c), send_sem=send1.at[t, j],
                    recv_sem=recv1.at[t, j], device_id=(px, py, c), device_id_type=MESH))
        for cp in first:
            cp.start()
        passed = []
        for t in range(n):
            for j, (px, py) in enumerate(peers):
                landed = part(outs[t].at[2 * px + py], t, c)
                pltpu.make_async_remote_copy(
                    src_ref=landed, dst_ref=landed, send_sem=send1.at[t, j], recv_sem=recv1.at[t, j],
                    device_id=(x, y, c), device_id_type=MESH).wait_recv()
                if split[t]:
                    cp = pltpu.make_async_remote_copy(
                        src_ref=landed, dst_ref=landed, send_sem=send2.at[t, j], recv_sem=recv2.at[t, j],
                        device_id=sibling, device_id_type=MESH)
                    cp.start()
                    passed.append(cp)
        for t in range(n):
            for j, (px, py) in enumerate(peers):
                if split[t]:
                    other = part(outs[t].at[2 * px + py], t, 1 - c)
                    pltpu.make_async_remote_copy(
                        src_ref=other, dst_ref=other, send_sem=send2.at[t, j], recv_sem=recv2.at[t, j],
                        device_id=(x, y, c), device_id_type=MESH).wait_recv()
        for cp in first + passed:
            cp.wait_send()
        for cp in local:
            cp.wait()

    return pl.pallas_call(
        body, name="gather_weights", in_specs=[ANY] * n, out_specs=[ANY] * n,
        out_shape=[jax.ShapeDtypeStruct((NCHIP,) + a.shape, a.dtype) for a in arrs],
        scratch_shapes=[pltpu.SemaphoreType.DMA((n, 3))] * 4 + [pltpu.SemaphoreType.DMA((n,))],
    )(*arrs)


def _adam_math(g, w, m, v):
    m = ADAM_B1 * m + (1.0 - ADAM_B1) * g
    v = ADAM_B2 * v + (1.0 - ADAM_B2) * (g * g)
    m_hat = m / (1.0 - ADAM_B1 ** ADAM_STEP)
    v_hat = v / (1.0 - ADAM_B2 ** ADAM_STEP)
    delta = -ADAM_LR * (m_hat / (jnp.sqrt(v_hat) + ADAM_EPS) + ADAM_WD * w)
    return delta, m, v


def _rows_tile(rows):
    return rows if rows <= 256 else 256


def sum_chips(own, parts, name):
    _, rows, cols = parts.shape
    tr = _rows_tile(rows)

    def body(own_ref, p_ref, o_ref):
        acc = own_ref[...]
        for s in range(NCHIP):
            acc = acc + p_ref[s].astype(F32)
        o_ref[...] = acc

    return pl.pallas_call(
        body, grid=(rows // tr,), name=name,
        in_specs=[pl.BlockSpec((tr, cols), lambda i: (i, 0)), pl.BlockSpec((NCHIP, tr, cols), lambda i: (0, i, 0))],
        out_specs=pl.BlockSpec((tr, cols), lambda i: (i, 0)),
        out_shape=jax.ShapeDtypeStruct((rows, cols), F32),
        compiler_params=_cparams(("parallel",)),
    )(own, parts)


def adam_shard(p_mine, p_sib, w, m, v, name):
    rows, cols = w.shape
    tr = _rows_tile(rows)

    def body(a_ref, b_ref, w_ref, m_ref, v_ref, g_ref, d_ref, mo_ref, vo_ref):
        g = a_ref[...] + b_ref[...]
        g_ref[...] = g
        d_ref[...], mo_ref[...], vo_ref[...] = _adam_math(g, w_ref[...], m_ref[...], v_ref[...])

    spec = pl.BlockSpec((tr, cols), lambda i: (i, 0))
    return pl.pallas_call(
        body, grid=(rows // tr,), name=name, in_specs=[spec] * 5, out_specs=[spec] * 4,
        out_shape=[jax.ShapeDtypeStruct((rows, cols), F32)] * 4,
        compiler_params=_cparams(("parallel",)),
    )(p_mine, p_sib, w, m, v)


def adam_replicated(parts, w, m, v, name):
    rows = w.shape[0]

    def body(p_ref, w_ref, m_ref, v_ref, g_ref, d_ref, mo_ref, vo_ref):
        g = p_ref[0]
        for d in range(1, NDEV):
            g = g + p_ref[d]
        g_ref[...] = g
        d_ref[...], mo_ref[...], vo_ref[...] = _adam_math(g, w_ref[...], m_ref[...], v_ref[...])

    return pl.pallas_call(
        body, name=name, out_shape=[jax.ShapeDtypeStruct((rows, 128), F32)] * 4,
    )(parts, w, m, v)


def _pack(arrs):
    pieces = []
    for a in arrs:
        flat = a.reshape(-1)
        pad = (-flat.shape[0]) % 128
        pieces.append(jnp.pad(flat, (0, pad)) if pad else flat)
    flat = jnp.concatenate(pieces)
    pad = (-flat.shape[0]) % 1024
    return jnp.pad(flat, (0, pad)).reshape(-1, 128)


def _unpack(buf, shapes):
    flat = buf.reshape(-1)
    out = []
    o = 0
    for s in shapes:
        n = int(np.prod(s))
        out.append(flat[o:o + n].reshape(s))
        o += n + (-n) % 128
    return out


EVEN_SPLITS = (SHIFT, W, W, W, W, W)
ODD_SPLITS = (D, D, D)


def _cols_to_chips(a):
    rows, cols = a.shape
    return a.reshape(rows, NCHIP, cols // NCHIP).transpose(1, 0, 2)


def _chips_to_cols(a):
    _, rows, n = a.shape
    return a.transpose(1, 0, 2).reshape(rows, NCHIP * n)


def kernel(x, norm_g, w_in_e, shift_mu, rw_w0, rw_w2, rw_a0, rw_a2, rw_kk, rw_ka, rw_rk, rw_lnx_g, rw_lnx_b, att_bias, w_out_e, w_in_o, sg_ln_g, sg_ln_b, sg_w, sg_b, w_out_o, final_g, loss_target, m_norm_g, m_w_in_e, m_shift_mu, m_rw_w0, m_rw_w2, m_rw_a0, m_rw_a2, m_rw_kk, m_rw_ka, m_rw_rk, m_rw_lnx_g, m_rw_lnx_b, m_att_bias, m_w_out_e, m_w_in_o, m_sg_ln_g, m_sg_ln_b, m_sg_w, m_sg_b, m_w_out_o, m_final_g, v_norm_g, v_w_in_e, v_shift_mu, v_rw_w0, v_rw_w2, v_rw_a0, v_rw_a2, v_rw_kk, v_rw_ka, v_rw_rk, v_rw_lnx_g, v_rw_lnx_b, v_att_bias, v_w_out_e, v_w_in_o, v_sg_ln_g, v_sg_ln_b, v_sg_w, v_sg_b, v_w_out_o, v_final_g):
    x2 = x.reshape(T, D)
    tgt = loss_target.reshape(T, D)

    my_chip = 2 * lax.axis_index("x") + lax.axis_index("y")
    gathered = gather_weights(
        [w_in_e[0].astype(BF16), jnp.concatenate([rw_w2[0], rw_a2[0]], axis=0),
         jnp.concatenate([sg_ln_g, sg_ln_b], axis=0)], [True, True, False])
    wie = _chips_to_cols(gathered[0])
    w2 = _chips_to_cols(gathered[1][:, :LORA])
    a2 = _chips_to_cols(gathered[1][:, LORA:])
    sglg = _chips_to_cols(gathered[2][:, 0:1])
    sglb = _chips_to_cols(gathered[2][:, 1:2])

    late = [w_out_e[0].astype(BF16), w_in_o[0].astype(BF16), w_out_o[0].astype(BF16)]
    late_started = send_start(late, [jnp.broadcast_to(a[None], (NCHIP,) + a.shape) for a in late], False, False,
                              "late_weights_start")

    def late_weights(after):
        woe, wio, woo = send_wait(late_started, after, False, False, "late_weights_wait")
        return woe.reshape(D, D), _chips_to_cols(wio), woo.reshape(D, D)

    def scatter_start(grads, name):
        srcs = [g_.astype(BF16) if g_.shape[-1] >= W else g_ for g_ in grads]
        return send_start(srcs, [jnp.zeros_like(s) for s in srcs], True, False, name)

    def own_block(g_):
        return lax.dynamic_index_in_dim(g_, my_chip, axis=0, keepdims=False)

    started = {}

    def on_odd_grads(d_woo, d_wio):
        blocks = [d_woo.reshape(NCHIP, D // NCHIP, D), d_wio]
        started["odd"] = (scatter_start(blocks, "odd_grads_start"), [own_block(b) for b in blocks])
        return started["odd"][0][-1]

    def on_even_grads(big_g):
        d_wie, d_woe, _, _, d_w2, d_a2, d_sglg, d_sglb = big_g
        blocks = [d_wie, d_woe.reshape(NCHIP, D // NCHIP, D), _cols_to_chips(d_w2), _cols_to_chips(d_a2),
                  _cols_to_chips(d_sglg), _cols_to_chips(d_sglb)]
        started["even"] = (scatter_start(blocks, "even_grads_start"), [own_block(b) for b in blocks])
        return started["even"][0][-1]

    def on_small_grads(layer, grads):
        mine = _pack(grads)
        started[layer + "_small"] = send_start([mine], [jnp.broadcast_to(mine[None], (NDEV,) + mine.shape)], False,
                                               True, layer + "_small_grads_start")
        return started[layer + "_small"][-1]

    loss_part, dx, _, _ = _local_step(
        x2, tgt, wie, late_weights, w2, a2, sglg, sglb, norm_g, shift_mu, rw_w0, rw_a0, rw_kk, rw_ka, rw_rk,
        rw_lnx_g, rw_lnx_b, att_bias, sg_w, sg_b, final_g, first_after=late_started[-1], on_odd_grads=on_odd_grads,
        on_even_grads=on_even_grads, on_small_grads=on_small_grads)
    loss = lax.psum(loss_part, ("x", "y", "c"))
    even_started, even_own = started["even"]

    wmv = {"w_in_e": (w_in_e[0], m_w_in_e[0], v_w_in_e[0]), "w_out_e": (w_out_e[0], m_w_out_e[0], v_w_out_e[0]),
           "w_in_o": (w_in_o[0], m_w_in_o[0], v_w_in_o[0]), "w_out_o": (w_out_o[0], m_w_out_o[0], v_w_out_o[0]),
           "rw_w2": (rw_w2[0], m_rw_w2[0], v_rw_w2[0]), "rw_a2": (rw_a2[0], m_rw_a2[0], v_rw_a2[0]),
           "sg_ln_g": (sg_ln_g, m_sg_ln_g, v_sg_ln_g), "sg_ln_b": (sg_ln_b, m_sg_ln_b, v_sg_ln_b)}
    sharded = {}

    def finish(names, own, landed, tag):
        partial = [sum_chips(o_, p_, "sum_" + nm) for o_, p_, nm in zip(own, landed, names)]
        from_sibling = exchange_c(partial, "swap_partials_" + tag)
        for nm, mine, sib in zip(names, partial, from_sibling):
            w_, m_, v_ = wmv[nm]
            res = adam_shard(mine, sib, w_, m_, v_, "adam_" + nm)
            lead = nm not in ("sg_ln_g", "sg_ln_b")
            sharded[nm] = [a[None] if lead else a for a in res]
        return partial[0]

    odd_started, odd_own = started["odd"]
    odd_landed = send_wait(odd_started, started["even_small"][-1], True, False, "odd_grads_wait")
    done = finish(["w_out_o", "w_in_o"], odd_own, odd_landed, "odd")

    groups = {
        "odd": (["sg_w", "sg_b", "final_g", "norm_g1"], [sg_w, sg_b, final_g, norm_g[1:2]],
                [m_sg_w, m_sg_b, m_final_g, m_norm_g[1:2]], [v_sg_w, v_sg_b, v_final_g, v_norm_g[1:2]]),
        "even": (["norm_g0", "shift_mu", "rw_w0", "rw_a0", "rw_kk", "rw_ka", "rw_rk", "rw_lnx_g", "rw_lnx_b",
                  "att_bias"],
                 [norm_g[0:1], shift_mu, rw_w0, rw_a0, rw_kk, rw_ka, rw_rk, rw_lnx_g, rw_lnx_b, att_bias],
                 [m_norm_g[0:1], m_shift_mu, m_rw_w0, m_rw_a0, m_rw_kk, m_rw_ka, m_rw_rk, m_rw_lnx_g, m_rw_lnx_b,
                  m_att_bias],
                 [v_norm_g[0:1], v_shift_mu, v_rw_w0, v_rw_a0, v_rw_kk, v_rw_ka, v_rw_rk, v_rw_lnx_g, v_rw_lnx_b,
                  v_att_bias]),
    }
    rep = {}
    for layer in ("odd", "even"):
        nms, ws, ms_, vs_ = groups[layer]
        (gathered_g,) = send_wait(started[layer + "_small"], done, False, True, layer + "_small_grads_wait")
        rep_out = adam_replicated(gathered_g, _pack(ws), _pack(ms_), _pack(vs_), "adam_" + layer + "_small")
        done = rep_out[0]
        for nm in nms:
            rep[nm] = []
        for buf in rep_out:
            for nm, a in zip(nms, _unpack(buf, [w_.shape for w_ in ws])):
                rep[nm].append(a)
    rep["norm_g"] = [jnp.concatenate([a, b], axis=0) for a, b in zip(rep["norm_g0"], rep["norm_g1"])]
    even_landed = send_wait(even_started, done, True, False, "even_grads_wait")
    finish(["w_in_e", "w_out_e", "rw_w2", "rw_a2", "sg_ln_g", "sg_ln_b"], even_own, even_landed, "even")

    order = ["norm_g", "w_in_e", "shift_mu", "rw_w0", "rw_w2", "rw_a0", "rw_a2", "rw_kk", "rw_ka", "rw_rk",
             "rw_lnx_g", "rw_lnx_b", "att_bias", "w_out_e", "w_in_o", "sg_ln_g", "sg_ln_b", "sg_w", "sg_b",
             "w_out_o", "final_g"]
    results = {**sharded, **rep}
    outs = [loss, dx.reshape(NSEQ, SEQ, D)]
    for kind in range(4):
        outs += [results[nm][kind] for nm in order]
    return tuple(outs)


def _local_step(x2, tgt, wie, late_weights, w2, a2, sglg, sglb, norm_g, shift_mu, rw_w0, rw_a0, rw_kk, rw_ka, rw_rk,
                rw_lnx_g, rw_lnx_b, att_bias, sg_w, sg_b, final_g, first_after=None, on_odd_grads=None,
                on_even_grads=None, on_small_grads=None):
    zl = jnp.zeros((LORA, W), F32)
    w2x = jnp.concatenate([w2, zl], axis=0)
    a2x = jnp.concatenate([zl, a2], axis=0)
    rk = rw_rk.reshape(1, W)
    pos = np.arange(SGC)
    sg_mask = jnp.asarray(((pos[None, :] // L) <= (pos[:, None] // L)).astype(np.float32))
    wm = (sg_w[0] * sg_mask[None]).astype(BF16)
    sgb_t = sg_b[0].T

    xn0, ps, ga, q, kb, vb, gb = ln_in_proj(x2, norm_g[0:1], wie, EVEN_SPLITS, "in_proj_even", after=first_after)
    r, lw, k2, v, aa, bb = even_prep(ps, shift_mu, rw_w0, w2x, rw_a0, a2x, rw_kk, rw_ka)
    y, hs, ms, ts = rwkv_fwd(r, lw, k2, v, aa, bb)
    bias = bias_expand(att_bias[0])

    def padded(a):
        return jnp.pad(a.astype(BF16).reshape(NSEQ, SEQ, W), ((0, 0), (LEFT * L, 0), (0, 0))).reshape(NSEQ * PADSEQ, W)

    kpad, vpad = padded(kb), padded(vb)
    o = attention_fwd(q, kpad, vpad, bias)
    z, zt = even_post(y, r, k2, v, ga, o, gb, rw_lnx_g, rw_lnx_b, rk)
    woe, wio, woo = late_weights(z)
    h1 = out_proj(x2, z, woe, "out_proj_even")
    xn1, u, vv, gt = ln_in_proj(h1, norm_g[1:2], wio, ODD_SPLITS, "in_proj_odd")
    z2, z2t = gmlp_fwd(u, vv, gt, sglg, sglb, wm, sgb_t)
    h2 = out_proj(h1, z2, woo, "out_proj_odd")
    dh2, loss_part, d_final_g = final_loss(h2, final_g[None], tgt)

    dz2, d_woo = out_proj_bwd(dh2, z2t, woo, "out_proj_odd_bwd")
    du, dvv, dgt, d_sglg, d_sglb, d_wm, d_sgb_t = gmlp_bwd(u, vv, gt, sglg, sglb, wm, sgb_t, dz2)
    dp_odd = [du, dvv, dgt]
    d_wio = matmul_acc_chips(xn1, dp_odd, "in_proj_odd_dw")
    token = on_odd_grads(d_woo, d_wio) if on_odd_grads else None
    dh1, d_g1 = in_proj_bwd_x(h1, norm_g[1:2], wio, dp_odd, dh2, "in_proj_odd_bwd", after=token)
    odd_small = [d_wm * sg_mask[None], d_sgb_t.T, d_final_g, d_g1]
    token = on_small_grads("odd", odd_small) if on_small_grads else None
    dz, d_woe = out_proj_bwd(dh1, zt, woe, "out_proj_even_bwd", after=token)
    dy, dr2, dk22, dv2, dga, do, dgb, d_lng, d_lnb, d_rk = even_post_bwd(
        y, r, k2, v, ga, o, gb, rw_lnx_g, rw_lnx_b, rk, dz)
    dq, dkb, dvb, dbias = attention_bwd(q, kpad, vpad, bias, do)
    d_att_bias = bias_grad(dbias)
    dr, dlw, dk2, dv, daa, dbb = rwkv_bwd(r, lw, k2, v, aa, bb, hs, ms, ts, dy)
    dps, d_mu, d_w0, d_w2x, d_a0, d_a2x, d_kk, d_ka = even_prep_bwd(
        ps, shift_mu, rw_w0, w2x, rw_a0, a2x, rw_kk, rw_ka, dr, dlw, dk2, dv, daa, dbb, dr2, dk22, dv2)
    dp_even = [dps, dga, dq, dkb, dvb, dgb]
    d_wie = matmul_acc_chips(xn0, dp_even, "in_proj_even_dw")
    big_g = (d_wie, d_woe, d_wio, d_woo, d_w2x[:LORA], d_a2x[LORA:], d_sglg, d_sglb)
    token = on_even_grads(big_g) if on_even_grads else None
    dx, d_g0 = in_proj_bwd_x(x2, norm_g[0:1], wie, dp_even, dh1, "in_proj_even_bwd", after=token)
    even_small = [d_g0, d_mu, d_w0, d_a0, d_kk, d_ka, d_rk, d_lng, d_lnb, d_att_bias]
    if on_small_grads:
        on_small_grads("even", even_small)
    rep_g = [jnp.concatenate([d_g0, d_g1], axis=0)] + even_small[1:] + odd_small[:3]
    return loss_part[0, 0], dx, big_g, rep_g
```

```python
import functools
import math

import jax
import jax.numpy as jnp
import numpy as np
from jax import lax
from jax.experimental import pallas as pl
from jax.experimental.pallas import tpu as pltpu

F32 = jnp.float32
BF16 = jnp.bfloat16
HI = lax.Precision.HIGHEST

D = 1024
SEQ = 2048
NSEQ = 2
T = NSEQ * SEQ
HD = 64
NH = 8
W = 512
SHIFT = 1664
LORA = 64
EVEN_IN = 4224
ODD_IN = 3072
L = 64
NC = SEQ // L
LEFT = 8
BAND = (LEFT + 1) * L
CLIP = 128
SGC = 128
NG = 8
RMS_EPS = 1e-6
LN_EPS = 1e-5
GN_EPS = 64e-5
NEG = -1e30
VMEM_BIG = 56 * 1024 * 1024

ADAM_LR = 0.001
ADAM_B1 = 0.9
ADAM_B2 = 0.999
ADAM_EPS = 1e-08
ADAM_WD = 0.01
ADAM_STEP = 10

MESH = pl.DeviceIdType.MESH


def _bdot(a, b):
    return jnp.dot(a.astype(BF16), b.astype(BF16), preferred_element_type=F32)


def _bdot_nt(a, b):
    return lax.dot_general(a.astype(BF16), b.astype(BF16), (((1,), (1,)), ((), ())), preferred_element_type=F32)


def _bdot_tn(a, b):
    return lax.dot_general(a.astype(BF16), b.astype(BF16), (((0,), (0,)), ((), ())), preferred_element_type=F32)


def _hdot(a, b):
    return jnp.dot(a, b, precision=HI, preferred_element_type=F32)


def _hdot_nt(a, b):
    return lax.dot_general(a, b, (((1,), (1,)), ((), ())), precision=HI, preferred_element_type=F32)


def _hdot_tn(a, b):
    return lax.dot_general(a, b, (((0,), (0,)), ((), ())), precision=HI, preferred_element_type=F32)


def _iota2(shape, dim):
    return lax.broadcasted_iota(jnp.int32, shape, dim)


def _head_blockdiag():
    r = _iota2((W, W), 0) // HD
    c = _iota2((W, W), 1) // HD
    return (r == c).astype(BF16)


def _headsum_impl(x, bd):
    hi = x.astype(BF16)
    mid = (x - hi.astype(F32)).astype(BF16)
    return jnp.dot(hi, bd, preferred_element_type=F32) + jnp.dot(mid, bd, preferred_element_type=F32)


@jax.custom_vjp
def _headsum(x, bd):
    return _headsum_impl(x, bd)


def _headsum_fwd(x, bd):
    return _headsum_impl(x, bd), bd


def _headsum_bwd(bd, ct):
    return _headsum_impl(ct, bd), None


_headsum.defvjp(_headsum_fwd, _headsum_bwd)


def _silu(x):
    return x * jax.nn.sigmoid(x)


def _dsilu(x):
    s = jax.nn.sigmoid(x)
    return s * (1.0 + x * (1.0 - s))


_GELU_C = math.sqrt(2.0 / math.pi)


def _gelu(x):
    return 0.5 * x * (1.0 + jnp.tanh(_GELU_C * (x + 0.044715 * (x * x * x))))


def _dgelu(x):
    t = jnp.tanh(_GELU_C * (x + 0.044715 * (x * x * x)))
    return 0.5 * (1.0 + t) + 0.5 * x * (1.0 - t * t) * _GELU_C * (1.0 + 3.0 * 0.044715 * x * x)


def _softplus(x):
    return jnp.maximum(x, 0.0) + jnp.log(1.0 + jnp.exp(-jnp.abs(x)))


def _cparams(sem, vmem=None):
    return pltpu.CompilerParams(dimension_semantics=sem, vmem_limit_bytes=vmem)


def _row_spec(tm, width):
    return pl.BlockSpec((tm, width), lambda i: (i, 0))


def _col_spec(height, tm):
    return pl.BlockSpec((height, tm), lambda i: (0, i))


def _const_spec(shape):
    nd = len(shape)
    return pl.BlockSpec(shape, lambda *_: (0,) * nd)


def ln_in_proj(x, g, w_bf, splits, name, after=None):
    n = w_bf.shape[1]
    tm = 256
    spans = []
    o = 0
    for s in splits:
        spans.append((o, o + s))
        o += s
    assert o == n
    extra_specs, extra = _after_operand(after)

    def body(x_ref, g_ref, w_ref, *rest):
        xn_ref, outs = rest[len(extra)], rest[len(extra) + 1:]
        xv = x_ref[...]
        rstd = lax.rsqrt(jnp.mean(xv * xv, axis=-1, keepdims=True) + RMS_EPS)
        xn = (xv * rstd * g_ref[...]).astype(BF16)
        xn_ref[...] = xn.T
        p = jnp.dot(xn, w_ref[...], preferred_element_type=F32)
        for o_ref, (a, b) in zip(outs, spans):
            o_ref[...] = p[:, a:b]

    return pl.pallas_call(
        body, grid=(T // tm,), name=name,
        in_specs=[_row_spec(tm, D), _const_spec((1, D)), _const_spec((D, n))] + extra_specs,
        out_specs=[_col_spec(D, tm)] + [_row_spec(tm, s) for s in splits],
        out_shape=[jax.ShapeDtypeStruct((D, T), BF16)] + [jax.ShapeDtypeStruct((T, s), F32) for s in splits],
        compiler_params=_cparams(("parallel",), VMEM_BIG),
    )(x, g, w_bf, *extra)


def in_proj_bwd_x(x, g, w_bf, dps, dres, name, after=None):
    n = w_bf.shape[1]
    tm = 256
    widths = [d.shape[1] for d in dps]
    extra_specs, extra = _after_operand(after)

    def body(x_ref, g_ref, w_ref, dres_ref, *rest):
        dp_refs = rest[:len(widths)]
        dx_ref, dg_ref = rest[-2:]
        dp = jnp.concatenate([r[...] for r in dp_refs], axis=-1)
        dxn = lax.dot_general(dp, w_ref[...], (((1,), (1,)), ((), ())), preferred_element_type=F32)
        xv = x_ref[...]
        rstd = lax.rsqrt(jnp.mean(xv * xv, axis=-1, keepdims=True) + RMS_EPS)
        xhat = xv * rstd
        dgp = jnp.sum(dxn * xhat, axis=0, keepdims=True)

        @pl.when(pl.program_id(0) == 0)
        def _():
            dg_ref[...] = jnp.zeros_like(dg_ref)

        dg_ref[...] += dgp
        dxh = dxn * g_ref[...]
        dx_ref[...] = dres_ref[...] + rstd * (dxh - xhat * jnp.mean(dxh * xhat, axis=-1, keepdims=True))

    return pl.pallas_call(
        body, grid=(T // tm,), name=name,
        in_specs=[_row_spec(tm, D), _const_spec((1, D)), _const_spec((D, n)), _row_spec(tm, D)]
        + [_row_spec(tm, s) for s in widths] + extra_specs,
        out_specs=[_row_spec(tm, D), _const_spec((1, D))],
        out_shape=[jax.ShapeDtypeStruct((T, D), F32), jax.ShapeDtypeStruct((1, D), F32)],
        compiler_params=_cparams(("arbitrary",), VMEM_BIG),
    )(x, g, w_bf, dres, *dps, *extra)


def _after_operand(after):
    return ([ANY], [after]) if after is not None else ([], [])


def matmul_acc_chips(at_bf, pieces, name, after=None):
    k = at_bf.shape[0]
    widths = [p.shape[1] for p in pieces]
    nb = sum(widths) // NCHIP
    tm = 512
    extra_specs, extra = _after_operand(after)

    def body(a_ref, *rest):
        o_ref = rest[-1]

        @pl.when(pl.program_id(0) == 0)
        def _():
            o_ref[...] = jnp.zeros_like(o_ref)

        a = a_ref[...]
        b = jnp.concatenate([r[...] for r in rest[:len(widths)]], axis=-1)
        for s in range(NCHIP):
            o_ref[s] += jnp.dot(a, b[:, s * nb:(s + 1) * nb], preferred_element_type=F32)

    return pl.pallas_call(
        body, grid=(T // tm,), name=name,
        in_specs=[_col_spec(k, tm)] + [_row_spec(tm, w_) for w_ in widths] + extra_specs,
        out_specs=_const_spec((NCHIP, k, nb)),
        out_shape=jax.ShapeDtypeStruct((NCHIP, k, nb), F32),
        compiler_params=_cparams(("arbitrary",), VMEM_BIG),
    )(at_bf, *pieces, *extra)


def out_proj(h, z_bf, w_bf, name):
    tm = 256

    def body(h_ref, z_ref, w_ref, o_ref):
        o_ref[...] = h_ref[...] + jnp.dot(z_ref[...], w_ref[...], preferred_element_type=F32)

    return pl.pallas_call(
        body, grid=(T // tm,), name=name,
        in_specs=[_row_spec(tm, D), _row_spec(tm, D), _const_spec((D, D))],
        out_specs=_row_spec(tm, D), out_shape=jax.ShapeDtypeStruct((T, D), F32),
        compiler_params=_cparams(("parallel",)),
    )(h, z_bf, w_bf)


def out_proj_bwd(dh, zt_bf, w_bf, name, after=None):
    tm = 256
    extra_specs, extra = _after_operand(after)

    def body(dh_ref, zt_ref, w_ref, *rest):
        dz_ref, dw_ref = rest[-2:]
        dhb = dh_ref[...].astype(BF16)
        dz_ref[...] = lax.dot_general(dhb, w_ref[...], (((1,), (1,)), ((), ())), preferred_element_type=F32)

        @pl.when(pl.program_id(0) == 0)
        def _():
            dw_ref[...] = jnp.zeros_like(dw_ref)

        dw_ref[...] += jnp.dot(zt_ref[...], dhb, preferred_element_type=F32)

    return pl.pallas_call(
        body, grid=(T // tm,), name=name,
        in_specs=[_row_spec(tm, D), _col_spec(D, tm), _const_spec((D, D))] + extra_specs,
        out_specs=[_row_spec(tm, D), _const_spec((D, D))],
        out_shape=[jax.ShapeDtypeStruct((T, D), F32), jax.ShapeDtypeStruct((D, D), F32)],
        compiler_params=_cparams(("arbitrary",)),
    )(dh, zt_bf, w_bf, *extra)


def final_loss(h, g, target):
    tm = 256

    def body(h_ref, g_ref, t_ref, dh_ref, loss_ref, dg_ref):
        xv = h_ref[...]
        rstd = lax.rsqrt(jnp.mean(xv * xv, axis=-1, keepdims=True) + RMS_EPS)
        xhat = xv * rstd
        err = xhat * g_ref[...] - t_ref[...]
        part = 0.5 * jnp.sum(jnp.mean(err * err, axis=-1, keepdims=True), axis=0, keepdims=True)
        dout = err * (1.0 / D)

        @pl.when(pl.program_id(0) == 0)
        def _():
            loss_ref[...] = jnp.zeros_like(loss_ref)
            dg_ref[...] = jnp.zeros_like(dg_ref)

        loss_ref[...] += jnp.broadcast_to(part, loss_ref.shape)
        dg_ref[...] += jnp.sum(dout * xhat, axis=0, keepdims=True)
        dxh = dout * g_ref[...]
        dh_ref[...] = rstd * (dxh - xhat * jnp.mean(dxh * xhat, axis=-1, keepdims=True))

    return pl.pallas_call(
        body, grid=(T // tm,), name="final_loss",
        in_specs=[_row_spec(tm, D), _const_spec((1, D)), _row_spec(tm, D)],
        out_specs=[_row_spec(tm, D), _const_spec((8, 128)), _const_spec((1, D))],
        out_shape=[jax.ShapeDtypeStruct((T, D), F32), jax.ShapeDtypeStruct((8, 128), F32),
                   jax.ShapeDtypeStruct((1, D), F32)],
        compiler_params=_cparams(("arbitrary",)),
    )(h, g, target)


PREP_TM = 256
PREP_NB = SEQ // PREP_TM


def _prep_elem(k, wl, apre, kkw, kaw, bd):
    wraw = -_softplus(-wl) - 0.5
    lw = -jnp.exp(wraw)
    asig = jax.nn.sigmoid(apre)
    kkr = k * kkw
    nrm = jnp.maximum(jnp.sqrt(_headsum(kkr * kkr, bd)), 1e-12)
    kk = kkr / nrm
    k2 = k * (1.0 + (asig - 1.0) * kaw)
    return lw, k2, -kk, kk * asig


def _shifted(ps_ref, prev_ref, mu, blk):
    p = ps_ref[...]
    first = (blk % PREP_NB) == 0
    prev_row = jnp.where(first, 0.0, prev_ref[7:8, :])
    rolled = pltpu.roll(p, 1, 0)
    p_prev = jnp.where(_iota2(p.shape, 0) == 0, prev_row, rolled)
    return p, p_prev, p + (p_prev - p) * mu


def _prev_spec(width, blk_of):
    return pl.BlockSpec((8, width), lambda i: (jnp.maximum(blk_of(i) * (PREP_TM // 8) - 1, 0), 0))


def even_prep(ps, mu, w0, w2x, a0, a2x, kkw, kaw):
    tm = PREP_TM

    def body(ps_ref, prev_ref, mu_ref, w0_ref, w2_ref, a0_ref, a2_ref, kk_ref, ka_ref,
             r_ref, lw_ref, k2_ref, v_ref, aa_ref, bb_ref):
        _, _, s = _shifted(ps_ref, prev_ref, mu_ref[...], pl.program_id(0))
        wa = s[:, 3 * W:]
        wl = w0_ref[...] + _bdot(jnp.tanh(wa), w2_ref[...])
        apre = a0_ref[...] + _bdot(wa, a2_ref[...])
        lw, k2, aa, bb = _prep_elem(s[:, W:2 * W], wl, apre, kk_ref[...], ka_ref[...], _head_blockdiag())
        r_ref[...] = s[:, 0:W]
        v_ref[...] = s[:, 2 * W:3 * W]
        lw_ref[...] = lw
        k2_ref[...] = k2
        aa_ref[...] = aa
        bb_ref[...] = bb

    vec = _const_spec((1, W))
    return pl.pallas_call(
        body, grid=(T // tm,), name="even_prep",
        in_specs=[_row_spec(tm, SHIFT), _prev_spec(SHIFT, lambda i: i), _const_spec((1, SHIFT)), vec,
                  _const_spec((2 * LORA, W)), vec, _const_spec((2 * LORA, W)), vec, vec],
        out_specs=[_row_spec(tm, W)] * 6,
        out_shape=[jax.ShapeDtypeStruct((T, W), F32)] * 6,
        compiler_params=_cparams(("parallel",)),
    )(ps, ps, mu, w0, w2x, a0, a2x, kkw, kaw)


def even_prep_bwd(ps, mu, w0, w2x, a0, a2x, kkw, kaw, dr, dlw, dk2, dv, daa, dbb, dr2, dk22, dv2):
    tm = PREP_TM
    nb = T // tm
    rev = lambda i: nb - 1 - i

    def body(ps_ref, prev_ref, mu_ref, w0_ref, w2_ref, a0_ref, a2_ref, kk_ref, ka_ref,
             dr_ref, dlw_ref, dk2_ref, dv_ref, daa_ref, dbb_ref, dr2_ref, dk22_ref, dv2_ref,
             dps_ref, dmu_ref, dw0_ref, dw2_ref, da0_ref, da2_ref, dkk_ref, dka_ref, carry):
        i = pl.program_id(0)
        blk = rev(i)
        mu_v = mu_ref[...]
        p, p_prev, s = _shifted(ps_ref, prev_ref, mu_v, blk)
        wa = s[:, 3 * W:]
        th = jnp.tanh(wa)
        wl = w0_ref[...] + _bdot(th, w2_ref[...])
        apre = a0_ref[...] + _bdot(wa, a2_ref[...])
        bd = _head_blockdiag()
        k = s[:, W:2 * W]
        _, vjp = jax.vjp(lambda k_, wl_, ap_, kkw_, kaw_: _prep_elem(k_, wl_, ap_, kkw_, kaw_, bd),
                         k, wl, apre, kk_ref[...], ka_ref[...])
        dk, dwl, dap, dkkw, dkaw = vjp((dlw_ref[...], dk2_ref[...] + dk22_ref[...], daa_ref[...], dbb_ref[...]))
        dwa = _bdot_nt(dwl, w2_ref[...]) * (1.0 - th * th) + _bdot_nt(dap, a2_ref[...])
        ds = jnp.concatenate([dr_ref[...] + dr2_ref[...], dk, dv_ref[...] + dv2_ref[...], dwa], axis=-1)

        @pl.when(i == 0)
        def _():
            for ref in (dmu_ref, dw0_ref, dw2_ref, da0_ref, da2_ref, dkk_ref, dka_ref, carry):
                ref[...] = jnp.zeros_like(ref)

        dmu_ref[...] += jnp.sum(ds * (p_prev - p), axis=0, keepdims=True)
        dw0_ref[...] += jnp.sum(dwl, axis=0, keepdims=True)
        da0_ref[...] += jnp.sum(dap, axis=0, keepdims=True)
        dw2_ref[...] += _bdot_tn(th, dwl)
        da2_ref[...] += _bdot_tn(wa, dap)
        dkk_ref[...] += dkkw
        dka_ref[...] += dkaw
        dsm = ds * mu_v
        last = (blk % PREP_NB) == PREP_NB - 1
        nxt = jnp.where(last, 0.0, carry[0:1, :])
        up = pltpu.roll(dsm, tm - 1, 0)
        up = jnp.where(_iota2(up.shape, 0) == tm - 1, nxt, up)
        dps_ref[...] = (ds - dsm + up).astype(BF16)
        carry[0:1, :] = dsm[0:1, :]

    vec = _const_spec((1, W))
    rrow = lambda width: pl.BlockSpec((tm, width), lambda i: (rev(i), 0))
    return pl.pallas_call(
        body, grid=(nb,), name="even_prep_bwd",
        in_specs=[rrow(SHIFT), _prev_spec(SHIFT, rev), _const_spec((1, SHIFT)), vec,
                  _const_spec((2 * LORA, W)), vec, _const_spec((2 * LORA, W)), vec, vec] + [rrow(W)] * 9,
        out_specs=[rrow(SHIFT), _const_spec((1, SHIFT)), vec, _const_spec((2 * LORA, W)), vec,
                   _const_spec((2 * LORA, W)), vec, vec],
        out_shape=[jax.ShapeDtypeStruct((T, SHIFT), BF16), jax.ShapeDtypeStruct((1, SHIFT), F32),
                   jax.ShapeDtypeStruct((1, W), F32), jax.ShapeDtypeStruct((2 * LORA, W), F32),
                   jax.ShapeDtypeStruct((1, W), F32), jax.ShapeDtypeStruct((2 * LORA, W), F32),
                   jax.ShapeDtypeStruct((1, W), F32), jax.ShapeDtypeStruct((1, W), F32)],
        scratch_shapes=[pltpu.VMEM((8, SHIFT), F32)],
        compiler_params=_cparams(("arbitrary",), VMEM_BIG),
    )(ps, ps, mu, w0, w2x, a0, a2x, kkw, kaw, dr, dlw, dk2, dv, daa, dbb, dr2, dk22, dv2)


def _chunk_masks():
    row = _iota2((2 * L, 2 * L), 0)
    col = _iota2((2 * L, 2 * L), 1)
    step = col & (L - 1)
    keep = ((row < L) & (row > step)) | ((row >= L) & (row - L >= step))
    r1 = _iota2((L, L), 0)
    c1 = _iota2((L, L), 1)
    return keep.astype(F32), (r1 >= c1).astype(F32), (r1 == c1).astype(F32)


def _scaled(r, lw, k2, aa, bb, tri):
    g = _hdot(tri, lw)
    eg = jnp.exp(g)
    eng = jnp.exp(-g)
    egp = jnp.exp(g - lw)
    return eg, eng, egp, aa * egp, r * eg, bb * eng, k2 * eng


def _head_cols(h):
    return slice(h * HD, (h + 1) * HD)


def _per_head(a):
    return [a[:, _head_cols(h)] for h in range(NH)]


def _heads_operands(at, rt, bt, kt):
    x = [jnp.concatenate([a, r], axis=0).astype(BF16) for a, r in zip(_per_head(at), _per_head(rt))]
    yk = [jnp.concatenate([b, k], axis=0).astype(BF16) for b, k in zip(_per_head(bt), _per_head(kt))]
    return x, yk


def _heads_matrices(x, yk, keep, eye):
    m = [_bdot_nt(a, b) * keep for a, b in zip(x, yk)]
    p = [a[:L, :L] for a in m]
    tinv = [eye + a for a in p]
    for _ in range(5):
        p = [_bdot(a, a) for a in p]
        tinv = [t + _bdot(t, a) for t, a in zip(tinv, p)]
    return [a.astype(BF16) for a in m], [a.astype(BF16) for a in tinv]


def _heads_fwd(x, yk, m, tinv, v, s0, egl):
    xh = [_bdot_nt(a, s) for a, s in zip(x, s0)]
    u = [_bdot(t, h[:L] + _bdot(a[:L, L:], w)) for t, h, a, w in zip(tinv, xh, m, v)]
    uv = [jnp.concatenate([a, w], axis=0).astype(BF16) for a, w in zip(u, v)]
    y = [h[L:] + _bdot(a[L:], w) for h, a, w in zip(xh, m, uv)]
    sn = [e * (s + _bdot_tn(w, b)) for e, s, w, b in zip(egl, s0, uv, yk)]
    return y, sn, uv


def _heads_bwd(x, yk, m, tinv, v, s0, egl, dy, dsn, keep):
    _, sn, uv = _heads_fwd(x, yk, m, tinv, v, s0, egl)
    dzs = [d * e for d, e in zip(dsn, egl)]
    dgl = [jnp.sum(d * s, axis=0, keepdims=True) for d, s in zip(dsn, sn)]
    dyb = [a.astype(BF16) for a in dy]
    t1 = [_bdot_tn(a[L:], d) for a, d in zip(m, dyb)]
    t2 = [_bdot_nt(b, d) for b, d in zip(yk, dzs)]
    drhs = [_bdot_tn(t, a[:L] + b[:L]) for t, a, b in zip(tinv, t1, t2)]
    dv = [a[L:] + b[L:] + _bdot_tn(c[:L, L:], d) for a, b, c, d in zip(t1, t2, m, drhs)]
    gg = [jnp.concatenate([a, b], axis=0).astype(BF16) for a, b in zip(drhs, dy)]
    ds0 = [d + _bdot_tn(g, a) for d, g, a in zip(dzs, gg, x)]
    dm = [_bdot_nt(g, w) * keep for g, w in zip(gg, uv)]
    dx = [_bdot(g, s) + _bdot(d, b) for g, s, d, b in zip(gg, s0, dm, yk)]
    dyk = [_bdot_tn(d, a) + _bdot(w, z) for d, a, w, z in zip(dm, x, uv, dzs)]
    return ([a[:L] for a in dx], [a[L:] for a in dx], [a[:L] for a in dyk], [a[L:] for a in dyk], dv, dgl, ds0)


def rwkv_fwd(r, lw, k2, v, aa, bb):
    def body(r_ref, lw_ref, k2_ref, v_ref, aa_ref, bb_ref, y_ref, hs_ref, m_ref, t_ref, state):
        @pl.when(pl.program_id(1) == 0)
        def _():
            state[...] = jnp.zeros_like(state)

        s_all = state[...]
        hs_ref[0] = s_all
        keep, tri, eye = _chunk_masks()
        eg, _, _, at, rt, bt, kt = _scaled(r_ref[...], lw_ref[...], k2_ref[...], aa_ref[...], bb_ref[...], tri)
        x, yk = _heads_operands(at, rt, bt, kt)
        m, tinv = _heads_matrices(x, yk, keep, eye)
        s0 = [s_all[_head_cols(h), :] for h in range(NH)]
        y, sn, _ = _heads_fwd(x, yk, m, tinv, _per_head(v_ref[...]), s0, _per_head(eg[L - 1:L, :]))
        y_ref[...] = jnp.concatenate(y, axis=-1)
        m_ref[0] = jnp.concatenate(m, axis=-1)
        t_ref[0] = jnp.concatenate(tinv, axis=-1)
        state[...] = jnp.concatenate(sn, axis=0)

    blk = pl.BlockSpec((L, W), lambda b, c: (b * NC + c, 0))
    per_chunk = lambda rows, cols: pl.BlockSpec((1, rows, cols), lambda b, c: (b * NC + c, 0, 0))
    return pl.pallas_call(
        body, grid=(NSEQ, NC), name="rwkv_fwd",
        in_specs=[blk] * 6,
        out_specs=[blk, per_chunk(W, HD), per_chunk(2 * L, NH * 2 * L), per_chunk(L, NH * L)],
        out_shape=[jax.ShapeDtypeStruct((T, W), F32), jax.ShapeDtypeStruct((NSEQ * NC, W, HD), F32),
                   jax.ShapeDtypeStruct((NSEQ * NC, 2 * L, NH * 2 * L), BF16),
                   jax.ShapeDtypeStruct((NSEQ * NC, L, NH * L), BF16)],
        scratch_shapes=[pltpu.VMEM((W, HD), F32)],
        compiler_params=_cparams(("parallel", "arbitrary")),
    )(r, lw, k2, v, aa, bb)


def rwkv_bwd(r, lw, k2, v, aa, bb, hs, ms, ts, dy):
    def body(r_ref, lw_ref, k2_ref, v_ref, aa_ref, bb_ref, hs_ref, m_ref, t_ref, dy_ref,
             dr_ref, dlw_ref, dk2_ref, dv_ref, daa_ref, dbb_ref, dstate):
        @pl.when(pl.program_id(1) == 0)
        def _():
            dstate[...] = jnp.zeros_like(dstate)

        keep, tri, _ = _chunk_masks()
        eg, eng, egp, at, rt, bt, kt = _scaled(r_ref[...], lw_ref[...], k2_ref[...], aa_ref[...], bb_ref[...], tri)
        x, yk = _heads_operands(at, rt, bt, kt)
        m_all = m_ref[0]
        t_all = t_ref[0]
        m = [m_all[:, h * 2 * L:(h + 1) * 2 * L] for h in range(NH)]
        tinv = [t_all[:, h * L:(h + 1) * L] for h in range(NH)]
        s_all = hs_ref[0]
        ds_all = dstate[...]
        s0 = [s_all[_head_cols(h), :] for h in range(NH)]
        dsn = [ds_all[_head_cols(h), :] for h in range(NH)]
        dat, drt, dbt, dkt, dv, dgl, ds0 = _heads_bwd(
            x, yk, m, tinv, _per_head(v_ref[...]), s0, _per_head(eg[L - 1:L, :]), _per_head(dy_ref[...]), dsn, keep)
        dstate[...] = jnp.concatenate(ds0, axis=0)
        dv_ref[...] = jnp.concatenate(dv, axis=-1)
        dat, drt, dbt, dkt, dgl = (jnp.concatenate(a, axis=-1) for a in (dat, drt, dbt, dkt, dgl))
        dg = drt * rt - dbt * bt - dkt * kt
        dg = dg + jnp.where(_iota2(dg.shape, 0) == L - 1, dgl, 0.0)
        dgp = dat * at
        dlw_ref[...] = _hdot_tn(tri, dg + dgp) - dgp
        dr_ref[...] = drt * eg
        daa_ref[...] = dat * egp
        dbb_ref[...] = dbt * eng
        dk2_ref[...] = dkt * eng

    blk = pl.BlockSpec((L, W), lambda b, c: (b * NC + NC - 1 - c, 0))
    per_chunk = lambda rows, cols: pl.BlockSpec((1, rows, cols), lambda b, c: (b * NC + NC - 1 - c, 0, 0))
    return pl.pallas_call(
        body, grid=(NSEQ, NC), name="rwkv_bwd",
        in_specs=[blk] * 6 + [per_chunk(W, HD), per_chunk(2 * L, NH * 2 * L), per_chunk(L, NH * L), blk],
        out_specs=[blk] * 6,
        out_shape=[jax.ShapeDtypeStruct((T, W), F32)] * 6,
        scratch_shapes=[pltpu.VMEM((W, HD), F32)],
        compiler_params=_cparams(("parallel", "arbitrary")),
    )(r, lw, k2, v, aa, bb, hs, ms, ts, dy)


def _post_math(y, r, k2, v, ga, o, gb, lng, lnb, rk, bd):
    mu = _headsum(y, bd) * (1.0 / HD)
    yc = y - mu
    var = _headsum(yc * yc, bd) * (1.0 / HD)
    yn = yc * lax.rsqrt(var + GN_EPS) * lng + lnb
    bonus = _headsum(r * k2 * rk, bd) * v
    return (yn + bonus) * _silu(ga), o * _silu(gb)


def even_post(y, r, k2, v, ga, o, gb, lng, lnb, rk):
    tm = 256

    def body(y_ref, r_ref, k2_ref, v_ref, ga_ref, o_ref, gb_ref, lng_ref, lnb_ref, rk_ref, z_ref, zt_ref):
        ya, yb = _post_math(y_ref[...], r_ref[...], k2_ref[...], v_ref[...], ga_ref[...], o_ref[...], gb_ref[...],
                            lng_ref[...], lnb_ref[...], rk_ref[...], _head_blockdiag())
        ya, yb = ya.astype(BF16), yb.astype(BF16)
        z_ref[:, 0:W] = ya
        z_ref[:, W:2 * W] = yb
        zt_ref[0:W, :] = ya.T
        zt_ref[W:2 * W, :] = yb.T

    vec = _const_spec((1, W))
    return pl.pallas_call(
        body, grid=(T // tm,), name="even_post",
        in_specs=[_row_spec(tm, W)] * 7 + [vec] * 3,
        out_specs=[_row_spec(tm, D), _col_spec(D, tm)],
        out_shape=[jax.ShapeDtypeStruct((T, D), BF16), jax.ShapeDtypeStruct((D, T), BF16)],
        compiler_params=_cparams(("parallel",)),
    )(y, r, k2, v, ga, o, gb, lng, lnb, rk)


def even_post_bwd(y, r, k2, v, ga, o, gb, lng, lnb, rk, dz):
    tm = 256

    def body(y_ref, r_ref, k2_ref, v_ref, ga_ref, o_ref, gb_ref, lng_ref, lnb_ref, rk_ref, dz_ref,
             dy_ref, dr_ref, dk2_ref, dv_ref, dga_ref, do_ref, dgb_ref, dlng_ref, dlnb_ref, drk_ref):
        bd = _head_blockdiag()
        _, vjp = jax.vjp(lambda *a: _post_math(*a, bd), y_ref[...], r_ref[...], k2_ref[...], v_ref[...], ga_ref[...],
                         o_ref[...], gb_ref[...], lng_ref[...], lnb_ref[...], rk_ref[...])
        dzv = dz_ref[...]
        dy, dr, dk2, dv, dga, do, dgb, dlng, dlnb, drk = vjp((dzv[:, 0:W], dzv[:, W:2 * W]))
        for ref, val in ((dy_ref, dy), (dr_ref, dr), (dk2_ref, dk2), (dv_ref, dv), (dga_ref, dga), (do_ref, do),
                         (dgb_ref, dgb)):
            ref[...] = val.astype(ref.dtype)

        @pl.when(pl.program_id(0) == 0)
        def _():
            for ref in (dlng_ref, dlnb_ref, drk_ref):
                ref[...] = jnp.zeros_like(ref)

        dlng_ref[...] += dlng
        dlnb_ref[...] += dlnb
        drk_ref[...] += drk

    vec = _const_spec((1, W))
    return pl.pallas_call(
        body, grid=(T // tm,), name="even_post_bwd",
        in_specs=[_row_spec(tm, W)] * 7 + [vec] * 3 + [_row_spec(tm, D)],
        out_specs=[_row_spec(tm, W)] * 7 + [vec] * 3,
        out_shape=[jax.ShapeDtypeStruct((T, W), dt) for dt in (F32, F32, F32, F32, BF16, F32, BF16)]
        + [jax.ShapeDtypeStruct((1, W), F32)] * 3,
        compiler_params=_cparams(("arbitrary",)),
    )(y, r, k2, v, ga, o, gb, lng, lnb, rk, dz)


PADSEQ = SEQ + LEFT * L
ATT_SCALE = 1.0 / math.sqrt(HD)
NPAIR = NH // 2
PW = 2 * HD


def _pair_cols(p):
    return slice(p * PW, (p + 1) * PW)


def _pairs(a):
    return [a[:, _pair_cols(p)] for p in range(NPAIR)]


def _stack_pair(a):
    first = _iota2(a.shape, 1) < HD
    zero = jnp.zeros_like(a)
    return jnp.concatenate([jnp.where(first, a, zero), jnp.where(first, zero, a)], axis=0)


def _unstack_pair(a):
    return jnp.where(_iota2((L, PW), 1) < HD, a[:L], a[L:])


def _att_probs(q2, kw, bias, c):
    valid = _iota2((1, BAND), 1) >= (LEFT - c) * L
    s = [jnp.where(valid, _bdot_nt(a, b) * ATT_SCALE + bias[p], NEG) for p, (a, b) in enumerate(zip(q2, kw))]
    e = [jnp.exp(a - jnp.max(a, axis=-1, keepdims=True)) for a in s]
    return [a / jnp.sum(a, axis=-1, keepdims=True) for a in e]


def attention_fwd(q, kpad, vpad, bias):
    def body(q_ref, k_ref, v_ref, b_ref, o_ref):
        c = pl.program_id(1)
        start = pl.multiple_of(c * L, L)
        kw = _pairs(k_ref[pl.ds(start, BAND), :])
        vw = _pairs(v_ref[pl.ds(start, BAND), :])
        q2 = [_stack_pair(a) for a in _pairs(q_ref[...].astype(BF16))]
        p = _att_probs(q2, kw, b_ref[...], c)
        o_ref[...] = jnp.concatenate([_unstack_pair(_bdot(a, b)) for a, b in zip(p, vw)], axis=-1)

    qblk = pl.BlockSpec((L, W), lambda b, c: (b * NC + c, 0))
    kblk = pl.BlockSpec((PADSEQ, W), lambda b, c: (b, 0))
    return pl.pallas_call(
        body, grid=(NSEQ, NC), name="attention_fwd",
        in_specs=[qblk, kblk, kblk, _const_spec((NPAIR, 2 * L, BAND))],
        out_specs=qblk, out_shape=jax.ShapeDtypeStruct((T, W), F32),
        compiler_params=_cparams(("parallel", "arbitrary")),
    )(q, kpad, vpad, bias)


def attention_bwd(q, kpad, vpad, bias, do):
    def body(q_ref, k_ref, v_ref, b_ref, do_ref, dq_ref, dko_ref, dvo_ref, db_ref, dk_ref, dv_ref):
        b = pl.program_id(0)
        c = pl.program_id(1)

        @pl.when(c == 0)
        def _():
            dk_ref[...] = jnp.zeros_like(dk_ref)
            dv_ref[...] = jnp.zeros_like(dv_ref)

        @pl.when((c == 0) & (b == 0))
        def _():
            db_ref[...] = jnp.zeros_like(db_ref)

        start = pl.multiple_of(c * L, L)
        kw = _pairs(k_ref[pl.ds(start, BAND), :])
        vw = _pairs(v_ref[pl.ds(start, BAND), :])
        q2 = [_stack_pair(a) for a in _pairs(q_ref[...].astype(BF16))]
        do2 = [_stack_pair(a) for a in _pairs(do_ref[...].astype(BF16))]
        p = _att_probs(q2, kw, b_ref[...], c)
        dp = [_bdot_nt(a, b) for a, b in zip(do2, vw)]
        ds = [a * (d - jnp.sum(d * a, axis=-1, keepdims=True)) for a, d in zip(p, dp)]
        dss = [(a * ATT_SCALE).astype(BF16) for a in ds]
        dq_ref[...] = jnp.concatenate([_unstack_pair(_bdot(a, b)) for a, b in zip(dss, kw)], axis=-1).astype(BF16)
        dk_ref[pl.ds(start, BAND), :] += jnp.concatenate([_bdot_tn(a, b) for a, b in zip(dss, q2)], axis=-1)
        dv_ref[pl.ds(start, BAND), :] += jnp.concatenate([_bdot_tn(a, b) for a, b in zip(p, do2)], axis=-1)
        for i in range(NPAIR):
            db_ref[i] += ds[i]

        @pl.when(c == NC - 1)
        def _():
            dko_ref[...] = dk_ref[LEFT * L:, :].astype(BF16)
            dvo_ref[...] = dv_ref[LEFT * L:, :].astype(BF16)

    qblk = pl.BlockSpec((L, W), lambda b, c: (b * NC + c, 0))
    kblk = pl.BlockSpec((PADSEQ, W), lambda b, c: (b, 0))
    sblk = pl.BlockSpec((SEQ, W), lambda b, c: (b, 0))
    bblk = _const_spec((NPAIR, 2 * L, BAND))
    return pl.pallas_call(
        body, grid=(NSEQ, NC), name="attention_bwd",
        in_specs=[qblk, kblk, kblk, bblk, qblk],
        out_specs=[qblk, sblk, sblk, bblk],
        out_shape=[jax.ShapeDtypeStruct((T, W), BF16), jax.ShapeDtypeStruct((T, W), BF16),
                   jax.ShapeDtypeStruct((T, W), BF16), jax.ShapeDtypeStruct((NPAIR, 2 * L, BAND), F32)],
        scratch_shapes=[pltpu.VMEM((PADSEQ, W), F32), pltpu.VMEM((PADSEQ, W), F32)],
        compiler_params=_cparams(("arbitrary", "arbitrary"), VMEM_BIG),
    )(q, kpad, vpad, bias, do)


NTAB = 2 * CLIP + 1
EXT = BAND + L


def _ext_onehot():
    n = _iota2((EXT, NTAB), 0)
    m = _iota2((EXT, NTAB), 1)
    return (jnp.clip(BAND - 1 - n, -CLIP, CLIP) + CLIP == m).astype(F32)


def bias_expand(table):
    def body(t_ref, o_ref):
        ext = _hdot_nt(t_ref[...], _ext_onehot())
        for i in range(L):
            s = L - 1 - i
            o_ref[:, i, :] = (pltpu.roll(ext, EXT - s, 1) if s else ext)[:, :BAND]

    return pl.pallas_call(body, name="bias_expand", out_shape=jax.ShapeDtypeStruct((NH, L, BAND), F32))(table)


def bias_grad(dbias):
    def body(d_ref, o_ref):
        acc = jnp.zeros((NH, EXT), F32)
        zpad = jnp.zeros((NH, EXT - BAND), F32)
        for i in range(L):
            s = L - 1 - i
            row = jnp.concatenate([d_ref[:, i, :], zpad], axis=-1)
            acc = acc + (pltpu.roll(row, s, 1) if s else row)
        o_ref[...] = _hdot(acc, _ext_onehot())

    return pl.pallas_call(body, name="bias_grad", out_shape=jax.ShapeDtypeStruct((NH, NTAB), F32))(dbias)


def _group_cols(g):
    return slice(g * SGC, (g + 1) * SGC)


def _sg_norm(v, lng, lnb):
    gv = _gelu(v)
    gc = gv - jnp.mean(gv, axis=-1, keepdims=True)
    rstd = lax.rsqrt(jnp.mean(gc * gc, axis=-1, keepdims=True) + LN_EPS)
    xhat = gc * rstd
    return xhat, rstd, xhat * lng + lnb


def gmlp_fwd(u, v, gate, lng, lnb, wm_bf, sgb_t):
    def body(u_ref, v_ref, gt_ref, lng_ref, lnb_ref, wm_ref, sb_ref, z_ref, zt_ref):
        _, _, vln = _sg_norm(v_ref[...], lng_ref[...], lnb_ref[...])
        vlb = vln.astype(BF16)
        for g in range(NG):
            cs = _group_cols(g)
            sv = jnp.dot(wm_ref[g], vlb[:, cs], preferred_element_type=F32) + sb_ref[:, g:g + 1]
            zg = (_gelu(u_ref[:, cs]) * sv * _silu(gt_ref[:, cs])).astype(BF16)
            z_ref[:, cs] = zg
            zt_ref[cs, :] = zg.T

    return pl.pallas_call(
        body, grid=(T // SGC,), name="gmlp_fwd",
        in_specs=[_row_spec(SGC, D)] * 3 + [_const_spec((1, D))] * 2 + [_const_spec((NG, SGC, SGC)),
                                                                      _const_spec((SGC, NG))],
        out_specs=[_row_spec(SGC, D), _col_spec(D, SGC)],
        out_shape=[jax.ShapeDtypeStruct((T, D), BF16), jax.ShapeDtypeStruct((D, T), BF16)],
        compiler_params=_cparams(("parallel",)),
    )(u, v, gate, lng, lnb, wm_bf, sgb_t)


def gmlp_bwd(u, v, gate, lng, lnb, wm_bf, sgb_t, dz):
    def body(u_ref, v_ref, gt_ref, lng_ref, lnb_ref, wm_ref, sb_ref, dz_ref,
             du_ref, dv_ref, dgt_ref, dlng_ref, dlnb_ref, dwm_ref, dsb_ref):
        @pl.when(pl.program_id(0) == 0)
        def _():
            for ref in (dlng_ref, dlnb_ref, dwm_ref, dsb_ref):
                ref[...] = jnp.zeros_like(ref)

        vv = v_ref[...]
        xhat, rstd, vln = _sg_norm(vv, lng_ref[...], lnb_ref[...])
        vlb = vln.astype(BF16)
        dvln = []
        dsv_all = []
        for g in range(NG):
            cs = _group_cols(g)
            uu = u_ref[:, cs]
            gg = gt_ref[:, cs]
            dzz = dz_ref[:, cs]
            sv = jnp.dot(wm_ref[g], vlb[:, cs], preferred_element_type=F32) + sb_ref[:, g:g + 1]
            gu = _gelu(uu)
            sg = _silu(gg)
            dsv = dzz * gu * sg
            dgt_ref[:, cs] = (dzz * gu * sv * _dsilu(gg)).astype(BF16)
            du_ref[:, cs] = (dzz * sv * sg * _dgelu(uu)).astype(BF16)
            dsb16 = dsv.astype(BF16)
            dvln.append(lax.dot_general(wm_ref[g], dsb16, (((0,), (0,)), ((), ())), preferred_element_type=F32))
            dwm_ref[g] += lax.dot_general(dsb16, vlb[:, cs], (((1,), (1,)), ((), ())), preferred_element_type=F32)
            dsv_all.append(dsv)
        dvl = jnp.concatenate(dvln, axis=-1)
        dsv_cat = jnp.concatenate(dsv_all, axis=-1)
        sel = (_iota2((D, NG), 0) // SGC == _iota2((D, NG), 1)).astype(F32)
        dsb_ref[...] += _hdot(dsv_cat, sel)
        dlng_ref[...] += jnp.sum(dvl * xhat, axis=0, keepdims=True)
        dlnb_ref[...] += jnp.sum(dvl, axis=0, keepdims=True)
        dxh = dvl * lng_ref[...]
        dgv = rstd * (dxh - jnp.mean(dxh, axis=-1, keepdims=True)
                      - xhat * jnp.mean(dxh * xhat, axis=-1, keepdims=True))
        dv_ref[...] = (dgv * _dgelu(vv)).astype(BF16)

    return pl.pallas_call(
        body, grid=(T // SGC,), name="gmlp_bwd",
        in_specs=[_row_spec(SGC, D)] * 3 + [_const_spec((1, D))] * 2
        + [_const_spec((NG, SGC, SGC)), _const_spec((SGC, NG)), _row_spec(SGC, D)],
        out_specs=[_row_spec(SGC, D)] * 3 + [_const_spec((1, D))] * 2 + [_const_spec((NG, SGC, SGC)),
                                                                       _const_spec((SGC, NG))],
        out_shape=[jax.ShapeDtypeStruct((T, D), BF16)] * 3 + [jax.ShapeDtypeStruct((1, D), F32)] * 2
        + [jax.ShapeDtypeStruct((NG, SGC, SGC), F32), jax.ShapeDtypeStruct((SGC, NG), F32)],
        compiler_params=_cparams(("arbitrary",)),
    )(u, v, gate, lng, lnb, wm_bf, sgb_t, dz)


NCHIP = 4
NDEV = 8
ANY = pl.BlockSpec(memory_space=pl.ANY)


HBM = pl.BlockSpec(memory_space=pltpu.HBM)
SEM = pl.BlockSpec(memory_space=pltpu.SEMAPHORE)
EFFECT = pltpu.SideEffectType.DATAFLOW_SIDE_EFFECTING


def _peers(whole_mesh):
    x, y, c = lax.axis_index("x"), lax.axis_index("y"), lax.axis_index("c")
    if not whole_mesh:
        return [((px, py, c), 2 * px + py) for px, py in ((1 - x, y), (x, 1 - y), (1 - x, 1 - y))], 2 * x + y
    out = []
    for j in range(1, NDEV):
        px, py, pc = x ^ (j >> 2), y ^ ((j >> 1) & 1), c ^ (j & 1)
        out.append(((px, py, pc), 4 * px + 2 * py + pc))
    return out, 4 * x + 2 * y + c


def _send_copies(src, land, send, recv, scatter, whole_mesh, starting):
    peers, me = _peers(whole_mesh)
    copies = []
    for t in range(len(src)):
        for j, (dev, slot) in enumerate(peers):
            k = t * len(peers) + j
            copies.append(pltpu.make_async_remote_copy(
                src_ref=src[t].at[slot] if scatter else src[t], dst_ref=land[t].at[me if starting else slot],
                send_sem=send.at[k], recv_sem=recv.at[k], device_id=dev, device_id_type=MESH))
    return copies


def send_start(srcs, lands, scatter, whole_mesh, name):
    n = len(srcs)
    nsem = n * (NDEV - 1 if whole_mesh else NCHIP - 1)

    def body(*refs):
        for cp in _send_copies(refs[:n], refs[n:2 * n], refs[2 * n], refs[2 * n + 1], scatter, whole_mesh, True):
            cp.start()
        refs[-1][...] = jnp.zeros_like(refs[-1])

    arrs = list(srcs) + list(lands)
    out = pl.pallas_call(
        body, name=name,
        out_shape=(pltpu.SemaphoreType.DMA((nsem,)), pltpu.SemaphoreType.DMA((nsem,)),
                   *[pltpu.HBM(a.shape, a.dtype) for a in arrs], jax.ShapeDtypeStruct((8, 128), F32)),
        in_specs=[HBM] * (2 * n), out_specs=(SEM, SEM, *[HBM] * (2 * n), pl.BlockSpec(memory_space=pltpu.VMEM)),
        input_output_aliases={i: 2 + i for i in range(2 * n)},
        compiler_params=pltpu.CompilerParams(has_side_effects=EFFECT),
    )(*[pltpu.with_memory_space_constraint(a, pltpu.HBM) for a in arrs])
    return out[0], out[1], list(out[2:2 + n]), list(out[2 + n:2 + 2 * n]), out[-1]


def send_wait(started, after, scatter, whole_mesh, name):
    send, recv, srcs, lands, _ = started
    n = len(srcs)

    def body(*refs):
        for cp in _send_copies(refs[:n], refs[n:2 * n], refs[2 * n], refs[2 * n + 1], scatter, whole_mesh, False):
            cp.wait_send()
            cp.wait_recv()

    arrs = list(srcs) + list(lands)
    out = pl.pallas_call(
        body, name=name, out_shape=tuple(pltpu.HBM(a.shape, a.dtype) for a in arrs),
        in_specs=[HBM] * (2 * n) + [SEM, SEM, ANY], out_specs=tuple([HBM] * (2 * n)),
        input_output_aliases={i: i for i in range(2 * n)},
        compiler_params=pltpu.CompilerParams(has_side_effects=EFFECT),
    )(*arrs, send, recv, after)
    return list(out[n:])


def exchange_c(arrs, name):
    n = len(arrs)

    def body(*refs):
        ins, outs = refs[:n], refs[n:2 * n]
        send, recv = refs[2 * n:]
        sibling = (lax.axis_index("x"), lax.axis_index("y"), 1 - lax.axis_index("c"))
        copies = [pltpu.make_async_remote_copy(src_ref=ins[t], dst_ref=outs[t], send_sem=send.at[t], recv_sem=recv.at[t],
                                               device_id=sibling, device_id_type=MESH) for t in range(n)]
        for cp in copies:
            cp.start()
        for cp in copies:
            cp.wait()

    return pl.pallas_call(
        body, name=name, in_specs=[ANY] * n, out_specs=[ANY] * n,
        out_shape=[jax.ShapeDtypeStruct(a.shape, a.dtype) for a in arrs],
        scratch_shapes=[pltpu.SemaphoreType.DMA((n,)), pltpu.SemaphoreType.DMA((n,))],
    )(*arrs)


def gather_weights(arrs, split):
    n = len(arrs)

    def body(*refs):
        ins, outs = refs[:n], refs[n:2 * n]
        send1, recv1, send2, recv2, loc = refs[2 * n:]
        x, y, c = lax.axis_index("x"), lax.axis_index("y"), lax.axis_index("c")
        me = 2 * x + y
        sibling = (x, y, 1 - c)
        peers = [(1 - x, y), (x, 1 - y), (1 - x, 1 - y)]

        def rows_of(t, core):
            half = arrs[t].shape[0] // 2
            return pl.ds(core * half, half)

        def part(ref, t, core):
            return ref.at[rows_of(t, core)] if split[t] else ref

        local = [pltpu.make_async_copy(ins[t], outs[t].at[me], loc.at[t]) for t in range(n)]
        for cp in local:
            cp.start()
        first = []
        for t in range(n):
            for j, (px, py) in enumerate(peers):
                first.append(pltpu.make_async_remote_copy(
                    src_ref=part(ins[t], t, c), dst_ref=part(outs[t].at[me], t, c), send_sem=send1.at[t, j],
                    recv_sem=recv1.at[t, j], device_id=(px, py, c), device_id_type=MESH))
        for cp in first:
            cp.start()
        passed = []
        for t in range(n):
            for j, (px, py) in enumerate(peers):
                landed = part(outs[t].at[2 * px + py], t, c)
                pltpu.make_async_remote_copy(
                    src_ref=landed, dst_ref=landed, send_sem=send1.at[t, j], recv_sem=recv1.at[t, j],
                    device_id=(x, y, c), device_id_type=MESH).wait_recv()
                if split[t]:
                    cp = pltpu.make_async_remote_copy(
                        src_ref=landed, dst_ref=landed, send_sem=send2.at[t, j], recv_sem=recv2.at[t, j],
                        device_id=sibling, device_id_type=MESH)
                    cp.start()
                    passed.append(cp)
        for t in range(n):
            for j, (px, py) in enumerate(peers):
                if split[t]:
                    other = part(outs[t].at[2 * px + py], t, 1 - c)
                    pltpu.make_async_remote_copy(
                        src_ref=other, dst_ref=other, send_sem=send2.at[t, j], recv_sem=recv2.at[t, j],
                        device_id=(x, y, c), device_id_type=MESH).wait_recv()
        for cp in first + passed:
            cp.wait_send()
        for cp in local:
            cp.wait()

    return pl.pallas_call(
        body, name="gather_weights", in_specs=[ANY] * n, out_specs=[ANY] * n,
        out_shape=[jax.ShapeDtypeStruct((NCHIP,) + a.shape, a.dtype) for a in arrs],
        scratch_shapes=[pltpu.SemaphoreType.DMA((n, 3))] * 4 + [pltpu.SemaphoreType.DMA((n,))],
    )(*arrs)


def _adam_math(g, w, m, v):
    m = ADAM_B1 * m + (1.0 - ADAM_B1) * g
    v = ADAM_B2 * v + (1.0 - ADAM_B2) * (g * g)
    m_hat = m / (1.0 - ADAM_B1 ** ADAM_STEP)
    v_hat = v / (1.0 - ADAM_B2 ** ADAM_STEP)
    delta = -ADAM_LR * (m_hat / (jnp.sqrt(v_hat) + ADAM_EPS) + ADAM_WD * w)
    return delta, m, v


def _rows_tile(rows):
    return rows if rows <= 256 else 256


def sum_chips(own, parts, name):
    _, rows, cols = parts.shape
    tr = _rows_tile(rows)

    def body(own_ref, p_ref, o_ref):
        acc = own_ref[...]
        for s in range(NCHIP):
            acc = acc + p_ref[s].astype(F32)
        o_ref[...] = acc

    return pl.pallas_call(
        body, grid=(rows // tr,), name=name,
        in_specs=[pl.BlockSpec((tr, cols), lambda i: (i, 0)), pl.BlockSpec((NCHIP, tr, cols), lambda i: (0, i, 0))],
        out_specs=pl.BlockSpec((tr, cols), lambda i: (i, 0)),
        out_shape=jax.ShapeDtypeStruct((rows, cols), F32),
        compiler_params=_cparams(("parallel",)),
    )(own, parts)


def adam_shard(p_mine, p_sib, w, m, v, name):
    rows, cols = w.shape
    tr = _rows_tile(rows)

    def body(a_ref, b_ref, w_ref, m_ref, v_ref, g_ref, d_ref, mo_ref, vo_ref):
        g = a_ref[...] + b_ref[...]
        g_ref[...] = g
        d_ref[...], mo_ref[...], vo_ref[...] = _adam_math(g, w_ref[...], m_ref[...], v_ref[...])

    spec = pl.BlockSpec((tr, cols), lambda i: (i, 0))
    return pl.pallas_call(
        body, grid=(rows // tr,), name=name, in_specs=[spec] * 5, out_specs=[spec] * 4,
        out_shape=[jax.ShapeDtypeStruct((rows, cols), F32)] * 4,
        compiler_params=_cparams(("parallel",)),
    )(p_mine, p_sib, w, m, v)


def adam_replicated(parts, w, m, v, name):
    rows = w.shape[0]

    def body(p_ref, w_ref, m_ref, v_ref, g_ref, d_ref, mo_ref, vo_ref):
        g = p_ref[0]
        for d in range(1, NDEV):
            g = g + p_ref[d]
        g_ref[...] = g
        d_ref[...], mo_ref[...], vo_ref[...] = _adam_math(g, w_ref[...], m_ref[...], v_ref[...])

    return pl.pallas_call(
        body, name=name, out_shape=[jax.ShapeDtypeStruct((rows, 128), F32)] * 4,
    )(parts, w, m, v)


def _pack(arrs):
    pieces = []
    for a in arrs:
        flat = a.reshape(-1)
        pad = (-flat.shape[0]) % 128
        pieces.append(jnp.pad(flat, (0, pad)) if pad else flat)
    flat = jnp.concatenate(pieces)
    pad = (-flat.shape[0]) % 1024
    return jnp.pad(flat, (0, pad)).reshape(-1, 128)


def _unpack(buf, shapes):
    flat = buf.reshape(-1)
    out = []
    o = 0
    for s in shapes:
        n = int(np.prod(s))
        out.append(flat[o:o + n].reshape(s))
        o += n + (-n) % 128
    return out


EVEN_SPLITS = (SHIFT, W, W, W, W, W)
ODD_SPLITS = (D, D, D)


def _cols_to_chips(a):
    rows, cols = a.shape
    return a.reshape(rows, NCHIP, cols // NCHIP).transpose(1, 0, 2)


def _chips_to_cols(a):
    _, rows, n = a.shape
    return a.transpose(1, 0, 2).reshape(rows, NCHIP * n)


def kernel(x, norm_g, w_in_e, shift_mu, rw_w0, rw_w2, rw_a0, rw_a2, rw_kk, rw_ka, rw_rk, rw_lnx_g, rw_lnx_b, att_bias, w_out_e, w_in_o, sg_ln_g, sg_ln_b, sg_w, sg_b, w_out_o, final_g, loss_target, m_norm_g, m_w_in_e, m_shift_mu, m_rw_w0, m_rw_w2, m_rw_a0, m_rw_a2, m_rw_kk, m_rw_ka, m_rw_rk, m_rw_lnx_g, m_rw_lnx_b, m_att_bias, m_w_out_e, m_w_in_o, m_sg_ln_g, m_sg_ln_b, m_sg_w, m_sg_b, m_w_out_o, m_final_g, v_norm_g, v_w_in_e, v_shift_mu, v_rw_w0, v_rw_w2, v_rw_a0, v_rw_a2, v_rw_kk, v_rw_ka, v_rw_rk, v_rw_lnx_g, v_rw_lnx_b, v_att_bias, v_w_out_e, v_w_in_o, v_sg_ln_g, v_sg_ln_b, v_sg_w, v_sg_b, v_w_out_o, v_final_g):
    x2 = x.reshape(T, D)
    tgt = loss_target.reshape(T, D)

    my_chip = 2 * lax.axis_index("x") + lax.axis_index("y")
    gathered = gather_weights(
        [w_in_e[0].astype(BF16), jnp.concatenate([rw_w2[0], rw_a2[0]], axis=0),
         jnp.concatenate([sg_ln_g, sg_ln_b], axis=0)], [True, True, False])
    wie = _chips_to_cols(gathered[0])
    w2 = _chips_to_cols(gathered[1][:, :LORA])
    a2 = _chips_to_cols(gathered[1][:, LORA:])
    sglg = _chips_to_cols(gathered[2][:, 0:1])
    sglb = _chips_to_cols(gathered[2][:, 1:2])

    late = [w_out_e[0].astype(BF16), w_in_o[0].astype(BF16), w_out_o[0].astype(BF16)]
    late_started = send_start(late, [jnp.broadcast_to(a[None], (NCHIP,) + a.shape) for a in late], False, False,
                              "late_weights_start")

    def late_weights(after):
        woe, wio, woo = send_wait(late_started, after, False, False, "late_weights_wait")
        return woe.reshape(D, D), _chips_to_cols(wio), woo.reshape(D, D)

    def scatter_start(grads, name):
        srcs = [g_.astype(BF16) if g_.shape[-1] >= W else g_ for g_ in grads]
        return send_start(srcs, [jnp.zeros_like(s) for s in srcs], True, False, name)

    def own_block(g_):
        return lax.dynamic_index_in_dim(g_, my_chip, axis=0, keepdims=False)

    started = {}

    def on_odd_grads(d_woo, d_wio):
        blocks = [d_woo.reshape(NCHIP, D // NCHIP, D), d_wio]
        started["odd"] = (scatter_start(blocks, "odd_grads_start"), [own_block(b) for b in blocks])
        return started["odd"][0][-1]

    def on_even_grads(big_g):
        d_wie, d_woe, _, _, d_w2, d_a2, d_sglg, d_sglb = big_g
        blocks = [d_wie, d_woe.reshape(NCHIP, D // NCHIP, D), _cols_to_chips(d_w2), _cols_to_chips(d_a2),
                  _cols_to_chips(d_sglg), _cols_to_chips(d_sglb)]
        started["even"] = (scatter_start(blocks, "even_grads_start"), [own_block(b) for b in blocks])
        return started["even"][0][-1]

    def on_small_grads(layer, grads):
        mine = _pack(grads)
        started[layer + "_small"] = send_start([mine], [jnp.broadcast_to(mine[None], (NDEV,) + mine.shape)], False,
                                               True, layer + "_small_grads_start")
        return started[layer + "_small"][-1]

    loss_part, dx, _, _ = _local_step(
        x2, tgt, wie, late_weights, w2, a2, sglg, sglb, norm_g, shift_mu, rw_w0, rw_a0, rw_kk, rw_ka, rw_rk,
        rw_lnx_g, rw_lnx_b, att_bias, sg_w, sg_b, final_g, first_after=late_started[-1], on_odd_grads=on_odd_grads,
        on_even_grads=on_even_grads, on_small_grads=on_small_grads)
    loss = lax.psum(loss_part, ("x", "y", "c"))
    even_started, even_own = started["even"]

    wmv = {"w_in_e": (w_in_e[0], m_w_in_e[0], v_w_in_e[0]), "w_out_e": (w_out_e[0], m_w_out_e[0], v_w_out_e[0]),
           "w_in_o": (w_in_o[0], m_w_in_o[0], v_w_in_o[0]), "w_out_o": (w_out_o[0], m_w_out_o[0], v_w_out_o[0]),
           "rw_w2": (rw_w2[0], m_rw_w2[0], v_rw_w2[0]), "rw_a2": (rw_a2[0], m_rw_a2[0], v_rw_a2[0]),
           "sg_ln_g": (sg_ln_g, m_sg_ln_g, v_sg_ln_g), "sg_ln_b": (sg_ln_b, m_sg_ln_b, v_sg_ln_b)}
    sharded = {}

    def finish(names, own, landed, tag):
        partial = [sum_chips(o_, p_, "sum_" + nm) for o_, p_, nm in zip(own, landed, names)]
        from_sibling = exchange_c(partial, "swap_partials_" + tag)
        for nm, mine, sib in zip(names, partial, from_sibling):
            w_, m_, v_ = wmv[nm]
            res = adam_shard(mine, sib, w_, m_, v_, "adam_" + nm)
            lead = nm not in ("sg_ln_g", "sg_ln_b")
            sharded[nm] = [a[None] if lead else a for a in res]
        return partial[0]

    odd_started, odd_own = started["odd"]
    odd_landed = send_wait(odd_started, started["even_small"][-1], True, False, "odd_grads_wait")
    done = finish(["w_out_o", "w_in_o"], odd_own, odd_landed, "odd")

    groups = {
        "odd": (["sg_w", "sg_b", "final_g", "norm_g1"], [sg_w, sg_b, final_g, norm_g[1:2]],
                [m_sg_w, m_sg_b, m_final_g, m_norm_g[1:2]], [v_sg_w, v_sg_b, v_final_g, v_norm_g[1:2]]),
        "even": (["norm_g0", "shift_mu", "rw_w0", "rw_a0", "rw_kk", "rw_ka", "rw_rk", "rw_lnx_g", "rw_lnx_b",
                  "att_bias"],
                 [norm_g[0:1], shift_mu, rw_w0, rw_a0, rw_kk, rw_ka, rw_rk, rw_lnx_g, rw_lnx_b, att_bias],
                 [m_norm_g[0:1], m_shift_mu, m_rw_w0, m_rw_a0, m_rw_kk, m_rw_ka, m_rw_rk, m_rw_lnx_g, m_rw_lnx_b,
                  m_att_bias],
                 [v_norm_g[0:1], v_shift_mu, v_rw_w0, v_rw_a0, v_rw_kk, v_rw_ka, v_rw_rk, v_rw_lnx_g, v_rw_lnx_b,
                  v_att_bias]),
    }
    rep = {}
    for layer in ("odd", "even"):
        nms, ws, ms_, vs_ = groups[layer]
        (gathered_g,) = send_wait(started[layer + "_small"], done, False, True, layer + "_small_grads_wait")
        rep_out = adam_replicated(gathered_g, _pack(ws), _pack(ms_), _pack(vs_), "adam_" + layer + "_small")
        done = rep_out[0]
        for nm in nms:
            rep[nm] = []
        for buf in rep_out:
            for nm, a in zip(nms, _unpack(buf, [w_.shape for w_ in ws])):
                rep[nm].append(a)
    rep["norm_g"] = [jnp.concatenate([a, b], axis=0) for a, b in zip(rep["norm_g0"], rep["norm_g1"])]
    even_landed = send_wait(even_started, done, True, False, "even_grads_wait")
    finish(["w_in_e", "w_out_e", "rw_w2", "rw_a2", "sg_ln_g", "sg_ln_b"], even_own, even_landed, "even")

    order = ["norm_g", "w_in_e", "shift_mu", "rw_w0", "rw_w2", "rw_a0", "rw_a2", "rw_kk", "rw_ka", "rw_rk",
             "rw_lnx_g", "rw_lnx_b", "att_bias", "w_out_e", "w_in_o", "sg_ln_g", "sg_ln_b", "sg_w", "sg_b",
             "w_out_o", "final_g"]
    results = {**sharded, **rep}
    outs = [loss, dx.reshape(NSEQ, SEQ, D)]
    for kind in range(4):
        outs += [results[nm][kind] for nm in order]
    return tuple(outs)


def _local_step(x2, tgt, wie, late_weights, w2, a2, sglg, sglb, norm_g, shift_mu, rw_w0, rw_a0, rw_kk, rw_ka, rw_rk,
                rw_lnx_g, rw_lnx_b, att_bias, sg_w, sg_b, final_g, first_after=None, on_odd_grads=None,
                on_even_grads=None, on_small_grads=None):
    zl = jnp.zeros((LORA, W), F32)
    w2x = jnp.concatenate([w2, zl], axis=0)
    a2x = jnp.concatenate([zl, a2], axis=0)
    rk = rw_rk.reshape(1, W)
    pos = np.arange(SGC)
    sg_mask = jnp.asarray(((pos[None, :] // L) <= (pos[:, None] // L)).astype(np.float32))
    wm = (sg_w[0] * sg_mask[None]).astype(BF16)
    sgb_t = sg_b[0].T

    xn0, ps, ga, q, kb, vb, gb = ln_in_proj(x2, norm_g[0:1], wie, EVEN_SPLITS, "in_proj_even", after=first_after)
    r, lw, k2, v, aa, bb = even_prep(ps, shift_mu, rw_w0, w2x, rw_a0, a2x, rw_kk, rw_ka)
    y, hs, ms, ts = rwkv_fwd(r, lw, k2, v, aa, bb)
    bias = bias_expand(att_bias[0]).reshape(NPAIR, 2 * L, BAND)

    def padded(a):
        return jnp.pad(a.astype(BF16).reshape(NSEQ, SEQ, W), ((0, 0), (LEFT * L, 0), (0, 0))).reshape(NSEQ * PADSEQ, W)

    kpad, vpad = padded(kb), padded(vb)
    o = attention_fwd(q, kpad, vpad, bias)
    z, zt = even_post(y, r, k2, v, ga, o, gb, rw_lnx_g, rw_lnx_b, rk)
    woe, wio, woo = late_weights(z)
    h1 = out_proj(x2, z, woe, "out_proj_even")
    xn1, u, vv, gt = ln_in_proj(h1, norm_g[1:2], wio, ODD_SPLITS, "in_proj_odd")
    z2, z2t = gmlp_fwd(u, vv, gt, sglg, sglb, wm, sgb_t)
    h2 = out_proj(h1, z2, woo, "out_proj_odd")
    dh2, loss_part, d_final_g = final_loss(h2, final_g[None], tgt)

    dz2, d_woo = out_proj_bwd(dh2, z2t, woo, "out_proj_odd_bwd")
    du, dvv, dgt, d_sglg, d_sglb, d_wm, d_sgb_t = gmlp_bwd(u, vv, gt, sglg, sglb, wm, sgb_t, dz2)
    dp_odd = [du, dvv, dgt]
    d_wio = matmul_acc_chips(xn1, dp_odd, "in_proj_odd_dw")
    token = on_odd_grads(d_woo, d_wio) if on_odd_grads else None
    dh1, d_g1 = in_proj_bwd_x(h1, norm_g[1:2], wio, dp_odd, dh2, "in_proj_odd_bwd", after=token)
    odd_small = [d_wm * sg_mask[None], d_sgb_t.T, d_final_g, d_g1]
    token = on_small_grads("odd", odd_small) if on_small_grads else None
    dz, d_woe = out_proj_bwd(dh1, zt, woe, "out_proj_even_bwd", after=token)
    dy, dr2, dk22, dv2, dga, do, dgb, d_lng, d_lnb, d_rk = even_post_bwd(
        y, r, k2, v, ga, o, gb, rw_lnx_g, rw_lnx_b, rk, dz)
    dq, dkb, dvb, dbias = attention_bwd(q, kpad, vpad, bias, do)
    d_att_bias = bias_grad(dbias.reshape(NH, L, BAND))
    dr, dlw, dk2, dv, daa, dbb = rwkv_bwd(r, lw, k2, v, aa, bb, hs, ms, ts, dy)
    dps, d_mu, d_w0, d_w2x, d_a0, d_a2x, d_kk, d_ka = even_prep_bwd(
        ps, shift_mu, rw_w0, w2x, rw_a0, a2x, rw_kk, rw_ka, dr, dlw, dk2, dv, daa, dbb, dr2, dk22, dv2)
    dp_even = [dps, dga, dq, dkb, dvb, dgb]
    d_wie = matmul_acc_chips(xn0, dp_even, "in_proj_even_dw")
    big_g = (d_wie, d_woe, d_wio, d_woo, d_w2x[:LORA], d_a2x[LORA:], d_sglg, d_sglb)
    token = on_even_grads(big_g) if on_even_grads else None
    dx, d_g0 = in_proj_bwd_x(x2, norm_g[0:1], wie, dp_even, dh1, "in_proj_even_bwd", after=token)
    even_small = [d_g0, d_mu, d_w0, d_a0, d_kk, d_ka, d_rk, d_lng, d_lnb, d_att_bias]
    if on_small_grads:
        on_small_grads("even", even_small)
    rep_g = [jnp.concatenate([d_g0, d_g1], axis=0)] + even_small[1:] + odd_small[:3]
    return loss_part[0, 0], dx, big_g, rep_g
```

```python
import functools
import math

import jax
import jax.numpy as jnp
import numpy as np
from jax import lax
from jax.experimental import pallas as pl
from jax.experimental.pallas import tpu as pltpu

F32 = jnp.float32
BF16 = jnp.bfloat16
HI = lax.Precision.HIGHEST

D = 1024
SEQ = 2048
NSEQ = 2
T = NSEQ * SEQ
HD = 64
NH = 8
W = 512
SHIFT = 1664
LORA = 64
EVEN_IN = 4224
ODD_IN = 3072
L = 64
NC = SEQ // L
LEFT = 8
BAND = (LEFT + 1) * L
CLIP = 128
SGC = 128
NG = 8
RMS_EPS = 1e-6
LN_EPS = 1e-5
GN_EPS = 64e-5
NEG = -1e30
VMEM_BIG = 56 * 1024 * 1024

ADAM_LR = 0.001
ADAM_B1 = 0.9
ADAM_B2 = 0.999
ADAM_EPS = 1e-08
ADAM_WD = 0.01
ADAM_STEP = 10

MESH = pl.DeviceIdType.MESH


def _bdot(a, b):
    return jnp.dot(a.astype(BF16), b.astype(BF16), preferred_element_type=F32)


def _bdot_nt(a, b):
    return lax.dot_general(a.astype(BF16), b.astype(BF16), (((1,), (1,)), ((), ())), preferred_element_type=F32)


def _bdot_tn(a, b):
    return lax.dot_general(a.astype(BF16), b.astype(BF16), (((0,), (0,)), ((), ())), preferred_element_type=F32)


def _hdot(a, b):
    return jnp.dot(a, b, precision=HI, preferred_element_type=F32)


def _hdot_nt(a, b):
    return lax.dot_general(a, b, (((1,), (1,)), ((), ())), precision=HI, preferred_element_type=F32)


def _hdot_tn(a, b):
    return lax.dot_general(a, b, (((0,), (0,)), ((), ())), precision=HI, preferred_element_type=F32)


def _iota2(shape, dim):
    return lax.broadcasted_iota(jnp.int32, shape, dim)


def _head_blockdiag():
    r = _iota2((W, W), 0) // HD
    c = _iota2((W, W), 1) // HD
    return (r == c).astype(BF16)


def _headsum_impl(x, bd):
    hi = x.astype(BF16)
    mid = (x - hi.astype(F32)).astype(BF16)
    return jnp.dot(hi, bd, preferred_element_type=F32) + jnp.dot(mid, bd, preferred_element_type=F32)


@jax.custom_vjp
def _headsum(x, bd):
    return _headsum_impl(x, bd)


def _headsum_fwd(x, bd):
    return _headsum_impl(x, bd), bd


def _headsum_bwd(bd, ct):
    return _headsum_impl(ct, bd), None


_headsum.defvjp(_headsum_fwd, _headsum_bwd)


def _silu(x):
    return x * jax.nn.sigmoid(x)


def _dsilu(x):
    s = jax.nn.sigmoid(x)
    return s * (1.0 + x * (1.0 - s))


_GELU_C = math.sqrt(2.0 / math.pi)


def _gelu(x):
    return 0.5 * x * (1.0 + jnp.tanh(_GELU_C * (x + 0.044715 * (x * x * x))))


def _dgelu(x):
    t = jnp.tanh(_GELU_C * (x + 0.044715 * (x * x * x)))
    return 0.5 * (1.0 + t) + 0.5 * x * (1.0 - t * t) * _GELU_C * (1.0 + 3.0 * 0.044715 * x * x)


def _softplus(x):
    return jnp.maximum(x, 0.0) + jnp.log(1.0 + jnp.exp(-jnp.abs(x)))


def _cparams(sem, vmem=None):
    return pltpu.CompilerParams(dimension_semantics=sem, vmem_limit_bytes=vmem)


def _row_spec(tm, width):
    return pl.BlockSpec((tm, width), lambda i: (i, 0))


def _col_spec(height, tm):
    return pl.BlockSpec((height, tm), lambda i: (0, i))


def _const_spec(shape):
    nd = len(shape)
    return pl.BlockSpec(shape, lambda *_: (0,) * nd)


def ln_in_proj(x, g, w_bf, splits, name, after=None):
    n = w_bf.shape[1]
    tm = 256
    spans = []
    o = 0
    for s in splits:
        spans.append((o, o + s))
        o += s
    assert o == n
    extra_specs, extra = _after_operand(after)

    def body(x_ref, g_ref, w_ref, *rest):
        xn_ref, outs = rest[len(extra)], rest[len(extra) + 1:]
        xv = x_ref[...]
        rstd = lax.rsqrt(jnp.mean(xv * xv, axis=-1, keepdims=True) + RMS_EPS)
        xn = (xv * rstd * g_ref[...]).astype(BF16)
        xn_ref[...] = xn.T
        p = jnp.dot(xn, w_ref[...], preferred_element_type=F32)
        for o_ref, (a, b) in zip(outs, spans):
            o_ref[...] = p[:, a:b]

    return pl.pallas_call(
        body, grid=(T // tm,), name=name,
        in_specs=[_row_spec(tm, D), _const_spec((1, D)), _const_spec((D, n))] + extra_specs,
        out_specs=[_col_spec(D, tm)] + [_row_spec(tm, s) for s in splits],
        out_shape=[jax.ShapeDtypeStruct((D, T), BF16)] + [jax.ShapeDtypeStruct((T, s), F32) for s in splits],
        compiler_params=_cparams(("parallel",), VMEM_BIG),
    )(x, g, w_bf, *extra)


def in_proj_bwd_x(x, g, w_bf, dps, dres, name, after=None):
    n = w_bf.shape[1]
    tm = 256
    widths = [d.shape[1] for d in dps]
    extra_specs, extra = _after_operand(after)

    def body(x_ref, g_ref, w_ref, dres_ref, *rest):
        dp_refs = rest[:len(widths)]
        dx_ref, dg_ref = rest[-2:]
        dp = jnp.concatenate([r[...] for r in dp_refs], axis=-1)
        dxn = lax.dot_general(dp, w_ref[...], (((1,), (1,)), ((), ())), preferred_element_type=F32)
        xv = x_ref[...]
        rstd = lax.rsqrt(jnp.mean(xv * xv, axis=-1, keepdims=True) + RMS_EPS)
        xhat = xv * rstd
        dgp = jnp.sum(dxn * xhat, axis=0, keepdims=True)

        @pl.when(pl.program_id(0) == 0)
        def _():
            dg_ref[...] = jnp.zeros_like(dg_ref)

        dg_ref[...] += dgp
        dxh = dxn * g_ref[...]
        dx_ref[...] = dres_ref[...] + rstd * (dxh - xhat * jnp.mean(dxh * xhat, axis=-1, keepdims=True))

    return pl.pallas_call(
        body, grid=(T // tm,), name=name,
        in_specs=[_row_spec(tm, D), _const_spec((1, D)), _const_spec((D, n)), _row_spec(tm, D)]
        + [_row_spec(tm, s) for s in widths] + extra_specs,
        out_specs=[_row_spec(tm, D), _const_spec((1, D))],
        out_shape=[jax.ShapeDtypeStruct((T, D), F32), jax.ShapeDtypeStruct((1, D), F32)],
        compiler_params=_cparams(("arbitrary",), VMEM_BIG),
    )(x, g, w_bf, dres, *dps, *extra)


def _after_operand(after):
    return ([ANY], [after]) if after is not None else ([], [])


def matmul_acc_chips(at_bf, pieces, name, after=None):
    k = at_bf.shape[0]
    widths = [p.shape[1] for p in pieces]
    nb = sum(widths) // NCHIP
    tm = 512
    extra_specs, extra = _after_operand(after)

    def body(a_ref, *rest):
        o_ref = rest[-1]

        @pl.when(pl.program_id(0) == 0)
        def _():
            o_ref[...] = jnp.zeros_like(o_ref)

        a = a_ref[...]
        b = jnp.concatenate([r[...] for r in rest[:len(widths)]], axis=-1)
        for s in range(NCHIP):
            o_ref[s] += jnp.dot(a, b[:, s * nb:(s + 1) * nb], preferred_element_type=F32)

    return pl.pallas_call(
        body, grid=(T // tm,), name=name,
        in_specs=[_col_spec(k, tm)] + [_row_spec(tm, w_) for w_ in widths] + extra_specs,
        out_specs=_const_spec((NCHIP, k, nb)),
        out_shape=jax.ShapeDtypeStruct((NCHIP, k, nb), F32),
        compiler_params=_cparams(("arbitrary",), VMEM_BIG),
    )(at_bf, *pieces, *extra)


def out_proj(h, z_bf, w_bf, name):
    tm = 256

    def body(h_ref, z_ref, w_ref, o_ref):
        o_ref[...] = h_ref[...] + jnp.dot(z_ref[...], w_ref[...], preferred_element_type=F32)

    return pl.pallas_call(
        body, grid=(T // tm,), name=name,
        in_specs=[_row_spec(tm, D), _row_spec(tm, D), _const_spec((D, D))],
        out_specs=_row_spec(tm, D), out_shape=jax.ShapeDtypeStruct((T, D), F32),
        compiler_params=_cparams(("parallel",)),
    )(h, z_bf, w_bf)


def out_proj_bwd(dh, zt_bf, w_bf, name, after=None):
    tm = 256
    extra_specs, extra = _after_operand(after)

    def body(dh_ref, zt_ref, w_ref, *rest):
        dz_ref, dw_ref = rest[-2:]
        dhb = dh_ref[...].astype(BF16)
        dz_ref[...] = lax.dot_general(dhb, w_ref[...], (((1,), (1,)), ((), ())), preferred_element_type=F32)

        @pl.when(pl.program_id(0) == 0)
        def _():
            dw_ref[...] = jnp.zeros_like(dw_ref)

        dw_ref[...] += jnp.dot(zt_ref[...], dhb, preferred_element_type=F32)

    return pl.pallas_call(
        body, grid=(T // tm,), name=name,
        in_specs=[_row_spec(tm, D), _col_spec(D, tm), _const_spec((D, D))] + extra_specs,
        out_specs=[_row_spec(tm, D), _const_spec((D, D))],
        out_shape=[jax.ShapeDtypeStruct((T, D), F32), jax.ShapeDtypeStruct((D, D), F32)],
        compiler_params=_cparams(("arbitrary",)),
    )(dh, zt_bf, w_bf, *extra)


def final_loss(h, g, target):
    tm = 256

    def body(h_ref, g_ref, t_ref, dh_ref, loss_ref, dg_ref):
        xv = h_ref[...]
        rstd = lax.rsqrt(jnp.mean(xv * xv, axis=-1, keepdims=True) + RMS_EPS)
        xhat = xv * rstd
        err = xhat * g_ref[...] - t_ref[...]
        part = 0.5 * jnp.sum(jnp.mean(err * err, axis=-1, keepdims=True), axis=0, keepdims=True)
        dout = err * (1.0 / D)

        @pl.when(pl.program_id(0) == 0)
        def _():
            loss_ref[...] = jnp.zeros_like(loss_ref)
            dg_ref[...] = jnp.zeros_like(dg_ref)

        loss_ref[...] += jnp.broadcast_to(part, loss_ref.shape)
        dg_ref[...] += jnp.sum(dout * xhat, axis=0, keepdims=True)
        dxh = dout * g_ref[...]
        dh_ref[...] = rstd * (dxh - xhat * jnp.mean(dxh * xhat, axis=-1, keepdims=True))

    return pl.pallas_call(
        body, grid=(T // tm,), name="final_loss",
        in_specs=[_row_spec(tm, D), _const_spec((1, D)), _row_spec(tm, D)],
        out_specs=[_row_spec(tm, D), _const_spec((8, 128)), _const_spec((1, D))],
        out_shape=[jax.ShapeDtypeStruct((T, D), F32), jax.ShapeDtypeStruct((8, 128), F32),
                   jax.ShapeDtypeStruct((1, D), F32)],
        compiler_params=_cparams(("arbitrary",)),
    )(h, g, target)


PREP_TM = 256
PREP_NB = SEQ // PREP_TM


def _prep_elem(k, wl, apre, kkw, kaw, bd):
    wraw = -_softplus(-wl) - 0.5
    lw = -jnp.exp(wraw)
    asig = jax.nn.sigmoid(apre)
    kkr = k * kkw
    nrm = jnp.maximum(jnp.sqrt(_headsum(kkr * kkr, bd)), 1e-12)
    kk = kkr / nrm
    k2 = k * (1.0 + (asig - 1.0) * kaw)
    return lw, k2, -kk, kk * asig


def _shifted(ps_ref, prev_ref, mu, blk):
    p = ps_ref[...]
    first = (blk % PREP_NB) == 0
    prev_row = jnp.where(first, 0.0, prev_ref[7:8, :])
    rolled = pltpu.roll(p, 1, 0)
    p_prev = jnp.where(_iota2(p.shape, 0) == 0, prev_row, rolled)
    return p, p_prev, p + (p_prev - p) * mu


def _prev_spec(width, blk_of):
    return pl.BlockSpec((8, width), lambda i: (jnp.maximum(blk_of(i) * (PREP_TM // 8) - 1, 0), 0))


def even_prep(ps, mu, w0, w2x, a0, a2x, kkw, kaw):
    tm = PREP_TM

    def body(ps_ref, prev_ref, mu_ref, w0_ref, w2_ref, a0_ref, a2_ref, kk_ref, ka_ref,
             r_ref, lw_ref, k2_ref, v_ref, aa_ref, bb_ref):
        _, _, s = _shifted(ps_ref, prev_ref, mu_ref[...], pl.program_id(0))
        wa = s[:, 3 * W:]
        wl = w0_ref[...] + _bdot(jnp.tanh(wa), w2_ref[...])
        apre = a0_ref[...] + _bdot(wa, a2_ref[...])
        lw, k2, aa, bb = _prep_elem(s[:, W:2 * W], wl, apre, kk_ref[...], ka_ref[...], _head_blockdiag())
        r_ref[...] = s[:, 0:W]
        v_ref[...] = s[:, 2 * W:3 * W]
        lw_ref[...] = lw
        k2_ref[...] = k2
        aa_ref[...] = aa
        bb_ref[...] = bb

    vec = _const_spec((1, W))
    return pl.pallas_call(
        body, grid=(T // tm,), name="even_prep",
        in_specs=[_row_spec(tm, SHIFT), _prev_spec(SHIFT, lambda i: i), _const_spec((1, SHIFT)), vec,
                  _const_spec((2 * LORA, W)), vec, _const_spec((2 * LORA, W)), vec, vec],
        out_specs=[_row_spec(tm, W)] * 6,
        out_shape=[jax.ShapeDtypeStruct((T, W), F32)] * 6,
        compiler_params=_cparams(("parallel",)),
    )(ps, ps, mu, w0, w2x, a0, a2x, kkw, kaw)


def even_prep_bwd(ps, mu, w0, w2x, a0, a2x, kkw, kaw, dr, dlw, dk2, dv, daa, dbb, dr2, dk22, dv2):
    tm = PREP_TM
    nb = T // tm
    rev = lambda i: nb - 1 - i

    def body(ps_ref, prev_ref, mu_ref, w0_ref, w2_ref, a0_ref, a2_ref, kk_ref, ka_ref,
             dr_ref, dlw_ref, dk2_ref, dv_ref, daa_ref, dbb_ref, dr2_ref, dk22_ref, dv2_ref,
             dps_ref, dmu_ref, dw0_ref, dw2_ref, da0_ref, da2_ref, dkk_ref, dka_ref, carry):
        i = pl.program_id(0)
        blk = rev(i)
        mu_v = mu_ref[...]
        p, p_prev, s = _shifted(ps_ref, prev_ref, mu_v, blk)
        wa = s[:, 3 * W:]
        th = jnp.tanh(wa)
        wl = w0_ref[...] + _bdot(th, w2_ref[...])
        apre = a0_ref[...] + _bdot(wa, a2_ref[...])
        bd = _head_blockdiag()
        k = s[:, W:2 * W]
        _, vjp = jax.vjp(lambda k_, wl_, ap_, kkw_, kaw_: _prep_elem(k_, wl_, ap_, kkw_, kaw_, bd),
                         k, wl, apre, kk_ref[...], ka_ref[...])
        dk, dwl, dap, dkkw, dkaw = vjp((dlw_ref[...], dk2_ref[...] + dk22_ref[...], daa_ref[...], dbb_ref[...]))
        dwa = _bdot_nt(dwl, w2_ref[...]) * (1.0 - th * th) + _bdot_nt(dap, a2_ref[...])
        ds = jnp.concatenate([dr_ref[...] + dr2_ref[...], dk, dv_ref[...] + dv2_ref[...], dwa], axis=-1)

        @pl.when(i == 0)
        def _():
            for ref in (dmu_ref, dw0_ref, dw2_ref, da0_ref, da2_ref, dkk_ref, dka_ref, carry):
                ref[...] = jnp.zeros_like(ref)

        dmu_ref[...] += jnp.sum(ds * (p_prev - p), axis=0, keepdims=True)
        dw0_ref[...] += jnp.sum(dwl, axis=0, keepdims=True)
        da0_ref[...] += jnp.sum(dap, axis=0, keepdims=True)
        dw2_ref[...] += _bdot_tn(th, dwl)
        da2_ref[...] += _bdot_tn(wa, dap)
        dkk_ref[...] += dkkw
        dka_ref[...] += dkaw
        dsm = ds * mu_v
        last = (blk % PREP_NB) == PREP_NB - 1
        nxt = jnp.where(last, 0.0, carry[0:1, :])
        up = pltpu.roll(dsm, tm - 1, 0)
        up = jnp.where(_iota2(up.shape, 0) == tm - 1, nxt, up)
        dps_ref[...] = (ds - dsm + up).astype(BF16)
        carry[0:1, :] = dsm[0:1, :]

    vec = _const_spec((1, W))
    rrow = lambda width: pl.BlockSpec((tm, width), lambda i: (rev(i), 0))
    return pl.pallas_call(
        body, grid=(nb,), name="even_prep_bwd",
        in_specs=[rrow(SHIFT), _prev_spec(SHIFT, rev), _const_spec((1, SHIFT)), vec,
                  _const_spec((2 * LORA, W)), vec, _const_spec((2 * LORA, W)), vec, vec] + [rrow(W)] * 9,
        out_specs=[rrow(SHIFT), _const_spec((1, SHIFT)), vec, _const_spec((2 * LORA, W)), vec,
                   _const_spec((2 * LORA, W)), vec, vec],
        out_shape=[jax.ShapeDtypeStruct((T, SHIFT), BF16), jax.ShapeDtypeStruct((1, SHIFT), F32),
                   jax.ShapeDtypeStruct((1, W), F32), jax.ShapeDtypeStruct((2 * LORA, W), F32),
                   jax.ShapeDtypeStruct((1, W), F32), jax.ShapeDtypeStruct((2 * LORA, W), F32),
                   jax.ShapeDtypeStruct((1, W), F32), jax.ShapeDtypeStruct((1, W), F32)],
        scratch_shapes=[pltpu.VMEM((8, SHIFT), F32)],
        compiler_params=_cparams(("arbitrary",), VMEM_BIG),
    )(ps, ps, mu, w0, w2x, a0, a2x, kkw, kaw, dr, dlw, dk2, dv, daa, dbb, dr2, dk22, dv2)


NPAIR = NH // 2
PW = 2 * HD


def _pair_cols(p):
    return slice(p * PW, (p + 1) * PW)


def _pairs(a):
    return [a[:, _pair_cols(p)] for p in range(NPAIR)]


def _stack_pair(a):
    first = _iota2(a.shape, 1) < HD
    zero = jnp.zeros_like(a)
    return jnp.concatenate([jnp.where(first, a, zero), jnp.where(first, zero, a)], axis=0)


def _unstack_pair(a):
    n = a.shape[0] // 2
    return jnp.where(_iota2((n, PW), 1) < HD, a[:n], a[n:])


def _fold_pair(a):
    n = a.shape[0] // 2
    return a[:n] + a[n:]


def _chunk_masks():
    n = 4 * L
    row = _iota2((n, n), 0)
    col = _iota2((n, n), 1)
    same = ((row // L) & 1) == ((col // L) & 1)
    ri = row & (L - 1)
    ci = col & (L - 1)
    keep = same & (((row < 2 * L) & (ri > ci)) | ((row >= 2 * L) & (ri >= ci)))
    r1 = _iota2((L, L), 0)
    c1 = _iota2((L, L), 1)
    r2 = _iota2((2 * L, 2 * L), 0)
    c2 = _iota2((2 * L, 2 * L), 1)
    return keep.astype(F32), (r1 >= c1).astype(F32), (r2 == c2).astype(F32)


def _scaled(r, lw, k2, aa, bb, tri):
    g = _hdot(tri, lw)
    eg = jnp.exp(g)
    eng = jnp.exp(-g)
    egp = jnp.exp(g - lw)
    return eg, eng, egp, aa * egp, r * eg, bb * eng, k2 * eng


def _head_cols(h):
    return slice(h * HD, (h + 1) * HD)


def _per_head(a):
    return [a[:, _head_cols(h)] for h in range(NH)]


def _pairs_operands(at, rt, bt, kt):
    x = [jnp.concatenate([_stack_pair(a), _stack_pair(r)], axis=0).astype(BF16) for a, r in zip(_pairs(at), _pairs(rt))]
    yk = [jnp.concatenate([_stack_pair(b), _stack_pair(k)], axis=0).astype(BF16) for b, k in zip(_pairs(bt), _pairs(kt))]
    return x, yk


def _pairs_matrices(x, yk, keep, eye):
    m = [_bdot_nt(a, b) * keep for a, b in zip(x, yk)]
    p = [a[:2 * L, :2 * L] for a in m]
    tinv = [eye + a for a in p]
    for _ in range(5):
        p = [_bdot(a, a) for a in p]
        tinv = [t + _bdot(t, a) for t, a in zip(tinv, p)]
    return [a.astype(BF16) for a in m], [a.astype(BF16) for a in tinv]


def _pairs_fwd(x, yk, m, tinv, vw, s0, egl):
    xh = [_bdot_nt(a, s) for a, s in zip(x, s0)]
    u = [_bdot(t, h[:2 * L] + _bdot(a[:2 * L, 2 * L:], w)) for t, h, a, w in zip(tinv, xh, m, vw)]
    uv = [jnp.concatenate([a, w], axis=0).astype(BF16) for a, w in zip(u, vw)]
    y = [h[2 * L:] + _bdot(a[2 * L:], w) for h, a, w in zip(xh, m, uv)]
    sn = [e * (s + _bdot_tn(w, b)) for e, s, w, b in zip(egl, s0, uv, yk)]
    return y, sn, uv


def _pairs_bwd(x, yk, m, tinv, vw, s0, egl, dyw, dsn, keep):
    _, sn, uv = _pairs_fwd(x, yk, m, tinv, vw, s0, egl)
    dzs = [d * e for d, e in zip(dsn, egl)]
    dgl = [jnp.sum(d * s, axis=0, keepdims=True) for d, s in zip(dsn, sn)]
    dyb = [a.astype(BF16) for a in dyw]
    t1 = [_bdot_tn(a[2 * L:], d) for a, d in zip(m, dyb)]
    t2 = [_bdot_nt(b, d) for b, d in zip(yk, dzs)]
    drhs = [_bdot_tn(t, a[:2 * L] + b[:2 * L]) for t, a, b in zip(tinv, t1, t2)]
    dv = [a[2 * L:] + b[2 * L:] + _bdot_tn(c[:2 * L, 2 * L:], d) for a, b, c, d in zip(t1, t2, m, drhs)]
    gg = [jnp.concatenate([a, b], axis=0).astype(BF16) for a, b in zip(drhs, dyw)]
    ds0 = [d + _bdot_tn(g, a) for d, g, a in zip(dzs, gg, x)]
    dm = [_bdot_nt(g, w) * keep for g, w in zip(gg, uv)]
    dx = [_bdot(g, s) + _bdot(d, b) for g, s, d, b in zip(gg, s0, dm, yk)]
    dyk = [_bdot_tn(d, a) + _bdot(w, z) for d, a, w, z in zip(dm, x, uv, dzs)]
    return dx, dyk, dv, dgl, ds0


STATE_SHAPE = (NPAIR * PW, PW)
M_SHAPE = (4 * L, NPAIR * 4 * L)
TINV_SHAPE = (2 * L, NPAIR * 2 * L)


def _rows_of(a, n):
    return [a[i * n:(i + 1) * n, :] for i in range(NPAIR)]


def _both(f):
    out = []
    for s in range(NSEQ):
        out += f(s)
    return out


def _seq_view(a):
    return a.reshape(NSEQ, SEQ, a.shape[-1])


def rwkv_fwd(r, lw, k2, v, aa, bb):
    def body(r_ref, lw_ref, k2_ref, v_ref, aa_ref, bb_ref, y_ref, hs_ref, m_ref, t_ref, state):
        @pl.when(pl.program_id(0) == 0)
        def _():
            state[...] = jnp.zeros_like(state)

        s_all = state[...]
        hs_ref[0] = s_all
        keep, tri, eye = _chunk_masks()
        sc = [_scaled(r_ref[s], lw_ref[s], k2_ref[s], aa_ref[s], bb_ref[s], tri) for s in range(NSEQ)]
        ops = [_pairs_operands(*sc[s][3:]) for s in range(NSEQ)]
        x, yk = _both(lambda s: ops[s][0]), _both(lambda s: ops[s][1])
        m, tinv = _pairs_matrices(x, yk, keep, eye)
        vw = _both(lambda s: [_stack_pair(a) for a in _pairs(v_ref[s])])
        s0 = _both(lambda s: _rows_of(s_all[s], PW))
        y, sn, _ = _pairs_fwd(x, yk, m, tinv, vw, s0, _both(lambda s: _pairs(sc[s][0][L - 1:L, :])))
        for s in range(NSEQ):
            mine = slice(s * NPAIR, (s + 1) * NPAIR)
            y_ref[s] = jnp.concatenate([_fold_pair(a) for a in y[mine]], axis=-1)
            m_ref[0, s] = jnp.concatenate(m[mine], axis=-1)
            t_ref[0, s] = jnp.concatenate(tinv[mine], axis=-1)
            state[s] = jnp.concatenate(sn[mine], axis=0)

    blk = pl.BlockSpec((NSEQ, L, W), lambda c: (0, c, 0))
    per_chunk = lambda shape: pl.BlockSpec((1, NSEQ) + shape, lambda c: (c, 0, 0, 0))
    y, hs, ms, ts = pl.pallas_call(
        body, grid=(NC,), name="rwkv_fwd",
        in_specs=[blk] * 6,
        out_specs=[blk, per_chunk(STATE_SHAPE), per_chunk(M_SHAPE), per_chunk(TINV_SHAPE)],
        out_shape=[jax.ShapeDtypeStruct((NSEQ, SEQ, W), F32), jax.ShapeDtypeStruct((NC, NSEQ) + STATE_SHAPE, F32),
                   jax.ShapeDtypeStruct((NC, NSEQ) + M_SHAPE, BF16),
                   jax.ShapeDtypeStruct((NC, NSEQ) + TINV_SHAPE, BF16)],
        scratch_shapes=[pltpu.VMEM((NSEQ,) + STATE_SHAPE, F32)],
        compiler_params=_cparams(("arbitrary",)),
    )(*[_seq_view(a) for a in (r, lw, k2, v, aa, bb)])
    return y.reshape(T, W), hs, ms, ts


def rwkv_bwd(r, lw, k2, v, aa, bb, hs, ms, ts, dy):
    def body(r_ref, lw_ref, k2_ref, v_ref, aa_ref, bb_ref, hs_ref, m_ref, t_ref, dy_ref,
             dr_ref, dlw_ref, dk2_ref, dv_ref, daa_ref, dbb_ref, dstate):
        @pl.when(pl.program_id(0) == 0)
        def _():
            dstate[...] = jnp.zeros_like(dstate)

        keep, tri, _ = _chunk_masks()
        sc = [_scaled(r_ref[s], lw_ref[s], k2_ref[s], aa_ref[s], bb_ref[s], tri) for s in range(NSEQ)]
        ops = [_pairs_operands(*sc[s][3:]) for s in range(NSEQ)]
        x, yk = _both(lambda s: ops[s][0]), _both(lambda s: ops[s][1])
        m = _both(lambda s: [m_ref[0, s][:, i * 4 * L:(i + 1) * 4 * L] for i in range(NPAIR)])
        tinv = _both(lambda s: [t_ref[0, s][:, i * 2 * L:(i + 1) * 2 * L] for i in range(NPAIR)])
        vw = _both(lambda s: [_stack_pair(a) for a in _pairs(v_ref[s])])
        dyw = _both(lambda s: [_stack_pair(a) for a in _pairs(dy_ref[s])])
        s0 = _both(lambda s: _rows_of(hs_ref[0, s], PW))
        dsn = _both(lambda s: _rows_of(dstate[s], PW))
        egl = _both(lambda s: _pairs(sc[s][0][L - 1:L, :]))
        dx, dyk, dvw, dgl, ds0 = _pairs_bwd(x, yk, m, tinv, vw, s0, egl, dyw, dsn, keep)
        for s in range(NSEQ):
            mine = slice(s * NPAIR, (s + 1) * NPAIR)
            eg, eng, egp, at, rt, bt, kt = sc[s]
            dstate[s] = jnp.concatenate(ds0[mine], axis=0)
            dv_ref[s] = jnp.concatenate([_fold_pair(a) for a in dvw[mine]], axis=-1)
            dat = jnp.concatenate([_fold_pair(a[:2 * L]) for a in dx[mine]], axis=-1)
            drt = jnp.concatenate([_fold_pair(a[2 * L:]) for a in dx[mine]], axis=-1)
            dbt = jnp.concatenate([_fold_pair(a[:2 * L]) for a in dyk[mine]], axis=-1)
            dkt = jnp.concatenate([_fold_pair(a[2 * L:]) for a in dyk[mine]], axis=-1)
            dg = drt * rt - dbt * bt - dkt * kt
            dg = dg + jnp.where(_iota2(dg.shape, 0) == L - 1, jnp.concatenate(dgl[mine], axis=-1), 0.0)
            dgp = dat * at
            dlw_ref[s] = _hdot_tn(tri, dg + dgp) - dgp
            dr_ref[s] = drt * eg
            daa_ref[s] = dat * egp
            dbb_ref[s] = dbt * eng
            dk2_ref[s] = dkt * eng

    blk = pl.BlockSpec((NSEQ, L, W), lambda c: (0, NC - 1 - c, 0))
    per_chunk = lambda shape: pl.BlockSpec((1, NSEQ) + shape, lambda c: (NC - 1 - c, 0, 0, 0))
    outs = pl.pallas_call(
        body, grid=(NC,), name="rwkv_bwd",
        in_specs=[blk] * 6 + [per_chunk(STATE_SHAPE), per_chunk(M_SHAPE), per_chunk(TINV_SHAPE), blk],
        out_specs=[blk] * 6,
        out_shape=[jax.ShapeDtypeStruct((NSEQ, SEQ, W), F32)] * 6,
        scratch_shapes=[pltpu.VMEM((NSEQ,) + STATE_SHAPE, F32)],
        compiler_params=_cparams(("arbitrary",)),
    )(*[_seq_view(a) for a in (r, lw, k2, v, aa, bb)], hs, ms, ts, _seq_view(dy))
    return [a.reshape(T, W) for a in outs]


def _post_math(y, r, k2, v, ga, o, gb, lng, lnb, rk, bd):
    mu = _headsum(y, bd) * (1.0 / HD)
    yc = y - mu
    var = _headsum(yc * yc, bd) * (1.0 / HD)
    yn = yc * lax.rsqrt(var + GN_EPS) * lng + lnb
    bonus = _headsum(r * k2 * rk, bd) * v
    return (yn + bonus) * _silu(ga), o * _silu(gb)


def even_post(y, r, k2, v, ga, o, gb, lng, lnb, rk):
    tm = 256

    def body(y_ref, r_ref, k2_ref, v_ref, ga_ref, o_ref, gb_ref, lng_ref, lnb_ref, rk_ref, z_ref, zt_ref):
        ya, yb = _post_math(y_ref[...], r_ref[...], k2_ref[...], v_ref[...], ga_ref[...], o_ref[...], gb_ref[...],
                            lng_ref[...], lnb_ref[...], rk_ref[...], _head_blockdiag())
        ya, yb = ya.astype(BF16), yb.astype(BF16)
        z_ref[:, 0:W] = ya
        z_ref[:, W:2 * W] = yb
        zt_ref[0:W, :] = ya.T
        zt_ref[W:2 * W, :] = yb.T

    vec = _const_spec((1, W))
    return pl.pallas_call(
        body, grid=(T // tm,), name="even_post",
        in_specs=[_row_spec(tm, W)] * 7 + [vec] * 3,
        out_specs=[_row_spec(tm, D), _col_spec(D, tm)],
        out_shape=[jax.ShapeDtypeStruct((T, D), BF16), jax.ShapeDtypeStruct((D, T), BF16)],
        compiler_params=_cparams(("parallel",)),
    )(y, r, k2, v, ga, o, gb, lng, lnb, rk)


def even_post_bwd(y, r, k2, v, ga, o, gb, lng, lnb, rk, dz):
    tm = 256

    def body(y_ref, r_ref, k2_ref, v_ref, ga_ref, o_ref, gb_ref, lng_ref, lnb_ref, rk_ref, dz_ref,
             dy_ref, dr_ref, dk2_ref, dv_ref, dga_ref, do_ref, dgb_ref, dlng_ref, dlnb_ref, drk_ref):
        bd = _head_blockdiag()
        _, vjp = jax.vjp(lambda *a: _post_math(*a, bd), y_ref[...], r_ref[...], k2_ref[...], v_ref[...], ga_ref[...],
                         o_ref[...], gb_ref[...], lng_ref[...], lnb_ref[...], rk_ref[...])
        dzv = dz_ref[...]
        dy, dr, dk2, dv, dga, do, dgb, dlng, dlnb, drk = vjp((dzv[:, 0:W], dzv[:, W:2 * W]))
        for ref, val in ((dy_ref, dy), (dr_ref, dr), (dk2_ref, dk2), (dv_ref, dv), (dga_ref, dga), (do_ref, do),
                         (dgb_ref, dgb)):
            ref[...] = val.astype(ref.dtype)

        @pl.when(pl.program_id(0) == 0)
        def _():
            for ref in (dlng_ref, dlnb_ref, drk_ref):
                ref[...] = jnp.zeros_like(ref)

        dlng_ref[...] += dlng
        dlnb_ref[...] += dlnb
        drk_ref[...] += drk

    vec = _const_spec((1, W))
    return pl.pallas_call(
        body, grid=(T // tm,), name="even_post_bwd",
        in_specs=[_row_spec(tm, W)] * 7 + [vec] * 3 + [_row_spec(tm, D)],
        out_specs=[_row_spec(tm, W)] * 7 + [vec] * 3,
        out_shape=[jax.ShapeDtypeStruct((T, W), dt) for dt in (F32, F32, F32, F32, BF16, F32, BF16)]
        + [jax.ShapeDtypeStruct((1, W), F32)] * 3,
        compiler_params=_cparams(("arbitrary",)),
    )(y, r, k2, v, ga, o, gb, lng, lnb, rk, dz)


PADSEQ = SEQ + LEFT * L
ATT_SCALE = 1.0 / math.sqrt(HD)


def _att_probs(q2, kw, bias, c):
    valid = _iota2((1, BAND), 1) >= (LEFT - c) * L
    s = [jnp.where(valid, _bdot_nt(a, b) * ATT_SCALE + bias[p], NEG) for p, (a, b) in enumerate(zip(q2, kw))]
    e = [jnp.exp(a - jnp.max(a, axis=-1, keepdims=True)) for a in s]
    return [a / jnp.sum(a, axis=-1, keepdims=True) for a in e]


def attention_fwd(q, kpad, vpad, bias):
    def body(q_ref, k_ref, v_ref, b_ref, o_ref):
        c = pl.program_id(1)
        start = pl.multiple_of(c * L, L)
        kw = _pairs(k_ref[pl.ds(start, BAND), :])
        vw = _pairs(v_ref[pl.ds(start, BAND), :])
        q2 = [_stack_pair(a) for a in _pairs(q_ref[...].astype(BF16))]
        p = _att_probs(q2, kw, b_ref[...], c)
        o_ref[...] = jnp.concatenate([_unstack_pair(_bdot(a, b)) for a, b in zip(p, vw)], axis=-1)

    qblk = pl.BlockSpec((L, W), lambda b, c: (b * NC + c, 0))
    kblk = pl.BlockSpec((PADSEQ, W), lambda b, c: (b, 0))
    return pl.pallas_call(
        body, grid=(NSEQ, NC), name="attention_fwd",
        in_specs=[qblk, kblk, kblk, _const_spec((NPAIR, 2 * L, BAND))],
        out_specs=qblk, out_shape=jax.ShapeDtypeStruct((T, W), F32),
        compiler_params=_cparams(("parallel", "arbitrary")),
    )(q, kpad, vpad, bias)


def attention_bwd(q, kpad, vpad, bias, do):
    def body(q_ref, k_ref, v_ref, b_ref, do_ref, dq_ref, dko_ref, dvo_ref, db_ref, dk_ref, dv_ref):
        b = pl.program_id(0)
        c = pl.program_id(1)

        @pl.when(c == 0)
        def _():
            dk_ref[...] = jnp.zeros_like(dk_ref)
            dv_ref[...] = jnp.zeros_like(dv_ref)

        @pl.when((c == 0) & (b == 0))
        def _():
            db_ref[...] = jnp.zeros_like(db_ref)

        start = pl.multiple_of(c * L, L)
        kw = _pairs(k_ref[pl.ds(start, BAND), :])
        vw = _pairs(v_ref[pl.ds(start, BAND), :])
        q2 = [_stack_pair(a) for a in _pairs(q_ref[...].astype(BF16))]
        do2 = [_stack_pair(a) for a in _pairs(do_ref[...].astype(BF16))]
        p = _att_probs(q2, kw, b_ref[...], c)
        dp = [_bdot_nt(a, b) for a, b in zip(do2, vw)]
        ds = [a * (d - jnp.sum(d * a, axis=-1, keepdims=True)) for a, d in zip(p, dp)]
        dss = [(a * ATT_SCALE).astype(BF16) for a in ds]
        dq_ref[...] = jnp.concatenate([_unstack_pair(_bdot(a, b)) for a, b in zip(dss, kw)], axis=-1).astype(BF16)
        dk_ref[pl.ds(start, BAND), :] += jnp.concatenate([_bdot_tn(a, b) for a, b in zip(dss, q2)], axis=-1)
        dv_ref[pl.ds(start, BAND), :] += jnp.concatenate([_bdot_tn(a, b) for a, b in zip(p, do2)], axis=-1)
        for i in range(NPAIR):
            db_ref[i] += ds[i]

        @pl.when(c == NC - 1)
        def _():
            dko_ref[...] = dk_ref[LEFT * L:, :].astype(BF16)
            dvo_ref[...] = dv_ref[LEFT * L:, :].astype(BF16)

    qblk = pl.BlockSpec((L, W), lambda b, c: (b * NC + c, 0))
    kblk = pl.BlockSpec((PADSEQ, W), lambda b, c: (b, 0))
    sblk = pl.BlockSpec((SEQ, W), lambda b, c: (b, 0))
    bblk = _const_spec((NPAIR, 2 * L, BAND))
    return pl.pallas_call(
        body, grid=(NSEQ, NC), name="attention_bwd",
        in_specs=[qblk, kblk, kblk, bblk, qblk],
        out_specs=[qblk, sblk, sblk, bblk],
        out_shape=[jax.ShapeDtypeStruct((T, W), BF16), jax.ShapeDtypeStruct((T, W), BF16),
                   jax.ShapeDtypeStruct((T, W), BF16), jax.ShapeDtypeStruct((NPAIR, 2 * L, BAND), F32)],
        scratch_shapes=[pltpu.VMEM((PADSEQ, W), F32), pltpu.VMEM((PADSEQ, W), F32)],
        compiler_params=_cparams(("arbitrary", "arbitrary"), VMEM_BIG),
    )(q, kpad, vpad, bias, do)


NTAB = 2 * CLIP + 1
EXT = BAND + L


def _ext_onehot():
    n = _iota2((EXT, NTAB), 0)
    m = _iota2((EXT, NTAB), 1)
    return (jnp.clip(BAND - 1 - n, -CLIP, CLIP) + CLIP == m).astype(F32)


def bias_expand(table):
    def body(t_ref, o_ref):
        ext = _hdot_nt(t_ref[...], _ext_onehot())
        for i in range(L):
            s = L - 1 - i
            o_ref[:, i, :] = (pltpu.roll(ext, EXT - s, 1) if s else ext)[:, :BAND]

    return pl.pallas_call(body, name="bias_expand", out_shape=jax.ShapeDtypeStruct((NH, L, BAND), F32))(table)


def bias_grad(dbias):
    def body(d_ref, o_ref):
        acc = jnp.zeros((NH, EXT), F32)
        zpad = jnp.zeros((NH, EXT - BAND), F32)
        for i in range(L):
            s = L - 1 - i
            row = jnp.concatenate([d_ref[:, i, :], zpad], axis=-1)
            acc = acc + (pltpu.roll(row, s, 1) if s else row)
        o_ref[...] = _hdot(acc, _ext_onehot())

    return pl.pallas_call(body, name="bias_grad", out_shape=jax.ShapeDtypeStruct((NH, NTAB), F32))(dbias)


def _group_cols(g):
    return slice(g * SGC, (g + 1) * SGC)


def _sg_norm(v, lng, lnb):
    gv = _gelu(v)
    gc = gv - jnp.mean(gv, axis=-1, keepdims=True)
    rstd = lax.rsqrt(jnp.mean(gc * gc, axis=-1, keepdims=True) + LN_EPS)
    xhat = gc * rstd
    return xhat, rstd, xhat * lng + lnb


def gmlp_fwd(u, v, gate, lng, lnb, wm_bf, sgb_t):
    def body(u_ref, v_ref, gt_ref, lng_ref, lnb_ref, wm_ref, sb_ref, z_ref, zt_ref):
        _, _, vln = _sg_norm(v_ref[...], lng_ref[...], lnb_ref[...])
        vlb = vln.astype(BF16)
        for g in range(NG):
            cs = _group_cols(g)
            sv = jnp.dot(wm_ref[g], vlb[:, cs], preferred_element_type=F32) + sb_ref[:, g:g + 1]
            zg = (_gelu(u_ref[:, cs]) * sv * _silu(gt_ref[:, cs])).astype(BF16)
            z_ref[:, cs] = zg
            zt_ref[cs, :] = zg.T

    return pl.pallas_call(
        body, grid=(T // SGC,), name="gmlp_fwd",
        in_specs=[_row_spec(SGC, D)] * 3 + [_const_spec((1, D))] * 2 + [_const_spec((NG, SGC, SGC)),
                                                                      _const_spec((SGC, NG))],
        out_specs=[_row_spec(SGC, D), _col_spec(D, SGC)],
        out_shape=[jax.ShapeDtypeStruct((T, D), BF16), jax.ShapeDtypeStruct((D, T), BF16)],
        compiler_params=_cparams(("parallel",)),
    )(u, v, gate, lng, lnb, wm_bf, sgb_t)


def gmlp_bwd(u, v, gate, lng, lnb, wm_bf, sgb_t, dz):
    def body(u_ref, v_ref, gt_ref, lng_ref, lnb_ref, wm_ref, sb_ref, dz_ref,
             du_ref, dv_ref, dgt_ref, dlng_ref, dlnb_ref, dwm_ref, dsb_ref):
        @pl.when(pl.program_id(0) == 0)
        def _():
            for ref in (dlng_ref, dlnb_ref, dwm_ref, dsb_ref):
                ref[...] = jnp.zeros_like(ref)

        vv = v_ref[...]
        xhat, rstd, vln = _sg_norm(vv, lng_ref[...], lnb_ref[...])
        vlb = vln.astype(BF16)
        dvln = []
        dsv_all = []
        for g in range(NG):
            cs = _group_cols(g)
            uu = u_ref[:, cs]
            gg = gt_ref[:, cs]
            dzz = dz_ref[:, cs]
            sv = jnp.dot(wm_ref[g], vlb[:, cs], preferred_element_type=F32) + sb_ref[:, g:g + 1]
            gu = _gelu(uu)
            sg = _silu(gg)
            dsv = dzz * gu * sg
            dgt_ref[:, cs] = (dzz * gu * sv * _dsilu(gg)).astype(BF16)
            du_ref[:, cs] = (dzz * sv * sg * _dgelu(uu)).astype(BF16)
            dsb16 = dsv.astype(BF16)
            dvln.append(lax.dot_general(wm_ref[g], dsb16, (((0,), (0,)), ((), ())), preferred_element_type=F32))
            dwm_ref[g] += lax.dot_general(dsb16, vlb[:, cs], (((1,), (1,)), ((), ())), preferred_element_type=F32)
            dsv_all.append(dsv)
        dvl = jnp.concatenate(dvln, axis=-1)
        dsv_cat = jnp.concatenate(dsv_all, axis=-1)
        sel = (_iota2((D, NG), 0) // SGC == _iota2((D, NG), 1)).astype(F32)
        dsb_ref[...] += _hdot(dsv_cat, sel)
        dlng_ref[...] += jnp.sum(dvl * xhat, axis=0, keepdims=True)
        dlnb_ref[...] += jnp.sum(dvl, axis=0, keepdims=True)
        dxh = dvl * lng_ref[...]
        dgv = rstd * (dxh - jnp.mean(dxh, axis=-1, keepdims=True)
                      - xhat * jnp.mean(dxh * xhat, axis=-1, keepdims=True))
        dv_ref[...] = (dgv * _dgelu(vv)).astype(BF16)

    return pl.pallas_call(
        body, grid=(T // SGC,), name="gmlp_bwd",
        in_specs=[_row_spec(SGC, D)] * 3 + [_const_spec((1, D))] * 2
        + [_const_spec((NG, SGC, SGC)), _const_spec((SGC, NG)), _row_spec(SGC, D)],
        out_specs=[_row_spec(SGC, D)] * 3 + [_const_spec((1, D))] * 2 + [_const_spec((NG, SGC, SGC)),
                                                                       _const_spec((SGC, NG))],
        out_shape=[jax.ShapeDtypeStruct((T, D), BF16)] * 3 + [jax.ShapeDtypeStruct((1, D), F32)] * 2
        + [jax.ShapeDtypeStruct((NG, SGC, SGC), F32), jax.ShapeDtypeStruct((SGC, NG), F32)],
        compiler_params=_cparams(("arbitrary",)),
    )(u, v, gate, lng, lnb, wm_bf, sgb_t, dz)


NCHIP = 4
NDEV = 8
ANY = pl.BlockSpec(memory_space=pl.ANY)


HBM = pl.BlockSpec(memory_space=pltpu.HBM)
SEM = pl.BlockSpec(memory_space=pltpu.SEMAPHORE)
EFFECT = pltpu.SideEffectType.DATAFLOW_SIDE_EFFECTING


def _peers(whole_mesh):
    x, y, c = lax.axis_index("x"), lax.axis_index("y"), lax.axis_index("c")
    if not whole_mesh:
        return [((px, py, c), 2 * px + py) for px, py in ((1 - x, y), (x, 1 - y), (1 - x, 1 - y))], 2 * x + y
    out = []
    for j in range(1, NDEV):
        px, py, pc = x ^ (j >> 2), y ^ ((j >> 1) & 1), c ^ (j & 1)
        out.append(((px, py, pc), 4 * px + 2 * py + pc))
    return out, 4 * x + 2 * y + c


def _send_copies(src, land, send, recv, scatter, whole_mesh, starting):
    peers, me = _peers(whole_mesh)
    copies = []
    for t in range(len(src)):
        for j, (dev, slot) in enumerate(peers):
            k = t * len(peers) + j
            copies.append(pltpu.make_async_remote_copy(
                src_ref=src[t].at[slot] if scatter else src[t], dst_ref=land[t].at[me if starting else slot],
                send_sem=send.at[k], recv_sem=recv.at[k], device_id=dev, device_id_type=MESH))
    return copies


def send_start(srcs, lands, scatter, whole_mesh, name):
    n = len(srcs)
    nsem = n * (NDEV - 1 if whole_mesh else NCHIP - 1)

    def body(*refs):
        for cp in _send_copies(refs[:n], refs[n:2 * n], refs[2 * n], refs[2 * n + 1], scatter, whole_mesh, True):
            cp.start()
        refs[-1][...] = jnp.zeros_like(refs[-1])

    arrs = list(srcs) + list(lands)
    out = pl.pallas_call(
        body, name=name,
        out_shape=(pltpu.SemaphoreType.DMA((nsem,)), pltpu.SemaphoreType.DMA((nsem,)),
                   *[pltpu.HBM(a.shape, a.dtype) for a in arrs], jax.ShapeDtypeStruct((8, 128), F32)),
        in_specs=[HBM] * (2 * n), out_specs=(SEM, SEM, *[HBM] * (2 * n), pl.BlockSpec(memory_space=pltpu.VMEM)),
        input_output_aliases={i: 2 + i for i in range(2 * n)},
        compiler_params=pltpu.CompilerParams(has_side_effects=EFFECT),
    )(*[pltpu.with_memory_space_constraint(a, pltpu.HBM) for a in arrs])
    return out[0], out[1], list(out[2:2 + n]), list(out[2 + n:2 + 2 * n]), out[-1]


def send_wait(started, after, scatter, whole_mesh, name):
    send, recv, srcs, lands, _ = started
    n = len(srcs)

    def body(*refs):
        for cp in _send_copies(refs[:n], refs[n:2 * n], refs[2 * n], refs[2 * n + 1], scatter, whole_mesh, False):
            cp.wait_send()
            cp.wait_recv()

    arrs = list(srcs) + list(lands)
    out = pl.pallas_call(
        body, name=name, out_shape=tuple(pltpu.HBM(a.shape, a.dtype) for a in arrs),
        in_specs=[HBM] * (2 * n) + [SEM, SEM, ANY], out_specs=tuple([HBM] * (2 * n)),
        input_output_aliases={i: i for i in range(2 * n)},
        compiler_params=pltpu.CompilerParams(has_side_effects=EFFECT),
    )(*arrs, send, recv, after)
    return list(out[n:])


def exchange_c(arrs, name):
    n = len(arrs)

    def body(*refs):
        ins, outs = refs[:n], refs[n:2 * n]
        send, recv = refs[2 * n:]
        sibling = (lax.axis_index("x"), lax.axis_index("y"), 1 - lax.axis_index("c"))
        copies = [pltpu.make_async_remote_copy(src_ref=ins[t], dst_ref=outs[t], send_sem=send.at[t], recv_sem=recv.at[t],
                                               device_id=sibling, device_id_type=MESH) for t in range(n)]
        for cp in copies:
            cp.start()
        for cp in copies:
            cp.wait()

    return pl.pallas_call(
        body, name=name, in_specs=[ANY] * n, out_specs=[ANY] * n,
        out_shape=[jax.ShapeDtypeStruct(a.shape, a.dtype) for a in arrs],
        scratch_shapes=[pltpu.SemaphoreType.DMA((n,)), pltpu.SemaphoreType.DMA((n,))],
    )(*arrs)


def gather_weights(arrs, split):
    n = len(arrs)

    def body(*refs):
        ins, outs = refs[:n], refs[n:2 * n]
        send1, recv1, send2, recv2, loc = refs[2 * n:]
        x, y, c = lax.axis_index("x"), lax.axis_index("y"), lax.axis_index("c")
        me = 2 * x + y
        sibling = (x, y, 1 - c)
        peers = [(1 - x, y), (x, 1 - y), (1 - x, 1 - y)]

        def rows_of(t, core):
            half = arrs[t].shape[0] // 2
            return pl.ds(core * half, half)

        def part(ref, t, core):
            return ref.at[rows_of(t, core)] if split[t] else ref

        local = [pltpu.make_async_copy(ins[t], outs[t].at[me], loc.at[t]) for t in range(n)]
        for cp in local:
            cp.start()
        first = []
        for t in range(n):
            for j, (px, py) in enumerate(peers):
                first.append(pltpu.make_async_remote_copy(
                    src_ref=part(ins[t], t, c), dst_ref=part(outs[t].at[me], t, c), send_sem=send1.at[t, j],
                    recv_sem=recv1.at[t, j], device_id=(px, py, c), device_id_type=MESH))
        for cp in first:
            cp.start()
        passed = []
        for t in range(n):
            for j, (px, py) in enumerate(peers):
                landed = part(outs[t].at[2 * px + py], t, c)
                pltpu.make_async_remote_copy(
                    src_ref=landed, dst_ref=landed, send_sem=send1.at[t, j], recv_sem=recv1.at[t, j],
                    device_id=(x, y, c), device_id_type=MESH).wait_recv()
                if split[t]:
                    cp = pltpu.make_async_remote_copy(
                        src_ref=landed, dst_ref=landed, send_sem=send2.at[t, j], recv_sem=recv2.at[t, j],
                        device_id=sibling, device_id_type=MESH)
                    cp.start()
                    passed.append(cp)
        for t in range(n):
            for j, (px, py) in enumerate(peers):
                if split[t]:
                    other = part(outs[t].at[2 * px + py], t, 1 - c)
                    pltpu.make_async_remote_copy(
                        src_ref=other, dst_ref=other, send_sem=send2.at[t, j], recv_sem=recv2.at[t, j],
                        device_id=(x, y, c), device_id_type=MESH).wait_recv()
        for cp in first + passed:
            cp.wait_send()
        for cp in local:
            cp.wait()

    return pl.pallas_call(
        body, name="gather_weights", in_specs=[ANY] * n, out_specs=[ANY] * n,
        out_shape=[jax.ShapeDtypeStruct((NCHIP,) + a.shape, a.dtype) for a in arrs],
        scratch_shapes=[pltpu.SemaphoreType.DMA((n, 3))] * 4 + [pltpu.SemaphoreType.DMA((n,))],
    )(*arrs)


def _adam_math(g, w, m, v):
    m = ADAM_B1 * m + (1.0 - ADAM_B1) * g
    v = ADAM_B2 * v + (1.0 - ADAM_B2) * (g * g)
    m_hat = m / (1.0 - ADAM_B1 ** ADAM_STEP)
    v_hat = v / (1.0 - ADAM_B2 ** ADAM_STEP)
    delta = -ADAM_LR * (m_hat / (jnp.sqrt(v_hat) + ADAM_EPS) + ADAM_WD * w)
    return delta, m, v


def _rows_tile(rows):
    return rows if rows <= 256 else 256


def sum_chips(own, parts, name):
    _, rows, cols = parts.shape
    tr = _rows_tile(rows)

    def body(own_ref, p_ref, o_ref):
        acc = own_ref[...]
        for s in range(NCHIP):
            acc = acc + p_ref[s].astype(F32)
        o_ref[...] = acc

    return pl.pallas_call(
        body, grid=(rows // tr,), name=name,
        in_specs=[pl.BlockSpec((tr, cols), lambda i: (i, 0)), pl.BlockSpec((NCHIP, tr, cols), lambda i: (0, i, 0))],
        out_specs=pl.BlockSpec((tr, cols), lambda i: (i, 0)),
        out_shape=jax.ShapeDtypeStruct((rows, cols), F32),
        compiler_params=_cparams(("parallel",)),
    )(own, parts)


def adam_shard(p_mine, p_sib, w, m, v, name):
    rows, cols = w.shape
    tr = _rows_tile(rows)

    def body(a_ref, b_ref, w_ref, m_ref, v_ref, g_ref, d_ref, mo_ref, vo_ref):
        g = a_ref[...] + b_ref[...]
        g_ref[...] = g
        d_ref[...], mo_ref[...], vo_ref[...] = _adam_math(g, w_ref[...], m_ref[...], v_ref[...])

    spec = pl.BlockSpec((tr, cols), lambda i: (i, 0))
    return pl.pallas_call(
        body, grid=(rows // tr,), name=name, in_specs=[spec] * 5, out_specs=[spec] * 4,
        out_shape=[jax.ShapeDtypeStruct((rows, cols), F32)] * 4,
        compiler_params=_cparams(("parallel",)),
    )(p_mine, p_sib, w, m, v)


def adam_replicated(parts, w, m, v, name):
    rows = w.shape[0]

    def body(p_ref, w_ref, m_ref, v_ref, g_ref, d_ref, mo_ref, vo_ref):
        g = p_ref[0]
        for d in range(1, NDEV):
            g = g + p_ref[d]
        g_ref[...] = g
        d_ref[...], mo_ref[...], vo_ref[...] = _adam_math(g, w_ref[...], m_ref[...], v_ref[...])

    return pl.pallas_call(
        body, name=name, out_shape=[jax.ShapeDtypeStruct((rows, 128), F32)] * 4,
    )(parts, w, m, v)


def _pack(arrs):
    pieces = []
    for a in arrs:
        flat = a.reshape(-1)
        pad = (-flat.shape[0]) % 128
        pieces.append(jnp.pad(flat, (0, pad)) if pad else flat)
    flat = jnp.concatenate(pieces)
    pad = (-flat.shape[0]) % 1024
    return jnp.pad(flat, (0, pad)).reshape(-1, 128)


def _unpack(buf, shapes):
    flat = buf.reshape(-1)
    out = []
    o = 0
    for s in shapes:
        n = int(np.prod(s))
        out.append(flat[o:o + n].reshape(s))
        o += n + (-n) % 128
    return out


EVEN_SPLITS = (SHIFT, W, W, W, W, W)
ODD_SPLITS = (D, D, D)


def _cols_to_chips(a):
    rows, cols = a.shape
    return a.reshape(rows, NCHIP, cols // NCHIP).transpose(1, 0, 2)


def _chips_to_cols(a):
    _, rows, n = a.shape
    return a.transpose(1, 0, 2).reshape(rows, NCHIP * n)


def kernel(x, norm_g, w_in_e, shift_mu, rw_w0, rw_w2, rw_a0, rw_a2, rw_kk, rw_ka, rw_rk, rw_lnx_g, rw_lnx_b, att_bias, w_out_e, w_in_o, sg_ln_g, sg_ln_b, sg_w, sg_b, w_out_o, final_g, loss_target, m_norm_g, m_w_in_e, m_shift_mu, m_rw_w0, m_rw_w2, m_rw_a0, m_rw_a2, m_rw_kk, m_rw_ka, m_rw_rk, m_rw_lnx_g, m_rw_lnx_b, m_att_bias, m_w_out_e, m_w_in_o, m_sg_ln_g, m_sg_ln_b, m_sg_w, m_sg_b, m_w_out_o, m_final_g, v_norm_g, v_w_in_e, v_shift_mu, v_rw_w0, v_rw_w2, v_rw_a0, v_rw_a2, v_rw_kk, v_rw_ka, v_rw_rk, v_rw_lnx_g, v_rw_lnx_b, v_att_bias, v_w_out_e, v_w_in_o, v_sg_ln_g, v_sg_ln_b, v_sg_w, v_sg_b, v_w_out_o, v_final_g):
    x2 = x.reshape(T, D)
    tgt = loss_target.reshape(T, D)

    my_chip = 2 * lax.axis_index("x") + lax.axis_index("y")
    gathered = gather_weights(
        [w_in_e[0].astype(BF16), jnp.concatenate([rw_w2[0], rw_a2[0]], axis=0),
         jnp.concatenate([sg_ln_g, sg_ln_b], axis=0)], [True, True, False])
    wie = _chips_to_cols(gathered[0])
    w2 = _chips_to_cols(gathered[1][:, :LORA])
    a2 = _chips_to_cols(gathered[1][:, LORA:])
    sglg = _chips_to_cols(gathered[2][:, 0:1])
    sglb = _chips_to_cols(gathered[2][:, 1:2])

    late = [w_out_e[0].astype(BF16), w_in_o[0].astype(BF16), w_out_o[0].astype(BF16)]
    late_started = send_start(late, [jnp.broadcast_to(a[None], (NCHIP,) + a.shape) for a in late], False, False,
                              "late_weights_start")

    def late_weights(after):
        woe, wio, woo = send_wait(late_started, after, False, False, "late_weights_wait")
        return woe.reshape(D, D), _chips_to_cols(wio), woo.reshape(D, D)

    def scatter_start(grads, name):
        srcs = [g_.astype(BF16) if g_.shape[-1] >= W else g_ for g_ in grads]
        return send_start(srcs, [jnp.zeros_like(s) for s in srcs], True, False, name)

    def own_block(g_):
        return lax.dynamic_index_in_dim(g_, my_chip, axis=0, keepdims=False)

    started = {}

    def on_odd_grads(d_woo, d_wio):
        blocks = [d_woo.reshape(NCHIP, D // NCHIP, D), d_wio]
        started["odd"] = (scatter_start(blocks, "odd_grads_start"), [own_block(b) for b in blocks])
        return started["odd"][0][-1]

    def on_even_grads(big_g):
        d_wie, d_woe, _, _, d_w2, d_a2, d_sglg, d_sglb = big_g
        blocks = [d_wie, d_woe.reshape(NCHIP, D // NCHIP, D), _cols_to_chips(d_w2), _cols_to_chips(d_a2),
                  _cols_to_chips(d_sglg), _cols_to_chips(d_sglb)]
        started["even"] = (scatter_start(blocks, "even_grads_start"), [own_block(b) for b in blocks])
        return started["even"][0][-1]

    def on_small_grads(layer, grads):
        mine = _pack(grads)
        started[layer + "_small"] = send_start([mine], [jnp.broadcast_to(mine[None], (NDEV,) + mine.shape)], False,
                                               True, layer + "_small_grads_start")
        return started[layer + "_small"][-1]

    loss_part, dx, _, _ = _local_step(
        x2, tgt, wie, late_weights, w2, a2, sglg, sglb, norm_g, shift_mu, rw_w0, rw_a0, rw_kk, rw_ka, rw_rk,
        rw_lnx_g, rw_lnx_b, att_bias, sg_w, sg_b, final_g, first_after=late_started[-1], on_odd_grads=on_odd_grads,
        on_even_grads=on_even_grads, on_small_grads=on_small_grads)
    loss = lax.psum(loss_part, ("x", "y", "c"))
    even_started, even_own = started["even"]

    wmv = {"w_in_e": (w_in_e[0], m_w_in_e[0], v_w_in_e[0]), "w_out_e": (w_out_e[0], m_w_out_e[0], v_w_out_e[0]),
           "w_in_o": (w_in_o[0], m_w_in_o[0], v_w_in_o[0]), "w_out_o": (w_out_o[0], m_w_out_o[0], v_w_out_o[0]),
           "rw_w2": (rw_w2[0], m_rw_w2[0], v_rw_w2[0]), "rw_a2": (rw_a2[0], m_rw_a2[0], v_rw_a2[0]),
           "sg_ln_g": (sg_ln_g, m_sg_ln_g, v_sg_ln_g), "sg_ln_b": (sg_ln_b, m_sg_ln_b, v_sg_ln_b)}
    sharded = {}

    def finish(names, own, landed, tag):
        partial = [sum_chips(o_, p_, "sum_" + nm) for o_, p_, nm in zip(own, landed, names)]
        from_sibling = exchange_c(partial, "swap_partials_" + tag)
        for nm, mine, sib in zip(names, partial, from_sibling):
            w_, m_, v_ = wmv[nm]
            res = adam_shard(mine, sib, w_, m_, v_, "adam_" + nm)
            lead = nm not in ("sg_ln_g", "sg_ln_b")
            sharded[nm] = [a[None] if lead else a for a in res]
        return partial[0]

    odd_started, odd_own = started["odd"]
    odd_landed = send_wait(odd_started, started["even_small"][-1], True, False, "odd_grads_wait")
    done = finish(["w_out_o", "w_in_o"], odd_own, odd_landed, "odd")

    groups = {
        "odd": (["sg_w", "sg_b", "final_g", "norm_g1"], [sg_w, sg_b, final_g, norm_g[1:2]],
                [m_sg_w, m_sg_b, m_final_g, m_norm_g[1:2]], [v_sg_w, v_sg_b, v_final_g, v_norm_g[1:2]]),
        "even": (["norm_g0", "shift_mu", "rw_w0", "rw_a0", "rw_kk", "rw_ka", "rw_rk", "rw_lnx_g", "rw_lnx_b",
                  "att_bias"],
                 [norm_g[0:1], shift_mu, rw_w0, rw_a0, rw_kk, rw_ka, rw_rk, rw_lnx_g, rw_lnx_b, att_bias],
                 [m_norm_g[0:1], m_shift_mu, m_rw_w0, m_rw_a0, m_rw_kk, m_rw_ka, m_rw_rk, m_rw_lnx_g, m_rw_lnx_b,
                  m_att_bias],
                 [v_norm_g[0:1], v_shift_mu, v_rw_w0, v_rw_a0, v_rw_kk, v_rw_ka, v_rw_rk, v_rw_lnx_g, v_rw_lnx_b,
                  v_att_bias]),
    }
    rep = {}
    for layer in ("odd", "even"):
        nms, ws, ms_, vs_ = groups[layer]
        (gathered_g,) = send_wait(started[layer + "_small"], done, False, True, layer + "_small_grads_wait")
        rep_out = adam_replicated(gathered_g, _pack(ws), _pack(ms_), _pack(vs_), "adam_" + layer + "_small")
        done = rep_out[0]
        for nm in nms:
            rep[nm] = []
        for buf in rep_out:
            for nm, a in zip(nms, _unpack(buf, [w_.shape for w_ in ws])):
                rep[nm].append(a)
    rep["norm_g"] = [jnp.concatenate([a, b], axis=0) for a, b in zip(rep["norm_g0"], rep["norm_g1"])]
    even_landed = send_wait(even_started, done, True, False, "even_grads_wait")
    finish(["w_in_e", "w_out_e", "rw_w2", "rw_a2", "sg_ln_g", "sg_ln_b"], even_own, even_landed, "even")

    order = ["norm_g", "w_in_e", "shift_mu", "rw_w0", "rw_w2", "rw_a0", "rw_a2", "rw_kk", "rw_ka", "rw_rk",
             "rw_lnx_g", "rw_lnx_b", "att_bias", "w_out_e", "w_in_o", "sg_ln_g", "sg_ln_b", "sg_w", "sg_b",
             "w_out_o", "final_g"]
    results = {**sharded, **rep}
    outs = [loss, dx.reshape(NSEQ, SEQ, D)]
    for kind in range(4):
        outs += [results[nm][kind] for nm in order]
    return tuple(outs)


def _local_step(x2, tgt, wie, late_weights, w2, a2, sglg, sglb, norm_g, shift_mu, rw_w0, rw_a0, rw_kk, rw_ka, rw_rk,
                rw_lnx_g, rw_lnx_b, att_bias, sg_w, sg_b, final_g, first_after=None, on_odd_grads=None,
                on_even_grads=None, on_small_grads=None):
    zl = jnp.zeros((LORA, W), F32)
    w2x = jnp.concatenate([w2, zl], axis=0)
    a2x = jnp.concatenate([zl, a2], axis=0)
    rk = rw_rk.reshape(1, W)
    pos = np.arange(SGC)
    sg_mask = jnp.asarray(((pos[None, :] // L) <= (pos[:, None] // L)).astype(np.float32))
    wm = (sg_w[0] * sg_mask[None]).astype(BF16)
    sgb_t = sg_b[0].T

    xn0, ps, ga, q, kb, vb, gb = ln_in_proj(x2, norm_g[0:1], wie, EVEN_SPLITS, "in_proj_even", after=first_after)
    r, lw, k2, v, aa, bb = even_prep(ps, shift_mu, rw_w0, w2x, rw_a0, a2x, rw_kk, rw_ka)
    y, hs, ms, ts = rwkv_fwd(r, lw, k2, v, aa, bb)
    bias = bias_expand(att_bias[0]).reshape(NPAIR, 2 * L, BAND)

    def padded(a):
        return jnp.pad(a.astype(BF16).reshape(NSEQ, SEQ, W), ((0, 0), (LEFT * L, 0), (0, 0))).reshape(NSEQ * PADSEQ, W)

    kpad, vpad = padded(kb), padded(vb)
    o = attention_fwd(q, kpad, vpad, bias)
    z, zt = even_post(y, r, k2, v, ga, o, gb, rw_lnx_g, rw_lnx_b, rk)
    woe, wio, woo = late_weights(z)
    h1 = out_proj(x2, z, woe, "out_proj_even")
    xn1, u, vv, gt = ln_in_proj(h1, norm_g[1:2], wio, ODD_SPLITS, "in_proj_odd")
    z2, z2t = gmlp_fwd(u, vv, gt, sglg, sglb, wm, sgb_t)
    h2 = out_proj(h1, z2, woo, "out_proj_odd")
    dh2, loss_part, d_final_g = final_loss(h2, final_g[None], tgt)

    dz2, d_woo = out_proj_bwd(dh2, z2t, woo, "out_proj_odd_bwd")
    du, dvv, dgt, d_sglg, d_sglb, d_wm, d_sgb_t = gmlp_bwd(u, vv, gt, sglg, sglb, wm, sgb_t, dz2)
    dp_odd = [du, dvv, dgt]
    d_wio = matmul_acc_chips(xn1, dp_odd, "in_proj_odd_dw")
    token = on_odd_grads(d_woo, d_wio) if on_odd_grads else None
    dh1, d_g1 = in_proj_bwd_x(h1, norm_g[1:2], wio, dp_odd, dh2, "in_proj_odd_bwd", after=token)
    odd_small = [d_wm * sg_mask[None], d_sgb_t.T, d_final_g, d_g1]
    token = on_small_grads("odd", odd_small) if on_small_grads else None
    dz, d_woe = out_proj_bwd(dh1, zt, woe, "out_proj_even_bwd", after=token)
    dy, dr2, dk22, dv2, dga, do, dgb, d_lng, d_lnb, d_rk = even_post_bwd(
        y, r, k2, v, ga, o, gb, rw_lnx_g, rw_lnx_b, rk, dz)
    dq, dkb, dvb, dbias = attention_bwd(q, kpad, vpad, bias, do)
    d_att_bias = bias_grad(dbias.reshape(NH, L, BAND))
    dr, dlw, dk2, dv, daa, dbb = rwkv_bwd(r, lw, k2, v, aa, bb, hs, ms, ts, dy)
    dps, d_mu, d_w0, d_w2x, d_a0, d_a2x, d_kk, d_ka = even_prep_bwd(
        ps, shift_mu, rw_w0, w2x, rw_a0, a2x, rw_kk, rw_ka, dr, dlw, dk2, dv, daa, dbb, dr2, dk22, dv2)
    dp_even = [dps, dga, dq, dkb, dvb, dgb]
    d_wie = matmul_acc_chips(xn0, dp_even, "in_proj_even_dw")
    big_g = (d_wie, d_woe, d_wio, d_woo, d_w2x[:LORA], d_a2x[LORA:], d_sglg, d_sglb)
    token = on_even_grads(big_g) if on_even_grads else None
    dx, d_g0 = in_proj_bwd_x(x2, norm_g[0:1], wie, dp_even, dh1, "in_proj_even_bwd", after=token)
    even_small = [d_g0, d_mu, d_w0, d_a0, d_kk, d_ka, d_rk, d_lng, d_lnb, d_att_bias]
    if on_small_grads:
        on_small_grads("even", even_small)
    rep_g = [jnp.concatenate([d_g0, d_g1], axis=0)] + even_small[1:] + odd_small[:3]
    return loss_part[0, 0], dx, big_g, rep_g
```

```python
import functools
import math

import jax
import jax.numpy as jnp
import numpy as np
from jax import lax
from jax.experimental import pallas as pl
from jax.experimental.pallas import tpu as pltpu

F32 = jnp.float32
BF16 = jnp.bfloat16
HI = lax.Precision.HIGHEST

D = 1024
SEQ = 2048
NSEQ = 2
T = NSEQ * SEQ
HD = 64
NH = 8
W = 512
SHIFT = 1664
LORA = 64
EVEN_IN = 4224
ODD_IN = 3072
L = 64
NC = SEQ // L
LEFT = 8
BAND = (LEFT + 1) * L
CLIP = 128
SGC = 128
NG = 8
RMS_EPS = 1e-6
LN_EPS = 1e-5
GN_EPS = 64e-5
NEG = -1e30
VMEM_BIG = 56 * 1024 * 1024

ADAM_LR = 0.001
ADAM_B1 = 0.9
ADAM_B2 = 0.999
ADAM_EPS = 1e-08
ADAM_WD = 0.01
ADAM_STEP = 10

MESH = pl.DeviceIdType.MESH


def _bdot(a, b):
    return jnp.dot(a.astype(BF16), b.astype(BF16), preferred_element_type=F32)


def _bdot_nt(a, b):
    return lax.dot_general(a.astype(BF16), b.astype(BF16), (((1,), (1,)), ((), ())), preferred_element_type=F32)


def _bdot_tn(a, b):
    return lax.dot_general(a.astype(BF16), b.astype(BF16), (((0,), (0,)), ((), ())), preferred_element_type=F32)


def _hdot(a, b):
    return jnp.dot(a, b, precision=HI, preferred_element_type=F32)


def _hdot_nt(a, b):
    return lax.dot_general(a, b, (((1,), (1,)), ((), ())), precision=HI, preferred_element_type=F32)


def _hdot_tn(a, b):
    return lax.dot_general(a, b, (((0,), (0,)), ((), ())), precision=HI, preferred_element_type=F32)


def _iota2(shape, dim):
    return lax.broadcasted_iota(jnp.int32, shape, dim)


def _head_blockdiag():
    r = _iota2((W, W), 0) // HD
    c = _iota2((W, W), 1) // HD
    return (r == c).astype(BF16)


def _headsum_impl(x, bd):
    hi = x.astype(BF16)
    mid = (x - hi.astype(F32)).astype(BF16)
    return jnp.dot(hi, bd, preferred_element_type=F32) + jnp.dot(mid, bd, preferred_element_type=F32)


@jax.custom_vjp
def _headsum(x, bd):
    return _headsum_impl(x, bd)


def _headsum_fwd(x, bd):
    return _headsum_impl(x, bd), bd


def _headsum_bwd(bd, ct):
    return _headsum_impl(ct, bd), None


_headsum.defvjp(_headsum_fwd, _headsum_bwd)


def _silu(x):
    return x * jax.nn.sigmoid(x)


def _dsilu(x):
    s = jax.nn.sigmoid(x)
    return s * (1.0 + x * (1.0 - s))


_GELU_C = math.sqrt(2.0 / math.pi)


def _gelu(x):
    return 0.5 * x * (1.0 + jnp.tanh(_GELU_C * (x + 0.044715 * (x * x * x))))


def _dgelu(x):
    t = jnp.tanh(_GELU_C * (x + 0.044715 * (x * x * x)))
    return 0.5 * (1.0 + t) + 0.5 * x * (1.0 - t * t) * _GELU_C * (1.0 + 3.0 * 0.044715 * x * x)


def _softplus(x):
    return jnp.maximum(x, 0.0) + jnp.log(1.0 + jnp.exp(-jnp.abs(x)))


def _cparams(sem, vmem=None):
    return pltpu.CompilerParams(dimension_semantics=sem, vmem_limit_bytes=vmem)


def _row_spec(tm, width):
    return pl.BlockSpec((tm, width), lambda i: (i, 0))


def _col_spec(height, tm):
    return pl.BlockSpec((height, tm), lambda i: (0, i))


def _const_spec(shape):
    nd = len(shape)
    return pl.BlockSpec(shape, lambda *_: (0,) * nd)


def ln_in_proj(x, g, w_bf, splits, name, after=None):
    n = w_bf.shape[1]
    tm = 256
    spans = []
    o = 0
    for s in splits:
        spans.append((o, o + s))
        o += s
    assert o == n
    extra_specs, extra = _after_operand(after)

    def body(x_ref, g_ref, w_ref, *rest):
        xn_ref, outs = rest[len(extra)], rest[len(extra) + 1:]
        xv = x_ref[...]
        rstd = lax.rsqrt(jnp.mean(xv * xv, axis=-1, keepdims=True) + RMS_EPS)
        xn = (xv * rstd * g_ref[...]).astype(BF16)
        xn_ref[...] = xn.T
        p = jnp.dot(xn, w_ref[...], preferred_element_type=F32)
        for o_ref, (a, b) in zip(outs, spans):
            o_ref[...] = p[:, a:b]

    return pl.pallas_call(
        body, grid=(T // tm,), name=name,
        in_specs=[_row_spec(tm, D), _const_spec((1, D)), _const_spec((D, n))] + extra_specs,
        out_specs=[_col_spec(D, tm)] + [_row_spec(tm, s) for s in splits],
        out_shape=[jax.ShapeDtypeStruct((D, T), BF16)] + [jax.ShapeDtypeStruct((T, s), F32) for s in splits],
        compiler_params=_cparams(("parallel",), VMEM_BIG),
    )(x, g, w_bf, *extra)


def in_proj_bwd_x(x, g, w_bf, dps, dres, name, after=None):
    n = w_bf.shape[1]
    tm = 256
    widths = [d.shape[1] for d in dps]
    extra_specs, extra = _after_operand(after)

    def body(x_ref, g_ref, w_ref, dres_ref, *rest):
        dp_refs = rest[:len(widths)]
        dx_ref, dg_ref = rest[-2:]
        dp = jnp.concatenate([r[...] for r in dp_refs], axis=-1)
        dxn = lax.dot_general(dp, w_ref[...], (((1,), (1,)), ((), ())), preferred_element_type=F32)
        xv = x_ref[...]
        rstd = lax.rsqrt(jnp.mean(xv * xv, axis=-1, keepdims=True) + RMS_EPS)
        xhat = xv * rstd
        dgp = jnp.sum(dxn * xhat, axis=0, keepdims=True)

        @pl.when(pl.program_id(0) == 0)
        def _():
            dg_ref[...] = jnp.zeros_like(dg_ref)

        dg_ref[...] += dgp
        dxh = dxn * g_ref[...]
        dx_ref[...] = dres_ref[...] + rstd * (dxh - xhat * jnp.mean(dxh * xhat, axis=-1, keepdims=True))

    return pl.pallas_call(
        body, grid=(T // tm,), name=name,
        in_specs=[_row_spec(tm, D), _const_spec((1, D)), _const_spec((D, n)), _row_spec(tm, D)]
        + [_row_spec(tm, s) for s in widths] + extra_specs,
        out_specs=[_row_spec(tm, D), _const_spec((1, D))],
        out_shape=[jax.ShapeDtypeStruct((T, D), F32), jax.ShapeDtypeStruct((1, D), F32)],
        compiler_params=_cparams(("arbitrary",), VMEM_BIG),
    )(x, g, w_bf, dres, *dps, *extra)


def _after_operand(after):
    return ([ANY], [after]) if after is not None else ([], [])


def matmul_acc_chips(at_bf, pieces, name, after=None):
    k = at_bf.shape[0]
    widths = [p.shape[1] for p in pieces]
    nb = sum(widths) // NCHIP
    tm = 512
    steps = T // tm
    extra_specs, extra = _after_operand(after)

    def body(a_ref, *rest):
        o_ref, acc = rest[-2:]

        @pl.when(pl.program_id(0) == 0)
        def _():
            acc[...] = jnp.zeros_like(acc)

        a = a_ref[...]
        b = jnp.concatenate([r[...] for r in rest[:len(widths)]], axis=-1)
        for s in range(NCHIP):
            acc[s] += jnp.dot(a, b[:, s * nb:(s + 1) * nb], preferred_element_type=F32)

        @pl.when(pl.program_id(0) == steps - 1)
        def _():
            o_ref[...] = acc[...].astype(BF16)

    return pl.pallas_call(
        body, grid=(steps,), name=name,
        in_specs=[_col_spec(k, tm)] + [_row_spec(tm, w_) for w_ in widths] + extra_specs,
        out_specs=_const_spec((NCHIP, k, nb)),
        out_shape=jax.ShapeDtypeStruct((NCHIP, k, nb), BF16),
        scratch_shapes=[pltpu.VMEM((NCHIP, k, nb), F32)],
        compiler_params=_cparams(("arbitrary",), VMEM_BIG),
    )(at_bf, *pieces, *extra)


def out_proj(h, z_bf, w_bf, name):
    tm = 256

    def body(h_ref, z_ref, w_ref, o_ref):
        o_ref[...] = h_ref[...] + jnp.dot(z_ref[...], w_ref[...], preferred_element_type=F32)

    return pl.pallas_call(
        body, grid=(T // tm,), name=name,
        in_specs=[_row_spec(tm, D), _row_spec(tm, D), _const_spec((D, D))],
        out_specs=_row_spec(tm, D), out_shape=jax.ShapeDtypeStruct((T, D), F32),
        compiler_params=_cparams(("parallel",)),
    )(h, z_bf, w_bf)


def out_proj_bwd(dh, zt_bf, w_bf, name, after=None):
    tm = 256
    extra_specs, extra = _after_operand(after)

    def body(dh_ref, zt_ref, w_ref, *rest):
        dz_ref, dw_ref = rest[-2:]
        dhb = dh_ref[...].astype(BF16)
        dz_ref[...] = lax.dot_general(dhb, w_ref[...], (((1,), (1,)), ((), ())), preferred_element_type=F32)

        @pl.when(pl.program_id(0) == 0)
        def _():
            dw_ref[...] = jnp.zeros_like(dw_ref)

        dw_ref[...] += jnp.dot(zt_ref[...], dhb, preferred_element_type=F32)

    return pl.pallas_call(
        body, grid=(T // tm,), name=name,
        in_specs=[_row_spec(tm, D), _col_spec(D, tm), _const_spec((D, D))] + extra_specs,
        out_specs=[_row_spec(tm, D), _const_spec((D, D))],
        out_shape=[jax.ShapeDtypeStruct((T, D), F32), jax.ShapeDtypeStruct((D, D), F32)],
        compiler_params=_cparams(("arbitrary",)),
    )(dh, zt_bf, w_bf, *extra)


def out_proj_loss(h, z_bf, w_bf, g, target):
    tm = 256

    def body(h_ref, z_ref, w_ref, g_ref, t_ref, dh_ref, loss_ref, dg_ref):
        xv = h_ref[...] + jnp.dot(z_ref[...], w_ref[...], preferred_element_type=F32)
        rstd = lax.rsqrt(jnp.mean(xv * xv, axis=-1, keepdims=True) + RMS_EPS)
        xhat = xv * rstd
        err = xhat * g_ref[...] - t_ref[...]
        part = 0.5 * jnp.sum(jnp.mean(err * err, axis=-1, keepdims=True), axis=0, keepdims=True)
        dout = err * (1.0 / D)

        @pl.when(pl.program_id(0) == 0)
        def _():
            loss_ref[...] = jnp.zeros_like(loss_ref)
            dg_ref[...] = jnp.zeros_like(dg_ref)

        loss_ref[...] += jnp.broadcast_to(part, loss_ref.shape)
        dg_ref[...] += jnp.sum(dout * xhat, axis=0, keepdims=True)
        dxh = dout * g_ref[...]
        dh_ref[...] = rstd * (dxh - xhat * jnp.mean(dxh * xhat, axis=-1, keepdims=True))

    return pl.pallas_call(
        body, grid=(T // tm,), name="out_proj_loss",
        in_specs=[_row_spec(tm, D), _row_spec(tm, D), _const_spec((D, D)), _const_spec((1, D)), _row_spec(tm, D)],
        out_specs=[_row_spec(tm, D), _const_spec((8, 128)), _const_spec((1, D))],
        out_shape=[jax.ShapeDtypeStruct((T, D), F32), jax.ShapeDtypeStruct((8, 128), F32),
                   jax.ShapeDtypeStruct((1, D), F32)],
        compiler_params=_cparams(("arbitrary",)),
    )(h, z_bf, w_bf, g, target)


PREP_TM = 256
PREP_NB = SEQ // PREP_TM


def _prep_elem(k, wl, apre, kkw, kaw, bd):
    wraw = -_softplus(-wl) - 0.5
    lw = -jnp.exp(wraw)
    asig = jax.nn.sigmoid(apre)
    kkr = k * kkw
    nrm = jnp.maximum(jnp.sqrt(_headsum(kkr * kkr, bd)), 1e-12)
    kk = kkr / nrm
    k2 = k * (1.0 + (asig - 1.0) * kaw)
    return lw, k2, -kk, kk * asig


def _shifted(ps_ref, prev_ref, mu, blk):
    p = ps_ref[...]
    first = (blk % PREP_NB) == 0
    prev_row = jnp.where(first, 0.0, prev_ref[7:8, :])
    rolled = pltpu.roll(p, 1, 0)
    p_prev = jnp.where(_iota2(p.shape, 0) == 0, prev_row, rolled)
    return p, p_prev, p + (p_prev - p) * mu


def _prev_spec(width, blk_of):
    return pl.BlockSpec((8, width), lambda i: (jnp.maximum(blk_of(i) * (PREP_TM // 8) - 1, 0), 0))


def even_prep(ps, mu, w0, w2x, a0, a2x, kkw, kaw):
    tm = PREP_TM

    def body(ps_ref, prev_ref, mu_ref, w0_ref, w2_ref, a0_ref, a2_ref, kk_ref, ka_ref,
             r_ref, lw_ref, k2_ref, v_ref, aa_ref, bb_ref):
        _, _, s = _shifted(ps_ref, prev_ref, mu_ref[...], pl.program_id(0))
        wa = s[:, 3 * W:]
        wl = w0_ref[...] + _bdot(jnp.tanh(wa), w2_ref[...])
        apre = a0_ref[...] + _bdot(wa, a2_ref[...])
        lw, k2, aa, bb = _prep_elem(s[:, W:2 * W], wl, apre, kk_ref[...], ka_ref[...], _head_blockdiag())
        r_ref[...] = s[:, 0:W]
        v_ref[...] = s[:, 2 * W:3 * W]
        lw_ref[...] = lw
        k2_ref[...] = k2
        aa_ref[...] = aa
        bb_ref[...] = bb

    vec = _const_spec((1, W))
    return pl.pallas_call(
        body, grid=(T // tm,), name="even_prep",
        in_specs=[_row_spec(tm, SHIFT), _prev_spec(SHIFT, lambda i: i), _const_spec((1, SHIFT)), vec,
                  _const_spec((2 * LORA, W)), vec, _const_spec((2 * LORA, W)), vec, vec],
        out_specs=[_row_spec(tm, W)] * 6,
        out_shape=[jax.ShapeDtypeStruct((T, W), F32)] * 6,
        compiler_params=_cparams(("parallel",)),
    )(ps, ps, mu, w0, w2x, a0, a2x, kkw, kaw)


def even_prep_bwd(ps, mu, w0, w2x, a0, a2x, kkw, kaw, dr, dlw, dk2, dv, daa, dbb, dr2, dk22, dv2):
    tm = PREP_TM
    nb = T // tm
    rev = lambda i: nb - 1 - i

    def body(ps_ref, prev_ref, mu_ref, w0_ref, w2_ref, a0_ref, a2_ref, kk_ref, ka_ref,
             dr_ref, dlw_ref, dk2_ref, dv_ref, daa_ref, dbb_ref, dr2_ref, dk22_ref, dv2_ref,
             dps_ref, dmu_ref, dw0_ref, dw2_ref, da0_ref, da2_ref, dkk_ref, dka_ref, carry):
        i = pl.program_id(0)
        blk = rev(i)
        mu_v = mu_ref[...]
        p, p_prev, s = _shifted(ps_ref, prev_ref, mu_v, blk)
        wa = s[:, 3 * W:]
        th = jnp.tanh(wa)
        wl = w0_ref[...] + _bdot(th, w2_ref[...])
        apre = a0_ref[...] + _bdot(wa, a2_ref[...])
        bd = _head_blockdiag()
        k = s[:, W:2 * W]
        _, vjp = jax.vjp(lambda k_, wl_, ap_, kkw_, kaw_: _prep_elem(k_, wl_, ap_, kkw_, kaw_, bd),
                         k, wl, apre, kk_ref[...], ka_ref[...])
        dk, dwl, dap, dkkw, dkaw = vjp((dlw_ref[...], dk2_ref[...] + dk22_ref[...], daa_ref[...], dbb_ref[...]))
        dwa = _bdot_nt(dwl, w2_ref[...]) * (1.0 - th * th) + _bdot_nt(dap, a2_ref[...])
        ds = jnp.concatenate([dr_ref[...] + dr2_ref[...], dk, dv_ref[...] + dv2_ref[...], dwa], axis=-1)

        @pl.when(i == 0)
        def _():
            for ref in (dmu_ref, dw0_ref, dw2_ref, da0_ref, da2_ref, dkk_ref, dka_ref, carry):
                ref[...] = jnp.zeros_like(ref)

        dmu_ref[...] += jnp.sum(ds * (p_prev - p), axis=0, keepdims=True)
        dw0_ref[...] += jnp.sum(dwl, axis=0, keepdims=True)
        da0_ref[...] += jnp.sum(dap, axis=0, keepdims=True)
        dw2_ref[...] += _bdot_tn(th, dwl)
        da2_ref[...] += _bdot_tn(wa, dap)
        dkk_ref[...] += dkkw
        dka_ref[...] += dkaw
        dsm = ds * mu_v
        last = (blk % PREP_NB) == PREP_NB - 1
        nxt = jnp.where(last, 0.0, carry[0:1, :])
        up = pltpu.roll(dsm, tm - 1, 0)
        up = jnp.where(_iota2(up.shape, 0) == tm - 1, nxt, up)
        dps_ref[...] = (ds - dsm + up).astype(BF16)
        carry[0:1, :] = dsm[0:1, :]

    vec = _const_spec((1, W))
    rrow = lambda width: pl.BlockSpec((tm, width), lambda i: (rev(i), 0))
    return pl.pallas_call(
        body, grid=(nb,), name="even_prep_bwd",
        in_specs=[rrow(SHIFT), _prev_spec(SHIFT, rev), _const_spec((1, SHIFT)), vec,
                  _const_spec((2 * LORA, W)), vec, _const_spec((2 * LORA, W)), vec, vec] + [rrow(W)] * 9,
        out_specs=[rrow(SHIFT), _const_spec((1, SHIFT)), vec, _const_spec((2 * LORA, W)), vec,
                   _const_spec((2 * LORA, W)), vec, vec],
        out_shape=[jax.ShapeDtypeStruct((T, SHIFT), BF16), jax.ShapeDtypeStruct((1, SHIFT), F32),
                   jax.ShapeDtypeStruct((1, W), F32), jax.ShapeDtypeStruct((2 * LORA, W), F32),
                   jax.ShapeDtypeStruct((1, W), F32), jax.ShapeDtypeStruct((2 * LORA, W), F32),
                   jax.ShapeDtypeStruct((1, W), F32), jax.ShapeDtypeStruct((1, W), F32)],
        scratch_shapes=[pltpu.VMEM((8, SHIFT), F32)],
        compiler_params=_cparams(("arbitrary",), VMEM_BIG),
    )(ps, ps, mu, w0, w2x, a0, a2x, kkw, kaw, dr, dlw, dk2, dv, daa, dbb, dr2, dk22, dv2)


NPAIR = NH // 2
PW = 2 * HD


def _pair_cols(p):
    return slice(p * PW, (p + 1) * PW)


def _pairs(a):
    return [a[:, _pair_cols(p)] for p in range(NPAIR)]


def _stack_pair(a):
    first = _iota2(a.shape, 1) < HD
    zero = jnp.zeros_like(a)
    return jnp.concatenate([jnp.where(first, a, zero), jnp.where(first, zero, a)], axis=0)


def _unstack_pair(a):
    n = a.shape[0] // 2
    return jnp.where(_iota2((n, PW), 1) < HD, a[:n], a[n:])


def _fold_pair(a):
    n = a.shape[0] // 2
    return a[:n] + a[n:]


def _chunk_masks():
    n = 4 * L
    row = _iota2((n, n), 0)
    col = _iota2((n, n), 1)
    same = ((row // L) & 1) == ((col // L) & 1)
    ri = row & (L - 1)
    ci = col & (L - 1)
    keep = same & (((row < 2 * L) & (ri > ci)) | ((row >= 2 * L) & (ri >= ci)))
    r1 = _iota2((L, L), 0)
    c1 = _iota2((L, L), 1)
    r2 = _iota2((2 * L, 2 * L), 0)
    c2 = _iota2((2 * L, 2 * L), 1)
    return keep.astype(F32), (r1 >= c1).astype(F32), (r2 == c2).astype(F32)


def _scaled(r, lw, k2, aa, bb, tri):
    g = _hdot(tri, lw)
    eg = jnp.exp(g)
    eng = jnp.exp(-g)
    egp = jnp.exp(g - lw)
    return eg, eng, egp, aa * egp, r * eg, bb * eng, k2 * eng


def _head_cols(h):
    return slice(h * HD, (h + 1) * HD)


def _per_head(a):
    return [a[:, _head_cols(h)] for h in range(NH)]


def _pairs_operands(at, rt, bt, kt):
    x = [jnp.concatenate([_stack_pair(a), _stack_pair(r)], axis=0).astype(BF16) for a, r in zip(_pairs(at), _pairs(rt))]
    yk = [jnp.concatenate([_stack_pair(b), _stack_pair(k)], axis=0).astype(BF16) for b, k in zip(_pairs(bt), _pairs(kt))]
    return x, yk


def _pairs_matrices(x, yk, keep, eye):
    m = [_bdot_nt(a, b) * keep for a, b in zip(x, yk)]
    p = [a[:2 * L, :2 * L] for a in m]
    tinv = [eye + a for a in p]
    for _ in range(5):
        p = [_bdot(a, a) for a in p]
        tinv = [t + _bdot(t, a) for t, a in zip(tinv, p)]
    return [a.astype(BF16) for a in m], [a.astype(BF16) for a in tinv]


def _pairs_fwd(x, yk, m, tinv, vw, s0, egl):
    xh = [_bdot_nt(a, s) for a, s in zip(x, s0)]
    u = [_bdot(t, h[:2 * L] + _bdot(a[:2 * L, 2 * L:], w)) for t, h, a, w in zip(tinv, xh, m, vw)]
    uv = [jnp.concatenate([a, w], axis=0).astype(BF16) for a, w in zip(u, vw)]
    y = [h[2 * L:] + _bdot(a[2 * L:], w) for h, a, w in zip(xh, m, uv)]
    sn = [e * (s + _bdot_tn(w, b)) for e, s, w, b in zip(egl, s0, uv, yk)]
    return y, sn, uv


def _pairs_bwd(x, yk, m, tinv, vw, s0, egl, dyw, dsn, keep):
    _, sn, uv = _pairs_fwd(x, yk, m, tinv, vw, s0, egl)
    dzs = [d * e for d, e in zip(dsn, egl)]
    dgl = [jnp.sum(d * s, axis=0, keepdims=True) for d, s in zip(dsn, sn)]
    dyb = [a.astype(BF16) for a in dyw]
    t1 = [_bdot_tn(a[2 * L:], d) for a, d in zip(m, dyb)]
    t2 = [_bdot_nt(b, d) for b, d in zip(yk, dzs)]
    drhs = [_bdot_tn(t, a[:2 * L] + b[:2 * L]) for t, a, b in zip(tinv, t1, t2)]
    dv = [a[2 * L:] + b[2 * L:] + _bdot_tn(c[:2 * L, 2 * L:], d) for a, b, c, d in zip(t1, t2, m, drhs)]
    gg = [jnp.concatenate([a, b], axis=0).astype(BF16) for a, b in zip(drhs, dyw)]
    ds0 = [d + _bdot_tn(g, a) for d, g, a in zip(dzs, gg, x)]
    dm = [_bdot_nt(g, w) * keep for g, w in zip(gg, uv)]
    dx = [_bdot(g, s) + _bdot(d, b) for g, s, d, b in zip(gg, s0, dm, yk)]
    dyk = [_bdot_tn(d, a) + _bdot(w, z) for d, a, w, z in zip(dm, x, uv, dzs)]
    return dx, dyk, dv, dgl, ds0


STATE_SHAPE = (NPAIR * PW, PW)
M_SHAPE = (4 * L, NPAIR * 4 * L)
TINV_SHAPE = (2 * L, NPAIR * 2 * L)


def _rows_of(a, n):
    return [a[i * n:(i + 1) * n, :] for i in range(NPAIR)]


def _both(f):
    out = []
    for s in range(NSEQ):
        out += f(s)
    return out


def _seq_view(a):
    return a.reshape(NSEQ, SEQ, a.shape[-1])


def rwkv_fwd(r, lw, k2, v, aa, bb):
    def body(r_ref, lw_ref, k2_ref, v_ref, aa_ref, bb_ref, y_ref, hs_ref, m_ref, t_ref, state):
        @pl.when(pl.program_id(0) == 0)
        def _():
            state[...] = jnp.zeros_like(state)

        s_all = state[...]
        hs_ref[0] = s_all
        keep, tri, eye = _chunk_masks()
        sc = [_scaled(r_ref[s], lw_ref[s], k2_ref[s], aa_ref[s], bb_ref[s], tri) for s in range(NSEQ)]
        ops = [_pairs_operands(*sc[s][3:]) for s in range(NSEQ)]
        x, yk = _both(lambda s: ops[s][0]), _both(lambda s: ops[s][1])
        m, tinv = _pairs_matrices(x, yk, keep, eye)
        vw = _both(lambda s: [_stack_pair(a) for a in _pairs(v_ref[s])])
        s0 = _both(lambda s: _rows_of(s_all[s], PW))
        y, sn, _ = _pairs_fwd(x, yk, m, tinv, vw, s0, _both(lambda s: _pairs(sc[s][0][L - 1:L, :])))
        for s in range(NSEQ):
            mine = slice(s * NPAIR, (s + 1) * NPAIR)
            y_ref[s] = jnp.concatenate([_fold_pair(a) for a in y[mine]], axis=-1)
            m_ref[0, s] = jnp.concatenate(m[mine], axis=-1)
            t_ref[0, s] = jnp.concatenate(tinv[mine], axis=-1)
            state[s] = jnp.concatenate(sn[mine], axis=0)

    blk = pl.BlockSpec((NSEQ, L, W), lambda c: (0, c, 0))
    per_chunk = lambda shape: pl.BlockSpec((1, NSEQ) + shape, lambda c: (c, 0, 0, 0))
    y, hs, ms, ts = pl.pallas_call(
        body, grid=(NC,), name="rwkv_fwd",
        in_specs=[blk] * 6,
        out_specs=[blk, per_chunk(STATE_SHAPE), per_chunk(M_SHAPE), per_chunk(TINV_SHAPE)],
        out_shape=[jax.ShapeDtypeStruct((NSEQ, SEQ, W), F32), jax.ShapeDtypeStruct((NC, NSEQ) + STATE_SHAPE, F32),
                   jax.ShapeDtypeStruct((NC, NSEQ) + M_SHAPE, BF16),
                   jax.ShapeDtypeStruct((NC, NSEQ) + TINV_SHAPE, BF16)],
        scratch_shapes=[pltpu.VMEM((NSEQ,) + STATE_SHAPE, F32)],
        compiler_params=_cparams(("arbitrary",)),
    )(*[_seq_view(a) for a in (r, lw, k2, v, aa, bb)])
    return y.reshape(T, W), hs, ms, ts


def rwkv_bwd(r, lw, k2, v, aa, bb, hs, ms, ts, dy):
    def body(r_ref, lw_ref, k2_ref, v_ref, aa_ref, bb_ref, hs_ref, m_ref, t_ref, dy_ref,
             dr_ref, dlw_ref, dk2_ref, dv_ref, daa_ref, dbb_ref, dstate):
        @pl.when(pl.program_id(0) == 0)
        def _():
            dstate[...] = jnp.zeros_like(dstate)

        keep, tri, _ = _chunk_masks()
        sc = [_scaled(r_ref[s], lw_ref[s], k2_ref[s], aa_ref[s], bb_ref[s], tri) for s in range(NSEQ)]
        ops = [_pairs_operands(*sc[s][3:]) for s in range(NSEQ)]
        x, yk = _both(lambda s: ops[s][0]), _both(lambda s: ops[s][1])
        m = _both(lambda s: [m_ref[0, s][:, i * 4 * L:(i + 1) * 4 * L] for i in range(NPAIR)])
        tinv = _both(lambda s: [t_ref[0, s][:, i * 2 * L:(i + 1) * 2 * L] for i in range(NPAIR)])
        vw = _both(lambda s: [_stack_pair(a) for a in _pairs(v_ref[s])])
        dyw = _both(lambda s: [_stack_pair(a) for a in _pairs(dy_ref[s])])
        s0 = _both(lambda s: _rows_of(hs_ref[0, s], PW))
        dsn = _both(lambda s: _rows_of(dstate[s], PW))
        egl = _both(lambda s: _pairs(sc[s][0][L - 1:L, :]))
        dx, dyk, dvw, dgl, ds0 = _pairs_bwd(x, yk, m, tinv, vw, s0, egl, dyw, dsn, keep)
        for s in range(NSEQ):
            mine = slice(s * NPAIR, (s + 1) * NPAIR)
            eg, eng, egp, at, rt, bt, kt = sc[s]
            dstate[s] = jnp.concatenate(ds0[mine], axis=0)
            dv_ref[s] = jnp.concatenate([_fold_pair(a) for a in dvw[mine]], axis=-1)
            dat = jnp.concatenate([_fold_pair(a[:2 * L]) for a in dx[mine]], axis=-1)
            drt = jnp.concatenate([_fold_pair(a[2 * L:]) for a in dx[mine]], axis=-1)
            dbt = jnp.concatenate([_fold_pair(a[:2 * L]) for a in dyk[mine]], axis=-1)
            dkt = jnp.concatenate([_fold_pair(a[2 * L:]) for a in dyk[mine]], axis=-1)
            dg = drt * rt - dbt * bt - dkt * kt
            dg = dg + jnp.where(_iota2(dg.shape, 0) == L - 1, jnp.concatenate(dgl[mine], axis=-1), 0.0)
            dgp = dat * at
            dlw_ref[s] = _hdot_tn(tri, dg + dgp) - dgp
            dr_ref[s] = drt * eg
            daa_ref[s] = dat * egp
            dbb_ref[s] = dbt * eng
            dk2_ref[s] = dkt * eng

    blk = pl.BlockSpec((NSEQ, L, W), lambda c: (0, NC - 1 - c, 0))
    per_chunk = lambda shape: pl.BlockSpec((1, NSEQ) + shape, lambda c: (NC - 1 - c, 0, 0, 0))
    outs = pl.pallas_call(
        body, grid=(NC,), name="rwkv_bwd",
        in_specs=[blk] * 6 + [per_chunk(STATE_SHAPE), per_chunk(M_SHAPE), per_chunk(TINV_SHAPE), blk],
        out_specs=[blk] * 6,
        out_shape=[jax.ShapeDtypeStruct((NSEQ, SEQ, W), F32)] * 6,
        scratch_shapes=[pltpu.VMEM((NSEQ,) + STATE_SHAPE, F32)],
        compiler_params=_cparams(("arbitrary",)),
    )(*[_seq_view(a) for a in (r, lw, k2, v, aa, bb)], hs, ms, ts, _seq_view(dy))
    return [a.reshape(T, W) for a in outs]


def _post_math(y, r, k2, v, ga, o, gb, lng, lnb, rk, bd):
    mu = _headsum(y, bd) * (1.0 / HD)
    yc = y - mu
    var = _headsum(yc * yc, bd) * (1.0 / HD)
    yn = yc * lax.rsqrt(var + GN_EPS) * lng + lnb
    bonus = _headsum(r * k2 * rk, bd) * v
    return (yn + bonus) * _silu(ga), o * _silu(gb)


def even_post(y, r, k2, v, ga, o, gb, lng, lnb, rk):
    tm = 256

    def body(y_ref, r_ref, k2_ref, v_ref, ga_ref, o_ref, gb_ref, lng_ref, lnb_ref, rk_ref, z_ref, zt_ref):
        ya, yb = _post_math(y_ref[...], r_ref[...], k2_ref[...], v_ref[...], ga_ref[...], o_ref[...], gb_ref[...],
                            lng_ref[...], lnb_ref[...], rk_ref[...], _head_blockdiag())
        ya, yb = ya.astype(BF16), yb.astype(BF16)
        z_ref[:, 0:W] = ya
        z_ref[:, W:2 * W] = yb
        zt_ref[0:W, :] = ya.T
        zt_ref[W:2 * W, :] = yb.T

    vec = _const_spec((1, W))
    return pl.pallas_call(
        body, grid=(T // tm,), name="even_post",
        in_specs=[_row_spec(tm, W)] * 7 + [vec] * 3,
        out_specs=[_row_spec(tm, D), _col_spec(D, tm)],
        out_shape=[jax.ShapeDtypeStruct((T, D), BF16), jax.ShapeDtypeStruct((D, T), BF16)],
        compiler_params=_cparams(("parallel",)),
    )(y, r, k2, v, ga, o, gb, lng, lnb, rk)


def even_post_bwd(y, r, k2, v, ga, o, gb, lng, lnb, rk, dz):
    tm = 256

    def body(y_ref, r_ref, k2_ref, v_ref, ga_ref, o_ref, gb_ref, lng_ref, lnb_ref, rk_ref, dz_ref,
             dy_ref, dr_ref, dk2_ref, dv_ref, dga_ref, do_ref, dgb_ref, dlng_ref, dlnb_ref, drk_ref):
        bd = _head_blockdiag()
        _, vjp = jax.vjp(lambda *a: _post_math(*a, bd), y_ref[...], r_ref[...], k2_ref[...], v_ref[...], ga_ref[...],
                         o_ref[...], gb_ref[...], lng_ref[...], lnb_ref[...], rk_ref[...])
        dzv = dz_ref[...]
        dy, dr, dk2, dv, dga, do, dgb, dlng, dlnb, drk = vjp((dzv[:, 0:W], dzv[:, W:2 * W]))
        for ref, val in ((dy_ref, dy), (dr_ref, dr), (dk2_ref, dk2), (dv_ref, dv), (dga_ref, dga), (do_ref, do),
                         (dgb_ref, dgb)):
            ref[...] = val.astype(ref.dtype)

        @pl.when(pl.program_id(0) == 0)
        def _():
            for ref in (dlng_ref, dlnb_ref, drk_ref):
                ref[...] = jnp.zeros_like(ref)

        dlng_ref[...] += dlng
        dlnb_ref[...] += dlnb
        drk_ref[...] += drk

    vec = _const_spec((1, W))
    return pl.pallas_call(
        body, grid=(T // tm,), name="even_post_bwd",
        in_specs=[_row_spec(tm, W)] * 7 + [vec] * 3 + [_row_spec(tm, D)],
        out_specs=[_row_spec(tm, W)] * 7 + [vec] * 3,
        out_shape=[jax.ShapeDtypeStruct((T, W), dt) for dt in (F32, F32, F32, F32, BF16, F32, BF16)]
        + [jax.ShapeDtypeStruct((1, W), F32)] * 3,
        compiler_params=_cparams(("arbitrary",)),
    )(y, r, k2, v, ga, o, gb, lng, lnb, rk, dz)


PADSEQ = SEQ + LEFT * L
ATT_SCALE = 1.0 / math.sqrt(HD)


def _att_probs(q2, kw, bias, c):
    valid = _iota2((1, BAND), 1) >= (LEFT - c) * L
    s = [jnp.where(valid, _bdot_nt(a, b) * ATT_SCALE + bias[p], NEG) for p, (a, b) in enumerate(zip(q2, kw))]
    e = [jnp.exp(a - jnp.max(a, axis=-1, keepdims=True)) for a in s]
    return [a / jnp.sum(a, axis=-1, keepdims=True) for a in e]


def attention_fwd(q, kpad, vpad, bias):
    def body(q_ref, k_ref, v_ref, b_ref, o_ref):
        c = pl.program_id(1)
        start = pl.multiple_of(c * L, L)
        kw = _pairs(k_ref[pl.ds(start, BAND), :])
        vw = _pairs(v_ref[pl.ds(start, BAND), :])
        q2 = [_stack_pair(a) for a in _pairs(q_ref[...].astype(BF16))]
        p = _att_probs(q2, kw, b_ref[...], c)
        o_ref[...] = jnp.concatenate([_unstack_pair(_bdot(a, b)) for a, b in zip(p, vw)], axis=-1)

    qblk = pl.BlockSpec((L, W), lambda b, c: (b * NC + c, 0))
    kblk = pl.BlockSpec((PADSEQ, W), lambda b, c: (b, 0))
    return pl.pallas_call(
        body, grid=(NSEQ, NC), name="attention_fwd",
        in_specs=[qblk, kblk, kblk, _const_spec((NPAIR, 2 * L, BAND))],
        out_specs=qblk, out_shape=jax.ShapeDtypeStruct((T, W), F32),
        compiler_params=_cparams(("parallel", "arbitrary")),
    )(q, kpad, vpad, bias)


def attention_bwd(q, kpad, vpad, bias, do):
    def body(q_ref, k_ref, v_ref, b_ref, do_ref, dq_ref, dko_ref, dvo_ref, db_ref, dk_ref, dv_ref):
        b = pl.program_id(0)
        c = pl.program_id(1)

        @pl.when(c == 0)
        def _():
            dk_ref[...] = jnp.zeros_like(dk_ref)
            dv_ref[...] = jnp.zeros_like(dv_ref)

        @pl.when((c == 0) & (b == 0))
        def _():
            db_ref[...] = jnp.zeros_like(db_ref)

        start = pl.multiple_of(c * L, L)
        kw = _pairs(k_ref[pl.ds(start, BAND), :])
        vw = _pairs(v_ref[pl.ds(start, BAND), :])
        q2 = [_stack_pair(a) for a in _pairs(q_ref[...].astype(BF16))]
        do2 = [_stack_pair(a) for a in _pairs(do_ref[...].astype(BF16))]
        p = _att_probs(q2, kw, b_ref[...], c)
        dp = [_bdot_nt(a, b) for a, b in zip(do2, vw)]
        ds = [a * (d - jnp.sum(d * a, axis=-1, keepdims=True)) for a, d in zip(p, dp)]
        dss = [(a * ATT_SCALE).astype(BF16) for a in ds]
        dq_ref[...] = jnp.concatenate([_unstack_pair(_bdot(a, b)) for a, b in zip(dss, kw)], axis=-1).astype(BF16)
        dk_ref[pl.ds(start, BAND), :] += jnp.concatenate([_bdot_tn(a, b) for a, b in zip(dss, q2)], axis=-1)
        dv_ref[pl.ds(start, BAND), :] += jnp.concatenate([_bdot_tn(a, b) for a, b in zip(p, do2)], axis=-1)
        for i in range(NPAIR):
            db_ref[i] += ds[i]

        @pl.when(c == NC - 1)
        def _():
            dko_ref[...] = dk_ref[LEFT * L:, :].astype(BF16)
            dvo_ref[...] = dv_ref[LEFT * L:, :].astype(BF16)

    qblk = pl.BlockSpec((L, W), lambda b, c: (b * NC + c, 0))
    kblk = pl.BlockSpec((PADSEQ, W), lambda b, c: (b, 0))
    sblk = pl.BlockSpec((SEQ, W), lambda b, c: (b, 0))
    bblk = _const_spec((NPAIR, 2 * L, BAND))
    return pl.pallas_call(
        body, grid=(NSEQ, NC), name="attention_bwd",
        in_specs=[qblk, kblk, kblk, bblk, qblk],
        out_specs=[qblk, sblk, sblk, bblk],
        out_shape=[jax.ShapeDtypeStruct((T, W), BF16), jax.ShapeDtypeStruct((T, W), BF16),
                   jax.ShapeDtypeStruct((T, W), BF16), jax.ShapeDtypeStruct((NPAIR, 2 * L, BAND), F32)],
        scratch_shapes=[pltpu.VMEM((PADSEQ, W), F32), pltpu.VMEM((PADSEQ, W), F32)],
        compiler_params=_cparams(("arbitrary", "arbitrary"), VMEM_BIG),
    )(q, kpad, vpad, bias, do)


NTAB = 2 * CLIP + 1
EXT = BAND + L


def _ext_onehot():
    n = _iota2((EXT, NTAB), 0)
    m = _iota2((EXT, NTAB), 1)
    return (jnp.clip(BAND - 1 - n, -CLIP, CLIP) + CLIP == m).astype(F32)


def bias_expand(table):
    def body(t_ref, o_ref):
        ext = _hdot_nt(t_ref[...], _ext_onehot())
        for i in range(L):
            s = L - 1 - i
            o_ref[:, i, :] = (pltpu.roll(ext, EXT - s, 1) if s else ext)[:, :BAND]

    return pl.pallas_call(body, name="bias_expand", out_shape=jax.ShapeDtypeStruct((NH, L, BAND), F32))(table)


def bias_grad(dbias):
    def body(d_ref, o_ref):
        acc = jnp.zeros((NH, EXT), F32)
        zpad = jnp.zeros((NH, EXT - BAND), F32)
        for i in range(L):
            s = L - 1 - i
            row = jnp.concatenate([d_ref[:, i, :], zpad], axis=-1)
            acc = acc + (pltpu.roll(row, s, 1) if s else row)
        o_ref[...] = _hdot(acc, _ext_onehot())

    return pl.pallas_call(body, name="bias_grad", out_shape=jax.ShapeDtypeStruct((NH, NTAB), F32))(dbias)


def _group_cols(g):
    return slice(g * SGC, (g + 1) * SGC)


def _sg_norm(v, lng, lnb):
    gv = _gelu(v)
    gc = gv - jnp.mean(gv, axis=-1, keepdims=True)
    rstd = lax.rsqrt(jnp.mean(gc * gc, axis=-1, keepdims=True) + LN_EPS)
    xhat = gc * rstd
    return xhat, rstd, xhat * lng + lnb


def gmlp_fwd(u, v, gate, lng, lnb, wm_bf, sgb_t):
    def body(u_ref, v_ref, gt_ref, lng_ref, lnb_ref, wm_ref, sb_ref, z_ref, zt_ref):
        _, _, vln = _sg_norm(v_ref[...], lng_ref[...], lnb_ref[...])
        vlb = vln.astype(BF16)
        for g in range(NG):
            cs = _group_cols(g)
            sv = jnp.dot(wm_ref[g], vlb[:, cs], preferred_element_type=F32) + sb_ref[:, g:g + 1]
            zg = (_gelu(u_ref[:, cs]) * sv * _silu(gt_ref[:, cs])).astype(BF16)
            z_ref[:, cs] = zg
            zt_ref[cs, :] = zg.T

    return pl.pallas_call(
        body, grid=(T // SGC,), name="gmlp_fwd",
        in_specs=[_row_spec(SGC, D)] * 3 + [_const_spec((1, D))] * 2 + [_const_spec((NG, SGC, SGC)),
                                                                      _const_spec((SGC, NG))],
        out_specs=[_row_spec(SGC, D), _col_spec(D, SGC)],
        out_shape=[jax.ShapeDtypeStruct((T, D), BF16), jax.ShapeDtypeStruct((D, T), BF16)],
        compiler_params=_cparams(("parallel",)),
    )(u, v, gate, lng, lnb, wm_bf, sgb_t)


def gmlp_bwd(u, v, gate, lng, lnb, wm_bf, sgb_t, dz):
    def body(u_ref, v_ref, gt_ref, lng_ref, lnb_ref, wm_ref, sb_ref, dz_ref,
             du_ref, dv_ref, dgt_ref, dlng_ref, dlnb_ref, dwm_ref, dsb_ref):
        @pl.when(pl.program_id(0) == 0)
        def _():
            for ref in (dlng_ref, dlnb_ref, dwm_ref, dsb_ref):
                ref[...] = jnp.zeros_like(ref)

        vv = v_ref[...]
        xhat, rstd, vln = _sg_norm(vv, lng_ref[...], lnb_ref[...])
        vlb = vln.astype(BF16)
        dvln = []
        dsv_all = []
        for g in range(NG):
            cs = _group_cols(g)
            uu = u_ref[:, cs]
            gg = gt_ref[:, cs]
            dzz = dz_ref[:, cs]
            sv = jnp.dot(wm_ref[g], vlb[:, cs], preferred_element_type=F32) + sb_ref[:, g:g + 1]
            gu = _gelu(uu)
            sg = _silu(gg)
            dsv = dzz * gu * sg
            dgt_ref[:, cs] = (dzz * gu * sv * _dsilu(gg)).astype(BF16)
            du_ref[:, cs] = (dzz * sv * sg * _dgelu(uu)).astype(BF16)
            dsb16 = dsv.astype(BF16)
            dvln.append(lax.dot_general(wm_ref[g], dsb16, (((0,), (0,)), ((), ())), preferred_element_type=F32))
            dwm_ref[g] += lax.dot_general(dsb16, vlb[:, cs], (((1,), (1,)), ((), ())), preferred_element_type=F32)
            dsv_all.append(dsv)
        dvl = jnp.concatenate(dvln, axis=-1)
        dsv_cat = jnp.concatenate(dsv_all, axis=-1)
        sel = (_iota2((D, NG), 0) // SGC == _iota2((D, NG), 1)).astype(F32)
        dsb_ref[...] += _hdot(dsv_cat, sel)
        dlng_ref[...] += jnp.sum(dvl * xhat, axis=0, keepdims=True)
        dlnb_ref[...] += jnp.sum(dvl, axis=0, keepdims=True)
        dxh = dvl * lng_ref[...]
        dgv = rstd * (dxh - jnp.mean(dxh, axis=-1, keepdims=True)
                      - xhat * jnp.mean(dxh * xhat, axis=-1, keepdims=True))
        dv_ref[...] = (dgv * _dgelu(vv)).astype(BF16)

    return pl.pallas_call(
        body, grid=(T // SGC,), name="gmlp_bwd",
        in_specs=[_row_spec(SGC, D)] * 3 + [_const_spec((1, D))] * 2
        + [_const_spec((NG, SGC, SGC)), _const_spec((SGC, NG)), _row_spec(SGC, D)],
        out_specs=[_row_spec(SGC, D)] * 3 + [_const_spec((1, D))] * 2 + [_const_spec((NG, SGC, SGC)),
                                                                       _const_spec((SGC, NG))],
        out_shape=[jax.ShapeDtypeStruct((T, D), BF16)] * 3 + [jax.ShapeDtypeStruct((1, D), F32)] * 2
        + [jax.ShapeDtypeStruct((NG, SGC, SGC), F32), jax.ShapeDtypeStruct((SGC, NG), F32)],
        compiler_params=_cparams(("arbitrary",)),
    )(u, v, gate, lng, lnb, wm_bf, sgb_t, dz)


NCHIP = 4
NDEV = 8
ANY = pl.BlockSpec(memory_space=pl.ANY)


HBM = pl.BlockSpec(memory_space=pltpu.HBM)
SEM = pl.BlockSpec(memory_space=pltpu.SEMAPHORE)
EFFECT = pltpu.SideEffectType.DATAFLOW_SIDE_EFFECTING


def _peers(whole_mesh):
    x, y, c = lax.axis_index("x"), lax.axis_index("y"), lax.axis_index("c")
    if not whole_mesh:
        return [((px, py, c), 2 * px + py) for px, py in ((1 - x, y), (x, 1 - y), (1 - x, 1 - y))], 2 * x + y
    out = []
    for j in range(1, NDEV):
        px, py, pc = x ^ (j >> 2), y ^ ((j >> 1) & 1), c ^ (j & 1)
        out.append(((px, py, pc), 4 * px + 2 * py + pc))
    return out, 4 * x + 2 * y + c


def _send_copies(src, land, send, recv, scatter, whole_mesh, starting):
    peers, me = _peers(whole_mesh)
    copies = []
    for t in range(len(src)):
        for j, (dev, slot) in enumerate(peers):
            k = t * len(peers) + j
            copies.append(pltpu.make_async_remote_copy(
                src_ref=src[t].at[slot] if scatter else src[t], dst_ref=land[t].at[me if starting else slot],
                send_sem=send.at[k], recv_sem=recv.at[k], device_id=dev, device_id_type=MESH))
    return copies


def send_start(srcs, lands, scatter, whole_mesh, name):
    n = len(srcs)
    nsem = n * (NDEV - 1 if whole_mesh else NCHIP - 1)

    def body(*refs):
        for cp in _send_copies(refs[:n], refs[n:2 * n], refs[2 * n], refs[2 * n + 1], scatter, whole_mesh, True):
            cp.start()
        refs[-1][...] = jnp.zeros_like(refs[-1])

    arrs = list(srcs) + list(lands)
    out = pl.pallas_call(
        body, name=name,
        out_shape=(pltpu.SemaphoreType.DMA((nsem,)), pltpu.SemaphoreType.DMA((nsem,)),
                   *[pltpu.HBM(a.shape, a.dtype) for a in arrs], jax.ShapeDtypeStruct((8, 128), F32)),
        in_specs=[HBM] * (2 * n), out_specs=(SEM, SEM, *[HBM] * (2 * n), pl.BlockSpec(memory_space=pltpu.VMEM)),
        input_output_aliases={i: 2 + i for i in range(2 * n)},
        compiler_params=pltpu.CompilerParams(has_side_effects=EFFECT),
    )(*[pltpu.with_memory_space_constraint(a, pltpu.HBM) for a in arrs])
    return out[0], out[1], list(out[2:2 + n]), list(out[2 + n:2 + 2 * n]), out[-1]


def send_wait(started, after, scatter, whole_mesh, name):
    send, recv, srcs, lands, _ = started
    n = len(srcs)

    def body(*refs):
        for cp in _send_copies(refs[:n], refs[n:2 * n], refs[2 * n], refs[2 * n + 1], scatter, whole_mesh, False):
            cp.wait_send()
            cp.wait_recv()

    arrs = list(srcs) + list(lands)
    out = pl.pallas_call(
        body, name=name, out_shape=tuple(pltpu.HBM(a.shape, a.dtype) for a in arrs),
        in_specs=[HBM] * (2 * n) + [SEM, SEM, ANY], out_specs=tuple([HBM] * (2 * n)),
        input_output_aliases={i: i for i in range(2 * n)},
        compiler_params=pltpu.CompilerParams(has_side_effects=EFFECT),
    )(*arrs, send, recv, after)
    return list(out[n:])


def exchange_c(arrs, name):
    n = len(arrs)

    def body(*refs):
        ins, outs = refs[:n], refs[n:2 * n]
        send, recv = refs[2 * n:]
        sibling = (lax.axis_index("x"), lax.axis_index("y"), 1 - lax.axis_index("c"))
        copies = [pltpu.make_async_remote_copy(src_ref=ins[t], dst_ref=outs[t], send_sem=send.at[t], recv_sem=recv.at[t],
                                               device_id=sibling, device_id_type=MESH) for t in range(n)]
        for cp in copies:
            cp.start()
        for cp in copies:
            cp.wait()

    return pl.pallas_call(
        body, name=name, in_specs=[ANY] * n, out_specs=[ANY] * n,
        out_shape=[jax.ShapeDtypeStruct(a.shape, a.dtype) for a in arrs],
        scratch_shapes=[pltpu.SemaphoreType.DMA((n,)), pltpu.SemaphoreType.DMA((n,))],
    )(*arrs)


def gather_weights(arrs, split):
    n = len(arrs)

    def body(*refs):
        ins, outs = refs[:n], refs[n:2 * n]
        send1, recv1, send2, recv2, loc = refs[2 * n:]
        x, y, c = lax.axis_index("x"), lax.axis_index("y"), lax.axis_index("c")
        me = 2 * x + y
        sibling = (x, y, 1 - c)
        peers = [(1 - x, y), (x, 1 - y), (1 - x, 1 - y)]

        def rows_of(t, core):
            half = arrs[t].shape[0] // 2
            return pl.ds(core * half, half)

        def part(ref, t, core):
            return ref.at[rows_of(t, core)] if split[t] else ref

        local = [pltpu.make_async_copy(ins[t], outs[t].at[me], loc.at[t]) for t in range(n)]
        for cp in local:
            cp.start()
        first = []
        for t in range(n):
            for j, (px, py) in enumerate(peers):
                first.append(pltpu.make_async_remote_copy(
                    src_ref=part(ins[t], t, c), dst_ref=part(outs[t].at[me], t, c), send_sem=send1.at[t, j],
                    recv_sem=recv1.at[t, j], device_id=(px, py, c), device_id_type=MESH))
        for cp in first:
            cp.start()
        passed = []
        for t in range(n):
            for j, (px, py) in enumerate(peers):
                landed = part(outs[t].at[2 * px + py], t, c)
                pltpu.make_async_remote_copy(
                    src_ref=landed, dst_ref=landed, send_sem=send1.at[t, j], recv_sem=recv1.at[t, j],
                    device_id=(x, y, c), device_id_type=MESH).wait_recv()
                if split[t]:
                    cp = pltpu.make_async_remote_copy(
                        src_ref=landed, dst_ref=landed, send_sem=send2.at[t, j], recv_sem=recv2.at[t, j],
                        device_id=sibling, device_id_type=MESH)
                    cp.start()
                    passed.append(cp)
        for t in range(n):
            for j, (px, py) in enumerate(peers):
                if split[t]:
                    other = part(outs[t].at[2 * px + py], t, 1 - c)
                    pltpu.make_async_remote_copy(
                        src_ref=other, dst_ref=other, send_sem=send2.at[t, j], recv_sem=recv2.at[t, j],
                        device_id=(x, y, c), device_id_type=MESH).wait_recv()
        for cp in first + passed:
            cp.wait_send()
        for cp in local:
            cp.wait()

    return pl.pallas_call(
        body, name="gather_weights", in_specs=[ANY] * n, out_specs=[ANY] * n,
        out_shape=[jax.ShapeDtypeStruct((NCHIP,) + a.shape, a.dtype) for a in arrs],
        scratch_shapes=[pltpu.SemaphoreType.DMA((n, 3))] * 4 + [pltpu.SemaphoreType.DMA((n,))],
    )(*arrs)


def _adam_math(g, w, m, v):
    m = ADAM_B1 * m + (1.0 - ADAM_B1) * g
    v = ADAM_B2 * v + (1.0 - ADAM_B2) * (g * g)
    m_hat = m / (1.0 - ADAM_B1 ** ADAM_STEP)
    v_hat = v / (1.0 - ADAM_B2 ** ADAM_STEP)
    delta = -ADAM_LR * (m_hat / (jnp.sqrt(v_hat) + ADAM_EPS) + ADAM_WD * w)
    return delta, m, v


def _rows_tile(rows):
    return rows if rows <= 256 else 256


def sum_chips(own, parts, name):
    _, rows, cols = parts.shape
    tr = _rows_tile(rows)

    def body(own_ref, p_ref, o_ref):
        acc = own_ref[...].astype(F32)
        for s in range(NCHIP):
            acc = acc + p_ref[s].astype(F32)
        o_ref[...] = acc

    return pl.pallas_call(
        body, grid=(rows // tr,), name=name,
        in_specs=[pl.BlockSpec((tr, cols), lambda i: (i, 0)), pl.BlockSpec((NCHIP, tr, cols), lambda i: (0, i, 0))],
        out_specs=pl.BlockSpec((tr, cols), lambda i: (i, 0)),
        out_shape=jax.ShapeDtypeStruct((rows, cols), F32),
        compiler_params=_cparams(("parallel",)),
    )(own, parts)


def adam_shard(p_mine, p_sib, w, m, v, name):
    rows, cols = w.shape[-2:]
    tr = _rows_tile(rows)
    lead = w.ndim == 3

    def body(a_ref, b_ref, w_ref, m_ref, v_ref, g_ref, d_ref, mo_ref, vo_ref):
        g = a_ref[...] + b_ref[...]
        g = g[None] if lead else g
        g_ref[...] = g
        d_ref[...], mo_ref[...], vo_ref[...] = _adam_math(g, w_ref[...], m_ref[...], v_ref[...])

    flat = pl.BlockSpec((tr, cols), lambda i: (i, 0))
    spec = pl.BlockSpec((1, tr, cols), lambda i: (0, i, 0)) if lead else flat
    return pl.pallas_call(
        body, grid=(rows // tr,), name=name, in_specs=[flat] * 2 + [spec] * 3, out_specs=[spec] * 4,
        out_shape=[jax.ShapeDtypeStruct(w.shape, F32)] * 4,
        compiler_params=_cparams(("parallel",)),
    )(p_mine, p_sib, w, m, v)


def adam_replicated(parts, w, m, v, name):
    rows = w.shape[0]

    def body(p_ref, w_ref, m_ref, v_ref, g_ref, d_ref, mo_ref, vo_ref):
        g = p_ref[0]
        for d in range(1, NDEV):
            g = g + p_ref[d]
        g_ref[...] = g
        d_ref[...], mo_ref[...], vo_ref[...] = _adam_math(g, w_ref[...], m_ref[...], v_ref[...])

    return pl.pallas_call(
        body, name=name, out_shape=[jax.ShapeDtypeStruct((rows, 128), F32)] * 4,
    )(parts, w, m, v)


def _pack(arrs):
    pieces = []
    for a in arrs:
        flat = a.reshape(-1)
        pad = (-flat.shape[0]) % 128
        pieces.append(jnp.pad(flat, (0, pad)) if pad else flat)
    flat = jnp.concatenate(pieces)
    pad = (-flat.shape[0]) % 1024
    return jnp.pad(flat, (0, pad)).reshape(-1, 128)


def _unpack(buf, shapes):
    flat = buf.reshape(-1)
    out = []
    o = 0
    for s in shapes:
        n = int(np.prod(s))
        out.append(flat[o:o + n].reshape(s))
        o += n + (-n) % 128
    return out


EVEN_SPLITS = (SHIFT, W, W, W, W, W)
ODD_SPLITS = (D, D, D)


def _cols_to_chips(a):
    rows, cols = a.shape
    return a.reshape(rows, NCHIP, cols // NCHIP).transpose(1, 0, 2)


def _chips_to_cols(a):
    _, rows, n = a.shape
    return a.transpose(1, 0, 2).reshape(rows, NCHIP * n)


def kernel(x, norm_g, w_in_e, shift_mu, rw_w0, rw_w2, rw_a0, rw_a2, rw_kk, rw_ka, rw_rk, rw_lnx_g, rw_lnx_b, att_bias, w_out_e, w_in_o, sg_ln_g, sg_ln_b, sg_w, sg_b, w_out_o, final_g, loss_target, m_norm_g, m_w_in_e, m_shift_mu, m_rw_w0, m_rw_w2, m_rw_a0, m_rw_a2, m_rw_kk, m_rw_ka, m_rw_rk, m_rw_lnx_g, m_rw_lnx_b, m_att_bias, m_w_out_e, m_w_in_o, m_sg_ln_g, m_sg_ln_b, m_sg_w, m_sg_b, m_w_out_o, m_final_g, v_norm_g, v_w_in_e, v_shift_mu, v_rw_w0, v_rw_w2, v_rw_a0, v_rw_a2, v_rw_kk, v_rw_ka, v_rw_rk, v_rw_lnx_g, v_rw_lnx_b, v_att_bias, v_w_out_e, v_w_in_o, v_sg_ln_g, v_sg_ln_b, v_sg_w, v_sg_b, v_w_out_o, v_final_g):
    x2 = x.reshape(T, D)
    tgt = loss_target.reshape(T, D)

    my_chip = 2 * lax.axis_index("x") + lax.axis_index("y")
    gathered = gather_weights(
        [w_in_e[0].astype(BF16), jnp.concatenate([rw_w2[0], rw_a2[0]], axis=0),
         jnp.concatenate([sg_ln_g, sg_ln_b], axis=0)], [True, True, False])
    wie = _chips_to_cols(gathered[0])
    w2 = _chips_to_cols(gathered[1][:, :LORA])
    a2 = _chips_to_cols(gathered[1][:, LORA:])
    sglg = _chips_to_cols(gathered[2][:, 0:1])
    sglb = _chips_to_cols(gathered[2][:, 1:2])

    late = [w_out_e[0].astype(BF16), w_in_o[0].astype(BF16), w_out_o[0].astype(BF16)]
    late_started = send_start(late, [jnp.broadcast_to(a[None], (NCHIP,) + a.shape) for a in late], False, False,
                              "late_weights_start")

    def late_weights(after):
        woe, wio, woo = send_wait(late_started, after, False, False, "late_weights_wait")
        return woe.reshape(D, D), _chips_to_cols(wio), woo.reshape(D, D)

    def scatter_start(grads, name):
        srcs = [g_.astype(BF16) if g_.shape[-1] >= W else g_ for g_ in grads]
        return send_start(srcs, [jnp.zeros_like(s) for s in srcs], True, False, name)

    def own_block(g_):
        return lax.dynamic_index_in_dim(g_, my_chip, axis=0, keepdims=False)

    started = {}

    def on_odd_grads(d_woo, d_wio):
        blocks = [d_woo.reshape(NCHIP, D // NCHIP, D), d_wio]
        started["odd"] = (scatter_start(blocks, "odd_grads_start"), [own_block(b) for b in blocks])
        return started["odd"][0][-1]

    def on_even_grads(big_g):
        d_wie, d_woe, _, _, d_w2, d_a2, d_sglg, d_sglb = big_g
        blocks = [d_wie, d_woe.reshape(NCHIP, D // NCHIP, D), _cols_to_chips(d_w2), _cols_to_chips(d_a2),
                  _cols_to_chips(d_sglg), _cols_to_chips(d_sglb)]
        started["even"] = (scatter_start(blocks, "even_grads_start"), [own_block(b) for b in blocks])
        return started["even"][0][-1]

    def on_small_grads(layer, grads):
        mine = _pack(grads)
        started[layer + "_small"] = send_start([mine], [jnp.broadcast_to(mine[None], (NDEV,) + mine.shape)], False,
                                               True, layer + "_small_grads_start")
        return started[layer + "_small"][-1]

    loss_part, dx, _, _ = _local_step(
        x2, tgt, wie, late_weights, w2, a2, sglg, sglb, norm_g, shift_mu, rw_w0, rw_a0, rw_kk, rw_ka, rw_rk,
        rw_lnx_g, rw_lnx_b, att_bias, sg_w, sg_b, final_g, first_after=late_started[-1], on_odd_grads=on_odd_grads,
        on_even_grads=on_even_grads, on_small_grads=on_small_grads)
    even_started, even_own = started["even"]

    wmv = {"w_in_e": (w_in_e, m_w_in_e, v_w_in_e), "w_out_e": (w_out_e, m_w_out_e, v_w_out_e),
           "w_in_o": (w_in_o, m_w_in_o, v_w_in_o), "w_out_o": (w_out_o, m_w_out_o, v_w_out_o),
           "rw_w2": (rw_w2, m_rw_w2, v_rw_w2), "rw_a2": (rw_a2, m_rw_a2, v_rw_a2),
           "sg_ln_g": (sg_ln_g, m_sg_ln_g, v_sg_ln_g), "sg_ln_b": (sg_ln_b, m_sg_ln_b, v_sg_ln_b)}
    sharded = {}

    def finish(names, own, landed, tag):
        partial = [sum_chips(o_, p_, "sum_" + nm) for o_, p_, nm in zip(own, landed, names)]
        from_sibling = exchange_c(partial, "swap_partials_" + tag)
        for nm, mine, sib in zip(names, partial, from_sibling):
            sharded[nm] = adam_shard(mine, sib, *wmv[nm], "adam_" + nm)
        return partial[0]

    odd_started, odd_own = started["odd"]
    odd_landed = send_wait(odd_started, started["even_small"][-1], True, False, "odd_grads_wait")
    done = finish(["w_out_o", "w_in_o"], odd_own, odd_landed, "odd")

    no_w = jnp.zeros((1, 1), F32)
    groups = {
        "odd": (["sg_w", "sg_b", "final_g", "norm_g1"], [sg_w, sg_b, final_g, norm_g[1:2]],
                [m_sg_w, m_sg_b, m_final_g, m_norm_g[1:2]], [v_sg_w, v_sg_b, v_final_g, v_norm_g[1:2]]),
        "even": (["norm_g0", "shift_mu", "rw_w0", "rw_a0", "rw_kk", "rw_ka", "rw_rk", "rw_lnx_g", "rw_lnx_b",
                  "att_bias", "loss"],
                 [norm_g[0:1], shift_mu, rw_w0, rw_a0, rw_kk, rw_ka, rw_rk, rw_lnx_g, rw_lnx_b, att_bias, no_w],
                 [m_norm_g[0:1], m_shift_mu, m_rw_w0, m_rw_a0, m_rw_kk, m_rw_ka, m_rw_rk, m_rw_lnx_g, m_rw_lnx_b,
                  m_att_bias, no_w],
                 [v_norm_g[0:1], v_shift_mu, v_rw_w0, v_rw_a0, v_rw_kk, v_rw_ka, v_rw_rk, v_rw_lnx_g, v_rw_lnx_b,
                  v_att_bias, no_w]),
    }
    rep = {}
    for layer in ("odd", "even"):
        nms, ws, ms_, vs_ = groups[layer]
        (gathered_g,) = send_wait(started[layer + "_small"], done, False, True, layer + "_small_grads_wait")
        rep_out = adam_replicated(gathered_g, _pack(ws), _pack(ms_), _pack(vs_), "adam_" + layer + "_small")
        done = rep_out[0]
        for nm in nms:
            rep[nm] = []
        for buf in rep_out:
            for nm, a in zip(nms, _unpack(buf, [w_.shape for w_ in ws])):
                rep[nm].append(a)
    rep["norm_g"] = [jnp.concatenate([a, b], axis=0) for a, b in zip(rep["norm_g0"], rep["norm_g1"])]
    even_landed = send_wait(even_started, done, True, False, "even_grads_wait")
    finish(["w_in_e", "w_out_e", "rw_w2", "rw_a2", "sg_ln_g", "sg_ln_b"], even_own, even_landed, "even")

    order = ["norm_g", "w_in_e", "shift_mu", "rw_w0", "rw_w2", "rw_a0", "rw_a2", "rw_kk", "rw_ka", "rw_rk",
             "rw_lnx_g", "rw_lnx_b", "att_bias", "w_out_e", "w_in_o", "sg_ln_g", "sg_ln_b", "sg_w", "sg_b",
             "w_out_o", "final_g"]
    results = {**sharded, **rep}
    outs = [rep["loss"][0].reshape(()), dx.reshape(NSEQ, SEQ, D)]
    for kind in range(4):
        outs += [results[nm][kind] for nm in order]
    return tuple(outs)


def _local_step(x2, tgt, wie, late_weights, w2, a2, sglg, sglb, norm_g, shift_mu, rw_w0, rw_a0, rw_kk, rw_ka, rw_rk,
                rw_lnx_g, rw_lnx_b, att_bias, sg_w, sg_b, final_g, first_after=None, on_odd_grads=None,
                on_even_grads=None, on_small_grads=None):
    zl = jnp.zeros((LORA, W), F32)
    w2x = jnp.concatenate([w2, zl], axis=0)
    a2x = jnp.concatenate([zl, a2], axis=0)
    rk = rw_rk.reshape(1, W)
    pos = np.arange(SGC)
    sg_mask = jnp.asarray(((pos[None, :] // L) <= (pos[:, None] // L)).astype(np.float32))
    wm = (sg_w[0] * sg_mask[None]).astype(BF16)
    sgb_t = sg_b[0].T

    xn0, ps, ga, q, kb, vb, gb = ln_in_proj(x2, norm_g[0:1], wie, EVEN_SPLITS, "in_proj_even", after=first_after)
    r, lw, k2, v, aa, bb = even_prep(ps, shift_mu, rw_w0, w2x, rw_a0, a2x, rw_kk, rw_ka)
    y, hs, ms, ts = rwkv_fwd(r, lw, k2, v, aa, bb)
    bias = bias_expand(att_bias[0]).reshape(NPAIR, 2 * L, BAND)

    def padded(a):
        return jnp.pad(a.astype(BF16).reshape(NSEQ, SEQ, W), ((0, 0), (LEFT * L, 0), (0, 0))).reshape(NSEQ * PADSEQ, W)

    kpad, vpad = padded(kb), padded(vb)
    o = attention_fwd(q, kpad, vpad, bias)
    z, zt = even_post(y, r, k2, v, ga, o, gb, rw_lnx_g, rw_lnx_b, rk)
    woe, wio, woo = late_weights(z)
    h1 = out_proj(x2, z, woe, "out_proj_even")
    xn1, u, vv, gt = ln_in_proj(h1, norm_g[1:2], wio, ODD_SPLITS, "in_proj_odd")
    z2, z2t = gmlp_fwd(u, vv, gt, sglg, sglb, wm, sgb_t)
    dh2, loss_part, d_final_g = out_proj_loss(h1, z2, woo, final_g[None], tgt)

    dz2, d_woo = out_proj_bwd(dh2, z2t, woo, "out_proj_odd_bwd")
    du, dvv, dgt, d_sglg, d_sglb, d_wm, d_sgb_t = gmlp_bwd(u, vv, gt, sglg, sglb, wm, sgb_t, dz2)
    dp_odd = [du, dvv, dgt]
    d_wio = matmul_acc_chips(xn1, dp_odd, "in_proj_odd_dw")
    token = on_odd_grads(d_woo, d_wio) if on_odd_grads else None
    dh1, d_g1 = in_proj_bwd_x(h1, norm_g[1:2], wio, dp_odd, dh2, "in_proj_odd_bwd", after=token)
    odd_small = [d_wm * sg_mask[None], d_sgb_t.T, d_final_g, d_g1]
    token = on_small_grads("odd", odd_small) if on_small_grads else None
    dz, d_woe = out_proj_bwd(dh1, zt, woe, "out_proj_even_bwd", after=token)
    dy, dr2, dk22, dv2, dga, do, dgb, d_lng, d_lnb, d_rk = even_post_bwd(
        y, r, k2, v, ga, o, gb, rw_lnx_g, rw_lnx_b, rk, dz)
    dq, dkb, dvb, dbias = attention_bwd(q, kpad, vpad, bias, do)
    d_att_bias = bias_grad(dbias.reshape(NH, L, BAND))
    dr, dlw, dk2, dv, daa, dbb = rwkv_bwd(r, lw, k2, v, aa, bb, hs, ms, ts, dy)
    dps, d_mu, d_w0, d_w2x, d_a0, d_a2x, d_kk, d_ka = even_prep_bwd(
        ps, shift_mu, rw_w0, w2x, rw_a0, a2x, rw_kk, rw_ka, dr, dlw, dk2, dv, daa, dbb, dr2, dk22, dv2)
    dp_even = [dps, dga, dq, dkb, dvb, dgb]
    d_wie = matmul_acc_chips(xn0, dp_even, "in_proj_even_dw")
    big_g = (d_wie, d_woe, d_wio, d_woo, d_w2x[:LORA], d_a2x[LORA:], d_sglg, d_sglb)
    token = on_even_grads(big_g) if on_even_grads else None
    dx, d_g0 = in_proj_bwd_x(x2, norm_g[0:1], wie, dp_even, dh1, "in_proj_even_bwd", after=token)
    even_small = [d_g0, d_mu, d_w0, d_a0, d_kk, d_ka, d_rk, d_lng, d_lnb, d_att_bias]
    if on_small_grads:
        on_small_grads("even", even_small + [loss_part[0:1, 0:1]])
    rep_g = [jnp.concatenate([d_g0, d_g1], axis=0)] + even_small[1:] + odd_small[:3]
    return loss_part[0, 0], dx, big_g, rep_g
```

```python
import functools
import math

import jax
import jax.numpy as jnp
import numpy as np
from jax import lax
from jax.experimental import pallas as pl
from jax.experimental.pallas import tpu as pltpu

F32 = jnp.float32
BF16 = jnp.bfloat16
HI = lax.Precision.HIGHEST

D = 1024
SEQ = 2048
NSEQ = 2
T = NSEQ * SEQ
HD = 64
NH = 8
W = 512
SHIFT = 1664
LORA = 64
EVEN_IN = 4224
ODD_IN = 3072
L = 64
NC = SEQ // L
LEFT = 8
BAND = (LEFT + 1) * L
CLIP = 128
SGC = 128
NG = 8
RMS_EPS = 1e-6
LN_EPS = 1e-5
GN_EPS = 64e-5
NEG = -1e30
VMEM_BIG = 56 * 1024 * 1024

ADAM_LR = 0.001
ADAM_B1 = 0.9
ADAM_B2 = 0.999
ADAM_EPS = 1e-08
ADAM_WD = 0.01
ADAM_STEP = 10

MESH = pl.DeviceIdType.MESH


def _bdot(a, b):
    return jnp.dot(a.astype(BF16), b.astype(BF16), preferred_element_type=F32)


def _bdot_nt(a, b):
    return lax.dot_general(a.astype(BF16), b.astype(BF16), (((1,), (1,)), ((), ())), preferred_element_type=F32)


def _bdot_tn(a, b):
    return lax.dot_general(a.astype(BF16), b.astype(BF16), (((0,), (0,)), ((), ())), preferred_element_type=F32)


def _hdot(a, b):
    return jnp.dot(a, b, precision=HI, preferred_element_type=F32)


def _hdot_nt(a, b):
    return lax.dot_general(a, b, (((1,), (1,)), ((), ())), precision=HI, preferred_element_type=F32)


def _hdot_tn(a, b):
    return lax.dot_general(a, b, (((0,), (0,)), ((), ())), precision=HI, preferred_element_type=F32)


def _iota2(shape, dim):
    return lax.broadcasted_iota(jnp.int32, shape, dim)


def _head_blockdiag():
    r = _iota2((W, W), 0) // HD
    c = _iota2((W, W), 1) // HD
    return (r == c).astype(BF16)


def _headsum_impl(x, bd):
    hi = x.astype(BF16)
    mid = (x - hi.astype(F32)).astype(BF16)
    return jnp.dot(hi, bd, preferred_element_type=F32) + jnp.dot(mid, bd, preferred_element_type=F32)


@jax.custom_vjp
def _headsum(x, bd):
    return _headsum_impl(x, bd)


def _headsum_fwd(x, bd):
    return _headsum_impl(x, bd), bd


def _headsum_bwd(bd, ct):
    return _headsum_impl(ct, bd), None


_headsum.defvjp(_headsum_fwd, _headsum_bwd)


def _silu(x):
    return x * jax.nn.sigmoid(x)


def _dsilu(x):
    s = jax.nn.sigmoid(x)
    return s * (1.0 + x * (1.0 - s))


_GELU_C = math.sqrt(2.0 / math.pi)


def _gelu(x):
    return 0.5 * x * (1.0 + jnp.tanh(_GELU_C * (x + 0.044715 * (x * x * x))))


def _dgelu(x):
    t = jnp.tanh(_GELU_C * (x + 0.044715 * (x * x * x)))
    return 0.5 * (1.0 + t) + 0.5 * x * (1.0 - t * t) * _GELU_C * (1.0 + 3.0 * 0.044715 * x * x)


def _softplus(x):
    return jnp.maximum(x, 0.0) + jnp.log(1.0 + jnp.exp(-jnp.abs(x)))


def _cparams(sem, vmem=None):
    return pltpu.CompilerParams(dimension_semantics=sem, vmem_limit_bytes=vmem)


def _row_spec(tm, width):
    return pl.BlockSpec((tm, width), lambda i: (i, 0))


def _col_spec(height, tm):
    return pl.BlockSpec((height, tm), lambda i: (0, i))


def _const_spec(shape):
    nd = len(shape)
    return pl.BlockSpec(shape, lambda *_: (0,) * nd)


def _weight_dims(w_bf, w_t):
    return (((1,), (1,)), ((), ())) if w_t else (((1,), (0,)), ((), ())), w_bf.shape[0 if w_t else 1]


def ln_in_proj(x, g, w_bf, splits, name, after=None, w_t=False):
    dims, n = _weight_dims(w_bf, w_t)
    tm = 256
    spans = []
    o = 0
    for s in splits:
        spans.append((o, o + s))
        o += s
    assert o == n
    extra_specs, extra = _after_operand(after)

    def body(x_ref, g_ref, w_ref, *rest):
        xn_ref, outs = rest[len(extra)], rest[len(extra) + 1:]
        xv = x_ref[...]
        rstd = lax.rsqrt(jnp.mean(xv * xv, axis=-1, keepdims=True) + RMS_EPS)
        xn = (xv * rstd * g_ref[...]).astype(BF16)
        xn_ref[...] = xn.T
        p = lax.dot_general(xn, w_ref[...], dims, preferred_element_type=F32)
        for o_ref, (a, b) in zip(outs, spans):
            o_ref[...] = p[:, a:b]

    return pl.pallas_call(
        body, grid=(T // tm,), name=name,
        in_specs=[_row_spec(tm, D), _const_spec((1, D)), _const_spec(w_bf.shape)] + extra_specs,
        out_specs=[_col_spec(D, tm)] + [_row_spec(tm, s) for s in splits],
        out_shape=[jax.ShapeDtypeStruct((D, T), BF16)] + [jax.ShapeDtypeStruct((T, s), F32) for s in splits],
        compiler_params=_cparams(("parallel",), VMEM_BIG),
    )(x, g, w_bf, *extra)


def in_proj_bwd_x(x, g, w_bf, dps, dres, name, after=None, w_t=False):
    tm = 256
    back = (((1,), (0,)), ((), ())) if w_t else (((1,), (1,)), ((), ()))
    widths = [d.shape[1] for d in dps]
    extra_specs, extra = _after_operand(after)

    def body(x_ref, g_ref, w_ref, dres_ref, *rest):
        dp_refs = rest[:len(widths)]
        dx_ref, dg_ref = rest[-2:]
        dp = jnp.concatenate([r[...] for r in dp_refs], axis=-1)
        dxn = lax.dot_general(dp, w_ref[...], back, preferred_element_type=F32)
        xv = x_ref[...]
        rstd = lax.rsqrt(jnp.mean(xv * xv, axis=-1, keepdims=True) + RMS_EPS)
        xhat = xv * rstd
        dgp = jnp.sum(dxn * xhat, axis=0, keepdims=True)

        @pl.when(pl.program_id(0) == 0)
        def _():
            dg_ref[...] = jnp.zeros_like(dg_ref)

        dg_ref[...] += dgp
        dxh = dxn * g_ref[...]
        dx_ref[...] = dres_ref[...] + rstd * (dxh - xhat * jnp.mean(dxh * xhat, axis=-1, keepdims=True))

    return pl.pallas_call(
        body, grid=(T // tm,), name=name,
        in_specs=[_row_spec(tm, D), _const_spec((1, D)), _const_spec(w_bf.shape), _row_spec(tm, D)]
        + [_row_spec(tm, s) for s in widths] + extra_specs,
        out_specs=[_row_spec(tm, D), _const_spec((1, D))],
        out_shape=[jax.ShapeDtypeStruct((T, D), F32), jax.ShapeDtypeStruct((1, D), F32)],
        compiler_params=_cparams(("arbitrary",), VMEM_BIG),
    )(x, g, w_bf, dres, *dps, *extra)


def _after_operand(after):
    return ([ANY], [after]) if after is not None else ([], [])


def matmul_acc_chips(at_bf, pieces, name, after=None):
    k = at_bf.shape[0]
    widths = [p.shape[1] for p in pieces]
    nb = sum(widths) // NCHIP
    tm = 512
    steps = T // tm
    extra_specs, extra = _after_operand(after)

    def body(a_ref, *rest):
        o_ref, acc = rest[-2:]

        @pl.when(pl.program_id(0) == 0)
        def _():
            acc[...] = jnp.zeros_like(acc)

        a = a_ref[...]
        b = jnp.concatenate([r[...] for r in rest[:len(widths)]], axis=-1)
        for s in range(NCHIP):
            acc[s] += jnp.dot(a, b[:, s * nb:(s + 1) * nb], preferred_element_type=F32)

        @pl.when(pl.program_id(0) == steps - 1)
        def _():
            o_ref[...] = acc[...].astype(BF16)

    return pl.pallas_call(
        body, grid=(steps,), name=name,
        in_specs=[_col_spec(k, tm)] + [_row_spec(tm, w_) for w_ in widths] + extra_specs,
        out_specs=_const_spec((NCHIP, k, nb)),
        out_shape=jax.ShapeDtypeStruct((NCHIP, k, nb), BF16),
        scratch_shapes=[pltpu.VMEM((NCHIP, k, nb), F32)],
        compiler_params=_cparams(("arbitrary",), VMEM_BIG),
    )(at_bf, *pieces, *extra)


def out_proj(h, z_bf, w_bf, name):
    tm = 256

    def body(h_ref, z_ref, w_ref, o_ref):
        o_ref[...] = h_ref[...] + jnp.dot(z_ref[...], w_ref[...], preferred_element_type=F32)

    return pl.pallas_call(
        body, grid=(T // tm,), name=name,
        in_specs=[_row_spec(tm, D), _row_spec(tm, D), _const_spec((D, D))],
        out_specs=_row_spec(tm, D), out_shape=jax.ShapeDtypeStruct((T, D), F32),
        compiler_params=_cparams(("parallel",)),
    )(h, z_bf, w_bf)


def out_proj_bwd(dh, zt_bf, w_bf, name, after=None):
    tm = 256
    extra_specs, extra = _after_operand(after)

    def body(dh_ref, zt_ref, w_ref, *rest):
        dz_ref, dw_ref = rest[-2:]
        dhb = dh_ref[...].astype(BF16)
        dz_ref[...] = lax.dot_general(dhb, w_ref[...], (((1,), (1,)), ((), ())), preferred_element_type=F32)

        @pl.when(pl.program_id(0) == 0)
        def _():
            dw_ref[...] = jnp.zeros_like(dw_ref)

        dw_ref[...] += jnp.dot(zt_ref[...], dhb, preferred_element_type=F32)

    return pl.pallas_call(
        body, grid=(T // tm,), name=name,
        in_specs=[_row_spec(tm, D), _col_spec(D, tm), _const_spec((D, D))] + extra_specs,
        out_specs=[_row_spec(tm, D), _const_spec((D, D))],
        out_shape=[jax.ShapeDtypeStruct((T, D), F32), jax.ShapeDtypeStruct((D, D), F32)],
        compiler_params=_cparams(("arbitrary",)),
    )(dh, zt_bf, w_bf, *extra)


def out_proj_loss(h, z_bf, w_bf, g, target):
    tm = 256

    def body(h_ref, z_ref, w_ref, g_ref, t_ref, dh_ref, loss_ref, dg_ref):
        xv = h_ref[...] + jnp.dot(z_ref[...], w_ref[...], preferred_element_type=F32)
        rstd = lax.rsqrt(jnp.mean(xv * xv, axis=-1, keepdims=True) + RMS_EPS)
        xhat = xv * rstd
        err = xhat * g_ref[...] - t_ref[...]
        part = 0.5 * jnp.sum(jnp.mean(err * err, axis=-1, keepdims=True), axis=0, keepdims=True)
        dout = err * (1.0 / D)

        @pl.when(pl.program_id(0) == 0)
        def _():
            loss_ref[...] = jnp.zeros_like(loss_ref)
            dg_ref[...] = jnp.zeros_like(dg_ref)

        loss_ref[...] += jnp.broadcast_to(part, loss_ref.shape)
        dg_ref[...] += jnp.sum(dout * xhat, axis=0, keepdims=True)
        dxh = dout * g_ref[...]
        dh_ref[...] = rstd * (dxh - xhat * jnp.mean(dxh * xhat, axis=-1, keepdims=True))

    return pl.pallas_call(
        body, grid=(T // tm,), name="out_proj_loss",
        in_specs=[_row_spec(tm, D), _row_spec(tm, D), _const_spec((D, D)), _const_spec((1, D)), _row_spec(tm, D)],
        out_specs=[_row_spec(tm, D), _const_spec((8, 128)), _const_spec((1, D))],
        out_shape=[jax.ShapeDtypeStruct((T, D), F32), jax.ShapeDtypeStruct((8, 128), F32),
                   jax.ShapeDtypeStruct((1, D), F32)],
        compiler_params=_cparams(("arbitrary",)),
    )(h, z_bf, w_bf, g, target)


PREP_TM = 256
PREP_NB = SEQ // PREP_TM


def _prep_elem(k, wl, apre, kkw, kaw, bd):
    wraw = -_softplus(-wl) - 0.5
    lw = -jnp.exp(wraw)
    asig = jax.nn.sigmoid(apre)
    kkr = k * kkw
    nrm = jnp.maximum(jnp.sqrt(_headsum(kkr * kkr, bd)), 1e-12)
    kk = kkr / nrm
    k2 = k * (1.0 + (asig - 1.0) * kaw)
    return lw, k2, -kk, kk * asig


def _shifted(ps_ref, prev_ref, mu, blk):
    p = ps_ref[...]
    first = (blk % PREP_NB) == 0
    prev_row = jnp.where(first, 0.0, prev_ref[7:8, :])
    rolled = pltpu.roll(p, 1, 0)
    p_prev = jnp.where(_iota2(p.shape, 0) == 0, prev_row, rolled)
    return p, p_prev, p + (p_prev - p) * mu


def _prev_spec(width, blk_of):
    return pl.BlockSpec((8, width), lambda i: (jnp.maximum(blk_of(i) * (PREP_TM // 8) - 1, 0), 0))


def even_prep(ps, mu, w0, w2x, a0, a2x, kkw, kaw):
    tm = PREP_TM

    def body(ps_ref, prev_ref, mu_ref, w0_ref, w2_ref, a0_ref, a2_ref, kk_ref, ka_ref,
             r_ref, lw_ref, k2_ref, v_ref, aa_ref, bb_ref):
        _, _, s = _shifted(ps_ref, prev_ref, mu_ref[...], pl.program_id(0))
        wa = s[:, 3 * W:]
        wl = w0_ref[...] + _bdot(jnp.tanh(wa), w2_ref[...])
        apre = a0_ref[...] + _bdot(wa, a2_ref[...])
        lw, k2, aa, bb = _prep_elem(s[:, W:2 * W], wl, apre, kk_ref[...], ka_ref[...], _head_blockdiag())
        r_ref[...] = s[:, 0:W]
        v_ref[...] = s[:, 2 * W:3 * W]
        lw_ref[...] = lw
        k2_ref[...] = k2
        aa_ref[...] = aa
        bb_ref[...] = bb

    vec = _const_spec((1, W))
    return pl.pallas_call(
        body, grid=(T // tm,), name="even_prep",
        in_specs=[_row_spec(tm, SHIFT), _prev_spec(SHIFT, lambda i: i), _const_spec((1, SHIFT)), vec,
                  _const_spec((2 * LORA, W)), vec, _const_spec((2 * LORA, W)), vec, vec],
        out_specs=[_row_spec(tm, W)] * 6,
        out_shape=[jax.ShapeDtypeStruct((T, W), F32)] * 6,
        compiler_params=_cparams(("parallel",)),
    )(ps, ps, mu, w0, w2x, a0, a2x, kkw, kaw)


def even_prep_bwd(ps, mu, w0, w2x, a0, a2x, kkw, kaw, dr, dlw, dk2, dv, daa, dbb, dr2, dk22, dv2):
    tm = PREP_TM
    nb = T // tm
    rev = lambda i: nb - 1 - i

    def body(ps_ref, prev_ref, mu_ref, w0_ref, w2_ref, a0_ref, a2_ref, kk_ref, ka_ref,
             dr_ref, dlw_ref, dk2_ref, dv_ref, daa_ref, dbb_ref, dr2_ref, dk22_ref, dv2_ref,
             dps_ref, dmu_ref, dw0_ref, dw2_ref, da0_ref, da2_ref, dkk_ref, dka_ref, carry):
        i = pl.program_id(0)
        blk = rev(i)
        mu_v = mu_ref[...]
        p, p_prev, s = _shifted(ps_ref, prev_ref, mu_v, blk)
        wa = s[:, 3 * W:]
        th = jnp.tanh(wa)
        wl = w0_ref[...] + _bdot(th, w2_ref[...])
        apre = a0_ref[...] + _bdot(wa, a2_ref[...])
        bd = _head_blockdiag()
        k = s[:, W:2 * W]
        _, vjp = jax.vjp(lambda k_, wl_, ap_, kkw_, kaw_: _prep_elem(k_, wl_, ap_, kkw_, kaw_, bd),
                         k, wl, apre, kk_ref[...], ka_ref[...])
        dk, dwl, dap, dkkw, dkaw = vjp((dlw_ref[...], dk2_ref[...] + dk22_ref[...], daa_ref[...], dbb_ref[...]))
        dwa = _bdot_nt(dwl, w2_ref[...]) * (1.0 - th * th) + _bdot_nt(dap, a2_ref[...])
        ds = jnp.concatenate([dr_ref[...] + dr2_ref[...], dk, dv_ref[...] + dv2_ref[...], dwa], axis=-1)

        @pl.when(i == 0)
        def _():
            for ref in (dmu_ref, dw0_ref, dw2_ref, da0_ref, da2_ref, dkk_ref, dka_ref, carry):
                ref[...] = jnp.zeros_like(ref)

        dmu_ref[...] += jnp.sum(ds * (p_prev - p), axis=0, keepdims=True)
        dw0_ref[...] += jnp.sum(dwl, axis=0, keepdims=True)
        da0_ref[...] += jnp.sum(dap, axis=0, keepdims=True)
        dw2_ref[...] += _bdot_tn(th, dwl)
        da2_ref[...] += _bdot_tn(wa, dap)
        dkk_ref[...] += dkkw
        dka_ref[...] += dkaw
        dsm = ds * mu_v
        last = (blk % PREP_NB) == PREP_NB - 1
        nxt = jnp.where(last, 0.0, carry[0:1, :])
        up = pltpu.roll(dsm, tm - 1, 0)
        up = jnp.where(_iota2(up.shape, 0) == tm - 1, nxt, up)
        dps_ref[...] = (ds - dsm + up).astype(BF16)
        carry[0:1, :] = dsm[0:1, :]

    vec = _const_spec((1, W))
    rrow = lambda width: pl.BlockSpec((tm, width), lambda i: (rev(i), 0))
    return pl.pallas_call(
        body, grid=(nb,), name="even_prep_bwd",
        in_specs=[rrow(SHIFT), _prev_spec(SHIFT, rev), _const_spec((1, SHIFT)), vec,
                  _const_spec((2 * LORA, W)), vec, _const_spec((2 * LORA, W)), vec, vec] + [rrow(W)] * 9,
        out_specs=[rrow(SHIFT), _const_spec((1, SHIFT)), vec, _const_spec((2 * LORA, W)), vec,
                   _const_spec((2 * LORA, W)), vec, vec],
        out_shape=[jax.ShapeDtypeStruct((T, SHIFT), BF16), jax.ShapeDtypeStruct((1, SHIFT), F32),
                   jax.ShapeDtypeStruct((1, W), F32), jax.ShapeDtypeStruct((2 * LORA, W), F32),
                   jax.ShapeDtypeStruct((1, W), F32), jax.ShapeDtypeStruct((2 * LORA, W), F32),
                   jax.ShapeDtypeStruct((1, W), F32), jax.ShapeDtypeStruct((1, W), F32)],
        scratch_shapes=[pltpu.VMEM((8, SHIFT), F32)],
        compiler_params=_cparams(("arbitrary",), VMEM_BIG),
    )(ps, ps, mu, w0, w2x, a0, a2x, kkw, kaw, dr, dlw, dk2, dv, daa, dbb, dr2, dk22, dv2)


NPAIR = NH // 2
PW = 2 * HD


def _pair_cols(p):
    return slice(p * PW, (p + 1) * PW)


def _pairs(a):
    return [a[:, _pair_cols(p)] for p in range(NPAIR)]


def _stack_pair(a):
    first = _iota2(a.shape, 1) < HD
    zero = jnp.zeros_like(a)
    return jnp.concatenate([jnp.where(first, a, zero), jnp.where(first, zero, a)], axis=0)


def _unstack_pair(a):
    n = a.shape[0] // 2
    return jnp.where(_iota2((n, PW), 1) < HD, a[:n], a[n:])


def _fold_pair(a):
    n = a.shape[0] // 2
    return a[:n] + a[n:]


def _chunk_masks():
    n = 4 * L
    row = _iota2((n, n), 0)
    col = _iota2((n, n), 1)
    same = ((row // L) & 1) == ((col // L) & 1)
    ri = row & (L - 1)
    ci = col & (L - 1)
    keep = same & (((row < 2 * L) & (ri > ci)) | ((row >= 2 * L) & (ri >= ci)))
    r1 = _iota2((L, L), 0)
    c1 = _iota2((L, L), 1)
    r2 = _iota2((2 * L, 2 * L), 0)
    c2 = _iota2((2 * L, 2 * L), 1)
    return keep.astype(F32), (r1 >= c1).astype(F32), (r2 == c2).astype(F32)


def _scaled(r, lw, k2, aa, bb, tri):
    g = _hdot(tri, lw)
    eg = jnp.exp(g)
    eng = jnp.exp(-g)
    egp = jnp.exp(g - lw)
    return eg, eng, egp, aa * egp, r * eg, bb * eng, k2 * eng


def _head_cols(h):
    return slice(h * HD, (h + 1) * HD)


def _per_head(a):
    return [a[:, _head_cols(h)] for h in range(NH)]


def _pairs_operands(at, rt, bt, kt):
    x = [jnp.concatenate([_stack_pair(a), _stack_pair(r)], axis=0).astype(BF16) for a, r in zip(_pairs(at), _pairs(rt))]
    yk = [jnp.concatenate([_stack_pair(b), _stack_pair(k)], axis=0).astype(BF16) for b, k in zip(_pairs(bt), _pairs(kt))]
    return x, yk


def _pairs_matrices(x, yk, keep, eye):
    m = [_bdot_nt(a, b) * keep for a, b in zip(x, yk)]
    p = [a[:2 * L, :2 * L] for a in m]
    tinv = [eye + a for a in p]
    for _ in range(5):
        p = [_bdot(a, a) for a in p]
        tinv = [t + _bdot(t, a) for t, a in zip(tinv, p)]
    return [a.astype(BF16) for a in m], [a.astype(BF16) for a in tinv]


def _pairs_fwd(x, yk, m, tinv, vw, s0, egl):
    xh = [_bdot_nt(a, s) for a, s in zip(x, s0)]
    u = [_bdot(t, h[:2 * L] + _bdot(a[:2 * L, 2 * L:], w)) for t, h, a, w in zip(tinv, xh, m, vw)]
    uv = [jnp.concatenate([a, w], axis=0).astype(BF16) for a, w in zip(u, vw)]
    y = [h[2 * L:] + _bdot(a[2 * L:], w) for h, a, w in zip(xh, m, uv)]
    sn = [e * (s + _bdot_tn(w, b)) for e, s, w, b in zip(egl, s0, uv, yk)]
    return y, sn, uv


def _pairs_bwd(x, yk, m, tinv, vw, s0, egl, dyw, dsn, keep):
    _, sn, uv = _pairs_fwd(x, yk, m, tinv, vw, s0, egl)
    dzs = [d * e for d, e in zip(dsn, egl)]
    dgl = [jnp.sum(d * s, axis=0, keepdims=True) for d, s in zip(dsn, sn)]
    dyb = [a.astype(BF16) for a in dyw]
    t1 = [_bdot_tn(a[2 * L:], d) for a, d in zip(m, dyb)]
    t2 = [_bdot_nt(b, d) for b, d in zip(yk, dzs)]
    drhs = [_bdot_tn(t, a[:2 * L] + b[:2 * L]) for t, a, b in zip(tinv, t1, t2)]
    dv = [a[2 * L:] + b[2 * L:] + _bdot_tn(c[:2 * L, 2 * L:], d) for a, b, c, d in zip(t1, t2, m, drhs)]
    gg = [jnp.concatenate([a, b], axis=0).astype(BF16) for a, b in zip(drhs, dyw)]
    ds0 = [d + _bdot_tn(g, a) for d, g, a in zip(dzs, gg, x)]
    dm = [_bdot_nt(g, w) * keep for g, w in zip(gg, uv)]
    dx = [_bdot(g, s) + _bdot(d, b) for g, s, d, b in zip(gg, s0, dm, yk)]
    dyk = [_bdot_tn(d, a) + _bdot(w, z) for d, a, w, z in zip(dm, x, uv, dzs)]
    return dx, dyk, dv, dgl, ds0


STATE_SHAPE = (NPAIR * PW, PW)
M_SHAPE = (4 * L, NPAIR * 4 * L)
TINV_SHAPE = (2 * L, NPAIR * 2 * L)


def _rows_of(a, n):
    return [a[i * n:(i + 1) * n, :] for i in range(NPAIR)]


def _both(f):
    out = []
    for s in range(NSEQ):
        out += f(s)
    return out


def _seq_view(a):
    return a.reshape(NSEQ, SEQ, a.shape[-1])


def rwkv_fwd(r, lw, k2, v, aa, bb):
    def body(r_ref, lw_ref, k2_ref, v_ref, aa_ref, bb_ref, y_ref, hs_ref, m_ref, t_ref, state):
        @pl.when(pl.program_id(0) == 0)
        def _():
            state[...] = jnp.zeros_like(state)

        s_all = state[...]
        hs_ref[0] = s_all
        keep, tri, eye = _chunk_masks()
        sc = [_scaled(r_ref[s], lw_ref[s], k2_ref[s], aa_ref[s], bb_ref[s], tri) for s in range(NSEQ)]
        ops = [_pairs_operands(*sc[s][3:]) for s in range(NSEQ)]
        x, yk = _both(lambda s: ops[s][0]), _both(lambda s: ops[s][1])
        m, tinv = _pairs_matrices(x, yk, keep, eye)
        vw = _both(lambda s: [_stack_pair(a) for a in _pairs(v_ref[s])])
        s0 = _both(lambda s: _rows_of(s_all[s], PW))
        y, sn, _ = _pairs_fwd(x, yk, m, tinv, vw, s0, _both(lambda s: _pairs(sc[s][0][L - 1:L, :])))
        for s in range(NSEQ):
            mine = slice(s * NPAIR, (s + 1) * NPAIR)
            y_ref[s] = jnp.concatenate([_fold_pair(a) for a in y[mine]], axis=-1)
            m_ref[0, s] = jnp.concatenate(m[mine], axis=-1)
            t_ref[0, s] = jnp.concatenate(tinv[mine], axis=-1)
            state[s] = jnp.concatenate(sn[mine], axis=0)

    blk = pl.BlockSpec((NSEQ, L, W), lambda c: (0, c, 0))
    per_chunk = lambda shape: pl.BlockSpec((1, NSEQ) + shape, lambda c: (c, 0, 0, 0))
    y, hs, ms, ts = pl.pallas_call(
        body, grid=(NC,), name="rwkv_fwd",
        in_specs=[blk] * 6,
        out_specs=[blk, per_chunk(STATE_SHAPE), per_chunk(M_SHAPE), per_chunk(TINV_SHAPE)],
        out_shape=[jax.ShapeDtypeStruct((NSEQ, SEQ, W), F32), jax.ShapeDtypeStruct((NC, NSEQ) + STATE_SHAPE, F32),
                   jax.ShapeDtypeStruct((NC, NSEQ) + M_SHAPE, BF16),
                   jax.ShapeDtypeStruct((NC, NSEQ) + TINV_SHAPE, BF16)],
        scratch_shapes=[pltpu.VMEM((NSEQ,) + STATE_SHAPE, F32)],
        compiler_params=_cparams(("arbitrary",)),
    )(*[_seq_view(a) for a in (r, lw, k2, v, aa, bb)])
    return y.reshape(T, W), hs, ms, ts


def rwkv_bwd(r, lw, k2, v, aa, bb, hs, ms, ts, dy):
    def body(r_ref, lw_ref, k2_ref, v_ref, aa_ref, bb_ref, hs_ref, m_ref, t_ref, dy_ref,
             dr_ref, dlw_ref, dk2_ref, dv_ref, daa_ref, dbb_ref, dstate):
        @pl.when(pl.program_id(0) == 0)
        def _():
            dstate[...] = jnp.zeros_like(dstate)

        keep, tri, _ = _chunk_masks()
        sc = [_scaled(r_ref[s], lw_ref[s], k2_ref[s], aa_ref[s], bb_ref[s], tri) for s in range(NSEQ)]
        ops = [_pairs_operands(*sc[s][3:]) for s in range(NSEQ)]
        x, yk = _both(lambda s: ops[s][0]), _both(lambda s: ops[s][1])
        m = _both(lambda s: [m_ref[0, s][:, i * 4 * L:(i + 1) * 4 * L] for i in range(NPAIR)])
        tinv = _both(lambda s: [t_ref[0, s][:, i * 2 * L:(i + 1) * 2 * L] for i in range(NPAIR)])
        vw = _both(lambda s: [_stack_pair(a) for a in _pairs(v_ref[s])])
        dyw = _both(lambda s: [_stack_pair(a) for a in _pairs(dy_ref[s])])
        s0 = _both(lambda s: _rows_of(hs_ref[0, s], PW))
        dsn = _both(lambda s: _rows_of(dstate[s], PW))
        egl = _both(lambda s: _pairs(sc[s][0][L - 1:L, :]))
        dx, dyk, dvw, dgl, ds0 = _pairs_bwd(x, yk, m, tinv, vw, s0, egl, dyw, dsn, keep)
        for s in range(NSEQ):
            mine = slice(s * NPAIR, (s + 1) * NPAIR)
            eg, eng, egp, at, rt, bt, kt = sc[s]
            dstate[s] = jnp.concatenate(ds0[mine], axis=0)
            dv_ref[s] = jnp.concatenate([_fold_pair(a) for a in dvw[mine]], axis=-1)
            dat = jnp.concatenate([_fold_pair(a[:2 * L]) for a in dx[mine]], axis=-1)
            drt = jnp.concatenate([_fold_pair(a[2 * L:]) for a in dx[mine]], axis=-1)
            dbt = jnp.concatenate([_fold_pair(a[:2 * L]) for a in dyk[mine]], axis=-1)
            dkt = jnp.concatenate([_fold_pair(a[2 * L:]) for a in dyk[mine]], axis=-1)
            dg = drt * rt - dbt * bt - dkt * kt
            dg = dg + jnp.where(_iota2(dg.shape, 0) == L - 1, jnp.concatenate(dgl[mine], axis=-1), 0.0)
            dgp = dat * at
            dlw_ref[s] = _hdot_tn(tri, dg + dgp) - dgp
            dr_ref[s] = drt * eg
            daa_ref[s] = dat * egp
            dbb_ref[s] = dbt * eng
            dk2_ref[s] = dkt * eng

    blk = pl.BlockSpec((NSEQ, L, W), lambda c: (0, NC - 1 - c, 0))
    per_chunk = lambda shape: pl.BlockSpec((1, NSEQ) + shape, lambda c: (NC - 1 - c, 0, 0, 0))
    outs = pl.pallas_call(
        body, grid=(NC,), name="rwkv_bwd",
        in_specs=[blk] * 6 + [per_chunk(STATE_SHAPE), per_chunk(M_SHAPE), per_chunk(TINV_SHAPE), blk],
        out_specs=[blk] * 6,
        out_shape=[jax.ShapeDtypeStruct((NSEQ, SEQ, W), F32)] * 6,
        scratch_shapes=[pltpu.VMEM((NSEQ,) + STATE_SHAPE, F32)],
        compiler_params=_cparams(("arbitrary",)),
    )(*[_seq_view(a) for a in (r, lw, k2, v, aa, bb)], hs, ms, ts, _seq_view(dy))
    return [a.reshape(T, W) for a in outs]


def _post_math(y, r, k2, v, ga, o, gb, lng, lnb, rk, bd):
    mu = _headsum(y, bd) * (1.0 / HD)
    yc = y - mu
    var = _headsum(yc * yc, bd) * (1.0 / HD)
    yn = yc * lax.rsqrt(var + GN_EPS) * lng + lnb
    bonus = _headsum(r * k2 * rk, bd) * v
    return (yn + bonus) * _silu(ga), o * _silu(gb)


def even_post(y, r, k2, v, ga, o, gb, lng, lnb, rk):
    tm = 256

    def body(y_ref, r_ref, k2_ref, v_ref, ga_ref, o_ref, gb_ref, lng_ref, lnb_ref, rk_ref, z_ref, zt_ref):
        ya, yb = _post_math(y_ref[...], r_ref[...], k2_ref[...], v_ref[...], ga_ref[...], o_ref[...], gb_ref[...],
                            lng_ref[...], lnb_ref[...], rk_ref[...], _head_blockdiag())
        ya, yb = ya.astype(BF16), yb.astype(BF16)
        z_ref[:, 0:W] = ya
        z_ref[:, W:2 * W] = yb
        zt_ref[0:W, :] = ya.T
        zt_ref[W:2 * W, :] = yb.T

    vec = _const_spec((1, W))
    return pl.pallas_call(
        body, grid=(T // tm,), name="even_post",
        in_specs=[_row_spec(tm, W)] * 7 + [vec] * 3,
        out_specs=[_row_spec(tm, D), _col_spec(D, tm)],
        out_shape=[jax.ShapeDtypeStruct((T, D), BF16), jax.ShapeDtypeStruct((D, T), BF16)],
        compiler_params=_cparams(("parallel",)),
    )(y, r, k2, v, ga, o, gb, lng, lnb, rk)


def even_post_bwd(y, r, k2, v, ga, o, gb, lng, lnb, rk, dz):
    tm = 256

    def body(y_ref, r_ref, k2_ref, v_ref, ga_ref, o_ref, gb_ref, lng_ref, lnb_ref, rk_ref, dz_ref,
             dy_ref, dr_ref, dk2_ref, dv_ref, dga_ref, do_ref, dgb_ref, dlng_ref, dlnb_ref, drk_ref):
        bd = _head_blockdiag()
        _, vjp = jax.vjp(lambda *a: _post_math(*a, bd), y_ref[...], r_ref[...], k2_ref[...], v_ref[...], ga_ref[...],
                         o_ref[...], gb_ref[...], lng_ref[...], lnb_ref[...], rk_ref[...])
        dzv = dz_ref[...]
        dy, dr, dk2, dv, dga, do, dgb, dlng, dlnb, drk = vjp((dzv[:, 0:W], dzv[:, W:2 * W]))
        for ref, val in ((dy_ref, dy), (dr_ref, dr), (dk2_ref, dk2), (dv_ref, dv), (dga_ref, dga), (do_ref, do),
                         (dgb_ref, dgb)):
            ref[...] = val.astype(ref.dtype)

        @pl.when(pl.program_id(0) == 0)
        def _():
            for ref in (dlng_ref, dlnb_ref, drk_ref):
                ref[...] = jnp.zeros_like(ref)

        dlng_ref[...] += dlng
        dlnb_ref[...] += dlnb
        drk_ref[...] += drk

    vec = _const_spec((1, W))
    return pl.pallas_call(
        body, grid=(T // tm,), name="even_post_bwd",
        in_specs=[_row_spec(tm, W)] * 7 + [vec] * 3 + [_row_spec(tm, D)],
        out_specs=[_row_spec(tm, W)] * 7 + [vec] * 3,
        out_shape=[jax.ShapeDtypeStruct((T, W), dt) for dt in (F32, F32, F32, F32, BF16, F32, BF16)]
        + [jax.ShapeDtypeStruct((1, W), F32)] * 3,
        compiler_params=_cparams(("arbitrary",)),
    )(y, r, k2, v, ga, o, gb, lng, lnb, rk, dz)


PADSEQ = SEQ + LEFT * L
ATT_SCALE = 1.0 / math.sqrt(HD)


def _att_probs(q2, kw, bias, c):
    valid = _iota2((1, BAND), 1) >= (LEFT - c) * L
    s = [jnp.where(valid, _bdot_nt(a, b) * ATT_SCALE + bias[p], NEG) for p, (a, b) in enumerate(zip(q2, kw))]
    e = [jnp.exp(a - jnp.max(a, axis=-1, keepdims=True)) for a in s]
    return [a / jnp.sum(a, axis=-1, keepdims=True) for a in e]


def attention_fwd(q, kpad, vpad, bias):
    def body(q_ref, k_ref, v_ref, b_ref, o_ref):
        c = pl.program_id(1)
        start = pl.multiple_of(c * L, L)
        kw = _pairs(k_ref[pl.ds(start, BAND), :])
        vw = _pairs(v_ref[pl.ds(start, BAND), :])
        q2 = [_stack_pair(a) for a in _pairs(q_ref[...].astype(BF16))]
        p = _att_probs(q2, kw, b_ref[...], c)
        o_ref[...] = jnp.concatenate([_unstack_pair(_bdot(a, b)) for a, b in zip(p, vw)], axis=-1)

    qblk = pl.BlockSpec((L, W), lambda b, c: (b * NC + c, 0))
    kblk = pl.BlockSpec((PADSEQ, W), lambda b, c: (b, 0))
    return pl.pallas_call(
        body, grid=(NSEQ, NC), name="attention_fwd",
        in_specs=[qblk, kblk, kblk, _const_spec((NPAIR, 2 * L, BAND))],
        out_specs=qblk, out_shape=jax.ShapeDtypeStruct((T, W), F32),
        compiler_params=_cparams(("parallel", "arbitrary")),
    )(q, kpad, vpad, bias)


def attention_bwd(q, kpad, vpad, bias, do):
    def body(q_ref, k_ref, v_ref, b_ref, do_ref, dq_ref, dko_ref, dvo_ref, db_ref, dk_ref, dv_ref):
        b = pl.program_id(0)
        c = pl.program_id(1)

        @pl.when(c == 0)
        def _():
            dk_ref[...] = jnp.zeros_like(dk_ref)
            dv_ref[...] = jnp.zeros_like(dv_ref)

        @pl.when((c == 0) & (b == 0))
        def _():
            db_ref[...] = jnp.zeros_like(db_ref)

        start = pl.multiple_of(c * L, L)
        kw = _pairs(k_ref[pl.ds(start, BAND), :])
        vw = _pairs(v_ref[pl.ds(start, BAND), :])
        q2 = [_stack_pair(a) for a in _pairs(q_ref[...].astype(BF16))]
        do2 = [_stack_pair(a) for a in _pairs(do_ref[...].astype(BF16))]
        p = _att_probs(q2, kw, b_ref[...], c)
        dp = [_bdot_nt(a, b) for a, b in zip(do2, vw)]
        ds = [a * (d - jnp.sum(d * a, axis=-1, keepdims=True)) for a, d in zip(p, dp)]
        dss = [(a * ATT_SCALE).astype(BF16) for a in ds]
        dq_ref[...] = jnp.concatenate([_unstack_pair(_bdot(a, b)) for a, b in zip(dss, kw)], axis=-1).astype(BF16)
        dk_ref[pl.ds(start, BAND), :] += jnp.concatenate([_bdot_tn(a, b) for a, b in zip(dss, q2)], axis=-1)
        dv_ref[pl.ds(start, BAND), :] += jnp.concatenate([_bdot_tn(a, b) for a, b in zip(p, do2)], axis=-1)
        for i in range(NPAIR):
            db_ref[i] += ds[i]

        @pl.when(c == NC - 1)
        def _():
            dko_ref[...] = dk_ref[LEFT * L:, :].astype(BF16)
            dvo_ref[...] = dv_ref[LEFT * L:, :].astype(BF16)

    qblk = pl.BlockSpec((L, W), lambda b, c: (b * NC + c, 0))
    kblk = pl.BlockSpec((PADSEQ, W), lambda b, c: (b, 0))
    sblk = pl.BlockSpec((SEQ, W), lambda b, c: (b, 0))
    bblk = _const_spec((NPAIR, 2 * L, BAND))
    return pl.pallas_call(
        body, grid=(NSEQ, NC), name="attention_bwd",
        in_specs=[qblk, kblk, kblk, bblk, qblk],
        out_specs=[qblk, sblk, sblk, bblk],
        out_shape=[jax.ShapeDtypeStruct((T, W), BF16), jax.ShapeDtypeStruct((T, W), BF16),
                   jax.ShapeDtypeStruct((T, W), BF16), jax.ShapeDtypeStruct((NPAIR, 2 * L, BAND), F32)],
        scratch_shapes=[pltpu.VMEM((PADSEQ, W), F32), pltpu.VMEM((PADSEQ, W), F32)],
        compiler_params=_cparams(("arbitrary", "arbitrary"), VMEM_BIG),
    )(q, kpad, vpad, bias, do)


NTAB = 2 * CLIP + 1
EXT = BAND + L


def _ext_onehot():
    n = _iota2((EXT, NTAB), 0)
    m = _iota2((EXT, NTAB), 1)
    return (jnp.clip(BAND - 1 - n, -CLIP, CLIP) + CLIP == m).astype(F32)


def bias_expand(table):
    def body(t_ref, o_ref):
        ext = _hdot_nt(t_ref[...], _ext_onehot())
        for i in range(L):
            s = L - 1 - i
            o_ref[:, i, :] = (pltpu.roll(ext, EXT - s, 1) if s else ext)[:, :BAND]

    return pl.pallas_call(body, name="bias_expand", out_shape=jax.ShapeDtypeStruct((NH, L, BAND), F32))(table)


def bias_grad(dbias):
    def body(d_ref, o_ref):
        acc = jnp.zeros((NH, EXT), F32)
        zpad = jnp.zeros((NH, EXT - BAND), F32)
        for i in range(L):
            s = L - 1 - i
            row = jnp.concatenate([d_ref[:, i, :], zpad], axis=-1)
            acc = acc + (pltpu.roll(row, s, 1) if s else row)
        o_ref[...] = _hdot(acc, _ext_onehot())

    return pl.pallas_call(body, name="bias_grad", out_shape=jax.ShapeDtypeStruct((NH, NTAB), F32))(dbias)


def _group_cols(g):
    return slice(g * SGC, (g + 1) * SGC)


def _sg_norm(v, lng, lnb):
    gv = _gelu(v)
    gc = gv - jnp.mean(gv, axis=-1, keepdims=True)
    rstd = lax.rsqrt(jnp.mean(gc * gc, axis=-1, keepdims=True) + LN_EPS)
    xhat = gc * rstd
    return xhat, rstd, xhat * lng + lnb


def gmlp_fwd(u, v, gate, lng, lnb, wm_bf, sgb_t):
    def body(u_ref, v_ref, gt_ref, lng_ref, lnb_ref, wm_ref, sb_ref, z_ref, zt_ref):
        _, _, vln = _sg_norm(v_ref[...], lng_ref[...], lnb_ref[...])
        vlb = vln.astype(BF16)
        for g in range(NG):
            cs = _group_cols(g)
            sv = jnp.dot(wm_ref[g], vlb[:, cs], preferred_element_type=F32) + sb_ref[:, g:g + 1]
            zg = (_gelu(u_ref[:, cs]) * sv * _silu(gt_ref[:, cs])).astype(BF16)
            z_ref[:, cs] = zg
            zt_ref[cs, :] = zg.T

    return pl.pallas_call(
        body, grid=(T // SGC,), name="gmlp_fwd",
        in_specs=[_row_spec(SGC, D)] * 3 + [_const_spec((1, D))] * 2 + [_const_spec((NG, SGC, SGC)),
                                                                      _const_spec((SGC, NG))],
        out_specs=[_row_spec(SGC, D), _col_spec(D, SGC)],
        out_shape=[jax.ShapeDtypeStruct((T, D), BF16), jax.ShapeDtypeStruct((D, T), BF16)],
        compiler_params=_cparams(("parallel",)),
    )(u, v, gate, lng, lnb, wm_bf, sgb_t)


def gmlp_bwd(u, v, gate, lng, lnb, wm_bf, sgb_t, dz):
    def body(u_ref, v_ref, gt_ref, lng_ref, lnb_ref, wm_ref, sb_ref, dz_ref,
             du_ref, dv_ref, dgt_ref, dlng_ref, dlnb_ref, dwm_ref, dsb_ref):
        @pl.when(pl.program_id(0) == 0)
        def _():
            for ref in (dlng_ref, dlnb_ref, dwm_ref, dsb_ref):
                ref[...] = jnp.zeros_like(ref)

        vv = v_ref[...]
        xhat, rstd, vln = _sg_norm(vv, lng_ref[...], lnb_ref[...])
        vlb = vln.astype(BF16)
        dvln = []
        dsv_all = []
        for g in range(NG):
            cs = _group_cols(g)
            uu = u_ref[:, cs]
            gg = gt_ref[:, cs]
            dzz = dz_ref[:, cs]
            sv = jnp.dot(wm_ref[g], vlb[:, cs], preferred_element_type=F32) + sb_ref[:, g:g + 1]
            gu = _gelu(uu)
            sg = _silu(gg)
            dsv = dzz * gu * sg
            dgt_ref[:, cs] = (dzz * gu * sv * _dsilu(gg)).astype(BF16)
            du_ref[:, cs] = (dzz * sv * sg * _dgelu(uu)).astype(BF16)
            dsb16 = dsv.astype(BF16)
            dvln.append(lax.dot_general(wm_ref[g], dsb16, (((0,), (0,)), ((), ())), preferred_element_type=F32))
            dwm_ref[g] += lax.dot_general(dsb16, vlb[:, cs], (((1,), (1,)), ((), ())), preferred_element_type=F32)
            dsv_all.append(dsv)
        dvl = jnp.concatenate(dvln, axis=-1)
        dsv_cat = jnp.concatenate(dsv_all, axis=-1)
        sel = (_iota2((D, NG), 0) // SGC == _iota2((D, NG), 1)).astype(F32)
        dsb_ref[...] += _hdot(dsv_cat, sel)
        dlng_ref[...] += jnp.sum(dvl * xhat, axis=0, keepdims=True)
        dlnb_ref[...] += jnp.sum(dvl, axis=0, keepdims=True)
        dxh = dvl * lng_ref[...]
        dgv = rstd * (dxh - jnp.mean(dxh, axis=-1, keepdims=True)
                      - xhat * jnp.mean(dxh * xhat, axis=-1, keepdims=True))
        dv_ref[...] = (dgv * _dgelu(vv)).astype(BF16)

    return pl.pallas_call(
        body, grid=(T // SGC,), name="gmlp_bwd",
        in_specs=[_row_spec(SGC, D)] * 3 + [_const_spec((1, D))] * 2
        + [_const_spec((NG, SGC, SGC)), _const_spec((SGC, NG)), _row_spec(SGC, D)],
        out_specs=[_row_spec(SGC, D)] * 3 + [_const_spec((1, D))] * 2 + [_const_spec((NG, SGC, SGC)),
                                                                       _const_spec((SGC, NG))],
        out_shape=[jax.ShapeDtypeStruct((T, D), BF16)] * 3 + [jax.ShapeDtypeStruct((1, D), F32)] * 2
        + [jax.ShapeDtypeStruct((NG, SGC, SGC), F32), jax.ShapeDtypeStruct((SGC, NG), F32)],
        compiler_params=_cparams(("arbitrary",)),
    )(u, v, gate, lng, lnb, wm_bf, sgb_t, dz)


NCHIP = 4
NDEV = 8
ANY = pl.BlockSpec(memory_space=pl.ANY)


HBM = pl.BlockSpec(memory_space=pltpu.HBM)
SEM = pl.BlockSpec(memory_space=pltpu.SEMAPHORE)
EFFECT = pltpu.SideEffectType.DATAFLOW_SIDE_EFFECTING


def _peers(whole_mesh):
    x, y, c = lax.axis_index("x"), lax.axis_index("y"), lax.axis_index("c")
    if not whole_mesh:
        return [((px, py, c), 2 * px + py) for px, py in ((1 - x, y), (x, 1 - y), (1 - x, 1 - y))], 2 * x + y
    out = []
    for j in range(1, NDEV):
        px, py, pc = x ^ (j >> 2), y ^ ((j >> 1) & 1), c ^ (j & 1)
        out.append(((px, py, pc), 4 * px + 2 * py + pc))
    return out, 4 * x + 2 * y + c


def _send_copies(src, land, send, recv, scatter, whole_mesh, starting):
    peers, me = _peers(whole_mesh)
    copies = []
    for t in range(len(src)):
        for j, (dev, slot) in enumerate(peers):
            k = t * len(peers) + j
            copies.append(pltpu.make_async_remote_copy(
                src_ref=src[t].at[slot] if scatter else src[t], dst_ref=land[t].at[me if starting else slot],
                send_sem=send.at[k], recv_sem=recv.at[k], device_id=dev, device_id_type=MESH))
    return copies


def send_start(srcs, lands, scatter, whole_mesh, name):
    n = len(srcs)
    nsem = n * (NDEV - 1 if whole_mesh else NCHIP - 1)

    def body(*refs):
        for cp in _send_copies(refs[:n], refs[n:2 * n], refs[2 * n], refs[2 * n + 1], scatter, whole_mesh, True):
            cp.start()
        refs[-1][...] = jnp.zeros_like(refs[-1])

    arrs = list(srcs) + list(lands)
    out = pl.pallas_call(
        body, name=name,
        out_shape=(pltpu.SemaphoreType.DMA((nsem,)), pltpu.SemaphoreType.DMA((nsem,)),
                   *[pltpu.HBM(a.shape, a.dtype) for a in arrs], jax.ShapeDtypeStruct((8, 128), F32)),
        in_specs=[HBM] * (2 * n), out_specs=(SEM, SEM, *[HBM] * (2 * n), pl.BlockSpec(memory_space=pltpu.VMEM)),
        input_output_aliases={i: 2 + i for i in range(2 * n)},
        compiler_params=pltpu.CompilerParams(has_side_effects=EFFECT),
    )(*[pltpu.with_memory_space_constraint(a, pltpu.HBM) for a in arrs])
    return out[0], out[1], list(out[2:2 + n]), list(out[2 + n:2 + 2 * n]), out[-1]


def send_wait(started, after, scatter, whole_mesh, name):
    send, recv, srcs, lands, _ = started
    n = len(srcs)

    def body(*refs):
        for cp in _send_copies(refs[:n], refs[n:2 * n], refs[2 * n], refs[2 * n + 1], scatter, whole_mesh, False):
            cp.wait_send()
            cp.wait_recv()

    arrs = list(srcs) + list(lands)
    out = pl.pallas_call(
        body, name=name, out_shape=tuple(pltpu.HBM(a.shape, a.dtype) for a in arrs),
        in_specs=[HBM] * (2 * n) + [SEM, SEM, ANY], out_specs=tuple([HBM] * (2 * n)),
        input_output_aliases={i: i for i in range(2 * n)},
        compiler_params=pltpu.CompilerParams(has_side_effects=EFFECT),
    )(*arrs, send, recv, after)
    return list(out[n:])


def exchange_c(arrs, name):
    n = len(arrs)

    def body(*refs):
        ins, outs = refs[:n], refs[n:2 * n]
        send, recv = refs[2 * n:]
        sibling = (lax.axis_index("x"), lax.axis_index("y"), 1 - lax.axis_index("c"))
        copies = [pltpu.make_async_remote_copy(src_ref=ins[t], dst_ref=outs[t], send_sem=send.at[t], recv_sem=recv.at[t],
                                               device_id=sibling, device_id_type=MESH) for t in range(n)]
        for cp in copies:
            cp.start()
        for cp in copies:
            cp.wait()

    return pl.pallas_call(
        body, name=name, in_specs=[ANY] * n, out_specs=[ANY] * n,
        out_shape=[jax.ShapeDtypeStruct(a.shape, a.dtype) for a in arrs],
        scratch_shapes=[pltpu.SemaphoreType.DMA((n,)), pltpu.SemaphoreType.DMA((n,))],
    )(*arrs)


def gather_weights(arrs, split):
    n = len(arrs)

    def body(*refs):
        ins, outs = refs[:n], refs[n:2 * n]
        send1, recv1, send2, recv2, loc = refs[2 * n:]
        x, y, c = lax.axis_index("x"), lax.axis_index("y"), lax.axis_index("c")
        me = 2 * x + y
        sibling = (x, y, 1 - c)
        peers = [(1 - x, y), (x, 1 - y), (1 - x, 1 - y)]

        def rows_of(t, core):
            half = arrs[t].shape[0] // 2
            return pl.ds(core * half, half)

        def part(ref, t, core):
            return ref.at[rows_of(t, core)] if split[t] else ref

        local = [pltpu.make_async_copy(ins[t], outs[t].at[me], loc.at[t]) for t in range(n)]
        for cp in local:
            cp.start()
        first = []
        for t in range(n):
            for j, (px, py) in enumerate(peers):
                first.append(pltpu.make_async_remote_copy(
                    src_ref=part(ins[t], t, c), dst_ref=part(outs[t].at[me], t, c), send_sem=send1.at[t, j],
                    recv_sem=recv1.at[t, j], device_id=(px, py, c), device_id_type=MESH))
        for cp in first:
            cp.start()
        passed = []
        for t in range(n):
            for j, (px, py) in enumerate(peers):
                landed = part(outs[t].at[2 * px + py], t, c)
                pltpu.make_async_remote_copy(
                    src_ref=landed, dst_ref=landed, send_sem=send1.at[t, j], recv_sem=recv1.at[t, j],
                    device_id=(x, y, c), device_id_type=MESH).wait_recv()
                if split[t]:
                    cp = pltpu.make_async_remote_copy(
                        src_ref=landed, dst_ref=landed, send_sem=send2.at[t, j], recv_sem=recv2.at[t, j],
                        device_id=sibling, device_id_type=MESH)
                    cp.start()
                    passed.append(cp)
        for t in range(n):
            for j, (px, py) in enumerate(peers):
                if split[t]:
                    other = part(outs[t].at[2 * px + py], t, 1 - c)
                    pltpu.make_async_remote_copy(
                        src_ref=other, dst_ref=other, send_sem=send2.at[t, j], recv_sem=recv2.at[t, j],
                        device_id=(x, y, c), device_id_type=MESH).wait_recv()
        for cp in first + passed:
            cp.wait_send()
        for cp in local:
            cp.wait()

    return pl.pallas_call(
        body, name="gather_weights", in_specs=[ANY] * n, out_specs=[ANY] * n,
        out_shape=[jax.ShapeDtypeStruct((NCHIP,) + a.shape, a.dtype) for a in arrs],
        scratch_shapes=[pltpu.SemaphoreType.DMA((n, 3))] * 4 + [pltpu.SemaphoreType.DMA((n,))],
    )(*arrs)


def _adam_math(g, w, m, v):
    m = ADAM_B1 * m + (1.0 - ADAM_B1) * g
    v = ADAM_B2 * v + (1.0 - ADAM_B2) * (g * g)
    m_hat = m / (1.0 - ADAM_B1 ** ADAM_STEP)
    v_hat = v / (1.0 - ADAM_B2 ** ADAM_STEP)
    delta = -ADAM_LR * (m_hat / (jnp.sqrt(v_hat) + ADAM_EPS) + ADAM_WD * w)
    return delta, m, v


def _rows_tile(rows):
    return rows if rows <= 256 else 256


def sum_chips(own, parts, name):
    _, rows, cols = parts.shape
    tr = _rows_tile(rows)

    def body(own_ref, p_ref, o_ref):
        acc = own_ref[...].astype(F32)
        for s in range(NCHIP):
            acc = acc + p_ref[s].astype(F32)
        o_ref[...] = acc

    return pl.pallas_call(
        body, grid=(rows // tr,), name=name,
        in_specs=[pl.BlockSpec((tr, cols), lambda i: (i, 0)), pl.BlockSpec((NCHIP, tr, cols), lambda i: (0, i, 0))],
        out_specs=pl.BlockSpec((tr, cols), lambda i: (i, 0)),
        out_shape=jax.ShapeDtypeStruct((rows, cols), F32),
        compiler_params=_cparams(("parallel",)),
    )(own, parts)


def adam_shard(p_mine, p_sib, w, m, v, name, transposed=False):
    rows, cols = p_mine.shape
    tr = _rows_tile(rows)
    lead = w.ndim == 3

    def body(a_ref, b_ref, w_ref, m_ref, v_ref, g_ref, d_ref, mo_ref, vo_ref):
        g = a_ref[...] + b_ref[...]
        g = g.T if transposed else g
        g = g[None] if lead else g
        g_ref[...] = g
        d_ref[...], mo_ref[...], vo_ref[...] = _adam_math(g, w_ref[...], m_ref[...], v_ref[...])

    flat = pl.BlockSpec((tr, cols), lambda i: (i, 0))
    if transposed:
        spec = pl.BlockSpec((1, cols, tr), lambda i: (0, 0, i))
    else:
        spec = pl.BlockSpec((1, tr, cols), lambda i: (0, i, 0)) if lead else flat
    return pl.pallas_call(
        body, grid=(rows // tr,), name=name, in_specs=[flat] * 2 + [spec] * 3, out_specs=[spec] * 4,
        out_shape=[jax.ShapeDtypeStruct(w.shape, F32)] * 4,
        compiler_params=_cparams(("parallel",)),
    )(p_mine, p_sib, w, m, v)


def adam_replicated(parts, w, m, v, name):
    rows = w.shape[0]

    def body(p_ref, w_ref, m_ref, v_ref, g_ref, d_ref, mo_ref, vo_ref):
        g = p_ref[0]
        for d in range(1, NDEV):
            g = g + p_ref[d]
        g_ref[...] = g
        d_ref[...], mo_ref[...], vo_ref[...] = _adam_math(g, w_ref[...], m_ref[...], v_ref[...])

    return pl.pallas_call(
        body, name=name, out_shape=[jax.ShapeDtypeStruct((rows, 128), F32)] * 4,
    )(parts, w, m, v)


def _pack(arrs):
    pieces = []
    for a in arrs:
        flat = a.reshape(-1)
        pad = (-flat.shape[0]) % 128
        pieces.append(jnp.pad(flat, (0, pad)) if pad else flat)
    flat = jnp.concatenate(pieces)
    pad = (-flat.shape[0]) % 1024
    return jnp.pad(flat, (0, pad)).reshape(-1, 128)


def _unpack(buf, shapes):
    flat = buf.reshape(-1)
    out = []
    o = 0
    for s in shapes:
        n = int(np.prod(s))
        out.append(flat[o:o + n].reshape(s))
        o += n + (-n) % 128
    return out


EVEN_SPLITS = (SHIFT, W, W, W, W, W)
ODD_SPLITS = (D, D, D)


def _cols_to_chips(a):
    rows, cols = a.shape
    return a.reshape(rows, NCHIP, cols // NCHIP).transpose(1, 0, 2)


def _chips_to_cols(a):
    _, rows, n = a.shape
    return a.transpose(1, 0, 2).reshape(rows, NCHIP * n)


def kernel(x, norm_g, w_in_e, shift_mu, rw_w0, rw_w2, rw_a0, rw_a2, rw_kk, rw_ka, rw_rk, rw_lnx_g, rw_lnx_b, att_bias, w_out_e, w_in_o, sg_ln_g, sg_ln_b, sg_w, sg_b, w_out_o, final_g, loss_target, m_norm_g, m_w_in_e, m_shift_mu, m_rw_w0, m_rw_w2, m_rw_a0, m_rw_a2, m_rw_kk, m_rw_ka, m_rw_rk, m_rw_lnx_g, m_rw_lnx_b, m_att_bias, m_w_out_e, m_w_in_o, m_sg_ln_g, m_sg_ln_b, m_sg_w, m_sg_b, m_w_out_o, m_final_g, v_norm_g, v_w_in_e, v_shift_mu, v_rw_w0, v_rw_w2, v_rw_a0, v_rw_a2, v_rw_kk, v_rw_ka, v_rw_rk, v_rw_lnx_g, v_rw_lnx_b, v_att_bias, v_w_out_e, v_w_in_o, v_sg_ln_g, v_sg_ln_b, v_sg_w, v_sg_b, v_w_out_o, v_final_g):
    x2 = x.reshape(T, D)
    tgt = loss_target.reshape(T, D)

    my_chip = 2 * lax.axis_index("x") + lax.axis_index("y")
    gathered = gather_weights(
        [jnp.swapaxes(w_in_e[0], 0, 1).astype(BF16), jnp.concatenate([rw_w2[0], rw_a2[0]], axis=0),
         jnp.concatenate([sg_ln_g, sg_ln_b], axis=0)], [True, True, False])
    wie = gathered[0].reshape(EVEN_IN, D)
    w2 = _chips_to_cols(gathered[1][:, :LORA])
    a2 = _chips_to_cols(gathered[1][:, LORA:])
    sglg = _chips_to_cols(gathered[2][:, 0:1])
    sglb = _chips_to_cols(gathered[2][:, 1:2])

    late = [w_out_e[0].astype(BF16), w_in_o[0].astype(BF16), w_out_o[0].astype(BF16)]
    late_started = send_start(late, [jnp.broadcast_to(a[None], (NCHIP,) + a.shape) for a in late], False, False,
                              "late_weights_start")

    def late_weights(after):
        woe, wio, woo = send_wait(late_started, after, False, False, "late_weights_wait")
        return woe.reshape(D, D), _chips_to_cols(wio), woo.reshape(D, D)

    def scatter_start(grads, name):
        srcs = [g_.astype(BF16) if g_.shape[-1] >= W else g_ for g_ in grads]
        return send_start(srcs, [jnp.zeros_like(s) for s in srcs], True, False, name)

    def own_block(g_):
        return lax.dynamic_index_in_dim(g_, my_chip, axis=0, keepdims=False)

    started = {}

    def on_odd_grads(d_woo, d_wio):
        blocks = [d_woo.reshape(NCHIP, D // NCHIP, D), d_wio]
        started["odd"] = (scatter_start(blocks, "odd_grads_start"), [own_block(b) for b in blocks])
        return started["odd"][0][-1]

    def on_even_grads(big_g):
        d_wie, d_woe, _, _, d_w2, d_a2, d_sglg, d_sglb = big_g
        blocks = [d_wie, d_woe.reshape(NCHIP, D // NCHIP, D), _cols_to_chips(d_w2), _cols_to_chips(d_a2),
                  _cols_to_chips(d_sglg), _cols_to_chips(d_sglb)]
        started["even"] = (scatter_start(blocks, "even_grads_start"), [own_block(b) for b in blocks])
        return started["even"][0][-1]

    def on_small_grads(layer, grads):
        mine = _pack(grads)
        started[layer + "_small"] = send_start([mine], [jnp.broadcast_to(mine[None], (NDEV,) + mine.shape)], False,
                                               True, layer + "_small_grads_start")
        return started[layer + "_small"][-1]

    loss_part, dx, _, _ = _local_step(
        x2, tgt, wie, late_weights, w2, a2, sglg, sglb, norm_g, shift_mu, rw_w0, rw_a0, rw_kk, rw_ka, rw_rk,
        rw_lnx_g, rw_lnx_b, att_bias, sg_w, sg_b, final_g, first_after=late_started[-1], on_odd_grads=on_odd_grads,
        on_even_grads=on_even_grads, on_small_grads=on_small_grads)
    even_started, even_own = started["even"]

    wmv = {"w_in_e": tuple(jnp.swapaxes(a, 1, 2) for a in (w_in_e, m_w_in_e, v_w_in_e)),
           "w_out_e": (w_out_e, m_w_out_e, v_w_out_e),
           "w_in_o": (w_in_o, m_w_in_o, v_w_in_o), "w_out_o": (w_out_o, m_w_out_o, v_w_out_o),
           "rw_w2": (rw_w2, m_rw_w2, v_rw_w2), "rw_a2": (rw_a2, m_rw_a2, v_rw_a2),
           "sg_ln_g": (sg_ln_g, m_sg_ln_g, v_sg_ln_g), "sg_ln_b": (sg_ln_b, m_sg_ln_b, v_sg_ln_b)}
    sharded = {}

    def finish(names, own, landed, tag):
        partial = [sum_chips(o_, p_, "sum_" + nm) for o_, p_, nm in zip(own, landed, names)]
        from_sibling = exchange_c(partial, "swap_partials_" + tag)
        for nm, mine, sib in zip(names, partial, from_sibling):
            res = adam_shard(mine, sib, *wmv[nm], "adam_" + nm, transposed=nm == "w_in_e")
            sharded[nm] = [jnp.swapaxes(a, 1, 2) for a in res] if nm == "w_in_e" else res
        return partial[0]

    odd_started, odd_own = started["odd"]
    odd_landed = send_wait(odd_started, started["even_small"][-1], True, False, "odd_grads_wait")
    done = finish(["w_out_o", "w_in_o"], odd_own, odd_landed, "odd")

    no_w = jnp.zeros((1, 1), F32)
    groups = {
        "odd": (["sg_w", "sg_b", "final_g", "norm_g1"], [sg_w, sg_b, final_g, norm_g[1:2]],
                [m_sg_w, m_sg_b, m_final_g, m_norm_g[1:2]], [v_sg_w, v_sg_b, v_final_g, v_norm_g[1:2]]),
        "even": (["norm_g0", "shift_mu", "rw_w0", "rw_a0", "rw_kk", "rw_ka", "rw_rk", "rw_lnx_g", "rw_lnx_b",
                  "att_bias", "loss"],
                 [norm_g[0:1], shift_mu, rw_w0, rw_a0, rw_kk, rw_ka, rw_rk, rw_lnx_g, rw_lnx_b, att_bias, no_w],
                 [m_norm_g[0:1], m_shift_mu, m_rw_w0, m_rw_a0, m_rw_kk, m_rw_ka, m_rw_rk, m_rw_lnx_g, m_rw_lnx_b,
                  m_att_bias, no_w],
                 [v_norm_g[0:1], v_shift_mu, v_rw_w0, v_rw_a0, v_rw_kk, v_rw_ka, v_rw_rk, v_rw_lnx_g, v_rw_lnx_b,
                  v_att_bias, no_w]),
    }
    rep = {}
    for layer in ("odd", "even"):
        nms, ws, ms_, vs_ = groups[layer]
        (gathered_g,) = send_wait(started[layer + "_small"], done, False, True, layer + "_small_grads_wait")
        rep_out = adam_replicated(gathered_g, _pack(ws), _pack(ms_), _pack(vs_), "adam_" + layer + "_small")
        done = rep_out[0]
        for nm in nms:
            rep[nm] = []
        for buf in rep_out:
            for nm, a in zip(nms, _unpack(buf, [w_.shape for w_ in ws])):
                rep[nm].append(a)
    rep["norm_g"] = [jnp.concatenate([a, b], axis=0) for a, b in zip(rep["norm_g0"], rep["norm_g1"])]
    even_landed = send_wait(even_started, done, True, False, "even_grads_wait")
    finish(["w_in_e", "w_out_e", "rw_w2", "rw_a2", "sg_ln_g", "sg_ln_b"], even_own, even_landed, "even")

    order = ["norm_g", "w_in_e", "shift_mu", "rw_w0", "rw_w2", "rw_a0", "rw_a2", "rw_kk", "rw_ka", "rw_rk",
             "rw_lnx_g", "rw_lnx_b", "att_bias", "w_out_e", "w_in_o", "sg_ln_g", "sg_ln_b", "sg_w", "sg_b",
             "w_out_o", "final_g"]
    results = {**sharded, **rep}
    outs = [rep["loss"][0].reshape(()), dx.reshape(NSEQ, SEQ, D)]
    for kind in range(4):
        outs += [results[nm][kind] for nm in order]
    return tuple(outs)


def _local_step(x2, tgt, wie_t, late_weights, w2, a2, sglg, sglb, norm_g, shift_mu, rw_w0, rw_a0, rw_kk, rw_ka, rw_rk,
                rw_lnx_g, rw_lnx_b, att_bias, sg_w, sg_b, final_g, first_after=None, on_odd_grads=None,
                on_even_grads=None, on_small_grads=None):
    zl = jnp.zeros((LORA, W), F32)
    w2x = jnp.concatenate([w2, zl], axis=0)
    a2x = jnp.concatenate([zl, a2], axis=0)
    rk = rw_rk.reshape(1, W)
    pos = np.arange(SGC)
    sg_mask = jnp.asarray(((pos[None, :] // L) <= (pos[:, None] // L)).astype(np.float32))
    wm = (sg_w[0] * sg_mask[None]).astype(BF16)
    sgb_t = sg_b[0].T

    xn0, ps, ga, q, kb, vb, gb = ln_in_proj(x2, norm_g[0:1], wie_t, EVEN_SPLITS, "in_proj_even", after=first_after,
                                            w_t=True)
    r, lw, k2, v, aa, bb = even_prep(ps, shift_mu, rw_w0, w2x, rw_a0, a2x, rw_kk, rw_ka)
    y, hs, ms, ts = rwkv_fwd(r, lw, k2, v, aa, bb)
    bias = bias_expand(att_bias[0]).reshape(NPAIR, 2 * L, BAND)

    def padded(a):
        return jnp.pad(a.astype(BF16).reshape(NSEQ, SEQ, W), ((0, 0), (LEFT * L, 0), (0, 0))).reshape(NSEQ * PADSEQ, W)

    kpad, vpad = padded(kb), padded(vb)
    o = attention_fwd(q, kpad, vpad, bias)
    z, zt = even_post(y, r, k2, v, ga, o, gb, rw_lnx_g, rw_lnx_b, rk)
    woe, wio, woo = late_weights(z)
    h1 = out_proj(x2, z, woe, "out_proj_even")
    xn1, u, vv, gt = ln_in_proj(h1, norm_g[1:2], wio, ODD_SPLITS, "in_proj_odd")
    z2, z2t = gmlp_fwd(u, vv, gt, sglg, sglb, wm, sgb_t)
    dh2, loss_part, d_final_g = out_proj_loss(h1, z2, woo, final_g[None], tgt)

    dz2, d_woo = out_proj_bwd(dh2, z2t, woo, "out_proj_odd_bwd")
    du, dvv, dgt, d_sglg, d_sglb, d_wm, d_sgb_t = gmlp_bwd(u, vv, gt, sglg, sglb, wm, sgb_t, dz2)
    dp_odd = [du, dvv, dgt]
    d_wio = matmul_acc_chips(xn1, dp_odd, "in_proj_odd_dw")
    token = on_odd_grads(d_woo, d_wio) if on_odd_grads else None
    dh1, d_g1 = in_proj_bwd_x(h1, norm_g[1:2], wio, dp_odd, dh2, "in_proj_odd_bwd", after=token)
    odd_small = [d_wm * sg_mask[None], d_sgb_t.T, d_final_g, d_g1]
    token = on_small_grads("odd", odd_small) if on_small_grads else None
    dz, d_woe = out_proj_bwd(dh1, zt, woe, "out_proj_even_bwd", after=token)
    dy, dr2, dk22, dv2, dga, do, dgb, d_lng, d_lnb, d_rk = even_post_bwd(
        y, r, k2, v, ga, o, gb, rw_lnx_g, rw_lnx_b, rk, dz)
    dq, dkb, dvb, dbias = attention_bwd(q, kpad, vpad, bias, do)
    d_att_bias = bias_grad(dbias.reshape(NH, L, BAND))
    dr, dlw, dk2, dv, daa, dbb = rwkv_bwd(r, lw, k2, v, aa, bb, hs, ms, ts, dy)
    dps, d_mu, d_w0, d_w2x, d_a0, d_a2x, d_kk, d_ka = even_prep_bwd(
        ps, shift_mu, rw_w0, w2x, rw_a0, a2x, rw_kk, rw_ka, dr, dlw, dk2, dv, daa, dbb, dr2, dk22, dv2)
    dp_even = [dps, dga, dq, dkb, dvb, dgb]
    d_wie = matmul_acc_chips(xn0, dp_even, "in_proj_even_dw")
    big_g = (d_wie, d_woe, d_wio, d_woo, d_w2x[:LORA], d_a2x[LORA:], d_sglg, d_sglb)
    token = on_even_grads(big_g) if on_even_grads else None
    dx, d_g0 = in_proj_bwd_x(x2, norm_g[0:1], wie_t, dp_even, dh1, "in_proj_even_bwd", after=token, w_t=True)
    even_small = [d_g0, d_mu, d_w0, d_a0, d_kk, d_ka, d_rk, d_lng, d_lnb, d_att_bias]
    if on_small_grads:
        on_small_grads("even", even_small + [loss_part[0:1, 0:1]])
    rep_g = [jnp.concatenate([d_g0, d_g1], axis=0)] + even_small[1:] + odd_small[:3]
    return loss_part[0, 0], dx, big_g, rep_g
```

```python
import functools
import math

import jax
import jax.numpy as jnp
import numpy as np
from jax import lax
from jax.experimental import pallas as pl
from jax.experimental.pallas import tpu as pltpu

F32 = jnp.float32
BF16 = jnp.bfloat16
HI = lax.Precision.HIGHEST

D = 1024
SEQ = 2048
NSEQ = 2
T = NSEQ * SEQ
HD = 64
NH = 8
W = 512
SHIFT = 1664
LORA = 64
EVEN_IN = 4224
ODD_IN = 3072
L = 64
NC = SEQ // L
LEFT = 8
BAND = (LEFT + 1) * L
CLIP = 128
SGC = 128
NG = 8
RMS_EPS = 1e-6
LN_EPS = 1e-5
GN_EPS = 64e-5
NEG = -1e30
VMEM_BIG = 56 * 1024 * 1024

ADAM_LR = 0.001
ADAM_B1 = 0.9
ADAM_B2 = 0.999
ADAM_EPS = 1e-08
ADAM_WD = 0.01
ADAM_STEP = 10

MESH = pl.DeviceIdType.MESH


def _bdot(a, b):
    return jnp.dot(a.astype(BF16), b.astype(BF16), preferred_element_type=F32)


def _bdot_nt(a, b):
    return lax.dot_general(a.astype(BF16), b.astype(BF16), (((1,), (1,)), ((), ())), preferred_element_type=F32)


def _bdot_tn(a, b):
    return lax.dot_general(a.astype(BF16), b.astype(BF16), (((0,), (0,)), ((), ())), preferred_element_type=F32)


def _hdot(a, b):
    return jnp.dot(a, b, precision=HI, preferred_element_type=F32)


def _hdot_nt(a, b):
    return lax.dot_general(a, b, (((1,), (1,)), ((), ())), precision=HI, preferred_element_type=F32)


def _hdot_tn(a, b):
    return lax.dot_general(a, b, (((0,), (0,)), ((), ())), precision=HI, preferred_element_type=F32)


def _iota2(shape, dim):
    return lax.broadcasted_iota(jnp.int32, shape, dim)


def _head_blockdiag():
    r = _iota2((W, W), 0) // HD
    c = _iota2((W, W), 1) // HD
    return (r == c).astype(BF16)


def _headsum_impl(x, bd):
    hi = x.astype(BF16)
    mid = (x - hi.astype(F32)).astype(BF16)
    return jnp.dot(hi, bd, preferred_element_type=F32) + jnp.dot(mid, bd, preferred_element_type=F32)


@jax.custom_vjp
def _headsum(x, bd):
    return _headsum_impl(x, bd)


def _headsum_fwd(x, bd):
    return _headsum_impl(x, bd), bd


def _headsum_bwd(bd, ct):
    return _headsum_impl(ct, bd), None


_headsum.defvjp(_headsum_fwd, _headsum_bwd)


def _silu(x):
    return x * jax.nn.sigmoid(x)


def _dsilu(x):
    s = jax.nn.sigmoid(x)
    return s * (1.0 + x * (1.0 - s))


_GELU_C = math.sqrt(2.0 / math.pi)


def _gelu(x):
    return 0.5 * x * (1.0 + jnp.tanh(_GELU_C * (x + 0.044715 * (x * x * x))))


def _dgelu(x):
    t = jnp.tanh(_GELU_C * (x + 0.044715 * (x * x * x)))
    return 0.5 * (1.0 + t) + 0.5 * x * (1.0 - t * t) * _GELU_C * (1.0 + 3.0 * 0.044715 * x * x)


def _softplus(x):
    return jnp.maximum(x, 0.0) + jnp.log(1.0 + jnp.exp(-jnp.abs(x)))


def _cparams(sem, vmem=None):
    return pltpu.CompilerParams(dimension_semantics=sem, vmem_limit_bytes=vmem)


def _row_spec(tm, width):
    return pl.BlockSpec((tm, width), lambda i: (i, 0))


def _col_spec(height, tm):
    return pl.BlockSpec((height, tm), lambda i: (0, i))


def _const_spec(shape):
    nd = len(shape)
    return pl.BlockSpec(shape, lambda *_: (0,) * nd)


def _weight_dims(w_bf, w_t):
    return (((1,), (1,)), ((), ())) if w_t else (((1,), (0,)), ((), ())), w_bf.shape[0 if w_t else 1]


def ln_in_proj(x, g, w_bf, splits, name, after=None, w_t=False):
    dims, n = _weight_dims(w_bf, w_t)
    tm = 256
    spans = []
    o = 0
    for s in splits:
        spans.append((o, o + s))
        o += s
    assert o == n
    extra_specs, extra = _after_operand(after)

    def body(x_ref, g_ref, w_ref, *rest):
        xn_ref, outs = rest[len(extra)], rest[len(extra) + 1:]
        xv = x_ref[...]
        rstd = lax.rsqrt(jnp.mean(xv * xv, axis=-1, keepdims=True) + RMS_EPS)
        xn = (xv * rstd * g_ref[...]).astype(BF16)
        xn_ref[...] = xn.T
        p = lax.dot_general(xn, w_ref[...], dims, preferred_element_type=F32)
        for o_ref, (a, b) in zip(outs, spans):
            o_ref[...] = p[:, a:b]

    return pl.pallas_call(
        body, grid=(T // tm,), name=name,
        in_specs=[_row_spec(tm, D), _const_spec((1, D)), _const_spec(w_bf.shape)] + extra_specs,
        out_specs=[_col_spec(D, tm)] + [_row_spec(tm, s) for s in splits],
        out_shape=[jax.ShapeDtypeStruct((D, T), BF16)] + [jax.ShapeDtypeStruct((T, s), F32) for s in splits],
        compiler_params=_cparams(("parallel",), VMEM_BIG),
    )(x, g, w_bf, *extra)


def in_proj_bwd_x(x, g, w_bf, dps, dres, name, after=None, w_t=False):
    tm = 256
    back = (((1,), (0,)), ((), ())) if w_t else (((1,), (1,)), ((), ()))
    widths = [d.shape[1] for d in dps]
    extra_specs, extra = _after_operand(after)

    def body(x_ref, g_ref, w_ref, dres_ref, *rest):
        dp_refs = rest[:len(widths)]
        dx_ref, dg_ref = rest[-2:]
        dp = jnp.concatenate([r[...] for r in dp_refs], axis=-1)
        dxn = lax.dot_general(dp, w_ref[...], back, preferred_element_type=F32)
        xv = x_ref[...]
        rstd = lax.rsqrt(jnp.mean(xv * xv, axis=-1, keepdims=True) + RMS_EPS)
        xhat = xv * rstd
        dgp = jnp.sum(dxn * xhat, axis=0, keepdims=True)

        @pl.when(pl.program_id(0) == 0)
        def _():
            dg_ref[...] = jnp.zeros_like(dg_ref)

        dg_ref[...] += dgp
        dxh = dxn * g_ref[...]
        dx_ref[...] = dres_ref[...] + rstd * (dxh - xhat * jnp.mean(dxh * xhat, axis=-1, keepdims=True))

    return pl.pallas_call(
        body, grid=(T // tm,), name=name,
        in_specs=[_row_spec(tm, D), _const_spec((1, D)), _const_spec(w_bf.shape), _row_spec(tm, D)]
        + [_row_spec(tm, s) for s in widths] + extra_specs,
        out_specs=[_row_spec(tm, D), _const_spec((1, D))],
        out_shape=[jax.ShapeDtypeStruct((T, D), F32), jax.ShapeDtypeStruct((1, D), F32)],
        compiler_params=_cparams(("arbitrary",), VMEM_BIG),
    )(x, g, w_bf, dres, *dps, *extra)


def _after_operand(after):
    return ([ANY], [after]) if after is not None else ([], [])


def matmul_acc_chips(at_bf, pieces, name, after=None):
    k = at_bf.shape[0]
    widths = [p.shape[1] for p in pieces]
    nb = sum(widths) // NCHIP
    tm = 512
    steps = T // tm
    extra_specs, extra = _after_operand(after)

    def body(a_ref, *rest):
        o_ref, acc = rest[-2:]

        @pl.when(pl.program_id(0) == 0)
        def _():
            acc[...] = jnp.zeros_like(acc)

        a = a_ref[...]
        b = jnp.concatenate([r[...] for r in rest[:len(widths)]], axis=-1)
        for s in range(NCHIP):
            acc[s] += jnp.dot(a, b[:, s * nb:(s + 1) * nb], preferred_element_type=F32)

        @pl.when(pl.program_id(0) == steps - 1)
        def _():
            o_ref[...] = acc[...].astype(BF16)

    return pl.pallas_call(
        body, grid=(steps,), name=name,
        in_specs=[_col_spec(k, tm)] + [_row_spec(tm, w_) for w_ in widths] + extra_specs,
        out_specs=_const_spec((NCHIP, k, nb)),
        out_shape=jax.ShapeDtypeStruct((NCHIP, k, nb), BF16),
        scratch_shapes=[pltpu.VMEM((NCHIP, k, nb), F32)],
        compiler_params=_cparams(("arbitrary",), VMEM_BIG),
    )(at_bf, *pieces, *extra)


def out_proj(h, z_bf, w_bf, name):
    tm = 256

    def body(h_ref, z_ref, w_ref, o_ref):
        o_ref[...] = h_ref[...] + jnp.dot(z_ref[...], w_ref[...], preferred_element_type=F32)

    return pl.pallas_call(
        body, grid=(T // tm,), name=name,
        in_specs=[_row_spec(tm, D), _row_spec(tm, D), _const_spec((D, D))],
        out_specs=_row_spec(tm, D), out_shape=jax.ShapeDtypeStruct((T, D), F32),
        compiler_params=_cparams(("parallel",)),
    )(h, z_bf, w_bf)


def out_proj_bwd(dh, zt_bf, w_bf, name, after=None):
    tm = 256
    extra_specs, extra = _after_operand(after)

    def body(dh_ref, zt_ref, w_ref, *rest):
        dz_ref, dw_ref = rest[-2:]
        dhb = dh_ref[...].astype(BF16)
        dz_ref[...] = lax.dot_general(dhb, w_ref[...], (((1,), (1,)), ((), ())), preferred_element_type=F32)

        @pl.when(pl.program_id(0) == 0)
        def _():
            dw_ref[...] = jnp.zeros_like(dw_ref)

        dw_ref[...] += jnp.dot(zt_ref[...], dhb, preferred_element_type=F32)

    return pl.pallas_call(
        body, grid=(T // tm,), name=name,
        in_specs=[_row_spec(tm, D), _col_spec(D, tm), _const_spec((D, D))] + extra_specs,
        out_specs=[_row_spec(tm, D), _const_spec((D, D))],
        out_shape=[jax.ShapeDtypeStruct((T, D), F32), jax.ShapeDtypeStruct((D, D), F32)],
        compiler_params=_cparams(("arbitrary",)),
    )(dh, zt_bf, w_bf, *extra)


def out_proj_loss(h, z_bf, w_bf, g, target):
    tm = 256

    def body(h_ref, z_ref, w_ref, g_ref, t_ref, dh_ref, loss_ref, dg_ref):
        xv = h_ref[...] + jnp.dot(z_ref[...], w_ref[...], preferred_element_type=F32)
        rstd = lax.rsqrt(jnp.mean(xv * xv, axis=-1, keepdims=True) + RMS_EPS)
        xhat = xv * rstd
        err = xhat * g_ref[...] - t_ref[...]
        part = 0.5 * jnp.sum(jnp.mean(err * err, axis=-1, keepdims=True), axis=0, keepdims=True)
        dout = err * (1.0 / D)

        @pl.when(pl.program_id(0) == 0)
        def _():
            loss_ref[...] = jnp.zeros_like(loss_ref)
            dg_ref[...] = jnp.zeros_like(dg_ref)

        loss_ref[...] += jnp.broadcast_to(part, loss_ref.shape)
        dg_ref[...] += jnp.sum(dout * xhat, axis=0, keepdims=True)
        dxh = dout * g_ref[...]
        dh_ref[...] = rstd * (dxh - xhat * jnp.mean(dxh * xhat, axis=-1, keepdims=True))

    return pl.pallas_call(
        body, grid=(T // tm,), name="out_proj_loss",
        in_specs=[_row_spec(tm, D), _row_spec(tm, D), _const_spec((D, D)), _const_spec((1, D)), _row_spec(tm, D)],
        out_specs=[_row_spec(tm, D), _const_spec((8, 128)), _const_spec((1, D))],
        out_shape=[jax.ShapeDtypeStruct((T, D), F32), jax.ShapeDtypeStruct((8, 128), F32),
                   jax.ShapeDtypeStruct((1, D), F32)],
        compiler_params=_cparams(("arbitrary",)),
    )(h, z_bf, w_bf, g, target)


PREP_TM = 256
PREP_NB = SEQ // PREP_TM


def _prep_elem(k, wl, apre, kkw, kaw, bd):
    wraw = -_softplus(-wl) - 0.5
    lw = -jnp.exp(wraw)
    asig = jax.nn.sigmoid(apre)
    kkr = k * kkw
    nrm = jnp.maximum(jnp.sqrt(_headsum(kkr * kkr, bd)), 1e-12)
    kk = kkr / nrm
    k2 = k * (1.0 + (asig - 1.0) * kaw)
    return lw, k2, -kk, kk * asig


def _shifted(ps_ref, prev_ref, mu, blk):
    p = ps_ref[...]
    first = (blk % PREP_NB) == 0
    prev_row = jnp.where(first, 0.0, prev_ref[7:8, :])
    rolled = pltpu.roll(p, 1, 0)
    p_prev = jnp.where(_iota2(p.shape, 0) == 0, prev_row, rolled)
    return p, p_prev, p + (p_prev - p) * mu


def _prev_spec(width, blk_of):
    return pl.BlockSpec((8, width), lambda i: (jnp.maximum(blk_of(i) * (PREP_TM // 8) - 1, 0), 0))


def even_prep(ps, mu, w0, w2x, a0, a2x, kkw, kaw):
    tm = PREP_TM

    def body(ps_ref, prev_ref, mu_ref, w0_ref, w2_ref, a0_ref, a2_ref, kk_ref, ka_ref,
             r_ref, lw_ref, k2_ref, v_ref, aa_ref, bb_ref):
        _, _, s = _shifted(ps_ref, prev_ref, mu_ref[...], pl.program_id(0))
        wa = s[:, 3 * W:]
        wl = w0_ref[...] + _bdot(jnp.tanh(wa), w2_ref[...])
        apre = a0_ref[...] + _bdot(wa, a2_ref[...])
        lw, k2, aa, bb = _prep_elem(s[:, W:2 * W], wl, apre, kk_ref[...], ka_ref[...], _head_blockdiag())
        r_ref[...] = s[:, 0:W]
        v_ref[...] = s[:, 2 * W:3 * W]
        lw_ref[...] = lw
        k2_ref[...] = k2
        aa_ref[...] = aa
        bb_ref[...] = bb

    vec = _const_spec((1, W))
    return pl.pallas_call(
        body, grid=(T // tm,), name="even_prep",
        in_specs=[_row_spec(tm, SHIFT), _prev_spec(SHIFT, lambda i: i), _const_spec((1, SHIFT)), vec,
                  _const_spec((2 * LORA, W)), vec, _const_spec((2 * LORA, W)), vec, vec],
        out_specs=[_row_spec(tm, W)] * 6,
        out_shape=[jax.ShapeDtypeStruct((T, W), F32)] * 6,
        compiler_params=_cparams(("parallel",)),
    )(ps, ps, mu, w0, w2x, a0, a2x, kkw, kaw)


def even_prep_bwd(ps, mu, w0, w2x, a0, a2x, kkw, kaw, dr, dlw, dk2, dv, daa, dbb, dr2, dk22, dv2):
    tm = PREP_TM
    nb = T // tm
    rev = lambda i: nb - 1 - i

    def body(ps_ref, prev_ref, mu_ref, w0_ref, w2_ref, a0_ref, a2_ref, kk_ref, ka_ref,
             dr_ref, dlw_ref, dk2_ref, dv_ref, daa_ref, dbb_ref, dr2_ref, dk22_ref, dv2_ref,
             dps_ref, dmu_ref, dw0_ref, dw2_ref, da0_ref, da2_ref, dkk_ref, dka_ref, carry):
        i = pl.program_id(0)
        blk = rev(i)
        mu_v = mu_ref[...]
        p, p_prev, s = _shifted(ps_ref, prev_ref, mu_v, blk)
        wa = s[:, 3 * W:]
        th = jnp.tanh(wa)
        wl = w0_ref[...] + _bdot(th, w2_ref[...])
        apre = a0_ref[...] + _bdot(wa, a2_ref[...])
        bd = _head_blockdiag()
        k = s[:, W:2 * W]
        _, vjp = jax.vjp(lambda k_, wl_, ap_, kkw_, kaw_: _prep_elem(k_, wl_, ap_, kkw_, kaw_, bd),
                         k, wl, apre, kk_ref[...], ka_ref[...])
        dk, dwl, dap, dkkw, dkaw = vjp((dlw_ref[...], dk2_ref[...] + dk22_ref[...], daa_ref[...], dbb_ref[...]))
        dwa = _bdot_nt(dwl, w2_ref[...]) * (1.0 - th * th) + _bdot_nt(dap, a2_ref[...])
        ds = jnp.concatenate([dr_ref[...] + dr2_ref[...], dk, dv_ref[...] + dv2_ref[...], dwa], axis=-1)

        @pl.when(i == 0)
        def _():
            for ref in (dmu_ref, dw0_ref, dw2_ref, da0_ref, da2_ref, dkk_ref, dka_ref, carry):
                ref[...] = jnp.zeros_like(ref)

        dmu_ref[...] += jnp.sum(ds * (p_prev - p), axis=0, keepdims=True)
        dw0_ref[...] += jnp.sum(dwl, axis=0, keepdims=True)
        da0_ref[...] += jnp.sum(dap, axis=0, keepdims=True)
        dw2_ref[...] += _bdot_tn(th, dwl)
        da2_ref[...] += _bdot_tn(wa, dap)
        dkk_ref[...] += dkkw
        dka_ref[...] += dkaw
        dsm = ds * mu_v
        last = (blk % PREP_NB) == PREP_NB - 1
        nxt = jnp.where(last, 0.0, carry[0:1, :])
        up = pltpu.roll(dsm, tm - 1, 0)
        up = jnp.where(_iota2(up.shape, 0) == tm - 1, nxt, up)
        dps_ref[...] = (ds - dsm + up).astype(BF16)
        carry[0:1, :] = dsm[0:1, :]

    vec = _const_spec((1, W))
    rrow = lambda width: pl.BlockSpec((tm, width), lambda i: (rev(i), 0))
    return pl.pallas_call(
        body, grid=(nb,), name="even_prep_bwd",
        in_specs=[rrow(SHIFT), _prev_spec(SHIFT, rev), _const_spec((1, SHIFT)), vec,
                  _const_spec((2 * LORA, W)), vec, _const_spec((2 * LORA, W)), vec, vec] + [rrow(W)] * 9,
        out_specs=[rrow(SHIFT), _const_spec((1, SHIFT)), vec, _const_spec((2 * LORA, W)), vec,
                   _const_spec((2 * LORA, W)), vec, vec],
        out_shape=[jax.ShapeDtypeStruct((T, SHIFT), BF16), jax.ShapeDtypeStruct((1, SHIFT), F32),
                   jax.ShapeDtypeStruct((1, W), F32), jax.ShapeDtypeStruct((2 * LORA, W), F32),
                   jax.ShapeDtypeStruct((1, W), F32), jax.ShapeDtypeStruct((2 * LORA, W), F32),
                   jax.ShapeDtypeStruct((1, W), F32), jax.ShapeDtypeStruct((1, W), F32)],
        scratch_shapes=[pltpu.VMEM((8, SHIFT), F32)],
        compiler_params=_cparams(("arbitrary",), VMEM_BIG),
    )(ps, ps, mu, w0, w2x, a0, a2x, kkw, kaw, dr, dlw, dk2, dv, daa, dbb, dr2, dk22, dv2)


NPAIR = NH // 2
PW = 2 * HD


def _pair_cols(p):
    return slice(p * PW, (p + 1) * PW)


def _pairs(a):
    return [a[:, _pair_cols(p)] for p in range(NPAIR)]


def _stack_pair(a):
    first = _iota2(a.shape, 1) < HD
    zero = jnp.zeros_like(a)
    return jnp.concatenate([jnp.where(first, a, zero), jnp.where(first, zero, a)], axis=0)


def _unstack_pair(a):
    n = a.shape[0] // 2
    return jnp.where(_iota2((n, PW), 1) < HD, a[:n], a[n:])


def _fold_pair(a):
    n = a.shape[0] // 2
    return a[:n] + a[n:]


def _chunk_masks():
    n = 4 * L
    row = _iota2((n, n), 0)
    col = _iota2((n, n), 1)
    same = ((row // L) & 1) == ((col // L) & 1)
    ri = row & (L - 1)
    ci = col & (L - 1)
    keep = same & (((row < 2 * L) & (ri > ci)) | ((row >= 2 * L) & (ri >= ci)))
    r1 = _iota2((L, L), 0)
    c1 = _iota2((L, L), 1)
    r2 = _iota2((2 * L, 2 * L), 0)
    c2 = _iota2((2 * L, 2 * L), 1)
    return keep.astype(F32), (r1 >= c1).astype(F32), (r2 == c2).astype(F32)


def _scaled(r, lw, k2, aa, bb, tri):
    g = _hdot(tri, lw)
    eg = jnp.exp(g)
    eng = jnp.exp(-g)
    egp = jnp.exp(g - lw)
    return eg, eng, egp, aa * egp, r * eg, bb * eng, k2 * eng


def _head_cols(h):
    return slice(h * HD, (h + 1) * HD)


def _per_head(a):
    return [a[:, _head_cols(h)] for h in range(NH)]


def _pairs_operands(at, rt, bt, kt):
    x = [jnp.concatenate([_stack_pair(a), _stack_pair(r)], axis=0).astype(BF16) for a, r in zip(_pairs(at), _pairs(rt))]
    yk = [jnp.concatenate([_stack_pair(b), _stack_pair(k)], axis=0).astype(BF16) for b, k in zip(_pairs(bt), _pairs(kt))]
    return x, yk


def _pairs_matrices(x, yk, keep, eye):
    m = [_bdot_nt(a, b) * keep for a, b in zip(x, yk)]
    p = [a[:2 * L, :2 * L] for a in m]
    tinv = [eye + a for a in p]
    for _ in range(5):
        p = [_bdot(a, a) for a in p]
        tinv = [t + _bdot(t, a) for t, a in zip(tinv, p)]
    return [a.astype(BF16) for a in m], [a.astype(BF16) for a in tinv]


def _pairs_fwd(x, yk, m, tinv, vw, s0, egl):
    xh = [_bdot_nt(a, s) for a, s in zip(x, s0)]
    u = [_bdot(t, h[:2 * L] + _bdot(a[:2 * L, 2 * L:], w)) for t, h, a, w in zip(tinv, xh, m, vw)]
    uv = [jnp.concatenate([a, w], axis=0).astype(BF16) for a, w in zip(u, vw)]
    y = [h[2 * L:] + _bdot(a[2 * L:], w) for h, a, w in zip(xh, m, uv)]
    sn = [e * (s + _bdot_tn(w, b)) for e, s, w, b in zip(egl, s0, uv, yk)]
    return y, sn, uv


def _pairs_bwd(x, yk, m, tinv, vw, s0, egl, dyw, dsn, keep):
    _, sn, uv = _pairs_fwd(x, yk, m, tinv, vw, s0, egl)
    dzs = [d * e for d, e in zip(dsn, egl)]
    dgl = [jnp.sum(d * s, axis=0, keepdims=True) for d, s in zip(dsn, sn)]
    dyb = [a.astype(BF16) for a in dyw]
    t1 = [_bdot_tn(a[2 * L:], d) for a, d in zip(m, dyb)]
    t2 = [_bdot_nt(b, d) for b, d in zip(yk, dzs)]
    drhs = [_bdot_tn(t, a[:2 * L] + b[:2 * L]) for t, a, b in zip(tinv, t1, t2)]
    dv = [a[2 * L:] + b[2 * L:] + _bdot_tn(c[:2 * L, 2 * L:], d) for a, b, c, d in zip(t1, t2, m, drhs)]
    gg = [jnp.concatenate([a, b], axis=0).astype(BF16) for a, b in zip(drhs, dyw)]
    ds0 = [d + _bdot_tn(g, a) for d, g, a in zip(dzs, gg, x)]
    dm = [_bdot_nt(g, w) * keep for g, w in zip(gg, uv)]
    dx = [_bdot(g, s) + _bdot(d, b) for g, s, d, b in zip(gg, s0, dm, yk)]
    dyk = [_bdot_tn(d, a) + _bdot(w, z) for d, a, w, z in zip(dm, x, uv, dzs)]
    return dx, dyk, dv, dgl, ds0


STATE_SHAPE = (NPAIR * PW, PW)
M_SHAPE = (4 * L, NPAIR * 4 * L)
TINV_SHAPE = (2 * L, NPAIR * 2 * L)


def _rows_of(a, n):
    return [a[i * n:(i + 1) * n, :] for i in range(NPAIR)]


def _both(f):
    out = []
    for s in range(NSEQ):
        out += f(s)
    return out


def _seq_view(a):
    return a.reshape(NSEQ, SEQ, a.shape[-1])


def rwkv_fwd(r, lw, k2, v, aa, bb):
    def body(r_ref, lw_ref, k2_ref, v_ref, aa_ref, bb_ref, y_ref, hs_ref, m_ref, t_ref, state):
        @pl.when(pl.program_id(0) == 0)
        def _():
            state[...] = jnp.zeros_like(state)

        s_all = state[...]
        hs_ref[0] = s_all
        keep, tri, eye = _chunk_masks()
        sc = [_scaled(r_ref[s], lw_ref[s], k2_ref[s], aa_ref[s], bb_ref[s], tri) for s in range(NSEQ)]
        ops = [_pairs_operands(*sc[s][3:]) for s in range(NSEQ)]
        x, yk = _both(lambda s: ops[s][0]), _both(lambda s: ops[s][1])
        m, tinv = _pairs_matrices(x, yk, keep, eye)
        vw = _both(lambda s: [_stack_pair(a) for a in _pairs(v_ref[s])])
        s0 = _both(lambda s: _rows_of(s_all[s], PW))
        y, sn, _ = _pairs_fwd(x, yk, m, tinv, vw, s0, _both(lambda s: _pairs(sc[s][0][L - 1:L, :])))
        for s in range(NSEQ):
            mine = slice(s * NPAIR, (s + 1) * NPAIR)
            y_ref[s] = jnp.concatenate([_fold_pair(a) for a in y[mine]], axis=-1)
            m_ref[0, s] = jnp.concatenate(m[mine], axis=-1)
            t_ref[0, s] = jnp.concatenate(tinv[mine], axis=-1)
            state[s] = jnp.concatenate(sn[mine], axis=0)

    blk = pl.BlockSpec((NSEQ, L, W), lambda c: (0, c, 0))
    per_chunk = lambda shape: pl.BlockSpec((1, NSEQ) + shape, lambda c: (c, 0, 0, 0))
    y, hs, ms, ts = pl.pallas_call(
        body, grid=(NC,), name="rwkv_fwd",
        in_specs=[blk] * 6,
        out_specs=[blk, per_chunk(STATE_SHAPE), per_chunk(M_SHAPE), per_chunk(TINV_SHAPE)],
        out_shape=[jax.ShapeDtypeStruct((NSEQ, SEQ, W), F32), jax.ShapeDtypeStruct((NC, NSEQ) + STATE_SHAPE, F32),
                   jax.ShapeDtypeStruct((NC, NSEQ) + M_SHAPE, BF16),
                   jax.ShapeDtypeStruct((NC, NSEQ) + TINV_SHAPE, BF16)],
        scratch_shapes=[pltpu.VMEM((NSEQ,) + STATE_SHAPE, F32)],
        compiler_params=_cparams(("arbitrary",)),
    )(*[_seq_view(a) for a in (r, lw, k2, v, aa, bb)])
    return y.reshape(T, W), hs, ms, ts


def rwkv_bwd(r, lw, k2, v, aa, bb, hs, ms, ts, dy):
    def body(r_ref, lw_ref, k2_ref, v_ref, aa_ref, bb_ref, hs_ref, m_ref, t_ref, dy_ref,
             dr_ref, dlw_ref, dk2_ref, dv_ref, daa_ref, dbb_ref, dstate):
        @pl.when(pl.program_id(0) == 0)
        def _():
            dstate[...] = jnp.zeros_like(dstate)

        keep, tri, _ = _chunk_masks()
        sc = [_scaled(r_ref[s], lw_ref[s], k2_ref[s], aa_ref[s], bb_ref[s], tri) for s in range(NSEQ)]
        ops = [_pairs_operands(*sc[s][3:]) for s in range(NSEQ)]
        x, yk = _both(lambda s: ops[s][0]), _both(lambda s: ops[s][1])
        m = _both(lambda s: [m_ref[0, s][:, i * 4 * L:(i + 1) * 4 * L] for i in range(NPAIR)])
        tinv = _both(lambda s: [t_ref[0, s][:, i * 2 * L:(i + 1) * 2 * L] for i in range(NPAIR)])
        vw = _both(lambda s: [_stack_pair(a) for a in _pairs(v_ref[s])])
        dyw = _both(lambda s: [_stack_pair(a) for a in _pairs(dy_ref[s])])
        s0 = _both(lambda s: _rows_of(hs_ref[0, s], PW))
        dsn = _both(lambda s: _rows_of(dstate[s], PW))
        egl = _both(lambda s: _pairs(sc[s][0][L - 1:L, :]))
        dx, dyk, dvw, dgl, ds0 = _pairs_bwd(x, yk, m, tinv, vw, s0, egl, dyw, dsn, keep)
        for s in range(NSEQ):
            mine = slice(s * NPAIR, (s + 1) * NPAIR)
            eg, eng, egp, at, rt, bt, kt = sc[s]
            dstate[s] = jnp.concatenate(ds0[mine], axis=0)
            dv_ref[s] = jnp.concatenate([_fold_pair(a) for a in dvw[mine]], axis=-1)
            dat = jnp.concatenate([_fold_pair(a[:2 * L]) for a in dx[mine]], axis=-1)
            drt = jnp.concatenate([_fold_pair(a[2 * L:]) for a in dx[mine]], axis=-1)
            dbt = jnp.concatenate([_fold_pair(a[:2 * L]) for a in dyk[mine]], axis=-1)
            dkt = jnp.concatenate([_fold_pair(a[2 * L:]) for a in dyk[mine]], axis=-1)
            dg = drt * rt - dbt * bt - dkt * kt
            dg = dg + jnp.where(_iota2(dg.shape, 0) == L - 1, jnp.concatenate(dgl[mine], axis=-1), 0.0)
            dgp = dat * at
            dlw_ref[s] = _hdot_tn(tri, dg + dgp) - dgp
            dr_ref[s] = drt * eg
            daa_ref[s] = dat * egp
            dbb_ref[s] = dbt * eng
            dk2_ref[s] = dkt * eng

    blk = pl.BlockSpec((NSEQ, L, W), lambda c: (0, NC - 1 - c, 0))
    per_chunk = lambda shape: pl.BlockSpec((1, NSEQ) + shape, lambda c: (NC - 1 - c, 0, 0, 0))
    outs = pl.pallas_call(
        body, grid=(NC,), name="rwkv_bwd",
        in_specs=[blk] * 6 + [per_chunk(STATE_SHAPE), per_chunk(M_SHAPE), per_chunk(TINV_SHAPE), blk],
        out_specs=[blk] * 6,
        out_shape=[jax.ShapeDtypeStruct((NSEQ, SEQ, W), F32)] * 6,
        scratch_shapes=[pltpu.VMEM((NSEQ,) + STATE_SHAPE, F32)],
        compiler_params=_cparams(("arbitrary",)),
    )(*[_seq_view(a) for a in (r, lw, k2, v, aa, bb)], hs, ms, ts, _seq_view(dy))
    return [a.reshape(T, W) for a in outs]


def _post_math(y, r, k2, v, ga, o, gb, lng, lnb, rk, bd):
    mu = _headsum(y, bd) * (1.0 / HD)
    yc = y - mu
    var = _headsum(yc * yc, bd) * (1.0 / HD)
    yn = yc * lax.rsqrt(var + GN_EPS) * lng + lnb
    bonus = _headsum(r * k2 * rk, bd) * v
    return (yn + bonus) * _silu(ga), o * _silu(gb)


def even_post(y, r, k2, v, ga, o, gb, lng, lnb, rk):
    tm = 256

    def body(y_ref, r_ref, k2_ref, v_ref, ga_ref, o_ref, gb_ref, lng_ref, lnb_ref, rk_ref, z_ref, zt_ref):
        ya, yb = _post_math(y_ref[...], r_ref[...], k2_ref[...], v_ref[...], ga_ref[...], o_ref[...], gb_ref[...],
                            lng_ref[...], lnb_ref[...], rk_ref[...], _head_blockdiag())
        ya, yb = ya.astype(BF16), yb.astype(BF16)
        z_ref[:, 0:W] = ya
        z_ref[:, W:2 * W] = yb
        zt_ref[0:W, :] = ya.T
        zt_ref[W:2 * W, :] = yb.T

    vec = _const_spec((1, W))
    return pl.pallas_call(
        body, grid=(T // tm,), name="even_post",
        in_specs=[_row_spec(tm, W)] * 7 + [vec] * 3,
        out_specs=[_row_spec(tm, D), _col_spec(D, tm)],
        out_shape=[jax.ShapeDtypeStruct((T, D), BF16), jax.ShapeDtypeStruct((D, T), BF16)],
        compiler_params=_cparams(("parallel",)),
    )(y, r, k2, v, ga, o, gb, lng, lnb, rk)


def even_post_bwd(y, r, k2, v, ga, o, gb, lng, lnb, rk, dz):
    tm = 256

    def body(y_ref, r_ref, k2_ref, v_ref, ga_ref, o_ref, gb_ref, lng_ref, lnb_ref, rk_ref, dz_ref,
             dy_ref, dr_ref, dk2_ref, dv_ref, dga_ref, do_ref, dgb_ref, dlng_ref, dlnb_ref, drk_ref):
        bd = _head_blockdiag()
        _, vjp = jax.vjp(lambda *a: _post_math(*a, bd), y_ref[...], r_ref[...], k2_ref[...], v_ref[...], ga_ref[...],
                         o_ref[...], gb_ref[...], lng_ref[...], lnb_ref[...], rk_ref[...])
        dzv = dz_ref[...]
        dy, dr, dk2, dv, dga, do, dgb, dlng, dlnb, drk = vjp((dzv[:, 0:W], dzv[:, W:2 * W]))
        for ref, val in ((dy_ref, dy), (dr_ref, dr), (dk2_ref, dk2), (dv_ref, dv), (dga_ref, dga), (do_ref, do),
                         (dgb_ref, dgb)):
            ref[...] = val.astype(ref.dtype)

        @pl.when(pl.program_id(0) == 0)
        def _():
            for ref in (dlng_ref, dlnb_ref, drk_ref):
                ref[...] = jnp.zeros_like(ref)

        dlng_ref[...] += dlng
        dlnb_ref[...] += dlnb
        drk_ref[...] += drk

    vec = _const_spec((1, W))
    return pl.pallas_call(
        body, grid=(T // tm,), name="even_post_bwd",
        in_specs=[_row_spec(tm, W)] * 7 + [vec] * 3 + [_row_spec(tm, D)],
        out_specs=[_row_spec(tm, W)] * 7 + [vec] * 3,
        out_shape=[jax.ShapeDtypeStruct((T, W), dt) for dt in (F32, F32, F32, F32, BF16, F32, BF16)]
        + [jax.ShapeDtypeStruct((1, W), F32)] * 3,
        compiler_params=_cparams(("arbitrary",)),
    )(y, r, k2, v, ga, o, gb, lng, lnb, rk, dz)


PADSEQ = SEQ + LEFT * L
ATT_SCALE = 1.0 / math.sqrt(HD)


def _att_probs(q2, kw, bias, c):
    valid = _iota2((1, BAND), 1) >= (LEFT - c) * L
    s = [jnp.where(valid, _bdot_nt(a, b) * ATT_SCALE + bias[p], NEG) for p, (a, b) in enumerate(zip(q2, kw))]
    e = [jnp.exp(a - jnp.max(a, axis=-1, keepdims=True)) for a in s]
    return [a / jnp.sum(a, axis=-1, keepdims=True) for a in e]


def attention_fwd(q, kpad, vpad, bias):
    def body(q_ref, k_ref, v_ref, b_ref, o_ref):
        c = pl.program_id(1)
        start = pl.multiple_of(c * L, L)
        kw = _pairs(k_ref[pl.ds(start, BAND), :])
        vw = _pairs(v_ref[pl.ds(start, BAND), :])
        q2 = [_stack_pair(a) for a in _pairs(q_ref[...].astype(BF16))]
        p = _att_probs(q2, kw, b_ref[...], c)
        o_ref[...] = jnp.concatenate([_unstack_pair(_bdot(a, b)) for a, b in zip(p, vw)], axis=-1)

    qblk = pl.BlockSpec((L, W), lambda b, c: (b * NC + c, 0))
    kblk = pl.BlockSpec((PADSEQ, W), lambda b, c: (b, 0))
    return pl.pallas_call(
        body, grid=(NSEQ, NC), name="attention_fwd",
        in_specs=[qblk, kblk, kblk, _const_spec((NPAIR, 2 * L, BAND))],
        out_specs=qblk, out_shape=jax.ShapeDtypeStruct((T, W), F32),
        compiler_params=_cparams(("parallel", "arbitrary")),
    )(q, kpad, vpad, bias)


def attention_bwd(q, kpad, vpad, bias, do):
    def body(q_ref, k_ref, v_ref, b_ref, do_ref, dq_ref, dko_ref, dvo_ref, db_ref, dk_ref, dv_ref):
        b = pl.program_id(0)
        c = pl.program_id(1)

        @pl.when(c == 0)
        def _():
            dk_ref[...] = jnp.zeros_like(dk_ref)
            dv_ref[...] = jnp.zeros_like(dv_ref)

        @pl.when((c == 0) & (b == 0))
        def _():
            db_ref[...] = jnp.zeros_like(db_ref)

        start = pl.multiple_of(c * L, L)
        kw = _pairs(k_ref[pl.ds(start, BAND), :])
        vw = _pairs(v_ref[pl.ds(start, BAND), :])
        q2 = [_stack_pair(a) for a in _pairs(q_ref[...].astype(BF16))]
        do2 = [_stack_pair(a) for a in _pairs(do_ref[...].astype(BF16))]
        p = _att_probs(q2, kw, b_ref[...], c)
        dp = [_bdot_nt(a, b) for a, b in zip(do2, vw)]
        ds = [a * (d - jnp.sum(d * a, axis=-1, keepdims=True)) for a, d in zip(p, dp)]
        dss = [(a * ATT_SCALE).astype(BF16) for a in ds]
        dq_ref[...] = jnp.concatenate([_unstack_pair(_bdot(a, b)) for a, b in zip(dss, kw)], axis=-1).astype(BF16)
        dk_ref[pl.ds(start, BAND), :] += jnp.concatenate([_bdot_tn(a, b) for a, b in zip(dss, q2)], axis=-1)
        dv_ref[pl.ds(start, BAND), :] += jnp.concatenate([_bdot_tn(a, b) for a, b in zip(p, do2)], axis=-1)
        for i in range(NPAIR):
            db_ref[i] += ds[i]

        @pl.when(c == NC - 1)
        def _():
            dko_ref[...] = dk_ref[LEFT * L:, :].astype(BF16)
            dvo_ref[...] = dv_ref[LEFT * L:, :].astype(BF16)

    qblk = pl.BlockSpec((L, W), lambda b, c: (b * NC + c, 0))
    kblk = pl.BlockSpec((PADSEQ, W), lambda b, c: (b, 0))
    sblk = pl.BlockSpec((SEQ, W), lambda b, c: (b, 0))
    bblk = _const_spec((NPAIR, 2 * L, BAND))
    return pl.pallas_call(
        body, grid=(NSEQ, NC), name="attention_bwd",
        in_specs=[qblk, kblk, kblk, bblk, qblk],
        out_specs=[qblk, sblk, sblk, bblk],
        out_shape=[jax.ShapeDtypeStruct((T, W), BF16), jax.ShapeDtypeStruct((T, W), BF16),
                   jax.ShapeDtypeStruct((T, W), BF16), jax.ShapeDtypeStruct((NPAIR, 2 * L, BAND), F32)],
        scratch_shapes=[pltpu.VMEM((PADSEQ, W), F32), pltpu.VMEM((PADSEQ, W), F32)],
        compiler_params=_cparams(("arbitrary", "arbitrary"), VMEM_BIG),
    )(q, kpad, vpad, bias, do)


NTAB = 2 * CLIP + 1
EXT = BAND + L


def _ext_onehot():
    n = _iota2((EXT, NTAB), 0)
    m = _iota2((EXT, NTAB), 1)
    return (jnp.clip(BAND - 1 - n, -CLIP, CLIP) + CLIP == m).astype(F32)


def bias_expand(table):
    def body(t_ref, o_ref):
        ext = _hdot_nt(t_ref[...], _ext_onehot())
        for i in range(L):
            s = L - 1 - i
            o_ref[:, i, :] = (pltpu.roll(ext, EXT - s, 1) if s else ext)[:, :BAND]

    return pl.pallas_call(body, name="bias_expand", out_shape=jax.ShapeDtypeStruct((NH, L, BAND), F32))(table)


def bias_grad(dbias):
    def body(d_ref, o_ref):
        acc = jnp.zeros((NH, EXT), F32)
        zpad = jnp.zeros((NH, EXT - BAND), F32)
        for i in range(L):
            s = L - 1 - i
            row = jnp.concatenate([d_ref[:, i, :], zpad], axis=-1)
            acc = acc + (pltpu.roll(row, s, 1) if s else row)
        o_ref[...] = _hdot(acc, _ext_onehot())

    return pl.pallas_call(body, name="bias_grad", out_shape=jax.ShapeDtypeStruct((NH, NTAB), F32))(dbias)


def _group_cols(g):
    return slice(g * SGC, (g + 1) * SGC)


def _sg_norm(v, lng, lnb):
    gv = _gelu(v)
    gc = gv - jnp.mean(gv, axis=-1, keepdims=True)
    rstd = lax.rsqrt(jnp.mean(gc * gc, axis=-1, keepdims=True) + LN_EPS)
    xhat = gc * rstd
    return xhat, rstd, xhat * lng + lnb


def gmlp_fwd(u, v, gate, lng, lnb, wm_bf, sgb_t):
    def body(u_ref, v_ref, gt_ref, lng_ref, lnb_ref, wm_ref, sb_ref, z_ref, zt_ref):
        _, _, vln = _sg_norm(v_ref[...], lng_ref[...], lnb_ref[...])
        vlb = vln.astype(BF16)
        for g in range(NG):
            cs = _group_cols(g)
            sv = jnp.dot(wm_ref[g], vlb[:, cs], preferred_element_type=F32) + sb_ref[:, g:g + 1]
            zg = (_gelu(u_ref[:, cs]) * sv * _silu(gt_ref[:, cs])).astype(BF16)
            z_ref[:, cs] = zg
            zt_ref[cs, :] = zg.T

    return pl.pallas_call(
        body, grid=(T // SGC,), name="gmlp_fwd",
        in_specs=[_row_spec(SGC, D)] * 3 + [_const_spec((1, D))] * 2 + [_const_spec((NG, SGC, SGC)),
                                                                      _const_spec((SGC, NG))],
        out_specs=[_row_spec(SGC, D), _col_spec(D, SGC)],
        out_shape=[jax.ShapeDtypeStruct((T, D), BF16), jax.ShapeDtypeStruct((D, T), BF16)],
        compiler_params=_cparams(("parallel",)),
    )(u, v, gate, lng, lnb, wm_bf, sgb_t)


def gmlp_bwd(u, v, gate, lng, lnb, wm_bf, sgb_t, dz):
    def body(u_ref, v_ref, gt_ref, lng_ref, lnb_ref, wm_ref, sb_ref, dz_ref,
             du_ref, dv_ref, dgt_ref, dlng_ref, dlnb_ref, dwm_ref, dsb_ref):
        @pl.when(pl.program_id(0) == 0)
        def _():
            for ref in (dlng_ref, dlnb_ref, dwm_ref, dsb_ref):
                ref[...] = jnp.zeros_like(ref)

        vv = v_ref[...]
        xhat, rstd, vln = _sg_norm(vv, lng_ref[...], lnb_ref[...])
        vlb = vln.astype(BF16)
        dvln = []
        dsv_all = []
        for g in range(NG):
            cs = _group_cols(g)
            uu = u_ref[:, cs]
            gg = gt_ref[:, cs]
            dzz = dz_ref[:, cs]
            sv = jnp.dot(wm_ref[g], vlb[:, cs], preferred_element_type=F32) + sb_ref[:, g:g + 1]
            gu = _gelu(uu)
            sg = _silu(gg)
            dsv = dzz * gu * sg
            dgt_ref[:, cs] = (dzz * gu * sv * _dsilu(gg)).astype(BF16)
            du_ref[:, cs] = (dzz * sv * sg * _dgelu(uu)).astype(BF16)
            dsb16 = dsv.astype(BF16)
            dvln.append(lax.dot_general(wm_ref[g], dsb16, (((0,), (0,)), ((), ())), preferred_element_type=F32))
            dwm_ref[g] += lax.dot_general(dsb16, vlb[:, cs], (((1,), (1,)), ((), ())), preferred_element_type=F32)
            dsv_all.append(dsv)
        dvl = jnp.concatenate(dvln, axis=-1)
        dsv_cat = jnp.concatenate(dsv_all, axis=-1)
        sel = (_iota2((D, NG), 0) // SGC == _iota2((D, NG), 1)).astype(F32)
        dsb_ref[...] += _hdot(dsv_cat, sel)
        dlng_ref[...] += jnp.sum(dvl * xhat, axis=0, keepdims=True)
        dlnb_ref[...] += jnp.sum(dvl, axis=0, keepdims=True)
        dxh = dvl * lng_ref[...]
        dgv = rstd * (dxh - jnp.mean(dxh, axis=-1, keepdims=True)
                      - xhat * jnp.mean(dxh * xhat, axis=-1, keepdims=True))
        dv_ref[...] = (dgv * _dgelu(vv)).astype(BF16)

    return pl.pallas_call(
        body, grid=(T // SGC,), name="gmlp_bwd",
        in_specs=[_row_spec(SGC, D)] * 3 + [_const_spec((1, D))] * 2
        + [_const_spec((NG, SGC, SGC)), _const_spec((SGC, NG)), _row_spec(SGC, D)],
        out_specs=[_row_spec(SGC, D)] * 3 + [_const_spec((1, D))] * 2 + [_const_spec((NG, SGC, SGC)),
                                                                       _const_spec((SGC, NG))],
        out_shape=[jax.ShapeDtypeStruct((T, D), BF16)] * 3 + [jax.ShapeDtypeStruct((1, D), F32)] * 2
        + [jax.ShapeDtypeStruct((NG, SGC, SGC), F32), jax.ShapeDtypeStruct((SGC, NG), F32)],
        compiler_params=_cparams(("arbitrary",)),
    )(u, v, gate, lng, lnb, wm_bf, sgb_t, dz)


NCHIP = 4
NDEV = 8
ANY = pl.BlockSpec(memory_space=pl.ANY)


HBM = pl.BlockSpec(memory_space=pltpu.HBM)
SEM = pl.BlockSpec(memory_space=pltpu.SEMAPHORE)
EFFECT = pltpu.SideEffectType.DATAFLOW_SIDE_EFFECTING


def _peers(whole_mesh):
    x, y, c = lax.axis_index("x"), lax.axis_index("y"), lax.axis_index("c")
    if not whole_mesh:
        return [((px, py, c), 2 * px + py) for px, py in ((1 - x, y), (x, 1 - y), (1 - x, 1 - y))], 2 * x + y
    out = []
    for j in range(1, NDEV):
        px, py, pc = x ^ (j >> 2), y ^ ((j >> 1) & 1), c ^ (j & 1)
        out.append(((px, py, pc), 4 * px + 2 * py + pc))
    return out, 4 * x + 2 * y + c


def _send_copies(src, land, send, recv, scatter, whole_mesh, starting):
    peers, me = _peers(whole_mesh)
    copies = []
    for t in range(len(src)):
        for j, (dev, slot) in enumerate(peers):
            k = t * len(peers) + j
            copies.append(pltpu.make_async_remote_copy(
                src_ref=src[t].at[slot] if scatter else src[t], dst_ref=land[t].at[me if starting else slot],
                send_sem=send.at[k], recv_sem=recv.at[k], device_id=dev, device_id_type=MESH))
    return copies


def send_start(srcs, lands, scatter, whole_mesh, name):
    n = len(srcs)
    nsem = n * (NDEV - 1 if whole_mesh else NCHIP - 1)

    def body(*refs):
        for cp in _send_copies(refs[:n], refs[n:2 * n], refs[2 * n], refs[2 * n + 1], scatter, whole_mesh, True):
            cp.start()
        refs[-1][...] = jnp.zeros_like(refs[-1])

    arrs = list(srcs) + list(lands)
    out = pl.pallas_call(
        body, name=name,
        out_shape=(pltpu.SemaphoreType.DMA((nsem,)), pltpu.SemaphoreType.DMA((nsem,)),
                   *[pltpu.HBM(a.shape, a.dtype) for a in arrs], jax.ShapeDtypeStruct((8, 128), F32)),
        in_specs=[HBM] * (2 * n), out_specs=(SEM, SEM, *[HBM] * (2 * n), pl.BlockSpec(memory_space=pltpu.VMEM)),
        input_output_aliases={i: 2 + i for i in range(2 * n)},
        compiler_params=pltpu.CompilerParams(has_side_effects=EFFECT),
    )(*[pltpu.with_memory_space_constraint(a, pltpu.HBM) for a in arrs])
    return out[0], out[1], list(out[2:2 + n]), list(out[2 + n:2 + 2 * n]), out[-1]


def send_wait(started, after, scatter, whole_mesh, name):
    send, recv, srcs, lands, _ = started
    n = len(srcs)

    def body(*refs):
        for cp in _send_copies(refs[:n], refs[n:2 * n], refs[2 * n], refs[2 * n + 1], scatter, whole_mesh, False):
            cp.wait_send()
            cp.wait_recv()

    arrs = list(srcs) + list(lands)
    out = pl.pallas_call(
        body, name=name, out_shape=tuple(pltpu.HBM(a.shape, a.dtype) for a in arrs),
        in_specs=[HBM] * (2 * n) + [SEM, SEM, ANY], out_specs=tuple([HBM] * (2 * n)),
        input_output_aliases={i: i for i in range(2 * n)},
        compiler_params=pltpu.CompilerParams(has_side_effects=EFFECT),
    )(*arrs, send, recv, after)
    return list(out[n:])


def exchange_c(arrs, name):
    n = len(arrs)

    def body(*refs):
        ins, outs = refs[:n], refs[n:2 * n]
        send, recv = refs[2 * n:]
        sibling = (lax.axis_index("x"), lax.axis_index("y"), 1 - lax.axis_index("c"))
        copies = [pltpu.make_async_remote_copy(src_ref=ins[t], dst_ref=outs[t], send_sem=send.at[t], recv_sem=recv.at[t],
                                               device_id=sibling, device_id_type=MESH) for t in range(n)]
        for cp in copies:
            cp.start()
        for cp in copies:
            cp.wait()

    return pl.pallas_call(
        body, name=name, in_specs=[ANY] * n, out_specs=[ANY] * n,
        out_shape=[jax.ShapeDtypeStruct(a.shape, a.dtype) for a in arrs],
        scratch_shapes=[pltpu.SemaphoreType.DMA((n,)), pltpu.SemaphoreType.DMA((n,))],
    )(*arrs)


def swap_row_halves(a, name):
    n, rows, cols = a.shape
    half = rows // 2

    def body(in_ref, out_ref, send, recv):
        x, y, c = lax.axis_index("x"), lax.axis_index("y"), lax.axis_index("c")
        cp = pltpu.make_async_remote_copy(src_ref=in_ref.at[:, pl.ds((1 - c) * half, half), :], dst_ref=out_ref,
                                          send_sem=send, recv_sem=recv, device_id=(x, y, 1 - c), device_id_type=MESH)
        cp.start()
        cp.wait()

    return pl.pallas_call(
        body, name=name, in_specs=[ANY], out_specs=ANY, out_shape=jax.ShapeDtypeStruct((n, half, cols), a.dtype),
        scratch_shapes=[pltpu.SemaphoreType.DMA, pltpu.SemaphoreType.DMA],
    )(a)


def add_blocks(a, b, name):
    n, rows, cols = a.shape
    tr = _rows_tile(rows)

    def body(a_ref, b_ref, o_ref):
        o_ref[...] = (a_ref[...].astype(F32) + b_ref[...].astype(F32)).astype(BF16)

    spec = pl.BlockSpec((1, tr, cols), lambda s, i: (s, i, 0))
    return pl.pallas_call(
        body, grid=(n, rows // tr), name=name, in_specs=[spec, spec], out_specs=spec,
        out_shape=jax.ShapeDtypeStruct(a.shape, BF16), compiler_params=_cparams(("parallel", "parallel")),
    )(a, b)


def gather_weights(arrs, split):
    n = len(arrs)

    def body(*refs):
        ins, outs = refs[:n], refs[n:2 * n]
        send1, recv1, send2, recv2, loc = refs[2 * n:]
        x, y, c = lax.axis_index("x"), lax.axis_index("y"), lax.axis_index("c")
        me = 2 * x + y
        sibling = (x, y, 1 - c)
        peers = [(1 - x, y), (x, 1 - y), (1 - x, 1 - y)]

        def rows_of(t, core):
            half = arrs[t].shape[0] // 2
            return pl.ds(core * half, half)

        def part(ref, t, core):
            return ref.at[rows_of(t, core)] if split[t] else ref

        local = [pltpu.make_async_copy(ins[t], outs[t].at[me], loc.at[t]) for t in range(n)]
        for cp in local:
            cp.start()
        first = []
        for t in range(n):
            for j, (px, py) in enumerate(peers):
                first.append(pltpu.make_async_remote_copy(
                    src_ref=part(ins[t], t, c), dst_ref=part(outs[t].at[me], t, c), send_sem=send1.at[t, j],
                    recv_sem=recv1.at[t, j], device_id=(px, py, c), device_id_type=MESH))
        for cp in first:
            cp.start()
        passed = []
        for t in range(n):
            for j, (px, py) in enumerate(peers):
                landed = part(outs[t].at[2 * px + py], t, c)
                pltpu.make_async_remote_copy(
                    src_ref=landed, dst_ref=landed, send_sem=send1.at[t, j], recv_sem=recv1.at[t, j],
                    device_id=(x, y, c), device_id_type=MESH).wait_recv()
                if split[t]:
                    cp = pltpu.make_async_remote_copy(
                        src_ref=landed, dst_ref=landed, send_sem=send2.at[t, j], recv_sem=recv2.at[t, j],
                        device_id=sibling, device_id_type=MESH)
                    cp.start()
                    passed.append(cp)
        for t in range(n):
            for j, (px, py) in enumerate(peers):
                if split[t]:
                    other = part(outs[t].at[2 * px + py], t, 1 - c)
                    pltpu.make_async_remote_copy(
                        src_ref=other, dst_ref=other, send_sem=send2.at[t, j], recv_sem=recv2.at[t, j],
                        device_id=(x, y, c), device_id_type=MESH).wait_recv()
        for cp in first + passed:
            cp.wait_send()
        for cp in local:
            cp.wait()

    return pl.pallas_call(
        body, name="gather_weights", in_specs=[ANY] * n, out_specs=[ANY] * n,
        out_shape=[jax.ShapeDtypeStruct((NCHIP,) + a.shape, a.dtype) for a in arrs],
        scratch_shapes=[pltpu.SemaphoreType.DMA((n, 3))] * 4 + [pltpu.SemaphoreType.DMA((n,))],
    )(*arrs)


def _adam_math(g, w, m, v):
    m = ADAM_B1 * m + (1.0 - ADAM_B1) * g
    v = ADAM_B2 * v + (1.0 - ADAM_B2) * (g * g)
    m_hat = m / (1.0 - ADAM_B1 ** ADAM_STEP)
    v_hat = v / (1.0 - ADAM_B2 ** ADAM_STEP)
    delta = -ADAM_LR * (m_hat / (jnp.sqrt(v_hat) + ADAM_EPS) + ADAM_WD * w)
    return delta, m, v


def _rows_tile(rows):
    return rows if rows <= 256 else 256


def sum_chips(own, parts, name):
    _, rows, cols = parts.shape
    tr = _rows_tile(rows)

    def body(own_ref, p_ref, o_ref):
        acc = own_ref[...].astype(F32)
        for s in range(NCHIP):
            acc = acc + p_ref[s].astype(F32)
        o_ref[...] = acc

    return pl.pallas_call(
        body, grid=(rows // tr,), name=name,
        in_specs=[pl.BlockSpec((tr, cols), lambda i: (i, 0)), pl.BlockSpec((NCHIP, tr, cols), lambda i: (0, i, 0))],
        out_specs=pl.BlockSpec((tr, cols), lambda i: (i, 0)),
        out_shape=jax.ShapeDtypeStruct((rows, cols), F32),
        compiler_params=_cparams(("parallel",)),
    )(own, parts)


def adam_shard(p_mine, p_sib, w, m, v, name):
    rows, cols = p_mine.shape
    tr = _rows_tile(rows)
    lead = w.ndim == 3

    def body(a_ref, b_ref, w_ref, m_ref, v_ref, g_ref, d_ref, mo_ref, vo_ref):
        g = a_ref[...] + b_ref[...]
        g = g[None] if lead else g
        g_ref[...] = g
        d_ref[...], mo_ref[...], vo_ref[...] = _adam_math(g, w_ref[...], m_ref[...], v_ref[...])

    flat = pl.BlockSpec((tr, cols), lambda i: (i, 0))
    spec = pl.BlockSpec((1, tr, cols), lambda i: (0, i, 0)) if lead else flat
    return pl.pallas_call(
        body, grid=(rows // tr,), name=name, in_specs=[flat] * 2 + [spec] * 3, out_specs=[spec] * 4,
        out_shape=[jax.ShapeDtypeStruct(w.shape, F32)] * 4,
        compiler_params=_cparams(("parallel",)),
    )(p_mine, p_sib, w, m, v)


def adam_shard_halves_t(r_mine, r_sib, wt, mt, vt, name):
    hrows, cols = r_mine.shape
    tr = _rows_tile(hrows)
    per_half = hrows // tr

    def body(a_ref, b_ref, w_ref, m_ref, v_ref, g_ref, d_ref, mo_ref, vo_ref):
        mine = pl.program_id(0) == lax.axis_index("c")
        g = jnp.where(mine, a_ref[...], b_ref[...]).T[None]
        g_ref[...] = g
        d_ref[...], mo_ref[...], vo_ref[...] = _adam_math(g, w_ref[...], m_ref[...], v_ref[...])

    flat = pl.BlockSpec((tr, cols), lambda h, i: (i, 0))
    spec = pl.BlockSpec((1, cols, tr), lambda h, i: (0, 0, h * per_half + i))
    return pl.pallas_call(
        body, grid=(2, per_half), name=name, in_specs=[flat] * 2 + [spec] * 3, out_specs=[spec] * 4,
        out_shape=[jax.ShapeDtypeStruct(wt.shape, F32)] * 4,
        compiler_params=_cparams(("parallel", "parallel")),
    )(r_mine, r_sib, wt, mt, vt)


def adam_replicated(parts, w, m, v, name):
    rows = w.shape[0]

    def body(p_ref, w_ref, m_ref, v_ref, g_ref, d_ref, mo_ref, vo_ref):
        g = p_ref[0]
        for d in range(1, NDEV):
            g = g + p_ref[d]
        g_ref[...] = g
        d_ref[...], mo_ref[...], vo_ref[...] = _adam_math(g, w_ref[...], m_ref[...], v_ref[...])

    return pl.pallas_call(
        body, name=name, out_shape=[jax.ShapeDtypeStruct((rows, 128), F32)] * 4,
    )(parts, w, m, v)


def _pack(arrs):
    pieces = []
    for a in arrs:
        flat = a.reshape(-1)
        pad = (-flat.shape[0]) % 128
        pieces.append(jnp.pad(flat, (0, pad)) if pad else flat)
    flat = jnp.concatenate(pieces)
    pad = (-flat.shape[0]) % 1024
    return jnp.pad(flat, (0, pad)).reshape(-1, 128)


def _unpack(buf, shapes):
    flat = buf.reshape(-1)
    out = []
    o = 0
    for s in shapes:
        n = int(np.prod(s))
        out.append(flat[o:o + n].reshape(s))
        o += n + (-n) % 128
    return out


EVEN_SPLITS = (SHIFT, W, W, W, W, W)
ODD_SPLITS = (D, D, D)


def _cols_to_chips(a):
    rows, cols = a.shape
    return a.reshape(rows, NCHIP, cols // NCHIP).transpose(1, 0, 2)


def _chips_to_cols(a):
    _, rows, n = a.shape
    return a.transpose(1, 0, 2).reshape(rows, NCHIP * n)


def kernel(x, norm_g, w_in_e, shift_mu, rw_w0, rw_w2, rw_a0, rw_a2, rw_kk, rw_ka, rw_rk, rw_lnx_g, rw_lnx_b, att_bias, w_out_e, w_in_o, sg_ln_g, sg_ln_b, sg_w, sg_b, w_out_o, final_g, loss_target, m_norm_g, m_w_in_e, m_shift_mu, m_rw_w0, m_rw_w2, m_rw_a0, m_rw_a2, m_rw_kk, m_rw_ka, m_rw_rk, m_rw_lnx_g, m_rw_lnx_b, m_att_bias, m_w_out_e, m_w_in_o, m_sg_ln_g, m_sg_ln_b, m_sg_w, m_sg_b, m_w_out_o, m_final_g, v_norm_g, v_w_in_e, v_shift_mu, v_rw_w0, v_rw_w2, v_rw_a0, v_rw_a2, v_rw_kk, v_rw_ka, v_rw_rk, v_rw_lnx_g, v_rw_lnx_b, v_att_bias, v_w_out_e, v_w_in_o, v_sg_ln_g, v_sg_ln_b, v_sg_w, v_sg_b, v_w_out_o, v_final_g):
    x2 = x.reshape(T, D)
    tgt = loss_target.reshape(T, D)

    my_chip = 2 * lax.axis_index("x") + lax.axis_index("y")
    gathered = gather_weights(
        [jnp.swapaxes(w_in_e[0], 0, 1).astype(BF16), jnp.concatenate([rw_w2[0], rw_a2[0]], axis=0),
         jnp.concatenate([sg_ln_g, sg_ln_b], axis=0)], [True, True, False])
    wie = gathered[0].reshape(EVEN_IN, D)
    w2 = _chips_to_cols(gathered[1][:, :LORA])
    a2 = _chips_to_cols(gathered[1][:, LORA:])
    sglg = _chips_to_cols(gathered[2][:, 0:1])
    sglb = _chips_to_cols(gathered[2][:, 1:2])

    late = [w_out_e[0].astype(BF16), w_in_o[0].astype(BF16), w_out_o[0].astype(BF16)]
    late_started = send_start(late, [jnp.broadcast_to(a[None], (NCHIP,) + a.shape) for a in late], False, False,
                              "late_weights_start")

    def late_weights(after):
        woe, wio, woo = send_wait(late_started, after, False, False, "late_weights_wait")
        return woe.reshape(D, D), _chips_to_cols(wio), woo.reshape(D, D)

    def scatter_start(grads, name):
        srcs = [g_.astype(BF16) if g_.shape[-1] >= W else g_ for g_ in grads]
        return send_start(srcs, [jnp.zeros_like(s) for s in srcs], True, False, name)

    def own_block(g_):
        return lax.dynamic_index_in_dim(g_, my_chip, axis=0, keepdims=False)

    started = {}

    def on_odd_grads(d_woo, d_wio):
        blocks = [d_woo.reshape(NCHIP, D // NCHIP, D), d_wio]
        started["odd"] = (scatter_start(blocks, "odd_grads_start"), [own_block(b) for b in blocks])
        return started["odd"][0][-1]

    def on_even_grads(big_g):
        d_wie, d_woe, _, _, d_w2, d_a2, d_sglg, d_sglb = big_g
        my_half = lax.dynamic_slice_in_dim(d_wie, lax.axis_index("c") * (D // 2), D // 2, axis=1)
        d_wie_half = add_blocks(my_half, swap_row_halves(d_wie, "swap_w_in_e_halves"), "add_w_in_e_halves")
        blocks = [d_wie_half, d_woe.reshape(NCHIP, D // NCHIP, D), _cols_to_chips(d_w2), _cols_to_chips(d_a2),
                  _cols_to_chips(d_sglg), _cols_to_chips(d_sglb)]
        started["even"] = (scatter_start(blocks, "even_grads_start"), [own_block(b) for b in blocks])
        return started["even"][0][-1]

    def on_small_grads(layer, grads):
        mine = _pack(grads)
        started[layer + "_small"] = send_start([mine], [jnp.broadcast_to(mine[None], (NDEV,) + mine.shape)], False,
                                               True, layer + "_small_grads_start")
        return started[layer + "_small"][-1]

    loss_part, dx, _, _ = _local_step(
        x2, tgt, wie, late_weights, w2, a2, sglg, sglb, norm_g, shift_mu, rw_w0, rw_a0, rw_kk, rw_ka, rw_rk,
        rw_lnx_g, rw_lnx_b, att_bias, sg_w, sg_b, final_g, first_after=late_started[-1], on_odd_grads=on_odd_grads,
        on_even_grads=on_even_grads, on_small_grads=on_small_grads)
    even_started, even_own = started["even"]

    wmv = {"w_in_e": tuple(jnp.swapaxes(a, 1, 2) for a in (w_in_e, m_w_in_e, v_w_in_e)),
           "w_out_e": (w_out_e, m_w_out_e, v_w_out_e),
           "w_in_o": (w_in_o, m_w_in_o, v_w_in_o), "w_out_o": (w_out_o, m_w_out_o, v_w_out_o),
           "rw_w2": (rw_w2, m_rw_w2, v_rw_w2), "rw_a2": (rw_a2, m_rw_a2, v_rw_a2),
           "sg_ln_g": (sg_ln_g, m_sg_ln_g, v_sg_ln_g), "sg_ln_b": (sg_ln_b, m_sg_ln_b, v_sg_ln_b)}
    sharded = {}

    def finish(names, own, landed, tag):
        partial = [sum_chips(o_, p_, "sum_" + nm) for o_, p_, nm in zip(own, landed, names)]
        from_sibling = exchange_c(partial, "swap_partials_" + tag)
        for nm, mine, sib in zip(names, partial, from_sibling):
            if nm == "w_in_e":
                res = adam_shard_halves_t(mine, sib, *wmv[nm], "adam_" + nm)
                sharded[nm] = [jnp.swapaxes(a, 1, 2) for a in res]
            else:
                sharded[nm] = adam_shard(mine, sib, *wmv[nm], "adam_" + nm)
        return partial[0]

    odd_started, odd_own = started["odd"]
    odd_landed = send_wait(odd_started, started["even_small"][-1], True, False, "odd_grads_wait")
    done = finish(["w_out_o", "w_in_o"], odd_own, odd_landed, "odd")

    no_w = jnp.zeros((1, 1), F32)
    groups = {
        "odd": (["sg_w", "sg_b", "final_g", "norm_g1"], [sg_w, sg_b, final_g, norm_g[1:2]],
                [m_sg_w, m_sg_b, m_final_g, m_norm_g[1:2]], [v_sg_w, v_sg_b, v_final_g, v_norm_g[1:2]]),
        "even": (["norm_g0", "shift_mu", "rw_w0", "rw_a0", "rw_kk", "rw_ka", "rw_rk", "rw_lnx_g", "rw_lnx_b",
                  "att_bias", "loss"],
                 [norm_g[0:1], shift_mu, rw_w0, rw_a0, rw_kk, rw_ka, rw_rk, rw_lnx_g, rw_lnx_b, att_bias, no_w],
                 [m_norm_g[0:1], m_shift_mu, m_rw_w0, m_rw_a0, m_rw_kk, m_rw_ka, m_rw_rk, m_rw_lnx_g, m_rw_lnx_b,
                  m_att_bias, no_w],
                 [v_norm_g[0:1], v_shift_mu, v_rw_w0, v_rw_a0, v_rw_kk, v_rw_ka, v_rw_rk, v_rw_lnx_g, v_rw_lnx_b,
                  v_att_bias, no_w]),
    }
    rep = {}
    for layer in ("odd", "even"):
        nms, ws, ms_, vs_ = groups[layer]
        (gathered_g,) = send_wait(started[layer + "_small"], done, False, True, layer + "_small_grads_wait")
        rep_out = adam_replicated(gathered_g, _pack(ws), _pack(ms_), _pack(vs_), "adam_" + layer + "_small")
        done = rep_out[0]
        for nm in nms:
            rep[nm] = []
        for buf in rep_out:
            for nm, a in zip(nms, _unpack(buf, [w_.shape for w_ in ws])):
                rep[nm].append(a)
    rep["norm_g"] = [jnp.concatenate([a, b], axis=0) for a, b in zip(rep["norm_g0"], rep["norm_g1"])]
    even_landed = send_wait(even_started, done, True, False, "even_grads_wait")
    finish(["w_in_e", "w_out_e", "rw_w2", "rw_a2", "sg_ln_g", "sg_ln_b"], even_own, even_landed, "even")

    order = ["norm_g", "w_in_e", "shift_mu", "rw_w0", "rw_w2", "rw_a0", "rw_a2", "rw_kk", "rw_ka", "rw_rk",
             "rw_lnx_g", "rw_lnx_b", "att_bias", "w_out_e", "w_in_o", "sg_ln_g", "sg_ln_b", "sg_w", "sg_b",
             "w_out_o", "final_g"]
    results = {**sharded, **rep}
    outs = [rep["loss"][0].reshape(()), dx.reshape(NSEQ, SEQ, D)]
    for kind in range(4):
        outs += [results[nm][kind] for nm in order]
    return tuple(outs)


def _local_step(x2, tgt, wie_t, late_weights, w2, a2, sglg, sglb, norm_g, shift_mu, rw_w0, rw_a0, rw_kk, rw_ka, rw_rk,
                rw_lnx_g, rw_lnx_b, att_bias, sg_w, sg_b, final_g, first_after=None, on_odd_grads=None,
                on_even_grads=None, on_small_grads=None):
    zl = jnp.zeros((LORA, W), F32)
    w2x = jnp.concatenate([w2, zl], axis=0)
    a2x = jnp.concatenate([zl, a2], axis=0)
    rk = rw_rk.reshape(1, W)
    pos = np.arange(SGC)
    sg_mask = jnp.asarray(((pos[None, :] // L) <= (pos[:, None] // L)).astype(np.float32))
    wm = (sg_w[0] * sg_mask[None]).astype(BF16)
    sgb_t = sg_b[0].T

    xn0, ps, ga, q, kb, vb, gb = ln_in_proj(x2, norm_g[0:1], wie_t, EVEN_SPLITS, "in_proj_even", after=first_after,
                                            w_t=True)
    r, lw, k2, v, aa, bb = even_prep(ps, shift_mu, rw_w0, w2x, rw_a0, a2x, rw_kk, rw_ka)
    y, hs, ms, ts = rwkv_fwd(r, lw, k2, v, aa, bb)
    bias = bias_expand(att_bias[0]).reshape(NPAIR, 2 * L, BAND)

    def padded(a):
        return jnp.pad(a.astype(BF16).reshape(NSEQ, SEQ, W), ((0, 0), (LEFT * L, 0), (0, 0))).reshape(NSEQ * PADSEQ, W)

    kpad, vpad = padded(kb), padded(vb)
    o = attention_fwd(q, kpad, vpad, bias)
    z, zt = even_post(y, r, k2, v, ga, o, gb, rw_lnx_g, rw_lnx_b, rk)
    woe, wio, woo = late_weights(z)
    h1 = out_proj(x2, z, woe, "out_proj_even")
    xn1, u, vv, gt = ln_in_proj(h1, norm_g[1:2], wio, ODD_SPLITS, "in_proj_odd")
    z2, z2t = gmlp_fwd(u, vv, gt, sglg, sglb, wm, sgb_t)
    dh2, loss_part, d_final_g = out_proj_loss(h1, z2, woo, final_g[None], tgt)

    dz2, d_woo = out_proj_bwd(dh2, z2t, woo, "out_proj_odd_bwd")
    du, dvv, dgt, d_sglg, d_sglb, d_wm, d_sgb_t = gmlp_bwd(u, vv, gt, sglg, sglb, wm, sgb_t, dz2)
    dp_odd = [du, dvv, dgt]
    d_wio = matmul_acc_chips(xn1, dp_odd, "in_proj_odd_dw")
    token = on_odd_grads(d_woo, d_wio) if on_odd_grads else None
    dh1, d_g1 = in_proj_bwd_x(h1, norm_g[1:2], wio, dp_odd, dh2, "in_proj_odd_bwd", after=token)
    odd_small = [d_wm * sg_mask[None], d_sgb_t.T, d_final_g, d_g1]
    token = on_small_grads("odd", odd_small) if on_small_grads else None
    dz, d_woe = out_proj_bwd(dh1, zt, woe, "out_proj_even_bwd", after=token)
    dy, dr2, dk22, dv2, dga, do, dgb, d_lng, d_lnb, d_rk = even_post_bwd(
        y, r, k2, v, ga, o, gb, rw_lnx_g, rw_lnx_b, rk, dz)
    dq, dkb, dvb, dbias = attention_bwd(q, kpad, vpad, bias, do)
    d_att_bias = bias_grad(dbias.reshape(NH, L, BAND))
    dr, dlw, dk2, dv, daa, dbb = rwkv_bwd(r, lw, k2, v, aa, bb, hs, ms, ts, dy)
    dps, d_mu, d_w0, d_w2x, d_a0, d_a2x, d_kk, d_ka = even_prep_bwd(
        ps, shift_mu, rw_w0, w2x, rw_a0, a2x, rw_kk, rw_ka, dr, dlw, dk2, dv, daa, dbb, dr2, dk22, dv2)
    dp_even = [dps, dga, dq, dkb, dvb, dgb]
    d_wie = matmul_acc_chips(xn0, dp_even, "in_proj_even_dw")
    big_g = (d_wie, d_woe, d_wio, d_woo, d_w2x[:LORA], d_a2x[LORA:], d_sglg, d_sglb)
    token = on_even_grads(big_g) if on_even_grads else None
    dx, d_g0 = in_proj_bwd_x(x2, norm_g[0:1], wie_t, dp_even, dh1, "in_proj_even_bwd", after=token, w_t=True)
    even_small = [d_g0, d_mu, d_w0, d_a0, d_kk, d_ka, d_rk, d_lng, d_lnb, d_att_bias]
    if on_small_grads:
        on_small_grads("even", even_small + [loss_part[0:1, 0:1]])
    rep_g = [jnp.concatenate([d_g0, d_g1], axis=0)] + even_small[1:] + odd_small[:3]
    return loss_part[0, 0], dx, big_g, rep_g
```

```python
import functools
import math

import jax
import jax.numpy as jnp
import numpy as np
from jax import lax
from jax.experimental import pallas as pl
from jax.experimental.pallas import tpu as pltpu

F32 = jnp.float32
BF16 = jnp.bfloat16
HI = lax.Precision.HIGHEST

D = 1024
SEQ = 2048
NSEQ = 2
T = NSEQ * SEQ
HD = 64
NH = 8
W = 512
SHIFT = 1664
LORA = 64
EVEN_IN = 4224
ODD_IN = 3072
L = 64
NC = SEQ // L
LEFT = 8
BAND = (LEFT + 1) * L
CLIP = 128
SGC = 128
NG = 8
RMS_EPS = 1e-6
LN_EPS = 1e-5
GN_EPS = 64e-5
NEG = -1e30
VMEM_BIG = 56 * 1024 * 1024

ADAM_LR = 0.001
ADAM_B1 = 0.9
ADAM_B2 = 0.999
ADAM_EPS = 1e-08
ADAM_WD = 0.01
ADAM_STEP = 10

MESH = pl.DeviceIdType.MESH


def _bdot(a, b):
    return jnp.dot(a.astype(BF16), b.astype(BF16), preferred_element_type=F32)


def _bdot_nt(a, b):
    return lax.dot_general(a.astype(BF16), b.astype(BF16), (((1,), (1,)), ((), ())), preferred_element_type=F32)


def _bdot_tn(a, b):
    return lax.dot_general(a.astype(BF16), b.astype(BF16), (((0,), (0,)), ((), ())), preferred_element_type=F32)


def _hdot(a, b):
    return jnp.dot(a, b, precision=HI, preferred_element_type=F32)


def _hdot_nt(a, b):
    return lax.dot_general(a, b, (((1,), (1,)), ((), ())), precision=HI, preferred_element_type=F32)


def _hdot_tn(a, b):
    return lax.dot_general(a, b, (((0,), (0,)), ((), ())), precision=HI, preferred_element_type=F32)


def _iota2(shape, dim):
    return lax.broadcasted_iota(jnp.int32, shape, dim)


def _head_blockdiag():
    r = _iota2((W, W), 0) // HD
    c = _iota2((W, W), 1) // HD
    return (r == c).astype(BF16)


def _headsum_impl(x, bd):
    hi = x.astype(BF16)
    mid = (x - hi.astype(F32)).astype(BF16)
    return jnp.dot(hi, bd, preferred_element_type=F32) + jnp.dot(mid, bd, preferred_element_type=F32)


@jax.custom_vjp
def _headsum(x, bd):
    return _headsum_impl(x, bd)


def _headsum_fwd(x, bd):
    return _headsum_impl(x, bd), bd


def _headsum_bwd(bd, ct):
    return _headsum_impl(ct, bd), None


_headsum.defvjp(_headsum_fwd, _headsum_bwd)


def _silu(x):
    return x * jax.nn.sigmoid(x)


def _dsilu(x):
    s = jax.nn.sigmoid(x)
    return s * (1.0 + x * (1.0 - s))


_GELU_C = math.sqrt(2.0 / math.pi)


def _gelu(x):
    return 0.5 * x * (1.0 + jnp.tanh(_GELU_C * (x + 0.044715 * (x * x * x))))


def _dgelu(x):
    t = jnp.tanh(_GELU_C * (x + 0.044715 * (x * x * x)))
    return 0.5 * (1.0 + t) + 0.5 * x * (1.0 - t * t) * _GELU_C * (1.0 + 3.0 * 0.044715 * x * x)


def _silu_both(x):
    s = jax.nn.sigmoid(x)
    xs = x * s
    return xs, s + xs * (1.0 - s)


def _gelu_both(x):
    x2 = x * x
    t = jnp.tanh(_GELU_C * (x + 0.044715 * (x2 * x)))
    half = 0.5 * (1.0 + t)
    return x * half, half + 0.5 * x * (1.0 - t * t) * _GELU_C * (1.0 + 3.0 * 0.044715 * x2)


def _softplus(x):
    return jnp.maximum(x, 0.0) + jnp.log(1.0 + jnp.exp(-jnp.abs(x)))


def _cparams(sem, vmem=None):
    return pltpu.CompilerParams(dimension_semantics=sem, vmem_limit_bytes=vmem)


def _row_spec(tm, width):
    return pl.BlockSpec((tm, width), lambda i: (i, 0))


def _col_spec(height, tm):
    return pl.BlockSpec((height, tm), lambda i: (0, i))


def _const_spec(shape):
    nd = len(shape)
    return pl.BlockSpec(shape, lambda *_: (0,) * nd)


def _weight_dims(w_bf, w_t):
    return (((1,), (1,)), ((), ())) if w_t else (((1,), (0,)), ((), ())), w_bf.shape[0 if w_t else 1]


def ln_in_proj(x, g, w_bf, splits, name, after=None, w_t=False):
    dims, n = _weight_dims(w_bf, w_t)
    tm = 256
    spans = []
    o = 0
    for s in splits:
        spans.append((o, o + s))
        o += s
    assert o == n
    extra_specs, extra = _after_operand(after)

    def body(x_ref, g_ref, w_ref, *rest):
        xn_ref, outs = rest[len(extra)], rest[len(extra) + 1:]
        xv = x_ref[...]
        rstd = lax.rsqrt(jnp.mean(xv * xv, axis=-1, keepdims=True) + RMS_EPS)
        xn = (xv * rstd * g_ref[...]).astype(BF16)
        xn_ref[...] = xn.T
        p = lax.dot_general(xn, w_ref[...], dims, preferred_element_type=F32)
        for o_ref, (a, b) in zip(outs, spans):
            o_ref[...] = p[:, a:b]

    return pl.pallas_call(
        body, grid=(T // tm,), name=name,
        in_specs=[_row_spec(tm, D), _const_spec((1, D)), _const_spec(w_bf.shape)] + extra_specs,
        out_specs=[_col_spec(D, tm)] + [_row_spec(tm, s) for s in splits],
        out_shape=[jax.ShapeDtypeStruct((D, T), BF16)] + [jax.ShapeDtypeStruct((T, s), F32) for s in splits],
        compiler_params=_cparams(("parallel",), VMEM_BIG),
    )(x, g, w_bf, *extra)


def in_proj_bwd_x(x, g, w_bf, dps, dres, name, after=None, w_t=False):
    tm = 512
    back = (((1,), (0,)), ((), ())) if w_t else (((1,), (1,)), ((), ()))
    widths = [d.shape[1] for d in dps]
    extra_specs, extra = _after_operand(after)

    def body(x_ref, g_ref, w_ref, dres_ref, *rest):
        dp_refs = rest[:len(widths)]
        dx_ref, dg_ref = rest[-2:]
        dp = jnp.concatenate([r[...] for r in dp_refs], axis=-1)
        dxn = lax.dot_general(dp, w_ref[...], back, preferred_element_type=F32)
        xv = x_ref[...]
        rstd = lax.rsqrt(jnp.mean(xv * xv, axis=-1, keepdims=True) + RMS_EPS)
        xhat = xv * rstd
        dgp = jnp.sum(dxn * xhat, axis=0, keepdims=True)

        @pl.when(pl.program_id(0) == 0)
        def _():
            dg_ref[...] = jnp.zeros_like(dg_ref)

        dg_ref[...] += dgp
        dxh = dxn * g_ref[...]
        dx_ref[...] = dres_ref[...] + rstd * (dxh - xhat * jnp.mean(dxh * xhat, axis=-1, keepdims=True))

    return pl.pallas_call(
        body, grid=(T // tm,), name=name,
        in_specs=[_row_spec(tm, D), _const_spec((1, D)), _const_spec(w_bf.shape), _row_spec(tm, D)]
        + [_row_spec(tm, s) for s in widths] + extra_specs,
        out_specs=[_row_spec(tm, D), _const_spec((1, D))],
        out_shape=[jax.ShapeDtypeStruct((T, D), F32), jax.ShapeDtypeStruct((1, D), F32)],
        compiler_params=_cparams(("arbitrary",), VMEM_BIG),
    )(x, g, w_bf, dres, *dps, *extra)


def _after_operand(after):
    return ([ANY], [after]) if after is not None else ([], [])


def matmul_acc_chips(at_bf, pieces, name, after=None):
    k = at_bf.shape[0]
    widths = [p.shape[1] for p in pieces]
    nb = sum(widths) // NCHIP
    tm = 512
    steps = T // tm
    extra_specs, extra = _after_operand(after)

    def body(a_ref, *rest):
        o_ref, acc = rest[-2:]

        @pl.when(pl.program_id(0) == 0)
        def _():
            acc[...] = jnp.zeros_like(acc)

        a = a_ref[...]
        b = jnp.concatenate([r[...] for r in rest[:len(widths)]], axis=-1)
        for s in range(NCHIP):
            acc[s] += jnp.dot(a, b[:, s * nb:(s + 1) * nb], preferred_element_type=F32)

        @pl.when(pl.program_id(0) == steps - 1)
        def _():
            o_ref[...] = acc[...].astype(BF16)

    return pl.pallas_call(
        body, grid=(steps,), name=name,
        in_specs=[_col_spec(k, tm)] + [_row_spec(tm, w_) for w_ in widths] + extra_specs,
        out_specs=_const_spec((NCHIP, k, nb)),
        out_shape=jax.ShapeDtypeStruct((NCHIP, k, nb), BF16),
        scratch_shapes=[pltpu.VMEM((NCHIP, k, nb), F32)],
        compiler_params=_cparams(("arbitrary",), VMEM_BIG),
    )(at_bf, *pieces, *extra)


def out_proj(h, z_bf, w_bf, name):
    tm = 512

    def body(h_ref, z_ref, w_ref, o_ref):
        o_ref[...] = h_ref[...] + jnp.dot(z_ref[...], w_ref[...], preferred_element_type=F32)

    return pl.pallas_call(
        body, grid=(T // tm,), name=name,
        in_specs=[_row_spec(tm, D), _row_spec(tm, D), _const_spec((D, D))],
        out_specs=_row_spec(tm, D), out_shape=jax.ShapeDtypeStruct((T, D), F32),
        compiler_params=_cparams(("parallel",)),
    )(h, z_bf, w_bf)


def out_proj_bwd(dh, zt_bf, w_bf, name, after=None):
    tm = 512
    extra_specs, extra = _after_operand(after)

    def body(dh_ref, zt_ref, w_ref, *rest):
        dz_ref, dw_ref = rest[-2:]
        dhb = dh_ref[...].astype(BF16)
        dz_ref[...] = lax.dot_general(dhb, w_ref[...], (((1,), (1,)), ((), ())), preferred_element_type=F32)

        @pl.when(pl.program_id(0) == 0)
        def _():
            dw_ref[...] = jnp.zeros_like(dw_ref)

        dw_ref[...] += jnp.dot(zt_ref[...], dhb, preferred_element_type=F32)

    return pl.pallas_call(
        body, grid=(T // tm,), name=name,
        in_specs=[_row_spec(tm, D), _col_spec(D, tm), _const_spec((D, D))] + extra_specs,
        out_specs=[_row_spec(tm, D), _const_spec((D, D))],
        out_shape=[jax.ShapeDtypeStruct((T, D), F32), jax.ShapeDtypeStruct((D, D), F32)],
        compiler_params=_cparams(("arbitrary",)),
    )(dh, zt_bf, w_bf, *extra)


def out_proj_loss(h, z_bf, w_bf, g, target):
    tm = 512

    def body(h_ref, z_ref, w_ref, g_ref, t_ref, dh_ref, loss_ref, dg_ref):
        xv = h_ref[...] + jnp.dot(z_ref[...], w_ref[...], preferred_element_type=F32)
        rstd = lax.rsqrt(jnp.mean(xv * xv, axis=-1, keepdims=True) + RMS_EPS)
        xhat = xv * rstd
        err = xhat * g_ref[...] - t_ref[...]
        part = 0.5 * jnp.sum(jnp.mean(err * err, axis=-1, keepdims=True), axis=0, keepdims=True)
        dout = err * (1.0 / D)

        @pl.when(pl.program_id(0) == 0)
        def _():
            loss_ref[...] = jnp.zeros_like(loss_ref)
            dg_ref[...] = jnp.zeros_like(dg_ref)

        loss_ref[...] += jnp.broadcast_to(part, loss_ref.shape)
        dg_ref[...] += jnp.sum(dout * xhat, axis=0, keepdims=True)
        dxh = dout * g_ref[...]
        dh_ref[...] = rstd * (dxh - xhat * jnp.mean(dxh * xhat, axis=-1, keepdims=True))

    return pl.pallas_call(
        body, grid=(T // tm,), name="out_proj_loss",
        in_specs=[_row_spec(tm, D), _row_spec(tm, D), _const_spec((D, D)), _const_spec((1, D)), _row_spec(tm, D)],
        out_specs=[_row_spec(tm, D), _const_spec((8, 128)), _const_spec((1, D))],
        out_shape=[jax.ShapeDtypeStruct((T, D), F32), jax.ShapeDtypeStruct((8, 128), F32),
                   jax.ShapeDtypeStruct((1, D), F32)],
        compiler_params=_cparams(("arbitrary",)),
    )(h, z_bf, w_bf, g, target)


PREP_TM = 256
PREP_NB = SEQ // PREP_TM


def _prep_elem(k, wl, apre, kkw, kaw, bd):
    wraw = -_softplus(-wl) - 0.5
    lw = -jnp.exp(wraw)
    asig = jax.nn.sigmoid(apre)
    kkr = k * kkw
    nrm = jnp.maximum(jnp.sqrt(_headsum(kkr * kkr, bd)), 1e-12)
    kk = kkr / nrm
    k2 = k * (1.0 + (asig - 1.0) * kaw)
    return lw, k2, -kk, kk * asig


def _shifted(ps_ref, prev_ref, mu, blk):
    p = ps_ref[...]
    first = (blk % PREP_NB) == 0
    prev_row = jnp.where(first, 0.0, prev_ref[7:8, :])
    rolled = pltpu.roll(p, 1, 0)
    p_prev = jnp.where(_iota2(p.shape, 0) == 0, prev_row, rolled)
    return p, p_prev, p + (p_prev - p) * mu


def _prev_spec(width, blk_of):
    return pl.BlockSpec((8, width), lambda i: (jnp.maximum(blk_of(i) * (PREP_TM // 8) - 1, 0), 0))


def even_prep(ps, mu, w0, w2x, a0, a2x, kkw, kaw):
    tm = PREP_TM

    def body(ps_ref, prev_ref, mu_ref, w0_ref, w2_ref, a0_ref, a2_ref, kk_ref, ka_ref,
             r_ref, lw_ref, k2_ref, v_ref, aa_ref, bb_ref):
        _, _, s = _shifted(ps_ref, prev_ref, mu_ref[...], pl.program_id(0))
        wa = s[:, 3 * W:]
        wl = w0_ref[...] + _bdot(jnp.tanh(wa), w2_ref[...])
        apre = a0_ref[...] + _bdot(wa, a2_ref[...])
        lw, k2, aa, bb = _prep_elem(s[:, W:2 * W], wl, apre, kk_ref[...], ka_ref[...], _head_blockdiag())
        r_ref[...] = s[:, 0:W]
        v_ref[...] = s[:, 2 * W:3 * W]
        lw_ref[...] = lw
        k2_ref[...] = k2
        aa_ref[...] = aa
        bb_ref[...] = bb

    vec = _const_spec((1, W))
    return pl.pallas_call(
        body, grid=(T // tm,), name="even_prep",
        in_specs=[_row_spec(tm, SHIFT), _prev_spec(SHIFT, lambda i: i), _const_spec((1, SHIFT)), vec,
                  _const_spec((2 * LORA, W)), vec, _const_spec((2 * LORA, W)), vec, vec],
        out_specs=[_row_spec(tm, W)] * 6,
        out_shape=[jax.ShapeDtypeStruct((T, W), F32)] * 6,
        compiler_params=_cparams(("parallel",)),
    )(ps, ps, mu, w0, w2x, a0, a2x, kkw, kaw)


def even_prep_bwd(ps, mu, w0, w2x, a0, a2x, kkw, kaw, dr, dlw, dk2, dv, daa, dbb, dr2, dk22, dv2):
    tm = PREP_TM
    nb = T // tm
    rev = lambda i: nb - 1 - i

    def body(ps_ref, prev_ref, mu_ref, w0_ref, w2_ref, a0_ref, a2_ref, kk_ref, ka_ref,
             dr_ref, dlw_ref, dk2_ref, dv_ref, daa_ref, dbb_ref, dr2_ref, dk22_ref, dv2_ref,
             dps_ref, dmu_ref, dw0_ref, dw2_ref, da0_ref, da2_ref, dkk_ref, dka_ref, carry):
        i = pl.program_id(0)
        blk = rev(i)
        mu_v = mu_ref[...]
        p, p_prev, s = _shifted(ps_ref, prev_ref, mu_v, blk)
        wa = s[:, 3 * W:]
        th = jnp.tanh(wa)
        wl = w0_ref[...] + _bdot(th, w2_ref[...])
        apre = a0_ref[...] + _bdot(wa, a2_ref[...])
        bd = _head_blockdiag()
        k = s[:, W:2 * W]
        _, vjp = jax.vjp(lambda k_, wl_, ap_, kkw_, kaw_: _prep_elem(k_, wl_, ap_, kkw_, kaw_, bd),
                         k, wl, apre, kk_ref[...], ka_ref[...])
        dk, dwl, dap, dkkw, dkaw = vjp((dlw_ref[...], dk2_ref[...] + dk22_ref[...], daa_ref[...], dbb_ref[...]))
        dwa = _bdot_nt(dwl, w2_ref[...]) * (1.0 - th * th) + _bdot_nt(dap, a2_ref[...])
        ds = jnp.concatenate([dr_ref[...] + dr2_ref[...], dk, dv_ref[...] + dv2_ref[...], dwa], axis=-1)

        @pl.when(i == 0)
        def _():
            for ref in (dmu_ref, dw0_ref, dw2_ref, da0_ref, da2_ref, dkk_ref, dka_ref, carry):
                ref[...] = jnp.zeros_like(ref)

        dmu_ref[...] += jnp.sum(ds * (p_prev - p), axis=0, keepdims=True)
        dw0_ref[...] += jnp.sum(dwl, axis=0, keepdims=True)
        da0_ref[...] += jnp.sum(dap, axis=0, keepdims=True)
        dw2_ref[...] += _bdot_tn(th, dwl)
        da2_ref[...] += _bdot_tn(wa, dap)
        dkk_ref[...] += dkkw
        dka_ref[...] += dkaw
        dsm = ds * mu_v
        last = (blk % PREP_NB) == PREP_NB - 1
        nxt = jnp.where(last, 0.0, carry[0:1, :])
        up = pltpu.roll(dsm, tm - 1, 0)
        up = jnp.where(_iota2(up.shape, 0) == tm - 1, nxt, up)
        dps_ref[...] = (ds - dsm + up).astype(BF16)
        carry[0:1, :] = dsm[0:1, :]

    vec = _const_spec((1, W))
    rrow = lambda width: pl.BlockSpec((tm, width), lambda i: (rev(i), 0))
    return pl.pallas_call(
        body, grid=(nb,), name="even_prep_bwd",
        in_specs=[rrow(SHIFT), _prev_spec(SHIFT, rev), _const_spec((1, SHIFT)), vec,
                  _const_spec((2 * LORA, W)), vec, _const_spec((2 * LORA, W)), vec, vec] + [rrow(W)] * 9,
        out_specs=[rrow(SHIFT), _const_spec((1, SHIFT)), vec, _const_spec((2 * LORA, W)), vec,
                   _const_spec((2 * LORA, W)), vec, vec],
        out_shape=[jax.ShapeDtypeStruct((T, SHIFT), BF16), jax.ShapeDtypeStruct((1, SHIFT), F32),
                   jax.ShapeDtypeStruct((1, W), F32), jax.ShapeDtypeStruct((2 * LORA, W), F32),
                   jax.ShapeDtypeStruct((1, W), F32), jax.ShapeDtypeStruct((2 * LORA, W), F32),
                   jax.ShapeDtypeStruct((1, W), F32), jax.ShapeDtypeStruct((1, W), F32)],
        scratch_shapes=[pltpu.VMEM((8, SHIFT), F32)],
        compiler_params=_cparams(("arbitrary",), VMEM_BIG),
    )(ps, ps, mu, w0, w2x, a0, a2x, kkw, kaw, dr, dlw, dk2, dv, daa, dbb, dr2, dk22, dv2)


NPAIR = NH // 2
PW = 2 * HD


def _pair_cols(p):
    return slice(p * PW, (p + 1) * PW)


def _pairs(a):
    return [a[:, _pair_cols(p)] for p in range(NPAIR)]


def _stack_pair(a):
    first = _iota2(a.shape, 1) < HD
    zero = jnp.zeros_like(a)
    return jnp.concatenate([jnp.where(first, a, zero), jnp.where(first, zero, a)], axis=0)


def _unstack_pair(a):
    n = a.shape[0] // 2
    return jnp.where(_iota2((n, PW), 1) < HD, a[:n], a[n:])


def _fold_pair(a):
    n = a.shape[0] // 2
    return a[:n] + a[n:]


def _chunk_masks():
    n = 4 * L
    row = _iota2((n, n), 0)
    col = _iota2((n, n), 1)
    same = ((row // L) & 1) == ((col // L) & 1)
    ri = row & (L - 1)
    ci = col & (L - 1)
    keep = same & (((row < 2 * L) & (ri > ci)) | ((row >= 2 * L) & (ri >= ci)))
    r1 = _iota2((L, L), 0)
    c1 = _iota2((L, L), 1)
    r2 = _iota2((2 * L, 2 * L), 0)
    c2 = _iota2((2 * L, 2 * L), 1)
    return keep.astype(F32), (r1 >= c1).astype(F32), (r2 == c2).astype(F32)


def _scaled(r, lw, k2, aa, bb, tri):
    g = _hdot(tri, lw)
    eg = jnp.exp(g)
    eng = jnp.exp(-g)
    egp = jnp.exp(g - lw)
    return eg, eng, egp, aa * egp, r * eg, bb * eng, k2 * eng


def _head_cols(h):
    return slice(h * HD, (h + 1) * HD)


def _per_head(a):
    return [a[:, _head_cols(h)] for h in range(NH)]


def _pairs_operands(at, rt, bt, kt):
    x = [jnp.concatenate([_stack_pair(a), _stack_pair(r)], axis=0).astype(BF16) for a, r in zip(_pairs(at), _pairs(rt))]
    yk = [jnp.concatenate([_stack_pair(b), _stack_pair(k)], axis=0).astype(BF16) for b, k in zip(_pairs(bt), _pairs(kt))]
    return x, yk


def _pairs_matrices(x, yk, keep, eye):
    m = [_bdot_nt(a, b) * keep for a, b in zip(x, yk)]
    p = [a[:2 * L, :2 * L] for a in m]
    tinv = [eye + a for a in p]
    for _ in range(5):
        p = [_bdot(a, a) for a in p]
        tinv = [t + _bdot(t, a) for t, a in zip(tinv, p)]
    return [a.astype(BF16) for a in m], [a.astype(BF16) for a in tinv]


def _pairs_fwd(x, yk, m, tinv, vw, s0, egl):
    xh = [_bdot_nt(a, s) for a, s in zip(x, s0)]
    u = [_bdot(t, h[:2 * L] + _bdot(a[:2 * L, 2 * L:], w)) for t, h, a, w in zip(tinv, xh, m, vw)]
    uv = [jnp.concatenate([a, w], axis=0).astype(BF16) for a, w in zip(u, vw)]
    y = [h[2 * L:] + _bdot(a[2 * L:], w) for h, a, w in zip(xh, m, uv)]
    sn = [e * (s + _bdot_tn(w, b)) for e, s, w, b in zip(egl, s0, uv, yk)]
    return y, sn, uv


def _pairs_bwd(x, yk, m, tinv, uv, s0, sn, egl, dyw, dsn, keep):
    dzs = [d * e for d, e in zip(dsn, egl)]
    dgl = [jnp.sum(d * s, axis=0, keepdims=True) for d, s in zip(dsn, sn)]
    dyb = [a.astype(BF16) for a in dyw]
    t1 = [_bdot_tn(a[2 * L:], d) for a, d in zip(m, dyb)]
    t2 = [_bdot_nt(b, d) for b, d in zip(yk, dzs)]
    drhs = [_bdot_tn(t, a[:2 * L] + b[:2 * L]) for t, a, b in zip(tinv, t1, t2)]
    dv = [a[2 * L:] + b[2 * L:] + _bdot_tn(c[:2 * L, 2 * L:], d) for a, b, c, d in zip(t1, t2, m, drhs)]
    gg = [jnp.concatenate([a, b], axis=0).astype(BF16) for a, b in zip(drhs, dyw)]
    ds0 = [d + _bdot_tn(g, a) for d, g, a in zip(dzs, gg, x)]
    dm = [_bdot_nt(g, w) * keep for g, w in zip(gg, uv)]
    dx = [_bdot(g, s) + _bdot(d, b) for g, s, d, b in zip(gg, s0, dm, yk)]
    dyk = [_bdot_tn(d, a) + _bdot(w, z) for d, a, w, z in zip(dm, x, uv, dzs)]
    return dx, dyk, dv, dgl, ds0


STATE_SHAPE = (NPAIR * PW, PW)
M_SHAPE = (4 * L, NPAIR * 4 * L)
TINV_SHAPE = (2 * L, NPAIR * 2 * L)


def _rows_of(a, n):
    return [a[i * n:(i + 1) * n, :] for i in range(NPAIR)]


def _both(f):
    out = []
    for s in range(NSEQ):
        out += f(s)
    return out


def _seq_view(a):
    return a.reshape(NSEQ, SEQ, a.shape[-1])


UV_SHAPE = (4 * L, NPAIR * PW)


def rwkv_fwd(r, lw, k2, v, aa, bb):
    def body(r_ref, lw_ref, k2_ref, v_ref, aa_ref, bb_ref, y_ref, hs_ref, hn_ref, m_ref, t_ref, uv_ref, state):
        @pl.when(pl.program_id(0) == 0)
        def _():
            state[...] = jnp.zeros_like(state)

        s_all = state[...]
        hs_ref[0] = s_all
        keep, tri, eye = _chunk_masks()
        sc = [_scaled(r_ref[s], lw_ref[s], k2_ref[s], aa_ref[s], bb_ref[s], tri) for s in range(NSEQ)]
        ops = [_pairs_operands(*sc[s][3:]) for s in range(NSEQ)]
        x, yk = _both(lambda s: ops[s][0]), _both(lambda s: ops[s][1])
        m, tinv = _pairs_matrices(x, yk, keep, eye)
        vw = _both(lambda s: [_stack_pair(a) for a in _pairs(v_ref[s])])
        s0 = _both(lambda s: _rows_of(s_all[s], PW))
        y, sn, uv = _pairs_fwd(x, yk, m, tinv, vw, s0, _both(lambda s: _pairs(sc[s][0][L - 1:L, :])))
        for s in range(NSEQ):
            mine = slice(s * NPAIR, (s + 1) * NPAIR)
            y_ref[s] = jnp.concatenate([_fold_pair(a) for a in y[mine]], axis=-1)
            m_ref[0, s] = jnp.concatenate(m[mine], axis=-1)
            t_ref[0, s] = jnp.concatenate(tinv[mine], axis=-1)
            uv_ref[0, s] = jnp.concatenate(uv[mine], axis=-1)
            s_new = jnp.concatenate(sn[mine], axis=0)
            hn_ref[0, s] = s_new
            state[s] = s_new

    blk = pl.BlockSpec((NSEQ, L, W), lambda c: (0, c, 0))
    per_chunk = lambda shape: pl.BlockSpec((1, NSEQ) + shape, lambda c: (c, 0, 0, 0))
    saved_shapes = [(STATE_SHAPE, F32), (STATE_SHAPE, F32), (M_SHAPE, BF16), (TINV_SHAPE, BF16), (UV_SHAPE, BF16)]
    y, *saved = pl.pallas_call(
        body, grid=(NC,), name="rwkv_fwd",
        in_specs=[blk] * 6,
        out_specs=[blk] + [per_chunk(shape) for shape, _ in saved_shapes],
        out_shape=[jax.ShapeDtypeStruct((NSEQ, SEQ, W), F32)]
        + [jax.ShapeDtypeStruct((NC, NSEQ) + shape, dt) for shape, dt in saved_shapes],
        scratch_shapes=[pltpu.VMEM((NSEQ,) + STATE_SHAPE, F32)],
        compiler_params=_cparams(("arbitrary",)),
    )(*[_seq_view(a) for a in (r, lw, k2, v, aa, bb)])
    return y.reshape(T, W), saved


def rwkv_bwd(r, lw, k2, aa, bb, saved, dy):
    def body(r_ref, lw_ref, k2_ref, aa_ref, bb_ref, hs_ref, hn_ref, m_ref, t_ref, uv_ref, dy_ref,
             dr_ref, dlw_ref, dk2_ref, dv_ref, daa_ref, dbb_ref, dstate):
        @pl.when(pl.program_id(0) == 0)
        def _():
            dstate[...] = jnp.zeros_like(dstate)

        keep, tri, _ = _chunk_masks()
        sc = [_scaled(r_ref[s], lw_ref[s], k2_ref[s], aa_ref[s], bb_ref[s], tri) for s in range(NSEQ)]
        ops = [_pairs_operands(*sc[s][3:]) for s in range(NSEQ)]
        x, yk = _both(lambda s: ops[s][0]), _both(lambda s: ops[s][1])
        m = _both(lambda s: [m_ref[0, s][:, i * 4 * L:(i + 1) * 4 * L] for i in range(NPAIR)])
        tinv = _both(lambda s: [t_ref[0, s][:, i * 2 * L:(i + 1) * 2 * L] for i in range(NPAIR)])
        uv = _both(lambda s: _pairs(uv_ref[0, s]))
        dyw = _both(lambda s: [_stack_pair(a) for a in _pairs(dy_ref[s])])
        s0 = _both(lambda s: _rows_of(hs_ref[0, s], PW))
        sn = _both(lambda s: _rows_of(hn_ref[0, s], PW))
        dsn = _both(lambda s: _rows_of(dstate[s], PW))
        egl = _both(lambda s: _pairs(sc[s][0][L - 1:L, :]))
        dx, dyk, dvw, dgl, ds0 = _pairs_bwd(x, yk, m, tinv, uv, s0, sn, egl, dyw, dsn, keep)
        for s in range(NSEQ):
            mine = slice(s * NPAIR, (s + 1) * NPAIR)
            eg, eng, egp, at, rt, bt, kt = sc[s]
            dstate[s] = jnp.concatenate(ds0[mine], axis=0)
            dv_ref[s] = jnp.concatenate([_fold_pair(a) for a in dvw[mine]], axis=-1)
            dat = jnp.concatenate([_fold_pair(a[:2 * L]) for a in dx[mine]], axis=-1)
            drt = jnp.concatenate([_fold_pair(a[2 * L:]) for a in dx[mine]], axis=-1)
            dbt = jnp.concatenate([_fold_pair(a[:2 * L]) for a in dyk[mine]], axis=-1)
            dkt = jnp.concatenate([_fold_pair(a[2 * L:]) for a in dyk[mine]], axis=-1)
            dg = drt * rt - dbt * bt - dkt * kt
            dg = dg + jnp.where(_iota2(dg.shape, 0) == L - 1, jnp.concatenate(dgl[mine], axis=-1), 0.0)
            dgp = dat * at
            dlw_ref[s] = _hdot_tn(tri, dg + dgp) - dgp
            dr_ref[s] = drt * eg
            daa_ref[s] = dat * egp
            dbb_ref[s] = dbt * eng
            dk2_ref[s] = dkt * eng

    blk = pl.BlockSpec((NSEQ, L, W), lambda c: (0, NC - 1 - c, 0))
    per_chunk = lambda shape: pl.BlockSpec((1, NSEQ) + shape, lambda c: (NC - 1 - c, 0, 0, 0))
    outs = pl.pallas_call(
        body, grid=(NC,), name="rwkv_bwd",
        in_specs=[blk] * 5 + [per_chunk(a.shape[2:]) for a in saved] + [blk],
        out_specs=[blk] * 6,
        out_shape=[jax.ShapeDtypeStruct((NSEQ, SEQ, W), F32)] * 6,
        scratch_shapes=[pltpu.VMEM((NSEQ,) + STATE_SHAPE, F32)],
        compiler_params=_cparams(("arbitrary",)),
    )(*[_seq_view(a) for a in (r, lw, k2, aa, bb)], *saved, _seq_view(dy))
    return [a.reshape(T, W) for a in outs]


def _post_math(y, r, k2, v, ga, o, gb, lng, lnb, rk, bd):
    mu = _headsum(y, bd) * (1.0 / HD)
    yc = y - mu
    var = _headsum(yc * yc, bd) * (1.0 / HD)
    yn = yc * lax.rsqrt(var + GN_EPS) * lng + lnb
    bonus = _headsum(r * k2 * rk, bd) * v
    return (yn + bonus) * _silu(ga), o * _silu(gb)


def even_post(y, r, k2, v, ga, o, gb, lng, lnb, rk):
    tm = 256

    def body(y_ref, r_ref, k2_ref, v_ref, ga_ref, o_ref, gb_ref, lng_ref, lnb_ref, rk_ref, z_ref, zt_ref):
        ya, yb = _post_math(y_ref[...], r_ref[...], k2_ref[...], v_ref[...], ga_ref[...], o_ref[...], gb_ref[...],
                            lng_ref[...], lnb_ref[...], rk_ref[...], _head_blockdiag())
        ya, yb = ya.astype(BF16), yb.astype(BF16)
        z_ref[:, 0:W] = ya
        z_ref[:, W:2 * W] = yb
        zt_ref[0:W, :] = ya.T
        zt_ref[W:2 * W, :] = yb.T

    vec = _const_spec((1, W))
    return pl.pallas_call(
        body, grid=(T // tm,), name="even_post",
        in_specs=[_row_spec(tm, W)] * 7 + [vec] * 3,
        out_specs=[_row_spec(tm, D), _col_spec(D, tm)],
        out_shape=[jax.ShapeDtypeStruct((T, D), BF16), jax.ShapeDtypeStruct((D, T), BF16)],
        compiler_params=_cparams(("parallel",)),
    )(y, r, k2, v, ga, o, gb, lng, lnb, rk)


def even_post_bwd(y, r, k2, v, ga, o, gb, lng, lnb, rk, dz):
    tm = 256

    def body(y_ref, r_ref, k2_ref, v_ref, ga_ref, o_ref, gb_ref, lng_ref, lnb_ref, rk_ref, dz_ref,
             dy_ref, dr_ref, dk2_ref, dv_ref, dga_ref, do_ref, dgb_ref, dlng_ref, dlnb_ref, drk_ref):
        bd = _head_blockdiag()
        _, vjp = jax.vjp(lambda *a: _post_math(*a, bd), y_ref[...], r_ref[...], k2_ref[...], v_ref[...], ga_ref[...],
                         o_ref[...], gb_ref[...], lng_ref[...], lnb_ref[...], rk_ref[...])
        dzv = dz_ref[...]
        dy, dr, dk2, dv, dga, do, dgb, dlng, dlnb, drk = vjp((dzv[:, 0:W], dzv[:, W:2 * W]))
        for ref, val in ((dy_ref, dy), (dr_ref, dr), (dk2_ref, dk2), (dv_ref, dv), (dga_ref, dga), (do_ref, do),
                         (dgb_ref, dgb)):
            ref[...] = val.astype(ref.dtype)

        @pl.when(pl.program_id(0) == 0)
        def _():
            for ref in (dlng_ref, dlnb_ref, drk_ref):
                ref[...] = jnp.zeros_like(ref)

        dlng_ref[...] += dlng
        dlnb_ref[...] += dlnb
        drk_ref[...] += drk

    vec = _const_spec((1, W))
    return pl.pallas_call(
        body, grid=(T // tm,), name="even_post_bwd",
        in_specs=[_row_spec(tm, W)] * 7 + [vec] * 3 + [_row_spec(tm, D)],
        out_specs=[_row_spec(tm, W)] * 7 + [vec] * 3,
        out_shape=[jax.ShapeDtypeStruct((T, W), dt) for dt in (F32, F32, F32, F32, BF16, F32, BF16)]
        + [jax.ShapeDtypeStruct((1, W), F32)] * 3,
        compiler_params=_cparams(("arbitrary",)),
    )(y, r, k2, v, ga, o, gb, lng, lnb, rk, dz)


PADSEQ = SEQ + LEFT * L
ATT_SCALE = 1.0 / math.sqrt(HD)


def _att_probs(q2, kw, bias, c):
    valid = _iota2((1, BAND), 1) >= (LEFT - c) * L
    s = [jnp.where(valid, _bdot_nt(a, b) * ATT_SCALE + bias[p], NEG) for p, (a, b) in enumerate(zip(q2, kw))]
    e = [jnp.exp(a - jnp.max(a, axis=-1, keepdims=True)) for a in s]
    return [a / jnp.sum(a, axis=-1, keepdims=True) for a in e]


def attention_fwd(q, kpad, vpad, bias):
    def body(q_ref, k_ref, v_ref, b_ref, o_ref):
        c = pl.program_id(1)
        start = pl.multiple_of(c * L, L)
        kw = _pairs(k_ref[pl.ds(start, BAND), :])
        vw = _pairs(v_ref[pl.ds(start, BAND), :])
        q2 = [_stack_pair(a) for a in _pairs(q_ref[...].astype(BF16))]
        p = _att_probs(q2, kw, b_ref[...], c)
        o_ref[...] = jnp.concatenate([_unstack_pair(_bdot(a, b)) for a, b in zip(p, vw)], axis=-1)

    qblk = pl.BlockSpec((L, W), lambda b, c: (b * NC + c, 0))
    kblk = pl.BlockSpec((PADSEQ, W), lambda b, c: (b, 0))
    return pl.pallas_call(
        body, grid=(NSEQ, NC), name="attention_fwd",
        in_specs=[qblk, kblk, kblk, _const_spec((NPAIR, 2 * L, BAND))],
        out_specs=qblk, out_shape=jax.ShapeDtypeStruct((T, W), F32),
        compiler_params=_cparams(("parallel", "arbitrary")),
    )(q, kpad, vpad, bias)


def attention_bwd(q, kpad, vpad, bias, do):
    def body(q_ref, k_ref, v_ref, b_ref, do_ref, dq_ref, dko_ref, dvo_ref, db_ref, dk_ref, dv_ref):
        b = pl.program_id(0)
        c = pl.program_id(1)

        @pl.when(c == 0)
        def _():
            dk_ref[...] = jnp.zeros_like(dk_ref)
            dv_ref[...] = jnp.zeros_like(dv_ref)

        @pl.when((c == 0) & (b == 0))
        def _():
            db_ref[...] = jnp.zeros_like(db_ref)

        start = pl.multiple_of(c * L, L)
        kw = _pairs(k_ref[pl.ds(start, BAND), :])
        vw = _pairs(v_ref[pl.ds(start, BAND), :])
        q2 = [_stack_pair(a) for a in _pairs(q_ref[...].astype(BF16))]
        do2 = [_stack_pair(a) for a in _pairs(do_ref[...].astype(BF16))]
        p = _att_probs(q2, kw, b_ref[...], c)
        dp = [_bdot_nt(a, b) for a, b in zip(do2, vw)]
        ds = [a * (d - jnp.sum(d * a, axis=-1, keepdims=True)) for a, d in zip(p, dp)]
        dss = [(a * ATT_SCALE).astype(BF16) for a in ds]
        dq_ref[...] = jnp.concatenate([_unstack_pair(_bdot(a, b)) for a, b in zip(dss, kw)], axis=-1).astype(BF16)
        dk_ref[pl.ds(start, BAND), :] += jnp.concatenate([_bdot_tn(a, b) for a, b in zip(dss, q2)], axis=-1)
        dv_ref[pl.ds(start, BAND), :] += jnp.concatenate([_bdot_tn(a, b) for a, b in zip(p, do2)], axis=-1)
        for i in range(NPAIR):
            db_ref[i] += ds[i]

        @pl.when(c == NC - 1)
        def _():
            dko_ref[...] = dk_ref[LEFT * L:, :].astype(BF16)
            dvo_ref[...] = dv_ref[LEFT * L:, :].astype(BF16)

    qblk = pl.BlockSpec((L, W), lambda b, c: (b * NC + c, 0))
    kblk = pl.BlockSpec((PADSEQ, W), lambda b, c: (b, 0))
    sblk = pl.BlockSpec((SEQ, W), lambda b, c: (b, 0))
    bblk = _const_spec((NPAIR, 2 * L, BAND))
    return pl.pallas_call(
        body, grid=(NSEQ, NC), name="attention_bwd",
        in_specs=[qblk, kblk, kblk, bblk, qblk],
        out_specs=[qblk, sblk, sblk, bblk],
        out_shape=[jax.ShapeDtypeStruct((T, W), BF16), jax.ShapeDtypeStruct((T, W), BF16),
                   jax.ShapeDtypeStruct((T, W), BF16), jax.ShapeDtypeStruct((NPAIR, 2 * L, BAND), F32)],
        scratch_shapes=[pltpu.VMEM((PADSEQ, W), F32), pltpu.VMEM((PADSEQ, W), F32)],
        compiler_params=_cparams(("arbitrary", "arbitrary"), VMEM_BIG),
    )(q, kpad, vpad, bias, do)


NTAB = 2 * CLIP + 1
EXT = BAND + L


def _ext_onehot():
    n = _iota2((EXT, NTAB), 0)
    m = _iota2((EXT, NTAB), 1)
    return (jnp.clip(BAND - 1 - n, -CLIP, CLIP) + CLIP == m).astype(F32)


def bias_expand(table):
    def body(t_ref, o_ref):
        ext = _hdot_nt(t_ref[...], _ext_onehot())
        for i in range(L):
            s = L - 1 - i
            o_ref[:, i, :] = (pltpu.roll(ext, EXT - s, 1) if s else ext)[:, :BAND]

    return pl.pallas_call(body, name="bias_expand", out_shape=jax.ShapeDtypeStruct((NH, L, BAND), F32))(table)


def bias_grad(dbias):
    def body(d_ref, o_ref):
        acc = jnp.zeros((NH, EXT), F32)
        zpad = jnp.zeros((NH, EXT - BAND), F32)
        for i in range(L):
            s = L - 1 - i
            row = jnp.concatenate([d_ref[:, i, :], zpad], axis=-1)
            acc = acc + (pltpu.roll(row, s, 1) if s else row)
        o_ref[...] = _hdot(acc, _ext_onehot())

    return pl.pallas_call(body, name="bias_grad", out_shape=jax.ShapeDtypeStruct((NH, NTAB), F32))(dbias)


def _group_cols(g):
    return slice(g * SGC, (g + 1) * SGC)


def _sg_norm(gv, lng, lnb):
    gc = gv - jnp.mean(gv, axis=-1, keepdims=True)
    rstd = lax.rsqrt(jnp.mean(gc * gc, axis=-1, keepdims=True) + LN_EPS)
    xhat = gc * rstd
    return xhat, rstd, xhat * lng + lnb


def gmlp_fwd(u, v, gate, lng, lnb, wm_bf, sgb_t):
    def body(u_ref, v_ref, gt_ref, lng_ref, lnb_ref, wm_ref, sb_ref, z_ref, zt_ref):
        _, _, vln = _sg_norm(_gelu(v_ref[...]), lng_ref[...], lnb_ref[...])
        vlb = vln.astype(BF16)
        for g in range(NG):
            cs = _group_cols(g)
            sv = jnp.dot(wm_ref[g], vlb[:, cs], preferred_element_type=F32) + sb_ref[:, g:g + 1]
            zg = (_gelu(u_ref[:, cs]) * sv * _silu(gt_ref[:, cs])).astype(BF16)
            z_ref[:, cs] = zg
            zt_ref[cs, :] = zg.T

    return pl.pallas_call(
        body, grid=(T // SGC,), name="gmlp_fwd",
        in_specs=[_row_spec(SGC, D)] * 3 + [_const_spec((1, D))] * 2 + [_const_spec((NG, SGC, SGC)),
                                                                      _const_spec((SGC, NG))],
        out_specs=[_row_spec(SGC, D), _col_spec(D, SGC)],
        out_shape=[jax.ShapeDtypeStruct((T, D), BF16), jax.ShapeDtypeStruct((D, T), BF16)],
        compiler_params=_cparams(("parallel",)),
    )(u, v, gate, lng, lnb, wm_bf, sgb_t)


def gmlp_bwd(u, v, gate, lng, lnb, wm_bf, sgb_t, dz):
    def body(u_ref, v_ref, gt_ref, lng_ref, lnb_ref, wm_ref, sb_ref, dz_ref,
             du_ref, dv_ref, dgt_ref, dlng_ref, dlnb_ref, dwm_ref, dsb_ref):
        @pl.when(pl.program_id(0) == 0)
        def _():
            for ref in (dlng_ref, dlnb_ref, dwm_ref, dsb_ref):
                ref[...] = jnp.zeros_like(ref)

        gv, dgv_dv = _gelu_both(v_ref[...])
        xhat, rstd, vln = _sg_norm(gv, lng_ref[...], lnb_ref[...])
        vlb = vln.astype(BF16)
        dvln = []
        dsv_all = []
        for g in range(NG):
            cs = _group_cols(g)
            uu = u_ref[:, cs]
            gg = gt_ref[:, cs]
            dzz = dz_ref[:, cs]
            sv = jnp.dot(wm_ref[g], vlb[:, cs], preferred_element_type=F32) + sb_ref[:, g:g + 1]
            gu, dgu = _gelu_both(uu)
            sg, dsg = _silu_both(gg)
            dzgu = dzz * gu
            dsv = dzgu * sg
            dgt_ref[:, cs] = (dzgu * sv * dsg).astype(BF16)
            du_ref[:, cs] = (dzz * sv * sg * dgu).astype(BF16)
            dsb16 = dsv.astype(BF16)
            dvln.append(lax.dot_general(wm_ref[g], dsb16, (((0,), (0,)), ((), ())), preferred_element_type=F32))
            dwm_ref[g] += lax.dot_general(dsb16, vlb[:, cs], (((1,), (1,)), ((), ())), preferred_element_type=F32)
            dsv_all.append(dsv)
        dvl = jnp.concatenate(dvln, axis=-1)
        dsv_cat = jnp.concatenate(dsv_all, axis=-1)
        sel = (_iota2((D, NG), 0) // SGC == _iota2((D, NG), 1)).astype(F32)
        dsb_ref[...] += _hdot(dsv_cat, sel)
        dlng_ref[...] += jnp.sum(dvl * xhat, axis=0, keepdims=True)
        dlnb_ref[...] += jnp.sum(dvl, axis=0, keepdims=True)
        dxh = dvl * lng_ref[...]
        dgv = rstd * (dxh - jnp.mean(dxh, axis=-1, keepdims=True)
                      - xhat * jnp.mean(dxh * xhat, axis=-1, keepdims=True))
        dv_ref[...] = (dgv * dgv_dv).astype(BF16)

    return pl.pallas_call(
        body, grid=(T // SGC,), name="gmlp_bwd",
        in_specs=[_row_spec(SGC, D)] * 3 + [_const_spec((1, D))] * 2
        + [_const_spec((NG, SGC, SGC)), _const_spec((SGC, NG)), _row_spec(SGC, D)],
        out_specs=[_row_spec(SGC, D)] * 3 + [_const_spec((1, D))] * 2 + [_const_spec((NG, SGC, SGC)),
                                                                       _const_spec((SGC, NG))],
        out_shape=[jax.ShapeDtypeStruct((T, D), BF16)] * 3 + [jax.ShapeDtypeStruct((1, D), F32)] * 2
        + [jax.ShapeDtypeStruct((NG, SGC, SGC), F32), jax.ShapeDtypeStruct((SGC, NG), F32)],
        compiler_params=_cparams(("arbitrary",)),
    )(u, v, gate, lng, lnb, wm_bf, sgb_t, dz)


NCHIP = 4
NDEV = 8
ANY = pl.BlockSpec(memory_space=pl.ANY)


HBM = pl.BlockSpec(memory_space=pltpu.HBM)
SEM = pl.BlockSpec(memory_space=pltpu.SEMAPHORE)
EFFECT = pltpu.SideEffectType.DATAFLOW_SIDE_EFFECTING


def _peers(whole_mesh):
    x, y, c = lax.axis_index("x"), lax.axis_index("y"), lax.axis_index("c")
    if not whole_mesh:
        return [((px, py, c), 2 * px + py) for px, py in ((1 - x, y), (x, 1 - y), (1 - x, 1 - y))], 2 * x + y
    out = []
    for j in range(1, NDEV):
        px, py, pc = x ^ (j >> 2), y ^ ((j >> 1) & 1), c ^ (j & 1)
        out.append(((px, py, pc), 4 * px + 2 * py + pc))
    return out, 4 * x + 2 * y + c


def _send_copies(src, land, send, recv, scatter, whole_mesh, starting):
    peers, me = _peers(whole_mesh)
    copies = []
    for t in range(len(src)):
        for j, (dev, slot) in enumerate(peers):
            k = t * len(peers) + j
            copies.append(pltpu.make_async_remote_copy(
                src_ref=src[t].at[slot] if scatter else src[t], dst_ref=land[t].at[me if starting else slot],
                send_sem=send.at[k], recv_sem=recv.at[k], device_id=dev, device_id_type=MESH))
    return copies


def send_start(srcs, lands, scatter, whole_mesh, name):
    n = len(srcs)
    nsem = n * (NDEV - 1 if whole_mesh else NCHIP - 1)

    def body(*refs):
        for cp in _send_copies(refs[:n], refs[n:2 * n], refs[2 * n], refs[2 * n + 1], scatter, whole_mesh, True):
            cp.start()
        refs[-1][...] = jnp.zeros_like(refs[-1])

    arrs = list(srcs) + list(lands)
    out = pl.pallas_call(
        body, name=name,
        out_shape=(pltpu.SemaphoreType.DMA((nsem,)), pltpu.SemaphoreType.DMA((nsem,)),
                   *[pltpu.HBM(a.shape, a.dtype) for a in arrs], jax.ShapeDtypeStruct((8, 128), F32)),
        in_specs=[HBM] * (2 * n), out_specs=(SEM, SEM, *[HBM] * (2 * n), pl.BlockSpec(memory_space=pltpu.VMEM)),
        input_output_aliases={i: 2 + i for i in range(2 * n)},
        compiler_params=pltpu.CompilerParams(has_side_effects=EFFECT),
    )(*[pltpu.with_memory_space_constraint(a, pltpu.HBM) for a in arrs])
    return out[0], out[1], list(out[2:2 + n]), list(out[2 + n:2 + 2 * n]), out[-1]


def send_wait(started, after, scatter, whole_mesh, name):
    send, recv, srcs, lands, _ = started
    n = len(srcs)

    def body(*refs):
        for cp in _send_copies(refs[:n], refs[n:2 * n], refs[2 * n], refs[2 * n + 1], scatter, whole_mesh, False):
            cp.wait_send()
            cp.wait_recv()

    arrs = list(srcs) + list(lands)
    out = pl.pallas_call(
        body, name=name, out_shape=tuple(pltpu.HBM(a.shape, a.dtype) for a in arrs),
        in_specs=[HBM] * (2 * n) + [SEM, SEM, ANY], out_specs=tuple([HBM] * (2 * n)),
        input_output_aliases={i: i for i in range(2 * n)},
        compiler_params=pltpu.CompilerParams(has_side_effects=EFFECT),
    )(*arrs, send, recv, after)
    return list(out[n:])


def exchange_c(arrs, name):
    n = len(arrs)

    def body(*refs):
        ins, outs = refs[:n], refs[n:2 * n]
        send, recv = refs[2 * n:]
        sibling = (lax.axis_index("x"), lax.axis_index("y"), 1 - lax.axis_index("c"))
        copies = [pltpu.make_async_remote_copy(src_ref=ins[t], dst_ref=outs[t], send_sem=send.at[t], recv_sem=recv.at[t],
                                               device_id=sibling, device_id_type=MESH) for t in range(n)]
        for cp in copies:
            cp.start()
        for cp in copies:
            cp.wait()

    return pl.pallas_call(
        body, name=name, in_specs=[ANY] * n, out_specs=[ANY] * n,
        out_shape=[jax.ShapeDtypeStruct(a.shape, a.dtype) for a in arrs],
        scratch_shapes=[pltpu.SemaphoreType.DMA((n,)), pltpu.SemaphoreType.DMA((n,))],
    )(*arrs)


def swap_row_halves(a, name):
    n, rows, cols = a.shape
    half = rows // 2

    def body(in_ref, out_ref, send, recv):
        x, y, c = lax.axis_index("x"), lax.axis_index("y"), lax.axis_index("c")
        cp = pltpu.make_async_remote_copy(src_ref=in_ref.at[:, pl.ds((1 - c) * half, half), :], dst_ref=out_ref,
                                          send_sem=send, recv_sem=recv, device_id=(x, y, 1 - c), device_id_type=MESH)
        cp.start()
        cp.wait()

    return pl.pallas_call(
        body, name=name, in_specs=[ANY], out_specs=ANY, out_shape=jax.ShapeDtypeStruct((n, half, cols), a.dtype),
        scratch_shapes=[pltpu.SemaphoreType.DMA, pltpu.SemaphoreType.DMA],
    )(a)


def add_blocks(a, b, name):
    n, rows, cols = a.shape
    tr = _rows_tile(rows)

    def body(a_ref, b_ref, o_ref):
        o_ref[...] = (a_ref[...].astype(F32) + b_ref[...].astype(F32)).astype(BF16)

    spec = pl.BlockSpec((1, tr, cols), lambda s, i: (s, i, 0))
    return pl.pallas_call(
        body, grid=(n, rows // tr), name=name, in_specs=[spec, spec], out_specs=spec,
        out_shape=jax.ShapeDtypeStruct(a.shape, BF16), compiler_params=_cparams(("parallel", "parallel")),
    )(a, b)


def gather_weights(arrs, split):
    n = len(arrs)

    def body(*refs):
        ins, outs = refs[:n], refs[n:2 * n]
        send1, recv1, send2, recv2, loc = refs[2 * n:]
        x, y, c = lax.axis_index("x"), lax.axis_index("y"), lax.axis_index("c")
        me = 2 * x + y
        sibling = (x, y, 1 - c)
        peers = [(1 - x, y), (x, 1 - y), (1 - x, 1 - y)]

        def rows_of(t, core):
            half = arrs[t].shape[0] // 2
            return pl.ds(core * half, half)

        def part(ref, t, core):
            return ref.at[rows_of(t, core)] if split[t] else ref

        local = [pltpu.make_async_copy(ins[t], outs[t].at[me], loc.at[t]) for t in range(n)]
        for cp in local:
            cp.start()
        first = []
        for t in range(n):
            for j, (px, py) in enumerate(peers):
                first.append(pltpu.make_async_remote_copy(
                    src_ref=part(ins[t], t, c), dst_ref=part(outs[t].at[me], t, c), send_sem=send1.at[t, j],
                    recv_sem=recv1.at[t, j], device_id=(px, py, c), device_id_type=MESH))
        for cp in first:
            cp.start()
        passed = []
        for t in range(n):
            for j, (px, py) in enumerate(peers):
                landed = part(outs[t].at[2 * px + py], t, c)
                pltpu.make_async_remote_copy(
                    src_ref=landed, dst_ref=landed, send_sem=send1.at[t, j], recv_sem=recv1.at[t, j],
                    device_id=(x, y, c), device_id_type=MESH).wait_recv()
                if split[t]:
                    cp = pltpu.make_async_remote_copy(
                        src_ref=landed, dst_ref=landed, send_sem=send2.at[t, j], recv_sem=recv2.at[t, j],
                        device_id=sibling, device_id_type=MESH)
                    cp.start()
                    passed.append(cp)
        for t in range(n):
            for j, (px, py) in enumerate(peers):
                if split[t]:
                    other = part(outs[t].at[2 * px + py], t, 1 - c)
                    pltpu.make_async_remote_copy(
                        src_ref=other, dst_ref=other, send_sem=send2.at[t, j], recv_sem=recv2.at[t, j],
                        device_id=(x, y, c), device_id_type=MESH).wait_recv()
        for cp in first + passed:
            cp.wait_send()
        for cp in local:
            cp.wait()

    return pl.pallas_call(
        body, name="gather_weights", in_specs=[ANY] * n, out_specs=[ANY] * n,
        out_shape=[jax.ShapeDtypeStruct((NCHIP,) + a.shape, a.dtype) for a in arrs],
        scratch_shapes=[pltpu.SemaphoreType.DMA((n, 3))] * 4 + [pltpu.SemaphoreType.DMA((n,))],
    )(*arrs)


def _adam_math(g, w, m, v):
    m = ADAM_B1 * m + (1.0 - ADAM_B1) * g
    v = ADAM_B2 * v + (1.0 - ADAM_B2) * (g * g)
    m_hat = m / (1.0 - ADAM_B1 ** ADAM_STEP)
    v_hat = v / (1.0 - ADAM_B2 ** ADAM_STEP)
    delta = -ADAM_LR * (m_hat / (jnp.sqrt(v_hat) + ADAM_EPS) + ADAM_WD * w)
    return delta, m, v


def _rows_tile(rows):
    return rows if rows <= 256 else 256


def sum_chips(own, parts, name):
    _, rows, cols = parts.shape
    tr = _rows_tile(rows)

    def body(own_ref, p_ref, o_ref):
        acc = own_ref[...].astype(F32)
        for s in range(NCHIP):
            acc = acc + p_ref[s].astype(F32)
        o_ref[...] = acc

    return pl.pallas_call(
        body, grid=(rows // tr,), name=name,
        in_specs=[pl.BlockSpec((tr, cols), lambda i: (i, 0)), pl.BlockSpec((NCHIP, tr, cols), lambda i: (0, i, 0))],
        out_specs=pl.BlockSpec((tr, cols), lambda i: (i, 0)),
        out_shape=jax.ShapeDtypeStruct((rows, cols), F32),
        compiler_params=_cparams(("parallel",)),
    )(own, parts)


def adam_shard(p_mine, p_sib, w, m, v, name):
    rows, cols = p_mine.shape
    tr = _rows_tile(rows)
    lead = w.ndim == 3

    def body(a_ref, b_ref, w_ref, m_ref, v_ref, g_ref, d_ref, mo_ref, vo_ref):
        g = a_ref[...] + b_ref[...]
        g = g[None] if lead else g
        g_ref[...] = g
        d_ref[...], mo_ref[...], vo_ref[...] = _adam_math(g, w_ref[...], m_ref[...], v_ref[...])

    flat = pl.BlockSpec((tr, cols), lambda i: (i, 0))
    spec = pl.BlockSpec((1, tr, cols), lambda i: (0, i, 0)) if lead else flat
    return pl.pallas_call(
        body, grid=(rows // tr,), name=name, in_specs=[flat] * 2 + [spec] * 3, out_specs=[spec] * 4,
        out_shape=[jax.ShapeDtypeStruct(w.shape, F32)] * 4,
        compiler_params=_cparams(("parallel",)),
    )(p_mine, p_sib, w, m, v)


def adam_shard_halves_t(r_mine, r_sib, wt, mt, vt, name):
    hrows, cols = r_mine.shape
    tr = _rows_tile(hrows)
    per_half = hrows // tr

    def body(a_ref, b_ref, w_ref, m_ref, v_ref, g_ref, d_ref, mo_ref, vo_ref):
        mine = pl.program_id(0) == lax.axis_index("c")
        g = jnp.where(mine, a_ref[...], b_ref[...]).T[None]
        g_ref[...] = g
        d_ref[...], mo_ref[...], vo_ref[...] = _adam_math(g, w_ref[...], m_ref[...], v_ref[...])

    flat = pl.BlockSpec((tr, cols), lambda h, i: (i, 0))
    spec = pl.BlockSpec((1, cols, tr), lambda h, i: (0, 0, h * per_half + i))
    return pl.pallas_call(
        body, grid=(2, per_half), name=name, in_specs=[flat] * 2 + [spec] * 3, out_specs=[spec] * 4,
        out_shape=[jax.ShapeDtypeStruct(wt.shape, F32)] * 4,
        compiler_params=_cparams(("parallel", "parallel")),
    )(r_mine, r_sib, wt, mt, vt)


def adam_replicated(parts, w, m, v, name):
    rows = w.shape[0]

    def body(p_ref, w_ref, m_ref, v_ref, g_ref, d_ref, mo_ref, vo_ref):
        g = p_ref[0]
        for d in range(1, NDEV):
            g = g + p_ref[d]
        g_ref[...] = g
        d_ref[...], mo_ref[...], vo_ref[...] = _adam_math(g, w_ref[...], m_ref[...], v_ref[...])

    return pl.pallas_call(
        body, name=name, out_shape=[jax.ShapeDtypeStruct((rows, 128), F32)] * 4,
    )(parts, w, m, v)


def _pack(arrs):
    pieces = []
    for a in arrs:
        flat = a.reshape(-1)
        pad = (-flat.shape[0]) % 128
        pieces.append(jnp.pad(flat, (0, pad)) if pad else flat)
    flat = jnp.concatenate(pieces)
    pad = (-flat.shape[0]) % 1024
    return jnp.pad(flat, (0, pad)).reshape(-1, 128)


def _unpack(buf, shapes):
    flat = buf.reshape(-1)
    out = []
    o = 0
    for s in shapes:
        n = int(np.prod(s))
        out.append(flat[o:o + n].reshape(s))
        o += n + (-n) % 128
    return out


EVEN_SPLITS = (SHIFT, W, W, W, W, W)
ODD_SPLITS = (D, D, D)


def _cols_to_chips(a):
    rows, cols = a.shape
    return a.reshape(rows, NCHIP, cols // NCHIP).transpose(1, 0, 2)


def _chips_to_cols(a):
    _, rows, n = a.shape
    return a.transpose(1, 0, 2).reshape(rows, NCHIP * n)


def kernel(x, norm_g, w_in_e, shift_mu, rw_w0, rw_w2, rw_a0, rw_a2, rw_kk, rw_ka, rw_rk, rw_lnx_g, rw_lnx_b, att_bias, w_out_e, w_in_o, sg_ln_g, sg_ln_b, sg_w, sg_b, w_out_o, final_g, loss_target, m_norm_g, m_w_in_e, m_shift_mu, m_rw_w0, m_rw_w2, m_rw_a0, m_rw_a2, m_rw_kk, m_rw_ka, m_rw_rk, m_rw_lnx_g, m_rw_lnx_b, m_att_bias, m_w_out_e, m_w_in_o, m_sg_ln_g, m_sg_ln_b, m_sg_w, m_sg_b, m_w_out_o, m_final_g, v_norm_g, v_w_in_e, v_shift_mu, v_rw_w0, v_rw_w2, v_rw_a0, v_rw_a2, v_rw_kk, v_rw_ka, v_rw_rk, v_rw_lnx_g, v_rw_lnx_b, v_att_bias, v_w_out_e, v_w_in_o, v_sg_ln_g, v_sg_ln_b, v_sg_w, v_sg_b, v_w_out_o, v_final_g):
    x2 = x.reshape(T, D)
    tgt = loss_target.reshape(T, D)

    my_chip = 2 * lax.axis_index("x") + lax.axis_index("y")
    gathered = gather_weights(
        [jnp.swapaxes(w_in_e[0], 0, 1).astype(BF16), jnp.concatenate([rw_w2[0], rw_a2[0]], axis=0),
         jnp.concatenate([sg_ln_g, sg_ln_b], axis=0)], [True, True, False])
    wie = gathered[0].reshape(EVEN_IN, D)
    w2 = _chips_to_cols(gathered[1][:, :LORA])
    a2 = _chips_to_cols(gathered[1][:, LORA:])
    sglg = _chips_to_cols(gathered[2][:, 0:1])
    sglb = _chips_to_cols(gathered[2][:, 1:2])

    late = [w_out_e[0].astype(BF16), w_in_o[0].astype(BF16), w_out_o[0].astype(BF16)]
    late_started = send_start(late, [jnp.broadcast_to(a[None], (NCHIP,) + a.shape) for a in late], False, False,
                              "late_weights_start")

    def late_weights(after):
        woe, wio, woo = send_wait(late_started, after, False, False, "late_weights_wait")
        return woe.reshape(D, D), _chips_to_cols(wio), woo.reshape(D, D)

    def scatter_start(grads, name):
        srcs = [g_.astype(BF16) if g_.shape[-1] >= W else g_ for g_ in grads]
        return send_start(srcs, [jnp.zeros_like(s) for s in srcs], True, False, name)

    def own_block(g_):
        return lax.dynamic_index_in_dim(g_, my_chip, axis=0, keepdims=False)

    started = {}

    def on_odd_grads(d_woo, d_wio):
        blocks = [d_woo.reshape(NCHIP, D // NCHIP, D), d_wio]
        started["odd"] = (scatter_start(blocks, "odd_grads_start"), [own_block(b) for b in blocks])
        return started["odd"][0][-1]

    def on_even_grads(big_g):
        d_wie, d_woe, _, _, d_w2, d_a2, d_sglg, d_sglb = big_g
        my_half = lax.dynamic_slice_in_dim(d_wie, lax.axis_index("c") * (D // 2), D // 2, axis=1)
        d_wie_half = add_blocks(my_half, swap_row_halves(d_wie, "swap_w_in_e_halves"), "add_w_in_e_halves")
        blocks = [d_wie_half, d_woe.reshape(NCHIP, D // NCHIP, D), _cols_to_chips(d_w2), _cols_to_chips(d_a2),
                  _cols_to_chips(d_sglg), _cols_to_chips(d_sglb)]
        started["even"] = (scatter_start(blocks, "even_grads_start"), [own_block(b) for b in blocks])
        return started["even"][0][-1]

    def on_small_grads(layer, grads):
        mine = _pack(grads)
        started[layer + "_small"] = send_start([mine], [jnp.broadcast_to(mine[None], (NDEV,) + mine.shape)], False,
                                               True, layer + "_small_grads_start")
        return started[layer + "_small"][-1]

    loss_part, dx, _, _ = _local_step(
        x2, tgt, wie, late_weights, w2, a2, sglg, sglb, norm_g, shift_mu, rw_w0, rw_a0, rw_kk, rw_ka, rw_rk,
        rw_lnx_g, rw_lnx_b, att_bias, sg_w, sg_b, final_g, first_after=late_started[-1], on_odd_grads=on_odd_grads,
        on_even_grads=on_even_grads, on_small_grads=on_small_grads)
    even_started, even_own = started["even"]

    wmv = {"w_in_e": tuple(jnp.swapaxes(a, 1, 2) for a in (w_in_e, m_w_in_e, v_w_in_e)),
           "w_out_e": (w_out_e, m_w_out_e, v_w_out_e),
           "w_in_o": (w_in_o, m_w_in_o, v_w_in_o), "w_out_o": (w_out_o, m_w_out_o, v_w_out_o),
           "rw_w2": (rw_w2, m_rw_w2, v_rw_w2), "rw_a2": (rw_a2, m_rw_a2, v_rw_a2),
           "sg_ln_g": (sg_ln_g, m_sg_ln_g, v_sg_ln_g), "sg_ln_b": (sg_ln_b, m_sg_ln_b, v_sg_ln_b)}
    sharded = {}

    def finish(names, own, landed, tag):
        partial = [sum_chips(o_, p_, "sum_" + nm) for o_, p_, nm in zip(own, landed, names)]
        from_sibling = exchange_c(partial, "swap_partials_" + tag)
        for nm, mine, sib in zip(names, partial, from_sibling):
            if nm == "w_in_e":
                res = adam_shard_halves_t(mine, sib, *wmv[nm], "adam_" + nm)
                sharded[nm] = [jnp.swapaxes(a, 1, 2) for a in res]
            else:
                sharded[nm] = adam_shard(mine, sib, *wmv[nm], "adam_" + nm)
        return partial[0]

    odd_started, odd_own = started["odd"]
    odd_landed = send_wait(odd_started, started["even_small"][-1], True, False, "odd_grads_wait")
    done = finish(["w_out_o", "w_in_o"], odd_own, odd_landed, "odd")

    no_w = jnp.zeros((1, 1), F32)
    groups = {
        "odd": (["sg_w", "sg_b", "final_g", "norm_g1"], [sg_w, sg_b, final_g, norm_g[1:2]],
                [m_sg_w, m_sg_b, m_final_g, m_norm_g[1:2]], [v_sg_w, v_sg_b, v_final_g, v_norm_g[1:2]]),
        "even": (["norm_g0", "shift_mu", "rw_w0", "rw_a0", "rw_kk", "rw_ka", "rw_rk", "rw_lnx_g", "rw_lnx_b",
                  "att_bias", "loss"],
                 [norm_g[0:1], shift_mu, rw_w0, rw_a0, rw_kk, rw_ka, rw_rk, rw_lnx_g, rw_lnx_b, att_bias, no_w],
                 [m_norm_g[0:1], m_shift_mu, m_rw_w0, m_rw_a0, m_rw_kk, m_rw_ka, m_rw_rk, m_rw_lnx_g, m_rw_lnx_b,
                  m_att_bias, no_w],
                 [v_norm_g[0:1], v_shift_mu, v_rw_w0, v_rw_a0, v_rw_kk, v_rw_ka, v_rw_rk, v_rw_lnx_g, v_rw_lnx_b,
                  v_att_bias, no_w]),
    }
    rep = {}
    for layer in ("odd", "even"):
        nms, ws, ms_, vs_ = groups[layer]
        (gathered_g,) = send_wait(started[layer + "_small"], done, False, True, layer + "_small_grads_wait")
        rep_out = adam_replicated(gathered_g, _pack(ws), _pack(ms_), _pack(vs_), "adam_" + layer + "_small")
        done = rep_out[0]
        for nm in nms:
            rep[nm] = []
        for buf in rep_out:
            for nm, a in zip(nms, _unpack(buf, [w_.shape for w_ in ws])):
                rep[nm].append(a)
    rep["norm_g"] = [jnp.concatenate([a, b], axis=0) for a, b in zip(rep["norm_g0"], rep["norm_g1"])]
    even_landed = send_wait(even_started, done, True, False, "even_grads_wait")
    finish(["w_in_e", "w_out_e", "rw_w2", "rw_a2", "sg_ln_g", "sg_ln_b"], even_own, even_landed, "even")

    order = ["norm_g", "w_in_e", "shift_mu", "rw_w0", "rw_w2", "rw_a0", "rw_a2", "rw_kk", "rw_ka", "rw_rk",
             "rw_lnx_g", "rw_lnx_b", "att_bias", "w_out_e", "w_in_o", "sg_ln_g", "sg_ln_b", "sg_w", "sg_b",
             "w_out_o", "final_g"]
    results = {**sharded, **rep}
    outs = [rep["loss"][0].reshape(()), dx.reshape(NSEQ, SEQ, D)]
    for kind in range(4):
        outs += [results[nm][kind] for nm in order]
    return tuple(outs)


def _local_step(x2, tgt, wie_t, late_weights, w2, a2, sglg, sglb, norm_g, shift_mu, rw_w0, rw_a0, rw_kk, rw_ka, rw_rk,
                rw_lnx_g, rw_lnx_b, att_bias, sg_w, sg_b, final_g, first_after=None, on_odd_grads=None,
                on_even_grads=None, on_small_grads=None):
    zl = jnp.zeros((LORA, W), F32)
    w2x = jnp.concatenate([w2, zl], axis=0)
    a2x = jnp.concatenate([zl, a2], axis=0)
    rk = rw_rk.reshape(1, W)
    pos = np.arange(SGC)
    sg_mask = jnp.asarray(((pos[None, :] // L) <= (pos[:, None] // L)).astype(np.float32))
    wm = (sg_w[0] * sg_mask[None]).astype(BF16)
    sgb_t = sg_b[0].T

    xn0, ps, ga, q, kb, vb, gb = ln_in_proj(x2, norm_g[0:1], wie_t, EVEN_SPLITS, "in_proj_even", after=first_after,
                                            w_t=True)
    r, lw, k2, v, aa, bb = even_prep(ps, shift_mu, rw_w0, w2x, rw_a0, a2x, rw_kk, rw_ka)
    y, rw_saved = rwkv_fwd(r, lw, k2, v, aa, bb)
    bias = bias_expand(att_bias[0]).reshape(NPAIR, 2 * L, BAND)

    def padded(a):
        return jnp.pad(a.astype(BF16).reshape(NSEQ, SEQ, W), ((0, 0), (LEFT * L, 0), (0, 0))).reshape(NSEQ * PADSEQ, W)

    kpad, vpad = padded(kb), padded(vb)
    o = attention_fwd(q, kpad, vpad, bias)
    z, zt = even_post(y, r, k2, v, ga, o, gb, rw_lnx_g, rw_lnx_b, rk)
    woe, wio, woo = late_weights(z)
    h1 = out_proj(x2, z, woe, "out_proj_even")
    xn1, u, vv, gt = ln_in_proj(h1, norm_g[1:2], wio, ODD_SPLITS, "in_proj_odd")
    z2, z2t = gmlp_fwd(u, vv, gt, sglg, sglb, wm, sgb_t)
    dh2, loss_part, d_final_g = out_proj_loss(h1, z2, woo, final_g[None], tgt)

    dz2, d_woo = out_proj_bwd(dh2, z2t, woo, "out_proj_odd_bwd")
    du, dvv, dgt, d_sglg, d_sglb, d_wm, d_sgb_t = gmlp_bwd(u, vv, gt, sglg, sglb, wm, sgb_t, dz2)
    dp_odd = [du, dvv, dgt]
    d_wio = matmul_acc_chips(xn1, dp_odd, "in_proj_odd_dw")
    token = on_odd_grads(d_woo, d_wio) if on_odd_grads else None
    dh1, d_g1 = in_proj_bwd_x(h1, norm_g[1:2], wio, dp_odd, dh2, "in_proj_odd_bwd", after=token)
    odd_small = [d_wm * sg_mask[None], d_sgb_t.T, d_final_g, d_g1]
    token = on_small_grads("odd", odd_small) if on_small_grads else None
    dz, d_woe = out_proj_bwd(dh1, zt, woe, "out_proj_even_bwd", after=token)
    dy, dr2, dk22, dv2, dga, do, dgb, d_lng, d_lnb, d_rk = even_post_bwd(
        y, r, k2, v, ga, o, gb, rw_lnx_g, rw_lnx_b, rk, dz)
    dq, dkb, dvb, dbias = attention_bwd(q, kpad, vpad, bias, do)
    d_att_bias = bias_grad(dbias.reshape(NH, L, BAND))
    dr, dlw, dk2, dv, daa, dbb = rwkv_bwd(r, lw, k2, aa, bb, rw_saved, dy)
    dps, d_mu, d_w0, d_w2x, d_a0, d_a2x, d_kk, d_ka = even_prep_bwd(
        ps, shift_mu, rw_w0, w2x, rw_a0, a2x, rw_kk, rw_ka, dr, dlw, dk2, dv, daa, dbb, dr2, dk22, dv2)
    dp_even = [dps, dga, dq, dkb, dvb, dgb]
    d_wie = matmul_acc_chips(xn0, dp_even, "in_proj_even_dw")
    big_g = (d_wie, d_woe, d_wio, d_woo, d_w2x[:LORA], d_a2x[LORA:], d_sglg, d_sglb)
    token = on_even_grads(big_g) if on_even_grads else None
    dx, d_g0 = in_proj_bwd_x(x2, norm_g[0:1], wie_t, dp_even, dh1, "in_proj_even_bwd", after=token, w_t=True)
    even_small = [d_g0, d_mu, d_w0, d_a0, d_kk, d_ka, d_rk, d_lng, d_lnb, d_att_bias]
    if on_small_grads:
        on_small_grads("even", even_small + [loss_part[0:1, 0:1]])
    rep_g = [jnp.concatenate([d_g0, d_g1], axis=0)] + even_small[1:] + odd_small[:3]
    return loss_part[0, 0], dx, big_g, rep_g
```

```python
import functools
import math

import jax
import jax.numpy as jnp
import numpy as np
from jax import lax
from jax.experimental import pallas as pl
from jax.experimental.pallas import tpu as pltpu

F32 = jnp.float32
BF16 = jnp.bfloat16
HI = lax.Precision.HIGHEST

D = 1024
SEQ = 2048
NSEQ = 2
T = NSEQ * SEQ
HD = 64
NH = 8
W = 512
SHIFT = 1664
LORA = 64
EVEN_IN = 4224
ODD_IN = 3072
L = 64
NC = SEQ // L
LEFT = 8
BAND = (LEFT + 1) * L
CLIP = 128
SGC = 128
NG = 8
RMS_EPS = 1e-6
LN_EPS = 1e-5
GN_EPS = 64e-5
NEG = -1e30
VMEM_BIG = 56 * 1024 * 1024

ADAM_LR = 0.001
ADAM_B1 = 0.9
ADAM_B2 = 0.999
ADAM_EPS = 1e-08
ADAM_WD = 0.01
ADAM_STEP = 10

MESH = pl.DeviceIdType.MESH


def _bdot(a, b):
    return jnp.dot(a.astype(BF16), b.astype(BF16), preferred_element_type=F32)


def _bdot_nt(a, b):
    return lax.dot_general(a.astype(BF16), b.astype(BF16), (((1,), (1,)), ((), ())), preferred_element_type=F32)


def _bdot_tn(a, b):
    return lax.dot_general(a.astype(BF16), b.astype(BF16), (((0,), (0,)), ((), ())), preferred_element_type=F32)


def _hdot(a, b):
    return jnp.dot(a, b, precision=HI, preferred_element_type=F32)


def _hdot_nt(a, b):
    return lax.dot_general(a, b, (((1,), (1,)), ((), ())), precision=HI, preferred_element_type=F32)


def _hdot_tn(a, b):
    return lax.dot_general(a, b, (((0,), (0,)), ((), ())), precision=HI, preferred_element_type=F32)


def _iota2(shape, dim):
    return lax.broadcasted_iota(jnp.int32, shape, dim)


def _head_blockdiag():
    r = _iota2((W, W), 0) // HD
    c = _iota2((W, W), 1) // HD
    return (r == c).astype(BF16)


def _headsum_impl(x, bd):
    hi = x.astype(BF16)
    mid = (x - hi.astype(F32)).astype(BF16)
    return jnp.dot(hi, bd, preferred_element_type=F32) + jnp.dot(mid, bd, preferred_element_type=F32)


@jax.custom_vjp
def _headsum(x, bd):
    return _headsum_impl(x, bd)


def _headsum_fwd(x, bd):
    return _headsum_impl(x, bd), bd


def _headsum_bwd(bd, ct):
    return _headsum_impl(ct, bd), None


_headsum.defvjp(_headsum_fwd, _headsum_bwd)


def _silu(x):
    return x * jax.nn.sigmoid(x)


def _dsilu(x):
    s = jax.nn.sigmoid(x)
    return s * (1.0 + x * (1.0 - s))


_GELU_C = math.sqrt(2.0 / math.pi)


def _gelu(x):
    return 0.5 * x * (1.0 + jnp.tanh(_GELU_C * (x + 0.044715 * (x * x * x))))


def _dgelu(x):
    t = jnp.tanh(_GELU_C * (x + 0.044715 * (x * x * x)))
    return 0.5 * (1.0 + t) + 0.5 * x * (1.0 - t * t) * _GELU_C * (1.0 + 3.0 * 0.044715 * x * x)


def _silu_both(x):
    s = jax.nn.sigmoid(x)
    xs = x * s
    return xs, s + xs * (1.0 - s)


def _gelu_both(x):
    x2 = x * x
    t = jnp.tanh(_GELU_C * (x + 0.044715 * (x2 * x)))
    half = 0.5 * (1.0 + t)
    return x * half, half + 0.5 * x * (1.0 - t * t) * _GELU_C * (1.0 + 3.0 * 0.044715 * x2)


def _softplus(x):
    return jnp.maximum(x, 0.0) + jnp.log(1.0 + jnp.exp(-jnp.abs(x)))


def _cparams(sem, vmem=None):
    return pltpu.CompilerParams(dimension_semantics=sem, vmem_limit_bytes=vmem)


def _row_spec(tm, width):
    return pl.BlockSpec((tm, width), lambda i: (i, 0))


def _col_spec(height, tm):
    return pl.BlockSpec((height, tm), lambda i: (0, i))


def _const_spec(shape):
    nd = len(shape)
    return pl.BlockSpec(shape, lambda *_: (0,) * nd)


def _weight_dims(w_bf, w_t):
    return (((1,), (1,)), ((), ())) if w_t else (((1,), (0,)), ((), ())), w_bf.shape[0 if w_t else 1]


def ln_in_proj(x, g, w_bf, splits, name, after=None, w_t=False):
    dims, n = _weight_dims(w_bf, w_t)
    tm = 256
    spans = []
    o = 0
    for s in splits:
        spans.append((o, o + s))
        o += s
    assert o == n
    extra_specs, extra = _after_operand(after)

    def body(x_ref, g_ref, w_ref, *rest):
        xn_ref, outs = rest[len(extra)], rest[len(extra) + 1:]
        xv = x_ref[...]
        rstd = lax.rsqrt(jnp.mean(xv * xv, axis=-1, keepdims=True) + RMS_EPS)
        xn = (xv * rstd * g_ref[...]).astype(BF16)
        xn_ref[...] = xn.T
        p = lax.dot_general(xn, w_ref[...], dims, preferred_element_type=F32)
        for o_ref, (a, b) in zip(outs, spans):
            o_ref[...] = p[:, a:b]

    return pl.pallas_call(
        body, grid=(T // tm,), name=name,
        in_specs=[_row_spec(tm, D), _const_spec((1, D)), _const_spec(w_bf.shape)] + extra_specs,
        out_specs=[_col_spec(D, tm)] + [_row_spec(tm, s) for s in splits],
        out_shape=[jax.ShapeDtypeStruct((D, T), BF16)] + [jax.ShapeDtypeStruct((T, s), F32) for s in splits],
        compiler_params=_cparams(("parallel",), VMEM_BIG),
    )(x, g, w_bf, *extra)


def in_proj_bwd_x(x, g, w_bf, dps, dres, name, after=None, w_t=False):
    tm = 512
    back = (((1,), (0,)), ((), ())) if w_t else (((1,), (1,)), ((), ()))
    widths = [d.shape[1] for d in dps]
    extra_specs, extra = _after_operand(after)

    def body(x_ref, g_ref, w_ref, dres_ref, *rest):
        dp_refs = rest[:len(widths)]
        dx_ref, dg_ref = rest[-2:]
        dp = jnp.concatenate([r[...] for r in dp_refs], axis=-1)
        dxn = lax.dot_general(dp, w_ref[...], back, preferred_element_type=F32)
        xv = x_ref[...]
        rstd = lax.rsqrt(jnp.mean(xv * xv, axis=-1, keepdims=True) + RMS_EPS)
        xhat = xv * rstd
        dgp = jnp.sum(dxn * xhat, axis=0, keepdims=True)

        @pl.when(pl.program_id(0) == 0)
        def _():
            dg_ref[...] = jnp.zeros_like(dg_ref)

        dg_ref[...] += dgp
        dxh = dxn * g_ref[...]
        dx_ref[...] = dres_ref[...] + rstd * (dxh - xhat * jnp.mean(dxh * xhat, axis=-1, keepdims=True))

    return pl.pallas_call(
        body, grid=(T // tm,), name=name,
        in_specs=[_row_spec(tm, D), _const_spec((1, D)), _const_spec(w_bf.shape), _row_spec(tm, D)]
        + [_row_spec(tm, s) for s in widths] + extra_specs,
        out_specs=[_row_spec(tm, D), _const_spec((1, D))],
        out_shape=[jax.ShapeDtypeStruct((T, D), F32), jax.ShapeDtypeStruct((1, D), F32)],
        compiler_params=_cparams(("arbitrary",), VMEM_BIG),
    )(x, g, w_bf, dres, *dps, *extra)


def _after_operand(after):
    return ([ANY], [after]) if after is not None else ([], [])


def matmul_acc_chips(at_bf, pieces, name, after=None):
    k = at_bf.shape[0]
    widths = [p.shape[1] for p in pieces]
    nb = sum(widths) // NCHIP
    tm = 512
    steps = T // tm
    extra_specs, extra = _after_operand(after)

    def body(a_ref, *rest):
        o_ref, acc = rest[-2:]

        @pl.when(pl.program_id(0) == 0)
        def _():
            acc[...] = jnp.zeros_like(acc)

        a = a_ref[...]
        b = jnp.concatenate([r[...] for r in rest[:len(widths)]], axis=-1)
        for s in range(NCHIP):
            acc[s] += jnp.dot(a, b[:, s * nb:(s + 1) * nb], preferred_element_type=F32)

        @pl.when(pl.program_id(0) == steps - 1)
        def _():
            o_ref[...] = acc[...].astype(BF16)

    return pl.pallas_call(
        body, grid=(steps,), name=name,
        in_specs=[_col_spec(k, tm)] + [_row_spec(tm, w_) for w_ in widths] + extra_specs,
        out_specs=_const_spec((NCHIP, k, nb)),
        out_shape=jax.ShapeDtypeStruct((NCHIP, k, nb), BF16),
        scratch_shapes=[pltpu.VMEM((NCHIP, k, nb), F32)],
        compiler_params=_cparams(("arbitrary",), VMEM_BIG),
    )(at_bf, *pieces, *extra)


def out_proj(h, z_bf, w_bf, name):
    tm = 512

    def body(h_ref, z_ref, w_ref, o_ref):
        o_ref[...] = h_ref[...] + jnp.dot(z_ref[...], w_ref[...], preferred_element_type=F32)

    return pl.pallas_call(
        body, grid=(T // tm,), name=name,
        in_specs=[_row_spec(tm, D), _row_spec(tm, D), _const_spec((D, D))],
        out_specs=_row_spec(tm, D), out_shape=jax.ShapeDtypeStruct((T, D), F32),
        compiler_params=_cparams(("parallel",)),
    )(h, z_bf, w_bf)


def _out_proj_back(dh_ref, zt_ref, w_ref, dw_ref):
    dhb = dh_ref[...].astype(BF16)

    @pl.when(pl.program_id(0) == 0)
    def _():
        dw_ref[...] = jnp.zeros_like(dw_ref)

    dw_ref[...] += jnp.dot(zt_ref[...], dhb, preferred_element_type=F32)
    return lax.dot_general(dhb, w_ref[...], (((1,), (1,)), ((), ())), preferred_element_type=F32)


def out_proj_loss(h, z_bf, w_bf, g, target):
    tm = 512

    def body(h_ref, z_ref, w_ref, g_ref, t_ref, dh_ref, loss_ref, dg_ref):
        xv = h_ref[...] + jnp.dot(z_ref[...], w_ref[...], preferred_element_type=F32)
        rstd = lax.rsqrt(jnp.mean(xv * xv, axis=-1, keepdims=True) + RMS_EPS)
        xhat = xv * rstd
        err = xhat * g_ref[...] - t_ref[...]
        part = 0.5 * jnp.sum(jnp.mean(err * err, axis=-1, keepdims=True), axis=0, keepdims=True)
        dout = err * (1.0 / D)

        @pl.when(pl.program_id(0) == 0)
        def _():
            loss_ref[...] = jnp.zeros_like(loss_ref)
            dg_ref[...] = jnp.zeros_like(dg_ref)

        loss_ref[...] += jnp.broadcast_to(part, loss_ref.shape)
        dg_ref[...] += jnp.sum(dout * xhat, axis=0, keepdims=True)
        dxh = dout * g_ref[...]
        dh_ref[...] = rstd * (dxh - xhat * jnp.mean(dxh * xhat, axis=-1, keepdims=True))

    return pl.pallas_call(
        body, grid=(T // tm,), name="out_proj_loss",
        in_specs=[_row_spec(tm, D), _row_spec(tm, D), _const_spec((D, D)), _const_spec((1, D)), _row_spec(tm, D)],
        out_specs=[_row_spec(tm, D), _const_spec((8, 128)), _const_spec((1, D))],
        out_shape=[jax.ShapeDtypeStruct((T, D), F32), jax.ShapeDtypeStruct((8, 128), F32),
                   jax.ShapeDtypeStruct((1, D), F32)],
        compiler_params=_cparams(("arbitrary",)),
    )(h, z_bf, w_bf, g, target)


PREP_TM = 256
PREP_NB = SEQ // PREP_TM


def _prep_elem(k, wl, apre, kkw, kaw, bd):
    wraw = -_softplus(-wl) - 0.5
    lw = -jnp.exp(wraw)
    asig = jax.nn.sigmoid(apre)
    kkr = k * kkw
    nrm = jnp.maximum(jnp.sqrt(_headsum(kkr * kkr, bd)), 1e-12)
    kk = kkr / nrm
    k2 = k * (1.0 + (asig - 1.0) * kaw)
    return lw, k2, -kk, kk * asig


def _shifted(ps_ref, prev_ref, mu, blk):
    p = ps_ref[...]
    first = (blk % PREP_NB) == 0
    prev_row = jnp.where(first, 0.0, prev_ref[7:8, :])
    rolled = pltpu.roll(p, 1, 0)
    p_prev = jnp.where(_iota2(p.shape, 0) == 0, prev_row, rolled)
    return p, p_prev, p + (p_prev - p) * mu


def _prev_spec(width, blk_of):
    return pl.BlockSpec((8, width), lambda i: (jnp.maximum(blk_of(i) * (PREP_TM // 8) - 1, 0), 0))


def even_prep(ps, mu, w0, w2x, a0, a2x, kkw, kaw):
    tm = PREP_TM

    def body(ps_ref, prev_ref, mu_ref, w0_ref, w2_ref, a0_ref, a2_ref, kk_ref, ka_ref,
             r_ref, lw_ref, k2_ref, v_ref, aa_ref, bb_ref):
        _, _, s = _shifted(ps_ref, prev_ref, mu_ref[...], pl.program_id(0))
        wa = s[:, 3 * W:]
        wl = w0_ref[...] + _bdot(jnp.tanh(wa), w2_ref[...])
        apre = a0_ref[...] + _bdot(wa, a2_ref[...])
        lw, k2, aa, bb = _prep_elem(s[:, W:2 * W], wl, apre, kk_ref[...], ka_ref[...], _head_blockdiag())
        r_ref[...] = s[:, 0:W]
        v_ref[...] = s[:, 2 * W:3 * W]
        lw_ref[...] = lw
        k2_ref[...] = k2
        aa_ref[...] = aa
        bb_ref[...] = bb

    vec = _const_spec((1, W))
    return pl.pallas_call(
        body, grid=(T // tm,), name="even_prep",
        in_specs=[_row_spec(tm, SHIFT), _prev_spec(SHIFT, lambda i: i), _const_spec((1, SHIFT)), vec,
                  _const_spec((2 * LORA, W)), vec, _const_spec((2 * LORA, W)), vec, vec],
        out_specs=[_row_spec(tm, W)] * 6,
        out_shape=[jax.ShapeDtypeStruct((T, W), F32)] * 6,
        compiler_params=_cparams(("parallel",)),
    )(ps, ps, mu, w0, w2x, a0, a2x, kkw, kaw)


def even_prep_bwd(ps, mu, w0, w2x, a0, a2x, kkw, kaw, dr, dlw, dk2, dv, daa, dbb, dr2, dk22, dv2):
    tm = PREP_TM
    nb = T // tm
    rev = lambda i: nb - 1 - i

    def body(ps_ref, prev_ref, mu_ref, w0_ref, w2_ref, a0_ref, a2_ref, kk_ref, ka_ref,
             dr_ref, dlw_ref, dk2_ref, dv_ref, daa_ref, dbb_ref, dr2_ref, dk22_ref, dv2_ref,
             dps_ref, dmu_ref, dw0_ref, dw2_ref, da0_ref, da2_ref, dkk_ref, dka_ref, carry):
        i = pl.program_id(0)
        blk = rev(i)
        mu_v = mu_ref[...]
        p, p_prev, s = _shifted(ps_ref, prev_ref, mu_v, blk)
        wa = s[:, 3 * W:]
        th = jnp.tanh(wa)
        wl = w0_ref[...] + _bdot(th, w2_ref[...])
        apre = a0_ref[...] + _bdot(wa, a2_ref[...])
        bd = _head_blockdiag()
        k = s[:, W:2 * W]
        _, vjp = jax.vjp(lambda k_, wl_, ap_, kkw_, kaw_: _prep_elem(k_, wl_, ap_, kkw_, kaw_, bd),
                         k, wl, apre, kk_ref[...], ka_ref[...])
        dk, dwl, dap, dkkw, dkaw = vjp((dlw_ref[...], dk2_ref[...] + dk22_ref[...], daa_ref[...], dbb_ref[...]))
        dwa = _bdot_nt(dwl, w2_ref[...]) * (1.0 - th * th) + _bdot_nt(dap, a2_ref[...])
        ds = jnp.concatenate([dr_ref[...] + dr2_ref[...], dk, dv_ref[...] + dv2_ref[...], dwa], axis=-1)

        @pl.when(i == 0)
        def _():
            for ref in (dmu_ref, dw0_ref, dw2_ref, da0_ref, da2_ref, dkk_ref, dka_ref, carry):
                ref[...] = jnp.zeros_like(ref)

        dmu_ref[...] += jnp.sum(ds * (p_prev - p), axis=0, keepdims=True)
        dw0_ref[...] += jnp.sum(dwl, axis=0, keepdims=True)
        da0_ref[...] += jnp.sum(dap, axis=0, keepdims=True)
        dw2_ref[...] += _bdot_tn(th, dwl)
        da2_ref[...] += _bdot_tn(wa, dap)
        dkk_ref[...] += dkkw
        dka_ref[...] += dkaw
        dsm = ds * mu_v
        last = (blk % PREP_NB) == PREP_NB - 1
        nxt = jnp.where(last, 0.0, carry[0:1, :])
        up = pltpu.roll(dsm, tm - 1, 0)
        up = jnp.where(_iota2(up.shape, 0) == tm - 1, nxt, up)
        dps_ref[...] = (ds - dsm + up).astype(BF16)
        carry[0:1, :] = dsm[0:1, :]

    vec = _const_spec((1, W))
    rrow = lambda width: pl.BlockSpec((tm, width), lambda i: (rev(i), 0))
    return pl.pallas_call(
        body, grid=(nb,), name="even_prep_bwd",
        in_specs=[rrow(SHIFT), _prev_spec(SHIFT, rev), _const_spec((1, SHIFT)), vec,
                  _const_spec((2 * LORA, W)), vec, _const_spec((2 * LORA, W)), vec, vec] + [rrow(W)] * 9,
        out_specs=[rrow(SHIFT), _const_spec((1, SHIFT)), vec, _const_spec((2 * LORA, W)), vec,
                   _const_spec((2 * LORA, W)), vec, vec],
        out_shape=[jax.ShapeDtypeStruct((T, SHIFT), BF16), jax.ShapeDtypeStruct((1, SHIFT), F32),
                   jax.ShapeDtypeStruct((1, W), F32), jax.ShapeDtypeStruct((2 * LORA, W), F32),
                   jax.ShapeDtypeStruct((1, W), F32), jax.ShapeDtypeStruct((2 * LORA, W), F32),
                   jax.ShapeDtypeStruct((1, W), F32), jax.ShapeDtypeStruct((1, W), F32)],
        scratch_shapes=[pltpu.VMEM((8, SHIFT), F32)],
        compiler_params=_cparams(("arbitrary",), VMEM_BIG),
    )(ps, ps, mu, w0, w2x, a0, a2x, kkw, kaw, dr, dlw, dk2, dv, daa, dbb, dr2, dk22, dv2)


NPAIR = NH // 2
PW = 2 * HD


def _pair_cols(p):
    return slice(p * PW, (p + 1) * PW)


def _pairs(a):
    return [a[:, _pair_cols(p)] for p in range(NPAIR)]


def _stack_pair(a):
    first = _iota2(a.shape, 1) < HD
    zero = jnp.zeros_like(a)
    return jnp.concatenate([jnp.where(first, a, zero), jnp.where(first, zero, a)], axis=0)


def _unstack_pair(a):
    n = a.shape[0] // 2
    return jnp.where(_iota2((n, PW), 1) < HD, a[:n], a[n:])


def _fold_pair(a):
    n = a.shape[0] // 2
    return a[:n] + a[n:]


def _chunk_masks():
    n = 4 * L
    row = _iota2((n, n), 0)
    col = _iota2((n, n), 1)
    same = ((row // L) & 1) == ((col // L) & 1)
    ri = row & (L - 1)
    ci = col & (L - 1)
    keep = same & (((row < 2 * L) & (ri > ci)) | ((row >= 2 * L) & (ri >= ci)))
    r1 = _iota2((L, L), 0)
    c1 = _iota2((L, L), 1)
    r2 = _iota2((2 * L, 2 * L), 0)
    c2 = _iota2((2 * L, 2 * L), 1)
    return keep.astype(F32), (r1 >= c1).astype(F32), (r2 == c2).astype(F32)


def _scaled(r, lw, k2, aa, bb, tri):
    g = _hdot(tri, lw)
    eg = jnp.exp(g)
    eng = jnp.exp(-g)
    egp = jnp.exp(g - lw)
    return eg, eng, egp, aa * egp, r * eg, bb * eng, k2 * eng


def _head_cols(h):
    return slice(h * HD, (h + 1) * HD)


def _per_head(a):
    return [a[:, _head_cols(h)] for h in range(NH)]


def _pairs_operands(at, rt, bt, kt):
    x = [jnp.concatenate([_stack_pair(a), _stack_pair(r)], axis=0).astype(BF16) for a, r in zip(_pairs(at), _pairs(rt))]
    yk = [jnp.concatenate([_stack_pair(b), _stack_pair(k)], axis=0).astype(BF16) for b, k in zip(_pairs(bt), _pairs(kt))]
    return x, yk


def _pairs_matrices(x, yk, keep, eye):
    m = [_bdot_nt(a, b) * keep for a, b in zip(x, yk)]
    p = [a[:2 * L, :2 * L] for a in m]
    tinv = [eye + a for a in p]
    for _ in range(5):
        p = [_bdot(a, a) for a in p]
        tinv = [t + _bdot(t, a) for t, a in zip(tinv, p)]
    return [a.astype(BF16) for a in m], [a.astype(BF16) for a in tinv]


def _pairs_fwd(x, yk, m, tinv, vw, s0, egl):
    xh = [_bdot_nt(a, s) for a, s in zip(x, s0)]
    u = [_bdot(t, h[:2 * L] + _bdot(a[:2 * L, 2 * L:], w)) for t, h, a, w in zip(tinv, xh, m, vw)]
    uv = [jnp.concatenate([a, w], axis=0).astype(BF16) for a, w in zip(u, vw)]
    y = [h[2 * L:] + _bdot(a[2 * L:], w) for h, a, w in zip(xh, m, uv)]
    sn = [e * (s + _bdot_tn(w, b)) for e, s, w, b in zip(egl, s0, uv, yk)]
    return y, sn, uv


def _pairs_bwd(x, yk, m, tinv, uv, s0, sn, egl, dyw, dsn, keep):
    dzs = [d * e for d, e in zip(dsn, egl)]
    dgl = [jnp.sum(d * s, axis=0, keepdims=True) for d, s in zip(dsn, sn)]
    dyb = [a.astype(BF16) for a in dyw]
    t1 = [_bdot_tn(a[2 * L:], d) for a, d in zip(m, dyb)]
    t2 = [_bdot_nt(b, d) for b, d in zip(yk, dzs)]
    drhs = [_bdot_tn(t, a[:2 * L] + b[:2 * L]) for t, a, b in zip(tinv, t1, t2)]
    dv = [a[2 * L:] + b[2 * L:] + _bdot_tn(c[:2 * L, 2 * L:], d) for a, b, c, d in zip(t1, t2, m, drhs)]
    gg = [jnp.concatenate([a, b], axis=0).astype(BF16) for a, b in zip(drhs, dyw)]
    ds0 = [d + _bdot_tn(g, a) for d, g, a in zip(dzs, gg, x)]
    dm = [_bdot_nt(g, w) * keep for g, w in zip(gg, uv)]
    dx = [_bdot(g, s) + _bdot(d, b) for g, s, d, b in zip(gg, s0, dm, yk)]
    dyk = [_bdot_tn(d, a) + _bdot(w, z) for d, a, w, z in zip(dm, x, uv, dzs)]
    return dx, dyk, dv, dgl, ds0


STATE_SHAPE = (NPAIR * PW, PW)
M_SHAPE = (4 * L, NPAIR * 4 * L)
TINV_SHAPE = (2 * L, NPAIR * 2 * L)


def _rows_of(a, n):
    return [a[i * n:(i + 1) * n, :] for i in range(NPAIR)]


def _both(f):
    out = []
    for s in range(NSEQ):
        out += f(s)
    return out


def _seq_view(a):
    return a.reshape(NSEQ, SEQ, a.shape[-1])


UV_SHAPE = (4 * L, NPAIR * PW)


def rwkv_fwd(r, lw, k2, v, aa, bb):
    def body(r_ref, lw_ref, k2_ref, v_ref, aa_ref, bb_ref, y_ref, hs_ref, hn_ref, m_ref, t_ref, uv_ref, state):
        @pl.when(pl.program_id(0) == 0)
        def _():
            state[...] = jnp.zeros_like(state)

        s_all = state[...]
        hs_ref[0] = s_all
        keep, tri, eye = _chunk_masks()
        sc = [_scaled(r_ref[s], lw_ref[s], k2_ref[s], aa_ref[s], bb_ref[s], tri) for s in range(NSEQ)]
        ops = [_pairs_operands(*sc[s][3:]) for s in range(NSEQ)]
        x, yk = _both(lambda s: ops[s][0]), _both(lambda s: ops[s][1])
        m, tinv = _pairs_matrices(x, yk, keep, eye)
        vw = _both(lambda s: [_stack_pair(a) for a in _pairs(v_ref[s])])
        s0 = _both(lambda s: _rows_of(s_all[s], PW))
        y, sn, uv = _pairs_fwd(x, yk, m, tinv, vw, s0, _both(lambda s: _pairs(sc[s][0][L - 1:L, :])))
        for s in range(NSEQ):
            mine = slice(s * NPAIR, (s + 1) * NPAIR)
            y_ref[s] = jnp.concatenate([_fold_pair(a) for a in y[mine]], axis=-1)
            m_ref[0, s] = jnp.concatenate(m[mine], axis=-1)
            t_ref[0, s] = jnp.concatenate(tinv[mine], axis=-1)
            uv_ref[0, s] = jnp.concatenate(uv[mine], axis=-1)
            s_new = jnp.concatenate(sn[mine], axis=0)
            hn_ref[0, s] = s_new
            state[s] = s_new

    blk = pl.BlockSpec((NSEQ, L, W), lambda c: (0, c, 0))
    per_chunk = lambda shape: pl.BlockSpec((1, NSEQ) + shape, lambda c: (c, 0, 0, 0))
    saved_shapes = [(STATE_SHAPE, F32), (STATE_SHAPE, F32), (M_SHAPE, BF16), (TINV_SHAPE, BF16), (UV_SHAPE, BF16)]
    y, *saved = pl.pallas_call(
        body, grid=(NC,), name="rwkv_fwd",
        in_specs=[blk] * 6,
        out_specs=[blk] + [per_chunk(shape) for shape, _ in saved_shapes],
        out_shape=[jax.ShapeDtypeStruct((NSEQ, SEQ, W), F32)]
        + [jax.ShapeDtypeStruct((NC, NSEQ) + shape, dt) for shape, dt in saved_shapes],
        scratch_shapes=[pltpu.VMEM((NSEQ,) + STATE_SHAPE, F32)],
        compiler_params=_cparams(("arbitrary",)),
    )(*[_seq_view(a) for a in (r, lw, k2, v, aa, bb)])
    return y.reshape(T, W), saved


def rwkv_bwd(r, lw, k2, aa, bb, saved, dy):
    def body(r_ref, lw_ref, k2_ref, aa_ref, bb_ref, hs_ref, hn_ref, m_ref, t_ref, uv_ref, dy_ref,
             dr_ref, dlw_ref, dk2_ref, dv_ref, daa_ref, dbb_ref, dstate):
        @pl.when(pl.program_id(0) == 0)
        def _():
            dstate[...] = jnp.zeros_like(dstate)

        keep, tri, _ = _chunk_masks()
        sc = [_scaled(r_ref[s], lw_ref[s], k2_ref[s], aa_ref[s], bb_ref[s], tri) for s in range(NSEQ)]
        ops = [_pairs_operands(*sc[s][3:]) for s in range(NSEQ)]
        x, yk = _both(lambda s: ops[s][0]), _both(lambda s: ops[s][1])
        m = _both(lambda s: [m_ref[0, s][:, i * 4 * L:(i + 1) * 4 * L] for i in range(NPAIR)])
        tinv = _both(lambda s: [t_ref[0, s][:, i * 2 * L:(i + 1) * 2 * L] for i in range(NPAIR)])
        uv = _both(lambda s: _pairs(uv_ref[0, s]))
        dyw = _both(lambda s: [_stack_pair(a) for a in _pairs(dy_ref[s])])
        s0 = _both(lambda s: _rows_of(hs_ref[0, s], PW))
        sn = _both(lambda s: _rows_of(hn_ref[0, s], PW))
        dsn = _both(lambda s: _rows_of(dstate[s], PW))
        egl = _both(lambda s: _pairs(sc[s][0][L - 1:L, :]))
        dx, dyk, dvw, dgl, ds0 = _pairs_bwd(x, yk, m, tinv, uv, s0, sn, egl, dyw, dsn, keep)
        for s in range(NSEQ):
            mine = slice(s * NPAIR, (s + 1) * NPAIR)
            eg, eng, egp, at, rt, bt, kt = sc[s]
            dstate[s] = jnp.concatenate(ds0[mine], axis=0)
            dv_ref[s] = jnp.concatenate([_fold_pair(a) for a in dvw[mine]], axis=-1)
            dat = jnp.concatenate([_fold_pair(a[:2 * L]) for a in dx[mine]], axis=-1)
            drt = jnp.concatenate([_fold_pair(a[2 * L:]) for a in dx[mine]], axis=-1)
            dbt = jnp.concatenate([_fold_pair(a[:2 * L]) for a in dyk[mine]], axis=-1)
            dkt = jnp.concatenate([_fold_pair(a[2 * L:]) for a in dyk[mine]], axis=-1)
            dg = drt * rt - dbt * bt - dkt * kt
            dg = dg + jnp.where(_iota2(dg.shape, 0) == L - 1, jnp.concatenate(dgl[mine], axis=-1), 0.0)
            dgp = dat * at
            dlw_ref[s] = _hdot_tn(tri, dg + dgp) - dgp
            dr_ref[s] = drt * eg
            daa_ref[s] = dat * egp
            dbb_ref[s] = dbt * eng
            dk2_ref[s] = dkt * eng

    blk = pl.BlockSpec((NSEQ, L, W), lambda c: (0, NC - 1 - c, 0))
    per_chunk = lambda shape: pl.BlockSpec((1, NSEQ) + shape, lambda c: (NC - 1 - c, 0, 0, 0))
    outs = pl.pallas_call(
        body, grid=(NC,), name="rwkv_bwd",
        in_specs=[blk] * 5 + [per_chunk(a.shape[2:]) for a in saved] + [blk],
        out_specs=[blk] * 6,
        out_shape=[jax.ShapeDtypeStruct((NSEQ, SEQ, W), F32)] * 6,
        scratch_shapes=[pltpu.VMEM((NSEQ,) + STATE_SHAPE, F32)],
        compiler_params=_cparams(("arbitrary",)),
    )(*[_seq_view(a) for a in (r, lw, k2, aa, bb)], *saved, _seq_view(dy))
    return [a.reshape(T, W) for a in outs]


def _post_math(y, r, k2, v, ga, o, gb, lng, lnb, rk, bd):
    mu = _headsum(y, bd) * (1.0 / HD)
    yc = y - mu
    var = _headsum(yc * yc, bd) * (1.0 / HD)
    yn = yc * lax.rsqrt(var + GN_EPS) * lng + lnb
    bonus = _headsum(r * k2 * rk, bd) * v
    return (yn + bonus) * _silu(ga), o * _silu(gb)


def even_post(y, r, k2, v, ga, o, gb, lng, lnb, rk):
    tm = 256

    def body(y_ref, r_ref, k2_ref, v_ref, ga_ref, o_ref, gb_ref, lng_ref, lnb_ref, rk_ref, z_ref, zt_ref):
        ya, yb = _post_math(y_ref[...], r_ref[...], k2_ref[...], v_ref[...], ga_ref[...], o_ref[...], gb_ref[...],
                            lng_ref[...], lnb_ref[...], rk_ref[...], _head_blockdiag())
        ya, yb = ya.astype(BF16), yb.astype(BF16)
        z_ref[:, 0:W] = ya
        z_ref[:, W:2 * W] = yb
        zt_ref[0:W, :] = ya.T
        zt_ref[W:2 * W, :] = yb.T

    vec = _const_spec((1, W))
    return pl.pallas_call(
        body, grid=(T // tm,), name="even_post",
        in_specs=[_row_spec(tm, W)] * 7 + [vec] * 3,
        out_specs=[_row_spec(tm, D), _col_spec(D, tm)],
        out_shape=[jax.ShapeDtypeStruct((T, D), BF16), jax.ShapeDtypeStruct((D, T), BF16)],
        compiler_params=_cparams(("parallel",)),
    )(y, r, k2, v, ga, o, gb, lng, lnb, rk)


def even_post_bwd(y, r, k2, v, ga, o, gb, lng, lnb, rk, dh, zt_bf, w_bf, after=None):
    tm = 256
    extra_specs, extra = _after_operand(after)

    def body(y_ref, r_ref, k2_ref, v_ref, ga_ref, o_ref, gb_ref, lng_ref, lnb_ref, rk_ref, dh_ref, zt_ref, w_ref,
             *rest):
        dy_ref, dr_ref, dk2_ref, dv_ref, dga_ref, do_ref, dgb_ref, dlng_ref, dlnb_ref, drk_ref, dw_ref = rest[-11:]
        dzv = _out_proj_back(dh_ref, zt_ref, w_ref, dw_ref)
        bd = _head_blockdiag()
        _, vjp = jax.vjp(lambda *a: _post_math(*a, bd), y_ref[...], r_ref[...], k2_ref[...], v_ref[...], ga_ref[...],
                         o_ref[...], gb_ref[...], lng_ref[...], lnb_ref[...], rk_ref[...])
        dy, dr, dk2, dv, dga, do, dgb, dlng, dlnb, drk = vjp((dzv[:, 0:W], dzv[:, W:2 * W]))
        for ref, val in ((dy_ref, dy), (dr_ref, dr), (dk2_ref, dk2), (dv_ref, dv), (dga_ref, dga), (do_ref, do),
                         (dgb_ref, dgb)):
            ref[...] = val.astype(ref.dtype)

        @pl.when(pl.program_id(0) == 0)
        def _():
            for ref in (dlng_ref, dlnb_ref, drk_ref):
                ref[...] = jnp.zeros_like(ref)

        dlng_ref[...] += dlng
        dlnb_ref[...] += dlnb
        drk_ref[...] += drk

    vec = _const_spec((1, W))
    return pl.pallas_call(
        body, grid=(T // tm,), name="even_post_bwd",
        in_specs=[_row_spec(tm, W)] * 7 + [vec] * 3 + [_row_spec(tm, D), _col_spec(D, tm), _const_spec((D, D))]
        + extra_specs,
        out_specs=[_row_spec(tm, W)] * 7 + [vec] * 3 + [_const_spec((D, D))],
        out_shape=[jax.ShapeDtypeStruct((T, W), dt) for dt in (F32, F32, F32, F32, BF16, F32, BF16)]
        + [jax.ShapeDtypeStruct((1, W), F32)] * 3 + [jax.ShapeDtypeStruct((D, D), F32)],
        compiler_params=_cparams(("arbitrary",), VMEM_BIG),
    )(y, r, k2, v, ga, o, gb, lng, lnb, rk, dh, zt_bf, w_bf, *extra)


PADSEQ = SEQ + LEFT * L
ATT_SCALE = 1.0 / math.sqrt(HD)


def _att_probs(q2, kw, bias, c):
    valid = _iota2((1, BAND), 1) >= (LEFT - c) * L
    s = [jnp.where(valid, _bdot_nt(a, b) * ATT_SCALE + bias[p], NEG) for p, (a, b) in enumerate(zip(q2, kw))]
    e = [jnp.exp(a - jnp.max(a, axis=-1, keepdims=True)) for a in s]
    return [a / jnp.sum(a, axis=-1, keepdims=True) for a in e]


def attention_fwd(q, kpad, vpad, bias):
    def body(q_ref, k_ref, v_ref, b_ref, o_ref):
        c = pl.program_id(1)
        start = pl.multiple_of(c * L, L)
        kw = _pairs(k_ref[pl.ds(start, BAND), :])
        vw = _pairs(v_ref[pl.ds(start, BAND), :])
        q2 = [_stack_pair(a) for a in _pairs(q_ref[...].astype(BF16))]
        p = _att_probs(q2, kw, b_ref[...], c)
        o_ref[...] = jnp.concatenate([_unstack_pair(_bdot(a, b)) for a, b in zip(p, vw)], axis=-1)

    qblk = pl.BlockSpec((L, W), lambda b, c: (b * NC + c, 0))
    kblk = pl.BlockSpec((PADSEQ, W), lambda b, c: (b, 0))
    return pl.pallas_call(
        body, grid=(NSEQ, NC), name="attention_fwd",
        in_specs=[qblk, kblk, kblk, _const_spec((NPAIR, 2 * L, BAND))],
        out_specs=qblk, out_shape=jax.ShapeDtypeStruct((T, W), F32),
        compiler_params=_cparams(("parallel", "arbitrary")),
    )(q, kpad, vpad, bias)


def attention_bwd(q, kpad, vpad, bias, do):
    def body(q_ref, k_ref, v_ref, b_ref, do_ref, dq_ref, dko_ref, dvo_ref, db_ref, dk_ref, dv_ref):
        b = pl.program_id(0)
        c = pl.program_id(1)

        @pl.when(c == 0)
        def _():
            dk_ref[...] = jnp.zeros_like(dk_ref)
            dv_ref[...] = jnp.zeros_like(dv_ref)

        @pl.when((c == 0) & (b == 0))
        def _():
            db_ref[...] = jnp.zeros_like(db_ref)

        start = pl.multiple_of(c * L, L)
        kw = _pairs(k_ref[pl.ds(start, BAND), :])
        vw = _pairs(v_ref[pl.ds(start, BAND), :])
        q2 = [_stack_pair(a) for a in _pairs(q_ref[...].astype(BF16))]
        do2 = [_stack_pair(a) for a in _pairs(do_ref[...].astype(BF16))]
        p = _att_probs(q2, kw, b_ref[...], c)
        dp = [_bdot_nt(a, b) for a, b in zip(do2, vw)]
        ds = [a * (d - jnp.sum(d * a, axis=-1, keepdims=True)) for a, d in zip(p, dp)]
        dss = [(a * ATT_SCALE).astype(BF16) for a in ds]
        dq_ref[...] = jnp.concatenate([_unstack_pair(_bdot(a, b)) for a, b in zip(dss, kw)], axis=-1).astype(BF16)
        dk_ref[pl.ds(start, BAND), :] += jnp.concatenate([_bdot_tn(a, b) for a, b in zip(dss, q2)], axis=-1)
        dv_ref[pl.ds(start, BAND), :] += jnp.concatenate([_bdot_tn(a, b) for a, b in zip(p, do2)], axis=-1)
        for i in range(NPAIR):
            db_ref[i] += ds[i]

        @pl.when(c == NC - 1)
        def _():
            dko_ref[...] = dk_ref[LEFT * L:, :].astype(BF16)
            dvo_ref[...] = dv_ref[LEFT * L:, :].astype(BF16)

    qblk = pl.BlockSpec((L, W), lambda b, c: (b * NC + c, 0))
    kblk = pl.BlockSpec((PADSEQ, W), lambda b, c: (b, 0))
    sblk = pl.BlockSpec((SEQ, W), lambda b, c: (b, 0))
    bblk = _const_spec((NPAIR, 2 * L, BAND))
    return pl.pallas_call(
        body, grid=(NSEQ, NC), name="attention_bwd",
        in_specs=[qblk, kblk, kblk, bblk, qblk],
        out_specs=[qblk, sblk, sblk, bblk],
        out_shape=[jax.ShapeDtypeStruct((T, W), BF16), jax.ShapeDtypeStruct((T, W), BF16),
                   jax.ShapeDtypeStruct((T, W), BF16), jax.ShapeDtypeStruct((NPAIR, 2 * L, BAND), F32)],
        scratch_shapes=[pltpu.VMEM((PADSEQ, W), F32), pltpu.VMEM((PADSEQ, W), F32)],
        compiler_params=_cparams(("arbitrary", "arbitrary"), VMEM_BIG),
    )(q, kpad, vpad, bias, do)


NTAB = 2 * CLIP + 1
EXT = BAND + L


def _ext_onehot():
    n = _iota2((EXT, NTAB), 0)
    m = _iota2((EXT, NTAB), 1)
    return (jnp.clip(BAND - 1 - n, -CLIP, CLIP) + CLIP == m).astype(F32)


def bias_expand(table):
    def body(t_ref, o_ref):
        ext = _hdot_nt(t_ref[...], _ext_onehot())
        for i in range(L):
            s = L - 1 - i
            o_ref[:, i, :] = (pltpu.roll(ext, EXT - s, 1) if s else ext)[:, :BAND]

    return pl.pallas_call(body, name="bias_expand", out_shape=jax.ShapeDtypeStruct((NH, L, BAND), F32))(table)


def bias_grad(dbias):
    def body(d_ref, o_ref):
        acc = jnp.zeros((NH, EXT), F32)
        zpad = jnp.zeros((NH, EXT - BAND), F32)
        for i in range(L):
            s = L - 1 - i
            row = jnp.concatenate([d_ref[:, i, :], zpad], axis=-1)
            acc = acc + (pltpu.roll(row, s, 1) if s else row)
        o_ref[...] = _hdot(acc, _ext_onehot())

    return pl.pallas_call(body, name="bias_grad", out_shape=jax.ShapeDtypeStruct((NH, NTAB), F32))(dbias)


def _group_cols(g):
    return slice(g * SGC, (g + 1) * SGC)


def _sg_norm(gv, lng, lnb):
    gc = gv - jnp.mean(gv, axis=-1, keepdims=True)
    rstd = lax.rsqrt(jnp.mean(gc * gc, axis=-1, keepdims=True) + LN_EPS)
    xhat = gc * rstd
    return xhat, rstd, xhat * lng + lnb


def gmlp_fwd(u, v, gate, lng, lnb, wm_bf, sgb_t):
    def body(u_ref, v_ref, gt_ref, lng_ref, lnb_ref, wm_ref, sb_ref, z_ref, zt_ref):
        _, _, vln = _sg_norm(_gelu(v_ref[...]), lng_ref[...], lnb_ref[...])
        vlb = vln.astype(BF16)
        for g in range(NG):
            cs = _group_cols(g)
            sv = jnp.dot(wm_ref[g], vlb[:, cs], preferred_element_type=F32) + sb_ref[:, g:g + 1]
            zg = (_gelu(u_ref[:, cs]) * sv * _silu(gt_ref[:, cs])).astype(BF16)
            z_ref[:, cs] = zg
            zt_ref[cs, :] = zg.T

    return pl.pallas_call(
        body, grid=(T // SGC,), name="gmlp_fwd",
        in_specs=[_row_spec(SGC, D)] * 3 + [_const_spec((1, D))] * 2 + [_const_spec((NG, SGC, SGC)),
                                                                      _const_spec((SGC, NG))],
        out_specs=[_row_spec(SGC, D), _col_spec(D, SGC)],
        out_shape=[jax.ShapeDtypeStruct((T, D), BF16), jax.ShapeDtypeStruct((D, T), BF16)],
        compiler_params=_cparams(("parallel",)),
    )(u, v, gate, lng, lnb, wm_bf, sgb_t)


GMLP_BWD_CHUNKS = 2


def gmlp_bwd(u, v, gate, lng, lnb, wm_bf, sgb_t, dh, zt_bf, w_bf):
    def body(u_ref, v_ref, gt_ref, lng_ref, lnb_ref, wm_ref, sb_ref, dh_ref, zt_ref, w_ref,
             du_ref, dv_ref, dgt_ref, dlng_ref, dlnb_ref, dwm_ref, dsb_ref, dw_ref):
        @pl.when(pl.program_id(0) == 0)
        def _():
            for ref in (dlng_ref, dlnb_ref, dwm_ref, dsb_ref):
                ref[...] = jnp.zeros_like(ref)

        dz = _out_proj_back(dh_ref, zt_ref, w_ref, dw_ref)
        sel = (_iota2((D, NG), 0) // SGC == _iota2((D, NG), 1)).astype(F32)
        for ch in range(GMLP_BWD_CHUNKS):
            rows = slice(ch * SGC, (ch + 1) * SGC)
            gv, dgv_dv = _gelu_both(v_ref[rows, :])
            xhat, rstd, vln = _sg_norm(gv, lng_ref[...], lnb_ref[...])
            vlb = vln.astype(BF16)
            dvln = []
            dsv_all = []
            for g in range(NG):
                cs = _group_cols(g)
                uu = u_ref[rows, cs]
                gg = gt_ref[rows, cs]
                dzz = dz[rows, cs]
                sv = jnp.dot(wm_ref[g], vlb[:, cs], preferred_element_type=F32) + sb_ref[:, g:g + 1]
                gu, dgu = _gelu_both(uu)
                sg, dsg = _silu_both(gg)
                dzgu = dzz * gu
                dsv = dzgu * sg
                dgt_ref[rows, cs] = (dzgu * sv * dsg).astype(BF16)
                du_ref[rows, cs] = (dzz * sv * sg * dgu).astype(BF16)
                dsb16 = dsv.astype(BF16)
                dvln.append(lax.dot_general(wm_ref[g], dsb16, (((0,), (0,)), ((), ())), preferred_element_type=F32))
                dwm_ref[g] += lax.dot_general(dsb16, vlb[:, cs], (((1,), (1,)), ((), ())),
                                              preferred_element_type=F32)
                dsv_all.append(dsv)
            dvl = jnp.concatenate(dvln, axis=-1)
            dsb_ref[...] += _hdot(jnp.concatenate(dsv_all, axis=-1), sel)
            dlng_ref[...] += jnp.sum(dvl * xhat, axis=0, keepdims=True)
            dlnb_ref[...] += jnp.sum(dvl, axis=0, keepdims=True)
            dxh = dvl * lng_ref[...]
            dgv = rstd * (dxh - jnp.mean(dxh, axis=-1, keepdims=True)
                          - xhat * jnp.mean(dxh * xhat, axis=-1, keepdims=True))
            dv_ref[rows, :] = (dgv * dgv_dv).astype(BF16)

    tm = GMLP_BWD_CHUNKS * SGC
    return pl.pallas_call(
        body, grid=(T // tm,), name="gmlp_bwd",
        in_specs=[_row_spec(tm, D)] * 3 + [_const_spec((1, D))] * 2
        + [_const_spec((NG, SGC, SGC)), _const_spec((SGC, NG)), _row_spec(tm, D), _col_spec(D, tm),
           _const_spec((D, D))],
        out_specs=[_row_spec(tm, D)] * 3 + [_const_spec((1, D))] * 2
        + [_const_spec((NG, SGC, SGC)), _const_spec((SGC, NG)), _const_spec((D, D))],
        out_shape=[jax.ShapeDtypeStruct((T, D), BF16)] * 3 + [jax.ShapeDtypeStruct((1, D), F32)] * 2
        + [jax.ShapeDtypeStruct((NG, SGC, SGC), F32), jax.ShapeDtypeStruct((SGC, NG), F32),
           jax.ShapeDtypeStruct((D, D), F32)],
        compiler_params=_cparams(("arbitrary",), VMEM_BIG),
    )(u, v, gate, lng, lnb, wm_bf, sgb_t, dh, zt_bf, w_bf)


NCHIP = 4
NDEV = 8
ANY = pl.BlockSpec(memory_space=pl.ANY)


HBM = pl.BlockSpec(memory_space=pltpu.HBM)
SEM = pl.BlockSpec(memory_space=pltpu.SEMAPHORE)
EFFECT = pltpu.SideEffectType.DATAFLOW_SIDE_EFFECTING


def _peers(whole_mesh):
    x, y, c = lax.axis_index("x"), lax.axis_index("y"), lax.axis_index("c")
    if not whole_mesh:
        return [((px, py, c), 2 * px + py) for px, py in ((1 - x, y), (x, 1 - y), (1 - x, 1 - y))], 2 * x + y
    out = []
    for j in range(1, NDEV):
        px, py, pc = x ^ (j >> 2), y ^ ((j >> 1) & 1), c ^ (j & 1)
        out.append(((px, py, pc), 4 * px + 2 * py + pc))
    return out, 4 * x + 2 * y + c


def _send_copies(src, land, send, recv, scatter, whole_mesh, starting):
    peers, me = _peers(whole_mesh)
    copies = []
    for t in range(len(src)):
        for j, (dev, slot) in enumerate(peers):
            k = t * len(peers) + j
            copies.append(pltpu.make_async_remote_copy(
                src_ref=src[t].at[slot] if scatter else src[t], dst_ref=land[t].at[me if starting else slot],
                send_sem=send.at[k], recv_sem=recv.at[k], device_id=dev, device_id_type=MESH))
    return copies


def send_start(srcs, lands, scatter, whole_mesh, name):
    n = len(srcs)
    nsem = n * (NDEV - 1 if whole_mesh else NCHIP - 1)

    def body(*refs):
        for cp in _send_copies(refs[:n], refs[n:2 * n], refs[2 * n], refs[2 * n + 1], scatter, whole_mesh, True):
            cp.start()
        refs[-1][...] = jnp.zeros_like(refs[-1])

    arrs = list(srcs) + list(lands)
    out = pl.pallas_call(
        body, name=name,
        out_shape=(pltpu.SemaphoreType.DMA((nsem,)), pltpu.SemaphoreType.DMA((nsem,)),
                   *[pltpu.HBM(a.shape, a.dtype) for a in arrs], jax.ShapeDtypeStruct((8, 128), F32)),
        in_specs=[HBM] * (2 * n), out_specs=(SEM, SEM, *[HBM] * (2 * n), pl.BlockSpec(memory_space=pltpu.VMEM)),
        input_output_aliases={i: 2 + i for i in range(2 * n)},
        compiler_params=pltpu.CompilerParams(has_side_effects=EFFECT),
    )(*[pltpu.with_memory_space_constraint(a, pltpu.HBM) for a in arrs])
    return out[0], out[1], list(out[2:2 + n]), list(out[2 + n:2 + 2 * n]), out[-1]


def send_wait(started, after, scatter, whole_mesh, name):
    send, recv, srcs, lands, _ = started
    n = len(srcs)

    def body(*refs):
        for cp in _send_copies(refs[:n], refs[n:2 * n], refs[2 * n], refs[2 * n + 1], scatter, whole_mesh, False):
            cp.wait_send()
            cp.wait_recv()

    arrs = list(srcs) + list(lands)
    out = pl.pallas_call(
        body, name=name, out_shape=tuple(pltpu.HBM(a.shape, a.dtype) for a in arrs),
        in_specs=[HBM] * (2 * n) + [SEM, SEM, ANY], out_specs=tuple([HBM] * (2 * n)),
        input_output_aliases={i: i for i in range(2 * n)},
        compiler_params=pltpu.CompilerParams(has_side_effects=EFFECT),
    )(*arrs, send, recv, after)
    return list(out[n:])


def exchange_c(arrs, name):
    n = len(arrs)

    def body(*refs):
        ins, outs = refs[:n], refs[n:2 * n]
        send, recv = refs[2 * n:]
        sibling = (lax.axis_index("x"), lax.axis_index("y"), 1 - lax.axis_index("c"))
        copies = [pltpu.make_async_remote_copy(src_ref=ins[t], dst_ref=outs[t], send_sem=send.at[t], recv_sem=recv.at[t],
                                               device_id=sibling, device_id_type=MESH) for t in range(n)]
        for cp in copies:
            cp.start()
        for cp in copies:
            cp.wait()

    return pl.pallas_call(
        body, name=name, in_specs=[ANY] * n, out_specs=[ANY] * n,
        out_shape=[jax.ShapeDtypeStruct(a.shape, a.dtype) for a in arrs],
        scratch_shapes=[pltpu.SemaphoreType.DMA((n,)), pltpu.SemaphoreType.DMA((n,))],
    )(*arrs)


def swap_row_halves(a, name):
    n, rows, cols = a.shape
    half = rows // 2

    def body(in_ref, out_ref, send, recv):
        x, y, c = lax.axis_index("x"), lax.axis_index("y"), lax.axis_index("c")
        cp = pltpu.make_async_remote_copy(src_ref=in_ref.at[:, pl.ds((1 - c) * half, half), :], dst_ref=out_ref,
                                          send_sem=send, recv_sem=recv, device_id=(x, y, 1 - c), device_id_type=MESH)
        cp.start()
        cp.wait()

    return pl.pallas_call(
        body, name=name, in_specs=[ANY], out_specs=ANY, out_shape=jax.ShapeDtypeStruct((n, half, cols), a.dtype),
        scratch_shapes=[pltpu.SemaphoreType.DMA, pltpu.SemaphoreType.DMA],
    )(a)


def add_blocks(a, b, name):
    n, rows, cols = a.shape
    tr = _rows_tile(rows)

    def body(a_ref, b_ref, o_ref):
        o_ref[...] = (a_ref[...].astype(F32) + b_ref[...].astype(F32)).astype(BF16)

    spec = pl.BlockSpec((1, tr, cols), lambda s, i: (s, i, 0))
    return pl.pallas_call(
        body, grid=(n, rows // tr), name=name, in_specs=[spec, spec], out_specs=spec,
        out_shape=jax.ShapeDtypeStruct(a.shape, BF16), compiler_params=_cparams(("parallel", "parallel")),
    )(a, b)


def gather_weights(arrs, split):
    n = len(arrs)

    def body(*refs):
        ins, outs = refs[:n], refs[n:2 * n]
        send1, recv1, send2, recv2, loc = refs[2 * n:]
        x, y, c = lax.axis_index("x"), lax.axis_index("y"), lax.axis_index("c")
        me = 2 * x + y
        sibling = (x, y, 1 - c)
        peers = [(1 - x, y), (x, 1 - y), (1 - x, 1 - y)]

        def rows_of(t, core):
            half = arrs[t].shape[0] // 2
            return pl.ds(core * half, half)

        def part(ref, t, core):
            return ref.at[rows_of(t, core)] if split[t] else ref

        local = [pltpu.make_async_copy(ins[t], outs[t].at[me], loc.at[t]) for t in range(n)]
        for cp in local:
            cp.start()
        first = []
        for t in range(n):
            for j, (px, py) in enumerate(peers):
                first.append(pltpu.make_async_remote_copy(
                    src_ref=part(ins[t], t, c), dst_ref=part(outs[t].at[me], t, c), send_sem=send1.at[t, j],
                    recv_sem=recv1.at[t, j], device_id=(px, py, c), device_id_type=MESH))
        for cp in first:
            cp.start()
        passed = []
        for t in range(n):
            for j, (px, py) in enumerate(peers):
                landed = part(outs[t].at[2 * px + py], t, c)
                pltpu.make_async_remote_copy(
                    src_ref=landed, dst_ref=landed, send_sem=send1.at[t, j], recv_sem=recv1.at[t, j],
                    device_id=(x, y, c), device_id_type=MESH).wait_recv()
                if split[t]:
                    cp = pltpu.make_async_remote_copy(
                        src_ref=landed, dst_ref=landed, send_sem=send2.at[t, j], recv_sem=recv2.at[t, j],
                        device_id=sibling, device_id_type=MESH)
                    cp.start()
                    passed.append(cp)
        for t in range(n):
            for j, (px, py) in enumerate(peers):
                if split[t]:
                    other = part(outs[t].at[2 * px + py], t, 1 - c)
                    pltpu.make_async_remote_copy(
                        src_ref=other, dst_ref=other, send_sem=send2.at[t, j], recv_sem=recv2.at[t, j],
                        device_id=(x, y, c), device_id_type=MESH).wait_recv()
        for cp in first + passed:
            cp.wait_send()
        for cp in local:
            cp.wait()

    return pl.pallas_call(
        body, name="gather_weights", in_specs=[ANY] * n, out_specs=[ANY] * n,
        out_shape=[jax.ShapeDtypeStruct((NCHIP,) + a.shape, a.dtype) for a in arrs],
        scratch_shapes=[pltpu.SemaphoreType.DMA((n, 3))] * 4 + [pltpu.SemaphoreType.DMA((n,))],
    )(*arrs)


def _adam_math(g, w, m, v):
    m = ADAM_B1 * m + (1.0 - ADAM_B1) * g
    v = ADAM_B2 * v + (1.0 - ADAM_B2) * (g * g)
    m_hat = m / (1.0 - ADAM_B1 ** ADAM_STEP)
    v_hat = v / (1.0 - ADAM_B2 ** ADAM_STEP)
    delta = -ADAM_LR * (m_hat / (jnp.sqrt(v_hat) + ADAM_EPS) + ADAM_WD * w)
    return delta, m, v


def _rows_tile(rows):
    return rows if rows <= 256 else 256


def sum_chips(own, parts, name):
    _, rows, cols = parts.shape
    tr = _rows_tile(rows)

    def body(own_ref, p_ref, o_ref):
        acc = own_ref[...].astype(F32)
        for s in range(NCHIP):
            acc = acc + p_ref[s].astype(F32)
        o_ref[...] = acc

    return pl.pallas_call(
        body, grid=(rows // tr,), name=name,
        in_specs=[pl.BlockSpec((tr, cols), lambda i: (i, 0)), pl.BlockSpec((NCHIP, tr, cols), lambda i: (0, i, 0))],
        out_specs=pl.BlockSpec((tr, cols), lambda i: (i, 0)),
        out_shape=jax.ShapeDtypeStruct((rows, cols), F32),
        compiler_params=_cparams(("parallel",)),
    )(own, parts)


def adam_shard(p_mine, p_sib, w, m, v, name):
    rows, cols = p_mine.shape
    tr = _rows_tile(rows)
    lead = w.ndim == 3

    def body(a_ref, b_ref, w_ref, m_ref, v_ref, g_ref, d_ref, mo_ref, vo_ref):
        g = a_ref[...] + b_ref[...]
        g = g[None] if lead else g
        g_ref[...] = g
        d_ref[...], mo_ref[...], vo_ref[...] = _adam_math(g, w_ref[...], m_ref[...], v_ref[...])

    flat = pl.BlockSpec((tr, cols), lambda i: (i, 0))
    spec = pl.BlockSpec((1, tr, cols), lambda i: (0, i, 0)) if lead else flat
    return pl.pallas_call(
        body, grid=(rows // tr,), name=name, in_specs=[flat] * 2 + [spec] * 3, out_specs=[spec] * 4,
        out_shape=[jax.ShapeDtypeStruct(w.shape, F32)] * 4,
        compiler_params=_cparams(("parallel",)),
    )(p_mine, p_sib, w, m, v)


def adam_shard_halves_t(r_mine, r_sib, wt, mt, vt, name):
    hrows, cols = r_mine.shape
    tr = _rows_tile(hrows)
    per_half = hrows // tr

    def body(a_ref, b_ref, w_ref, m_ref, v_ref, g_ref, d_ref, mo_ref, vo_ref):
        mine = pl.program_id(0) == lax.axis_index("c")
        g = jnp.where(mine, a_ref[...], b_ref[...]).T[None]
        g_ref[...] = g
        d_ref[...], mo_ref[...], vo_ref[...] = _adam_math(g, w_ref[...], m_ref[...], v_ref[...])

    flat = pl.BlockSpec((tr, cols), lambda h, i: (i, 0))
    spec = pl.BlockSpec((1, cols, tr), lambda h, i: (0, 0, h * per_half + i))
    return pl.pallas_call(
        body, grid=(2, per_half), name=name, in_specs=[flat] * 2 + [spec] * 3, out_specs=[spec] * 4,
        out_shape=[jax.ShapeDtypeStruct(wt.shape, F32)] * 4,
        compiler_params=_cparams(("parallel", "parallel")),
    )(r_mine, r_sib, wt, mt, vt)


def adam_replicated(parts, w, m, v, name):
    rows = w.shape[0]

    def body(p_ref, w_ref, m_ref, v_ref, g_ref, d_ref, mo_ref, vo_ref):
        g = p_ref[0]
        for d in range(1, NDEV):
            g = g + p_ref[d]
        g_ref[...] = g
        d_ref[...], mo_ref[...], vo_ref[...] = _adam_math(g, w_ref[...], m_ref[...], v_ref[...])

    return pl.pallas_call(
        body, name=name, out_shape=[jax.ShapeDtypeStruct((rows, 128), F32)] * 4,
    )(parts, w, m, v)


def _pack(arrs):
    pieces = []
    for a in arrs:
        flat = a.reshape(-1)
        pad = (-flat.shape[0]) % 128
        pieces.append(jnp.pad(flat, (0, pad)) if pad else flat)
    flat = jnp.concatenate(pieces)
    pad = (-flat.shape[0]) % 1024
    return jnp.pad(flat, (0, pad)).reshape(-1, 128)


def _unpack(buf, shapes):
    flat = buf.reshape(-1)
    out = []
    o = 0
    for s in shapes:
        n = int(np.prod(s))
        out.append(flat[o:o + n].reshape(s))
        o += n + (-n) % 128
    return out


EVEN_SPLITS = (SHIFT, W, W, W, W, W)
ODD_SPLITS = (D, D, D)


def _cols_to_chips(a):
    rows, cols = a.shape
    return a.reshape(rows, NCHIP, cols // NCHIP).transpose(1, 0, 2)


def _chips_to_cols(a):
    _, rows, n = a.shape
    return a.transpose(1, 0, 2).reshape(rows, NCHIP * n)


def kernel(x, norm_g, w_in_e, shift_mu, rw_w0, rw_w2, rw_a0, rw_a2, rw_kk, rw_ka, rw_rk, rw_lnx_g, rw_lnx_b, att_bias, w_out_e, w_in_o, sg_ln_g, sg_ln_b, sg_w, sg_b, w_out_o, final_g, loss_target, m_norm_g, m_w_in_e, m_shift_mu, m_rw_w0, m_rw_w2, m_rw_a0, m_rw_a2, m_rw_kk, m_rw_ka, m_rw_rk, m_rw_lnx_g, m_rw_lnx_b, m_att_bias, m_w_out_e, m_w_in_o, m_sg_ln_g, m_sg_ln_b, m_sg_w, m_sg_b, m_w_out_o, m_final_g, v_norm_g, v_w_in_e, v_shift_mu, v_rw_w0, v_rw_w2, v_rw_a0, v_rw_a2, v_rw_kk, v_rw_ka, v_rw_rk, v_rw_lnx_g, v_rw_lnx_b, v_att_bias, v_w_out_e, v_w_in_o, v_sg_ln_g, v_sg_ln_b, v_sg_w, v_sg_b, v_w_out_o, v_final_g):
    x2 = x.reshape(T, D)
    tgt = loss_target.reshape(T, D)

    my_chip = 2 * lax.axis_index("x") + lax.axis_index("y")
    gathered = gather_weights(
        [jnp.swapaxes(w_in_e[0], 0, 1).astype(BF16), jnp.concatenate([rw_w2[0], rw_a2[0]], axis=0),
         jnp.concatenate([sg_ln_g, sg_ln_b], axis=0)], [True, True, False])
    wie = gathered[0].reshape(EVEN_IN, D)
    w2 = _chips_to_cols(gathered[1][:, :LORA])
    a2 = _chips_to_cols(gathered[1][:, LORA:])
    sglg = _chips_to_cols(gathered[2][:, 0:1])
    sglb = _chips_to_cols(gathered[2][:, 1:2])

    late = [w_out_e[0].astype(BF16), w_in_o[0].astype(BF16), w_out_o[0].astype(BF16)]
    late_started = send_start(late, [jnp.broadcast_to(a[None], (NCHIP,) + a.shape) for a in late], False, False,
                              "late_weights_start")

    def late_weights(after):
        woe, wio, woo = send_wait(late_started, after, False, False, "late_weights_wait")
        return woe.reshape(D, D), _chips_to_cols(wio), woo.reshape(D, D)

    def scatter_start(grads, name):
        srcs = [g_.astype(BF16) if g_.shape[-1] >= W else g_ for g_ in grads]
        return send_start(srcs, [jnp.zeros_like(s) for s in srcs], True, False, name)

    def own_block(g_):
        return lax.dynamic_index_in_dim(g_, my_chip, axis=0, keepdims=False)

    started = {}

    def on_odd_grads(d_woo, d_wio):
        blocks = [d_woo.reshape(NCHIP, D // NCHIP, D), d_wio]
        started["odd"] = (scatter_start(blocks, "odd_grads_start"), [own_block(b) for b in blocks])
        return started["odd"][0][-1]

    def on_even_grads(big_g):
        d_wie, d_woe, _, _, d_w2, d_a2, d_sglg, d_sglb = big_g
        my_half = lax.dynamic_slice_in_dim(d_wie, lax.axis_index("c") * (D // 2), D // 2, axis=1)
        d_wie_half = add_blocks(my_half, swap_row_halves(d_wie, "swap_w_in_e_halves"), "add_w_in_e_halves")
        blocks = [d_wie_half, d_woe.reshape(NCHIP, D // NCHIP, D), _cols_to_chips(d_w2), _cols_to_chips(d_a2),
                  _cols_to_chips(d_sglg), _cols_to_chips(d_sglb)]
        started["even"] = (scatter_start(blocks, "even_grads_start"), [own_block(b) for b in blocks])
        return started["even"][0][-1]

    def on_small_grads(layer, grads):
        mine = _pack(grads)
        started[layer + "_small"] = send_start([mine], [jnp.broadcast_to(mine[None], (NDEV,) + mine.shape)], False,
                                               True, layer + "_small_grads_start")
        return started[layer + "_small"][-1]

    loss_part, dx, _, _ = _local_step(
        x2, tgt, wie, late_weights, w2, a2, sglg, sglb, norm_g, shift_mu, rw_w0, rw_a0, rw_kk, rw_ka, rw_rk,
        rw_lnx_g, rw_lnx_b, att_bias, sg_w, sg_b, final_g, first_after=late_started[-1], on_odd_grads=on_odd_grads,
        on_even_grads=on_even_grads, on_small_grads=on_small_grads)
    even_started, even_own = started["even"]

    wmv = {"w_in_e": tuple(jnp.swapaxes(a, 1, 2) for a in (w_in_e, m_w_in_e, v_w_in_e)),
           "w_out_e": (w_out_e, m_w_out_e, v_w_out_e),
           "w_in_o": (w_in_o, m_w_in_o, v_w_in_o), "w_out_o": (w_out_o, m_w_out_o, v_w_out_o),
           "rw_w2": (rw_w2, m_rw_w2, v_rw_w2), "rw_a2": (rw_a2, m_rw_a2, v_rw_a2),
           "sg_ln_g": (sg_ln_g, m_sg_ln_g, v_sg_ln_g), "sg_ln_b": (sg_ln_b, m_sg_ln_b, v_sg_ln_b)}
    sharded = {}

    def finish(names, own, landed, tag):
        partial = [sum_chips(o_, p_, "sum_" + nm) for o_, p_, nm in zip(own, landed, names)]
        from_sibling = exchange_c(partial, "swap_partials_" + tag)
        for nm, mine, sib in zip(names, partial, from_sibling):
            if nm == "w_in_e":
                res = adam_shard_halves_t(mine, sib, *wmv[nm], "adam_" + nm)
                sharded[nm] = [jnp.swapaxes(a, 1, 2) for a in res]
            else:
                sharded[nm] = adam_shard(mine, sib, *wmv[nm], "adam_" + nm)
        return partial[0]

    odd_started, odd_own = started["odd"]
    odd_landed = send_wait(odd_started, started["even_small"][-1], True, False, "odd_grads_wait")
    done = finish(["w_out_o", "w_in_o"], odd_own, odd_landed, "odd")

    no_w = jnp.zeros((1, 1), F32)
    groups = {
        "odd": (["sg_w", "sg_b", "final_g", "norm_g1"], [sg_w, sg_b, final_g, norm_g[1:2]],
                [m_sg_w, m_sg_b, m_final_g, m_norm_g[1:2]], [v_sg_w, v_sg_b, v_final_g, v_norm_g[1:2]]),
        "even": (["norm_g0", "shift_mu", "rw_w0", "rw_a0", "rw_kk", "rw_ka", "rw_rk", "rw_lnx_g", "rw_lnx_b",
                  "att_bias", "loss"],
                 [norm_g[0:1], shift_mu, rw_w0, rw_a0, rw_kk, rw_ka, rw_rk, rw_lnx_g, rw_lnx_b, att_bias, no_w],
                 [m_norm_g[0:1], m_shift_mu, m_rw_w0, m_rw_a0, m_rw_kk, m_rw_ka, m_rw_rk, m_rw_lnx_g, m_rw_lnx_b,
                  m_att_bias, no_w],
                 [v_norm_g[0:1], v_shift_mu, v_rw_w0, v_rw_a0, v_rw_kk, v_rw_ka, v_rw_rk, v_rw_lnx_g, v_rw_lnx_b,
                  v_att_bias, no_w]),
    }
    rep = {}
    for layer in ("odd", "even"):
        nms, ws, ms_, vs_ = groups[layer]
        (gathered_g,) = send_wait(started[layer + "_small"], done, False, True, layer + "_small_grads_wait")
        rep_out = adam_replicated(gathered_g, _pack(ws), _pack(ms_), _pack(vs_), "adam_" + layer + "_small")
        done = rep_out[0]
        for nm in nms:
            rep[nm] = []
        for buf in rep_out:
            for nm, a in zip(nms, _unpack(buf, [w_.shape for w_ in ws])):
                rep[nm].append(a)
    rep["norm_g"] = [jnp.concatenate([a, b], axis=0) for a, b in zip(rep["norm_g0"], rep["norm_g1"])]
    even_landed = send_wait(even_started, done, True, False, "even_grads_wait")
    finish(["w_in_e", "w_out_e", "rw_w2", "rw_a2", "sg_ln_g", "sg_ln_b"], even_own, even_landed, "even")

    order = ["norm_g", "w_in_e", "shift_mu", "rw_w0", "rw_w2", "rw_a0", "rw_a2", "rw_kk", "rw_ka", "rw_rk",
             "rw_lnx_g", "rw_lnx_b", "att_bias", "w_out_e", "w_in_o", "sg_ln_g", "sg_ln_b", "sg_w", "sg_b",
             "w_out_o", "final_g"]
    results = {**sharded, **rep}
    outs = [rep["loss"][0].reshape(()), dx.reshape(NSEQ, SEQ, D)]
    for kind in range(4):
        outs += [results[nm][kind] for nm in order]
    return tuple(outs)


def _local_step(x2, tgt, wie_t, late_weights, w2, a2, sglg, sglb, norm_g, shift_mu, rw_w0, rw_a0, rw_kk, rw_ka, rw_rk,
                rw_lnx_g, rw_lnx_b, att_bias, sg_w, sg_b, final_g, first_after=None, on_odd_grads=None,
                on_even_grads=None, on_small_grads=None):
    zl = jnp.zeros((LORA, W), F32)
    w2x = jnp.concatenate([w2, zl], axis=0)
    a2x = jnp.concatenate([zl, a2], axis=0)
    rk = rw_rk.reshape(1, W)
    pos = np.arange(SGC)
    sg_mask = jnp.asarray(((pos[None, :] // L) <= (pos[:, None] // L)).astype(np.float32))
    wm = (sg_w[0] * sg_mask[None]).astype(BF16)
    sgb_t = sg_b[0].T

    xn0, ps, ga, q, kb, vb, gb = ln_in_proj(x2, norm_g[0:1], wie_t, EVEN_SPLITS, "in_proj_even", after=first_after,
                                            w_t=True)
    r, lw, k2, v, aa, bb = even_prep(ps, shift_mu, rw_w0, w2x, rw_a0, a2x, rw_kk, rw_ka)
    y, rw_saved = rwkv_fwd(r, lw, k2, v, aa, bb)
    bias = bias_expand(att_bias[0]).reshape(NPAIR, 2 * L, BAND)

    def padded(a):
        return jnp.pad(a.astype(BF16).reshape(NSEQ, SEQ, W), ((0, 0), (LEFT * L, 0), (0, 0))).reshape(NSEQ * PADSEQ, W)

    kpad, vpad = padded(kb), padded(vb)
    o = attention_fwd(q, kpad, vpad, bias)
    z, zt = even_post(y, r, k2, v, ga, o, gb, rw_lnx_g, rw_lnx_b, rk)
    woe, wio, woo = late_weights(z)
    h1 = out_proj(x2, z, woe, "out_proj_even")
    xn1, u, vv, gt = ln_in_proj(h1, norm_g[1:2], wio, ODD_SPLITS, "in_proj_odd")
    z2, z2t = gmlp_fwd(u, vv, gt, sglg, sglb, wm, sgb_t)
    dh2, loss_part, d_final_g = out_proj_loss(h1, z2, woo, final_g[None], tgt)

    du, dvv, dgt, d_sglg, d_sglb, d_wm, d_sgb_t, d_woo = gmlp_bwd(u, vv, gt, sglg, sglb, wm, sgb_t, dh2, z2t, woo)
    dp_odd = [du, dvv, dgt]
    d_wio = matmul_acc_chips(xn1, dp_odd, "in_proj_odd_dw")
    token = on_odd_grads(d_woo, d_wio) if on_odd_grads else None
    dh1, d_g1 = in_proj_bwd_x(h1, norm_g[1:2], wio, dp_odd, dh2, "in_proj_odd_bwd", after=token)
    odd_small = [d_wm * sg_mask[None], d_sgb_t.T, d_final_g, d_g1]
    token = on_small_grads("odd", odd_small) if on_small_grads else None
    dy, dr2, dk22, dv2, dga, do, dgb, d_lng, d_lnb, d_rk, d_woe = even_post_bwd(
        y, r, k2, v, ga, o, gb, rw_lnx_g, rw_lnx_b, rk, dh1, zt, woe, after=token)
    dq, dkb, dvb, dbias = attention_bwd(q, kpad, vpad, bias, do)
    d_att_bias = bias_grad(dbias.reshape(NH, L, BAND))
    dr, dlw, dk2, dv, daa, dbb = rwkv_bwd(r, lw, k2, aa, bb, rw_saved, dy)
    dps, d_mu, d_w0, d_w2x, d_a0, d_a2x, d_kk, d_ka = even_prep_bwd(
        ps, shift_mu, rw_w0, w2x, rw_a0, a2x, rw_kk, rw_ka, dr, dlw, dk2, dv, daa, dbb, dr2, dk22, dv2)
    dp_even = [dps, dga, dq, dkb, dvb, dgb]
    d_wie = matmul_acc_chips(xn0, dp_even, "in_proj_even_dw")
    big_g = (d_wie, d_woe, d_wio, d_woo, d_w2x[:LORA], d_a2x[LORA:], d_sglg, d_sglb)
    token = on_even_grads(big_g) if on_even_grads else None
    dx, d_g0 = in_proj_bwd_x(x2, norm_g[0:1], wie_t, dp_even, dh1, "in_proj_even_bwd", after=token, w_t=True)
    even_small = [d_g0, d_mu, d_w0, d_a0, d_kk, d_ka, d_rk, d_lng, d_lnb, d_att_bias]
    if on_small_grads:
        on_small_grads("even", even_small + [loss_part[0:1, 0:1]])
    rep_g = [jnp.concatenate([d_g0, d_g1], axis=0)] + even_small[1:] + odd_small[:3]
    return loss_part[0, 0], dx, big_g, rep_g
```

```python
import functools
import math

import jax
import jax.numpy as jnp
import numpy as np
from jax import lax
from jax.experimental import pallas as pl
from jax.experimental.pallas import tpu as pltpu

F32 = jnp.float32
BF16 = jnp.bfloat16
HI = lax.Precision.HIGHEST

D = 1024
SEQ = 2048
NSEQ = 2
T = NSEQ * SEQ
HD = 64
NH = 8
W = 512
SHIFT = 1664
LORA = 64
EVEN_IN = 4224
ODD_IN = 3072
L = 64
NC = SEQ // L
LEFT = 8
BAND = (LEFT + 1) * L
CLIP = 128
SGC = 128
NG = 8
RMS_EPS = 1e-6
LN_EPS = 1e-5
GN_EPS = 64e-5
NEG = -1e30
VMEM_BIG = 56 * 1024 * 1024

ADAM_LR = 0.001
ADAM_B1 = 0.9
ADAM_B2 = 0.999
ADAM_EPS = 1e-08
ADAM_WD = 0.01
ADAM_STEP = 10

MESH = pl.DeviceIdType.MESH


def _bdot(a, b):
    return jnp.dot(a.astype(BF16), b.astype(BF16), preferred_element_type=F32)


def _bdot_nt(a, b):
    return lax.dot_general(a.astype(BF16), b.astype(BF16), (((1,), (1,)), ((), ())), preferred_element_type=F32)


def _bdot_tn(a, b):
    return lax.dot_general(a.astype(BF16), b.astype(BF16), (((0,), (0,)), ((), ())), preferred_element_type=F32)


def _hdot(a, b):
    return jnp.dot(a, b, precision=HI, preferred_element_type=F32)


def _hdot_nt(a, b):
    return lax.dot_general(a, b, (((1,), (1,)), ((), ())), precision=HI, preferred_element_type=F32)


def _hdot_tn(a, b):
    return lax.dot_general(a, b, (((0,), (0,)), ((), ())), precision=HI, preferred_element_type=F32)


def _iota2(shape, dim):
    return lax.broadcasted_iota(jnp.int32, shape, dim)


def _head_blockdiag():
    r = _iota2((W, W), 0) // HD
    c = _iota2((W, W), 1) // HD
    return (r == c).astype(BF16)


def _headsum_impl(x, bd):
    hi = x.astype(BF16)
    mid = (x - hi.astype(F32)).astype(BF16)
    return jnp.dot(hi, bd, preferred_element_type=F32) + jnp.dot(mid, bd, preferred_element_type=F32)


@jax.custom_vjp
def _headsum(x, bd):
    return _headsum_impl(x, bd)


def _headsum_fwd(x, bd):
    return _headsum_impl(x, bd), bd


def _headsum_bwd(bd, ct):
    return _headsum_impl(ct, bd), None


_headsum.defvjp(_headsum_fwd, _headsum_bwd)


def _silu(x):
    return x * jax.nn.sigmoid(x)


def _dsilu(x):
    s = jax.nn.sigmoid(x)
    return s * (1.0 + x * (1.0 - s))


_GELU_C = math.sqrt(2.0 / math.pi)


def _gelu(x):
    return 0.5 * x * (1.0 + jnp.tanh(_GELU_C * (x + 0.044715 * (x * x * x))))


def _dgelu(x):
    t = jnp.tanh(_GELU_C * (x + 0.044715 * (x * x * x)))
    return 0.5 * (1.0 + t) + 0.5 * x * (1.0 - t * t) * _GELU_C * (1.0 + 3.0 * 0.044715 * x * x)


def _silu_both(x):
    s = jax.nn.sigmoid(x)
    xs = x * s
    return xs, s + xs * (1.0 - s)


def _gelu_both(x):
    x2 = x * x
    t = jnp.tanh(_GELU_C * (x + 0.044715 * (x2 * x)))
    half = 0.5 * (1.0 + t)
    return x * half, half + 0.5 * x * (1.0 - t * t) * _GELU_C * (1.0 + 3.0 * 0.044715 * x2)


def _softplus(x):
    return jnp.maximum(x, 0.0) + jnp.log(1.0 + jnp.exp(-jnp.abs(x)))


def _cparams(sem, vmem=None):
    return pltpu.CompilerParams(dimension_semantics=sem, vmem_limit_bytes=vmem)


def _row_spec(tm, width):
    return pl.BlockSpec((tm, width), lambda i: (i, 0))


def _col_spec(height, tm):
    return pl.BlockSpec((height, tm), lambda i: (0, i))


def _const_spec(shape):
    nd = len(shape)
    return pl.BlockSpec(shape, lambda *_: (0,) * nd)


def _weight_dims(w_bf, w_t):
    return (((1,), (1,)), ((), ())) if w_t else (((1,), (0,)), ((), ())), w_bf.shape[0 if w_t else 1]


def ln_in_proj(x, g, w_bf, splits, name, after=None, w_t=False):
    dims, n = _weight_dims(w_bf, w_t)
    tm = 256
    spans = []
    o = 0
    for s in splits:
        spans.append((o, o + s))
        o += s
    assert o == n
    extra_specs, extra = _after_operand(after)

    def body(x_ref, g_ref, w_ref, *rest):
        xn_ref, outs = rest[len(extra)], rest[len(extra) + 1:]
        xv = x_ref[...]
        rstd = lax.rsqrt(jnp.mean(xv * xv, axis=-1, keepdims=True) + RMS_EPS)
        xn = (xv * rstd * g_ref[...]).astype(BF16)
        xn_ref[...] = xn.T
        p = lax.dot_general(xn, w_ref[...], dims, preferred_element_type=F32)
        for o_ref, (a, b) in zip(outs, spans):
            o_ref[...] = p[:, a:b]

    return pl.pallas_call(
        body, grid=(T // tm,), name=name,
        in_specs=[_row_spec(tm, D), _const_spec((1, D)), _const_spec(w_bf.shape)] + extra_specs,
        out_specs=[_col_spec(D, tm)] + [_row_spec(tm, s) for s in splits],
        out_shape=[jax.ShapeDtypeStruct((D, T), BF16)] + [jax.ShapeDtypeStruct((T, s), F32) for s in splits],
        compiler_params=_cparams(("parallel",), VMEM_BIG),
    )(x, g, w_bf, *extra)


def in_proj_bwd_x(x, g, w_bf, dps, dres, name, after=None, w_t=False):
    tm = 512
    back = (((1,), (0,)), ((), ())) if w_t else (((1,), (1,)), ((), ()))
    widths = [d.shape[1] for d in dps]
    extra_specs, extra = _after_operand(after)

    def body(x_ref, g_ref, w_ref, dres_ref, *rest):
        dp_refs = rest[:len(widths)]
        dx_ref, dg_ref = rest[-2:]
        dp = jnp.concatenate([r[...] for r in dp_refs], axis=-1)
        dxn = lax.dot_general(dp, w_ref[...], back, preferred_element_type=F32)
        xv = x_ref[...]
        rstd = lax.rsqrt(jnp.mean(xv * xv, axis=-1, keepdims=True) + RMS_EPS)
        xhat = xv * rstd
        dgp = jnp.sum(dxn * xhat, axis=0, keepdims=True)

        @pl.when(pl.program_id(0) == 0)
        def _():
            dg_ref[...] = jnp.zeros_like(dg_ref)

        dg_ref[...] += dgp
        dxh = dxn * g_ref[...]
        dx_ref[...] = dres_ref[...] + rstd * (dxh - xhat * jnp.mean(dxh * xhat, axis=-1, keepdims=True))

    return pl.pallas_call(
        body, grid=(T // tm,), name=name,
        in_specs=[_row_spec(tm, D), _const_spec((1, D)), _const_spec(w_bf.shape), _row_spec(tm, D)]
        + [_row_spec(tm, s) for s in widths] + extra_specs,
        out_specs=[_row_spec(tm, D), _const_spec((1, D))],
        out_shape=[jax.ShapeDtypeStruct((T, D), F32), jax.ShapeDtypeStruct((1, D), F32)],
        compiler_params=_cparams(("arbitrary",), VMEM_BIG),
    )(x, g, w_bf, dres, *dps, *extra)


def _after_operand(after):
    return ([ANY], [after]) if after is not None else ([], [])


def matmul_acc_chips(at_bf, pieces, name, after=None):
    k = at_bf.shape[0]
    widths = [p.shape[1] for p in pieces]
    nb = sum(widths) // NCHIP
    tm = 512
    steps = T // tm
    extra_specs, extra = _after_operand(after)

    def body(a_ref, *rest):
        o_ref, acc = rest[-2:]

        @pl.when(pl.program_id(0) == 0)
        def _():
            acc[...] = jnp.zeros_like(acc)

        a = a_ref[...]
        b = jnp.concatenate([r[...] for r in rest[:len(widths)]], axis=-1)
        for s in range(NCHIP):
            acc[s] += jnp.dot(a, b[:, s * nb:(s + 1) * nb], preferred_element_type=F32)

        @pl.when(pl.program_id(0) == steps - 1)
        def _():
            o_ref[...] = acc[...].astype(BF16)

    return pl.pallas_call(
        body, grid=(steps,), name=name,
        in_specs=[_col_spec(k, tm)] + [_row_spec(tm, w_) for w_ in widths] + extra_specs,
        out_specs=_const_spec((NCHIP, k, nb)),
        out_shape=jax.ShapeDtypeStruct((NCHIP, k, nb), BF16),
        scratch_shapes=[pltpu.VMEM((NCHIP, k, nb), F32)],
        compiler_params=_cparams(("arbitrary",), VMEM_BIG),
    )(at_bf, *pieces, *extra)


def out_proj(h, z_bf, w_bf, name):
    tm = 512

    def body(h_ref, z_ref, w_ref, o_ref):
        o_ref[...] = h_ref[...] + jnp.dot(z_ref[...], w_ref[...], preferred_element_type=F32)

    return pl.pallas_call(
        body, grid=(T // tm,), name=name,
        in_specs=[_row_spec(tm, D), _row_spec(tm, D), _const_spec((D, D))],
        out_specs=_row_spec(tm, D), out_shape=jax.ShapeDtypeStruct((T, D), F32),
        compiler_params=_cparams(("parallel",)),
    )(h, z_bf, w_bf)


def _out_proj_back(dh_ref, zt_ref, w_ref, dw_ref):
    dhb = dh_ref[...].astype(BF16)

    @pl.when(pl.program_id(0) == 0)
    def _():
        dw_ref[...] = jnp.zeros_like(dw_ref)

    dw_ref[...] += jnp.dot(zt_ref[...], dhb, preferred_element_type=F32)
    return lax.dot_general(dhb, w_ref[...], (((1,), (1,)), ((), ())), preferred_element_type=F32)


def out_proj_loss(h, z_bf, w_bf, g, target):
    tm = 512

    def body(h_ref, z_ref, w_ref, g_ref, t_ref, dh_ref, loss_ref, dg_ref):
        xv = h_ref[...] + jnp.dot(z_ref[...], w_ref[...], preferred_element_type=F32)
        rstd = lax.rsqrt(jnp.mean(xv * xv, axis=-1, keepdims=True) + RMS_EPS)
        xhat = xv * rstd
        err = xhat * g_ref[...] - t_ref[...]
        part = 0.5 * jnp.sum(jnp.mean(err * err, axis=-1, keepdims=True), axis=0, keepdims=True)
        dout = err * (1.0 / D)

        @pl.when(pl.program_id(0) == 0)
        def _():
            loss_ref[...] = jnp.zeros_like(loss_ref)
            dg_ref[...] = jnp.zeros_like(dg_ref)

        loss_ref[...] += jnp.broadcast_to(part, loss_ref.shape)
        dg_ref[...] += jnp.sum(dout * xhat, axis=0, keepdims=True)
        dxh = dout * g_ref[...]
        dh_ref[...] = rstd * (dxh - xhat * jnp.mean(dxh * xhat, axis=-1, keepdims=True))

    return pl.pallas_call(
        body, grid=(T // tm,), name="out_proj_loss",
        in_specs=[_row_spec(tm, D), _row_spec(tm, D), _const_spec((D, D)), _const_spec((1, D)), _row_spec(tm, D)],
        out_specs=[_row_spec(tm, D), _const_spec((8, 128)), _const_spec((1, D))],
        out_shape=[jax.ShapeDtypeStruct((T, D), F32), jax.ShapeDtypeStruct((8, 128), F32),
                   jax.ShapeDtypeStruct((1, D), F32)],
        compiler_params=_cparams(("arbitrary",)),
    )(h, z_bf, w_bf, g, target)


PREP_TM = 256
PREP_NB = SEQ // PREP_TM


def _prep_elem(k, wl, apre, kkw, kaw, bd):
    wraw = -_softplus(-wl) - 0.5
    lw = -jnp.exp(wraw)
    asig = jax.nn.sigmoid(apre)
    kkr = k * kkw
    nrm = jnp.maximum(jnp.sqrt(_headsum(kkr * kkr, bd)), 1e-12)
    kk = kkr / nrm
    k2 = k * (1.0 + (asig - 1.0) * kaw)
    return lw, k2, -kk, kk * asig


def _shifted(ps_ref, prev_ref, mu, blk):
    p = ps_ref[...]
    first = (blk % PREP_NB) == 0
    prev_row = jnp.where(first, 0.0, prev_ref[7:8, :])
    rolled = pltpu.roll(p, 1, 0)
    p_prev = jnp.where(_iota2(p.shape, 0) == 0, prev_row, rolled)
    return p, p_prev, p + (p_prev - p) * mu


def _prev_spec(width, blk_of):
    return pl.BlockSpec((8, width), lambda i: (jnp.maximum(blk_of(i) * (PREP_TM // 8) - 1, 0), 0))


def even_prep(ps, mu, w0, w2x, a0, a2x, kkw, kaw):
    tm = PREP_TM

    def body(ps_ref, prev_ref, mu_ref, w0_ref, w2_ref, a0_ref, a2_ref, kk_ref, ka_ref,
             r_ref, lw_ref, k2_ref, v_ref, aa_ref, bb_ref):
        _, _, s = _shifted(ps_ref, prev_ref, mu_ref[...], pl.program_id(0))
        wa = s[:, 3 * W:]
        wl = w0_ref[...] + _bdot(jnp.tanh(wa), w2_ref[...])
        apre = a0_ref[...] + _bdot(wa, a2_ref[...])
        lw, k2, aa, bb = _prep_elem(s[:, W:2 * W], wl, apre, kk_ref[...], ka_ref[...], _head_blockdiag())
        r_ref[...] = s[:, 0:W]
        v_ref[...] = s[:, 2 * W:3 * W]
        lw_ref[...] = lw
        k2_ref[...] = k2
        aa_ref[...] = aa
        bb_ref[...] = bb

    vec = _const_spec((1, W))
    return pl.pallas_call(
        body, grid=(T // tm,), name="even_prep",
        in_specs=[_row_spec(tm, SHIFT), _prev_spec(SHIFT, lambda i: i), _const_spec((1, SHIFT)), vec,
                  _const_spec((2 * LORA, W)), vec, _const_spec((2 * LORA, W)), vec, vec],
        out_specs=[_row_spec(tm, W)] * 6,
        out_shape=[jax.ShapeDtypeStruct((T, W), F32)] * 6,
        compiler_params=_cparams(("parallel",)),
    )(ps, ps, mu, w0, w2x, a0, a2x, kkw, kaw)


def even_prep_bwd(ps, mu, w0, w2x, a0, a2x, kkw, kaw, dr, dlw, dk2, dv, daa, dbb, dr2, dk22, dv2):
    tm = PREP_TM
    nb = T // tm
    rev = lambda i: nb - 1 - i

    def body(ps_ref, prev_ref, mu_ref, w0_ref, w2_ref, a0_ref, a2_ref, kk_ref, ka_ref,
             dr_ref, dlw_ref, dk2_ref, dv_ref, daa_ref, dbb_ref, dr2_ref, dk22_ref, dv2_ref,
             dps_ref, dmu_ref, dw0_ref, dw2_ref, da0_ref, da2_ref, dkk_ref, dka_ref, carry):
        i = pl.program_id(0)
        blk = rev(i)
        mu_v = mu_ref[...]
        p, p_prev, s = _shifted(ps_ref, prev_ref, mu_v, blk)
        wa = s[:, 3 * W:]
        th = jnp.tanh(wa)
        wl = w0_ref[...] + _bdot(th, w2_ref[...])
        apre = a0_ref[...] + _bdot(wa, a2_ref[...])
        bd = _head_blockdiag()
        k = s[:, W:2 * W]
        _, vjp = jax.vjp(lambda k_, wl_, ap_, kkw_, kaw_: _prep_elem(k_, wl_, ap_, kkw_, kaw_, bd),
                         k, wl, apre, kk_ref[...], ka_ref[...])
        dk, dwl, dap, dkkw, dkaw = vjp((dlw_ref[...], dk2_ref[...] + dk22_ref[...], daa_ref[...], dbb_ref[...]))
        dwa = _bdot_nt(dwl, w2_ref[...]) * (1.0 - th * th) + _bdot_nt(dap, a2_ref[...])
        ds = jnp.concatenate([dr_ref[...] + dr2_ref[...], dk, dv_ref[...] + dv2_ref[...], dwa], axis=-1)

        @pl.when(i == 0)
        def _():
            for ref in (dmu_ref, dw0_ref, dw2_ref, da0_ref, da2_ref, dkk_ref, dka_ref, carry):
                ref[...] = jnp.zeros_like(ref)

        dmu_ref[...] += jnp.sum(ds * (p_prev - p), axis=0, keepdims=True)
        dw0_ref[...] += jnp.sum(dwl, axis=0, keepdims=True)
        da0_ref[...] += jnp.sum(dap, axis=0, keepdims=True)
        dw2_ref[...] += _bdot_tn(th, dwl)
        da2_ref[...] += _bdot_tn(wa, dap)
        dkk_ref[...] += dkkw
        dka_ref[...] += dkaw
        dsm = ds * mu_v
        last = (blk % PREP_NB) == PREP_NB - 1
        nxt = jnp.where(last, 0.0, carry[0:1, :])
        up = pltpu.roll(dsm, tm - 1, 0)
        up = jnp.where(_iota2(up.shape, 0) == tm - 1, nxt, up)
        dps_ref[...] = (ds - dsm + up).astype(BF16)
        carry[0:1, :] = dsm[0:1, :]

    vec = _const_spec((1, W))
    rrow = lambda width: pl.BlockSpec((tm, width), lambda i: (rev(i), 0))
    return pl.pallas_call(
        body, grid=(nb,), name="even_prep_bwd",
        in_specs=[rrow(SHIFT), _prev_spec(SHIFT, rev), _const_spec((1, SHIFT)), vec,
                  _const_spec((2 * LORA, W)), vec, _const_spec((2 * LORA, W)), vec, vec] + [rrow(W)] * 9,
        out_specs=[rrow(SHIFT), _const_spec((1, SHIFT)), vec, _const_spec((2 * LORA, W)), vec,
                   _const_spec((2 * LORA, W)), vec, vec],
        out_shape=[jax.ShapeDtypeStruct((T, SHIFT), BF16), jax.ShapeDtypeStruct((1, SHIFT), F32),
                   jax.ShapeDtypeStruct((1, W), F32), jax.ShapeDtypeStruct((2 * LORA, W), F32),
                   jax.ShapeDtypeStruct((1, W), F32), jax.ShapeDtypeStruct((2 * LORA, W), F32),
                   jax.ShapeDtypeStruct((1, W), F32), jax.ShapeDtypeStruct((1, W), F32)],
        scratch_shapes=[pltpu.VMEM((8, SHIFT), F32)],
        compiler_params=_cparams(("arbitrary",), VMEM_BIG),
    )(ps, ps, mu, w0, w2x, a0, a2x, kkw, kaw, dr, dlw, dk2, dv, daa, dbb, dr2, dk22, dv2)


NPAIR = NH // 2
PW = 2 * HD


def _pair_cols(p):
    return slice(p * PW, (p + 1) * PW)


def _pairs(a):
    return [a[:, _pair_cols(p)] for p in range(NPAIR)]


def _stack_pair(a):
    first = _iota2(a.shape, 1) < HD
    zero = jnp.zeros_like(a)
    return jnp.concatenate([jnp.where(first, a, zero), jnp.where(first, zero, a)], axis=0)


def _unstack_pair(a):
    n = a.shape[0] // 2
    return jnp.where(_iota2((n, PW), 1) < HD, a[:n], a[n:])


def _fold_pair(a):
    n = a.shape[0] // 2
    return a[:n] + a[n:]


def _chunk_masks():
    n = 4 * L
    row = _iota2((n, n), 0)
    col = _iota2((n, n), 1)
    same = ((row // L) & 1) == ((col // L) & 1)
    ri = row & (L - 1)
    ci = col & (L - 1)
    keep = same & (((row < 2 * L) & (ri > ci)) | ((row >= 2 * L) & (ri >= ci)))
    r1 = _iota2((L, L), 0)
    c1 = _iota2((L, L), 1)
    r2 = _iota2((2 * L, 2 * L), 0)
    c2 = _iota2((2 * L, 2 * L), 1)
    return keep.astype(F32), (r1 >= c1).astype(F32), (r2 == c2).astype(F32)


def _scaled(r, lw, k2, aa, bb, tri):
    g = _hdot(tri, lw)
    eg = jnp.exp(g)
    eng = jnp.exp(-g)
    egp = jnp.exp(g - lw)
    return eg, eng, egp, aa * egp, r * eg, bb * eng, k2 * eng


def _head_cols(h):
    return slice(h * HD, (h + 1) * HD)


def _per_head(a):
    return [a[:, _head_cols(h)] for h in range(NH)]


def _pairs_operands(at, rt, bt, kt):
    x = [jnp.concatenate([_stack_pair(a), _stack_pair(r)], axis=0).astype(BF16) for a, r in zip(_pairs(at), _pairs(rt))]
    yk = [jnp.concatenate([_stack_pair(b), _stack_pair(k)], axis=0).astype(BF16) for b, k in zip(_pairs(bt), _pairs(kt))]
    return x, yk


def _pairs_matrices(x, yk, keep, eye):
    m = [_bdot_nt(a, b) * keep for a, b in zip(x, yk)]
    p = [a[:2 * L, :2 * L] for a in m]
    tinv = [eye + a for a in p]
    for _ in range(5):
        p = [_bdot(a, a) for a in p]
        tinv = [t + _bdot(t, a) for t, a in zip(tinv, p)]
    return [a.astype(BF16) for a in m], [a.astype(BF16) for a in tinv]


def _pairs_fwd(x, yk, m, tinv, vw, s0, egl):
    xh = [_bdot_nt(a, s) for a, s in zip(x, s0)]
    u = [_bdot(t, h[:2 * L] + _bdot(a[:2 * L, 2 * L:], w)) for t, h, a, w in zip(tinv, xh, m, vw)]
    uv = [jnp.concatenate([a, w], axis=0).astype(BF16) for a, w in zip(u, vw)]
    y = [h[2 * L:] + _bdot(a[2 * L:], w) for h, a, w in zip(xh, m, uv)]
    sn = [e * (s + _bdot_tn(w, b)) for e, s, w, b in zip(egl, s0, uv, yk)]
    return y, sn, uv


def _pairs_bwd(x, yk, m, tinv, uv, s0, sn, egl, dyw, dsn, keep):
    dzs = [d * e for d, e in zip(dsn, egl)]
    dgl = [jnp.sum(d * s, axis=0, keepdims=True) for d, s in zip(dsn, sn)]
    dyb = [a.astype(BF16) for a in dyw]
    t1 = [_bdot_tn(a[2 * L:], d) for a, d in zip(m, dyb)]
    t2 = [_bdot_nt(b, d) for b, d in zip(yk, dzs)]
    drhs = [_bdot_tn(t, a[:2 * L] + b[:2 * L]) for t, a, b in zip(tinv, t1, t2)]
    dv = [a[2 * L:] + b[2 * L:] + _bdot_tn(c[:2 * L, 2 * L:], d) for a, b, c, d in zip(t1, t2, m, drhs)]
    gg = [jnp.concatenate([a, b], axis=0).astype(BF16) for a, b in zip(drhs, dyw)]
    ds0 = [d + _bdot_tn(g, a) for d, g, a in zip(dzs, gg, x)]
    dm = [_bdot_nt(g, w) * keep for g, w in zip(gg, uv)]
    dx = [_bdot(g, s) + _bdot(d, b) for g, s, d, b in zip(gg, s0, dm, yk)]
    dyk = [_bdot_tn(d, a) + _bdot(w, z) for d, a, w, z in zip(dm, x, uv, dzs)]
    return dx, dyk, dv, dgl, ds0


STATE_SHAPE = (NPAIR * PW, PW)
M_SHAPE = (4 * L, NPAIR * 4 * L)
TINV_SHAPE = (2 * L, NPAIR * 2 * L)


def _rows_of(a, n):
    return [a[i * n:(i + 1) * n, :] for i in range(NPAIR)]


def _both(f):
    out = []
    for s in range(NSEQ):
        out += f(s)
    return out


def _seq_view(a):
    return a.reshape(NSEQ, SEQ, a.shape[-1])


UV_SHAPE = (4 * L, NPAIR * PW)


def rwkv_fwd(r, lw, k2, v, aa, bb):
    def body(r_ref, lw_ref, k2_ref, v_ref, aa_ref, bb_ref, y_ref, hs_ref, hn_ref, m_ref, t_ref, uv_ref, state):
        @pl.when(pl.program_id(0) == 0)
        def _():
            state[...] = jnp.zeros_like(state)

        s_all = state[...]
        hs_ref[0] = s_all
        keep, tri, eye = _chunk_masks()
        sc = [_scaled(r_ref[s], lw_ref[s], k2_ref[s], aa_ref[s], bb_ref[s], tri) for s in range(NSEQ)]
        ops = [_pairs_operands(*sc[s][3:]) for s in range(NSEQ)]
        x, yk = _both(lambda s: ops[s][0]), _both(lambda s: ops[s][1])
        m, tinv = _pairs_matrices(x, yk, keep, eye)
        vw = _both(lambda s: [_stack_pair(a) for a in _pairs(v_ref[s])])
        s0 = _both(lambda s: _rows_of(s_all[s], PW))
        y, sn, uv = _pairs_fwd(x, yk, m, tinv, vw, s0, _both(lambda s: _pairs(sc[s][0][L - 1:L, :])))
        for s in range(NSEQ):
            mine = slice(s * NPAIR, (s + 1) * NPAIR)
            y_ref[s] = jnp.concatenate([_fold_pair(a) for a in y[mine]], axis=-1)
            m_ref[0, s] = jnp.concatenate(m[mine], axis=-1)
            t_ref[0, s] = jnp.concatenate(tinv[mine], axis=-1)
            uv_ref[0, s] = jnp.concatenate(uv[mine], axis=-1)
            s_new = jnp.concatenate(sn[mine], axis=0)
            hn_ref[0, s] = s_new
            state[s] = s_new

    blk = pl.BlockSpec((NSEQ, L, W), lambda c: (0, c, 0))
    per_chunk = lambda shape: pl.BlockSpec((1, NSEQ) + shape, lambda c: (c, 0, 0, 0))
    saved_shapes = [(STATE_SHAPE, F32), (STATE_SHAPE, F32), (M_SHAPE, BF16), (TINV_SHAPE, BF16), (UV_SHAPE, BF16)]
    y, *saved = pl.pallas_call(
        body, grid=(NC,), name="rwkv_fwd",
        in_specs=[blk] * 6,
        out_specs=[blk] + [per_chunk(shape) for shape, _ in saved_shapes],
        out_shape=[jax.ShapeDtypeStruct((NSEQ, SEQ, W), F32)]
        + [jax.ShapeDtypeStruct((NC, NSEQ) + shape, dt) for shape, dt in saved_shapes],
        scratch_shapes=[pltpu.VMEM((NSEQ,) + STATE_SHAPE, F32)],
        compiler_params=_cparams(("arbitrary",)),
    )(*[_seq_view(a) for a in (r, lw, k2, v, aa, bb)])
    return y.reshape(T, W), saved


def rwkv_bwd(r, lw, k2, aa, bb, saved, dy):
    def body(r_ref, lw_ref, k2_ref, aa_ref, bb_ref, hs_ref, hn_ref, m_ref, t_ref, uv_ref, dy_ref,
             dr_ref, dlw_ref, dk2_ref, dv_ref, daa_ref, dbb_ref, dstate):
        @pl.when(pl.program_id(0) == 0)
        def _():
            dstate[...] = jnp.zeros_like(dstate)

        keep, tri, _ = _chunk_masks()
        sc = [_scaled(r_ref[s], lw_ref[s], k2_ref[s], aa_ref[s], bb_ref[s], tri) for s in range(NSEQ)]
        ops = [_pairs_operands(*sc[s][3:]) for s in range(NSEQ)]
        x, yk = _both(lambda s: ops[s][0]), _both(lambda s: ops[s][1])
        m = _both(lambda s: [m_ref[0, s][:, i * 4 * L:(i + 1) * 4 * L] for i in range(NPAIR)])
        tinv = _both(lambda s: [t_ref[0, s][:, i * 2 * L:(i + 1) * 2 * L] for i in range(NPAIR)])
        uv = _both(lambda s: _pairs(uv_ref[0, s]))
        dyw = _both(lambda s: [_stack_pair(a) for a in _pairs(dy_ref[s])])
        s0 = _both(lambda s: _rows_of(hs_ref[0, s], PW))
        sn = _both(lambda s: _rows_of(hn_ref[0, s], PW))
        dsn = _both(lambda s: _rows_of(dstate[s], PW))
        egl = _both(lambda s: _pairs(sc[s][0][L - 1:L, :]))
        dx, dyk, dvw, dgl, ds0 = _pairs_bwd(x, yk, m, tinv, uv, s0, sn, egl, dyw, dsn, keep)
        for s in range(NSEQ):
            mine = slice(s * NPAIR, (s + 1) * NPAIR)
            eg, eng, egp, at, rt, bt, kt = sc[s]
            dstate[s] = jnp.concatenate(ds0[mine], axis=0)
            dv_ref[s] = jnp.concatenate([_fold_pair(a) for a in dvw[mine]], axis=-1)
            dat = jnp.concatenate([_fold_pair(a[:2 * L]) for a in dx[mine]], axis=-1)
            drt = jnp.concatenate([_fold_pair(a[2 * L:]) for a in dx[mine]], axis=-1)
            dbt = jnp.concatenate([_fold_pair(a[:2 * L]) for a in dyk[mine]], axis=-1)
            dkt = jnp.concatenate([_fold_pair(a[2 * L:]) for a in dyk[mine]], axis=-1)
            dg = drt * rt - dbt * bt - dkt * kt
            dg = dg + jnp.where(_iota2(dg.shape, 0) == L - 1, jnp.concatenate(dgl[mine], axis=-1), 0.0)
            dgp = dat * at
            dlw_ref[s] = _hdot_tn(tri, dg + dgp) - dgp
            dr_ref[s] = drt * eg
            daa_ref[s] = dat * egp
            dbb_ref[s] = dbt * eng
            dk2_ref[s] = dkt * eng

    blk = pl.BlockSpec((NSEQ, L, W), lambda c: (0, NC - 1 - c, 0))
    per_chunk = lambda shape: pl.BlockSpec((1, NSEQ) + shape, lambda c: (NC - 1 - c, 0, 0, 0))
    outs = pl.pallas_call(
        body, grid=(NC,), name="rwkv_bwd",
        in_specs=[blk] * 5 + [per_chunk(a.shape[2:]) for a in saved] + [blk],
        out_specs=[blk] * 6,
        out_shape=[jax.ShapeDtypeStruct((NSEQ, SEQ, W), F32)] * 6,
        scratch_shapes=[pltpu.VMEM((NSEQ,) + STATE_SHAPE, F32)],
        compiler_params=_cparams(("arbitrary",)),
    )(*[_seq_view(a) for a in (r, lw, k2, aa, bb)], *saved, _seq_view(dy))
    return [a.reshape(T, W) for a in outs]


def _post_math(y, r, k2, v, ga, o, gb, lng, lnb, rk, bd):
    mu = _headsum(y, bd) * (1.0 / HD)
    yc = y - mu
    var = _headsum(yc * yc, bd) * (1.0 / HD)
    yn = yc * lax.rsqrt(var + GN_EPS) * lng + lnb
    bonus = _headsum(r * k2 * rk, bd) * v
    return (yn + bonus) * _silu(ga), o * _silu(gb)


def even_post(y, r, k2, v, ga, o, gb, lng, lnb, rk):
    tm = 256

    def body(y_ref, r_ref, k2_ref, v_ref, ga_ref, o_ref, gb_ref, lng_ref, lnb_ref, rk_ref, z_ref, zt_ref):
        ya, yb = _post_math(y_ref[...], r_ref[...], k2_ref[...], v_ref[...], ga_ref[...], o_ref[...], gb_ref[...],
                            lng_ref[...], lnb_ref[...], rk_ref[...], _head_blockdiag())
        ya, yb = ya.astype(BF16), yb.astype(BF16)
        z_ref[:, 0:W] = ya
        z_ref[:, W:2 * W] = yb
        zt_ref[0:W, :] = ya.T
        zt_ref[W:2 * W, :] = yb.T

    vec = _const_spec((1, W))
    return pl.pallas_call(
        body, grid=(T // tm,), name="even_post",
        in_specs=[_row_spec(tm, W)] * 7 + [vec] * 3,
        out_specs=[_row_spec(tm, D), _col_spec(D, tm)],
        out_shape=[jax.ShapeDtypeStruct((T, D), BF16), jax.ShapeDtypeStruct((D, T), BF16)],
        compiler_params=_cparams(("parallel",)),
    )(y, r, k2, v, ga, o, gb, lng, lnb, rk)


def even_post_bwd(y, r, k2, v, ga, o, gb, lng, lnb, rk, dh, zt_bf, w_bf, after=None):
    tm = 256
    extra_specs, extra = _after_operand(after)

    def body(y_ref, r_ref, k2_ref, v_ref, ga_ref, o_ref, gb_ref, lng_ref, lnb_ref, rk_ref, dh_ref, zt_ref, w_ref,
             *rest):
        dy_ref, dr_ref, dk2_ref, dv_ref, dga_ref, do_ref, dgb_ref, dlng_ref, dlnb_ref, drk_ref, dw_ref = rest[-11:]
        dzv = _out_proj_back(dh_ref, zt_ref, w_ref, dw_ref)
        bd = _head_blockdiag()
        _, vjp = jax.vjp(lambda *a: _post_math(*a, bd), y_ref[...], r_ref[...], k2_ref[...], v_ref[...], ga_ref[...],
                         o_ref[...], gb_ref[...], lng_ref[...], lnb_ref[...], rk_ref[...])
        dy, dr, dk2, dv, dga, do, dgb, dlng, dlnb, drk = vjp((dzv[:, 0:W], dzv[:, W:2 * W]))
        for ref, val in ((dy_ref, dy), (dr_ref, dr), (dk2_ref, dk2), (dv_ref, dv), (dga_ref, dga), (do_ref, do),
                         (dgb_ref, dgb)):
            ref[...] = val.astype(ref.dtype)

        @pl.when(pl.program_id(0) == 0)
        def _():
            for ref in (dlng_ref, dlnb_ref, drk_ref):
                ref[...] = jnp.zeros_like(ref)

        dlng_ref[...] += dlng
        dlnb_ref[...] += dlnb
        drk_ref[...] += drk

    vec = _const_spec((1, W))
    return pl.pallas_call(
        body, grid=(T // tm,), name="even_post_bwd",
        in_specs=[_row_spec(tm, W)] * 7 + [vec] * 3 + [_row_spec(tm, D), _col_spec(D, tm), _const_spec((D, D))]
        + extra_specs,
        out_specs=[_row_spec(tm, W)] * 7 + [vec] * 3 + [_const_spec((D, D))],
        out_shape=[jax.ShapeDtypeStruct((T, W), dt) for dt in (F32, F32, F32, F32, BF16, F32, BF16)]
        + [jax.ShapeDtypeStruct((1, W), F32)] * 3 + [jax.ShapeDtypeStruct((D, D), F32)],
        compiler_params=_cparams(("arbitrary",), VMEM_BIG),
    )(y, r, k2, v, ga, o, gb, lng, lnb, rk, dh, zt_bf, w_bf, *extra)


PADSEQ = SEQ + LEFT * L
ATT_SCALE = 1.0 / math.sqrt(HD)
ATT_Q = 4
WIN = BAND + (ATT_Q - 1) * L
ATT_STEPS = NC // ATT_Q
ATT_BIAS_SHAPE = (NPAIR, ATT_Q * 2 * L, WIN)


def _stack_chunks(a):
    return jnp.concatenate([_stack_pair(a[i * L:(i + 1) * L]) for i in range(ATT_Q)], axis=0)


def _unstack_chunks(a):
    return jnp.concatenate([_unstack_pair(a[i * 2 * L:(i + 1) * 2 * L]) for i in range(ATT_Q)], axis=0)


def window_bias(bias):
    parts = [jnp.pad(bias, ((0, 0), (0, 0), (i * L, (ATT_Q - 1 - i) * L)), constant_values=NEG) for i in range(ATT_Q)]
    return jnp.concatenate(parts, axis=1)


def _att_probs(q2, kw, bias, step):
    valid = _iota2((1, WIN), 1) >= (LEFT - step * ATT_Q) * L
    s = [jnp.where(valid, _bdot_nt(a, b) * ATT_SCALE + bias[p], NEG) for p, (a, b) in enumerate(zip(q2, kw))]
    e = [jnp.exp(a - jnp.max(a, axis=-1, keepdims=True)) for a in s]
    return [a / jnp.sum(a, axis=-1, keepdims=True) for a in e]


def attention_fwd(q, kpad, vpad, bias):
    def body(q_ref, k_ref, v_ref, b_ref, o_ref):
        step = pl.program_id(1)
        start = pl.multiple_of(step * (ATT_Q * L), L)
        kw = _pairs(k_ref[pl.ds(start, WIN), :])
        vw = _pairs(v_ref[pl.ds(start, WIN), :])
        q2 = [_stack_chunks(a) for a in _pairs(q_ref[...].astype(BF16))]
        p = _att_probs(q2, kw, b_ref[...], step)
        o_ref[...] = jnp.concatenate([_unstack_chunks(_bdot(a, b)) for a, b in zip(p, vw)], axis=-1)

    qblk = pl.BlockSpec((ATT_Q * L, W), lambda b, c: (b * ATT_STEPS + c, 0))
    kblk = pl.BlockSpec((PADSEQ, W), lambda b, c: (b, 0))
    return pl.pallas_call(
        body, grid=(NSEQ, ATT_STEPS), name="attention_fwd",
        in_specs=[qblk, kblk, kblk, _const_spec(ATT_BIAS_SHAPE)],
        out_specs=qblk, out_shape=jax.ShapeDtypeStruct((T, W), F32),
        compiler_params=_cparams(("parallel", "arbitrary")),
    )(q, kpad, vpad, bias)


def attention_bwd(q, kpad, vpad, bias, do):
    def body(q_ref, k_ref, v_ref, b_ref, do_ref, dq_ref, dko_ref, dvo_ref, db_ref, dk_ref, dv_ref):
        b = pl.program_id(0)
        c = pl.program_id(1)

        @pl.when(c == 0)
        def _():
            dk_ref[...] = jnp.zeros_like(dk_ref)
            dv_ref[...] = jnp.zeros_like(dv_ref)

        @pl.when((c == 0) & (b == 0))
        def _():
            db_ref[...] = jnp.zeros_like(db_ref)

        start = pl.multiple_of(c * (ATT_Q * L), L)
        kw = _pairs(k_ref[pl.ds(start, WIN), :])
        vw = _pairs(v_ref[pl.ds(start, WIN), :])
        q2 = [_stack_chunks(a) for a in _pairs(q_ref[...].astype(BF16))]
        do2 = [_stack_chunks(a) for a in _pairs(do_ref[...].astype(BF16))]
        p = _att_probs(q2, kw, b_ref[...], c)
        dp = [_bdot_nt(a, b) for a, b in zip(do2, vw)]
        ds = [a * (d - jnp.sum(d * a, axis=-1, keepdims=True)) for a, d in zip(p, dp)]
        dss = [(a * ATT_SCALE).astype(BF16) for a in ds]
        dq_ref[...] = jnp.concatenate([_unstack_chunks(_bdot(a, b)) for a, b in zip(dss, kw)], axis=-1).astype(BF16)
        dk_ref[pl.ds(start, WIN), :] += jnp.concatenate([_bdot_tn(a, b) for a, b in zip(dss, q2)], axis=-1)
        dv_ref[pl.ds(start, WIN), :] += jnp.concatenate([_bdot_tn(a, b) for a, b in zip(p, do2)], axis=-1)
        for i in range(NPAIR):
            db_ref[i] += ds[i]

        @pl.when(c == ATT_STEPS - 1)
        def _():
            dko_ref[...] = dk_ref[LEFT * L:, :].astype(BF16)
            dvo_ref[...] = dv_ref[LEFT * L:, :].astype(BF16)

    qblk = pl.BlockSpec((ATT_Q * L, W), lambda b, c: (b * ATT_STEPS + c, 0))
    kblk = pl.BlockSpec((PADSEQ, W), lambda b, c: (b, 0))
    sblk = pl.BlockSpec((SEQ, W), lambda b, c: (b, 0))
    bblk = _const_spec(ATT_BIAS_SHAPE)
    return pl.pallas_call(
        body, grid=(NSEQ, ATT_STEPS), name="attention_bwd",
        in_specs=[qblk, kblk, kblk, bblk, qblk],
        out_specs=[qblk, sblk, sblk, bblk],
        out_shape=[jax.ShapeDtypeStruct((T, W), BF16), jax.ShapeDtypeStruct((T, W), BF16),
                   jax.ShapeDtypeStruct((T, W), BF16), jax.ShapeDtypeStruct(ATT_BIAS_SHAPE, F32)],
        scratch_shapes=[pltpu.VMEM((PADSEQ, W), F32), pltpu.VMEM((PADSEQ, W), F32)],
        compiler_params=_cparams(("arbitrary", "arbitrary"), VMEM_BIG),
    )(q, kpad, vpad, bias, do)


NTAB = 2 * CLIP + 1
EXT = BAND + L


def _ext_onehot():
    n = _iota2((EXT, NTAB), 0)
    m = _iota2((EXT, NTAB), 1)
    return (jnp.clip(BAND - 1 - n, -CLIP, CLIP) + CLIP == m).astype(F32)


def bias_expand(table):
    def body(t_ref, o_ref):
        ext = _hdot_nt(t_ref[...], _ext_onehot())
        for i in range(L):
            s = L - 1 - i
            o_ref[:, i, :] = (pltpu.roll(ext, EXT - s, 1) if s else ext)[:, :BAND]

    return pl.pallas_call(body, name="bias_expand", out_shape=jax.ShapeDtypeStruct((NH, L, BAND), F32))(table)


def bias_grad(dbias):
    def body(d_ref, o_ref):
        acc = jnp.zeros((NH, EXT), F32)
        zpad = jnp.zeros((NH, EXT - BAND), F32)
        for i in range(L):
            s = L - 1 - i
            row = jnp.concatenate([d_ref[:, i, :], zpad], axis=-1)
            acc = acc + (pltpu.roll(row, s, 1) if s else row)
        o_ref[...] = _hdot(acc, _ext_onehot())

    return pl.pallas_call(body, name="bias_grad", out_shape=jax.ShapeDtypeStruct((NH, NTAB), F32))(dbias)


def _group_cols(g):
    return slice(g * SGC, (g + 1) * SGC)


def _sg_norm(gv, lng, lnb):
    gc = gv - jnp.mean(gv, axis=-1, keepdims=True)
    rstd = lax.rsqrt(jnp.mean(gc * gc, axis=-1, keepdims=True) + LN_EPS)
    xhat = gc * rstd
    return xhat, rstd, xhat * lng + lnb


def gmlp_fwd(u, v, gate, lng, lnb, wm_bf, sgb_t):
    def body(u_ref, v_ref, gt_ref, lng_ref, lnb_ref, wm_ref, sb_ref, z_ref, zt_ref):
        _, _, vln = _sg_norm(_gelu(v_ref[...]), lng_ref[...], lnb_ref[...])
        vlb = vln.astype(BF16)
        for g in range(NG):
            cs = _group_cols(g)
            sv = jnp.dot(wm_ref[g], vlb[:, cs], preferred_element_type=F32) + sb_ref[:, g:g + 1]
            zg = (_gelu(u_ref[:, cs]) * sv * _silu(gt_ref[:, cs])).astype(BF16)
            z_ref[:, cs] = zg
            zt_ref[cs, :] = zg.T

    return pl.pallas_call(
        body, grid=(T // SGC,), name="gmlp_fwd",
        in_specs=[_row_spec(SGC, D)] * 3 + [_const_spec((1, D))] * 2 + [_const_spec((NG, SGC, SGC)),
                                                                      _const_spec((SGC, NG))],
        out_specs=[_row_spec(SGC, D), _col_spec(D, SGC)],
        out_shape=[jax.ShapeDtypeStruct((T, D), BF16), jax.ShapeDtypeStruct((D, T), BF16)],
        compiler_params=_cparams(("parallel",)),
    )(u, v, gate, lng, lnb, wm_bf, sgb_t)


GMLP_BWD_CHUNKS = 2


def gmlp_bwd(u, v, gate, lng, lnb, wm_bf, sgb_t, dh, zt_bf, w_bf):
    def body(u_ref, v_ref, gt_ref, lng_ref, lnb_ref, wm_ref, sb_ref, dh_ref, zt_ref, w_ref,
             du_ref, dv_ref, dgt_ref, dlng_ref, dlnb_ref, dwm_ref, dsb_ref, dw_ref):
        @pl.when(pl.program_id(0) == 0)
        def _():
            for ref in (dlng_ref, dlnb_ref, dwm_ref, dsb_ref):
                ref[...] = jnp.zeros_like(ref)

        dz = _out_proj_back(dh_ref, zt_ref, w_ref, dw_ref)
        sel = (_iota2((D, NG), 0) // SGC == _iota2((D, NG), 1)).astype(F32)
        for ch in range(GMLP_BWD_CHUNKS):
            rows = slice(ch * SGC, (ch + 1) * SGC)
            gv, dgv_dv = _gelu_both(v_ref[rows, :])
            xhat, rstd, vln = _sg_norm(gv, lng_ref[...], lnb_ref[...])
            vlb = vln.astype(BF16)
            dvln = []
            dsv_all = []
            for g in range(NG):
                cs = _group_cols(g)
                uu = u_ref[rows, cs]
                gg = gt_ref[rows, cs]
                dzz = dz[rows, cs]
                sv = jnp.dot(wm_ref[g], vlb[:, cs], preferred_element_type=F32) + sb_ref[:, g:g + 1]
                gu, dgu = _gelu_both(uu)
                sg, dsg = _silu_both(gg)
                dzgu = dzz * gu
                dsv = dzgu * sg
                dgt_ref[rows, cs] = (dzgu * sv * dsg).astype(BF16)
                du_ref[rows, cs] = (dzz * sv * sg * dgu).astype(BF16)
                dsb16 = dsv.astype(BF16)
                dvln.append(lax.dot_general(wm_ref[g], dsb16, (((0,), (0,)), ((), ())), preferred_element_type=F32))
                dwm_ref[g] += lax.dot_general(dsb16, vlb[:, cs], (((1,), (1,)), ((), ())),
                                              preferred_element_type=F32)
                dsv_all.append(dsv)
            dvl = jnp.concatenate(dvln, axis=-1)
            dsb_ref[...] += _hdot(jnp.concatenate(dsv_all, axis=-1), sel)
            dlng_ref[...] += jnp.sum(dvl * xhat, axis=0, keepdims=True)
            dlnb_ref[...] += jnp.sum(dvl, axis=0, keepdims=True)
            dxh = dvl * lng_ref[...]
            dgv = rstd * (dxh - jnp.mean(dxh, axis=-1, keepdims=True)
                          - xhat * jnp.mean(dxh * xhat, axis=-1, keepdims=True))
            dv_ref[rows, :] = (dgv * dgv_dv).astype(BF16)

    tm = GMLP_BWD_CHUNKS * SGC
    return pl.pallas_call(
        body, grid=(T // tm,), name="gmlp_bwd",
        in_specs=[_row_spec(tm, D)] * 3 + [_const_spec((1, D))] * 2
        + [_const_spec((NG, SGC, SGC)), _const_spec((SGC, NG)), _row_spec(tm, D), _col_spec(D, tm),
           _const_spec((D, D))],
        out_specs=[_row_spec(tm, D)] * 3 + [_const_spec((1, D))] * 2
        + [_const_spec((NG, SGC, SGC)), _const_spec((SGC, NG)), _const_spec((D, D))],
        out_shape=[jax.ShapeDtypeStruct((T, D), BF16)] * 3 + [jax.ShapeDtypeStruct((1, D), F32)] * 2
        + [jax.ShapeDtypeStruct((NG, SGC, SGC), F32), jax.ShapeDtypeStruct((SGC, NG), F32),
           jax.ShapeDtypeStruct((D, D), F32)],
        compiler_params=_cparams(("arbitrary",), VMEM_BIG),
    )(u, v, gate, lng, lnb, wm_bf, sgb_t, dh, zt_bf, w_bf)


NCHIP = 4
NDEV = 8
ANY = pl.BlockSpec(memory_space=pl.ANY)


HBM = pl.BlockSpec(memory_space=pltpu.HBM)
SEM = pl.BlockSpec(memory_space=pltpu.SEMAPHORE)
EFFECT = pltpu.SideEffectType.DATAFLOW_SIDE_EFFECTING


def _peers(whole_mesh):
    x, y, c = lax.axis_index("x"), lax.axis_index("y"), lax.axis_index("c")
    if not whole_mesh:
        return [((px, py, c), 2 * px + py) for px, py in ((1 - x, y), (x, 1 - y), (1 - x, 1 - y))], 2 * x + y
    out = []
    for j in range(1, NDEV):
        px, py, pc = x ^ (j >> 2), y ^ ((j >> 1) & 1), c ^ (j & 1)
        out.append(((px, py, pc), 4 * px + 2 * py + pc))
    return out, 4 * x + 2 * y + c


def _send_copies(src, land, send, recv, scatter, whole_mesh, starting):
    peers, me = _peers(whole_mesh)
    copies = []
    for t in range(len(src)):
        for j, (dev, slot) in enumerate(peers):
            k = t * len(peers) + j
            copies.append(pltpu.make_async_remote_copy(
                src_ref=src[t].at[slot] if scatter else src[t], dst_ref=land[t].at[me if starting else slot],
                send_sem=send.at[k], recv_sem=recv.at[k], device_id=dev, device_id_type=MESH))
    return copies


def send_start(srcs, lands, scatter, whole_mesh, name):
    n = len(srcs)
    nsem = n * (NDEV - 1 if whole_mesh else NCHIP - 1)

    def body(*refs):
        for cp in _send_copies(refs[:n], refs[n:2 * n], refs[2 * n], refs[2 * n + 1], scatter, whole_mesh, True):
            cp.start()
        refs[-1][...] = jnp.zeros_like(refs[-1])

    arrs = list(srcs) + list(lands)
    out = pl.pallas_call(
        body, name=name,
        out_shape=(pltpu.SemaphoreType.DMA((nsem,)), pltpu.SemaphoreType.DMA((nsem,)),
                   *[pltpu.HBM(a.shape, a.dtype) for a in arrs], jax.ShapeDtypeStruct((8, 128), F32)),
        in_specs=[HBM] * (2 * n), out_specs=(SEM, SEM, *[HBM] * (2 * n), pl.BlockSpec(memory_space=pltpu.VMEM)),
        input_output_aliases={i: 2 + i for i in range(2 * n)},
        compiler_params=pltpu.CompilerParams(has_side_effects=EFFECT),
    )(*[pltpu.with_memory_space_constraint(a, pltpu.HBM) for a in arrs])
    return out[0], out[1], list(out[2:2 + n]), list(out[2 + n:2 + 2 * n]), out[-1]


def send_wait(started, after, scatter, whole_mesh, name):
    send, recv, srcs, lands, _ = started
    n = len(srcs)

    def body(*refs):
        for cp in _send_copies(refs[:n], refs[n:2 * n], refs[2 * n], refs[2 * n + 1], scatter, whole_mesh, False):
            cp.wait_send()
            cp.wait_recv()

    arrs = list(srcs) + list(lands)
    out = pl.pallas_call(
        body, name=name, out_shape=tuple(pltpu.HBM(a.shape, a.dtype) for a in arrs),
        in_specs=[HBM] * (2 * n) + [SEM, SEM, ANY], out_specs=tuple([HBM] * (2 * n)),
        input_output_aliases={i: i for i in range(2 * n)},
        compiler_params=pltpu.CompilerParams(has_side_effects=EFFECT),
    )(*arrs, send, recv, after)
    return list(out[n:])


def exchange_c(arrs, name):
    n = len(arrs)

    def body(*refs):
        ins, outs = refs[:n], refs[n:2 * n]
        send, recv = refs[2 * n:]
        sibling = (lax.axis_index("x"), lax.axis_index("y"), 1 - lax.axis_index("c"))
        copies = [pltpu.make_async_remote_copy(src_ref=ins[t], dst_ref=outs[t], send_sem=send.at[t], recv_sem=recv.at[t],
                                               device_id=sibling, device_id_type=MESH) for t in range(n)]
        for cp in copies:
            cp.start()
        for cp in copies:
            cp.wait()

    return pl.pallas_call(
        body, name=name, in_specs=[ANY] * n, out_specs=[ANY] * n,
        out_shape=[jax.ShapeDtypeStruct(a.shape, a.dtype) for a in arrs],
        scratch_shapes=[pltpu.SemaphoreType.DMA((n,)), pltpu.SemaphoreType.DMA((n,))],
    )(*arrs)


def swap_row_halves(a, name):
    n, rows, cols = a.shape
    half = rows // 2

    def body(in_ref, out_ref, send, recv):
        x, y, c = lax.axis_index("x"), lax.axis_index("y"), lax.axis_index("c")
        cp = pltpu.make_async_remote_copy(src_ref=in_ref.at[:, pl.ds((1 - c) * half, half), :], dst_ref=out_ref,
                                          send_sem=send, recv_sem=recv, device_id=(x, y, 1 - c), device_id_type=MESH)
        cp.start()
        cp.wait()

    return pl.pallas_call(
        body, name=name, in_specs=[ANY], out_specs=ANY, out_shape=jax.ShapeDtypeStruct((n, half, cols), a.dtype),
        scratch_shapes=[pltpu.SemaphoreType.DMA, pltpu.SemaphoreType.DMA],
    )(a)


def add_blocks(a, b, name):
    n, rows, cols = a.shape
    tr = _rows_tile(rows)

    def body(a_ref, b_ref, o_ref):
        o_ref[...] = (a_ref[...].astype(F32) + b_ref[...].astype(F32)).astype(BF16)

    spec = pl.BlockSpec((1, tr, cols), lambda s, i: (s, i, 0))
    return pl.pallas_call(
        body, grid=(n, rows // tr), name=name, in_specs=[spec, spec], out_specs=spec,
        out_shape=jax.ShapeDtypeStruct(a.shape, BF16), compiler_params=_cparams(("parallel", "parallel")),
    )(a, b)


def gather_weights(arrs, split):
    n = len(arrs)

    def body(*refs):
        ins, outs = refs[:n], refs[n:2 * n]
        send1, recv1, send2, recv2, loc = refs[2 * n:]
        x, y, c = lax.axis_index("x"), lax.axis_index("y"), lax.axis_index("c")
        me = 2 * x + y
        sibling = (x, y, 1 - c)
        peers = [(1 - x, y), (x, 1 - y), (1 - x, 1 - y)]

        def rows_of(t, core):
            half = arrs[t].shape[0] // 2
            return pl.ds(core * half, half)

        def part(ref, t, core):
            return ref.at[rows_of(t, core)] if split[t] else ref

        local = [pltpu.make_async_copy(ins[t], outs[t].at[me], loc.at[t]) for t in range(n)]
        for cp in local:
            cp.start()
        first = []
        for t in range(n):
            for j, (px, py) in enumerate(peers):
                first.append(pltpu.make_async_remote_copy(
                    src_ref=part(ins[t], t, c), dst_ref=part(outs[t].at[me], t, c), send_sem=send1.at[t, j],
                    recv_sem=recv1.at[t, j], device_id=(px, py, c), device_id_type=MESH))
        for cp in first:
            cp.start()
        passed = []
        for t in range(n):
            for j, (px, py) in enumerate(peers):
                landed = part(outs[t].at[2 * px + py], t, c)
                pltpu.make_async_remote_copy(
                    src_ref=landed, dst_ref=landed, send_sem=send1.at[t, j], recv_sem=recv1.at[t, j],
                    device_id=(x, y, c), device_id_type=MESH).wait_recv()
                if split[t]:
                    cp = pltpu.make_async_remote_copy(
                        src_ref=landed, dst_ref=landed, send_sem=send2.at[t, j], recv_sem=recv2.at[t, j],
                        device_id=sibling, device_id_type=MESH)
                    cp.start()
                    passed.append(cp)
        for t in range(n):
            for j, (px, py) in enumerate(peers):
                if split[t]:
                    other = part(outs[t].at[2 * px + py], t, 1 - c)
                    pltpu.make_async_remote_copy(
                        src_ref=other, dst_ref=other, send_sem=send2.at[t, j], recv_sem=recv2.at[t, j],
                        device_id=(x, y, c), device_id_type=MESH).wait_recv()
        for cp in first + passed:
            cp.wait_send()
        for cp in local:
            cp.wait()

    return pl.pallas_call(
        body, name="gather_weights", in_specs=[ANY] * n, out_specs=[ANY] * n,
        out_shape=[jax.ShapeDtypeStruct((NCHIP,) + a.shape, a.dtype) for a in arrs],
        scratch_shapes=[pltpu.SemaphoreType.DMA((n, 3))] * 4 + [pltpu.SemaphoreType.DMA((n,))],
    )(*arrs)


def _adam_math(g, w, m, v):
    m = ADAM_B1 * m + (1.0 - ADAM_B1) * g
    v = ADAM_B2 * v + (1.0 - ADAM_B2) * (g * g)
    m_hat = m / (1.0 - ADAM_B1 ** ADAM_STEP)
    v_hat = v / (1.0 - ADAM_B2 ** ADAM_STEP)
    delta = -ADAM_LR * (m_hat / (jnp.sqrt(v_hat) + ADAM_EPS) + ADAM_WD * w)
    return delta, m, v


def _rows_tile(rows):
    return rows if rows <= 256 else 256


def sum_chips(own, parts, name):
    _, rows, cols = parts.shape
    tr = _rows_tile(rows)

    def body(own_ref, p_ref, o_ref):
        acc = own_ref[...].astype(F32)
        for s in range(NCHIP):
            acc = acc + p_ref[s].astype(F32)
        o_ref[...] = acc

    return pl.pallas_call(
        body, grid=(rows // tr,), name=name,
        in_specs=[pl.BlockSpec((tr, cols), lambda i: (i, 0)), pl.BlockSpec((NCHIP, tr, cols), lambda i: (0, i, 0))],
        out_specs=pl.BlockSpec((tr, cols), lambda i: (i, 0)),
        out_shape=jax.ShapeDtypeStruct((rows, cols), F32),
        compiler_params=_cparams(("parallel",)),
    )(own, parts)


def adam_shard(p_mine, p_sib, w, m, v, name):
    rows, cols = p_mine.shape
    tr = _rows_tile(rows)
    lead = w.ndim == 3

    def body(a_ref, b_ref, w_ref, m_ref, v_ref, g_ref, d_ref, mo_ref, vo_ref):
        g = a_ref[...] + b_ref[...]
        g = g[None] if lead else g
        g_ref[...] = g
        d_ref[...], mo_ref[...], vo_ref[...] = _adam_math(g, w_ref[...], m_ref[...], v_ref[...])

    flat = pl.BlockSpec((tr, cols), lambda i: (i, 0))
    spec = pl.BlockSpec((1, tr, cols), lambda i: (0, i, 0)) if lead else flat
    return pl.pallas_call(
        body, grid=(rows // tr,), name=name, in_specs=[flat] * 2 + [spec] * 3, out_specs=[spec] * 4,
        out_shape=[jax.ShapeDtypeStruct(w.shape, F32)] * 4,
        compiler_params=_cparams(("parallel",)),
    )(p_mine, p_sib, w, m, v)


def adam_shard_halves_t(r_mine, r_sib, wt, mt, vt, name):
    hrows, cols = r_mine.shape
    tr = _rows_tile(hrows)
    per_half = hrows // tr

    def body(a_ref, b_ref, w_ref, m_ref, v_ref, g_ref, d_ref, mo_ref, vo_ref):
        mine = pl.program_id(0) == lax.axis_index("c")
        g = jnp.where(mine, a_ref[...], b_ref[...]).T[None]
        g_ref[...] = g
        d_ref[...], mo_ref[...], vo_ref[...] = _adam_math(g, w_ref[...], m_ref[...], v_ref[...])

    flat = pl.BlockSpec((tr, cols), lambda h, i: (i, 0))
    spec = pl.BlockSpec((1, cols, tr), lambda h, i: (0, 0, h * per_half + i))
    return pl.pallas_call(
        body, grid=(2, per_half), name=name, in_specs=[flat] * 2 + [spec] * 3, out_specs=[spec] * 4,
        out_shape=[jax.ShapeDtypeStruct(wt.shape, F32)] * 4,
        compiler_params=_cparams(("parallel", "parallel")),
    )(r_mine, r_sib, wt, mt, vt)


def adam_replicated(parts, w, m, v, name):
    rows = w.shape[0]

    def body(p_ref, w_ref, m_ref, v_ref, g_ref, d_ref, mo_ref, vo_ref):
        g = p_ref[0]
        for d in range(1, NDEV):
            g = g + p_ref[d]
        g_ref[...] = g
        d_ref[...], mo_ref[...], vo_ref[...] = _adam_math(g, w_ref[...], m_ref[...], v_ref[...])

    return pl.pallas_call(
        body, name=name, out_shape=[jax.ShapeDtypeStruct((rows, 128), F32)] * 4,
    )(parts, w, m, v)


def _pack(arrs):
    pieces = []
    for a in arrs:
        flat = a.reshape(-1)
        pad = (-flat.shape[0]) % 128
        pieces.append(jnp.pad(flat, (0, pad)) if pad else flat)
    flat = jnp.concatenate(pieces)
    pad = (-flat.shape[0]) % 1024
    return jnp.pad(flat, (0, pad)).reshape(-1, 128)


def _unpack(buf, shapes):
    flat = buf.reshape(-1)
    out = []
    o = 0
    for s in shapes:
        n = int(np.prod(s))
        out.append(flat[o:o + n].reshape(s))
        o += n + (-n) % 128
    return out


EVEN_SPLITS = (SHIFT, W, W, W, W, W)
ODD_SPLITS = (D, D, D)


def _cols_to_chips(a):
    rows, cols = a.shape
    return a.reshape(rows, NCHIP, cols // NCHIP).transpose(1, 0, 2)


def _chips_to_cols(a):
    _, rows, n = a.shape
    return a.transpose(1, 0, 2).reshape(rows, NCHIP * n)


def kernel(x, norm_g, w_in_e, shift_mu, rw_w0, rw_w2, rw_a0, rw_a2, rw_kk, rw_ka, rw_rk, rw_lnx_g, rw_lnx_b, att_bias, w_out_e, w_in_o, sg_ln_g, sg_ln_b, sg_w, sg_b, w_out_o, final_g, loss_target, m_norm_g, m_w_in_e, m_shift_mu, m_rw_w0, m_rw_w2, m_rw_a0, m_rw_a2, m_rw_kk, m_rw_ka, m_rw_rk, m_rw_lnx_g, m_rw_lnx_b, m_att_bias, m_w_out_e, m_w_in_o, m_sg_ln_g, m_sg_ln_b, m_sg_w, m_sg_b, m_w_out_o, m_final_g, v_norm_g, v_w_in_e, v_shift_mu, v_rw_w0, v_rw_w2, v_rw_a0, v_rw_a2, v_rw_kk, v_rw_ka, v_rw_rk, v_rw_lnx_g, v_rw_lnx_b, v_att_bias, v_w_out_e, v_w_in_o, v_sg_ln_g, v_sg_ln_b, v_sg_w, v_sg_b, v_w_out_o, v_final_g):
    x2 = x.reshape(T, D)
    tgt = loss_target.reshape(T, D)

    my_chip = 2 * lax.axis_index("x") + lax.axis_index("y")
    gathered = gather_weights(
        [jnp.swapaxes(w_in_e[0], 0, 1).astype(BF16), jnp.concatenate([rw_w2[0], rw_a2[0]], axis=0),
         jnp.concatenate([sg_ln_g, sg_ln_b], axis=0)], [True, True, False])
    wie = gathered[0].reshape(EVEN_IN, D)
    w2 = _chips_to_cols(gathered[1][:, :LORA])
    a2 = _chips_to_cols(gathered[1][:, LORA:])
    sglg = _chips_to_cols(gathered[2][:, 0:1])
    sglb = _chips_to_cols(gathered[2][:, 1:2])

    late = [w_out_e[0].astype(BF16), w_in_o[0].astype(BF16), w_out_o[0].astype(BF16)]
    late_started = send_start(late, [jnp.broadcast_to(a[None], (NCHIP,) + a.shape) for a in late], False, False,
                              "late_weights_start")

    def late_weights(after):
        woe, wio, woo = send_wait(late_started, after, False, False, "late_weights_wait")
        return woe.reshape(D, D), _chips_to_cols(wio), woo.reshape(D, D)

    def scatter_start(grads, name):
        srcs = [g_.astype(BF16) if g_.shape[-1] >= W else g_ for g_ in grads]
        return send_start(srcs, [jnp.zeros_like(s) for s in srcs], True, False, name)

    def own_block(g_):
        return lax.dynamic_index_in_dim(g_, my_chip, axis=0, keepdims=False)

    started = {}

    def on_odd_grads(d_woo, d_wio):
        blocks = [d_woo.reshape(NCHIP, D // NCHIP, D), d_wio]
        started["odd"] = (scatter_start(blocks, "odd_grads_start"), [own_block(b) for b in blocks])
        return started["odd"][0][-1]

    def on_even_grads(big_g):
        d_wie, d_woe, _, _, d_w2, d_a2, d_sglg, d_sglb = big_g
        my_half = lax.dynamic_slice_in_dim(d_wie, lax.axis_index("c") * (D // 2), D // 2, axis=1)
        d_wie_half = add_blocks(my_half, swap_row_halves(d_wie, "swap_w_in_e_halves"), "add_w_in_e_halves")
        blocks = [d_wie_half, d_woe.reshape(NCHIP, D // NCHIP, D), _cols_to_chips(d_w2), _cols_to_chips(d_a2),
                  _cols_to_chips(d_sglg), _cols_to_chips(d_sglb)]
        started["even"] = (scatter_start(blocks, "even_grads_start"), [own_block(b) for b in blocks])
        return started["even"][0][-1]

    def on_small_grads(layer, grads):
        mine = _pack(grads)
        started[layer + "_small"] = send_start([mine], [jnp.broadcast_to(mine[None], (NDEV,) + mine.shape)], False,
                                               True, layer + "_small_grads_start")
        return started[layer + "_small"][-1]

    loss_part, dx, _, _ = _local_step(
        x2, tgt, wie, late_weights, w2, a2, sglg, sglb, norm_g, shift_mu, rw_w0, rw_a0, rw_kk, rw_ka, rw_rk,
        rw_lnx_g, rw_lnx_b, att_bias, sg_w, sg_b, final_g, first_after=late_started[-1], on_odd_grads=on_odd_grads,
        on_even_grads=on_even_grads, on_small_grads=on_small_grads)
    even_started, even_own = started["even"]

    wmv = {"w_in_e": tuple(jnp.swapaxes(a, 1, 2) for a in (w_in_e, m_w_in_e, v_w_in_e)),
           "w_out_e": (w_out_e, m_w_out_e, v_w_out_e),
           "w_in_o": (w_in_o, m_w_in_o, v_w_in_o), "w_out_o": (w_out_o, m_w_out_o, v_w_out_o),
           "rw_w2": (rw_w2, m_rw_w2, v_rw_w2), "rw_a2": (rw_a2, m_rw_a2, v_rw_a2),
           "sg_ln_g": (sg_ln_g, m_sg_ln_g, v_sg_ln_g), "sg_ln_b": (sg_ln_b, m_sg_ln_b, v_sg_ln_b)}
    sharded = {}

    def finish(names, own, landed, tag):
        partial = [sum_chips(o_, p_, "sum_" + nm) for o_, p_, nm in zip(own, landed, names)]
        from_sibling = exchange_c(partial, "swap_partials_" + tag)
        for nm, mine, sib in zip(names, partial, from_sibling):
            if nm == "w_in_e":
                res = adam_shard_halves_t(mine, sib, *wmv[nm], "adam_" + nm)
                sharded[nm] = [jnp.swapaxes(a, 1, 2) for a in res]
            else:
                sharded[nm] = adam_shard(mine, sib, *wmv[nm], "adam_" + nm)
        return partial[0]

    odd_started, odd_own = started["odd"]
    odd_landed = send_wait(odd_started, started["even_small"][-1], True, False, "odd_grads_wait")
    done = finish(["w_out_o", "w_in_o"], odd_own, odd_landed, "odd")

    no_w = jnp.zeros((1, 1), F32)
    groups = {
        "odd": (["sg_w", "sg_b", "final_g", "norm_g1"], [sg_w, sg_b, final_g, norm_g[1:2]],
                [m_sg_w, m_sg_b, m_final_g, m_norm_g[1:2]], [v_sg_w, v_sg_b, v_final_g, v_norm_g[1:2]]),
        "even": (["norm_g0", "shift_mu", "rw_w0", "rw_a0", "rw_kk", "rw_ka", "rw_rk", "rw_lnx_g", "rw_lnx_b",
                  "att_bias", "loss"],
                 [norm_g[0:1], shift_mu, rw_w0, rw_a0, rw_kk, rw_ka, rw_rk, rw_lnx_g, rw_lnx_b, att_bias, no_w],
                 [m_norm_g[0:1], m_shift_mu, m_rw_w0, m_rw_a0, m_rw_kk, m_rw_ka, m_rw_rk, m_rw_lnx_g, m_rw_lnx_b,
                  m_att_bias, no_w],
                 [v_norm_g[0:1], v_shift_mu, v_rw_w0, v_rw_a0, v_rw_kk, v_rw_ka, v_rw_rk, v_rw_lnx_g, v_rw_lnx_b,
                  v_att_bias, no_w]),
    }
    rep = {}
    for layer in ("odd", "even"):
        nms, ws, ms_, vs_ = groups[layer]
        (gathered_g,) = send_wait(started[layer + "_small"], done, False, True, layer + "_small_grads_wait")
        rep_out = adam_replicated(gathered_g, _pack(ws), _pack(ms_), _pack(vs_), "adam_" + layer + "_small")
        done = rep_out[0]
        for nm in nms:
            rep[nm] = []
        for buf in rep_out:
            for nm, a in zip(nms, _unpack(buf, [w_.shape for w_ in ws])):
                rep[nm].append(a)
    rep["norm_g"] = [jnp.concatenate([a, b], axis=0) for a, b in zip(rep["norm_g0"], rep["norm_g1"])]
    even_landed = send_wait(even_started, done, True, False, "even_grads_wait")
    finish(["w_in_e", "w_out_e", "rw_w2", "rw_a2", "sg_ln_g", "sg_ln_b"], even_own, even_landed, "even")

    order = ["norm_g", "w_in_e", "shift_mu", "rw_w0", "rw_w2", "rw_a0", "rw_a2", "rw_kk", "rw_ka", "rw_rk",
             "rw_lnx_g", "rw_lnx_b", "att_bias", "w_out_e", "w_in_o", "sg_ln_g", "sg_ln_b", "sg_w", "sg_b",
             "w_out_o", "final_g"]
    results = {**sharded, **rep}
    outs = [rep["loss"][0].reshape(()), dx.reshape(NSEQ, SEQ, D)]
    for kind in range(4):
        outs += [results[nm][kind] for nm in order]
    return tuple(outs)


def _local_step(x2, tgt, wie_t, late_weights, w2, a2, sglg, sglb, norm_g, shift_mu, rw_w0, rw_a0, rw_kk, rw_ka, rw_rk,
                rw_lnx_g, rw_lnx_b, att_bias, sg_w, sg_b, final_g, first_after=None, on_odd_grads=None,
                on_even_grads=None, on_small_grads=None):
    zl = jnp.zeros((LORA, W), F32)
    w2x = jnp.concatenate([w2, zl], axis=0)
    a2x = jnp.concatenate([zl, a2], axis=0)
    rk = rw_rk.reshape(1, W)
    pos = np.arange(SGC)
    sg_mask = jnp.asarray(((pos[None, :] // L) <= (pos[:, None] // L)).astype(np.float32))
    wm = (sg_w[0] * sg_mask[None]).astype(BF16)
    sgb_t = sg_b[0].T

    xn0, ps, ga, q, kb, vb, gb = ln_in_proj(x2, norm_g[0:1], wie_t, EVEN_SPLITS, "in_proj_even", after=first_after,
                                            w_t=True)
    r, lw, k2, v, aa, bb = even_prep(ps, shift_mu, rw_w0, w2x, rw_a0, a2x, rw_kk, rw_ka)
    y, rw_saved = rwkv_fwd(r, lw, k2, v, aa, bb)
    bias = window_bias(bias_expand(att_bias[0]).reshape(NPAIR, 2 * L, BAND))

    def padded(a):
        return jnp.pad(a.astype(BF16).reshape(NSEQ, SEQ, W), ((0, 0), (LEFT * L, 0), (0, 0))).reshape(NSEQ * PADSEQ, W)

    kpad, vpad = padded(kb), padded(vb)
    o = attention_fwd(q, kpad, vpad, bias)
    z, zt = even_post(y, r, k2, v, ga, o, gb, rw_lnx_g, rw_lnx_b, rk)
    woe, wio, woo = late_weights(z)
    h1 = out_proj(x2, z, woe, "out_proj_even")
    xn1, u, vv, gt = ln_in_proj(h1, norm_g[1:2], wio, ODD_SPLITS, "in_proj_odd")
    z2, z2t = gmlp_fwd(u, vv, gt, sglg, sglb, wm, sgb_t)
    dh2, loss_part, d_final_g = out_proj_loss(h1, z2, woo, final_g[None], tgt)

    du, dvv, dgt, d_sglg, d_sglb, d_wm, d_sgb_t, d_woo = gmlp_bwd(u, vv, gt, sglg, sglb, wm, sgb_t, dh2, z2t, woo)
    dp_odd = [du, dvv, dgt]
    d_wio = matmul_acc_chips(xn1, dp_odd, "in_proj_odd_dw")
    token = on_odd_grads(d_woo, d_wio) if on_odd_grads else None
    dh1, d_g1 = in_proj_bwd_x(h1, norm_g[1:2], wio, dp_odd, dh2, "in_proj_odd_bwd", after=token)
    odd_small = [d_wm * sg_mask[None], d_sgb_t.T, d_final_g, d_g1]
    token = on_small_grads("odd", odd_small) if on_small_grads else None
    dy, dr2, dk22, dv2, dga, do, dgb, d_lng, d_lnb, d_rk, d_woe = even_post_bwd(
        y, r, k2, v, ga, o, gb, rw_lnx_g, rw_lnx_b, rk, dh1, zt, woe, after=token)
    dq, dkb, dvb, dbias = attention_bwd(q, kpad, vpad, bias, do)
    dbias = sum(dbias[:, i * 2 * L:(i + 1) * 2 * L, i * L:i * L + BAND] for i in range(ATT_Q))
    d_att_bias = bias_grad(dbias.reshape(NH, L, BAND))
    dr, dlw, dk2, dv, daa, dbb = rwkv_bwd(r, lw, k2, aa, bb, rw_saved, dy)
    dps, d_mu, d_w0, d_w2x, d_a0, d_a2x, d_kk, d_ka = even_prep_bwd(
        ps, shift_mu, rw_w0, w2x, rw_a0, a2x, rw_kk, rw_ka, dr, dlw, dk2, dv, daa, dbb, dr2, dk22, dv2)
    dp_even = [dps, dga, dq, dkb, dvb, dgb]
    d_wie = matmul_acc_chips(xn0, dp_even, "in_proj_even_dw")
    big_g = (d_wie, d_woe, d_wio, d_woo, d_w2x[:LORA], d_a2x[LORA:], d_sglg, d_sglb)
    token = on_even_grads(big_g) if on_even_grads else None
    dx, d_g0 = in_proj_bwd_x(x2, norm_g[0:1], wie_t, dp_even, dh1, "in_proj_even_bwd", after=token, w_t=True)
    even_small = [d_g0, d_mu, d_w0, d_a0, d_kk, d_ka, d_rk, d_lng, d_lnb, d_att_bias]
    if on_small_grads:
        on_small_grads("even", even_small + [loss_part[0:1, 0:1]])
    rep_g = [jnp.concatenate([d_g0, d_g1], axis=0)] + even_small[1:] + odd_small[:3]
    return loss_part[0, 0], dx, big_g, rep_g
```

```python
import functools
import math

import jax
import jax.numpy as jnp
import numpy as np
from jax import lax
from jax.experimental import pallas as pl
from jax.experimental.pallas import tpu as pltpu

F32 = jnp.float32
BF16 = jnp.bfloat16
HI = lax.Precision.HIGHEST

D = 1024
SEQ = 2048
NSEQ = 2
T = NSEQ * SEQ
HD = 64
NH = 8
W = 512
SHIFT = 1664
LORA = 64
EVEN_IN = 4224
ODD_IN = 3072
L = 64
NC = SEQ // L
LEFT = 8
BAND = (LEFT + 1) * L
CLIP = 128
SGC = 128
NG = 8
RMS_EPS = 1e-6
LN_EPS = 1e-5
GN_EPS = 64e-5
NEG = -1e30
VMEM_BIG = 56 * 1024 * 1024

ADAM_LR = 0.001
ADAM_B1 = 0.9
ADAM_B2 = 0.999
ADAM_EPS = 1e-08
ADAM_WD = 0.01
ADAM_STEP = 10

MESH = pl.DeviceIdType.MESH


def _bdot(a, b):
    return jnp.dot(a.astype(BF16), b.astype(BF16), preferred_element_type=F32)


def _bdot_nt(a, b):
    return lax.dot_general(a.astype(BF16), b.astype(BF16), (((1,), (1,)), ((), ())), preferred_element_type=F32)


def _bdot_tn(a, b):
    return lax.dot_general(a.astype(BF16), b.astype(BF16), (((0,), (0,)), ((), ())), preferred_element_type=F32)


def _hdot(a, b):
    return jnp.dot(a, b, precision=HI, preferred_element_type=F32)


def _hdot_nt(a, b):
    return lax.dot_general(a, b, (((1,), (1,)), ((), ())), precision=HI, preferred_element_type=F32)


def _hdot_tn(a, b):
    return lax.dot_general(a, b, (((0,), (0,)), ((), ())), precision=HI, preferred_element_type=F32)


def _iota2(shape, dim):
    return lax.broadcasted_iota(jnp.int32, shape, dim)


def _head_blockdiag():
    r = _iota2((W, W), 0) // HD
    c = _iota2((W, W), 1) // HD
    return (r == c).astype(BF16)


def _headsum_impl(x, bd):
    hi = x.astype(BF16)
    mid = (x - hi.astype(F32)).astype(BF16)
    return jnp.dot(hi, bd, preferred_element_type=F32) + jnp.dot(mid, bd, preferred_element_type=F32)


@jax.custom_vjp
def _headsum(x, bd):
    return _headsum_impl(x, bd)


def _headsum_fwd(x, bd):
    return _headsum_impl(x, bd), bd


def _headsum_bwd(bd, ct):
    return _headsum_impl(ct, bd), None


_headsum.defvjp(_headsum_fwd, _headsum_bwd)


def _silu(x):
    return x * jax.nn.sigmoid(x)


def _dsilu(x):
    s = jax.nn.sigmoid(x)
    return s * (1.0 + x * (1.0 - s))


_GELU_C = math.sqrt(2.0 / math.pi)


def _gelu(x):
    return 0.5 * x * (1.0 + jnp.tanh(_GELU_C * (x + 0.044715 * (x * x * x))))


def _dgelu(x):
    t = jnp.tanh(_GELU_C * (x + 0.044715 * (x * x * x)))
    return 0.5 * (1.0 + t) + 0.5 * x * (1.0 - t * t) * _GELU_C * (1.0 + 3.0 * 0.044715 * x * x)


def _silu_both(x):
    s = jax.nn.sigmoid(x)
    xs = x * s
    return xs, s + xs * (1.0 - s)


def _gelu_both(x):
    x2 = x * x
    t = jnp.tanh(_GELU_C * (x + 0.044715 * (x2 * x)))
    half = 0.5 * (1.0 + t)
    return x * half, half + 0.5 * x * (1.0 - t * t) * _GELU_C * (1.0 + 3.0 * 0.044715 * x2)


def _softplus(x):
    return jnp.maximum(x, 0.0) + jnp.log(1.0 + jnp.exp(-jnp.abs(x)))


def _cparams(sem, vmem=None):
    return pltpu.CompilerParams(dimension_semantics=sem, vmem_limit_bytes=vmem)


def _row_spec(tm, width):
    return pl.BlockSpec((tm, width), lambda i: (i, 0))


def _col_spec(height, tm):
    return pl.BlockSpec((height, tm), lambda i: (0, i))


def _const_spec(shape):
    nd = len(shape)
    return pl.BlockSpec(shape, lambda *_: (0,) * nd)


def _weight_dims(w_bf, w_t):
    return (((1,), (1,)), ((), ())) if w_t else (((1,), (0,)), ((), ())), w_bf.shape[0 if w_t else 1]


def ln_in_proj(x, g, w_bf, splits, name, after=None, w_t=False):
    dims, n = _weight_dims(w_bf, w_t)
    tm = 256
    spans = []
    o = 0
    for s in splits:
        spans.append((o, o + s))
        o += s
    assert o == n
    extra_specs, extra = _after_operand(after)

    def body(x_ref, g_ref, w_ref, *rest):
        xn_ref, outs = rest[len(extra)], rest[len(extra) + 1:]
        xv = x_ref[...]
        rstd = lax.rsqrt(jnp.mean(xv * xv, axis=-1, keepdims=True) + RMS_EPS)
        xn = (xv * rstd * g_ref[...]).astype(BF16)
        xn_ref[...] = xn.T
        p = lax.dot_general(xn, w_ref[...], dims, preferred_element_type=F32)
        for o_ref, (a, b) in zip(outs, spans):
            o_ref[...] = p[:, a:b]

    return pl.pallas_call(
        body, grid=(T // tm,), name=name,
        in_specs=[_row_spec(tm, D), _const_spec((1, D)), _const_spec(w_bf.shape)] + extra_specs,
        out_specs=[_col_spec(D, tm)] + [_row_spec(tm, s) for s in splits],
        out_shape=[jax.ShapeDtypeStruct((D, T), BF16)] + [jax.ShapeDtypeStruct((T, s), F32) for s in splits],
        compiler_params=_cparams(("parallel",), VMEM_BIG),
    )(x, g, w_bf, *extra)


def in_proj_bwd_x(x, g, w_bf, dps, dres, name, after=None, w_t=False):
    tm = 512
    back = (((1,), (0,)), ((), ())) if w_t else (((1,), (1,)), ((), ()))
    widths = [d.shape[1] for d in dps]
    extra_specs, extra = _after_operand(after)

    def body(x_ref, g_ref, w_ref, dres_ref, *rest):
        dp_refs = rest[:len(widths)]
        dx_ref, dg_ref = rest[-2:]
        dp = jnp.concatenate([r[...] for r in dp_refs], axis=-1)
        dxn = lax.dot_general(dp, w_ref[...], back, preferred_element_type=F32)
        xv = x_ref[...]
        rstd = lax.rsqrt(jnp.mean(xv * xv, axis=-1, keepdims=True) + RMS_EPS)
        xhat = xv * rstd
        dgp = jnp.sum(dxn * xhat, axis=0, keepdims=True)

        @pl.when(pl.program_id(0) == 0)
        def _():
            dg_ref[...] = jnp.zeros_like(dg_ref)

        dg_ref[...] += dgp
        dxh = dxn * g_ref[...]
        dx_ref[...] = dres_ref[...] + rstd * (dxh - xhat * jnp.mean(dxh * xhat, axis=-1, keepdims=True))

    return pl.pallas_call(
        body, grid=(T // tm,), name=name,
        in_specs=[_row_spec(tm, D), _const_spec((1, D)), _const_spec(w_bf.shape), _row_spec(tm, D)]
        + [_row_spec(tm, s) for s in widths] + extra_specs,
        out_specs=[_row_spec(tm, D), _const_spec((1, D))],
        out_shape=[jax.ShapeDtypeStruct((T, D), F32), jax.ShapeDtypeStruct((1, D), F32)],
        compiler_params=_cparams(("arbitrary",), VMEM_BIG),
    )(x, g, w_bf, dres, *dps, *extra)


def _after_operand(after):
    return ([ANY], [after]) if after is not None else ([], [])


def matmul_acc_chips(at_bf, pieces, name, after=None):
    k = at_bf.shape[0]
    widths = [p.shape[1] for p in pieces]
    nb = sum(widths) // NCHIP
    tm = 512
    steps = T // tm
    extra_specs, extra = _after_operand(after)

    def body(a_ref, *rest):
        o_ref, acc = rest[-2:]

        @pl.when(pl.program_id(0) == 0)
        def _():
            acc[...] = jnp.zeros_like(acc)

        a = a_ref[...]
        b = jnp.concatenate([r[...] for r in rest[:len(widths)]], axis=-1)
        for s in range(NCHIP):
            acc[s] += jnp.dot(a, b[:, s * nb:(s + 1) * nb], preferred_element_type=F32)

        @pl.when(pl.program_id(0) == steps - 1)
        def _():
            o_ref[...] = acc[...].astype(BF16)

    return pl.pallas_call(
        body, grid=(steps,), name=name,
        in_specs=[_col_spec(k, tm)] + [_row_spec(tm, w_) for w_ in widths] + extra_specs,
        out_specs=_const_spec((NCHIP, k, nb)),
        out_shape=jax.ShapeDtypeStruct((NCHIP, k, nb), BF16),
        scratch_shapes=[pltpu.VMEM((NCHIP, k, nb), F32)],
        compiler_params=_cparams(("arbitrary",), VMEM_BIG),
    )(at_bf, *pieces, *extra)


def out_proj(h, z_bf, w_bf, name):
    tm = 512

    def body(h_ref, z_ref, w_ref, o_ref):
        o_ref[...] = h_ref[...] + jnp.dot(z_ref[...], w_ref[...], preferred_element_type=F32)

    return pl.pallas_call(
        body, grid=(T // tm,), name=name,
        in_specs=[_row_spec(tm, D), _row_spec(tm, D), _const_spec((D, D))],
        out_specs=_row_spec(tm, D), out_shape=jax.ShapeDtypeStruct((T, D), F32),
        compiler_params=_cparams(("parallel",)),
    )(h, z_bf, w_bf)


def _out_proj_back(dh_ref, zt_ref, w_ref, dw_ref):
    dhb = dh_ref[...].astype(BF16)

    @pl.when(pl.program_id(0) == 0)
    def _():
        dw_ref[...] = jnp.zeros_like(dw_ref)

    dw_ref[...] += jnp.dot(zt_ref[...], dhb, preferred_element_type=F32)
    return lax.dot_general(dhb, w_ref[...], (((1,), (1,)), ((), ())), preferred_element_type=F32)


def out_proj_loss(h, z_bf, w_bf, g, target):
    tm = 512

    def body(h_ref, z_ref, w_ref, g_ref, t_ref, dh_ref, loss_ref, dg_ref):
        xv = h_ref[...] + jnp.dot(z_ref[...], w_ref[...], preferred_element_type=F32)
        rstd = lax.rsqrt(jnp.mean(xv * xv, axis=-1, keepdims=True) + RMS_EPS)
        xhat = xv * rstd
        err = xhat * g_ref[...] - t_ref[...]
        part = 0.5 * jnp.sum(jnp.mean(err * err, axis=-1, keepdims=True), axis=0, keepdims=True)
        dout = err * (1.0 / D)

        @pl.when(pl.program_id(0) == 0)
        def _():
            loss_ref[...] = jnp.zeros_like(loss_ref)
            dg_ref[...] = jnp.zeros_like(dg_ref)

        loss_ref[...] += jnp.broadcast_to(part, loss_ref.shape)
        dg_ref[...] += jnp.sum(dout * xhat, axis=0, keepdims=True)
        dxh = dout * g_ref[...]
        dh_ref[...] = rstd * (dxh - xhat * jnp.mean(dxh * xhat, axis=-1, keepdims=True))

    return pl.pallas_call(
        body, grid=(T // tm,), name="out_proj_loss",
        in_specs=[_row_spec(tm, D), _row_spec(tm, D), _const_spec((D, D)), _const_spec((1, D)), _row_spec(tm, D)],
        out_specs=[_row_spec(tm, D), _const_spec((8, 128)), _const_spec((1, D))],
        out_shape=[jax.ShapeDtypeStruct((T, D), F32), jax.ShapeDtypeStruct((8, 128), F32),
                   jax.ShapeDtypeStruct((1, D), F32)],
        compiler_params=_cparams(("arbitrary",)),
    )(h, z_bf, w_bf, g, target)


PREP_TM = 256
PREP_NB = SEQ // PREP_TM


def _prep_elem(k, wl, apre, kkw, kaw, bd):
    wraw = -_softplus(-wl) - 0.5
    lw = -jnp.exp(wraw)
    asig = jax.nn.sigmoid(apre)
    kkr = k * kkw
    nrm = jnp.maximum(jnp.sqrt(_headsum(kkr * kkr, bd)), 1e-12)
    kk = kkr / nrm
    k2 = k * (1.0 + (asig - 1.0) * kaw)
    return lw, k2, -kk, kk * asig


def _shifted(ps_ref, prev_ref, mu, blk):
    p = ps_ref[...]
    first = (blk % PREP_NB) == 0
    prev_row = jnp.where(first, 0.0, prev_ref[7:8, :])
    rolled = pltpu.roll(p, 1, 0)
    p_prev = jnp.where(_iota2(p.shape, 0) == 0, prev_row, rolled)
    return p, p_prev, p + (p_prev - p) * mu


def _prev_spec(width, blk_of):
    return pl.BlockSpec((8, width), lambda i: (jnp.maximum(blk_of(i) * (PREP_TM // 8) - 1, 0), 0))


def even_prep(ps, mu, w0, w2x, a0, a2x, kkw, kaw):
    tm = PREP_TM

    def body(ps_ref, prev_ref, mu_ref, w0_ref, w2_ref, a0_ref, a2_ref, kk_ref, ka_ref,
             r_ref, lw_ref, k2_ref, v_ref, aa_ref, bb_ref):
        _, _, s = _shifted(ps_ref, prev_ref, mu_ref[...], pl.program_id(0))
        wa = s[:, 3 * W:]
        wl = w0_ref[...] + _bdot(jnp.tanh(wa), w2_ref[...])
        apre = a0_ref[...] + _bdot(wa, a2_ref[...])
        lw, k2, aa, bb = _prep_elem(s[:, W:2 * W], wl, apre, kk_ref[...], ka_ref[...], _head_blockdiag())
        r_ref[...] = s[:, 0:W]
        v_ref[...] = s[:, 2 * W:3 * W]
        lw_ref[...] = lw
        k2_ref[...] = k2
        aa_ref[...] = aa
        bb_ref[...] = bb

    vec = _const_spec((1, W))
    return pl.pallas_call(
        body, grid=(T // tm,), name="even_prep",
        in_specs=[_row_spec(tm, SHIFT), _prev_spec(SHIFT, lambda i: i), _const_spec((1, SHIFT)), vec,
                  _const_spec((2 * LORA, W)), vec, _const_spec((2 * LORA, W)), vec, vec],
        out_specs=[_row_spec(tm, W)] * 6,
        out_shape=[jax.ShapeDtypeStruct((T, W), F32)] * 6,
        compiler_params=_cparams(("parallel",)),
    )(ps, ps, mu, w0, w2x, a0, a2x, kkw, kaw)


def even_prep_bwd(ps, mu, w0, w2x, a0, a2x, kkw, kaw, dr, dlw, dk2, dv, daa, dbb, dr2, dk22, dv2):
    tm = PREP_TM
    nb = T // tm
    rev = lambda i: nb - 1 - i

    def body(ps_ref, prev_ref, mu_ref, w0_ref, w2_ref, a0_ref, a2_ref, kk_ref, ka_ref,
             dr_ref, dlw_ref, dk2_ref, dv_ref, daa_ref, dbb_ref, dr2_ref, dk22_ref, dv2_ref,
             dps_ref, dmu_ref, dw0_ref, dw2_ref, da0_ref, da2_ref, dkk_ref, dka_ref, carry):
        i = pl.program_id(0)
        blk = rev(i)
        mu_v = mu_ref[...]
        p, p_prev, s = _shifted(ps_ref, prev_ref, mu_v, blk)
        wa = s[:, 3 * W:]
        th = jnp.tanh(wa)
        wl = w0_ref[...] + _bdot(th, w2_ref[...])
        apre = a0_ref[...] + _bdot(wa, a2_ref[...])
        bd = _head_blockdiag()
        k = s[:, W:2 * W]
        _, vjp = jax.vjp(lambda k_, wl_, ap_, kkw_, kaw_: _prep_elem(k_, wl_, ap_, kkw_, kaw_, bd),
                         k, wl, apre, kk_ref[...], ka_ref[...])
        dk, dwl, dap, dkkw, dkaw = vjp((dlw_ref[...], dk2_ref[...] + dk22_ref[...], daa_ref[...], dbb_ref[...]))
        dwa = _bdot_nt(dwl, w2_ref[...]) * (1.0 - th * th) + _bdot_nt(dap, a2_ref[...])
        ds = jnp.concatenate([dr_ref[...] + dr2_ref[...], dk, dv_ref[...] + dv2_ref[...], dwa], axis=-1)

        @pl.when(i == 0)
        def _():
            for ref in (dmu_ref, dw0_ref, dw2_ref, da0_ref, da2_ref, dkk_ref, dka_ref, carry):
                ref[...] = jnp.zeros_like(ref)

        dmu_ref[...] += jnp.sum(ds * (p_prev - p), axis=0, keepdims=True)
        dw0_ref[...] += jnp.sum(dwl, axis=0, keepdims=True)
        da0_ref[...] += jnp.sum(dap, axis=0, keepdims=True)
        dw2_ref[...] += _bdot_tn(th, dwl)
        da2_ref[...] += _bdot_tn(wa, dap)
        dkk_ref[...] += dkkw
        dka_ref[...] += dkaw
        dsm = ds * mu_v
        last = (blk % PREP_NB) == PREP_NB - 1
        nxt = jnp.where(last, 0.0, carry[0:1, :])
        up = pltpu.roll(dsm, tm - 1, 0)
        up = jnp.where(_iota2(up.shape, 0) == tm - 1, nxt, up)
        dps_ref[...] = (ds - dsm + up).astype(BF16)
        carry[0:1, :] = dsm[0:1, :]

    vec = _const_spec((1, W))
    rrow = lambda width: pl.BlockSpec((tm, width), lambda i: (rev(i), 0))
    return pl.pallas_call(
        body, grid=(nb,), name="even_prep_bwd",
        in_specs=[rrow(SHIFT), _prev_spec(SHIFT, rev), _const_spec((1, SHIFT)), vec,
                  _const_spec((2 * LORA, W)), vec, _const_spec((2 * LORA, W)), vec, vec] + [rrow(W)] * 9,
        out_specs=[rrow(SHIFT), _const_spec((1, SHIFT)), vec, _const_spec((2 * LORA, W)), vec,
                   _const_spec((2 * LORA, W)), vec, vec],
        out_shape=[jax.ShapeDtypeStruct((T, SHIFT), BF16), jax.ShapeDtypeStruct((1, SHIFT), F32),
                   jax.ShapeDtypeStruct((1, W), F32), jax.ShapeDtypeStruct((2 * LORA, W), F32),
                   jax.ShapeDtypeStruct((1, W), F32), jax.ShapeDtypeStruct((2 * LORA, W), F32),
                   jax.ShapeDtypeStruct((1, W), F32), jax.ShapeDtypeStruct((1, W), F32)],
        scratch_shapes=[pltpu.VMEM((8, SHIFT), F32)],
        compiler_params=_cparams(("arbitrary",), VMEM_BIG),
    )(ps, ps, mu, w0, w2x, a0, a2x, kkw, kaw, dr, dlw, dk2, dv, daa, dbb, dr2, dk22, dv2)


NPAIR = NH // 2
PW = 2 * HD


def _pair_cols(p):
    return slice(p * PW, (p + 1) * PW)


def _pairs(a):
    return [a[:, _pair_cols(p)] for p in range(NPAIR)]


def _stack_pair(a):
    first = _iota2(a.shape, 1) < HD
    zero = jnp.zeros_like(a)
    return jnp.concatenate([jnp.where(first, a, zero), jnp.where(first, zero, a)], axis=0)


def _unstack_pair(a):
    n = a.shape[0] // 2
    return jnp.where(_iota2((n, PW), 1) < HD, a[:n], a[n:])


def _fold_pair(a):
    n = a.shape[0] // 2
    return a[:n] + a[n:]


def _chunk_masks():
    n = 4 * L
    row = _iota2((n, n), 0)
    col = _iota2((n, n), 1)
    same = ((row // L) & 1) == ((col // L) & 1)
    ri = row & (L - 1)
    ci = col & (L - 1)
    keep = same & (((row < 2 * L) & (ri > ci)) | ((row >= 2 * L) & (ri >= ci)))
    r1 = _iota2((L, L), 0)
    c1 = _iota2((L, L), 1)
    r2 = _iota2((2 * L, 2 * L), 0)
    c2 = _iota2((2 * L, 2 * L), 1)
    return keep.astype(F32), (r1 >= c1).astype(F32), (r2 == c2).astype(F32)


def _scaled(r, lw, k2, aa, bb, tri):
    g = _hdot(tri, lw)
    eg = jnp.exp(g)
    eng = jnp.exp(-g)
    egp = jnp.exp(g - lw)
    return eg, eng, egp, aa * egp, r * eg, bb * eng, k2 * eng


def _head_cols(h):
    return slice(h * HD, (h + 1) * HD)


def _per_head(a):
    return [a[:, _head_cols(h)] for h in range(NH)]


def _pairs_operands(at, rt, bt, kt):
    x = [jnp.concatenate([_stack_pair(a), _stack_pair(r)], axis=0).astype(BF16) for a, r in zip(_pairs(at), _pairs(rt))]
    yk = [jnp.concatenate([_stack_pair(b), _stack_pair(k)], axis=0).astype(BF16) for b, k in zip(_pairs(bt), _pairs(kt))]
    return x, yk


def _pairs_matrices(x, yk, keep, eye):
    m = [_bdot_nt(a, b) * keep for a, b in zip(x, yk)]
    p = [a[:2 * L, :2 * L] for a in m]
    tinv = [eye + a for a in p]
    for _ in range(5):
        p = [_bdot(a, a) for a in p]
        tinv = [t + _bdot(t, a) for t, a in zip(tinv, p)]
    return [a.astype(BF16) for a in m], [a.astype(BF16) for a in tinv]


def _pairs_fwd(x, yk, m, tinv, vw, s0, egl):
    xh = [_bdot_nt(a, s) for a, s in zip(x, s0)]
    u = [_bdot(t, h[:2 * L] + _bdot(a[:2 * L, 2 * L:], w)) for t, h, a, w in zip(tinv, xh, m, vw)]
    uv = [jnp.concatenate([a, w], axis=0).astype(BF16) for a, w in zip(u, vw)]
    y = [h[2 * L:] + _bdot(a[2 * L:], w) for h, a, w in zip(xh, m, uv)]
    sn = [e * (s + _bdot_tn(w, b)) for e, s, w, b in zip(egl, s0, uv, yk)]
    return y, sn, uv


def _pairs_bwd(x, yk, m, tinv, uv, s0, sn, egl, dyw, dsn, keep):
    dzs = [d * e for d, e in zip(dsn, egl)]
    dgl = [jnp.sum(d * s, axis=0, keepdims=True) for d, s in zip(dsn, sn)]
    dyb = [a.astype(BF16) for a in dyw]
    t1 = [_bdot_tn(a[2 * L:], d) for a, d in zip(m, dyb)]
    t2 = [_bdot_nt(b, d) for b, d in zip(yk, dzs)]
    drhs = [_bdot_tn(t, a[:2 * L] + b[:2 * L]) for t, a, b in zip(tinv, t1, t2)]
    dv = [a[2 * L:] + b[2 * L:] + _bdot_tn(c[:2 * L, 2 * L:], d) for a, b, c, d in zip(t1, t2, m, drhs)]
    gg = [jnp.concatenate([a, b], axis=0).astype(BF16) for a, b in zip(drhs, dyw)]
    ds0 = [d + _bdot_tn(g, a) for d, g, a in zip(dzs, gg, x)]
    dm = [_bdot_nt(g, w) * keep for g, w in zip(gg, uv)]
    dx = [_bdot(g, s) + _bdot(d, b) for g, s, d, b in zip(gg, s0, dm, yk)]
    dyk = [_bdot_tn(d, a) + _bdot(w, z) for d, a, w, z in zip(dm, x, uv, dzs)]
    return dx, dyk, dv, dgl, ds0


STATE_SHAPE = (NPAIR * PW, PW)
M_SHAPE = (4 * L, NPAIR * 4 * L)
TINV_SHAPE = (2 * L, NPAIR * 2 * L)


def _rows_of(a, n):
    return [a[i * n:(i + 1) * n, :] for i in range(NPAIR)]


def _both(f):
    out = []
    for s in range(NSEQ):
        out += f(s)
    return out


def _seq_view(a):
    return a.reshape(NSEQ, SEQ, a.shape[-1])


UV_SHAPE = (4 * L, NPAIR * PW)
MATS_CHUNKS = 2


def rwkv_mats(r, lw, k2, aa, bb):
    def body(r_ref, lw_ref, k2_ref, aa_ref, bb_ref, x_ref, yk_ref, m_ref, t_ref, eg_ref):
        keep, tri, eye = _chunk_masks()
        where = [(j, s) for j in range(MATS_CHUNKS) for s in range(NSEQ)]
        rows = lambda j: slice(j * L, (j + 1) * L)
        sc = [_scaled(r_ref[s, rows(j)], lw_ref[s, rows(j)], k2_ref[s, rows(j)], aa_ref[s, rows(j)],
                      bb_ref[s, rows(j)], tri) for j, s in where]
        ops = [_pairs_operands(*a[3:]) for a in sc]
        x = [a for o in ops for a in o[0]]
        yk = [a for o in ops for a in o[1]]
        m, tinv = _pairs_matrices(x, yk, keep, eye)
        for i, (j, s) in enumerate(where):
            mine = slice(i * NPAIR, (i + 1) * NPAIR)
            x_ref[j, s] = jnp.concatenate(x[mine], axis=-1)
            yk_ref[j, s] = jnp.concatenate(yk[mine], axis=-1)
            m_ref[j, s] = jnp.concatenate(m[mine], axis=-1)
            t_ref[j, s] = jnp.concatenate(tinv[mine], axis=-1)
            eg_ref[j, s] = sc[i][0][L - 1:L, :]

    blk = pl.BlockSpec((NSEQ, MATS_CHUNKS * L, W), lambda c: (0, c, 0))
    per_chunk = lambda shape: pl.BlockSpec((MATS_CHUNKS, NSEQ) + shape, lambda c: (c, 0, 0, 0))
    shapes = [(UV_SHAPE, BF16), (UV_SHAPE, BF16), (M_SHAPE, BF16), (TINV_SHAPE, BF16), ((1, W), F32)]
    return pl.pallas_call(
        body, grid=(NC // MATS_CHUNKS,), name="rwkv_mats",
        in_specs=[blk] * 5,
        out_specs=[per_chunk(shape) for shape, _ in shapes],
        out_shape=[jax.ShapeDtypeStruct((NC, NSEQ) + shape, dt) for shape, dt in shapes],
        compiler_params=_cparams(("parallel",), VMEM_BIG),
    )(*[_seq_view(a) for a in (r, lw, k2, aa, bb)])


def rwkv_fwd(v, mats):
    def body(v_ref, x_ref, yk_ref, m_ref, t_ref, eg_ref, y_ref, hs_ref, hn_ref, uv_ref, state):
        @pl.when(pl.program_id(0) == 0)
        def _():
            state[...] = jnp.zeros_like(state)

        s_all = state[...]
        hs_ref[0] = s_all
        x = _both(lambda s: _pairs(x_ref[0, s]))
        yk = _both(lambda s: _pairs(yk_ref[0, s]))
        m = _both(lambda s: [m_ref[0, s][:, i * 4 * L:(i + 1) * 4 * L] for i in range(NPAIR)])
        tinv = _both(lambda s: [t_ref[0, s][:, i * 2 * L:(i + 1) * 2 * L] for i in range(NPAIR)])
        vw = _both(lambda s: [_stack_pair(a) for a in _pairs(v_ref[s])])
        s0 = _both(lambda s: _rows_of(s_all[s], PW))
        y, sn, uv = _pairs_fwd(x, yk, m, tinv, vw, s0, _both(lambda s: _pairs(eg_ref[0, s])))
        for s in range(NSEQ):
            mine = slice(s * NPAIR, (s + 1) * NPAIR)
            y_ref[s] = jnp.concatenate([_fold_pair(a) for a in y[mine]], axis=-1)
            uv_ref[0, s] = jnp.concatenate(uv[mine], axis=-1)
            s_new = jnp.concatenate(sn[mine], axis=0)
            hn_ref[0, s] = s_new
            state[s] = s_new

    blk = pl.BlockSpec((NSEQ, L, W), lambda c: (0, c, 0))
    per_chunk = lambda shape: pl.BlockSpec((1, NSEQ) + shape, lambda c: (c, 0, 0, 0))
    saved_shapes = [(STATE_SHAPE, F32), (STATE_SHAPE, F32), (UV_SHAPE, BF16)]
    y, *saved = pl.pallas_call(
        body, grid=(NC,), name="rwkv_fwd",
        in_specs=[blk] + [per_chunk(a.shape[2:]) for a in mats],
        out_specs=[blk] + [per_chunk(shape) for shape, _ in saved_shapes],
        out_shape=[jax.ShapeDtypeStruct((NSEQ, SEQ, W), F32)]
        + [jax.ShapeDtypeStruct((NC, NSEQ) + shape, dt) for shape, dt in saved_shapes],
        scratch_shapes=[pltpu.VMEM((NSEQ,) + STATE_SHAPE, F32)],
        compiler_params=_cparams(("arbitrary",)),
    )(_seq_view(v), *mats)
    return y.reshape(T, W), saved


def rwkv_bwd(r, lw, k2, aa, bb, mats, saved, dy):
    def body(r_ref, lw_ref, k2_ref, aa_ref, bb_ref, x_ref, yk_ref, m_ref, t_ref, eg_ref, hs_ref, hn_ref, uv_ref,
             dy_ref, dr_ref, dlw_ref, dk2_ref, dv_ref, daa_ref, dbb_ref, dstate):
        @pl.when(pl.program_id(0) == 0)
        def _():
            dstate[...] = jnp.zeros_like(dstate)

        keep, tri, _ = _chunk_masks()
        sc = [_scaled(r_ref[s], lw_ref[s], k2_ref[s], aa_ref[s], bb_ref[s], tri) for s in range(NSEQ)]
        x = _both(lambda s: _pairs(x_ref[0, s]))
        yk = _both(lambda s: _pairs(yk_ref[0, s]))
        m = _both(lambda s: [m_ref[0, s][:, i * 4 * L:(i + 1) * 4 * L] for i in range(NPAIR)])
        tinv = _both(lambda s: [t_ref[0, s][:, i * 2 * L:(i + 1) * 2 * L] for i in range(NPAIR)])
        uv = _both(lambda s: _pairs(uv_ref[0, s]))
        dyw = _both(lambda s: [_stack_pair(a) for a in _pairs(dy_ref[s])])
        s0 = _both(lambda s: _rows_of(hs_ref[0, s], PW))
        sn = _both(lambda s: _rows_of(hn_ref[0, s], PW))
        dsn = _both(lambda s: _rows_of(dstate[s], PW))
        egl = _both(lambda s: _pairs(sc[s][0][L - 1:L, :]))
        dx, dyk, dvw, dgl, ds0 = _pairs_bwd(x, yk, m, tinv, uv, s0, sn, egl, dyw, dsn, keep)
        for s in range(NSEQ):
            mine = slice(s * NPAIR, (s + 1) * NPAIR)
            eg, eng, egp, at, rt, bt, kt = sc[s]
            dstate[s] = jnp.concatenate(ds0[mine], axis=0)
            dv_ref[s] = jnp.concatenate([_fold_pair(a) for a in dvw[mine]], axis=-1)
            dat = jnp.concatenate([_fold_pair(a[:2 * L]) for a in dx[mine]], axis=-1)
            drt = jnp.concatenate([_fold_pair(a[2 * L:]) for a in dx[mine]], axis=-1)
            dbt = jnp.concatenate([_fold_pair(a[:2 * L]) for a in dyk[mine]], axis=-1)
            dkt = jnp.concatenate([_fold_pair(a[2 * L:]) for a in dyk[mine]], axis=-1)
            dg = drt * rt - dbt * bt - dkt * kt
            dg = dg + jnp.where(_iota2(dg.shape, 0) == L - 1, jnp.concatenate(dgl[mine], axis=-1), 0.0)
            dgp = dat * at
            dlw_ref[s] = _hdot_tn(tri, dg + dgp) - dgp
            dr_ref[s] = drt * eg
            daa_ref[s] = dat * egp
            dbb_ref[s] = dbt * eng
            dk2_ref[s] = dkt * eng

    blk = pl.BlockSpec((NSEQ, L, W), lambda c: (0, NC - 1 - c, 0))
    per_chunk = lambda shape: pl.BlockSpec((1, NSEQ) + shape, lambda c: (NC - 1 - c, 0, 0, 0))
    outs = pl.pallas_call(
        body, grid=(NC,), name="rwkv_bwd",
        in_specs=[blk] * 5 + [per_chunk(a.shape[2:]) for a in list(mats) + list(saved)] + [blk],
        out_specs=[blk] * 6,
        out_shape=[jax.ShapeDtypeStruct((NSEQ, SEQ, W), F32)] * 6,
        scratch_shapes=[pltpu.VMEM((NSEQ,) + STATE_SHAPE, F32)],
        compiler_params=_cparams(("arbitrary",)),
    )(*[_seq_view(a) for a in (r, lw, k2, aa, bb)], *mats, *saved, _seq_view(dy))
    return [a.reshape(T, W) for a in outs]


def _post_math(y, r, k2, v, ga, o, gb, lng, lnb, rk, bd):
    mu = _headsum(y, bd) * (1.0 / HD)
    yc = y - mu
    var = _headsum(yc * yc, bd) * (1.0 / HD)
    yn = yc * lax.rsqrt(var + GN_EPS) * lng + lnb
    bonus = _headsum(r * k2 * rk, bd) * v
    return (yn + bonus) * _silu(ga), o * _silu(gb)


def even_post(y, r, k2, v, ga, o, gb, lng, lnb, rk):
    tm = 256

    def body(y_ref, r_ref, k2_ref, v_ref, ga_ref, o_ref, gb_ref, lng_ref, lnb_ref, rk_ref, z_ref, zt_ref):
        ya, yb = _post_math(y_ref[...], r_ref[...], k2_ref[...], v_ref[...], ga_ref[...], o_ref[...], gb_ref[...],
                            lng_ref[...], lnb_ref[...], rk_ref[...], _head_blockdiag())
        ya, yb = ya.astype(BF16), yb.astype(BF16)
        z_ref[:, 0:W] = ya
        z_ref[:, W:2 * W] = yb
        zt_ref[0:W, :] = ya.T
        zt_ref[W:2 * W, :] = yb.T

    vec = _const_spec((1, W))
    return pl.pallas_call(
        body, grid=(T // tm,), name="even_post",
        in_specs=[_row_spec(tm, W)] * 7 + [vec] * 3,
        out_specs=[_row_spec(tm, D), _col_spec(D, tm)],
        out_shape=[jax.ShapeDtypeStruct((T, D), BF16), jax.ShapeDtypeStruct((D, T), BF16)],
        compiler_params=_cparams(("parallel",)),
    )(y, r, k2, v, ga, o, gb, lng, lnb, rk)


def even_post_bwd(y, r, k2, v, ga, o, gb, lng, lnb, rk, dh, zt_bf, w_bf, after=None):
    tm = 256
    extra_specs, extra = _after_operand(after)

    def body(y_ref, r_ref, k2_ref, v_ref, ga_ref, o_ref, gb_ref, lng_ref, lnb_ref, rk_ref, dh_ref, zt_ref, w_ref,
             *rest):
        dy_ref, dr_ref, dk2_ref, dv_ref, dga_ref, do_ref, dgb_ref, dlng_ref, dlnb_ref, drk_ref, dw_ref = rest[-11:]
        dzv = _out_proj_back(dh_ref, zt_ref, w_ref, dw_ref)
        bd = _head_blockdiag()
        _, vjp = jax.vjp(lambda *a: _post_math(*a, bd), y_ref[...], r_ref[...], k2_ref[...], v_ref[...], ga_ref[...],
                         o_ref[...], gb_ref[...], lng_ref[...], lnb_ref[...], rk_ref[...])
        dy, dr, dk2, dv, dga, do, dgb, dlng, dlnb, drk = vjp((dzv[:, 0:W], dzv[:, W:2 * W]))
        for ref, val in ((dy_ref, dy), (dr_ref, dr), (dk2_ref, dk2), (dv_ref, dv), (dga_ref, dga), (do_ref, do),
                         (dgb_ref, dgb)):
            ref[...] = val.astype(ref.dtype)

        @pl.when(pl.program_id(0) == 0)
        def _():
            for ref in (dlng_ref, dlnb_ref, drk_ref):
                ref[...] = jnp.zeros_like(ref)

        dlng_ref[...] += dlng
        dlnb_ref[...] += dlnb
        drk_ref[...] += drk

    vec = _const_spec((1, W))
    return pl.pallas_call(
        body, grid=(T // tm,), name="even_post_bwd",
        in_specs=[_row_spec(tm, W)] * 7 + [vec] * 3 + [_row_spec(tm, D), _col_spec(D, tm), _const_spec((D, D))]
        + extra_specs,
        out_specs=[_row_spec(tm, W)] * 7 + [vec] * 3 + [_const_spec((D, D))],
        out_shape=[jax.ShapeDtypeStruct((T, W), dt) for dt in (F32, F32, F32, F32, BF16, F32, BF16)]
        + [jax.ShapeDtypeStruct((1, W), F32)] * 3 + [jax.ShapeDtypeStruct((D, D), F32)],
        compiler_params=_cparams(("arbitrary",), VMEM_BIG),
    )(y, r, k2, v, ga, o, gb, lng, lnb, rk, dh, zt_bf, w_bf, *extra)


PADSEQ = SEQ + LEFT * L
ATT_SCALE = 1.0 / math.sqrt(HD)
ATT_Q = 4
WIN = BAND + (ATT_Q - 1) * L
ATT_STEPS = NC // ATT_Q
ATT_BIAS_SHAPE = (NPAIR, ATT_Q * 2 * L, WIN)


def _stack_chunks(a):
    return jnp.concatenate([_stack_pair(a[i * L:(i + 1) * L]) for i in range(ATT_Q)], axis=0)


def _unstack_chunks(a):
    return jnp.concatenate([_unstack_pair(a[i * 2 * L:(i + 1) * 2 * L]) for i in range(ATT_Q)], axis=0)


def window_bias(bias):
    parts = [jnp.pad(bias, ((0, 0), (0, 0), (i * L, (ATT_Q - 1 - i) * L)), constant_values=NEG) for i in range(ATT_Q)]
    return jnp.concatenate(parts, axis=1)


def _att_probs(q2, kw, bias, step):
    valid = _iota2((1, WIN), 1) >= (LEFT - step * ATT_Q) * L
    s = [jnp.where(valid, _bdot_nt(a, b) * ATT_SCALE + bias[p], NEG) for p, (a, b) in enumerate(zip(q2, kw))]
    e = [jnp.exp(a - jnp.max(a, axis=-1, keepdims=True)) for a in s]
    return [a / jnp.sum(a, axis=-1, keepdims=True) for a in e]


def attention_fwd(q, kpad, vpad, bias):
    def body(q_ref, k_ref, v_ref, b_ref, o_ref):
        step = pl.program_id(1)
        start = pl.multiple_of(step * (ATT_Q * L), L)
        kw = _pairs(k_ref[pl.ds(start, WIN), :])
        vw = _pairs(v_ref[pl.ds(start, WIN), :])
        q2 = [_stack_chunks(a) for a in _pairs(q_ref[...].astype(BF16))]
        p = _att_probs(q2, kw, b_ref[...], step)
        o_ref[...] = jnp.concatenate([_unstack_chunks(_bdot(a, b)) for a, b in zip(p, vw)], axis=-1)

    qblk = pl.BlockSpec((ATT_Q * L, W), lambda b, c: (b * ATT_STEPS + c, 0))
    kblk = pl.BlockSpec((PADSEQ, W), lambda b, c: (b, 0))
    return pl.pallas_call(
        body, grid=(NSEQ, ATT_STEPS), name="attention_fwd",
        in_specs=[qblk, kblk, kblk, _const_spec(ATT_BIAS_SHAPE)],
        out_specs=qblk, out_shape=jax.ShapeDtypeStruct((T, W), F32),
        compiler_params=_cparams(("parallel", "arbitrary")),
    )(q, kpad, vpad, bias)


def attention_bwd(q, kpad, vpad, bias, do):
    def body(q_ref, k_ref, v_ref, b_ref, do_ref, dq_ref, dko_ref, dvo_ref, db_ref, dk_ref, dv_ref):
        b = pl.program_id(0)
        c = pl.program_id(1)

        @pl.when(c == 0)
        def _():
            dk_ref[...] = jnp.zeros_like(dk_ref)
            dv_ref[...] = jnp.zeros_like(dv_ref)

        @pl.when((c == 0) & (b == 0))
        def _():
            db_ref[...] = jnp.zeros_like(db_ref)

        start = pl.multiple_of(c * (ATT_Q * L), L)
        kw = _pairs(k_ref[pl.ds(start, WIN), :])
        vw = _pairs(v_ref[pl.ds(start, WIN), :])
        q2 = [_stack_chunks(a) for a in _pairs(q_ref[...].astype(BF16))]
        do2 = [_stack_chunks(a) for a in _pairs(do_ref[...].astype(BF16))]
        p = _att_probs(q2, kw, b_ref[...], c)
        dp = [_bdot_nt(a, b) for a, b in zip(do2, vw)]
        ds = [a * (d - jnp.sum(d * a, axis=-1, keepdims=True)) for a, d in zip(p, dp)]
        dss = [(a * ATT_SCALE).astype(BF16) for a in ds]
        dq_ref[...] = jnp.concatenate([_unstack_chunks(_bdot(a, b)) for a, b in zip(dss, kw)], axis=-1).astype(BF16)
        dk_ref[pl.ds(start, WIN), :] += jnp.concatenate([_bdot_tn(a, b) for a, b in zip(dss, q2)], axis=-1)
        dv_ref[pl.ds(start, WIN), :] += jnp.concatenate([_bdot_tn(a, b) for a, b in zip(p, do2)], axis=-1)
        for i in range(NPAIR):
            db_ref[i] += ds[i]

        @pl.when(c == ATT_STEPS - 1)
        def _():
            dko_ref[...] = dk_ref[LEFT * L:, :].astype(BF16)
            dvo_ref[...] = dv_ref[LEFT * L:, :].astype(BF16)

    qblk = pl.BlockSpec((ATT_Q * L, W), lambda b, c: (b * ATT_STEPS + c, 0))
    kblk = pl.BlockSpec((PADSEQ, W), lambda b, c: (b, 0))
    sblk = pl.BlockSpec((SEQ, W), lambda b, c: (b, 0))
    bblk = _const_spec(ATT_BIAS_SHAPE)
    return pl.pallas_call(
        body, grid=(NSEQ, ATT_STEPS), name="attention_bwd",
        in_specs=[qblk, kblk, kblk, bblk, qblk],
        out_specs=[qblk, sblk, sblk, bblk],
        out_shape=[jax.ShapeDtypeStruct((T, W), BF16), jax.ShapeDtypeStruct((T, W), BF16),
                   jax.ShapeDtypeStruct((T, W), BF16), jax.ShapeDtypeStruct(ATT_BIAS_SHAPE, F32)],
        scratch_shapes=[pltpu.VMEM((PADSEQ, W), F32), pltpu.VMEM((PADSEQ, W), F32)],
        compiler_params=_cparams(("arbitrary", "arbitrary"), VMEM_BIG),
    )(q, kpad, vpad, bias, do)


NTAB = 2 * CLIP + 1
EXT = BAND + L


def _ext_onehot():
    n = _iota2((EXT, NTAB), 0)
    m = _iota2((EXT, NTAB), 1)
    return (jnp.clip(BAND - 1 - n, -CLIP, CLIP) + CLIP == m).astype(F32)


def bias_expand(table):
    def body(t_ref, o_ref):
        ext = _hdot_nt(t_ref[...], _ext_onehot())
        for i in range(L):
            s = L - 1 - i
            o_ref[:, i, :] = (pltpu.roll(ext, EXT - s, 1) if s else ext)[:, :BAND]

    return pl.pallas_call(body, name="bias_expand", out_shape=jax.ShapeDtypeStruct((NH, L, BAND), F32))(table)


def bias_grad(dbias):
    def body(d_ref, o_ref):
        acc = jnp.zeros((NH, EXT), F32)
        zpad = jnp.zeros((NH, EXT - BAND), F32)
        for i in range(L):
            s = L - 1 - i
            row = jnp.concatenate([d_ref[:, i, :], zpad], axis=-1)
            acc = acc + (pltpu.roll(row, s, 1) if s else row)
        o_ref[...] = _hdot(acc, _ext_onehot())

    return pl.pallas_call(body, name="bias_grad", out_shape=jax.ShapeDtypeStruct((NH, NTAB), F32))(dbias)


def _group_cols(g):
    return slice(g * SGC, (g + 1) * SGC)


def _sg_norm(gv, lng, lnb):
    gc = gv - jnp.mean(gv, axis=-1, keepdims=True)
    rstd = lax.rsqrt(jnp.mean(gc * gc, axis=-1, keepdims=True) + LN_EPS)
    xhat = gc * rstd
    return xhat, rstd, xhat * lng + lnb


def gmlp_fwd(u, v, gate, lng, lnb, wm_bf, sgb_t):
    def body(u_ref, v_ref, gt_ref, lng_ref, lnb_ref, wm_ref, sb_ref, z_ref, zt_ref):
        _, _, vln = _sg_norm(_gelu(v_ref[...]), lng_ref[...], lnb_ref[...])
        vlb = vln.astype(BF16)
        for g in range(NG):
            cs = _group_cols(g)
            sv = jnp.dot(wm_ref[g], vlb[:, cs], preferred_element_type=F32) + sb_ref[:, g:g + 1]
            zg = (_gelu(u_ref[:, cs]) * sv * _silu(gt_ref[:, cs])).astype(BF16)
            z_ref[:, cs] = zg
            zt_ref[cs, :] = zg.T

    return pl.pallas_call(
        body, grid=(T // SGC,), name="gmlp_fwd",
        in_specs=[_row_spec(SGC, D)] * 3 + [_const_spec((1, D))] * 2 + [_const_spec((NG, SGC, SGC)),
                                                                      _const_spec((SGC, NG))],
        out_specs=[_row_spec(SGC, D), _col_spec(D, SGC)],
        out_shape=[jax.ShapeDtypeStruct((T, D), BF16), jax.ShapeDtypeStruct((D, T), BF16)],
        compiler_params=_cparams(("parallel",)),
    )(u, v, gate, lng, lnb, wm_bf, sgb_t)


GMLP_BWD_CHUNKS = 2


def gmlp_bwd(u, v, gate, lng, lnb, wm_bf, sgb_t, dh, zt_bf, w_bf):
    def body(u_ref, v_ref, gt_ref, lng_ref, lnb_ref, wm_ref, sb_ref, dh_ref, zt_ref, w_ref,
             du_ref, dv_ref, dgt_ref, dlng_ref, dlnb_ref, dwm_ref, dsb_ref, dw_ref):
        @pl.when(pl.program_id(0) == 0)
        def _():
            for ref in (dlng_ref, dlnb_ref, dwm_ref, dsb_ref):
                ref[...] = jnp.zeros_like(ref)

        dz = _out_proj_back(dh_ref, zt_ref, w_ref, dw_ref)
        sel = (_iota2((D, NG), 0) // SGC == _iota2((D, NG), 1)).astype(F32)
        for ch in range(GMLP_BWD_CHUNKS):
            rows = slice(ch * SGC, (ch + 1) * SGC)
            gv, dgv_dv = _gelu_both(v_ref[rows, :])
            xhat, rstd, vln = _sg_norm(gv, lng_ref[...], lnb_ref[...])
            vlb = vln.astype(BF16)
            dvln = []
            dsv_all = []
            for g in range(NG):
                cs = _group_cols(g)
                uu = u_ref[rows, cs]
                gg = gt_ref[rows, cs]
                dzz = dz[rows, cs]
                sv = jnp.dot(wm_ref[g], vlb[:, cs], preferred_element_type=F32) + sb_ref[:, g:g + 1]
                gu, dgu = _gelu_both(uu)
                sg, dsg = _silu_both(gg)
                dzgu = dzz * gu
                dsv = dzgu * sg
                dgt_ref[rows, cs] = (dzgu * sv * dsg).astype(BF16)
                du_ref[rows, cs] = (dzz * sv * sg * dgu).astype(BF16)
                dsb16 = dsv.astype(BF16)
                dvln.append(lax.dot_general(wm_ref[g], dsb16, (((0,), (0,)), ((), ())), preferred_element_type=F32))
                dwm_ref[g] += lax.dot_general(dsb16, vlb[:, cs], (((1,), (1,)), ((), ())),
                                              preferred_element_type=F32)
                dsv_all.append(dsv)
            dvl = jnp.concatenate(dvln, axis=-1)
            dsb_ref[...] += _hdot(jnp.concatenate(dsv_all, axis=-1), sel)
            dlng_ref[...] += jnp.sum(dvl * xhat, axis=0, keepdims=True)
            dlnb_ref[...] += jnp.sum(dvl, axis=0, keepdims=True)
            dxh = dvl * lng_ref[...]
            dgv = rstd * (dxh - jnp.mean(dxh, axis=-1, keepdims=True)
                          - xhat * jnp.mean(dxh * xhat, axis=-1, keepdims=True))
            dv_ref[rows, :] = (dgv * dgv_dv).astype(BF16)

    tm = GMLP_BWD_CHUNKS * SGC
    return pl.pallas_call(
        body, grid=(T // tm,), name="gmlp_bwd",
        in_specs=[_row_spec(tm, D)] * 3 + [_const_spec((1, D))] * 2
        + [_const_spec((NG, SGC, SGC)), _const_spec((SGC, NG)), _row_spec(tm, D), _col_spec(D, tm),
           _const_spec((D, D))],
        out_specs=[_row_spec(tm, D)] * 3 + [_const_spec((1, D))] * 2
        + [_const_spec((NG, SGC, SGC)), _const_spec((SGC, NG)), _const_spec((D, D))],
        out_shape=[jax.ShapeDtypeStruct((T, D), BF16)] * 3 + [jax.ShapeDtypeStruct((1, D), F32)] * 2
        + [jax.ShapeDtypeStruct((NG, SGC, SGC), F32), jax.ShapeDtypeStruct((SGC, NG), F32),
           jax.ShapeDtypeStruct((D, D), F32)],
        compiler_params=_cparams(("arbitrary",), VMEM_BIG),
    )(u, v, gate, lng, lnb, wm_bf, sgb_t, dh, zt_bf, w_bf)


NCHIP = 4
NDEV = 8
ANY = pl.BlockSpec(memory_space=pl.ANY)


HBM = pl.BlockSpec(memory_space=pltpu.HBM)
SEM = pl.BlockSpec(memory_space=pltpu.SEMAPHORE)
EFFECT = pltpu.SideEffectType.DATAFLOW_SIDE_EFFECTING


def _peers(whole_mesh):
    x, y, c = lax.axis_index("x"), lax.axis_index("y"), lax.axis_index("c")
    if not whole_mesh:
        return [((px, py, c), 2 * px + py) for px, py in ((1 - x, y), (x, 1 - y), (1 - x, 1 - y))], 2 * x + y
    out = []
    for j in range(1, NDEV):
        px, py, pc = x ^ (j >> 2), y ^ ((j >> 1) & 1), c ^ (j & 1)
        out.append(((px, py, pc), 4 * px + 2 * py + pc))
    return out, 4 * x + 2 * y + c


def _send_copies(src, land, send, recv, scatter, whole_mesh, starting):
    peers, me = _peers(whole_mesh)
    copies = []
    for t in range(len(src)):
        for j, (dev, slot) in enumerate(peers):
            k = t * len(peers) + j
            copies.append(pltpu.make_async_remote_copy(
                src_ref=src[t].at[slot] if scatter else src[t], dst_ref=land[t].at[me if starting else slot],
                send_sem=send.at[k], recv_sem=recv.at[k], device_id=dev, device_id_type=MESH))
    return copies


def send_start(srcs, lands, scatter, whole_mesh, name):
    n = len(srcs)
    nsem = n * (NDEV - 1 if whole_mesh else NCHIP - 1)

    def body(*refs):
        for cp in _send_copies(refs[:n], refs[n:2 * n], refs[2 * n], refs[2 * n + 1], scatter, whole_mesh, True):
            cp.start()
        refs[-1][...] = jnp.zeros_like(refs[-1])

    arrs = list(srcs) + list(lands)
    out = pl.pallas_call(
        body, name=name,
        out_shape=(pltpu.SemaphoreType.DMA((nsem,)), pltpu.SemaphoreType.DMA((nsem,)),
                   *[pltpu.HBM(a.shape, a.dtype) for a in arrs], jax.ShapeDtypeStruct((8, 128), F32)),
        in_specs=[HBM] * (2 * n), out_specs=(SEM, SEM, *[HBM] * (2 * n), pl.BlockSpec(memory_space=pltpu.VMEM)),
        input_output_aliases={i: 2 + i for i in range(2 * n)},
        compiler_params=pltpu.CompilerParams(has_side_effects=EFFECT),
    )(*[pltpu.with_memory_space_constraint(a, pltpu.HBM) for a in arrs])
    return out[0], out[1], list(out[2:2 + n]), list(out[2 + n:2 + 2 * n]), out[-1]


def send_wait(started, after, scatter, whole_mesh, name):
    send, recv, srcs, lands, _ = started
    n = len(srcs)

    def body(*refs):
        for cp in _send_copies(refs[:n], refs[n:2 * n], refs[2 * n], refs[2 * n + 1], scatter, whole_mesh, False):
            cp.wait_send()
            cp.wait_recv()

    arrs = list(srcs) + list(lands)
    out = pl.pallas_call(
        body, name=name, out_shape=tuple(pltpu.HBM(a.shape, a.dtype) for a in arrs),
        in_specs=[HBM] * (2 * n) + [SEM, SEM, ANY], out_specs=tuple([HBM] * (2 * n)),
        input_output_aliases={i: i for i in range(2 * n)},
        compiler_params=pltpu.CompilerParams(has_side_effects=EFFECT),
    )(*arrs, send, recv, after)
    return list(out[n:])


def exchange_c(arrs, name):
    n = len(arrs)

    def body(*refs):
        ins, outs = refs[:n], refs[n:2 * n]
        send, recv = refs[2 * n:]
        sibling = (lax.axis_index("x"), lax.axis_index("y"), 1 - lax.axis_index("c"))
        copies = [pltpu.make_async_remote_copy(src_ref=ins[t], dst_ref=outs[t], send_sem=send.at[t], recv_sem=recv.at[t],
                                               device_id=sibling, device_id_type=MESH) for t in range(n)]
        for cp in copies:
            cp.start()
        for cp in copies:
            cp.wait()

    return pl.pallas_call(
        body, name=name, in_specs=[ANY] * n, out_specs=[ANY] * n,
        out_shape=[jax.ShapeDtypeStruct(a.shape, a.dtype) for a in arrs],
        scratch_shapes=[pltpu.SemaphoreType.DMA((n,)), pltpu.SemaphoreType.DMA((n,))],
    )(*arrs)


def swap_row_halves(a, name):
    n, rows, cols = a.shape
    half = rows // 2

    def body(in_ref, out_ref, send, recv):
        x, y, c = lax.axis_index("x"), lax.axis_index("y"), lax.axis_index("c")
        cp = pltpu.make_async_remote_copy(src_ref=in_ref.at[:, pl.ds((1 - c) * half, half), :], dst_ref=out_ref,
                                          send_sem=send, recv_sem=recv, device_id=(x, y, 1 - c), device_id_type=MESH)
        cp.start()
        cp.wait()

    return pl.pallas_call(
        body, name=name, in_specs=[ANY], out_specs=ANY, out_shape=jax.ShapeDtypeStruct((n, half, cols), a.dtype),
        scratch_shapes=[pltpu.SemaphoreType.DMA, pltpu.SemaphoreType.DMA],
    )(a)


def add_blocks(a, b, name):
    n, rows, cols = a.shape
    tr = _rows_tile(rows)

    def body(a_ref, b_ref, o_ref):
        o_ref[...] = (a_ref[...].astype(F32) + b_ref[...].astype(F32)).astype(BF16)

    spec = pl.BlockSpec((1, tr, cols), lambda s, i: (s, i, 0))
    return pl.pallas_call(
        body, grid=(n, rows // tr), name=name, in_specs=[spec, spec], out_specs=spec,
        out_shape=jax.ShapeDtypeStruct(a.shape, BF16), compiler_params=_cparams(("parallel", "parallel")),
    )(a, b)


def gather_weights(arrs, split):
    n = len(arrs)

    def body(*refs):
        ins, outs = refs[:n], refs[n:2 * n]
        send1, recv1, send2, recv2, loc = refs[2 * n:]
        x, y, c = lax.axis_index("x"), lax.axis_index("y"), lax.axis_index("c")
        me = 2 * x + y
        sibling = (x, y, 1 - c)
        peers = [(1 - x, y), (x, 1 - y), (1 - x, 1 - y)]

        def rows_of(t, core):
            half = arrs[t].shape[0] // 2
            return pl.ds(core * half, half)

        def part(ref, t, core):
            return ref.at[rows_of(t, core)] if split[t] else ref

        local = [pltpu.make_async_copy(ins[t], outs[t].at[me], loc.at[t]) for t in range(n)]
        for cp in local:
            cp.start()
        first = []
        for t in range(n):
            for j, (px, py) in enumerate(peers):
                first.append(pltpu.make_async_remote_copy(
                    src_ref=part(ins[t], t, c), dst_ref=part(outs[t].at[me], t, c), send_sem=send1.at[t, j],
                    recv_sem=recv1.at[t, j], device_id=(px, py, c), device_id_type=MESH))
        for cp in first:
            cp.start()
        passed = []
        for t in range(n):
            for j, (px, py) in enumerate(peers):
                landed = part(outs[t].at[2 * px + py], t, c)
                pltpu.make_async_remote_copy(
                    src_ref=landed, dst_ref=landed, send_sem=send1.at[t, j], recv_sem=recv1.at[t, j],
                    device_id=(x, y, c), device_id_type=MESH).wait_recv()
                if split[t]:
                    cp = pltpu.make_async_remote_copy(
                        src_ref=landed, dst_ref=landed, send_sem=send2.at[t, j], recv_sem=recv2.at[t, j],
                        device_id=sibling, device_id_type=MESH)
                    cp.start()
                    passed.append(cp)
        for t in range(n):
            for j, (px, py) in enumerate(peers):
                if split[t]:
                    other = part(outs[t].at[2 * px + py], t, 1 - c)
                    pltpu.make_async_remote_copy(
                        src_ref=other, dst_ref=other, send_sem=send2.at[t, j], recv_sem=recv2.at[t, j],
                        device_id=(x, y, c), device_id_type=MESH).wait_recv()
        for cp in first + passed:
            cp.wait_send()
        for cp in local:
            cp.wait()

    return pl.pallas_call(
        body, name="gather_weights", in_specs=[ANY] * n, out_specs=[ANY] * n,
        out_shape=[jax.ShapeDtypeStruct((NCHIP,) + a.shape, a.dtype) for a in arrs],
        scratch_shapes=[pltpu.SemaphoreType.DMA((n, 3))] * 4 + [pltpu.SemaphoreType.DMA((n,))],
    )(*arrs)


def _adam_math(g, w, m, v):
    m = ADAM_B1 * m + (1.0 - ADAM_B1) * g
    v = ADAM_B2 * v + (1.0 - ADAM_B2) * (g * g)
    m_hat = m / (1.0 - ADAM_B1 ** ADAM_STEP)
    v_hat = v / (1.0 - ADAM_B2 ** ADAM_STEP)
    delta = -ADAM_LR * (m_hat / (jnp.sqrt(v_hat) + ADAM_EPS) + ADAM_WD * w)
    return delta, m, v


def _rows_tile(rows):
    return rows if rows <= 256 else 256


def sum_chips(own, parts, name):
    _, rows, cols = parts.shape
    tr = _rows_tile(rows)

    def body(own_ref, p_ref, o_ref):
        acc = own_ref[...].astype(F32)
        for s in range(NCHIP):
            acc = acc + p_ref[s].astype(F32)
        o_ref[...] = acc

    return pl.pallas_call(
        body, grid=(rows // tr,), name=name,
        in_specs=[pl.BlockSpec((tr, cols), lambda i: (i, 0)), pl.BlockSpec((NCHIP, tr, cols), lambda i: (0, i, 0))],
        out_specs=pl.BlockSpec((tr, cols), lambda i: (i, 0)),
        out_shape=jax.ShapeDtypeStruct((rows, cols), F32),
        compiler_params=_cparams(("parallel",)),
    )(own, parts)


def adam_shard(p_mine, p_sib, w, m, v, name):
    rows, cols = p_mine.shape
    tr = _rows_tile(rows)
    lead = w.ndim == 3

    def body(a_ref, b_ref, w_ref, m_ref, v_ref, g_ref, d_ref, mo_ref, vo_ref):
        g = a_ref[...] + b_ref[...]
        g = g[None] if lead else g
        g_ref[...] = g
        d_ref[...], mo_ref[...], vo_ref[...] = _adam_math(g, w_ref[...], m_ref[...], v_ref[...])

    flat = pl.BlockSpec((tr, cols), lambda i: (i, 0))
    spec = pl.BlockSpec((1, tr, cols), lambda i: (0, i, 0)) if lead else flat
    return pl.pallas_call(
        body, grid=(rows // tr,), name=name, in_specs=[flat] * 2 + [spec] * 3, out_specs=[spec] * 4,
        out_shape=[jax.ShapeDtypeStruct(w.shape, F32)] * 4,
        compiler_params=_cparams(("parallel",)),
    )(p_mine, p_sib, w, m, v)


def adam_shard_halves_t(r_mine, r_sib, wt, mt, vt, name):
    hrows, cols = r_mine.shape
    tr = _rows_tile(hrows)
    per_half = hrows // tr

    def body(a_ref, b_ref, w_ref, m_ref, v_ref, g_ref, d_ref, mo_ref, vo_ref):
        mine = pl.program_id(0) == lax.axis_index("c")
        g = jnp.where(mine, a_ref[...], b_ref[...]).T[None]
        g_ref[...] = g
        d_ref[...], mo_ref[...], vo_ref[...] = _adam_math(g, w_ref[...], m_ref[...], v_ref[...])

    flat = pl.BlockSpec((tr, cols), lambda h, i: (i, 0))
    spec = pl.BlockSpec((1, cols, tr), lambda h, i: (0, 0, h * per_half + i))
    return pl.pallas_call(
        body, grid=(2, per_half), name=name, in_specs=[flat] * 2 + [spec] * 3, out_specs=[spec] * 4,
        out_shape=[jax.ShapeDtypeStruct(wt.shape, F32)] * 4,
        compiler_params=_cparams(("parallel", "parallel")),
    )(r_mine, r_sib, wt, mt, vt)


def adam_replicated(parts, w, m, v, name):
    rows = w.shape[0]

    def body(p_ref, w_ref, m_ref, v_ref, g_ref, d_ref, mo_ref, vo_ref):
        g = p_ref[0]
        for d in range(1, NDEV):
            g = g + p_ref[d]
        g_ref[...] = g
        d_ref[...], mo_ref[...], vo_ref[...] = _adam_math(g, w_ref[...], m_ref[...], v_ref[...])

    return pl.pallas_call(
        body, name=name, out_shape=[jax.ShapeDtypeStruct((rows, 128), F32)] * 4,
    )(parts, w, m, v)


def _pack(arrs):
    pieces = []
    for a in arrs:
        flat = a.reshape(-1)
        pad = (-flat.shape[0]) % 128
        pieces.append(jnp.pad(flat, (0, pad)) if pad else flat)
    flat = jnp.concatenate(pieces)
    pad = (-flat.shape[0]) % 1024
    return jnp.pad(flat, (0, pad)).reshape(-1, 128)


def _unpack(buf, shapes):
    flat = buf.reshape(-1)
    out = []
    o = 0
    for s in shapes:
        n = int(np.prod(s))
        out.append(flat[o:o + n].reshape(s))
        o += n + (-n) % 128
    return out


EVEN_SPLITS = (SHIFT, W, W, W, W, W)
ODD_SPLITS = (D, D, D)


def _cols_to_chips(a):
    rows, cols = a.shape
    return a.reshape(rows, NCHIP, cols // NCHIP).transpose(1, 0, 2)


def _chips_to_cols(a):
    _, rows, n = a.shape
    return a.transpose(1, 0, 2).reshape(rows, NCHIP * n)


def kernel(x, norm_g, w_in_e, shift_mu, rw_w0, rw_w2, rw_a0, rw_a2, rw_kk, rw_ka, rw_rk, rw_lnx_g, rw_lnx_b, att_bias, w_out_e, w_in_o, sg_ln_g, sg_ln_b, sg_w, sg_b, w_out_o, final_g, loss_target, m_norm_g, m_w_in_e, m_shift_mu, m_rw_w0, m_rw_w2, m_rw_a0, m_rw_a2, m_rw_kk, m_rw_ka, m_rw_rk, m_rw_lnx_g, m_rw_lnx_b, m_att_bias, m_w_out_e, m_w_in_o, m_sg_ln_g, m_sg_ln_b, m_sg_w, m_sg_b, m_w_out_o, m_final_g, v_norm_g, v_w_in_e, v_shift_mu, v_rw_w0, v_rw_w2, v_rw_a0, v_rw_a2, v_rw_kk, v_rw_ka, v_rw_rk, v_rw_lnx_g, v_rw_lnx_b, v_att_bias, v_w_out_e, v_w_in_o, v_sg_ln_g, v_sg_ln_b, v_sg_w, v_sg_b, v_w_out_o, v_final_g):
    x2 = x.reshape(T, D)
    tgt = loss_target.reshape(T, D)

    my_chip = 2 * lax.axis_index("x") + lax.axis_index("y")
    gathered = gather_weights(
        [jnp.swapaxes(w_in_e[0], 0, 1).astype(BF16), jnp.concatenate([rw_w2[0], rw_a2[0]], axis=0),
         jnp.concatenate([sg_ln_g, sg_ln_b], axis=0)], [True, True, False])
    wie = gathered[0].reshape(EVEN_IN, D)
    w2 = _chips_to_cols(gathered[1][:, :LORA])
    a2 = _chips_to_cols(gathered[1][:, LORA:])
    sglg = _chips_to_cols(gathered[2][:, 0:1])
    sglb = _chips_to_cols(gathered[2][:, 1:2])

    late = [w_out_e[0].astype(BF16), w_in_o[0].astype(BF16), w_out_o[0].astype(BF16)]
    late_started = send_start(late, [jnp.broadcast_to(a[None], (NCHIP,) + a.shape) for a in late], False, False,
                              "late_weights_start")

    def late_weights(after):
        woe, wio, woo = send_wait(late_started, after, False, False, "late_weights_wait")
        return woe.reshape(D, D), _chips_to_cols(wio), woo.reshape(D, D)

    def scatter_start(grads, name):
        srcs = [g_.astype(BF16) if g_.shape[-1] >= W else g_ for g_ in grads]
        return send_start(srcs, [jnp.zeros_like(s) for s in srcs], True, False, name)

    def own_block(g_):
        return lax.dynamic_index_in_dim(g_, my_chip, axis=0, keepdims=False)

    started = {}

    def on_odd_grads(d_woo, d_wio):
        blocks = [d_woo.reshape(NCHIP, D // NCHIP, D), d_wio]
        started["odd"] = (scatter_start(blocks, "odd_grads_start"), [own_block(b) for b in blocks])
        return started["odd"][0][-1]

    def on_even_grads(big_g):
        d_wie, d_woe, _, _, d_w2, d_a2, d_sglg, d_sglb = big_g
        my_half = lax.dynamic_slice_in_dim(d_wie, lax.axis_index("c") * (D // 2), D // 2, axis=1)
        d_wie_half = add_blocks(my_half, swap_row_halves(d_wie, "swap_w_in_e_halves"), "add_w_in_e_halves")
        blocks = [d_wie_half, d_woe.reshape(NCHIP, D // NCHIP, D), _cols_to_chips(d_w2), _cols_to_chips(d_a2),
                  _cols_to_chips(d_sglg), _cols_to_chips(d_sglb)]
        started["even"] = (scatter_start(blocks, "even_grads_start"), [own_block(b) for b in blocks])
        return started["even"][0][-1]

    def on_small_grads(layer, grads):
        mine = _pack(grads)
        started[layer + "_small"] = send_start([mine], [jnp.broadcast_to(mine[None], (NDEV,) + mine.shape)], False,
                                               True, layer + "_small_grads_start")
        return started[layer + "_small"][-1]

    loss_part, dx, _, _ = _local_step(
        x2, tgt, wie, late_weights, w2, a2, sglg, sglb, norm_g, shift_mu, rw_w0, rw_a0, rw_kk, rw_ka, rw_rk,
        rw_lnx_g, rw_lnx_b, att_bias, sg_w, sg_b, final_g, first_after=late_started[-1], on_odd_grads=on_odd_grads,
        on_even_grads=on_even_grads, on_small_grads=on_small_grads)
    even_started, even_own = started["even"]

    wmv = {"w_in_e": tuple(jnp.swapaxes(a, 1, 2) for a in (w_in_e, m_w_in_e, v_w_in_e)),
           "w_out_e": (w_out_e, m_w_out_e, v_w_out_e),
           "w_in_o": (w_in_o, m_w_in_o, v_w_in_o), "w_out_o": (w_out_o, m_w_out_o, v_w_out_o),
           "rw_w2": (rw_w2, m_rw_w2, v_rw_w2), "rw_a2": (rw_a2, m_rw_a2, v_rw_a2),
           "sg_ln_g": (sg_ln_g, m_sg_ln_g, v_sg_ln_g), "sg_ln_b": (sg_ln_b, m_sg_ln_b, v_sg_ln_b)}
    sharded = {}

    def finish(names, own, landed, tag):
        partial = [sum_chips(o_, p_, "sum_" + nm) for o_, p_, nm in zip(own, landed, names)]
        from_sibling = exchange_c(partial, "swap_partials_" + tag)
        for nm, mine, sib in zip(names, partial, from_sibling):
            if nm == "w_in_e":
                res = adam_shard_halves_t(mine, sib, *wmv[nm], "adam_" + nm)
                sharded[nm] = [jnp.swapaxes(a, 1, 2) for a in res]
            else:
                sharded[nm] = adam_shard(mine, sib, *wmv[nm], "adam_" + nm)
        return partial[0]

    odd_started, odd_own = started["odd"]
    odd_landed = send_wait(odd_started, started["even_small"][-1], True, False, "odd_grads_wait")
    done = finish(["w_out_o", "w_in_o"], odd_own, odd_landed, "odd")

    no_w = jnp.zeros((1, 1), F32)
    groups = {
        "odd": (["sg_w", "sg_b", "final_g", "norm_g1"], [sg_w, sg_b, final_g, norm_g[1:2]],
                [m_sg_w, m_sg_b, m_final_g, m_norm_g[1:2]], [v_sg_w, v_sg_b, v_final_g, v_norm_g[1:2]]),
        "even": (["norm_g0", "shift_mu", "rw_w0", "rw_a0", "rw_kk", "rw_ka", "rw_rk", "rw_lnx_g", "rw_lnx_b",
                  "att_bias", "loss"],
                 [norm_g[0:1], shift_mu, rw_w0, rw_a0, rw_kk, rw_ka, rw_rk, rw_lnx_g, rw_lnx_b, att_bias, no_w],
                 [m_norm_g[0:1], m_shift_mu, m_rw_w0, m_rw_a0, m_rw_kk, m_rw_ka, m_rw_rk, m_rw_lnx_g, m_rw_lnx_b,
                  m_att_bias, no_w],
                 [v_norm_g[0:1], v_shift_mu, v_rw_w0, v_rw_a0, v_rw_kk, v_rw_ka, v_rw_rk, v_rw_lnx_g, v_rw_lnx_b,
                  v_att_bias, no_w]),
    }
    rep = {}
    for layer in ("odd", "even"):
        nms, ws, ms_, vs_ = groups[layer]
        (gathered_g,) = send_wait(started[layer + "_small"], done, False, True, layer + "_small_grads_wait")
        rep_out = adam_replicated(gathered_g, _pack(ws), _pack(ms_), _pack(vs_), "adam_" + layer + "_small")
        done = rep_out[0]
        for nm in nms:
            rep[nm] = []
        for buf in rep_out:
            for nm, a in zip(nms, _unpack(buf, [w_.shape for w_ in ws])):
                rep[nm].append(a)
    rep["norm_g"] = [jnp.concatenate([a, b], axis=0) for a, b in zip(rep["norm_g0"], rep["norm_g1"])]
    even_landed = send_wait(even_started, done, True, False, "even_grads_wait")
    finish(["w_in_e", "w_out_e", "rw_w2", "rw_a2", "sg_ln_g", "sg_ln_b"], even_own, even_landed, "even")

    order = ["norm_g", "w_in_e", "shift_mu", "rw_w0", "rw_w2", "rw_a0", "rw_a2", "rw_kk", "rw_ka", "rw_rk",
             "rw_lnx_g", "rw_lnx_b", "att_bias", "w_out_e", "w_in_o", "sg_ln_g", "sg_ln_b", "sg_w", "sg_b",
             "w_out_o", "final_g"]
    results = {**sharded, **rep}
    outs = [rep["loss"][0].reshape(()), dx.reshape(NSEQ, SEQ, D)]
    for kind in range(4):
        outs += [results[nm][kind] for nm in order]
    return tuple(outs)


def _local_step(x2, tgt, wie_t, late_weights, w2, a2, sglg, sglb, norm_g, shift_mu, rw_w0, rw_a0, rw_kk, rw_ka, rw_rk,
                rw_lnx_g, rw_lnx_b, att_bias, sg_w, sg_b, final_g, first_after=None, on_odd_grads=None,
                on_even_grads=None, on_small_grads=None):
    zl = jnp.zeros((LORA, W), F32)
    w2x = jnp.concatenate([w2, zl], axis=0)
    a2x = jnp.concatenate([zl, a2], axis=0)
    rk = rw_rk.reshape(1, W)
    pos = np.arange(SGC)
    sg_mask = jnp.asarray(((pos[None, :] // L) <= (pos[:, None] // L)).astype(np.float32))
    wm = (sg_w[0] * sg_mask[None]).astype(BF16)
    sgb_t = sg_b[0].T

    xn0, ps, ga, q, kb, vb, gb = ln_in_proj(x2, norm_g[0:1], wie_t, EVEN_SPLITS, "in_proj_even", after=first_after,
                                            w_t=True)
    r, lw, k2, v, aa, bb = even_prep(ps, shift_mu, rw_w0, w2x, rw_a0, a2x, rw_kk, rw_ka)
    rw_mats = rwkv_mats(r, lw, k2, aa, bb)
    y, rw_saved = rwkv_fwd(v, rw_mats)
    bias = window_bias(bias_expand(att_bias[0]).reshape(NPAIR, 2 * L, BAND))

    def padded(a):
        return jnp.pad(a.astype(BF16).reshape(NSEQ, SEQ, W), ((0, 0), (LEFT * L, 0), (0, 0))).reshape(NSEQ * PADSEQ, W)

    kpad, vpad = padded(kb), padded(vb)
    o = attention_fwd(q, kpad, vpad, bias)
    z, zt = even_post(y, r, k2, v, ga, o, gb, rw_lnx_g, rw_lnx_b, rk)
    woe, wio, woo = late_weights(z)
    h1 = out_proj(x2, z, woe, "out_proj_even")
    xn1, u, vv, gt = ln_in_proj(h1, norm_g[1:2], wio, ODD_SPLITS, "in_proj_odd")
    z2, z2t = gmlp_fwd(u, vv, gt, sglg, sglb, wm, sgb_t)
    dh2, loss_part, d_final_g = out_proj_loss(h1, z2, woo, final_g[None], tgt)

    du, dvv, dgt, d_sglg, d_sglb, d_wm, d_sgb_t, d_woo = gmlp_bwd(u, vv, gt, sglg, sglb, wm, sgb_t, dh2, z2t, woo)
    dp_odd = [du, dvv, dgt]
    d_wio = matmul_acc_chips(xn1, dp_odd, "in_proj_odd_dw")
    token = on_odd_grads(d_woo, d_wio) if on_odd_grads else None
    dh1, d_g1 = in_proj_bwd_x(h1, norm_g[1:2], wio, dp_odd, dh2, "in_proj_odd_bwd", after=token)
    odd_small = [d_wm * sg_mask[None], d_sgb_t.T, d_final_g, d_g1]
    token = on_small_grads("odd", odd_small) if on_small_grads else None
    dy, dr2, dk22, dv2, dga, do, dgb, d_lng, d_lnb, d_rk, d_woe = even_post_bwd(
        y, r, k2, v, ga, o, gb, rw_lnx_g, rw_lnx_b, rk, dh1, zt, woe, after=token)
    dq, dkb, dvb, dbias = attention_bwd(q, kpad, vpad, bias, do)
    dbias = sum(dbias[:, i * 2 * L:(i + 1) * 2 * L, i * L:i * L + BAND] for i in range(ATT_Q))
    d_att_bias = bias_grad(dbias.reshape(NH, L, BAND))
    dr, dlw, dk2, dv, daa, dbb = rwkv_bwd(r, lw, k2, aa, bb, rw_mats, rw_saved, dy)
    dps, d_mu, d_w0, d_w2x, d_a0, d_a2x, d_kk, d_ka = even_prep_bwd(
        ps, shift_mu, rw_w0, w2x, rw_a0, a2x, rw_kk, rw_ka, dr, dlw, dk2, dv, daa, dbb, dr2, dk22, dv2)
    dp_even = [dps, dga, dq, dkb, dvb, dgb]
    d_wie = matmul_acc_chips(xn0, dp_even, "in_proj_even_dw")
    big_g = (d_wie, d_woe, d_wio, d_woo, d_w2x[:LORA], d_a2x[LORA:], d_sglg, d_sglb)
    token = on_even_grads(big_g) if on_even_grads else None
    dx, d_g0 = in_proj_bwd_x(x2, norm_g[0:1], wie_t, dp_even, dh1, "in_proj_even_bwd", after=token, w_t=True)
    even_small = [d_g0, d_mu, d_w0, d_a0, d_kk, d_ka, d_rk, d_lng, d_lnb, d_att_bias]
    if on_small_grads:
        on_small_grads("even", even_small + [loss_part[0:1, 0:1]])
    rep_g = [jnp.concatenate([d_g0, d_g1], axis=0)] + even_small[1:] + odd_small[:3]
    return loss_part[0, 0], dx, big_g, rep_g
```

```python
import functools
import math

import jax
import jax.numpy as jnp
import numpy as np
from jax import lax
from jax.experimental import pallas as pl
from jax.experimental.pallas import tpu as pltpu

F32 = jnp.float32
BF16 = jnp.bfloat16
HI = lax.Precision.HIGHEST

D = 1024
SEQ = 2048
NSEQ = 2
T = NSEQ * SEQ
HD = 64
NH = 8
W = 512
SHIFT = 1664
LORA = 64
EVEN_IN = 4224
ODD_IN = 3072
L = 64
NC = SEQ // L
LEFT = 8
BAND = (LEFT + 1) * L
CLIP = 128
SGC = 128
NG = 8
RMS_EPS = 1e-6
LN_EPS = 1e-5
GN_EPS = 64e-5
NEG = -1e30
VMEM_BIG = 56 * 1024 * 1024

ADAM_LR = 0.001
ADAM_B1 = 0.9
ADAM_B2 = 0.999
ADAM_EPS = 1e-08
ADAM_WD = 0.01
ADAM_STEP = 10

MESH = pl.DeviceIdType.MESH


def _bdot(a, b):
    return jnp.dot(a.astype(BF16), b.astype(BF16), preferred_element_type=F32)


def _bdot_nt(a, b):
    return lax.dot_general(a.astype(BF16), b.astype(BF16), (((1,), (1,)), ((), ())), preferred_element_type=F32)


def _bdot_tn(a, b):
    return lax.dot_general(a.astype(BF16), b.astype(BF16), (((0,), (0,)), ((), ())), preferred_element_type=F32)


def _hdot(a, b):
    return jnp.dot(a, b, precision=HI, preferred_element_type=F32)


def _hdot_nt(a, b):
    return lax.dot_general(a, b, (((1,), (1,)), ((), ())), precision=HI, preferred_element_type=F32)


def _hdot_tn(a, b):
    return lax.dot_general(a, b, (((0,), (0,)), ((), ())), precision=HI, preferred_element_type=F32)


def _iota2(shape, dim):
    return lax.broadcasted_iota(jnp.int32, shape, dim)


def _head_blockdiag():
    r = _iota2((W, W), 0) // HD
    c = _iota2((W, W), 1) // HD
    return (r == c).astype(BF16)


def _headsum_impl(x, bd):
    hi = x.astype(BF16)
    mid = (x - hi.astype(F32)).astype(BF16)
    return jnp.dot(hi, bd, preferred_element_type=F32) + jnp.dot(mid, bd, preferred_element_type=F32)


@jax.custom_vjp
def _headsum(x, bd):
    return _headsum_impl(x, bd)


def _headsum_fwd(x, bd):
    return _headsum_impl(x, bd), bd


def _headsum_bwd(bd, ct):
    return _headsum_impl(ct, bd), None


_headsum.defvjp(_headsum_fwd, _headsum_bwd)


def _silu(x):
    return x * jax.nn.sigmoid(x)


def _dsilu(x):
    s = jax.nn.sigmoid(x)
    return s * (1.0 + x * (1.0 - s))


_GELU_C = math.sqrt(2.0 / math.pi)


def _gelu(x):
    return 0.5 * x * (1.0 + jnp.tanh(_GELU_C * (x + 0.044715 * (x * x * x))))


def _dgelu(x):
    t = jnp.tanh(_GELU_C * (x + 0.044715 * (x * x * x)))
    return 0.5 * (1.0 + t) + 0.5 * x * (1.0 - t * t) * _GELU_C * (1.0 + 3.0 * 0.044715 * x * x)


def _silu_both(x):
    s = jax.nn.sigmoid(x)
    xs = x * s
    return xs, s + xs * (1.0 - s)


def _gelu_both(x):
    x2 = x * x
    t = jnp.tanh(_GELU_C * (x + 0.044715 * (x2 * x)))
    half = 0.5 * (1.0 + t)
    return x * half, half + 0.5 * x * (1.0 - t * t) * _GELU_C * (1.0 + 3.0 * 0.044715 * x2)


def _softplus(x):
    return jnp.maximum(x, 0.0) + jnp.log(1.0 + jnp.exp(-jnp.abs(x)))


def _cparams(sem, vmem=None):
    return pltpu.CompilerParams(dimension_semantics=sem, vmem_limit_bytes=vmem)


def _row_spec(tm, width):
    return pl.BlockSpec((tm, width), lambda i: (i, 0))


def _col_spec(height, tm):
    return pl.BlockSpec((height, tm), lambda i: (0, i))


def _const_spec(shape):
    nd = len(shape)
    return pl.BlockSpec(shape, lambda *_: (0,) * nd)


def _weight_dims(w_bf, w_t):
    return (((1,), (1,)), ((), ())) if w_t else (((1,), (0,)), ((), ())), w_bf.shape[0 if w_t else 1]


def ln_in_proj(x, g, w_bf, splits, name, after=None, w_t=False):
    dims, n = _weight_dims(w_bf, w_t)
    tm = 256
    spans = []
    o = 0
    for s in splits:
        spans.append((o, o + s))
        o += s
    assert o == n
    extra_specs, extra = _after_operand(after)

    def body(x_ref, g_ref, w_ref, *rest):
        xn_ref, outs = rest[len(extra)], rest[len(extra) + 1:]
        xv = x_ref[...]
        rstd = lax.rsqrt(jnp.mean(xv * xv, axis=-1, keepdims=True) + RMS_EPS)
        xn = (xv * rstd * g_ref[...]).astype(BF16)
        xn_ref[...] = xn.T
        p = lax.dot_general(xn, w_ref[...], dims, preferred_element_type=F32)
        for o_ref, (a, b) in zip(outs, spans):
            o_ref[...] = p[:, a:b]

    return pl.pallas_call(
        body, grid=(T // tm,), name=name,
        in_specs=[_row_spec(tm, D), _const_spec((1, D)), _const_spec(w_bf.shape)] + extra_specs,
        out_specs=[_col_spec(D, tm)] + [_row_spec(tm, s) for s in splits],
        out_shape=[jax.ShapeDtypeStruct((D, T), BF16)] + [jax.ShapeDtypeStruct((T, s), F32) for s in splits],
        compiler_params=_cparams(("parallel",), VMEM_BIG),
    )(x, g, w_bf, *extra)


def in_proj_bwd_x(x, g, w_bf, dps, dres, name, after=None, w_t=False):
    tm = 512
    back = (((1,), (0,)), ((), ())) if w_t else (((1,), (1,)), ((), ()))
    widths = [d.shape[1] for d in dps]
    extra_specs, extra = _after_operand(after)

    def body(x_ref, g_ref, w_ref, dres_ref, *rest):
        dp_refs = rest[:len(widths)]
        dx_ref, dg_ref = rest[-2:]
        dp = jnp.concatenate([r[...] for r in dp_refs], axis=-1)
        dxn = lax.dot_general(dp, w_ref[...], back, preferred_element_type=F32)
        xv = x_ref[...]
        rstd = lax.rsqrt(jnp.mean(xv * xv, axis=-1, keepdims=True) + RMS_EPS)
        xhat = xv * rstd
        dgp = jnp.sum(dxn * xhat, axis=0, keepdims=True)

        @pl.when(pl.program_id(0) == 0)
        def _():
            dg_ref[...] = jnp.zeros_like(dg_ref)

        dg_ref[...] += dgp
        dxh = dxn * g_ref[...]
        dx_ref[...] = dres_ref[...] + rstd * (dxh - xhat * jnp.mean(dxh * xhat, axis=-1, keepdims=True))

    return pl.pallas_call(
        body, grid=(T // tm,), name=name,
        in_specs=[_row_spec(tm, D), _const_spec((1, D)), _const_spec(w_bf.shape), _row_spec(tm, D)]
        + [_row_spec(tm, s) for s in widths] + extra_specs,
        out_specs=[_row_spec(tm, D), _const_spec((1, D))],
        out_shape=[jax.ShapeDtypeStruct((T, D), F32), jax.ShapeDtypeStruct((1, D), F32)],
        compiler_params=_cparams(("arbitrary",), VMEM_BIG),
    )(x, g, w_bf, dres, *dps, *extra)


def _after_operand(after):
    return ([ANY], [after]) if after is not None else ([], [])


def matmul_acc_chips(at_bf, pieces, name, after=None):
    k = at_bf.shape[0]
    widths = [p.shape[1] for p in pieces]
    nb = sum(widths) // NCHIP
    tm = 512
    steps = T // tm
    extra_specs, extra = _after_operand(after)

    def body(a_ref, *rest):
        o_ref, acc = rest[-2:]

        @pl.when(pl.program_id(0) == 0)
        def _():
            acc[...] = jnp.zeros_like(acc)

        a = a_ref[...]
        b = jnp.concatenate([r[...] for r in rest[:len(widths)]], axis=-1)
        for s in range(NCHIP):
            acc[s] += jnp.dot(a, b[:, s * nb:(s + 1) * nb], preferred_element_type=F32)

        @pl.when(pl.program_id(0) == steps - 1)
        def _():
            o_ref[...] = acc[...].astype(BF16)

    return pl.pallas_call(
        body, grid=(steps,), name=name,
        in_specs=[_col_spec(k, tm)] + [_row_spec(tm, w_) for w_ in widths] + extra_specs,
        out_specs=_const_spec((NCHIP, k, nb)),
        out_shape=jax.ShapeDtypeStruct((NCHIP, k, nb), BF16),
        scratch_shapes=[pltpu.VMEM((NCHIP, k, nb), F32)],
        compiler_params=_cparams(("arbitrary",), VMEM_BIG),
    )(at_bf, *pieces, *extra)


def _out_proj_back(dh_ref, zt_ref, w_ref, dw_ref):
    dhb = dh_ref[...].astype(BF16)

    @pl.when(pl.program_id(0) == 0)
    def _():
        dw_ref[...] = jnp.zeros_like(dw_ref)

    dw_ref[...] += jnp.dot(zt_ref[...], dhb, preferred_element_type=F32)
    return lax.dot_general(dhb, w_ref[...], (((1,), (1,)), ((), ())), preferred_element_type=F32)


PREP_TM = 512
PREP_NB = SEQ // PREP_TM


def _prep_elem(k, wl, apre, kkw, kaw, bd):
    wraw = -_softplus(-wl) - 0.5
    lw = -jnp.exp(wraw)
    asig = jax.nn.sigmoid(apre)
    kkr = k * kkw
    nrm = jnp.maximum(jnp.sqrt(_headsum(kkr * kkr, bd)), 1e-12)
    kk = kkr / nrm
    k2 = k * (1.0 + (asig - 1.0) * kaw)
    return lw, k2, -kk, kk * asig


def _shifted(ps_ref, prev_ref, mu, blk):
    p = ps_ref[...]
    first = (blk % PREP_NB) == 0
    prev_row = jnp.where(first, 0.0, prev_ref[7:8, :])
    rolled = pltpu.roll(p, 1, 0)
    p_prev = jnp.where(_iota2(p.shape, 0) == 0, prev_row, rolled)
    return p, p_prev, p + (p_prev - p) * mu


def _prev_spec(width, blk_of):
    return pl.BlockSpec((8, width), lambda i: (jnp.maximum(blk_of(i) * (PREP_TM // 8) - 1, 0), 0))


def even_prep(ps, mu, w0, w2x, a0, a2x, kkw, kaw):
    tm = PREP_TM

    def body(ps_ref, prev_ref, mu_ref, w0_ref, w2_ref, a0_ref, a2_ref, kk_ref, ka_ref,
             r_ref, lw_ref, k2_ref, v_ref, aa_ref, bb_ref):
        _, _, s = _shifted(ps_ref, prev_ref, mu_ref[...], pl.program_id(0))
        wa = s[:, 3 * W:]
        wl = w0_ref[...] + _bdot(jnp.tanh(wa), w2_ref[...])
        apre = a0_ref[...] + _bdot(wa, a2_ref[...])
        lw, k2, aa, bb = _prep_elem(s[:, W:2 * W], wl, apre, kk_ref[...], ka_ref[...], _head_blockdiag())
        r_ref[...] = s[:, 0:W]
        v_ref[...] = s[:, 2 * W:3 * W]
        lw_ref[...] = lw
        k2_ref[...] = k2
        aa_ref[...] = aa
        bb_ref[...] = bb

    vec = _const_spec((1, W))
    return pl.pallas_call(
        body, grid=(T // tm,), name="even_prep",
        in_specs=[_row_spec(tm, SHIFT), _prev_spec(SHIFT, lambda i: i), _const_spec((1, SHIFT)), vec,
                  _const_spec((2 * LORA, W)), vec, _const_spec((2 * LORA, W)), vec, vec],
        out_specs=[_row_spec(tm, W)] * 6,
        out_shape=[jax.ShapeDtypeStruct((T, W), F32)] * 6,
        compiler_params=_cparams(("parallel",), VMEM_BIG),
    )(ps, ps, mu, w0, w2x, a0, a2x, kkw, kaw)


def even_prep_bwd(ps, mu, w0, w2x, a0, a2x, kkw, kaw, dr, dlw, dk2, dv, daa, dbb, dr2, dk22, dv2):
    tm = PREP_TM
    nb = T // tm
    rev = lambda i: nb - 1 - i

    def body(ps_ref, prev_ref, mu_ref, w0_ref, w2_ref, a0_ref, a2_ref, kk_ref, ka_ref,
             dr_ref, dlw_ref, dk2_ref, dv_ref, daa_ref, dbb_ref, dr2_ref, dk22_ref, dv2_ref,
             dps_ref, dmu_ref, dw0_ref, dw2_ref, da0_ref, da2_ref, dkk_ref, dka_ref, carry):
        i = pl.program_id(0)
        blk = rev(i)
        mu_v = mu_ref[...]
        p, p_prev, s = _shifted(ps_ref, prev_ref, mu_v, blk)
        wa = s[:, 3 * W:]
        th = jnp.tanh(wa)
        wl = w0_ref[...] + _bdot(th, w2_ref[...])
        apre = a0_ref[...] + _bdot(wa, a2_ref[...])
        bd = _head_blockdiag()
        k = s[:, W:2 * W]
        _, vjp = jax.vjp(lambda k_, wl_, ap_, kkw_, kaw_: _prep_elem(k_, wl_, ap_, kkw_, kaw_, bd),
                         k, wl, apre, kk_ref[...], ka_ref[...])
        dk, dwl, dap, dkkw, dkaw = vjp((dlw_ref[...], dk2_ref[...] + dk22_ref[...], daa_ref[...], dbb_ref[...]))
        dwa = _bdot_nt(dwl, w2_ref[...]) * (1.0 - th * th) + _bdot_nt(dap, a2_ref[...])
        ds = jnp.concatenate([dr_ref[...] + dr2_ref[...], dk, dv_ref[...] + dv2_ref[...], dwa], axis=-1)

        @pl.when(i == 0)
        def _():
            for ref in (dmu_ref, dw0_ref, dw2_ref, da0_ref, da2_ref, dkk_ref, dka_ref, carry):
                ref[...] = jnp.zeros_like(ref)

        dmu_ref[...] += jnp.sum(ds * (p_prev - p), axis=0, keepdims=True)
        dw0_ref[...] += jnp.sum(dwl, axis=0, keepdims=True)
        da0_ref[...] += jnp.sum(dap, axis=0, keepdims=True)
        dw2_ref[...] += _bdot_tn(th, dwl)
        da2_ref[...] += _bdot_tn(wa, dap)
        dkk_ref[...] += dkkw
        dka_ref[...] += dkaw
        dsm = ds * mu_v
        last = (blk % PREP_NB) == PREP_NB - 1
        nxt = jnp.where(last, 0.0, carry[0:1, :])
        up = pltpu.roll(dsm, tm - 1, 0)
        up = jnp.where(_iota2(up.shape, 0) == tm - 1, nxt, up)
        dps_ref[...] = (ds - dsm + up).astype(BF16)
        carry[0:1, :] = dsm[0:1, :]

    vec = _const_spec((1, W))
    rrow = lambda width: pl.BlockSpec((tm, width), lambda i: (rev(i), 0))
    return pl.pallas_call(
        body, grid=(nb,), name="even_prep_bwd",
        in_specs=[rrow(SHIFT), _prev_spec(SHIFT, rev), _const_spec((1, SHIFT)), vec,
                  _const_spec((2 * LORA, W)), vec, _const_spec((2 * LORA, W)), vec, vec] + [rrow(W)] * 9,
        out_specs=[rrow(SHIFT), _const_spec((1, SHIFT)), vec, _const_spec((2 * LORA, W)), vec,
                   _const_spec((2 * LORA, W)), vec, vec],
        out_shape=[jax.ShapeDtypeStruct((T, SHIFT), BF16), jax.ShapeDtypeStruct((1, SHIFT), F32),
                   jax.ShapeDtypeStruct((1, W), F32), jax.ShapeDtypeStruct((2 * LORA, W), F32),
                   jax.ShapeDtypeStruct((1, W), F32), jax.ShapeDtypeStruct((2 * LORA, W), F32),
                   jax.ShapeDtypeStruct((1, W), F32), jax.ShapeDtypeStruct((1, W), F32)],
        scratch_shapes=[pltpu.VMEM((8, SHIFT), F32)],
        compiler_params=_cparams(("arbitrary",), VMEM_BIG),
    )(ps, ps, mu, w0, w2x, a0, a2x, kkw, kaw, dr, dlw, dk2, dv, daa, dbb, dr2, dk22, dv2)


NPAIR = NH // 2
PW = 2 * HD


def _pair_cols(p):
    return slice(p * PW, (p + 1) * PW)


def _pairs(a):
    return [a[:, _pair_cols(p)] for p in range(NPAIR)]


def _stack_pair(a):
    first = _iota2(a.shape, 1) < HD
    zero = jnp.zeros_like(a)
    return jnp.concatenate([jnp.where(first, a, zero), jnp.where(first, zero, a)], axis=0)


def _unstack_pair(a):
    n = a.shape[0] // 2
    return jnp.where(_iota2((n, PW), 1) < HD, a[:n], a[n:])


def _fold_pair(a):
    n = a.shape[0] // 2
    return a[:n] + a[n:]


def _chunk_masks():
    n = 4 * L
    row = _iota2((n, n), 0)
    col = _iota2((n, n), 1)
    same = ((row // L) & 1) == ((col // L) & 1)
    ri = row & (L - 1)
    ci = col & (L - 1)
    keep = same & (((row < 2 * L) & (ri > ci)) | ((row >= 2 * L) & (ri >= ci)))
    r1 = _iota2((L, L), 0)
    c1 = _iota2((L, L), 1)
    r2 = _iota2((2 * L, 2 * L), 0)
    c2 = _iota2((2 * L, 2 * L), 1)
    return keep.astype(F32), (r1 >= c1).astype(F32), (r2 == c2).astype(F32)


def _scaled(r, lw, k2, aa, bb, tri):
    g = _hdot(tri, lw)
    eg = jnp.exp(g)
    eng = jnp.exp(-g)
    egp = jnp.exp(g - lw)
    return eg, eng, egp, aa * egp, r * eg, bb * eng, k2 * eng


def _head_cols(h):
    return slice(h * HD, (h + 1) * HD)


def _per_head(a):
    return [a[:, _head_cols(h)] for h in range(NH)]


def _pairs_operands(at, rt, bt, kt):
    x = [jnp.concatenate([_stack_pair(a), _stack_pair(r)], axis=0).astype(BF16) for a, r in zip(_pairs(at), _pairs(rt))]
    yk = [jnp.concatenate([_stack_pair(b), _stack_pair(k)], axis=0).astype(BF16) for b, k in zip(_pairs(bt), _pairs(kt))]
    return x, yk


def _pairs_matrices(x, yk, keep, eye):
    m = [_bdot_nt(a, b) * keep for a, b in zip(x, yk)]
    p = [a[:2 * L, :2 * L] for a in m]
    tinv = [eye + a for a in p]
    for _ in range(5):
        p = [_bdot(a, a) for a in p]
        tinv = [t + _bdot(t, a) for t, a in zip(tinv, p)]
    return [a.astype(BF16) for a in m], [a.astype(BF16) for a in tinv]


def _pairs_fwd(x, yk, m, tinv, vw, s0, egl):
    xh = [_bdot_nt(a, s) for a, s in zip(x, s0)]
    u = [_bdot(t, h[:2 * L] + _bdot(a[:2 * L, 2 * L:], w)) for t, h, a, w in zip(tinv, xh, m, vw)]
    uv = [jnp.concatenate([a, w], axis=0).astype(BF16) for a, w in zip(u, vw)]
    y = [h[2 * L:] + _bdot(a[2 * L:], w) for h, a, w in zip(xh, m, uv)]
    sn = [e * (s + _bdot_tn(w, b)) for e, s, w, b in zip(egl, s0, uv, yk)]
    return y, sn, uv


def _pairs_bwd(x, yk, m, tinv, uv, s0, sn, egl, dyw, dsn, keep):
    dzs = [d * e for d, e in zip(dsn, egl)]
    dgl = [jnp.sum(d * s, axis=0, keepdims=True) for d, s in zip(dsn, sn)]
    dyb = [a.astype(BF16) for a in dyw]
    t1 = [_bdot_tn(a[2 * L:], d) for a, d in zip(m, dyb)]
    t2 = [_bdot_nt(b, d) for b, d in zip(yk, dzs)]
    drhs = [_bdot_tn(t, a[:2 * L] + b[:2 * L]) for t, a, b in zip(tinv, t1, t2)]
    dv = [a[2 * L:] + b[2 * L:] + _bdot_tn(c[:2 * L, 2 * L:], d) for a, b, c, d in zip(t1, t2, m, drhs)]
    gg = [jnp.concatenate([a, b], axis=0).astype(BF16) for a, b in zip(drhs, dyw)]
    ds0 = [d + _bdot_tn(g, a) for d, g, a in zip(dzs, gg, x)]
    dm = [_bdot_nt(g, w) * keep for g, w in zip(gg, uv)]
    dx = [_bdot(g, s) + _bdot(d, b) for g, s, d, b in zip(gg, s0, dm, yk)]
    dyk = [_bdot_tn(d, a) + _bdot(w, z) for d, a, w, z in zip(dm, x, uv, dzs)]
    return dx, dyk, dv, dgl, ds0


STATE_SHAPE = (NPAIR * PW, PW)
M_SHAPE = (4 * L, NPAIR * 4 * L)
TINV_SHAPE = (2 * L, NPAIR * 2 * L)


def _rows_of(a, n):
    return [a[i * n:(i + 1) * n, :] for i in range(NPAIR)]


def _both(f):
    out = []
    for s in range(NSEQ):
        out += f(s)
    return out


def _seq_view(a):
    return a.reshape(NSEQ, SEQ, a.shape[-1])


UV_SHAPE = (4 * L, NPAIR * PW)


def rwkv_fwd(r, lw, k2, v, aa, bb):
    def body(r_ref, lw_ref, k2_ref, v_ref, aa_ref, bb_ref, y_ref, hs_ref, hn_ref, m_ref, t_ref, uv_ref, state):
        @pl.when(pl.program_id(0) == 0)
        def _():
            state[...] = jnp.zeros_like(state)

        s_all = state[...]
        hs_ref[0] = s_all
        keep, tri, eye = _chunk_masks()
        sc = [_scaled(r_ref[s], lw_ref[s], k2_ref[s], aa_ref[s], bb_ref[s], tri) for s in range(NSEQ)]
        ops = [_pairs_operands(*sc[s][3:]) for s in range(NSEQ)]
        x, yk = _both(lambda s: ops[s][0]), _both(lambda s: ops[s][1])
        m, tinv = _pairs_matrices(x, yk, keep, eye)
        vw = _both(lambda s: [_stack_pair(a) for a in _pairs(v_ref[s])])
        s0 = _both(lambda s: _rows_of(s_all[s], PW))
        y, sn, uv = _pairs_fwd(x, yk, m, tinv, vw, s0, _both(lambda s: _pairs(sc[s][0][L - 1:L, :])))
        for s in range(NSEQ):
            mine = slice(s * NPAIR, (s + 1) * NPAIR)
            y_ref[s] = jnp.concatenate([_fold_pair(a) for a in y[mine]], axis=-1)
            m_ref[0, s] = jnp.concatenate(m[mine], axis=-1)
            t_ref[0, s] = jnp.concatenate(tinv[mine], axis=-1)
            uv_ref[0, s] = jnp.concatenate(uv[mine], axis=-1)
            s_new = jnp.concatenate(sn[mine], axis=0)
            hn_ref[0, s] = s_new
            state[s] = s_new

    blk = pl.BlockSpec((NSEQ, L, W), lambda c: (0, c, 0))
    per_chunk = lambda shape: pl.BlockSpec((1, NSEQ) + shape, lambda c: (c, 0, 0, 0))
    saved_shapes = [(STATE_SHAPE, F32), (STATE_SHAPE, F32), (M_SHAPE, BF16), (TINV_SHAPE, BF16), (UV_SHAPE, BF16)]
    y, *saved = pl.pallas_call(
        body, grid=(NC,), name="rwkv_fwd",
        in_specs=[blk] * 6,
        out_specs=[blk] + [per_chunk(shape) for shape, _ in saved_shapes],
        out_shape=[jax.ShapeDtypeStruct((NSEQ, SEQ, W), F32)]
        + [jax.ShapeDtypeStruct((NC, NSEQ) + shape, dt) for shape, dt in saved_shapes],
        scratch_shapes=[pltpu.VMEM((NSEQ,) + STATE_SHAPE, F32)],
        compiler_params=_cparams(("arbitrary",)),
    )(*[_seq_view(a) for a in (r, lw, k2, v, aa, bb)])
    return y.reshape(T, W), saved


def rwkv_bwd(r, lw, k2, aa, bb, saved, dy):
    def body(r_ref, lw_ref, k2_ref, aa_ref, bb_ref, hs_ref, hn_ref, m_ref, t_ref, uv_ref, dy_ref,
             dr_ref, dlw_ref, dk2_ref, dv_ref, daa_ref, dbb_ref, dstate):
        @pl.when(pl.program_id(0) == 0)
        def _():
            dstate[...] = jnp.zeros_like(dstate)

        keep, tri, _ = _chunk_masks()
        sc = [_scaled(r_ref[s], lw_ref[s], k2_ref[s], aa_ref[s], bb_ref[s], tri) for s in range(NSEQ)]
        ops = [_pairs_operands(*sc[s][3:]) for s in range(NSEQ)]
        x, yk = _both(lambda s: ops[s][0]), _both(lambda s: ops[s][1])
        m = _both(lambda s: [m_ref[0, s][:, i * 4 * L:(i + 1) * 4 * L] for i in range(NPAIR)])
        tinv = _both(lambda s: [t_ref[0, s][:, i * 2 * L:(i + 1) * 2 * L] for i in range(NPAIR)])
        uv = _both(lambda s: _pairs(uv_ref[0, s]))
        dyw = _both(lambda s: [_stack_pair(a) for a in _pairs(dy_ref[s])])
        s0 = _both(lambda s: _rows_of(hs_ref[0, s], PW))
        sn = _both(lambda s: _rows_of(hn_ref[0, s], PW))
        dsn = _both(lambda s: _rows_of(dstate[s], PW))
        egl = _both(lambda s: _pairs(sc[s][0][L - 1:L, :]))
        dx, dyk, dvw, dgl, ds0 = _pairs_bwd(x, yk, m, tinv, uv, s0, sn, egl, dyw, dsn, keep)
        for s in range(NSEQ):
            mine = slice(s * NPAIR, (s + 1) * NPAIR)
            eg, eng, egp, at, rt, bt, kt = sc[s]
            dstate[s] = jnp.concatenate(ds0[mine], axis=0)
            dv_ref[s] = jnp.concatenate([_fold_pair(a) for a in dvw[mine]], axis=-1)
            dat = jnp.concatenate([_fold_pair(a[:2 * L]) for a in dx[mine]], axis=-1)
            drt = jnp.concatenate([_fold_pair(a[2 * L:]) for a in dx[mine]], axis=-1)
            dbt = jnp.concatenate([_fold_pair(a[:2 * L]) for a in dyk[mine]], axis=-1)
            dkt = jnp.concatenate([_fold_pair(a[2 * L:]) for a in dyk[mine]], axis=-1)
            dg = drt * rt - dbt * bt - dkt * kt
            dg = dg + jnp.where(_iota2(dg.shape, 0) == L - 1, jnp.concatenate(dgl[mine], axis=-1), 0.0)
            dgp = dat * at
            dlw_ref[s] = _hdot_tn(tri, dg + dgp) - dgp
            dr_ref[s] = drt * eg
            daa_ref[s] = dat * egp
            dbb_ref[s] = dbt * eng
            dk2_ref[s] = dkt * eng

    blk = pl.BlockSpec((NSEQ, L, W), lambda c: (0, NC - 1 - c, 0))
    per_chunk = lambda shape: pl.BlockSpec((1, NSEQ) + shape, lambda c: (NC - 1 - c, 0, 0, 0))
    outs = pl.pallas_call(
        body, grid=(NC,), name="rwkv_bwd",
        in_specs=[blk] * 5 + [per_chunk(a.shape[2:]) for a in saved] + [blk],
        out_specs=[blk] * 6,
        out_shape=[jax.ShapeDtypeStruct((NSEQ, SEQ, W), F32)] * 6,
        scratch_shapes=[pltpu.VMEM((NSEQ,) + STATE_SHAPE, F32)],
        compiler_params=_cparams(("arbitrary",)),
    )(*[_seq_view(a) for a in (r, lw, k2, aa, bb)], *saved, _seq_view(dy))
    return [a.reshape(T, W) for a in outs]


def _post_math(y, r, k2, v, ga, o, gb, lng, lnb, rk, bd):
    mu = _headsum(y, bd) * (1.0 / HD)
    yc = y - mu
    var = _headsum(yc * yc, bd) * (1.0 / HD)
    yn = yc * lax.rsqrt(var + GN_EPS) * lng + lnb
    bonus = _headsum(r * k2 * rk, bd) * v
    return (yn + bonus) * _silu(ga), o * _silu(gb)


def even_post(y, r, k2, v, ga, o, gb, lng, lnb, rk, h, w_bf):
    tm = 512

    def body(y_ref, r_ref, k2_ref, v_ref, ga_ref, o_ref, gb_ref, lng_ref, lnb_ref, rk_ref, h_ref, w_ref,
             ho_ref, zt_ref):
        ya, yb = _post_math(y_ref[...], r_ref[...], k2_ref[...], v_ref[...], ga_ref[...], o_ref[...], gb_ref[...],
                            lng_ref[...], lnb_ref[...], rk_ref[...], _head_blockdiag())
        z = jnp.concatenate([ya.astype(BF16), yb.astype(BF16)], axis=-1)
        zt_ref[...] = z.T
        ho_ref[...] = h_ref[...] + jnp.dot(z, w_ref[...], preferred_element_type=F32)

    vec = _const_spec((1, W))
    return pl.pallas_call(
        body, grid=(T // tm,), name="even_post",
        in_specs=[_row_spec(tm, W)] * 7 + [vec] * 3 + [_row_spec(tm, D), _const_spec((D, D))],
        out_specs=[_row_spec(tm, D), _col_spec(D, tm)],
        out_shape=[jax.ShapeDtypeStruct((T, D), F32), jax.ShapeDtypeStruct((D, T), BF16)],
        compiler_params=_cparams(("parallel",), VMEM_BIG),
    )(y, r, k2, v, ga, o, gb, lng, lnb, rk, h, w_bf)


def even_post_bwd(y, r, k2, v, ga, o, gb, lng, lnb, rk, dh, zt_bf, w_bf, after=None):
    tm = 512
    extra_specs, extra = _after_operand(after)

    def body(y_ref, r_ref, k2_ref, v_ref, ga_ref, o_ref, gb_ref, lng_ref, lnb_ref, rk_ref, dh_ref, zt_ref, w_ref,
             *rest):
        dy_ref, dr_ref, dk2_ref, dv_ref, dga_ref, do_ref, dgb_ref, dlng_ref, dlnb_ref, drk_ref, dw_ref = rest[-11:]
        dzv = _out_proj_back(dh_ref, zt_ref, w_ref, dw_ref)
        bd = _head_blockdiag()
        _, vjp = jax.vjp(lambda *a: _post_math(*a, bd), y_ref[...], r_ref[...], k2_ref[...], v_ref[...], ga_ref[...],
                         o_ref[...], gb_ref[...], lng_ref[...], lnb_ref[...], rk_ref[...])
        dy, dr, dk2, dv, dga, do, dgb, dlng, dlnb, drk = vjp((dzv[:, 0:W], dzv[:, W:2 * W]))
        for ref, val in ((dy_ref, dy), (dr_ref, dr), (dk2_ref, dk2), (dv_ref, dv), (dga_ref, dga), (do_ref, do),
                         (dgb_ref, dgb)):
            ref[...] = val.astype(ref.dtype)

        @pl.when(pl.program_id(0) == 0)
        def _():
            for ref in (dlng_ref, dlnb_ref, drk_ref):
                ref[...] = jnp.zeros_like(ref)

        dlng_ref[...] += dlng
        dlnb_ref[...] += dlnb
        drk_ref[...] += drk

    vec = _const_spec((1, W))
    return pl.pallas_call(
        body, grid=(T // tm,), name="even_post_bwd",
        in_specs=[_row_spec(tm, W)] * 7 + [vec] * 3 + [_row_spec(tm, D), _col_spec(D, tm), _const_spec((D, D))]
        + extra_specs,
        out_specs=[_row_spec(tm, W)] * 7 + [vec] * 3 + [_const_spec((D, D))],
        out_shape=[jax.ShapeDtypeStruct((T, W), dt) for dt in (F32, F32, F32, F32, BF16, F32, BF16)]
        + [jax.ShapeDtypeStruct((1, W), F32)] * 3 + [jax.ShapeDtypeStruct((D, D), F32)],
        compiler_params=_cparams(("arbitrary",), VMEM_BIG),
    )(y, r, k2, v, ga, o, gb, lng, lnb, rk, dh, zt_bf, w_bf, *extra)


PADSEQ = SEQ + LEFT * L
ATT_SCALE = 1.0 / math.sqrt(HD)
ATT_Q = 4
WIN = BAND + (ATT_Q - 1) * L
ATT_STEPS = NC // ATT_Q
ATT_BIAS_SHAPE = (NPAIR, ATT_Q * 2 * L, WIN)


def _stack_chunks(a):
    return jnp.concatenate([_stack_pair(a[i * L:(i + 1) * L]) for i in range(ATT_Q)], axis=0)


def _unstack_chunks(a):
    return jnp.concatenate([_unstack_pair(a[i * 2 * L:(i + 1) * 2 * L]) for i in range(ATT_Q)], axis=0)


def window_bias(bias):
    parts = [jnp.pad(bias, ((0, 0), (0, 0), (i * L, (ATT_Q - 1 - i) * L)), constant_values=NEG) for i in range(ATT_Q)]
    return jnp.concatenate(parts, axis=1)


def _att_probs(q2, kw, bias, step):
    valid = _iota2((1, WIN), 1) >= (LEFT - step * ATT_Q) * L
    s = [jnp.where(valid, _bdot_nt(a, b) * ATT_SCALE + bias[p], NEG) for p, (a, b) in enumerate(zip(q2, kw))]
    e = [jnp.exp(a - jnp.max(a, axis=-1, keepdims=True)) for a in s]
    return [a / jnp.sum(a, axis=-1, keepdims=True) for a in e]


def attention_fwd(q, kpad, vpad, bias):
    def body(q_ref, k_ref, v_ref, b_ref, o_ref):
        step = pl.program_id(1)
        start = pl.multiple_of(step * (ATT_Q * L), L)
        kw = _pairs(k_ref[pl.ds(start, WIN), :])
        vw = _pairs(v_ref[pl.ds(start, WIN), :])
        q2 = [_stack_chunks(a) for a in _pairs(q_ref[...].astype(BF16))]
        p = _att_probs(q2, kw, b_ref[...], step)
        o_ref[...] = jnp.concatenate([_unstack_chunks(_bdot(a, b)) for a, b in zip(p, vw)], axis=-1)

    qblk = pl.BlockSpec((ATT_Q * L, W), lambda b, c: (b * ATT_STEPS + c, 0))
    kblk = pl.BlockSpec((PADSEQ, W), lambda b, c: (b, 0))
    return pl.pallas_call(
        body, grid=(NSEQ, ATT_STEPS), name="attention_fwd",
        in_specs=[qblk, kblk, kblk, _const_spec(ATT_BIAS_SHAPE)],
        out_specs=qblk, out_shape=jax.ShapeDtypeStruct((T, W), F32),
        compiler_params=_cparams(("parallel", "arbitrary")),
    )(q, kpad, vpad, bias)


def attention_bwd(q, kpad, vpad, bias, do):
    def body(q_ref, k_ref, v_ref, b_ref, do_ref, dq_ref, dko_ref, dvo_ref, db_ref, dk_ref, dv_ref):
        b = pl.program_id(0)
        c = pl.program_id(1)

        @pl.when(c == 0)
        def _():
            dk_ref[...] = jnp.zeros_like(dk_ref)
            dv_ref[...] = jnp.zeros_like(dv_ref)

        @pl.when((c == 0) & (b == 0))
        def _():
            db_ref[...] = jnp.zeros_like(db_ref)

        start = pl.multiple_of(c * (ATT_Q * L), L)
        kw = _pairs(k_ref[pl.ds(start, WIN), :])
        vw = _pairs(v_ref[pl.ds(start, WIN), :])
        q2 = [_stack_chunks(a) for a in _pairs(q_ref[...].astype(BF16))]
        do2 = [_stack_chunks(a) for a in _pairs(do_ref[...].astype(BF16))]
        p = _att_probs(q2, kw, b_ref[...], c)
        dp = [_bdot_nt(a, b) for a, b in zip(do2, vw)]
        ds = [a * (d - jnp.sum(d * a, axis=-1, keepdims=True)) for a, d in zip(p, dp)]
        dss = [(a * ATT_SCALE).astype(BF16) for a in ds]
        dq_ref[...] = jnp.concatenate([_unstack_chunks(_bdot(a, b)) for a, b in zip(dss, kw)], axis=-1).astype(BF16)
        dk_ref[pl.ds(start, WIN), :] += jnp.concatenate([_bdot_tn(a, b) for a, b in zip(dss, q2)], axis=-1)
        dv_ref[pl.ds(start, WIN), :] += jnp.concatenate([_bdot_tn(a, b) for a, b in zip(p, do2)], axis=-1)
        for i in range(NPAIR):
            db_ref[i] += ds[i]

        @pl.when(c == ATT_STEPS - 1)
        def _():
            dko_ref[...] = dk_ref[LEFT * L:, :].astype(BF16)
            dvo_ref[...] = dv_ref[LEFT * L:, :].astype(BF16)

    qblk = pl.BlockSpec((ATT_Q * L, W), lambda b, c: (b * ATT_STEPS + c, 0))
    kblk = pl.BlockSpec((PADSEQ, W), lambda b, c: (b, 0))
    sblk = pl.BlockSpec((SEQ, W), lambda b, c: (b, 0))
    bblk = _const_spec(ATT_BIAS_SHAPE)
    return pl.pallas_call(
        body, grid=(NSEQ, ATT_STEPS), name="attention_bwd",
        in_specs=[qblk, kblk, kblk, bblk, qblk],
        out_specs=[qblk, sblk, sblk, bblk],
        out_shape=[jax.ShapeDtypeStruct((T, W), BF16), jax.ShapeDtypeStruct((T, W), BF16),
                   jax.ShapeDtypeStruct((T, W), BF16), jax.ShapeDtypeStruct(ATT_BIAS_SHAPE, F32)],
        scratch_shapes=[pltpu.VMEM((PADSEQ, W), F32), pltpu.VMEM((PADSEQ, W), F32)],
        compiler_params=_cparams(("arbitrary", "arbitrary"), VMEM_BIG),
    )(q, kpad, vpad, bias, do)


NTAB = 2 * CLIP + 1
EXT = BAND + L


def _ext_onehot():
    n = _iota2((EXT, NTAB), 0)
    m = _iota2((EXT, NTAB), 1)
    return (jnp.clip(BAND - 1 - n, -CLIP, CLIP) + CLIP == m).astype(F32)


def bias_expand(table):
    def body(t_ref, o_ref):
        ext = _hdot_nt(t_ref[...], _ext_onehot())
        for i in range(L):
            s = L - 1 - i
            o_ref[:, i, :] = (pltpu.roll(ext, EXT - s, 1) if s else ext)[:, :BAND]

    return pl.pallas_call(body, name="bias_expand", out_shape=jax.ShapeDtypeStruct((NH, L, BAND), F32))(table)


def bias_grad(dbias):
    def body(d_ref, o_ref):
        acc = jnp.zeros((NH, EXT), F32)
        zpad = jnp.zeros((NH, EXT - BAND), F32)
        for i in range(L):
            s = L - 1 - i
            row = jnp.concatenate([d_ref[:, i, :], zpad], axis=-1)
            acc = acc + (pltpu.roll(row, s, 1) if s else row)
        o_ref[...] = _hdot(acc, _ext_onehot())

    return pl.pallas_call(body, name="bias_grad", out_shape=jax.ShapeDtypeStruct((NH, NTAB), F32))(dbias)


def _group_cols(g):
    return slice(g * SGC, (g + 1) * SGC)


def _sg_norm(gv, lng, lnb):
    gc = gv - jnp.mean(gv, axis=-1, keepdims=True)
    rstd = lax.rsqrt(jnp.mean(gc * gc, axis=-1, keepdims=True) + LN_EPS)
    xhat = gc * rstd
    return xhat, rstd, xhat * lng + lnb


GMLP_BWD_CHUNKS = 2


def gmlp_fwd_loss(u, v, gate, lng, lnb, wm_bf, sgb_t, h, w_bf, g_final, target):
    tm = GMLP_BWD_CHUNKS * SGC

    def body(u_ref, v_ref, gt_ref, lng_ref, lnb_ref, wm_ref, sb_ref, h_ref, w_ref, g_ref, t_ref,
             dh_ref, loss_ref, dg_ref, zt_ref):
        zs = []
        for ch in range(GMLP_BWD_CHUNKS):
            rows = slice(ch * SGC, (ch + 1) * SGC)
            _, _, vln = _sg_norm(_gelu(v_ref[rows, :]), lng_ref[...], lnb_ref[...])
            vlb = vln.astype(BF16)
            zg = []
            for g in range(NG):
                cs = _group_cols(g)
                sv = jnp.dot(wm_ref[g], vlb[:, cs], preferred_element_type=F32) + sb_ref[:, g:g + 1]
                zg.append((_gelu(u_ref[rows, cs]) * sv * _silu(gt_ref[rows, cs])).astype(BF16))
            zs.append(jnp.concatenate(zg, axis=-1))
        z = jnp.concatenate(zs, axis=0)
        zt_ref[...] = z.T
        xv = h_ref[...] + jnp.dot(z, w_ref[...], preferred_element_type=F32)
        rstd = lax.rsqrt(jnp.mean(xv * xv, axis=-1, keepdims=True) + RMS_EPS)
        xhat = xv * rstd
        err = xhat * g_ref[...] - t_ref[...]
        part = 0.5 * jnp.sum(jnp.mean(err * err, axis=-1, keepdims=True), axis=0, keepdims=True)
        dout = err * (1.0 / D)

        @pl.when(pl.program_id(0) == 0)
        def _():
            loss_ref[...] = jnp.zeros_like(loss_ref)
            dg_ref[...] = jnp.zeros_like(dg_ref)

        loss_ref[...] += jnp.broadcast_to(part, loss_ref.shape)
        dg_ref[...] += jnp.sum(dout * xhat, axis=0, keepdims=True)
        dxh = dout * g_ref[...]
        dh_ref[...] = rstd * (dxh - xhat * jnp.mean(dxh * xhat, axis=-1, keepdims=True))

    return pl.pallas_call(
        body, grid=(T // tm,), name="gmlp_fwd_loss",
        in_specs=[_row_spec(tm, D)] * 3 + [_const_spec((1, D))] * 2
        + [_const_spec((NG, SGC, SGC)), _const_spec((SGC, NG)), _row_spec(tm, D), _const_spec((D, D)),
           _const_spec((1, D)), _row_spec(tm, D)],
        out_specs=[_row_spec(tm, D), _const_spec((8, 128)), _const_spec((1, D)), _col_spec(D, tm)],
        out_shape=[jax.ShapeDtypeStruct((T, D), F32), jax.ShapeDtypeStruct((8, 128), F32),
                   jax.ShapeDtypeStruct((1, D), F32), jax.ShapeDtypeStruct((D, T), BF16)],
        compiler_params=_cparams(("arbitrary",), VMEM_BIG),
    )(u, v, gate, lng, lnb, wm_bf, sgb_t, h, w_bf, g_final, target)


def gmlp_bwd(u, v, gate, lng, lnb, wm_bf, sgb_t, dh, zt_bf, w_bf):
    def body(u_ref, v_ref, gt_ref, lng_ref, lnb_ref, wm_ref, sb_ref, dh_ref, zt_ref, w_ref,
             du_ref, dv_ref, dgt_ref, dlng_ref, dlnb_ref, dwm_ref, dsb_ref, dw_ref):
        @pl.when(pl.program_id(0) == 0)
        def _():
            for ref in (dlng_ref, dlnb_ref, dwm_ref, dsb_ref):
                ref[...] = jnp.zeros_like(ref)

        dz = _out_proj_back(dh_ref, zt_ref, w_ref, dw_ref)
        sel = (_iota2((D, NG), 0) // SGC == _iota2((D, NG), 1)).astype(F32)
        for ch in range(GMLP_BWD_CHUNKS):
            rows = slice(ch * SGC, (ch + 1) * SGC)
            gv, dgv_dv = _gelu_both(v_ref[rows, :])
            xhat, rstd, vln = _sg_norm(gv, lng_ref[...], lnb_ref[...])
            vlb = vln.astype(BF16)
            dvln = []
            dsv_all = []
            for g in range(NG):
                cs = _group_cols(g)
                uu = u_ref[rows, cs]
                gg = gt_ref[rows, cs]
                dzz = dz[rows, cs]
                sv = jnp.dot(wm_ref[g], vlb[:, cs], preferred_element_type=F32) + sb_ref[:, g:g + 1]
                gu, dgu = _gelu_both(uu)
                sg, dsg = _silu_both(gg)
                dzgu = dzz * gu
                dsv = dzgu * sg
                dgt_ref[rows, cs] = (dzgu * sv * dsg).astype(BF16)
                du_ref[rows, cs] = (dzz * sv * sg * dgu).astype(BF16)
                dsb16 = dsv.astype(BF16)
                dvln.append(lax.dot_general(wm_ref[g], dsb16, (((0,), (0,)), ((), ())), preferred_element_type=F32))
                dwm_ref[g] += lax.dot_general(dsb16, vlb[:, cs], (((1,), (1,)), ((), ())),
                                              preferred_element_type=F32)
                dsv_all.append(dsv)
            dvl = jnp.concatenate(dvln, axis=-1)
            dsb_ref[...] += _hdot(jnp.concatenate(dsv_all, axis=-1), sel)
            dlng_ref[...] += jnp.sum(dvl * xhat, axis=0, keepdims=True)
            dlnb_ref[...] += jnp.sum(dvl, axis=0, keepdims=True)
            dxh = dvl * lng_ref[...]
            dgv = rstd * (dxh - jnp.mean(dxh, axis=-1, keepdims=True)
                          - xhat * jnp.mean(dxh * xhat, axis=-1, keepdims=True))
            dv_ref[rows, :] = (dgv * dgv_dv).astype(BF16)

    tm = GMLP_BWD_CHUNKS * SGC
    return pl.pallas_call(
        body, grid=(T // tm,), name="gmlp_bwd",
        in_specs=[_row_spec(tm, D)] * 3 + [_const_spec((1, D))] * 2
        + [_const_spec((NG, SGC, SGC)), _const_spec((SGC, NG)), _row_spec(tm, D), _col_spec(D, tm),
           _const_spec((D, D))],
        out_specs=[_row_spec(tm, D)] * 3 + [_const_spec((1, D))] * 2
        + [_const_spec((NG, SGC, SGC)), _const_spec((SGC, NG)), _const_spec((D, D))],
        out_shape=[jax.ShapeDtypeStruct((T, D), BF16)] * 3 + [jax.ShapeDtypeStruct((1, D), F32)] * 2
        + [jax.ShapeDtypeStruct((NG, SGC, SGC), F32), jax.ShapeDtypeStruct((SGC, NG), F32),
           jax.ShapeDtypeStruct((D, D), F32)],
        compiler_params=_cparams(("arbitrary",), VMEM_BIG),
    )(u, v, gate, lng, lnb, wm_bf, sgb_t, dh, zt_bf, w_bf)


NCHIP = 4
NDEV = 8
ANY = pl.BlockSpec(memory_space=pl.ANY)


HBM = pl.BlockSpec(memory_space=pltpu.HBM)
SEM = pl.BlockSpec(memory_space=pltpu.SEMAPHORE)
EFFECT = pltpu.SideEffectType.DATAFLOW_SIDE_EFFECTING


def _peers(whole_mesh):
    x, y, c = lax.axis_index("x"), lax.axis_index("y"), lax.axis_index("c")
    if not whole_mesh:
        return [((px, py, c), 2 * px + py) for px, py in ((1 - x, y), (x, 1 - y), (1 - x, 1 - y))], 2 * x + y
    out = []
    for j in range(1, NDEV):
        px, py, pc = x ^ (j >> 2), y ^ ((j >> 1) & 1), c ^ (j & 1)
        out.append(((px, py, pc), 4 * px + 2 * py + pc))
    return out, 4 * x + 2 * y + c


def _send_copies(src, land, send, recv, scatter, whole_mesh, starting):
    peers, me = _peers(whole_mesh)
    copies = []
    for t in range(len(src)):
        for j, (dev, slot) in enumerate(peers):
            k = t * len(peers) + j
            copies.append(pltpu.make_async_remote_copy(
                src_ref=src[t].at[slot] if scatter else src[t], dst_ref=land[t].at[me if starting else slot],
                send_sem=send.at[k], recv_sem=recv.at[k], device_id=dev, device_id_type=MESH))
    return copies


def send_start(srcs, lands, scatter, whole_mesh, name):
    n = len(srcs)
    nsem = n * (NDEV - 1 if whole_mesh else NCHIP - 1)

    def body(*refs):
        for cp in _send_copies(refs[:n], refs[n:2 * n], refs[2 * n], refs[2 * n + 1], scatter, whole_mesh, True):
            cp.start()
        refs[-1][...] = jnp.zeros_like(refs[-1])

    arrs = list(srcs) + list(lands)
    out = pl.pallas_call(
        body, name=name,
        out_shape=(pltpu.SemaphoreType.DMA((nsem,)), pltpu.SemaphoreType.DMA((nsem,)),
                   *[pltpu.HBM(a.shape, a.dtype) for a in arrs], jax.ShapeDtypeStruct((8, 128), F32)),
        in_specs=[HBM] * (2 * n), out_specs=(SEM, SEM, *[HBM] * (2 * n), pl.BlockSpec(memory_space=pltpu.VMEM)),
        input_output_aliases={i: 2 + i for i in range(2 * n)},
        compiler_params=pltpu.CompilerParams(has_side_effects=EFFECT),
    )(*[pltpu.with_memory_space_constraint(a, pltpu.HBM) for a in arrs])
    return out[0], out[1], list(out[2:2 + n]), list(out[2 + n:2 + 2 * n]), out[-1]


def send_wait(started, after, scatter, whole_mesh, name):
    send, recv, srcs, lands, _ = started
    n = len(srcs)

    def body(*refs):
        for cp in _send_copies(refs[:n], refs[n:2 * n], refs[2 * n], refs[2 * n + 1], scatter, whole_mesh, False):
            cp.wait_send()
            cp.wait_recv()

    arrs = list(srcs) + list(lands)
    out = pl.pallas_call(
        body, name=name, out_shape=tuple(pltpu.HBM(a.shape, a.dtype) for a in arrs),
        in_specs=[HBM] * (2 * n) + [SEM, SEM, ANY], out_specs=tuple([HBM] * (2 * n)),
        input_output_aliases={i: i for i in range(2 * n)},
        compiler_params=pltpu.CompilerParams(has_side_effects=EFFECT),
    )(*arrs, send, recv, after)
    return list(out[n:])


def exchange_c(arrs, name):
    n = len(arrs)

    def body(*refs):
        ins, outs = refs[:n], refs[n:2 * n]
        send, recv = refs[2 * n:]
        sibling = (lax.axis_index("x"), lax.axis_index("y"), 1 - lax.axis_index("c"))
        copies = [pltpu.make_async_remote_copy(src_ref=ins[t], dst_ref=outs[t], send_sem=send.at[t], recv_sem=recv.at[t],
                                               device_id=sibling, device_id_type=MESH) for t in range(n)]
        for cp in copies:
            cp.start()
        for cp in copies:
            cp.wait()

    return pl.pallas_call(
        body, name=name, in_specs=[ANY] * n, out_specs=[ANY] * n,
        out_shape=[jax.ShapeDtypeStruct(a.shape, a.dtype) for a in arrs],
        scratch_shapes=[pltpu.SemaphoreType.DMA((n,)), pltpu.SemaphoreType.DMA((n,))],
    )(*arrs)


def swap_row_halves(a, name):
    n, rows, cols = a.shape
    half = rows // 2

    def body(in_ref, out_ref, send, recv):
        x, y, c = lax.axis_index("x"), lax.axis_index("y"), lax.axis_index("c")
        cp = pltpu.make_async_remote_copy(src_ref=in_ref.at[:, pl.ds((1 - c) * half, half), :], dst_ref=out_ref,
                                          send_sem=send, recv_sem=recv, device_id=(x, y, 1 - c), device_id_type=MESH)
        cp.start()
        cp.wait()

    return pl.pallas_call(
        body, name=name, in_specs=[ANY], out_specs=ANY, out_shape=jax.ShapeDtypeStruct((n, half, cols), a.dtype),
        scratch_shapes=[pltpu.SemaphoreType.DMA, pltpu.SemaphoreType.DMA],
    )(a)


def add_blocks(a, b, name):
    n, rows, cols = a.shape
    tr = _rows_tile(rows)

    def body(a_ref, b_ref, o_ref):
        o_ref[...] = (a_ref[...].astype(F32) + b_ref[...].astype(F32)).astype(BF16)

    spec = pl.BlockSpec((1, tr, cols), lambda s, i: (s, i, 0))
    return pl.pallas_call(
        body, grid=(n, rows // tr), name=name, in_specs=[spec, spec], out_specs=spec,
        out_shape=jax.ShapeDtypeStruct(a.shape, BF16), compiler_params=_cparams(("parallel", "parallel")),
    )(a, b)


def gather_weights(arrs, split):
    n = len(arrs)

    def body(*refs):
        ins, outs = refs[:n], refs[n:2 * n]
        send1, recv1, send2, recv2, loc = refs[2 * n:]
        x, y, c = lax.axis_index("x"), lax.axis_index("y"), lax.axis_index("c")
        me = 2 * x + y
        sibling = (x, y, 1 - c)
        peers = [(1 - x, y), (x, 1 - y), (1 - x, 1 - y)]

        def rows_of(t, core):
            half = arrs[t].shape[0] // 2
            return pl.ds(core * half, half)

        def part(ref, t, core):
            return ref.at[rows_of(t, core)] if split[t] else ref

        local = [pltpu.make_async_copy(ins[t], outs[t].at[me], loc.at[t]) for t in range(n)]
        for cp in local:
            cp.start()
        first = []
        for t in range(n):
            for j, (px, py) in enumerate(peers):
                first.append(pltpu.make_async_remote_copy(
                    src_ref=part(ins[t], t, c), dst_ref=part(outs[t].at[me], t, c), send_sem=send1.at[t, j],
                    recv_sem=recv1.at[t, j], device_id=(px, py, c), device_id_type=MESH))
        for cp in first:
            cp.start()
        passed = []
        for t in range(n):
            for j, (px, py) in enumerate(peers):
                landed = part(outs[t].at[2 * px + py], t, c)
                pltpu.make_async_remote_copy(
                    src_ref=landed, dst_ref=landed, send_sem=send1.at[t, j], recv_sem=recv1.at[t, j],
                    device_id=(x, y, c), device_id_type=MESH).wait_recv()
                if split[t]:
                    cp = pltpu.make_async_remote_copy(
                        src_ref=landed, dst_ref=landed, send_sem=send2.at[t, j], recv_sem=recv2.at[t, j],
                        device_id=sibling, device_id_type=MESH)
                    cp.start()
                    passed.append(cp)
        for t in range(n):
            for j, (px, py) in enumerate(peers):
                if split[t]:
                    other = part(outs[t].at[2 * px + py], t, 1 - c)
                    pltpu.make_async_remote_copy(
                        src_ref=other, dst_ref=other, send_sem=send2.at[t, j], recv_sem=recv2.at[t, j],
                        device_id=(x, y, c), device_id_type=MESH).wait_recv()
        for cp in first + passed:
            cp.wait_send()
        for cp in local:
            cp.wait()

    return pl.pallas_call(
        body, name="gather_weights", in_specs=[ANY] * n, out_specs=[ANY] * n,
        out_shape=[jax.ShapeDtypeStruct((NCHIP,) + a.shape, a.dtype) for a in arrs],
        scratch_shapes=[pltpu.SemaphoreType.DMA((n, 3))] * 4 + [pltpu.SemaphoreType.DMA((n,))],
    )(*arrs)


def _adam_math(g, w, m, v):
    m = ADAM_B1 * m + (1.0 - ADAM_B1) * g
    v = ADAM_B2 * v + (1.0 - ADAM_B2) * (g * g)
    m_hat = m / (1.0 - ADAM_B1 ** ADAM_STEP)
    v_hat = v / (1.0 - ADAM_B2 ** ADAM_STEP)
    delta = -ADAM_LR * (m_hat / (jnp.sqrt(v_hat) + ADAM_EPS) + ADAM_WD * w)
    return delta, m, v


def _rows_tile(rows):
    return rows if rows <= 256 else 256


def sum_chips(own, parts, name):
    _, rows, cols = parts.shape
    tr = _rows_tile(rows)

    def body(own_ref, p_ref, o_ref):
        acc = own_ref[...].astype(F32)
        for s in range(NCHIP):
            acc = acc + p_ref[s].astype(F32)
        o_ref[...] = acc

    return pl.pallas_call(
        body, grid=(rows // tr,), name=name,
        in_specs=[pl.BlockSpec((tr, cols), lambda i: (i, 0)), pl.BlockSpec((NCHIP, tr, cols), lambda i: (0, i, 0))],
        out_specs=pl.BlockSpec((tr, cols), lambda i: (i, 0)),
        out_shape=jax.ShapeDtypeStruct((rows, cols), F32),
        compiler_params=_cparams(("parallel",)),
    )(own, parts)


def adam_shard(p_mine, p_sib, w, m, v, name):
    rows, cols = p_mine.shape
    tr = _rows_tile(rows)
    lead = w.ndim == 3

    def body(a_ref, b_ref, w_ref, m_ref, v_ref, g_ref, d_ref, mo_ref, vo_ref):
        g = a_ref[...] + b_ref[...]
        g = g[None] if lead else g
        g_ref[...] = g
        d_ref[...], mo_ref[...], vo_ref[...] = _adam_math(g, w_ref[...], m_ref[...], v_ref[...])

    flat = pl.BlockSpec((tr, cols), lambda i: (i, 0))
    spec = pl.BlockSpec((1, tr, cols), lambda i: (0, i, 0)) if lead else flat
    return pl.pallas_call(
        body, grid=(rows // tr,), name=name, in_specs=[flat] * 2 + [spec] * 3, out_specs=[spec] * 4,
        out_shape=[jax.ShapeDtypeStruct(w.shape, F32)] * 4,
        compiler_params=_cparams(("parallel",)),
    )(p_mine, p_sib, w, m, v)


def adam_shard_halves_t(r_mine, r_sib, wt, mt, vt, name):
    hrows, cols = r_mine.shape
    tr = _rows_tile(hrows)
    per_half = hrows // tr

    def body(a_ref, b_ref, w_ref, m_ref, v_ref, g_ref, d_ref, mo_ref, vo_ref):
        mine = pl.program_id(0) == lax.axis_index("c")
        g = jnp.where(mine, a_ref[...], b_ref[...]).T[None]
        g_ref[...] = g
        d_ref[...], mo_ref[...], vo_ref[...] = _adam_math(g, w_ref[...], m_ref[...], v_ref[...])

    flat = pl.BlockSpec((tr, cols), lambda h, i: (i, 0))
    spec = pl.BlockSpec((1, cols, tr), lambda h, i: (0, 0, h * per_half + i))
    return pl.pallas_call(
        body, grid=(2, per_half), name=name, in_specs=[flat] * 2 + [spec] * 3, out_specs=[spec] * 4,
        out_shape=[jax.ShapeDtypeStruct(wt.shape, F32)] * 4,
        compiler_params=_cparams(("parallel", "parallel")),
    )(r_mine, r_sib, wt, mt, vt)


def adam_replicated(parts, w, m, v, name):
    rows = w.shape[0]

    def body(p_ref, w_ref, m_ref, v_ref, g_ref, d_ref, mo_ref, vo_ref):
        g = p_ref[0]
        for d in range(1, NDEV):
            g = g + p_ref[d]
        g_ref[...] = g
        d_ref[...], mo_ref[...], vo_ref[...] = _adam_math(g, w_ref[...], m_ref[...], v_ref[...])

    return pl.pallas_call(
        body, name=name, out_shape=[jax.ShapeDtypeStruct((rows, 128), F32)] * 4,
    )(parts, w, m, v)


def _pack(arrs):
    pieces = []
    for a in arrs:
        flat = a.reshape(-1)
        pad = (-flat.shape[0]) % 128
        pieces.append(jnp.pad(flat, (0, pad)) if pad else flat)
    flat = jnp.concatenate(pieces)
    pad = (-flat.shape[0]) % 1024
    return jnp.pad(flat, (0, pad)).reshape(-1, 128)


def _unpack(buf, shapes):
    flat = buf.reshape(-1)
    out = []
    o = 0
    for s in shapes:
        n = int(np.prod(s))
        out.append(flat[o:o + n].reshape(s))
        o += n + (-n) % 128
    return out


EVEN_SPLITS = (SHIFT, W, W, W, W, W)
ODD_SPLITS = (D, D, D)


def _cols_to_chips(a):
    rows, cols = a.shape
    return a.reshape(rows, NCHIP, cols // NCHIP).transpose(1, 0, 2)


def _chips_to_cols(a):
    _, rows, n = a.shape
    return a.transpose(1, 0, 2).reshape(rows, NCHIP * n)


def kernel(x, norm_g, w_in_e, shift_mu, rw_w0, rw_w2, rw_a0, rw_a2, rw_kk, rw_ka, rw_rk, rw_lnx_g, rw_lnx_b, att_bias, w_out_e, w_in_o, sg_ln_g, sg_ln_b, sg_w, sg_b, w_out_o, final_g, loss_target, m_norm_g, m_w_in_e, m_shift_mu, m_rw_w0, m_rw_w2, m_rw_a0, m_rw_a2, m_rw_kk, m_rw_ka, m_rw_rk, m_rw_lnx_g, m_rw_lnx_b, m_att_bias, m_w_out_e, m_w_in_o, m_sg_ln_g, m_sg_ln_b, m_sg_w, m_sg_b, m_w_out_o, m_final_g, v_norm_g, v_w_in_e, v_shift_mu, v_rw_w0, v_rw_w2, v_rw_a0, v_rw_a2, v_rw_kk, v_rw_ka, v_rw_rk, v_rw_lnx_g, v_rw_lnx_b, v_att_bias, v_w_out_e, v_w_in_o, v_sg_ln_g, v_sg_ln_b, v_sg_w, v_sg_b, v_w_out_o, v_final_g):
    x2 = x.reshape(T, D)
    tgt = loss_target.reshape(T, D)

    my_chip = 2 * lax.axis_index("x") + lax.axis_index("y")
    gathered = gather_weights(
        [jnp.swapaxes(w_in_e[0], 0, 1).astype(BF16), jnp.concatenate([rw_w2[0], rw_a2[0]], axis=0),
         jnp.concatenate([sg_ln_g, sg_ln_b], axis=0)], [True, True, False])
    wie = gathered[0].reshape(EVEN_IN, D)
    w2 = _chips_to_cols(gathered[1][:, :LORA])
    a2 = _chips_to_cols(gathered[1][:, LORA:])
    sglg = _chips_to_cols(gathered[2][:, 0:1])
    sglb = _chips_to_cols(gathered[2][:, 1:2])

    late = [w_out_e[0].astype(BF16), w_in_o[0].astype(BF16), w_out_o[0].astype(BF16)]
    late_started = send_start(late, [jnp.broadcast_to(a[None], (NCHIP,) + a.shape) for a in late], False, False,
                              "late_weights_start")

    def late_weights(after):
        woe, wio, woo = send_wait(late_started, after, False, False, "late_weights_wait")
        return woe.reshape(D, D), _chips_to_cols(wio), woo.reshape(D, D)

    def scatter_start(grads, name):
        srcs = [g_.astype(BF16) if g_.shape[-1] >= W else g_ for g_ in grads]
        return send_start(srcs, [jnp.zeros_like(s) for s in srcs], True, False, name)

    def own_block(g_):
        return lax.dynamic_index_in_dim(g_, my_chip, axis=0, keepdims=False)

    started = {}

    def on_odd_grads(d_woo, d_wio):
        blocks = [d_woo.reshape(NCHIP, D // NCHIP, D), d_wio]
        started["odd"] = (scatter_start(blocks, "odd_grads_start"), [own_block(b) for b in blocks])
        return started["odd"][0][-1]

    def on_even_grads(big_g):
        d_wie, d_woe, _, _, d_w2, d_a2, d_sglg, d_sglb = big_g
        my_half = lax.dynamic_slice_in_dim(d_wie, lax.axis_index("c") * (D // 2), D // 2, axis=1)
        d_wie_half = add_blocks(my_half, swap_row_halves(d_wie, "swap_w_in_e_halves"), "add_w_in_e_halves")
        blocks = [d_wie_half, d_woe.reshape(NCHIP, D // NCHIP, D), _cols_to_chips(d_w2), _cols_to_chips(d_a2),
                  _cols_to_chips(d_sglg), _cols_to_chips(d_sglb)]
        started["even"] = (scatter_start(blocks, "even_grads_start"), [own_block(b) for b in blocks])
        return started["even"][0][-1]

    def on_small_grads(layer, grads):
        mine = _pack(grads)
        started[layer + "_small"] = send_start([mine], [jnp.broadcast_to(mine[None], (NDEV,) + mine.shape)], False,
                                               True, layer + "_small_grads_start")
        return started[layer + "_small"][-1]

    loss_part, dx, _, _ = _local_step(
        x2, tgt, wie, late_weights, w2, a2, sglg, sglb, norm_g, shift_mu, rw_w0, rw_a0, rw_kk, rw_ka, rw_rk,
        rw_lnx_g, rw_lnx_b, att_bias, sg_w, sg_b, final_g, first_after=late_started[-1], on_odd_grads=on_odd_grads,
        on_even_grads=on_even_grads, on_small_grads=on_small_grads)
    even_started, even_own = started["even"]

    wmv = {"w_in_e": tuple(jnp.swapaxes(a, 1, 2) for a in (w_in_e, m_w_in_e, v_w_in_e)),
           "w_out_e": (w_out_e, m_w_out_e, v_w_out_e),
           "w_in_o": (w_in_o, m_w_in_o, v_w_in_o), "w_out_o": (w_out_o, m_w_out_o, v_w_out_o),
           "rw_w2": (rw_w2, m_rw_w2, v_rw_w2), "rw_a2": (rw_a2, m_rw_a2, v_rw_a2),
           "sg_ln_g": (sg_ln_g, m_sg_ln_g, v_sg_ln_g), "sg_ln_b": (sg_ln_b, m_sg_ln_b, v_sg_ln_b)}
    sharded = {}

    def finish(names, own, landed, tag):
        partial = [sum_chips(o_, p_, "sum_" + nm) for o_, p_, nm in zip(own, landed, names)]
        from_sibling = exchange_c(partial, "swap_partials_" + tag)
        for nm, mine, sib in zip(names, partial, from_sibling):
            if nm == "w_in_e":
                res = adam_shard_halves_t(mine, sib, *wmv[nm], "adam_" + nm)
                sharded[nm] = [jnp.swapaxes(a, 1, 2) for a in res]
            else:
                sharded[nm] = adam_shard(mine, sib, *wmv[nm], "adam_" + nm)
        return partial[0]

    odd_started, odd_own = started["odd"]
    odd_landed = send_wait(odd_started, started["even_small"][-1], True, False, "odd_grads_wait")
    done = finish(["w_out_o", "w_in_o"], odd_own, odd_landed, "odd")

    no_w = jnp.zeros((1, 1), F32)
    groups = {
        "odd": (["sg_w", "sg_b", "final_g", "norm_g1"], [sg_w, sg_b, final_g, norm_g[1:2]],
                [m_sg_w, m_sg_b, m_final_g, m_norm_g[1:2]], [v_sg_w, v_sg_b, v_final_g, v_norm_g[1:2]]),
        "even": (["norm_g0", "shift_mu", "rw_w0", "rw_a0", "rw_kk", "rw_ka", "rw_rk", "rw_lnx_g", "rw_lnx_b",
                  "att_bias", "loss"],
                 [norm_g[0:1], shift_mu, rw_w0, rw_a0, rw_kk, rw_ka, rw_rk, rw_lnx_g, rw_lnx_b, att_bias, no_w],
                 [m_norm_g[0:1], m_shift_mu, m_rw_w0, m_rw_a0, m_rw_kk, m_rw_ka, m_rw_rk, m_rw_lnx_g, m_rw_lnx_b,
                  m_att_bias, no_w],
                 [v_norm_g[0:1], v_shift_mu, v_rw_w0, v_rw_a0, v_rw_kk, v_rw_ka, v_rw_rk, v_rw_lnx_g, v_rw_lnx_b,
                  v_att_bias, no_w]),
    }
    rep = {}
    for layer in ("odd", "even"):
        nms, ws, ms_, vs_ = groups[layer]
        (gathered_g,) = send_wait(started[layer + "_small"], done, False, True, layer + "_small_grads_wait")
        rep_out = adam_replicated(gathered_g, _pack(ws), _pack(ms_), _pack(vs_), "adam_" + layer + "_small")
        done = rep_out[0]
        for nm in nms:
            rep[nm] = []
        for buf in rep_out:
            for nm, a in zip(nms, _unpack(buf, [w_.shape for w_ in ws])):
                rep[nm].append(a)
    rep["norm_g"] = [jnp.concatenate([a, b], axis=0) for a, b in zip(rep["norm_g0"], rep["norm_g1"])]
    even_landed = send_wait(even_started, done, True, False, "even_grads_wait")
    finish(["w_in_e", "w_out_e", "rw_w2", "rw_a2", "sg_ln_g", "sg_ln_b"], even_own, even_landed, "even")

    order = ["norm_g", "w_in_e", "shift_mu", "rw_w0", "rw_w2", "rw_a0", "rw_a2", "rw_kk", "rw_ka", "rw_rk",
             "rw_lnx_g", "rw_lnx_b", "att_bias", "w_out_e", "w_in_o", "sg_ln_g", "sg_ln_b", "sg_w", "sg_b",
             "w_out_o", "final_g"]
    results = {**sharded, **rep}
    outs = [rep["loss"][0].reshape(()), dx.reshape(NSEQ, SEQ, D)]
    for kind in range(4):
        outs += [results[nm][kind] for nm in order]
    return tuple(outs)


def _local_step(x2, tgt, wie_t, late_weights, w2, a2, sglg, sglb, norm_g, shift_mu, rw_w0, rw_a0, rw_kk, rw_ka, rw_rk,
                rw_lnx_g, rw_lnx_b, att_bias, sg_w, sg_b, final_g, first_after=None, on_odd_grads=None,
                on_even_grads=None, on_small_grads=None):
    zl = jnp.zeros((LORA, W), F32)
    w2x = jnp.concatenate([w2, zl], axis=0)
    a2x = jnp.concatenate([zl, a2], axis=0)
    rk = rw_rk.reshape(1, W)
    pos = np.arange(SGC)
    sg_mask = jnp.asarray(((pos[None, :] // L) <= (pos[:, None] // L)).astype(np.float32))
    wm = (sg_w[0] * sg_mask[None]).astype(BF16)
    sgb_t = sg_b[0].T

    xn0, ps, ga, q, kb, vb, gb = ln_in_proj(x2, norm_g[0:1], wie_t, EVEN_SPLITS, "in_proj_even", after=first_after,
                                            w_t=True)
    r, lw, k2, v, aa, bb = even_prep(ps, shift_mu, rw_w0, w2x, rw_a0, a2x, rw_kk, rw_ka)
    y, rw_saved = rwkv_fwd(r, lw, k2, v, aa, bb)
    bias = window_bias(bias_expand(att_bias[0]).reshape(NPAIR, 2 * L, BAND))

    def padded(a):
        return jnp.pad(a.astype(BF16).reshape(NSEQ, SEQ, W), ((0, 0), (LEFT * L, 0), (0, 0))).reshape(NSEQ * PADSEQ, W)

    kpad, vpad = padded(kb), padded(vb)
    o = attention_fwd(q, kpad, vpad, bias)
    woe, wio, woo = late_weights(o)
    h1, zt = even_post(y, r, k2, v, ga, o, gb, rw_lnx_g, rw_lnx_b, rk, x2, woe)
    xn1, u, vv, gt = ln_in_proj(h1, norm_g[1:2], wio, ODD_SPLITS, "in_proj_odd")
    dh2, loss_part, d_final_g, z2t = gmlp_fwd_loss(u, vv, gt, sglg, sglb, wm, sgb_t, h1, woo, final_g[None], tgt)

    du, dvv, dgt, d_sglg, d_sglb, d_wm, d_sgb_t, d_woo = gmlp_bwd(u, vv, gt, sglg, sglb, wm, sgb_t, dh2, z2t, woo)
    dp_odd = [du, dvv, dgt]
    d_wio = matmul_acc_chips(xn1, dp_odd, "in_proj_odd_dw")
    token = on_odd_grads(d_woo, d_wio) if on_odd_grads else None
    dh1, d_g1 = in_proj_bwd_x(h1, norm_g[1:2], wio, dp_odd, dh2, "in_proj_odd_bwd", after=token)
    odd_small = [d_wm * sg_mask[None], d_sgb_t.T, d_final_g, d_g1]
    token = on_small_grads("odd", odd_small) if on_small_grads else None
    dy, dr2, dk22, dv2, dga, do, dgb, d_lng, d_lnb, d_rk, d_woe = even_post_bwd(
        y, r, k2, v, ga, o, gb, rw_lnx_g, rw_lnx_b, rk, dh1, zt, woe, after=token)
    dq, dkb, dvb, dbias = attention_bwd(q, kpad, vpad, bias, do)
    dbias = sum(dbias[:, i * 2 * L:(i + 1) * 2 * L, i * L:i * L + BAND] for i in range(ATT_Q))
    d_att_bias = bias_grad(dbias.reshape(NH, L, BAND))
    dr, dlw, dk2, dv, daa, dbb = rwkv_bwd(r, lw, k2, aa, bb, rw_saved, dy)
    dps, d_mu, d_w0, d_w2x, d_a0, d_a2x, d_kk, d_ka = even_prep_bwd(
        ps, shift_mu, rw_w0, w2x, rw_a0, a2x, rw_kk, rw_ka, dr, dlw, dk2, dv, daa, dbb, dr2, dk22, dv2)
    dp_even = [dps, dga, dq, dkb, dvb, dgb]
    d_wie = matmul_acc_chips(xn0, dp_even, "in_proj_even_dw")
    big_g = (d_wie, d_woe, d_wio, d_woo, d_w2x[:LORA], d_a2x[LORA:], d_sglg, d_sglb)
    token = on_even_grads(big_g) if on_even_grads else None
    dx, d_g0 = in_proj_bwd_x(x2, norm_g[0:1], wie_t, dp_even, dh1, "in_proj_even_bwd", after=token, w_t=True)
    even_small = [d_g0, d_mu, d_w0, d_a0, d_kk, d_ka, d_rk, d_lng, d_lnb, d_att_bias]
    if on_small_grads:
        on_small_grads("even", even_small + [loss_part[0:1, 0:1]])
    rep_g = [jnp.concatenate([d_g0, d_g1], axis=0)] + even_small[1:] + odd_small[:3]
    return loss_part[0, 0], dx, big_g, rep_g
```

```python
import functools
import math

import jax
import jax.numpy as jnp
import numpy as np
from jax import lax
from jax.experimental import pallas as pl
from jax.experimental.pallas import tpu as pltpu

F32 = jnp.float32
BF16 = jnp.bfloat16
HI = lax.Precision.HIGHEST

D = 1024
SEQ = 2048
NSEQ = 2
T = NSEQ * SEQ
HD = 64
NH = 8
W = 512
SHIFT = 1664
LORA = 64
EVEN_IN = 4224
ODD_IN = 3072
L = 64
NC = SEQ // L
LEFT = 8
BAND = (LEFT + 1) * L
CLIP = 128
SGC = 128
NG = 8
RMS_EPS = 1e-6
LN_EPS = 1e-5
GN_EPS = 64e-5
NEG = -1e30
VMEM_BIG = 56 * 1024 * 1024

ADAM_LR = 0.001
ADAM_B1 = 0.9
ADAM_B2 = 0.999
ADAM_EPS = 1e-08
ADAM_WD = 0.01
ADAM_STEP = 10

MESH = pl.DeviceIdType.MESH


def _bdot(a, b):
    return jnp.dot(a.astype(BF16), b.astype(BF16), preferred_element_type=F32)


def _bdot_nt(a, b):
    return lax.dot_general(a.astype(BF16), b.astype(BF16), (((1,), (1,)), ((), ())), preferred_element_type=F32)


def _bdot_tn(a, b):
    return lax.dot_general(a.astype(BF16), b.astype(BF16), (((0,), (0,)), ((), ())), preferred_element_type=F32)


def _hdot(a, b):
    return jnp.dot(a, b, precision=HI, preferred_element_type=F32)


def _hdot_nt(a, b):
    return lax.dot_general(a, b, (((1,), (1,)), ((), ())), precision=HI, preferred_element_type=F32)


def _hdot_tn(a, b):
    return lax.dot_general(a, b, (((0,), (0,)), ((), ())), precision=HI, preferred_element_type=F32)


def _iota2(shape, dim):
    return lax.broadcasted_iota(jnp.int32, shape, dim)


def _head_blockdiag():
    r = _iota2((W, W), 0) // HD
    c = _iota2((W, W), 1) // HD
    return (r == c).astype(BF16)


def _headsum_impl(x, bd):
    hi = x.astype(BF16)
    mid = (x - hi.astype(F32)).astype(BF16)
    return jnp.dot(hi, bd, preferred_element_type=F32) + jnp.dot(mid, bd, preferred_element_type=F32)


@jax.custom_vjp
def _headsum(x, bd):
    return _headsum_impl(x, bd)


def _headsum_fwd(x, bd):
    return _headsum_impl(x, bd), bd


def _headsum_bwd(bd, ct):
    return _headsum_impl(ct, bd), None


_headsum.defvjp(_headsum_fwd, _headsum_bwd)


def _silu(x):
    return x * jax.nn.sigmoid(x)


def _dsilu(x):
    s = jax.nn.sigmoid(x)
    return s * (1.0 + x * (1.0 - s))


_GELU_C = math.sqrt(2.0 / math.pi)


def _gelu(x):
    return 0.5 * x * (1.0 + jnp.tanh(_GELU_C * (x + 0.044715 * (x * x * x))))


def _dgelu(x):
    t = jnp.tanh(_GELU_C * (x + 0.044715 * (x * x * x)))
    return 0.5 * (1.0 + t) + 0.5 * x * (1.0 - t * t) * _GELU_C * (1.0 + 3.0 * 0.044715 * x * x)


def _silu_both(x):
    s = jax.nn.sigmoid(x)
    xs = x * s
    return xs, s + xs * (1.0 - s)


def _gelu_both(x):
    x2 = x * x
    t = jnp.tanh(_GELU_C * (x + 0.044715 * (x2 * x)))
    half = 0.5 * (1.0 + t)
    return x * half, half + 0.5 * x * (1.0 - t * t) * _GELU_C * (1.0 + 3.0 * 0.044715 * x2)


def _softplus(x):
    return jnp.maximum(x, 0.0) + jnp.log(1.0 + jnp.exp(-jnp.abs(x)))


def _cparams(sem, vmem=None):
    return pltpu.CompilerParams(dimension_semantics=sem, vmem_limit_bytes=vmem)


def _row_spec(tm, width):
    return pl.BlockSpec((tm, width), lambda i: (i, 0))


def _col_spec(height, tm):
    return pl.BlockSpec((height, tm), lambda i: (0, i))


def _const_spec(shape):
    nd = len(shape)
    return pl.BlockSpec(shape, lambda *_: (0,) * nd)


def _weight_dims(w_bf, w_t):
    return (((1,), (1,)), ((), ())) if w_t else (((1,), (0,)), ((), ())), w_bf.shape[0 if w_t else 1]


def ln_in_proj(x, g, w_bf, splits, name, after=None, w_t=False):
    dims, n = _weight_dims(w_bf, w_t)
    tm = 256
    spans = []
    o = 0
    for s in splits:
        spans.append((o, o + s))
        o += s
    assert o == n
    extra_specs, extra = _after_operand(after)

    def body(x_ref, g_ref, w_ref, *rest):
        xn_ref, outs = rest[len(extra)], rest[len(extra) + 1:]
        xv = x_ref[...]
        rstd = lax.rsqrt(jnp.mean(xv * xv, axis=-1, keepdims=True) + RMS_EPS)
        xn = (xv * rstd * g_ref[...]).astype(BF16)
        xn_ref[...] = xn.T
        p = lax.dot_general(xn, w_ref[...], dims, preferred_element_type=F32)
        for o_ref, (a, b) in zip(outs, spans):
            o_ref[...] = p[:, a:b]

    return pl.pallas_call(
        body, grid=(T // tm,), name=name,
        in_specs=[_row_spec(tm, D), _const_spec((1, D)), _const_spec(w_bf.shape)] + extra_specs,
        out_specs=[_col_spec(D, tm)] + [_row_spec(tm, s) for s in splits],
        out_shape=[jax.ShapeDtypeStruct((D, T), BF16)] + [jax.ShapeDtypeStruct((T, s), F32) for s in splits],
        compiler_params=_cparams(("parallel",), VMEM_BIG),
    )(x, g, w_bf, *extra)


def in_proj_bwd_x(x, g, w_bf, dps, dres, name, after=None, w_t=False):
    tm = 512
    back = (((1,), (0,)), ((), ())) if w_t else (((1,), (1,)), ((), ()))
    widths = [d.shape[1] for d in dps]
    extra_specs, extra = _after_operand(after)

    def body(x_ref, g_ref, w_ref, dres_ref, *rest):
        dp_refs = rest[:len(widths)]
        dx_ref, dg_ref = rest[-2:]
        dp = jnp.concatenate([r[...] for r in dp_refs], axis=-1)
        dxn = lax.dot_general(dp, w_ref[...], back, preferred_element_type=F32)
        xv = x_ref[...]
        rstd = lax.rsqrt(jnp.mean(xv * xv, axis=-1, keepdims=True) + RMS_EPS)
        xhat = xv * rstd
        dgp = jnp.sum(dxn * xhat, axis=0, keepdims=True)

        @pl.when(pl.program_id(0) == 0)
        def _():
            dg_ref[...] = jnp.zeros_like(dg_ref)

        dg_ref[...] += dgp
        dxh = dxn * g_ref[...]
        dx_ref[...] = dres_ref[...] + rstd * (dxh - xhat * jnp.mean(dxh * xhat, axis=-1, keepdims=True))

    return pl.pallas_call(
        body, grid=(T // tm,), name=name,
        in_specs=[_row_spec(tm, D), _const_spec((1, D)), _const_spec(w_bf.shape), _row_spec(tm, D)]
        + [_row_spec(tm, s) for s in widths] + extra_specs,
        out_specs=[_row_spec(tm, D), _const_spec((1, D))],
        out_shape=[jax.ShapeDtypeStruct((T, D), F32), jax.ShapeDtypeStruct((1, D), F32)],
        compiler_params=_cparams(("arbitrary",), VMEM_BIG),
    )(x, g, w_bf, dres, *dps, *extra)


def _after_operand(after):
    return ([ANY], [after]) if after is not None else ([], [])


def matmul_acc_chips(at_bf, pieces, name, after=None):
    k = at_bf.shape[0]
    widths = [p.shape[1] for p in pieces]
    nb = sum(widths) // NCHIP
    tm = 512
    steps = T // tm
    extra_specs, extra = _after_operand(after)

    def body(a_ref, *rest):
        o_ref, acc = rest[-2:]

        @pl.when(pl.program_id(0) == 0)
        def _():
            acc[...] = jnp.zeros_like(acc)

        a = a_ref[...]
        b = jnp.concatenate([r[...] for r in rest[:len(widths)]], axis=-1)
        for s in range(NCHIP):
            acc[s] += jnp.dot(a, b[:, s * nb:(s + 1) * nb], preferred_element_type=F32)

        @pl.when(pl.program_id(0) == steps - 1)
        def _():
            o_ref[...] = acc[...].astype(BF16)

    return pl.pallas_call(
        body, grid=(steps,), name=name,
        in_specs=[_col_spec(k, tm)] + [_row_spec(tm, w_) for w_ in widths] + extra_specs,
        out_specs=_const_spec((NCHIP, k, nb)),
        out_shape=jax.ShapeDtypeStruct((NCHIP, k, nb), BF16),
        scratch_shapes=[pltpu.VMEM((NCHIP, k, nb), F32)],
        compiler_params=_cparams(("arbitrary",), VMEM_BIG),
    )(at_bf, *pieces, *extra)


def _out_proj_back(dh_ref, zt_ref, w_ref, dw_ref):
    dhb = dh_ref[...].astype(BF16)

    @pl.when(pl.program_id(0) == 0)
    def _():
        dw_ref[...] = jnp.zeros_like(dw_ref)

    dw_ref[...] += jnp.dot(zt_ref[...], dhb, preferred_element_type=F32)
    return lax.dot_general(dhb, w_ref[...], (((1,), (1,)), ((), ())), preferred_element_type=F32)


PREP_TM = 512
PREP_NB = SEQ // PREP_TM


def _prep_elem(k, wl, apre, kkw, kaw, bd):
    wraw = -_softplus(-wl) - 0.5
    lw = -jnp.exp(wraw)
    asig = jax.nn.sigmoid(apre)
    kkr = k * kkw
    nrm = jnp.maximum(jnp.sqrt(_headsum(kkr * kkr, bd)), 1e-12)
    kk = kkr / nrm
    k2 = k * (1.0 + (asig - 1.0) * kaw)
    return lw, k2, -kk, kk * asig


def _shifted(ps_ref, prev_ref, mu, blk):
    p = ps_ref[...]
    first = (blk % PREP_NB) == 0
    prev_row = jnp.where(first, 0.0, prev_ref[7:8, :])
    rolled = pltpu.roll(p, 1, 0)
    p_prev = jnp.where(_iota2(p.shape, 0) == 0, prev_row, rolled)
    return p, p_prev, p + (p_prev - p) * mu


def _prev_spec(width, blk_of):
    return pl.BlockSpec((8, width), lambda i: (jnp.maximum(blk_of(i) * (PREP_TM // 8) - 1, 0), 0))


def even_prep(ps, mu, w0, w2x, a0, a2x, kkw, kaw):
    tm = PREP_TM

    def body(ps_ref, prev_ref, mu_ref, w0_ref, w2_ref, a0_ref, a2_ref, kk_ref, ka_ref,
             r_ref, lw_ref, k2_ref, v_ref, aa_ref, bb_ref):
        _, _, s = _shifted(ps_ref, prev_ref, mu_ref[...], pl.program_id(0))
        wa = s[:, 3 * W:]
        wl = w0_ref[...] + _bdot(jnp.tanh(wa), w2_ref[...])
        apre = a0_ref[...] + _bdot(wa, a2_ref[...])
        lw, k2, aa, bb = _prep_elem(s[:, W:2 * W], wl, apre, kk_ref[...], ka_ref[...], _head_blockdiag())
        r_ref[...] = s[:, 0:W]
        v_ref[...] = s[:, 2 * W:3 * W]
        lw_ref[...] = lw
        k2_ref[...] = k2
        aa_ref[...] = aa
        bb_ref[...] = bb

    vec = _const_spec((1, W))
    return pl.pallas_call(
        body, grid=(T // tm,), name="even_prep",
        in_specs=[_row_spec(tm, SHIFT), _prev_spec(SHIFT, lambda i: i), _const_spec((1, SHIFT)), vec,
                  _const_spec((2 * LORA, W)), vec, _const_spec((2 * LORA, W)), vec, vec],
        out_specs=[_row_spec(tm, W)] * 6,
        out_shape=[jax.ShapeDtypeStruct((T, W), F32)] * 6,
        compiler_params=_cparams(("parallel",), VMEM_BIG),
    )(ps, ps, mu, w0, w2x, a0, a2x, kkw, kaw)


def even_prep_bwd(ps, mu, w0, w2x, a0, a2x, kkw, kaw, dr, dlw, dk2, dv, daa, dbb, dr2, dk22, dv2):
    tm = PREP_TM
    nb = T // tm
    rev = lambda i: nb - 1 - i

    def body(ps_ref, prev_ref, mu_ref, w0_ref, w2_ref, a0_ref, a2_ref, kk_ref, ka_ref,
             dr_ref, dlw_ref, dk2_ref, dv_ref, daa_ref, dbb_ref, dr2_ref, dk22_ref, dv2_ref,
             dps_ref, dmu_ref, dw0_ref, dw2_ref, da0_ref, da2_ref, dkk_ref, dka_ref, carry):
        i = pl.program_id(0)
        blk = rev(i)
        mu_v = mu_ref[...]
        p, p_prev, s = _shifted(ps_ref, prev_ref, mu_v, blk)
        wa = s[:, 3 * W:]
        th = jnp.tanh(wa)
        wl = w0_ref[...] + _bdot(th, w2_ref[...])
        apre = a0_ref[...] + _bdot(wa, a2_ref[...])
        bd = _head_blockdiag()
        k = s[:, W:2 * W]
        _, vjp = jax.vjp(lambda k_, wl_, ap_, kkw_, kaw_: _prep_elem(k_, wl_, ap_, kkw_, kaw_, bd),
                         k, wl, apre, kk_ref[...], ka_ref[...])
        dk, dwl, dap, dkkw, dkaw = vjp((dlw_ref[...], dk2_ref[...] + dk22_ref[...], daa_ref[...], dbb_ref[...]))
        dwa = _bdot_nt(dwl, w2_ref[...]) * (1.0 - th * th) + _bdot_nt(dap, a2_ref[...])
        ds = jnp.concatenate([dr_ref[...] + dr2_ref[...], dk, dv_ref[...] + dv2_ref[...], dwa], axis=-1)

        @pl.when(i == 0)
        def _():
            for ref in (dmu_ref, dw0_ref, dw2_ref, da0_ref, da2_ref, dkk_ref, dka_ref, carry):
                ref[...] = jnp.zeros_like(ref)

        dmu_ref[...] += jnp.sum(ds * (p_prev - p), axis=0, keepdims=True)
        dw0_ref[...] += jnp.sum(dwl, axis=0, keepdims=True)
        da0_ref[...] += jnp.sum(dap, axis=0, keepdims=True)
        dw2_ref[...] += _bdot_tn(th, dwl)
        da2_ref[...] += _bdot_tn(wa, dap)
        dkk_ref[...] += dkkw
        dka_ref[...] += dkaw
        dsm = ds * mu_v
        last = (blk % PREP_NB) == PREP_NB - 1
        nxt = jnp.where(last, 0.0, carry[0:1, :])
        up = pltpu.roll(dsm, tm - 1, 0)
        up = jnp.where(_iota2(up.shape, 0) == tm - 1, nxt, up)
        dps_ref[...] = (ds - dsm + up).astype(BF16)
        carry[0:1, :] = dsm[0:1, :]

    vec = _const_spec((1, W))
    rrow = lambda width: pl.BlockSpec((tm, width), lambda i: (rev(i), 0))
    return pl.pallas_call(
        body, grid=(nb,), name="even_prep_bwd",
        in_specs=[rrow(SHIFT), _prev_spec(SHIFT, rev), _const_spec((1, SHIFT)), vec,
                  _const_spec((2 * LORA, W)), vec, _const_spec((2 * LORA, W)), vec, vec] + [rrow(W)] * 9,
        out_specs=[rrow(SHIFT), _const_spec((1, SHIFT)), vec, _const_spec((2 * LORA, W)), vec,
                   _const_spec((2 * LORA, W)), vec, vec],
        out_shape=[jax.ShapeDtypeStruct((T, SHIFT), BF16), jax.ShapeDtypeStruct((1, SHIFT), F32),
                   jax.ShapeDtypeStruct((1, W), F32), jax.ShapeDtypeStruct((2 * LORA, W), F32),
                   jax.ShapeDtypeStruct((1, W), F32), jax.ShapeDtypeStruct((2 * LORA, W), F32),
                   jax.ShapeDtypeStruct((1, W), F32), jax.ShapeDtypeStruct((1, W), F32)],
        scratch_shapes=[pltpu.VMEM((8, SHIFT), F32)],
        compiler_params=_cparams(("arbitrary",), VMEM_BIG),
    )(ps, ps, mu, w0, w2x, a0, a2x, kkw, kaw, dr, dlw, dk2, dv, daa, dbb, dr2, dk22, dv2)


NPAIR = NH // 2
PW = 2 * HD


def _pair_cols(p):
    return slice(p * PW, (p + 1) * PW)


def _pairs(a):
    return [a[:, _pair_cols(p)] for p in range(NPAIR)]


def _stack_pair(a):
    first = _iota2(a.shape, 1) < HD
    zero = jnp.zeros_like(a)
    return jnp.concatenate([jnp.where(first, a, zero), jnp.where(first, zero, a)], axis=0)


def _unstack_pair(a):
    n = a.shape[0] // 2
    return jnp.where(_iota2((n, PW), 1) < HD, a[:n], a[n:])


def _fold_pair(a):
    n = a.shape[0] // 2
    return a[:n] + a[n:]


def _chunk_masks():
    n = 4 * L
    row = _iota2((n, n), 0)
    col = _iota2((n, n), 1)
    same = ((row // L) & 1) == ((col // L) & 1)
    ri = row & (L - 1)
    ci = col & (L - 1)
    keep = same & (((row < 2 * L) & (ri > ci)) | ((row >= 2 * L) & (ri >= ci)))
    r1 = _iota2((L, L), 0)
    c1 = _iota2((L, L), 1)
    r2 = _iota2((2 * L, 2 * L), 0)
    c2 = _iota2((2 * L, 2 * L), 1)
    return keep.astype(F32), (r1 >= c1).astype(F32), (r2 == c2).astype(F32)


def _scaled(r, lw, k2, aa, bb, tri):
    g = _hdot(tri, lw)
    eg = jnp.exp(g)
    eng = jnp.exp(-g)
    egp = jnp.exp(g - lw)
    return eg, eng, egp, aa * egp, r * eg, bb * eng, k2 * eng


def _head_cols(h):
    return slice(h * HD, (h + 1) * HD)


def _per_head(a):
    return [a[:, _head_cols(h)] for h in range(NH)]


def _pairs_operands(at, rt, bt, kt):
    x = [jnp.concatenate([_stack_pair(a), _stack_pair(r)], axis=0).astype(BF16) for a, r in zip(_pairs(at), _pairs(rt))]
    yk = [jnp.concatenate([_stack_pair(b), _stack_pair(k)], axis=0).astype(BF16) for b, k in zip(_pairs(bt), _pairs(kt))]
    return x, yk


def _pairs_matrices(x, yk, keep, eye):
    m = [_bdot_nt(a, b) * keep for a, b in zip(x, yk)]
    p = [a[:2 * L, :2 * L] for a in m]
    tinv = [eye + a for a in p]
    for _ in range(5):
        p = [_bdot(a, a) for a in p]
        tinv = [t + _bdot(t, a) for t, a in zip(tinv, p)]
    return [a.astype(BF16) for a in m], [a.astype(BF16) for a in tinv]


def _pairs_fwd(x, yk, m, tinv, vw, s0, egl):
    xh = [_bdot_nt(a, s) for a, s in zip(x, s0)]
    u = [_bdot(t, h[:2 * L] + _bdot(a[:2 * L, 2 * L:], w)) for t, h, a, w in zip(tinv, xh, m, vw)]
    uv = [jnp.concatenate([a, w], axis=0).astype(BF16) for a, w in zip(u, vw)]
    y = [h[2 * L:] + _bdot(a[2 * L:], w) for h, a, w in zip(xh, m, uv)]
    sn = [e * (s + _bdot_tn(w, b)) for e, s, w, b in zip(egl, s0, uv, yk)]
    return y, sn, uv


def _pairs_bwd(x, yk, m, tinv, uv, s0, sn, egl, dyw, dsn, keep):
    dzs = [d * e for d, e in zip(dsn, egl)]
    dgl = [jnp.sum(d * s, axis=0, keepdims=True) for d, s in zip(dsn, sn)]
    dyb = [a.astype(BF16) for a in dyw]
    t1 = [_bdot_tn(a[2 * L:], d) for a, d in zip(m, dyb)]
    t2 = [_bdot_nt(b, d) for b, d in zip(yk, dzs)]
    drhs = [_bdot_tn(t, a[:2 * L] + b[:2 * L]) for t, a, b in zip(tinv, t1, t2)]
    dv = [a[2 * L:] + b[2 * L:] + _bdot_tn(c[:2 * L, 2 * L:], d) for a, b, c, d in zip(t1, t2, m, drhs)]
    gg = [jnp.concatenate([a, b], axis=0).astype(BF16) for a, b in zip(drhs, dyw)]
    ds0 = [d + _bdot_tn(g, a) for d, g, a in zip(dzs, gg, x)]
    dm = [_bdot_nt(g, w) * keep for g, w in zip(gg, uv)]
    dx = [_bdot(g, s) + _bdot(d, b) for g, s, d, b in zip(gg, s0, dm, yk)]
    dyk = [_bdot_tn(d, a) + _bdot(w, z) for d, a, w, z in zip(dm, x, uv, dzs)]
    return dx, dyk, dv, dgl, ds0


STATE_SHAPE = (NPAIR * PW, PW)
M_SHAPE = (4 * L, NPAIR * 4 * L)
TINV_SHAPE = (2 * L, NPAIR * 2 * L)


def _rows_of(a, n):
    return [a[i * n:(i + 1) * n, :] for i in range(NPAIR)]


def _both(f):
    out = []
    for s in range(NSEQ):
        out += f(s)
    return out


def _seq_view(a):
    return a.reshape(NSEQ, SEQ, a.shape[-1])


UV_SHAPE = (4 * L, NPAIR * PW)


def rwkv_fwd(r, lw, k2, v, aa, bb):
    def body(r_ref, lw_ref, k2_ref, v_ref, aa_ref, bb_ref, y_ref, hs_ref, hn_ref, m_ref, t_ref, uv_ref, state):
        @pl.when(pl.program_id(0) == 0)
        def _():
            state[...] = jnp.zeros_like(state)

        s_all = state[...]
        hs_ref[0] = s_all
        keep, tri, eye = _chunk_masks()
        sc = [_scaled(r_ref[s], lw_ref[s], k2_ref[s], aa_ref[s], bb_ref[s], tri) for s in range(NSEQ)]
        ops = [_pairs_operands(*sc[s][3:]) for s in range(NSEQ)]
        x, yk = _both(lambda s: ops[s][0]), _both(lambda s: ops[s][1])
        m, tinv = _pairs_matrices(x, yk, keep, eye)
        vw = _both(lambda s: [_stack_pair(a) for a in _pairs(v_ref[s])])
        s0 = _both(lambda s: _rows_of(s_all[s], PW))
        y, sn, uv = _pairs_fwd(x, yk, m, tinv, vw, s0, _both(lambda s: _pairs(sc[s][0][L - 1:L, :])))
        for s in range(NSEQ):
            mine = slice(s * NPAIR, (s + 1) * NPAIR)
            y_ref[s] = jnp.concatenate([_fold_pair(a) for a in y[mine]], axis=-1)
            m_ref[0, s] = jnp.concatenate(m[mine], axis=-1)
            t_ref[0, s] = jnp.concatenate(tinv[mine], axis=-1)
            uv_ref[0, s] = jnp.concatenate(uv[mine], axis=-1)
            s_new = jnp.concatenate(sn[mine], axis=0)
            hn_ref[0, s] = s_new
            state[s] = s_new

    blk = pl.BlockSpec((NSEQ, L, W), lambda c: (0, c, 0))
    per_chunk = lambda shape: pl.BlockSpec((1, NSEQ) + shape, lambda c: (c, 0, 0, 0))
    saved_shapes = [(STATE_SHAPE, F32), (STATE_SHAPE, F32), (M_SHAPE, BF16), (TINV_SHAPE, BF16), (UV_SHAPE, BF16)]
    y, *saved = pl.pallas_call(
        body, grid=(NC,), name="rwkv_fwd",
        in_specs=[blk] * 6,
        out_specs=[blk] + [per_chunk(shape) for shape, _ in saved_shapes],
        out_shape=[jax.ShapeDtypeStruct((NSEQ, SEQ, W), F32)]
        + [jax.ShapeDtypeStruct((NC, NSEQ) + shape, dt) for shape, dt in saved_shapes],
        scratch_shapes=[pltpu.VMEM((NSEQ,) + STATE_SHAPE, F32)],
        compiler_params=_cparams(("arbitrary",)),
    )(*[_seq_view(a) for a in (r, lw, k2, v, aa, bb)])
    return y.reshape(T, W), saved


def rwkv_bwd(r, lw, k2, aa, bb, saved, dy):
    def body(r_ref, lw_ref, k2_ref, aa_ref, bb_ref, hs_ref, hn_ref, m_ref, t_ref, uv_ref, dy_ref,
             dr_ref, dlw_ref, dk2_ref, dv_ref, daa_ref, dbb_ref, dstate):
        @pl.when(pl.program_id(0) == 0)
        def _():
            dstate[...] = jnp.zeros_like(dstate)

        keep, tri, _ = _chunk_masks()
        sc = [_scaled(r_ref[s], lw_ref[s], k2_ref[s], aa_ref[s], bb_ref[s], tri) for s in range(NSEQ)]
        ops = [_pairs_operands(*sc[s][3:]) for s in range(NSEQ)]
        x, yk = _both(lambda s: ops[s][0]), _both(lambda s: ops[s][1])
        m = _both(lambda s: [m_ref[0, s][:, i * 4 * L:(i + 1) * 4 * L] for i in range(NPAIR)])
        tinv = _both(lambda s: [t_ref[0, s][:, i * 2 * L:(i + 1) * 2 * L] for i in range(NPAIR)])
        uv = _both(lambda s: _pairs(uv_ref[0, s]))
        dyw = _both(lambda s: [_stack_pair(a) for a in _pairs(dy_ref[s])])
        s0 = _both(lambda s: _rows_of(hs_ref[0, s], PW))
        sn = _both(lambda s: _rows_of(hn_ref[0, s], PW))
        dsn = _both(lambda s: _rows_of(dstate[s], PW))
        egl = _both(lambda s: _pairs(sc[s][0][L - 1:L, :]))
        dx, dyk, dvw, dgl, ds0 = _pairs_bwd(x, yk, m, tinv, uv, s0, sn, egl, dyw, dsn, keep)
        for s in range(NSEQ):
            mine = slice(s * NPAIR, (s + 1) * NPAIR)
            eg, eng, egp, at, rt, bt, kt = sc[s]
            dstate[s] = jnp.concatenate(ds0[mine], axis=0)
            dv_ref[s] = jnp.concatenate([_fold_pair(a) for a in dvw[mine]], axis=-1)
            dat = jnp.concatenate([_fold_pair(a[:2 * L]) for a in dx[mine]], axis=-1)
            drt = jnp.concatenate([_fold_pair(a[2 * L:]) for a in dx[mine]], axis=-1)
            dbt = jnp.concatenate([_fold_pair(a[:2 * L]) for a in dyk[mine]], axis=-1)
            dkt = jnp.concatenate([_fold_pair(a[2 * L:]) for a in dyk[mine]], axis=-1)
            dg = drt * rt - dbt * bt - dkt * kt
            dg = dg + jnp.where(_iota2(dg.shape, 0) == L - 1, jnp.concatenate(dgl[mine], axis=-1), 0.0)
            dgp = dat * at
            dlw_ref[s] = _hdot_tn(tri, dg + dgp) - dgp
            dr_ref[s] = drt * eg
            daa_ref[s] = dat * egp
            dbb_ref[s] = dbt * eng
            dk2_ref[s] = dkt * eng

    blk = pl.BlockSpec((NSEQ, L, W), lambda c: (0, NC - 1 - c, 0))
    per_chunk = lambda shape: pl.BlockSpec((1, NSEQ) + shape, lambda c: (NC - 1 - c, 0, 0, 0))
    outs = pl.pallas_call(
        body, grid=(NC,), name="rwkv_bwd",
        in_specs=[blk] * 5 + [per_chunk(a.shape[2:]) for a in saved] + [blk],
        out_specs=[blk] * 6,
        out_shape=[jax.ShapeDtypeStruct((NSEQ, SEQ, W), F32)] * 6,
        scratch_shapes=[pltpu.VMEM((NSEQ,) + STATE_SHAPE, F32)],
        compiler_params=_cparams(("arbitrary",)),
    )(*[_seq_view(a) for a in (r, lw, k2, aa, bb)], *saved, _seq_view(dy))
    return [a.reshape(T, W) for a in outs]


def _post_math(y, r, k2, v, ga, o, gb, lng, lnb, rk, bd):
    mu = _headsum(y, bd) * (1.0 / HD)
    yc = y - mu
    var = _headsum(yc * yc, bd) * (1.0 / HD)
    yn = yc * lax.rsqrt(var + GN_EPS) * lng + lnb
    bonus = _headsum(r * k2 * rk, bd) * v
    return (yn + bonus) * _silu(ga), o * _silu(gb)


def even_post(y, r, k2, v, ga, o, gb, lng, lnb, rk, h, w_bf):
    tm = 512

    def body(y_ref, r_ref, k2_ref, v_ref, ga_ref, o_ref, gb_ref, lng_ref, lnb_ref, rk_ref, h_ref, w_ref,
             ho_ref, zt_ref):
        ya, yb = _post_math(y_ref[...], r_ref[...], k2_ref[...], v_ref[...], ga_ref[...], o_ref[...], gb_ref[...],
                            lng_ref[...], lnb_ref[...], rk_ref[...], _head_blockdiag())
        z = jnp.concatenate([ya.astype(BF16), yb.astype(BF16)], axis=-1)
        zt_ref[...] = z.T
        ho_ref[...] = h_ref[...] + jnp.dot(z, w_ref[...], preferred_element_type=F32)

    vec = _const_spec((1, W))
    return pl.pallas_call(
        body, grid=(T // tm,), name="even_post",
        in_specs=[_row_spec(tm, W)] * 7 + [vec] * 3 + [_row_spec(tm, D), _const_spec((D, D))],
        out_specs=[_row_spec(tm, D), _col_spec(D, tm)],
        out_shape=[jax.ShapeDtypeStruct((T, D), F32), jax.ShapeDtypeStruct((D, T), BF16)],
        compiler_params=_cparams(("parallel",), VMEM_BIG),
    )(y, r, k2, v, ga, o, gb, lng, lnb, rk, h, w_bf)


def even_post_bwd(y, r, k2, v, ga, o, gb, lng, lnb, rk, dh, zt_bf, w_bf, after=None):
    tm = 512
    extra_specs, extra = _after_operand(after)

    def body(y_ref, r_ref, k2_ref, v_ref, ga_ref, o_ref, gb_ref, lng_ref, lnb_ref, rk_ref, dh_ref, zt_ref, w_ref,
             *rest):
        dy_ref, dr_ref, dk2_ref, dv_ref, dga_ref, do_ref, dgb_ref, dlng_ref, dlnb_ref, drk_ref, dw_ref = rest[-11:]
        dzv = _out_proj_back(dh_ref, zt_ref, w_ref, dw_ref)
        bd = _head_blockdiag()
        _, vjp = jax.vjp(lambda *a: _post_math(*a, bd), y_ref[...], r_ref[...], k2_ref[...], v_ref[...], ga_ref[...],
                         o_ref[...], gb_ref[...], lng_ref[...], lnb_ref[...], rk_ref[...])
        dy, dr, dk2, dv, dga, do, dgb, dlng, dlnb, drk = vjp((dzv[:, 0:W], dzv[:, W:2 * W]))
        for ref, val in ((dy_ref, dy), (dr_ref, dr), (dk2_ref, dk2), (dv_ref, dv), (dga_ref, dga), (do_ref, do),
                         (dgb_ref, dgb)):
            ref[...] = val.astype(ref.dtype)

        @pl.when(pl.program_id(0) == 0)
        def _():
            for ref in (dlng_ref, dlnb_ref, drk_ref):
                ref[...] = jnp.zeros_like(ref)

        dlng_ref[...] += dlng
        dlnb_ref[...] += dlnb
        drk_ref[...] += drk

    vec = _const_spec((1, W))
    return pl.pallas_call(
        body, grid=(T // tm,), name="even_post_bwd",
        in_specs=[_row_spec(tm, W)] * 7 + [vec] * 3 + [_row_spec(tm, D), _col_spec(D, tm), _const_spec((D, D))]
        + extra_specs,
        out_specs=[_row_spec(tm, W)] * 7 + [vec] * 3 + [_const_spec((D, D))],
        out_shape=[jax.ShapeDtypeStruct((T, W), dt) for dt in (F32, F32, F32, F32, BF16, F32, BF16)]
        + [jax.ShapeDtypeStruct((1, W), F32)] * 3 + [jax.ShapeDtypeStruct((D, D), F32)],
        compiler_params=_cparams(("arbitrary",), VMEM_BIG),
    )(y, r, k2, v, ga, o, gb, lng, lnb, rk, dh, zt_bf, w_bf, *extra)


PADSEQ = SEQ + LEFT * L
ATT_SCALE = 1.0 / math.sqrt(HD)
ATT_Q = 4
WIN = BAND + (ATT_Q - 1) * L
ATT_STEPS = NC // ATT_Q
ATT_BIAS_SHAPE = (NPAIR, ATT_Q * 2 * L, WIN)


def _stack_chunks(a):
    return jnp.concatenate([_stack_pair(a[i * L:(i + 1) * L]) for i in range(ATT_Q)], axis=0)


def _unstack_chunks(a):
    return jnp.concatenate([_unstack_pair(a[i * 2 * L:(i + 1) * 2 * L]) for i in range(ATT_Q)], axis=0)


def window_bias(bias):
    parts = [jnp.pad(bias, ((0, 0), (0, 0), (i * L, (ATT_Q - 1 - i) * L)), constant_values=NEG) for i in range(ATT_Q)]
    return jnp.concatenate(parts, axis=1)


def _att_probs(q2, kw, bias, step):
    valid = _iota2((1, WIN), 1) >= (LEFT - step * ATT_Q) * L
    s = [jnp.where(valid, _bdot_nt(a, b) * ATT_SCALE + bias[p], NEG) for p, (a, b) in enumerate(zip(q2, kw))]
    e = [jnp.exp(a - jnp.max(a, axis=-1, keepdims=True)) for a in s]
    return [a / jnp.sum(a, axis=-1, keepdims=True) for a in e]


def attention_fwd(q, kpad, vpad, bias):
    def body(q_ref, k_ref, v_ref, b_ref, o_ref):
        step = pl.program_id(1)
        start = pl.multiple_of(step * (ATT_Q * L), L)
        kw = _pairs(k_ref[pl.ds(start, WIN), :])
        vw = _pairs(v_ref[pl.ds(start, WIN), :])
        q2 = [_stack_chunks(a) for a in _pairs(q_ref[...].astype(BF16))]
        p = _att_probs(q2, kw, b_ref[...], step)
        o_ref[...] = jnp.concatenate([_unstack_chunks(_bdot(a, b)) for a, b in zip(p, vw)], axis=-1)

    qblk = pl.BlockSpec((ATT_Q * L, W), lambda b, c: (b * ATT_STEPS + c, 0))
    kblk = pl.BlockSpec((PADSEQ, W), lambda b, c: (b, 0))
    return pl.pallas_call(
        body, grid=(NSEQ, ATT_STEPS), name="attention_fwd",
        in_specs=[qblk, kblk, kblk, _const_spec(ATT_BIAS_SHAPE)],
        out_specs=qblk, out_shape=jax.ShapeDtypeStruct((T, W), F32),
        compiler_params=_cparams(("parallel", "arbitrary")),
    )(q, kpad, vpad, bias)


def attention_bwd(q, kpad, vpad, bias, do):
    def body(q_ref, k_ref, v_ref, b_ref, do_ref, dq_ref, dko_ref, dvo_ref, db_ref, dk_ref, dv_ref):
        b = pl.program_id(0)
        c = pl.program_id(1)

        @pl.when(c == 0)
        def _():
            dk_ref[...] = jnp.zeros_like(dk_ref)
            dv_ref[...] = jnp.zeros_like(dv_ref)

        @pl.when((c == 0) & (b == 0))
        def _():
            db_ref[...] = jnp.zeros_like(db_ref)

        start = pl.multiple_of(c * (ATT_Q * L), L)
        kw = _pairs(k_ref[pl.ds(start, WIN), :])
        vw = _pairs(v_ref[pl.ds(start, WIN), :])
        q2 = [_stack_chunks(a) for a in _pairs(q_ref[...].astype(BF16))]
        do2 = [_stack_chunks(a) for a in _pairs(do_ref[...].astype(BF16))]
        p = _att_probs(q2, kw, b_ref[...], c)
        dp = [_bdot_nt(a, b) for a, b in zip(do2, vw)]
        ds = [a * (d - jnp.sum(d * a, axis=-1, keepdims=True)) for a, d in zip(p, dp)]
        dss = [(a * ATT_SCALE).astype(BF16) for a in ds]
        dq_ref[...] = jnp.concatenate([_unstack_chunks(_bdot(a, b)) for a, b in zip(dss, kw)], axis=-1).astype(BF16)
        dk_ref[pl.ds(start, WIN), :] += jnp.concatenate([_bdot_tn(a, b) for a, b in zip(dss, q2)], axis=-1)
        dv_ref[pl.ds(start, WIN), :] += jnp.concatenate([_bdot_tn(a, b) for a, b in zip(p, do2)], axis=-1)
        for i in range(NPAIR):
            db_ref[i] += ds[i]

        @pl.when(c == ATT_STEPS - 1)
        def _():
            dko_ref[...] = dk_ref[LEFT * L:, :].astype(BF16)
            dvo_ref[...] = dv_ref[LEFT * L:, :].astype(BF16)

    qblk = pl.BlockSpec((ATT_Q * L, W), lambda b, c: (b * ATT_STEPS + c, 0))
    kblk = pl.BlockSpec((PADSEQ, W), lambda b, c: (b, 0))
    sblk = pl.BlockSpec((SEQ, W), lambda b, c: (b, 0))
    bblk = _const_spec(ATT_BIAS_SHAPE)
    return pl.pallas_call(
        body, grid=(NSEQ, ATT_STEPS), name="attention_bwd",
        in_specs=[qblk, kblk, kblk, bblk, qblk],
        out_specs=[qblk, sblk, sblk, bblk],
        out_shape=[jax.ShapeDtypeStruct((T, W), BF16), jax.ShapeDtypeStruct((T, W), BF16),
                   jax.ShapeDtypeStruct((T, W), BF16), jax.ShapeDtypeStruct(ATT_BIAS_SHAPE, F32)],
        scratch_shapes=[pltpu.VMEM((PADSEQ, W), F32), pltpu.VMEM((PADSEQ, W), F32)],
        compiler_params=_cparams(("arbitrary", "arbitrary"), VMEM_BIG),
    )(q, kpad, vpad, bias, do)


NTAB = 2 * CLIP + 1
EXT = BAND + L


def _ext_onehot():
    n = _iota2((EXT, NTAB), 0)
    m = _iota2((EXT, NTAB), 1)
    return (jnp.clip(BAND - 1 - n, -CLIP, CLIP) + CLIP == m).astype(F32)


def bias_expand(table):
    def body(t_ref, o_ref):
        ext = _hdot_nt(t_ref[...], _ext_onehot())
        for i in range(L):
            s = L - 1 - i
            o_ref[:, i, :] = (pltpu.roll(ext, EXT - s, 1) if s else ext)[:, :BAND]

    return pl.pallas_call(body, name="bias_expand", out_shape=jax.ShapeDtypeStruct((NH, L, BAND), F32))(table)


def bias_grad(dbias):
    def body(d_ref, o_ref):
        acc = jnp.zeros((NH, EXT), F32)
        zpad = jnp.zeros((NH, EXT - BAND), F32)
        for i in range(L):
            s = L - 1 - i
            row = jnp.concatenate([d_ref[:, i, :], zpad], axis=-1)
            acc = acc + (pltpu.roll(row, s, 1) if s else row)
        o_ref[...] = _hdot(acc, _ext_onehot())

    return pl.pallas_call(body, name="bias_grad", out_shape=jax.ShapeDtypeStruct((NH, NTAB), F32))(dbias)


def _group_cols(g):
    return slice(g * SGC, (g + 1) * SGC)


def _sg_norm(gv, lng, lnb):
    gc = gv - jnp.mean(gv, axis=-1, keepdims=True)
    rstd = lax.rsqrt(jnp.mean(gc * gc, axis=-1, keepdims=True) + LN_EPS)
    xhat = gc * rstd
    return xhat, rstd, xhat * lng + lnb


GMLP_BWD_CHUNKS = 2


def gmlp_fwd_loss(u, v, gate, lng, lnb, wm_bf, sgb_t, h, w_bf, g_final, target):
    tm = GMLP_BWD_CHUNKS * SGC

    def body(u_ref, v_ref, gt_ref, lng_ref, lnb_ref, wm_ref, sb_ref, h_ref, w_ref, g_ref, t_ref,
             dh_ref, loss_ref, dg_ref, zt_ref):
        zs = []
        for ch in range(GMLP_BWD_CHUNKS):
            rows = slice(ch * SGC, (ch + 1) * SGC)
            _, _, vln = _sg_norm(_gelu(v_ref[rows, :]), lng_ref[...], lnb_ref[...])
            vlb = vln.astype(BF16)
            zg = []
            for g in range(NG):
                cs = _group_cols(g)
                sv = jnp.dot(wm_ref[g], vlb[:, cs], preferred_element_type=F32) + sb_ref[:, g:g + 1]
                zg.append((_gelu(u_ref[rows, cs]) * sv * _silu(gt_ref[rows, cs])).astype(BF16))
            zs.append(jnp.concatenate(zg, axis=-1))
        z = jnp.concatenate(zs, axis=0)
        zt_ref[...] = z.T
        xv = h_ref[...] + jnp.dot(z, w_ref[...], preferred_element_type=F32)
        rstd = lax.rsqrt(jnp.mean(xv * xv, axis=-1, keepdims=True) + RMS_EPS)
        xhat = xv * rstd
        err = xhat * g_ref[...] - t_ref[...]
        part = 0.5 * jnp.sum(jnp.mean(err * err, axis=-1, keepdims=True), axis=0, keepdims=True)
        dout = err * (1.0 / D)

        @pl.when(pl.program_id(0) == 0)
        def _():
            loss_ref[...] = jnp.zeros_like(loss_ref)
            dg_ref[...] = jnp.zeros_like(dg_ref)

        loss_ref[...] += jnp.broadcast_to(part, loss_ref.shape)
        dg_ref[...] += jnp.sum(dout * xhat, axis=0, keepdims=True)
        dxh = dout * g_ref[...]
        dh_ref[...] = rstd * (dxh - xhat * jnp.mean(dxh * xhat, axis=-1, keepdims=True))

    return pl.pallas_call(
        body, grid=(T // tm,), name="gmlp_fwd_loss",
        in_specs=[_row_spec(tm, D)] * 3 + [_const_spec((1, D))] * 2
        + [_const_spec((NG, SGC, SGC)), _const_spec((SGC, NG)), _row_spec(tm, D), _const_spec((D, D)),
           _const_spec((1, D)), _row_spec(tm, D)],
        out_specs=[_row_spec(tm, D), _const_spec((8, 128)), _const_spec((1, D)), _col_spec(D, tm)],
        out_shape=[jax.ShapeDtypeStruct((T, D), F32), jax.ShapeDtypeStruct((8, 128), F32),
                   jax.ShapeDtypeStruct((1, D), F32), jax.ShapeDtypeStruct((D, T), BF16)],
        compiler_params=_cparams(("arbitrary",), VMEM_BIG),
    )(u, v, gate, lng, lnb, wm_bf, sgb_t, h, w_bf, g_final, target)


def gmlp_bwd(u, v, gate, lng, lnb, wm_bf, sgb_t, dh, zt_bf, w_bf):
    def body(u_ref, v_ref, gt_ref, lng_ref, lnb_ref, wm_ref, sb_ref, dh_ref, zt_ref, w_ref,
             du_ref, dv_ref, dgt_ref, dlng_ref, dlnb_ref, dwm_ref, dsb_ref, dw_ref):
        @pl.when(pl.program_id(0) == 0)
        def _():
            for ref in (dlng_ref, dlnb_ref, dwm_ref, dsb_ref):
                ref[...] = jnp.zeros_like(ref)

        dz = _out_proj_back(dh_ref, zt_ref, w_ref, dw_ref)
        sel = (_iota2((D, NG), 0) // SGC == _iota2((D, NG), 1)).astype(F32)
        for ch in range(GMLP_BWD_CHUNKS):
            rows = slice(ch * SGC, (ch + 1) * SGC)
            gv, dgv_dv = _gelu_both(v_ref[rows, :])
            xhat, rstd, vln = _sg_norm(gv, lng_ref[...], lnb_ref[...])
            vlb = vln.astype(BF16)
            dvln = []
            dsv_all = []
            for g in range(NG):
                cs = _group_cols(g)
                uu = u_ref[rows, cs]
                gg = gt_ref[rows, cs]
                dzz = dz[rows, cs]
                sv = jnp.dot(wm_ref[g], vlb[:, cs], preferred_element_type=F32) + sb_ref[:, g:g + 1]
                gu, dgu = _gelu_both(uu)
                sg, dsg = _silu_both(gg)
                dzgu = dzz * gu
                dsv = dzgu * sg
                dgt_ref[rows, cs] = (dzgu * sv * dsg).astype(BF16)
                du_ref[rows, cs] = (dzz * sv * sg * dgu).astype(BF16)
                dsb16 = dsv.astype(BF16)
                dvln.append(lax.dot_general(wm_ref[g], dsb16, (((0,), (0,)), ((), ())), preferred_element_type=F32))
                dwm_ref[g] += lax.dot_general(dsb16, vlb[:, cs], (((1,), (1,)), ((), ())),
                                              preferred_element_type=F32)
                dsv_all.append(dsv)
            dvl = jnp.concatenate(dvln, axis=-1)
            dsb_ref[...] += _hdot(jnp.concatenate(dsv_all, axis=-1), sel)
            dlng_ref[...] += jnp.sum(dvl * xhat, axis=0, keepdims=True)
            dlnb_ref[...] += jnp.sum(dvl, axis=0, keepdims=True)
            dxh = dvl * lng_ref[...]
            dgv = rstd * (dxh - jnp.mean(dxh, axis=-1, keepdims=True)
                          - xhat * jnp.mean(dxh * xhat, axis=-1, keepdims=True))
            dv_ref[rows, :] = (dgv * dgv_dv).astype(BF16)

    tm = GMLP_BWD_CHUNKS * SGC
    return pl.pallas_call(
        body, grid=(T // tm,), name="gmlp_bwd",
        in_specs=[_row_spec(tm, D)] * 3 + [_const_spec((1, D))] * 2
        + [_const_spec((NG, SGC, SGC)), _const_spec((SGC, NG)), _row_spec(tm, D), _col_spec(D, tm),
           _const_spec((D, D))],
        out_specs=[_row_spec(tm, D)] * 3 + [_const_spec((1, D))] * 2
        + [_const_spec((NG, SGC, SGC)), _const_spec((SGC, NG)), _const_spec((D, D))],
        out_shape=[jax.ShapeDtypeStruct((T, D), BF16)] * 3 + [jax.ShapeDtypeStruct((1, D), F32)] * 2
        + [jax.ShapeDtypeStruct((NG, SGC, SGC), F32), jax.ShapeDtypeStruct((SGC, NG), F32),
           jax.ShapeDtypeStruct((D, D), F32)],
        compiler_params=_cparams(("arbitrary",), VMEM_BIG),
    )(u, v, gate, lng, lnb, wm_bf, sgb_t, dh, zt_bf, w_bf)


NCHIP = 4
NDEV = 8
ANY = pl.BlockSpec(memory_space=pl.ANY)


HBM = pl.BlockSpec(memory_space=pltpu.HBM)
SEM = pl.BlockSpec(memory_space=pltpu.SEMAPHORE)
EFFECT = pltpu.SideEffectType.DATAFLOW_SIDE_EFFECTING


def _peers(whole_mesh):
    x, y, c = lax.axis_index("x"), lax.axis_index("y"), lax.axis_index("c")
    if not whole_mesh:
        return [((px, py, c), 2 * px + py) for px, py in ((1 - x, y), (x, 1 - y), (1 - x, 1 - y))], 2 * x + y
    out = []
    for j in range(1, NDEV):
        px, py, pc = x ^ (j >> 2), y ^ ((j >> 1) & 1), c ^ (j & 1)
        out.append(((px, py, pc), 4 * px + 2 * py + pc))
    return out, 4 * x + 2 * y + c


def _send_copies(src, land, send, recv, scatter, whole_mesh, starting):
    peers, me = _peers(whole_mesh)
    copies = []
    for t in range(len(src)):
        for j, (dev, slot) in enumerate(peers):
            k = t * len(peers) + j
            copies.append(pltpu.make_async_remote_copy(
                src_ref=src[t].at[slot] if scatter else src[t], dst_ref=land[t].at[me if starting else slot],
                send_sem=send.at[k], recv_sem=recv.at[k], device_id=dev, device_id_type=MESH))
    return copies


def send_start(srcs, lands, scatter, whole_mesh, name, after=None):
    n = len(srcs)
    nsem = n * (NDEV - 1 if whole_mesh else NCHIP - 1)
    extra_specs, extra = _after_operand(after)

    def body(*refs):
        send, recv = refs[2 * n + len(extra)], refs[2 * n + len(extra) + 1]
        for cp in _send_copies(refs[:n], refs[n:2 * n], send, recv, scatter, whole_mesh, True):
            cp.start()
        refs[-1][...] = jnp.zeros_like(refs[-1])

    arrs = list(srcs) + list(lands)
    out = pl.pallas_call(
        body, name=name,
        out_shape=(pltpu.SemaphoreType.DMA((nsem,)), pltpu.SemaphoreType.DMA((nsem,)),
                   *[pltpu.HBM(a.shape, a.dtype) for a in arrs], jax.ShapeDtypeStruct((8, 128), F32)),
        in_specs=[HBM] * (2 * n) + extra_specs,
        out_specs=(SEM, SEM, *[HBM] * (2 * n), pl.BlockSpec(memory_space=pltpu.VMEM)),
        input_output_aliases={i: 2 + i for i in range(2 * n)},
        compiler_params=pltpu.CompilerParams(has_side_effects=EFFECT),
    )(*[pltpu.with_memory_space_constraint(a, pltpu.HBM) for a in arrs], *extra)
    return out[0], out[1], list(out[2:2 + n]), list(out[2 + n:2 + 2 * n]), out[-1]


def send_wait(started, after, scatter, whole_mesh, name):
    send, recv, srcs, lands, _ = started
    n = len(srcs)

    def body(*refs):
        for cp in _send_copies(refs[:n], refs[n:2 * n], refs[2 * n], refs[2 * n + 1], scatter, whole_mesh, False):
            cp.wait_send()
            cp.wait_recv()

    arrs = list(srcs) + list(lands)
    out = pl.pallas_call(
        body, name=name, out_shape=tuple(pltpu.HBM(a.shape, a.dtype) for a in arrs),
        in_specs=[HBM] * (2 * n) + [SEM, SEM, ANY], out_specs=tuple([HBM] * (2 * n)),
        input_output_aliases={i: i for i in range(2 * n)},
        compiler_params=pltpu.CompilerParams(has_side_effects=EFFECT),
    )(*arrs, send, recv, after)
    return list(out[n:])


def exchange_c(arrs, name):
    n = len(arrs)

    def body(*refs):
        ins, outs = refs[:n], refs[n:2 * n]
        send, recv = refs[2 * n:]
        sibling = (lax.axis_index("x"), lax.axis_index("y"), 1 - lax.axis_index("c"))
        copies = [pltpu.make_async_remote_copy(src_ref=ins[t], dst_ref=outs[t], send_sem=send.at[t], recv_sem=recv.at[t],
                                               device_id=sibling, device_id_type=MESH) for t in range(n)]
        for cp in copies:
            cp.start()
        for cp in copies:
            cp.wait()

    return pl.pallas_call(
        body, name=name, in_specs=[ANY] * n, out_specs=[ANY] * n,
        out_shape=[jax.ShapeDtypeStruct(a.shape, a.dtype) for a in arrs],
        scratch_shapes=[pltpu.SemaphoreType.DMA((n,)), pltpu.SemaphoreType.DMA((n,))],
    )(*arrs)


def swap_row_halves(a, name):
    n, rows, cols = a.shape
    half = rows // 2

    def body(in_ref, out_ref, send, recv):
        x, y, c = lax.axis_index("x"), lax.axis_index("y"), lax.axis_index("c")
        cp = pltpu.make_async_remote_copy(src_ref=in_ref.at[:, pl.ds((1 - c) * half, half), :], dst_ref=out_ref,
                                          send_sem=send, recv_sem=recv, device_id=(x, y, 1 - c), device_id_type=MESH)
        cp.start()
        cp.wait()

    return pl.pallas_call(
        body, name=name, in_specs=[ANY], out_specs=ANY, out_shape=jax.ShapeDtypeStruct((n, half, cols), a.dtype),
        scratch_shapes=[pltpu.SemaphoreType.DMA, pltpu.SemaphoreType.DMA],
    )(a)


def add_blocks(a, b, name):
    n, rows, cols = a.shape
    tr = _rows_tile(rows)

    def body(a_ref, b_ref, o_ref):
        o_ref[...] = (a_ref[...].astype(F32) + b_ref[...].astype(F32)).astype(BF16)

    spec = pl.BlockSpec((1, tr, cols), lambda s, i: (s, i, 0))
    return pl.pallas_call(
        body, grid=(n, rows // tr), name=name, in_specs=[spec, spec], out_specs=spec,
        out_shape=jax.ShapeDtypeStruct(a.shape, BF16), compiler_params=_cparams(("parallel", "parallel")),
    )(a, b)


def gather_weights(arrs, split):
    n = len(arrs)

    def body(*refs):
        ins, outs = refs[:n], refs[n:2 * n]
        send1, recv1, send2, recv2, loc = refs[2 * n:]
        x, y, c = lax.axis_index("x"), lax.axis_index("y"), lax.axis_index("c")
        me = 2 * x + y
        sibling = (x, y, 1 - c)
        peers = [(1 - x, y), (x, 1 - y), (1 - x, 1 - y)]

        def rows_of(t, core):
            half = arrs[t].shape[0] // 2
            return pl.ds(core * half, half)

        def part(ref, t, core):
            return ref.at[rows_of(t, core)] if split[t] else ref

        local = [pltpu.make_async_copy(ins[t], outs[t].at[me], loc.at[t]) for t in range(n)]
        for cp in local:
            cp.start()
        first = []
        for t in range(n):
            for j, (px, py) in enumerate(peers):
                first.append(pltpu.make_async_remote_copy(
                    src_ref=part(ins[t], t, c), dst_ref=part(outs[t].at[me], t, c), send_sem=send1.at[t, j],
                    recv_sem=recv1.at[t, j], device_id=(px, py, c), device_id_type=MESH))
        for cp in first:
            cp.start()
        passed = []
        for t in range(n):
            for j, (px, py) in enumerate(peers):
                landed = part(outs[t].at[2 * px + py], t, c)
                pltpu.make_async_remote_copy(
                    src_ref=landed, dst_ref=landed, send_sem=send1.at[t, j], recv_sem=recv1.at[t, j],
                    device_id=(x, y, c), device_id_type=MESH).wait_recv()
                if split[t]:
                    cp = pltpu.make_async_remote_copy(
                        src_ref=landed, dst_ref=landed, send_sem=send2.at[t, j], recv_sem=recv2.at[t, j],
                        device_id=sibling, device_id_type=MESH)
                    cp.start()
                    passed.append(cp)
        for t in range(n):
            for j, (px, py) in enumerate(peers):
                if split[t]:
                    other = part(outs[t].at[2 * px + py], t, 1 - c)
                    pltpu.make_async_remote_copy(
                        src_ref=other, dst_ref=other, send_sem=send2.at[t, j], recv_sem=recv2.at[t, j],
                        device_id=(x, y, c), device_id_type=MESH).wait_recv()
        for cp in first + passed:
            cp.wait_send()
        for cp in local:
            cp.wait()

    return pl.pallas_call(
        body, name="gather_weights", in_specs=[ANY] * n, out_specs=[ANY] * n,
        out_shape=[jax.ShapeDtypeStruct((NCHIP,) + a.shape, a.dtype) for a in arrs],
        scratch_shapes=[pltpu.SemaphoreType.DMA((n, 3))] * 4 + [pltpu.SemaphoreType.DMA((n,))],
    )(*arrs)


def _adam_math(g, w, m, v):
    m = ADAM_B1 * m + (1.0 - ADAM_B1) * g
    v = ADAM_B2 * v + (1.0 - ADAM_B2) * (g * g)
    m_hat = m / (1.0 - ADAM_B1 ** ADAM_STEP)
    v_hat = v / (1.0 - ADAM_B2 ** ADAM_STEP)
    delta = -ADAM_LR * (m_hat / (jnp.sqrt(v_hat) + ADAM_EPS) + ADAM_WD * w)
    return delta, m, v


def _rows_tile(rows):
    return rows if rows <= 256 else 256


def sum_chips(own, parts, name):
    _, rows, cols = parts.shape
    tr = _rows_tile(rows)

    def body(own_ref, p_ref, o_ref):
        acc = own_ref[...].astype(F32)
        for s in range(NCHIP):
            acc = acc + p_ref[s].astype(F32)
        o_ref[...] = acc

    return pl.pallas_call(
        body, grid=(rows // tr,), name=name,
        in_specs=[pl.BlockSpec((tr, cols), lambda i: (i, 0)), pl.BlockSpec((NCHIP, tr, cols), lambda i: (0, i, 0))],
        out_specs=pl.BlockSpec((tr, cols), lambda i: (i, 0)),
        out_shape=jax.ShapeDtypeStruct((rows, cols), F32),
        compiler_params=_cparams(("parallel",)),
    )(own, parts)


def adam_shard(p_mine, p_sib, w, m, v, name):
    rows, cols = p_mine.shape
    tr = _rows_tile(rows)
    lead = w.ndim == 3

    def body(a_ref, b_ref, w_ref, m_ref, v_ref, g_ref, d_ref, mo_ref, vo_ref):
        g = a_ref[...] + b_ref[...]
        g = g[None] if lead else g
        g_ref[...] = g
        d_ref[...], mo_ref[...], vo_ref[...] = _adam_math(g, w_ref[...], m_ref[...], v_ref[...])

    flat = pl.BlockSpec((tr, cols), lambda i: (i, 0))
    spec = pl.BlockSpec((1, tr, cols), lambda i: (0, i, 0)) if lead else flat
    return pl.pallas_call(
        body, grid=(rows // tr,), name=name, in_specs=[flat] * 2 + [spec] * 3, out_specs=[spec] * 4,
        out_shape=[jax.ShapeDtypeStruct(w.shape, F32)] * 4,
        compiler_params=_cparams(("parallel",)),
    )(p_mine, p_sib, w, m, v)


def adam_shard_halves_t(r_mine, r_sib, wt, mt, vt, name):
    hrows, cols = r_mine.shape
    tr = _rows_tile(hrows)
    per_half = hrows // tr

    def body(a_ref, b_ref, w_ref, m_ref, v_ref, g_ref, d_ref, mo_ref, vo_ref):
        mine = pl.program_id(0) == lax.axis_index("c")
        g = jnp.where(mine, a_ref[...], b_ref[...]).T[None]
        g_ref[...] = g
        d_ref[...], mo_ref[...], vo_ref[...] = _adam_math(g, w_ref[...], m_ref[...], v_ref[...])

    flat = pl.BlockSpec((tr, cols), lambda h, i: (i, 0))
    spec = pl.BlockSpec((1, cols, tr), lambda h, i: (0, 0, h * per_half + i))
    return pl.pallas_call(
        body, grid=(2, per_half), name=name, in_specs=[flat] * 2 + [spec] * 3, out_specs=[spec] * 4,
        out_shape=[jax.ShapeDtypeStruct(wt.shape, F32)] * 4,
        compiler_params=_cparams(("parallel", "parallel")),
    )(r_mine, r_sib, wt, mt, vt)


def adam_replicated(parts, w, m, v, name):
    rows = w.shape[0]

    def body(p_ref, w_ref, m_ref, v_ref, g_ref, d_ref, mo_ref, vo_ref):
        g = p_ref[0]
        for d in range(1, NDEV):
            g = g + p_ref[d]
        g_ref[...] = g
        d_ref[...], mo_ref[...], vo_ref[...] = _adam_math(g, w_ref[...], m_ref[...], v_ref[...])

    return pl.pallas_call(
        body, name=name, out_shape=[jax.ShapeDtypeStruct((rows, 128), F32)] * 4,
    )(parts, w, m, v)


def _pack(arrs):
    pieces = []
    for a in arrs:
        flat = a.reshape(-1)
        pad = (-flat.shape[0]) % 128
        pieces.append(jnp.pad(flat, (0, pad)) if pad else flat)
    flat = jnp.concatenate(pieces)
    pad = (-flat.shape[0]) % 1024
    return jnp.pad(flat, (0, pad)).reshape(-1, 128)


def _unpack(buf, shapes):
    flat = buf.reshape(-1)
    out = []
    o = 0
    for s in shapes:
        n = int(np.prod(s))
        out.append(flat[o:o + n].reshape(s))
        o += n + (-n) % 128
    return out


EVEN_SPLITS = (SHIFT, W, W, W, W, W)
ODD_SPLITS = (D, D, D)


def _cols_to_chips(a):
    rows, cols = a.shape
    return a.reshape(rows, NCHIP, cols // NCHIP).transpose(1, 0, 2)


def _chips_to_cols(a):
    _, rows, n = a.shape
    return a.transpose(1, 0, 2).reshape(rows, NCHIP * n)


def kernel(x, norm_g, w_in_e, shift_mu, rw_w0, rw_w2, rw_a0, rw_a2, rw_kk, rw_ka, rw_rk, rw_lnx_g, rw_lnx_b, att_bias, w_out_e, w_in_o, sg_ln_g, sg_ln_b, sg_w, sg_b, w_out_o, final_g, loss_target, m_norm_g, m_w_in_e, m_shift_mu, m_rw_w0, m_rw_w2, m_rw_a0, m_rw_a2, m_rw_kk, m_rw_ka, m_rw_rk, m_rw_lnx_g, m_rw_lnx_b, m_att_bias, m_w_out_e, m_w_in_o, m_sg_ln_g, m_sg_ln_b, m_sg_w, m_sg_b, m_w_out_o, m_final_g, v_norm_g, v_w_in_e, v_shift_mu, v_rw_w0, v_rw_w2, v_rw_a0, v_rw_a2, v_rw_kk, v_rw_ka, v_rw_rk, v_rw_lnx_g, v_rw_lnx_b, v_att_bias, v_w_out_e, v_w_in_o, v_sg_ln_g, v_sg_ln_b, v_sg_w, v_sg_b, v_w_out_o, v_final_g):
    x2 = x.reshape(T, D)
    tgt = loss_target.reshape(T, D)

    my_chip = 2 * lax.axis_index("x") + lax.axis_index("y")
    gathered = gather_weights(
        [jnp.swapaxes(w_in_e[0], 0, 1).astype(BF16), jnp.concatenate([rw_w2[0], rw_a2[0]], axis=0),
         jnp.concatenate([sg_ln_g, sg_ln_b], axis=0)], [True, True, False])
    wie = gathered[0].reshape(EVEN_IN, D)
    w2 = _chips_to_cols(gathered[1][:, :LORA])
    a2 = _chips_to_cols(gathered[1][:, LORA:])
    sglg = _chips_to_cols(gathered[2][:, 0:1])
    sglb = _chips_to_cols(gathered[2][:, 1:2])

    late = [w_out_e[0].astype(BF16), w_in_o[0].astype(BF16), w_out_o[0].astype(BF16)]
    late_started = send_start(late, [jnp.broadcast_to(a[None], (NCHIP,) + a.shape) for a in late], False, False,
                              "late_weights_start", after=gathered[0])

    def late_weights(after):
        woe, wio, woo = send_wait(late_started, after, False, False, "late_weights_wait")
        return woe.reshape(D, D), _chips_to_cols(wio), woo.reshape(D, D)

    def scatter_start(grads, name):
        srcs = [g_.astype(BF16) if g_.shape[-1] >= W else g_ for g_ in grads]
        return send_start(srcs, [jnp.zeros_like(s) for s in srcs], True, False, name)

    def own_block(g_):
        return lax.dynamic_index_in_dim(g_, my_chip, axis=0, keepdims=False)

    started = {}

    def on_odd_grads(d_woo, d_wio):
        blocks = [d_woo.reshape(NCHIP, D // NCHIP, D), d_wio]
        started["odd"] = (scatter_start(blocks, "odd_grads_start"), [own_block(b) for b in blocks])
        return started["odd"][0][-1]

    def on_even_grads(big_g):
        d_wie, d_woe, _, _, d_w2, d_a2, d_sglg, d_sglb = big_g
        my_half = lax.dynamic_slice_in_dim(d_wie, lax.axis_index("c") * (D // 2), D // 2, axis=1)
        d_wie_half = add_blocks(my_half, swap_row_halves(d_wie, "swap_w_in_e_halves"), "add_w_in_e_halves")
        blocks = [d_wie_half, d_woe.reshape(NCHIP, D // NCHIP, D), _cols_to_chips(d_w2), _cols_to_chips(d_a2),
                  _cols_to_chips(d_sglg), _cols_to_chips(d_sglb)]
        started["even"] = (scatter_start(blocks, "even_grads_start"), [own_block(b) for b in blocks])
        return started["even"][0][-1]

    def on_small_grads(layer, grads):
        mine = _pack(grads)
        started[layer + "_small"] = send_start([mine], [jnp.broadcast_to(mine[None], (NDEV,) + mine.shape)], False,
                                               True, layer + "_small_grads_start")
        return started[layer + "_small"][-1]

    loss_part, dx, _, _ = _local_step(
        x2, tgt, wie, late_weights, w2, a2, sglg, sglb, norm_g, shift_mu, rw_w0, rw_a0, rw_kk, rw_ka, rw_rk,
        rw_lnx_g, rw_lnx_b, att_bias, sg_w, sg_b, final_g, first_after=late_started[-1], on_odd_grads=on_odd_grads,
        on_even_grads=on_even_grads, on_small_grads=on_small_grads)
    even_started, even_own = started["even"]

    wmv = {"w_in_e": tuple(jnp.swapaxes(a, 1, 2) for a in (w_in_e, m_w_in_e, v_w_in_e)),
           "w_out_e": (w_out_e, m_w_out_e, v_w_out_e),
           "w_in_o": (w_in_o, m_w_in_o, v_w_in_o), "w_out_o": (w_out_o, m_w_out_o, v_w_out_o),
           "rw_w2": (rw_w2, m_rw_w2, v_rw_w2), "rw_a2": (rw_a2, m_rw_a2, v_rw_a2),
           "sg_ln_g": (sg_ln_g, m_sg_ln_g, v_sg_ln_g), "sg_ln_b": (sg_ln_b, m_sg_ln_b, v_sg_ln_b)}
    sharded = {}

    def finish(names, own, landed, tag):
        partial = [sum_chips(o_, p_, "sum_" + nm) for o_, p_, nm in zip(own, landed, names)]
        from_sibling = exchange_c(partial, "swap_partials_" + tag)
        for nm, mine, sib in zip(names, partial, from_sibling):
            if nm == "w_in_e":
                res = adam_shard_halves_t(mine, sib, *wmv[nm], "adam_" + nm)
                sharded[nm] = [jnp.swapaxes(a, 1, 2) for a in res]
            else:
                sharded[nm] = adam_shard(mine, sib, *wmv[nm], "adam_" + nm)
        return partial[0]

    odd_started, odd_own = started["odd"]
    odd_landed = send_wait(odd_started, started["even_small"][-1], True, False, "odd_grads_wait")
    done = finish(["w_out_o", "w_in_o"], odd_own, odd_landed, "odd")

    no_w = jnp.zeros((1, 1), F32)
    groups = {
        "odd": (["sg_w", "sg_b", "final_g", "norm_g1"], [sg_w, sg_b, final_g, norm_g[1:2]],
                [m_sg_w, m_sg_b, m_final_g, m_norm_g[1:2]], [v_sg_w, v_sg_b, v_final_g, v_norm_g[1:2]]),
        "even": (["norm_g0", "shift_mu", "rw_w0", "rw_a0", "rw_kk", "rw_ka", "rw_rk", "rw_lnx_g", "rw_lnx_b",
                  "att_bias", "loss"],
                 [norm_g[0:1], shift_mu, rw_w0, rw_a0, rw_kk, rw_ka, rw_rk, rw_lnx_g, rw_lnx_b, att_bias, no_w],
                 [m_norm_g[0:1], m_shift_mu, m_rw_w0, m_rw_a0, m_rw_kk, m_rw_ka, m_rw_rk, m_rw_lnx_g, m_rw_lnx_b,
                  m_att_bias, no_w],
                 [v_norm_g[0:1], v_shift_mu, v_rw_w0, v_rw_a0, v_rw_kk, v_rw_ka, v_rw_rk, v_rw_lnx_g, v_rw_lnx_b,
                  v_att_bias, no_w]),
    }
    rep = {}
    for layer in ("odd", "even"):
        nms, ws, ms_, vs_ = groups[layer]
        (gathered_g,) = send_wait(started[layer + "_small"], done, False, True, layer + "_small_grads_wait")
        rep_out = adam_replicated(gathered_g, _pack(ws), _pack(ms_), _pack(vs_), "adam_" + layer + "_small")
        done = rep_out[0]
        for nm in nms:
            rep[nm] = []
        for buf in rep_out:
            for nm, a in zip(nms, _unpack(buf, [w_.shape for w_ in ws])):
                rep[nm].append(a)
    rep["norm_g"] = [jnp.concatenate([a, b], axis=0) for a, b in zip(rep["norm_g0"], rep["norm_g1"])]
    even_landed = send_wait(even_started, done, True, False, "even_grads_wait")
    finish(["w_in_e", "w_out_e", "rw_w2", "rw_a2", "sg_ln_g", "sg_ln_b"], even_own, even_landed, "even")

    order = ["norm_g", "w_in_e", "shift_mu", "rw_w0", "rw_w2", "rw_a0", "rw_a2", "rw_kk", "rw_ka", "rw_rk",
             "rw_lnx_g", "rw_lnx_b", "att_bias", "w_out_e", "w_in_o", "sg_ln_g", "sg_ln_b", "sg_w", "sg_b",
             "w_out_o", "final_g"]
    results = {**sharded, **rep}
    outs = [rep["loss"][0].reshape(()), dx.reshape(NSEQ, SEQ, D)]
    for kind in range(4):
        outs += [results[nm][kind] for nm in order]
    return tuple(outs)


def _local_step(x2, tgt, wie_t, late_weights, w2, a2, sglg, sglb, norm_g, shift_mu, rw_w0, rw_a0, rw_kk, rw_ka, rw_rk,
                rw_lnx_g, rw_lnx_b, att_bias, sg_w, sg_b, final_g, first_after=None, on_odd_grads=None,
                on_even_grads=None, on_small_grads=None):
    zl = jnp.zeros((LORA, W), F32)
    w2x = jnp.concatenate([w2, zl], axis=0)
    a2x = jnp.concatenate([zl, a2], axis=0)
    rk = rw_rk.reshape(1, W)
    pos = np.arange(SGC)
    sg_mask = jnp.asarray(((pos[None, :] // L) <= (pos[:, None] // L)).astype(np.float32))
    wm = (sg_w[0] * sg_mask[None]).astype(BF16)
    sgb_t = sg_b[0].T

    xn0, ps, ga, q, kb, vb, gb = ln_in_proj(x2, norm_g[0:1], wie_t, EVEN_SPLITS, "in_proj_even", after=first_after,
                                            w_t=True)
    r, lw, k2, v, aa, bb = even_prep(ps, shift_mu, rw_w0, w2x, rw_a0, a2x, rw_kk, rw_ka)
    y, rw_saved = rwkv_fwd(r, lw, k2, v, aa, bb)
    bias = window_bias(bias_expand(att_bias[0]).reshape(NPAIR, 2 * L, BAND))

    def padded(a):
        return jnp.pad(a.astype(BF16).reshape(NSEQ, SEQ, W), ((0, 0), (LEFT * L, 0), (0, 0))).reshape(NSEQ * PADSEQ, W)

    kpad, vpad = padded(kb), padded(vb)
    o = attention_fwd(q, kpad, vpad, bias)
    woe, wio, woo = late_weights(o)
    h1, zt = even_post(y, r, k2, v, ga, o, gb, rw_lnx_g, rw_lnx_b, rk, x2, woe)
    xn1, u, vv, gt = ln_in_proj(h1, norm_g[1:2], wio, ODD_SPLITS, "in_proj_odd")
    dh2, loss_part, d_final_g, z2t = gmlp_fwd_loss(u, vv, gt, sglg, sglb, wm, sgb_t, h1, woo, final_g[None], tgt)

    du, dvv, dgt, d_sglg, d_sglb, d_wm, d_sgb_t, d_woo = gmlp_bwd(u, vv, gt, sglg, sglb, wm, sgb_t, dh2, z2t, woo)
    dp_odd = [du, dvv, dgt]
    d_wio = matmul_acc_chips(xn1, dp_odd, "in_proj_odd_dw")
    token = on_odd_grads(d_woo, d_wio) if on_odd_grads else None
    dh1, d_g1 = in_proj_bwd_x(h1, norm_g[1:2], wio, dp_odd, dh2, "in_proj_odd_bwd", after=token)
    odd_small = [d_wm * sg_mask[None], d_sgb_t.T, d_final_g, d_g1]
    token = on_small_grads("odd", odd_small) if on_small_grads else None
    dy, dr2, dk22, dv2, dga, do, dgb, d_lng, d_lnb, d_rk, d_woe = even_post_bwd(
        y, r, k2, v, ga, o, gb, rw_lnx_g, rw_lnx_b, rk, dh1, zt, woe, after=token)
    dq, dkb, dvb, dbias = attention_bwd(q, kpad, vpad, bias, do)
    dbias = sum(dbias[:, i * 2 * L:(i + 1) * 2 * L, i * L:i * L + BAND] for i in range(ATT_Q))
    d_att_bias = bias_grad(dbias.reshape(NH, L, BAND))
    dr, dlw, dk2, dv, daa, dbb = rwkv_bwd(r, lw, k2, aa, bb, rw_saved, dy)
    dps, d_mu, d_w0, d_w2x, d_a0, d_a2x, d_kk, d_ka = even_prep_bwd(
        ps, shift_mu, rw_w0, w2x, rw_a0, a2x, rw_kk, rw_ka, dr, dlw, dk2, dv, daa, dbb, dr2, dk22, dv2)
    dp_even = [dps, dga, dq, dkb, dvb, dgb]
    d_wie = matmul_acc_chips(xn0, dp_even, "in_proj_even_dw")
    big_g = (d_wie, d_woe, d_wio, d_woo, d_w2x[:LORA], d_a2x[LORA:], d_sglg, d_sglb)
    token = on_even_grads(big_g) if on_even_grads else None
    dx, d_g0 = in_proj_bwd_x(x2, norm_g[0:1], wie_t, dp_even, dh1, "in_proj_even_bwd", after=token, w_t=True)
    even_small = [d_g0, d_mu, d_w0, d_a0, d_kk, d_ka, d_rk, d_lng, d_lnb, d_att_bias]
    if on_small_grads:
        on_small_grads("even", even_small + [loss_part[0:1, 0:1]])
    rep_g = [jnp.concatenate([d_g0, d_g1], axis=0)] + even_small[1:] + odd_small[:3]
    return loss_part[0, 0], dx, big_g, rep_g
```

```python
import functools
import math

import jax
import jax.numpy as jnp
import numpy as np
from jax import lax
from jax.experimental import pallas as pl
from jax.experimental.pallas import tpu as pltpu

F32 = jnp.float32
BF16 = jnp.bfloat16
HI = lax.Precision.HIGHEST

D = 1024
SEQ = 2048
NSEQ = 2
T = NSEQ * SEQ
HD = 64
NH = 8
W = 512
SHIFT = 1664
LORA = 64
EVEN_IN = 4224
ODD_IN = 3072
L = 64
NC = SEQ // L
LEFT = 8
BAND = (LEFT + 1) * L
CLIP = 128
SGC = 128
NG = 8
RMS_EPS = 1e-6
LN_EPS = 1e-5
GN_EPS = 64e-5
NEG = -1e30
VMEM_BIG = 56 * 1024 * 1024

ADAM_LR = 0.001
ADAM_B1 = 0.9
ADAM_B2 = 0.999
ADAM_EPS = 1e-08
ADAM_WD = 0.01
ADAM_STEP = 10

MESH = pl.DeviceIdType.MESH


def _bdot(a, b):
    return jnp.dot(a.astype(BF16), b.astype(BF16), preferred_element_type=F32)


def _bdot_nt(a, b):
    return lax.dot_general(a.astype(BF16), b.astype(BF16), (((1,), (1,)), ((), ())), preferred_element_type=F32)


def _bdot_tn(a, b):
    return lax.dot_general(a.astype(BF16), b.astype(BF16), (((0,), (0,)), ((), ())), preferred_element_type=F32)


def _hdot(a, b):
    return jnp.dot(a, b, precision=HI, preferred_element_type=F32)


def _hdot_nt(a, b):
    return lax.dot_general(a, b, (((1,), (1,)), ((), ())), precision=HI, preferred_element_type=F32)


def _hdot_tn(a, b):
    return lax.dot_general(a, b, (((0,), (0,)), ((), ())), precision=HI, preferred_element_type=F32)


def _iota2(shape, dim):
    return lax.broadcasted_iota(jnp.int32, shape, dim)


def _head_blockdiag():
    r = _iota2((2 * HD, 2 * HD), 0) // HD
    c = _iota2((2 * HD, 2 * HD), 1) // HD
    return (r == c).astype(BF16)


def _headsum_impl(x, bd):
    hi = x.astype(BF16)
    mid = (x - hi.astype(F32)).astype(BF16)
    n = bd.shape[0]
    out = [jnp.dot(hi[:, i:i + n], bd, preferred_element_type=F32) + jnp.dot(mid[:, i:i + n], bd, preferred_element_type=F32)
           for i in range(0, x.shape[1], n)]
    return jnp.concatenate(out, axis=-1)


@jax.custom_vjp
def _headsum(x, bd):
    return _headsum_impl(x, bd)


def _headsum_fwd(x, bd):
    return _headsum_impl(x, bd), bd


def _headsum_bwd(bd, ct):
    return _headsum_impl(ct, bd), None


_headsum.defvjp(_headsum_fwd, _headsum_bwd)


def _silu(x):
    return x * jax.nn.sigmoid(x)


def _dsilu(x):
    s = jax.nn.sigmoid(x)
    return s * (1.0 + x * (1.0 - s))


_GELU_C = math.sqrt(2.0 / math.pi)


def _gelu(x):
    return 0.5 * x * (1.0 + jnp.tanh(_GELU_C * (x + 0.044715 * (x * x * x))))


def _dgelu(x):
    t = jnp.tanh(_GELU_C * (x + 0.044715 * (x * x * x)))
    return 0.5 * (1.0 + t) + 0.5 * x * (1.0 - t * t) * _GELU_C * (1.0 + 3.0 * 0.044715 * x * x)


def _silu_both(x):
    s = jax.nn.sigmoid(x)
    xs = x * s
    return xs, s + xs * (1.0 - s)


def _gelu_both(x):
    x2 = x * x
    t = jnp.tanh(_GELU_C * (x + 0.044715 * (x2 * x)))
    half = 0.5 * (1.0 + t)
    return x * half, half + 0.5 * x * (1.0 - t * t) * _GELU_C * (1.0 + 3.0 * 0.044715 * x2)


def _softplus(x):
    return jnp.maximum(x, 0.0) + jnp.log(1.0 + jnp.exp(-jnp.abs(x)))


def _cparams(sem, vmem=None):
    return pltpu.CompilerParams(dimension_semantics=sem, vmem_limit_bytes=vmem)


def _row_spec(tm, width):
    return pl.BlockSpec((tm, width), lambda i: (i, 0))


def _col_spec(height, tm):
    return pl.BlockSpec((height, tm), lambda i: (0, i))


def _const_spec(shape):
    nd = len(shape)
    return pl.BlockSpec(shape, lambda *_: (0,) * nd)


def _weight_dims(w_bf, w_t):
    return (((1,), (1,)), ((), ())) if w_t else (((1,), (0,)), ((), ())), w_bf.shape[0 if w_t else 1]


def ln_in_proj(x, g, w_bf, splits, name, after=None, w_t=False):
    dims, n = _weight_dims(w_bf, w_t)
    tm = 512 if n <= ODD_IN else 256
    spans = []
    o = 0
    for s in splits:
        spans.append((o, o + s))
        o += s
    assert o == n
    extra_specs, extra = _after_operand(after)

    def body(x_ref, g_ref, w_ref, *rest):
        xn_ref, outs = rest[len(extra)], rest[len(extra) + 1:]
        xv = x_ref[...]
        rstd = lax.rsqrt(jnp.mean(xv * xv, axis=-1, keepdims=True) + RMS_EPS)
        xn = (xv * rstd * g_ref[...]).astype(BF16)
        xn_ref[...] = xn.T
        p = lax.dot_general(xn, w_ref[...], dims, preferred_element_type=F32)
        for o_ref, (a, b) in zip(outs, spans):
            o_ref[...] = p[:, a:b]

    return pl.pallas_call(
        body, grid=(T // tm,), name=name,
        in_specs=[_row_spec(tm, D), _const_spec((1, D)), _const_spec(w_bf.shape)] + extra_specs,
        out_specs=[_col_spec(D, tm)] + [_row_spec(tm, s) for s in splits],
        out_shape=[jax.ShapeDtypeStruct((D, T), BF16)] + [jax.ShapeDtypeStruct((T, s), F32) for s in splits],
        compiler_params=_cparams(("parallel",), VMEM_BIG),
    )(x, g, w_bf, *extra)


def in_proj_bwd_x(x, g, w_bf, dps, dres, name, after=None, w_t=False):
    tm = 512
    back = (((1,), (0,)), ((), ())) if w_t else (((1,), (1,)), ((), ()))
    widths = [d.shape[1] for d in dps]
    extra_specs, extra = _after_operand(after)

    def body(x_ref, g_ref, w_ref, dres_ref, *rest):
        dp_refs = rest[:len(widths)]
        dx_ref, dg_ref = rest[-2:]
        dp = jnp.concatenate([r[...] for r in dp_refs], axis=-1)
        dxn = lax.dot_general(dp, w_ref[...], back, preferred_element_type=F32)
        xv = x_ref[...]
        rstd = lax.rsqrt(jnp.mean(xv * xv, axis=-1, keepdims=True) + RMS_EPS)
        xhat = xv * rstd
        dgp = jnp.sum(dxn * xhat, axis=0, keepdims=True)

        @pl.when(pl.program_id(0) == 0)
        def _():
            dg_ref[...] = jnp.zeros_like(dg_ref)

        dg_ref[...] += dgp
        dxh = dxn * g_ref[...]
        dx_ref[...] = dres_ref[...] + rstd * (dxh - xhat * jnp.mean(dxh * xhat, axis=-1, keepdims=True))

    return pl.pallas_call(
        body, grid=(T // tm,), name=name,
        in_specs=[_row_spec(tm, D), _const_spec((1, D)), _const_spec(w_bf.shape), _row_spec(tm, D)]
        + [_row_spec(tm, s) for s in widths] + extra_specs,
        out_specs=[_row_spec(tm, D), _const_spec((1, D))],
        out_shape=[jax.ShapeDtypeStruct((T, D), F32), jax.ShapeDtypeStruct((1, D), F32)],
        compiler_params=_cparams(("arbitrary",), VMEM_BIG),
    )(x, g, w_bf, dres, *dps, *extra)


def _after_operand(after):
    return ([ANY], [after]) if after is not None else ([], [])


def matmul_acc_chips(at_bf, pieces, name, after=None):
    k = at_bf.shape[0]
    widths = [p.shape[1] for p in pieces]
    nb = sum(widths) // NCHIP
    tm = 512
    steps = T // tm
    extra_specs, extra = _after_operand(after)

    def body(a_ref, *rest):
        o_ref, acc = rest[-2:]

        @pl.when(pl.program_id(0) == 0)
        def _():
            acc[...] = jnp.zeros_like(acc)

        a = a_ref[...]
        b = jnp.concatenate([r[...] for r in rest[:len(widths)]], axis=-1)
        for s in range(NCHIP):
            acc[s] += jnp.dot(a, b[:, s * nb:(s + 1) * nb], preferred_element_type=F32)

        @pl.when(pl.program_id(0) == steps - 1)
        def _():
            o_ref[...] = acc[...].astype(BF16)

    return pl.pallas_call(
        body, grid=(steps,), name=name,
        in_specs=[_col_spec(k, tm)] + [_row_spec(tm, w_) for w_ in widths] + extra_specs,
        out_specs=_const_spec((NCHIP, k, nb)),
        out_shape=jax.ShapeDtypeStruct((NCHIP, k, nb), BF16),
        scratch_shapes=[pltpu.VMEM((NCHIP, k, nb), F32)],
        compiler_params=_cparams(("arbitrary",), VMEM_BIG),
    )(at_bf, *pieces, *extra)


def _out_proj_back(dh_ref, zt_ref, w_ref, dw_ref):
    dhb = dh_ref[...].astype(BF16)

    @pl.when(pl.program_id(0) == 0)
    def _():
        dw_ref[...] = jnp.zeros_like(dw_ref)

    dw_ref[...] += jnp.dot(zt_ref[...], dhb, preferred_element_type=F32)
    return lax.dot_general(dhb, w_ref[...], (((1,), (1,)), ((), ())), preferred_element_type=F32)


PREP_TM = 512
PREP_NB = SEQ // PREP_TM


def _prep_elem(k, wl, apre, kkw, kaw, bd):
    wraw = -_softplus(-wl) - 0.5
    lw = -jnp.exp(wraw)
    asig = jax.nn.sigmoid(apre)
    kkr = k * kkw
    nrm = jnp.maximum(jnp.sqrt(_headsum(kkr * kkr, bd)), 1e-12)
    kk = kkr / nrm
    k2 = k * (1.0 + (asig - 1.0) * kaw)
    return lw, k2, -kk, kk * asig


def _prep_elem_bwd(k, wl, apre, kkw, kaw, bd, dlw, dk2, daa, dbb):
    s = -wl
    sp = _softplus(s)
    dwl = dlw * (-jnp.exp(-sp - 0.5)) * jnp.exp(s - sp)
    asig = jax.nn.sigmoid(apre)
    kkr = k * kkw
    root = jnp.sqrt(_headsum(kkr * kkr, bd))
    inv = 1.0 / jnp.maximum(root, 1e-12)
    kk = kkr * inv
    dkk = dbb * asig - daa
    dap = (dbb * kk + dk2 * k * kaw) * asig * (1.0 - asig)
    through_norm = jnp.where(root > 1e-12, kk * _headsum(dkk * kkr, bd) * inv, 0.0)
    dkkr = inv * (dkk - through_norm)
    gain = 1.0 + (asig - 1.0) * kaw
    dk = dkkr * kkw + dk2 * gain
    dkkw = jnp.sum(dkkr * k, axis=0, keepdims=True)
    dkaw = jnp.sum(dk2 * k * (asig - 1.0), axis=0, keepdims=True)
    return dk, dwl, dap, dkkw, dkaw


def _shifted(ps_ref, prev_ref, mu, blk):
    p = ps_ref[...]
    first = (blk % PREP_NB) == 0
    prev_row = jnp.where(first, 0.0, prev_ref[7:8, :])
    rolled = pltpu.roll(p, 1, 0)
    p_prev = jnp.where(_iota2(p.shape, 0) == 0, prev_row, rolled)
    return p, p_prev, p + (p_prev - p) * mu


def _prev_spec(width, blk_of):
    return pl.BlockSpec((8, width), lambda i: (jnp.maximum(blk_of(i) * (PREP_TM // 8) - 1, 0), 0))


def even_prep(ps, mu, w0, w2x, a0, a2x, kkw, kaw):
    tm = PREP_TM

    def body(ps_ref, prev_ref, mu_ref, w0_ref, w2_ref, a0_ref, a2_ref, kk_ref, ka_ref,
             r_ref, lw_ref, k2_ref, v_ref, aa_ref, bb_ref):
        _, _, s = _shifted(ps_ref, prev_ref, mu_ref[...], pl.program_id(0))
        wa = s[:, 3 * W:]
        wl = w0_ref[...] + _bdot(jnp.tanh(wa), w2_ref[...])
        apre = a0_ref[...] + _bdot(wa, a2_ref[...])
        lw, k2, aa, bb = _prep_elem(s[:, W:2 * W], wl, apre, kk_ref[...], ka_ref[...], _head_blockdiag())
        r_ref[...] = s[:, 0:W]
        v_ref[...] = s[:, 2 * W:3 * W]
        lw_ref[...] = lw
        k2_ref[...] = k2
        aa_ref[...] = aa
        bb_ref[...] = bb

    vec = _const_spec((1, W))
    return pl.pallas_call(
        body, grid=(T // tm,), name="even_prep",
        in_specs=[_row_spec(tm, SHIFT), _prev_spec(SHIFT, lambda i: i), _const_spec((1, SHIFT)), vec,
                  _const_spec((2 * LORA, W)), vec, _const_spec((2 * LORA, W)), vec, vec],
        out_specs=[_row_spec(tm, W)] * 6,
        out_shape=[jax.ShapeDtypeStruct((T, W), F32)] * 6,
        compiler_params=_cparams(("parallel",), VMEM_BIG),
    )(ps, ps, mu, w0, w2x, a0, a2x, kkw, kaw)


def even_prep_bwd(ps, mu, w0, w2x, a0, a2x, kkw, kaw, dr, dlw, dk2, dv, daa, dbb, dr2, dk22, dv2):
    tm = PREP_TM
    nb = T // tm
    rev = lambda i: nb - 1 - i

    def body(ps_ref, prev_ref, mu_ref, w0_ref, w2_ref, a0_ref, a2_ref, kk_ref, ka_ref,
             dr_ref, dlw_ref, dk2_ref, dv_ref, daa_ref, dbb_ref, dr2_ref, dk22_ref, dv2_ref,
             dps_ref, dmu_ref, dw0_ref, dw2_ref, da0_ref, da2_ref, dkk_ref, dka_ref, carry):
        i = pl.program_id(0)
        blk = rev(i)
        mu_v = mu_ref[...]
        p, p_prev, s = _shifted(ps_ref, prev_ref, mu_v, blk)
        wa = s[:, 3 * W:]
        th = jnp.tanh(wa)
        wl = w0_ref[...] + _bdot(th, w2_ref[...])
        apre = a0_ref[...] + _bdot(wa, a2_ref[...])
        bd = _head_blockdiag()
        k = s[:, W:2 * W]
        dk, dwl, dap, dkkw, dkaw = _prep_elem_bwd(k, wl, apre, kk_ref[...], ka_ref[...], bd, dlw_ref[...],
                                                  dk2_ref[...] + dk22_ref[...], daa_ref[...], dbb_ref[...])
        dwa = _bdot_nt(dwl, w2_ref[...]) * (1.0 - th * th) + _bdot_nt(dap, a2_ref[...])
        ds = jnp.concatenate([dr_ref[...] + dr2_ref[...], dk, dv_ref[...] + dv2_ref[...], dwa], axis=-1)

        @pl.when(i == 0)
        def _():
            for ref in (dmu_ref, dw0_ref, dw2_ref, da0_ref, da2_ref, dkk_ref, dka_ref, carry):
                ref[...] = jnp.zeros_like(ref)

        dmu_ref[...] += jnp.sum(ds * (p_prev - p), axis=0, keepdims=True)
        dw0_ref[...] += jnp.sum(dwl, axis=0, keepdims=True)
        da0_ref[...] += jnp.sum(dap, axis=0, keepdims=True)
        dw2_ref[...] += _bdot_tn(th, dwl)
        da2_ref[...] += _bdot_tn(wa, dap)
        dkk_ref[...] += dkkw
        dka_ref[...] += dkaw
        dsm = ds * mu_v
        last = (blk % PREP_NB) == PREP_NB - 1
        nxt = jnp.where(last, 0.0, carry[0:1, :])
        up = pltpu.roll(dsm, tm - 1, 0)
        up = jnp.where(_iota2(up.shape, 0) == tm - 1, nxt, up)
        dps_ref[...] = (ds - dsm + up).astype(BF16)
        carry[0:1, :] = dsm[0:1, :]

    vec = _const_spec((1, W))
    rrow = lambda width: pl.BlockSpec((tm, width), lambda i: (rev(i), 0))
    return pl.pallas_call(
        body, grid=(nb,), name="even_prep_bwd",
        in_specs=[rrow(SHIFT), _prev_spec(SHIFT, rev), _const_spec((1, SHIFT)), vec,
                  _const_spec((2 * LORA, W)), vec, _const_spec((2 * LORA, W)), vec, vec] + [rrow(W)] * 9,
        out_specs=[rrow(SHIFT), _const_spec((1, SHIFT)), vec, _const_spec((2 * LORA, W)), vec,
                   _const_spec((2 * LORA, W)), vec, vec],
        out_shape=[jax.ShapeDtypeStruct((T, SHIFT), BF16), jax.ShapeDtypeStruct((1, SHIFT), F32),
                   jax.ShapeDtypeStruct((1, W), F32), jax.ShapeDtypeStruct((2 * LORA, W), F32),
                   jax.ShapeDtypeStruct((1, W), F32), jax.ShapeDtypeStruct((2 * LORA, W), F32),
                   jax.ShapeDtypeStruct((1, W), F32), jax.ShapeDtypeStruct((1, W), F32)],
        scratch_shapes=[pltpu.VMEM((8, SHIFT), F32)],
        compiler_params=_cparams(("arbitrary",), VMEM_BIG),
    )(ps, ps, mu, w0, w2x, a0, a2x, kkw, kaw, dr, dlw, dk2, dv, daa, dbb, dr2, dk22, dv2)


NPAIR = NH // 2
PW = 2 * HD


def _pair_cols(p):
    return slice(p * PW, (p + 1) * PW)


def _pairs(a):
    return [a[:, _pair_cols(p)] for p in range(NPAIR)]


def _stack_pair(a):
    first = _iota2(a.shape, 1) < HD
    zero = jnp.zeros_like(a)
    return jnp.concatenate([jnp.where(first, a, zero), jnp.where(first, zero, a)], axis=0)


def _unstack_pair(a):
    n = a.shape[0] // 2
    return jnp.where(_iota2((n, PW), 1) < HD, a[:n], a[n:])


def _fold_pair(a):
    n = a.shape[0] // 2
    return a[:n] + a[n:]


def _chunk_masks():
    n = 4 * L
    row = _iota2((n, n), 0)
    col = _iota2((n, n), 1)
    same = ((row // L) & 1) == ((col // L) & 1)
    ri = row & (L - 1)
    ci = col & (L - 1)
    keep = same & (((row < 2 * L) & (ri > ci)) | ((row >= 2 * L) & (ri >= ci)))
    r1 = _iota2((L, L), 0)
    c1 = _iota2((L, L), 1)
    r2 = _iota2((2 * L, 2 * L), 0)
    c2 = _iota2((2 * L, 2 * L), 1)
    return keep.astype(F32), (r1 >= c1).astype(F32), (r2 == c2).astype(F32)


def _scaled(r, lw, k2, aa, bb, tri):
    g = _hdot(tri, lw)
    eg = jnp.exp(g)
    eng = jnp.exp(-g)
    egp = jnp.exp(g - lw)
    return eg, eng, egp, aa * egp, r * eg, bb * eng, k2 * eng


def _head_cols(h):
    return slice(h * HD, (h + 1) * HD)


def _per_head(a):
    return [a[:, _head_cols(h)] for h in range(NH)]


def _pairs_operands(at, rt, bt, kt):
    x = [jnp.concatenate([_stack_pair(a), _stack_pair(r)], axis=0).astype(BF16) for a, r in zip(_pairs(at), _pairs(rt))]
    yk = [jnp.concatenate([_stack_pair(b), _stack_pair(k)], axis=0).astype(BF16) for b, k in zip(_pairs(bt), _pairs(kt))]
    return x, yk


def _pairs_matrices(x, yk, keep, eye):
    m = [_bdot_nt(a, b) * keep for a, b in zip(x, yk)]
    p = [a[:2 * L, :2 * L] for a in m]
    tinv = [eye + a for a in p]
    for _ in range(5):
        p = [_bdot(a, a) for a in p]
        tinv = [t + _bdot(t, a) for t, a in zip(tinv, p)]
    return [a.astype(BF16) for a in m], [a.astype(BF16) for a in tinv]


def _pairs_fwd(x, yk, m, tinv, vw, s0, egl):
    xh = [_bdot_nt(a, s) for a, s in zip(x, s0)]
    u = [_bdot(t, h[:2 * L] + _bdot(a[:2 * L, 2 * L:], w)) for t, h, a, w in zip(tinv, xh, m, vw)]
    uv = [jnp.concatenate([a, w], axis=0).astype(BF16) for a, w in zip(u, vw)]
    y = [h[2 * L:] + _bdot(a[2 * L:], w) for h, a, w in zip(xh, m, uv)]
    sn = [e * (s + _bdot_tn(w, b)) for e, s, w, b in zip(egl, s0, uv, yk)]
    return y, sn, uv


def _pairs_bwd(x, yk, m, tinv, uv, s0, sn, egl, dyw, dsn, keep):
    dzs = [d * e for d, e in zip(dsn, egl)]
    dgl = [jnp.sum(d * s, axis=0, keepdims=True) for d, s in zip(dsn, sn)]
    dyb = [a.astype(BF16) for a in dyw]
    t1 = [_bdot_tn(a[2 * L:], d) for a, d in zip(m, dyb)]
    t2 = [_bdot_nt(b, d) for b, d in zip(yk, dzs)]
    drhs = [_bdot_tn(t, a[:2 * L] + b[:2 * L]) for t, a, b in zip(tinv, t1, t2)]
    dv = [a[2 * L:] + b[2 * L:] + _bdot_tn(c[:2 * L, 2 * L:], d) for a, b, c, d in zip(t1, t2, m, drhs)]
    gg = [jnp.concatenate([a, b], axis=0).astype(BF16) for a, b in zip(drhs, dyw)]
    ds0 = [d + _bdot_tn(g, a) for d, g, a in zip(dzs, gg, x)]
    dm = [_bdot_nt(g, w) * keep for g, w in zip(gg, uv)]
    dx = [_bdot(g, s) + _bdot(d, b) for g, s, d, b in zip(gg, s0, dm, yk)]
    dyk = [_bdot_tn(d, a) + _bdot(w, z) for d, a, w, z in zip(dm, x, uv, dzs)]
    return dx, dyk, dv, dgl, ds0


STATE_SHAPE = (NPAIR * PW, PW)
M_SHAPE = (4 * L, NPAIR * 4 * L)
TINV_SHAPE = (2 * L, NPAIR * 2 * L)


def _rows_of(a, n):
    return [a[i * n:(i + 1) * n, :] for i in range(NPAIR)]


def _both(f):
    out = []
    for s in range(NSEQ):
        out += f(s)
    return out


def _seq_view(a):
    return a.reshape(NSEQ, SEQ, a.shape[-1])


UV_SHAPE = (4 * L, NPAIR * PW)


def rwkv_fwd(r, lw, k2, v, aa, bb):
    def body(r_ref, lw_ref, k2_ref, v_ref, aa_ref, bb_ref, y_ref, hs_ref, hn_ref, m_ref, t_ref, uv_ref, state):
        @pl.when(pl.program_id(0) == 0)
        def _():
            state[...] = jnp.zeros_like(state)

        s_all = state[...]
        hs_ref[0] = s_all
        keep, tri, eye = _chunk_masks()
        sc = [_scaled(r_ref[s], lw_ref[s], k2_ref[s], aa_ref[s], bb_ref[s], tri) for s in range(NSEQ)]
        ops = [_pairs_operands(*sc[s][3:]) for s in range(NSEQ)]
        x, yk = _both(lambda s: ops[s][0]), _both(lambda s: ops[s][1])
        m, tinv = _pairs_matrices(x, yk, keep, eye)
        vw = _both(lambda s: [_stack_pair(a) for a in _pairs(v_ref[s])])
        s0 = _both(lambda s: _rows_of(s_all[s], PW))
        y, sn, uv = _pairs_fwd(x, yk, m, tinv, vw, s0, _both(lambda s: _pairs(sc[s][0][L - 1:L, :])))
        for s in range(NSEQ):
            mine = slice(s * NPAIR, (s + 1) * NPAIR)
            y_ref[s] = jnp.concatenate([_fold_pair(a) for a in y[mine]], axis=-1)
            m_ref[0, s] = jnp.concatenate(m[mine], axis=-1)
            t_ref[0, s] = jnp.concatenate(tinv[mine], axis=-1)
            uv_ref[0, s] = jnp.concatenate(uv[mine], axis=-1)
            s_new = jnp.concatenate(sn[mine], axis=0)
            hn_ref[0, s] = s_new
            state[s] = s_new

    blk = pl.BlockSpec((NSEQ, L, W), lambda c: (0, c, 0))
    per_chunk = lambda shape: pl.BlockSpec((1, NSEQ) + shape, lambda c: (c, 0, 0, 0))
    saved_shapes = [(STATE_SHAPE, F32), (STATE_SHAPE, F32), (M_SHAPE, BF16), (TINV_SHAPE, BF16), (UV_SHAPE, BF16)]
    y, *saved = pl.pallas_call(
        body, grid=(NC,), name="rwkv_fwd",
        in_specs=[blk] * 6,
        out_specs=[blk] + [per_chunk(shape) for shape, _ in saved_shapes],
        out_shape=[jax.ShapeDtypeStruct((NSEQ, SEQ, W), F32)]
        + [jax.ShapeDtypeStruct((NC, NSEQ) + shape, dt) for shape, dt in saved_shapes],
        scratch_shapes=[pltpu.VMEM((NSEQ,) + STATE_SHAPE, F32)],
        compiler_params=_cparams(("arbitrary",)),
    )(*[_seq_view(a) for a in (r, lw, k2, v, aa, bb)])
    return y.reshape(T, W), saved


def rwkv_bwd(r, lw, k2, aa, bb, saved, dy):
    def body(r_ref, lw_ref, k2_ref, aa_ref, bb_ref, hs_ref, hn_ref, m_ref, t_ref, uv_ref, dy_ref,
             dr_ref, dlw_ref, dk2_ref, dv_ref, daa_ref, dbb_ref, dstate):
        @pl.when(pl.program_id(0) == 0)
        def _():
            dstate[...] = jnp.zeros_like(dstate)

        keep, tri, _ = _chunk_masks()
        sc = [_scaled(r_ref[s], lw_ref[s], k2_ref[s], aa_ref[s], bb_ref[s], tri) for s in range(NSEQ)]
        ops = [_pairs_operands(*sc[s][3:]) for s in range(NSEQ)]
        x, yk = _both(lambda s: ops[s][0]), _both(lambda s: ops[s][1])
        m = _both(lambda s: [m_ref[0, s][:, i * 4 * L:(i + 1) * 4 * L] for i in range(NPAIR)])
        tinv = _both(lambda s: [t_ref[0, s][:, i * 2 * L:(i + 1) * 2 * L] for i in range(NPAIR)])
        uv = _both(lambda s: _pairs(uv_ref[0, s]))
        dyw = _both(lambda s: [_stack_pair(a) for a in _pairs(dy_ref[s])])
        s0 = _both(lambda s: _rows_of(hs_ref[0, s], PW))
        sn = _both(lambda s: _rows_of(hn_ref[0, s], PW))
        dsn = _both(lambda s: _rows_of(dstate[s], PW))
        egl = _both(lambda s: _pairs(sc[s][0][L - 1:L, :]))
        dx, dyk, dvw, dgl, ds0 = _pairs_bwd(x, yk, m, tinv, uv, s0, sn, egl, dyw, dsn, keep)
        for s in range(NSEQ):
            mine = slice(s * NPAIR, (s + 1) * NPAIR)
            eg, eng, egp, at, rt, bt, kt = sc[s]
            dstate[s] = jnp.concatenate(ds0[mine], axis=0)
            dv_ref[s] = jnp.concatenate([_fold_pair(a) for a in dvw[mine]], axis=-1)
            dat = jnp.concatenate([_fold_pair(a[:2 * L]) for a in dx[mine]], axis=-1)
            drt = jnp.concatenate([_fold_pair(a[2 * L:]) for a in dx[mine]], axis=-1)
            dbt = jnp.concatenate([_fold_pair(a[:2 * L]) for a in dyk[mine]], axis=-1)
            dkt = jnp.concatenate([_fold_pair(a[2 * L:]) for a in dyk[mine]], axis=-1)
            dg = drt * rt - dbt * bt - dkt * kt
            dg = dg + jnp.where(_iota2(dg.shape, 0) == L - 1, jnp.concatenate(dgl[mine], axis=-1), 0.0)
            dgp = dat * at
            dlw_ref[s] = _hdot_tn(tri, dg + dgp) - dgp
            dr_ref[s] = drt * eg
            daa_ref[s] = dat * egp
            dbb_ref[s] = dbt * eng
            dk2_ref[s] = dkt * eng

    blk = pl.BlockSpec((NSEQ, L, W), lambda c: (0, NC - 1 - c, 0))
    per_chunk = lambda shape: pl.BlockSpec((1, NSEQ) + shape, lambda c: (NC - 1 - c, 0, 0, 0))
    outs = pl.pallas_call(
        body, grid=(NC,), name="rwkv_bwd",
        in_specs=[blk] * 5 + [per_chunk(a.shape[2:]) for a in saved] + [blk],
        out_specs=[blk] * 6,
        out_shape=[jax.ShapeDtypeStruct((NSEQ, SEQ, W), F32)] * 6,
        scratch_shapes=[pltpu.VMEM((NSEQ,) + STATE_SHAPE, F32)],
        compiler_params=_cparams(("arbitrary",)),
    )(*[_seq_view(a) for a in (r, lw, k2, aa, bb)], *saved, _seq_view(dy))
    return [a.reshape(T, W) for a in outs]


def _post_math(y, r, k2, v, ga, o, gb, lng, lnb, rk, bd):
    mu = _headsum(y, bd) * (1.0 / HD)
    yc = y - mu
    var = _headsum(yc * yc, bd) * (1.0 / HD)
    yn = yc * lax.rsqrt(var + GN_EPS) * lng + lnb
    bonus = _headsum(r * k2 * rk, bd) * v
    return (yn + bonus) * _silu(ga), o * _silu(gb)


def even_post(y, r, k2, v, ga, o, gb, lng, lnb, rk, h, w_bf):
    tm = 512

    def body(y_ref, r_ref, k2_ref, v_ref, ga_ref, o_ref, gb_ref, lng_ref, lnb_ref, rk_ref, h_ref, w_ref,
             ho_ref, zt_ref):
        ya, yb = _post_math(y_ref[...], r_ref[...], k2_ref[...], v_ref[...], ga_ref[...], o_ref[...], gb_ref[...],
                            lng_ref[...], lnb_ref[...], rk_ref[...], _head_blockdiag())
        z = jnp.concatenate([ya.astype(BF16), yb.astype(BF16)], axis=-1)
        zt_ref[...] = z.T
        ho_ref[...] = h_ref[...] + jnp.dot(z, w_ref[...], preferred_element_type=F32)

    vec = _const_spec((1, W))
    return pl.pallas_call(
        body, grid=(T // tm,), name="even_post",
        in_specs=[_row_spec(tm, W)] * 7 + [vec] * 3 + [_row_spec(tm, D), _const_spec((D, D))],
        out_specs=[_row_spec(tm, D), _col_spec(D, tm)],
        out_shape=[jax.ShapeDtypeStruct((T, D), F32), jax.ShapeDtypeStruct((D, T), BF16)],
        compiler_params=_cparams(("parallel",), VMEM_BIG),
    )(y, r, k2, v, ga, o, gb, lng, lnb, rk, h, w_bf)


def even_post_bwd(y, r, k2, v, ga, o, gb, lng, lnb, rk, dh, zt_bf, w_bf, after=None):
    tm = 512
    extra_specs, extra = _after_operand(after)

    def body(y_ref, r_ref, k2_ref, v_ref, ga_ref, o_ref, gb_ref, lng_ref, lnb_ref, rk_ref, dh_ref, zt_ref, w_ref,
             *rest):
        dy_ref, dr_ref, dk2_ref, dv_ref, dga_ref, do_ref, dgb_ref, dlng_ref, dlnb_ref, drk_ref, dw_ref = rest[-11:]
        dzv = _out_proj_back(dh_ref, zt_ref, w_ref, dw_ref)
        bd = _head_blockdiag()
        _, vjp = jax.vjp(lambda *a: _post_math(*a, bd), y_ref[...], r_ref[...], k2_ref[...], v_ref[...], ga_ref[...],
                         o_ref[...], gb_ref[...], lng_ref[...], lnb_ref[...], rk_ref[...])
        dy, dr, dk2, dv, dga, do, dgb, dlng, dlnb, drk = vjp((dzv[:, 0:W], dzv[:, W:2 * W]))
        for ref, val in ((dy_ref, dy), (dr_ref, dr), (dk2_ref, dk2), (dv_ref, dv), (dga_ref, dga), (do_ref, do),
                         (dgb_ref, dgb)):
            ref[...] = val.astype(ref.dtype)

        @pl.when(pl.program_id(0) == 0)
        def _():
            for ref in (dlng_ref, dlnb_ref, drk_ref):
                ref[...] = jnp.zeros_like(ref)

        dlng_ref[...] += dlng
        dlnb_ref[...] += dlnb
        drk_ref[...] += drk

    vec = _const_spec((1, W))
    return pl.pallas_call(
        body, grid=(T // tm,), name="even_post_bwd",
        in_specs=[_row_spec(tm, W)] * 7 + [vec] * 3 + [_row_spec(tm, D), _col_spec(D, tm), _const_spec((D, D))]
        + extra_specs,
        out_specs=[_row_spec(tm, W)] * 7 + [vec] * 3 + [_const_spec((D, D))],
        out_shape=[jax.ShapeDtypeStruct((T, W), dt) for dt in (F32, F32, F32, F32, BF16, F32, BF16)]
        + [jax.ShapeDtypeStruct((1, W), F32)] * 3 + [jax.ShapeDtypeStruct((D, D), F32)],
        compiler_params=_cparams(("arbitrary",), VMEM_BIG),
    )(y, r, k2, v, ga, o, gb, lng, lnb, rk, dh, zt_bf, w_bf, *extra)


PADSEQ = SEQ + LEFT * L
ATT_SCALE = 1.0 / math.sqrt(HD)
ATT_Q = 4
WIN = BAND + (ATT_Q - 1) * L
ATT_STEPS = NC // ATT_Q
ATT_BIAS_SHAPE = (NPAIR, ATT_Q * 2 * L, WIN)


def _stack_chunks(a):
    return jnp.concatenate([_stack_pair(a[i * L:(i + 1) * L]) for i in range(ATT_Q)], axis=0)


def _unstack_chunks(a):
    return jnp.concatenate([_unstack_pair(a[i * 2 * L:(i + 1) * 2 * L]) for i in range(ATT_Q)], axis=0)


def window_bias(bias):
    parts = [jnp.pad(bias, ((0, 0), (0, 0), (i * L, (ATT_Q - 1 - i) * L)), constant_values=NEG) for i in range(ATT_Q)]
    return jnp.concatenate(parts, axis=1)


def _att_probs(q2, kw, bias, step):
    valid = _iota2((1, WIN), 1) >= (LEFT - step * ATT_Q) * L
    s = [jnp.where(valid, _bdot_nt(a, b) * ATT_SCALE + bias[p], NEG) for p, (a, b) in enumerate(zip(q2, kw))]
    e = [jnp.exp(a - jnp.max(a, axis=-1, keepdims=True)) for a in s]
    return [a / jnp.sum(a, axis=-1, keepdims=True) for a in e]


def attention_fwd(q, kpad, vpad, bias):
    def body(q_ref, k_ref, v_ref, b_ref, o_ref):
        step = pl.program_id(1)
        start = pl.multiple_of(step * (ATT_Q * L), L)
        kw = _pairs(k_ref[pl.ds(start, WIN), :])
        vw = _pairs(v_ref[pl.ds(start, WIN), :])
        q2 = [_stack_chunks(a) for a in _pairs(q_ref[...].astype(BF16))]
        p = _att_probs(q2, kw, b_ref[...], step)
        o_ref[...] = jnp.concatenate([_unstack_chunks(_bdot(a, b)) for a, b in zip(p, vw)], axis=-1)

    qblk = pl.BlockSpec((ATT_Q * L, W), lambda b, c: (b * ATT_STEPS + c, 0))
    kblk = pl.BlockSpec((PADSEQ, W), lambda b, c: (b, 0))
    return pl.pallas_call(
        body, grid=(NSEQ, ATT_STEPS), name="attention_fwd",
        in_specs=[qblk, kblk, kblk, _const_spec(ATT_BIAS_SHAPE)],
        out_specs=qblk, out_shape=jax.ShapeDtypeStruct((T, W), F32),
        compiler_params=_cparams(("parallel", "arbitrary")),
    )(q, kpad, vpad, bias)


def attention_bwd(q, kpad, vpad, bias, do):
    def body(q_ref, k_ref, v_ref, b_ref, do_ref, dq_ref, dko_ref, dvo_ref, db_ref, dk_ref, dv_ref):
        b = pl.program_id(0)
        c = pl.program_id(1)

        @pl.when(c == 0)
        def _():
            dk_ref[...] = jnp.zeros_like(dk_ref)
            dv_ref[...] = jnp.zeros_like(dv_ref)

        @pl.when((c == 0) & (b == 0))
        def _():
            db_ref[...] = jnp.zeros_like(db_ref)

        start = pl.multiple_of(c * (ATT_Q * L), L)
        kw = _pairs(k_ref[pl.ds(start, WIN), :])
        vw = _pairs(v_ref[pl.ds(start, WIN), :])
        q2 = [_stack_chunks(a) for a in _pairs(q_ref[...].astype(BF16))]
        do2 = [_stack_chunks(a) for a in _pairs(do_ref[...].astype(BF16))]
        p = _att_probs(q2, kw, b_ref[...], c)
        dp = [_bdot_nt(a, b) for a, b in zip(do2, vw)]
        ds = [a * (d - jnp.sum(d * a, axis=-1, keepdims=True)) for a, d in zip(p, dp)]
        dss = [(a * ATT_SCALE).astype(BF16) for a in ds]
        dq_ref[...] = jnp.concatenate([_unstack_chunks(_bdot(a, b)) for a, b in zip(dss, kw)], axis=-1).astype(BF16)
        dk_ref[pl.ds(start, WIN), :] += jnp.concatenate([_bdot_tn(a, b) for a, b in zip(dss, q2)], axis=-1)
        dv_ref[pl.ds(start, WIN), :] += jnp.concatenate([_bdot_tn(a, b) for a, b in zip(p, do2)], axis=-1)
        for i in range(NPAIR):
            db_ref[i] += ds[i]

        @pl.when(c == ATT_STEPS - 1)
        def _():
            dko_ref[...] = dk_ref[LEFT * L:, :].astype(BF16)
            dvo_ref[...] = dv_ref[LEFT * L:, :].astype(BF16)

    qblk = pl.BlockSpec((ATT_Q * L, W), lambda b, c: (b * ATT_STEPS + c, 0))
    kblk = pl.BlockSpec((PADSEQ, W), lambda b, c: (b, 0))
    sblk = pl.BlockSpec((SEQ, W), lambda b, c: (b, 0))
    bblk = _const_spec(ATT_BIAS_SHAPE)
    return pl.pallas_call(
        body, grid=(NSEQ, ATT_STEPS), name="attention_bwd",
        in_specs=[qblk, kblk, kblk, bblk, qblk],
        out_specs=[qblk, sblk, sblk, bblk],
        out_shape=[jax.ShapeDtypeStruct((T, W), BF16), jax.ShapeDtypeStruct((T, W), BF16),
                   jax.ShapeDtypeStruct((T, W), BF16), jax.ShapeDtypeStruct(ATT_BIAS_SHAPE, F32)],
        scratch_shapes=[pltpu.VMEM((PADSEQ, W), F32), pltpu.VMEM((PADSEQ, W), F32)],
        compiler_params=_cparams(("arbitrary", "arbitrary"), VMEM_BIG),
    )(q, kpad, vpad, bias, do)


NTAB = 2 * CLIP + 1
EXT = BAND + L


def _ext_onehot():
    n = _iota2((EXT, NTAB), 0)
    m = _iota2((EXT, NTAB), 1)
    return (jnp.clip(BAND - 1 - n, -CLIP, CLIP) + CLIP == m).astype(F32)


def bias_expand(table):
    def body(t_ref, o_ref):
        ext = _hdot_nt(t_ref[...], _ext_onehot())
        for i in range(L):
            s = L - 1 - i
            o_ref[:, i, :] = (pltpu.roll(ext, EXT - s, 1) if s else ext)[:, :BAND]

    return pl.pallas_call(body, name="bias_expand", out_shape=jax.ShapeDtypeStruct((NH, L, BAND), F32))(table)


def bias_grad(dbias):
    def body(d_ref, o_ref):
        acc = jnp.zeros((NH, EXT), F32)
        zpad = jnp.zeros((NH, EXT - BAND), F32)
        for i in range(L):
            s = L - 1 - i
            row = jnp.concatenate([d_ref[:, i, :], zpad], axis=-1)
            acc = acc + (pltpu.roll(row, s, 1) if s else row)
        o_ref[...] = _hdot(acc, _ext_onehot())

    return pl.pallas_call(body, name="bias_grad", out_shape=jax.ShapeDtypeStruct((NH, NTAB), F32))(dbias)


def _group_cols(g):
    return slice(g * SGC, (g + 1) * SGC)


def _sg_norm(gv, lng, lnb):
    gc = gv - jnp.mean(gv, axis=-1, keepdims=True)
    rstd = lax.rsqrt(jnp.mean(gc * gc, axis=-1, keepdims=True) + LN_EPS)
    xhat = gc * rstd
    return xhat, rstd, xhat * lng + lnb


GMLP_BWD_CHUNKS = 2


def gmlp_fwd_loss(u, v, gate, lng, lnb, wm_bf, sgb_t, h, w_bf, g_final, target):
    tm = GMLP_BWD_CHUNKS * SGC

    def body(u_ref, v_ref, gt_ref, lng_ref, lnb_ref, wm_ref, sb_ref, h_ref, w_ref, g_ref, t_ref,
             dh_ref, loss_ref, dg_ref, zt_ref):
        zs = []
        for ch in range(GMLP_BWD_CHUNKS):
            rows = slice(ch * SGC, (ch + 1) * SGC)
            _, _, vln = _sg_norm(_gelu(v_ref[rows, :]), lng_ref[...], lnb_ref[...])
            vlb = vln.astype(BF16)
            zg = []
            for g in range(NG):
                cs = _group_cols(g)
                sv = jnp.dot(wm_ref[g], vlb[:, cs], preferred_element_type=F32) + sb_ref[:, g:g + 1]
                zg.append((_gelu(u_ref[rows, cs]) * sv * _silu(gt_ref[rows, cs])).astype(BF16))
            zs.append(jnp.concatenate(zg, axis=-1))
        z = jnp.concatenate(zs, axis=0)
        zt_ref[...] = z.T
        xv = h_ref[...] + jnp.dot(z, w_ref[...], preferred_element_type=F32)
        rstd = lax.rsqrt(jnp.mean(xv * xv, axis=-1, keepdims=True) + RMS_EPS)
        xhat = xv * rstd
        err = xhat * g_ref[...] - t_ref[...]
        part = 0.5 * jnp.sum(jnp.mean(err * err, axis=-1, keepdims=True), axis=0, keepdims=True)
        dout = err * (1.0 / D)

        @pl.when(pl.program_id(0) == 0)
        def _():
            loss_ref[...] = jnp.zeros_like(loss_ref)
            dg_ref[...] = jnp.zeros_like(dg_ref)

        loss_ref[...] += jnp.broadcast_to(part, loss_ref.shape)
        dg_ref[...] += jnp.sum(dout * xhat, axis=0, keepdims=True)
        dxh = dout * g_ref[...]
        dh_ref[...] = rstd * (dxh - xhat * jnp.mean(dxh * xhat, axis=-1, keepdims=True))

    return pl.pallas_call(
        body, grid=(T // tm,), name="gmlp_fwd_loss",
        in_specs=[_row_spec(tm, D)] * 3 + [_const_spec((1, D))] * 2
        + [_const_spec((NG, SGC, SGC)), _const_spec((SGC, NG)), _row_spec(tm, D), _const_spec((D, D)),
           _const_spec((1, D)), _row_spec(tm, D)],
        out_specs=[_row_spec(tm, D), _const_spec((8, 128)), _const_spec((1, D)), _col_spec(D, tm)],
        out_shape=[jax.ShapeDtypeStruct((T, D), F32), jax.ShapeDtypeStruct((8, 128), F32),
                   jax.ShapeDtypeStruct((1, D), F32), jax.ShapeDtypeStruct((D, T), BF16)],
        compiler_params=_cparams(("arbitrary",), VMEM_BIG),
    )(u, v, gate, lng, lnb, wm_bf, sgb_t, h, w_bf, g_final, target)


def gmlp_bwd(u, v, gate, lng, lnb, wm_bf, sgb_t, dh, zt_bf, w_bf):
    def body(u_ref, v_ref, gt_ref, lng_ref, lnb_ref, wm_ref, sb_ref, dh_ref, zt_ref, w_ref,
             du_ref, dv_ref, dgt_ref, dlng_ref, dlnb_ref, dwm_ref, dsb_ref, dw_ref):
        @pl.when(pl.program_id(0) == 0)
        def _():
            for ref in (dlng_ref, dlnb_ref, dwm_ref, dsb_ref):
                ref[...] = jnp.zeros_like(ref)

        dz = _out_proj_back(dh_ref, zt_ref, w_ref, dw_ref)
        sel = (_iota2((D, NG), 0) // SGC == _iota2((D, NG), 1)).astype(F32)
        for ch in range(GMLP_BWD_CHUNKS):
            rows = slice(ch * SGC, (ch + 1) * SGC)
            gv, dgv_dv = _gelu_both(v_ref[rows, :])
            xhat, rstd, vln = _sg_norm(gv, lng_ref[...], lnb_ref[...])
            vlb = vln.astype(BF16)
            dvln = []
            dsv_all = []
            for g in range(NG):
                cs = _group_cols(g)
                uu = u_ref[rows, cs]
                gg = gt_ref[rows, cs]
                dzz = dz[rows, cs]
                sv = jnp.dot(wm_ref[g], vlb[:, cs], preferred_element_type=F32) + sb_ref[:, g:g + 1]
                gu, dgu = _gelu_both(uu)
                sg, dsg = _silu_both(gg)
                dzgu = dzz * gu
                dsv = dzgu * sg
                dgt_ref[rows, cs] = (dzgu * sv * dsg).astype(BF16)
                du_ref[rows, cs] = (dzz * sv * sg * dgu).astype(BF16)
                dsb16 = dsv.astype(BF16)
                dvln.append(lax.dot_general(wm_ref[g], dsb16, (((0,), (0,)), ((), ())), preferred_element_type=F32))
                dwm_ref[g] += lax.dot_general(dsb16, vlb[:, cs], (((1,), (1,)), ((), ())),
                                              preferred_element_type=F32)
                dsv_all.append(dsv)
            dvl = jnp.concatenate(dvln, axis=-1)
            dsb_ref[...] += _hdot(jnp.concatenate(dsv_all, axis=-1), sel)
            dlng_ref[...] += jnp.sum(dvl * xhat, axis=0, keepdims=True)
            dlnb_ref[...] += jnp.sum(dvl, axis=0, keepdims=True)
            dxh = dvl * lng_ref[...]
            dgv = rstd * (dxh - jnp.mean(dxh, axis=-1, keepdims=True)
                          - xhat * jnp.mean(dxh * xhat, axis=-1, keepdims=True))
            dv_ref[rows, :] = (dgv * dgv_dv).astype(BF16)

    tm = GMLP_BWD_CHUNKS * SGC
    return pl.pallas_call(
        body, grid=(T // tm,), name="gmlp_bwd",
        in_specs=[_row_spec(tm, D)] * 3 + [_const_spec((1, D))] * 2
        + [_const_spec((NG, SGC, SGC)), _const_spec((SGC, NG)), _row_spec(tm, D), _col_spec(D, tm),
           _const_spec((D, D))],
        out_specs=[_row_spec(tm, D)] * 3 + [_const_spec((1, D))] * 2
        + [_const_spec((NG, SGC, SGC)), _const_spec((SGC, NG)), _const_spec((D, D))],
        out_shape=[jax.ShapeDtypeStruct((T, D), BF16)] * 3 + [jax.ShapeDtypeStruct((1, D), F32)] * 2
        + [jax.ShapeDtypeStruct((NG, SGC, SGC), F32), jax.ShapeDtypeStruct((SGC, NG), F32),
           jax.ShapeDtypeStruct((D, D), F32)],
        compiler_params=_cparams(("arbitrary",), VMEM_BIG),
    )(u, v, gate, lng, lnb, wm_bf, sgb_t, dh, zt_bf, w_bf)


NCHIP = 4
NDEV = 8
ANY = pl.BlockSpec(memory_space=pl.ANY)


HBM = pl.BlockSpec(memory_space=pltpu.HBM)
SEM = pl.BlockSpec(memory_space=pltpu.SEMAPHORE)
EFFECT = pltpu.SideEffectType.DATAFLOW_SIDE_EFFECTING


def _peers(whole_mesh):
    x, y, c = lax.axis_index("x"), lax.axis_index("y"), lax.axis_index("c")
    if not whole_mesh:
        return [((px, py, c), 2 * px + py) for px, py in ((1 - x, y), (x, 1 - y), (1 - x, 1 - y))], 2 * x + y
    out = []
    for j in range(1, NDEV):
        px, py, pc = x ^ (j >> 2), y ^ ((j >> 1) & 1), c ^ (j & 1)
        out.append(((px, py, pc), 4 * px + 2 * py + pc))
    return out, 4 * x + 2 * y + c


def _send_copies(src, land, send, recv, scatter, whole_mesh, starting):
    peers, me = _peers(whole_mesh)
    copies = []
    for t in range(len(src)):
        for j, (dev, slot) in enumerate(peers):
            k = t * len(peers) + j
            copies.append(pltpu.make_async_remote_copy(
                src_ref=src[t].at[slot] if scatter else src[t], dst_ref=land[t].at[me if starting else slot],
                send_sem=send.at[k], recv_sem=recv.at[k], device_id=dev, device_id_type=MESH))
    return copies


def send_start(srcs, lands, scatter, whole_mesh, name, after=None):
    n = len(srcs)
    nsem = n * (NDEV - 1 if whole_mesh else NCHIP - 1)
    extra_specs, extra = _after_operand(after)

    def body(*refs):
        send, recv = refs[2 * n + len(extra)], refs[2 * n + len(extra) + 1]
        for cp in _send_copies(refs[:n], refs[n:2 * n], send, recv, scatter, whole_mesh, True):
            cp.start()
        refs[-1][...] = jnp.zeros_like(refs[-1])

    arrs = list(srcs) + list(lands)
    out = pl.pallas_call(
        body, name=name,
        out_shape=(pltpu.SemaphoreType.DMA((nsem,)), pltpu.SemaphoreType.DMA((nsem,)),
                   *[pltpu.HBM(a.shape, a.dtype) for a in arrs], jax.ShapeDtypeStruct((8, 128), F32)),
        in_specs=[HBM] * (2 * n) + extra_specs,
        out_specs=(SEM, SEM, *[HBM] * (2 * n), pl.BlockSpec(memory_space=pltpu.VMEM)),
        input_output_aliases={i: 2 + i for i in range(2 * n)},
        compiler_params=pltpu.CompilerParams(has_side_effects=EFFECT),
    )(*[pltpu.with_memory_space_constraint(a, pltpu.HBM) for a in arrs], *extra)
    return out[0], out[1], list(out[2:2 + n]), list(out[2 + n:2 + 2 * n]), out[-1]


def send_wait(started, after, scatter, whole_mesh, name):
    send, recv, srcs, lands, _ = started
    n = len(srcs)

    def body(*refs):
        for cp in _send_copies(refs[:n], refs[n:2 * n], refs[2 * n], refs[2 * n + 1], scatter, whole_mesh, False):
            cp.wait_send()
            cp.wait_recv()

    arrs = list(srcs) + list(lands)
    out = pl.pallas_call(
        body, name=name, out_shape=tuple(pltpu.HBM(a.shape, a.dtype) for a in arrs),
        in_specs=[HBM] * (2 * n) + [SEM, SEM, ANY], out_specs=tuple([HBM] * (2 * n)),
        input_output_aliases={i: i for i in range(2 * n)},
        compiler_params=pltpu.CompilerParams(has_side_effects=EFFECT),
    )(*arrs, send, recv, after)
    return list(out[n:])


def exchange_c(arrs, name):
    n = len(arrs)

    def body(*refs):
        ins, outs = refs[:n], refs[n:2 * n]
        send, recv = refs[2 * n:]
        sibling = (lax.axis_index("x"), lax.axis_index("y"), 1 - lax.axis_index("c"))
        copies = [pltpu.make_async_remote_copy(src_ref=ins[t], dst_ref=outs[t], send_sem=send.at[t], recv_sem=recv.at[t],
                                               device_id=sibling, device_id_type=MESH) for t in range(n)]
        for cp in copies:
            cp.start()
        for cp in copies:
            cp.wait()

    return pl.pallas_call(
        body, name=name, in_specs=[ANY] * n, out_specs=[ANY] * n,
        out_shape=[jax.ShapeDtypeStruct(a.shape, a.dtype) for a in arrs],
        scratch_shapes=[pltpu.SemaphoreType.DMA((n,)), pltpu.SemaphoreType.DMA((n,))],
    )(*arrs)


def swap_row_halves(a, name):
    n, rows, cols = a.shape
    half = rows // 2

    def body(in_ref, out_ref, send, recv):
        x, y, c = lax.axis_index("x"), lax.axis_index("y"), lax.axis_index("c")
        cp = pltpu.make_async_remote_copy(src_ref=in_ref.at[:, pl.ds((1 - c) * half, half), :], dst_ref=out_ref,
                                          send_sem=send, recv_sem=recv, device_id=(x, y, 1 - c), device_id_type=MESH)
        cp.start()
        cp.wait()

    return pl.pallas_call(
        body, name=name, in_specs=[ANY], out_specs=ANY, out_shape=jax.ShapeDtypeStruct((n, half, cols), a.dtype),
        scratch_shapes=[pltpu.SemaphoreType.DMA, pltpu.SemaphoreType.DMA],
    )(a)


def add_blocks(a, b, name):
    n, rows, cols = a.shape
    tr = _rows_tile(rows)

    def body(a_ref, b_ref, o_ref):
        o_ref[...] = (a_ref[...].astype(F32) + b_ref[...].astype(F32)).astype(BF16)

    spec = pl.BlockSpec((1, tr, cols), lambda s, i: (s, i, 0))
    return pl.pallas_call(
        body, grid=(n, rows // tr), name=name, in_specs=[spec, spec], out_specs=spec,
        out_shape=jax.ShapeDtypeStruct(a.shape, BF16), compiler_params=_cparams(("parallel", "parallel")),
    )(a, b)


def gather_weights(arrs, split):
    n = len(arrs)

    def body(*refs):
        ins, outs = refs[:n], refs[n:2 * n]
        send1, recv1, send2, recv2, loc = refs[2 * n:]
        x, y, c = lax.axis_index("x"), lax.axis_index("y"), lax.axis_index("c")
        me = 2 * x + y
        sibling = (x, y, 1 - c)
        peers = [(1 - x, y), (x, 1 - y), (1 - x, 1 - y)]

        def rows_of(t, core):
            half = arrs[t].shape[0] // 2
            return pl.ds(core * half, half)

        def part(ref, t, core):
            return ref.at[rows_of(t, core)] if split[t] else ref

        local = [pltpu.make_async_copy(ins[t], outs[t].at[me], loc.at[t]) for t in range(n)]
        for cp in local:
            cp.start()
        first = []
        for t in range(n):
            for j, (px, py) in enumerate(peers):
                first.append(pltpu.make_async_remote_copy(
                    src_ref=part(ins[t], t, c), dst_ref=part(outs[t].at[me], t, c), send_sem=send1.at[t, j],
                    recv_sem=recv1.at[t, j], device_id=(px, py, c), device_id_type=MESH))
        for cp in first:
            cp.start()
        passed = []
        for t in range(n):
            for j, (px, py) in enumerate(peers):
                landed = part(outs[t].at[2 * px + py], t, c)
                pltpu.make_async_remote_copy(
                    src_ref=landed, dst_ref=landed, send_sem=send1.at[t, j], recv_sem=recv1.at[t, j],
                    device_id=(x, y, c), device_id_type=MESH).wait_recv()
                if split[t]:
                    cp = pltpu.make_async_remote_copy(
                        src_ref=landed, dst_ref=landed, send_sem=send2.at[t, j], recv_sem=recv2.at[t, j],
                        device_id=sibling, device_id_type=MESH)
                    cp.start()
                    passed.append(cp)
        for t in range(n):
            for j, (px, py) in enumerate(peers):
                if split[t]:
                    other = part(outs[t].at[2 * px + py], t, 1 - c)
                    pltpu.make_async_remote_copy(
                        src_ref=other, dst_ref=other, send_sem=send2.at[t, j], recv_sem=recv2.at[t, j],
                        device_id=(x, y, c), device_id_type=MESH).wait_recv()
        for cp in first + passed:
            cp.wait_send()
        for cp in local:
            cp.wait()

    return pl.pallas_call(
        body, name="gather_weights", in_specs=[ANY] * n, out_specs=[ANY] * n,
        out_shape=[jax.ShapeDtypeStruct((NCHIP,) + a.shape, a.dtype) for a in arrs],
        scratch_shapes=[pltpu.SemaphoreType.DMA((n, 3))] * 4 + [pltpu.SemaphoreType.DMA((n,))],
    )(*arrs)


def _adam_math(g, w, m, v):
    m = ADAM_B1 * m + (1.0 - ADAM_B1) * g
    v = ADAM_B2 * v + (1.0 - ADAM_B2) * (g * g)
    m_hat = m / (1.0 - ADAM_B1 ** ADAM_STEP)
    v_hat = v / (1.0 - ADAM_B2 ** ADAM_STEP)
    delta = -ADAM_LR * (m_hat / (jnp.sqrt(v_hat) + ADAM_EPS) + ADAM_WD * w)
    return delta, m, v


def _rows_tile(rows):
    return rows if rows <= 256 else 256


def sum_chips(own, parts, name):
    _, rows, cols = parts.shape
    tr = _rows_tile(rows)

    def body(own_ref, p_ref, o_ref):
        acc = own_ref[...].astype(F32)
        for s in range(NCHIP):
            acc = acc + p_ref[s].astype(F32)
        o_ref[...] = acc

    return pl.pallas_call(
        body, grid=(rows // tr,), name=name,
        in_specs=[pl.BlockSpec((tr, cols), lambda i: (i, 0)), pl.BlockSpec((NCHIP, tr, cols), lambda i: (0, i, 0))],
        out_specs=pl.BlockSpec((tr, cols), lambda i: (i, 0)),
        out_shape=jax.ShapeDtypeStruct((rows, cols), F32),
        compiler_params=_cparams(("parallel",)),
    )(own, parts)


def adam_shard(p_mine, p_sib, w, m, v, name):
    rows, cols = p_mine.shape
    tr = _rows_tile(rows)
    lead = w.ndim == 3

    def body(a_ref, b_ref, w_ref, m_ref, v_ref, g_ref, d_ref, mo_ref, vo_ref):
        g = a_ref[...] + b_ref[...]
        g = g[None] if lead else g
        g_ref[...] = g
        d_ref[...], mo_ref[...], vo_ref[...] = _adam_math(g, w_ref[...], m_ref[...], v_ref[...])

    flat = pl.BlockSpec((tr, cols), lambda i: (i, 0))
    spec = pl.BlockSpec((1, tr, cols), lambda i: (0, i, 0)) if lead else flat
    return pl.pallas_call(
        body, grid=(rows // tr,), name=name, in_specs=[flat] * 2 + [spec] * 3, out_specs=[spec] * 4,
        out_shape=[jax.ShapeDtypeStruct(w.shape, F32)] * 4,
        compiler_params=_cparams(("parallel",)),
    )(p_mine, p_sib, w, m, v)


def adam_shard_halves_t(r_mine, r_sib, wt, mt, vt, name):
    hrows, cols = r_mine.shape
    tr = _rows_tile(hrows)
    per_half = hrows // tr

    def body(a_ref, b_ref, w_ref, m_ref, v_ref, g_ref, d_ref, mo_ref, vo_ref):
        mine = pl.program_id(0) == lax.axis_index("c")
        g = jnp.where(mine, a_ref[...], b_ref[...]).T[None]
        g_ref[...] = g
        d_ref[...], mo_ref[...], vo_ref[...] = _adam_math(g, w_ref[...], m_ref[...], v_ref[...])

    flat = pl.BlockSpec((tr, cols), lambda h, i: (i, 0))
    spec = pl.BlockSpec((1, cols, tr), lambda h, i: (0, 0, h * per_half + i))
    return pl.pallas_call(
        body, grid=(2, per_half), name=name, in_specs=[flat] * 2 + [spec] * 3, out_specs=[spec] * 4,
        out_shape=[jax.ShapeDtypeStruct(wt.shape, F32)] * 4,
        compiler_params=_cparams(("parallel", "parallel")),
    )(r_mine, r_sib, wt, mt, vt)


def adam_replicated(parts, w, m, v, name):
    rows = w.shape[0]

    def body(p_ref, w_ref, m_ref, v_ref, g_ref, d_ref, mo_ref, vo_ref):
        g = p_ref[0]
        for d in range(1, NDEV):
            g = g + p_ref[d]
        g_ref[...] = g
        d_ref[...], mo_ref[...], vo_ref[...] = _adam_math(g, w_ref[...], m_ref[...], v_ref[...])

    return pl.pallas_call(
        body, name=name, out_shape=[jax.ShapeDtypeStruct((rows, 128), F32)] * 4,
    )(parts, w, m, v)


def _pack(arrs):
    pieces = []
    for a in arrs:
        flat = a.reshape(-1)
        pad = (-flat.shape[0]) % 128
        pieces.append(jnp.pad(flat, (0, pad)) if pad else flat)
    flat = jnp.concatenate(pieces)
    pad = (-flat.shape[0]) % 1024
    return jnp.pad(flat, (0, pad)).reshape(-1, 128)


def _unpack(buf, shapes):
    flat = buf.reshape(-1)
    out = []
    o = 0
    for s in shapes:
        n = int(np.prod(s))
        out.append(flat[o:o + n].reshape(s))
        o += n + (-n) % 128
    return out


EVEN_SPLITS = (SHIFT, W, W, W, W, W)
ODD_SPLITS = (D, D, D)


def _cols_to_chips(a):
    rows, cols = a.shape
    return a.reshape(rows, NCHIP, cols // NCHIP).transpose(1, 0, 2)


def _chips_to_cols(a):
    _, rows, n = a.shape
    return a.transpose(1, 0, 2).reshape(rows, NCHIP * n)


def kernel(x, norm_g, w_in_e, shift_mu, rw_w0, rw_w2, rw_a0, rw_a2, rw_kk, rw_ka, rw_rk, rw_lnx_g, rw_lnx_b, att_bias, w_out_e, w_in_o, sg_ln_g, sg_ln_b, sg_w, sg_b, w_out_o, final_g, loss_target, m_norm_g, m_w_in_e, m_shift_mu, m_rw_w0, m_rw_w2, m_rw_a0, m_rw_a2, m_rw_kk, m_rw_ka, m_rw_rk, m_rw_lnx_g, m_rw_lnx_b, m_att_bias, m_w_out_e, m_w_in_o, m_sg_ln_g, m_sg_ln_b, m_sg_w, m_sg_b, m_w_out_o, m_final_g, v_norm_g, v_w_in_e, v_shift_mu, v_rw_w0, v_rw_w2, v_rw_a0, v_rw_a2, v_rw_kk, v_rw_ka, v_rw_rk, v_rw_lnx_g, v_rw_lnx_b, v_att_bias, v_w_out_e, v_w_in_o, v_sg_ln_g, v_sg_ln_b, v_sg_w, v_sg_b, v_w_out_o, v_final_g):
    x2 = x.reshape(T, D)
    tgt = loss_target.reshape(T, D)

    my_chip = 2 * lax.axis_index("x") + lax.axis_index("y")
    gathered = gather_weights(
        [jnp.swapaxes(w_in_e[0], 0, 1).astype(BF16), jnp.concatenate([rw_w2[0], rw_a2[0]], axis=0),
         jnp.concatenate([sg_ln_g, sg_ln_b], axis=0)], [True, True, False])
    wie = gathered[0].reshape(EVEN_IN, D)
    w2 = _chips_to_cols(gathered[1][:, :LORA])
    a2 = _chips_to_cols(gathered[1][:, LORA:])
    sglg = _chips_to_cols(gathered[2][:, 0:1])
    sglb = _chips_to_cols(gathered[2][:, 1:2])

    late = [w_out_e[0].astype(BF16), w_in_o[0].astype(BF16), w_out_o[0].astype(BF16)]
    late_started = send_start(late, [jnp.broadcast_to(a[None], (NCHIP,) + a.shape) for a in late], False, False,
                              "late_weights_start", after=gathered[0])

    def late_weights(after):
        woe, wio, woo = send_wait(late_started, after, False, False, "late_weights_wait")
        return woe.reshape(D, D), _chips_to_cols(wio), woo.reshape(D, D)

    def scatter_start(grads, name):
        srcs = [g_.astype(BF16) if g_.shape[-1] >= W else g_ for g_ in grads]
        return send_start(srcs, [jnp.zeros_like(s) for s in srcs], True, False, name)

    def own_block(g_):
        return lax.dynamic_index_in_dim(g_, my_chip, axis=0, keepdims=False)

    started = {}

    def on_odd_grads(d_woo, d_wio):
        blocks = [d_woo.reshape(NCHIP, D // NCHIP, D), d_wio]
        started["odd"] = (scatter_start(blocks, "odd_grads_start"), [own_block(b) for b in blocks])
        return started["odd"][0][-1]

    def on_even_grads(big_g):
        d_wie, d_woe, _, _, d_w2, d_a2, d_sglg, d_sglb = big_g
        my_half = lax.dynamic_slice_in_dim(d_wie, lax.axis_index("c") * (D // 2), D // 2, axis=1)
        d_wie_half = add_blocks(my_half, swap_row_halves(d_wie, "swap_w_in_e_halves"), "add_w_in_e_halves")
        blocks = [d_wie_half, d_woe.reshape(NCHIP, D // NCHIP, D), _cols_to_chips(d_w2), _cols_to_chips(d_a2),
                  _cols_to_chips(d_sglg), _cols_to_chips(d_sglb)]
        started["even"] = (scatter_start(blocks, "even_grads_start"), [own_block(b) for b in blocks])
        return started["even"][0][-1]

    def on_small_grads(layer, grads):
        mine = _pack(grads)
        started[layer + "_small"] = send_start([mine], [jnp.broadcast_to(mine[None], (NDEV,) + mine.shape)], False,
                                               True, layer + "_small_grads_start")
        return started[layer + "_small"][-1]

    loss_part, dx, _, _ = _local_step(
        x2, tgt, wie, late_weights, w2, a2, sglg, sglb, norm_g, shift_mu, rw_w0, rw_a0, rw_kk, rw_ka, rw_rk,
        rw_lnx_g, rw_lnx_b, att_bias, sg_w, sg_b, final_g, first_after=late_started[-1], on_odd_grads=on_odd_grads,
        on_even_grads=on_even_grads, on_small_grads=on_small_grads)
    even_started, even_own = started["even"]

    wmv = {"w_in_e": tuple(jnp.swapaxes(a, 1, 2) for a in (w_in_e, m_w_in_e, v_w_in_e)),
           "w_out_e": (w_out_e, m_w_out_e, v_w_out_e),
           "w_in_o": (w_in_o, m_w_in_o, v_w_in_o), "w_out_o": (w_out_o, m_w_out_o, v_w_out_o),
           "rw_w2": (rw_w2, m_rw_w2, v_rw_w2), "rw_a2": (rw_a2, m_rw_a2, v_rw_a2),
           "sg_ln_g": (sg_ln_g, m_sg_ln_g, v_sg_ln_g), "sg_ln_b": (sg_ln_b, m_sg_ln_b, v_sg_ln_b)}
    sharded = {}

    def finish(names, own, landed, tag):
        partial = [sum_chips(o_, p_, "sum_" + nm) for o_, p_, nm in zip(own, landed, names)]
        from_sibling = exchange_c(partial, "swap_partials_" + tag)
        for nm, mine, sib in zip(names, partial, from_sibling):
            if nm == "w_in_e":
                res = adam_shard_halves_t(mine, sib, *wmv[nm], "adam_" + nm)
                sharded[nm] = [jnp.swapaxes(a, 1, 2) for a in res]
            else:
                sharded[nm] = adam_shard(mine, sib, *wmv[nm], "adam_" + nm)
        return partial[0]

    odd_started, odd_own = started["odd"]
    odd_landed = send_wait(odd_started, started["even_small"][-1], True, False, "odd_grads_wait")
    done = finish(["w_out_o", "w_in_o"], odd_own, odd_landed, "odd")

    no_w = jnp.zeros((1, 1), F32)
    groups = {
        "odd": (["sg_w", "sg_b", "final_g", "norm_g1"], [sg_w, sg_b, final_g, norm_g[1:2]],
                [m_sg_w, m_sg_b, m_final_g, m_norm_g[1:2]], [v_sg_w, v_sg_b, v_final_g, v_norm_g[1:2]]),
        "even": (["norm_g0", "shift_mu", "rw_w0", "rw_a0", "rw_kk", "rw_ka", "rw_rk", "rw_lnx_g", "rw_lnx_b",
                  "att_bias", "loss"],
                 [norm_g[0:1], shift_mu, rw_w0, rw_a0, rw_kk, rw_ka, rw_rk, rw_lnx_g, rw_lnx_b, att_bias, no_w],
                 [m_norm_g[0:1], m_shift_mu, m_rw_w0, m_rw_a0, m_rw_kk, m_rw_ka, m_rw_rk, m_rw_lnx_g, m_rw_lnx_b,
                  m_att_bias, no_w],
                 [v_norm_g[0:1], v_shift_mu, v_rw_w0, v_rw_a0, v_rw_kk, v_rw_ka, v_rw_rk, v_rw_lnx_g, v_rw_lnx_b,
                  v_att_bias, no_w]),
    }
    rep = {}
    for layer in ("odd", "even"):
        nms, ws, ms_, vs_ = groups[layer]
        (gathered_g,) = send_wait(started[layer + "_small"], done, False, True, layer + "_small_grads_wait")
        rep_out = adam_replicated(gathered_g, _pack(ws), _pack(ms_), _pack(vs_), "adam_" + layer + "_small")
        done = rep_out[0]
        for nm in nms:
            rep[nm] = []
        for buf in rep_out:
            for nm, a in zip(nms, _unpack(buf, [w_.shape for w_ in ws])):
                rep[nm].append(a)
    rep["norm_g"] = [jnp.concatenate([a, b], axis=0) for a, b in zip(rep["norm_g0"], rep["norm_g1"])]
    even_landed = send_wait(even_started, done, True, False, "even_grads_wait")
    finish(["w_in_e", "w_out_e", "rw_w2", "rw_a2", "sg_ln_g", "sg_ln_b"], even_own, even_landed, "even")

    order = ["norm_g", "w_in_e", "shift_mu", "rw_w0", "rw_w2", "rw_a0", "rw_a2", "rw_kk", "rw_ka", "rw_rk",
             "rw_lnx_g", "rw_lnx_b", "att_bias", "w_out_e", "w_in_o", "sg_ln_g", "sg_ln_b", "sg_w", "sg_b",
             "w_out_o", "final_g"]
    results = {**sharded, **rep}
    outs = [rep["loss"][0].reshape(()), dx.reshape(NSEQ, SEQ, D)]
    for kind in range(4):
        outs += [results[nm][kind] for nm in order]
    return tuple(outs)


def _local_step(x2, tgt, wie_t, late_weights, w2, a2, sglg, sglb, norm_g, shift_mu, rw_w0, rw_a0, rw_kk, rw_ka, rw_rk,
                rw_lnx_g, rw_lnx_b, att_bias, sg_w, sg_b, final_g, first_after=None, on_odd_grads=None,
                on_even_grads=None, on_small_grads=None):
    zl = jnp.zeros((LORA, W), F32)
    w2x = jnp.concatenate([w2, zl], axis=0)
    a2x = jnp.concatenate([zl, a2], axis=0)
    rk = rw_rk.reshape(1, W)
    pos = np.arange(SGC)
    sg_mask = jnp.asarray(((pos[None, :] // L) <= (pos[:, None] // L)).astype(np.float32))
    wm = (sg_w[0] * sg_mask[None]).astype(BF16)
    sgb_t = sg_b[0].T

    xn0, ps, ga, q, kb, vb, gb = ln_in_proj(x2, norm_g[0:1], wie_t, EVEN_SPLITS, "in_proj_even", after=first_after,
                                            w_t=True)
    r, lw, k2, v, aa, bb = even_prep(ps, shift_mu, rw_w0, w2x, rw_a0, a2x, rw_kk, rw_ka)
    y, rw_saved = rwkv_fwd(r, lw, k2, v, aa, bb)
    bias = window_bias(bias_expand(att_bias[0]).reshape(NPAIR, 2 * L, BAND))

    def padded(a):
        return jnp.pad(a.astype(BF16).reshape(NSEQ, SEQ, W), ((0, 0), (LEFT * L, 0), (0, 0))).reshape(NSEQ * PADSEQ, W)

    kpad, vpad = padded(kb), padded(vb)
    o = attention_fwd(q, kpad, vpad, bias)
    woe, wio, woo = late_weights(o)
    h1, zt = even_post(y, r, k2, v, ga, o, gb, rw_lnx_g, rw_lnx_b, rk, x2, woe)
    xn1, u, vv, gt = ln_in_proj(h1, norm_g[1:2], wio, ODD_SPLITS, "in_proj_odd")
    dh2, loss_part, d_final_g, z2t = gmlp_fwd_loss(u, vv, gt, sglg, sglb, wm, sgb_t, h1, woo, final_g[None], tgt)

    du, dvv, dgt, d_sglg, d_sglb, d_wm, d_sgb_t, d_woo = gmlp_bwd(u, vv, gt, sglg, sglb, wm, sgb_t, dh2, z2t, woo)
    dp_odd = [du, dvv, dgt]
    d_wio = matmul_acc_chips(xn1, dp_odd, "in_proj_odd_dw")
    token = on_odd_grads(d_woo, d_wio) if on_odd_grads else None
    dh1, d_g1 = in_proj_bwd_x(h1, norm_g[1:2], wio, dp_odd, dh2, "in_proj_odd_bwd", after=token)
    odd_small = [d_wm * sg_mask[None], d_sgb_t.T, d_final_g, d_g1]
    token = on_small_grads("odd", odd_small) if on_small_grads else None
    dy, dr2, dk22, dv2, dga, do, dgb, d_lng, d_lnb, d_rk, d_woe = even_post_bwd(
        y, r, k2, v, ga, o, gb, rw_lnx_g, rw_lnx_b, rk, dh1, zt, woe, after=token)
    dq, dkb, dvb, dbias = attention_bwd(q, kpad, vpad, bias, do)
    dbias = sum(dbias[:, i * 2 * L:(i + 1) * 2 * L, i * L:i * L + BAND] for i in range(ATT_Q))
    d_att_bias = bias_grad(dbias.reshape(NH, L, BAND))
    dr, dlw, dk2, dv, daa, dbb = rwkv_bwd(r, lw, k2, aa, bb, rw_saved, dy)
    dps, d_mu, d_w0, d_w2x, d_a0, d_a2x, d_kk, d_ka = even_prep_bwd(
        ps, shift_mu, rw_w0, w2x, rw_a0, a2x, rw_kk, rw_ka, dr, dlw, dk2, dv, daa, dbb, dr2, dk22, dv2)
    dp_even = [dps, dga, dq, dkb, dvb, dgb]
    d_wie = matmul_acc_chips(xn0, dp_even, "in_proj_even_dw")
    big_g = (d_wie, d_woe, d_wio, d_woo, d_w2x[:LORA], d_a2x[LORA:], d_sglg, d_sglb)
    token = on_even_grads(big_g) if on_even_grads else None
    dx, d_g0 = in_proj_bwd_x(x2, norm_g[0:1], wie_t, dp_even, dh1, "in_proj_even_bwd", after=token, w_t=True)
    even_small = [d_g0, d_mu, d_w0, d_a0, d_kk, d_ka, d_rk, d_lng, d_lnb, d_att_bias]
    if on_small_grads:
        on_small_grads("even", even_small + [loss_part[0:1, 0:1]])
    rep_g = [jnp.concatenate([d_g0, d_g1], axis=0)] + even_small[1:] + odd_small[:3]
    return loss_part[0, 0], dx, big_g, rep_g
```

```python
import functools
import math

import jax
import jax.numpy as jnp
import numpy as np
from jax import lax
from jax.experimental import pallas as pl
from jax.experimental.pallas import tpu as pltpu

F32 = jnp.float32
BF16 = jnp.bfloat16
HI = lax.Precision.HIGHEST

D = 1024
SEQ = 2048
NSEQ = 2
T = NSEQ * SEQ
HD = 64
NH = 8
W = 512
SHIFT = 1664
LORA = 64
EVEN_IN = 4224
ODD_IN = 3072
L = 64
NC = SEQ // L
LEFT = 8
BAND = (LEFT + 1) * L
CLIP = 128
SGC = 128
NG = 8
RMS_EPS = 1e-6
LN_EPS = 1e-5
GN_EPS = 64e-5
NEG = -1e30
VMEM_BIG = 56 * 1024 * 1024

ADAM_LR = 0.001
ADAM_B1 = 0.9
ADAM_B2 = 0.999
ADAM_EPS = 1e-08
ADAM_WD = 0.01
ADAM_STEP = 10

MESH = pl.DeviceIdType.MESH


def _bdot(a, b):
    return jnp.dot(a.astype(BF16), b.astype(BF16), preferred_element_type=F32)


def _bdot_nt(a, b):
    return lax.dot_general(a.astype(BF16), b.astype(BF16), (((1,), (1,)), ((), ())), preferred_element_type=F32)


def _bdot_tn(a, b):
    return lax.dot_general(a.astype(BF16), b.astype(BF16), (((0,), (0,)), ((), ())), preferred_element_type=F32)


def _hdot(a, b):
    return jnp.dot(a, b, precision=HI, preferred_element_type=F32)


def _hdot_nt(a, b):
    return lax.dot_general(a, b, (((1,), (1,)), ((), ())), precision=HI, preferred_element_type=F32)


def _hdot_tn(a, b):
    return lax.dot_general(a, b, (((0,), (0,)), ((), ())), precision=HI, preferred_element_type=F32)


def _iota2(shape, dim):
    return lax.broadcasted_iota(jnp.int32, shape, dim)


def _head_blockdiag():
    r = _iota2((2 * HD, 2 * HD), 0) // HD
    c = _iota2((2 * HD, 2 * HD), 1) // HD
    return (r == c).astype(BF16)


def _headsum_impl(x, bd):
    hi = x.astype(BF16)
    mid = (x - hi.astype(F32)).astype(BF16)
    n = bd.shape[0]
    out = [jnp.dot(hi[:, i:i + n], bd, preferred_element_type=F32) + jnp.dot(mid[:, i:i + n], bd, preferred_element_type=F32)
           for i in range(0, x.shape[1], n)]
    return jnp.concatenate(out, axis=-1)


@jax.custom_vjp
def _headsum(x, bd):
    return _headsum_impl(x, bd)


def _headsum_fwd(x, bd):
    return _headsum_impl(x, bd), bd


def _headsum_bwd(bd, ct):
    return _headsum_impl(ct, bd), None


_headsum.defvjp(_headsum_fwd, _headsum_bwd)


def _silu(x):
    return x * jax.nn.sigmoid(x)


def _dsilu(x):
    s = jax.nn.sigmoid(x)
    return s * (1.0 + x * (1.0 - s))


_GELU_C = math.sqrt(2.0 / math.pi)


def _gelu(x):
    return 0.5 * x * (1.0 + jnp.tanh(_GELU_C * (x + 0.044715 * (x * x * x))))


def _dgelu(x):
    t = jnp.tanh(_GELU_C * (x + 0.044715 * (x * x * x)))
    return 0.5 * (1.0 + t) + 0.5 * x * (1.0 - t * t) * _GELU_C * (1.0 + 3.0 * 0.044715 * x * x)


def _silu_both(x):
    s = jax.nn.sigmoid(x)
    xs = x * s
    return xs, s + xs * (1.0 - s)


def _gelu_both(x):
    x2 = x * x
    t = jnp.tanh(_GELU_C * (x + 0.044715 * (x2 * x)))
    half = 0.5 * (1.0 + t)
    return x * half, half + 0.5 * x * (1.0 - t * t) * _GELU_C * (1.0 + 3.0 * 0.044715 * x2)


def _softplus(x):
    return jnp.maximum(x, 0.0) + jnp.log(1.0 + jnp.exp(-jnp.abs(x)))


def _cparams(sem, vmem=None):
    return pltpu.CompilerParams(dimension_semantics=sem, vmem_limit_bytes=vmem)


def _row_spec(tm, width):
    return pl.BlockSpec((tm, width), lambda i: (i, 0))


def _col_spec(height, tm):
    return pl.BlockSpec((height, tm), lambda i: (0, i))


def _const_spec(shape):
    nd = len(shape)
    return pl.BlockSpec(shape, lambda *_: (0,) * nd)


def _weight_dims(w_bf, w_t):
    return (((1,), (1,)), ((), ())) if w_t else (((1,), (0,)), ((), ())), w_bf.shape[0 if w_t else 1]


def ln_in_proj(x, g, w_bf, splits, name, after=None, w_t=False):
    dims, n = _weight_dims(w_bf, w_t)
    tm = 512 if n <= ODD_IN else 256
    spans = []
    o = 0
    for s in splits:
        spans.append((o, o + s))
        o += s
    assert o == n
    extra_specs, extra = _after_operand(after)

    def body(x_ref, g_ref, w_ref, *rest):
        xn_ref, outs = rest[len(extra)], rest[len(extra) + 1:]
        xv = x_ref[...]
        rstd = lax.rsqrt(jnp.mean(xv * xv, axis=-1, keepdims=True) + RMS_EPS)
        xn = (xv * rstd * g_ref[...]).astype(BF16)
        xn_ref[...] = xn.T
        p = lax.dot_general(xn, w_ref[...], dims, preferred_element_type=F32)
        for o_ref, (a, b) in zip(outs, spans):
            o_ref[...] = p[:, a:b]

    return pl.pallas_call(
        body, grid=(T // tm,), name=name,
        in_specs=[_row_spec(tm, D), _const_spec((1, D)), _const_spec(w_bf.shape)] + extra_specs,
        out_specs=[_col_spec(D, tm)] + [_row_spec(tm, s) for s in splits],
        out_shape=[jax.ShapeDtypeStruct((D, T), BF16)] + [jax.ShapeDtypeStruct((T, s), F32) for s in splits],
        compiler_params=_cparams(("parallel",), VMEM_BIG),
    )(x, g, w_bf, *extra)


def in_proj_bwd_x(x, g, w_bf, dps, dres, name, after=None, w_t=False):
    tm = 512
    back = (((1,), (0,)), ((), ())) if w_t else (((1,), (1,)), ((), ()))
    widths = [d.shape[1] for d in dps]
    extra_specs, extra = _after_operand(after)

    def body(x_ref, g_ref, w_ref, dres_ref, *rest):
        dp_refs = rest[:len(widths)]
        dx_ref, dg_ref = rest[-2:]
        dp = jnp.concatenate([r[...] for r in dp_refs], axis=-1)
        dxn = lax.dot_general(dp, w_ref[...], back, preferred_element_type=F32)
        xv = x_ref[...]
        rstd = lax.rsqrt(jnp.mean(xv * xv, axis=-1, keepdims=True) + RMS_EPS)
        xhat = xv * rstd
        dgp = jnp.sum(dxn * xhat, axis=0, keepdims=True)

        @pl.when(pl.program_id(0) == 0)
        def _():
            dg_ref[...] = jnp.zeros_like(dg_ref)

        dg_ref[...] += dgp
        dxh = dxn * g_ref[...]
        dx_ref[...] = dres_ref[...] + rstd * (dxh - xhat * jnp.mean(dxh * xhat, axis=-1, keepdims=True))

    return pl.pallas_call(
        body, grid=(T // tm,), name=name,
        in_specs=[_row_spec(tm, D), _const_spec((1, D)), _const_spec(w_bf.shape), _row_spec(tm, D)]
        + [_row_spec(tm, s) for s in widths] + extra_specs,
        out_specs=[_row_spec(tm, D), _const_spec((1, D))],
        out_shape=[jax.ShapeDtypeStruct((T, D), F32), jax.ShapeDtypeStruct((1, D), F32)],
        compiler_params=_cparams(("arbitrary",), VMEM_BIG),
    )(x, g, w_bf, dres, *dps, *extra)


def _after_operand(after):
    return ([ANY], [after]) if after is not None else ([], [])


def matmul_acc_chips(at_bf, pieces, name, after=None):
    k = at_bf.shape[0]
    widths = [p.shape[1] for p in pieces]
    nb = sum(widths) // NCHIP
    tm = 512
    steps = T // tm
    extra_specs, extra = _after_operand(after)

    def body(a_ref, *rest):
        o_ref, acc = rest[-2:]

        @pl.when(pl.program_id(0) == 0)
        def _():
            acc[...] = jnp.zeros_like(acc)

        a = a_ref[...]
        b = jnp.concatenate([r[...] for r in rest[:len(widths)]], axis=-1)
        for s in range(NCHIP):
            acc[s] += jnp.dot(a, b[:, s * nb:(s + 1) * nb], preferred_element_type=F32)

        @pl.when(pl.program_id(0) == steps - 1)
        def _():
            o_ref[...] = acc[...].astype(BF16)

    return pl.pallas_call(
        body, grid=(steps,), name=name,
        in_specs=[_col_spec(k, tm)] + [_row_spec(tm, w_) for w_ in widths] + extra_specs,
        out_specs=_const_spec((NCHIP, k, nb)),
        out_shape=jax.ShapeDtypeStruct((NCHIP, k, nb), BF16),
        scratch_shapes=[pltpu.VMEM((NCHIP, k, nb), F32)],
        compiler_params=_cparams(("arbitrary",), VMEM_BIG),
    )(at_bf, *pieces, *extra)


def _out_proj_back(dh_ref, zt_ref, w_ref, dw_ref):
    dhb = dh_ref[...].astype(BF16)

    @pl.when(pl.program_id(0) == 0)
    def _():
        dw_ref[...] = jnp.zeros_like(dw_ref)

    dw_ref[...] += jnp.dot(zt_ref[...], dhb, preferred_element_type=F32)
    return lax.dot_general(dhb, w_ref[...], (((1,), (1,)), ((), ())), preferred_element_type=F32)


PREP_TM = 512
PREP_NB = SEQ // PREP_TM


def _prep_elem(k, wl, apre, kkw, kaw, bd):
    wraw = -_softplus(-wl) - 0.5
    lw = -jnp.exp(wraw)
    asig = jax.nn.sigmoid(apre)
    kkr = k * kkw
    nrm = jnp.maximum(jnp.sqrt(_headsum(kkr * kkr, bd)), 1e-12)
    kk = kkr / nrm
    k2 = k * (1.0 + (asig - 1.0) * kaw)
    return lw, k2, -kk, kk * asig


def _prep_elem_bwd(k, wl, apre, kkw, kaw, bd, dlw, dk2, daa, dbb):
    s = -wl
    sp = _softplus(s)
    dwl = dlw * (-jnp.exp(-sp - 0.5)) * jnp.exp(s - sp)
    asig = jax.nn.sigmoid(apre)
    kkr = k * kkw
    root = jnp.sqrt(_headsum(kkr * kkr, bd))
    inv = 1.0 / jnp.maximum(root, 1e-12)
    kk = kkr * inv
    dkk = dbb * asig - daa
    dap = (dbb * kk + dk2 * k * kaw) * asig * (1.0 - asig)
    through_norm = jnp.where(root > 1e-12, kk * _headsum(dkk * kkr, bd) * inv, 0.0)
    dkkr = inv * (dkk - through_norm)
    gain = 1.0 + (asig - 1.0) * kaw
    dk = dkkr * kkw + dk2 * gain
    dkkw = jnp.sum(dkkr * k, axis=0, keepdims=True)
    dkaw = jnp.sum(dk2 * k * (asig - 1.0), axis=0, keepdims=True)
    return dk, dwl, dap, dkkw, dkaw


def _shifted(ps_ref, prev_ref, mu, blk):
    p = ps_ref[...]
    first = (blk % PREP_NB) == 0
    prev_row = jnp.where(first, 0.0, prev_ref[7:8, :])
    rolled = pltpu.roll(p, 1, 0)
    p_prev = jnp.where(_iota2(p.shape, 0) == 0, prev_row, rolled)
    return p, p_prev, p + (p_prev - p) * mu


def _prev_spec(width, blk_of):
    return pl.BlockSpec((8, width), lambda i: (jnp.maximum(blk_of(i) * (PREP_TM // 8) - 1, 0), 0))


def even_prep(ps, mu, w0, w2x, a0, a2x, kkw, kaw):
    tm = PREP_TM

    def body(ps_ref, prev_ref, mu_ref, w0_ref, w2_ref, a0_ref, a2_ref, kk_ref, ka_ref,
             r_ref, lw_ref, k2_ref, v_ref, aa_ref, bb_ref):
        _, _, s = _shifted(ps_ref, prev_ref, mu_ref[...], pl.program_id(0))
        wa = s[:, 3 * W:]
        wl = w0_ref[...] + _bdot(jnp.tanh(wa), w2_ref[...])
        apre = a0_ref[...] + _bdot(wa, a2_ref[...])
        lw, k2, aa, bb = _prep_elem(s[:, W:2 * W], wl, apre, kk_ref[...], ka_ref[...], _head_blockdiag())
        r_ref[...] = s[:, 0:W]
        v_ref[...] = s[:, 2 * W:3 * W]
        lw_ref[...] = lw
        k2_ref[...] = k2
        aa_ref[...] = aa
        bb_ref[...] = bb

    vec = _const_spec((1, W))
    return pl.pallas_call(
        body, grid=(T // tm,), name="even_prep",
        in_specs=[_row_spec(tm, SHIFT), _prev_spec(SHIFT, lambda i: i), _const_spec((1, SHIFT)), vec,
                  _const_spec((2 * LORA, W)), vec, _const_spec((2 * LORA, W)), vec, vec],
        out_specs=[_row_spec(tm, W)] * 6,
        out_shape=[jax.ShapeDtypeStruct((T, W), F32)] * 6,
        compiler_params=_cparams(("parallel",), VMEM_BIG),
    )(ps, ps, mu, w0, w2x, a0, a2x, kkw, kaw)


def even_prep_bwd(ps, mu, w0, w2x, a0, a2x, kkw, kaw, dr, dlw, dk2, dv, daa, dbb, dr2, dk22, dv2):
    tm = PREP_TM
    nb = T // tm
    rev = lambda i: nb - 1 - i

    def body(ps_ref, prev_ref, mu_ref, w0_ref, w2_ref, a0_ref, a2_ref, kk_ref, ka_ref,
             dr_ref, dlw_ref, dk2_ref, dv_ref, daa_ref, dbb_ref, dr2_ref, dk22_ref, dv2_ref,
             dps_ref, dmu_ref, dw0_ref, dw2_ref, da0_ref, da2_ref, dkk_ref, dka_ref, carry):
        i = pl.program_id(0)
        blk = rev(i)
        mu_v = mu_ref[...]
        p, p_prev, s = _shifted(ps_ref, prev_ref, mu_v, blk)
        wa = s[:, 3 * W:]
        th = jnp.tanh(wa)
        wl = w0_ref[...] + _bdot(th, w2_ref[...])
        apre = a0_ref[...] + _bdot(wa, a2_ref[...])
        bd = _head_blockdiag()
        k = s[:, W:2 * W]
        dk, dwl, dap, dkkw, dkaw = _prep_elem_bwd(k, wl, apre, kk_ref[...], ka_ref[...], bd, dlw_ref[...],
                                                  dk2_ref[...] + dk22_ref[...], daa_ref[...], dbb_ref[...])
        dwa = _bdot_nt(dwl, w2_ref[...]) * (1.0 - th * th) + _bdot_nt(dap, a2_ref[...])
        ds = jnp.concatenate([dr_ref[...] + dr2_ref[...], dk, dv_ref[...] + dv2_ref[...], dwa], axis=-1)

        @pl.when(i == 0)
        def _():
            for ref in (dmu_ref, dw0_ref, dw2_ref, da0_ref, da2_ref, dkk_ref, dka_ref, carry):
                ref[...] = jnp.zeros_like(ref)

        dmu_ref[...] += jnp.sum(ds * (p_prev - p), axis=0, keepdims=True)
        dw0_ref[...] += jnp.sum(dwl, axis=0, keepdims=True)
        da0_ref[...] += jnp.sum(dap, axis=0, keepdims=True)
        dw2_ref[...] += _bdot_tn(th, dwl)
        da2_ref[...] += _bdot_tn(wa, dap)
        dkk_ref[...] += dkkw
        dka_ref[...] += dkaw
        dsm = ds * mu_v
        last = (blk % PREP_NB) == PREP_NB - 1
        nxt = jnp.where(last, 0.0, carry[0:1, :])
        up = pltpu.roll(dsm, tm - 1, 0)
        up = jnp.where(_iota2(up.shape, 0) == tm - 1, nxt, up)
        dps_ref[...] = (ds - dsm + up).astype(BF16)
        carry[0:1, :] = dsm[0:1, :]

    vec = _const_spec((1, W))
    rrow = lambda width: pl.BlockSpec((tm, width), lambda i: (rev(i), 0))
    return pl.pallas_call(
        body, grid=(nb,), name="even_prep_bwd",
        in_specs=[rrow(SHIFT), _prev_spec(SHIFT, rev), _const_spec((1, SHIFT)), vec,
                  _const_spec((2 * LORA, W)), vec, _const_spec((2 * LORA, W)), vec, vec] + [rrow(W)] * 9,
        out_specs=[rrow(SHIFT), _const_spec((1, SHIFT)), vec, _const_spec((2 * LORA, W)), vec,
                   _const_spec((2 * LORA, W)), vec, vec],
        out_shape=[jax.ShapeDtypeStruct((T, SHIFT), BF16), jax.ShapeDtypeStruct((1, SHIFT), F32),
                   jax.ShapeDtypeStruct((1, W), F32), jax.ShapeDtypeStruct((2 * LORA, W), F32),
                   jax.ShapeDtypeStruct((1, W), F32), jax.ShapeDtypeStruct((2 * LORA, W), F32),
                   jax.ShapeDtypeStruct((1, W), F32), jax.ShapeDtypeStruct((1, W), F32)],
        scratch_shapes=[pltpu.VMEM((8, SHIFT), F32)],
        compiler_params=_cparams(("arbitrary",), VMEM_BIG),
    )(ps, ps, mu, w0, w2x, a0, a2x, kkw, kaw, dr, dlw, dk2, dv, daa, dbb, dr2, dk22, dv2)


NPAIR = NH // 2
PW = 2 * HD


def _pair_cols(p):
    return slice(p * PW, (p + 1) * PW)


def _pairs(a):
    return [a[:, _pair_cols(p)] for p in range(NPAIR)]


def _stack_pair(a):
    first = _iota2(a.shape, 1) < HD
    zero = jnp.zeros_like(a)
    return jnp.concatenate([jnp.where(first, a, zero), jnp.where(first, zero, a)], axis=0)


def _unstack_pair(a):
    n = a.shape[0] // 2
    return jnp.where(_iota2((n, PW), 1) < HD, a[:n], a[n:])


def _fold_pair(a):
    n = a.shape[0] // 2
    return a[:n] + a[n:]


def _chunk_masks():
    n = 4 * L
    row = _iota2((n, n), 0)
    col = _iota2((n, n), 1)
    same = ((row // L) & 1) == ((col // L) & 1)
    ri = row & (L - 1)
    ci = col & (L - 1)
    keep = same & (((row < 2 * L) & (ri > ci)) | ((row >= 2 * L) & (ri >= ci)))
    r1 = _iota2((L, L), 0)
    c1 = _iota2((L, L), 1)
    r2 = _iota2((2 * L, 2 * L), 0)
    c2 = _iota2((2 * L, 2 * L), 1)
    return keep.astype(F32), (r1 >= c1).astype(F32), (r2 == c2).astype(F32)


def _scaled(r, lw, k2, aa, bb, tri):
    g = _hdot(tri, lw)
    eg = jnp.exp(g)
    eng = jnp.exp(-g)
    egp = jnp.exp(g - lw)
    return eg, eng, egp, aa * egp, r * eg, bb * eng, k2 * eng


def _head_cols(h):
    return slice(h * HD, (h + 1) * HD)


def _per_head(a):
    return [a[:, _head_cols(h)] for h in range(NH)]


def _pairs_operands(at, rt, bt, kt):
    x = [jnp.concatenate([_stack_pair(a), _stack_pair(r)], axis=0).astype(BF16) for a, r in zip(_pairs(at), _pairs(rt))]
    yk = [jnp.concatenate([_stack_pair(b), _stack_pair(k)], axis=0).astype(BF16) for b, k in zip(_pairs(bt), _pairs(kt))]
    return x, yk


def _pairs_matrices(x, yk, keep, eye):
    m = [_bdot_nt(a, b) * keep for a, b in zip(x, yk)]
    p = [a[:2 * L, :2 * L] for a in m]
    tinv = [eye + a for a in p]
    for _ in range(5):
        p = [_bdot(a, a) for a in p]
        tinv = [t + _bdot(t, a) for t, a in zip(tinv, p)]
    return [a.astype(BF16) for a in m], [a.astype(BF16) for a in tinv]


def _pairs_fwd(x, yk, m, tinv, vw, s0, egl):
    xh = [_bdot_nt(a, s) for a, s in zip(x, s0)]
    u = [_bdot(t, h[:2 * L] + _bdot(a[:2 * L, 2 * L:], w)) for t, h, a, w in zip(tinv, xh, m, vw)]
    uv = [jnp.concatenate([a, w], axis=0).astype(BF16) for a, w in zip(u, vw)]
    y = [h[2 * L:] + _bdot(a[2 * L:], w) for h, a, w in zip(xh, m, uv)]
    sn = [e * (s + _bdot_tn(w, b)) for e, s, w, b in zip(egl, s0, uv, yk)]
    return y, sn, uv


def _pairs_bwd(x, yk, m, tinv, uv, s0, sn, egl, dyw, dsn, keep):
    dzs = [d * e for d, e in zip(dsn, egl)]
    dgl = [jnp.sum(d * s, axis=0, keepdims=True) for d, s in zip(dsn, sn)]
    dyb = [a.astype(BF16) for a in dyw]
    t1 = [_bdot_tn(a[2 * L:], d) for a, d in zip(m, dyb)]
    t2 = [_bdot_nt(b, d) for b, d in zip(yk, dzs)]
    drhs = [_bdot_tn(t, a[:2 * L] + b[:2 * L]) for t, a, b in zip(tinv, t1, t2)]
    dv = [a[2 * L:] + b[2 * L:] + _bdot_tn(c[:2 * L, 2 * L:], d) for a, b, c, d in zip(t1, t2, m, drhs)]
    gg = [jnp.concatenate([a, b], axis=0).astype(BF16) for a, b in zip(drhs, dyw)]
    ds0 = [d + _bdot_tn(g, a) for d, g, a in zip(dzs, gg, x)]
    dm = [_bdot_nt(g, w) * keep for g, w in zip(gg, uv)]
    dx = [_bdot(g, s) + _bdot(d, b) for g, s, d, b in zip(gg, s0, dm, yk)]
    dyk = [_bdot_tn(d, a) + _bdot(w, z) for d, a, w, z in zip(dm, x, uv, dzs)]
    return dx, dyk, dv, dgl, ds0


STATE_SHAPE = (NPAIR * PW, PW)
M_SHAPE = (4 * L, NPAIR * 4 * L)
TINV_SHAPE = (2 * L, NPAIR * 2 * L)


def _rows_of(a, n):
    return [a[i * n:(i + 1) * n, :] for i in range(NPAIR)]


def _both(f):
    out = []
    for s in range(NSEQ):
        out += f(s)
    return out


def _seq_view(a):
    return a.reshape(NSEQ, SEQ, a.shape[-1])


UV_SHAPE = (4 * L, NPAIR * PW)
RW_CHUNKS = 2


def rwkv_fwd(r, lw, k2, v, aa, bb):
    def body(r_ref, lw_ref, k2_ref, v_ref, aa_ref, bb_ref, y_ref, hs_ref, hn_ref, m_ref, t_ref, uv_ref, state):
        @pl.when(pl.program_id(0) == 0)
        def _():
            state[...] = jnp.zeros_like(state)

        keep, tri, eye = _chunk_masks()
        where = [(j, s) for j in range(RW_CHUNKS) for s in range(NSEQ)]
        rows = lambda j: slice(j * L, (j + 1) * L)
        sc = [_scaled(r_ref[s, rows(j)], lw_ref[s, rows(j)], k2_ref[s, rows(j)], aa_ref[s, rows(j)],
                      bb_ref[s, rows(j)], tri) for j, s in where]
        ops = [_pairs_operands(*a[3:]) for a in sc]
        m, tinv = _pairs_matrices([a for o in ops for a in o[0]], [a for o in ops for a in o[1]], keep, eye)
        s_cur = [state[s] for s in range(NSEQ)]
        for j in range(RW_CHUNKS):
            mine = slice(j * NSEQ * NPAIR, (j + 1) * NSEQ * NPAIR)
            x = [a for o in ops[j * NSEQ:(j + 1) * NSEQ] for a in o[0]]
            yk = [a for o in ops[j * NSEQ:(j + 1) * NSEQ] for a in o[1]]
            vw = _both(lambda s: [_stack_pair(a) for a in _pairs(v_ref[s, rows(j)])])
            egl = _both(lambda s: _pairs(sc[j * NSEQ + s][0][L - 1:L, :]))
            y, sn, uv = _pairs_fwd(x, yk, m[mine], tinv[mine], vw, _both(lambda s: _rows_of(s_cur[s], PW)), egl)
            for s in range(NSEQ):
                ps = slice(s * NPAIR, (s + 1) * NPAIR)
                hs_ref[j, s] = s_cur[s]
                y_ref[s, rows(j)] = jnp.concatenate([_fold_pair(a) for a in y[ps]], axis=-1)
                m_ref[j, s] = jnp.concatenate(m[mine][ps], axis=-1)
                t_ref[j, s] = jnp.concatenate(tinv[mine][ps], axis=-1)
                uv_ref[j, s] = jnp.concatenate(uv[ps], axis=-1)
                s_cur[s] = jnp.concatenate(sn[ps], axis=0)
                hn_ref[j, s] = s_cur[s]
        for s in range(NSEQ):
            state[s] = s_cur[s]

    blk = pl.BlockSpec((NSEQ, RW_CHUNKS * L, W), lambda c: (0, c, 0))
    per_chunk = lambda shape: pl.BlockSpec((RW_CHUNKS, NSEQ) + shape, lambda c: (c, 0, 0, 0))
    saved_shapes = [(STATE_SHAPE, F32), (STATE_SHAPE, F32), (M_SHAPE, BF16), (TINV_SHAPE, BF16), (UV_SHAPE, BF16)]
    y, *saved = pl.pallas_call(
        body, grid=(NC // RW_CHUNKS,), name="rwkv_fwd",
        in_specs=[blk] * 6,
        out_specs=[blk] + [per_chunk(shape) for shape, _ in saved_shapes],
        out_shape=[jax.ShapeDtypeStruct((NSEQ, SEQ, W), F32)]
        + [jax.ShapeDtypeStruct((NC, NSEQ) + shape, dt) for shape, dt in saved_shapes],
        scratch_shapes=[pltpu.VMEM((NSEQ,) + STATE_SHAPE, F32)],
        compiler_params=_cparams(("arbitrary",), VMEM_BIG),
    )(*[_seq_view(a) for a in (r, lw, k2, v, aa, bb)])
    return y.reshape(T, W), saved


def rwkv_bwd(r, lw, k2, aa, bb, saved, dy):
    def body(r_ref, lw_ref, k2_ref, aa_ref, bb_ref, hs_ref, hn_ref, m_ref, t_ref, uv_ref, dy_ref,
             dr_ref, dlw_ref, dk2_ref, dv_ref, daa_ref, dbb_ref, dstate):
        @pl.when(pl.program_id(0) == 0)
        def _():
            dstate[...] = jnp.zeros_like(dstate)

        keep, tri, _ = _chunk_masks()
        sc = [_scaled(r_ref[s], lw_ref[s], k2_ref[s], aa_ref[s], bb_ref[s], tri) for s in range(NSEQ)]
        ops = [_pairs_operands(*sc[s][3:]) for s in range(NSEQ)]
        x, yk = _both(lambda s: ops[s][0]), _both(lambda s: ops[s][1])
        m = _both(lambda s: [m_ref[0, s][:, i * 4 * L:(i + 1) * 4 * L] for i in range(NPAIR)])
        tinv = _both(lambda s: [t_ref[0, s][:, i * 2 * L:(i + 1) * 2 * L] for i in range(NPAIR)])
        uv = _both(lambda s: _pairs(uv_ref[0, s]))
        dyw = _both(lambda s: [_stack_pair(a) for a in _pairs(dy_ref[s])])
        s0 = _both(lambda s: _rows_of(hs_ref[0, s], PW))
        sn = _both(lambda s: _rows_of(hn_ref[0, s], PW))
        dsn = _both(lambda s: _rows_of(dstate[s], PW))
        egl = _both(lambda s: _pairs(sc[s][0][L - 1:L, :]))
        dx, dyk, dvw, dgl, ds0 = _pairs_bwd(x, yk, m, tinv, uv, s0, sn, egl, dyw, dsn, keep)
        for s in range(NSEQ):
            mine = slice(s * NPAIR, (s + 1) * NPAIR)
            eg, eng, egp, at, rt, bt, kt = sc[s]
            dstate[s] = jnp.concatenate(ds0[mine], axis=0)
            dv_ref[s] = jnp.concatenate([_fold_pair(a) for a in dvw[mine]], axis=-1)
            dat = jnp.concatenate([_fold_pair(a[:2 * L]) for a in dx[mine]], axis=-1)
            drt = jnp.concatenate([_fold_pair(a[2 * L:]) for a in dx[mine]], axis=-1)
            dbt = jnp.concatenate([_fold_pair(a[:2 * L]) for a in dyk[mine]], axis=-1)
            dkt = jnp.concatenate([_fold_pair(a[2 * L:]) for a in dyk[mine]], axis=-1)
            dg = drt * rt - dbt * bt - dkt * kt
            dg = dg + jnp.where(_iota2(dg.shape, 0) == L - 1, jnp.concatenate(dgl[mine], axis=-1), 0.0)
            dgp = dat * at
            dlw_ref[s] = _hdot_tn(tri, dg + dgp) - dgp
            dr_ref[s] = drt * eg
            daa_ref[s] = dat * egp
            dbb_ref[s] = dbt * eng
            dk2_ref[s] = dkt * eng

    blk = pl.BlockSpec((NSEQ, L, W), lambda c: (0, NC - 1 - c, 0))
    per_chunk = lambda shape: pl.BlockSpec((1, NSEQ) + shape, lambda c: (NC - 1 - c, 0, 0, 0))
    outs = pl.pallas_call(
        body, grid=(NC,), name="rwkv_bwd",
        in_specs=[blk] * 5 + [per_chunk(a.shape[2:]) for a in saved] + [blk],
        out_specs=[blk] * 6,
        out_shape=[jax.ShapeDtypeStruct((NSEQ, SEQ, W), F32)] * 6,
        scratch_shapes=[pltpu.VMEM((NSEQ,) + STATE_SHAPE, F32)],
        compiler_params=_cparams(("arbitrary",)),
    )(*[_seq_view(a) for a in (r, lw, k2, aa, bb)], *saved, _seq_view(dy))
    return [a.reshape(T, W) for a in outs]


def _post_math(y, r, k2, v, ga, o, gb, lng, lnb, rk, bd):
    mu = _headsum(y, bd) * (1.0 / HD)
    yc = y - mu
    var = _headsum(yc * yc, bd) * (1.0 / HD)
    yn = yc * lax.rsqrt(var + GN_EPS) * lng + lnb
    bonus = _headsum(r * k2 * rk, bd) * v
    return (yn + bonus) * _silu(ga), o * _silu(gb)


def even_post(y, r, k2, v, ga, o, gb, lng, lnb, rk, h, w_bf):
    tm = 512

    def body(y_ref, r_ref, k2_ref, v_ref, ga_ref, o_ref, gb_ref, lng_ref, lnb_ref, rk_ref, h_ref, w_ref,
             ho_ref, zt_ref):
        ya, yb = _post_math(y_ref[...], r_ref[...], k2_ref[...], v_ref[...], ga_ref[...], o_ref[...], gb_ref[...],
                            lng_ref[...], lnb_ref[...], rk_ref[...], _head_blockdiag())
        z = jnp.concatenate([ya.astype(BF16), yb.astype(BF16)], axis=-1)
        zt_ref[...] = z.T
        ho_ref[...] = h_ref[...] + jnp.dot(z, w_ref[...], preferred_element_type=F32)

    vec = _const_spec((1, W))
    return pl.pallas_call(
        body, grid=(T // tm,), name="even_post",
        in_specs=[_row_spec(tm, W)] * 7 + [vec] * 3 + [_row_spec(tm, D), _const_spec((D, D))],
        out_specs=[_row_spec(tm, D), _col_spec(D, tm)],
        out_shape=[jax.ShapeDtypeStruct((T, D), F32), jax.ShapeDtypeStruct((D, T), BF16)],
        compiler_params=_cparams(("parallel",), VMEM_BIG),
    )(y, r, k2, v, ga, o, gb, lng, lnb, rk, h, w_bf)


def even_post_bwd(y, r, k2, v, ga, o, gb, lng, lnb, rk, dh, zt_bf, w_bf, after=None):
    tm = 512
    extra_specs, extra = _after_operand(after)

    def body(y_ref, r_ref, k2_ref, v_ref, ga_ref, o_ref, gb_ref, lng_ref, lnb_ref, rk_ref, dh_ref, zt_ref, w_ref,
             *rest):
        dy_ref, dr_ref, dk2_ref, dv_ref, dga_ref, do_ref, dgb_ref, dlng_ref, dlnb_ref, drk_ref, dw_ref = rest[-11:]
        dzv = _out_proj_back(dh_ref, zt_ref, w_ref, dw_ref)
        bd = _head_blockdiag()
        _, vjp = jax.vjp(lambda *a: _post_math(*a, bd), y_ref[...], r_ref[...], k2_ref[...], v_ref[...], ga_ref[...],
                         o_ref[...], gb_ref[...], lng_ref[...], lnb_ref[...], rk_ref[...])
        dy, dr, dk2, dv, dga, do, dgb, dlng, dlnb, drk = vjp((dzv[:, 0:W], dzv[:, W:2 * W]))
        for ref, val in ((dy_ref, dy), (dr_ref, dr), (dk2_ref, dk2), (dv_ref, dv), (dga_ref, dga), (do_ref, do),
                         (dgb_ref, dgb)):
            ref[...] = val.astype(ref.dtype)

        @pl.when(pl.program_id(0) == 0)
        def _():
            for ref in (dlng_ref, dlnb_ref, drk_ref):
                ref[...] = jnp.zeros_like(ref)

        dlng_ref[...] += dlng
        dlnb_ref[...] += dlnb
        drk_ref[...] += drk

    vec = _const_spec((1, W))
    return pl.pallas_call(
        body, grid=(T // tm,), name="even_post_bwd",
        in_specs=[_row_spec(tm, W)] * 7 + [vec] * 3 + [_row_spec(tm, D), _col_spec(D, tm), _const_spec((D, D))]
        + extra_specs,
        out_specs=[_row_spec(tm, W)] * 7 + [vec] * 3 + [_const_spec((D, D))],
        out_shape=[jax.ShapeDtypeStruct((T, W), dt) for dt in (F32, F32, F32, F32, BF16, F32, BF16)]
        + [jax.ShapeDtypeStruct((1, W), F32)] * 3 + [jax.ShapeDtypeStruct((D, D), F32)],
        compiler_params=_cparams(("arbitrary",), VMEM_BIG),
    )(y, r, k2, v, ga, o, gb, lng, lnb, rk, dh, zt_bf, w_bf, *extra)


PADSEQ = SEQ + LEFT * L
ATT_SCALE = 1.0 / math.sqrt(HD)
ATT_Q = 4
WIN = BAND + (ATT_Q - 1) * L
ATT_STEPS = NC // ATT_Q
ATT_BIAS_SHAPE = (NPAIR, ATT_Q * 2 * L, WIN)


def _stack_chunks(a):
    return jnp.concatenate([_stack_pair(a[i * L:(i + 1) * L]) for i in range(ATT_Q)], axis=0)


def _unstack_chunks(a):
    return jnp.concatenate([_unstack_pair(a[i * 2 * L:(i + 1) * 2 * L]) for i in range(ATT_Q)], axis=0)


def window_bias(bias):
    parts = [jnp.pad(bias, ((0, 0), (0, 0), (i * L, (ATT_Q - 1 - i) * L)), constant_values=NEG) for i in range(ATT_Q)]
    return jnp.concatenate(parts, axis=1)


def _att_probs(q2, kw, bias, step):
    valid = _iota2((1, WIN), 1) >= (LEFT - step * ATT_Q) * L
    s = [jnp.where(valid, _bdot_nt(a, b) * ATT_SCALE + bias[p], NEG) for p, (a, b) in enumerate(zip(q2, kw))]
    e = [jnp.exp(a - jnp.max(a, axis=-1, keepdims=True)) for a in s]
    return [a / jnp.sum(a, axis=-1, keepdims=True) for a in e]


def attention_fwd(q, kpad, vpad, bias):
    def body(q_ref, k_ref, v_ref, b_ref, o_ref):
        step = pl.program_id(1)
        start = pl.multiple_of(step * (ATT_Q * L), L)
        kw = _pairs(k_ref[pl.ds(start, WIN), :])
        vw = _pairs(v_ref[pl.ds(start, WIN), :])
        q2 = [_stack_chunks(a) for a in _pairs(q_ref[...].astype(BF16))]
        p = _att_probs(q2, kw, b_ref[...], step)
        o_ref[...] = jnp.concatenate([_unstack_chunks(_bdot(a, b)) for a, b in zip(p, vw)], axis=-1)

    qblk = pl.BlockSpec((ATT_Q * L, W), lambda b, c: (b * ATT_STEPS + c, 0))
    kblk = pl.BlockSpec((PADSEQ, W), lambda b, c: (b, 0))
    return pl.pallas_call(
        body, grid=(NSEQ, ATT_STEPS), name="attention_fwd",
        in_specs=[qblk, kblk, kblk, _const_spec(ATT_BIAS_SHAPE)],
        out_specs=qblk, out_shape=jax.ShapeDtypeStruct((T, W), F32),
        compiler_params=_cparams(("parallel", "arbitrary")),
    )(q, kpad, vpad, bias)


def attention_bwd(q, kpad, vpad, bias, do):
    def body(q_ref, k_ref, v_ref, b_ref, do_ref, dq_ref, dko_ref, dvo_ref, db_ref, dk_ref, dv_ref):
        b = pl.program_id(0)
        c = pl.program_id(1)

        @pl.when(c == 0)
        def _():
            dk_ref[...] = jnp.zeros_like(dk_ref)
            dv_ref[...] = jnp.zeros_like(dv_ref)

        @pl.when((c == 0) & (b == 0))
        def _():
            db_ref[...] = jnp.zeros_like(db_ref)

        start = pl.multiple_of(c * (ATT_Q * L), L)
        kw = _pairs(k_ref[pl.ds(start, WIN), :])
        vw = _pairs(v_ref[pl.ds(start, WIN), :])
        q2 = [_stack_chunks(a) for a in _pairs(q_ref[...].astype(BF16))]
        do2 = [_stack_chunks(a) for a in _pairs(do_ref[...].astype(BF16))]
        p = _att_probs(q2, kw, b_ref[...], c)
        dp = [_bdot_nt(a, b) for a, b in zip(do2, vw)]
        ds = [a * (d - jnp.sum(d * a, axis=-1, keepdims=True)) for a, d in zip(p, dp)]
        dss = [(a * ATT_SCALE).astype(BF16) for a in ds]
        dq_ref[...] = jnp.concatenate([_unstack_chunks(_bdot(a, b)) for a, b in zip(dss, kw)], axis=-1).astype(BF16)
        dk_ref[pl.ds(start, WIN), :] += jnp.concatenate([_bdot_tn(a, b) for a, b in zip(dss, q2)], axis=-1)
        dv_ref[pl.ds(start, WIN), :] += jnp.concatenate([_bdot_tn(a, b) for a, b in zip(p, do2)], axis=-1)
        for i in range(NPAIR):
            db_ref[i] += ds[i]

        @pl.when(c == ATT_STEPS - 1)
        def _():
            dko_ref[...] = dk_ref[LEFT * L:, :].astype(BF16)
            dvo_ref[...] = dv_ref[LEFT * L:, :].astype(BF16)

    qblk = pl.BlockSpec((ATT_Q * L, W), lambda b, c: (b * ATT_STEPS + c, 0))
    kblk = pl.BlockSpec((PADSEQ, W), lambda b, c: (b, 0))
    sblk = pl.BlockSpec((SEQ, W), lambda b, c: (b, 0))
    bblk = _const_spec(ATT_BIAS_SHAPE)
    return pl.pallas_call(
        body, grid=(NSEQ, ATT_STEPS), name="attention_bwd",
        in_specs=[qblk, kblk, kblk, bblk, qblk],
        out_specs=[qblk, sblk, sblk, bblk],
        out_shape=[jax.ShapeDtypeStruct((T, W), BF16), jax.ShapeDtypeStruct((T, W), BF16),
                   jax.ShapeDtypeStruct((T, W), BF16), jax.ShapeDtypeStruct(ATT_BIAS_SHAPE, F32)],
        scratch_shapes=[pltpu.VMEM((PADSEQ, W), F32), pltpu.VMEM((PADSEQ, W), F32)],
        compiler_params=_cparams(("arbitrary", "arbitrary"), VMEM_BIG),
    )(q, kpad, vpad, bias, do)


NTAB = 2 * CLIP + 1
EXT = BAND + L


def _ext_onehot():
    n = _iota2((EXT, NTAB), 0)
    m = _iota2((EXT, NTAB), 1)
    return (jnp.clip(BAND - 1 - n, -CLIP, CLIP) + CLIP == m).astype(F32)


def bias_expand(table):
    def body(t_ref, o_ref):
        ext = _hdot_nt(t_ref[...], _ext_onehot())
        for i in range(L):
            s = L - 1 - i
            o_ref[:, i, :] = (pltpu.roll(ext, EXT - s, 1) if s else ext)[:, :BAND]

    return pl.pallas_call(body, name="bias_expand", out_shape=jax.ShapeDtypeStruct((NH, L, BAND), F32))(table)


def bias_grad(dbias):
    def body(d_ref, o_ref):
        acc = jnp.zeros((NH, EXT), F32)
        zpad = jnp.zeros((NH, EXT - BAND), F32)
        for i in range(L):
            s = L - 1 - i
            row = jnp.concatenate([d_ref[:, i, :], zpad], axis=-1)
            acc = acc + (pltpu.roll(row, s, 1) if s else row)
        o_ref[...] = _hdot(acc, _ext_onehot())

    return pl.pallas_call(body, name="bias_grad", out_shape=jax.ShapeDtypeStruct((NH, NTAB), F32))(dbias)


def _group_cols(g):
    return slice(g * SGC, (g + 1) * SGC)


def _sg_norm(gv, lng, lnb):
    gc = gv - jnp.mean(gv, axis=-1, keepdims=True)
    rstd = lax.rsqrt(jnp.mean(gc * gc, axis=-1, keepdims=True) + LN_EPS)
    xhat = gc * rstd
    return xhat, rstd, xhat * lng + lnb


GMLP_BWD_CHUNKS = 2


def gmlp_fwd_loss(u, v, gate, lng, lnb, wm_bf, sgb_t, h, w_bf, g_final, target):
    tm = GMLP_BWD_CHUNKS * SGC

    def body(u_ref, v_ref, gt_ref, lng_ref, lnb_ref, wm_ref, sb_ref, h_ref, w_ref, g_ref, t_ref,
             dh_ref, loss_ref, dg_ref, zt_ref):
        zs = []
        for ch in range(GMLP_BWD_CHUNKS):
            rows = slice(ch * SGC, (ch + 1) * SGC)
            _, _, vln = _sg_norm(_gelu(v_ref[rows, :]), lng_ref[...], lnb_ref[...])
            vlb = vln.astype(BF16)
            zg = []
            for g in range(NG):
                cs = _group_cols(g)
                sv = jnp.dot(wm_ref[g], vlb[:, cs], preferred_element_type=F32) + sb_ref[:, g:g + 1]
                zg.append((_gelu(u_ref[rows, cs]) * sv * _silu(gt_ref[rows, cs])).astype(BF16))
            zs.append(jnp.concatenate(zg, axis=-1))
        z = jnp.concatenate(zs, axis=0)
        zt_ref[...] = z.T
        xv = h_ref[...] + jnp.dot(z, w_ref[...], preferred_element_type=F32)
        rstd = lax.rsqrt(jnp.mean(xv * xv, axis=-1, keepdims=True) + RMS_EPS)
        xhat = xv * rstd
        err = xhat * g_ref[...] - t_ref[...]
        part = 0.5 * jnp.sum(jnp.mean(err * err, axis=-1, keepdims=True), axis=0, keepdims=True)
        dout = err * (1.0 / D)

        @pl.when(pl.program_id(0) == 0)
        def _():
            loss_ref[...] = jnp.zeros_like(loss_ref)
            dg_ref[...] = jnp.zeros_like(dg_ref)

        loss_ref[...] += jnp.broadcast_to(part, loss_ref.shape)
        dg_ref[...] += jnp.sum(dout * xhat, axis=0, keepdims=True)
        dxh = dout * g_ref[...]
        dh_ref[...] = rstd * (dxh - xhat * jnp.mean(dxh * xhat, axis=-1, keepdims=True))

    return pl.pallas_call(
        body, grid=(T // tm,), name="gmlp_fwd_loss",
        in_specs=[_row_spec(tm, D)] * 3 + [_const_spec((1, D))] * 2
        + [_const_spec((NG, SGC, SGC)), _const_spec((SGC, NG)), _row_spec(tm, D), _const_spec((D, D)),
           _const_spec((1, D)), _row_spec(tm, D)],
        out_specs=[_row_spec(tm, D), _const_spec((8, 128)), _const_spec((1, D)), _col_spec(D, tm)],
        out_shape=[jax.ShapeDtypeStruct((T, D), F32), jax.ShapeDtypeStruct((8, 128), F32),
                   jax.ShapeDtypeStruct((1, D), F32), jax.ShapeDtypeStruct((D, T), BF16)],
        compiler_params=_cparams(("arbitrary",), VMEM_BIG),
    )(u, v, gate, lng, lnb, wm_bf, sgb_t, h, w_bf, g_final, target)


def gmlp_bwd(u, v, gate, lng, lnb, wm_bf, sgb_t, dh, zt_bf, w_bf):
    def body(u_ref, v_ref, gt_ref, lng_ref, lnb_ref, wm_ref, sb_ref, dh_ref, zt_ref, w_ref,
             du_ref, dv_ref, dgt_ref, dlng_ref, dlnb_ref, dwm_ref, dsb_ref, dw_ref):
        @pl.when(pl.program_id(0) == 0)
        def _():
            for ref in (dlng_ref, dlnb_ref, dwm_ref, dsb_ref):
                ref[...] = jnp.zeros_like(ref)

        dz = _out_proj_back(dh_ref, zt_ref, w_ref, dw_ref)
        sel = (_iota2((D, NG), 0) // SGC == _iota2((D, NG), 1)).astype(F32)
        for ch in range(GMLP_BWD_CHUNKS):
            rows = slice(ch * SGC, (ch + 1) * SGC)
            gv, dgv_dv = _gelu_both(v_ref[rows, :])
            xhat, rstd, vln = _sg_norm(gv, lng_ref[...], lnb_ref[...])
            vlb = vln.astype(BF16)
            dvln = []
            dsv_all = []
            for g in range(NG):
                cs = _group_cols(g)
                uu = u_ref[rows, cs]
                gg = gt_ref[rows, cs]
                dzz = dz[rows, cs]
                sv = jnp.dot(wm_ref[g], vlb[:, cs], preferred_element_type=F32) + sb_ref[:, g:g + 1]
                gu, dgu = _gelu_both(uu)
                sg, dsg = _silu_both(gg)
                dzgu = dzz * gu
                dsv = dzgu * sg
                dgt_ref[rows, cs] = (dzgu * sv * dsg).astype(BF16)
                du_ref[rows, cs] = (dzz * sv * sg * dgu).astype(BF16)
                dsb16 = dsv.astype(BF16)
                dvln.append(lax.dot_general(wm_ref[g], dsb16, (((0,), (0,)), ((), ())), preferred_element_type=F32))
                dwm_ref[g] += lax.dot_general(dsb16, vlb[:, cs], (((1,), (1,)), ((), ())),
                                              preferred_element_type=F32)
                dsv_all.append(dsv)
            dvl = jnp.concatenate(dvln, axis=-1)
            dsb_ref[...] += _hdot(jnp.concatenate(dsv_all, axis=-1), sel)
            dlng_ref[...] += jnp.sum(dvl * xhat, axis=0, keepdims=True)
            dlnb_ref[...] += jnp.sum(dvl, axis=0, keepdims=True)
            dxh = dvl * lng_ref[...]
            dgv = rstd * (dxh - jnp.mean(dxh, axis=-1, keepdims=True)
                          - xhat * jnp.mean(dxh * xhat, axis=-1, keepdims=True))
            dv_ref[rows, :] = (dgv * dgv_dv).astype(BF16)

    tm = GMLP_BWD_CHUNKS * SGC
    return pl.pallas_call(
        body, grid=(T // tm,), name="gmlp_bwd",
        in_specs=[_row_spec(tm, D)] * 3 + [_const_spec((1, D))] * 2
        + [_const_spec((NG, SGC, SGC)), _const_spec((SGC, NG)), _row_spec(tm, D), _col_spec(D, tm),
           _const_spec((D, D))],
        out_specs=[_row_spec(tm, D)] * 3 + [_const_spec((1, D))] * 2
        + [_const_spec((NG, SGC, SGC)), _const_spec((SGC, NG)), _const_spec((D, D))],
        out_shape=[jax.ShapeDtypeStruct((T, D), BF16)] * 3 + [jax.ShapeDtypeStruct((1, D), F32)] * 2
        + [jax.ShapeDtypeStruct((NG, SGC, SGC), F32), jax.ShapeDtypeStruct((SGC, NG), F32),
           jax.ShapeDtypeStruct((D, D), F32)],
        compiler_params=_cparams(("arbitrary",), VMEM_BIG),
    )(u, v, gate, lng, lnb, wm_bf, sgb_t, dh, zt_bf, w_bf)


NCHIP = 4
NDEV = 8
ANY = pl.BlockSpec(memory_space=pl.ANY)


HBM = pl.BlockSpec(memory_space=pltpu.HBM)
SEM = pl.BlockSpec(memory_space=pltpu.SEMAPHORE)
EFFECT = pltpu.SideEffectType.DATAFLOW_SIDE_EFFECTING


def _peers(whole_mesh):
    x, y, c = lax.axis_index("x"), lax.axis_index("y"), lax.axis_index("c")
    if not whole_mesh:
        return [((px, py, c), 2 * px + py) for px, py in ((1 - x, y), (x, 1 - y), (1 - x, 1 - y))], 2 * x + y
    out = []
    for j in range(1, NDEV):
        px, py, pc = x ^ (j >> 2), y ^ ((j >> 1) & 1), c ^ (j & 1)
        out.append(((px, py, pc), 4 * px + 2 * py + pc))
    return out, 4 * x + 2 * y + c


def _send_copies(src, land, send, recv, scatter, whole_mesh, starting):
    peers, me = _peers(whole_mesh)
    copies = []
    for t in range(len(src)):
        for j, (dev, slot) in enumerate(peers):
            k = t * len(peers) + j
            copies.append(pltpu.make_async_remote_copy(
                src_ref=src[t].at[slot] if scatter else src[t], dst_ref=land[t].at[me if starting else slot],
                send_sem=send.at[k], recv_sem=recv.at[k], device_id=dev, device_id_type=MESH))
    return copies


def send_start(srcs, lands, scatter, whole_mesh, name, after=None):
    n = len(srcs)
    nsem = n * (NDEV - 1 if whole_mesh else NCHIP - 1)
    extra_specs, extra = _after_operand(after)

    def body(*refs):
        send, recv = refs[2 * n + len(extra)], refs[2 * n + len(extra) + 1]
        for cp in _send_copies(refs[:n], refs[n:2 * n], send, recv, scatter, whole_mesh, True):
            cp.start()
        refs[-1][...] = jnp.zeros_like(refs[-1])

    arrs = list(srcs) + list(lands)
    out = pl.pallas_call(
        body, name=name,
        out_shape=(pltpu.SemaphoreType.DMA((nsem,)), pltpu.SemaphoreType.DMA((nsem,)),
                   *[pltpu.HBM(a.shape, a.dtype) for a in arrs], jax.ShapeDtypeStruct((8, 128), F32)),
        in_specs=[HBM] * (2 * n) + extra_specs,
        out_specs=(SEM, SEM, *[HBM] * (2 * n), pl.BlockSpec(memory_space=pltpu.VMEM)),
        input_output_aliases={i: 2 + i for i in range(2 * n)},
        compiler_params=pltpu.CompilerParams(has_side_effects=EFFECT),
    )(*[pltpu.with_memory_space_constraint(a, pltpu.HBM) for a in arrs], *extra)
    return out[0], out[1], list(out[2:2 + n]), list(out[2 + n:2 + 2 * n]), out[-1]


def send_wait(started, after, scatter, whole_mesh, name):
    send, recv, srcs, lands, _ = started
    n = len(srcs)

    def body(*refs):
        for cp in _send_copies(refs[:n], refs[n:2 * n], refs[2 * n], refs[2 * n + 1], scatter, whole_mesh, False):
            cp.wait_send()
            cp.wait_recv()

    arrs = list(srcs) + list(lands)
    out = pl.pallas_call(
        body, name=name, out_shape=tuple(pltpu.HBM(a.shape, a.dtype) for a in arrs),
        in_specs=[HBM] * (2 * n) + [SEM, SEM, ANY], out_specs=tuple([HBM] * (2 * n)),
        input_output_aliases={i: i for i in range(2 * n)},
        compiler_params=pltpu.CompilerParams(has_side_effects=EFFECT),
    )(*arrs, send, recv, after)
    return list(out[n:])


def exchange_c(arrs, name):
    n = len(arrs)

    def body(*refs):
        ins, outs = refs[:n], refs[n:2 * n]
        send, recv = refs[2 * n:]
        sibling = (lax.axis_index("x"), lax.axis_index("y"), 1 - lax.axis_index("c"))
        copies = [pltpu.make_async_remote_copy(src_ref=ins[t], dst_ref=outs[t], send_sem=send.at[t], recv_sem=recv.at[t],
                                               device_id=sibling, device_id_type=MESH) for t in range(n)]
        for cp in copies:
            cp.start()
        for cp in copies:
            cp.wait()

    return pl.pallas_call(
        body, name=name, in_specs=[ANY] * n, out_specs=[ANY] * n,
        out_shape=[jax.ShapeDtypeStruct(a.shape, a.dtype) for a in arrs],
        scratch_shapes=[pltpu.SemaphoreType.DMA((n,)), pltpu.SemaphoreType.DMA((n,))],
    )(*arrs)


def swap_row_halves(a, name):
    n, rows, cols = a.shape
    half = rows // 2

    def body(in_ref, out_ref, send, recv):
        x, y, c = lax.axis_index("x"), lax.axis_index("y"), lax.axis_index("c")
        cp = pltpu.make_async_remote_copy(src_ref=in_ref.at[:, pl.ds((1 - c) * half, half), :], dst_ref=out_ref,
                                          send_sem=send, recv_sem=recv, device_id=(x, y, 1 - c), device_id_type=MESH)
        cp.start()
        cp.wait()

    return pl.pallas_call(
        body, name=name, in_specs=[ANY], out_specs=ANY, out_shape=jax.ShapeDtypeStruct((n, half, cols), a.dtype),
        scratch_shapes=[pltpu.SemaphoreType.DMA, pltpu.SemaphoreType.DMA],
    )(a)


def add_blocks(a, b, name):
    n, rows, cols = a.shape
    tr = _rows_tile(rows)

    def body(a_ref, b_ref, o_ref):
        o_ref[...] = (a_ref[...].astype(F32) + b_ref[...].astype(F32)).astype(BF16)

    spec = pl.BlockSpec((1, tr, cols), lambda s, i: (s, i, 0))
    return pl.pallas_call(
        body, grid=(n, rows // tr), name=name, in_specs=[spec, spec], out_specs=spec,
        out_shape=jax.ShapeDtypeStruct(a.shape, BF16), compiler_params=_cparams(("parallel", "parallel")),
    )(a, b)


def gather_weights(arrs, split):
    n = len(arrs)

    def body(*refs):
        ins, outs = refs[:n], refs[n:2 * n]
        send1, recv1, send2, recv2, loc = refs[2 * n:]
        x, y, c = lax.axis_index("x"), lax.axis_index("y"), lax.axis_index("c")
        me = 2 * x + y
        sibling = (x, y, 1 - c)
        peers = [(1 - x, y), (x, 1 - y), (1 - x, 1 - y)]

        def rows_of(t, core):
            half = arrs[t].shape[0] // 2
            return pl.ds(core * half, half)

        def part(ref, t, core):
            return ref.at[rows_of(t, core)] if split[t] else ref

        local = [pltpu.make_async_copy(ins[t], outs[t].at[me], loc.at[t]) for t in range(n)]
        for cp in local:
            cp.start()
        first = []
        for t in range(n):
            for j, (px, py) in enumerate(peers):
                first.append(pltpu.make_async_remote_copy(
                    src_ref=part(ins[t], t, c), dst_ref=part(outs[t].at[me], t, c), send_sem=send1.at[t, j],
                    recv_sem=recv1.at[t, j], device_id=(px, py, c), device_id_type=MESH))
        for cp in first:
            cp.start()
        passed = []
        for t in range(n):
            for j, (px, py) in enumerate(peers):
                landed = part(outs[t].at[2 * px + py], t, c)
                pltpu.make_async_remote_copy(
                    src_ref=landed, dst_ref=landed, send_sem=send1.at[t, j], recv_sem=recv1.at[t, j],
                    device_id=(x, y, c), device_id_type=MESH).wait_recv()
                if split[t]:
                    cp = pltpu.make_async_remote_copy(
                        src_ref=landed, dst_ref=landed, send_sem=send2.at[t, j], recv_sem=recv2.at[t, j],
                        device_id=sibling, device_id_type=MESH)
                    cp.start()
                    passed.append(cp)
        for t in range(n):
            for j, (px, py) in enumerate(peers):
                if split[t]:
                    other = part(outs[t].at[2 * px + py], t, 1 - c)
                    pltpu.make_async_remote_copy(
                        src_ref=other, dst_ref=other, send_sem=send2.at[t, j], recv_sem=recv2.at[t, j],
                        device_id=(x, y, c), device_id_type=MESH).wait_recv()
        for cp in first + passed:
            cp.wait_send()
        for cp in local:
            cp.wait()

    return pl.pallas_call(
        body, name="gather_weights", in_specs=[ANY] * n, out_specs=[ANY] * n,
        out_shape=[jax.ShapeDtypeStruct((NCHIP,) + a.shape, a.dtype) for a in arrs],
        scratch_shapes=[pltpu.SemaphoreType.DMA((n, 3))] * 4 + [pltpu.SemaphoreType.DMA((n,))],
    )(*arrs)


def _adam_math(g, w, m, v):
    m = ADAM_B1 * m + (1.0 - ADAM_B1) * g
    v = ADAM_B2 * v + (1.0 - ADAM_B2) * (g * g)
    m_hat = m / (1.0 - ADAM_B1 ** ADAM_STEP)
    v_hat = v / (1.0 - ADAM_B2 ** ADAM_STEP)
    delta = -ADAM_LR * (m_hat / (jnp.sqrt(v_hat) + ADAM_EPS) + ADAM_WD * w)
    return delta, m, v


def _rows_tile(rows):
    return rows if rows <= 256 else 256


def sum_chips(own, parts, name):
    _, rows, cols = parts.shape
    tr = _rows_tile(rows)

    def body(own_ref, p_ref, o_ref):
        acc = own_ref[...].astype(F32)
        for s in range(NCHIP):
            acc = acc + p_ref[s].astype(F32)
        o_ref[...] = acc

    return pl.pallas_call(
        body, grid=(rows // tr,), name=name,
        in_specs=[pl.BlockSpec((tr, cols), lambda i: (i, 0)), pl.BlockSpec((NCHIP, tr, cols), lambda i: (0, i, 0))],
        out_specs=pl.BlockSpec((tr, cols), lambda i: (i, 0)),
        out_shape=jax.ShapeDtypeStruct((rows, cols), F32),
        compiler_params=_cparams(("parallel",)),
    )(own, parts)


def adam_shard(p_mine, p_sib, w, m, v, name):
    rows, cols = p_mine.shape
    tr = _rows_tile(rows)
    lead = w.ndim == 3

    def body(a_ref, b_ref, w_ref, m_ref, v_ref, g_ref, d_ref, mo_ref, vo_ref):
        g = a_ref[...] + b_ref[...]
        g = g[None] if lead else g
        g_ref[...] = g
        d_ref[...], mo_ref[...], vo_ref[...] = _adam_math(g, w_ref[...], m_ref[...], v_ref[...])

    flat = pl.BlockSpec((tr, cols), lambda i: (i, 0))
    spec = pl.BlockSpec((1, tr, cols), lambda i: (0, i, 0)) if lead else flat
    return pl.pallas_call(
        body, grid=(rows // tr,), name=name, in_specs=[flat] * 2 + [spec] * 3, out_specs=[spec] * 4,
        out_shape=[jax.ShapeDtypeStruct(w.shape, F32)] * 4,
        compiler_params=_cparams(("parallel",)),
    )(p_mine, p_sib, w, m, v)


def adam_shard_halves_t(r_mine, r_sib, wt, mt, vt, name):
    hrows, cols = r_mine.shape
    tr = _rows_tile(hrows)
    per_half = hrows // tr

    def body(a_ref, b_ref, w_ref, m_ref, v_ref, g_ref, d_ref, mo_ref, vo_ref):
        mine = pl.program_id(0) == lax.axis_index("c")
        g = jnp.where(mine, a_ref[...], b_ref[...]).T[None]
        g_ref[...] = g
        d_ref[...], mo_ref[...], vo_ref[...] = _adam_math(g, w_ref[...], m_ref[...], v_ref[...])

    flat = pl.BlockSpec((tr, cols), lambda h, i: (i, 0))
    spec = pl.BlockSpec((1, cols, tr), lambda h, i: (0, 0, h * per_half + i))
    return pl.pallas_call(
        body, grid=(2, per_half), name=name, in_specs=[flat] * 2 + [spec] * 3, out_specs=[spec] * 4,
        out_shape=[jax.ShapeDtypeStruct(wt.shape, F32)] * 4,
        compiler_params=_cparams(("parallel", "parallel")),
    )(r_mine, r_sib, wt, mt, vt)


def adam_replicated(parts, w, m, v, name):
    rows = w.shape[0]

    def body(p_ref, w_ref, m_ref, v_ref, g_ref, d_ref, mo_ref, vo_ref):
        g = p_ref[0]
        for d in range(1, NDEV):
            g = g + p_ref[d]
        g_ref[...] = g
        d_ref[...], mo_ref[...], vo_ref[...] = _adam_math(g, w_ref[...], m_ref[...], v_ref[...])

    return pl.pallas_call(
        body, name=name, out_shape=[jax.ShapeDtypeStruct((rows, 128), F32)] * 4,
    )(parts, w, m, v)


def _pack(arrs):
    pieces = []
    for a in arrs:
        flat = a.reshape(-1)
        pad = (-flat.shape[0]) % 128
        pieces.append(jnp.pad(flat, (0, pad)) if pad else flat)
    flat = jnp.concatenate(pieces)
    pad = (-flat.shape[0]) % 1024
    return jnp.pad(flat, (0, pad)).reshape(-1, 128)


def _unpack(buf, shapes):
    flat = buf.reshape(-1)
    out = []
    o = 0
    for s in shapes:
        n = int(np.prod(s))
        out.append(flat[o:o + n].reshape(s))
        o += n + (-n) % 128
    return out


EVEN_SPLITS = (SHIFT, W, W, W, W, W)
ODD_SPLITS = (D, D, D)


def _cols_to_chips(a):
    rows, cols = a.shape
    return a.reshape(rows, NCHIP, cols // NCHIP).transpose(1, 0, 2)


def _chips_to_cols(a):
    _, rows, n = a.shape
    return a.transpose(1, 0, 2).reshape(rows, NCHIP * n)


def kernel(x, norm_g, w_in_e, shift_mu, rw_w0, rw_w2, rw_a0, rw_a2, rw_kk, rw_ka, rw_rk, rw_lnx_g, rw_lnx_b, att_bias, w_out_e, w_in_o, sg_ln_g, sg_ln_b, sg_w, sg_b, w_out_o, final_g, loss_target, m_norm_g, m_w_in_e, m_shift_mu, m_rw_w0, m_rw_w2, m_rw_a0, m_rw_a2, m_rw_kk, m_rw_ka, m_rw_rk, m_rw_lnx_g, m_rw_lnx_b, m_att_bias, m_w_out_e, m_w_in_o, m_sg_ln_g, m_sg_ln_b, m_sg_w, m_sg_b, m_w_out_o, m_final_g, v_norm_g, v_w_in_e, v_shift_mu, v_rw_w0, v_rw_w2, v_rw_a0, v_rw_a2, v_rw_kk, v_rw_ka, v_rw_rk, v_rw_lnx_g, v_rw_lnx_b, v_att_bias, v_w_out_e, v_w_in_o, v_sg_ln_g, v_sg_ln_b, v_sg_w, v_sg_b, v_w_out_o, v_final_g):
    x2 = x.reshape(T, D)
    tgt = loss_target.reshape(T, D)

    my_chip = 2 * lax.axis_index("x") + lax.axis_index("y")
    gathered = gather_weights(
        [jnp.swapaxes(w_in_e[0], 0, 1).astype(BF16), jnp.concatenate([rw_w2[0], rw_a2[0]], axis=0),
         jnp.concatenate([sg_ln_g, sg_ln_b], axis=0)], [True, True, False])
    wie = gathered[0].reshape(EVEN_IN, D)
    w2 = _chips_to_cols(gathered[1][:, :LORA])
    a2 = _chips_to_cols(gathered[1][:, LORA:])
    sglg = _chips_to_cols(gathered[2][:, 0:1])
    sglb = _chips_to_cols(gathered[2][:, 1:2])

    late = [w_out_e[0].astype(BF16), w_in_o[0].astype(BF16), w_out_o[0].astype(BF16)]
    late_started = send_start(late, [jnp.broadcast_to(a[None], (NCHIP,) + a.shape) for a in late], False, False,
                              "late_weights_start", after=gathered[0])

    def late_weights(after):
        woe, wio, woo = send_wait(late_started, after, False, False, "late_weights_wait")
        return woe.reshape(D, D), _chips_to_cols(wio), woo.reshape(D, D)

    def scatter_start(grads, name):
        srcs = [g_.astype(BF16) if g_.shape[-1] >= W else g_ for g_ in grads]
        return send_start(srcs, [jnp.zeros_like(s) for s in srcs], True, False, name)

    def own_block(g_):
        return lax.dynamic_index_in_dim(g_, my_chip, axis=0, keepdims=False)

    started = {}

    def on_odd_grads(d_woo, d_wio):
        blocks = [d_woo.reshape(NCHIP, D // NCHIP, D), d_wio]
        started["odd"] = (scatter_start(blocks, "odd_grads_start"), [own_block(b) for b in blocks])
        return started["odd"][0][-1]

    def on_even_grads(big_g):
        d_wie, d_woe, _, _, d_w2, d_a2, d_sglg, d_sglb = big_g
        my_half = lax.dynamic_slice_in_dim(d_wie, lax.axis_index("c") * (D // 2), D // 2, axis=1)
        d_wie_half = add_blocks(my_half, swap_row_halves(d_wie, "swap_w_in_e_halves"), "add_w_in_e_halves")
        blocks = [d_wie_half, d_woe.reshape(NCHIP, D // NCHIP, D), _cols_to_chips(d_w2), _cols_to_chips(d_a2),
                  _cols_to_chips(d_sglg), _cols_to_chips(d_sglb)]
        started["even"] = (scatter_start(blocks, "even_grads_start"), [own_block(b) for b in blocks])
        return started["even"][0][-1]

    def on_small_grads(layer, grads):
        mine = _pack(grads)
        started[layer + "_small"] = send_start([mine], [jnp.broadcast_to(mine[None], (NDEV,) + mine.shape)], False,
                                               True, layer + "_small_grads_start")
        return started[layer + "_small"][-1]

    loss_part, dx, _, _ = _local_step(
        x2, tgt, wie, late_weights, w2, a2, sglg, sglb, norm_g, shift_mu, rw_w0, rw_a0, rw_kk, rw_ka, rw_rk,
        rw_lnx_g, rw_lnx_b, att_bias, sg_w, sg_b, final_g, first_after=late_started[-1], on_odd_grads=on_odd_grads,
        on_even_grads=on_even_grads, on_small_grads=on_small_grads)
    even_started, even_own = started["even"]

    wmv = {"w_in_e": tuple(jnp.swapaxes(a, 1, 2) for a in (w_in_e, m_w_in_e, v_w_in_e)),
           "w_out_e": (w_out_e, m_w_out_e, v_w_out_e),
           "w_in_o": (w_in_o, m_w_in_o, v_w_in_o), "w_out_o": (w_out_o, m_w_out_o, v_w_out_o),
           "rw_w2": (rw_w2, m_rw_w2, v_rw_w2), "rw_a2": (rw_a2, m_rw_a2, v_rw_a2),
           "sg_ln_g": (sg_ln_g, m_sg_ln_g, v_sg_ln_g), "sg_ln_b": (sg_ln_b, m_sg_ln_b, v_sg_ln_b)}
    sharded = {}

    def finish(names, own, landed, tag):
        partial = [sum_chips(o_, p_, "sum_" + nm) for o_, p_, nm in zip(own, landed, names)]
        from_sibling = exchange_c(partial, "swap_partials_" + tag)
        for nm, mine, sib in zip(names, partial, from_sibling):
            if nm == "w_in_e":
                res = adam_shard_halves_t(mine, sib, *wmv[nm], "adam_" + nm)
                sharded[nm] = [jnp.swapaxes(a, 1, 2) for a in res]
            else:
                sharded[nm] = adam_shard(mine, sib, *wmv[nm], "adam_" + nm)
        return partial[0]

    odd_started, odd_own = started["odd"]
    odd_landed = send_wait(odd_started, started["even_small"][-1], True, False, "odd_grads_wait")
    done = finish(["w_out_o", "w_in_o"], odd_own, odd_landed, "odd")

    no_w = jnp.zeros((1, 1), F32)
    groups = {
        "odd": (["sg_w", "sg_b", "final_g", "norm_g1"], [sg_w, sg_b, final_g, norm_g[1:2]],
                [m_sg_w, m_sg_b, m_final_g, m_norm_g[1:2]], [v_sg_w, v_sg_b, v_final_g, v_norm_g[1:2]]),
        "even": (["norm_g0", "shift_mu", "rw_w0", "rw_a0", "rw_kk", "rw_ka", "rw_rk", "rw_lnx_g", "rw_lnx_b",
                  "att_bias", "loss"],
                 [norm_g[0:1], shift_mu, rw_w0, rw_a0, rw_kk, rw_ka, rw_rk, rw_lnx_g, rw_lnx_b, att_bias, no_w],
                 [m_norm_g[0:1], m_shift_mu, m_rw_w0, m_rw_a0, m_rw_kk, m_rw_ka, m_rw_rk, m_rw_lnx_g, m_rw_lnx_b,
                  m_att_bias, no_w],
                 [v_norm_g[0:1], v_shift_mu, v_rw_w0, v_rw_a0, v_rw_kk, v_rw_ka, v_rw_rk, v_rw_lnx_g, v_rw_lnx_b,
                  v_att_bias, no_w]),
    }
    rep = {}
    for layer in ("odd", "even"):
        nms, ws, ms_, vs_ = groups[layer]
        (gathered_g,) = send_wait(started[layer + "_small"], done, False, True, layer + "_small_grads_wait")
        rep_out = adam_replicated(gathered_g, _pack(ws), _pack(ms_), _pack(vs_), "adam_" + layer + "_small")
        done = rep_out[0]
        for nm in nms:
            rep[nm] = []
        for buf in rep_out:
            for nm, a in zip(nms, _unpack(buf, [w_.shape for w_ in ws])):
                rep[nm].append(a)
    rep["norm_g"] = [jnp.concatenate([a, b], axis=0) for a, b in zip(rep["norm_g0"], rep["norm_g1"])]
    even_landed = send_wait(even_started, done, True, False, "even_grads_wait")
    finish(["w_in_e", "w_out_e", "rw_w2", "rw_a2", "sg_ln_g", "sg_ln_b"], even_own, even_landed, "even")

    order = ["norm_g", "w_in_e", "shift_mu", "rw_w0", "rw_w2", "rw_a0", "rw_a2", "rw_kk", "rw_ka", "rw_rk",
             "rw_lnx_g", "rw_lnx_b", "att_bias", "w_out_e", "w_in_o", "sg_ln_g", "sg_ln_b", "sg_w", "sg_b",
             "w_out_o", "final_g"]
    results = {**sharded, **rep}
    outs = [rep["loss"][0].reshape(()), dx.reshape(NSEQ, SEQ, D)]
    for kind in range(4):
        outs += [results[nm][kind] for nm in order]
    return tuple(outs)


def _local_step(x2, tgt, wie_t, late_weights, w2, a2, sglg, sglb, norm_g, shift_mu, rw_w0, rw_a0, rw_kk, rw_ka, rw_rk,
                rw_lnx_g, rw_lnx_b, att_bias, sg_w, sg_b, final_g, first_after=None, on_odd_grads=None,
                on_even_grads=None, on_small_grads=None):
    zl = jnp.zeros((LORA, W), F32)
    w2x = jnp.concatenate([w2, zl], axis=0)
    a2x = jnp.concatenate([zl, a2], axis=0)
    rk = rw_rk.reshape(1, W)
    pos = np.arange(SGC)
    sg_mask = jnp.asarray(((pos[None, :] // L) <= (pos[:, None] // L)).astype(np.float32))
    wm = (sg_w[0] * sg_mask[None]).astype(BF16)
    sgb_t = sg_b[0].T

    xn0, ps, ga, q, kb, vb, gb = ln_in_proj(x2, norm_g[0:1], wie_t, EVEN_SPLITS, "in_proj_even", after=first_after,
                                            w_t=True)
    r, lw, k2, v, aa, bb = even_prep(ps, shift_mu, rw_w0, w2x, rw_a0, a2x, rw_kk, rw_ka)
    y, rw_saved = rwkv_fwd(r, lw, k2, v, aa, bb)
    bias = window_bias(bias_expand(att_bias[0]).reshape(NPAIR, 2 * L, BAND))

    def padded(a):
        return jnp.pad(a.astype(BF16).reshape(NSEQ, SEQ, W), ((0, 0), (LEFT * L, 0), (0, 0))).reshape(NSEQ * PADSEQ, W)

    kpad, vpad = padded(kb), padded(vb)
    o = attention_fwd(q, kpad, vpad, bias)
    woe, wio, woo = late_weights(o)
    h1, zt = even_post(y, r, k2, v, ga, o, gb, rw_lnx_g, rw_lnx_b, rk, x2, woe)
    xn1, u, vv, gt = ln_in_proj(h1, norm_g[1:2], wio, ODD_SPLITS, "in_proj_odd")
    dh2, loss_part, d_final_g, z2t = gmlp_fwd_loss(u, vv, gt, sglg, sglb, wm, sgb_t, h1, woo, final_g[None], tgt)

    du, dvv, dgt, d_sglg, d_sglb, d_wm, d_sgb_t, d_woo = gmlp_bwd(u, vv, gt, sglg, sglb, wm, sgb_t, dh2, z2t, woo)
    dp_odd = [du, dvv, dgt]
    d_wio = matmul_acc_chips(xn1, dp_odd, "in_proj_odd_dw")
    token = on_odd_grads(d_woo, d_wio) if on_odd_grads else None
    dh1, d_g1 = in_proj_bwd_x(h1, norm_g[1:2], wio, dp_odd, dh2, "in_proj_odd_bwd", after=token)
    odd_small = [d_wm * sg_mask[None], d_sgb_t.T, d_final_g, d_g1]
    token = on_small_grads("odd", odd_small) if on_small_grads else None
    dy, dr2, dk22, dv2, dga, do, dgb, d_lng, d_lnb, d_rk, d_woe = even_post_bwd(
        y, r, k2, v, ga, o, gb, rw_lnx_g, rw_lnx_b, rk, dh1, zt, woe, after=token)
    dq, dkb, dvb, dbias = attention_bwd(q, kpad, vpad, bias, do)
    dbias = sum(dbias[:, i * 2 * L:(i + 1) * 2 * L, i * L:i * L + BAND] for i in range(ATT_Q))
    d_att_bias = bias_grad(dbias.reshape(NH, L, BAND))
    dr, dlw, dk2, dv, daa, dbb = rwkv_bwd(r, lw, k2, aa, bb, rw_saved, dy)
    dps, d_mu, d_w0, d_w2x, d_a0, d_a2x, d_kk, d_ka = even_prep_bwd(
        ps, shift_mu, rw_w0, w2x, rw_a0, a2x, rw_kk, rw_ka, dr, dlw, dk2, dv, daa, dbb, dr2, dk22, dv2)
    dp_even = [dps, dga, dq, dkb, dvb, dgb]
    d_wie = matmul_acc_chips(xn0, dp_even, "in_proj_even_dw")
    big_g = (d_wie, d_woe, d_wio, d_woo, d_w2x[:LORA], d_a2x[LORA:], d_sglg, d_sglb)
    token = on_even_grads(big_g) if on_even_grads else None
    dx, d_g0 = in_proj_bwd_x(x2, norm_g[0:1], wie_t, dp_even, dh1, "in_proj_even_bwd", after=token, w_t=True)
    even_small = [d_g0, d_mu, d_w0, d_a0, d_kk, d_ka, d_rk, d_lng, d_lnb, d_att_bias]
    if on_small_grads:
        on_small_grads("even", even_small + [loss_part[0:1, 0:1]])
    rep_g = [jnp.concatenate([d_g0, d_g1], axis=0)] + even_small[1:] + odd_small[:3]
    return loss_part[0, 0], dx, big_g, rep_g
```

```python
import functools
import math

import jax
import jax.numpy as jnp
import numpy as np
from jax import lax
from jax.experimental import pallas as pl
from jax.experimental.pallas import tpu as pltpu

F32 = jnp.float32
BF16 = jnp.bfloat16
HI = lax.Precision.HIGHEST

D = 1024
SEQ = 2048
NSEQ = 2
T = NSEQ * SEQ
HD = 64
NH = 8
W = 512
SHIFT = 1664
LORA = 64
EVEN_IN = 4224
ODD_IN = 3072
L = 64
NC = SEQ // L
LEFT = 8
BAND = (LEFT + 1) * L
CLIP = 128
SGC = 128
NG = 8
RMS_EPS = 1e-6
LN_EPS = 1e-5
GN_EPS = 64e-5
NEG = -1e30
VMEM_BIG = 56 * 1024 * 1024

ADAM_LR = 0.001
ADAM_B1 = 0.9
ADAM_B2 = 0.999
ADAM_EPS = 1e-08
ADAM_WD = 0.01
ADAM_STEP = 10

MESH = pl.DeviceIdType.MESH


def _bdot(a, b):
    return jnp.dot(a.astype(BF16), b.astype(BF16), preferred_element_type=F32)


def _bdot_nt(a, b):
    return lax.dot_general(a.astype(BF16), b.astype(BF16), (((1,), (1,)), ((), ())), preferred_element_type=F32)


def _bdot_tn(a, b):
    return lax.dot_general(a.astype(BF16), b.astype(BF16), (((0,), (0,)), ((), ())), preferred_element_type=F32)


def _hdot(a, b):
    return jnp.dot(a, b, precision=HI, preferred_element_type=F32)


def _hdot_nt(a, b):
    return lax.dot_general(a, b, (((1,), (1,)), ((), ())), precision=HI, preferred_element_type=F32)


def _hdot_tn(a, b):
    return lax.dot_general(a, b, (((0,), (0,)), ((), ())), precision=HI, preferred_element_type=F32)


def _iota2(shape, dim):
    return lax.broadcasted_iota(jnp.int32, shape, dim)


def _head_blockdiag():
    r = _iota2((2 * HD, 2 * HD), 0) // HD
    c = _iota2((2 * HD, 2 * HD), 1) // HD
    return (r == c).astype(BF16)


def _headsum_impl(x, bd):
    hi = x.astype(BF16)
    mid = (x - hi.astype(F32)).astype(BF16)
    n = bd.shape[0]
    out = [jnp.dot(hi[:, i:i + n], bd, preferred_element_type=F32) + jnp.dot(mid[:, i:i + n], bd, preferred_element_type=F32)
           for i in range(0, x.shape[1], n)]
    return jnp.concatenate(out, axis=-1)


@jax.custom_vjp
def _headsum(x, bd):
    return _headsum_impl(x, bd)


def _headsum_fwd(x, bd):
    return _headsum_impl(x, bd), bd


def _headsum_bwd(bd, ct):
    return _headsum_impl(ct, bd), None


_headsum.defvjp(_headsum_fwd, _headsum_bwd)


def _silu(x):
    return x * jax.nn.sigmoid(x)


def _dsilu(x):
    s = jax.nn.sigmoid(x)
    return s * (1.0 + x * (1.0 - s))


_GELU_C = math.sqrt(2.0 / math.pi)


def _gelu(x):
    return 0.5 * x * (1.0 + jnp.tanh(_GELU_C * (x + 0.044715 * (x * x * x))))


def _dgelu(x):
    t = jnp.tanh(_GELU_C * (x + 0.044715 * (x * x * x)))
    return 0.5 * (1.0 + t) + 0.5 * x * (1.0 - t * t) * _GELU_C * (1.0 + 3.0 * 0.044715 * x * x)


def _silu_both(x):
    s = jax.nn.sigmoid(x)
    xs = x * s
    return xs, s + xs * (1.0 - s)


def _gelu_both(x):
    x2 = x * x
    t = jnp.tanh(_GELU_C * (x + 0.044715 * (x2 * x)))
    half = 0.5 * (1.0 + t)
    return x * half, half + 0.5 * x * (1.0 - t * t) * _GELU_C * (1.0 + 3.0 * 0.044715 * x2)


def _softplus(x):
    return jnp.maximum(x, 0.0) + jnp.log(1.0 + jnp.exp(-jnp.abs(x)))


def _cparams(sem, vmem=None):
    return pltpu.CompilerParams(dimension_semantics=sem, vmem_limit_bytes=vmem)


def _row_spec(tm, width):
    return pl.BlockSpec((tm, width), lambda i: (i, 0))


def _col_spec(height, tm):
    return pl.BlockSpec((height, tm), lambda i: (0, i))


def _const_spec(shape):
    nd = len(shape)
    return pl.BlockSpec(shape, lambda *_: (0,) * nd)


def _weight_dims(w_bf, w_t):
    return (((1,), (1,)), ((), ())) if w_t else (((1,), (0,)), ((), ())), w_bf.shape[0 if w_t else 1]


def ln_in_proj(x, g, w_bf, splits, name, after=None, w_t=False):
    dims, n = _weight_dims(w_bf, w_t)
    tm = 512 if n <= ODD_IN else 256
    spans = []
    o = 0
    for s in splits:
        spans.append((o, o + s))
        o += s
    assert o == n
    extra_specs, extra = _after_operand(after)

    def body(x_ref, g_ref, w_ref, *rest):
        xn_ref, outs = rest[len(extra)], rest[len(extra) + 1:]
        xv = x_ref[...]
        rstd = lax.rsqrt(jnp.mean(xv * xv, axis=-1, keepdims=True) + RMS_EPS)
        xn = (xv * rstd * g_ref[...]).astype(BF16)
        xn_ref[...] = xn.T
        p = lax.dot_general(xn, w_ref[...], dims, preferred_element_type=F32)
        for o_ref, (a, b) in zip(outs, spans):
            o_ref[...] = p[:, a:b]

    return pl.pallas_call(
        body, grid=(T // tm,), name=name,
        in_specs=[_row_spec(tm, D), _const_spec((1, D)), _const_spec(w_bf.shape)] + extra_specs,
        out_specs=[_col_spec(D, tm)] + [_row_spec(tm, s) for s in splits],
        out_shape=[jax.ShapeDtypeStruct((D, T), BF16)] + [jax.ShapeDtypeStruct((T, s), F32) for s in splits],
        compiler_params=_cparams(("parallel",), VMEM_BIG),
    )(x, g, w_bf, *extra)


def in_proj_bwd_x(x, g, w_bf, dps, dres, name, after=None, w_t=False):
    tm = 512
    back = (((1,), (0,)), ((), ())) if w_t else (((1,), (1,)), ((), ()))
    widths = [d.shape[1] for d in dps]
    extra_specs, extra = _after_operand(after)

    def body(x_ref, g_ref, w_ref, dres_ref, *rest):
        dp_refs = rest[:len(widths)]
        dx_ref, dg_ref = rest[-2:]
        dp = jnp.concatenate([r[...] for r in dp_refs], axis=-1)
        dxn = lax.dot_general(dp, w_ref[...], back, preferred_element_type=F32)
        xv = x_ref[...]
        rstd = lax.rsqrt(jnp.mean(xv * xv, axis=-1, keepdims=True) + RMS_EPS)
        xhat = xv * rstd
        dgp = jnp.sum(dxn * xhat, axis=0, keepdims=True)

        @pl.when(pl.program_id(0) == 0)
        def _():
            dg_ref[...] = jnp.zeros_like(dg_ref)

        dg_ref[...] += dgp
        dxh = dxn * g_ref[...]
        dx_ref[...] = dres_ref[...] + rstd * (dxh - xhat * jnp.mean(dxh * xhat, axis=-1, keepdims=True))

    return pl.pallas_call(
        body, grid=(T // tm,), name=name,
        in_specs=[_row_spec(tm, D), _const_spec((1, D)), _const_spec(w_bf.shape), _row_spec(tm, D)]
        + [_row_spec(tm, s) for s in widths] + extra_specs,
        out_specs=[_row_spec(tm, D), _const_spec((1, D))],
        out_shape=[jax.ShapeDtypeStruct((T, D), F32), jax.ShapeDtypeStruct((1, D), F32)],
        compiler_params=_cparams(("arbitrary",), VMEM_BIG),
    )(x, g, w_bf, dres, *dps, *extra)


def _after_operand(after):
    return ([ANY], [after]) if after is not None else ([], [])


def matmul_acc_chips(at_bf, pieces, name, after=None):
    k = at_bf.shape[0]
    widths = [p.shape[1] for p in pieces]
    nb = sum(widths) // NCHIP
    tm = 512
    steps = T // tm
    extra_specs, extra = _after_operand(after)

    def body(a_ref, *rest):
        o_ref, acc = rest[-2:]

        @pl.when(pl.program_id(0) == 0)
        def _():
            acc[...] = jnp.zeros_like(acc)

        a = a_ref[...]
        b = jnp.concatenate([r[...] for r in rest[:len(widths)]], axis=-1)
        for s in range(NCHIP):
            acc[s] += jnp.dot(a, b[:, s * nb:(s + 1) * nb], preferred_element_type=F32)

        @pl.when(pl.program_id(0) == steps - 1)
        def _():
            o_ref[...] = acc[...].astype(BF16)

    return pl.pallas_call(
        body, grid=(steps,), name=name,
        in_specs=[_col_spec(k, tm)] + [_row_spec(tm, w_) for w_ in widths] + extra_specs,
        out_specs=_const_spec((NCHIP, k, nb)),
        out_shape=jax.ShapeDtypeStruct((NCHIP, k, nb), BF16),
        scratch_shapes=[pltpu.VMEM((NCHIP, k, nb), F32)],
        compiler_params=_cparams(("arbitrary",), VMEM_BIG),
    )(at_bf, *pieces, *extra)


def _out_proj_back(dh_ref, zt_ref, w_ref, dw_ref):
    dhb = dh_ref[...].astype(BF16)

    @pl.when(pl.program_id(0) == 0)
    def _():
        dw_ref[...] = jnp.zeros_like(dw_ref)

    dw_ref[...] += jnp.dot(zt_ref[...], dhb, preferred_element_type=F32)
    return lax.dot_general(dhb, w_ref[...], (((1,), (1,)), ((), ())), preferred_element_type=F32)


PREP_TM = 512
PREP_NB = SEQ // PREP_TM


def _prep_elem(k, wl, apre, kkw, kaw, bd):
    wraw = -_softplus(-wl) - 0.5
    lw = -jnp.exp(wraw)
    asig = jax.nn.sigmoid(apre)
    kkr = k * kkw
    nrm = jnp.maximum(jnp.sqrt(_headsum(kkr * kkr, bd)), 1e-12)
    kk = kkr / nrm
    k2 = k * (1.0 + (asig - 1.0) * kaw)
    return lw, k2, -kk, kk * asig


def _prep_elem_bwd(k, wl, apre, kkw, kaw, bd, dlw, dk2, daa, dbb):
    s = -wl
    sp = _softplus(s)
    dwl = dlw * (-jnp.exp(-sp - 0.5)) * jnp.exp(s - sp)
    asig = jax.nn.sigmoid(apre)
    kkr = k * kkw
    root = jnp.sqrt(_headsum(kkr * kkr, bd))
    inv = 1.0 / jnp.maximum(root, 1e-12)
    kk = kkr * inv
    dkk = dbb * asig - daa
    dap = (dbb * kk + dk2 * k * kaw) * asig * (1.0 - asig)
    through_norm = jnp.where(root > 1e-12, kk * _headsum(dkk * kkr, bd) * inv, 0.0)
    dkkr = inv * (dkk - through_norm)
    gain = 1.0 + (asig - 1.0) * kaw
    dk = dkkr * kkw + dk2 * gain
    dkkw = jnp.sum(dkkr * k, axis=0, keepdims=True)
    dkaw = jnp.sum(dk2 * k * (asig - 1.0), axis=0, keepdims=True)
    return dk, dwl, dap, dkkw, dkaw


def _shifted(ps_ref, prev_ref, mu, blk):
    p = ps_ref[...]
    first = (blk % PREP_NB) == 0
    prev_row = jnp.where(first, 0.0, prev_ref[7:8, :])
    rolled = pltpu.roll(p, 1, 0)
    p_prev = jnp.where(_iota2(p.shape, 0) == 0, prev_row, rolled)
    return p, p_prev, p + (p_prev - p) * mu


def _prev_spec(width, blk_of):
    return pl.BlockSpec((8, width), lambda i: (jnp.maximum(blk_of(i) * (PREP_TM // 8) - 1, 0), 0))


def even_prep(ps, mu, w0, w2x, a0, a2x, kkw, kaw):
    tm = PREP_TM

    def body(ps_ref, prev_ref, mu_ref, w0_ref, w2_ref, a0_ref, a2_ref, kk_ref, ka_ref,
             r_ref, lw_ref, k2_ref, v_ref, aa_ref, bb_ref):
        _, _, s = _shifted(ps_ref, prev_ref, mu_ref[...], pl.program_id(0))
        wa = s[:, 3 * W:]
        wl = w0_ref[...] + _bdot(jnp.tanh(wa), w2_ref[...])
        apre = a0_ref[...] + _bdot(wa, a2_ref[...])
        lw, k2, aa, bb = _prep_elem(s[:, W:2 * W], wl, apre, kk_ref[...], ka_ref[...], _head_blockdiag())
        r_ref[...] = s[:, 0:W]
        v_ref[...] = s[:, 2 * W:3 * W]
        lw_ref[...] = lw
        k2_ref[...] = k2
        aa_ref[...] = aa
        bb_ref[...] = bb

    vec = _const_spec((1, W))
    return pl.pallas_call(
        body, grid=(T // tm,), name="even_prep",
        in_specs=[_row_spec(tm, SHIFT), _prev_spec(SHIFT, lambda i: i), _const_spec((1, SHIFT)), vec,
                  _const_spec((2 * LORA, W)), vec, _const_spec((2 * LORA, W)), vec, vec],
        out_specs=[_row_spec(tm, W)] * 6,
        out_shape=[jax.ShapeDtypeStruct((T, W), F32)] * 6,
        compiler_params=_cparams(("parallel",), VMEM_BIG),
    )(ps, ps, mu, w0, w2x, a0, a2x, kkw, kaw)


def even_prep_bwd(ps, mu, w0, w2x, a0, a2x, kkw, kaw, dr, dlw, dk2, dv, daa, dbb, dr2, dk22, dv2):
    tm = PREP_TM
    nb = T // tm
    rev = lambda i: nb - 1 - i

    def body(ps_ref, prev_ref, mu_ref, w0_ref, w2_ref, a0_ref, a2_ref, kk_ref, ka_ref,
             dr_ref, dlw_ref, dk2_ref, dv_ref, daa_ref, dbb_ref, dr2_ref, dk22_ref, dv2_ref,
             dps_ref, dmu_ref, dw0_ref, dw2_ref, da0_ref, da2_ref, dkk_ref, dka_ref, carry):
        i = pl.program_id(0)
        blk = rev(i)
        mu_v = mu_ref[...]
        p, p_prev, s = _shifted(ps_ref, prev_ref, mu_v, blk)
        wa = s[:, 3 * W:]
        th = jnp.tanh(wa)
        wl = w0_ref[...] + _bdot(th, w2_ref[...])
        apre = a0_ref[...] + _bdot(wa, a2_ref[...])
        bd = _head_blockdiag()
        k = s[:, W:2 * W]
        dk, dwl, dap, dkkw, dkaw = _prep_elem_bwd(k, wl, apre, kk_ref[...], ka_ref[...], bd, dlw_ref[...],
                                                  dk2_ref[...] + dk22_ref[...], daa_ref[...], dbb_ref[...])
        dwa = _bdot_nt(dwl, w2_ref[...]) * (1.0 - th * th) + _bdot_nt(dap, a2_ref[...])
        ds = jnp.concatenate([dr_ref[...] + dr2_ref[...], dk, dv_ref[...] + dv2_ref[...], dwa], axis=-1)

        @pl.when(i == 0)
        def _():
            for ref in (dmu_ref, dw0_ref, dw2_ref, da0_ref, da2_ref, dkk_ref, dka_ref, carry):
                ref[...] = jnp.zeros_like(ref)

        dmu_ref[...] += jnp.sum(ds * (p_prev - p), axis=0, keepdims=True)
        dw0_ref[...] += jnp.sum(dwl, axis=0, keepdims=True)
        da0_ref[...] += jnp.sum(dap, axis=0, keepdims=True)
        dw2_ref[...] += _bdot_tn(th, dwl)
        da2_ref[...] += _bdot_tn(wa, dap)
        dkk_ref[...] += dkkw
        dka_ref[...] += dkaw
        dsm = ds * mu_v
        last = (blk % PREP_NB) == PREP_NB - 1
        nxt = jnp.where(last, 0.0, carry[0:1, :])
        up = pltpu.roll(dsm, tm - 1, 0)
        up = jnp.where(_iota2(up.shape, 0) == tm - 1, nxt, up)
        dps_ref[...] = (ds - dsm + up).astype(BF16)
        carry[0:1, :] = dsm[0:1, :]

    vec = _const_spec((1, W))
    rrow = lambda width: pl.BlockSpec((tm, width), lambda i: (rev(i), 0))
    return pl.pallas_call(
        body, grid=(nb,), name="even_prep_bwd",
        in_specs=[rrow(SHIFT), _prev_spec(SHIFT, rev), _const_spec((1, SHIFT)), vec,
                  _const_spec((2 * LORA, W)), vec, _const_spec((2 * LORA, W)), vec, vec] + [rrow(W)] * 9,
        out_specs=[rrow(SHIFT), _const_spec((1, SHIFT)), vec, _const_spec((2 * LORA, W)), vec,
                   _const_spec((2 * LORA, W)), vec, vec],
        out_shape=[jax.ShapeDtypeStruct((T, SHIFT), BF16), jax.ShapeDtypeStruct((1, SHIFT), F32),
                   jax.ShapeDtypeStruct((1, W), F32), jax.ShapeDtypeStruct((2 * LORA, W), F32),
                   jax.ShapeDtypeStruct((1, W), F32), jax.ShapeDtypeStruct((2 * LORA, W), F32),
                   jax.ShapeDtypeStruct((1, W), F32), jax.ShapeDtypeStruct((1, W), F32)],
        scratch_shapes=[pltpu.VMEM((8, SHIFT), F32)],
        compiler_params=_cparams(("arbitrary",), VMEM_BIG),
    )(ps, ps, mu, w0, w2x, a0, a2x, kkw, kaw, dr, dlw, dk2, dv, daa, dbb, dr2, dk22, dv2)


NPAIR = NH // 2
PW = 2 * HD


def _pair_cols(p):
    return slice(p * PW, (p + 1) * PW)


def _pairs(a):
    return [a[:, _pair_cols(p)] for p in range(NPAIR)]


def _stack_pair(a):
    first = _iota2(a.shape, 1) < HD
    zero = jnp.zeros_like(a)
    return jnp.concatenate([jnp.where(first, a, zero), jnp.where(first, zero, a)], axis=0)


def _unstack_pair(a):
    n = a.shape[0] // 2
    return jnp.where(_iota2((n, PW), 1) < HD, a[:n], a[n:])


def _fold_pair(a):
    n = a.shape[0] // 2
    return a[:n] + a[n:]


def _chunk_masks():
    n = 4 * L
    row = _iota2((n, n), 0)
    col = _iota2((n, n), 1)
    same = ((row // L) & 1) == ((col // L) & 1)
    ri = row & (L - 1)
    ci = col & (L - 1)
    keep = same & (((row < 2 * L) & (ri > ci)) | ((row >= 2 * L) & (ri >= ci)))
    r1 = _iota2((L, L), 0)
    c1 = _iota2((L, L), 1)
    r2 = _iota2((2 * L, 2 * L), 0)
    c2 = _iota2((2 * L, 2 * L), 1)
    return keep.astype(F32), (r1 >= c1).astype(F32), (r2 == c2).astype(F32)


def _scaled(r, lw, k2, aa, bb, tri):
    g = _hdot(tri, lw)
    eg = jnp.exp(g)
    eng = jnp.exp(-g)
    egp = jnp.exp(g - lw)
    return eg, eng, egp, aa * egp, r * eg, bb * eng, k2 * eng


def _head_cols(h):
    return slice(h * HD, (h + 1) * HD)


def _per_head(a):
    return [a[:, _head_cols(h)] for h in range(NH)]


def _pairs_operands(at, rt, bt, kt):
    x = [jnp.concatenate([_stack_pair(a), _stack_pair(r)], axis=0).astype(BF16) for a, r in zip(_pairs(at), _pairs(rt))]
    yk = [jnp.concatenate([_stack_pair(b), _stack_pair(k)], axis=0).astype(BF16) for b, k in zip(_pairs(bt), _pairs(kt))]
    return x, yk


def _pairs_matrices(x, yk, keep, eye):
    m = [_bdot_nt(a, b) * keep for a, b in zip(x, yk)]
    p = [a[:2 * L, :2 * L] for a in m]
    tinv = [eye + a for a in p]
    for _ in range(5):
        p = [_bdot(a, a) for a in p]
        tinv = [t + _bdot(t, a) for t, a in zip(tinv, p)]
    return [a.astype(BF16) for a in m], [a.astype(BF16) for a in tinv]


def _pairs_fwd(x, yk, m, tinv, vw, s0, egl):
    xh = [_bdot_nt(a, s) for a, s in zip(x, s0)]
    u = [_bdot(t, h[:2 * L] + _bdot(a[:2 * L, 2 * L:], w)) for t, h, a, w in zip(tinv, xh, m, vw)]
    uv = [jnp.concatenate([a, w], axis=0).astype(BF16) for a, w in zip(u, vw)]
    y = [h[2 * L:] + _bdot(a[2 * L:], w) for h, a, w in zip(xh, m, uv)]
    sn = [e * (s + _bdot_tn(w, b)) for e, s, w, b in zip(egl, s0, uv, yk)]
    return y, sn, uv


def _pairs_bwd(x, yk, m, tinv, uv, s0, sn, egl, dyw, dsn, keep):
    dzs = [d * e for d, e in zip(dsn, egl)]
    dgl = [jnp.sum(d * s, axis=0, keepdims=True) for d, s in zip(dsn, sn)]
    dyb = [a.astype(BF16) for a in dyw]
    t1 = [_bdot_tn(a[2 * L:], d) for a, d in zip(m, dyb)]
    t2 = [_bdot_nt(b, d) for b, d in zip(yk, dzs)]
    drhs = [_bdot_tn(t, a[:2 * L] + b[:2 * L]) for t, a, b in zip(tinv, t1, t2)]
    dv = [a[2 * L:] + b[2 * L:] + _bdot_tn(c[:2 * L, 2 * L:], d) for a, b, c, d in zip(t1, t2, m, drhs)]
    gg = [jnp.concatenate([a, b], axis=0).astype(BF16) for a, b in zip(drhs, dyw)]
    ds0 = [d + _bdot_tn(g, a) for d, g, a in zip(dzs, gg, x)]
    dm = [_bdot_nt(g, w) * keep for g, w in zip(gg, uv)]
    dx = [_bdot(g, s) + _bdot(d, b) for g, s, d, b in zip(gg, s0, dm, yk)]
    dyk = [_bdot_tn(d, a) + _bdot(w, z) for d, a, w, z in zip(dm, x, uv, dzs)]
    return dx, dyk, dv, dgl, ds0


STATE_SHAPE = (NPAIR * PW, PW)
M_SHAPE = (4 * L, NPAIR * 4 * L)
TINV_SHAPE = (2 * L, NPAIR * 2 * L)


def _rows_of(a, n):
    return [a[i * n:(i + 1) * n, :] for i in range(NPAIR)]


def _both(f):
    out = []
    for s in range(NSEQ):
        out += f(s)
    return out


def _seq_view(a):
    return a.reshape(NSEQ, SEQ, a.shape[-1])


UV_SHAPE = (4 * L, NPAIR * PW)
RW_CHUNKS = 2


def rwkv_fwd(r, lw, k2, v, aa, bb):
    def body(r_ref, lw_ref, k2_ref, v_ref, aa_ref, bb_ref, y_ref, hs_ref, hn_ref, m_ref, t_ref, uv_ref, state):
        @pl.when(pl.program_id(0) == 0)
        def _():
            state[...] = jnp.zeros_like(state)

        keep, tri, eye = _chunk_masks()
        where = [(j, s) for j in range(RW_CHUNKS) for s in range(NSEQ)]
        rows = lambda j: slice(j * L, (j + 1) * L)
        sc = [_scaled(r_ref[s, rows(j)], lw_ref[s, rows(j)], k2_ref[s, rows(j)], aa_ref[s, rows(j)],
                      bb_ref[s, rows(j)], tri) for j, s in where]
        ops = [_pairs_operands(*a[3:]) for a in sc]
        m, tinv = _pairs_matrices([a for o in ops for a in o[0]], [a for o in ops for a in o[1]], keep, eye)
        s_cur = [state[s] for s in range(NSEQ)]
        for j in range(RW_CHUNKS):
            mine = slice(j * NSEQ * NPAIR, (j + 1) * NSEQ * NPAIR)
            x = [a for o in ops[j * NSEQ:(j + 1) * NSEQ] for a in o[0]]
            yk = [a for o in ops[j * NSEQ:(j + 1) * NSEQ] for a in o[1]]
            vw = _both(lambda s: [_stack_pair(a) for a in _pairs(v_ref[s, rows(j)])])
            egl = _both(lambda s: _pairs(sc[j * NSEQ + s][0][L - 1:L, :]))
            y, sn, uv = _pairs_fwd(x, yk, m[mine], tinv[mine], vw, _both(lambda s: _rows_of(s_cur[s], PW)), egl)
            for s in range(NSEQ):
                ps = slice(s * NPAIR, (s + 1) * NPAIR)
                hs_ref[j, s] = s_cur[s]
                y_ref[s, rows(j)] = jnp.concatenate([_fold_pair(a) for a in y[ps]], axis=-1)
                m_ref[j, s] = jnp.concatenate(m[mine][ps], axis=-1)
                t_ref[j, s] = jnp.concatenate(tinv[mine][ps], axis=-1)
                uv_ref[j, s] = jnp.concatenate(uv[ps], axis=-1)
                s_cur[s] = jnp.concatenate(sn[ps], axis=0)
                hn_ref[j, s] = s_cur[s]
        for s in range(NSEQ):
            state[s] = s_cur[s]

    blk = pl.BlockSpec((NSEQ, RW_CHUNKS * L, W), lambda c: (0, c, 0))
    per_chunk = lambda shape: pl.BlockSpec((RW_CHUNKS, NSEQ) + shape, lambda c: (c, 0, 0, 0))
    saved_shapes = [(STATE_SHAPE, F32), (STATE_SHAPE, F32), (M_SHAPE, BF16), (TINV_SHAPE, BF16), (UV_SHAPE, BF16)]
    y, *saved = pl.pallas_call(
        body, grid=(NC // RW_CHUNKS,), name="rwkv_fwd",
        in_specs=[blk] * 6,
        out_specs=[blk] + [per_chunk(shape) for shape, _ in saved_shapes],
        out_shape=[jax.ShapeDtypeStruct((NSEQ, SEQ, W), F32)]
        + [jax.ShapeDtypeStruct((NC, NSEQ) + shape, dt) for shape, dt in saved_shapes],
        scratch_shapes=[pltpu.VMEM((NSEQ,) + STATE_SHAPE, F32)],
        compiler_params=_cparams(("arbitrary",), VMEM_BIG),
    )(*[_seq_view(a) for a in (r, lw, k2, v, aa, bb)])
    return y.reshape(T, W), saved


def rwkv_bwd(r, lw, k2, aa, bb, saved, dy):
    def body(r_ref, lw_ref, k2_ref, aa_ref, bb_ref, hs_ref, hn_ref, m_ref, t_ref, uv_ref, dy_ref,
             dr_ref, dlw_ref, dk2_ref, dv_ref, daa_ref, dbb_ref, dstate):
        @pl.when(pl.program_id(0) == 0)
        def _():
            dstate[...] = jnp.zeros_like(dstate)

        keep, tri, _ = _chunk_masks()
        sc = [_scaled(r_ref[s], lw_ref[s], k2_ref[s], aa_ref[s], bb_ref[s], tri) for s in range(NSEQ)]
        ops = [_pairs_operands(*sc[s][3:]) for s in range(NSEQ)]
        x, yk = _both(lambda s: ops[s][0]), _both(lambda s: ops[s][1])
        m = _both(lambda s: [m_ref[0, s][:, i * 4 * L:(i + 1) * 4 * L] for i in range(NPAIR)])
        tinv = _both(lambda s: [t_ref[0, s][:, i * 2 * L:(i + 1) * 2 * L] for i in range(NPAIR)])
        uv = _both(lambda s: _pairs(uv_ref[0, s]))
        dyw = _both(lambda s: [_stack_pair(a) for a in _pairs(dy_ref[s])])
        s0 = _both(lambda s: _rows_of(hs_ref[0, s], PW))
        sn = _both(lambda s: _rows_of(hn_ref[0, s], PW))
        dsn = _both(lambda s: _rows_of(dstate[s], PW))
        egl = _both(lambda s: _pairs(sc[s][0][L - 1:L, :]))
        dx, dyk, dvw, dgl, ds0 = _pairs_bwd(x, yk, m, tinv, uv, s0, sn, egl, dyw, dsn, keep)
        for s in range(NSEQ):
            mine = slice(s * NPAIR, (s + 1) * NPAIR)
            eg, eng, egp, at, rt, bt, kt = sc[s]
            dstate[s] = jnp.concatenate(ds0[mine], axis=0)
            dv_ref[s] = jnp.concatenate([_fold_pair(a) for a in dvw[mine]], axis=-1)
            dat = jnp.concatenate([_fold_pair(a[:2 * L]) for a in dx[mine]], axis=-1)
            drt = jnp.concatenate([_fold_pair(a[2 * L:]) for a in dx[mine]], axis=-1)
            dbt = jnp.concatenate([_fold_pair(a[:2 * L]) for a in dyk[mine]], axis=-1)
            dkt = jnp.concatenate([_fold_pair(a[2 * L:]) for a in dyk[mine]], axis=-1)
            dg = drt * rt - dbt * bt - dkt * kt
            dg = dg + jnp.where(_iota2(dg.shape, 0) == L - 1, jnp.concatenate(dgl[mine], axis=-1), 0.0)
            dgp = dat * at
            dlw_ref[s] = _hdot_tn(tri, dg + dgp) - dgp
            dr_ref[s] = drt * eg
            daa_ref[s] = dat * egp
            dbb_ref[s] = dbt * eng
            dk2_ref[s] = dkt * eng

    blk = pl.BlockSpec((NSEQ, L, W), lambda c: (0, NC - 1 - c, 0))
    per_chunk = lambda shape: pl.BlockSpec((1, NSEQ) + shape, lambda c: (NC - 1 - c, 0, 0, 0))
    outs = pl.pallas_call(
        body, grid=(NC,), name="rwkv_bwd",
        in_specs=[blk] * 5 + [per_chunk(a.shape[2:]) for a in saved] + [blk],
        out_specs=[blk] * 6,
        out_shape=[jax.ShapeDtypeStruct((NSEQ, SEQ, W), F32)] * 6,
        scratch_shapes=[pltpu.VMEM((NSEQ,) + STATE_SHAPE, F32)],
        compiler_params=_cparams(("arbitrary",)),
    )(*[_seq_view(a) for a in (r, lw, k2, aa, bb)], *saved, _seq_view(dy))
    return [a.reshape(T, W) for a in outs]


def _post_math(y, r, k2, v, ga, o, gb, lng, lnb, rk, bd):
    mu = _headsum(y, bd) * (1.0 / HD)
    yc = y - mu
    var = _headsum(yc * yc, bd) * (1.0 / HD)
    yn = yc * lax.rsqrt(var + GN_EPS) * lng + lnb
    bonus = _headsum(r * k2 * rk, bd) * v
    return (yn + bonus) * _silu(ga), o * _silu(gb)


def even_post(y, r, k2, v, ga, o, gb, lng, lnb, rk, h, w_bf):
    tm = 512

    def body(y_ref, r_ref, k2_ref, v_ref, ga_ref, o_ref, gb_ref, lng_ref, lnb_ref, rk_ref, h_ref, w_ref,
             ho_ref, zt_ref):
        ya, yb = _post_math(y_ref[...], r_ref[...], k2_ref[...], v_ref[...], ga_ref[...], o_ref[...], gb_ref[...],
                            lng_ref[...], lnb_ref[...], rk_ref[...], _head_blockdiag())
        z = jnp.concatenate([ya.astype(BF16), yb.astype(BF16)], axis=-1)
        zt_ref[...] = z.T
        ho_ref[...] = h_ref[...] + jnp.dot(z, w_ref[...], preferred_element_type=F32)

    vec = _const_spec((1, W))
    return pl.pallas_call(
        body, grid=(T // tm,), name="even_post",
        in_specs=[_row_spec(tm, W)] * 7 + [vec] * 3 + [_row_spec(tm, D), _const_spec((D, D))],
        out_specs=[_row_spec(tm, D), _col_spec(D, tm)],
        out_shape=[jax.ShapeDtypeStruct((T, D), F32), jax.ShapeDtypeStruct((D, T), BF16)],
        compiler_params=_cparams(("parallel",), VMEM_BIG),
    )(y, r, k2, v, ga, o, gb, lng, lnb, rk, h, w_bf)


def even_post_bwd(y, r, k2, v, ga, o, gb, lng, lnb, rk, dh, zt_bf, w_bf, after=None):
    tm = 512
    extra_specs, extra = _after_operand(after)

    def body(y_ref, r_ref, k2_ref, v_ref, ga_ref, o_ref, gb_ref, lng_ref, lnb_ref, rk_ref, dh_ref, zt_ref, w_ref,
             *rest):
        dy_ref, dr_ref, dk2_ref, dv_ref, dga_ref, do_ref, dgb_ref, dlng_ref, dlnb_ref, drk_ref, dw_ref = rest[-11:]
        dzv = _out_proj_back(dh_ref, zt_ref, w_ref, dw_ref)
        bd = _head_blockdiag()
        _, vjp = jax.vjp(lambda *a: _post_math(*a, bd), y_ref[...], r_ref[...], k2_ref[...], v_ref[...], ga_ref[...],
                         o_ref[...], gb_ref[...], lng_ref[...], lnb_ref[...], rk_ref[...])
        dy, dr, dk2, dv, dga, do, dgb, dlng, dlnb, drk = vjp((dzv[:, 0:W], dzv[:, W:2 * W]))
        for ref, val in ((dy_ref, dy), (dr_ref, dr), (dk2_ref, dk2), (dv_ref, dv), (dga_ref, dga), (do_ref, do),
                         (dgb_ref, dgb)):
            ref[...] = val.astype(ref.dtype)

        @pl.when(pl.program_id(0) == 0)
        def _():
            for ref in (dlng_ref, dlnb_ref, drk_ref):
                ref[...] = jnp.zeros_like(ref)

        dlng_ref[...] += dlng
        dlnb_ref[...] += dlnb
        drk_ref[...] += drk

    vec = _const_spec((1, W))
    return pl.pallas_call(
        body, grid=(T // tm,), name="even_post_bwd",
        in_specs=[_row_spec(tm, W)] * 7 + [vec] * 3 + [_row_spec(tm, D), _col_spec(D, tm), _const_spec((D, D))]
        + extra_specs,
        out_specs=[_row_spec(tm, W)] * 7 + [vec] * 3 + [_const_spec((D, D))],
        out_shape=[jax.ShapeDtypeStruct((T, W), dt) for dt in (F32, F32, F32, F32, BF16, F32, BF16)]
        + [jax.ShapeDtypeStruct((1, W), F32)] * 3 + [jax.ShapeDtypeStruct((D, D), F32)],
        compiler_params=_cparams(("arbitrary",), VMEM_BIG),
    )(y, r, k2, v, ga, o, gb, lng, lnb, rk, dh, zt_bf, w_bf, *extra)


PADSEQ = SEQ + LEFT * L
ATT_SCALE = 1.0 / math.sqrt(HD)
ATT_Q = 4
WIN = BAND + (ATT_Q - 1) * L
ATT_STEPS = NC // ATT_Q
ATT_BIAS_SHAPE = (NPAIR, ATT_Q * 2 * L, WIN)


def _stack_chunks(a):
    return jnp.concatenate([_stack_pair(a[i * L:(i + 1) * L]) for i in range(ATT_Q)], axis=0)


def _unstack_chunks(a):
    return jnp.concatenate([_unstack_pair(a[i * 2 * L:(i + 1) * 2 * L]) for i in range(ATT_Q)], axis=0)


def window_bias(bias):
    parts = [jnp.pad(bias, ((0, 0), (0, 0), (i * L, (ATT_Q - 1 - i) * L)), constant_values=NEG) for i in range(ATT_Q)]
    return jnp.concatenate(parts, axis=1)


def _att_probs(q2, kw, bias, step):
    valid = _iota2((1, WIN), 1) >= (LEFT - step * ATT_Q) * L
    s = [jnp.where(valid, _bdot_nt(a, b) * ATT_SCALE + bias[p], NEG) for p, (a, b) in enumerate(zip(q2, kw))]
    e = [jnp.exp(a - jnp.max(a, axis=-1, keepdims=True)) for a in s]
    return [a / jnp.sum(a, axis=-1, keepdims=True) for a in e]


def attention_fwd(q, kpad, vpad, bias):
    def body(q_ref, k_ref, v_ref, b_ref, o_ref):
        step = pl.program_id(1)
        start = pl.multiple_of(step * (ATT_Q * L), L)
        kw = _pairs(k_ref[pl.ds(start, WIN), :])
        vw = _pairs(v_ref[pl.ds(start, WIN), :])
        q2 = [_stack_chunks(a) for a in _pairs(q_ref[...].astype(BF16))]
        p = _att_probs(q2, kw, b_ref[...], step)
        o_ref[...] = jnp.concatenate([_unstack_chunks(_bdot(a, b)) for a, b in zip(p, vw)], axis=-1)

    qblk = pl.BlockSpec((ATT_Q * L, W), lambda b, c: (b * ATT_STEPS + c, 0))
    kblk = pl.BlockSpec((PADSEQ, W), lambda b, c: (b, 0))
    return pl.pallas_call(
        body, grid=(NSEQ, ATT_STEPS), name="attention_fwd",
        in_specs=[qblk, kblk, kblk, _const_spec(ATT_BIAS_SHAPE)],
        out_specs=qblk, out_shape=jax.ShapeDtypeStruct((T, W), F32),
        compiler_params=_cparams(("parallel", "arbitrary")),
    )(q, kpad, vpad, bias)


def attention_bwd(q, kpad, vpad, bias, do):
    def body(q_ref, k_ref, v_ref, b_ref, do_ref, dq_ref, dko_ref, dvo_ref, db_ref, dk_ref, dv_ref):
        b = pl.program_id(0)
        c = pl.program_id(1)

        @pl.when(c == 0)
        def _():
            dk_ref[...] = jnp.zeros_like(dk_ref)
            dv_ref[...] = jnp.zeros_like(dv_ref)

        @pl.when((c == 0) & (b == 0))
        def _():
            db_ref[...] = jnp.zeros_like(db_ref)

        start = pl.multiple_of(c * (ATT_Q * L), L)
        kw = _pairs(k_ref[pl.ds(start, WIN), :])
        vw = _pairs(v_ref[pl.ds(start, WIN), :])
        q2 = [_stack_chunks(a) for a in _pairs(q_ref[...].astype(BF16))]
        do2 = [_stack_chunks(a) for a in _pairs(do_ref[...].astype(BF16))]
        p = _att_probs(q2, kw, b_ref[...], c)
        dp = [_bdot_nt(a, b) for a, b in zip(do2, vw)]
        ds = [a * (d - jnp.sum(d * a, axis=-1, keepdims=True)) for a, d in zip(p, dp)]
        dss = [(a * ATT_SCALE).astype(BF16) for a in ds]
        dq_ref[...] = jnp.concatenate([_unstack_chunks(_bdot(a, b)) for a, b in zip(dss, kw)], axis=-1).astype(BF16)
        dk_ref[pl.ds(start, WIN), :] += jnp.concatenate([_bdot_tn(a, b) for a, b in zip(dss, q2)], axis=-1)
        dv_ref[pl.ds(start, WIN), :] += jnp.concatenate([_bdot_tn(a, b) for a, b in zip(p, do2)], axis=-1)
        for i in range(NPAIR):
            db_ref[i] += ds[i]

        @pl.when(c == ATT_STEPS - 1)
        def _():
            dko_ref[...] = dk_ref[LEFT * L:, :].astype(BF16)
            dvo_ref[...] = dv_ref[LEFT * L:, :].astype(BF16)

    qblk = pl.BlockSpec((ATT_Q * L, W), lambda b, c: (b * ATT_STEPS + c, 0))
    kblk = pl.BlockSpec((PADSEQ, W), lambda b, c: (b, 0))
    sblk = pl.BlockSpec((SEQ, W), lambda b, c: (b, 0))
    bblk = _const_spec(ATT_BIAS_SHAPE)
    return pl.pallas_call(
        body, grid=(NSEQ, ATT_STEPS), name="attention_bwd",
        in_specs=[qblk, kblk, kblk, bblk, qblk],
        out_specs=[qblk, sblk, sblk, bblk],
        out_shape=[jax.ShapeDtypeStruct((T, W), BF16), jax.ShapeDtypeStruct((T, W), BF16),
                   jax.ShapeDtypeStruct((T, W), BF16), jax.ShapeDtypeStruct(ATT_BIAS_SHAPE, F32)],
        scratch_shapes=[pltpu.VMEM((PADSEQ, W), F32), pltpu.VMEM((PADSEQ, W), F32)],
        compiler_params=_cparams(("arbitrary", "arbitrary"), VMEM_BIG),
    )(q, kpad, vpad, bias, do)


NTAB = 2 * CLIP + 1
EXT = BAND + L


def _ext_onehot():
    n = _iota2((EXT, NTAB), 0)
    m = _iota2((EXT, NTAB), 1)
    return (jnp.clip(BAND - 1 - n, -CLIP, CLIP) + CLIP == m).astype(F32)


def bias_expand(table):
    def body(t_ref, o_ref):
        ext = _hdot_nt(t_ref[...], _ext_onehot())
        for i in range(L):
            s = L - 1 - i
            o_ref[:, i, :] = (pltpu.roll(ext, EXT - s, 1) if s else ext)[:, :BAND]

    return pl.pallas_call(body, name="bias_expand", out_shape=jax.ShapeDtypeStruct((NH, L, BAND), F32))(table)


def bias_grad(dbias):
    def body(d_ref, o_ref):
        acc = jnp.zeros((NH, EXT), F32)
        zpad = jnp.zeros((NH, EXT - BAND), F32)
        for i in range(L):
            s = L - 1 - i
            row = jnp.concatenate([d_ref[:, i, :], zpad], axis=-1)
            acc = acc + (pltpu.roll(row, s, 1) if s else row)
        o_ref[...] = _hdot(acc, _ext_onehot())

    return pl.pallas_call(body, name="bias_grad", out_shape=jax.ShapeDtypeStruct((NH, NTAB), F32))(dbias)


def _group_cols(g):
    return slice(g * SGC, (g + 1) * SGC)


def _sg_norm(gv, lng, lnb):
    gc = gv - jnp.mean(gv, axis=-1, keepdims=True)
    rstd = lax.rsqrt(jnp.mean(gc * gc, axis=-1, keepdims=True) + LN_EPS)
    xhat = gc * rstd
    return xhat, rstd, xhat * lng + lnb


GMLP_BWD_CHUNKS = 2


def gmlp_fwd_loss(u, v, gate, lng, lnb, wm_bf, sgb_t, h, w_bf, g_final, target):
    tm = GMLP_BWD_CHUNKS * SGC

    def body(u_ref, v_ref, gt_ref, lng_ref, lnb_ref, wm_ref, sb_ref, h_ref, w_ref, g_ref, t_ref,
             dh_ref, loss_ref, dg_ref, zt_ref):
        zs = []
        for ch in range(GMLP_BWD_CHUNKS):
            rows = slice(ch * SGC, (ch + 1) * SGC)
            _, _, vln = _sg_norm(_gelu(v_ref[rows, :]), lng_ref[...], lnb_ref[...])
            vlb = vln.astype(BF16)
            zg = []
            for g in range(NG):
                cs = _group_cols(g)
                sv = jnp.dot(wm_ref[g], vlb[:, cs], preferred_element_type=F32) + sb_ref[:, g:g + 1]
                zg.append((_gelu(u_ref[rows, cs]) * sv * _silu(gt_ref[rows, cs])).astype(BF16))
            zs.append(jnp.concatenate(zg, axis=-1))
        z = jnp.concatenate(zs, axis=0)
        zt_ref[...] = z.T
        xv = h_ref[...] + jnp.dot(z, w_ref[...], preferred_element_type=F32)
        rstd = lax.rsqrt(jnp.mean(xv * xv, axis=-1, keepdims=True) + RMS_EPS)
        xhat = xv * rstd
        err = xhat * g_ref[...] - t_ref[...]
        part = 0.5 * jnp.sum(jnp.mean(err * err, axis=-1, keepdims=True), axis=0, keepdims=True)
        dout = err * (1.0 / D)

        @pl.when(pl.program_id(0) == 0)
        def _():
            loss_ref[...] = jnp.zeros_like(loss_ref)
            dg_ref[...] = jnp.zeros_like(dg_ref)

        loss_ref[...] += jnp.broadcast_to(part, loss_ref.shape)
        dg_ref[...] += jnp.sum(dout * xhat, axis=0, keepdims=True)
        dxh = dout * g_ref[...]
        dh_ref[...] = rstd * (dxh - xhat * jnp.mean(dxh * xhat, axis=-1, keepdims=True))

    return pl.pallas_call(
        body, grid=(T // tm,), name="gmlp_fwd_loss",
        in_specs=[_row_spec(tm, D)] * 3 + [_const_spec((1, D))] * 2
        + [_const_spec((NG, SGC, SGC)), _const_spec((SGC, NG)), _row_spec(tm, D), _const_spec((D, D)),
           _const_spec((1, D)), _row_spec(tm, D)],
        out_specs=[_row_spec(tm, D), _const_spec((8, 128)), _const_spec((1, D)), _col_spec(D, tm)],
        out_shape=[jax.ShapeDtypeStruct((T, D), F32), jax.ShapeDtypeStruct((8, 128), F32),
                   jax.ShapeDtypeStruct((1, D), F32), jax.ShapeDtypeStruct((D, T), BF16)],
        compiler_params=_cparams(("arbitrary",), VMEM_BIG),
    )(u, v, gate, lng, lnb, wm_bf, sgb_t, h, w_bf, g_final, target)


def gmlp_bwd(u, v, gate, lng, lnb, wm_bf, sgb_t, dh, zt_bf, w_bf):
    def body(u_ref, v_ref, gt_ref, lng_ref, lnb_ref, wm_ref, sb_ref, dh_ref, zt_ref, w_ref,
             du_ref, dv_ref, dgt_ref, dlng_ref, dlnb_ref, dwm_ref, dsb_ref, dw_ref):
        @pl.when(pl.program_id(0) == 0)
        def _():
            for ref in (dlng_ref, dlnb_ref, dwm_ref, dsb_ref):
                ref[...] = jnp.zeros_like(ref)

        dz = _out_proj_back(dh_ref, zt_ref, w_ref, dw_ref)
        sel = (_iota2((D, NG), 0) // SGC == _iota2((D, NG), 1)).astype(F32)
        for ch in range(GMLP_BWD_CHUNKS):
            rows = slice(ch * SGC, (ch + 1) * SGC)
            gv, dgv_dv = _gelu_both(v_ref[rows, :])
            xhat, rstd, vln = _sg_norm(gv, lng_ref[...], lnb_ref[...])
            vlb = vln.astype(BF16)
            dvln = []
            dsv_all = []
            for g in range(NG):
                cs = _group_cols(g)
                uu = u_ref[rows, cs]
                gg = gt_ref[rows, cs]
                dzz = dz[rows, cs]
                sv = jnp.dot(wm_ref[g], vlb[:, cs], preferred_element_type=F32) + sb_ref[:, g:g + 1]
                gu, dgu = _gelu_both(uu)
                sg, dsg = _silu_both(gg)
                dzgu = dzz * gu
                dsv = dzgu * sg
                dgt_ref[rows, cs] = (dzgu * sv * dsg).astype(BF16)
                du_ref[rows, cs] = (dzz * sv * sg * dgu).astype(BF16)
                dsb16 = dsv.astype(BF16)
                dvln.append(lax.dot_general(wm_ref[g], dsb16, (((0,), (0,)), ((), ())), preferred_element_type=F32))
                dwm_ref[g] += lax.dot_general(dsb16, vlb[:, cs], (((1,), (1,)), ((), ())),
                                              preferred_element_type=F32)
                dsv_all.append(dsv)
            dvl = jnp.concatenate(dvln, axis=-1)
            dsb_ref[...] += _hdot(jnp.concatenate(dsv_all, axis=-1), sel)
            dlng_ref[...] += jnp.sum(dvl * xhat, axis=0, keepdims=True)
            dlnb_ref[...] += jnp.sum(dvl, axis=0, keepdims=True)
            dxh = dvl * lng_ref[...]
            dgv = rstd * (dxh - jnp.mean(dxh, axis=-1, keepdims=True)
                          - xhat * jnp.mean(dxh * xhat, axis=-1, keepdims=True))
            dv_ref[rows, :] = (dgv * dgv_dv).astype(BF16)

    tm = GMLP_BWD_CHUNKS * SGC
    return pl.pallas_call(
        body, grid=(T // tm,), name="gmlp_bwd",
        in_specs=[_row_spec(tm, D)] * 3 + [_const_spec((1, D))] * 2
        + [_const_spec((NG, SGC, SGC)), _const_spec((SGC, NG)), _row_spec(tm, D), _col_spec(D, tm),
           _const_spec((D, D))],
        out_specs=[_row_spec(tm, D)] * 3 + [_const_spec((1, D))] * 2
        + [_const_spec((NG, SGC, SGC)), _const_spec((SGC, NG)), _const_spec((D, D))],
        out_shape=[jax.ShapeDtypeStruct((T, D), BF16)] * 3 + [jax.ShapeDtypeStruct((1, D), F32)] * 2
        + [jax.ShapeDtypeStruct((NG, SGC, SGC), F32), jax.ShapeDtypeStruct((SGC, NG), F32),
           jax.ShapeDtypeStruct((D, D), F32)],
        compiler_params=_cparams(("arbitrary",), VMEM_BIG),
    )(u, v, gate, lng, lnb, wm_bf, sgb_t, dh, zt_bf, w_bf)


NCHIP = 4
NDEV = 8
ANY = pl.BlockSpec(memory_space=pl.ANY)


HBM = pl.BlockSpec(memory_space=pltpu.HBM)
SEM = pl.BlockSpec(memory_space=pltpu.SEMAPHORE)
EFFECT = pltpu.SideEffectType.DATAFLOW_SIDE_EFFECTING


def _peers(whole_mesh):
    x, y, c = lax.axis_index("x"), lax.axis_index("y"), lax.axis_index("c")
    if not whole_mesh:
        return [((px, py, c), 2 * px + py) for px, py in ((1 - x, y), (x, 1 - y), (1 - x, 1 - y))], 2 * x + y
    out = []
    for j in range(1, NDEV):
        px, py, pc = x ^ (j >> 2), y ^ ((j >> 1) & 1), c ^ (j & 1)
        out.append(((px, py, pc), 4 * px + 2 * py + pc))
    return out, 4 * x + 2 * y + c


def _send_copies(src, land, send, recv, scatter, whole_mesh, starting):
    peers, me = _peers(whole_mesh)
    copies = []
    for t in range(len(src)):
        for j, (dev, slot) in enumerate(peers):
            k = t * len(peers) + j
            copies.append(pltpu.make_async_remote_copy(
                src_ref=src[t].at[slot] if scatter else src[t], dst_ref=land[t].at[me if starting else slot],
                send_sem=send.at[k], recv_sem=recv.at[k], device_id=dev, device_id_type=MESH))
    return copies


def send_start(srcs, lands, scatter, whole_mesh, name, after=None):
    n = len(srcs)
    nsem = n * (NDEV - 1 if whole_mesh else NCHIP - 1)
    extra_specs, extra = _after_operand(after)

    def body(*refs):
        send, recv = refs[2 * n + len(extra)], refs[2 * n + len(extra) + 1]
        for cp in _send_copies(refs[:n], refs[n:2 * n], send, recv, scatter, whole_mesh, True):
            cp.start()
        refs[-1][...] = jnp.zeros_like(refs[-1])

    arrs = list(srcs) + list(lands)
    out = pl.pallas_call(
        body, name=name,
        out_shape=(pltpu.SemaphoreType.DMA((nsem,)), pltpu.SemaphoreType.DMA((nsem,)),
                   *[pltpu.HBM(a.shape, a.dtype) for a in arrs], jax.ShapeDtypeStruct((8, 128), F32)),
        in_specs=[HBM] * (2 * n) + extra_specs,
        out_specs=(SEM, SEM, *[HBM] * (2 * n), pl.BlockSpec(memory_space=pltpu.VMEM)),
        input_output_aliases={i: 2 + i for i in range(2 * n)},
        compiler_params=pltpu.CompilerParams(has_side_effects=EFFECT),
    )(*[pltpu.with_memory_space_constraint(a, pltpu.HBM) for a in arrs], *extra)
    return out[0], out[1], list(out[2:2 + n]), list(out[2 + n:2 + 2 * n]), out[-1]


def send_wait(started, after, scatter, whole_mesh, name):
    send, recv, srcs, lands, _ = started
    n = len(srcs)

    def body(*refs):
        for cp in _send_copies(refs[:n], refs[n:2 * n], refs[2 * n], refs[2 * n + 1], scatter, whole_mesh, False):
            cp.wait_send()
            cp.wait_recv()

    arrs = list(srcs) + list(lands)
    out = pl.pallas_call(
        body, name=name, out_shape=tuple(pltpu.HBM(a.shape, a.dtype) for a in arrs),
        in_specs=[HBM] * (2 * n) + [SEM, SEM, ANY], out_specs=tuple([HBM] * (2 * n)),
        input_output_aliases={i: i for i in range(2 * n)},
        compiler_params=pltpu.CompilerParams(has_side_effects=EFFECT),
    )(*arrs, send, recv, after)
    return list(out[n:])


def exchange_c(arrs, name):
    n = len(arrs)

    def body(*refs):
        ins, outs = refs[:n], refs[n:2 * n]
        send, recv = refs[2 * n:]
        sibling = (lax.axis_index("x"), lax.axis_index("y"), 1 - lax.axis_index("c"))
        copies = [pltpu.make_async_remote_copy(src_ref=ins[t], dst_ref=outs[t], send_sem=send.at[t], recv_sem=recv.at[t],
                                               device_id=sibling, device_id_type=MESH) for t in range(n)]
        for cp in copies:
            cp.start()
        for cp in copies:
            cp.wait()

    return pl.pallas_call(
        body, name=name, in_specs=[ANY] * n, out_specs=[ANY] * n,
        out_shape=[jax.ShapeDtypeStruct(a.shape, a.dtype) for a in arrs],
        scratch_shapes=[pltpu.SemaphoreType.DMA((n,)), pltpu.SemaphoreType.DMA((n,))],
    )(*arrs)


def swap_row_halves(a, name):
    n, rows, cols = a.shape
    half = rows // 2

    def body(in_ref, out_ref, send, recv):
        x, y, c = lax.axis_index("x"), lax.axis_index("y"), lax.axis_index("c")
        cp = pltpu.make_async_remote_copy(src_ref=in_ref.at[:, pl.ds((1 - c) * half, half), :], dst_ref=out_ref,
                                          send_sem=send, recv_sem=recv, device_id=(x, y, 1 - c), device_id_type=MESH)
        cp.start()
        cp.wait()

    return pl.pallas_call(
        body, name=name, in_specs=[ANY], out_specs=ANY, out_shape=jax.ShapeDtypeStruct((n, half, cols), a.dtype),
        scratch_shapes=[pltpu.SemaphoreType.DMA, pltpu.SemaphoreType.DMA],
    )(a)


def add_blocks(a, b, name):
    n, rows, cols = a.shape
    tr = _rows_tile(rows)

    def body(a_ref, b_ref, o_ref):
        o_ref[...] = (a_ref[...].astype(F32) + b_ref[...].astype(F32)).astype(BF16)

    spec = pl.BlockSpec((1, tr, cols), lambda s, i: (s, i, 0))
    return pl.pallas_call(
        body, grid=(n, rows // tr), name=name, in_specs=[spec, spec], out_specs=spec,
        out_shape=jax.ShapeDtypeStruct(a.shape, BF16), compiler_params=_cparams(("parallel", "parallel")),
    )(a, b)


def gather_weights(arrs, split):
    n = len(arrs)

    def body(*refs):
        ins, outs = refs[:n], refs[n:2 * n]
        send1, recv1, send2, recv2, loc = refs[2 * n:]
        x, y, c = lax.axis_index("x"), lax.axis_index("y"), lax.axis_index("c")
        me = 2 * x + y
        sibling = (x, y, 1 - c)
        peers = [(1 - x, y), (x, 1 - y), (1 - x, 1 - y)]

        def rows_of(t, core):
            half = arrs[t].shape[0] // 2
            return pl.ds(core * half, half)

        def part(ref, t, core):
            return ref.at[rows_of(t, core)] if split[t] else ref

        local = [pltpu.make_async_copy(ins[t], outs[t].at[me], loc.at[t]) for t in range(n)]
        for cp in local:
            cp.start()
        first = []
        for t in range(n):
            for j, (px, py) in enumerate(peers):
                first.append(pltpu.make_async_remote_copy(
                    src_ref=part(ins[t], t, c), dst_ref=part(outs[t].at[me], t, c), send_sem=send1.at[t, j],
                    recv_sem=recv1.at[t, j], device_id=(px, py, c), device_id_type=MESH))
        for cp in first:
            cp.start()
        passed = []
        for t in range(n):
            for j, (px, py) in enumerate(peers):
                landed = part(outs[t].at[2 * px + py], t, c)
                pltpu.make_async_remote_copy(
                    src_ref=landed, dst_ref=landed, send_sem=send1.at[t, j], recv_sem=recv1.at[t, j],
                    device_id=(x, y, c), device_id_type=MESH).wait_recv()
                if split[t]:
                    cp = pltpu.make_async_remote_copy(
                        src_ref=landed, dst_ref=landed, send_sem=send2.at[t, j], recv_sem=recv2.at[t, j],
                        device_id=sibling, device_id_type=MESH)
                    cp.start()
                    passed.append(cp)
        for t in range(n):
            for j, (px, py) in enumerate(peers):
                if split[t]:
                    other = part(outs[t].at[2 * px + py], t, 1 - c)
                    pltpu.make_async_remote_copy(
                        src_ref=other, dst_ref=other, send_sem=send2.at[t, j], recv_sem=recv2.at[t, j],
                        device_id=(x, y, c), device_id_type=MESH).wait_recv()
        for cp in first + passed:
            cp.wait_send()
        for cp in local:
            cp.wait()

    return pl.pallas_call(
        body, name="gather_weights", in_specs=[ANY] * n, out_specs=[ANY] * n,
        out_shape=[jax.ShapeDtypeStruct((NCHIP,) + a.shape, a.dtype) for a in arrs],
        scratch_shapes=[pltpu.SemaphoreType.DMA((n, 3))] * 4 + [pltpu.SemaphoreType.DMA((n,))],
    )(*arrs)


def _adam_math(g, w, m, v):
    m = ADAM_B1 * m + (1.0 - ADAM_B1) * g
    v = ADAM_B2 * v + (1.0 - ADAM_B2) * (g * g)
    m_hat = m / (1.0 - ADAM_B1 ** ADAM_STEP)
    v_hat = v / (1.0 - ADAM_B2 ** ADAM_STEP)
    delta = -ADAM_LR * (m_hat / (jnp.sqrt(v_hat) + ADAM_EPS) + ADAM_WD * w)
    return delta, m, v


def _rows_tile(rows):
    return rows if rows <= 256 else 256


def sum_chips(own, parts, name):
    _, rows, cols = parts.shape
    tr = _rows_tile(rows)

    def body(own_ref, p_ref, o_ref):
        acc = own_ref[...].astype(F32)
        for s in range(NCHIP):
            acc = acc + p_ref[s].astype(F32)
        o_ref[...] = acc

    return pl.pallas_call(
        body, grid=(rows // tr,), name=name,
        in_specs=[pl.BlockSpec((tr, cols), lambda i: (i, 0)), pl.BlockSpec((NCHIP, tr, cols), lambda i: (0, i, 0))],
        out_specs=pl.BlockSpec((tr, cols), lambda i: (i, 0)),
        out_shape=jax.ShapeDtypeStruct((rows, cols), F32),
        compiler_params=_cparams(("parallel",)),
    )(own, parts)


def adam_shard(p_mine, p_sib, w, m, v, name):
    rows, cols = p_mine.shape
    tr = _rows_tile(rows)
    lead = w.ndim == 3

    def body(a_ref, b_ref, w_ref, m_ref, v_ref, g_ref, d_ref, mo_ref, vo_ref):
        g = a_ref[...] + b_ref[...]
        g = g[None] if lead else g
        g_ref[...] = g
        d_ref[...], mo_ref[...], vo_ref[...] = _adam_math(g, w_ref[...], m_ref[...], v_ref[...])

    flat = pl.BlockSpec((tr, cols), lambda i: (i, 0))
    spec = pl.BlockSpec((1, tr, cols), lambda i: (0, i, 0)) if lead else flat
    return pl.pallas_call(
        body, grid=(rows // tr,), name=name, in_specs=[flat] * 2 + [spec] * 3, out_specs=[spec] * 4,
        out_shape=[jax.ShapeDtypeStruct(w.shape, F32)] * 4,
        compiler_params=_cparams(("parallel",)),
    )(p_mine, p_sib, w, m, v)


def adam_shard_halves_t(r_mine, r_sib, wt, mt, vt, name):
    hrows, cols = r_mine.shape
    tr = _rows_tile(hrows)
    per_half = hrows // tr

    def body(a_ref, b_ref, w_ref, m_ref, v_ref, g_ref, d_ref, mo_ref, vo_ref):
        mine = pl.program_id(0) == lax.axis_index("c")
        g = jnp.where(mine, a_ref[...], b_ref[...]).T[None]
        g_ref[...] = g
        d_ref[...], mo_ref[...], vo_ref[...] = _adam_math(g, w_ref[...], m_ref[...], v_ref[...])

    flat = pl.BlockSpec((tr, cols), lambda h, i: (i, 0))
    spec = pl.BlockSpec((1, cols, tr), lambda h, i: (0, 0, h * per_half + i))
    return pl.pallas_call(
        body, grid=(2, per_half), name=name, in_specs=[flat] * 2 + [spec] * 3, out_specs=[spec] * 4,
        out_shape=[jax.ShapeDtypeStruct(wt.shape, F32)] * 4,
        compiler_params=_cparams(("parallel", "parallel")),
    )(r_mine, r_sib, wt, mt, vt)


def adam_replicated(gathered, params, name):
    flat = []
    for i, p in enumerate(params):
        if isinstance(p, list):
            off = 0
            for wmv in p:
                n = gathered[i].shape[-1] - off if wmv[0] is None else wmv[0].shape[-1]
                flat.append((i, (off, n), wmv))
                off += n
        else:
            flat.append((i, None, p))
    ins = [a for _, _, wmv in flat for a in wmv if a is not None]
    ng = len(gathered)

    def body(*refs):
        g_refs = refs[:ng]
        in_refs = list(refs[ng:ng + len(ins)])
        out_refs = list(refs[ng + len(ins):])
        sums = []
        for r in g_refs:
            g = r[0]
            for d in range(1, NDEV):
                g = g + r[d]
            sums.append(g)
        for i, lanes, wmv in flat:
            g = sums[i] if lanes is None else sums[i][:, lanes[0]:lanes[0] + lanes[1]]
            out_refs.pop(0)[...] = g
            if wmv[0] is not None:
                w_ref, m_ref, v_ref = in_refs.pop(0), in_refs.pop(0), in_refs.pop(0)
                d_ref, mo_ref, vo_ref = out_refs.pop(0), out_refs.pop(0), out_refs.pop(0)
                d_ref[...], mo_ref[...], vo_ref[...] = _adam_math(g, w_ref[...], m_ref[...], v_ref[...])

    out_shape = []
    for i, lanes, wmv in flat:
        shape = gathered[i].shape[1:] if lanes is None else (1, lanes[1])
        out_shape += [jax.ShapeDtypeStruct(shape, F32)] * (4 if wmv[0] is not None else 1)
    outs = list(pl.pallas_call(body, name=name, out_shape=out_shape)(*gathered, *ins))
    return [[outs.pop(0) for _ in range(4 if wmv[0] is not None else 1)] for _, _, wmv in flat]


EVEN_SPLITS = (SHIFT, W, W, W, W, W)
ODD_SPLITS = (D, D, D)


def _cols_to_chips(a):
    rows, cols = a.shape
    return a.reshape(rows, NCHIP, cols // NCHIP).transpose(1, 0, 2)


def _chips_to_cols(a):
    _, rows, n = a.shape
    return a.transpose(1, 0, 2).reshape(rows, NCHIP * n)


def kernel(x, norm_g, w_in_e, shift_mu, rw_w0, rw_w2, rw_a0, rw_a2, rw_kk, rw_ka, rw_rk, rw_lnx_g, rw_lnx_b, att_bias, w_out_e, w_in_o, sg_ln_g, sg_ln_b, sg_w, sg_b, w_out_o, final_g, loss_target, m_norm_g, m_w_in_e, m_shift_mu, m_rw_w0, m_rw_w2, m_rw_a0, m_rw_a2, m_rw_kk, m_rw_ka, m_rw_rk, m_rw_lnx_g, m_rw_lnx_b, m_att_bias, m_w_out_e, m_w_in_o, m_sg_ln_g, m_sg_ln_b, m_sg_w, m_sg_b, m_w_out_o, m_final_g, v_norm_g, v_w_in_e, v_shift_mu, v_rw_w0, v_rw_w2, v_rw_a0, v_rw_a2, v_rw_kk, v_rw_ka, v_rw_rk, v_rw_lnx_g, v_rw_lnx_b, v_att_bias, v_w_out_e, v_w_in_o, v_sg_ln_g, v_sg_ln_b, v_sg_w, v_sg_b, v_w_out_o, v_final_g):
    x2 = x.reshape(T, D)
    tgt = loss_target.reshape(T, D)

    my_chip = 2 * lax.axis_index("x") + lax.axis_index("y")
    gathered = gather_weights(
        [jnp.swapaxes(w_in_e[0], 0, 1).astype(BF16), jnp.concatenate([rw_w2[0], rw_a2[0]], axis=0),
         jnp.concatenate([sg_ln_g, sg_ln_b], axis=0)], [True, True, False])
    wie = gathered[0].reshape(EVEN_IN, D)
    w2 = _chips_to_cols(gathered[1][:, :LORA])
    a2 = _chips_to_cols(gathered[1][:, LORA:])
    sglg = _chips_to_cols(gathered[2][:, 0:1])
    sglb = _chips_to_cols(gathered[2][:, 1:2])

    late = [w_out_e[0].astype(BF16), w_in_o[0].astype(BF16), w_out_o[0].astype(BF16)]
    late_started = send_start(late, [jnp.broadcast_to(a[None], (NCHIP,) + a.shape) for a in late], False, False,
                              "late_weights_start", after=gathered[0])

    def late_weights(after):
        woe, wio, woo = send_wait(late_started, after, False, False, "late_weights_wait")
        return woe.reshape(D, D), _chips_to_cols(wio), woo.reshape(D, D)

    def scatter_start(grads, name):
        srcs = [g_.astype(BF16) if g_.shape[-1] >= W else g_ for g_ in grads]
        return send_start(srcs, [jnp.zeros_like(s) for s in srcs], True, False, name)

    def own_block(g_):
        return lax.dynamic_index_in_dim(g_, my_chip, axis=0, keepdims=False)

    started = {}

    def on_odd_grads(d_woo, d_wio):
        blocks = [d_woo.reshape(NCHIP, D // NCHIP, D), d_wio]
        started["odd"] = (scatter_start(blocks, "odd_grads_start"), [own_block(b) for b in blocks])
        return started["odd"][0][-1]

    def on_even_grads(big_g):
        d_wie, d_woe, _, _, d_w2, d_a2, d_sglg, d_sglb = big_g
        my_half = lax.dynamic_slice_in_dim(d_wie, lax.axis_index("c") * (D // 2), D // 2, axis=1)
        d_wie_half = add_blocks(my_half, swap_row_halves(d_wie, "swap_w_in_e_halves"), "add_w_in_e_halves")
        blocks = [d_wie_half, d_woe.reshape(NCHIP, D // NCHIP, D), _cols_to_chips(d_w2), _cols_to_chips(d_a2),
                  _cols_to_chips(d_sglg), _cols_to_chips(d_sglb)]
        started["even"] = (scatter_start(blocks, "even_grads_start"), [own_block(b) for b in blocks])
        return started["even"][0][-1]

    def on_small_grads(layer, grads):
        if layer == "odd":
            d_sg_w, d_sg_b, d_final, d_g1 = grads
            mine = [d_sg_w.reshape(NG * SGC, SGC), d_sg_b, jnp.concatenate([d_final, d_g1], axis=1)]
        else:
            mine = [grads[-2], jnp.concatenate(grads[:-2] + grads[-1:], axis=1)]
        started[layer + "_small"] = send_start(mine, [jnp.broadcast_to(a[None], (NDEV,) + a.shape) for a in mine],
                                               False, True, layer + "_small_grads_start")
        return started[layer + "_small"][-1]

    loss_part, dx, _, _ = _local_step(
        x2, tgt, wie, late_weights, w2, a2, sglg, sglb, norm_g, shift_mu, rw_w0, rw_a0, rw_kk, rw_ka, rw_rk,
        rw_lnx_g, rw_lnx_b, att_bias, sg_w, sg_b, final_g, first_after=late_started[-1], on_odd_grads=on_odd_grads,
        on_even_grads=on_even_grads, on_small_grads=on_small_grads)
    even_started, even_own = started["even"]

    wmv = {"w_in_e": tuple(jnp.swapaxes(a, 1, 2) for a in (w_in_e, m_w_in_e, v_w_in_e)),
           "w_out_e": (w_out_e, m_w_out_e, v_w_out_e),
           "w_in_o": (w_in_o, m_w_in_o, v_w_in_o), "w_out_o": (w_out_o, m_w_out_o, v_w_out_o),
           "rw_w2": (rw_w2, m_rw_w2, v_rw_w2), "rw_a2": (rw_a2, m_rw_a2, v_rw_a2),
           "sg_ln_g": (sg_ln_g, m_sg_ln_g, v_sg_ln_g), "sg_ln_b": (sg_ln_b, m_sg_ln_b, v_sg_ln_b)}
    sharded = {}

    def finish(names, own, landed, tag):
        partial = [sum_chips(o_, p_, "sum_" + nm) for o_, p_, nm in zip(own, landed, names)]
        from_sibling = exchange_c(partial, "swap_partials_" + tag)
        for nm, mine, sib in zip(names, partial, from_sibling):
            if nm == "w_in_e":
                res = adam_shard_halves_t(mine, sib, *wmv[nm], "adam_" + nm)
                sharded[nm] = [jnp.swapaxes(a, 1, 2) for a in res]
            else:
                sharded[nm] = adam_shard(mine, sib, *wmv[nm], "adam_" + nm)
        return partial[0]

    odd_started, odd_own = started["odd"]
    odd_landed = send_wait(odd_started, started["even_small"][-1], True, False, "odd_grads_wait")
    done = finish(["w_out_o", "w_in_o"], odd_own, odd_landed, "odd")

    def wmv_of(*arrs, view=lambda a: a):
        return tuple(view(a) for a in arrs)

    vec = lambda a: a.reshape(1, -1)
    groups = {
        "odd": (["sg_w", "sg_b", "final_g", "norm_g1"],
                [wmv_of(sg_w, m_sg_w, v_sg_w, view=lambda a: a.reshape(NG * SGC, SGC)),
                 wmv_of(sg_b, m_sg_b, v_sg_b, view=lambda a: a[0]),
                 [wmv_of(final_g, m_final_g, v_final_g, view=vec),
                  wmv_of(norm_g, m_norm_g, v_norm_g, view=lambda a: a[1:2])]]),
        "even": (["att_bias", "norm_g0", "shift_mu", "rw_w0", "rw_a0", "rw_kk", "rw_ka", "rw_rk", "rw_lnx_g",
                  "rw_lnx_b", "loss"],
                 [wmv_of(att_bias, m_att_bias, v_att_bias, view=lambda a: a[0]),
                  [wmv_of(norm_g, m_norm_g, v_norm_g, view=lambda a: a[0:1]),
                   wmv_of(shift_mu, m_shift_mu, v_shift_mu), wmv_of(rw_w0, m_rw_w0, v_rw_w0),
                   wmv_of(rw_a0, m_rw_a0, v_rw_a0), wmv_of(rw_kk, m_rw_kk, v_rw_kk), wmv_of(rw_ka, m_rw_ka, v_rw_ka),
                   wmv_of(rw_rk, m_rw_rk, v_rw_rk, view=vec), wmv_of(rw_lnx_g, m_rw_lnx_g, v_rw_lnx_g),
                   wmv_of(rw_lnx_b, m_rw_lnx_b, v_rw_lnx_b), (None, None, None)]]),
    }
    rep = {}
    for layer in ("odd", "even"):
        nms, params = groups[layer]
        gathered_g = send_wait(started[layer + "_small"], done, False, True, layer + "_small_grads_wait")
        for nm, res in zip(nms, adam_replicated(gathered_g, params, "adam_" + layer + "_small")):
            rep[nm] = res
        done = rep[nms[0]][0]
    native = {"sg_w": sg_w.shape, "sg_b": sg_b.shape, "final_g": final_g.shape, "rw_rk": rw_rk.shape,
              "att_bias": att_bias.shape}
    for nm, shape in native.items():
        rep[nm] = [a.reshape(shape) for a in rep[nm]]
    rep["norm_g"] = [jnp.concatenate([a, b], axis=0) for a, b in zip(rep["norm_g0"], rep["norm_g1"])]
    even_landed = send_wait(even_started, done, True, False, "even_grads_wait")
    finish(["w_in_e", "w_out_e", "rw_w2", "rw_a2", "sg_ln_g", "sg_ln_b"], even_own, even_landed, "even")

    order = ["norm_g", "w_in_e", "shift_mu", "rw_w0", "rw_w2", "rw_a0", "rw_a2", "rw_kk", "rw_ka", "rw_rk",
             "rw_lnx_g", "rw_lnx_b", "att_bias", "w_out_e", "w_in_o", "sg_ln_g", "sg_ln_b", "sg_w", "sg_b",
             "w_out_o", "final_g"]
    results = {**sharded, **rep}
    outs = [rep["loss"][0][0, 0], dx.reshape(NSEQ, SEQ, D)]
    for kind in range(4):
        outs += [results[nm][kind] for nm in order]
    return tuple(outs)


def _local_step(x2, tgt, wie_t, late_weights, w2, a2, sglg, sglb, norm_g, shift_mu, rw_w0, rw_a0, rw_kk, rw_ka, rw_rk,
                rw_lnx_g, rw_lnx_b, att_bias, sg_w, sg_b, final_g, first_after=None, on_odd_grads=None,
                on_even_grads=None, on_small_grads=None):
    zl = jnp.zeros((LORA, W), F32)
    w2x = jnp.concatenate([w2, zl], axis=0)
    a2x = jnp.concatenate([zl, a2], axis=0)
    rk = rw_rk.reshape(1, W)
    pos = np.arange(SGC)
    sg_mask = jnp.asarray(((pos[None, :] // L) <= (pos[:, None] // L)).astype(np.float32))
    wm = (sg_w[0] * sg_mask[None]).astype(BF16)
    sgb_t = sg_b[0].T

    xn0, ps, ga, q, kb, vb, gb = ln_in_proj(x2, norm_g[0:1], wie_t, EVEN_SPLITS, "in_proj_even", after=first_after,
                                            w_t=True)
    r, lw, k2, v, aa, bb = even_prep(ps, shift_mu, rw_w0, w2x, rw_a0, a2x, rw_kk, rw_ka)
    y, rw_saved = rwkv_fwd(r, lw, k2, v, aa, bb)
    bias = window_bias(bias_expand(att_bias[0]).reshape(NPAIR, 2 * L, BAND))

    def padded(a):
        return jnp.pad(a.astype(BF16).reshape(NSEQ, SEQ, W), ((0, 0), (LEFT * L, 0), (0, 0))).reshape(NSEQ * PADSEQ, W)

    kpad, vpad = padded(kb), padded(vb)
    o = attention_fwd(q, kpad, vpad, bias)
    woe, wio, woo = late_weights(o)
    h1, zt = even_post(y, r, k2, v, ga, o, gb, rw_lnx_g, rw_lnx_b, rk, x2, woe)
    xn1, u, vv, gt = ln_in_proj(h1, norm_g[1:2], wio, ODD_SPLITS, "in_proj_odd")
    dh2, loss_part, d_final_g, z2t = gmlp_fwd_loss(u, vv, gt, sglg, sglb, wm, sgb_t, h1, woo, final_g[None], tgt)

    du, dvv, dgt, d_sglg, d_sglb, d_wm, d_sgb_t, d_woo = gmlp_bwd(u, vv, gt, sglg, sglb, wm, sgb_t, dh2, z2t, woo)
    dp_odd = [du, dvv, dgt]
    d_wio = matmul_acc_chips(xn1, dp_odd, "in_proj_odd_dw")
    token = on_odd_grads(d_woo, d_wio) if on_odd_grads else None
    dh1, d_g1 = in_proj_bwd_x(h1, norm_g[1:2], wio, dp_odd, dh2, "in_proj_odd_bwd", after=token)
    odd_small = [d_wm * sg_mask[None], d_sgb_t.T, d_final_g, d_g1]
    token = on_small_grads("odd", odd_small) if on_small_grads else None
    dy, dr2, dk22, dv2, dga, do, dgb, d_lng, d_lnb, d_rk, d_woe = even_post_bwd(
        y, r, k2, v, ga, o, gb, rw_lnx_g, rw_lnx_b, rk, dh1, zt, woe, after=token)
    dq, dkb, dvb, dbias = attention_bwd(q, kpad, vpad, bias, do)
    dbias = sum(dbias[:, i * 2 * L:(i + 1) * 2 * L, i * L:i * L + BAND] for i in range(ATT_Q))
    d_att_bias = bias_grad(dbias.reshape(NH, L, BAND))
    dr, dlw, dk2, dv, daa, dbb = rwkv_bwd(r, lw, k2, aa, bb, rw_saved, dy)
    dps, d_mu, d_w0, d_w2x, d_a0, d_a2x, d_kk, d_ka = even_prep_bwd(
        ps, shift_mu, rw_w0, w2x, rw_a0, a2x, rw_kk, rw_ka, dr, dlw, dk2, dv, daa, dbb, dr2, dk22, dv2)
    dp_even = [dps, dga, dq, dkb, dvb, dgb]
    d_wie = matmul_acc_chips(xn0, dp_even, "in_proj_even_dw")
    big_g = (d_wie, d_woe, d_wio, d_woo, d_w2x[:LORA], d_a2x[LORA:], d_sglg, d_sglb)
    token = on_even_grads(big_g) if on_even_grads else None
    dx, d_g0 = in_proj_bwd_x(x2, norm_g[0:1], wie_t, dp_even, dh1, "in_proj_even_bwd", after=token, w_t=True)
    even_small = [d_g0, d_mu, d_w0, d_a0, d_kk, d_ka, d_rk, d_lng, d_lnb, d_att_bias]
    if on_small_grads:
        on_small_grads("even", even_small + [loss_part[0:1, :]])
    rep_g = [jnp.concatenate([d_g0, d_g1], axis=0)] + even_small[1:] + odd_small[:3]
    return loss_part[0, 0], dx, big_g, rep_g
```

```python
import functools
import math

import jax
import jax.numpy as jnp
import numpy as np
from jax import lax
from jax.experimental import pallas as pl
from jax.experimental.pallas import tpu as pltpu

F32 = jnp.float32
BF16 = jnp.bfloat16
HI = lax.Precision.HIGHEST

D = 1024
SEQ = 2048
NSEQ = 2
T = NSEQ * SEQ
HD = 64
NH = 8
W = 512
SHIFT = 1664
LORA = 64
EVEN_IN = 4224
ODD_IN = 3072
L = 64
NC = SEQ // L
LEFT = 8
BAND = (LEFT + 1) * L
CLIP = 128
SGC = 128
NG = 8
RMS_EPS = 1e-6
LN_EPS = 1e-5
GN_EPS = 64e-5
NEG = -1e30
VMEM_BIG = 56 * 1024 * 1024

ADAM_LR = 0.001
ADAM_B1 = 0.9
ADAM_B2 = 0.999
ADAM_EPS = 1e-08
ADAM_WD = 0.01
ADAM_STEP = 10

MESH = pl.DeviceIdType.MESH


def _bdot(a, b):
    return jnp.dot(a.astype(BF16), b.astype(BF16), preferred_element_type=F32)


def _bdot_nt(a, b):
    return lax.dot_general(a.astype(BF16), b.astype(BF16), (((1,), (1,)), ((), ())), preferred_element_type=F32)


def _bdot_tn(a, b):
    return lax.dot_general(a.astype(BF16), b.astype(BF16), (((0,), (0,)), ((), ())), preferred_element_type=F32)


def _hdot(a, b):
    return jnp.dot(a, b, precision=HI, preferred_element_type=F32)


def _hdot_nt(a, b):
    return lax.dot_general(a, b, (((1,), (1,)), ((), ())), precision=HI, preferred_element_type=F32)


def _hdot_tn(a, b):
    return lax.dot_general(a, b, (((0,), (0,)), ((), ())), precision=HI, preferred_element_type=F32)


def _iota2(shape, dim):
    return lax.broadcasted_iota(jnp.int32, shape, dim)


def _head_blockdiag():
    r = _iota2((2 * HD, 2 * HD), 0) // HD
    c = _iota2((2 * HD, 2 * HD), 1) // HD
    return (r == c).astype(BF16)


def _headsum_impl(x, bd):
    hi = x.astype(BF16)
    mid = (x - hi.astype(F32)).astype(BF16)
    n = bd.shape[0]
    out = [jnp.dot(hi[:, i:i + n], bd, preferred_element_type=F32) + jnp.dot(mid[:, i:i + n], bd, preferred_element_type=F32)
           for i in range(0, x.shape[1], n)]
    return jnp.concatenate(out, axis=-1)


@jax.custom_vjp
def _headsum(x, bd):
    return _headsum_impl(x, bd)


def _headsum_fwd(x, bd):
    return _headsum_impl(x, bd), bd


def _headsum_bwd(bd, ct):
    return _headsum_impl(ct, bd), None


_headsum.defvjp(_headsum_fwd, _headsum_bwd)


def _silu(x):
    return x * jax.nn.sigmoid(x)


def _dsilu(x):
    s = jax.nn.sigmoid(x)
    return s * (1.0 + x * (1.0 - s))


_GELU_C = math.sqrt(2.0 / math.pi)


def _gelu(x):
    return 0.5 * x * (1.0 + jnp.tanh(_GELU_C * (x + 0.044715 * (x * x * x))))


def _dgelu(x):
    t = jnp.tanh(_GELU_C * (x + 0.044715 * (x * x * x)))
    return 0.5 * (1.0 + t) + 0.5 * x * (1.0 - t * t) * _GELU_C * (1.0 + 3.0 * 0.044715 * x * x)


def _silu_both(x):
    s = jax.nn.sigmoid(x)
    xs = x * s
    return xs, s + xs * (1.0 - s)


def _gelu_both(x):
    x2 = x * x
    t = jnp.tanh(_GELU_C * (x + 0.044715 * (x2 * x)))
    half = 0.5 * (1.0 + t)
    return x * half, half + 0.5 * x * (1.0 - t * t) * _GELU_C * (1.0 + 3.0 * 0.044715 * x2)


def _softplus(x):
    return jnp.maximum(x, 0.0) + jnp.log(1.0 + jnp.exp(-jnp.abs(x)))


def _cparams(sem, vmem=None):
    return pltpu.CompilerParams(dimension_semantics=sem, vmem_limit_bytes=vmem)


def _row_spec(tm, width):
    return pl.BlockSpec((tm, width), lambda i: (i, 0))


def _col_spec(height, tm):
    return pl.BlockSpec((height, tm), lambda i: (0, i))


def _const_spec(shape):
    nd = len(shape)
    return pl.BlockSpec(shape, lambda *_: (0,) * nd)


def _weight_dims(w_bf, w_t):
    return (((1,), (1,)), ((), ())) if w_t else (((1,), (0,)), ((), ())), w_bf.shape[0 if w_t else 1]


def ln_in_proj(x, g, w_bf, splits, name, after=None, w_t=False):
    dims, n = _weight_dims(w_bf, w_t)
    tm = 512 if n <= ODD_IN else 256
    spans = []
    o = 0
    for s in splits:
        spans.append((o, o + s))
        o += s
    assert o == n
    extra_specs, extra = _after_operand(after)

    def body(x_ref, g_ref, w_ref, *rest):
        xn_ref, outs = rest[len(extra)], rest[len(extra) + 1:]
        xv = x_ref[...]
        rstd = lax.rsqrt(jnp.mean(xv * xv, axis=-1, keepdims=True) + RMS_EPS)
        xn = (xv * rstd * g_ref[...]).astype(BF16)
        xn_ref[...] = xn.T
        p = lax.dot_general(xn, w_ref[...], dims, preferred_element_type=F32)
        for o_ref, (a, b) in zip(outs, spans):
            o_ref[...] = p[:, a:b]

    return pl.pallas_call(
        body, grid=(T // tm,), name=name,
        in_specs=[_row_spec(tm, D), _const_spec((1, D)), _const_spec(w_bf.shape)] + extra_specs,
        out_specs=[_col_spec(D, tm)] + [_row_spec(tm, s) for s in splits],
        out_shape=[jax.ShapeDtypeStruct((D, T), BF16)] + [jax.ShapeDtypeStruct((T, s), F32) for s in splits],
        compiler_params=_cparams(("parallel",), VMEM_BIG),
    )(x, g, w_bf, *extra)


def in_proj_bwd_x(x, g, w_bf, dps, dres, name, after=None, w_t=False):
    tm = 512
    back = (((1,), (0,)), ((), ())) if w_t else (((1,), (1,)), ((), ()))
    widths = [d.shape[1] for d in dps]
    extra_specs, extra = _after_operand(after)

    def body(x_ref, g_ref, w_ref, dres_ref, *rest):
        dp_refs = rest[:len(widths)]
        dx_ref, dg_ref = rest[-2:]
        dp = jnp.concatenate([r[...] for r in dp_refs], axis=-1)
        dxn = lax.dot_general(dp, w_ref[...], back, preferred_element_type=F32)
        xv = x_ref[...]
        rstd = lax.rsqrt(jnp.mean(xv * xv, axis=-1, keepdims=True) + RMS_EPS)
        xhat = xv * rstd
        dgp = jnp.sum(dxn * xhat, axis=0, keepdims=True)

        @pl.when(pl.program_id(0) == 0)
        def _():
            dg_ref[...] = jnp.zeros_like(dg_ref)

        dg_ref[...] += dgp
        dxh = dxn * g_ref[...]
        dx_ref[...] = dres_ref[...] + rstd * (dxh - xhat * jnp.mean(dxh * xhat, axis=-1, keepdims=True))

    return pl.pallas_call(
        body, grid=(T // tm,), name=name,
        in_specs=[_row_spec(tm, D), _const_spec((1, D)), _const_spec(w_bf.shape), _row_spec(tm, D)]
        + [_row_spec(tm, s) for s in widths] + extra_specs,
        out_specs=[_row_spec(tm, D), _const_spec((1, D))],
        out_shape=[jax.ShapeDtypeStruct((T, D), F32), jax.ShapeDtypeStruct((1, D), F32)],
        compiler_params=_cparams(("arbitrary",), VMEM_BIG),
    )(x, g, w_bf, dres, *dps, *extra)


def _after_operand(after):
    return ([ANY], [after]) if after is not None else ([], [])


def matmul_acc_chips(at_bf, pieces, name, after=None, add_cores=False):
    k = at_bf.shape[0]
    widths = [p.shape[1] for p in pieces]
    nb = sum(widths) // NCHIP
    tm = 512
    steps = T // tm
    half = k // 2
    extra_specs, extra = _after_operand(after)

    def body(a_ref, *rest):
        o_ref, acc = rest[len(widths) + len(extra):][:2]

        @pl.when(pl.program_id(0) == 0)
        def _():
            acc[...] = jnp.zeros_like(acc)

        a = a_ref[...]
        b = jnp.concatenate([r[...] for r in rest[:len(widths)]], axis=-1)
        for s in range(NCHIP):
            acc[s] += jnp.dot(a, b[:, s * nb:(s + 1) * nb], preferred_element_type=F32)

        @pl.when(pl.program_id(0) == steps - 1)
        def _():
            if not add_cores:
                o_ref[...] = acc[...].astype(BF16)
            else:
                give, got, send, recv = rest[-4:]
                x, y, c = lax.axis_index("x"), lax.axis_index("y"), lax.axis_index("c")
                theirs = pl.multiple_of((1 - c) * half, half)
                mine = pl.multiple_of(c * half, half)
                give[...] = acc[:, pl.ds(theirs, half), :].astype(BF16)
                cp = pltpu.make_async_remote_copy(src_ref=give, dst_ref=got, send_sem=send, recv_sem=recv,
                                                  device_id=(x, y, 1 - c), device_id_type=MESH)
                cp.start()
                cp.wait()
                o_ref[...] = (acc[:, pl.ds(mine, half), :] + got[...].astype(F32)).astype(BF16)

    out_rows = half if add_cores else k
    exchange = [pltpu.VMEM((NCHIP, half, nb), BF16)] * 2 + [pltpu.SemaphoreType.DMA] * 2 if add_cores else []
    return pl.pallas_call(
        body, grid=(steps,), name=name,
        in_specs=[_col_spec(k, tm)] + [_row_spec(tm, w_) for w_ in widths] + extra_specs,
        out_specs=_const_spec((NCHIP, out_rows, nb)),
        out_shape=jax.ShapeDtypeStruct((NCHIP, out_rows, nb), BF16),
        scratch_shapes=[pltpu.VMEM((NCHIP, k, nb), F32)] + exchange,
        compiler_params=_cparams(("arbitrary",), VMEM_BIG),
    )(at_bf, *pieces, *extra)


def _out_proj_back(dh_ref, zt_ref, w_ref, dw_ref):
    dhb = dh_ref[...].astype(BF16)

    @pl.when(pl.program_id(0) == 0)
    def _():
        dw_ref[...] = jnp.zeros_like(dw_ref)

    dw_ref[...] += jnp.dot(zt_ref[...], dhb, preferred_element_type=F32)
    return lax.dot_general(dhb, w_ref[...], (((1,), (1,)), ((), ())), preferred_element_type=F32)


PREP_TM = 512
PREP_NB = SEQ // PREP_TM


def _prep_elem(k, wl, apre, kkw, kaw, bd):
    wraw = -_softplus(-wl) - 0.5
    lw = -jnp.exp(wraw)
    asig = jax.nn.sigmoid(apre)
    kkr = k * kkw
    nrm = jnp.maximum(jnp.sqrt(_headsum(kkr * kkr, bd)), 1e-12)
    kk = kkr / nrm
    k2 = k * (1.0 + (asig - 1.0) * kaw)
    return lw, k2, -kk, kk * asig


def _prep_elem_bwd(k, wl, apre, kkw, kaw, bd, dlw, dk2, daa, dbb):
    s = -wl
    sp = _softplus(s)
    dwl = dlw * (-jnp.exp(-sp - 0.5)) * jnp.exp(s - sp)
    asig = jax.nn.sigmoid(apre)
    kkr = k * kkw
    root = jnp.sqrt(_headsum(kkr * kkr, bd))
    inv = 1.0 / jnp.maximum(root, 1e-12)
    kk = kkr * inv
    dkk = dbb * asig - daa
    dap = (dbb * kk + dk2 * k * kaw) * asig * (1.0 - asig)
    through_norm = jnp.where(root > 1e-12, kk * _headsum(dkk * kkr, bd) * inv, 0.0)
    dkkr = inv * (dkk - through_norm)
    gain = 1.0 + (asig - 1.0) * kaw
    dk = dkkr * kkw + dk2 * gain
    dkkw = jnp.sum(dkkr * k, axis=0, keepdims=True)
    dkaw = jnp.sum(dk2 * k * (asig - 1.0), axis=0, keepdims=True)
    return dk, dwl, dap, dkkw, dkaw


def _shifted(ps_ref, prev_ref, mu, blk):
    p = ps_ref[...]
    first = (blk % PREP_NB) == 0
    prev_row = jnp.where(first, 0.0, prev_ref[7:8, :])
    rolled = pltpu.roll(p, 1, 0)
    p_prev = jnp.where(_iota2(p.shape, 0) == 0, prev_row, rolled)
    return p, p_prev, p + (p_prev - p) * mu


def _prev_spec(width, blk_of):
    return pl.BlockSpec((8, width), lambda i: (jnp.maximum(blk_of(i) * (PREP_TM // 8) - 1, 0), 0))


def even_prep(ps, mu, w0, w2x, a0, a2x, kkw, kaw):
    tm = PREP_TM

    def body(ps_ref, prev_ref, mu_ref, w0_ref, w2_ref, a0_ref, a2_ref, kk_ref, ka_ref,
             r_ref, lw_ref, k2_ref, v_ref, aa_ref, bb_ref):
        _, _, s = _shifted(ps_ref, prev_ref, mu_ref[...], pl.program_id(0))
        wa = s[:, 3 * W:]
        wl = w0_ref[...] + _bdot(jnp.tanh(wa), w2_ref[...])
        apre = a0_ref[...] + _bdot(wa, a2_ref[...])
        lw, k2, aa, bb = _prep_elem(s[:, W:2 * W], wl, apre, kk_ref[...], ka_ref[...], _head_blockdiag())
        r_ref[...] = s[:, 0:W]
        v_ref[...] = s[:, 2 * W:3 * W]
        lw_ref[...] = lw
        k2_ref[...] = k2
        aa_ref[...] = aa
        bb_ref[...] = bb

    vec = _const_spec((1, W))
    return pl.pallas_call(
        body, grid=(T // tm,), name="even_prep",
        in_specs=[_row_spec(tm, SHIFT), _prev_spec(SHIFT, lambda i: i), _const_spec((1, SHIFT)), vec,
                  _const_spec((2 * LORA, W)), vec, _const_spec((2 * LORA, W)), vec, vec],
        out_specs=[_row_spec(tm, W)] * 6,
        out_shape=[jax.ShapeDtypeStruct((T, W), F32)] * 6,
        compiler_params=_cparams(("parallel",), VMEM_BIG),
    )(ps, ps, mu, w0, w2x, a0, a2x, kkw, kaw)


def even_prep_bwd(ps, mu, w0, w2x, a0, a2x, kkw, kaw, dr, dlw, dk2, dv, daa, dbb, dr2, dk22, dv2):
    tm = PREP_TM
    nb = T // tm
    rev = lambda i: nb - 1 - i

    def body(ps_ref, prev_ref, mu_ref, w0_ref, w2_ref, a0_ref, a2_ref, kk_ref, ka_ref,
             dr_ref, dlw_ref, dk2_ref, dv_ref, daa_ref, dbb_ref, dr2_ref, dk22_ref, dv2_ref,
             dps_ref, dmu_ref, dw0_ref, dw2_ref, da0_ref, da2_ref, dkk_ref, dka_ref, carry):
        i = pl.program_id(0)
        blk = rev(i)
        mu_v = mu_ref[...]
        p, p_prev, s = _shifted(ps_ref, prev_ref, mu_v, blk)
        wa = s[:, 3 * W:]
        th = jnp.tanh(wa)
        wl = w0_ref[...] + _bdot(th, w2_ref[...])
        apre = a0_ref[...] + _bdot(wa, a2_ref[...])
        bd = _head_blockdiag()
        k = s[:, W:2 * W]
        dk, dwl, dap, dkkw, dkaw = _prep_elem_bwd(k, wl, apre, kk_ref[...], ka_ref[...], bd, dlw_ref[...],
                                                  dk2_ref[...] + dk22_ref[...], daa_ref[...], dbb_ref[...])
        dwa = _bdot_nt(dwl, w2_ref[...]) * (1.0 - th * th) + _bdot_nt(dap, a2_ref[...])
        ds = jnp.concatenate([dr_ref[...] + dr2_ref[...], dk, dv_ref[...] + dv2_ref[...], dwa], axis=-1)

        @pl.when(i == 0)
        def _():
            for ref in (dmu_ref, dw0_ref, dw2_ref, da0_ref, da2_ref, dkk_ref, dka_ref, carry):
                ref[...] = jnp.zeros_like(ref)

        dmu_ref[...] += jnp.sum(ds * (p_prev - p), axis=0, keepdims=True)
        dw0_ref[...] += jnp.sum(dwl, axis=0, keepdims=True)
        da0_ref[...] += jnp.sum(dap, axis=0, keepdims=True)
        dw2_ref[...] += _bdot_tn(th, dwl)
        da2_ref[...] += _bdot_tn(wa, dap)
        dkk_ref[...] += dkkw
        dka_ref[...] += dkaw
        dsm = ds * mu_v
        last = (blk % PREP_NB) == PREP_NB - 1
        nxt = jnp.where(last, 0.0, carry[0:1, :])
        up = pltpu.roll(dsm, tm - 1, 0)
        up = jnp.where(_iota2(up.shape, 0) == tm - 1, nxt, up)
        dps_ref[...] = (ds - dsm + up).astype(BF16)
        carry[0:1, :] = dsm[0:1, :]

    vec = _const_spec((1, W))
    rrow = lambda width: pl.BlockSpec((tm, width), lambda i: (rev(i), 0))
    return pl.pallas_call(
        body, grid=(nb,), name="even_prep_bwd",
        in_specs=[rrow(SHIFT), _prev_spec(SHIFT, rev), _const_spec((1, SHIFT)), vec,
                  _const_spec((2 * LORA, W)), vec, _const_spec((2 * LORA, W)), vec, vec] + [rrow(W)] * 9,
        out_specs=[rrow(SHIFT), _const_spec((1, SHIFT)), vec, _const_spec((2 * LORA, W)), vec,
                   _const_spec((2 * LORA, W)), vec, vec],
        out_shape=[jax.ShapeDtypeStruct((T, SHIFT), BF16), jax.ShapeDtypeStruct((1, SHIFT), F32),
                   jax.ShapeDtypeStruct((1, W), F32), jax.ShapeDtypeStruct((2 * LORA, W), F32),
                   jax.ShapeDtypeStruct((1, W), F32), jax.ShapeDtypeStruct((2 * LORA, W), F32),
                   jax.ShapeDtypeStruct((1, W), F32), jax.ShapeDtypeStruct((1, W), F32)],
        scratch_shapes=[pltpu.VMEM((8, SHIFT), F32)],
        compiler_params=_cparams(("arbitrary",), VMEM_BIG),
    )(ps, ps, mu, w0, w2x, a0, a2x, kkw, kaw, dr, dlw, dk2, dv, daa, dbb, dr2, dk22, dv2)


NPAIR = NH // 2
PW = 2 * HD


def _pair_cols(p):
    return slice(p * PW, (p + 1) * PW)


def _pairs(a):
    return [a[:, _pair_cols(p)] for p in range(NPAIR)]


def _stack_pair(a):
    first = _iota2(a.shape, 1) < HD
    zero = jnp.zeros_like(a)
    return jnp.concatenate([jnp.where(first, a, zero), jnp.where(first, zero, a)], axis=0)


def _unstack_pair(a):
    n = a.shape[0] // 2
    return jnp.where(_iota2((n, PW), 1) < HD, a[:n], a[n:])


def _fold_pair(a):
    n = a.shape[0] // 2
    return a[:n] + a[n:]


def _chunk_masks():
    n = 4 * L
    row = _iota2((n, n), 0)
    col = _iota2((n, n), 1)
    same = ((row // L) & 1) == ((col // L) & 1)
    ri = row & (L - 1)
    ci = col & (L - 1)
    keep = same & (((row < 2 * L) & (ri > ci)) | ((row >= 2 * L) & (ri >= ci)))
    r1 = _iota2((L, L), 0)
    c1 = _iota2((L, L), 1)
    r2 = _iota2((2 * L, 2 * L), 0)
    c2 = _iota2((2 * L, 2 * L), 1)
    return keep.astype(F32), (r1 >= c1).astype(F32), (r2 == c2).astype(F32)


def _scaled(r, lw, k2, aa, bb, tri):
    g = _hdot(tri, lw)
    eg = jnp.exp(g)
    eng = jnp.exp(-g)
    egp = jnp.exp(g - lw)
    return eg, eng, egp, aa * egp, r * eg, bb * eng, k2 * eng


def _head_cols(h):
    return slice(h * HD, (h + 1) * HD)


def _per_head(a):
    return [a[:, _head_cols(h)] for h in range(NH)]


def _pairs_operands(at, rt, bt, kt):
    x = [jnp.concatenate([_stack_pair(a), _stack_pair(r)], axis=0).astype(BF16) for a, r in zip(_pairs(at), _pairs(rt))]
    yk = [jnp.concatenate([_stack_pair(b), _stack_pair(k)], axis=0).astype(BF16) for b, k in zip(_pairs(bt), _pairs(kt))]
    return x, yk


def _pairs_matrices(x, yk, keep, eye):
    m = [_bdot_nt(a, b) * keep for a, b in zip(x, yk)]
    p = [a[:2 * L, :2 * L] for a in m]
    tinv = [eye + a for a in p]
    for _ in range(5):
        p = [_bdot(a, a) for a in p]
        tinv = [t + _bdot(t, a) for t, a in zip(tinv, p)]
    return [a.astype(BF16) for a in m], [a.astype(BF16) for a in tinv]


def _pairs_fwd(x, yk, m, tinv, vw, s0, egl):
    xh = [_bdot_nt(a, s) for a, s in zip(x, s0)]
    u = [_bdot(t, h[:2 * L] + _bdot(a[:2 * L, 2 * L:], w)) for t, h, a, w in zip(tinv, xh, m, vw)]
    uv = [jnp.concatenate([a, w], axis=0).astype(BF16) for a, w in zip(u, vw)]
    y = [h[2 * L:] + _bdot(a[2 * L:], w) for h, a, w in zip(xh, m, uv)]
    sn = [e * (s + _bdot_tn(w, b)) for e, s, w, b in zip(egl, s0, uv, yk)]
    return y, sn, uv


def _pairs_bwd(x, yk, m, tinv, uv, s0, sn, egl, dyw, dsn, keep):
    dzs = [d * e for d, e in zip(dsn, egl)]
    dgl = [jnp.sum(d * s, axis=0, keepdims=True) for d, s in zip(dsn, sn)]
    dyb = [a.astype(BF16) for a in dyw]
    t1 = [_bdot_tn(a[2 * L:], d) for a, d in zip(m, dyb)]
    t2 = [_bdot_nt(b, d) for b, d in zip(yk, dzs)]
    drhs = [_bdot_tn(t, a[:2 * L] + b[:2 * L]) for t, a, b in zip(tinv, t1, t2)]
    dv = [a[2 * L:] + b[2 * L:] + _bdot_tn(c[:2 * L, 2 * L:], d) for a, b, c, d in zip(t1, t2, m, drhs)]
    gg = [jnp.concatenate([a, b], axis=0).astype(BF16) for a, b in zip(drhs, dyw)]
    ds0 = [d + _bdot_tn(g, a) for d, g, a in zip(dzs, gg, x)]
    dm = [_bdot_nt(g, w) * keep for g, w in zip(gg, uv)]
    dx = [_bdot(g, s) + _bdot(d, b) for g, s, d, b in zip(gg, s0, dm, yk)]
    dyk = [_bdot_tn(d, a) + _bdot(w, z) for d, a, w, z in zip(dm, x, uv, dzs)]
    return dx, dyk, dv, dgl, ds0


STATE_SHAPE = (NPAIR * PW, PW)
M_SHAPE = (4 * L, NPAIR * 4 * L)
TINV_SHAPE = (2 * L, NPAIR * 2 * L)


def _rows_of(a, n):
    return [a[i * n:(i + 1) * n, :] for i in range(NPAIR)]


def _both(f):
    out = []
    for s in range(NSEQ):
        out += f(s)
    return out


def _seq_view(a):
    return a.reshape(NSEQ, SEQ, a.shape[-1])


UV_SHAPE = (4 * L, NPAIR * PW)
RW_CHUNKS = 2


def rwkv_fwd(r, lw, k2, v, aa, bb):
    def body(r_ref, lw_ref, k2_ref, v_ref, aa_ref, bb_ref, y_ref, hs_ref, hn_ref, m_ref, t_ref, uv_ref, state):
        @pl.when(pl.program_id(0) == 0)
        def _():
            state[...] = jnp.zeros_like(state)

        keep, tri, eye = _chunk_masks()
        where = [(j, s) for j in range(RW_CHUNKS) for s in range(NSEQ)]
        rows = lambda j: slice(j * L, (j + 1) * L)
        sc = [_scaled(r_ref[s, rows(j)], lw_ref[s, rows(j)], k2_ref[s, rows(j)], aa_ref[s, rows(j)],
                      bb_ref[s, rows(j)], tri) for j, s in where]
        ops = [_pairs_operands(*a[3:]) for a in sc]
        m, tinv = _pairs_matrices([a for o in ops for a in o[0]], [a for o in ops for a in o[1]], keep, eye)
        s_cur = [state[s] for s in range(NSEQ)]
        for j in range(RW_CHUNKS):
            mine = slice(j * NSEQ * NPAIR, (j + 1) * NSEQ * NPAIR)
            x = [a for o in ops[j * NSEQ:(j + 1) * NSEQ] for a in o[0]]
            yk = [a for o in ops[j * NSEQ:(j + 1) * NSEQ] for a in o[1]]
            vw = _both(lambda s: [_stack_pair(a) for a in _pairs(v_ref[s, rows(j)])])
            egl = _both(lambda s: _pairs(sc[j * NSEQ + s][0][L - 1:L, :]))
            y, sn, uv = _pairs_fwd(x, yk, m[mine], tinv[mine], vw, _both(lambda s: _rows_of(s_cur[s], PW)), egl)
            for s in range(NSEQ):
                ps = slice(s * NPAIR, (s + 1) * NPAIR)
                hs_ref[j, s] = s_cur[s]
                y_ref[s, rows(j)] = jnp.concatenate([_fold_pair(a) for a in y[ps]], axis=-1)
                m_ref[j, s] = jnp.concatenate(m[mine][ps], axis=-1)
                t_ref[j, s] = jnp.concatenate(tinv[mine][ps], axis=-1)
                uv_ref[j, s] = jnp.concatenate(uv[ps], axis=-1)
                s_cur[s] = jnp.concatenate(sn[ps], axis=0)
                hn_ref[j, s] = s_cur[s]
        for s in range(NSEQ):
            state[s] = s_cur[s]

    blk = pl.BlockSpec((NSEQ, RW_CHUNKS * L, W), lambda c: (0, c, 0))
    per_chunk = lambda shape: pl.BlockSpec((RW_CHUNKS, NSEQ) + shape, lambda c: (c, 0, 0, 0))
    saved_shapes = [(STATE_SHAPE, F32), (STATE_SHAPE, F32), (M_SHAPE, BF16), (TINV_SHAPE, BF16), (UV_SHAPE, BF16)]
    y, *saved = pl.pallas_call(
        body, grid=(NC // RW_CHUNKS,), name="rwkv_fwd",
        in_specs=[blk] * 6,
        out_specs=[blk] + [per_chunk(shape) for shape, _ in saved_shapes],
        out_shape=[jax.ShapeDtypeStruct((NSEQ, SEQ, W), F32)]
        + [jax.ShapeDtypeStruct((NC, NSEQ) + shape, dt) for shape, dt in saved_shapes],
        scratch_shapes=[pltpu.VMEM((NSEQ,) + STATE_SHAPE, F32)],
        compiler_params=_cparams(("arbitrary",), VMEM_BIG),
    )(*[_seq_view(a) for a in (r, lw, k2, v, aa, bb)])
    return y.reshape(T, W), saved


def rwkv_bwd(r, lw, k2, aa, bb, saved, dy):
    def body(r_ref, lw_ref, k2_ref, aa_ref, bb_ref, hs_ref, hn_ref, m_ref, t_ref, uv_ref, dy_ref,
             dr_ref, dlw_ref, dk2_ref, dv_ref, daa_ref, dbb_ref, dstate):
        @pl.when(pl.program_id(0) == 0)
        def _():
            dstate[...] = jnp.zeros_like(dstate)

        keep, tri, _ = _chunk_masks()
        sc = [_scaled(r_ref[s], lw_ref[s], k2_ref[s], aa_ref[s], bb_ref[s], tri) for s in range(NSEQ)]
        ops = [_pairs_operands(*sc[s][3:]) for s in range(NSEQ)]
        x, yk = _both(lambda s: ops[s][0]), _both(lambda s: ops[s][1])
        m = _both(lambda s: [m_ref[0, s][:, i * 4 * L:(i + 1) * 4 * L] for i in range(NPAIR)])
        tinv = _both(lambda s: [t_ref[0, s][:, i * 2 * L:(i + 1) * 2 * L] for i in range(NPAIR)])
        uv = _both(lambda s: _pairs(uv_ref[0, s]))
        dyw = _both(lambda s: [_stack_pair(a) for a in _pairs(dy_ref[s])])
        s0 = _both(lambda s: _rows_of(hs_ref[0, s], PW))
        sn = _both(lambda s: _rows_of(hn_ref[0, s], PW))
        dsn = _both(lambda s: _rows_of(dstate[s], PW))
        egl = _both(lambda s: _pairs(sc[s][0][L - 1:L, :]))
        dx, dyk, dvw, dgl, ds0 = _pairs_bwd(x, yk, m, tinv, uv, s0, sn, egl, dyw, dsn, keep)
        for s in range(NSEQ):
            mine = slice(s * NPAIR, (s + 1) * NPAIR)
            eg, eng, egp, at, rt, bt, kt = sc[s]
            dstate[s] = jnp.concatenate(ds0[mine], axis=0)
            dv_ref[s] = jnp.concatenate([_fold_pair(a) for a in dvw[mine]], axis=-1)
            dat = jnp.concatenate([_fold_pair(a[:2 * L]) for a in dx[mine]], axis=-1)
            drt = jnp.concatenate([_fold_pair(a[2 * L:]) for a in dx[mine]], axis=-1)
            dbt = jnp.concatenate([_fold_pair(a[:2 * L]) for a in dyk[mine]], axis=-1)
            dkt = jnp.concatenate([_fold_pair(a[2 * L:]) for a in dyk[mine]], axis=-1)
            dg = drt * rt - dbt * bt - dkt * kt
            dg = dg + jnp.where(_iota2(dg.shape, 0) == L - 1, jnp.concatenate(dgl[mine], axis=-1), 0.0)
            dgp = dat * at
            dlw_ref[s] = _hdot_tn(tri, dg + dgp) - dgp
            dr_ref[s] = drt * eg
            daa_ref[s] = dat * egp
            dbb_ref[s] = dbt * eng
            dk2_ref[s] = dkt * eng

    blk = pl.BlockSpec((NSEQ, L, W), lambda c: (0, NC - 1 - c, 0))
    per_chunk = lambda shape: pl.BlockSpec((1, NSEQ) + shape, lambda c: (NC - 1 - c, 0, 0, 0))
    outs = pl.pallas_call(
        body, grid=(NC,), name="rwkv_bwd",
        in_specs=[blk] * 5 + [per_chunk(a.shape[2:]) for a in saved] + [blk],
        out_specs=[blk] * 6,
        out_shape=[jax.ShapeDtypeStruct((NSEQ, SEQ, W), F32)] * 6,
        scratch_shapes=[pltpu.VMEM((NSEQ,) + STATE_SHAPE, F32)],
        compiler_params=_cparams(("arbitrary",)),
    )(*[_seq_view(a) for a in (r, lw, k2, aa, bb)], *saved, _seq_view(dy))
    return [a.reshape(T, W) for a in outs]


def _post_math(y, r, k2, v, ga, o, gb, lng, lnb, rk, bd):
    mu = _headsum(y, bd) * (1.0 / HD)
    yc = y - mu
    var = _headsum(yc * yc, bd) * (1.0 / HD)
    yn = yc * lax.rsqrt(var + GN_EPS) * lng + lnb
    bonus = _headsum(r * k2 * rk, bd) * v
    return (yn + bonus) * _silu(ga), o * _silu(gb)


def even_post(y, r, k2, v, ga, o, gb, lng, lnb, rk, h, w_bf):
    tm = 512

    def body(y_ref, r_ref, k2_ref, v_ref, ga_ref, o_ref, gb_ref, lng_ref, lnb_ref, rk_ref, h_ref, w_ref,
             ho_ref, zt_ref):
        ya, yb = _post_math(y_ref[...], r_ref[...], k2_ref[...], v_ref[...], ga_ref[...], o_ref[...], gb_ref[...],
                            lng_ref[...], lnb_ref[...], rk_ref[...], _head_blockdiag())
        z = jnp.concatenate([ya.astype(BF16), yb.astype(BF16)], axis=-1)
        zt_ref[...] = z.T
        ho_ref[...] = h_ref[...] + jnp.dot(z, w_ref[...], preferred_element_type=F32)

    vec = _const_spec((1, W))
    return pl.pallas_call(
        body, grid=(T // tm,), name="even_post",
        in_specs=[_row_spec(tm, W)] * 7 + [vec] * 3 + [_row_spec(tm, D), _const_spec((D, D))],
        out_specs=[_row_spec(tm, D), _col_spec(D, tm)],
        out_shape=[jax.ShapeDtypeStruct((T, D), F32), jax.ShapeDtypeStruct((D, T), BF16)],
        compiler_params=_cparams(("parallel",), VMEM_BIG),
    )(y, r, k2, v, ga, o, gb, lng, lnb, rk, h, w_bf)


def even_post_bwd(y, r, k2, v, ga, o, gb, lng, lnb, rk, dh, zt_bf, w_bf, after=None):
    tm = 512
    extra_specs, extra = _after_operand(after)

    def body(y_ref, r_ref, k2_ref, v_ref, ga_ref, o_ref, gb_ref, lng_ref, lnb_ref, rk_ref, dh_ref, zt_ref, w_ref,
             *rest):
        dy_ref, dr_ref, dk2_ref, dv_ref, dga_ref, do_ref, dgb_ref, dlng_ref, dlnb_ref, drk_ref, dw_ref = rest[-11:]
        dzv = _out_proj_back(dh_ref, zt_ref, w_ref, dw_ref)
        bd = _head_blockdiag()
        _, vjp = jax.vjp(lambda *a: _post_math(*a, bd), y_ref[...], r_ref[...], k2_ref[...], v_ref[...], ga_ref[...],
                         o_ref[...], gb_ref[...], lng_ref[...], lnb_ref[...], rk_ref[...])
        dy, dr, dk2, dv, dga, do, dgb, dlng, dlnb, drk = vjp((dzv[:, 0:W], dzv[:, W:2 * W]))
        for ref, val in ((dy_ref, dy), (dr_ref, dr), (dk2_ref, dk2), (dv_ref, dv), (dga_ref, dga), (do_ref, do),
                         (dgb_ref, dgb)):
            ref[...] = val.astype(ref.dtype)

        @pl.when(pl.program_id(0) == 0)
        def _():
            for ref in (dlng_ref, dlnb_ref, drk_ref):
                ref[...] = jnp.zeros_like(ref)

        dlng_ref[...] += dlng
        dlnb_ref[...] += dlnb
        drk_ref[...] += drk

    vec = _const_spec((1, W))
    return pl.pallas_call(
        body, grid=(T // tm,), name="even_post_bwd",
        in_specs=[_row_spec(tm, W)] * 7 + [vec] * 3 + [_row_spec(tm, D), _col_spec(D, tm), _const_spec((D, D))]
        + extra_specs,
        out_specs=[_row_spec(tm, W)] * 7 + [vec] * 3 + [_const_spec((D, D))],
        out_shape=[jax.ShapeDtypeStruct((T, W), dt) for dt in (F32, F32, F32, F32, BF16, F32, BF16)]
        + [jax.ShapeDtypeStruct((1, W), F32)] * 3 + [jax.ShapeDtypeStruct((D, D), F32)],
        compiler_params=_cparams(("arbitrary",), VMEM_BIG),
    )(y, r, k2, v, ga, o, gb, lng, lnb, rk, dh, zt_bf, w_bf, *extra)


PADSEQ = SEQ + LEFT * L
ATT_SCALE = 1.0 / math.sqrt(HD)
ATT_Q = 4
WIN = BAND + (ATT_Q - 1) * L
ATT_STEPS = NC // ATT_Q
ATT_BIAS_SHAPE = (NPAIR, ATT_Q * 2 * L, WIN)


def _stack_chunks(a):
    return jnp.concatenate([_stack_pair(a[i * L:(i + 1) * L]) for i in range(ATT_Q)], axis=0)


def _unstack_chunks(a):
    return jnp.concatenate([_unstack_pair(a[i * 2 * L:(i + 1) * 2 * L]) for i in range(ATT_Q)], axis=0)


def window_bias(bias):
    parts = [jnp.pad(bias, ((0, 0), (0, 0), (i * L, (ATT_Q - 1 - i) * L)), constant_values=NEG) for i in range(ATT_Q)]
    return jnp.concatenate(parts, axis=1)


def _att_probs(q2, kw, bias, step):
    valid = _iota2((1, WIN), 1) >= (LEFT - step * ATT_Q) * L
    s = [jnp.where(valid, _bdot_nt(a, b) * ATT_SCALE + bias[p], NEG) for p, (a, b) in enumerate(zip(q2, kw))]
    e = [jnp.exp(a - jnp.max(a, axis=-1, keepdims=True)) for a in s]
    return [a / jnp.sum(a, axis=-1, keepdims=True) for a in e]


def attention_fwd(q, kpad, vpad, bias):
    def body(q_ref, k_ref, v_ref, b_ref, o_ref):
        step = pl.program_id(1)
        start = pl.multiple_of(step * (ATT_Q * L), L)
        kw = _pairs(k_ref[pl.ds(start, WIN), :])
        vw = _pairs(v_ref[pl.ds(start, WIN), :])
        q2 = [_stack_chunks(a) for a in _pairs(q_ref[...].astype(BF16))]
        p = _att_probs(q2, kw, b_ref[...], step)
        o_ref[...] = jnp.concatenate([_unstack_chunks(_bdot(a, b)) for a, b in zip(p, vw)], axis=-1)

    qblk = pl.BlockSpec((ATT_Q * L, W), lambda b, c: (b * ATT_STEPS + c, 0))
    kblk = pl.BlockSpec((PADSEQ, W), lambda b, c: (b, 0))
    return pl.pallas_call(
        body, grid=(NSEQ, ATT_STEPS), name="attention_fwd",
        in_specs=[qblk, kblk, kblk, _const_spec(ATT_BIAS_SHAPE)],
        out_specs=qblk, out_shape=jax.ShapeDtypeStruct((T, W), F32),
        compiler_params=_cparams(("parallel", "arbitrary")),
    )(q, kpad, vpad, bias)


def attention_bwd(q, kpad, vpad, bias, do):
    def body(q_ref, k_ref, v_ref, b_ref, do_ref, dq_ref, dko_ref, dvo_ref, db_ref, dk_ref, dv_ref):
        b = pl.program_id(0)
        c = pl.program_id(1)

        @pl.when(c == 0)
        def _():
            dk_ref[...] = jnp.zeros_like(dk_ref)
            dv_ref[...] = jnp.zeros_like(dv_ref)

        @pl.when((c == 0) & (b == 0))
        def _():
            db_ref[...] = jnp.zeros_like(db_ref)

        start = pl.multiple_of(c * (ATT_Q * L), L)
        kw = _pairs(k_ref[pl.ds(start, WIN), :])
        vw = _pairs(v_ref[pl.ds(start, WIN), :])
        q2 = [_stack_chunks(a) for a in _pairs(q_ref[...].astype(BF16))]
        do2 = [_stack_chunks(a) for a in _pairs(do_ref[...].astype(BF16))]
        p = _att_probs(q2, kw, b_ref[...], c)
        dp = [_bdot_nt(a, b) for a, b in zip(do2, vw)]
        ds = [a * (d - jnp.sum(d * a, axis=-1, keepdims=True)) for a, d in zip(p, dp)]
        dss = [(a * ATT_SCALE).astype(BF16) for a in ds]
        dq_ref[...] = jnp.concatenate([_unstack_chunks(_bdot(a, b)) for a, b in zip(dss, kw)], axis=-1).astype(BF16)
        dk_ref[pl.ds(start, WIN), :] += jnp.concatenate([_bdot_tn(a, b) for a, b in zip(dss, q2)], axis=-1)
        dv_ref[pl.ds(start, WIN), :] += jnp.concatenate([_bdot_tn(a, b) for a, b in zip(p, do2)], axis=-1)
        for i in range(NPAIR):
            db_ref[i] += ds[i]

        @pl.when(c == ATT_STEPS - 1)
        def _():
            dko_ref[...] = dk_ref[LEFT * L:, :].astype(BF16)
            dvo_ref[...] = dv_ref[LEFT * L:, :].astype(BF16)

    qblk = pl.BlockSpec((ATT_Q * L, W), lambda b, c: (b * ATT_STEPS + c, 0))
    kblk = pl.BlockSpec((PADSEQ, W), lambda b, c: (b, 0))
    sblk = pl.BlockSpec((SEQ, W), lambda b, c: (b, 0))
    bblk = _const_spec(ATT_BIAS_SHAPE)
    return pl.pallas_call(
        body, grid=(NSEQ, ATT_STEPS), name="attention_bwd",
        in_specs=[qblk, kblk, kblk, bblk, qblk],
        out_specs=[qblk, sblk, sblk, bblk],
        out_shape=[jax.ShapeDtypeStruct((T, W), BF16), jax.ShapeDtypeStruct((T, W), BF16),
                   jax.ShapeDtypeStruct((T, W), BF16), jax.ShapeDtypeStruct(ATT_BIAS_SHAPE, F32)],
        scratch_shapes=[pltpu.VMEM((PADSEQ, W), F32), pltpu.VMEM((PADSEQ, W), F32)],
        compiler_params=_cparams(("arbitrary", "arbitrary"), VMEM_BIG),
    )(q, kpad, vpad, bias, do)


NTAB = 2 * CLIP + 1
EXT = BAND + L


def _ext_onehot():
    n = _iota2((EXT, NTAB), 0)
    m = _iota2((EXT, NTAB), 1)
    return (jnp.clip(BAND - 1 - n, -CLIP, CLIP) + CLIP == m).astype(F32)


def bias_expand(table):
    def body(t_ref, o_ref):
        ext = _hdot_nt(t_ref[...], _ext_onehot())
        for i in range(L):
            s = L - 1 - i
            o_ref[:, i, :] = (pltpu.roll(ext, EXT - s, 1) if s else ext)[:, :BAND]

    return pl.pallas_call(body, name="bias_expand", out_shape=jax.ShapeDtypeStruct((NH, L, BAND), F32))(table)


def bias_grad(dbias):
    def body(d_ref, o_ref):
        acc = jnp.zeros((NH, EXT), F32)
        zpad = jnp.zeros((NH, EXT - BAND), F32)
        for i in range(L):
            s = L - 1 - i
            row = jnp.concatenate([d_ref[:, i, :], zpad], axis=-1)
            acc = acc + (pltpu.roll(row, s, 1) if s else row)
        o_ref[...] = _hdot(acc, _ext_onehot())

    return pl.pallas_call(body, name="bias_grad", out_shape=jax.ShapeDtypeStruct((NH, NTAB), F32))(dbias)


def _group_cols(g):
    return slice(g * SGC, (g + 1) * SGC)


def _sg_norm(gv, lng, lnb):
    gc = gv - jnp.mean(gv, axis=-1, keepdims=True)
    rstd = lax.rsqrt(jnp.mean(gc * gc, axis=-1, keepdims=True) + LN_EPS)
    xhat = gc * rstd
    return xhat, rstd, xhat * lng + lnb


GMLP_BWD_CHUNKS = 2


def gmlp_fwd_loss(u, v, gate, lng, lnb, wm_bf, sgb_t, h, w_bf, g_final, target):
    tm = GMLP_BWD_CHUNKS * SGC

    def body(u_ref, v_ref, gt_ref, lng_ref, lnb_ref, wm_ref, sb_ref, h_ref, w_ref, g_ref, t_ref,
             dh_ref, loss_ref, dg_ref, zt_ref):
        zs = []
        for ch in range(GMLP_BWD_CHUNKS):
            rows = slice(ch * SGC, (ch + 1) * SGC)
            _, _, vln = _sg_norm(_gelu(v_ref[rows, :]), lng_ref[...], lnb_ref[...])
            vlb = vln.astype(BF16)
            zg = []
            for g in range(NG):
                cs = _group_cols(g)
                sv = jnp.dot(wm_ref[g], vlb[:, cs], preferred_element_type=F32) + sb_ref[:, g:g + 1]
                zg.append((_gelu(u_ref[rows, cs]) * sv * _silu(gt_ref[rows, cs])).astype(BF16))
            zs.append(jnp.concatenate(zg, axis=-1))
        z = jnp.concatenate(zs, axis=0)
        zt_ref[...] = z.T
        xv = h_ref[...] + jnp.dot(z, w_ref[...], preferred_element_type=F32)
        rstd = lax.rsqrt(jnp.mean(xv * xv, axis=-1, keepdims=True) + RMS_EPS)
        xhat = xv * rstd
        err = xhat * g_ref[...] - t_ref[...]
        part = 0.5 * jnp.sum(jnp.mean(err * err, axis=-1, keepdims=True), axis=0, keepdims=True)
        dout = err * (1.0 / D)

        @pl.when(pl.program_id(0) == 0)
        def _():
            loss_ref[...] = jnp.zeros_like(loss_ref)
            dg_ref[...] = jnp.zeros_like(dg_ref)

        loss_ref[...] += jnp.broadcast_to(part, loss_ref.shape)
        dg_ref[...] += jnp.sum(dout * xhat, axis=0, keepdims=True)
        dxh = dout * g_ref[...]
        dh_ref[...] = rstd * (dxh - xhat * jnp.mean(dxh * xhat, axis=-1, keepdims=True))

    return pl.pallas_call(
        body, grid=(T // tm,), name="gmlp_fwd_loss",
        in_specs=[_row_spec(tm, D)] * 3 + [_const_spec((1, D))] * 2
        + [_const_spec((NG, SGC, SGC)), _const_spec((SGC, NG)), _row_spec(tm, D), _const_spec((D, D)),
           _const_spec((1, D)), _row_spec(tm, D)],
        out_specs=[_row_spec(tm, D), _const_spec((8, 128)), _const_spec((1, D)), _col_spec(D, tm)],
        out_shape=[jax.ShapeDtypeStruct((T, D), F32), jax.ShapeDtypeStruct((8, 128), F32),
                   jax.ShapeDtypeStruct((1, D), F32), jax.ShapeDtypeStruct((D, T), BF16)],
        compiler_params=_cparams(("arbitrary",), VMEM_BIG),
    )(u, v, gate, lng, lnb, wm_bf, sgb_t, h, w_bf, g_final, target)


def gmlp_bwd(u, v, gate, lng, lnb, wm_bf, sgb_t, dh, zt_bf, w_bf):
    def body(u_ref, v_ref, gt_ref, lng_ref, lnb_ref, wm_ref, sb_ref, dh_ref, zt_ref, w_ref,
             du_ref, dv_ref, dgt_ref, dlng_ref, dlnb_ref, dwm_ref, dsb_ref, dw_ref):
        @pl.when(pl.program_id(0) == 0)
        def _():
            for ref in (dlng_ref, dlnb_ref, dwm_ref, dsb_ref):
                ref[...] = jnp.zeros_like(ref)

        dz = _out_proj_back(dh_ref, zt_ref, w_ref, dw_ref)
        sel = (_iota2((D, NG), 0) // SGC == _iota2((D, NG), 1)).astype(F32)
        for ch in range(GMLP_BWD_CHUNKS):
            rows = slice(ch * SGC, (ch + 1) * SGC)
            gv, dgv_dv = _gelu_both(v_ref[rows, :])
            xhat, rstd, vln = _sg_norm(gv, lng_ref[...], lnb_ref[...])
            vlb = vln.astype(BF16)
            dvln = []
            dsv_all = []
            for g in range(NG):
                cs = _group_cols(g)
                uu = u_ref[rows, cs]
                gg = gt_ref[rows, cs]
                dzz = dz[rows, cs]
                sv = jnp.dot(wm_ref[g], vlb[:, cs], preferred_element_type=F32) + sb_ref[:, g:g + 1]
                gu, dgu = _gelu_both(uu)
                sg, dsg = _silu_both(gg)
                dzgu = dzz * gu
                dsv = dzgu * sg
                dgt_ref[rows, cs] = (dzgu * sv * dsg).astype(BF16)
                du_ref[rows, cs] = (dzz * sv * sg * dgu).astype(BF16)
                dsb16 = dsv.astype(BF16)
                dvln.append(lax.dot_general(wm_ref[g], dsb16, (((0,), (0,)), ((), ())), preferred_element_type=F32))
                dwm_ref[g] += lax.dot_general(dsb16, vlb[:, cs], (((1,), (1,)), ((), ())),
                                              preferred_element_type=F32)
                dsv_all.append(dsv)
            dvl = jnp.concatenate(dvln, axis=-1)
            dsb_ref[...] += _hdot(jnp.concatenate(dsv_all, axis=-1), sel)
            dlng_ref[...] += jnp.sum(dvl * xhat, axis=0, keepdims=True)
            dlnb_ref[...] += jnp.sum(dvl, axis=0, keepdims=True)
            dxh = dvl * lng_ref[...]
            dgv = rstd * (dxh - jnp.mean(dxh, axis=-1, keepdims=True)
                          - xhat * jnp.mean(dxh * xhat, axis=-1, keepdims=True))
            dv_ref[rows, :] = (dgv * dgv_dv).astype(BF16)

    tm = GMLP_BWD_CHUNKS * SGC
    return pl.pallas_call(
        body, grid=(T // tm,), name="gmlp_bwd",
        in_specs=[_row_spec(tm, D)] * 3 + [_const_spec((1, D))] * 2
        + [_const_spec((NG, SGC, SGC)), _const_spec((SGC, NG)), _row_spec(tm, D), _col_spec(D, tm),
           _const_spec((D, D))],
        out_specs=[_row_spec(tm, D)] * 3 + [_const_spec((1, D))] * 2
        + [_const_spec((NG, SGC, SGC)), _const_spec((SGC, NG)), _const_spec((D, D))],
        out_shape=[jax.ShapeDtypeStruct((T, D), BF16)] * 3 + [jax.ShapeDtypeStruct((1, D), F32)] * 2
        + [jax.ShapeDtypeStruct((NG, SGC, SGC), F32), jax.ShapeDtypeStruct((SGC, NG), F32),
           jax.ShapeDtypeStruct((D, D), F32)],
        compiler_params=_cparams(("arbitrary",), VMEM_BIG),
    )(u, v, gate, lng, lnb, wm_bf, sgb_t, dh, zt_bf, w_bf)


NCHIP = 4
NDEV = 8
ANY = pl.BlockSpec(memory_space=pl.ANY)


HBM = pl.BlockSpec(memory_space=pltpu.HBM)
SEM = pl.BlockSpec(memory_space=pltpu.SEMAPHORE)
EFFECT = pltpu.SideEffectType.DATAFLOW_SIDE_EFFECTING


def _peers(whole_mesh):
    x, y, c = lax.axis_index("x"), lax.axis_index("y"), lax.axis_index("c")
    if not whole_mesh:
        return [((px, py, c), 2 * px + py) for px, py in ((1 - x, y), (x, 1 - y), (1 - x, 1 - y))], 2 * x + y
    out = []
    for j in range(1, NDEV):
        px, py, pc = x ^ (j >> 2), y ^ ((j >> 1) & 1), c ^ (j & 1)
        out.append(((px, py, pc), 4 * px + 2 * py + pc))
    return out, 4 * x + 2 * y + c


def _send_copies(src, land, send, recv, scatter, whole_mesh, starting):
    peers, me = _peers(whole_mesh)
    copies = []
    for t in range(len(src)):
        for j, (dev, slot) in enumerate(peers):
            k = t * len(peers) + j
            copies.append(pltpu.make_async_remote_copy(
                src_ref=src[t].at[slot] if scatter else src[t], dst_ref=land[t].at[me if starting else slot],
                send_sem=send.at[k], recv_sem=recv.at[k], device_id=dev, device_id_type=MESH))
    return copies


def send_start(srcs, lands, scatter, whole_mesh, name, after=None):
    n = len(srcs)
    nsem = n * (NDEV - 1 if whole_mesh else NCHIP - 1)
    extra_specs, extra = _after_operand(after)

    def body(*refs):
        send, recv = refs[2 * n + len(extra)], refs[2 * n + len(extra) + 1]
        for cp in _send_copies(refs[:n], refs[n:2 * n], send, recv, scatter, whole_mesh, True):
            cp.start()
        refs[-1][...] = jnp.zeros_like(refs[-1])

    arrs = list(srcs) + list(lands)
    out = pl.pallas_call(
        body, name=name,
        out_shape=(pltpu.SemaphoreType.DMA((nsem,)), pltpu.SemaphoreType.DMA((nsem,)),
                   *[pltpu.HBM(a.shape, a.dtype) for a in arrs], jax.ShapeDtypeStruct((8, 128), F32)),
        in_specs=[HBM] * (2 * n) + extra_specs,
        out_specs=(SEM, SEM, *[HBM] * (2 * n), pl.BlockSpec(memory_space=pltpu.VMEM)),
        input_output_aliases={i: 2 + i for i in range(2 * n)},
        compiler_params=pltpu.CompilerParams(has_side_effects=EFFECT),
    )(*[pltpu.with_memory_space_constraint(a, pltpu.HBM) for a in arrs], *extra)
    return out[0], out[1], list(out[2:2 + n]), list(out[2 + n:2 + 2 * n]), out[-1]


def send_wait(started, after, scatter, whole_mesh, name):
    send, recv, srcs, lands, _ = started
    n = len(srcs)

    def body(*refs):
        for cp in _send_copies(refs[:n], refs[n:2 * n], refs[2 * n], refs[2 * n + 1], scatter, whole_mesh, False):
            cp.wait_send()
            cp.wait_recv()

    arrs = list(srcs) + list(lands)
    out = pl.pallas_call(
        body, name=name, out_shape=tuple(pltpu.HBM(a.shape, a.dtype) for a in arrs),
        in_specs=[HBM] * (2 * n) + [SEM, SEM, ANY], out_specs=tuple([HBM] * (2 * n)),
        input_output_aliases={i: i for i in range(2 * n)},
        compiler_params=pltpu.CompilerParams(has_side_effects=EFFECT),
    )(*arrs, send, recv, after)
    return list(out[n:])


def exchange_c(arrs, name):
    n = len(arrs)

    def body(*refs):
        ins, outs = refs[:n], refs[n:2 * n]
        send, recv = refs[2 * n:]
        sibling = (lax.axis_index("x"), lax.axis_index("y"), 1 - lax.axis_index("c"))
        copies = [pltpu.make_async_remote_copy(src_ref=ins[t], dst_ref=outs[t], send_sem=send.at[t], recv_sem=recv.at[t],
                                               device_id=sibling, device_id_type=MESH) for t in range(n)]
        for cp in copies:
            cp.start()
        for cp in copies:
            cp.wait()

    return pl.pallas_call(
        body, name=name, in_specs=[ANY] * n, out_specs=[ANY] * n,
        out_shape=[jax.ShapeDtypeStruct(a.shape, a.dtype) for a in arrs],
        scratch_shapes=[pltpu.SemaphoreType.DMA((n,)), pltpu.SemaphoreType.DMA((n,))],
    )(*arrs)


def gather_weights(arrs, split):
    n = len(arrs)

    def body(*refs):
        ins, outs = refs[:n], refs[n:2 * n]
        send1, recv1, send2, recv2, loc = refs[2 * n:]
        x, y, c = lax.axis_index("x"), lax.axis_index("y"), lax.axis_index("c")
        me = 2 * x + y
        sibling = (x, y, 1 - c)
        peers = [(1 - x, y), (x, 1 - y), (1 - x, 1 - y)]

        def rows_of(t, core):
            half = arrs[t].shape[0] // 2
            return pl.ds(core * half, half)

        def part(ref, t, core):
            return ref.at[rows_of(t, core)] if split[t] else ref

        local = [pltpu.make_async_copy(ins[t], outs[t].at[me], loc.at[t]) for t in range(n)]
        for cp in local:
            cp.start()
        first = []
        for t in range(n):
            for j, (px, py) in enumerate(peers):
                first.append(pltpu.make_async_remote_copy(
                    src_ref=part(ins[t], t, c), dst_ref=part(outs[t].at[me], t, c), send_sem=send1.at[t, j],
                    recv_sem=recv1.at[t, j], device_id=(px, py, c), device_id_type=MESH))
        for cp in first:
            cp.start()
        passed = []
        for t in range(n):
            for j, (px, py) in enumerate(peers):
                landed = part(outs[t].at[2 * px + py], t, c)
                pltpu.make_async_remote_copy(
                    src_ref=landed, dst_ref=landed, send_sem=send1.at[t, j], recv_sem=recv1.at[t, j],
                    device_id=(x, y, c), device_id_type=MESH).wait_recv()
                if split[t]:
                    cp = pltpu.make_async_remote_copy(
                        src_ref=landed, dst_ref=landed, send_sem=send2.at[t, j], recv_sem=recv2.at[t, j],
                        device_id=sibling, device_id_type=MESH)
                    cp.start()
                    passed.append(cp)
        for t in range(n):
            for j, (px, py) in enumerate(peers):
                if split[t]:
                    other = part(outs[t].at[2 * px + py], t, 1 - c)
                    pltpu.make_async_remote_copy(
                        src_ref=other, dst_ref=other, send_sem=send2.at[t, j], recv_sem=recv2.at[t, j],
                        device_id=(x, y, c), device_id_type=MESH).wait_recv()
        for cp in first + passed:
            cp.wait_send()
        for cp in local:
            cp.wait()

    return pl.pallas_call(
        body, name="gather_weights", in_specs=[ANY] * n, out_specs=[ANY] * n,
        out_shape=[jax.ShapeDtypeStruct((NCHIP,) + a.shape, a.dtype) for a in arrs],
        scratch_shapes=[pltpu.SemaphoreType.DMA((n, 3))] * 4 + [pltpu.SemaphoreType.DMA((n,))],
    )(*arrs)


def _adam_math(g, w, m, v):
    m = ADAM_B1 * m + (1.0 - ADAM_B1) * g
    v = ADAM_B2 * v + (1.0 - ADAM_B2) * (g * g)
    m_hat = m / (1.0 - ADAM_B1 ** ADAM_STEP)
    v_hat = v / (1.0 - ADAM_B2 ** ADAM_STEP)
    delta = -ADAM_LR * (m_hat / (jnp.sqrt(v_hat) + ADAM_EPS) + ADAM_WD * w)
    return delta, m, v


def _rows_tile(rows):
    return rows if rows <= 256 else 256


def sum_chips(own, parts, name):
    _, rows, cols = parts.shape
    tr = _rows_tile(rows)

    def body(own_ref, p_ref, o_ref):
        acc = own_ref[...].astype(F32)
        for s in range(NCHIP):
            acc = acc + p_ref[s].astype(F32)
        o_ref[...] = acc

    return pl.pallas_call(
        body, grid=(rows // tr,), name=name,
        in_specs=[pl.BlockSpec((tr, cols), lambda i: (i, 0)), pl.BlockSpec((NCHIP, tr, cols), lambda i: (0, i, 0))],
        out_specs=pl.BlockSpec((tr, cols), lambda i: (i, 0)),
        out_shape=jax.ShapeDtypeStruct((rows, cols), F32),
        compiler_params=_cparams(("parallel",)),
    )(own, parts)


def adam_shard(p_mine, p_sib, w, m, v, name):
    rows, cols = p_mine.shape
    tr = _rows_tile(rows)
    lead = w.ndim == 3

    def body(a_ref, b_ref, w_ref, m_ref, v_ref, g_ref, d_ref, mo_ref, vo_ref):
        g = a_ref[...] + b_ref[...]
        g = g[None] if lead else g
        g_ref[...] = g
        d_ref[...], mo_ref[...], vo_ref[...] = _adam_math(g, w_ref[...], m_ref[...], v_ref[...])

    flat = pl.BlockSpec((tr, cols), lambda i: (i, 0))
    spec = pl.BlockSpec((1, tr, cols), lambda i: (0, i, 0)) if lead else flat
    return pl.pallas_call(
        body, grid=(rows // tr,), name=name, in_specs=[flat] * 2 + [spec] * 3, out_specs=[spec] * 4,
        out_shape=[jax.ShapeDtypeStruct(w.shape, F32)] * 4,
        compiler_params=_cparams(("parallel",)),
    )(p_mine, p_sib, w, m, v)


def adam_shard_halves_t(r_mine, r_sib, wt, mt, vt, name):
    hrows, cols = r_mine.shape
    tr = _rows_tile(hrows)
    per_half = hrows // tr

    def body(a_ref, b_ref, w_ref, m_ref, v_ref, g_ref, d_ref, mo_ref, vo_ref):
        mine = pl.program_id(0) == lax.axis_index("c")
        g = jnp.where(mine, a_ref[...], b_ref[...]).T[None]
        g_ref[...] = g
        d_ref[...], mo_ref[...], vo_ref[...] = _adam_math(g, w_ref[...], m_ref[...], v_ref[...])

    flat = pl.BlockSpec((tr, cols), lambda h, i: (i, 0))
    spec = pl.BlockSpec((1, cols, tr), lambda h, i: (0, 0, h * per_half + i))
    return pl.pallas_call(
        body, grid=(2, per_half), name=name, in_specs=[flat] * 2 + [spec] * 3, out_specs=[spec] * 4,
        out_shape=[jax.ShapeDtypeStruct(wt.shape, F32)] * 4,
        compiler_params=_cparams(("parallel", "parallel")),
    )(r_mine, r_sib, wt, mt, vt)


def adam_replicated(gathered, params, name):
    flat = []
    for i, p in enumerate(params):
        if isinstance(p, list):
            off = 0
            for wmv in p:
                n = gathered[i].shape[-1] - off if wmv[0] is None else wmv[0].shape[-1]
                flat.append((i, (off, n), wmv))
                off += n
        else:
            flat.append((i, None, p))
    ins = [a for _, _, wmv in flat for a in wmv if a is not None]
    ng = len(gathered)

    def body(*refs):
        g_refs = refs[:ng]
        in_refs = list(refs[ng:ng + len(ins)])
        out_refs = list(refs[ng + len(ins):])
        sums = []
        for r in g_refs:
            g = r[0]
            for d in range(1, NDEV):
                g = g + r[d]
            sums.append(g)
        for i, lanes, wmv in flat:
            g = sums[i] if lanes is None else sums[i][:, lanes[0]:lanes[0] + lanes[1]]
            out_refs.pop(0)[...] = g
            if wmv[0] is not None:
                w_ref, m_ref, v_ref = in_refs.pop(0), in_refs.pop(0), in_refs.pop(0)
                d_ref, mo_ref, vo_ref = out_refs.pop(0), out_refs.pop(0), out_refs.pop(0)
                d_ref[...], mo_ref[...], vo_ref[...] = _adam_math(g, w_ref[...], m_ref[...], v_ref[...])

    out_shape = []
    for i, lanes, wmv in flat:
        shape = gathered[i].shape[1:] if lanes is None else (1, lanes[1])
        out_shape += [jax.ShapeDtypeStruct(shape, F32)] * (4 if wmv[0] is not None else 1)
    outs = list(pl.pallas_call(body, name=name, out_shape=out_shape)(*gathered, *ins))
    return [[outs.pop(0) for _ in range(4 if wmv[0] is not None else 1)] for _, _, wmv in flat]


EVEN_SPLITS = (SHIFT, W, W, W, W, W)
ODD_SPLITS = (D, D, D)


def _cols_to_chips(a):
    rows, cols = a.shape
    return a.reshape(rows, NCHIP, cols // NCHIP).transpose(1, 0, 2)


def _chips_to_cols(a):
    _, rows, n = a.shape
    return a.transpose(1, 0, 2).reshape(rows, NCHIP * n)


def kernel(x, norm_g, w_in_e, shift_mu, rw_w0, rw_w2, rw_a0, rw_a2, rw_kk, rw_ka, rw_rk, rw_lnx_g, rw_lnx_b, att_bias, w_out_e, w_in_o, sg_ln_g, sg_ln_b, sg_w, sg_b, w_out_o, final_g, loss_target, m_norm_g, m_w_in_e, m_shift_mu, m_rw_w0, m_rw_w2, m_rw_a0, m_rw_a2, m_rw_kk, m_rw_ka, m_rw_rk, m_rw_lnx_g, m_rw_lnx_b, m_att_bias, m_w_out_e, m_w_in_o, m_sg_ln_g, m_sg_ln_b, m_sg_w, m_sg_b, m_w_out_o, m_final_g, v_norm_g, v_w_in_e, v_shift_mu, v_rw_w0, v_rw_w2, v_rw_a0, v_rw_a2, v_rw_kk, v_rw_ka, v_rw_rk, v_rw_lnx_g, v_rw_lnx_b, v_att_bias, v_w_out_e, v_w_in_o, v_sg_ln_g, v_sg_ln_b, v_sg_w, v_sg_b, v_w_out_o, v_final_g):
    x2 = x.reshape(T, D)
    tgt = loss_target.reshape(T, D)

    my_chip = 2 * lax.axis_index("x") + lax.axis_index("y")
    gathered = gather_weights(
        [jnp.swapaxes(w_in_e[0], 0, 1).astype(BF16), jnp.concatenate([rw_w2[0], rw_a2[0]], axis=0),
         jnp.concatenate([sg_ln_g, sg_ln_b], axis=0)], [True, True, False])
    wie = gathered[0].reshape(EVEN_IN, D)
    w2 = _chips_to_cols(gathered[1][:, :LORA])
    a2 = _chips_to_cols(gathered[1][:, LORA:])
    sglg = _chips_to_cols(gathered[2][:, 0:1])
    sglb = _chips_to_cols(gathered[2][:, 1:2])

    late = [w_out_e[0].astype(BF16), w_in_o[0].astype(BF16), w_out_o[0].astype(BF16)]
    late_started = send_start(late, [jnp.broadcast_to(a[None], (NCHIP,) + a.shape) for a in late], False, False,
                              "late_weights_start", after=gathered[0])

    def late_weights(after):
        woe, wio, woo = send_wait(late_started, after, False, False, "late_weights_wait")
        return woe.reshape(D, D), _chips_to_cols(wio), woo.reshape(D, D)

    def scatter_start(grads, name):
        srcs = [g_.astype(BF16) if g_.shape[-1] >= W else g_ for g_ in grads]
        return send_start(srcs, [jnp.zeros_like(s) for s in srcs], True, False, name)

    def own_block(g_):
        return lax.dynamic_index_in_dim(g_, my_chip, axis=0, keepdims=False)

    started = {}

    def on_odd_grads(d_woo, d_wio):
        blocks = [d_woo.reshape(NCHIP, D // NCHIP, D), d_wio]
        started["odd"] = (scatter_start(blocks, "odd_grads_start"), [own_block(b) for b in blocks])
        return started["odd"][0][-1]

    def on_even_grads(big_g):
        d_wie_half, d_woe, _, _, d_w2, d_a2, d_sglg, d_sglb = big_g
        blocks = [d_wie_half, d_woe.reshape(NCHIP, D // NCHIP, D), _cols_to_chips(d_w2), _cols_to_chips(d_a2),
                  _cols_to_chips(d_sglg), _cols_to_chips(d_sglb)]
        started["even"] = (scatter_start(blocks, "even_grads_start"), [own_block(b) for b in blocks])
        return started["even"][0][-1]

    def on_small_grads(layer, grads):
        if layer == "odd":
            d_sg_w, d_sg_b, d_final, d_g1 = grads
            mine = [d_sg_w.reshape(NG * SGC, SGC), d_sg_b, jnp.concatenate([d_final, d_g1], axis=1)]
        else:
            mine = [grads[-2], jnp.concatenate(grads[:-2] + grads[-1:], axis=1)]
        started[layer + "_small"] = send_start(mine, [jnp.broadcast_to(a[None], (NDEV,) + a.shape) for a in mine],
                                               False, True, layer + "_small_grads_start")
        return started[layer + "_small"][-1]

    loss_part, dx, _, _ = _local_step(
        x2, tgt, wie, late_weights, w2, a2, sglg, sglb, norm_g, shift_mu, rw_w0, rw_a0, rw_kk, rw_ka, rw_rk,
        rw_lnx_g, rw_lnx_b, att_bias, sg_w, sg_b, final_g, first_after=late_started[-1], on_odd_grads=on_odd_grads,
        on_even_grads=on_even_grads, on_small_grads=on_small_grads)
    even_started, even_own = started["even"]

    wmv = {"w_in_e": tuple(jnp.swapaxes(a, 1, 2) for a in (w_in_e, m_w_in_e, v_w_in_e)),
           "w_out_e": (w_out_e, m_w_out_e, v_w_out_e),
           "w_in_o": (w_in_o, m_w_in_o, v_w_in_o), "w_out_o": (w_out_o, m_w_out_o, v_w_out_o),
           "rw_w2": (rw_w2, m_rw_w2, v_rw_w2), "rw_a2": (rw_a2, m_rw_a2, v_rw_a2),
           "sg_ln_g": (sg_ln_g, m_sg_ln_g, v_sg_ln_g), "sg_ln_b": (sg_ln_b, m_sg_ln_b, v_sg_ln_b)}
    sharded = {}

    def finish(names, own, landed, tag):
        partial = [sum_chips(o_, p_, "sum_" + nm) for o_, p_, nm in zip(own, landed, names)]
        from_sibling = exchange_c(partial, "swap_partials_" + tag)
        for nm, mine, sib in zip(names, partial, from_sibling):
            if nm == "w_in_e":
                res = adam_shard_halves_t(mine, sib, *wmv[nm], "adam_" + nm)
                sharded[nm] = [jnp.swapaxes(a, 1, 2) for a in res]
            else:
                sharded[nm] = adam_shard(mine, sib, *wmv[nm], "adam_" + nm)
        return partial[0]

    odd_started, odd_own = started["odd"]
    odd_landed = send_wait(odd_started, started["even_small"][-1], True, False, "odd_grads_wait")
    done = finish(["w_out_o", "w_in_o"], odd_own, odd_landed, "odd")

    def wmv_of(*arrs, view=lambda a: a):
        return tuple(view(a) for a in arrs)

    vec = lambda a: a.reshape(1, -1)
    groups = {
        "odd": (["sg_w", "sg_b", "final_g", "norm_g1"],
                [wmv_of(sg_w, m_sg_w, v_sg_w, view=lambda a: a.reshape(NG * SGC, SGC)),
                 wmv_of(sg_b, m_sg_b, v_sg_b, view=lambda a: a[0]),
                 [wmv_of(final_g, m_final_g, v_final_g, view=vec),
                  wmv_of(norm_g, m_norm_g, v_norm_g, view=lambda a: a[1:2])]]),
        "even": (["att_bias", "norm_g0", "shift_mu", "rw_w0", "rw_a0", "rw_kk", "rw_ka", "rw_rk", "rw_lnx_g",
                  "rw_lnx_b", "loss"],
                 [wmv_of(att_bias, m_att_bias, v_att_bias, view=lambda a: a[0]),
                  [wmv_of(norm_g, m_norm_g, v_norm_g, view=lambda a: a[0:1]),
                   wmv_of(shift_mu, m_shift_mu, v_shift_mu), wmv_of(rw_w0, m_rw_w0, v_rw_w0),
                   wmv_of(rw_a0, m_rw_a0, v_rw_a0), wmv_of(rw_kk, m_rw_kk, v_rw_kk), wmv_of(rw_ka, m_rw_ka, v_rw_ka),
                   wmv_of(rw_rk, m_rw_rk, v_rw_rk, view=vec), wmv_of(rw_lnx_g, m_rw_lnx_g, v_rw_lnx_g),
                   wmv_of(rw_lnx_b, m_rw_lnx_b, v_rw_lnx_b), (None, None, None)]]),
    }
    rep = {}
    for layer in ("odd", "even"):
        nms, params = groups[layer]
        gathered_g = send_wait(started[layer + "_small"], done, False, True, layer + "_small_grads_wait")
        for nm, res in zip(nms, adam_replicated(gathered_g, params, "adam_" + layer + "_small")):
            rep[nm] = res
        done = rep[nms[0]][0]
    native = {"sg_w": sg_w.shape, "sg_b": sg_b.shape, "final_g": final_g.shape, "rw_rk": rw_rk.shape,
              "att_bias": att_bias.shape}
    for nm, shape in native.items():
        rep[nm] = [a.reshape(shape) for a in rep[nm]]
    rep["norm_g"] = [jnp.concatenate([a, b], axis=0) for a, b in zip(rep["norm_g0"], rep["norm_g1"])]
    even_landed = send_wait(even_started, done, True, False, "even_grads_wait")
    finish(["w_in_e", "w_out_e", "rw_w2", "rw_a2", "sg_ln_g", "sg_ln_b"], even_own, even_landed, "even")

    order = ["norm_g", "w_in_e", "shift_mu", "rw_w0", "rw_w2", "rw_a0", "rw_a2", "rw_kk", "rw_ka", "rw_rk",
             "rw_lnx_g", "rw_lnx_b", "att_bias", "w_out_e", "w_in_o", "sg_ln_g", "sg_ln_b", "sg_w", "sg_b",
             "w_out_o", "final_g"]
    results = {**sharded, **rep}
    outs = [rep["loss"][0][0, 0], dx.reshape(NSEQ, SEQ, D)]
    for kind in range(4):
        outs += [results[nm][kind] for nm in order]
    return tuple(outs)


def _local_step(x2, tgt, wie_t, late_weights, w2, a2, sglg, sglb, norm_g, shift_mu, rw_w0, rw_a0, rw_kk, rw_ka, rw_rk,
                rw_lnx_g, rw_lnx_b, att_bias, sg_w, sg_b, final_g, first_after=None, on_odd_grads=None,
                on_even_grads=None, on_small_grads=None):
    zl = jnp.zeros((LORA, W), F32)
    w2x = jnp.concatenate([w2, zl], axis=0)
    a2x = jnp.concatenate([zl, a2], axis=0)
    rk = rw_rk.reshape(1, W)
    pos = np.arange(SGC)
    sg_mask = jnp.asarray(((pos[None, :] // L) <= (pos[:, None] // L)).astype(np.float32))
    wm = (sg_w[0] * sg_mask[None]).astype(BF16)
    sgb_t = sg_b[0].T

    xn0, ps, ga, q, kb, vb, gb = ln_in_proj(x2, norm_g[0:1], wie_t, EVEN_SPLITS, "in_proj_even", after=first_after,
                                            w_t=True)
    r, lw, k2, v, aa, bb = even_prep(ps, shift_mu, rw_w0, w2x, rw_a0, a2x, rw_kk, rw_ka)
    y, rw_saved = rwkv_fwd(r, lw, k2, v, aa, bb)
    bias = window_bias(bias_expand(att_bias[0]).reshape(NPAIR, 2 * L, BAND))

    def padded(a):
        return jnp.pad(a.astype(BF16).reshape(NSEQ, SEQ, W), ((0, 0), (LEFT * L, 0), (0, 0))).reshape(NSEQ * PADSEQ, W)

    kpad, vpad = padded(kb), padded(vb)
    o = attention_fwd(q, kpad, vpad, bias)
    woe, wio, woo = late_weights(o)
    h1, zt = even_post(y, r, k2, v, ga, o, gb, rw_lnx_g, rw_lnx_b, rk, x2, woe)
    xn1, u, vv, gt = ln_in_proj(h1, norm_g[1:2], wio, ODD_SPLITS, "in_proj_odd")
    dh2, loss_part, d_final_g, z2t = gmlp_fwd_loss(u, vv, gt, sglg, sglb, wm, sgb_t, h1, woo, final_g[None], tgt)

    du, dvv, dgt, d_sglg, d_sglb, d_wm, d_sgb_t, d_woo = gmlp_bwd(u, vv, gt, sglg, sglb, wm, sgb_t, dh2, z2t, woo)
    dp_odd = [du, dvv, dgt]
    d_wio = matmul_acc_chips(xn1, dp_odd, "in_proj_odd_dw")
    token = on_odd_grads(d_woo, d_wio) if on_odd_grads else None
    dh1, d_g1 = in_proj_bwd_x(h1, norm_g[1:2], wio, dp_odd, dh2, "in_proj_odd_bwd", after=token)
    odd_small = [d_wm * sg_mask[None], d_sgb_t.T, d_final_g, d_g1]
    token = on_small_grads("odd", odd_small) if on_small_grads else None
    dy, dr2, dk22, dv2, dga, do, dgb, d_lng, d_lnb, d_rk, d_woe = even_post_bwd(
        y, r, k2, v, ga, o, gb, rw_lnx_g, rw_lnx_b, rk, dh1, zt, woe, after=token)
    dq, dkb, dvb, dbias = attention_bwd(q, kpad, vpad, bias, do)
    dbias = sum(dbias[:, i * 2 * L:(i + 1) * 2 * L, i * L:i * L + BAND] for i in range(ATT_Q))
    d_att_bias = bias_grad(dbias.reshape(NH, L, BAND))
    dr, dlw, dk2, dv, daa, dbb = rwkv_bwd(r, lw, k2, aa, bb, rw_saved, dy)
    dps, d_mu, d_w0, d_w2x, d_a0, d_a2x, d_kk, d_ka = even_prep_bwd(
        ps, shift_mu, rw_w0, w2x, rw_a0, a2x, rw_kk, rw_ka, dr, dlw, dk2, dv, daa, dbb, dr2, dk22, dv2)
    dp_even = [dps, dga, dq, dkb, dvb, dgb]
    d_wie = matmul_acc_chips(xn0, dp_even, "in_proj_even_dw", add_cores=on_even_grads is not None)
    big_g = (d_wie, d_woe, d_wio, d_woo, d_w2x[:LORA], d_a2x[LORA:], d_sglg, d_sglb)
    token = on_even_grads(big_g) if on_even_grads else None
    dx, d_g0 = in_proj_bwd_x(x2, norm_g[0:1], wie_t, dp_even, dh1, "in_proj_even_bwd", after=token, w_t=True)
    even_small = [d_g0, d_mu, d_w0, d_a0, d_kk, d_ka, d_rk, d_lng, d_lnb, d_att_bias]
    if on_small_grads:
        on_small_grads("even", even_small + [loss_part[0:1, :]])
    rep_g = [jnp.concatenate([d_g0, d_g1], axis=0)] + even_small[1:] + odd_small[:3]
    return loss_part[0, 0], dx, big_g, rep_g
```

```python
import functools
import math

import jax
import jax.numpy as jnp
import numpy as np
from jax import lax
from jax.experimental import pallas as pl
from jax.experimental.pallas import tpu as pltpu

F32 = jnp.float32
BF16 = jnp.bfloat16
HI = lax.Precision.HIGHEST

D = 1024
SEQ = 2048
NSEQ = 2
T = NSEQ * SEQ
HD = 64
NH = 8
W = 512
SHIFT = 1664
LORA = 64
EVEN_IN = 4224
ODD_IN = 3072
L = 64
NC = SEQ // L
LEFT = 8
BAND = (LEFT + 1) * L
CLIP = 128
SGC = 128
NG = 8
RMS_EPS = 1e-6
LN_EPS = 1e-5
GN_EPS = 64e-5
NEG = -1e30
VMEM_BIG = 56 * 1024 * 1024

ADAM_LR = 0.001
ADAM_B1 = 0.9
ADAM_B2 = 0.999
ADAM_EPS = 1e-08
ADAM_WD = 0.01
ADAM_STEP = 10

MESH = pl.DeviceIdType.MESH


def _bdot(a, b):
    return jnp.dot(a.astype(BF16), b.astype(BF16), preferred_element_type=F32)


def _bdot_nt(a, b):
    return lax.dot_general(a.astype(BF16), b.astype(BF16), (((1,), (1,)), ((), ())), preferred_element_type=F32)


def _bdot_tn(a, b):
    return lax.dot_general(a.astype(BF16), b.astype(BF16), (((0,), (0,)), ((), ())), preferred_element_type=F32)


def _hdot(a, b):
    return jnp.dot(a, b, precision=HI, preferred_element_type=F32)


def _hdot_nt(a, b):
    return lax.dot_general(a, b, (((1,), (1,)), ((), ())), precision=HI, preferred_element_type=F32)


def _hdot_tn(a, b):
    return lax.dot_general(a, b, (((0,), (0,)), ((), ())), precision=HI, preferred_element_type=F32)


def _iota2(shape, dim):
    return lax.broadcasted_iota(jnp.int32, shape, dim)


def _head_blockdiag():
    r = _iota2((2 * HD, 2 * HD), 0) // HD
    c = _iota2((2 * HD, 2 * HD), 1) // HD
    return (r == c).astype(BF16)


def _headsum_impl(x, bd):
    hi = x.astype(BF16)
    mid = (x - hi.astype(F32)).astype(BF16)
    n = bd.shape[0]
    out = [jnp.dot(hi[:, i:i + n], bd, preferred_element_type=F32) + jnp.dot(mid[:, i:i + n], bd, preferred_element_type=F32)
           for i in range(0, x.shape[1], n)]
    return jnp.concatenate(out, axis=-1)


@jax.custom_vjp
def _headsum(x, bd):
    return _headsum_impl(x, bd)


def _headsum_fwd(x, bd):
    return _headsum_impl(x, bd), bd


def _headsum_bwd(bd, ct):
    return _headsum_impl(ct, bd), None


_headsum.defvjp(_headsum_fwd, _headsum_bwd)


def _silu(x):
    return x * jax.nn.sigmoid(x)


def _dsilu(x):
    s = jax.nn.sigmoid(x)
    return s * (1.0 + x * (1.0 - s))


_GELU_C = math.sqrt(2.0 / math.pi)


def _gelu(x):
    return 0.5 * x * (1.0 + jnp.tanh(_GELU_C * (x + 0.044715 * (x * x * x))))


def _dgelu(x):
    t = jnp.tanh(_GELU_C * (x + 0.044715 * (x * x * x)))
    return 0.5 * (1.0 + t) + 0.5 * x * (1.0 - t * t) * _GELU_C * (1.0 + 3.0 * 0.044715 * x * x)


def _silu_both(x):
    s = jax.nn.sigmoid(x)
    xs = x * s
    return xs, s + xs * (1.0 - s)


def _gelu_both(x):
    x2 = x * x
    t = jnp.tanh(_GELU_C * (x + 0.044715 * (x2 * x)))
    half = 0.5 * (1.0 + t)
    return x * half, half + 0.5 * x * (1.0 - t * t) * _GELU_C * (1.0 + 3.0 * 0.044715 * x2)


def _softplus(x):
    return jnp.maximum(x, 0.0) + jnp.log(1.0 + jnp.exp(-jnp.abs(x)))


def _cparams(sem, vmem=None):
    return pltpu.CompilerParams(dimension_semantics=sem, vmem_limit_bytes=vmem)


def _row_spec(tm, width):
    return pl.BlockSpec((tm, width), lambda i: (i, 0))


def _col_spec(height, tm):
    return pl.BlockSpec((height, tm), lambda i: (0, i))


def _const_spec(shape):
    nd = len(shape)
    return pl.BlockSpec(shape, lambda *_: (0,) * nd)


def _weight_dims(w_bf, w_t):
    return (((1,), (1,)), ((), ())) if w_t else (((1,), (0,)), ((), ())), w_bf.shape[0 if w_t else 1]


def ln_in_proj(x, g, w_bf, splits, name, after=None, w_t=False):
    dims, n = _weight_dims(w_bf, w_t)
    tm = 512 if n <= ODD_IN else 256
    spans = []
    o = 0
    for s in splits:
        spans.append((o, o + s))
        o += s
    assert o == n
    extra_specs, extra = _after_operand(after)

    def body(x_ref, g_ref, w_ref, *rest):
        xn_ref, outs = rest[len(extra)], rest[len(extra) + 1:]
        xv = x_ref[...]
        rstd = lax.rsqrt(jnp.mean(xv * xv, axis=-1, keepdims=True) + RMS_EPS)
        xn = (xv * rstd * g_ref[...]).astype(BF16)
        xn_ref[...] = xn.T
        p = lax.dot_general(xn, w_ref[...], dims, preferred_element_type=F32)
        for o_ref, (a, b) in zip(outs, spans):
            o_ref[...] = p[:, a:b]

    return pl.pallas_call(
        body, grid=(T // tm,), name=name,
        in_specs=[_row_spec(tm, D), _const_spec((1, D)), _const_spec(w_bf.shape)] + extra_specs,
        out_specs=[_col_spec(D, tm)] + [_row_spec(tm, s) for s in splits],
        out_shape=[jax.ShapeDtypeStruct((D, T), BF16)] + [jax.ShapeDtypeStruct((T, s), F32) for s in splits],
        compiler_params=_cparams(("parallel",), VMEM_BIG),
    )(x, g, w_bf, *extra)


def in_proj_bwd_x(x, g, w_bf, dps, dres, name, after=None, w_t=False):
    tm = 512
    back = (((1,), (0,)), ((), ())) if w_t else (((1,), (1,)), ((), ()))
    widths = [d.shape[1] for d in dps]
    extra_specs, extra = _after_operand(after)

    def body(x_ref, g_ref, w_ref, dres_ref, *rest):
        dp_refs = rest[:len(widths)]
        dx_ref, dg_ref = rest[-2:]
        dp = jnp.concatenate([r[...] for r in dp_refs], axis=-1)
        dxn = lax.dot_general(dp, w_ref[...], back, preferred_element_type=F32)
        xv = x_ref[...]
        rstd = lax.rsqrt(jnp.mean(xv * xv, axis=-1, keepdims=True) + RMS_EPS)
        xhat = xv * rstd
        dgp = jnp.sum(dxn * xhat, axis=0, keepdims=True)

        @pl.when(pl.program_id(0) == 0)
        def _():
            dg_ref[...] = jnp.zeros_like(dg_ref)

        dg_ref[...] += dgp
        dxh = dxn * g_ref[...]
        dx_ref[...] = dres_ref[...] + rstd * (dxh - xhat * jnp.mean(dxh * xhat, axis=-1, keepdims=True))

    return pl.pallas_call(
        body, grid=(T // tm,), name=name,
        in_specs=[_row_spec(tm, D), _const_spec((1, D)), _const_spec(w_bf.shape), _row_spec(tm, D)]
        + [_row_spec(tm, s) for s in widths] + extra_specs,
        out_specs=[_row_spec(tm, D), _const_spec((1, D))],
        out_shape=[jax.ShapeDtypeStruct((T, D), F32), jax.ShapeDtypeStruct((1, D), F32)],
        compiler_params=_cparams(("arbitrary",), VMEM_BIG),
    )(x, g, w_bf, dres, *dps, *extra)


def _after_operand(after):
    return ([ANY], [after]) if after is not None else ([], [])


def matmul_acc_chips(at_bf, pieces, name, after=None, add_cores=False):
    k = at_bf.shape[0]
    widths = [p.shape[1] for p in pieces]
    nb = sum(widths) // NCHIP
    tm = 512
    steps = T // tm
    half = k // 2
    extra_specs, extra = _after_operand(after)

    def body(a_ref, *rest):
        o_ref, acc = rest[len(widths) + len(extra):][:2]

        @pl.when(pl.program_id(0) == 0)
        def _():
            acc[...] = jnp.zeros_like(acc)

        a = a_ref[...]
        b = jnp.concatenate([r[...] for r in rest[:len(widths)]], axis=-1)
        for s in range(NCHIP):
            acc[s] += jnp.dot(a, b[:, s * nb:(s + 1) * nb], preferred_element_type=F32)

        @pl.when(pl.program_id(0) == steps - 1)
        def _():
            if not add_cores:
                o_ref[...] = acc[...].astype(BF16)
            else:
                give, got, send, recv = rest[-4:]
                x, y, c = lax.axis_index("x"), lax.axis_index("y"), lax.axis_index("c")
                theirs = pl.multiple_of((1 - c) * half, half)
                mine = pl.multiple_of(c * half, half)
                give[...] = acc[:, pl.ds(theirs, half), :].astype(BF16)
                cp = pltpu.make_async_remote_copy(src_ref=give, dst_ref=got, send_sem=send, recv_sem=recv,
                                                  device_id=(x, y, 1 - c), device_id_type=MESH)
                cp.start()
                cp.wait()
                o_ref[...] = (acc[:, pl.ds(mine, half), :] + got[...].astype(F32)).astype(BF16)

    out_rows = half if add_cores else k
    exchange = [pltpu.VMEM((NCHIP, half, nb), BF16)] * 2 + [pltpu.SemaphoreType.DMA] * 2 if add_cores else []
    return pl.pallas_call(
        body, grid=(steps,), name=name,
        in_specs=[_col_spec(k, tm)] + [_row_spec(tm, w_) for w_ in widths] + extra_specs,
        out_specs=_const_spec((NCHIP, out_rows, nb)),
        out_shape=jax.ShapeDtypeStruct((NCHIP, out_rows, nb), BF16),
        scratch_shapes=[pltpu.VMEM((NCHIP, k, nb), F32)] + exchange,
        compiler_params=_cparams(("arbitrary",), VMEM_BIG),
    )(at_bf, *pieces, *extra)


def _out_proj_back(dh_ref, zt_ref, w_ref, dw_ref):
    dhb = dh_ref[...].astype(BF16)

    @pl.when(pl.program_id(0) == 0)
    def _():
        dw_ref[...] = jnp.zeros_like(dw_ref)

    dw_ref[...] += jnp.dot(zt_ref[...], dhb, preferred_element_type=F32)
    return lax.dot_general(dhb, w_ref[...], (((1,), (1,)), ((), ())), preferred_element_type=F32)


PREP_TM = 512
PREP_NB = SEQ // PREP_TM


def _prep_elem(k, wl, apre, kkw, kaw, bd):
    wraw = -_softplus(-wl) - 0.5
    lw = -jnp.exp(wraw)
    asig = jax.nn.sigmoid(apre)
    kkr = k * kkw
    nrm = jnp.maximum(jnp.sqrt(_headsum(kkr * kkr, bd)), 1e-12)
    kk = kkr / nrm
    k2 = k * (1.0 + (asig - 1.0) * kaw)
    return lw, k2, -kk, kk * asig


def _prep_elem_bwd(k, wl, apre, kkw, kaw, bd, dlw, dk2, daa, dbb):
    s = -wl
    sp = _softplus(s)
    dwl = dlw * (-jnp.exp(-sp - 0.5)) * jnp.exp(s - sp)
    asig = jax.nn.sigmoid(apre)
    kkr = k * kkw
    root = jnp.sqrt(_headsum(kkr * kkr, bd))
    inv = 1.0 / jnp.maximum(root, 1e-12)
    kk = kkr * inv
    dkk = dbb * asig - daa
    dap = (dbb * kk + dk2 * k * kaw) * asig * (1.0 - asig)
    through_norm = jnp.where(root > 1e-12, kk * _headsum(dkk * kkr, bd) * inv, 0.0)
    dkkr = inv * (dkk - through_norm)
    gain = 1.0 + (asig - 1.0) * kaw
    dk = dkkr * kkw + dk2 * gain
    dkkw = jnp.sum(dkkr * k, axis=0, keepdims=True)
    dkaw = jnp.sum(dk2 * k * (asig - 1.0), axis=0, keepdims=True)
    return dk, dwl, dap, dkkw, dkaw


def _shifted(ps_ref, prev_ref, mu, blk):
    p = ps_ref[...]
    first = (blk % PREP_NB) == 0
    prev_row = jnp.where(first, 0.0, prev_ref[7:8, :])
    rolled = pltpu.roll(p, 1, 0)
    p_prev = jnp.where(_iota2(p.shape, 0) == 0, prev_row, rolled)
    return p, p_prev, p + (p_prev - p) * mu


def _prev_spec(width, blk_of):
    return pl.BlockSpec((8, width), lambda i: (jnp.maximum(blk_of(i) * (PREP_TM // 8) - 1, 0), 0))


def even_prep(ps, mu, w0, w2x, a0, a2x, kkw, kaw):
    tm = PREP_TM

    def body(ps_ref, prev_ref, mu_ref, w0_ref, w2_ref, a0_ref, a2_ref, kk_ref, ka_ref,
             r_ref, lw_ref, k2_ref, v_ref, aa_ref, bb_ref):
        _, _, s = _shifted(ps_ref, prev_ref, mu_ref[...], pl.program_id(0))
        wa = s[:, 3 * W:]
        wl = w0_ref[...] + _bdot(jnp.tanh(wa), w2_ref[...])
        apre = a0_ref[...] + _bdot(wa, a2_ref[...])
        lw, k2, aa, bb = _prep_elem(s[:, W:2 * W], wl, apre, kk_ref[...], ka_ref[...], _head_blockdiag())
        r_ref[...] = s[:, 0:W]
        v_ref[...] = s[:, 2 * W:3 * W]
        lw_ref[...] = lw
        k2_ref[...] = k2
        aa_ref[...] = aa
        bb_ref[...] = bb

    vec = _const_spec((1, W))
    return pl.pallas_call(
        body, grid=(T // tm,), name="even_prep",
        in_specs=[_row_spec(tm, SHIFT), _prev_spec(SHIFT, lambda i: i), _const_spec((1, SHIFT)), vec,
                  _const_spec((2 * LORA, W)), vec, _const_spec((2 * LORA, W)), vec, vec],
        out_specs=[_row_spec(tm, W)] * 6,
        out_shape=[jax.ShapeDtypeStruct((T, W), F32)] * 6,
        compiler_params=_cparams(("parallel",), VMEM_BIG),
    )(ps, ps, mu, w0, w2x, a0, a2x, kkw, kaw)


def even_prep_bwd(ps, mu, w0, w2x, a0, a2x, kkw, kaw, dr, dlw, dk2, dv, daa, dbb, dr2, dk22, dv2):
    tm = PREP_TM
    nb = T // tm
    rev = lambda i: nb - 1 - i

    def body(ps_ref, prev_ref, mu_ref, w0_ref, w2_ref, a0_ref, a2_ref, kk_ref, ka_ref,
             dr_ref, dlw_ref, dk2_ref, dv_ref, daa_ref, dbb_ref, dr2_ref, dk22_ref, dv2_ref,
             dps_ref, dmu_ref, dw0_ref, dw2_ref, da0_ref, da2_ref, dkk_ref, dka_ref, carry):
        i = pl.program_id(0)
        blk = rev(i)
        mu_v = mu_ref[...]
        p, p_prev, s = _shifted(ps_ref, prev_ref, mu_v, blk)
        wa = s[:, 3 * W:]
        th = jnp.tanh(wa)
        wl = w0_ref[...] + _bdot(th, w2_ref[...])
        apre = a0_ref[...] + _bdot(wa, a2_ref[...])
        bd = _head_blockdiag()
        k = s[:, W:2 * W]
        dk, dwl, dap, dkkw, dkaw = _prep_elem_bwd(k, wl, apre, kk_ref[...], ka_ref[...], bd, dlw_ref[...],
                                                  dk2_ref[...] + dk22_ref[...], daa_ref[...], dbb_ref[...])
        dwa = _bdot_nt(dwl, w2_ref[...]) * (1.0 - th * th) + _bdot_nt(dap, a2_ref[...])
        ds = jnp.concatenate([dr_ref[...] + dr2_ref[...], dk, dv_ref[...] + dv2_ref[...], dwa], axis=-1)

        @pl.when(i == 0)
        def _():
            for ref in (dmu_ref, dw0_ref, dw2_ref, da0_ref, da2_ref, dkk_ref, dka_ref, carry):
                ref[...] = jnp.zeros_like(ref)

        dmu_ref[...] += jnp.sum(ds * (p_prev - p), axis=0, keepdims=True)
        dw0_ref[...] += jnp.sum(dwl, axis=0, keepdims=True)
        da0_ref[...] += jnp.sum(dap, axis=0, keepdims=True)
        dw2_ref[...] += _bdot_tn(th, dwl)
        da2_ref[...] += _bdot_tn(wa, dap)
        dkk_ref[...] += dkkw
        dka_ref[...] += dkaw
        dsm = ds * mu_v
        last = (blk % PREP_NB) == PREP_NB - 1
        nxt = jnp.where(last, 0.0, carry[0:1, :])
        up = pltpu.roll(dsm, tm - 1, 0)
        up = jnp.where(_iota2(up.shape, 0) == tm - 1, nxt, up)
        dps_ref[...] = (ds - dsm + up).astype(BF16)
        carry[0:1, :] = dsm[0:1, :]

    vec = _const_spec((1, W))
    rrow = lambda width: pl.BlockSpec((tm, width), lambda i: (rev(i), 0))
    return pl.pallas_call(
        body, grid=(nb,), name="even_prep_bwd",
        in_specs=[rrow(SHIFT), _prev_spec(SHIFT, rev), _const_spec((1, SHIFT)), vec,
                  _const_spec((2 * LORA, W)), vec, _const_spec((2 * LORA, W)), vec, vec] + [rrow(W)] * 9,
        out_specs=[rrow(SHIFT), _const_spec((1, SHIFT)), vec, _const_spec((2 * LORA, W)), vec,
                   _const_spec((2 * LORA, W)), vec, vec],
        out_shape=[jax.ShapeDtypeStruct((T, SHIFT), BF16), jax.ShapeDtypeStruct((1, SHIFT), F32),
                   jax.ShapeDtypeStruct((1, W), F32), jax.ShapeDtypeStruct((2 * LORA, W), F32),
                   jax.ShapeDtypeStruct((1, W), F32), jax.ShapeDtypeStruct((2 * LORA, W), F32),
                   jax.ShapeDtypeStruct((1, W), F32), jax.ShapeDtypeStruct((1, W), F32)],
        scratch_shapes=[pltpu.VMEM((8, SHIFT), F32)],
        compiler_params=_cparams(("arbitrary",), VMEM_BIG),
    )(ps, ps, mu, w0, w2x, a0, a2x, kkw, kaw, dr, dlw, dk2, dv, daa, dbb, dr2, dk22, dv2)


NPAIR = NH // 2
PW = 2 * HD


def _pair_cols(p):
    return slice(p * PW, (p + 1) * PW)


def _pairs(a):
    return [a[:, _pair_cols(p)] for p in range(NPAIR)]


def _stack_pair(a):
    first = _iota2(a.shape, 1) < HD
    zero = jnp.zeros_like(a)
    return jnp.concatenate([jnp.where(first, a, zero), jnp.where(first, zero, a)], axis=0)


def _unstack_pair(a):
    n = a.shape[0] // 2
    return jnp.where(_iota2((n, PW), 1) < HD, a[:n], a[n:])


def _fold_pair(a):
    n = a.shape[0] // 2
    return a[:n] + a[n:]


def _chunk_masks():
    n = 4 * L
    row = _iota2((n, n), 0)
    col = _iota2((n, n), 1)
    same = ((row // L) & 1) == ((col // L) & 1)
    ri = row & (L - 1)
    ci = col & (L - 1)
    keep = same & (((row < 2 * L) & (ri > ci)) | ((row >= 2 * L) & (ri >= ci)))
    r1 = _iota2((L, L), 0)
    c1 = _iota2((L, L), 1)
    r2 = _iota2((2 * L, 2 * L), 0)
    c2 = _iota2((2 * L, 2 * L), 1)
    return keep.astype(F32), (r1 >= c1).astype(F32), (r2 == c2).astype(F32)


def _scaled(r, lw, k2, aa, bb, tri):
    g = _hdot(tri, lw)
    eg = jnp.exp(g)
    eng = jnp.exp(-g)
    egp = jnp.exp(g - lw)
    return eg, eng, egp, aa * egp, r * eg, bb * eng, k2 * eng


def _head_cols(h):
    return slice(h * HD, (h + 1) * HD)


def _per_head(a):
    return [a[:, _head_cols(h)] for h in range(NH)]


def _pairs_operands(at, rt, bt, kt):
    x = [jnp.concatenate([_stack_pair(a), _stack_pair(r)], axis=0).astype(BF16) for a, r in zip(_pairs(at), _pairs(rt))]
    yk = [jnp.concatenate([_stack_pair(b), _stack_pair(k)], axis=0).astype(BF16) for b, k in zip(_pairs(bt), _pairs(kt))]
    return x, yk


def _pairs_matrices(x, yk, keep, eye):
    m = [_bdot_nt(a, b) * keep for a, b in zip(x, yk)]
    p = [a[:2 * L, :2 * L] for a in m]
    tinv = [eye + a for a in p]
    for _ in range(5):
        p = [_bdot(a, a) for a in p]
        tinv = [t + _bdot(t, a) for t, a in zip(tinv, p)]
    return [a.astype(BF16) for a in m], [a.astype(BF16) for a in tinv]


def _pairs_fwd(x, yk, m, tinv, vw, s0, egl):
    xh = [_bdot_nt(a, s) for a, s in zip(x, s0)]
    u = [_bdot(t, h[:2 * L] + _bdot(a[:2 * L, 2 * L:], w)) for t, h, a, w in zip(tinv, xh, m, vw)]
    uv = [jnp.concatenate([a, w], axis=0).astype(BF16) for a, w in zip(u, vw)]
    y = [h[2 * L:] + _bdot(a[2 * L:], w) for h, a, w in zip(xh, m, uv)]
    sn = [e * (s + _bdot_tn(w, b)) for e, s, w, b in zip(egl, s0, uv, yk)]
    return y, sn, uv


def _pairs_bwd(x, yk, m, tinv, uv, s0, sn, egl, dyw, dsn, keep):
    dzs = [d * e for d, e in zip(dsn, egl)]
    dgl = [jnp.sum(d * s, axis=0, keepdims=True) for d, s in zip(dsn, sn)]
    dyb = [a.astype(BF16) for a in dyw]
    t1 = [_bdot_tn(a[2 * L:], d) for a, d in zip(m, dyb)]
    t2 = [_bdot_nt(b, d) for b, d in zip(yk, dzs)]
    drhs = [_bdot_tn(t, a[:2 * L] + b[:2 * L]) for t, a, b in zip(tinv, t1, t2)]
    dv = [a[2 * L:] + b[2 * L:] + _bdot_tn(c[:2 * L, 2 * L:], d) for a, b, c, d in zip(t1, t2, m, drhs)]
    gg = [jnp.concatenate([a, b], axis=0).astype(BF16) for a, b in zip(drhs, dyw)]
    ds0 = [d + _bdot_tn(g, a) for d, g, a in zip(dzs, gg, x)]
    dm = [_bdot_nt(g, w) * keep for g, w in zip(gg, uv)]
    dx = [_bdot(g, s) + _bdot(d, b) for g, s, d, b in zip(gg, s0, dm, yk)]
    dyk = [_bdot_tn(d, a) + _bdot(w, z) for d, a, w, z in zip(dm, x, uv, dzs)]
    return dx, dyk, dv, dgl, ds0


STATE_SHAPE = (NPAIR * PW, PW)
M_SHAPE = (4 * L, NPAIR * 4 * L)
TINV_SHAPE = (2 * L, NPAIR * 2 * L)


def _rows_of(a, n):
    return [a[i * n:(i + 1) * n, :] for i in range(NPAIR)]


def _both(f):
    out = []
    for s in range(NSEQ):
        out += f(s)
    return out


def _seq_view(a):
    return a.reshape(NSEQ, SEQ, a.shape[-1])


UV_SHAPE = (4 * L, NPAIR * PW)
RW_CHUNKS = 2


def rwkv_fwd(r, lw, k2, v, aa, bb):
    def body(r_ref, lw_ref, k2_ref, v_ref, aa_ref, bb_ref, y_ref, hs_ref, hn_ref, m_ref, t_ref, uv_ref, state):
        @pl.when(pl.program_id(0) == 0)
        def _():
            state[...] = jnp.zeros_like(state)

        keep, tri, eye = _chunk_masks()
        where = [(j, s) for j in range(RW_CHUNKS) for s in range(NSEQ)]
        rows = lambda j: slice(j * L, (j + 1) * L)
        sc = [_scaled(r_ref[s, rows(j)], lw_ref[s, rows(j)], k2_ref[s, rows(j)], aa_ref[s, rows(j)],
                      bb_ref[s, rows(j)], tri) for j, s in where]
        ops = [_pairs_operands(*a[3:]) for a in sc]
        m, tinv = _pairs_matrices([a for o in ops for a in o[0]], [a for o in ops for a in o[1]], keep, eye)
        s_cur = [state[s] for s in range(NSEQ)]
        for j in range(RW_CHUNKS):
            mine = slice(j * NSEQ * NPAIR, (j + 1) * NSEQ * NPAIR)
            x = [a for o in ops[j * NSEQ:(j + 1) * NSEQ] for a in o[0]]
            yk = [a for o in ops[j * NSEQ:(j + 1) * NSEQ] for a in o[1]]
            vw = _both(lambda s: [_stack_pair(a) for a in _pairs(v_ref[s, rows(j)])])
            egl = _both(lambda s: _pairs(sc[j * NSEQ + s][0][L - 1:L, :]))
            y, sn, uv = _pairs_fwd(x, yk, m[mine], tinv[mine], vw, _both(lambda s: _rows_of(s_cur[s], PW)), egl)
            for s in range(NSEQ):
                ps = slice(s * NPAIR, (s + 1) * NPAIR)
                hs_ref[j, s] = s_cur[s]
                y_ref[s, rows(j)] = jnp.concatenate([_fold_pair(a) for a in y[ps]], axis=-1)
                m_ref[j, s] = jnp.concatenate(m[mine][ps], axis=-1)
                t_ref[j, s] = jnp.concatenate(tinv[mine][ps], axis=-1)
                uv_ref[j, s] = jnp.concatenate(uv[ps], axis=-1)
                s_cur[s] = jnp.concatenate(sn[ps], axis=0)
                hn_ref[j, s] = s_cur[s]
        for s in range(NSEQ):
            state[s] = s_cur[s]

    blk = pl.BlockSpec((NSEQ, RW_CHUNKS * L, W), lambda c: (0, c, 0))
    per_chunk = lambda shape: pl.BlockSpec((RW_CHUNKS, NSEQ) + shape, lambda c: (c, 0, 0, 0))
    saved_shapes = [(STATE_SHAPE, F32), (STATE_SHAPE, F32), (M_SHAPE, BF16), (TINV_SHAPE, BF16), (UV_SHAPE, BF16)]
    y, *saved = pl.pallas_call(
        body, grid=(NC // RW_CHUNKS,), name="rwkv_fwd",
        in_specs=[blk] * 6,
        out_specs=[blk] + [per_chunk(shape) for shape, _ in saved_shapes],
        out_shape=[jax.ShapeDtypeStruct((NSEQ, SEQ, W), F32)]
        + [jax.ShapeDtypeStruct((NC, NSEQ) + shape, dt) for shape, dt in saved_shapes],
        scratch_shapes=[pltpu.VMEM((NSEQ,) + STATE_SHAPE, F32)],
        compiler_params=_cparams(("arbitrary",), VMEM_BIG),
    )(*[_seq_view(a) for a in (r, lw, k2, v, aa, bb)])
    return y.reshape(T, W), saved


def rwkv_bwd(r, lw, k2, aa, bb, saved, dy):
    def body(r_ref, lw_ref, k2_ref, aa_ref, bb_ref, hs_ref, hn_ref, m_ref, t_ref, uv_ref, dy_ref,
             dr_ref, dlw_ref, dk2_ref, dv_ref, daa_ref, dbb_ref, dstate):
        @pl.when(pl.program_id(0) == 0)
        def _():
            dstate[...] = jnp.zeros_like(dstate)

        keep, tri, _ = _chunk_masks()
        sc = [_scaled(r_ref[s], lw_ref[s], k2_ref[s], aa_ref[s], bb_ref[s], tri) for s in range(NSEQ)]
        ops = [_pairs_operands(*sc[s][3:]) for s in range(NSEQ)]
        x, yk = _both(lambda s: ops[s][0]), _both(lambda s: ops[s][1])
        m = _both(lambda s: [m_ref[0, s][:, i * 4 * L:(i + 1) * 4 * L] for i in range(NPAIR)])
        tinv = _both(lambda s: [t_ref[0, s][:, i * 2 * L:(i + 1) * 2 * L] for i in range(NPAIR)])
        uv = _both(lambda s: _pairs(uv_ref[0, s]))
        dyw = _both(lambda s: [_stack_pair(a) for a in _pairs(dy_ref[s])])
        s0 = _both(lambda s: _rows_of(hs_ref[0, s], PW))
        sn = _both(lambda s: _rows_of(hn_ref[0, s], PW))
        dsn = _both(lambda s: _rows_of(dstate[s], PW))
        egl = _both(lambda s: _pairs(sc[s][0][L - 1:L, :]))
        dx, dyk, dvw, dgl, ds0 = _pairs_bwd(x, yk, m, tinv, uv, s0, sn, egl, dyw, dsn, keep)
        for s in range(NSEQ):
            mine = slice(s * NPAIR, (s + 1) * NPAIR)
            eg, eng, egp, at, rt, bt, kt = sc[s]
            dstate[s] = jnp.concatenate(ds0[mine], axis=0)
            dv_ref[s] = jnp.concatenate([_fold_pair(a) for a in dvw[mine]], axis=-1)
            dat = jnp.concatenate([_fold_pair(a[:2 * L]) for a in dx[mine]], axis=-1)
            drt = jnp.concatenate([_fold_pair(a[2 * L:]) for a in dx[mine]], axis=-1)
            dbt = jnp.concatenate([_fold_pair(a[:2 * L]) for a in dyk[mine]], axis=-1)
            dkt = jnp.concatenate([_fold_pair(a[2 * L:]) for a in dyk[mine]], axis=-1)
            dg = drt * rt - dbt * bt - dkt * kt
            dg = dg + jnp.where(_iota2(dg.shape, 0) == L - 1, jnp.concatenate(dgl[mine], axis=-1), 0.0)
            dgp = dat * at
            dlw_ref[s] = _hdot_tn(tri, dg + dgp) - dgp
            dr_ref[s] = drt * eg
            daa_ref[s] = dat * egp
            dbb_ref[s] = dbt * eng
            dk2_ref[s] = dkt * eng

    blk = pl.BlockSpec((NSEQ, L, W), lambda c: (0, NC - 1 - c, 0))
    per_chunk = lambda shape: pl.BlockSpec((1, NSEQ) + shape, lambda c: (NC - 1 - c, 0, 0, 0))
    outs = pl.pallas_call(
        body, grid=(NC,), name="rwkv_bwd",
        in_specs=[blk] * 5 + [per_chunk(a.shape[2:]) for a in saved] + [blk],
        out_specs=[blk] * 6,
        out_shape=[jax.ShapeDtypeStruct((NSEQ, SEQ, W), F32)] * 6,
        scratch_shapes=[pltpu.VMEM((NSEQ,) + STATE_SHAPE, F32)],
        compiler_params=_cparams(("arbitrary",)),
    )(*[_seq_view(a) for a in (r, lw, k2, aa, bb)], *saved, _seq_view(dy))
    return [a.reshape(T, W) for a in outs]


def _post_math(y, r, k2, v, ga, o, gb, lng, lnb, rk, bd):
    mu = _headsum(y, bd) * (1.0 / HD)
    yc = y - mu
    var = _headsum(yc * yc, bd) * (1.0 / HD)
    yn = yc * lax.rsqrt(var + GN_EPS) * lng + lnb
    bonus = _headsum(r * k2 * rk, bd) * v
    return (yn + bonus) * _silu(ga), o * _silu(gb)


def even_post(y, r, k2, v, ga, o, gb, lng, lnb, rk, h, w_bf):
    tm = 512

    def body(y_ref, r_ref, k2_ref, v_ref, ga_ref, o_ref, gb_ref, lng_ref, lnb_ref, rk_ref, h_ref, w_ref,
             ho_ref, zt_ref):
        ya, yb = _post_math(y_ref[...], r_ref[...], k2_ref[...], v_ref[...], ga_ref[...], o_ref[...], gb_ref[...],
                            lng_ref[...], lnb_ref[...], rk_ref[...], _head_blockdiag())
        z = jnp.concatenate([ya.astype(BF16), yb.astype(BF16)], axis=-1)
        zt_ref[...] = z.T
        ho_ref[...] = h_ref[...] + jnp.dot(z, w_ref[...], preferred_element_type=F32)

    vec = _const_spec((1, W))
    return pl.pallas_call(
        body, grid=(T // tm,), name="even_post",
        in_specs=[_row_spec(tm, W)] * 7 + [vec] * 3 + [_row_spec(tm, D), _const_spec((D, D))],
        out_specs=[_row_spec(tm, D), _col_spec(D, tm)],
        out_shape=[jax.ShapeDtypeStruct((T, D), F32), jax.ShapeDtypeStruct((D, T), BF16)],
        compiler_params=_cparams(("parallel",), VMEM_BIG),
    )(y, r, k2, v, ga, o, gb, lng, lnb, rk, h, w_bf)


def even_post_bwd(y, r, k2, v, ga, o, gb, lng, lnb, rk, dh, zt_bf, w_bf, after=None):
    tm = 512
    extra_specs, extra = _after_operand(after)

    def body(y_ref, r_ref, k2_ref, v_ref, ga_ref, o_ref, gb_ref, lng_ref, lnb_ref, rk_ref, dh_ref, zt_ref, w_ref,
             *rest):
        dy_ref, dr_ref, dk2_ref, dv_ref, dga_ref, do_ref, dgb_ref, dlng_ref, dlnb_ref, drk_ref, dw_ref = rest[-11:]
        dzv = _out_proj_back(dh_ref, zt_ref, w_ref, dw_ref)
        bd = _head_blockdiag()
        _, vjp = jax.vjp(lambda *a: _post_math(*a, bd), y_ref[...], r_ref[...], k2_ref[...], v_ref[...], ga_ref[...],
                         o_ref[...], gb_ref[...], lng_ref[...], lnb_ref[...], rk_ref[...])
        dy, dr, dk2, dv, dga, do, dgb, dlng, dlnb, drk = vjp((dzv[:, 0:W], dzv[:, W:2 * W]))
        for ref, val in ((dy_ref, dy), (dr_ref, dr), (dk2_ref, dk2), (dv_ref, dv), (dga_ref, dga), (do_ref, do),
                         (dgb_ref, dgb)):
            ref[...] = val.astype(ref.dtype)

        @pl.when(pl.program_id(0) == 0)
        def _():
            for ref in (dlng_ref, dlnb_ref, drk_ref):
                ref[...] = jnp.zeros_like(ref)

        dlng_ref[...] += dlng
        dlnb_ref[...] += dlnb
        drk_ref[...] += drk

    vec = _const_spec((1, W))
    return pl.pallas_call(
        body, grid=(T // tm,), name="even_post_bwd",
        in_specs=[_row_spec(tm, W)] * 7 + [vec] * 3 + [_row_spec(tm, D), _col_spec(D, tm), _const_spec((D, D))]
        + extra_specs,
        out_specs=[_row_spec(tm, W)] * 7 + [vec] * 3 + [_const_spec((D, D))],
        out_shape=[jax.ShapeDtypeStruct((T, W), dt) for dt in (F32, F32, F32, F32, BF16, F32, BF16)]
        + [jax.ShapeDtypeStruct((1, W), F32)] * 3 + [jax.ShapeDtypeStruct((D, D), F32)],
        compiler_params=_cparams(("arbitrary",), VMEM_BIG),
    )(y, r, k2, v, ga, o, gb, lng, lnb, rk, dh, zt_bf, w_bf, *extra)


PADSEQ = SEQ + LEFT * L
ATT_SCALE = 1.0 / math.sqrt(HD)
ATT_Q = 4
WIN = BAND + (ATT_Q - 1) * L
ATT_STEPS = NC // ATT_Q
ATT_BIAS_SHAPE = (NPAIR, ATT_Q * 2 * L, WIN)


def _stack_chunks(a):
    return jnp.concatenate([_stack_pair(a[i * L:(i + 1) * L]) for i in range(ATT_Q)], axis=0)


def _unstack_chunks(a):
    return jnp.concatenate([_unstack_pair(a[i * 2 * L:(i + 1) * 2 * L]) for i in range(ATT_Q)], axis=0)


def window_bias(bias):
    parts = [jnp.pad(bias, ((0, 0), (0, 0), (i * L, (ATT_Q - 1 - i) * L)), constant_values=NEG) for i in range(ATT_Q)]
    return jnp.concatenate(parts, axis=1)


def _att_probs(q2, kw, bias, step):
    valid = _iota2((1, WIN), 1) >= (LEFT - step * ATT_Q) * L
    s = [jnp.where(valid, _bdot_nt(a, b) * ATT_SCALE + bias[p], NEG) for p, (a, b) in enumerate(zip(q2, kw))]
    e = [jnp.exp(a - jnp.max(a, axis=-1, keepdims=True)) for a in s]
    return [a / jnp.sum(a, axis=-1, keepdims=True) for a in e]


def attention_fwd(q, kpad, vpad, bias):
    def body(q_ref, k_ref, v_ref, b_ref, o_ref):
        step = pl.program_id(1)
        start = pl.multiple_of(step * (ATT_Q * L), L)
        kw = _pairs(k_ref[pl.ds(start, WIN), :])
        vw = _pairs(v_ref[pl.ds(start, WIN), :])
        q2 = [_stack_chunks(a) for a in _pairs(q_ref[...].astype(BF16))]
        p = _att_probs(q2, kw, b_ref[...], step)
        o_ref[...] = jnp.concatenate([_unstack_chunks(_bdot(a, b)) for a, b in zip(p, vw)], axis=-1)

    qblk = pl.BlockSpec((ATT_Q * L, W), lambda b, c: (b * ATT_STEPS + c, 0))
    kblk = pl.BlockSpec((PADSEQ, W), lambda b, c: (b, 0))
    return pl.pallas_call(
        body, grid=(NSEQ, ATT_STEPS), name="attention_fwd",
        in_specs=[qblk, kblk, kblk, _const_spec(ATT_BIAS_SHAPE)],
        out_specs=qblk, out_shape=jax.ShapeDtypeStruct((T, W), F32),
        compiler_params=_cparams(("parallel", "arbitrary")),
    )(q, kpad, vpad, bias)


def attention_bwd(q, kpad, vpad, bias, do):
    def body(q_ref, k_ref, v_ref, b_ref, do_ref, dq_ref, dko_ref, dvo_ref, db_ref, dk_ref, dv_ref):
        b = pl.program_id(0)
        c = pl.program_id(1)

        @pl.when(c == 0)
        def _():
            dk_ref[...] = jnp.zeros_like(dk_ref)
            dv_ref[...] = jnp.zeros_like(dv_ref)

        @pl.when((c == 0) & (b == 0))
        def _():
            db_ref[...] = jnp.zeros_like(db_ref)

        start = pl.multiple_of(c * (ATT_Q * L), L)
        kw = _pairs(k_ref[pl.ds(start, WIN), :])
        vw = _pairs(v_ref[pl.ds(start, WIN), :])
        q2 = [_stack_chunks(a) for a in _pairs(q_ref[...].astype(BF16))]
        do2 = [_stack_chunks(a) for a in _pairs(do_ref[...].astype(BF16))]
        p = _att_probs(q2, kw, b_ref[...], c)
        dp = [_bdot_nt(a, b) for a, b in zip(do2, vw)]
        ds = [a * (d - jnp.sum(d * a, axis=-1, keepdims=True)) for a, d in zip(p, dp)]
        dss = [(a * ATT_SCALE).astype(BF16) for a in ds]
        dq_ref[...] = jnp.concatenate([_unstack_chunks(_bdot(a, b)) for a, b in zip(dss, kw)], axis=-1).astype(BF16)
        dk_ref[pl.ds(start, WIN), :] += jnp.concatenate([_bdot_tn(a, b) for a, b in zip(dss, q2)], axis=-1)
        dv_ref[pl.ds(start, WIN), :] += jnp.concatenate([_bdot_tn(a, b) for a, b in zip(p, do2)], axis=-1)
        for i in range(NPAIR):
            db_ref[i] += ds[i]

        @pl.when(c == ATT_STEPS - 1)
        def _():
            dko_ref[...] = dk_ref[LEFT * L:, :].astype(BF16)
            dvo_ref[...] = dv_ref[LEFT * L:, :].astype(BF16)

    qblk = pl.BlockSpec((ATT_Q * L, W), lambda b, c: (b * ATT_STEPS + c, 0))
    kblk = pl.BlockSpec((PADSEQ, W), lambda b, c: (b, 0))
    sblk = pl.BlockSpec((SEQ, W), lambda b, c: (b, 0))
    bblk = _const_spec(ATT_BIAS_SHAPE)
    return pl.pallas_call(
        body, grid=(NSEQ, ATT_STEPS), name="attention_bwd",
        in_specs=[qblk, kblk, kblk, bblk, qblk],
        out_specs=[qblk, sblk, sblk, bblk],
        out_shape=[jax.ShapeDtypeStruct((T, W), BF16), jax.ShapeDtypeStruct((T, W), BF16),
                   jax.ShapeDtypeStruct((T, W), BF16), jax.ShapeDtypeStruct(ATT_BIAS_SHAPE, F32)],
        scratch_shapes=[pltpu.VMEM((PADSEQ, W), F32), pltpu.VMEM((PADSEQ, W), F32)],
        compiler_params=_cparams(("arbitrary", "arbitrary"), VMEM_BIG),
    )(q, kpad, vpad, bias, do)


NTAB = 2 * CLIP + 1
EXT = BAND + L


def _ext_onehot():
    n = _iota2((EXT, NTAB), 0)
    m = _iota2((EXT, NTAB), 1)
    return (jnp.clip(BAND - 1 - n, -CLIP, CLIP) + CLIP == m).astype(F32)


def bias_expand(table):
    def body(t_ref, o_ref):
        ext = _hdot_nt(t_ref[...], _ext_onehot())
        for i in range(L):
            s = L - 1 - i
            o_ref[:, i, :] = (pltpu.roll(ext, EXT - s, 1) if s else ext)[:, :BAND]

    return pl.pallas_call(body, name="bias_expand", out_shape=jax.ShapeDtypeStruct((NH, L, BAND), F32))(table)


def bias_grad(dbias):
    def body(d_ref, o_ref):
        acc = jnp.zeros((NH, EXT), F32)
        zpad = jnp.zeros((NH, EXT - BAND), F32)
        for i in range(L):
            s = L - 1 - i
            row = jnp.concatenate([d_ref[:, i, :], zpad], axis=-1)
            acc = acc + (pltpu.roll(row, s, 1) if s else row)
        o_ref[...] = _hdot(acc, _ext_onehot())

    return pl.pallas_call(body, name="bias_grad", out_shape=jax.ShapeDtypeStruct((NH, NTAB), F32))(dbias)


def _group_cols(g):
    return slice(g * SGC, (g + 1) * SGC)


def _sg_norm(gv, lng, lnb):
    gc = gv - jnp.mean(gv, axis=-1, keepdims=True)
    rstd = lax.rsqrt(jnp.mean(gc * gc, axis=-1, keepdims=True) + LN_EPS)
    xhat = gc * rstd
    return xhat, rstd, xhat * lng + lnb


GMLP_BWD_CHUNKS = 2


def gmlp_fwd_loss(u, v, gate, lng, lnb, wm_bf, sgb_t, h, w_bf, g_final, target):
    tm = GMLP_BWD_CHUNKS * SGC

    def body(u_ref, v_ref, gt_ref, lng_ref, lnb_ref, wm_ref, sb_ref, h_ref, w_ref, g_ref, t_ref,
             dh_ref, loss_ref, dg_ref, zt_ref):
        zs = []
        for ch in range(GMLP_BWD_CHUNKS):
            rows = slice(ch * SGC, (ch + 1) * SGC)
            _, _, vln = _sg_norm(_gelu(v_ref[rows, :]), lng_ref[...], lnb_ref[...])
            vlb = vln.astype(BF16)
            zg = []
            for g in range(NG):
                cs = _group_cols(g)
                sv = jnp.dot(wm_ref[g], vlb[:, cs], preferred_element_type=F32) + sb_ref[:, g:g + 1]
                zg.append((_gelu(u_ref[rows, cs]) * sv * _silu(gt_ref[rows, cs])).astype(BF16))
            zs.append(jnp.concatenate(zg, axis=-1))
        z = jnp.concatenate(zs, axis=0)
        zt_ref[...] = z.T
        xv = h_ref[...] + jnp.dot(z, w_ref[...], preferred_element_type=F32)
        rstd = lax.rsqrt(jnp.mean(xv * xv, axis=-1, keepdims=True) + RMS_EPS)
        xhat = xv * rstd
        err = xhat * g_ref[...] - t_ref[...]
        part = 0.5 * jnp.sum(jnp.mean(err * err, axis=-1, keepdims=True), axis=0, keepdims=True)
        dout = err * (1.0 / D)

        @pl.when(pl.program_id(0) == 0)
        def _():
            loss_ref[...] = jnp.zeros_like(loss_ref)
            dg_ref[...] = jnp.zeros_like(dg_ref)

        loss_ref[...] += jnp.broadcast_to(part, loss_ref.shape)
        dg_ref[...] += jnp.sum(dout * xhat, axis=0, keepdims=True)
        dxh = dout * g_ref[...]
        dh_ref[...] = rstd * (dxh - xhat * jnp.mean(dxh * xhat, axis=-1, keepdims=True))

    return pl.pallas_call(
        body, grid=(T // tm,), name="gmlp_fwd_loss",
        in_specs=[_row_spec(tm, D)] * 3 + [_const_spec((1, D))] * 2
        + [_const_spec((NG, SGC, SGC)), _const_spec((SGC, NG)), _row_spec(tm, D), _const_spec((D, D)),
           _const_spec((1, D)), _row_spec(tm, D)],
        out_specs=[_row_spec(tm, D), _const_spec((8, 128)), _const_spec((1, D)), _col_spec(D, tm)],
        out_shape=[jax.ShapeDtypeStruct((T, D), F32), jax.ShapeDtypeStruct((8, 128), F32),
                   jax.ShapeDtypeStruct((1, D), F32), jax.ShapeDtypeStruct((D, T), BF16)],
        compiler_params=_cparams(("arbitrary",), VMEM_BIG),
    )(u, v, gate, lng, lnb, wm_bf, sgb_t, h, w_bf, g_final, target)


def gmlp_bwd(u, v, gate, lng, lnb, wm_bf, sgb_t, dh, zt_bf, w_bf):
    def body(u_ref, v_ref, gt_ref, lng_ref, lnb_ref, wm_ref, sb_ref, dh_ref, zt_ref, w_ref,
             du_ref, dv_ref, dgt_ref, dlng_ref, dlnb_ref, dwm_ref, dsb_ref, dw_ref):
        @pl.when(pl.program_id(0) == 0)
        def _():
            for ref in (dlng_ref, dlnb_ref, dwm_ref, dsb_ref):
                ref[...] = jnp.zeros_like(ref)

        dz = _out_proj_back(dh_ref, zt_ref, w_ref, dw_ref)
        sel = (_iota2((D, NG), 0) // SGC == _iota2((D, NG), 1)).astype(F32)
        for ch in range(GMLP_BWD_CHUNKS):
            rows = slice(ch * SGC, (ch + 1) * SGC)
            gv, dgv_dv = _gelu_both(v_ref[rows, :])
            xhat, rstd, vln = _sg_norm(gv, lng_ref[...], lnb_ref[...])
            vlb = vln.astype(BF16)
            dvln = []
            dsv_all = []
            for g in range(NG):
                cs = _group_cols(g)
                uu = u_ref[rows, cs]
                gg = gt_ref[rows, cs]
                dzz = dz[rows, cs]
                sv = jnp.dot(wm_ref[g], vlb[:, cs], preferred_element_type=F32) + sb_ref[:, g:g + 1]
                gu, dgu = _gelu_both(uu)
                sg, dsg = _silu_both(gg)
                dzgu = dzz * gu
                dsv = dzgu * sg
                dgt_ref[rows, cs] = (dzgu * sv * dsg).astype(BF16)
                du_ref[rows, cs] = (dzz * sv * sg * dgu).astype(BF16)
                dsb16 = dsv.astype(BF16)
                dvln.append(lax.dot_general(wm_ref[g], dsb16, (((0,), (0,)), ((), ())), preferred_element_type=F32))
                dwm_ref[g] += lax.dot_general(dsb16, vlb[:, cs], (((1,), (1,)), ((), ())),
                                              preferred_element_type=F32)
                dsv_all.append(dsv)
            dvl = jnp.concatenate(dvln, axis=-1)
            dsb_ref[...] += _hdot(jnp.concatenate(dsv_all, axis=-1), sel)
            dlng_ref[...] += jnp.sum(dvl * xhat, axis=0, keepdims=True)
            dlnb_ref[...] += jnp.sum(dvl, axis=0, keepdims=True)
            dxh = dvl * lng_ref[...]
            dgv = rstd * (dxh - jnp.mean(dxh, axis=-1, keepdims=True)
                          - xhat * jnp.mean(dxh * xhat, axis=-1, keepdims=True))
            dv_ref[rows, :] = (dgv * dgv_dv).astype(BF16)

    tm = GMLP_BWD_CHUNKS * SGC
    return pl.pallas_call(
        body, grid=(T // tm,), name="gmlp_bwd",
        in_specs=[_row_spec(tm, D)] * 3 + [_const_spec((1, D))] * 2
        + [_const_spec((NG, SGC, SGC)), _const_spec((SGC, NG)), _row_spec(tm, D), _col_spec(D, tm),
           _const_spec((D, D))],
        out_specs=[_row_spec(tm, D)] * 3 + [_const_spec((1, D))] * 2
        + [_const_spec((NG, SGC, SGC)), _const_spec((SGC, NG)), _const_spec((D, D))],
        out_shape=[jax.ShapeDtypeStruct((T, D), BF16)] * 3 + [jax.ShapeDtypeStruct((1, D), F32)] * 2
        + [jax.ShapeDtypeStruct((NG, SGC, SGC), F32), jax.ShapeDtypeStruct((SGC, NG), F32),
           jax.ShapeDtypeStruct((D, D), F32)],
        compiler_params=_cparams(("arbitrary",), VMEM_BIG),
    )(u, v, gate, lng, lnb, wm_bf, sgb_t, dh, zt_bf, w_bf)


NCHIP = 4
NDEV = 8
ANY = pl.BlockSpec(memory_space=pl.ANY)


HBM = pl.BlockSpec(memory_space=pltpu.HBM)
SEM = pl.BlockSpec(memory_space=pltpu.SEMAPHORE)
EFFECT = pltpu.SideEffectType.DATAFLOW_SIDE_EFFECTING


def _peers(whole_mesh):
    x, y, c = lax.axis_index("x"), lax.axis_index("y"), lax.axis_index("c")
    if not whole_mesh:
        return [((px, py, c), 2 * px + py) for px, py in ((1 - x, y), (x, 1 - y), (1 - x, 1 - y))], 2 * x + y
    out = []
    for j in range(1, NDEV):
        px, py, pc = x ^ (j >> 2), y ^ ((j >> 1) & 1), c ^ (j & 1)
        out.append(((px, py, pc), 4 * px + 2 * py + pc))
    return out, 4 * x + 2 * y + c


def _send_copies(src, land, send, recv, scatter, whole_mesh, starting):
    peers, me = _peers(whole_mesh)
    copies = []
    for t in range(len(src)):
        for j, (dev, slot) in enumerate(peers):
            k = t * len(peers) + j
            copies.append(pltpu.make_async_remote_copy(
                src_ref=src[t].at[slot] if scatter else src[t], dst_ref=land[t].at[me if starting else slot],
                send_sem=send.at[k], recv_sem=recv.at[k], device_id=dev, device_id_type=MESH))
    return copies


def _own_copies(src, land, sems, scatter, whole_mesh):
    _, me = _peers(whole_mesh)
    return [pltpu.make_async_copy(src[t].at[me] if scatter else src[t], land[t].at[me], sems.at[t])
            for t in range(len(src))]


def send_start(srcs, scatter, whole_mesh, name, after=None):
    n = len(srcs)
    slots = NDEV if whole_mesh else NCHIP
    nsem = n * (slots - 1)
    extra_specs, extra = _after_operand(after)
    lands = [pltpu.HBM(a.shape if scatter else (slots,) + a.shape, a.dtype) for a in srcs]

    def body(*refs):
        first_out = n + len(extra)
        send, recv, own = refs[first_out:first_out + 3]
        src, land = refs[:n], refs[first_out + 3 + n:first_out + 3 + 2 * n]
        for cp in _own_copies(src, land, own, scatter, whole_mesh):
            cp.start()
        for cp in _send_copies(src, land, send, recv, scatter, whole_mesh, True):
            cp.start()
        refs[-1][...] = jnp.zeros_like(refs[-1])

    out = pl.pallas_call(
        body, name=name,
        out_shape=(pltpu.SemaphoreType.DMA((nsem,)), pltpu.SemaphoreType.DMA((nsem,)), pltpu.SemaphoreType.DMA((n,)),
                   *[pltpu.HBM(a.shape, a.dtype) for a in srcs], *lands, jax.ShapeDtypeStruct((8, 128), F32)),
        in_specs=[HBM] * n + extra_specs,
        out_specs=(SEM, SEM, SEM, *[HBM] * (2 * n), pl.BlockSpec(memory_space=pltpu.VMEM)),
        input_output_aliases={i: 3 + i for i in range(n)},
        compiler_params=pltpu.CompilerParams(has_side_effects=EFFECT),
    )(*[pltpu.with_memory_space_constraint(a, pltpu.HBM) for a in srcs], *extra)
    return out[0], out[1], out[2], list(out[3:3 + n]), list(out[3 + n:3 + 2 * n]), out[-1]


def send_wait(started, after, scatter, whole_mesh, name):
    send, recv, own, srcs, lands, _ = started
    n = len(srcs)

    def body(*refs):
        src, land = refs[:n], refs[n:2 * n]
        for cp in _own_copies(src, land, refs[2 * n + 2], scatter, whole_mesh):
            cp.wait()
        for cp in _send_copies(src, land, refs[2 * n], refs[2 * n + 1], scatter, whole_mesh, False):
            cp.wait_send()
            cp.wait_recv()

    arrs = list(srcs) + list(lands)
    out = pl.pallas_call(
        body, name=name, out_shape=tuple(pltpu.HBM(a.shape, a.dtype) for a in arrs),
        in_specs=[HBM] * (2 * n) + [SEM, SEM, SEM, ANY], out_specs=tuple([HBM] * (2 * n)),
        input_output_aliases={i: i for i in range(2 * n)},
        compiler_params=pltpu.CompilerParams(has_side_effects=EFFECT),
    )(*arrs, send, recv, own, after)
    return list(out[n:])


def exchange_c(arrs, name):
    n = len(arrs)

    def body(*refs):
        ins, outs = refs[:n], refs[n:2 * n]
        send, recv = refs[2 * n:]
        sibling = (lax.axis_index("x"), lax.axis_index("y"), 1 - lax.axis_index("c"))
        copies = [pltpu.make_async_remote_copy(src_ref=ins[t], dst_ref=outs[t], send_sem=send.at[t], recv_sem=recv.at[t],
                                               device_id=sibling, device_id_type=MESH) for t in range(n)]
        for cp in copies:
            cp.start()
        for cp in copies:
            cp.wait()

    return pl.pallas_call(
        body, name=name, in_specs=[ANY] * n, out_specs=[ANY] * n,
        out_shape=[jax.ShapeDtypeStruct(a.shape, a.dtype) for a in arrs],
        scratch_shapes=[pltpu.SemaphoreType.DMA((n,)), pltpu.SemaphoreType.DMA((n,))],
    )(*arrs)


def gather_weights(arrs, split):
    n = len(arrs)

    def body(*refs):
        ins, outs = refs[:n], refs[n:2 * n]
        send1, recv1, send2, recv2, loc = refs[2 * n:]
        x, y, c = lax.axis_index("x"), lax.axis_index("y"), lax.axis_index("c")
        me = 2 * x + y
        sibling = (x, y, 1 - c)
        peers = [(1 - x, y), (x, 1 - y), (1 - x, 1 - y)]

        def rows_of(t, core):
            half = arrs[t].shape[0] // 2
            return pl.ds(core * half, half)

        def part(ref, t, core):
            return ref.at[rows_of(t, core)] if split[t] else ref

        local = [pltpu.make_async_copy(ins[t], outs[t].at[me], loc.at[t]) for t in range(n)]
        for cp in local:
            cp.start()
        first = []
        for t in range(n):
            for j, (px, py) in enumerate(peers):
                first.append(pltpu.make_async_remote_copy(
                    src_ref=part(ins[t], t, c), dst_ref=part(outs[t].at[me], t, c), send_sem=send1.at[t, j],
                    recv_sem=recv1.at[t, j], device_id=(px, py, c), device_id_type=MESH))
        for cp in first:
            cp.start()
        passed = []
        for t in range(n):
            for j, (px, py) in enumerate(peers):
                landed = part(outs[t].at[2 * px + py], t, c)
                pltpu.make_async_remote_copy(
                    src_ref=landed, dst_ref=landed, send_sem=send1.at[t, j], recv_sem=recv1.at[t, j],
                    device_id=(x, y, c), device_id_type=MESH).wait_recv()
                if split[t]:
                    cp = pltpu.make_async_remote_copy(
                        src_ref=landed, dst_ref=landed, send_sem=send2.at[t, j], recv_sem=recv2.at[t, j],
                        device_id=sibling, device_id_type=MESH)
                    cp.start()
                    passed.append(cp)
        for t in range(n):
            for j, (px, py) in enumerate(peers):
                if split[t]:
                    other = part(outs[t].at[2 * px + py], t, 1 - c)
                    pltpu.make_async_remote_copy(
                        src_ref=other, dst_ref=other, send_sem=send2.at[t, j], recv_sem=recv2.at[t, j],
                        device_id=(x, y, c), device_id_type=MESH).wait_recv()
        for cp in first + passed:
            cp.wait_send()
        for cp in local:
            cp.wait()

    return pl.pallas_call(
        body, name="gather_weights", in_specs=[ANY] * n, out_specs=[ANY] * n,
        out_shape=[jax.ShapeDtypeStruct((NCHIP,) + a.shape, a.dtype) for a in arrs],
        scratch_shapes=[pltpu.SemaphoreType.DMA((n, 3))] * 4 + [pltpu.SemaphoreType.DMA((n,))],
    )(*arrs)


def _adam_math(g, w, m, v):
    m = ADAM_B1 * m + (1.0 - ADAM_B1) * g
    v = ADAM_B2 * v + (1.0 - ADAM_B2) * (g * g)
    m_hat = m / (1.0 - ADAM_B1 ** ADAM_STEP)
    v_hat = v / (1.0 - ADAM_B2 ** ADAM_STEP)
    delta = -ADAM_LR * (m_hat / (jnp.sqrt(v_hat) + ADAM_EPS) + ADAM_WD * w)
    return delta, m, v


def _rows_tile(rows):
    return rows if rows <= 256 else 256


def sum_chips(parts, name):
    _, rows, cols = parts.shape
    tr = _rows_tile(rows)

    def body(p_ref, o_ref):
        acc = p_ref[0].astype(F32)
        for s in range(1, NCHIP):
            acc = acc + p_ref[s].astype(F32)
        o_ref[...] = acc

    return pl.pallas_call(
        body, grid=(rows // tr,), name=name,
        in_specs=[pl.BlockSpec((NCHIP, tr, cols), lambda i: (0, i, 0))],
        out_specs=pl.BlockSpec((tr, cols), lambda i: (i, 0)),
        out_shape=jax.ShapeDtypeStruct((rows, cols), F32),
        compiler_params=_cparams(("parallel",)),
    )(parts)


def adam_shard(p_mine, p_sib, w, m, v, name):
    rows, cols = p_mine.shape
    tr = _rows_tile(rows)
    lead = w.ndim == 3

    def body(a_ref, b_ref, w_ref, m_ref, v_ref, g_ref, d_ref, mo_ref, vo_ref):
        g = a_ref[...] + b_ref[...]
        g = g[None] if lead else g
        g_ref[...] = g
        d_ref[...], mo_ref[...], vo_ref[...] = _adam_math(g, w_ref[...], m_ref[...], v_ref[...])

    flat = pl.BlockSpec((tr, cols), lambda i: (i, 0))
    spec = pl.BlockSpec((1, tr, cols), lambda i: (0, i, 0)) if lead else flat
    return pl.pallas_call(
        body, grid=(rows // tr,), name=name, in_specs=[flat] * 2 + [spec] * 3, out_specs=[spec] * 4,
        out_shape=[jax.ShapeDtypeStruct(w.shape, F32)] * 4,
        compiler_params=_cparams(("parallel",)),
    )(p_mine, p_sib, w, m, v)


def adam_shard_halves_t(r_mine, r_sib, wt, mt, vt, name):
    hrows, cols = r_mine.shape
    tr = _rows_tile(hrows)
    per_half = hrows // tr

    def body(a_ref, b_ref, w_ref, m_ref, v_ref, g_ref, d_ref, mo_ref, vo_ref):
        mine = pl.program_id(0) == lax.axis_index("c")
        g = jnp.where(mine, a_ref[...], b_ref[...]).T[None]
        g_ref[...] = g
        d_ref[...], mo_ref[...], vo_ref[...] = _adam_math(g, w_ref[...], m_ref[...], v_ref[...])

    flat = pl.BlockSpec((tr, cols), lambda h, i: (i, 0))
    spec = pl.BlockSpec((1, cols, tr), lambda h, i: (0, 0, h * per_half + i))
    return pl.pallas_call(
        body, grid=(2, per_half), name=name, in_specs=[flat] * 2 + [spec] * 3, out_specs=[spec] * 4,
        out_shape=[jax.ShapeDtypeStruct(wt.shape, F32)] * 4,
        compiler_params=_cparams(("parallel", "parallel")),
    )(r_mine, r_sib, wt, mt, vt)


def adam_replicated(gathered, params, name):
    flat = []
    for i, p in enumerate(params):
        if isinstance(p, list):
            off = 0
            for wmv in p:
                n = gathered[i].shape[-1] - off if wmv[0] is None else wmv[0].shape[-1]
                flat.append((i, (off, n), wmv))
                off += n
        else:
            flat.append((i, None, p))
    ins = [a for _, _, wmv in flat for a in wmv if a is not None]
    ng = len(gathered)

    def body(*refs):
        g_refs = refs[:ng]
        in_refs = list(refs[ng:ng + len(ins)])
        out_refs = list(refs[ng + len(ins):])
        sums = []
        for r in g_refs:
            g = r[0]
            for d in range(1, NDEV):
                g = g + r[d]
            sums.append(g)
        for i, lanes, wmv in flat:
            g = sums[i] if lanes is None else sums[i][:, lanes[0]:lanes[0] + lanes[1]]
            out_refs.pop(0)[...] = g
            if wmv[0] is not None:
                w_ref, m_ref, v_ref = in_refs.pop(0), in_refs.pop(0), in_refs.pop(0)
                d_ref, mo_ref, vo_ref = out_refs.pop(0), out_refs.pop(0), out_refs.pop(0)
                d_ref[...], mo_ref[...], vo_ref[...] = _adam_math(g, w_ref[...], m_ref[...], v_ref[...])

    out_shape = []
    for i, lanes, wmv in flat:
        shape = gathered[i].shape[1:] if lanes is None else (1, lanes[1])
        out_shape += [jax.ShapeDtypeStruct(shape, F32)] * (4 if wmv[0] is not None else 1)
    outs = list(pl.pallas_call(body, name=name, out_shape=out_shape)(*gathered, *ins))
    return [[outs.pop(0) for _ in range(4 if wmv[0] is not None else 1)] for _, _, wmv in flat]


EVEN_SPLITS = (SHIFT, W, W, W, W, W)
ODD_SPLITS = (D, D, D)


def _cols_to_chips(a):
    rows, cols = a.shape
    return a.reshape(rows, NCHIP, cols // NCHIP).transpose(1, 0, 2)


def _chips_to_cols(a):
    _, rows, n = a.shape
    return a.transpose(1, 0, 2).reshape(rows, NCHIP * n)


def kernel(x, norm_g, w_in_e, shift_mu, rw_w0, rw_w2, rw_a0, rw_a2, rw_kk, rw_ka, rw_rk, rw_lnx_g, rw_lnx_b, att_bias, w_out_e, w_in_o, sg_ln_g, sg_ln_b, sg_w, sg_b, w_out_o, final_g, loss_target, m_norm_g, m_w_in_e, m_shift_mu, m_rw_w0, m_rw_w2, m_rw_a0, m_rw_a2, m_rw_kk, m_rw_ka, m_rw_rk, m_rw_lnx_g, m_rw_lnx_b, m_att_bias, m_w_out_e, m_w_in_o, m_sg_ln_g, m_sg_ln_b, m_sg_w, m_sg_b, m_w_out_o, m_final_g, v_norm_g, v_w_in_e, v_shift_mu, v_rw_w0, v_rw_w2, v_rw_a0, v_rw_a2, v_rw_kk, v_rw_ka, v_rw_rk, v_rw_lnx_g, v_rw_lnx_b, v_att_bias, v_w_out_e, v_w_in_o, v_sg_ln_g, v_sg_ln_b, v_sg_w, v_sg_b, v_w_out_o, v_final_g):
    x2 = x.reshape(T, D)
    tgt = loss_target.reshape(T, D)

    gathered = gather_weights(
        [jnp.swapaxes(w_in_e[0], 0, 1).astype(BF16), jnp.concatenate([rw_w2[0], rw_a2[0]], axis=0),
         jnp.concatenate([sg_ln_g, sg_ln_b], axis=0)], [True, True, False])
    wie = gathered[0].reshape(EVEN_IN, D)
    w2 = _chips_to_cols(gathered[1][:, :LORA])
    a2 = _chips_to_cols(gathered[1][:, LORA:])
    sglg = _chips_to_cols(gathered[2][:, 0:1])
    sglb = _chips_to_cols(gathered[2][:, 1:2])

    late = [w_out_e[0].astype(BF16), w_in_o[0].astype(BF16), w_out_o[0].astype(BF16)]
    late_started = send_start(late, False, False, "late_weights_start", after=gathered[0])

    def late_weights(after):
        woe, wio, woo = send_wait(late_started, after, False, False, "late_weights_wait")
        return woe.reshape(D, D), _chips_to_cols(wio), woo.reshape(D, D)

    def scatter_start(grads, name):
        return send_start([g_.astype(BF16) if g_.shape[-1] >= W else g_ for g_ in grads], True, False, name)

    started = {}

    def on_odd_grads(d_woo, d_wio):
        started["odd"] = scatter_start([d_woo.reshape(NCHIP, D // NCHIP, D), d_wio], "odd_grads_start")
        return started["odd"][-1]

    def on_even_grads(big_g):
        d_wie_half, d_woe, _, _, d_w2, d_a2, d_sglg, d_sglb = big_g
        blocks = [d_wie_half, d_woe.reshape(NCHIP, D // NCHIP, D), _cols_to_chips(d_w2), _cols_to_chips(d_a2),
                  _cols_to_chips(d_sglg), _cols_to_chips(d_sglb)]
        started["even"] = scatter_start(blocks, "even_grads_start")
        return started["even"][-1]

    def on_small_grads(layer, grads):
        if layer == "odd":
            d_sg_w, d_sg_b, d_final, d_g1 = grads
            mine = [d_sg_w.reshape(NG * SGC, SGC), d_sg_b, jnp.concatenate([d_final, d_g1], axis=1)]
        else:
            mine = [grads[-2], jnp.concatenate(grads[:-2] + grads[-1:], axis=1)]
        started[layer + "_small"] = send_start(mine, False, True, layer + "_small_grads_start")
        return started[layer + "_small"][-1]

    loss_part, dx, _, _ = _local_step(
        x2, tgt, wie, late_weights, w2, a2, sglg, sglb, norm_g, shift_mu, rw_w0, rw_a0, rw_kk, rw_ka, rw_rk,
        rw_lnx_g, rw_lnx_b, att_bias, sg_w, sg_b, final_g, first_after=late_started[-1], on_odd_grads=on_odd_grads,
        on_even_grads=on_even_grads, on_small_grads=on_small_grads)
    wmv = {"w_in_e": tuple(jnp.swapaxes(a, 1, 2) for a in (w_in_e, m_w_in_e, v_w_in_e)),
           "w_out_e": (w_out_e, m_w_out_e, v_w_out_e),
           "w_in_o": (w_in_o, m_w_in_o, v_w_in_o), "w_out_o": (w_out_o, m_w_out_o, v_w_out_o),
           "rw_w2": (rw_w2, m_rw_w2, v_rw_w2), "rw_a2": (rw_a2, m_rw_a2, v_rw_a2),
           "sg_ln_g": (sg_ln_g, m_sg_ln_g, v_sg_ln_g), "sg_ln_b": (sg_ln_b, m_sg_ln_b, v_sg_ln_b)}
    sharded = {}

    def finish(names, landed, tag):
        partial = [sum_chips(p_, "sum_" + nm) for p_, nm in zip(landed, names)]
        from_sibling = exchange_c(partial, "swap_partials_" + tag)
        for nm, mine, sib in zip(names, partial, from_sibling):
            if nm == "w_in_e":
                res = adam_shard_halves_t(mine, sib, *wmv[nm], "adam_" + nm)
                sharded[nm] = [jnp.swapaxes(a, 1, 2) for a in res]
            else:
                sharded[nm] = adam_shard(mine, sib, *wmv[nm], "adam_" + nm)
        return partial[0]

    odd_landed = send_wait(started["odd"], started["even_small"][-1], True, False, "odd_grads_wait")
    done = finish(["w_out_o", "w_in_o"], odd_landed, "odd")

    def wmv_of(*arrs, view=lambda a: a):
        return tuple(view(a) for a in arrs)

    vec = lambda a: a.reshape(1, -1)
    groups = {
        "odd": (["sg_w", "sg_b", "final_g", "norm_g1"],
                [wmv_of(sg_w, m_sg_w, v_sg_w, view=lambda a: a.reshape(NG * SGC, SGC)),
                 wmv_of(sg_b, m_sg_b, v_sg_b, view=lambda a: a[0]),
                 [wmv_of(final_g, m_final_g, v_final_g, view=vec),
                  wmv_of(norm_g, m_norm_g, v_norm_g, view=lambda a: a[1:2])]]),
        "even": (["att_bias", "norm_g0", "shift_mu", "rw_w0", "rw_a0", "rw_kk", "rw_ka", "rw_rk", "rw_lnx_g",
                  "rw_lnx_b", "loss"],
                 [wmv_of(att_bias, m_att_bias, v_att_bias, view=lambda a: a[0]),
                  [wmv_of(norm_g, m_norm_g, v_norm_g, view=lambda a: a[0:1]),
                   wmv_of(shift_mu, m_shift_mu, v_shift_mu), wmv_of(rw_w0, m_rw_w0, v_rw_w0),
                   wmv_of(rw_a0, m_rw_a0, v_rw_a0), wmv_of(rw_kk, m_rw_kk, v_rw_kk), wmv_of(rw_ka, m_rw_ka, v_rw_ka),
                   wmv_of(rw_rk, m_rw_rk, v_rw_rk, view=vec), wmv_of(rw_lnx_g, m_rw_lnx_g, v_rw_lnx_g),
                   wmv_of(rw_lnx_b, m_rw_lnx_b, v_rw_lnx_b), (None, None, None)]]),
    }
    rep = {}
    for layer in ("odd", "even"):
        nms, params = groups[layer]
        gathered_g = send_wait(started[layer + "_small"], done, False, True, layer + "_small_grads_wait")
        for nm, res in zip(nms, adam_replicated(gathered_g, params, "adam_" + layer + "_small")):
            rep[nm] = res
        done = rep[nms[0]][0]
    native = {"sg_w": sg_w.shape, "sg_b": sg_b.shape, "final_g": final_g.shape, "rw_rk": rw_rk.shape,
              "att_bias": att_bias.shape}
    for nm, shape in native.items():
        rep[nm] = [a.reshape(shape) for a in rep[nm]]
    rep["norm_g"] = [jnp.concatenate([a, b], axis=0) for a, b in zip(rep["norm_g0"], rep["norm_g1"])]
    even_landed = send_wait(started["even"], done, True, False, "even_grads_wait")
    finish(["w_in_e", "w_out_e", "rw_w2", "rw_a2", "sg_ln_g", "sg_ln_b"], even_landed, "even")

    order = ["norm_g", "w_in_e", "shift_mu", "rw_w0", "rw_w2", "rw_a0", "rw_a2", "rw_kk", "rw_ka", "rw_rk",
             "rw_lnx_g", "rw_lnx_b", "att_bias", "w_out_e", "w_in_o", "sg_ln_g", "sg_ln_b", "sg_w", "sg_b",
             "w_out_o", "final_g"]
    results = {**sharded, **rep}
    outs = [rep["loss"][0][0, 0], dx.reshape(NSEQ, SEQ, D)]
    for kind in range(4):
        outs += [results[nm][kind] for nm in order]
    return tuple(outs)


def _local_step(x2, tgt, wie_t, late_weights, w2, a2, sglg, sglb, norm_g, shift_mu, rw_w0, rw_a0, rw_kk, rw_ka, rw_rk,
                rw_lnx_g, rw_lnx_b, att_bias, sg_w, sg_b, final_g, first_after=None, on_odd_grads=None,
                on_even_grads=None, on_small_grads=None):
    zl = jnp.zeros((LORA, W), F32)
    w2x = jnp.concatenate([w2, zl], axis=0)
    a2x = jnp.concatenate([zl, a2], axis=0)
    rk = rw_rk.reshape(1, W)
    pos = np.arange(SGC)
    sg_mask = jnp.asarray(((pos[None, :] // L) <= (pos[:, None] // L)).astype(np.float32))
    wm = (sg_w[0] * sg_mask[None]).astype(BF16)
    sgb_t = sg_b[0].T

    xn0, ps, ga, q, kb, vb, gb = ln_in_proj(x2, norm_g[0:1], wie_t, EVEN_SPLITS, "in_proj_even", after=first_after,
                                            w_t=True)
    r, lw, k2, v, aa, bb = even_prep(ps, shift_mu, rw_w0, w2x, rw_a0, a2x, rw_kk, rw_ka)
    y, rw_saved = rwkv_fwd(r, lw, k2, v, aa, bb)
    bias = window_bias(bias_expand(att_bias[0]).reshape(NPAIR, 2 * L, BAND))

    def padded(a):
        return jnp.pad(a.astype(BF16).reshape(NSEQ, SEQ, W), ((0, 0), (LEFT * L, 0), (0, 0))).reshape(NSEQ * PADSEQ, W)

    kpad, vpad = padded(kb), padded(vb)
    o = attention_fwd(q, kpad, vpad, bias)
    woe, wio, woo = late_weights(o)
    h1, zt = even_post(y, r, k2, v, ga, o, gb, rw_lnx_g, rw_lnx_b, rk, x2, woe)
    xn1, u, vv, gt = ln_in_proj(h1, norm_g[1:2], wio, ODD_SPLITS, "in_proj_odd")
    dh2, loss_part, d_final_g, z2t = gmlp_fwd_loss(u, vv, gt, sglg, sglb, wm, sgb_t, h1, woo, final_g[None], tgt)

    du, dvv, dgt, d_sglg, d_sglb, d_wm, d_sgb_t, d_woo = gmlp_bwd(u, vv, gt, sglg, sglb, wm, sgb_t, dh2, z2t, woo)
    dp_odd = [du, dvv, dgt]
    d_wio = matmul_acc_chips(xn1, dp_odd, "in_proj_odd_dw")
    token = on_odd_grads(d_woo, d_wio) if on_odd_grads else None
    dh1, d_g1 = in_proj_bwd_x(h1, norm_g[1:2], wio, dp_odd, dh2, "in_proj_odd_bwd", after=token)
    odd_small = [d_wm * sg_mask[None], d_sgb_t.T, d_final_g, d_g1]
    token = on_small_grads("odd", odd_small) if on_small_grads else None
    dy, dr2, dk22, dv2, dga, do, dgb, d_lng, d_lnb, d_rk, d_woe = even_post_bwd(
        y, r, k2, v, ga, o, gb, rw_lnx_g, rw_lnx_b, rk, dh1, zt, woe, after=token)
    dq, dkb, dvb, dbias = attention_bwd(q, kpad, vpad, bias, do)
    dbias = sum(dbias[:, i * 2 * L:(i + 1) * 2 * L, i * L:i * L + BAND] for i in range(ATT_Q))
    d_att_bias = bias_grad(dbias.reshape(NH, L, BAND))
    dr, dlw, dk2, dv, daa, dbb = rwkv_bwd(r, lw, k2, aa, bb, rw_saved, dy)
    dps, d_mu, d_w0, d_w2x, d_a0, d_a2x, d_kk, d_ka = even_prep_bwd(
        ps, shift_mu, rw_w0, w2x, rw_a0, a2x, rw_kk, rw_ka, dr, dlw, dk2, dv, daa, dbb, dr2, dk22, dv2)
    dp_even = [dps, dga, dq, dkb, dvb, dgb]
    d_wie = matmul_acc_chips(xn0, dp_even, "in_proj_even_dw", add_cores=on_even_grads is not None)
    big_g = (d_wie, d_woe, d_wio, d_woo, d_w2x[:LORA], d_a2x[LORA:], d_sglg, d_sglb)
    token = on_even_grads(big_g) if on_even_grads else None
    dx, d_g0 = in_proj_bwd_x(x2, norm_g[0:1], wie_t, dp_even, dh1, "in_proj_even_bwd", after=token, w_t=True)
    even_small = [d_g0, d_mu, d_w0, d_a0, d_kk, d_ka, d_rk, d_lng, d_lnb, d_att_bias]
    if on_small_grads:
        on_small_grads("even", even_small + [loss_part[0:1, :]])
    rep_g = [jnp.concatenate([d_g0, d_g1], axis=0)] + even_small[1:] + odd_small[:3]
    return loss_part[0, 0], dx, big_g, rep_g
```

```python
import functools
import math

import jax
import jax.numpy as jnp
import numpy as np
from jax import lax
from jax.experimental import pallas as pl
from jax.experimental.pallas import tpu as pltpu

F32 = jnp.float32
BF16 = jnp.bfloat16
HI = lax.Precision.HIGHEST

D = 1024
SEQ = 2048
NSEQ = 2
T = NSEQ * SEQ
HD = 64
NH = 8
W = 512
SHIFT = 1664
LORA = 64
EVEN_IN = 4224
ODD_IN = 3072
L = 64
NC = SEQ // L
LEFT = 8
BAND = (LEFT + 1) * L
CLIP = 128
SGC = 128
NG = 8
RMS_EPS = 1e-6
LN_EPS = 1e-5
GN_EPS = 64e-5
NEG = -1e30
VMEM_BIG = 56 * 1024 * 1024

ADAM_LR = 0.001
ADAM_B1 = 0.9
ADAM_B2 = 0.999
ADAM_EPS = 1e-08
ADAM_WD = 0.01
ADAM_STEP = 10

MESH = pl.DeviceIdType.MESH


def _bdot(a, b):
    return jnp.dot(a.astype(BF16), b.astype(BF16), preferred_element_type=F32)


def _bdot_nt(a, b):
    return lax.dot_general(a.astype(BF16), b.astype(BF16), (((1,), (1,)), ((), ())), preferred_element_type=F32)


def _bdot_tn(a, b):
    return lax.dot_general(a.astype(BF16), b.astype(BF16), (((0,), (0,)), ((), ())), preferred_element_type=F32)


def _hdot(a, b):
    return jnp.dot(a, b, precision=HI, preferred_element_type=F32)


def _hdot_nt(a, b):
    return lax.dot_general(a, b, (((1,), (1,)), ((), ())), precision=HI, preferred_element_type=F32)


def _hdot_tn(a, b):
    return lax.dot_general(a, b, (((0,), (0,)), ((), ())), precision=HI, preferred_element_type=F32)


def _iota2(shape, dim):
    return lax.broadcasted_iota(jnp.int32, shape, dim)


def _head_blockdiag():
    r = _iota2((2 * HD, 2 * HD), 0) // HD
    c = _iota2((2 * HD, 2 * HD), 1) // HD
    return (r == c).astype(BF16)


def _headsum_impl(x, bd):
    hi = x.astype(BF16)
    mid = (x - hi.astype(F32)).astype(BF16)
    n = bd.shape[0]
    out = [jnp.dot(hi[:, i:i + n], bd, preferred_element_type=F32) + jnp.dot(mid[:, i:i + n], bd, preferred_element_type=F32)
           for i in range(0, x.shape[1], n)]
    return jnp.concatenate(out, axis=-1)


@jax.custom_vjp
def _headsum(x, bd):
    return _headsum_impl(x, bd)


def _headsum_fwd(x, bd):
    return _headsum_impl(x, bd), bd


def _headsum_bwd(bd, ct):
    return _headsum_impl(ct, bd), None


_headsum.defvjp(_headsum_fwd, _headsum_bwd)


def _silu(x):
    return x * jax.nn.sigmoid(x)


def _dsilu(x):
    s = jax.nn.sigmoid(x)
    return s * (1.0 + x * (1.0 - s))


_GELU_C = math.sqrt(2.0 / math.pi)


def _gelu(x):
    return 0.5 * x * (1.0 + jnp.tanh(_GELU_C * (x + 0.044715 * (x * x * x))))


def _dgelu(x):
    t = jnp.tanh(_GELU_C * (x + 0.044715 * (x * x * x)))
    return 0.5 * (1.0 + t) + 0.5 * x * (1.0 - t * t) * _GELU_C * (1.0 + 3.0 * 0.044715 * x * x)


def _silu_both(x):
    s = jax.nn.sigmoid(x)
    xs = x * s
    return xs, s + xs * (1.0 - s)


def _gelu_both(x):
    x2 = x * x
    t = jnp.tanh(_GELU_C * (x + 0.044715 * (x2 * x)))
    half = 0.5 * (1.0 + t)
    return x * half, half + 0.5 * x * (1.0 - t * t) * _GELU_C * (1.0 + 3.0 * 0.044715 * x2)


def _softplus(x):
    return jnp.maximum(x, 0.0) + jnp.log(1.0 + jnp.exp(-jnp.abs(x)))


def _cparams(sem, vmem=None):
    return pltpu.CompilerParams(dimension_semantics=sem, vmem_limit_bytes=vmem)


def _row_spec(tm, width):
    return pl.BlockSpec((tm, width), lambda i: (i, 0))


def _col_spec(height, tm):
    return pl.BlockSpec((height, tm), lambda i: (0, i))


def _const_spec(shape):
    nd = len(shape)
    return pl.BlockSpec(shape, lambda *_: (0,) * nd)


def _weight_dims(w_bf, w_t):
    return (((1,), (1,)), ((), ())) if w_t else (((1,), (0,)), ((), ())), w_bf.shape[0 if w_t else 1]


def ln_in_proj(x, g, w_bf, splits, name, after=None, w_t=False, bf16_pieces=()):
    dims, n = _weight_dims(w_bf, w_t)
    dtypes = [BF16 if i in bf16_pieces else F32 for i in range(len(splits))]
    tm = 512 if n <= ODD_IN else 256
    spans = []
    o = 0
    for s in splits:
        spans.append((o, o + s))
        o += s
    assert o == n
    extra_specs, extra = _after_operand(after)

    def body(x_ref, g_ref, w_ref, *rest):
        xn_ref, outs = rest[len(extra)], rest[len(extra) + 1:]
        xv = x_ref[...]
        rstd = lax.rsqrt(jnp.mean(xv * xv, axis=-1, keepdims=True) + RMS_EPS)
        xn = (xv * rstd * g_ref[...]).astype(BF16)
        xn_ref[...] = xn.T
        p = lax.dot_general(xn, w_ref[...], dims, preferred_element_type=F32)
        for o_ref, (a, b) in zip(outs, spans):
            o_ref[...] = p[:, a:b].astype(o_ref.dtype)

    return pl.pallas_call(
        body, grid=(T // tm,), name=name,
        in_specs=[_row_spec(tm, D), _const_spec((1, D)), _const_spec(w_bf.shape)] + extra_specs,
        out_specs=[_col_spec(D, tm)] + [_row_spec(tm, s) for s in splits],
        out_shape=[jax.ShapeDtypeStruct((D, T), BF16)]
        + [jax.ShapeDtypeStruct((T, s), dt) for s, dt in zip(splits, dtypes)],
        compiler_params=_cparams(("parallel",), VMEM_BIG),
    )(x, g, w_bf, *extra)


def in_proj_bwd_x(x, g, w_bf, dps, dres, name, after=None, w_t=False):
    tm = 512
    back = (((1,), (0,)), ((), ())) if w_t else (((1,), (1,)), ((), ()))
    widths = [d.shape[1] for d in dps]
    extra_specs, extra = _after_operand(after)

    def body(x_ref, g_ref, w_ref, dres_ref, *rest):
        dp_refs = rest[:len(widths)]
        dx_ref, dg_ref = rest[-2:]
        dp = jnp.concatenate([r[...] for r in dp_refs], axis=-1)
        dxn = lax.dot_general(dp, w_ref[...], back, preferred_element_type=F32)
        xv = x_ref[...]
        rstd = lax.rsqrt(jnp.mean(xv * xv, axis=-1, keepdims=True) + RMS_EPS)
        xhat = xv * rstd
        dgp = jnp.sum(dxn * xhat, axis=0, keepdims=True)

        @pl.when(pl.program_id(0) == 0)
        def _():
            dg_ref[...] = jnp.zeros_like(dg_ref)

        dg_ref[...] += dgp
        dxh = dxn * g_ref[...]
        dx_ref[...] = dres_ref[...] + rstd * (dxh - xhat * jnp.mean(dxh * xhat, axis=-1, keepdims=True))

    return pl.pallas_call(
        body, grid=(T // tm,), name=name,
        in_specs=[_row_spec(tm, D), _const_spec((1, D)), _const_spec(w_bf.shape), _row_spec(tm, D)]
        + [_row_spec(tm, s) for s in widths] + extra_specs,
        out_specs=[_row_spec(tm, D), _const_spec((1, D))],
        out_shape=[jax.ShapeDtypeStruct((T, D), F32), jax.ShapeDtypeStruct((1, D), F32)],
        compiler_params=_cparams(("arbitrary",), VMEM_BIG),
    )(x, g, w_bf, dres, *dps, *extra)


def _after_operand(after):
    return ([ANY], [after]) if after is not None else ([], [])


def matmul_acc_chips(at_bf, pieces, name, after=None, add_cores=False):
    k = at_bf.shape[0]
    widths = [p.shape[1] for p in pieces]
    nb = sum(widths) // NCHIP
    tm = 512
    steps = T // tm
    half = k // 2
    extra_specs, extra = _after_operand(after)

    def body(a_ref, *rest):
        o_ref, acc = rest[len(widths) + len(extra):][:2]

        @pl.when(pl.program_id(0) == 0)
        def _():
            acc[...] = jnp.zeros_like(acc)

        a = a_ref[...]
        b = jnp.concatenate([r[...] for r in rest[:len(widths)]], axis=-1)
        for s in range(NCHIP):
            acc[s] += jnp.dot(a, b[:, s * nb:(s + 1) * nb], preferred_element_type=F32)

        @pl.when(pl.program_id(0) == steps - 1)
        def _():
            if not add_cores:
                o_ref[...] = acc[...].astype(BF16)
            else:
                give, got, send, recv = rest[-4:]
                x, y, c = lax.axis_index("x"), lax.axis_index("y"), lax.axis_index("c")
                theirs = pl.multiple_of((1 - c) * half, half)
                mine = pl.multiple_of(c * half, half)
                give[...] = acc[:, pl.ds(theirs, half), :].astype(BF16)
                cp = pltpu.make_async_remote_copy(src_ref=give, dst_ref=got, send_sem=send, recv_sem=recv,
                                                  device_id=(x, y, 1 - c), device_id_type=MESH)
                cp.start()
                cp.wait()
                o_ref[...] = (acc[:, pl.ds(mine, half), :] + got[...].astype(F32)).astype(BF16)

    out_rows = half if add_cores else k
    exchange = [pltpu.VMEM((NCHIP, half, nb), BF16)] * 2 + [pltpu.SemaphoreType.DMA] * 2 if add_cores else []
    return pl.pallas_call(
        body, grid=(steps,), name=name,
        in_specs=[_col_spec(k, tm)] + [_row_spec(tm, w_) for w_ in widths] + extra_specs,
        out_specs=_const_spec((NCHIP, out_rows, nb)),
        out_shape=jax.ShapeDtypeStruct((NCHIP, out_rows, nb), BF16),
        scratch_shapes=[pltpu.VMEM((NCHIP, k, nb), F32)] + exchange,
        compiler_params=_cparams(("arbitrary",), VMEM_BIG),
    )(at_bf, *pieces, *extra)


def _out_proj_back(dh_ref, zt_ref, w_ref, dw_ref):
    dhb = dh_ref[...].astype(BF16)

    @pl.when(pl.program_id(0) == 0)
    def _():
        dw_ref[...] = jnp.zeros_like(dw_ref)

    dw_ref[...] += jnp.dot(zt_ref[...], dhb, preferred_element_type=F32)
    return lax.dot_general(dhb, w_ref[...], (((1,), (1,)), ((), ())), preferred_element_type=F32)


PREP_TM = 512
PREP_NB = SEQ // PREP_TM


def _prep_elem(k, wl, apre, kkw, kaw, bd):
    wraw = -_softplus(-wl) - 0.5
    lw = -jnp.exp(wraw)
    asig = jax.nn.sigmoid(apre)
    kkr = k * kkw
    nrm = jnp.maximum(jnp.sqrt(_headsum(kkr * kkr, bd)), 1e-12)
    kk = kkr / nrm
    k2 = k * (1.0 + (asig - 1.0) * kaw)
    return lw, k2, -kk, kk * asig


def _prep_elem_bwd(k, wl, apre, kkw, kaw, bd, dlw, dk2, daa, dbb):
    s = -wl
    sp = _softplus(s)
    dwl = dlw * (-jnp.exp(-sp - 0.5)) * jnp.exp(s - sp)
    asig = jax.nn.sigmoid(apre)
    kkr = k * kkw
    root = jnp.sqrt(_headsum(kkr * kkr, bd))
    inv = 1.0 / jnp.maximum(root, 1e-12)
    kk = kkr * inv
    dkk = dbb * asig - daa
    dap = (dbb * kk + dk2 * k * kaw) * asig * (1.0 - asig)
    through_norm = jnp.where(root > 1e-12, kk * _headsum(dkk * kkr, bd) * inv, 0.0)
    dkkr = inv * (dkk - through_norm)
    gain = 1.0 + (asig - 1.0) * kaw
    dk = dkkr * kkw + dk2 * gain
    dkkw = jnp.sum(dkkr * k, axis=0, keepdims=True)
    dkaw = jnp.sum(dk2 * k * (asig - 1.0), axis=0, keepdims=True)
    return dk, dwl, dap, dkkw, dkaw


def _shifted(ps_ref, prev_ref, mu, blk):
    p = ps_ref[...]
    first = (blk % PREP_NB) == 0
    prev_row = jnp.where(first, 0.0, prev_ref[7:8, :])
    rolled = pltpu.roll(p, 1, 0)
    p_prev = jnp.where(_iota2(p.shape, 0) == 0, prev_row, rolled)
    return p, p_prev, p + (p_prev - p) * mu


def _prev_spec(width, blk_of):
    return pl.BlockSpec((8, width), lambda i: (jnp.maximum(blk_of(i) * (PREP_TM // 8) - 1, 0), 0))


def even_prep(ps, mu, w0, w2x, a0, a2x, kkw, kaw):
    tm = PREP_TM

    def body(ps_ref, prev_ref, mu_ref, w0_ref, w2_ref, a0_ref, a2_ref, kk_ref, ka_ref,
             r_ref, lw_ref, k2_ref, v_ref, aa_ref, bb_ref):
        _, _, s = _shifted(ps_ref, prev_ref, mu_ref[...], pl.program_id(0))
        wa = s[:, 3 * W:]
        wl = w0_ref[...] + _bdot(jnp.tanh(wa), w2_ref[...])
        apre = a0_ref[...] + _bdot(wa, a2_ref[...])
        lw, k2, aa, bb = _prep_elem(s[:, W:2 * W], wl, apre, kk_ref[...], ka_ref[...], _head_blockdiag())
        r_ref[...] = s[:, 0:W]
        v_ref[...] = s[:, 2 * W:3 * W]
        lw_ref[...] = lw
        k2_ref[...] = k2
        aa_ref[...] = aa
        bb_ref[...] = bb

    vec = _const_spec((1, W))
    return pl.pallas_call(
        body, grid=(T // tm,), name="even_prep",
        in_specs=[_row_spec(tm, SHIFT), _prev_spec(SHIFT, lambda i: i), _const_spec((1, SHIFT)), vec,
                  _const_spec((2 * LORA, W)), vec, _const_spec((2 * LORA, W)), vec, vec],
        out_specs=[_row_spec(tm, W)] * 6,
        out_shape=[jax.ShapeDtypeStruct((T, W), F32)] * 6,
        compiler_params=_cparams(("parallel",), VMEM_BIG),
    )(ps, ps, mu, w0, w2x, a0, a2x, kkw, kaw)


def even_prep_bwd(ps, mu, w0, w2x, a0, a2x, kkw, kaw, dr, dlw, dk2, dv, daa, dbb, dr2, dk22, dv2):
    tm = PREP_TM
    nb = T // tm
    rev = lambda i: nb - 1 - i

    def body(ps_ref, prev_ref, mu_ref, w0_ref, w2_ref, a0_ref, a2_ref, kk_ref, ka_ref,
             dr_ref, dlw_ref, dk2_ref, dv_ref, daa_ref, dbb_ref, dr2_ref, dk22_ref, dv2_ref,
             dps_ref, dmu_ref, dw0_ref, dw2_ref, da0_ref, da2_ref, dkk_ref, dka_ref, carry):
        i = pl.program_id(0)
        blk = rev(i)
        mu_v = mu_ref[...]
        p, p_prev, s = _shifted(ps_ref, prev_ref, mu_v, blk)
        wa = s[:, 3 * W:]
        th = jnp.tanh(wa)
        wl = w0_ref[...] + _bdot(th, w2_ref[...])
        apre = a0_ref[...] + _bdot(wa, a2_ref[...])
        bd = _head_blockdiag()
        k = s[:, W:2 * W]
        dk, dwl, dap, dkkw, dkaw = _prep_elem_bwd(k, wl, apre, kk_ref[...], ka_ref[...], bd, dlw_ref[...],
                                                  dk2_ref[...] + dk22_ref[...], daa_ref[...], dbb_ref[...])
        dwa = _bdot_nt(dwl, w2_ref[...]) * (1.0 - th * th) + _bdot_nt(dap, a2_ref[...])
        ds = jnp.concatenate([dr_ref[...] + dr2_ref[...], dk, dv_ref[...] + dv2_ref[...], dwa], axis=-1)

        @pl.when(i == 0)
        def _():
            for ref in (dmu_ref, dw0_ref, dw2_ref, da0_ref, da2_ref, dkk_ref, dka_ref, carry):
                ref[...] = jnp.zeros_like(ref)

        dmu_ref[...] += jnp.sum(ds * (p_prev - p), axis=0, keepdims=True)
        dw0_ref[...] += jnp.sum(dwl, axis=0, keepdims=True)
        da0_ref[...] += jnp.sum(dap, axis=0, keepdims=True)
        dw2_ref[...] += _bdot_tn(th, dwl)
        da2_ref[...] += _bdot_tn(wa, dap)
        dkk_ref[...] += dkkw
        dka_ref[...] += dkaw
        dsm = ds * mu_v
        last = (blk % PREP_NB) == PREP_NB - 1
        nxt = jnp.where(last, 0.0, carry[0:1, :])
        up = pltpu.roll(dsm, tm - 1, 0)
        up = jnp.where(_iota2(up.shape, 0) == tm - 1, nxt, up)
        dps_ref[...] = (ds - dsm + up).astype(BF16)
        carry[0:1, :] = dsm[0:1, :]

    vec = _const_spec((1, W))
    rrow = lambda width: pl.BlockSpec((tm, width), lambda i: (rev(i), 0))
    return pl.pallas_call(
        body, grid=(nb,), name="even_prep_bwd",
        in_specs=[rrow(SHIFT), _prev_spec(SHIFT, rev), _const_spec((1, SHIFT)), vec,
                  _const_spec((2 * LORA, W)), vec, _const_spec((2 * LORA, W)), vec, vec] + [rrow(W)] * 9,
        out_specs=[rrow(SHIFT), _const_spec((1, SHIFT)), vec, _const_spec((2 * LORA, W)), vec,
                   _const_spec((2 * LORA, W)), vec, vec],
        out_shape=[jax.ShapeDtypeStruct((T, SHIFT), BF16), jax.ShapeDtypeStruct((1, SHIFT), F32),
                   jax.ShapeDtypeStruct((1, W), F32), jax.ShapeDtypeStruct((2 * LORA, W), F32),
                   jax.ShapeDtypeStruct((1, W), F32), jax.ShapeDtypeStruct((2 * LORA, W), F32),
                   jax.ShapeDtypeStruct((1, W), F32), jax.ShapeDtypeStruct((1, W), F32)],
        scratch_shapes=[pltpu.VMEM((8, SHIFT), F32)],
        compiler_params=_cparams(("arbitrary",), VMEM_BIG),
    )(ps, ps, mu, w0, w2x, a0, a2x, kkw, kaw, dr, dlw, dk2, dv, daa, dbb, dr2, dk22, dv2)


NPAIR = NH // 2
PW = 2 * HD


def _pair_cols(p):
    return slice(p * PW, (p + 1) * PW)


def _pairs(a):
    return [a[:, _pair_cols(p)] for p in range(NPAIR)]


def _stack_pair(a):
    first = _iota2(a.shape, 1) < HD
    zero = jnp.zeros_like(a)
    return jnp.concatenate([jnp.where(first, a, zero), jnp.where(first, zero, a)], axis=0)


def _unstack_pair(a):
    n = a.shape[0] // 2
    return jnp.where(_iota2((n, PW), 1) < HD, a[:n], a[n:])


def _fold_pair(a):
    n = a.shape[0] // 2
    return a[:n] + a[n:]


def _chunk_masks():
    n = 4 * L
    row = _iota2((n, n), 0)
    col = _iota2((n, n), 1)
    same = ((row // L) & 1) == ((col // L) & 1)
    ri = row & (L - 1)
    ci = col & (L - 1)
    keep = same & (((row < 2 * L) & (ri > ci)) | ((row >= 2 * L) & (ri >= ci)))
    r1 = _iota2((L, L), 0)
    c1 = _iota2((L, L), 1)
    r2 = _iota2((2 * L, 2 * L), 0)
    c2 = _iota2((2 * L, 2 * L), 1)
    return keep.astype(F32), (r1 >= c1).astype(F32), (r2 == c2).astype(F32)


def _scaled(r, lw, k2, aa, bb, tri):
    g = _hdot(tri, lw)
    eg = jnp.exp(g)
    eng = jnp.exp(-g)
    egp = jnp.exp(g - lw)
    return eg, eng, egp, aa * egp, r * eg, bb * eng, k2 * eng


def _head_cols(h):
    return slice(h * HD, (h + 1) * HD)


def _per_head(a):
    return [a[:, _head_cols(h)] for h in range(NH)]


def _pairs_operands(at, rt, bt, kt):
    x = [jnp.concatenate([_stack_pair(a), _stack_pair(r)], axis=0).astype(BF16) for a, r in zip(_pairs(at), _pairs(rt))]
    yk = [jnp.concatenate([_stack_pair(b), _stack_pair(k)], axis=0).astype(BF16) for b, k in zip(_pairs(bt), _pairs(kt))]
    return x, yk


def _pairs_matrices(x, yk, keep, eye):
    m = [_bdot_nt(a, b) * keep for a, b in zip(x, yk)]
    p = [a[:2 * L, :2 * L] for a in m]
    tinv = [eye + a for a in p]
    for _ in range(5):
        p = [_bdot(a, a) for a in p]
        tinv = [t + _bdot(t, a) for t, a in zip(tinv, p)]
    return [a.astype(BF16) for a in m], [a.astype(BF16) for a in tinv]


def _pairs_fwd(x, yk, m, tinv, vw, s0, egl):
    xh = [_bdot_nt(a, s) for a, s in zip(x, s0)]
    u = [_bdot(t, h[:2 * L] + _bdot(a[:2 * L, 2 * L:], w)) for t, h, a, w in zip(tinv, xh, m, vw)]
    uv = [jnp.concatenate([a, w], axis=0).astype(BF16) for a, w in zip(u, vw)]
    y = [h[2 * L:] + _bdot(a[2 * L:], w) for h, a, w in zip(xh, m, uv)]
    sn = [e * (s + _bdot_tn(w, b)) for e, s, w, b in zip(egl, s0, uv, yk)]
    return y, sn, uv


def _pairs_bwd(x, yk, m, tinv, uv, s0, sn, egl, dyw, dsn, keep):
    dzs = [d * e for d, e in zip(dsn, egl)]
    dgl = [jnp.sum(d * s, axis=0, keepdims=True) for d, s in zip(dsn, sn)]
    dyb = [a.astype(BF16) for a in dyw]
    t1 = [_bdot_tn(a[2 * L:], d) for a, d in zip(m, dyb)]
    t2 = [_bdot_nt(b, d) for b, d in zip(yk, dzs)]
    drhs = [_bdot_tn(t, a[:2 * L] + b[:2 * L]) for t, a, b in zip(tinv, t1, t2)]
    dv = [a[2 * L:] + b[2 * L:] + _bdot_tn(c[:2 * L, 2 * L:], d) for a, b, c, d in zip(t1, t2, m, drhs)]
    gg = [jnp.concatenate([a, b], axis=0).astype(BF16) for a, b in zip(drhs, dyw)]
    ds0 = [d + _bdot_tn(g, a) for d, g, a in zip(dzs, gg, x)]
    dm = [_bdot_nt(g, w) * keep for g, w in zip(gg, uv)]
    dx = [_bdot(g, s) + _bdot(d, b) for g, s, d, b in zip(gg, s0, dm, yk)]
    dyk = [_bdot_tn(d, a) + _bdot(w, z) for d, a, w, z in zip(dm, x, uv, dzs)]
    return dx, dyk, dv, dgl, ds0


STATE_SHAPE = (NPAIR * PW, PW)
M_SHAPE = (4 * L, NPAIR * 4 * L)
TINV_SHAPE = (2 * L, NPAIR * 2 * L)


def _rows_of(a, n):
    return [a[i * n:(i + 1) * n, :] for i in range(NPAIR)]


def _both(f):
    out = []
    for s in range(NSEQ):
        out += f(s)
    return out


def _seq_view(a):
    return a.reshape(NSEQ, SEQ, a.shape[-1])


UV_SHAPE = (4 * L, NPAIR * PW)
RW_CHUNKS = 2


def rwkv_fwd(r, lw, k2, v, aa, bb):
    def body(r_ref, lw_ref, k2_ref, v_ref, aa_ref, bb_ref, y_ref, hs_ref, hn_ref, m_ref, t_ref, uv_ref, state):
        @pl.when(pl.program_id(0) == 0)
        def _():
            state[...] = jnp.zeros_like(state)

        keep, tri, eye = _chunk_masks()
        where = [(j, s) for j in range(RW_CHUNKS) for s in range(NSEQ)]
        rows = lambda j: slice(j * L, (j + 1) * L)
        sc = [_scaled(r_ref[s, rows(j)], lw_ref[s, rows(j)], k2_ref[s, rows(j)], aa_ref[s, rows(j)],
                      bb_ref[s, rows(j)], tri) for j, s in where]
        ops = [_pairs_operands(*a[3:]) for a in sc]
        m, tinv = _pairs_matrices([a for o in ops for a in o[0]], [a for o in ops for a in o[1]], keep, eye)
        s_cur = [state[s] for s in range(NSEQ)]
        for j in range(RW_CHUNKS):
            mine = slice(j * NSEQ * NPAIR, (j + 1) * NSEQ * NPAIR)
            x = [a for o in ops[j * NSEQ:(j + 1) * NSEQ] for a in o[0]]
            yk = [a for o in ops[j * NSEQ:(j + 1) * NSEQ] for a in o[1]]
            vw = _both(lambda s: [_stack_pair(a) for a in _pairs(v_ref[s, rows(j)])])
            egl = _both(lambda s: _pairs(sc[j * NSEQ + s][0][L - 1:L, :]))
            y, sn, uv = _pairs_fwd(x, yk, m[mine], tinv[mine], vw, _both(lambda s: _rows_of(s_cur[s], PW)), egl)
            for s in range(NSEQ):
                ps = slice(s * NPAIR, (s + 1) * NPAIR)
                hs_ref[j, s] = s_cur[s]
                y_ref[s, rows(j)] = jnp.concatenate([_fold_pair(a) for a in y[ps]], axis=-1)
                m_ref[j, s] = jnp.concatenate(m[mine][ps], axis=-1)
                t_ref[j, s] = jnp.concatenate(tinv[mine][ps], axis=-1)
                uv_ref[j, s] = jnp.concatenate(uv[ps], axis=-1)
                s_cur[s] = jnp.concatenate(sn[ps], axis=0)
                hn_ref[j, s] = s_cur[s]
        for s in range(NSEQ):
            state[s] = s_cur[s]

    blk = pl.BlockSpec((NSEQ, RW_CHUNKS * L, W), lambda c: (0, c, 0))
    per_chunk = lambda shape: pl.BlockSpec((RW_CHUNKS, NSEQ) + shape, lambda c: (c, 0, 0, 0))
    saved_shapes = [(STATE_SHAPE, F32), (STATE_SHAPE, F32), (M_SHAPE, BF16), (TINV_SHAPE, BF16), (UV_SHAPE, BF16)]
    y, *saved = pl.pallas_call(
        body, grid=(NC // RW_CHUNKS,), name="rwkv_fwd",
        in_specs=[blk] * 6,
        out_specs=[blk] + [per_chunk(shape) for shape, _ in saved_shapes],
        out_shape=[jax.ShapeDtypeStruct((NSEQ, SEQ, W), F32)]
        + [jax.ShapeDtypeStruct((NC, NSEQ) + shape, dt) for shape, dt in saved_shapes],
        scratch_shapes=[pltpu.VMEM((NSEQ,) + STATE_SHAPE, F32)],
        compiler_params=_cparams(("arbitrary",), VMEM_BIG),
    )(*[_seq_view(a) for a in (r, lw, k2, v, aa, bb)])
    return y.reshape(T, W), saved


def rwkv_bwd(r, lw, k2, aa, bb, saved, dy):
    def body(r_ref, lw_ref, k2_ref, aa_ref, bb_ref, hs_ref, hn_ref, m_ref, t_ref, uv_ref, dy_ref,
             dr_ref, dlw_ref, dk2_ref, dv_ref, daa_ref, dbb_ref, dstate):
        @pl.when(pl.program_id(0) == 0)
        def _():
            dstate[...] = jnp.zeros_like(dstate)

        keep, tri, _ = _chunk_masks()
        sc = [_scaled(r_ref[s], lw_ref[s], k2_ref[s], aa_ref[s], bb_ref[s], tri) for s in range(NSEQ)]
        ops = [_pairs_operands(*sc[s][3:]) for s in range(NSEQ)]
        x, yk = _both(lambda s: ops[s][0]), _both(lambda s: ops[s][1])
        m = _both(lambda s: [m_ref[0, s][:, i * 4 * L:(i + 1) * 4 * L] for i in range(NPAIR)])
        tinv = _both(lambda s: [t_ref[0, s][:, i * 2 * L:(i + 1) * 2 * L] for i in range(NPAIR)])
        uv = _both(lambda s: _pairs(uv_ref[0, s]))
        dyw = _both(lambda s: [_stack_pair(a) for a in _pairs(dy_ref[s])])
        s0 = _both(lambda s: _rows_of(hs_ref[0, s], PW))
        sn = _both(lambda s: _rows_of(hn_ref[0, s], PW))
        dsn = _both(lambda s: _rows_of(dstate[s], PW))
        egl = _both(lambda s: _pairs(sc[s][0][L - 1:L, :]))
        dx, dyk, dvw, dgl, ds0 = _pairs_bwd(x, yk, m, tinv, uv, s0, sn, egl, dyw, dsn, keep)
        for s in range(NSEQ):
            mine = slice(s * NPAIR, (s + 1) * NPAIR)
            eg, eng, egp, at, rt, bt, kt = sc[s]
            dstate[s] = jnp.concatenate(ds0[mine], axis=0)
            dv_ref[s] = jnp.concatenate([_fold_pair(a) for a in dvw[mine]], axis=-1)
            dat = jnp.concatenate([_fold_pair(a[:2 * L]) for a in dx[mine]], axis=-1)
            drt = jnp.concatenate([_fold_pair(a[2 * L:]) for a in dx[mine]], axis=-1)
            dbt = jnp.concatenate([_fold_pair(a[:2 * L]) for a in dyk[mine]], axis=-1)
            dkt = jnp.concatenate([_fold_pair(a[2 * L:]) for a in dyk[mine]], axis=-1)
            dg = drt * rt - dbt * bt - dkt * kt
            dg = dg + jnp.where(_iota2(dg.shape, 0) == L - 1, jnp.concatenate(dgl[mine], axis=-1), 0.0)
            dgp = dat * at
            dlw_ref[s] = _hdot_tn(tri, dg + dgp) - dgp
            dr_ref[s] = drt * eg
            daa_ref[s] = dat * egp
            dbb_ref[s] = dbt * eng
            dk2_ref[s] = dkt * eng

    blk = pl.BlockSpec((NSEQ, L, W), lambda c: (0, NC - 1 - c, 0))
    per_chunk = lambda shape: pl.BlockSpec((1, NSEQ) + shape, lambda c: (NC - 1 - c, 0, 0, 0))
    outs = pl.pallas_call(
        body, grid=(NC,), name="rwkv_bwd",
        in_specs=[blk] * 5 + [per_chunk(a.shape[2:]) for a in saved] + [blk],
        out_specs=[blk] * 6,
        out_shape=[jax.ShapeDtypeStruct((NSEQ, SEQ, W), F32)] * 6,
        scratch_shapes=[pltpu.VMEM((NSEQ,) + STATE_SHAPE, F32)],
        compiler_params=_cparams(("arbitrary",)),
    )(*[_seq_view(a) for a in (r, lw, k2, aa, bb)], *saved, _seq_view(dy))
    return [a.reshape(T, W) for a in outs]


def _post_math(y, r, k2, v, ga, o, gb, lng, lnb, rk, bd):
    mu = _headsum(y, bd) * (1.0 / HD)
    yc = y - mu
    var = _headsum(yc * yc, bd) * (1.0 / HD)
    yn = yc * lax.rsqrt(var + GN_EPS) * lng + lnb
    bonus = _headsum(r * k2 * rk, bd) * v
    return (yn + bonus) * _silu(ga), o * _silu(gb)


def even_post(y, r, k2, v, ga, o, gb, lng, lnb, rk, h, w_bf):
    tm = 512

    def body(y_ref, r_ref, k2_ref, v_ref, ga_ref, o_ref, gb_ref, lng_ref, lnb_ref, rk_ref, h_ref, w_ref,
             ho_ref, zt_ref):
        ya, yb = _post_math(y_ref[...], r_ref[...], k2_ref[...], v_ref[...], ga_ref[...], o_ref[...], gb_ref[...],
                            lng_ref[...], lnb_ref[...], rk_ref[...], _head_blockdiag())
        z = jnp.concatenate([ya.astype(BF16), yb.astype(BF16)], axis=-1)
        zt_ref[...] = z.T
        ho_ref[...] = h_ref[...] + jnp.dot(z, w_ref[...], preferred_element_type=F32)

    vec = _const_spec((1, W))
    return pl.pallas_call(
        body, grid=(T // tm,), name="even_post",
        in_specs=[_row_spec(tm, W)] * 7 + [vec] * 3 + [_row_spec(tm, D), _const_spec((D, D))],
        out_specs=[_row_spec(tm, D), _col_spec(D, tm)],
        out_shape=[jax.ShapeDtypeStruct((T, D), F32), jax.ShapeDtypeStruct((D, T), BF16)],
        compiler_params=_cparams(("parallel",), VMEM_BIG),
    )(y, r, k2, v, ga, o, gb, lng, lnb, rk, h, w_bf)


def even_post_bwd(y, r, k2, v, ga, o, gb, lng, lnb, rk, dh, zt_bf, w_bf, after=None):
    tm = 512
    extra_specs, extra = _after_operand(after)

    def body(y_ref, r_ref, k2_ref, v_ref, ga_ref, o_ref, gb_ref, lng_ref, lnb_ref, rk_ref, dh_ref, zt_ref, w_ref,
             *rest):
        dy_ref, dr_ref, dk2_ref, dv_ref, dga_ref, do_ref, dgb_ref, dlng_ref, dlnb_ref, drk_ref, dw_ref = rest[-11:]
        dzv = _out_proj_back(dh_ref, zt_ref, w_ref, dw_ref)
        bd = _head_blockdiag()
        _, vjp = jax.vjp(lambda *a: _post_math(*a, bd), y_ref[...], r_ref[...], k2_ref[...], v_ref[...], ga_ref[...],
                         o_ref[...], gb_ref[...], lng_ref[...], lnb_ref[...], rk_ref[...])
        dy, dr, dk2, dv, dga, do, dgb, dlng, dlnb, drk = vjp((dzv[:, 0:W], dzv[:, W:2 * W]))
        for ref, val in ((dy_ref, dy), (dr_ref, dr), (dk2_ref, dk2), (dv_ref, dv), (dga_ref, dga), (do_ref, do),
                         (dgb_ref, dgb)):
            ref[...] = val.astype(ref.dtype)

        @pl.when(pl.program_id(0) == 0)
        def _():
            for ref in (dlng_ref, dlnb_ref, drk_ref):
                ref[...] = jnp.zeros_like(ref)

        dlng_ref[...] += dlng
        dlnb_ref[...] += dlnb
        drk_ref[...] += drk

    vec = _const_spec((1, W))
    return pl.pallas_call(
        body, grid=(T // tm,), name="even_post_bwd",
        in_specs=[_row_spec(tm, W)] * 7 + [vec] * 3 + [_row_spec(tm, D), _col_spec(D, tm), _const_spec((D, D))]
        + extra_specs,
        out_specs=[_row_spec(tm, W)] * 7 + [vec] * 3 + [_const_spec((D, D))],
        out_shape=[jax.ShapeDtypeStruct((T, W), dt) for dt in (F32, F32, F32, F32, BF16, F32, BF16)]
        + [jax.ShapeDtypeStruct((1, W), F32)] * 3 + [jax.ShapeDtypeStruct((D, D), F32)],
        compiler_params=_cparams(("arbitrary",), VMEM_BIG),
    )(y, r, k2, v, ga, o, gb, lng, lnb, rk, dh, zt_bf, w_bf, *extra)


PADSEQ = SEQ + LEFT * L
ATT_SCALE = 1.0 / math.sqrt(HD)
ATT_Q = 4
WIN = BAND + (ATT_Q - 1) * L
ATT_STEPS = NC // ATT_Q
ATT_BIAS_SHAPE = (NPAIR, ATT_Q * 2 * L, WIN)
ATT_WINDOW_BIAS_SHAPE = (ATT_Q, NPAIR, 2 * L, WIN)


def _stack_chunks(a):
    return jnp.concatenate([_stack_pair(a[i * L:(i + 1) * L]) for i in range(ATT_Q)], axis=0)


def _unstack_chunks(a):
    return jnp.concatenate([_unstack_pair(a[i * 2 * L:(i + 1) * 2 * L]) for i in range(ATT_Q)], axis=0)


def _window_bias(b_ref):
    return [jnp.concatenate([b_ref[c, p] for c in range(ATT_Q)], axis=0) for p in range(NPAIR)]


def _att_probs(q2, kw, bias, step):
    valid = _iota2((1, WIN), 1) >= (LEFT - step * ATT_Q) * L
    s = [jnp.where(valid, _bdot_nt(a, b) * ATT_SCALE + bias[p], NEG) for p, (a, b) in enumerate(zip(q2, kw))]
    e = [jnp.exp(a - jnp.max(a, axis=-1, keepdims=True)) for a in s]
    return [a / jnp.sum(a, axis=-1, keepdims=True) for a in e]


def attention_fwd(q, kpad, vpad, bias):
    def body(q_ref, k_ref, v_ref, b_ref, o_ref):
        step = pl.program_id(1)
        start = pl.multiple_of(step * (ATT_Q * L), L)
        kw = _pairs(k_ref[pl.ds(start, WIN), :])
        vw = _pairs(v_ref[pl.ds(start, WIN), :])
        q2 = [_stack_chunks(a) for a in _pairs(q_ref[...])]
        p = _att_probs(q2, kw, _window_bias(b_ref), step)
        o_ref[...] = jnp.concatenate([_unstack_chunks(_bdot(a, b)) for a, b in zip(p, vw)], axis=-1)

    qblk = pl.BlockSpec((ATT_Q * L, W), lambda b, c: (b * ATT_STEPS + c, 0))
    kblk = pl.BlockSpec((PADSEQ, W), lambda b, c: (b, 0))
    return pl.pallas_call(
        body, grid=(NSEQ, ATT_STEPS), name="attention_fwd",
        in_specs=[qblk, kblk, kblk, _const_spec(ATT_WINDOW_BIAS_SHAPE)],
        out_specs=qblk, out_shape=jax.ShapeDtypeStruct((T, W), F32),
        compiler_params=_cparams(("parallel", "arbitrary")),
    )(q, kpad, vpad, bias)


def attention_bwd(q, kpad, vpad, bias, do):
    def body(q_ref, k_ref, v_ref, b_ref, do_ref, dq_ref, dko_ref, dvo_ref, db_ref, dk_ref, dv_ref):
        b = pl.program_id(0)
        c = pl.program_id(1)

        @pl.when(c == 0)
        def _():
            dk_ref[...] = jnp.zeros_like(dk_ref)
            dv_ref[...] = jnp.zeros_like(dv_ref)

        @pl.when((c == 0) & (b == 0))
        def _():
            db_ref[...] = jnp.zeros_like(db_ref)

        start = pl.multiple_of(c * (ATT_Q * L), L)
        kw = _pairs(k_ref[pl.ds(start, WIN), :])
        vw = _pairs(v_ref[pl.ds(start, WIN), :])
        q2 = [_stack_chunks(a) for a in _pairs(q_ref[...])]
        do2 = [_stack_chunks(a) for a in _pairs(do_ref[...].astype(BF16))]
        p = _att_probs(q2, kw, _window_bias(b_ref), c)
        dp = [_bdot_nt(a, b) for a, b in zip(do2, vw)]
        ds = [a * (d - jnp.sum(d * a, axis=-1, keepdims=True)) for a, d in zip(p, dp)]
        dss = [(a * ATT_SCALE).astype(BF16) for a in ds]
        dq_ref[...] = jnp.concatenate([_unstack_chunks(_bdot(a, b)) for a, b in zip(dss, kw)], axis=-1).astype(BF16)
        dk_ref[pl.ds(start, WIN), :] += jnp.concatenate([_bdot_tn(a, b) for a, b in zip(dss, q2)], axis=-1)
        dv_ref[pl.ds(start, WIN), :] += jnp.concatenate([_bdot_tn(a, b) for a, b in zip(p, do2)], axis=-1)
        for i in range(NPAIR):
            db_ref[i] += ds[i]

        @pl.when(c == ATT_STEPS - 1)
        def _():
            dko_ref[...] = dk_ref[LEFT * L:, :].astype(BF16)
            dvo_ref[...] = dv_ref[LEFT * L:, :].astype(BF16)

    qblk = pl.BlockSpec((ATT_Q * L, W), lambda b, c: (b * ATT_STEPS + c, 0))
    kblk = pl.BlockSpec((PADSEQ, W), lambda b, c: (b, 0))
    sblk = pl.BlockSpec((SEQ, W), lambda b, c: (b, 0))
    bblk = _const_spec(ATT_BIAS_SHAPE)
    return pl.pallas_call(
        body, grid=(NSEQ, ATT_STEPS), name="attention_bwd",
        in_specs=[qblk, kblk, kblk, _const_spec(ATT_WINDOW_BIAS_SHAPE), qblk],
        out_specs=[qblk, sblk, sblk, bblk],
        out_shape=[jax.ShapeDtypeStruct((T, W), BF16), jax.ShapeDtypeStruct((T, W), BF16),
                   jax.ShapeDtypeStruct((T, W), BF16), jax.ShapeDtypeStruct(ATT_BIAS_SHAPE, F32)],
        scratch_shapes=[pltpu.VMEM((PADSEQ, W), F32), pltpu.VMEM((PADSEQ, W), F32)],
        compiler_params=_cparams(("arbitrary", "arbitrary"), VMEM_BIG),
    )(q, kpad, vpad, bias, do)


NTAB = 2 * CLIP + 1
EXT = BAND + L


def _ext_onehot():
    n = _iota2((EXT, NTAB), 0)
    m = _iota2((EXT, NTAB), 1)
    return (jnp.clip(BAND - 1 - n, -CLIP, CLIP) + CLIP == m).astype(F32)


def bias_expand(table):
    def body(t_ref, o_ref):
        ext = _hdot_nt(t_ref[...], _ext_onehot())
        ext = jnp.concatenate([ext, jnp.zeros((NH, WIN - EXT), F32)], axis=-1)
        col = _iota2((NH, WIN), 1)
        for c in range(ATT_Q):
            inside = (col >= c * L) & (col < c * L + BAND)
            for i in range(L):
                shift = (c * L - (L - 1 - i)) % WIN
                o_ref[c, :, i, :] = jnp.where(inside, pltpu.roll(ext, shift, 1) if shift else ext, NEG)

    out = pl.pallas_call(body, name="bias_expand", out_shape=jax.ShapeDtypeStruct((ATT_Q, NH, L, WIN), F32))(table)
    return out.reshape(ATT_WINDOW_BIAS_SHAPE)


def bias_grad(dbias):
    def body(d_ref, o_ref):
        acc = jnp.zeros((NH, EXT), F32)
        zpad = jnp.zeros((NH, EXT - BAND), F32)
        for i in range(L):
            s = L - 1 - i
            row = jnp.concatenate([d_ref[:, i, :], zpad], axis=-1)
            acc = acc + (pltpu.roll(row, s, 1) if s else row)
        o_ref[...] = _hdot(acc, _ext_onehot())

    return pl.pallas_call(body, name="bias_grad", out_shape=jax.ShapeDtypeStruct((NH, NTAB), F32))(dbias)


def _group_cols(g):
    return slice(g * SGC, (g + 1) * SGC)


def _sg_norm(gv, lng, lnb):
    gc = gv - jnp.mean(gv, axis=-1, keepdims=True)
    rstd = lax.rsqrt(jnp.mean(gc * gc, axis=-1, keepdims=True) + LN_EPS)
    xhat = gc * rstd
    return xhat, rstd, xhat * lng + lnb


GMLP_BWD_CHUNKS = 2


def gmlp_fwd_loss(u, v, gate, lng, lnb, wm_bf, sgb_t, h, w_bf, g_final, target):
    tm = GMLP_BWD_CHUNKS * SGC

    def body(u_ref, v_ref, gt_ref, lng_ref, lnb_ref, wm_ref, sb_ref, h_ref, w_ref, g_ref, t_ref,
             dh_ref, loss_ref, dg_ref, zt_ref):
        zs = []
        for ch in range(GMLP_BWD_CHUNKS):
            rows = slice(ch * SGC, (ch + 1) * SGC)
            _, _, vln = _sg_norm(_gelu(v_ref[rows, :]), lng_ref[...], lnb_ref[...])
            vlb = vln.astype(BF16)
            zg = []
            for g in range(NG):
                cs = _group_cols(g)
                sv = jnp.dot(wm_ref[g], vlb[:, cs], preferred_element_type=F32) + sb_ref[:, g:g + 1]
                zg.append((_gelu(u_ref[rows, cs]) * sv * _silu(gt_ref[rows, cs])).astype(BF16))
            zs.append(jnp.concatenate(zg, axis=-1))
        z = jnp.concatenate(zs, axis=0)
        zt_ref[...] = z.T
        xv = h_ref[...] + jnp.dot(z, w_ref[...], preferred_element_type=F32)
        rstd = lax.rsqrt(jnp.mean(xv * xv, axis=-1, keepdims=True) + RMS_EPS)
        xhat = xv * rstd
        err = xhat * g_ref[...] - t_ref[...]
        part = 0.5 * jnp.sum(jnp.mean(err * err, axis=-1, keepdims=True), axis=0, keepdims=True)
        dout = err * (1.0 / D)

        @pl.when(pl.program_id(0) == 0)
        def _():
            loss_ref[...] = jnp.zeros_like(loss_ref)
            dg_ref[...] = jnp.zeros_like(dg_ref)

        loss_ref[...] += jnp.broadcast_to(part, loss_ref.shape)
        dg_ref[...] += jnp.sum(dout * xhat, axis=0, keepdims=True)
        dxh = dout * g_ref[...]
        dh_ref[...] = rstd * (dxh - xhat * jnp.mean(dxh * xhat, axis=-1, keepdims=True))

    return pl.pallas_call(
        body, grid=(T // tm,), name="gmlp_fwd_loss",
        in_specs=[_row_spec(tm, D)] * 3 + [_const_spec((1, D))] * 2
        + [_const_spec((NG, SGC, SGC)), _const_spec((SGC, NG)), _row_spec(tm, D), _const_spec((D, D)),
           _const_spec((1, D)), _row_spec(tm, D)],
        out_specs=[_row_spec(tm, D), _const_spec((8, 128)), _const_spec((1, D)), _col_spec(D, tm)],
        out_shape=[jax.ShapeDtypeStruct((T, D), F32), jax.ShapeDtypeStruct((8, 128), F32),
                   jax.ShapeDtypeStruct((1, D), F32), jax.ShapeDtypeStruct((D, T), BF16)],
        compiler_params=_cparams(("arbitrary",), VMEM_BIG),
    )(u, v, gate, lng, lnb, wm_bf, sgb_t, h, w_bf, g_final, target)


def gmlp_bwd(u, v, gate, lng, lnb, wm_bf, sgb_t, dh, zt_bf, w_bf):
    def body(u_ref, v_ref, gt_ref, lng_ref, lnb_ref, wm_ref, sb_ref, dh_ref, zt_ref, w_ref,
             du_ref, dv_ref, dgt_ref, dlng_ref, dlnb_ref, dwm_ref, dsb_ref, dw_ref):
        @pl.when(pl.program_id(0) == 0)
        def _():
            for ref in (dlng_ref, dlnb_ref, dwm_ref, dsb_ref):
                ref[...] = jnp.zeros_like(ref)

        dz = _out_proj_back(dh_ref, zt_ref, w_ref, dw_ref)
        sel = (_iota2((D, NG), 0) // SGC == _iota2((D, NG), 1)).astype(F32)
        for ch in range(GMLP_BWD_CHUNKS):
            rows = slice(ch * SGC, (ch + 1) * SGC)
            gv, dgv_dv = _gelu_both(v_ref[rows, :])
            xhat, rstd, vln = _sg_norm(gv, lng_ref[...], lnb_ref[...])
            vlb = vln.astype(BF16)
            dvln = []
            dsv_all = []
            for g in range(NG):
                cs = _group_cols(g)
                uu = u_ref[rows, cs]
                gg = gt_ref[rows, cs]
                dzz = dz[rows, cs]
                sv = jnp.dot(wm_ref[g], vlb[:, cs], preferred_element_type=F32) + sb_ref[:, g:g + 1]
                gu, dgu = _gelu_both(uu)
                sg, dsg = _silu_both(gg)
                dzgu = dzz * gu
                dsv = dzgu * sg
                dgt_ref[rows, cs] = (dzgu * sv * dsg).astype(BF16)
                du_ref[rows, cs] = (dzz * sv * sg * dgu).astype(BF16)
                dsb16 = dsv.astype(BF16)
                dvln.append(lax.dot_general(wm_ref[g], dsb16, (((0,), (0,)), ((), ())), preferred_element_type=F32))
                dwm_ref[g] += lax.dot_general(dsb16, vlb[:, cs], (((1,), (1,)), ((), ())),
                                              preferred_element_type=F32)
                dsv_all.append(dsv)
            dvl = jnp.concatenate(dvln, axis=-1)
            dsb_ref[...] += _hdot(jnp.concatenate(dsv_all, axis=-1), sel)
            dlng_ref[...] += jnp.sum(dvl * xhat, axis=0, keepdims=True)
            dlnb_ref[...] += jnp.sum(dvl, axis=0, keepdims=True)
            dxh = dvl * lng_ref[...]
            dgv = rstd * (dxh - jnp.mean(dxh, axis=-1, keepdims=True)
                          - xhat * jnp.mean(dxh * xhat, axis=-1, keepdims=True))
            dv_ref[rows, :] = (dgv * dgv_dv).astype(BF16)

    tm = GMLP_BWD_CHUNKS * SGC
    return pl.pallas_call(
        body, grid=(T // tm,), name="gmlp_bwd",
        in_specs=[_row_spec(tm, D)] * 3 + [_const_spec((1, D))] * 2
        + [_const_spec((NG, SGC, SGC)), _const_spec((SGC, NG)), _row_spec(tm, D), _col_spec(D, tm),
           _const_spec((D, D))],
        out_specs=[_row_spec(tm, D)] * 3 + [_const_spec((1, D))] * 2
        + [_const_spec((NG, SGC, SGC)), _const_spec((SGC, NG)), _const_spec((D, D))],
        out_shape=[jax.ShapeDtypeStruct((T, D), BF16)] * 3 + [jax.ShapeDtypeStruct((1, D), F32)] * 2
        + [jax.ShapeDtypeStruct((NG, SGC, SGC), F32), jax.ShapeDtypeStruct((SGC, NG), F32),
           jax.ShapeDtypeStruct((D, D), F32)],
        compiler_params=_cparams(("arbitrary",), VMEM_BIG),
    )(u, v, gate, lng, lnb, wm_bf, sgb_t, dh, zt_bf, w_bf)


NCHIP = 4
NDEV = 8
ANY = pl.BlockSpec(memory_space=pl.ANY)


HBM = pl.BlockSpec(memory_space=pltpu.HBM)
SEM = pl.BlockSpec(memory_space=pltpu.SEMAPHORE)
EFFECT = pltpu.SideEffectType.DATAFLOW_SIDE_EFFECTING


def _peers(whole_mesh):
    x, y, c = lax.axis_index("x"), lax.axis_index("y"), lax.axis_index("c")
    if not whole_mesh:
        return [((px, py, c), 2 * px + py) for px, py in ((1 - x, y), (x, 1 - y), (1 - x, 1 - y))], 2 * x + y
    out = []
    for j in range(1, NDEV):
        px, py, pc = x ^ (j >> 2), y ^ ((j >> 1) & 1), c ^ (j & 1)
        out.append(((px, py, pc), 4 * px + 2 * py + pc))
    return out, 4 * x + 2 * y + c


def _send_copies(src, land, send, recv, scatter, whole_mesh, starting):
    peers, me = _peers(whole_mesh)
    copies = []
    for t in range(len(src)):
        for j, (dev, slot) in enumerate(peers):
            k = t * len(peers) + j
            copies.append(pltpu.make_async_remote_copy(
                src_ref=src[t].at[slot] if scatter else src[t], dst_ref=land[t].at[me if starting else slot],
                send_sem=send.at[k], recv_sem=recv.at[k], device_id=dev, device_id_type=MESH))
    return copies


def _own_copies(src, land, sems, scatter, whole_mesh):
    _, me = _peers(whole_mesh)
    return [pltpu.make_async_copy(src[t].at[me] if scatter else src[t], land[t].at[me], sems.at[t])
            for t in range(len(src))]


def send_start(srcs, scatter, whole_mesh, name, after=None):
    n = len(srcs)
    slots = NDEV if whole_mesh else NCHIP
    nsem = n * (slots - 1)
    extra_specs, extra = _after_operand(after)
    lands = [pltpu.HBM(a.shape if scatter else (slots,) + a.shape, a.dtype) for a in srcs]

    def body(*refs):
        first_out = n + len(extra)
        send, recv, own = refs[first_out:first_out + 3]
        src, land = refs[:n], refs[first_out + 3 + n:first_out + 3 + 2 * n]
        for cp in _own_copies(src, land, own, scatter, whole_mesh):
            cp.start()
        for cp in _send_copies(src, land, send, recv, scatter, whole_mesh, True):
            cp.start()
        refs[-1][...] = jnp.zeros_like(refs[-1])

    out = pl.pallas_call(
        body, name=name,
        out_shape=(pltpu.SemaphoreType.DMA((nsem,)), pltpu.SemaphoreType.DMA((nsem,)), pltpu.SemaphoreType.DMA((n,)),
                   *[pltpu.HBM(a.shape, a.dtype) for a in srcs], *lands, jax.ShapeDtypeStruct((8, 128), F32)),
        in_specs=[HBM] * n + extra_specs,
        out_specs=(SEM, SEM, SEM, *[HBM] * (2 * n), pl.BlockSpec(memory_space=pltpu.VMEM)),
        input_output_aliases={i: 3 + i for i in range(n)},
        compiler_params=pltpu.CompilerParams(has_side_effects=EFFECT),
    )(*[pltpu.with_memory_space_constraint(a, pltpu.HBM) for a in srcs], *extra)
    return out[0], out[1], out[2], list(out[3:3 + n]), list(out[3 + n:3 + 2 * n]), out[-1]


def send_wait(started, after, scatter, whole_mesh, name):
    send, recv, own, srcs, lands, _ = started
    n = len(srcs)

    def body(*refs):
        src, land = refs[:n], refs[n:2 * n]
        for cp in _own_copies(src, land, refs[2 * n + 2], scatter, whole_mesh):
            cp.wait()
        for cp in _send_copies(src, land, refs[2 * n], refs[2 * n + 1], scatter, whole_mesh, False):
            cp.wait_send()
            cp.wait_recv()

    arrs = list(srcs) + list(lands)
    out = pl.pallas_call(
        body, name=name, out_shape=tuple(pltpu.HBM(a.shape, a.dtype) for a in arrs),
        in_specs=[HBM] * (2 * n) + [SEM, SEM, SEM, ANY], out_specs=tuple([HBM] * (2 * n)),
        input_output_aliases={i: i for i in range(2 * n)},
        compiler_params=pltpu.CompilerParams(has_side_effects=EFFECT),
    )(*arrs, send, recv, own, after)
    return list(out[n:])


def exchange_c(arrs, name):
    n = len(arrs)

    def body(*refs):
        ins, outs = refs[:n], refs[n:2 * n]
        send, recv = refs[2 * n:]
        sibling = (lax.axis_index("x"), lax.axis_index("y"), 1 - lax.axis_index("c"))
        copies = [pltpu.make_async_remote_copy(src_ref=ins[t], dst_ref=outs[t], send_sem=send.at[t], recv_sem=recv.at[t],
                                               device_id=sibling, device_id_type=MESH) for t in range(n)]
        for cp in copies:
            cp.start()
        for cp in copies:
            cp.wait()

    return pl.pallas_call(
        body, name=name, in_specs=[ANY] * n, out_specs=[ANY] * n,
        out_shape=[jax.ShapeDtypeStruct(a.shape, a.dtype) for a in arrs],
        scratch_shapes=[pltpu.SemaphoreType.DMA((n,)), pltpu.SemaphoreType.DMA((n,))],
    )(*arrs)


def gather_weights(arrs, split):
    n = len(arrs)

    def body(*refs):
        ins, outs = refs[:n], refs[n:2 * n]
        send1, recv1, send2, recv2, loc = refs[2 * n:]
        x, y, c = lax.axis_index("x"), lax.axis_index("y"), lax.axis_index("c")
        me = 2 * x + y
        sibling = (x, y, 1 - c)
        peers = [(1 - x, y), (x, 1 - y), (1 - x, 1 - y)]

        def rows_of(t, core):
            half = arrs[t].shape[0] // 2
            return pl.ds(core * half, half)

        def part(ref, t, core):
            return ref.at[rows_of(t, core)] if split[t] else ref

        local = [pltpu.make_async_copy(ins[t], outs[t].at[me], loc.at[t]) for t in range(n)]
        for cp in local:
            cp.start()
        first = []
        for t in range(n):
            for j, (px, py) in enumerate(peers):
                first.append(pltpu.make_async_remote_copy(
                    src_ref=part(ins[t], t, c), dst_ref=part(outs[t].at[me], t, c), send_sem=send1.at[t, j],
                    recv_sem=recv1.at[t, j], device_id=(px, py, c), device_id_type=MESH))
        for cp in first:
            cp.start()
        passed = []
        for t in range(n):
            for j, (px, py) in enumerate(peers):
                landed = part(outs[t].at[2 * px + py], t, c)
                pltpu.make_async_remote_copy(
                    src_ref=landed, dst_ref=landed, send_sem=send1.at[t, j], recv_sem=recv1.at[t, j],
                    device_id=(x, y, c), device_id_type=MESH).wait_recv()
                if split[t]:
                    cp = pltpu.make_async_remote_copy(
                        src_ref=landed, dst_ref=landed, send_sem=send2.at[t, j], recv_sem=recv2.at[t, j],
                        device_id=sibling, device_id_type=MESH)
                    cp.start()
                    passed.append(cp)
        for t in range(n):
            for j, (px, py) in enumerate(peers):
                if split[t]:
                    other = part(outs[t].at[2 * px + py], t, 1 - c)
                    pltpu.make_async_remote_copy(
                        src_ref=other, dst_ref=other, send_sem=send2.at[t, j], recv_sem=recv2.at[t, j],
                        device_id=(x, y, c), device_id_type=MESH).wait_recv()
        for cp in first + passed:
            cp.wait_send()
        for cp in local:
            cp.wait()

    return pl.pallas_call(
        body, name="gather_weights", in_specs=[ANY] * n, out_specs=[ANY] * n,
        out_shape=[jax.ShapeDtypeStruct((NCHIP,) + a.shape, a.dtype) for a in arrs],
        scratch_shapes=[pltpu.SemaphoreType.DMA((n, 3))] * 4 + [pltpu.SemaphoreType.DMA((n,))],
    )(*arrs)


def _adam_math(g, w, m, v):
    m = ADAM_B1 * m + (1.0 - ADAM_B1) * g
    v = ADAM_B2 * v + (1.0 - ADAM_B2) * (g * g)
    m_hat = m / (1.0 - ADAM_B1 ** ADAM_STEP)
    v_hat = v / (1.0 - ADAM_B2 ** ADAM_STEP)
    delta = -ADAM_LR * (m_hat / (jnp.sqrt(v_hat) + ADAM_EPS) + ADAM_WD * w)
    return delta, m, v


def _rows_tile(rows):
    return rows if rows <= 256 else 256


def sum_chips(parts, name):
    _, rows, cols = parts.shape
    tr = _rows_tile(rows)

    def body(p_ref, o_ref):
        acc = p_ref[0].astype(F32)
        for s in range(1, NCHIP):
            acc = acc + p_ref[s].astype(F32)
        o_ref[...] = acc

    return pl.pallas_call(
        body, grid=(rows // tr,), name=name,
        in_specs=[pl.BlockSpec((NCHIP, tr, cols), lambda i: (0, i, 0))],
        out_specs=pl.BlockSpec((tr, cols), lambda i: (i, 0)),
        out_shape=jax.ShapeDtypeStruct((rows, cols), F32),
        compiler_params=_cparams(("parallel",)),
    )(parts)


def sum_chips_small(parts, name):
    n = len(parts)

    def body(*refs):
        for p_ref, o_ref in zip(refs[:n], refs[n:]):
            acc = p_ref[0]
            for s in range(1, NCHIP):
                acc = acc + p_ref[s]
            o_ref[...] = acc

    return pl.pallas_call(body, name=name, out_shape=[jax.ShapeDtypeStruct(p.shape[1:], F32) for p in parts])(*parts)


def adam_shard_small(items, name):
    n = len(items)

    def body(*refs):
        for t in range(n):
            a_ref, b_ref, w_ref, m_ref, v_ref = refs[5 * t:5 * t + 5]
            g_ref, d_ref, mo_ref, vo_ref = refs[5 * n + 4 * t:5 * n + 4 * t + 4]
            g = (a_ref[...] + b_ref[...]).reshape(w_ref.shape)
            g_ref[...] = g
            d_ref[...], mo_ref[...], vo_ref[...] = _adam_math(g, w_ref[...], m_ref[...], v_ref[...])

    out = pl.pallas_call(
        body, name=name, out_shape=[jax.ShapeDtypeStruct(it[2].shape, F32) for it in items for _ in range(4)],
    )(*[a for it in items for a in it])
    return [out[4 * t:4 * t + 4] for t in range(n)]


def adam_shard(p_mine, p_sib, w, m, v, name):
    rows, cols = p_mine.shape
    tr = _rows_tile(rows)
    lead = w.ndim == 3

    def body(a_ref, b_ref, w_ref, m_ref, v_ref, g_ref, d_ref, mo_ref, vo_ref):
        g = a_ref[...] + b_ref[...]
        g = g[None] if lead else g
        g_ref[...] = g
        d_ref[...], mo_ref[...], vo_ref[...] = _adam_math(g, w_ref[...], m_ref[...], v_ref[...])

    flat = pl.BlockSpec((tr, cols), lambda i: (i, 0))
    spec = pl.BlockSpec((1, tr, cols), lambda i: (0, i, 0)) if lead else flat
    return pl.pallas_call(
        body, grid=(rows // tr,), name=name, in_specs=[flat] * 2 + [spec] * 3, out_specs=[spec] * 4,
        out_shape=[jax.ShapeDtypeStruct(w.shape, F32)] * 4,
        compiler_params=_cparams(("parallel",)),
    )(p_mine, p_sib, w, m, v)


def adam_shard_halves_t(r_mine, r_sib, wt, mt, vt, name):
    hrows, cols = r_mine.shape
    tr = _rows_tile(hrows)
    per_half = hrows // tr

    def body(a_ref, b_ref, w_ref, m_ref, v_ref, g_ref, d_ref, mo_ref, vo_ref):
        mine = pl.program_id(0) == lax.axis_index("c")
        g = jnp.where(mine, a_ref[...], b_ref[...]).T[None]
        g_ref[...] = g
        d_ref[...], mo_ref[...], vo_ref[...] = _adam_math(g, w_ref[...], m_ref[...], v_ref[...])

    flat = pl.BlockSpec((tr, cols), lambda h, i: (i, 0))
    spec = pl.BlockSpec((1, cols, tr), lambda h, i: (0, 0, h * per_half + i))
    return pl.pallas_call(
        body, grid=(2, per_half), name=name, in_specs=[flat] * 2 + [spec] * 3, out_specs=[spec] * 4,
        out_shape=[jax.ShapeDtypeStruct(wt.shape, F32)] * 4,
        compiler_params=_cparams(("parallel", "parallel")),
    )(r_mine, r_sib, wt, mt, vt)


def adam_replicated(gathered, params, name):
    flat = []
    for i, p in enumerate(params):
        if isinstance(p, list):
            off = 0
            for wmv in p:
                n = gathered[i].shape[-1] - off if wmv[0] is None else wmv[0].shape[-1]
                flat.append((i, (off, n), wmv))
                off += n
        else:
            flat.append((i, None, p))
    ins = [a for _, _, wmv in flat for a in wmv if a is not None]
    ng = len(gathered)

    def body(*refs):
        g_refs = refs[:ng]
        in_refs = list(refs[ng:ng + len(ins)])
        out_refs = list(refs[ng + len(ins):])
        sums = []
        for r in g_refs:
            g = r[0]
            for d in range(1, NDEV):
                g = g + r[d]
            sums.append(g)
        for i, lanes, wmv in flat:
            g = sums[i] if lanes is None else sums[i][:, lanes[0]:lanes[0] + lanes[1]]
            out_refs.pop(0)[...] = g
            if wmv[0] is not None:
                w_ref, m_ref, v_ref = in_refs.pop(0), in_refs.pop(0), in_refs.pop(0)
                d_ref, mo_ref, vo_ref = out_refs.pop(0), out_refs.pop(0), out_refs.pop(0)
                d_ref[...], mo_ref[...], vo_ref[...] = _adam_math(g, w_ref[...], m_ref[...], v_ref[...])

    out_shape = []
    for i, lanes, wmv in flat:
        shape = gathered[i].shape[1:] if lanes is None else (1, lanes[1])
        out_shape += [jax.ShapeDtypeStruct(shape, F32)] * (4 if wmv[0] is not None else 1)
    outs = list(pl.pallas_call(body, name=name, out_shape=out_shape)(*gathered, *ins))
    return [[outs.pop(0) for _ in range(4 if wmv[0] is not None else 1)] for _, _, wmv in flat]


EVEN_SPLITS = (SHIFT, W, W, W, W, W)
ODD_SPLITS = (D, D, D)


def _cols_to_chips(a):
    rows, cols = a.shape
    return a.reshape(rows, NCHIP, cols // NCHIP).transpose(1, 0, 2)


def _chips_to_cols(a):
    _, rows, n = a.shape
    return a.transpose(1, 0, 2).reshape(rows, NCHIP * n)


def kernel(x, norm_g, w_in_e, shift_mu, rw_w0, rw_w2, rw_a0, rw_a2, rw_kk, rw_ka, rw_rk, rw_lnx_g, rw_lnx_b, att_bias, w_out_e, w_in_o, sg_ln_g, sg_ln_b, sg_w, sg_b, w_out_o, final_g, loss_target, m_norm_g, m_w_in_e, m_shift_mu, m_rw_w0, m_rw_w2, m_rw_a0, m_rw_a2, m_rw_kk, m_rw_ka, m_rw_rk, m_rw_lnx_g, m_rw_lnx_b, m_att_bias, m_w_out_e, m_w_in_o, m_sg_ln_g, m_sg_ln_b, m_sg_w, m_sg_b, m_w_out_o, m_final_g, v_norm_g, v_w_in_e, v_shift_mu, v_rw_w0, v_rw_w2, v_rw_a0, v_rw_a2, v_rw_kk, v_rw_ka, v_rw_rk, v_rw_lnx_g, v_rw_lnx_b, v_att_bias, v_w_out_e, v_w_in_o, v_sg_ln_g, v_sg_ln_b, v_sg_w, v_sg_b, v_w_out_o, v_final_g):
    x2 = x.reshape(T, D)
    tgt = loss_target.reshape(T, D)

    gathered = gather_weights(
        [jnp.swapaxes(w_in_e[0], 0, 1).astype(BF16), jnp.concatenate([rw_w2[0], rw_a2[0]], axis=0),
         jnp.concatenate([sg_ln_g, sg_ln_b], axis=0)], [True, True, False])
    wie = gathered[0].reshape(EVEN_IN, D)
    w2 = _chips_to_cols(gathered[1][:, :LORA])
    a2 = _chips_to_cols(gathered[1][:, LORA:])
    sglg = _chips_to_cols(gathered[2][:, 0:1])
    sglb = _chips_to_cols(gathered[2][:, 1:2])

    late = [w_out_e[0].astype(BF16), w_in_o[0].astype(BF16), w_out_o[0].astype(BF16)]
    late_started = send_start(late, False, False, "late_weights_start", after=gathered[0])

    def late_weights(after):
        woe, wio, woo = send_wait(late_started, after, False, False, "late_weights_wait")
        return woe.reshape(D, D), _chips_to_cols(wio), woo.reshape(D, D)

    def scatter_start(grads, name):
        return send_start([g_.astype(BF16) if g_.shape[-1] >= W else g_ for g_ in grads], True, False, name)

    started = {}

    def on_odd_grads(d_woo, d_wio):
        started["odd"] = scatter_start([d_woo.reshape(NCHIP, D // NCHIP, D), d_wio], "odd_grads_start")
        return started["odd"][-1]

    def on_even_grads(big_g):
        d_wie_half, d_woe, _, _, d_w2, d_a2, d_sglg, d_sglb = big_g
        blocks = [d_wie_half, d_woe.reshape(NCHIP, D // NCHIP, D), _cols_to_chips(d_w2), _cols_to_chips(d_a2),
                  _cols_to_chips(d_sglg), _cols_to_chips(d_sglb)]
        started["even"] = scatter_start(blocks, "even_grads_start")
        return started["even"][-1]

    def on_small_grads(layer, grads):
        if layer == "odd":
            d_sg_w, d_sg_b, d_final, d_g1 = grads
            mine = [d_sg_w.reshape(NG * SGC, SGC), d_sg_b, jnp.concatenate([d_final, d_g1], axis=1)]
        else:
            mine = [grads[-2], jnp.concatenate(grads[:-2] + grads[-1:], axis=1)]
        started[layer + "_small"] = send_start(mine, False, True, layer + "_small_grads_start")
        return started[layer + "_small"][-1]

    loss_part, dx, _, _ = _local_step(
        x2, tgt, wie, late_weights, w2, a2, sglg, sglb, norm_g, shift_mu, rw_w0, rw_a0, rw_kk, rw_ka, rw_rk,
        rw_lnx_g, rw_lnx_b, att_bias, sg_w, sg_b, final_g, first_after=late_started[-1], on_odd_grads=on_odd_grads,
        on_even_grads=on_even_grads, on_small_grads=on_small_grads)
    wmv = {"w_in_e": tuple(jnp.swapaxes(a, 1, 2) for a in (w_in_e, m_w_in_e, v_w_in_e)),
           "w_out_e": (w_out_e, m_w_out_e, v_w_out_e),
           "w_in_o": (w_in_o, m_w_in_o, v_w_in_o), "w_out_o": (w_out_o, m_w_out_o, v_w_out_o),
           "rw_w2": (rw_w2, m_rw_w2, v_rw_w2), "rw_a2": (rw_a2, m_rw_a2, v_rw_a2),
           "sg_ln_g": (sg_ln_g, m_sg_ln_g, v_sg_ln_g), "sg_ln_b": (sg_ln_b, m_sg_ln_b, v_sg_ln_b)}
    sharded = {}

    def finish(names, landed, tag):
        nbig = sum(p_.dtype == BF16 for p_ in landed)
        partial = [sum_chips(p_, "sum_" + nm) for p_, nm in zip(landed[:nbig], names)]
        if nbig < len(names):
            partial += sum_chips_small(landed[nbig:], "sum_small_" + tag)
        from_sibling = exchange_c(partial, "swap_partials_" + tag)
        for nm, mine, sib in zip(names[:nbig], partial, from_sibling):
            if nm == "w_in_e":
                res = adam_shard_halves_t(mine, sib, *wmv[nm], "adam_" + nm)
                sharded[nm] = [jnp.swapaxes(a, 1, 2) for a in res]
            else:
                sharded[nm] = adam_shard(mine, sib, *wmv[nm], "adam_" + nm)
        if nbig < len(names):
            items = [(mine, sib, *wmv[nm]) for nm, mine, sib in zip(names, partial, from_sibling)][nbig:]
            for nm, res in zip(names[nbig:], adam_shard_small(items, "adam_small_" + tag)):
                sharded[nm] = res
        return partial[0]

    odd_landed = send_wait(started["odd"], started["even_small"][-1], True, False, "odd_grads_wait")
    done = finish(["w_out_o", "w_in_o"], odd_landed, "odd")

    def wmv_of(*arrs, view=lambda a: a):
        return tuple(view(a) for a in arrs)

    vec = lambda a: a.reshape(1, -1)
    groups = {
        "odd": (["sg_w", "sg_b", "final_g", "norm_g1"],
                [wmv_of(sg_w, m_sg_w, v_sg_w, view=lambda a: a.reshape(NG * SGC, SGC)),
                 wmv_of(sg_b, m_sg_b, v_sg_b, view=lambda a: a[0]),
                 [wmv_of(final_g, m_final_g, v_final_g, view=vec),
                  wmv_of(norm_g, m_norm_g, v_norm_g, view=lambda a: a[1:2])]]),
        "even": (["att_bias", "norm_g0", "shift_mu", "rw_w0", "rw_a0", "rw_kk", "rw_ka", "rw_rk", "rw_lnx_g",
                  "rw_lnx_b", "loss"],
                 [wmv_of(att_bias, m_att_bias, v_att_bias, view=lambda a: a[0]),
                  [wmv_of(norm_g, m_norm_g, v_norm_g, view=lambda a: a[0:1]),
                   wmv_of(shift_mu, m_shift_mu, v_shift_mu), wmv_of(rw_w0, m_rw_w0, v_rw_w0),
                   wmv_of(rw_a0, m_rw_a0, v_rw_a0), wmv_of(rw_kk, m_rw_kk, v_rw_kk), wmv_of(rw_ka, m_rw_ka, v_rw_ka),
                   wmv_of(rw_rk, m_rw_rk, v_rw_rk, view=vec), wmv_of(rw_lnx_g, m_rw_lnx_g, v_rw_lnx_g),
                   wmv_of(rw_lnx_b, m_rw_lnx_b, v_rw_lnx_b), (None, None, None)]]),
    }
    rep = {}
    for layer in ("odd", "even"):
        nms, params = groups[layer]
        gathered_g = send_wait(started[layer + "_small"], done, False, True, layer + "_small_grads_wait")
        for nm, res in zip(nms, adam_replicated(gathered_g, params, "adam_" + layer + "_small")):
            rep[nm] = res
        done = rep[nms[0]][0]
    native = {"sg_w": sg_w.shape, "sg_b": sg_b.shape, "final_g": final_g.shape, "rw_rk": rw_rk.shape,
              "att_bias": att_bias.shape}
    for nm, shape in native.items():
        rep[nm] = [a.reshape(shape) for a in rep[nm]]
    rep["norm_g"] = [jnp.concatenate([a, b], axis=0) for a, b in zip(rep["norm_g0"], rep["norm_g1"])]
    even_landed = send_wait(started["even"], done, True, False, "even_grads_wait")
    finish(["w_in_e", "w_out_e", "rw_w2", "rw_a2", "sg_ln_g", "sg_ln_b"], even_landed, "even")

    order = ["norm_g", "w_in_e", "shift_mu", "rw_w0", "rw_w2", "rw_a0", "rw_a2", "rw_kk", "rw_ka", "rw_rk",
             "rw_lnx_g", "rw_lnx_b", "att_bias", "w_out_e", "w_in_o", "sg_ln_g", "sg_ln_b", "sg_w", "sg_b",
             "w_out_o", "final_g"]
    results = {**sharded, **rep}
    outs = [rep["loss"][0][0, 0], dx.reshape(NSEQ, SEQ, D)]
    for kind in range(4):
        outs += [results[nm][kind] for nm in order]
    return tuple(outs)


def _local_step(x2, tgt, wie_t, late_weights, w2, a2, sglg, sglb, norm_g, shift_mu, rw_w0, rw_a0, rw_kk, rw_ka, rw_rk,
                rw_lnx_g, rw_lnx_b, att_bias, sg_w, sg_b, final_g, first_after=None, on_odd_grads=None,
                on_even_grads=None, on_small_grads=None):
    zl = jnp.zeros((LORA, W), F32)
    w2x = jnp.concatenate([w2, zl], axis=0)
    a2x = jnp.concatenate([zl, a2], axis=0)
    rk = rw_rk.reshape(1, W)
    pos = np.arange(SGC)
    sg_mask = jnp.asarray(((pos[None, :] // L) <= (pos[:, None] // L)).astype(np.float32))
    wm = (sg_w[0] * sg_mask[None]).astype(BF16)
    sgb_t = sg_b[0].T

    xn0, ps, ga, q, kb, vb, gb = ln_in_proj(x2, norm_g[0:1], wie_t, EVEN_SPLITS, "in_proj_even", after=first_after,
                                            w_t=True, bf16_pieces=(2, 3, 4))
    r, lw, k2, v, aa, bb = even_prep(ps, shift_mu, rw_w0, w2x, rw_a0, a2x, rw_kk, rw_ka)
    y, rw_saved = rwkv_fwd(r, lw, k2, v, aa, bb)
    bias = bias_expand(att_bias[0])

    def padded(a):
        return jnp.pad(a.astype(BF16).reshape(NSEQ, SEQ, W), ((0, 0), (LEFT * L, 0), (0, 0))).reshape(NSEQ * PADSEQ, W)

    kpad, vpad = padded(kb), padded(vb)
    o = attention_fwd(q, kpad, vpad, bias)
    woe, wio, woo = late_weights(o)
    h1, zt = even_post(y, r, k2, v, ga, o, gb, rw_lnx_g, rw_lnx_b, rk, x2, woe)
    xn1, u, vv, gt = ln_in_proj(h1, norm_g[1:2], wio, ODD_SPLITS, "in_proj_odd")
    dh2, loss_part, d_final_g, z2t = gmlp_fwd_loss(u, vv, gt, sglg, sglb, wm, sgb_t, h1, woo, final_g[None], tgt)

    du, dvv, dgt, d_sglg, d_sglb, d_wm, d_sgb_t, d_woo = gmlp_bwd(u, vv, gt, sglg, sglb, wm, sgb_t, dh2, z2t, woo)
    dp_odd = [du, dvv, dgt]
    d_wio = matmul_acc_chips(xn1, dp_odd, "in_proj_odd_dw")
    token = on_odd_grads(d_woo, d_wio) if on_odd_grads else None
    dh1, d_g1 = in_proj_bwd_x(h1, norm_g[1:2], wio, dp_odd, dh2, "in_proj_odd_bwd", after=token)
    odd_small = [d_wm * sg_mask[None], d_sgb_t.T, d_final_g, d_g1]
    token = on_small_grads("odd", odd_small) if on_small_grads else None
    dy, dr2, dk22, dv2, dga, do, dgb, d_lng, d_lnb, d_rk, d_woe = even_post_bwd(
        y, r, k2, v, ga, o, gb, rw_lnx_g, rw_lnx_b, rk, dh1, zt, woe, after=token)
    dq, dkb, dvb, dbias = attention_bwd(q, kpad, vpad, bias, do)
    dbias = sum(dbias[:, i * 2 * L:(i + 1) * 2 * L, i * L:i * L + BAND] for i in range(ATT_Q))
    d_att_bias = bias_grad(dbias.reshape(NH, L, BAND))
    dr, dlw, dk2, dv, daa, dbb = rwkv_bwd(r, lw, k2, aa, bb, rw_saved, dy)
    dps, d_mu, d_w0, d_w2x, d_a0, d_a2x, d_kk, d_ka = even_prep_bwd(
        ps, shift_mu, rw_w0, w2x, rw_a0, a2x, rw_kk, rw_ka, dr, dlw, dk2, dv, daa, dbb, dr2, dk22, dv2)
    dp_even = [dps, dga, dq, dkb, dvb, dgb]
    d_wie = matmul_acc_chips(xn0, dp_even, "in_proj_even_dw", add_cores=on_even_grads is not None)
    big_g = (d_wie, d_woe, d_wio, d_woo, d_w2x[:LORA], d_a2x[LORA:], d_sglg, d_sglb)
    token = on_even_grads(big_g) if on_even_grads else None
    dx, d_g0 = in_proj_bwd_x(x2, norm_g[0:1], wie_t, dp_even, dh1, "in_proj_even_bwd", after=token, w_t=True)
    even_small = [d_g0, d_mu, d_w0, d_a0, d_kk, d_ka, d_rk, d_lng, d_lnb, d_att_bias]
    if on_small_grads:
        on_small_grads("even", even_small + [loss_part[0:1, :]])
    rep_g = [jnp.concatenate([d_g0, d_g1], axis=0)] + even_small[1:] + odd_small[:3]
    return loss_part[0, 0], dx, big_g, rep_g
```

```python
import functools
import math

import jax
import jax.numpy as jnp
import numpy as np
from jax import lax
from jax.experimental import pallas as pl
from jax.experimental.pallas import tpu as pltpu

F32 = jnp.float32
BF16 = jnp.bfloat16
HI = lax.Precision.HIGHEST

D = 1024
SEQ = 2048
NSEQ = 2
T = NSEQ * SEQ
HD = 64
NH = 8
W = 512
SHIFT = 1664
LORA = 64
EVEN_IN = 4224
ODD_IN = 3072
L = 64
NC = SEQ // L
LEFT = 8
BAND = (LEFT + 1) * L
CLIP = 128
SGC = 128
NG = 8
RMS_EPS = 1e-6
LN_EPS = 1e-5
GN_EPS = 64e-5
NEG = -1e30
VMEM_BIG = 56 * 1024 * 1024

ADAM_LR = 0.001
ADAM_B1 = 0.9
ADAM_B2 = 0.999
ADAM_EPS = 1e-08
ADAM_WD = 0.01
ADAM_STEP = 10

MESH = pl.DeviceIdType.MESH


def _bdot(a, b):
    return jnp.dot(a.astype(BF16), b.astype(BF16), preferred_element_type=F32)


def _bdot_nt(a, b):
    return lax.dot_general(a.astype(BF16), b.astype(BF16), (((1,), (1,)), ((), ())), preferred_element_type=F32)


def _bdot_tn(a, b):
    return lax.dot_general(a.astype(BF16), b.astype(BF16), (((0,), (0,)), ((), ())), preferred_element_type=F32)


def _hdot(a, b):
    return jnp.dot(a, b, precision=HI, preferred_element_type=F32)


def _hdot_nt(a, b):
    return lax.dot_general(a, b, (((1,), (1,)), ((), ())), precision=HI, preferred_element_type=F32)


def _hdot_tn(a, b):
    return lax.dot_general(a, b, (((0,), (0,)), ((), ())), precision=HI, preferred_element_type=F32)


def _iota2(shape, dim):
    return lax.broadcasted_iota(jnp.int32, shape, dim)


def _head_blockdiag():
    r = _iota2((2 * HD, 2 * HD), 0) // HD
    c = _iota2((2 * HD, 2 * HD), 1) // HD
    return (r == c).astype(BF16)


def _headsum_impl(x, bd):
    hi = x.astype(BF16)
    mid = (x - hi.astype(F32)).astype(BF16)
    n = bd.shape[0]
    out = [jnp.dot(hi[:, i:i + n], bd, preferred_element_type=F32) + jnp.dot(mid[:, i:i + n], bd, preferred_element_type=F32)
           for i in range(0, x.shape[1], n)]
    return jnp.concatenate(out, axis=-1)


@jax.custom_vjp
def _headsum(x, bd):
    return _headsum_impl(x, bd)


def _headsum_fwd(x, bd):
    return _headsum_impl(x, bd), bd


def _headsum_bwd(bd, ct):
    return _headsum_impl(ct, bd), None


_headsum.defvjp(_headsum_fwd, _headsum_bwd)


def _silu(x):
    return x * jax.nn.sigmoid(x)


def _dsilu(x):
    s = jax.nn.sigmoid(x)
    return s * (1.0 + x * (1.0 - s))


_GELU_C = math.sqrt(2.0 / math.pi)


def _gelu(x):
    return 0.5 * x * (1.0 + jnp.tanh(_GELU_C * (x + 0.044715 * (x * x * x))))


def _dgelu(x):
    t = jnp.tanh(_GELU_C * (x + 0.044715 * (x * x * x)))
    return 0.5 * (1.0 + t) + 0.5 * x * (1.0 - t * t) * _GELU_C * (1.0 + 3.0 * 0.044715 * x * x)


def _silu_both(x):
    s = jax.nn.sigmoid(x)
    xs = x * s
    return xs, s + xs * (1.0 - s)


def _gelu_both(x):
    x2 = x * x
    t = jnp.tanh(_GELU_C * (x + 0.044715 * (x2 * x)))
    half = 0.5 * (1.0 + t)
    return x * half, half + 0.5 * x * (1.0 - t * t) * _GELU_C * (1.0 + 3.0 * 0.044715 * x2)


def _softplus(x):
    return jnp.maximum(x, 0.0) + jnp.log(1.0 + jnp.exp(-jnp.abs(x)))


def _cparams(sem, vmem=None):
    return pltpu.CompilerParams(dimension_semantics=sem, vmem_limit_bytes=vmem)


def _row_spec(tm, width):
    return pl.BlockSpec((tm, width), lambda i: (i, 0))


def _col_spec(height, tm):
    return pl.BlockSpec((height, tm), lambda i: (0, i))


def _const_spec(shape):
    nd = len(shape)
    return pl.BlockSpec(shape, lambda *_: (0,) * nd)


def _weight_dims(w_bf, w_t):
    return (((1,), (1,)), ((), ())) if w_t else (((1,), (0,)), ((), ())), w_bf.shape[0 if w_t else 1]


def ln_in_proj(x, g, w_bf, splits, name, after=None, w_t=False, bf16_pieces=()):
    dims, n = _weight_dims(w_bf, w_t)
    dtypes = [BF16 if i in bf16_pieces else F32 for i in range(len(splits))]
    tm = 512 if n <= ODD_IN else 256
    spans = []
    o = 0
    for s in splits:
        spans.append((o, o + s))
        o += s
    assert o == n
    extra_specs, extra = _after_operand(after)

    def body(x_ref, g_ref, w_ref, *rest):
        xn_ref, outs = rest[len(extra)], rest[len(extra) + 1:]
        xv = x_ref[...]
        rstd = lax.rsqrt(jnp.mean(xv * xv, axis=-1, keepdims=True) + RMS_EPS)
        xn = (xv * rstd * g_ref[...]).astype(BF16)
        xn_ref[...] = xn.T
        p = lax.dot_general(xn, w_ref[...], dims, preferred_element_type=F32)
        for o_ref, (a, b) in zip(outs, spans):
            o_ref[...] = p[:, a:b].astype(o_ref.dtype)

    return pl.pallas_call(
        body, grid=(T // tm,), name=name,
        in_specs=[_row_spec(tm, D), _const_spec((1, D)), _const_spec(w_bf.shape)] + extra_specs,
        out_specs=[_col_spec(D, tm)] + [_row_spec(tm, s) for s in splits],
        out_shape=[jax.ShapeDtypeStruct((D, T), BF16)]
        + [jax.ShapeDtypeStruct((T, s), dt) for s, dt in zip(splits, dtypes)],
        compiler_params=_cparams(("parallel",), VMEM_BIG),
    )(x, g, w_bf, *extra)


def in_proj_bwd_x(x, g, w_bf, dps, dres, name, after=None, w_t=False):
    tm = 512
    back = (((1,), (0,)), ((), ())) if w_t else (((1,), (1,)), ((), ()))
    widths = [d.shape[1] for d in dps]
    extra_specs, extra = _after_operand(after)

    def body(x_ref, g_ref, w_ref, dres_ref, *rest):
        dp_refs = rest[:len(widths)]
        dx_ref, dg_ref = rest[-2:]
        dp = jnp.concatenate([r[...] for r in dp_refs], axis=-1)
        dxn = lax.dot_general(dp, w_ref[...], back, preferred_element_type=F32)
        xv = x_ref[...]
        rstd = lax.rsqrt(jnp.mean(xv * xv, axis=-1, keepdims=True) + RMS_EPS)
        xhat = xv * rstd
        dgp = jnp.sum(dxn * xhat, axis=0, keepdims=True)

        @pl.when(pl.program_id(0) == 0)
        def _():
            dg_ref[...] = jnp.zeros_like(dg_ref)

        dg_ref[...] += dgp
        dxh = dxn * g_ref[...]
        dx_ref[...] = dres_ref[...] + rstd * (dxh - xhat * jnp.mean(dxh * xhat, axis=-1, keepdims=True))

    return pl.pallas_call(
        body, grid=(T // tm,), name=name,
        in_specs=[_row_spec(tm, D), _const_spec((1, D)), _const_spec(w_bf.shape), _row_spec(tm, D)]
        + [_row_spec(tm, s) for s in widths] + extra_specs,
        out_specs=[_row_spec(tm, D), _const_spec((1, D))],
        out_shape=[jax.ShapeDtypeStruct((T, D), F32), jax.ShapeDtypeStruct((1, D), F32)],
        compiler_params=_cparams(("arbitrary",), VMEM_BIG),
    )(x, g, w_bf, dres, *dps, *extra)


def _after_operand(after):
    return ([ANY], [after]) if after is not None else ([], [])


def matmul_acc_chips(at_bf, pieces, name, after=None, add_cores=False):
    k = at_bf.shape[0]
    widths = [p.shape[1] for p in pieces]
    nb = sum(widths) // NCHIP
    tm = 512
    steps = T // tm
    half = k // 2
    extra_specs, extra = _after_operand(after)

    def body(a_ref, *rest):
        o_ref, acc = rest[len(widths) + len(extra):][:2]

        @pl.when(pl.program_id(0) == 0)
        def _():
            acc[...] = jnp.zeros_like(acc)

        a = a_ref[...]
        b = jnp.concatenate([r[...] for r in rest[:len(widths)]], axis=-1)
        for s in range(NCHIP):
            acc[s] += jnp.dot(a, b[:, s * nb:(s + 1) * nb], preferred_element_type=F32)

        @pl.when(pl.program_id(0) == steps - 1)
        def _():
            if not add_cores:
                o_ref[...] = acc[...].astype(BF16)
            else:
                give, got, send, recv = rest[-4:]
                x, y, c = lax.axis_index("x"), lax.axis_index("y"), lax.axis_index("c")
                theirs = pl.multiple_of((1 - c) * half, half)
                mine = pl.multiple_of(c * half, half)
                give[...] = acc[:, pl.ds(theirs, half), :].astype(BF16)
                cp = pltpu.make_async_remote_copy(src_ref=give, dst_ref=got, send_sem=send, recv_sem=recv,
                                                  device_id=(x, y, 1 - c), device_id_type=MESH)
                cp.start()
                cp.wait()
                o_ref[...] = (acc[:, pl.ds(mine, half), :] + got[...].astype(F32)).astype(BF16)

    out_rows = half if add_cores else k
    exchange = [pltpu.VMEM((NCHIP, half, nb), BF16)] * 2 + [pltpu.SemaphoreType.DMA] * 2 if add_cores else []
    return pl.pallas_call(
        body, grid=(steps,), name=name,
        in_specs=[_col_spec(k, tm)] + [_row_spec(tm, w_) for w_ in widths] + extra_specs,
        out_specs=_const_spec((NCHIP, out_rows, nb)),
        out_shape=jax.ShapeDtypeStruct((NCHIP, out_rows, nb), BF16),
        scratch_shapes=[pltpu.VMEM((NCHIP, k, nb), F32)] + exchange,
        compiler_params=_cparams(("arbitrary",), VMEM_BIG),
    )(at_bf, *pieces, *extra)


def _out_proj_back(dh_ref, zt_ref, w_ref, dw_ref):
    dhb = dh_ref[...].astype(BF16)

    @pl.when(pl.program_id(0) == 0)
    def _():
        dw_ref[...] = jnp.zeros_like(dw_ref)

    dw_ref[...] += jnp.dot(zt_ref[...], dhb, preferred_element_type=F32)
    return lax.dot_general(dhb, w_ref[...], (((1,), (1,)), ((), ())), preferred_element_type=F32)


PREP_TM = 512
PREP_NB = SEQ // PREP_TM


def _prep_elem(k, wl, apre, kkw, kaw, bd):
    wraw = -_softplus(-wl) - 0.5
    lw = -jnp.exp(wraw)
    asig = jax.nn.sigmoid(apre)
    kkr = k * kkw
    nrm = jnp.maximum(jnp.sqrt(_headsum(kkr * kkr, bd)), 1e-12)
    kk = kkr / nrm
    k2 = k * (1.0 + (asig - 1.0) * kaw)
    return lw, k2, -kk, kk * asig


def _prep_elem_bwd(k, wl, apre, kkw, kaw, bd, dlw, dk2, daa, dbb):
    s = -wl
    sp = _softplus(s)
    dwl = dlw * (-jnp.exp(-sp - 0.5)) * jnp.exp(s - sp)
    asig = jax.nn.sigmoid(apre)
    kkr = k * kkw
    root = jnp.sqrt(_headsum(kkr * kkr, bd))
    inv = 1.0 / jnp.maximum(root, 1e-12)
    kk = kkr * inv
    dkk = dbb * asig - daa
    dap = (dbb * kk + dk2 * k * kaw) * asig * (1.0 - asig)
    through_norm = jnp.where(root > 1e-12, kk * _headsum(dkk * kkr, bd) * inv, 0.0)
    dkkr = inv * (dkk - through_norm)
    gain = 1.0 + (asig - 1.0) * kaw
    dk = dkkr * kkw + dk2 * gain
    dkkw = jnp.sum(dkkr * k, axis=0, keepdims=True)
    dkaw = jnp.sum(dk2 * k * (asig - 1.0), axis=0, keepdims=True)
    return dk, dwl, dap, dkkw, dkaw


def _shifted(ps_ref, prev_ref, mu, blk):
    p = ps_ref[...]
    first = (blk % PREP_NB) == 0
    prev_row = jnp.where(first, 0.0, prev_ref[7:8, :])
    rolled = pltpu.roll(p, 1, 0)
    p_prev = jnp.where(_iota2(p.shape, 0) == 0, prev_row, rolled)
    return p, p_prev, p + (p_prev - p) * mu


def _prev_spec(width, blk_of):
    return pl.BlockSpec((8, width), lambda i: (jnp.maximum(blk_of(i) * (PREP_TM // 8) - 1, 0), 0))


def even_prep(ps, mu, w0, w2x, a0, a2x, kkw, kaw):
    tm = PREP_TM

    def body(ps_ref, prev_ref, mu_ref, w0_ref, w2_ref, a0_ref, a2_ref, kk_ref, ka_ref,
             r_ref, lw_ref, k2_ref, v_ref, aa_ref, bb_ref):
        _, _, s = _shifted(ps_ref, prev_ref, mu_ref[...], pl.program_id(0))
        wa = s[:, 3 * W:]
        wl = w0_ref[...] + _bdot(jnp.tanh(wa), w2_ref[...])
        apre = a0_ref[...] + _bdot(wa, a2_ref[...])
        lw, k2, aa, bb = _prep_elem(s[:, W:2 * W], wl, apre, kk_ref[...], ka_ref[...], _head_blockdiag())
        r_ref[...] = s[:, 0:W]
        v_ref[...] = s[:, 2 * W:3 * W]
        lw_ref[...] = lw
        k2_ref[...] = k2
        aa_ref[...] = aa
        bb_ref[...] = bb

    vec = _const_spec((1, W))
    return pl.pallas_call(
        body, grid=(T // tm,), name="even_prep",
        in_specs=[_row_spec(tm, SHIFT), _prev_spec(SHIFT, lambda i: i), _const_spec((1, SHIFT)), vec,
                  _const_spec((2 * LORA, W)), vec, _const_spec((2 * LORA, W)), vec, vec],
        out_specs=[_row_spec(tm, W)] * 6,
        out_shape=[jax.ShapeDtypeStruct((T, W), F32)] * 6,
        compiler_params=_cparams(("parallel",), VMEM_BIG),
    )(ps, ps, mu, w0, w2x, a0, a2x, kkw, kaw)


def even_prep_bwd(ps, mu, w0, w2x, a0, a2x, kkw, kaw, dr, dlw, dk2, dv, daa, dbb, dr2, dk22, dv2):
    tm = PREP_TM
    nb = T // tm
    rev = lambda i: nb - 1 - i

    def body(ps_ref, prev_ref, mu_ref, w0_ref, w2_ref, a0_ref, a2_ref, kk_ref, ka_ref,
             dr_ref, dlw_ref, dk2_ref, dv_ref, daa_ref, dbb_ref, dr2_ref, dk22_ref, dv2_ref,
             dps_ref, dmu_ref, dw0_ref, dw2_ref, da0_ref, da2_ref, dkk_ref, dka_ref, carry):
        i = pl.program_id(0)
        blk = rev(i)
        mu_v = mu_ref[...]
        p, p_prev, s = _shifted(ps_ref, prev_ref, mu_v, blk)
        wa = s[:, 3 * W:]
        th = jnp.tanh(wa)
        wl = w0_ref[...] + _bdot(th, w2_ref[...])
        apre = a0_ref[...] + _bdot(wa, a2_ref[...])
        bd = _head_blockdiag()
        k = s[:, W:2 * W]
        dk, dwl, dap, dkkw, dkaw = _prep_elem_bwd(k, wl, apre, kk_ref[...], ka_ref[...], bd, dlw_ref[...],
                                                  dk2_ref[...] + dk22_ref[...], daa_ref[...], dbb_ref[...])
        dwa = _bdot_nt(dwl, w2_ref[...]) * (1.0 - th * th) + _bdot_nt(dap, a2_ref[...])
        ds = jnp.concatenate([dr_ref[...] + dr2_ref[...], dk, dv_ref[...] + dv2_ref[...], dwa], axis=-1)

        @pl.when(i == 0)
        def _():
            for ref in (dmu_ref, dw0_ref, dw2_ref, da0_ref, da2_ref, dkk_ref, dka_ref, carry):
                ref[...] = jnp.zeros_like(ref)

        dmu_ref[...] += jnp.sum(ds * (p_prev - p), axis=0, keepdims=True)
        dw0_ref[...] += jnp.sum(dwl, axis=0, keepdims=True)
        da0_ref[...] += jnp.sum(dap, axis=0, keepdims=True)
        dw2_ref[...] += _bdot_tn(th, dwl)
        da2_ref[...] += _bdot_tn(wa, dap)
        dkk_ref[...] += dkkw
        dka_ref[...] += dkaw
        dsm = ds * mu_v
        last = (blk % PREP_NB) == PREP_NB - 1
        nxt = jnp.where(last, 0.0, carry[0:1, :])
        up = pltpu.roll(dsm, tm - 1, 0)
        up = jnp.where(_iota2(up.shape, 0) == tm - 1, nxt, up)
        dps_ref[...] = (ds - dsm + up).astype(BF16)
        carry[0:1, :] = dsm[0:1, :]

    vec = _const_spec((1, W))
    rrow = lambda width: pl.BlockSpec((tm, width), lambda i: (rev(i), 0))
    return pl.pallas_call(
        body, grid=(nb,), name="even_prep_bwd",
        in_specs=[rrow(SHIFT), _prev_spec(SHIFT, rev), _const_spec((1, SHIFT)), vec,
                  _const_spec((2 * LORA, W)), vec, _const_spec((2 * LORA, W)), vec, vec] + [rrow(W)] * 9,
        out_specs=[rrow(SHIFT), _const_spec((1, SHIFT)), vec, _const_spec((2 * LORA, W)), vec,
                   _const_spec((2 * LORA, W)), vec, vec],
        out_shape=[jax.ShapeDtypeStruct((T, SHIFT), BF16), jax.ShapeDtypeStruct((1, SHIFT), F32),
                   jax.ShapeDtypeStruct((1, W), F32), jax.ShapeDtypeStruct((2 * LORA, W), F32),
                   jax.ShapeDtypeStruct((1, W), F32), jax.ShapeDtypeStruct((2 * LORA, W), F32),
                   jax.ShapeDtypeStruct((1, W), F32), jax.ShapeDtypeStruct((1, W), F32)],
        scratch_shapes=[pltpu.VMEM((8, SHIFT), F32)],
        compiler_params=_cparams(("arbitrary",), VMEM_BIG),
    )(ps, ps, mu, w0, w2x, a0, a2x, kkw, kaw, dr, dlw, dk2, dv, daa, dbb, dr2, dk22, dv2)


NPAIR = NH // 2
PW = 2 * HD


def _pair_cols(p):
    return slice(p * PW, (p + 1) * PW)


def _pairs(a):
    return [a[:, _pair_cols(p)] for p in range(NPAIR)]


def _stack_pair(a):
    first = _iota2(a.shape, 1) < HD
    zero = jnp.zeros_like(a)
    return jnp.concatenate([jnp.where(first, a, zero), jnp.where(first, zero, a)], axis=0)


def _unstack_pair(a):
    n = a.shape[0] // 2
    return jnp.where(_iota2((n, PW), 1) < HD, a[:n], a[n:])


def _fold_pair(a):
    n = a.shape[0] // 2
    return a[:n] + a[n:]


def _chunk_masks():
    n = 4 * L
    row = _iota2((n, n), 0)
    col = _iota2((n, n), 1)
    same = ((row // L) & 1) == ((col // L) & 1)
    ri = row & (L - 1)
    ci = col & (L - 1)
    keep = same & (((row < 2 * L) & (ri > ci)) | ((row >= 2 * L) & (ri >= ci)))
    r1 = _iota2((L, L), 0)
    c1 = _iota2((L, L), 1)
    r2 = _iota2((2 * L, 2 * L), 0)
    c2 = _iota2((2 * L, 2 * L), 1)
    return keep.astype(F32), (r1 >= c1).astype(F32), (r2 == c2).astype(F32)


def _scaled(r, lw, k2, aa, bb, tri):
    g = _hdot(tri, lw)
    eg = jnp.exp(g)
    eng = jnp.exp(-g)
    egp = jnp.exp(g - lw)
    return eg, eng, egp, aa * egp, r * eg, bb * eng, k2 * eng


def _head_cols(h):
    return slice(h * HD, (h + 1) * HD)


def _per_head(a):
    return [a[:, _head_cols(h)] for h in range(NH)]


def _pairs_operands(at, rt, bt, kt):
    x = [jnp.concatenate([_stack_pair(a), _stack_pair(r)], axis=0).astype(BF16) for a, r in zip(_pairs(at), _pairs(rt))]
    yk = [jnp.concatenate([_stack_pair(b), _stack_pair(k)], axis=0).astype(BF16) for b, k in zip(_pairs(bt), _pairs(kt))]
    return x, yk


def _pairs_matrices(x, yk, keep, eye):
    m = [_bdot_nt(a, b) * keep for a, b in zip(x, yk)]
    p = [a[:2 * L, :2 * L] for a in m]
    tinv = [eye + a for a in p]
    for _ in range(5):
        p = [_bdot(a, a) for a in p]
        tinv = [t + _bdot(t, a) for t, a in zip(tinv, p)]
    return [a.astype(BF16) for a in m], [a.astype(BF16) for a in tinv]


def _pairs_fwd(x, yk, m, tinv, vw, s0, egl):
    xh = [_bdot_nt(a, s) for a, s in zip(x, s0)]
    u = [_bdot(t, h[:2 * L] + _bdot(a[:2 * L, 2 * L:], w)) for t, h, a, w in zip(tinv, xh, m, vw)]
    uv = [jnp.concatenate([a, w], axis=0).astype(BF16) for a, w in zip(u, vw)]
    y = [h[2 * L:] + _bdot(a[2 * L:], w) for h, a, w in zip(xh, m, uv)]
    sn = [e * (s + _bdot_tn(w, b)) for e, s, w, b in zip(egl, s0, uv, yk)]
    return y, sn, uv


def _pairs_bwd(x, yk, m, tinv, uv, s0, sn, egl, dyw, dsn, keep):
    dzs = [d * e for d, e in zip(dsn, egl)]
    dgl = [jnp.sum(d * s, axis=0, keepdims=True) for d, s in zip(dsn, sn)]
    dyb = [a.astype(BF16) for a in dyw]
    t1 = [_bdot_tn(a[2 * L:], d) for a, d in zip(m, dyb)]
    t2 = [_bdot_nt(b, d) for b, d in zip(yk, dzs)]
    drhs = [_bdot_tn(t, a[:2 * L] + b[:2 * L]) for t, a, b in zip(tinv, t1, t2)]
    dv = [a[2 * L:] + b[2 * L:] + _bdot_tn(c[:2 * L, 2 * L:], d) for a, b, c, d in zip(t1, t2, m, drhs)]
    gg = [jnp.concatenate([a, b], axis=0).astype(BF16) for a, b in zip(drhs, dyw)]
    ds0 = [d + _bdot_tn(g, a) for d, g, a in zip(dzs, gg, x)]
    dm = [_bdot_nt(g, w) * keep for g, w in zip(gg, uv)]
    dx = [_bdot(g, s) + _bdot(d, b) for g, s, d, b in zip(gg, s0, dm, yk)]
    dyk = [_bdot_tn(d, a) + _bdot(w, z) for d, a, w, z in zip(dm, x, uv, dzs)]
    return dx, dyk, dv, dgl, ds0


STATE_SHAPE = (NPAIR * PW, PW)
M_SHAPE = (4 * L, NPAIR * 4 * L)
TINV_SHAPE = (2 * L, NPAIR * 2 * L)


def _rows_of(a, n):
    return [a[i * n:(i + 1) * n, :] for i in range(NPAIR)]


def _both(f):
    out = []
    for s in range(NSEQ):
        out += f(s)
    return out


def _seq_view(a):
    return a.reshape(NSEQ, SEQ, a.shape[-1])


UV_SHAPE = (4 * L, NPAIR * PW)
RW_CHUNKS = 2


def rwkv_fwd(r, lw, k2, v, aa, bb):
    def body(r_ref, lw_ref, k2_ref, v_ref, aa_ref, bb_ref, y_ref, hs_ref, hn_ref, m_ref, t_ref, uv_ref, state):
        @pl.when(pl.program_id(0) == 0)
        def _():
            state[...] = jnp.zeros_like(state)

        keep, tri, eye = _chunk_masks()
        where = [(j, s) for j in range(RW_CHUNKS) for s in range(NSEQ)]
        rows = lambda j: slice(j * L, (j + 1) * L)
        sc = [_scaled(r_ref[s, rows(j)], lw_ref[s, rows(j)], k2_ref[s, rows(j)], aa_ref[s, rows(j)],
                      bb_ref[s, rows(j)], tri) for j, s in where]
        ops = [_pairs_operands(*a[3:]) for a in sc]
        m, tinv = _pairs_matrices([a for o in ops for a in o[0]], [a for o in ops for a in o[1]], keep, eye)
        s_cur = [state[s] for s in range(NSEQ)]
        for j in range(RW_CHUNKS):
            mine = slice(j * NSEQ * NPAIR, (j + 1) * NSEQ * NPAIR)
            x = [a for o in ops[j * NSEQ:(j + 1) * NSEQ] for a in o[0]]
            yk = [a for o in ops[j * NSEQ:(j + 1) * NSEQ] for a in o[1]]
            vw = _both(lambda s: [_stack_pair(a) for a in _pairs(v_ref[s, rows(j)])])
            egl = _both(lambda s: _pairs(sc[j * NSEQ + s][0][L - 1:L, :]))
            y, sn, uv = _pairs_fwd(x, yk, m[mine], tinv[mine], vw, _both(lambda s: _rows_of(s_cur[s], PW)), egl)
            for s in range(NSEQ):
                ps = slice(s * NPAIR, (s + 1) * NPAIR)
                hs_ref[j, s] = s_cur[s]
                y_ref[s, rows(j)] = jnp.concatenate([_fold_pair(a) for a in y[ps]], axis=-1)
                m_ref[j, s] = jnp.concatenate(m[mine][ps], axis=-1)
                t_ref[j, s] = jnp.concatenate(tinv[mine][ps], axis=-1)
                uv_ref[j, s] = jnp.concatenate(uv[ps], axis=-1)
                s_cur[s] = jnp.concatenate(sn[ps], axis=0)
                hn_ref[j, s] = s_cur[s]
        for s in range(NSEQ):
            state[s] = s_cur[s]

    blk = pl.BlockSpec((NSEQ, RW_CHUNKS * L, W), lambda c: (0, c, 0))
    per_chunk = lambda shape: pl.BlockSpec((RW_CHUNKS, NSEQ) + shape, lambda c: (c, 0, 0, 0))
    saved_shapes = [(STATE_SHAPE, F32), (STATE_SHAPE, F32), (M_SHAPE, BF16), (TINV_SHAPE, BF16), (UV_SHAPE, BF16)]
    y, *saved = pl.pallas_call(
        body, grid=(NC // RW_CHUNKS,), name="rwkv_fwd",
        in_specs=[blk] * 6,
        out_specs=[blk] + [per_chunk(shape) for shape, _ in saved_shapes],
        out_shape=[jax.ShapeDtypeStruct((NSEQ, SEQ, W), F32)]
        + [jax.ShapeDtypeStruct((NC, NSEQ) + shape, dt) for shape, dt in saved_shapes],
        scratch_shapes=[pltpu.VMEM((NSEQ,) + STATE_SHAPE, F32)],
        compiler_params=_cparams(("arbitrary",), VMEM_BIG),
    )(*[_seq_view(a) for a in (r, lw, k2, v, aa, bb)])
    return y.reshape(T, W), saved


def rwkv_bwd(r, lw, k2, aa, bb, saved, dy):
    def body(r_ref, lw_ref, k2_ref, aa_ref, bb_ref, hs_ref, hn_ref, m_ref, t_ref, uv_ref, dy_ref,
             dr_ref, dlw_ref, dk2_ref, dv_ref, daa_ref, dbb_ref, dstate):
        @pl.when(pl.program_id(0) == 0)
        def _():
            dstate[...] = jnp.zeros_like(dstate)

        keep, tri, _ = _chunk_masks()
        sc = [_scaled(r_ref[s], lw_ref[s], k2_ref[s], aa_ref[s], bb_ref[s], tri) for s in range(NSEQ)]
        ops = [_pairs_operands(*sc[s][3:]) for s in range(NSEQ)]
        x, yk = _both(lambda s: ops[s][0]), _both(lambda s: ops[s][1])
        m = _both(lambda s: [m_ref[0, s][:, i * 4 * L:(i + 1) * 4 * L] for i in range(NPAIR)])
        tinv = _both(lambda s: [t_ref[0, s][:, i * 2 * L:(i + 1) * 2 * L] for i in range(NPAIR)])
        uv = _both(lambda s: _pairs(uv_ref[0, s]))
        dyw = _both(lambda s: [_stack_pair(a) for a in _pairs(dy_ref[s])])
        s0 = _both(lambda s: _rows_of(hs_ref[0, s], PW))
        sn = _both(lambda s: _rows_of(hn_ref[0, s], PW))
        dsn = _both(lambda s: _rows_of(dstate[s], PW))
        egl = _both(lambda s: _pairs(sc[s][0][L - 1:L, :]))
        dx, dyk, dvw, dgl, ds0 = _pairs_bwd(x, yk, m, tinv, uv, s0, sn, egl, dyw, dsn, keep)
        for s in range(NSEQ):
            mine = slice(s * NPAIR, (s + 1) * NPAIR)
            eg, eng, egp, at, rt, bt, kt = sc[s]
            dstate[s] = jnp.concatenate(ds0[mine], axis=0)
            dv_ref[s] = jnp.concatenate([_fold_pair(a) for a in dvw[mine]], axis=-1)
            dat = jnp.concatenate([_fold_pair(a[:2 * L]) for a in dx[mine]], axis=-1)
            drt = jnp.concatenate([_fold_pair(a[2 * L:]) for a in dx[mine]], axis=-1)
            dbt = jnp.concatenate([_fold_pair(a[:2 * L]) for a in dyk[mine]], axis=-1)
            dkt = jnp.concatenate([_fold_pair(a[2 * L:]) for a in dyk[mine]], axis=-1)
            dg = drt * rt - dbt * bt - dkt * kt
            dg = dg + jnp.where(_iota2(dg.shape, 0) == L - 1, jnp.concatenate(dgl[mine], axis=-1), 0.0)
            dgp = dat * at
            dlw_ref[s] = _hdot_tn(tri, dg + dgp) - dgp
            dr_ref[s] = drt * eg
            daa_ref[s] = dat * egp
            dbb_ref[s] = dbt * eng
            dk2_ref[s] = dkt * eng

    blk = pl.BlockSpec((NSEQ, L, W), lambda c: (0, NC - 1 - c, 0))
    per_chunk = lambda shape: pl.BlockSpec((1, NSEQ) + shape, lambda c: (NC - 1 - c, 0, 0, 0))
    outs = pl.pallas_call(
        body, grid=(NC,), name="rwkv_bwd",
        in_specs=[blk] * 5 + [per_chunk(a.shape[2:]) for a in saved] + [blk],
        out_specs=[blk] * 6,
        out_shape=[jax.ShapeDtypeStruct((NSEQ, SEQ, W), F32)] * 6,
        scratch_shapes=[pltpu.VMEM((NSEQ,) + STATE_SHAPE, F32)],
        compiler_params=_cparams(("arbitrary",)),
    )(*[_seq_view(a) for a in (r, lw, k2, aa, bb)], *saved, _seq_view(dy))
    return [a.reshape(T, W) for a in outs]


def _post_math(y, r, k2, v, ga, o, gb, lng, lnb, rk, bd):
    mu = _headsum(y, bd) * (1.0 / HD)
    yc = y - mu
    var = _headsum(yc * yc, bd) * (1.0 / HD)
    yn = yc * lax.rsqrt(var + GN_EPS) * lng + lnb
    bonus = _headsum(r * k2 * rk, bd) * v
    return (yn + bonus) * _silu(ga), o * _silu(gb)


def even_post(y, r, k2, v, ga, o, gb, lng, lnb, rk, h, w_bf):
    tm = 512

    def body(y_ref, r_ref, k2_ref, v_ref, ga_ref, o_ref, gb_ref, lng_ref, lnb_ref, rk_ref, h_ref, w_ref,
             ho_ref, zt_ref):
        ya, yb = _post_math(y_ref[...], r_ref[...], k2_ref[...], v_ref[...], ga_ref[...], o_ref[...], gb_ref[...],
                            lng_ref[...], lnb_ref[...], rk_ref[...], _head_blockdiag())
        z = jnp.concatenate([ya.astype(BF16), yb.astype(BF16)], axis=-1)
        zt_ref[...] = z.T
        ho_ref[...] = h_ref[...] + jnp.dot(z, w_ref[...], preferred_element_type=F32)

    vec = _const_spec((1, W))
    return pl.pallas_call(
        body, grid=(T // tm,), name="even_post",
        in_specs=[_row_spec(tm, W)] * 7 + [vec] * 3 + [_row_spec(tm, D), _const_spec((D, D))],
        out_specs=[_row_spec(tm, D), _col_spec(D, tm)],
        out_shape=[jax.ShapeDtypeStruct((T, D), F32), jax.ShapeDtypeStruct((D, T), BF16)],
        compiler_params=_cparams(("parallel",), VMEM_BIG),
    )(y, r, k2, v, ga, o, gb, lng, lnb, rk, h, w_bf)


def even_post_bwd(y, r, k2, v, ga, o, gb, lng, lnb, rk, dh, zt_bf, w_bf, after=None):
    tm = 512
    extra_specs, extra = _after_operand(after)

    def body(y_ref, r_ref, k2_ref, v_ref, ga_ref, o_ref, gb_ref, lng_ref, lnb_ref, rk_ref, dh_ref, zt_ref, w_ref,
             *rest):
        dy_ref, dr_ref, dk2_ref, dv_ref, dga_ref, do_ref, dgb_ref, dlng_ref, dlnb_ref, drk_ref, dw_ref = rest[-11:]
        dzv = _out_proj_back(dh_ref, zt_ref, w_ref, dw_ref)
        bd = _head_blockdiag()
        _, vjp = jax.vjp(lambda *a: _post_math(*a, bd), y_ref[...], r_ref[...], k2_ref[...], v_ref[...], ga_ref[...],
                         o_ref[...], gb_ref[...], lng_ref[...], lnb_ref[...], rk_ref[...])
        dy, dr, dk2, dv, dga, do, dgb, dlng, dlnb, drk = vjp((dzv[:, 0:W], dzv[:, W:2 * W]))
        for ref, val in ((dy_ref, dy), (dr_ref, dr), (dk2_ref, dk2), (dv_ref, dv), (dga_ref, dga), (do_ref, do),
                         (dgb_ref, dgb)):
            ref[...] = val.astype(ref.dtype)

        @pl.when(pl.program_id(0) == 0)
        def _():
            for ref in (dlng_ref, dlnb_ref, drk_ref):
                ref[...] = jnp.zeros_like(ref)

        dlng_ref[...] += dlng
        dlnb_ref[...] += dlnb
        drk_ref[...] += drk

    vec = _const_spec((1, W))
    return pl.pallas_call(
        body, grid=(T // tm,), name="even_post_bwd",
        in_specs=[_row_spec(tm, W)] * 7 + [vec] * 3 + [_row_spec(tm, D), _col_spec(D, tm), _const_spec((D, D))]
        + extra_specs,
        out_specs=[_row_spec(tm, W)] * 7 + [vec] * 3 + [_const_spec((D, D))],
        out_shape=[jax.ShapeDtypeStruct((T, W), dt) for dt in (F32, F32, F32, F32, BF16, F32, BF16)]
        + [jax.ShapeDtypeStruct((1, W), F32)] * 3 + [jax.ShapeDtypeStruct((D, D), F32)],
        compiler_params=_cparams(("arbitrary",), VMEM_BIG),
    )(y, r, k2, v, ga, o, gb, lng, lnb, rk, dh, zt_bf, w_bf, *extra)


PADSEQ = SEQ + LEFT * L
ATT_SCALE = 1.0 / math.sqrt(HD)
ATT_Q = 4
WIN = BAND + (ATT_Q - 1) * L
ATT_STEPS = NC // ATT_Q
ATT_BIAS_SHAPE = (NPAIR, ATT_Q * 2 * L, WIN)
ATT_WINDOW_BIAS_SHAPE = (ATT_Q, NPAIR, 2 * L, WIN)


def _stack_chunks(a):
    return jnp.concatenate([_stack_pair(a[i * L:(i + 1) * L]) for i in range(ATT_Q)], axis=0)


def _unstack_chunks(a):
    return jnp.concatenate([_unstack_pair(a[i * 2 * L:(i + 1) * 2 * L]) for i in range(ATT_Q)], axis=0)


def _window_bias(b_ref):
    return [jnp.concatenate([b_ref[c, p] for c in range(ATT_Q)], axis=0) for p in range(NPAIR)]


def _att_probs(q2, kw, bias, step):
    valid = _iota2((1, WIN), 1) >= (LEFT - step * ATT_Q) * L
    s = [jnp.where(valid, _bdot_nt(a, b) * ATT_SCALE + bias[p], NEG) for p, (a, b) in enumerate(zip(q2, kw))]
    e = [jnp.exp(a - jnp.max(a, axis=-1, keepdims=True)) for a in s]
    return [a / jnp.sum(a, axis=-1, keepdims=True) for a in e]


def attention_fwd(q, kpad, vpad, bias):
    def body(q_ref, k_ref, v_ref, b_ref, o_ref):
        step = pl.program_id(1)
        start = pl.multiple_of(step * (ATT_Q * L), L)
        kw = _pairs(k_ref[pl.ds(start, WIN), :])
        vw = _pairs(v_ref[pl.ds(start, WIN), :])
        q2 = [_stack_chunks(a) for a in _pairs(q_ref[...])]
        p = _att_probs(q2, kw, _window_bias(b_ref), step)
        o_ref[...] = jnp.concatenate([_unstack_chunks(_bdot(a, b)) for a, b in zip(p, vw)], axis=-1)

    qblk = pl.BlockSpec((ATT_Q * L, W), lambda b, c: (b * ATT_STEPS + c, 0))
    kblk = pl.BlockSpec((PADSEQ, W), lambda b, c: (b, 0))
    return pl.pallas_call(
        body, grid=(NSEQ, ATT_STEPS), name="attention_fwd",
        in_specs=[qblk, kblk, kblk, _const_spec(ATT_WINDOW_BIAS_SHAPE)],
        out_specs=qblk, out_shape=jax.ShapeDtypeStruct((T, W), F32),
        compiler_params=_cparams(("parallel", "arbitrary")),
    )(q, kpad, vpad, bias)


def attention_bwd(q, kpad, vpad, bias, do):
    def body(q_ref, k_ref, v_ref, b_ref, do_ref, dq_ref, dko_ref, dvo_ref, db_ref, dk_ref, dv_ref):
        b = pl.program_id(0)
        c = pl.program_id(1)

        @pl.when(c == 0)
        def _():
            dk_ref[...] = jnp.zeros_like(dk_ref)
            dv_ref[...] = jnp.zeros_like(dv_ref)

        @pl.when((c == 0) & (b == 0))
        def _():
            db_ref[...] = jnp.zeros_like(db_ref)

        start = pl.multiple_of(c * (ATT_Q * L), L)
        kw = _pairs(k_ref[pl.ds(start, WIN), :])
        vw = _pairs(v_ref[pl.ds(start, WIN), :])
        q2 = [_stack_chunks(a) for a in _pairs(q_ref[...])]
        do2 = [_stack_chunks(a) for a in _pairs(do_ref[...].astype(BF16))]
        p = _att_probs(q2, kw, _window_bias(b_ref), c)
        dp = [_bdot_nt(a, b) for a, b in zip(do2, vw)]
        ds = [a * (d - jnp.sum(d * a, axis=-1, keepdims=True)) for a, d in zip(p, dp)]
        dss = [(a * ATT_SCALE).astype(BF16) for a in ds]
        dq_ref[...] = jnp.concatenate([_unstack_chunks(_bdot(a, b)) for a, b in zip(dss, kw)], axis=-1).astype(BF16)
        dk_ref[pl.ds(start, WIN), :] += jnp.concatenate([_bdot_tn(a, b) for a, b in zip(dss, q2)], axis=-1)
        dv_ref[pl.ds(start, WIN), :] += jnp.concatenate([_bdot_tn(a, b) for a, b in zip(p, do2)], axis=-1)
        for i in range(NPAIR):
            db_ref[i] += ds[i]

        @pl.when(c == ATT_STEPS - 1)
        def _():
            dko_ref[...] = dk_ref[LEFT * L:, :].astype(BF16)
            dvo_ref[...] = dv_ref[LEFT * L:, :].astype(BF16)

    qblk = pl.BlockSpec((ATT_Q * L, W), lambda b, c: (b * ATT_STEPS + c, 0))
    kblk = pl.BlockSpec((PADSEQ, W), lambda b, c: (b, 0))
    sblk = pl.BlockSpec((SEQ, W), lambda b, c: (b, 0))
    bblk = _const_spec(ATT_BIAS_SHAPE)
    return pl.pallas_call(
        body, grid=(NSEQ, ATT_STEPS), name="attention_bwd",
        in_specs=[qblk, kblk, kblk, _const_spec(ATT_WINDOW_BIAS_SHAPE), qblk],
        out_specs=[qblk, sblk, sblk, bblk],
        out_shape=[jax.ShapeDtypeStruct((T, W), BF16), jax.ShapeDtypeStruct((T, W), BF16),
                   jax.ShapeDtypeStruct((T, W), BF16), jax.ShapeDtypeStruct(ATT_BIAS_SHAPE, F32)],
        scratch_shapes=[pltpu.VMEM((PADSEQ, W), F32), pltpu.VMEM((PADSEQ, W), F32)],
        compiler_params=_cparams(("arbitrary", "arbitrary"), VMEM_BIG),
    )(q, kpad, vpad, bias, do)


NTAB = 2 * CLIP + 1
EXT = BAND + L


def _ext_onehot():
    n = _iota2((EXT, NTAB), 0)
    m = _iota2((EXT, NTAB), 1)
    return (jnp.clip(BAND - 1 - n, -CLIP, CLIP) + CLIP == m).astype(F32)


def bias_expand(table):
    def body(t_ref, o_ref):
        ext = _hdot_nt(t_ref[...], _ext_onehot())
        ext = jnp.concatenate([ext, jnp.zeros((NH, WIN - EXT), F32)], axis=-1)
        col = _iota2((NH, WIN), 1)
        for c in range(ATT_Q):
            inside = (col >= c * L) & (col < c * L + BAND)
            for i in range(L):
                shift = (c * L - (L - 1 - i)) % WIN
                o_ref[c, :, i, :] = jnp.where(inside, pltpu.roll(ext, shift, 1) if shift else ext, NEG)

    out = pl.pallas_call(body, name="bias_expand", out_shape=jax.ShapeDtypeStruct((ATT_Q, NH, L, WIN), F32))(table)
    return out.reshape(ATT_WINDOW_BIAS_SHAPE)


def bias_grad(dbias):
    def body(d_ref, o_ref):
        acc = jnp.zeros((NH, EXT), F32)
        zpad = jnp.zeros((NH, EXT - BAND), F32)
        for i in range(L):
            s = L - 1 - i
            row = jnp.concatenate([d_ref[:, i, :], zpad], axis=-1)
            acc = acc + (pltpu.roll(row, s, 1) if s else row)
        o_ref[...] = _hdot(acc, _ext_onehot())

    return pl.pallas_call(body, name="bias_grad", out_shape=jax.ShapeDtypeStruct((NH, NTAB), F32))(dbias)


def _group_cols(g):
    return slice(g * SGC, (g + 1) * SGC)


def _sg_norm(gv, lng, lnb):
    gc = gv - jnp.mean(gv, axis=-1, keepdims=True)
    rstd = lax.rsqrt(jnp.mean(gc * gc, axis=-1, keepdims=True) + LN_EPS)
    xhat = gc * rstd
    return xhat, rstd, xhat * lng + lnb


GMLP_BWD_CHUNKS = 2


def gmlp_fwd_loss(u, v, gate, lng, lnb, wm_bf, sgb_t, h, w_bf, g_final, target):
    tm = GMLP_BWD_CHUNKS * SGC

    def body(u_ref, v_ref, gt_ref, lng_ref, lnb_ref, wm_ref, sb_ref, h_ref, w_ref, g_ref, t_ref,
             dh_ref, loss_ref, dg_ref, zt_ref):
        zs = []
        for ch in range(GMLP_BWD_CHUNKS):
            rows = slice(ch * SGC, (ch + 1) * SGC)
            _, _, vln = _sg_norm(_gelu(v_ref[rows, :]), lng_ref[...], lnb_ref[...])
            vlb = vln.astype(BF16)
            zg = []
            for g in range(NG):
                cs = _group_cols(g)
                sv = jnp.dot(wm_ref[g], vlb[:, cs], preferred_element_type=F32) + sb_ref[:, g:g + 1]
                zg.append((_gelu(u_ref[rows, cs]) * sv * _silu(gt_ref[rows, cs])).astype(BF16))
            zs.append(jnp.concatenate(zg, axis=-1))
        z = jnp.concatenate(zs, axis=0)
        zt_ref[...] = z.T
        xv = h_ref[...] + jnp.dot(z, w_ref[...], preferred_element_type=F32)
        rstd = lax.rsqrt(jnp.mean(xv * xv, axis=-1, keepdims=True) + RMS_EPS)
        xhat = xv * rstd
        err = xhat * g_ref[...] - t_ref[...]
        part = 0.5 * jnp.sum(jnp.mean(err * err, axis=-1, keepdims=True), axis=0, keepdims=True)
        dout = err * (1.0 / D)

        @pl.when(pl.program_id(0) == 0)
        def _():
            loss_ref[...] = jnp.zeros_like(loss_ref)
            dg_ref[...] = jnp.zeros_like(dg_ref)

        loss_ref[...] += jnp.broadcast_to(part, loss_ref.shape)
        dg_ref[...] += jnp.sum(dout * xhat, axis=0, keepdims=True)
        dxh = dout * g_ref[...]
        dh_ref[...] = rstd * (dxh - xhat * jnp.mean(dxh * xhat, axis=-1, keepdims=True))

    return pl.pallas_call(
        body, grid=(T // tm,), name="gmlp_fwd_loss",
        in_specs=[_row_spec(tm, D)] * 3 + [_const_spec((1, D))] * 2
        + [_const_spec((NG, SGC, SGC)), _const_spec((SGC, NG)), _row_spec(tm, D), _const_spec((D, D)),
           _const_spec((1, D)), _row_spec(tm, D)],
        out_specs=[_row_spec(tm, D), _const_spec((8, 128)), _const_spec((1, D)), _col_spec(D, tm)],
        out_shape=[jax.ShapeDtypeStruct((T, D), F32), jax.ShapeDtypeStruct((8, 128), F32),
                   jax.ShapeDtypeStruct((1, D), F32), jax.ShapeDtypeStruct((D, T), BF16)],
        compiler_params=_cparams(("arbitrary",), VMEM_BIG),
    )(u, v, gate, lng, lnb, wm_bf, sgb_t, h, w_bf, g_final, target)


def gmlp_bwd(u, v, gate, lng, lnb, wm_bf, sgb_t, dh, zt_bf, w_bf):
    def body(u_ref, v_ref, gt_ref, lng_ref, lnb_ref, wm_ref, sb_ref, dh_ref, zt_ref, w_ref,
             du_ref, dv_ref, dgt_ref, dlng_ref, dlnb_ref, dwm_ref, dsb_ref, dw_ref):
        @pl.when(pl.program_id(0) == 0)
        def _():
            for ref in (dlng_ref, dlnb_ref, dwm_ref, dsb_ref):
                ref[...] = jnp.zeros_like(ref)

        dz = _out_proj_back(dh_ref, zt_ref, w_ref, dw_ref)
        sel = (_iota2((D, NG), 0) // SGC == _iota2((D, NG), 1)).astype(F32)
        for ch in range(GMLP_BWD_CHUNKS):
            rows = slice(ch * SGC, (ch + 1) * SGC)
            gv, dgv_dv = _gelu_both(v_ref[rows, :])
            xhat, rstd, vln = _sg_norm(gv, lng_ref[...], lnb_ref[...])
            vlb = vln.astype(BF16)
            dvln = []
            dsv_all = []
            for g in range(NG):
                cs = _group_cols(g)
                uu = u_ref[rows, cs]
                gg = gt_ref[rows, cs]
                dzz = dz[rows, cs]
                sv = jnp.dot(wm_ref[g], vlb[:, cs], preferred_element_type=F32) + sb_ref[:, g:g + 1]
                gu, dgu = _gelu_both(uu)
                sg, dsg = _silu_both(gg)
                dzgu = dzz * gu
                dsv = dzgu * sg
                dgt_ref[rows, cs] = (dzgu * sv * dsg).astype(BF16)
                du_ref[rows, cs] = (dzz * sv * sg * dgu).astype(BF16)
                dsb16 = dsv.astype(BF16)
                dvln.append(lax.dot_general(wm_ref[g], dsb16, (((0,), (0,)), ((), ())), preferred_element_type=F32))
                dwm_ref[g] += lax.dot_general(dsb16, vlb[:, cs], (((1,), (1,)), ((), ())),
                                              preferred_element_type=F32)
                dsv_all.append(dsv)
            dvl = jnp.concatenate(dvln, axis=-1)
            dsb_ref[...] += _hdot(jnp.concatenate(dsv_all, axis=-1), sel)
            dlng_ref[...] += jnp.sum(dvl * xhat, axis=0, keepdims=True)
            dlnb_ref[...] += jnp.sum(dvl, axis=0, keepdims=True)
            dxh = dvl * lng_ref[...]
            dgv = rstd * (dxh - jnp.mean(dxh, axis=-1, keepdims=True)
                          - xhat * jnp.mean(dxh * xhat, axis=-1, keepdims=True))
            dv_ref[rows, :] = (dgv * dgv_dv).astype(BF16)

    tm = GMLP_BWD_CHUNKS * SGC
    return pl.pallas_call(
        body, grid=(T // tm,), name="gmlp_bwd",
        in_specs=[_row_spec(tm, D)] * 3 + [_const_spec((1, D))] * 2
        + [_const_spec((NG, SGC, SGC)), _const_spec((SGC, NG)), _row_spec(tm, D), _col_spec(D, tm),
           _const_spec((D, D))],
        out_specs=[_row_spec(tm, D)] * 3 + [_const_spec((1, D))] * 2
        + [_const_spec((NG, SGC, SGC)), _const_spec((SGC, NG)), _const_spec((D, D))],
        out_shape=[jax.ShapeDtypeStruct((T, D), BF16)] * 3 + [jax.ShapeDtypeStruct((1, D), F32)] * 2
        + [jax.ShapeDtypeStruct((NG, SGC, SGC), F32), jax.ShapeDtypeStruct((SGC, NG), F32),
           jax.ShapeDtypeStruct((D, D), F32)],
        compiler_params=_cparams(("arbitrary",), VMEM_BIG),
    )(u, v, gate, lng, lnb, wm_bf, sgb_t, dh, zt_bf, w_bf)


NCHIP = 4
NDEV = 8
ANY = pl.BlockSpec(memory_space=pl.ANY)


HBM = pl.BlockSpec(memory_space=pltpu.HBM)
SEM = pl.BlockSpec(memory_space=pltpu.SEMAPHORE)
EFFECT = pltpu.SideEffectType.DATAFLOW_SIDE_EFFECTING


CHIPS, EVERY, SIBLING = "chips", "every", "sibling"
SLOTS = {CHIPS: NCHIP, EVERY: NDEV, SIBLING: 2}


def _peers(scope):
    x, y, c = lax.axis_index("x"), lax.axis_index("y"), lax.axis_index("c")
    if scope == SIBLING:
        return [((x, y, 1 - c), 1 - c)], c
    if scope == CHIPS:
        return [((px, py, c), 2 * px + py) for px, py in ((1 - x, y), (x, 1 - y), (1 - x, 1 - y))], 2 * x + y
    out = []
    for j in range(1, NDEV):
        px, py, pc = x ^ (j >> 2), y ^ ((j >> 1) & 1), c ^ (j & 1)
        out.append(((px, py, pc), 4 * px + 2 * py + pc))
    return out, 4 * x + 2 * y + c


def _send_copies(src, land, send, recv, scatter, scope, starting):
    peers, me = _peers(scope)
    copies = []
    for t in range(len(src)):
        for j, (dev, slot) in enumerate(peers):
            k = t * len(peers) + j
            copies.append(pltpu.make_async_remote_copy(
                src_ref=src[t].at[slot] if scatter else src[t], dst_ref=land[t].at[me if starting else slot],
                send_sem=send.at[k], recv_sem=recv.at[k], device_id=dev, device_id_type=MESH))
    return copies


def _own_copies(src, land, sems, scatter, scope):
    _, me = _peers(scope)
    return [pltpu.make_async_copy(src[t].at[me] if scatter else src[t], land[t].at[me], sems.at[t])
            for t in range(len(src))]


def send_start(srcs, scatter, scope, name, after=None):
    n = len(srcs)
    slots = SLOTS[scope]
    nsem = n * (slots - 1)
    extra_specs, extra = _after_operand(after)
    lands = [pltpu.HBM(a.shape if scatter else (slots,) + a.shape, a.dtype) for a in srcs]

    def body(*refs):
        first_out = n + len(extra)
        send, recv, own = refs[first_out:first_out + 3]
        src, land = refs[:n], refs[first_out + 3 + n:first_out + 3 + 2 * n]
        for cp in _own_copies(src, land, own, scatter, scope):
            cp.start()
        for cp in _send_copies(src, land, send, recv, scatter, scope, True):
            cp.start()
        refs[-1][...] = jnp.zeros_like(refs[-1])

    out = pl.pallas_call(
        body, name=name,
        out_shape=(pltpu.SemaphoreType.DMA((nsem,)), pltpu.SemaphoreType.DMA((nsem,)), pltpu.SemaphoreType.DMA((n,)),
                   *[pltpu.HBM(a.shape, a.dtype) for a in srcs], *lands, jax.ShapeDtypeStruct((8, 128), F32)),
        in_specs=[HBM] * n + extra_specs,
        out_specs=(SEM, SEM, SEM, *[HBM] * (2 * n), pl.BlockSpec(memory_space=pltpu.VMEM)),
        input_output_aliases={i: 3 + i for i in range(n)},
        compiler_params=pltpu.CompilerParams(has_side_effects=EFFECT),
    )(*[pltpu.with_memory_space_constraint(a, pltpu.HBM) for a in srcs], *extra)
    return out[0], out[1], out[2], list(out[3:3 + n]), list(out[3 + n:3 + 2 * n]), out[-1]


def send_wait(started, after, scatter, scope, name):
    send, recv, own, srcs, lands, _ = started
    n = len(srcs)

    def body(*refs):
        src, land = refs[:n], refs[n:2 * n]
        for cp in _own_copies(src, land, refs[2 * n + 2], scatter, scope):
            cp.wait()
        for cp in _send_copies(src, land, refs[2 * n], refs[2 * n + 1], scatter, scope, False):
            cp.wait_send()
            cp.wait_recv()

    arrs = list(srcs) + list(lands)
    out = pl.pallas_call(
        body, name=name, out_shape=tuple(pltpu.HBM(a.shape, a.dtype) for a in arrs),
        in_specs=[HBM] * (2 * n) + [SEM, SEM, SEM, ANY], out_specs=tuple([HBM] * (2 * n)),
        input_output_aliases={i: i for i in range(2 * n)},
        compiler_params=pltpu.CompilerParams(has_side_effects=EFFECT),
    )(*arrs, send, recv, own, after)
    return list(out[n:])


def gather_weights(arrs, split):
    n = len(arrs)

    def body(*refs):
        ins, outs = refs[:n], refs[n:2 * n]
        send1, recv1, send2, recv2, loc = refs[2 * n:]
        x, y, c = lax.axis_index("x"), lax.axis_index("y"), lax.axis_index("c")
        me = 2 * x + y
        sibling = (x, y, 1 - c)
        peers = [(1 - x, y), (x, 1 - y), (1 - x, 1 - y)]

        def rows_of(t, core):
            half = arrs[t].shape[0] // 2
            return pl.ds(core * half, half)

        def part(ref, t, core):
            return ref.at[rows_of(t, core)] if split[t] else ref

        local = [pltpu.make_async_copy(ins[t], outs[t].at[me], loc.at[t]) for t in range(n)]
        for cp in local:
            cp.start()
        first = []
        for t in range(n):
            for j, (px, py) in enumerate(peers):
                first.append(pltpu.make_async_remote_copy(
                    src_ref=part(ins[t], t, c), dst_ref=part(outs[t].at[me], t, c), send_sem=send1.at[t, j],
                    recv_sem=recv1.at[t, j], device_id=(px, py, c), device_id_type=MESH))
        for cp in first:
            cp.start()
        passed = []
        for t in range(n):
            for j, (px, py) in enumerate(peers):
                landed = part(outs[t].at[2 * px + py], t, c)
                pltpu.make_async_remote_copy(
                    src_ref=landed, dst_ref=landed, send_sem=send1.at[t, j], recv_sem=recv1.at[t, j],
                    device_id=(x, y, c), device_id_type=MESH).wait_recv()
                if split[t]:
                    cp = pltpu.make_async_remote_copy(
                        src_ref=landed, dst_ref=landed, send_sem=send2.at[t, j], recv_sem=recv2.at[t, j],
                        device_id=sibling, device_id_type=MESH)
                    cp.start()
                    passed.append(cp)
        for t in range(n):
            for j, (px, py) in enumerate(peers):
                if split[t]:
                    other = part(outs[t].at[2 * px + py], t, 1 - c)
                    pltpu.make_async_remote_copy(
                        src_ref=other, dst_ref=other, send_sem=send2.at[t, j], recv_sem=recv2.at[t, j],
                        device_id=(x, y, c), device_id_type=MESH).wait_recv()
        for cp in first + passed:
            cp.wait_send()
        for cp in local:
            cp.wait()

    return pl.pallas_call(
        body, name="gather_weights", in_specs=[ANY] * n, out_specs=[ANY] * n,
        out_shape=[jax.ShapeDtypeStruct((NCHIP,) + a.shape, a.dtype) for a in arrs],
        scratch_shapes=[pltpu.SemaphoreType.DMA((n, 3))] * 4 + [pltpu.SemaphoreType.DMA((n,))],
    )(*arrs)


def _adam_math(g, w, m, v):
    m = ADAM_B1 * m + (1.0 - ADAM_B1) * g
    v = ADAM_B2 * v + (1.0 - ADAM_B2) * (g * g)
    m_hat = m / (1.0 - ADAM_B1 ** ADAM_STEP)
    v_hat = v / (1.0 - ADAM_B2 ** ADAM_STEP)
    delta = -ADAM_LR * (m_hat / (jnp.sqrt(v_hat) + ADAM_EPS) + ADAM_WD * w)
    return delta, m, v


def _rows_tile(rows):
    return rows if rows <= 256 else 256


def sum_chips(parts, name):
    _, rows, cols = parts.shape
    tr = _rows_tile(rows)

    def body(p_ref, o_ref):
        acc = p_ref[0].astype(F32)
        for s in range(1, NCHIP):
            acc = acc + p_ref[s].astype(F32)
        o_ref[...] = acc

    return pl.pallas_call(
        body, grid=(rows // tr,), name=name,
        in_specs=[pl.BlockSpec((NCHIP, tr, cols), lambda i: (0, i, 0))],
        out_specs=pl.BlockSpec((tr, cols), lambda i: (i, 0)),
        out_shape=jax.ShapeDtypeStruct((rows, cols), F32),
        compiler_params=_cparams(("parallel",)),
    )(parts)


def sum_chips_small(parts, name):
    n = len(parts)

    def body(*refs):
        for p_ref, o_ref in zip(refs[:n], refs[n:]):
            acc = p_ref[0]
            for s in range(1, NCHIP):
                acc = acc + p_ref[s]
            o_ref[...] = acc

    return pl.pallas_call(body, name=name, out_shape=[jax.ShapeDtypeStruct(p.shape[1:], F32) for p in parts])(*parts)


def adam_shard_small(items, name):
    n = len(items)

    def body(*refs):
        for t in range(n):
            p_ref, w_ref, m_ref, v_ref = refs[4 * t:4 * t + 4]
            g_ref, d_ref, mo_ref, vo_ref = refs[4 * n + 4 * t:4 * n + 4 * t + 4]
            g = (p_ref[0] + p_ref[1]).reshape(w_ref.shape)
            g_ref[...] = g
            d_ref[...], mo_ref[...], vo_ref[...] = _adam_math(g, w_ref[...], m_ref[...], v_ref[...])

    out = pl.pallas_call(
        body, name=name, out_shape=[jax.ShapeDtypeStruct(it[1].shape, F32) for it in items for _ in range(4)],
    )(*[a for it in items for a in it])
    return [out[4 * t:4 * t + 4] for t in range(n)]


def adam_shard(parts, w, m, v, name):
    _, rows, cols = parts.shape
    tr = _rows_tile(rows)
    lead = w.ndim == 3

    def body(p_ref, w_ref, m_ref, v_ref, g_ref, d_ref, mo_ref, vo_ref):
        g = p_ref[0] + p_ref[1]
        g = g[None] if lead else g
        g_ref[...] = g
        d_ref[...], mo_ref[...], vo_ref[...] = _adam_math(g, w_ref[...], m_ref[...], v_ref[...])

    both = pl.BlockSpec((2, tr, cols), lambda i: (0, i, 0))
    spec = pl.BlockSpec((1, tr, cols), lambda i: (0, i, 0)) if lead else pl.BlockSpec((tr, cols), lambda i: (i, 0))
    return pl.pallas_call(
        body, grid=(rows // tr,), name=name, in_specs=[both] + [spec] * 3, out_specs=[spec] * 4,
        out_shape=[jax.ShapeDtypeStruct(w.shape, F32)] * 4,
        compiler_params=_cparams(("parallel",)),
    )(parts, w, m, v)


def adam_shard_halves_t(halves, wt, mt, vt, name):
    _, hrows, cols = halves.shape
    tr = _rows_tile(hrows)
    per_half = hrows // tr

    def body(h_ref, w_ref, m_ref, v_ref, g_ref, d_ref, mo_ref, vo_ref):
        g = h_ref[0].T[None]
        g_ref[...] = g
        d_ref[...], mo_ref[...], vo_ref[...] = _adam_math(g, w_ref[...], m_ref[...], v_ref[...])

    half = pl.BlockSpec((1, tr, cols), lambda h, i: (h, i, 0))
    spec = pl.BlockSpec((1, cols, tr), lambda h, i: (0, 0, h * per_half + i))
    return pl.pallas_call(
        body, grid=(2, per_half), name=name, in_specs=[half] + [spec] * 3, out_specs=[spec] * 4,
        out_shape=[jax.ShapeDtypeStruct(wt.shape, F32)] * 4,
        compiler_params=_cparams(("parallel", "parallel")),
    )(halves, wt, mt, vt)


def adam_replicated(gathered, params, name):
    flat = []
    for i, p in enumerate(params):
        if isinstance(p, list):
            off = 0
            for wmv in p:
                n = gathered[i].shape[-1] - off if wmv[0] is None else wmv[0].shape[-1]
                flat.append((i, (off, n), wmv))
                off += n
        else:
            flat.append((i, None, p))
    ins = [a for _, _, wmv in flat for a in wmv if a is not None]
    ng = len(gathered)

    def body(*refs):
        g_refs = refs[:ng]
        in_refs = list(refs[ng:ng + len(ins)])
        out_refs = list(refs[ng + len(ins):])
        sums = []
        for r in g_refs:
            g = r[0]
            for d in range(1, NDEV):
                g = g + r[d]
            sums.append(g)
        for i, lanes, wmv in flat:
            g = sums[i] if lanes is None else sums[i][:, lanes[0]:lanes[0] + lanes[1]]
            out_refs.pop(0)[...] = g
            if wmv[0] is not None:
                w_ref, m_ref, v_ref = in_refs.pop(0), in_refs.pop(0), in_refs.pop(0)
                d_ref, mo_ref, vo_ref = out_refs.pop(0), out_refs.pop(0), out_refs.pop(0)
                d_ref[...], mo_ref[...], vo_ref[...] = _adam_math(g, w_ref[...], m_ref[...], v_ref[...])

    out_shape = []
    for i, lanes, wmv in flat:
        shape = gathered[i].shape[1:] if lanes is None else (1, lanes[1])
        out_shape += [jax.ShapeDtypeStruct(shape, F32)] * (4 if wmv[0] is not None else 1)
    outs = list(pl.pallas_call(body, name=name, out_shape=out_shape)(*gathered, *ins))
    return [[outs.pop(0) for _ in range(4 if wmv[0] is not None else 1)] for _, _, wmv in flat]


EVEN_SPLITS = (SHIFT, W, W, W, W, W)
ODD_SPLITS = (D, D, D)


def _cols_to_chips(a):
    rows, cols = a.shape
    return a.reshape(rows, NCHIP, cols // NCHIP).transpose(1, 0, 2)


def _chips_to_cols(a):
    _, rows, n = a.shape
    return a.transpose(1, 0, 2).reshape(rows, NCHIP * n)


def kernel(x, norm_g, w_in_e, shift_mu, rw_w0, rw_w2, rw_a0, rw_a2, rw_kk, rw_ka, rw_rk, rw_lnx_g, rw_lnx_b, att_bias, w_out_e, w_in_o, sg_ln_g, sg_ln_b, sg_w, sg_b, w_out_o, final_g, loss_target, m_norm_g, m_w_in_e, m_shift_mu, m_rw_w0, m_rw_w2, m_rw_a0, m_rw_a2, m_rw_kk, m_rw_ka, m_rw_rk, m_rw_lnx_g, m_rw_lnx_b, m_att_bias, m_w_out_e, m_w_in_o, m_sg_ln_g, m_sg_ln_b, m_sg_w, m_sg_b, m_w_out_o, m_final_g, v_norm_g, v_w_in_e, v_shift_mu, v_rw_w0, v_rw_w2, v_rw_a0, v_rw_a2, v_rw_kk, v_rw_ka, v_rw_rk, v_rw_lnx_g, v_rw_lnx_b, v_att_bias, v_w_out_e, v_w_in_o, v_sg_ln_g, v_sg_ln_b, v_sg_w, v_sg_b, v_w_out_o, v_final_g):
    x2 = x.reshape(T, D)
    tgt = loss_target.reshape(T, D)

    gathered = gather_weights(
        [jnp.swapaxes(w_in_e[0], 0, 1).astype(BF16), jnp.concatenate([rw_w2[0], rw_a2[0]], axis=0),
         jnp.concatenate([sg_ln_g, sg_ln_b], axis=0)], [True, True, False])
    wie = gathered[0].reshape(EVEN_IN, D)
    w2 = _chips_to_cols(gathered[1][:, :LORA])
    a2 = _chips_to_cols(gathered[1][:, LORA:])
    sglg = _chips_to_cols(gathered[2][:, 0:1])
    sglb = _chips_to_cols(gathered[2][:, 1:2])

    late = [w_out_e[0].astype(BF16), w_in_o[0].astype(BF16), w_out_o[0].astype(BF16)]
    late_started = send_start(late, False, CHIPS, "late_weights_start", after=gathered[0])

    def late_weights(after):
        woe, wio, woo = send_wait(late_started, after, False, CHIPS, "late_weights_wait")
        return woe.reshape(D, D), _chips_to_cols(wio), woo.reshape(D, D)

    def scatter_start(grads, name):
        return send_start([g_.astype(BF16) if g_.shape[-1] >= W else g_ for g_ in grads], True, CHIPS, name)

    started = {}

    def on_odd_grads(d_woo, d_wio):
        started["odd"] = scatter_start([d_woo.reshape(NCHIP, D // NCHIP, D), d_wio], "odd_grads_start")
        return started["odd"][-1]

    def on_even_grads(big_g):
        d_wie_half, d_woe, _, _, d_w2, d_a2, d_sglg, d_sglb = big_g
        blocks = [d_wie_half, d_woe.reshape(NCHIP, D // NCHIP, D), _cols_to_chips(d_w2), _cols_to_chips(d_a2),
                  _cols_to_chips(d_sglg), _cols_to_chips(d_sglb)]
        started["even"] = scatter_start(blocks, "even_grads_start")
        return started["even"][-1]

    def on_small_grads(layer, grads):
        if layer == "odd":
            d_sg_w, d_sg_b, d_final, d_g1 = grads
            mine = [d_sg_w.reshape(NG * SGC, SGC), d_sg_b, jnp.concatenate([d_final, d_g1], axis=1)]
        else:
            mine = [grads[-2], jnp.concatenate(grads[:-2] + grads[-1:], axis=1)]
        started[layer + "_small"] = send_start(mine, False, EVERY, layer + "_small_grads_start")
        return started[layer + "_small"][-1]

    loss_part, dx, _, _ = _local_step(
        x2, tgt, wie, late_weights, w2, a2, sglg, sglb, norm_g, shift_mu, rw_w0, rw_a0, rw_kk, rw_ka, rw_rk,
        rw_lnx_g, rw_lnx_b, att_bias, sg_w, sg_b, final_g, first_after=late_started[-1], on_odd_grads=on_odd_grads,
        on_even_grads=on_even_grads, on_small_grads=on_small_grads)
    wmv = {"w_in_e": tuple(jnp.swapaxes(a, 1, 2) for a in (w_in_e, m_w_in_e, v_w_in_e)),
           "w_out_e": (w_out_e, m_w_out_e, v_w_out_e),
           "w_in_o": (w_in_o, m_w_in_o, v_w_in_o), "w_out_o": (w_out_o, m_w_out_o, v_w_out_o),
           "rw_w2": (rw_w2, m_rw_w2, v_rw_w2), "rw_a2": (rw_a2, m_rw_a2, v_rw_a2),
           "sg_ln_g": (sg_ln_g, m_sg_ln_g, v_sg_ln_g), "sg_ln_b": (sg_ln_b, m_sg_ln_b, v_sg_ln_b)}
    sharded = {}

    def sum_and_swap(names, landed, tag):
        nbig = sum(p_.dtype == BF16 for p_ in landed)
        partial = [sum_chips(p_, "sum_" + nm) for p_, nm in zip(landed[:nbig], names)]
        if nbig < len(names):
            partial += sum_chips_small(landed[nbig:], "sum_small_" + tag)
        return send_start(partial, False, SIBLING, "swap_partials_" + tag + "_start")

    def update(names, swap_started, after, tag):
        both = send_wait(swap_started, after, False, SIBLING, "swap_partials_" + tag + "_wait")
        nbig = sum(p_.shape[-1] >= W for p_ in both)
        for nm, parts in zip(names[:nbig], both):
            if nm == "w_in_e":
                res = adam_shard_halves_t(parts, *wmv[nm], "adam_" + nm)
                sharded[nm] = [jnp.swapaxes(a, 1, 2) for a in res]
            else:
                sharded[nm] = adam_shard(parts, *wmv[nm], "adam_" + nm)
        if nbig < len(names):
            items = [(parts, *wmv[nm]) for nm, parts in zip(names, both)][nbig:]
            for nm, res in zip(names[nbig:], adam_shard_small(items, "adam_small_" + tag)):
                sharded[nm] = res

    odd_names = ["w_out_o", "w_in_o"]
    even_names = ["w_in_e", "w_out_e", "rw_w2", "rw_a2", "sg_ln_g", "sg_ln_b"]
    odd_landed = send_wait(started["odd"], started["even_small"][-1], True, CHIPS, "odd_grads_wait")
    odd_swap = sum_and_swap(odd_names, odd_landed, "odd")
    done = odd_swap[-1]

    def wmv_of(*arrs, view=lambda a: a):
        return tuple(view(a) for a in arrs)

    vec = lambda a: a.reshape(1, -1)
    groups = {
        "odd": (["sg_w", "sg_b", "final_g", "norm_g1"],
                [wmv_of(sg_w, m_sg_w, v_sg_w, view=lambda a: a.reshape(NG * SGC, SGC)),
                 wmv_of(sg_b, m_sg_b, v_sg_b, view=lambda a: a[0]),
                 [wmv_of(final_g, m_final_g, v_final_g, view=vec),
                  wmv_of(norm_g, m_norm_g, v_norm_g, view=lambda a: a[1:2])]]),
        "even": (["att_bias", "norm_g0", "shift_mu", "rw_w0", "rw_a0", "rw_kk", "rw_ka", "rw_rk", "rw_lnx_g",
                  "rw_lnx_b", "loss"],
                 [wmv_of(att_bias, m_att_bias, v_att_bias, view=lambda a: a[0]),
                  [wmv_of(norm_g, m_norm_g, v_norm_g, view=lambda a: a[0:1]),
                   wmv_of(shift_mu, m_shift_mu, v_shift_mu), wmv_of(rw_w0, m_rw_w0, v_rw_w0),
                   wmv_of(rw_a0, m_rw_a0, v_rw_a0), wmv_of(rw_kk, m_rw_kk, v_rw_kk), wmv_of(rw_ka, m_rw_ka, v_rw_ka),
                   wmv_of(rw_rk, m_rw_rk, v_rw_rk, view=vec), wmv_of(rw_lnx_g, m_rw_lnx_g, v_rw_lnx_g),
                   wmv_of(rw_lnx_b, m_rw_lnx_b, v_rw_lnx_b), (None, None, None)]]),
    }
    rep = {}
    for layer in ("odd", "even"):
        nms, params = groups[layer]
        gathered_g = send_wait(started[layer + "_small"], done, False, EVERY, layer + "_small_grads_wait")
        for nm, res in zip(nms, adam_replicated(gathered_g, params, "adam_" + layer + "_small")):
            rep[nm] = res
        done = rep[nms[0]][0]
    native = {"sg_w": sg_w.shape, "sg_b": sg_b.shape, "final_g": final_g.shape, "rw_rk": rw_rk.shape,
              "att_bias": att_bias.shape}
    for nm, shape in native.items():
        rep[nm] = [a.reshape(shape) for a in rep[nm]]
    rep["norm_g"] = [jnp.concatenate([a, b], axis=0) for a, b in zip(rep["norm_g0"], rep["norm_g1"])]
    even_landed = send_wait(started["even"], done, True, CHIPS, "even_grads_wait")
    even_swap = sum_and_swap(even_names, even_landed, "even")
    update(odd_names, odd_swap, even_swap[-1], "odd")
    update(even_names, even_swap, sharded["w_in_o"][0], "even")

    order = ["norm_g", "w_in_e", "shift_mu", "rw_w0", "rw_w2", "rw_a0", "rw_a2", "rw_kk", "rw_ka", "rw_rk",
             "rw_lnx_g", "rw_lnx_b", "att_bias", "w_out_e", "w_in_o", "sg_ln_g", "sg_ln_b", "sg_w", "sg_b",
             "w_out_o", "final_g"]
    results = {**sharded, **rep}
    outs = [rep["loss"][0][0, 0], dx.reshape(NSEQ, SEQ, D)]
    for kind in range(4):
        outs += [results[nm][kind] for nm in order]
    return tuple(outs)


def _local_step(x2, tgt, wie_t, late_weights, w2, a2, sglg, sglb, norm_g, shift_mu, rw_w0, rw_a0, rw_kk, rw_ka, rw_rk,
                rw_lnx_g, rw_lnx_b, att_bias, sg_w, sg_b, final_g, first_after=None, on_odd_grads=None,
                on_even_grads=None, on_small_grads=None):
    zl = jnp.zeros((LORA, W), F32)
    w2x = jnp.concatenate([w2, zl], axis=0)
    a2x = jnp.concatenate([zl, a2], axis=0)
    rk = rw_rk.reshape(1, W)
    pos = np.arange(SGC)
    sg_mask = jnp.asarray(((pos[None, :] // L) <= (pos[:, None] // L)).astype(np.float32))
    wm = (sg_w[0] * sg_mask[None]).astype(BF16)
    sgb_t = sg_b[0].T

    xn0, ps, ga, q, kb, vb, gb = ln_in_proj(x2, norm_g[0:1], wie_t, EVEN_SPLITS, "in_proj_even", after=first_after,
                                            w_t=True, bf16_pieces=(2, 3, 4))
    r, lw, k2, v, aa, bb = even_prep(ps, shift_mu, rw_w0, w2x, rw_a0, a2x, rw_kk, rw_ka)
    y, rw_saved = rwkv_fwd(r, lw, k2, v, aa, bb)
    bias = bias_expand(att_bias[0])

    def padded(a):
        return jnp.pad(a.astype(BF16).reshape(NSEQ, SEQ, W), ((0, 0), (LEFT * L, 0), (0, 0))).reshape(NSEQ * PADSEQ, W)

    kpad, vpad = padded(kb), padded(vb)
    o = attention_fwd(q, kpad, vpad, bias)
    woe, wio, woo = late_weights(o)
    h1, zt = even_post(y, r, k2, v, ga, o, gb, rw_lnx_g, rw_lnx_b, rk, x2, woe)
    xn1, u, vv, gt = ln_in_proj(h1, norm_g[1:2], wio, ODD_SPLITS, "in_proj_odd")
    dh2, loss_part, d_final_g, z2t = gmlp_fwd_loss(u, vv, gt, sglg, sglb, wm, sgb_t, h1, woo, final_g[None], tgt)

    du, dvv, dgt, d_sglg, d_sglb, d_wm, d_sgb_t, d_woo = gmlp_bwd(u, vv, gt, sglg, sglb, wm, sgb_t, dh2, z2t, woo)
    dp_odd = [du, dvv, dgt]
    d_wio = matmul_acc_chips(xn1, dp_odd, "in_proj_odd_dw")
    token = on_odd_grads(d_woo, d_wio) if on_odd_grads else None
    dh1, d_g1 = in_proj_bwd_x(h1, norm_g[1:2], wio, dp_odd, dh2, "in_proj_odd_bwd", after=token)
    odd_small = [d_wm * sg_mask[None], d_sgb_t.T, d_final_g, d_g1]
    token = on_small_grads("odd", odd_small) if on_small_grads else None
    dy, dr2, dk22, dv2, dga, do, dgb, d_lng, d_lnb, d_rk, d_woe = even_post_bwd(
        y, r, k2, v, ga, o, gb, rw_lnx_g, rw_lnx_b, rk, dh1, zt, woe, after=token)
    dq, dkb, dvb, dbias = attention_bwd(q, kpad, vpad, bias, do)
    dbias = sum(dbias[:, i * 2 * L:(i + 1) * 2 * L, i * L:i * L + BAND] for i in range(ATT_Q))
    d_att_bias = bias_grad(dbias.reshape(NH, L, BAND))
    dr, dlw, dk2, dv, daa, dbb = rwkv_bwd(r, lw, k2, aa, bb, rw_saved, dy)
    dps, d_mu, d_w0, d_w2x, d_a0, d_a2x, d_kk, d_ka = even_prep_bwd(
        ps, shift_mu, rw_w0, w2x, rw_a0, a2x, rw_kk, rw_ka, dr, dlw, dk2, dv, daa, dbb, dr2, dk22, dv2)
    dp_even = [dps, dga, dq, dkb, dvb, dgb]
    d_wie = matmul_acc_chips(xn0, dp_even, "in_proj_even_dw", add_cores=on_even_grads is not None)
    big_g = (d_wie, d_woe, d_wio, d_woo, d_w2x[:LORA], d_a2x[LORA:], d_sglg, d_sglb)
    token = on_even_grads(big_g) if on_even_grads else None
    dx, d_g0 = in_proj_bwd_x(x2, norm_g[0:1], wie_t, dp_even, dh1, "in_proj_even_bwd", after=token, w_t=True)
    even_small = [d_g0, d_mu, d_w0, d_a0, d_kk, d_ka, d_rk, d_lng, d_lnb, d_att_bias]
    if on_small_grads:
        on_small_grads("even", even_small + [loss_part[0:1, :]])
    rep_g = [jnp.concatenate([d_g0, d_g1], axis=0)] + even_small[1:] + odd_small[:3]
    return loss_part[0, 0], dx, big_g, rep_g
```

```python
import functools
import math

import jax
import jax.numpy as jnp
import numpy as np
from jax import lax
from jax.experimental import pallas as pl
from jax.experimental.pallas import tpu as pltpu

F32 = jnp.float32
BF16 = jnp.bfloat16
HI = lax.Precision.HIGHEST

D = 1024
SEQ = 2048
NSEQ = 2
T = NSEQ * SEQ
HD = 64
NH = 8
W = 512
SHIFT = 1664
LORA = 64
EVEN_IN = 4224
ODD_IN = 3072
L = 64
NC = SEQ // L
LEFT = 8
BAND = (LEFT + 1) * L
CLIP = 128
SGC = 128
NG = 8
RMS_EPS = 1e-6
LN_EPS = 1e-5
GN_EPS = 64e-5
NEG = -1e30
VMEM_BIG = 56 * 1024 * 1024

ADAM_LR = 0.001
ADAM_B1 = 0.9
ADAM_B2 = 0.999
ADAM_EPS = 1e-08
ADAM_WD = 0.01
ADAM_STEP = 10

MESH = pl.DeviceIdType.MESH


def _bdot(a, b):
    return jnp.dot(a.astype(BF16), b.astype(BF16), preferred_element_type=F32)


def _bdot_nt(a, b):
    return lax.dot_general(a.astype(BF16), b.astype(BF16), (((1,), (1,)), ((), ())), preferred_element_type=F32)


def _bdot_tn(a, b):
    return lax.dot_general(a.astype(BF16), b.astype(BF16), (((0,), (0,)), ((), ())), preferred_element_type=F32)


def _hdot(a, b):
    return jnp.dot(a, b, precision=HI, preferred_element_type=F32)


def _hdot_nt(a, b):
    return lax.dot_general(a, b, (((1,), (1,)), ((), ())), precision=HI, preferred_element_type=F32)


def _hdot_tn(a, b):
    return lax.dot_general(a, b, (((0,), (0,)), ((), ())), precision=HI, preferred_element_type=F32)


def _iota2(shape, dim):
    return lax.broadcasted_iota(jnp.int32, shape, dim)


def _head_blockdiag():
    r = _iota2((2 * HD, 2 * HD), 0) // HD
    c = _iota2((2 * HD, 2 * HD), 1) // HD
    return (r == c).astype(BF16)


def _headsum_impl(x, bd):
    hi = x.astype(BF16)
    mid = (x - hi.astype(F32)).astype(BF16)
    n = bd.shape[0]
    out = [jnp.dot(hi[:, i:i + n], bd, preferred_element_type=F32) + jnp.dot(mid[:, i:i + n], bd, preferred_element_type=F32)
           for i in range(0, x.shape[1], n)]
    return jnp.concatenate(out, axis=-1)


@jax.custom_vjp
def _headsum(x, bd):
    return _headsum_impl(x, bd)


def _headsum_fwd(x, bd):
    return _headsum_impl(x, bd), bd


def _headsum_bwd(bd, ct):
    return _headsum_impl(ct, bd), None


_headsum.defvjp(_headsum_fwd, _headsum_bwd)


def _silu(x):
    return x * jax.nn.sigmoid(x)


def _dsilu(x):
    s = jax.nn.sigmoid(x)
    return s * (1.0 + x * (1.0 - s))


_GELU_C = math.sqrt(2.0 / math.pi)


def _gelu(x):
    return 0.5 * x * (1.0 + jnp.tanh(_GELU_C * (x + 0.044715 * (x * x * x))))


def _dgelu(x):
    t = jnp.tanh(_GELU_C * (x + 0.044715 * (x * x * x)))
    return 0.5 * (1.0 + t) + 0.5 * x * (1.0 - t * t) * _GELU_C * (1.0 + 3.0 * 0.044715 * x * x)


def _silu_both(x):
    s = jax.nn.sigmoid(x)
    xs = x * s
    return xs, s + xs * (1.0 - s)


def _gelu_both(x):
    x2 = x * x
    t = jnp.tanh(_GELU_C * (x + 0.044715 * (x2 * x)))
    half = 0.5 * (1.0 + t)
    return x * half, half + 0.5 * x * (1.0 - t * t) * _GELU_C * (1.0 + 3.0 * 0.044715 * x2)


def _softplus(x):
    return jnp.maximum(x, 0.0) + jnp.log(1.0 + jnp.exp(-jnp.abs(x)))


def _cparams(sem, vmem=None):
    return pltpu.CompilerParams(dimension_semantics=sem, vmem_limit_bytes=vmem)


def _row_spec(tm, width):
    return pl.BlockSpec((tm, width), lambda i: (i, 0))


def _col_spec(height, tm):
    return pl.BlockSpec((height, tm), lambda i: (0, i))


def _const_spec(shape):
    nd = len(shape)
    return pl.BlockSpec(shape, lambda *_: (0,) * nd)


def _weight_dims(w_bf, w_t):
    return (((1,), (1,)), ((), ())) if w_t else (((1,), (0,)), ((), ())), w_bf.shape[0 if w_t else 1]


def ln_in_proj(x, g, w_bf, splits, name, after=None, w_t=False, bf16_pieces=()):
    dims, n = _weight_dims(w_bf, w_t)
    dtypes = [BF16 if i in bf16_pieces else F32 for i in range(len(splits))]
    tm = 512 if n <= ODD_IN else 256
    spans = []
    o = 0
    for s in splits:
        spans.append((o, o + s))
        o += s
    assert o == n
    extra_specs, extra = _after_operand(after)

    def body(x_ref, g_ref, w_ref, *rest):
        xn_ref, outs = rest[len(extra)], rest[len(extra) + 1:]
        xv = x_ref[...]
        rstd = lax.rsqrt(jnp.mean(xv * xv, axis=-1, keepdims=True) + RMS_EPS)
        xn = (xv * rstd * g_ref[...]).astype(BF16)
        xn_ref[...] = xn.T
        p = lax.dot_general(xn, w_ref[...], dims, preferred_element_type=F32)
        for o_ref, (a, b) in zip(outs, spans):
            o_ref[...] = p[:, a:b].astype(o_ref.dtype)

    return pl.pallas_call(
        body, grid=(T // tm,), name=name,
        in_specs=[_row_spec(tm, D), _const_spec((1, D)), _const_spec(w_bf.shape)] + extra_specs,
        out_specs=[_col_spec(D, tm)] + [_row_spec(tm, s) for s in splits],
        out_shape=[jax.ShapeDtypeStruct((D, T), BF16)]
        + [jax.ShapeDtypeStruct((T, s), dt) for s, dt in zip(splits, dtypes)],
        compiler_params=_cparams(("parallel",), VMEM_BIG),
    )(x, g, w_bf, *extra)


def in_proj_bwd_x(x, g, w_bf, dps, dres, name, after=None, w_t=False):
    tm = 512
    back = (((1,), (0,)), ((), ())) if w_t else (((1,), (1,)), ((), ()))
    widths = [d.shape[1] for d in dps]
    extra_specs, extra = _after_operand(after)

    def body(x_ref, g_ref, w_ref, dres_ref, *rest):
        dp_refs = rest[:len(widths)]
        dx_ref, dg_ref = rest[-2:]
        dp = jnp.concatenate([r[...] for r in dp_refs], axis=-1)
        dxn = lax.dot_general(dp, w_ref[...], back, preferred_element_type=F32)
        xv = x_ref[...]
        rstd = lax.rsqrt(jnp.mean(xv * xv, axis=-1, keepdims=True) + RMS_EPS)
        xhat = xv * rstd
        dgp = jnp.sum(dxn * xhat, axis=0, keepdims=True)

        @pl.when(pl.program_id(0) == 0)
        def _():
            dg_ref[...] = jnp.zeros_like(dg_ref)

        dg_ref[...] += dgp
        dxh = dxn * g_ref[...]
        dx_ref[...] = dres_ref[...] + rstd * (dxh - xhat * jnp.mean(dxh * xhat, axis=-1, keepdims=True))

    return pl.pallas_call(
        body, grid=(T // tm,), name=name,
        in_specs=[_row_spec(tm, D), _const_spec((1, D)), _const_spec(w_bf.shape), _row_spec(tm, D)]
        + [_row_spec(tm, s) for s in widths] + extra_specs,
        out_specs=[_row_spec(tm, D), _const_spec((1, D))],
        out_shape=[jax.ShapeDtypeStruct((T, D), F32), jax.ShapeDtypeStruct((1, D), F32)],
        compiler_params=_cparams(("arbitrary",), VMEM_BIG),
    )(x, g, w_bf, dres, *dps, *extra)


def _after_operand(after):
    return ([ANY], [after]) if after is not None else ([], [])


def matmul_acc_chips(at_bf, pieces, name, after=None, add_cores=False):
    k = at_bf.shape[0]
    widths = [p.shape[1] for p in pieces]
    nb = sum(widths) // NCHIP
    tm = 512
    steps = T // tm
    half = k // 2
    extra_specs, extra = _after_operand(after)

    def body(a_ref, *rest):
        o_ref, acc = rest[len(widths) + len(extra):][:2]

        @pl.when(pl.program_id(0) == 0)
        def _():
            acc[...] = jnp.zeros_like(acc)

        a = a_ref[...]
        b = jnp.concatenate([r[...] for r in rest[:len(widths)]], axis=-1)
        for s in range(NCHIP):
            acc[s] += jnp.dot(a, b[:, s * nb:(s + 1) * nb], preferred_element_type=F32)

        @pl.when(pl.program_id(0) == steps - 1)
        def _():
            if not add_cores:
                o_ref[...] = acc[...].astype(BF16)
            else:
                give, got, send, recv = rest[-4:]
                x, y, c = lax.axis_index("x"), lax.axis_index("y"), lax.axis_index("c")
                theirs = pl.multiple_of((1 - c) * half, half)
                mine = pl.multiple_of(c * half, half)
                give[...] = acc[:, pl.ds(theirs, half), :].astype(BF16)
                cp = pltpu.make_async_remote_copy(src_ref=give, dst_ref=got, send_sem=send, recv_sem=recv,
                                                  device_id=(x, y, 1 - c), device_id_type=MESH)
                cp.start()
                cp.wait()
                o_ref[...] = (acc[:, pl.ds(mine, half), :] + got[...].astype(F32)).astype(BF16)

    out_rows = half if add_cores else k
    exchange = [pltpu.VMEM((NCHIP, half, nb), BF16)] * 2 + [pltpu.SemaphoreType.DMA] * 2 if add_cores else []
    return pl.pallas_call(
        body, grid=(steps,), name=name,
        in_specs=[_col_spec(k, tm)] + [_row_spec(tm, w_) for w_ in widths] + extra_specs,
        out_specs=_const_spec((NCHIP, out_rows, nb)),
        out_shape=jax.ShapeDtypeStruct((NCHIP, out_rows, nb), BF16),
        scratch_shapes=[pltpu.VMEM((NCHIP, k, nb), F32)] + exchange,
        compiler_params=_cparams(("arbitrary",), VMEM_BIG),
    )(at_bf, *pieces, *extra)


def _out_proj_back(dh_ref, zt_ref, w_ref, dw_ref):
    dhb = dh_ref[...].astype(BF16)

    @pl.when(pl.program_id(0) == 0)
    def _():
        dw_ref[...] = jnp.zeros_like(dw_ref)

    dw_ref[...] += jnp.dot(zt_ref[...], dhb, preferred_element_type=F32)
    return lax.dot_general(dhb, w_ref[...], (((1,), (1,)), ((), ())), preferred_element_type=F32)


PREP_TM = 512
PREP_NB = SEQ // PREP_TM


def _prep_elem(k, wl, apre, kkw, kaw, bd):
    wraw = -_softplus(-wl) - 0.5
    lw = -jnp.exp(wraw)
    asig = jax.nn.sigmoid(apre)
    kkr = k * kkw
    nrm = jnp.maximum(jnp.sqrt(_headsum(kkr * kkr, bd)), 1e-12)
    kk = kkr / nrm
    k2 = k * (1.0 + (asig - 1.0) * kaw)
    return lw, k2, -kk, kk * asig


def _prep_elem_bwd(k, wl, apre, kkw, kaw, bd, dlw, dk2, daa, dbb):
    s = -wl
    sp = _softplus(s)
    dwl = dlw * (-jnp.exp(-sp - 0.5)) * jnp.exp(s - sp)
    asig = jax.nn.sigmoid(apre)
    kkr = k * kkw
    root = jnp.sqrt(_headsum(kkr * kkr, bd))
    inv = 1.0 / jnp.maximum(root, 1e-12)
    kk = kkr * inv
    dkk = dbb * asig - daa
    dap = (dbb * kk + dk2 * k * kaw) * asig * (1.0 - asig)
    through_norm = jnp.where(root > 1e-12, kk * _headsum(dkk * kkr, bd) * inv, 0.0)
    dkkr = inv * (dkk - through_norm)
    gain = 1.0 + (asig - 1.0) * kaw
    dk = dkkr * kkw + dk2 * gain
    dkkw = jnp.sum(dkkr * k, axis=0, keepdims=True)
    dkaw = jnp.sum(dk2 * k * (asig - 1.0), axis=0, keepdims=True)
    return dk, dwl, dap, dkkw, dkaw


def _shifted(ps_ref, prev_ref, mu, blk):
    p = ps_ref[...]
    first = (blk % PREP_NB) == 0
    prev_row = jnp.where(first, 0.0, prev_ref[7:8, :])
    rolled = pltpu.roll(p, 1, 0)
    p_prev = jnp.where(_iota2(p.shape, 0) == 0, prev_row, rolled)
    return p, p_prev, p + (p_prev - p) * mu


def _prev_spec(width, blk_of):
    return pl.BlockSpec((8, width), lambda i: (jnp.maximum(blk_of(i) * (PREP_TM // 8) - 1, 0), 0))


def even_prep(ps, mu, w0, w2x, a0, a2x, kkw, kaw):
    tm = PREP_TM

    def body(ps_ref, prev_ref, mu_ref, w0_ref, w2_ref, a0_ref, a2_ref, kk_ref, ka_ref,
             r_ref, lw_ref, k2_ref, v_ref, aa_ref, bb_ref):
        _, _, s = _shifted(ps_ref, prev_ref, mu_ref[...], pl.program_id(0))
        wa = s[:, 3 * W:]
        wl = w0_ref[...] + _bdot(jnp.tanh(wa), w2_ref[...])
        apre = a0_ref[...] + _bdot(wa, a2_ref[...])
        lw, k2, aa, bb = _prep_elem(s[:, W:2 * W], wl, apre, kk_ref[...], ka_ref[...], _head_blockdiag())
        r_ref[...] = s[:, 0:W]
        v_ref[...] = s[:, 2 * W:3 * W]
        lw_ref[...] = lw
        k2_ref[...] = k2
        aa_ref[...] = aa
        bb_ref[...] = bb

    vec = _const_spec((1, W))
    return pl.pallas_call(
        body, grid=(T // tm,), name="even_prep",
        in_specs=[_row_spec(tm, SHIFT), _prev_spec(SHIFT, lambda i: i), _const_spec((1, SHIFT)), vec,
                  _const_spec((2 * LORA, W)), vec, _const_spec((2 * LORA, W)), vec, vec],
        out_specs=[_row_spec(tm, W)] * 6,
        out_shape=[jax.ShapeDtypeStruct((T, W), F32)] * 6,
        compiler_params=_cparams(("parallel",), VMEM_BIG),
    )(ps, ps, mu, w0, w2x, a0, a2x, kkw, kaw)


def even_prep_bwd(ps, mu, w0, w2x, a0, a2x, kkw, kaw, dr, dlw, dk2, dv, daa, dbb, dr2, dk22, dv2):
    tm = PREP_TM
    nb = T // tm
    rev = lambda i: nb - 1 - i

    def body(ps_ref, prev_ref, mu_ref, w0_ref, w2_ref, a0_ref, a2_ref, kk_ref, ka_ref,
             dr_ref, dlw_ref, dk2_ref, dv_ref, daa_ref, dbb_ref, dr2_ref, dk22_ref, dv2_ref,
             dps_ref, dmu_ref, dw0_ref, dw2_ref, da0_ref, da2_ref, dkk_ref, dka_ref, carry):
        i = pl.program_id(0)
        blk = rev(i)
        mu_v = mu_ref[...]
        p, p_prev, s = _shifted(ps_ref, prev_ref, mu_v, blk)
        wa = s[:, 3 * W:]
        th = jnp.tanh(wa)
        wl = w0_ref[...] + _bdot(th, w2_ref[...])
        apre = a0_ref[...] + _bdot(wa, a2_ref[...])
        bd = _head_blockdiag()
        k = s[:, W:2 * W]
        dk, dwl, dap, dkkw, dkaw = _prep_elem_bwd(k, wl, apre, kk_ref[...], ka_ref[...], bd, dlw_ref[...],
                                                  dk2_ref[...] + dk22_ref[...], daa_ref[...], dbb_ref[...])
        dwa = _bdot_nt(dwl, w2_ref[...]) * (1.0 - th * th) + _bdot_nt(dap, a2_ref[...])
        ds = jnp.concatenate([dr_ref[...] + dr2_ref[...], dk, dv_ref[...] + dv2_ref[...], dwa], axis=-1)

        @pl.when(i == 0)
        def _():
            for ref in (dmu_ref, dw0_ref, dw2_ref, da0_ref, da2_ref, dkk_ref, dka_ref, carry):
                ref[...] = jnp.zeros_like(ref)

        dmu_ref[...] += jnp.sum(ds * (p_prev - p), axis=0, keepdims=True)
        dw0_ref[...] += jnp.sum(dwl, axis=0, keepdims=True)
        da0_ref[...] += jnp.sum(dap, axis=0, keepdims=True)
        dw2_ref[...] += _bdot_tn(th, dwl)
        da2_ref[...] += _bdot_tn(wa, dap)
        dkk_ref[...] += dkkw
        dka_ref[...] += dkaw
        dsm = ds * mu_v
        last = (blk % PREP_NB) == PREP_NB - 1
        nxt = jnp.where(last, 0.0, carry[0:1, :])
        up = pltpu.roll(dsm, tm - 1, 0)
        up = jnp.where(_iota2(up.shape, 0) == tm - 1, nxt, up)
        dps_ref[...] = (ds - dsm + up).astype(BF16)
        carry[0:1, :] = dsm[0:1, :]

    vec = _const_spec((1, W))
    rrow = lambda width: pl.BlockSpec((tm, width), lambda i: (rev(i), 0))
    return pl.pallas_call(
        body, grid=(nb,), name="even_prep_bwd",
        in_specs=[rrow(SHIFT), _prev_spec(SHIFT, rev), _const_spec((1, SHIFT)), vec,
                  _const_spec((2 * LORA, W)), vec, _const_spec((2 * LORA, W)), vec, vec] + [rrow(W)] * 9,
        out_specs=[rrow(SHIFT), _const_spec((1, SHIFT)), vec, _const_spec((2 * LORA, W)), vec,
                   _const_spec((2 * LORA, W)), vec, vec],
        out_shape=[jax.ShapeDtypeStruct((T, SHIFT), BF16), jax.ShapeDtypeStruct((1, SHIFT), F32),
                   jax.ShapeDtypeStruct((1, W), F32), jax.ShapeDtypeStruct((2 * LORA, W), F32),
                   jax.ShapeDtypeStruct((1, W), F32), jax.ShapeDtypeStruct((2 * LORA, W), F32),
                   jax.ShapeDtypeStruct((1, W), F32), jax.ShapeDtypeStruct((1, W), F32)],
        scratch_shapes=[pltpu.VMEM((8, SHIFT), F32)],
        compiler_params=_cparams(("arbitrary",), VMEM_BIG),
    )(ps, ps, mu, w0, w2x, a0, a2x, kkw, kaw, dr, dlw, dk2, dv, daa, dbb, dr2, dk22, dv2)


NPAIR = NH // 2
PW = 2 * HD


def _pair_cols(p):
    return slice(p * PW, (p + 1) * PW)


def _pairs(a):
    return [a[:, _pair_cols(p)] for p in range(NPAIR)]


def _stack_pair(a):
    first = _iota2(a.shape, 1) < HD
    zero = jnp.zeros_like(a)
    return jnp.concatenate([jnp.where(first, a, zero), jnp.where(first, zero, a)], axis=0)


def _unstack_pair(a):
    n = a.shape[0] // 2
    return jnp.where(_iota2((n, PW), 1) < HD, a[:n], a[n:])


def _fold_pair(a):
    n = a.shape[0] // 2
    return a[:n] + a[n:]


def _chunk_masks():
    n = 4 * L
    row = _iota2((n, n), 0)
    col = _iota2((n, n), 1)
    same = ((row // L) & 1) == ((col // L) & 1)
    ri = row & (L - 1)
    ci = col & (L - 1)
    keep = same & (((row < 2 * L) & (ri > ci)) | ((row >= 2 * L) & (ri >= ci)))
    r1 = _iota2((L, L), 0)
    c1 = _iota2((L, L), 1)
    r2 = _iota2((2 * L, 2 * L), 0)
    c2 = _iota2((2 * L, 2 * L), 1)
    return keep.astype(F32), (r1 >= c1).astype(F32), (r2 == c2).astype(F32)


def _scaled(r, lw, k2, aa, bb, tri):
    g = _hdot(tri, lw)
    eg = jnp.exp(g)
    eng = jnp.exp(-g)
    egp = jnp.exp(g - lw)
    return eg, eng, egp, aa * egp, r * eg, bb * eng, k2 * eng


def _head_cols(h):
    return slice(h * HD, (h + 1) * HD)


def _per_head(a):
    return [a[:, _head_cols(h)] for h in range(NH)]


def _pairs_operands(at, rt, bt, kt):
    x = [jnp.concatenate([_stack_pair(a), _stack_pair(r)], axis=0).astype(BF16) for a, r in zip(_pairs(at), _pairs(rt))]
    yk = [jnp.concatenate([_stack_pair(b), _stack_pair(k)], axis=0).astype(BF16) for b, k in zip(_pairs(bt), _pairs(kt))]
    return x, yk


def _pairs_matrices(x, yk, keep, eye):
    m = [_bdot_nt(a, b) * keep for a, b in zip(x, yk)]
    p = [a[:2 * L, :2 * L] for a in m]
    tinv = [eye + a for a in p]
    for _ in range(5):
        p = [_bdot(a, a) for a in p]
        tinv = [t + _bdot(t, a) for t, a in zip(tinv, p)]
    return [a.astype(BF16) for a in m], [a.astype(BF16) for a in tinv]


def _pairs_fwd(x, yk, m, tinv, vw, s0, egl):
    xh = [_bdot_nt(a, s) for a, s in zip(x, s0)]
    u = [_bdot(t, h[:2 * L] + _bdot(a[:2 * L, 2 * L:], w)) for t, h, a, w in zip(tinv, xh, m, vw)]
    uv = [jnp.concatenate([a, w], axis=0).astype(BF16) for a, w in zip(u, vw)]
    y = [h[2 * L:] + _bdot(a[2 * L:], w) for h, a, w in zip(xh, m, uv)]
    sn = [e * (s + _bdot_tn(w, b)) for e, s, w, b in zip(egl, s0, uv, yk)]
    return y, sn, uv


def _pairs_bwd(x, yk, m, tinv, uv, s0, sn, egl, dyw, dsn, keep):
    dzs = [d * e for d, e in zip(dsn, egl)]
    dgl = [jnp.sum(d * s, axis=0, keepdims=True) for d, s in zip(dsn, sn)]
    dyb = [a.astype(BF16) for a in dyw]
    t1 = [_bdot_tn(a[2 * L:], d) for a, d in zip(m, dyb)]
    t2 = [_bdot_nt(b, d) for b, d in zip(yk, dzs)]
    drhs = [_bdot_tn(t, a[:2 * L] + b[:2 * L]) for t, a, b in zip(tinv, t1, t2)]
    dv = [a[2 * L:] + b[2 * L:] + _bdot_tn(c[:2 * L, 2 * L:], d) for a, b, c, d in zip(t1, t2, m, drhs)]
    gg = [jnp.concatenate([a, b], axis=0).astype(BF16) for a, b in zip(drhs, dyw)]
    ds0 = [d + _bdot_tn(g, a) for d, g, a in zip(dzs, gg, x)]
    dm = [_bdot_nt(g, w) * keep for g, w in zip(gg, uv)]
    dx = [_bdot(g, s) + _bdot(d, b) for g, s, d, b in zip(gg, s0, dm, yk)]
    dyk = [_bdot_tn(d, a) + _bdot(w, z) for d, a, w, z in zip(dm, x, uv, dzs)]
    return dx, dyk, dv, dgl, ds0


STATE_SHAPE = (NPAIR * PW, PW)
M_SHAPE = (4 * L, NPAIR * 4 * L)
TINV_SHAPE = (2 * L, NPAIR * 2 * L)


def _rows_of(a, n):
    return [a[i * n:(i + 1) * n, :] for i in range(NPAIR)]


def _both(f):
    out = []
    for s in range(NSEQ):
        out += f(s)
    return out


def _seq_view(a):
    return a.reshape(NSEQ, SEQ, a.shape[-1])


UV_SHAPE = (4 * L, NPAIR * PW)
RW_CHUNKS = 2


def rwkv_fwd(r, lw, k2, v, aa, bb):
    def body(r_ref, lw_ref, k2_ref, v_ref, aa_ref, bb_ref, y_ref, hs_ref, hn_ref, m_ref, t_ref, uv_ref, state):
        @pl.when(pl.program_id(0) == 0)
        def _():
            state[...] = jnp.zeros_like(state)

        keep, tri, eye = _chunk_masks()
        where = [(j, s) for j in range(RW_CHUNKS) for s in range(NSEQ)]
        rows = lambda j: slice(j * L, (j + 1) * L)
        sc = [_scaled(r_ref[s, rows(j)], lw_ref[s, rows(j)], k2_ref[s, rows(j)], aa_ref[s, rows(j)],
                      bb_ref[s, rows(j)], tri) for j, s in where]
        ops = [_pairs_operands(*a[3:]) for a in sc]
        m, tinv = _pairs_matrices([a for o in ops for a in o[0]], [a for o in ops for a in o[1]], keep, eye)
        s_cur = [state[s] for s in range(NSEQ)]
        for j in range(RW_CHUNKS):
            mine = slice(j * NSEQ * NPAIR, (j + 1) * NSEQ * NPAIR)
            x = [a for o in ops[j * NSEQ:(j + 1) * NSEQ] for a in o[0]]
            yk = [a for o in ops[j * NSEQ:(j + 1) * NSEQ] for a in o[1]]
            vw = _both(lambda s: [_stack_pair(a) for a in _pairs(v_ref[s, rows(j)])])
            egl = _both(lambda s: _pairs(sc[j * NSEQ + s][0][L - 1:L, :]))
            y, sn, uv = _pairs_fwd(x, yk, m[mine], tinv[mine], vw, _both(lambda s: _rows_of(s_cur[s], PW)), egl)
            for s in range(NSEQ):
                ps = slice(s * NPAIR, (s + 1) * NPAIR)
                hs_ref[j, s] = s_cur[s]
                y_ref[s, rows(j)] = jnp.concatenate([_fold_pair(a) for a in y[ps]], axis=-1)
                m_ref[j, s] = jnp.concatenate(m[mine][ps], axis=-1)
                t_ref[j, s] = jnp.concatenate(tinv[mine][ps], axis=-1)
                uv_ref[j, s] = jnp.concatenate(uv[ps], axis=-1)
                s_cur[s] = jnp.concatenate(sn[ps], axis=0)
                hn_ref[j, s] = s_cur[s]
        for s in range(NSEQ):
            state[s] = s_cur[s]

    blk = pl.BlockSpec((NSEQ, RW_CHUNKS * L, W), lambda c: (0, c, 0))
    per_chunk = lambda shape: pl.BlockSpec((RW_CHUNKS, NSEQ) + shape, lambda c: (c, 0, 0, 0))
    saved_shapes = [(STATE_SHAPE, F32), (STATE_SHAPE, F32), (M_SHAPE, BF16), (TINV_SHAPE, BF16), (UV_SHAPE, BF16)]
    y, *saved = pl.pallas_call(
        body, grid=(NC // RW_CHUNKS,), name="rwkv_fwd",
        in_specs=[blk] * 6,
        out_specs=[blk] + [per_chunk(shape) for shape, _ in saved_shapes],
        out_shape=[jax.ShapeDtypeStruct((NSEQ, SEQ, W), F32)]
        + [jax.ShapeDtypeStruct((NC, NSEQ) + shape, dt) for shape, dt in saved_shapes],
        scratch_shapes=[pltpu.VMEM((NSEQ,) + STATE_SHAPE, F32)],
        compiler_params=_cparams(("arbitrary",), VMEM_BIG),
    )(*[_seq_view(a) for a in (r, lw, k2, v, aa, bb)])
    return y.reshape(T, W), saved


def rwkv_bwd(r, lw, k2, aa, bb, saved, dy):
    def body(r_ref, lw_ref, k2_ref, aa_ref, bb_ref, hs_ref, hn_ref, m_ref, t_ref, uv_ref, dy_ref,
             dr_ref, dlw_ref, dk2_ref, dv_ref, daa_ref, dbb_ref, dstate):
        @pl.when(pl.program_id(0) == 0)
        def _():
            dstate[...] = jnp.zeros_like(dstate)

        keep, tri, _ = _chunk_masks()
        sc = [_scaled(r_ref[s], lw_ref[s], k2_ref[s], aa_ref[s], bb_ref[s], tri) for s in range(NSEQ)]
        ops = [_pairs_operands(*sc[s][3:]) for s in range(NSEQ)]
        x, yk = _both(lambda s: ops[s][0]), _both(lambda s: ops[s][1])
        m = _both(lambda s: [m_ref[0, s][:, i * 4 * L:(i + 1) * 4 * L] for i in range(NPAIR)])
        tinv = _both(lambda s: [t_ref[0, s][:, i * 2 * L:(i + 1) * 2 * L] for i in range(NPAIR)])
        uv = _both(lambda s: _pairs(uv_ref[0, s]))
        dyw = _both(lambda s: [_stack_pair(a) for a in _pairs(dy_ref[s])])
        s0 = _both(lambda s: _rows_of(hs_ref[0, s], PW))
        sn = _both(lambda s: _rows_of(hn_ref[0, s], PW))
        dsn = _both(lambda s: _rows_of(dstate[s], PW))
        egl = _both(lambda s: _pairs(sc[s][0][L - 1:L, :]))
        dx, dyk, dvw, dgl, ds0 = _pairs_bwd(x, yk, m, tinv, uv, s0, sn, egl, dyw, dsn, keep)
        for s in range(NSEQ):
            mine = slice(s * NPAIR, (s + 1) * NPAIR)
            eg, eng, egp, at, rt, bt, kt = sc[s]
            dstate[s] = jnp.concatenate(ds0[mine], axis=0)
            dv_ref[s] = jnp.concatenate([_fold_pair(a) for a in dvw[mine]], axis=-1)
            dat = jnp.concatenate([_fold_pair(a[:2 * L]) for a in dx[mine]], axis=-1)
            drt = jnp.concatenate([_fold_pair(a[2 * L:]) for a in dx[mine]], axis=-1)
            dbt = jnp.concatenate([_fold_pair(a[:2 * L]) for a in dyk[mine]], axis=-1)
            dkt = jnp.concatenate([_fold_pair(a[2 * L:]) for a in dyk[mine]], axis=-1)
            dg = drt * rt - dbt * bt - dkt * kt
            dg = dg + jnp.where(_iota2(dg.shape, 0) == L - 1, jnp.concatenate(dgl[mine], axis=-1), 0.0)
            dgp = dat * at
            dlw_ref[s] = _hdot_tn(tri, dg + dgp) - dgp
            dr_ref[s] = drt * eg
            daa_ref[s] = dat * egp
            dbb_ref[s] = dbt * eng
            dk2_ref[s] = dkt * eng

    blk = pl.BlockSpec((NSEQ, L, W), lambda c: (0, NC - 1 - c, 0))
    per_chunk = lambda shape: pl.BlockSpec((1, NSEQ) + shape, lambda c: (NC - 1 - c, 0, 0, 0))
    outs = pl.pallas_call(
        body, grid=(NC,), name="rwkv_bwd",
        in_specs=[blk] * 5 + [per_chunk(a.shape[2:]) for a in saved] + [blk],
        out_specs=[blk] * 6,
        out_shape=[jax.ShapeDtypeStruct((NSEQ, SEQ, W), F32)] * 6,
        scratch_shapes=[pltpu.VMEM((NSEQ,) + STATE_SHAPE, F32)],
        compiler_params=_cparams(("arbitrary",)),
    )(*[_seq_view(a) for a in (r, lw, k2, aa, bb)], *saved, _seq_view(dy))
    return [a.reshape(T, W) for a in outs]


def _post_math(y, r, k2, v, ga, o, gb, lng, lnb, rk, bd):
    mu = _headsum(y, bd) * (1.0 / HD)
    yc = y - mu
    var = _headsum(yc * yc, bd) * (1.0 / HD)
    yn = yc * lax.rsqrt(var + GN_EPS) * lng + lnb
    bonus = _headsum(r * k2 * rk, bd) * v
    return (yn + bonus) * _silu(ga), o * _silu(gb)


def even_post(y, r, k2, v, ga, o, gb, lng, lnb, rk, h, w_bf):
    tm = 512

    def body(y_ref, r_ref, k2_ref, v_ref, ga_ref, o_ref, gb_ref, lng_ref, lnb_ref, rk_ref, h_ref, w_ref,
             ho_ref, zt_ref):
        ya, yb = _post_math(y_ref[...], r_ref[...], k2_ref[...], v_ref[...], ga_ref[...], o_ref[...], gb_ref[...],
                            lng_ref[...], lnb_ref[...], rk_ref[...], _head_blockdiag())
        z = jnp.concatenate([ya.astype(BF16), yb.astype(BF16)], axis=-1)
        zt_ref[...] = z.T
        ho_ref[...] = h_ref[...] + jnp.dot(z, w_ref[...], preferred_element_type=F32)

    vec = _const_spec((1, W))
    return pl.pallas_call(
        body, grid=(T // tm,), name="even_post",
        in_specs=[_row_spec(tm, W)] * 7 + [vec] * 3 + [_row_spec(tm, D), _const_spec((D, D))],
        out_specs=[_row_spec(tm, D), _col_spec(D, tm)],
        out_shape=[jax.ShapeDtypeStruct((T, D), F32), jax.ShapeDtypeStruct((D, T), BF16)],
        compiler_params=_cparams(("parallel",), VMEM_BIG),
    )(y, r, k2, v, ga, o, gb, lng, lnb, rk, h, w_bf)


def even_post_bwd(y, r, k2, v, ga, o, gb, lng, lnb, rk, dh, zt_bf, w_bf, after=None):
    tm = 512
    extra_specs, extra = _after_operand(after)

    def body(y_ref, r_ref, k2_ref, v_ref, ga_ref, o_ref, gb_ref, lng_ref, lnb_ref, rk_ref, dh_ref, zt_ref, w_ref,
             *rest):
        dy_ref, dr_ref, dk2_ref, dv_ref, dga_ref, do_ref, dgb_ref, dlng_ref, dlnb_ref, drk_ref, dw_ref = rest[-11:]
        dzv = _out_proj_back(dh_ref, zt_ref, w_ref, dw_ref)
        bd = _head_blockdiag()
        _, vjp = jax.vjp(lambda *a: _post_math(*a, bd), y_ref[...], r_ref[...], k2_ref[...], v_ref[...], ga_ref[...],
                         o_ref[...], gb_ref[...], lng_ref[...], lnb_ref[...], rk_ref[...])
        dy, dr, dk2, dv, dga, do, dgb, dlng, dlnb, drk = vjp((dzv[:, 0:W], dzv[:, W:2 * W]))
        for ref, val in ((dy_ref, dy), (dr_ref, dr), (dk2_ref, dk2), (dv_ref, dv), (dga_ref, dga), (do_ref, do),
                         (dgb_ref, dgb)):
            ref[...] = val.astype(ref.dtype)

        @pl.when(pl.program_id(0) == 0)
        def _():
            for ref in (dlng_ref, dlnb_ref, drk_ref):
                ref[...] = jnp.zeros_like(ref)

        dlng_ref[...] += dlng
        dlnb_ref[...] += dlnb
        drk_ref[...] += drk

    vec = _const_spec((1, W))
    return pl.pallas_call(
        body, grid=(T // tm,), name="even_post_bwd",
        in_specs=[_row_spec(tm, W)] * 7 + [vec] * 3 + [_row_spec(tm, D), _col_spec(D, tm), _const_spec((D, D))]
        + extra_specs,
        out_specs=[_row_spec(tm, W)] * 7 + [vec] * 3 + [_const_spec((D, D))],
        out_shape=[jax.ShapeDtypeStruct((T, W), dt) for dt in (F32, F32, F32, F32, BF16, F32, BF16)]
        + [jax.ShapeDtypeStruct((1, W), F32)] * 3 + [jax.ShapeDtypeStruct((D, D), F32)],
        compiler_params=_cparams(("arbitrary",), VMEM_BIG),
    )(y, r, k2, v, ga, o, gb, lng, lnb, rk, dh, zt_bf, w_bf, *extra)


PADSEQ = SEQ + LEFT * L
ATT_SCALE = 1.0 / math.sqrt(HD)
ATT_Q = 4
WIN = BAND + (ATT_Q - 1) * L
ATT_STEPS = NC // ATT_Q
ATT_BIAS_SHAPE = (NPAIR, ATT_Q * 2 * L, WIN)
ATT_WINDOW_BIAS_SHAPE = (ATT_Q, NPAIR, 2 * L, WIN)


def _stack_chunks(a):
    return jnp.concatenate([_stack_pair(a[i * L:(i + 1) * L]) for i in range(ATT_Q)], axis=0)


def _unstack_chunks(a):
    return jnp.concatenate([_unstack_pair(a[i * 2 * L:(i + 1) * 2 * L]) for i in range(ATT_Q)], axis=0)


def _window_bias(b_ref):
    return [jnp.concatenate([b_ref[c, p] for c in range(ATT_Q)], axis=0) for p in range(NPAIR)]


def _att_probs(q2, kw, bias, step):
    valid = _iota2((1, WIN), 1) >= (LEFT - step * ATT_Q) * L
    s = [jnp.where(valid, _bdot_nt(a, b) * ATT_SCALE + bias[p], NEG) for p, (a, b) in enumerate(zip(q2, kw))]
    e = [jnp.exp(a - jnp.max(a, axis=-1, keepdims=True)) for a in s]
    return [a / jnp.sum(a, axis=-1, keepdims=True) for a in e]


def attention_fwd(q, kpad, vpad, bias):
    def body(q_ref, k_ref, v_ref, b_ref, o_ref):
        step = pl.program_id(1)
        start = pl.multiple_of(step * (ATT_Q * L), L)
        kw = _pairs(k_ref[pl.ds(start, WIN), :])
        vw = _pairs(v_ref[pl.ds(start, WIN), :])
        q2 = [_stack_chunks(a) for a in _pairs(q_ref[...])]
        p = _att_probs(q2, kw, _window_bias(b_ref), step)
        o_ref[...] = jnp.concatenate([_unstack_chunks(_bdot(a, b)) for a, b in zip(p, vw)], axis=-1)

    qblk = pl.BlockSpec((ATT_Q * L, W), lambda b, c: (b * ATT_STEPS + c, 0))
    kblk = pl.BlockSpec((PADSEQ, W), lambda b, c: (b, 0))
    return pl.pallas_call(
        body, grid=(NSEQ, ATT_STEPS), name="attention_fwd",
        in_specs=[qblk, kblk, kblk, _const_spec(ATT_WINDOW_BIAS_SHAPE)],
        out_specs=qblk, out_shape=jax.ShapeDtypeStruct((T, W), F32),
        compiler_params=_cparams(("parallel", "arbitrary")),
    )(q, kpad, vpad, bias)


def attention_bwd(q, kpad, vpad, bias, do):
    def body(q_ref, k_ref, v_ref, b_ref, do_ref, dq_ref, dko_ref, dvo_ref, db_ref, dk_ref, dv_ref):
        b = pl.program_id(0)
        c = pl.program_id(1)

        @pl.when(c == 0)
        def _():
            dk_ref[...] = jnp.zeros_like(dk_ref)
            dv_ref[...] = jnp.zeros_like(dv_ref)

        @pl.when((c == 0) & (b == 0))
        def _():
            db_ref[...] = jnp.zeros_like(db_ref)

        start = pl.multiple_of(c * (ATT_Q * L), L)
        kw = _pairs(k_ref[pl.ds(start, WIN), :])
        vw = _pairs(v_ref[pl.ds(start, WIN), :])
        q2 = [_stack_chunks(a) for a in _pairs(q_ref[...])]
        do2 = [_stack_chunks(a) for a in _pairs(do_ref[...].astype(BF16))]
        p = _att_probs(q2, kw, _window_bias(b_ref), c)
        dp = [_bdot_nt(a, b) for a, b in zip(do2, vw)]
        ds = [a * (d - jnp.sum(d * a, axis=-1, keepdims=True)) for a, d in zip(p, dp)]
        dss = [(a * ATT_SCALE).astype(BF16) for a in ds]
        dq_ref[...] = jnp.concatenate([_unstack_chunks(_bdot(a, b)) for a, b in zip(dss, kw)], axis=-1).astype(BF16)
        dk_ref[pl.ds(start, WIN), :] += jnp.concatenate([_bdot_tn(a, b) for a, b in zip(dss, q2)], axis=-1)
        dv_ref[pl.ds(start, WIN), :] += jnp.concatenate([_bdot_tn(a, b) for a, b in zip(p, do2)], axis=-1)
        for i in range(NPAIR):
            db_ref[i] += ds[i]

        @pl.when(c == ATT_STEPS - 1)
        def _():
            dko_ref[...] = dk_ref[LEFT * L:, :].astype(BF16)
            dvo_ref[...] = dv_ref[LEFT * L:, :].astype(BF16)

    qblk = pl.BlockSpec((ATT_Q * L, W), lambda b, c: (b * ATT_STEPS + c, 0))
    kblk = pl.BlockSpec((PADSEQ, W), lambda b, c: (b, 0))
    sblk = pl.BlockSpec((SEQ, W), lambda b, c: (b, 0))
    bblk = _const_spec(ATT_BIAS_SHAPE)
    return pl.pallas_call(
        body, grid=(NSEQ, ATT_STEPS), name="attention_bwd",
        in_specs=[qblk, kblk, kblk, _const_spec(ATT_WINDOW_BIAS_SHAPE), qblk],
        out_specs=[qblk, sblk, sblk, bblk],
        out_shape=[jax.ShapeDtypeStruct((T, W), BF16), jax.ShapeDtypeStruct((T, W), BF16),
                   jax.ShapeDtypeStruct((T, W), BF16), jax.ShapeDtypeStruct(ATT_BIAS_SHAPE, F32)],
        scratch_shapes=[pltpu.VMEM((PADSEQ, W), F32), pltpu.VMEM((PADSEQ, W), F32)],
        compiler_params=_cparams(("arbitrary", "arbitrary"), VMEM_BIG),
    )(q, kpad, vpad, bias, do)


NTAB = 2 * CLIP + 1
EXT = BAND + L


def _ext_onehot():
    n = _iota2((EXT, NTAB), 0)
    m = _iota2((EXT, NTAB), 1)
    return (jnp.clip(BAND - 1 - n, -CLIP, CLIP) + CLIP == m).astype(F32)


def bias_expand(table):
    def body(t_ref, o_ref):
        ext = _hdot_nt(t_ref[...], _ext_onehot())
        ext = jnp.concatenate([ext, jnp.zeros((NH, WIN - EXT), F32)], axis=-1)
        col = _iota2((NH, WIN), 1)
        for c in range(ATT_Q):
            inside = (col >= c * L) & (col < c * L + BAND)
            for i in range(L):
                shift = (c * L - (L - 1 - i)) % WIN
                o_ref[c, :, i, :] = jnp.where(inside, pltpu.roll(ext, shift, 1) if shift else ext, NEG)

    out = pl.pallas_call(body, name="bias_expand", out_shape=jax.ShapeDtypeStruct((ATT_Q, NH, L, WIN), F32))(table)
    return out.reshape(ATT_WINDOW_BIAS_SHAPE)


def bias_grad(dbias):
    def body(d_ref, o_ref):
        acc = jnp.zeros((NH, EXT), F32)
        zpad = jnp.zeros((NH, EXT - BAND), F32)
        for i in range(L):
            s = L - 1 - i
            row = jnp.concatenate([d_ref[:, i, :], zpad], axis=-1)
            acc = acc + (pltpu.roll(row, s, 1) if s else row)
        o_ref[...] = _hdot(acc, _ext_onehot())

    return pl.pallas_call(body, name="bias_grad", out_shape=jax.ShapeDtypeStruct((NH, NTAB), F32))(dbias)


def _group_cols(g):
    return slice(g * SGC, (g + 1) * SGC)


def _sg_norm(gv, lng, lnb):
    gc = gv - jnp.mean(gv, axis=-1, keepdims=True)
    rstd = lax.rsqrt(jnp.mean(gc * gc, axis=-1, keepdims=True) + LN_EPS)
    xhat = gc * rstd
    return xhat, rstd, xhat * lng + lnb


GMLP_BWD_CHUNKS = 2


def gmlp_fwd_loss(u, v, gate, lng, lnb, wm_bf, sgb_t, h, w_bf, g_final, target):
    tm = GMLP_BWD_CHUNKS * SGC

    def body(u_ref, v_ref, gt_ref, lng_ref, lnb_ref, wm_ref, sb_ref, h_ref, w_ref, g_ref, t_ref,
             dh_ref, loss_ref, dg_ref, zt_ref):
        zs = []
        for ch in range(GMLP_BWD_CHUNKS):
            rows = slice(ch * SGC, (ch + 1) * SGC)
            _, _, vln = _sg_norm(_gelu(v_ref[rows, :]), lng_ref[...], lnb_ref[...])
            vlb = vln.astype(BF16)
            zg = []
            for g in range(NG):
                cs = _group_cols(g)
                sv = jnp.dot(wm_ref[g], vlb[:, cs], preferred_element_type=F32) + sb_ref[:, g:g + 1]
                zg.append((_gelu(u_ref[rows, cs]) * sv * _silu(gt_ref[rows, cs])).astype(BF16))
            zs.append(jnp.concatenate(zg, axis=-1))
        z = jnp.concatenate(zs, axis=0)
        zt_ref[...] = z.T
        xv = h_ref[...] + jnp.dot(z, w_ref[...], preferred_element_type=F32)
        rstd = lax.rsqrt(jnp.mean(xv * xv, axis=-1, keepdims=True) + RMS_EPS)
        xhat = xv * rstd
        err = xhat * g_ref[...] - t_ref[...]
        part = 0.5 * jnp.sum(jnp.mean(err * err, axis=-1, keepdims=True), axis=0, keepdims=True)
        dout = err * (1.0 / D)

        @pl.when(pl.program_id(0) == 0)
        def _():
            loss_ref[...] = jnp.zeros_like(loss_ref)
            dg_ref[...] = jnp.zeros_like(dg_ref)

        loss_ref[...] += jnp.broadcast_to(part, loss_ref.shape)
        dg_ref[...] += jnp.sum(dout * xhat, axis=0, keepdims=True)
        dxh = dout * g_ref[...]
        dh_ref[...] = rstd * (dxh - xhat * jnp.mean(dxh * xhat, axis=-1, keepdims=True))

    return pl.pallas_call(
        body, grid=(T // tm,), name="gmlp_fwd_loss",
        in_specs=[_row_spec(tm, D)] * 3 + [_const_spec((1, D))] * 2
        + [_const_spec((NG, SGC, SGC)), _const_spec((SGC, NG)), _row_spec(tm, D), _const_spec((D, D)),
           _const_spec((1, D)), _row_spec(tm, D)],
        out_specs=[_row_spec(tm, D), _const_spec((8, 128)), _const_spec((1, D)), _col_spec(D, tm)],
        out_shape=[jax.ShapeDtypeStruct((T, D), F32), jax.ShapeDtypeStruct((8, 128), F32),
                   jax.ShapeDtypeStruct((1, D), F32), jax.ShapeDtypeStruct((D, T), BF16)],
        compiler_params=_cparams(("arbitrary",), VMEM_BIG),
    )(u, v, gate, lng, lnb, wm_bf, sgb_t, h, w_bf, g_final, target)


def gmlp_bwd(u, v, gate, lng, lnb, wm_bf, sgb_t, dh, zt_bf, w_bf):
    def body(u_ref, v_ref, gt_ref, lng_ref, lnb_ref, wm_ref, sb_ref, dh_ref, zt_ref, w_ref,
             du_ref, dv_ref, dgt_ref, dlng_ref, dlnb_ref, dwm_ref, dsb_ref, dw_ref):
        @pl.when(pl.program_id(0) == 0)
        def _():
            for ref in (dlng_ref, dlnb_ref, dwm_ref, dsb_ref):
                ref[...] = jnp.zeros_like(ref)

        dz = _out_proj_back(dh_ref, zt_ref, w_ref, dw_ref)
        sel = (_iota2((D, NG), 0) // SGC == _iota2((D, NG), 1)).astype(F32)
        for ch in range(GMLP_BWD_CHUNKS):
            rows = slice(ch * SGC, (ch + 1) * SGC)
            gv, dgv_dv = _gelu_both(v_ref[rows, :])
            xhat, rstd, vln = _sg_norm(gv, lng_ref[...], lnb_ref[...])
            vlb = vln.astype(BF16)
            dvln = []
            dsv_all = []
            for g in range(NG):
                cs = _group_cols(g)
                uu = u_ref[rows, cs]
                gg = gt_ref[rows, cs]
                dzz = dz[rows, cs]
                sv = jnp.dot(wm_ref[g], vlb[:, cs], preferred_element_type=F32) + sb_ref[:, g:g + 1]
                gu, dgu = _gelu_both(uu)
                sg, dsg = _silu_both(gg)
                dzgu = dzz * gu
                dsv = dzgu * sg
                dgt_ref[rows, cs] = (dzgu * sv * dsg).astype(BF16)
                du_ref[rows, cs] = (dzz * sv * sg * dgu).astype(BF16)
                dsb16 = dsv.astype(BF16)
                dvln.append(lax.dot_general(wm_ref[g], dsb16, (((0,), (0,)), ((), ())), preferred_element_type=F32))
                dwm_ref[g] += lax.dot_general(dsb16, vlb[:, cs], (((1,), (1,)), ((), ())),
                                              preferred_element_type=F32)
                dsv_all.append(dsv)
            dvl = jnp.concatenate(dvln, axis=-1)
            dsb_ref[...] += _hdot(jnp.concatenate(dsv_all, axis=-1), sel)
            dlng_ref[...] += jnp.sum(dvl * xhat, axis=0, keepdims=True)
            dlnb_ref[...] += jnp.sum(dvl, axis=0, keepdims=True)
            dxh = dvl * lng_ref[...]
            dgv = rstd * (dxh - jnp.mean(dxh, axis=-1, keepdims=True)
                          - xhat * jnp.mean(dxh * xhat, axis=-1, keepdims=True))
            dv_ref[rows, :] = (dgv * dgv_dv).astype(BF16)

    tm = GMLP_BWD_CHUNKS * SGC
    return pl.pallas_call(
        body, grid=(T // tm,), name="gmlp_bwd",
        in_specs=[_row_spec(tm, D)] * 3 + [_const_spec((1, D))] * 2
        + [_const_spec((NG, SGC, SGC)), _const_spec((SGC, NG)), _row_spec(tm, D), _col_spec(D, tm),
           _const_spec((D, D))],
        out_specs=[_row_spec(tm, D)] * 3 + [_const_spec((1, D))] * 2
        + [_const_spec((NG, SGC, SGC)), _const_spec((SGC, NG)), _const_spec((D, D))],
        out_shape=[jax.ShapeDtypeStruct((T, D), BF16)] * 3 + [jax.ShapeDtypeStruct((1, D), F32)] * 2
        + [jax.ShapeDtypeStruct((NG, SGC, SGC), F32), jax.ShapeDtypeStruct((SGC, NG), F32),
           jax.ShapeDtypeStruct((D, D), F32)],
        compiler_params=_cparams(("arbitrary",), VMEM_BIG),
    )(u, v, gate, lng, lnb, wm_bf, sgb_t, dh, zt_bf, w_bf)


NCHIP = 4
NDEV = 8
ANY = pl.BlockSpec(memory_space=pl.ANY)


HBM = pl.BlockSpec(memory_space=pltpu.HBM)
SEM = pl.BlockSpec(memory_space=pltpu.SEMAPHORE)
EFFECT = pltpu.SideEffectType.DATAFLOW_SIDE_EFFECTING


CHIPS, EVERY, SIBLING = "chips", "every", "sibling"
SLOTS = {CHIPS: NCHIP, EVERY: NDEV, SIBLING: 1}


def _peers(scope):
    x, y, c = lax.axis_index("x"), lax.axis_index("y"), lax.axis_index("c")
    if scope == SIBLING:
        return [((x, y, 1 - c), 0)], 0
    if scope == CHIPS:
        return [((px, py, c), 2 * px + py) for px, py in ((1 - x, y), (x, 1 - y), (1 - x, 1 - y))], 2 * x + y
    out = []
    for j in range(1, NDEV):
        px, py, pc = x ^ (j >> 2), y ^ ((j >> 1) & 1), c ^ (j & 1)
        out.append(((px, py, pc), 4 * px + 2 * py + pc))
    return out, 4 * x + 2 * y + c


def _send_copies(src, land, send, recv, scatter, scope, starting):
    peers, me = _peers(scope)
    copies = []
    for t in range(len(src)):
        for j, (dev, slot) in enumerate(peers):
            k = t * len(peers) + j
            copies.append(pltpu.make_async_remote_copy(
                src_ref=src[t].at[slot] if scatter else src[t], dst_ref=land[t].at[me if starting else slot],
                send_sem=send.at[k], recv_sem=recv.at[k], device_id=dev, device_id_type=MESH))
    return copies


def _own_copies(src, land, sems, scatter, scope):
    if scope == SIBLING:
        return []
    _, me = _peers(scope)
    return [pltpu.make_async_copy(src[t].at[me] if scatter else src[t], land[t].at[me], sems.at[t])
            for t in range(len(src))]


def send_start(srcs, scatter, scope, name, after=None):
    n = len(srcs)
    slots = SLOTS[scope]
    extra_specs, extra = _after_operand(after)
    lands = [pltpu.HBM(a.shape if scatter else (slots,) + a.shape, a.dtype) for a in srcs]
    sems = [pltpu.SemaphoreType.DMA((n * max(slots - 1, 1),))] * 2 + ([] if scope == SIBLING else
                                                                     [pltpu.SemaphoreType.DMA((n,))])
    k = len(sems)

    def body(*refs):
        first_out = n + len(extra)
        src, land = refs[:n], refs[first_out + k + n:first_out + k + 2 * n]
        for cp in _send_copies(src, land, refs[first_out], refs[first_out + 1], scatter, scope, True):
            cp.start()
        for cp in _own_copies(src, land, refs[first_out + k - 1], scatter, scope):
            cp.start()
        refs[-1][...] = jnp.zeros_like(refs[-1])

    out = pl.pallas_call(
        body, name=name,
        out_shape=(*sems, *[pltpu.HBM(a.shape, a.dtype) for a in srcs], *lands, jax.ShapeDtypeStruct((8, 128), F32)),
        in_specs=[HBM] * n + extra_specs,
        out_specs=(*[SEM] * k, *[HBM] * (2 * n), pl.BlockSpec(memory_space=pltpu.VMEM)),
        input_output_aliases={i: k + i for i in range(n)},
        compiler_params=pltpu.CompilerParams(has_side_effects=EFFECT),
    )(*[pltpu.with_memory_space_constraint(a, pltpu.HBM) for a in srcs], *extra)
    return list(out[:k]), list(out[k:k + n]), list(out[k + n:k + 2 * n]), out[-1]


def send_wait(started, after, scatter, scope, name):
    sems, srcs, lands, _ = started
    n, k = len(srcs), len(sems)

    def body(*refs):
        src, land = refs[:n], refs[n:2 * n]
        for cp in _own_copies(src, land, refs[2 * n + k - 1], scatter, scope):
            cp.wait()
        for cp in _send_copies(src, land, refs[2 * n], refs[2 * n + 1], scatter, scope, False):
            cp.wait_send()
            cp.wait_recv()

    arrs = list(srcs) + list(lands)
    out = pl.pallas_call(
        body, name=name, out_shape=tuple(pltpu.HBM(a.shape, a.dtype) for a in arrs),
        in_specs=[HBM] * (2 * n) + [SEM] * k + [ANY], out_specs=tuple([HBM] * (2 * n)),
        input_output_aliases={i: i for i in range(2 * n)},
        compiler_params=pltpu.CompilerParams(has_side_effects=EFFECT),
    )(*arrs, *sems, after)
    return list(out[n:])


def gather_weights(arrs, split):
    n = len(arrs)

    def body(*refs):
        ins, outs = refs[:n], refs[n:2 * n]
        send1, recv1, send2, recv2, loc_in, loc_out = refs[2 * n:2 * n + 6]
        staged = refs[2 * n + 6:]
        x, y, c = lax.axis_index("x"), lax.axis_index("y"), lax.axis_index("c")
        me = 2 * x + y
        sibling = (x, y, 1 - c)
        peers = [(1 - x, y), (x, 1 - y), (1 - x, 1 - y)]

        def rows_of(t, core):
            half = arrs[t].shape[0] // 2
            return pl.ds(core * half, half)

        def part(ref, t, core):
            return ref.at[rows_of(t, core)] if split[t] else ref

        load = [pltpu.make_async_copy(ins[t], staged[t], loc_in.at[t]) for t in range(n)]
        store = [pltpu.make_async_copy(staged[t], outs[t].at[me], loc_out.at[t]) for t in range(n)]
        for cp in load:
            cp.start()
        first = []
        for t in range(n):
            for j, (px, py) in enumerate(peers):
                first.append(pltpu.make_async_remote_copy(
                    src_ref=part(ins[t], t, c), dst_ref=part(outs[t].at[me], t, c), send_sem=send1.at[t, j],
                    recv_sem=recv1.at[t, j], device_id=(px, py, c), device_id_type=MESH))
        for cp in first:
            cp.start()
        for cp_in, cp_out in zip(load, store):
            cp_in.wait()
            cp_out.start()
        passed = []
        for t in range(n):
            for j, (px, py) in enumerate(peers):
                landed = part(outs[t].at[2 * px + py], t, c)
                pltpu.make_async_remote_copy(
                    src_ref=landed, dst_ref=landed, send_sem=send1.at[t, j], recv_sem=recv1.at[t, j],
                    device_id=(x, y, c), device_id_type=MESH).wait_recv()
                if split[t]:
                    cp = pltpu.make_async_remote_copy(
                        src_ref=landed, dst_ref=landed, send_sem=send2.at[t, j], recv_sem=recv2.at[t, j],
                        device_id=sibling, device_id_type=MESH)
                    cp.start()
                    passed.append(cp)
        for t in range(n):
            for j, (px, py) in enumerate(peers):
                if split[t]:
                    other = part(outs[t].at[2 * px + py], t, 1 - c)
                    pltpu.make_async_remote_copy(
                        src_ref=other, dst_ref=other, send_sem=send2.at[t, j], recv_sem=recv2.at[t, j],
                        device_id=(x, y, c), device_id_type=MESH).wait_recv()
        for cp in first + passed:
            cp.wait_send()
        for cp in store:
            cp.wait()

    return pl.pallas_call(
        body, name="gather_weights", in_specs=[ANY] * n, out_specs=[ANY] * n,
        out_shape=[jax.ShapeDtypeStruct((NCHIP,) + a.shape, a.dtype) for a in arrs],
        scratch_shapes=[pltpu.SemaphoreType.DMA((n, 3))] * 4 + [pltpu.SemaphoreType.DMA((n,))] * 2
        + [pltpu.VMEM(a.shape, a.dtype) for a in arrs],
    )(*arrs)


def _adam_math(g, w, m, v):
    m = ADAM_B1 * m + (1.0 - ADAM_B1) * g
    v = ADAM_B2 * v + (1.0 - ADAM_B2) * (g * g)
    m_hat = m / (1.0 - ADAM_B1 ** ADAM_STEP)
    v_hat = v / (1.0 - ADAM_B2 ** ADAM_STEP)
    delta = -ADAM_LR * (m_hat / (jnp.sqrt(v_hat) + ADAM_EPS) + ADAM_WD * w)
    return delta, m, v


def _rows_tile(rows):
    return rows if rows <= 256 else 256


def sum_chips(parts, name):
    _, rows, cols = parts.shape
    tr = _rows_tile(rows)

    def body(p_ref, o_ref):
        acc = p_ref[0].astype(F32)
        for s in range(1, NCHIP):
            acc = acc + p_ref[s].astype(F32)
        o_ref[...] = acc

    return pl.pallas_call(
        body, grid=(rows // tr,), name=name,
        in_specs=[pl.BlockSpec((NCHIP, tr, cols), lambda i: (0, i, 0))],
        out_specs=pl.BlockSpec((tr, cols), lambda i: (i, 0)),
        out_shape=jax.ShapeDtypeStruct((rows, cols), F32),
        compiler_params=_cparams(("parallel",)),
    )(parts)


def sum_chips_small(parts, name):
    n = len(parts)

    def body(*refs):
        for p_ref, o_ref in zip(refs[:n], refs[n:]):
            acc = p_ref[0]
            for s in range(1, NCHIP):
                acc = acc + p_ref[s]
            o_ref[...] = acc

    return pl.pallas_call(body, name=name, out_shape=[jax.ShapeDtypeStruct(p.shape[1:], F32) for p in parts])(*parts)


def adam_shard_small(items, name):
    n = len(items)

    def body(*refs):
        for t in range(n):
            a_ref, b_ref, w_ref, m_ref, v_ref = refs[5 * t:5 * t + 5]
            g_ref, d_ref, mo_ref, vo_ref = refs[5 * n + 4 * t:5 * n + 4 * t + 4]
            g = (a_ref[...] + b_ref[...]).reshape(w_ref.shape)
            g_ref[...] = g
            d_ref[...], mo_ref[...], vo_ref[...] = _adam_math(g, w_ref[...], m_ref[...], v_ref[...])

    out = pl.pallas_call(
        body, name=name, out_shape=[jax.ShapeDtypeStruct(it[2].shape, F32) for it in items for _ in range(4)],
    )(*[a for it in items for a in it])
    return [out[4 * t:4 * t + 4] for t in range(n)]


def adam_shard(p_mine, p_sib, w, m, v, name):
    rows, cols = p_mine.shape
    tr = _rows_tile(rows)
    lead = w.ndim == 3

    def body(a_ref, b_ref, w_ref, m_ref, v_ref, g_ref, d_ref, mo_ref, vo_ref):
        g = a_ref[...] + b_ref[...]
        g = g[None] if lead else g
        g_ref[...] = g
        d_ref[...], mo_ref[...], vo_ref[...] = _adam_math(g, w_ref[...], m_ref[...], v_ref[...])

    flat = pl.BlockSpec((tr, cols), lambda i: (i, 0))
    spec = pl.BlockSpec((1, tr, cols), lambda i: (0, i, 0)) if lead else flat
    return pl.pallas_call(
        body, grid=(rows // tr,), name=name, in_specs=[flat] * 2 + [spec] * 3, out_specs=[spec] * 4,
        out_shape=[jax.ShapeDtypeStruct(w.shape, F32)] * 4,
        compiler_params=_cparams(("parallel",)),
    )(p_mine, p_sib, w, m, v)


def adam_shard_halves_t(r_mine, r_sib, wt, mt, vt, name):
    hrows, cols = r_mine.shape
    tr = _rows_tile(hrows)
    per_half = hrows // tr

    def body(a_ref, b_ref, w_ref, m_ref, v_ref, g_ref, d_ref, mo_ref, vo_ref):
        mine = pl.program_id(0) == lax.axis_index("c")
        g = jnp.where(mine, a_ref[...], b_ref[...]).T[None]
        g_ref[...] = g
        d_ref[...], mo_ref[...], vo_ref[...] = _adam_math(g, w_ref[...], m_ref[...], v_ref[...])

    flat = pl.BlockSpec((tr, cols), lambda h, i: (i, 0))
    spec = pl.BlockSpec((1, cols, tr), lambda h, i: (0, 0, h * per_half + i))
    return pl.pallas_call(
        body, grid=(2, per_half), name=name, in_specs=[flat] * 2 + [spec] * 3, out_specs=[spec] * 4,
        out_shape=[jax.ShapeDtypeStruct(wt.shape, F32)] * 4,
        compiler_params=_cparams(("parallel", "parallel")),
    )(r_mine, r_sib, wt, mt, vt)


def adam_replicated(gathered, params, name):
    flat = []
    for i, p in enumerate(params):
        if isinstance(p, list):
            off = 0
            for wmv in p:
                n = gathered[i].shape[-1] - off if wmv[0] is None else wmv[0].shape[-1]
                flat.append((i, (off, n), wmv))
                off += n
        else:
            flat.append((i, None, p))
    ins = [a for _, _, wmv in flat for a in wmv if a is not None]
    ng = len(gathered)

    def body(*refs):
        g_refs = refs[:ng]
        in_refs = list(refs[ng:ng + len(ins)])
        out_refs = list(refs[ng + len(ins):])
        sums = []
        for r in g_refs:
            g = r[0]
            for d in range(1, NDEV):
                g = g + r[d]
            sums.append(g)
        for i, lanes, wmv in flat:
            g = sums[i] if lanes is None else sums[i][:, lanes[0]:lanes[0] + lanes[1]]
            out_refs.pop(0)[...] = g
            if wmv[0] is not None:
                w_ref, m_ref, v_ref = in_refs.pop(0), in_refs.pop(0), in_refs.pop(0)
                d_ref, mo_ref, vo_ref = out_refs.pop(0), out_refs.pop(0), out_refs.pop(0)
                d_ref[...], mo_ref[...], vo_ref[...] = _adam_math(g, w_ref[...], m_ref[...], v_ref[...])

    out_shape = []
    for i, lanes, wmv in flat:
        shape = gathered[i].shape[1:] if lanes is None else (1, lanes[1])
        out_shape += [jax.ShapeDtypeStruct(shape, F32)] * (4 if wmv[0] is not None else 1)
    outs = list(pl.pallas_call(body, name=name, out_shape=out_shape)(*gathered, *ins))
    return [[outs.pop(0) for _ in range(4 if wmv[0] is not None else 1)] for _, _, wmv in flat]


EVEN_SPLITS = (SHIFT, W, W, W, W, W)
ODD_SPLITS = (D, D, D)


def _cols_to_chips(a):
    rows, cols = a.shape
    return a.reshape(rows, NCHIP, cols // NCHIP).transpose(1, 0, 2)


def _chips_to_cols(a):
    _, rows, n = a.shape
    return a.transpose(1, 0, 2).reshape(rows, NCHIP * n)


def kernel(x, norm_g, w_in_e, shift_mu, rw_w0, rw_w2, rw_a0, rw_a2, rw_kk, rw_ka, rw_rk, rw_lnx_g, rw_lnx_b, att_bias, w_out_e, w_in_o, sg_ln_g, sg_ln_b, sg_w, sg_b, w_out_o, final_g, loss_target, m_norm_g, m_w_in_e, m_shift_mu, m_rw_w0, m_rw_w2, m_rw_a0, m_rw_a2, m_rw_kk, m_rw_ka, m_rw_rk, m_rw_lnx_g, m_rw_lnx_b, m_att_bias, m_w_out_e, m_w_in_o, m_sg_ln_g, m_sg_ln_b, m_sg_w, m_sg_b, m_w_out_o, m_final_g, v_norm_g, v_w_in_e, v_shift_mu, v_rw_w0, v_rw_w2, v_rw_a0, v_rw_a2, v_rw_kk, v_rw_ka, v_rw_rk, v_rw_lnx_g, v_rw_lnx_b, v_att_bias, v_w_out_e, v_w_in_o, v_sg_ln_g, v_sg_ln_b, v_sg_w, v_sg_b, v_w_out_o, v_final_g):
    x2 = x.reshape(T, D)
    tgt = loss_target.reshape(T, D)

    gathered = gather_weights(
        [jnp.swapaxes(w_in_e[0], 0, 1).astype(BF16), jnp.concatenate([rw_w2[0], rw_a2[0]], axis=0),
         jnp.concatenate([sg_ln_g, sg_ln_b], axis=0)], [True, True, False])
    wie = gathered[0].reshape(EVEN_IN, D)
    w2 = _chips_to_cols(gathered[1][:, :LORA])
    a2 = _chips_to_cols(gathered[1][:, LORA:])
    sglg = _chips_to_cols(gathered[2][:, 0:1])
    sglb = _chips_to_cols(gathered[2][:, 1:2])

    late = [w_out_e[0].astype(BF16), w_in_o[0].astype(BF16), w_out_o[0].astype(BF16)]
    late_started = send_start(late, False, CHIPS, "late_weights_start", after=gathered[0])

    def late_weights(after):
        woe, wio, woo = send_wait(late_started, after, False, CHIPS, "late_weights_wait")
        return woe.reshape(D, D), _chips_to_cols(wio), woo.reshape(D, D)

    def scatter_start(grads, name):
        return send_start([g_.astype(BF16) if g_.shape[-1] >= W else g_ for g_ in grads], True, CHIPS, name)

    started = {}

    def on_odd_grads(d_woo, d_wio):
        started["odd"] = scatter_start([d_woo.reshape(NCHIP, D // NCHIP, D), d_wio], "odd_grads_start")
        return started["odd"][-1]

    def on_even_grads(big_g):
        d_wie_half, d_woe, _, _, d_w2, d_a2, d_sglg, d_sglb = big_g
        blocks = [d_wie_half, d_woe.reshape(NCHIP, D // NCHIP, D), _cols_to_chips(d_w2), _cols_to_chips(d_a2),
                  _cols_to_chips(d_sglg), _cols_to_chips(d_sglb)]
        started["even"] = scatter_start(blocks, "even_grads_start")
        return started["even"][-1]

    def on_small_grads(layer, grads):
        if layer == "odd":
            d_sg_w, d_sg_b, d_final, d_g1 = grads
            mine = [d_sg_w.reshape(NG * SGC, SGC), d_sg_b, jnp.concatenate([d_final, d_g1], axis=1)]
        else:
            mine = [grads[-2], jnp.concatenate(grads[:-2] + grads[-1:], axis=1)]
        started[layer + "_small"] = send_start(mine, False, EVERY, layer + "_small_grads_start")
        return started[layer + "_small"][-1]

    loss_part, dx, _, _ = _local_step(
        x2, tgt, wie, late_weights, w2, a2, sglg, sglb, norm_g, shift_mu, rw_w0, rw_a0, rw_kk, rw_ka, rw_rk,
        rw_lnx_g, rw_lnx_b, att_bias, sg_w, sg_b, final_g, first_after=late_started[-1], on_odd_grads=on_odd_grads,
        on_even_grads=on_even_grads, on_small_grads=on_small_grads)
    wmv = {"w_in_e": tuple(jnp.swapaxes(a, 1, 2) for a in (w_in_e, m_w_in_e, v_w_in_e)),
           "w_out_e": (w_out_e, m_w_out_e, v_w_out_e),
           "w_in_o": (w_in_o, m_w_in_o, v_w_in_o), "w_out_o": (w_out_o, m_w_out_o, v_w_out_o),
           "rw_w2": (rw_w2, m_rw_w2, v_rw_w2), "rw_a2": (rw_a2, m_rw_a2, v_rw_a2),
           "sg_ln_g": (sg_ln_g, m_sg_ln_g, v_sg_ln_g), "sg_ln_b": (sg_ln_b, m_sg_ln_b, v_sg_ln_b)}
    sharded = {}

    def sum_and_swap(names, landed, tag):
        nbig = sum(p_.dtype == BF16 for p_ in landed)
        partial = [sum_chips(p_, "sum_" + nm) for p_, nm in zip(landed[:nbig], names)]
        if nbig < len(names):
            partial += sum_chips_small(landed[nbig:], "sum_small_" + tag)
        return send_start(partial, False, SIBLING, "swap_partials_" + tag + "_start")

    def update(names, swap_started, after, tag):
        landed = send_wait(swap_started, after, False, SIBLING, "swap_partials_" + tag + "_wait")
        partial, from_sibling = swap_started[1], [a[0] for a in landed]
        nbig = sum(p_.shape[-1] >= W for p_ in partial)
        for nm, mine, sib in zip(names[:nbig], partial, from_sibling):
            if nm == "w_in_e":
                res = adam_shard_halves_t(mine, sib, *wmv[nm], "adam_" + nm)
                sharded[nm] = [jnp.swapaxes(a, 1, 2) for a in res]
            else:
                sharded[nm] = adam_shard(mine, sib, *wmv[nm], "adam_" + nm)
        if nbig < len(names):
            items = [(mine, sib, *wmv[nm]) for nm, mine, sib in zip(names, partial, from_sibling)][nbig:]
            for nm, res in zip(names[nbig:], adam_shard_small(items, "adam_small_" + tag)):
                sharded[nm] = res

    odd_names = ["w_out_o", "w_in_o"]
    even_names = ["w_in_e", "w_out_e", "rw_w2", "rw_a2", "sg_ln_g", "sg_ln_b"]
    odd_landed = send_wait(started["odd"], started["even_small"][-1], True, CHIPS, "odd_grads_wait")
    odd_swap = sum_and_swap(odd_names, odd_landed, "odd")
    done = odd_swap[-1]

    def wmv_of(*arrs, view=lambda a: a):
        return tuple(view(a) for a in arrs)

    vec = lambda a: a.reshape(1, -1)
    groups = {
        "odd": (["sg_w", "sg_b", "final_g", "norm_g1"],
                [wmv_of(sg_w, m_sg_w, v_sg_w, view=lambda a: a.reshape(NG * SGC, SGC)),
                 wmv_of(sg_b, m_sg_b, v_sg_b, view=lambda a: a[0]),
                 [wmv_of(final_g, m_final_g, v_final_g, view=vec),
                  wmv_of(norm_g, m_norm_g, v_norm_g, view=lambda a: a[1:2])]]),
        "even": (["att_bias", "norm_g0", "shift_mu", "rw_w0", "rw_a0", "rw_kk", "rw_ka", "rw_rk", "rw_lnx_g",
                  "rw_lnx_b", "loss"],
                 [wmv_of(att_bias, m_att_bias, v_att_bias, view=lambda a: a[0]),
                  [wmv_of(norm_g, m_norm_g, v_norm_g, view=lambda a: a[0:1]),
                   wmv_of(shift_mu, m_shift_mu, v_shift_mu), wmv_of(rw_w0, m_rw_w0, v_rw_w0),
                   wmv_of(rw_a0, m_rw_a0, v_rw_a0), wmv_of(rw_kk, m_rw_kk, v_rw_kk), wmv_of(rw_ka, m_rw_ka, v_rw_ka),
                   wmv_of(rw_rk, m_rw_rk, v_rw_rk, view=vec), wmv_of(rw_lnx_g, m_rw_lnx_g, v_rw_lnx_g),
                   wmv_of(rw_lnx_b, m_rw_lnx_b, v_rw_lnx_b), (None, None, None)]]),
    }
    rep = {}
    for layer in ("odd", "even"):
        nms, params = groups[layer]
        gathered_g = send_wait(started[layer + "_small"], done, False, EVERY, layer + "_small_grads_wait")
        for nm, res in zip(nms, adam_replicated(gathered_g, params, "adam_" + layer + "_small")):
            rep[nm] = res
        done = rep[nms[0]][0]
    native = {"sg_w": sg_w.shape, "sg_b": sg_b.shape, "final_g": final_g.shape, "rw_rk": rw_rk.shape,
              "att_bias": att_bias.shape}
    for nm, shape in native.items():
        rep[nm] = [a.reshape(shape) for a in rep[nm]]
    rep["norm_g"] = [jnp.concatenate([a, b], axis=0) for a, b in zip(rep["norm_g0"], rep["norm_g1"])]
    even_landed = send_wait(started["even"], done, True, CHIPS, "even_grads_wait")
    even_swap = sum_and_swap(even_names, even_landed, "even")
    update(odd_names, odd_swap, even_swap[-1], "odd")
    update(even_names, even_swap, sharded["w_in_o"][0], "even")

    order = ["norm_g", "w_in_e", "shift_mu", "rw_w0", "rw_w2", "rw_a0", "rw_a2", "rw_kk", "rw_ka", "rw_rk",
             "rw_lnx_g", "rw_lnx_b", "att_bias", "w_out_e", "w_in_o", "sg_ln_g", "sg_ln_b", "sg_w", "sg_b",
             "w_out_o", "final_g"]
    results = {**sharded, **rep}
    outs = [rep["loss"][0][0, 0], dx.reshape(NSEQ, SEQ, D)]
    for kind in range(4):
        outs += [results[nm][kind] for nm in order]
    return tuple(outs)


def _local_step(x2, tgt, wie_t, late_weights, w2, a2, sglg, sglb, norm_g, shift_mu, rw_w0, rw_a0, rw_kk, rw_ka, rw_rk,
                rw_lnx_g, rw_lnx_b, att_bias, sg_w, sg_b, final_g, first_after=None, on_odd_grads=None,
                on_even_grads=None, on_small_grads=None):
    zl = jnp.zeros((LORA, W), F32)
    w2x = jnp.concatenate([w2, zl], axis=0)
    a2x = jnp.concatenate([zl, a2], axis=0)
    rk = rw_rk.reshape(1, W)
    pos = np.arange(SGC)
    sg_mask = jnp.asarray(((pos[None, :] // L) <= (pos[:, None] // L)).astype(np.float32))
    wm = (sg_w[0] * sg_mask[None]).astype(BF16)
    sgb_t = sg_b[0].T

    xn0, ps, ga, q, kb, vb, gb = ln_in_proj(x2, norm_g[0:1], wie_t, EVEN_SPLITS, "in_proj_even", after=first_after,
                                            w_t=True, bf16_pieces=(2, 3, 4))
    r, lw, k2, v, aa, bb = even_prep(ps, shift_mu, rw_w0, w2x, rw_a0, a2x, rw_kk, rw_ka)
    y, rw_saved = rwkv_fwd(r, lw, k2, v, aa, bb)
    bias = bias_expand(att_bias[0])

    def padded(a):
        return jnp.pad(a.astype(BF16).reshape(NSEQ, SEQ, W), ((0, 0), (LEFT * L, 0), (0, 0))).reshape(NSEQ * PADSEQ, W)

    kpad, vpad = padded(kb), padded(vb)
    o = attention_fwd(q, kpad, vpad, bias)
    woe, wio, woo = late_weights(o)
    h1, zt = even_post(y, r, k2, v, ga, o, gb, rw_lnx_g, rw_lnx_b, rk, x2, woe)
    xn1, u, vv, gt = ln_in_proj(h1, norm_g[1:2], wio, ODD_SPLITS, "in_proj_odd")
    dh2, loss_part, d_final_g, z2t = gmlp_fwd_loss(u, vv, gt, sglg, sglb, wm, sgb_t, h1, woo, final_g[None], tgt)

    du, dvv, dgt, d_sglg, d_sglb, d_wm, d_sgb_t, d_woo = gmlp_bwd(u, vv, gt, sglg, sglb, wm, sgb_t, dh2, z2t, woo)
    dp_odd = [du, dvv, dgt]
    d_wio = matmul_acc_chips(xn1, dp_odd, "in_proj_odd_dw")
    token = on_odd_grads(d_woo, d_wio) if on_odd_grads else None
    dh1, d_g1 = in_proj_bwd_x(h1, norm_g[1:2], wio, dp_odd, dh2, "in_proj_odd_bwd", after=token)
    odd_small = [d_wm * sg_mask[None], d_sgb_t.T, d_final_g, d_g1]
    token = on_small_grads("odd", odd_small) if on_small_grads else None
    dy, dr2, dk22, dv2, dga, do, dgb, d_lng, d_lnb, d_rk, d_woe = even_post_bwd(
        y, r, k2, v, ga, o, gb, rw_lnx_g, rw_lnx_b, rk, dh1, zt, woe, after=token)
    dq, dkb, dvb, dbias = attention_bwd(q, kpad, vpad, bias, do)
    dbias = sum(dbias[:, i * 2 * L:(i + 1) * 2 * L, i * L:i * L + BAND] for i in range(ATT_Q))
    d_att_bias = bias_grad(dbias.reshape(NH, L, BAND))
    dr, dlw, dk2, dv, daa, dbb = rwkv_bwd(r, lw, k2, aa, bb, rw_saved, dy)
    dps, d_mu, d_w0, d_w2x, d_a0, d_a2x, d_kk, d_ka = even_prep_bwd(
        ps, shift_mu, rw_w0, w2x, rw_a0, a2x, rw_kk, rw_ka, dr, dlw, dk2, dv, daa, dbb, dr2, dk22, dv2)
    dp_even = [dps, dga, dq, dkb, dvb, dgb]
    d_wie = matmul_acc_chips(xn0, dp_even, "in_proj_even_dw", add_cores=on_even_grads is not None)
    big_g = (d_wie, d_woe, d_wio, d_woo, d_w2x[:LORA], d_a2x[LORA:], d_sglg, d_sglb)
    token = on_even_grads(big_g) if on_even_grads else None
    dx, d_g0 = in_proj_bwd_x(x2, norm_g[0:1], wie_t, dp_even, dh1, "in_proj_even_bwd", after=token, w_t=True)
    even_small = [d_g0, d_mu, d_w0, d_a0, d_kk, d_ka, d_rk, d_lng, d_lnb, d_att_bias]
    if on_small_grads:
        on_small_grads("even", even_small + [loss_part[0:1, :]])
    rep_g = [jnp.concatenate([d_g0, d_g1], axis=0)] + even_small[1:] + odd_small[:3]
    return loss_part[0, 0], dx, big_g, rep_g
```

```python
import functools
import math

import jax
import jax.numpy as jnp
import numpy as np
from jax import lax
from jax.experimental import pallas as pl
from jax.experimental.pallas import tpu as pltpu

F32 = jnp.float32
BF16 = jnp.bfloat16
HI = lax.Precision.HIGHEST

D = 1024
SEQ = 2048
NSEQ = 2
T = NSEQ * SEQ
HD = 64
NH = 8
W = 512
SHIFT = 1664
LORA = 64
EVEN_IN = 4224
ODD_IN = 3072
L = 64
NC = SEQ // L
LEFT = 8
BAND = (LEFT + 1) * L
CLIP = 128
SGC = 128
NG = 8
RMS_EPS = 1e-6
LN_EPS = 1e-5
GN_EPS = 64e-5
NEG = -1e30
VMEM_BIG = 56 * 1024 * 1024

ADAM_LR = 0.001
ADAM_B1 = 0.9
ADAM_B2 = 0.999
ADAM_EPS = 1e-08
ADAM_WD = 0.01
ADAM_STEP = 10

MESH = pl.DeviceIdType.MESH


def _bdot(a, b):
    return jnp.dot(a.astype(BF16), b.astype(BF16), preferred_element_type=F32)


def _bdot_nt(a, b):
    return lax.dot_general(a.astype(BF16), b.astype(BF16), (((1,), (1,)), ((), ())), preferred_element_type=F32)


def _bdot_tn(a, b):
    return lax.dot_general(a.astype(BF16), b.astype(BF16), (((0,), (0,)), ((), ())), preferred_element_type=F32)


def _hdot(a, b):
    return jnp.dot(a, b, precision=HI, preferred_element_type=F32)


def _hdot_nt(a, b):
    return lax.dot_general(a, b, (((1,), (1,)), ((), ())), precision=HI, preferred_element_type=F32)


def _hdot_tn(a, b):
    return lax.dot_general(a, b, (((0,), (0,)), ((), ())), precision=HI, preferred_element_type=F32)


def _iota2(shape, dim):
    return lax.broadcasted_iota(jnp.int32, shape, dim)


def _head_blockdiag():
    r = _iota2((2 * HD, 2 * HD), 0) // HD
    c = _iota2((2 * HD, 2 * HD), 1) // HD
    return (r == c).astype(BF16)


def _headsum_impl(x, bd):
    hi = x.astype(BF16)
    mid = (x - hi.astype(F32)).astype(BF16)
    n = bd.shape[0]
    out = [jnp.dot(hi[:, i:i + n], bd, preferred_element_type=F32) + jnp.dot(mid[:, i:i + n], bd, preferred_element_type=F32)
           for i in range(0, x.shape[1], n)]
    return jnp.concatenate(out, axis=-1)


@jax.custom_vjp
def _headsum(x, bd):
    return _headsum_impl(x, bd)


def _headsum_fwd(x, bd):
    return _headsum_impl(x, bd), bd


def _headsum_bwd(bd, ct):
    return _headsum_impl(ct, bd), None


_headsum.defvjp(_headsum_fwd, _headsum_bwd)


def _silu(x):
    return x * jax.nn.sigmoid(x)


def _dsilu(x):
    s = jax.nn.sigmoid(x)
    return s * (1.0 + x * (1.0 - s))


_GELU_C = math.sqrt(2.0 / math.pi)


def _gelu(x):
    return 0.5 * x * (1.0 + jnp.tanh(_GELU_C * (x + 0.044715 * (x * x * x))))


def _dgelu(x):
    t = jnp.tanh(_GELU_C * (x + 0.044715 * (x * x * x)))
    return 0.5 * (1.0 + t) + 0.5 * x * (1.0 - t * t) * _GELU_C * (1.0 + 3.0 * 0.044715 * x * x)


def _silu_both(x):
    s = jax.nn.sigmoid(x)
    xs = x * s
    return xs, s + xs * (1.0 - s)


def _gelu_both(x):
    x2 = x * x
    t = jnp.tanh(_GELU_C * (x + 0.044715 * (x2 * x)))
    half = 0.5 * (1.0 + t)
    return x * half, half + 0.5 * x * (1.0 - t * t) * _GELU_C * (1.0 + 3.0 * 0.044715 * x2)


def _softplus(x):
    return jnp.maximum(x, 0.0) + jnp.log(1.0 + jnp.exp(-jnp.abs(x)))


def _cparams(sem, vmem=None):
    return pltpu.CompilerParams(dimension_semantics=sem, vmem_limit_bytes=vmem)


def _row_spec(tm, width):
    return pl.BlockSpec((tm, width), lambda i: (i, 0))


def _col_spec(height, tm):
    return pl.BlockSpec((height, tm), lambda i: (0, i))


def _const_spec(shape):
    nd = len(shape)
    return pl.BlockSpec(shape, lambda *_: (0,) * nd)


def _weight_dims(w_bf, w_t):
    return (((1,), (1,)), ((), ())) if w_t else (((1,), (0,)), ((), ())), w_bf.shape[0 if w_t else 1]


def ln_in_proj(x, g, w_bf, splits, name, after=None, w_t=False, bf16_pieces=()):
    dims, n = _weight_dims(w_bf, w_t)
    dtypes = [BF16 if i in bf16_pieces else F32 for i in range(len(splits))]
    tm = 512 if n <= ODD_IN else 256
    spans = []
    o = 0
    for s in splits:
        spans.append((o, o + s))
        o += s
    assert o == n
    extra_specs, extra = _after_operand(after)

    def body(x_ref, g_ref, w_ref, *rest):
        xn_ref, outs = rest[len(extra)], rest[len(extra) + 1:]
        xv = x_ref[...]
        rstd = lax.rsqrt(jnp.mean(xv * xv, axis=-1, keepdims=True) + RMS_EPS)
        xn = (xv * rstd * g_ref[...]).astype(BF16)
        xn_ref[...] = xn.T
        p = lax.dot_general(xn, w_ref[...], dims, preferred_element_type=F32)
        for o_ref, (a, b) in zip(outs, spans):
            o_ref[...] = p[:, a:b].astype(o_ref.dtype)

    return pl.pallas_call(
        body, grid=(T // tm,), name=name,
        in_specs=[_row_spec(tm, D), _const_spec((1, D)), _const_spec(w_bf.shape)] + extra_specs,
        out_specs=[_col_spec(D, tm)] + [_row_spec(tm, s) for s in splits],
        out_shape=[jax.ShapeDtypeStruct((D, T), BF16)]
        + [jax.ShapeDtypeStruct((T, s), dt) for s, dt in zip(splits, dtypes)],
        compiler_params=_cparams(("parallel",), VMEM_BIG),
    )(x, g, w_bf, *extra)


def in_proj_bwd_x(x, g, w_bf, dps, dres, name, after=None, w_t=False):
    tm = 512
    back = (((1,), (0,)), ((), ())) if w_t else (((1,), (1,)), ((), ()))
    widths = [d.shape[1] for d in dps]
    extra_specs, extra = _after_operand(after)

    def body(x_ref, g_ref, w_ref, dres_ref, *rest):
        dp_refs = rest[:len(widths)]
        dx_ref, dg_ref = rest[-2:]
        dp = jnp.concatenate([r[...] for r in dp_refs], axis=-1)
        dxn = lax.dot_general(dp, w_ref[...], back, preferred_element_type=F32)
        xv = x_ref[...]
        rstd = lax.rsqrt(jnp.mean(xv * xv, axis=-1, keepdims=True) + RMS_EPS)
        xhat = xv * rstd
        dgp = jnp.sum(dxn * xhat, axis=0, keepdims=True)

        @pl.when(pl.program_id(0) == 0)
        def _():
            dg_ref[...] = jnp.zeros_like(dg_ref)

        dg_ref[...] += dgp
        dxh = dxn * g_ref[...]
        dx_ref[...] = dres_ref[...] + rstd * (dxh - xhat * jnp.mean(dxh * xhat, axis=-1, keepdims=True))

    return pl.pallas_call(
        body, grid=(T // tm,), name=name,
        in_specs=[_row_spec(tm, D), _const_spec((1, D)), _const_spec(w_bf.shape), _row_spec(tm, D)]
        + [_row_spec(tm, s) for s in widths] + extra_specs,
        out_specs=[_row_spec(tm, D), _const_spec((1, D))],
        out_shape=[jax.ShapeDtypeStruct((T, D), F32), jax.ShapeDtypeStruct((1, D), F32)],
        compiler_params=_cparams(("arbitrary",), VMEM_BIG),
    )(x, g, w_bf, dres, *dps, *extra)


def _after_operand(after):
    return ([ANY], [after]) if after is not None else ([], [])


def matmul_acc_chips(at_bf, pieces, name, after=None, add_cores=False):
    k = at_bf.shape[0]
    widths = [p.shape[1] for p in pieces]
    nb = sum(widths) // NCHIP
    tm = 512
    steps = T // tm
    half = k // 2
    extra_specs, extra = _after_operand(after)

    def body(a_ref, *rest):
        o_ref, acc = rest[len(widths) + len(extra):][:2]

        @pl.when(pl.program_id(0) == 0)
        def _():
            acc[...] = jnp.zeros_like(acc)

        a = a_ref[...]
        b = jnp.concatenate([r[...] for r in rest[:len(widths)]], axis=-1)
        for s in range(NCHIP):
            acc[s] += jnp.dot(a, b[:, s * nb:(s + 1) * nb], preferred_element_type=F32)

        @pl.when(pl.program_id(0) == steps - 1)
        def _():
            if not add_cores:
                o_ref[...] = acc[...].astype(BF16)
            else:
                give, got, send, recv = rest[-4:]
                x, y, c = lax.axis_index("x"), lax.axis_index("y"), lax.axis_index("c")
                theirs = pl.multiple_of((1 - c) * half, half)
                mine = pl.multiple_of(c * half, half)
                give[...] = acc[:, pl.ds(theirs, half), :].astype(BF16)
                cp = pltpu.make_async_remote_copy(src_ref=give, dst_ref=got, send_sem=send, recv_sem=recv,
                                                  device_id=(x, y, 1 - c), device_id_type=MESH)
                cp.start()
                cp.wait()
                o_ref[...] = (acc[:, pl.ds(mine, half), :] + got[...].astype(F32)).astype(BF16)

    out_rows = half if add_cores else k
    exchange = [pltpu.VMEM((NCHIP, half, nb), BF16)] * 2 + [pltpu.SemaphoreType.DMA] * 2 if add_cores else []
    return pl.pallas_call(
        body, grid=(steps,), name=name,
        in_specs=[_col_spec(k, tm)] + [_row_spec(tm, w_) for w_ in widths] + extra_specs,
        out_specs=_const_spec((NCHIP, out_rows, nb)),
        out_shape=jax.ShapeDtypeStruct((NCHIP, out_rows, nb), BF16),
        scratch_shapes=[pltpu.VMEM((NCHIP, k, nb), F32)] + exchange,
        compiler_params=_cparams(("arbitrary",), VMEM_BIG),
    )(at_bf, *pieces, *extra)


def _out_proj_back(dh_ref, zt_ref, w_ref, dw_ref):
    dhb = dh_ref[...].astype(BF16)

    @pl.when(pl.program_id(0) == 0)
    def _():
        dw_ref[...] = jnp.zeros_like(dw_ref)

    dw_ref[...] += jnp.dot(zt_ref[...], dhb, preferred_element_type=F32)
    return lax.dot_general(dhb, w_ref[...], (((1,), (1,)), ((), ())), preferred_element_type=F32)


PREP_TM = 512
PREP_NB = SEQ // PREP_TM


def _prep_elem(k, wl, apre, kkw, kaw, bd):
    wraw = -_softplus(-wl) - 0.5
    lw = -jnp.exp(wraw)
    asig = jax.nn.sigmoid(apre)
    kkr = k * kkw
    nrm = jnp.maximum(jnp.sqrt(_headsum(kkr * kkr, bd)), 1e-12)
    kk = kkr / nrm
    k2 = k * (1.0 + (asig - 1.0) * kaw)
    return lw, k2, -kk, kk * asig


def _prep_elem_bwd(k, wl, apre, kkw, kaw, bd, dlw, dk2, daa, dbb):
    s = -wl
    sp = _softplus(s)
    dwl = dlw * (-jnp.exp(-sp - 0.5)) * jnp.exp(s - sp)
    asig = jax.nn.sigmoid(apre)
    kkr = k * kkw
    root = jnp.sqrt(_headsum(kkr * kkr, bd))
    inv = 1.0 / jnp.maximum(root, 1e-12)
    kk = kkr * inv
    dkk = dbb * asig - daa
    dap = (dbb * kk + dk2 * k * kaw) * asig * (1.0 - asig)
    through_norm = jnp.where(root > 1e-12, kk * _headsum(dkk * kkr, bd) * inv, 0.0)
    dkkr = inv * (dkk - through_norm)
    gain = 1.0 + (asig - 1.0) * kaw
    dk = dkkr * kkw + dk2 * gain
    dkkw = jnp.sum(dkkr * k, axis=0, keepdims=True)
    dkaw = jnp.sum(dk2 * k * (asig - 1.0), axis=0, keepdims=True)
    return dk, dwl, dap, dkkw, dkaw


def _shifted(ps_ref, prev_ref, mu, blk):
    p = ps_ref[...]
    first = (blk % PREP_NB) == 0
    prev_row = jnp.where(first, 0.0, prev_ref[7:8, :])
    rolled = pltpu.roll(p, 1, 0)
    p_prev = jnp.where(_iota2(p.shape, 0) == 0, prev_row, rolled)
    return p, p_prev, p + (p_prev - p) * mu


def _prev_spec(width, blk_of):
    return pl.BlockSpec((8, width), lambda i: (jnp.maximum(blk_of(i) * (PREP_TM // 8) - 1, 0), 0))


def even_prep(ps, mu, w0, w2x, a0, a2x, kkw, kaw):
    tm = PREP_TM

    def body(ps_ref, prev_ref, mu_ref, w0_ref, w2_ref, a0_ref, a2_ref, kk_ref, ka_ref,
             r_ref, lw_ref, k2_ref, v_ref, aa_ref, bb_ref):
        _, _, s = _shifted(ps_ref, prev_ref, mu_ref[...], pl.program_id(0))
        wa = s[:, 3 * W:]
        wl = w0_ref[...] + _bdot(jnp.tanh(wa), w2_ref[...])
        apre = a0_ref[...] + _bdot(wa, a2_ref[...])
        lw, k2, aa, bb = _prep_elem(s[:, W:2 * W], wl, apre, kk_ref[...], ka_ref[...], _head_blockdiag())
        r_ref[...] = s[:, 0:W]
        v_ref[...] = s[:, 2 * W:3 * W]
        lw_ref[...] = lw
        k2_ref[...] = k2
        aa_ref[...] = aa
        bb_ref[...] = bb

    vec = _const_spec((1, W))
    return pl.pallas_call(
        body, grid=(T // tm,), name="even_prep",
        in_specs=[_row_spec(tm, SHIFT), _prev_spec(SHIFT, lambda i: i), _const_spec((1, SHIFT)), vec,
                  _const_spec((2 * LORA, W)), vec, _const_spec((2 * LORA, W)), vec, vec],
        out_specs=[_row_spec(tm, W)] * 6,
        out_shape=[jax.ShapeDtypeStruct((T, W), F32)] * 6,
        compiler_params=_cparams(("parallel",), VMEM_BIG),
    )(ps, ps, mu, w0, w2x, a0, a2x, kkw, kaw)


def even_prep_bwd(ps, mu, w0, w2x, a0, a2x, kkw, kaw, dr, dlw, dk2, dv, daa, dbb, dr2, dk22, dv2):
    tm = PREP_TM
    nb = T // tm
    rev = lambda i: nb - 1 - i

    def body(ps_ref, prev_ref, mu_ref, w0_ref, w2_ref, a0_ref, a2_ref, kk_ref, ka_ref,
             dr_ref, dlw_ref, dk2_ref, dv_ref, daa_ref, dbb_ref, dr2_ref, dk22_ref, dv2_ref,
             dps_ref, dmu_ref, dw0_ref, dw2_ref, da0_ref, da2_ref, dkk_ref, dka_ref, carry):
        i = pl.program_id(0)
        blk = rev(i)
        mu_v = mu_ref[...]
        p, p_prev, s = _shifted(ps_ref, prev_ref, mu_v, blk)
        wa = s[:, 3 * W:]
        th = jnp.tanh(wa)
        wl = w0_ref[...] + _bdot(th, w2_ref[...])
        apre = a0_ref[...] + _bdot(wa, a2_ref[...])
        bd = _head_blockdiag()
        k = s[:, W:2 * W]
        dk, dwl, dap, dkkw, dkaw = _prep_elem_bwd(k, wl, apre, kk_ref[...], ka_ref[...], bd, dlw_ref[...],
                                                  dk2_ref[...] + dk22_ref[...], daa_ref[...], dbb_ref[...])
        dwa = _bdot_nt(dwl, w2_ref[...]) * (1.0 - th * th) + _bdot_nt(dap, a2_ref[...])
        ds = jnp.concatenate([dr_ref[...] + dr2_ref[...], dk, dv_ref[...] + dv2_ref[...], dwa], axis=-1)

        @pl.when(i == 0)
        def _():
            for ref in (dmu_ref, dw0_ref, dw2_ref, da0_ref, da2_ref, dkk_ref, dka_ref, carry):
                ref[...] = jnp.zeros_like(ref)

        dmu_ref[...] += jnp.sum(ds * (p_prev - p), axis=0, keepdims=True)
        dw0_ref[...] += jnp.sum(dwl, axis=0, keepdims=True)
        da0_ref[...] += jnp.sum(dap, axis=0, keepdims=True)
        dw2_ref[...] += _bdot_tn(th, dwl)
        da2_ref[...] += _bdot_tn(wa, dap)
        dkk_ref[...] += dkkw
        dka_ref[...] += dkaw
        dsm = ds * mu_v
        last = (blk % PREP_NB) == PREP_NB - 1
        nxt = jnp.where(last, 0.0, carry[0:1, :])
        up = pltpu.roll(dsm, tm - 1, 0)
        up = jnp.where(_iota2(up.shape, 0) == tm - 1, nxt, up)
        dps_ref[...] = (ds - dsm + up).astype(BF16)
        carry[0:1, :] = dsm[0:1, :]

    vec = _const_spec((1, W))
    rrow = lambda width: pl.BlockSpec((tm, width), lambda i: (rev(i), 0))
    return pl.pallas_call(
        body, grid=(nb,), name="even_prep_bwd",
        in_specs=[rrow(SHIFT), _prev_spec(SHIFT, rev), _const_spec((1, SHIFT)), vec,
                  _const_spec((2 * LORA, W)), vec, _const_spec((2 * LORA, W)), vec, vec] + [rrow(W)] * 9,
        out_specs=[rrow(SHIFT), _const_spec((1, SHIFT)), vec, _const_spec((2 * LORA, W)), vec,
                   _const_spec((2 * LORA, W)), vec, vec],
        out_shape=[jax.ShapeDtypeStruct((T, SHIFT), BF16), jax.ShapeDtypeStruct((1, SHIFT), F32),
                   jax.ShapeDtypeStruct((1, W), F32), jax.ShapeDtypeStruct((2 * LORA, W), F32),
                   jax.ShapeDtypeStruct((1, W), F32), jax.ShapeDtypeStruct((2 * LORA, W), F32),
                   jax.ShapeDtypeStruct((1, W), F32), jax.ShapeDtypeStruct((1, W), F32)],
        scratch_shapes=[pltpu.VMEM((8, SHIFT), F32)],
        compiler_params=_cparams(("arbitrary",), VMEM_BIG),
    )(ps, ps, mu, w0, w2x, a0, a2x, kkw, kaw, dr, dlw, dk2, dv, daa, dbb, dr2, dk22, dv2)


NPAIR = NH // 2
PW = 2 * HD


def _pair_cols(p):
    return slice(p * PW, (p + 1) * PW)


def _pairs(a):
    return [a[:, _pair_cols(p)] for p in range(NPAIR)]


def _stack_pair(a):
    first = _iota2(a.shape, 1) < HD
    zero = jnp.zeros_like(a)
    return jnp.concatenate([jnp.where(first, a, zero), jnp.where(first, zero, a)], axis=0)


def _unstack_pair(a):
    n = a.shape[0] // 2
    return jnp.where(_iota2((n, PW), 1) < HD, a[:n], a[n:])


def _fold_pair(a):
    n = a.shape[0] // 2
    return a[:n] + a[n:]


def _chunk_masks():
    n = 4 * L
    row = _iota2((n, n), 0)
    col = _iota2((n, n), 1)
    same = ((row // L) & 1) == ((col // L) & 1)
    ri = row & (L - 1)
    ci = col & (L - 1)
    keep = same & (((row < 2 * L) & (ri > ci)) | ((row >= 2 * L) & (ri >= ci)))
    r1 = _iota2((L, L), 0)
    c1 = _iota2((L, L), 1)
    r2 = _iota2((2 * L, 2 * L), 0)
    c2 = _iota2((2 * L, 2 * L), 1)
    return keep.astype(F32), (r1 >= c1).astype(F32), (r2 == c2).astype(F32)


def _scaled(r, lw, k2, aa, bb, tri):
    g = _hdot(tri, lw)
    eg = jnp.exp(g)
    eng = jnp.exp(-g)
    egp = jnp.exp(g - lw)
    return eg, eng, egp, aa * egp, r * eg, bb * eng, k2 * eng


def _head_cols(h):
    return slice(h * HD, (h + 1) * HD)


def _per_head(a):
    return [a[:, _head_cols(h)] for h in range(NH)]


def _pairs_operands(at, rt, bt, kt):
    x = [jnp.concatenate([_stack_pair(a), _stack_pair(r)], axis=0).astype(BF16) for a, r in zip(_pairs(at), _pairs(rt))]
    yk = [jnp.concatenate([_stack_pair(b), _stack_pair(k)], axis=0).astype(BF16) for b, k in zip(_pairs(bt), _pairs(kt))]
    return x, yk


def _pairs_matrices(x, yk, keep, eye):
    m = [_bdot_nt(a, b) * keep for a, b in zip(x, yk)]
    p = [a[:2 * L, :2 * L] for a in m]
    tinv = [eye + a for a in p]
    for _ in range(5):
        p = [_bdot(a, a) for a in p]
        tinv = [t + _bdot(t, a) for t, a in zip(tinv, p)]
    return [a.astype(BF16) for a in m], [a.astype(BF16) for a in tinv]


def _pairs_fwd(x, yk, m, tinv, vw, s0, egl):
    xh = [_bdot_nt(a, s) for a, s in zip(x, s0)]
    u = [_bdot(t, h[:2 * L] + _bdot(a[:2 * L, 2 * L:], w)) for t, h, a, w in zip(tinv, xh, m, vw)]
    uv = [jnp.concatenate([a, w], axis=0).astype(BF16) for a, w in zip(u, vw)]
    y = [h[2 * L:] + _bdot(a[2 * L:], w) for h, a, w in zip(xh, m, uv)]
    sn = [e * (s + _bdot_tn(w, b)) for e, s, w, b in zip(egl, s0, uv, yk)]
    return y, sn, uv


def _pairs_bwd(x, yk, m, tinv, uv, s0, sn, egl, dyw, dsn, keep):
    dzs = [d * e for d, e in zip(dsn, egl)]
    dgl = [jnp.sum(d * s, axis=0, keepdims=True) for d, s in zip(dsn, sn)]
    dyb = [a.astype(BF16) for a in dyw]
    t1 = [_bdot_tn(a[2 * L:], d) for a, d in zip(m, dyb)]
    t2 = [_bdot_nt(b, d) for b, d in zip(yk, dzs)]
    drhs = [_bdot_tn(t, a[:2 * L] + b[:2 * L]) for t, a, b in zip(tinv, t1, t2)]
    dv = [a[2 * L:] + b[2 * L:] + _bdot_tn(c[:2 * L, 2 * L:], d) for a, b, c, d in zip(t1, t2, m, drhs)]
    gg = [jnp.concatenate([a, b], axis=0).astype(BF16) for a, b in zip(drhs, dyw)]
    ds0 = [d + _bdot_tn(g, a) for d, g, a in zip(dzs, gg, x)]
    dm = [_bdot_nt(g, w) * keep for g, w in zip(gg, uv)]
    dx = [_bdot(g, s) + _bdot(d, b) for g, s, d, b in zip(gg, s0, dm, yk)]
    dyk = [_bdot_tn(d, a) + _bdot(w, z) for d, a, w, z in zip(dm, x, uv, dzs)]
    return dx, dyk, dv, dgl, ds0


STATE_SHAPE = (NPAIR * PW, PW)
M_SHAPE = (4 * L, NPAIR * 4 * L)
TINV_SHAPE = (2 * L, NPAIR * 2 * L)


def _rows_of(a, n):
    return [a[i * n:(i + 1) * n, :] for i in range(NPAIR)]


def _both(f):
    out = []
    for s in range(NSEQ):
        out += f(s)
    return out


def _seq_view(a):
    return a.reshape(NSEQ, SEQ, a.shape[-1])


UV_SHAPE = (4 * L, NPAIR * PW)
RW_CHUNKS = 2


def rwkv_fwd(r, lw, k2, v, aa, bb):
    def body(r_ref, lw_ref, k2_ref, v_ref, aa_ref, bb_ref, y_ref, hs_ref, hn_ref, m_ref, t_ref, uv_ref, state):
        @pl.when(pl.program_id(0) == 0)
        def _():
            state[...] = jnp.zeros_like(state)

        keep, tri, eye = _chunk_masks()
        where = [(j, s) for j in range(RW_CHUNKS) for s in range(NSEQ)]
        rows = lambda j: slice(j * L, (j + 1) * L)
        sc = [_scaled(r_ref[s, rows(j)], lw_ref[s, rows(j)], k2_ref[s, rows(j)], aa_ref[s, rows(j)],
                      bb_ref[s, rows(j)], tri) for j, s in where]
        ops = [_pairs_operands(*a[3:]) for a in sc]
        m, tinv = _pairs_matrices([a for o in ops for a in o[0]], [a for o in ops for a in o[1]], keep, eye)
        s_cur = [state[s] for s in range(NSEQ)]
        for j in range(RW_CHUNKS):
            mine = slice(j * NSEQ * NPAIR, (j + 1) * NSEQ * NPAIR)
            x = [a for o in ops[j * NSEQ:(j + 1) * NSEQ] for a in o[0]]
            yk = [a for o in ops[j * NSEQ:(j + 1) * NSEQ] for a in o[1]]
            vw = _both(lambda s: [_stack_pair(a) for a in _pairs(v_ref[s, rows(j)])])
            egl = _both(lambda s: _pairs(sc[j * NSEQ + s][0][L - 1:L, :]))
            y, sn, uv = _pairs_fwd(x, yk, m[mine], tinv[mine], vw, _both(lambda s: _rows_of(s_cur[s], PW)), egl)
            for s in range(NSEQ):
                ps = slice(s * NPAIR, (s + 1) * NPAIR)
                hs_ref[j, s] = s_cur[s]
                y_ref[s, rows(j)] = jnp.concatenate([_fold_pair(a) for a in y[ps]], axis=-1)
                m_ref[j, s] = jnp.concatenate(m[mine][ps], axis=-1)
                t_ref[j, s] = jnp.concatenate(tinv[mine][ps], axis=-1)
                uv_ref[j, s] = jnp.concatenate(uv[ps], axis=-1)
                s_cur[s] = jnp.concatenate(sn[ps], axis=0)
                hn_ref[j, s] = s_cur[s]
        for s in range(NSEQ):
            state[s] = s_cur[s]

    blk = pl.BlockSpec((NSEQ, RW_CHUNKS * L, W), lambda c: (0, c, 0))
    per_chunk = lambda shape: pl.BlockSpec((RW_CHUNKS, NSEQ) + shape, lambda c: (c, 0, 0, 0))
    saved_shapes = [(STATE_SHAPE, F32), (STATE_SHAPE, F32), (M_SHAPE, BF16), (TINV_SHAPE, BF16), (UV_SHAPE, BF16)]
    y, *saved = pl.pallas_call(
        body, grid=(NC // RW_CHUNKS,), name="rwkv_fwd",
        in_specs=[blk] * 6,
        out_specs=[blk] + [per_chunk(shape) for shape, _ in saved_shapes],
        out_shape=[jax.ShapeDtypeStruct((NSEQ, SEQ, W), F32)]
        + [jax.ShapeDtypeStruct((NC, NSEQ) + shape, dt) for shape, dt in saved_shapes],
        scratch_shapes=[pltpu.VMEM((NSEQ,) + STATE_SHAPE, F32)],
        compiler_params=_cparams(("arbitrary",), VMEM_BIG),
    )(*[_seq_view(a) for a in (r, lw, k2, v, aa, bb)])
    return y.reshape(T, W), saved


def rwkv_bwd(r, lw, k2, aa, bb, saved, dy):
    def body(r_ref, lw_ref, k2_ref, aa_ref, bb_ref, hs_ref, hn_ref, m_ref, t_ref, uv_ref, dy_ref,
             dr_ref, dlw_ref, dk2_ref, dv_ref, daa_ref, dbb_ref, dstate):
        @pl.when(pl.program_id(0) == 0)
        def _():
            dstate[...] = jnp.zeros_like(dstate)

        keep, tri, _ = _chunk_masks()
        sc = [_scaled(r_ref[s], lw_ref[s], k2_ref[s], aa_ref[s], bb_ref[s], tri) for s in range(NSEQ)]
        ops = [_pairs_operands(*sc[s][3:]) for s in range(NSEQ)]
        x, yk = _both(lambda s: ops[s][0]), _both(lambda s: ops[s][1])
        m = _both(lambda s: [m_ref[0, s][:, i * 4 * L:(i + 1) * 4 * L] for i in range(NPAIR)])
        tinv = _both(lambda s: [t_ref[0, s][:, i * 2 * L:(i + 1) * 2 * L] for i in range(NPAIR)])
        uv = _both(lambda s: _pairs(uv_ref[0, s]))
        dyw = _both(lambda s: [_stack_pair(a) for a in _pairs(dy_ref[s])])
        s0 = _both(lambda s: _rows_of(hs_ref[0, s], PW))
        sn = _both(lambda s: _rows_of(hn_ref[0, s], PW))
        dsn = _both(lambda s: _rows_of(dstate[s], PW))
        egl = _both(lambda s: _pairs(sc[s][0][L - 1:L, :]))
        dx, dyk, dvw, dgl, ds0 = _pairs_bwd(x, yk, m, tinv, uv, s0, sn, egl, dyw, dsn, keep)
        for s in range(NSEQ):
            mine = slice(s * NPAIR, (s + 1) * NPAIR)
            eg, eng, egp, at, rt, bt, kt = sc[s]
            dstate[s] = jnp.concatenate(ds0[mine], axis=0)
            dv_ref[s] = jnp.concatenate([_fold_pair(a) for a in dvw[mine]], axis=-1)
            dat = jnp.concatenate([_fold_pair(a[:2 * L]) for a in dx[mine]], axis=-1)
            drt = jnp.concatenate([_fold_pair(a[2 * L:]) for a in dx[mine]], axis=-1)
            dbt = jnp.concatenate([_fold_pair(a[:2 * L]) for a in dyk[mine]], axis=-1)
            dkt = jnp.concatenate([_fold_pair(a[2 * L:]) for a in dyk[mine]], axis=-1)
            dg = drt * rt - dbt * bt - dkt * kt
            dg = dg + jnp.where(_iota2(dg.shape, 0) == L - 1, jnp.concatenate(dgl[mine], axis=-1), 0.0)
            dgp = dat * at
            dlw_ref[s] = _hdot_tn(tri, dg + dgp) - dgp
            dr_ref[s] = drt * eg
            daa_ref[s] = dat * egp
            dbb_ref[s] = dbt * eng
            dk2_ref[s] = dkt * eng

    blk = pl.BlockSpec((NSEQ, L, W), lambda c: (0, NC - 1 - c, 0))
    per_chunk = lambda shape: pl.BlockSpec((1, NSEQ) + shape, lambda c: (NC - 1 - c, 0, 0, 0))
    outs = pl.pallas_call(
        body, grid=(NC,), name="rwkv_bwd",
        in_specs=[blk] * 5 + [per_chunk(a.shape[2:]) for a in saved] + [blk],
        out_specs=[blk] * 6,
        out_shape=[jax.ShapeDtypeStruct((NSEQ, SEQ, W), F32)] * 6,
        scratch_shapes=[pltpu.VMEM((NSEQ,) + STATE_SHAPE, F32)],
        compiler_params=_cparams(("arbitrary",)),
    )(*[_seq_view(a) for a in (r, lw, k2, aa, bb)], *saved, _seq_view(dy))
    return [a.reshape(T, W) for a in outs]


def _post_math(y, r, k2, v, ga, o, gb, lng, lnb, rk, bd):
    mu = _headsum(y, bd) * (1.0 / HD)
    yc = y - mu
    var = _headsum(yc * yc, bd) * (1.0 / HD)
    yn = yc * lax.rsqrt(var + GN_EPS) * lng + lnb
    bonus = _headsum(r * k2 * rk, bd) * v
    return (yn + bonus) * _silu(ga), o * _silu(gb)


def even_post(y, r, k2, v, ga, o, gb, lng, lnb, rk, h, w_bf):
    tm = 512

    def body(y_ref, r_ref, k2_ref, v_ref, ga_ref, o_ref, gb_ref, lng_ref, lnb_ref, rk_ref, h_ref, w_ref,
             ho_ref, zt_ref):
        ya, yb = _post_math(y_ref[...], r_ref[...], k2_ref[...], v_ref[...], ga_ref[...], o_ref[...], gb_ref[...],
                            lng_ref[...], lnb_ref[...], rk_ref[...], _head_blockdiag())
        z = jnp.concatenate([ya.astype(BF16), yb.astype(BF16)], axis=-1)
        zt_ref[...] = z.T
        ho_ref[...] = h_ref[...] + jnp.dot(z, w_ref[...], preferred_element_type=F32)

    vec = _const_spec((1, W))
    return pl.pallas_call(
        body, grid=(T // tm,), name="even_post",
        in_specs=[_row_spec(tm, W)] * 7 + [vec] * 3 + [_row_spec(tm, D), _const_spec((D, D))],
        out_specs=[_row_spec(tm, D), _col_spec(D, tm)],
        out_shape=[jax.ShapeDtypeStruct((T, D), F32), jax.ShapeDtypeStruct((D, T), BF16)],
        compiler_params=_cparams(("parallel",), VMEM_BIG),
    )(y, r, k2, v, ga, o, gb, lng, lnb, rk, h, w_bf)


def even_post_bwd(y, r, k2, v, ga, o, gb, lng, lnb, rk, dh, zt_bf, w_bf, after=None):
    tm = 512
    extra_specs, extra = _after_operand(after)

    def body(y_ref, r_ref, k2_ref, v_ref, ga_ref, o_ref, gb_ref, lng_ref, lnb_ref, rk_ref, dh_ref, zt_ref, w_ref,
             *rest):
        dy_ref, dr_ref, dk2_ref, dv_ref, dga_ref, do_ref, dgb_ref, dlng_ref, dlnb_ref, drk_ref, dw_ref = rest[-11:]
        dzv = _out_proj_back(dh_ref, zt_ref, w_ref, dw_ref)
        bd = _head_blockdiag()
        _, vjp = jax.vjp(lambda *a: _post_math(*a, bd), y_ref[...], r_ref[...], k2_ref[...], v_ref[...], ga_ref[...],
                         o_ref[...], gb_ref[...], lng_ref[...], lnb_ref[...], rk_ref[...])
        dy, dr, dk2, dv, dga, do, dgb, dlng, dlnb, drk = vjp((dzv[:, 0:W], dzv[:, W:2 * W]))
        for ref, val in ((dy_ref, dy), (dr_ref, dr), (dk2_ref, dk2), (dv_ref, dv), (dga_ref, dga), (do_ref, do),
                         (dgb_ref, dgb)):
            ref[...] = val.astype(ref.dtype)

        @pl.when(pl.program_id(0) == 0)
        def _():
            for ref in (dlng_ref, dlnb_ref, drk_ref):
                ref[...] = jnp.zeros_like(ref)

        dlng_ref[...] += dlng
        dlnb_ref[...] += dlnb
        drk_ref[...] += drk

    vec = _const_spec((1, W))
    return pl.pallas_call(
        body, grid=(T // tm,), name="even_post_bwd",
        in_specs=[_row_spec(tm, W)] * 7 + [vec] * 3 + [_row_spec(tm, D), _col_spec(D, tm), _const_spec((D, D))]
        + extra_specs,
        out_specs=[_row_spec(tm, W)] * 7 + [vec] * 3 + [_const_spec((D, D))],
        out_shape=[jax.ShapeDtypeStruct((T, W), dt) for dt in (F32, F32, F32, F32, BF16, F32, BF16)]
        + [jax.ShapeDtypeStruct((1, W), F32)] * 3 + [jax.ShapeDtypeStruct((D, D), F32)],
        compiler_params=_cparams(("arbitrary",), VMEM_BIG),
    )(y, r, k2, v, ga, o, gb, lng, lnb, rk, dh, zt_bf, w_bf, *extra)


PADSEQ = SEQ + LEFT * L
ATT_SCALE = 1.0 / math.sqrt(HD)
ATT_Q = 4
WIN = BAND + (ATT_Q - 1) * L
ATT_STEPS = NC // ATT_Q
ATT_BIAS_SHAPE = (NPAIR, ATT_Q * 2 * L, WIN)
ATT_WINDOW_BIAS_SHAPE = (ATT_Q, NPAIR, 2 * L, WIN)


def _stack_chunks(a):
    return jnp.concatenate([_stack_pair(a[i * L:(i + 1) * L]) for i in range(ATT_Q)], axis=0)


def _unstack_chunks(a):
    return jnp.concatenate([_unstack_pair(a[i * 2 * L:(i + 1) * 2 * L]) for i in range(ATT_Q)], axis=0)


def _window_bias(b_ref):
    return [jnp.concatenate([b_ref[c, p] for c in range(ATT_Q)], axis=0) for p in range(NPAIR)]


def _att_probs(q2, kw, bias, step):
    valid = _iota2((1, WIN), 1) >= (LEFT - step * ATT_Q) * L
    s = [jnp.where(valid, _bdot_nt(a, b) * ATT_SCALE + bias[p], NEG) for p, (a, b) in enumerate(zip(q2, kw))]
    e = [jnp.exp(a - jnp.max(a, axis=-1, keepdims=True)) for a in s]
    return [a / jnp.sum(a, axis=-1, keepdims=True) for a in e]


def attention_fwd(q, kpad, vpad, bias):
    def body(q_ref, k_ref, v_ref, b_ref, o_ref):
        step = pl.program_id(1)
        start = pl.multiple_of(step * (ATT_Q * L), L)
        kw = _pairs(k_ref[pl.ds(start, WIN), :])
        vw = _pairs(v_ref[pl.ds(start, WIN), :])
        q2 = [_stack_chunks(a) for a in _pairs(q_ref[...])]
        p = _att_probs(q2, kw, _window_bias(b_ref), step)
        o_ref[...] = jnp.concatenate([_unstack_chunks(_bdot(a, b)) for a, b in zip(p, vw)], axis=-1)

    qblk = pl.BlockSpec((ATT_Q * L, W), lambda b, c: (b * ATT_STEPS + c, 0))
    kblk = pl.BlockSpec((PADSEQ, W), lambda b, c: (b, 0))
    return pl.pallas_call(
        body, grid=(NSEQ, ATT_STEPS), name="attention_fwd",
        in_specs=[qblk, kblk, kblk, _const_spec(ATT_WINDOW_BIAS_SHAPE)],
        out_specs=qblk, out_shape=jax.ShapeDtypeStruct((T, W), F32),
        compiler_params=_cparams(("parallel", "arbitrary")),
    )(q, kpad, vpad, bias)


def attention_bwd(q, kpad, vpad, bias, do):
    def body(q_ref, k_ref, v_ref, b_ref, do_ref, dq_ref, dko_ref, dvo_ref, db_ref, dk_ref, dv_ref):
        b = pl.program_id(0)
        c = pl.program_id(1)

        @pl.when(c == 0)
        def _():
            dk_ref[...] = jnp.zeros_like(dk_ref)
            dv_ref[...] = jnp.zeros_like(dv_ref)

        @pl.when((c == 0) & (b == 0))
        def _():
            db_ref[...] = jnp.zeros_like(db_ref)

        start = pl.multiple_of(c * (ATT_Q * L), L)
        kw = _pairs(k_ref[pl.ds(start, WIN), :])
        vw = _pairs(v_ref[pl.ds(start, WIN), :])
        q2 = [_stack_chunks(a) for a in _pairs(q_ref[...])]
        do2 = [_stack_chunks(a) for a in _pairs(do_ref[...].astype(BF16))]
        p = _att_probs(q2, kw, _window_bias(b_ref), c)
        dp = [_bdot_nt(a, b) for a, b in zip(do2, vw)]
        ds = [a * (d - jnp.sum(d * a, axis=-1, keepdims=True)) for a, d in zip(p, dp)]
        dss = [(a * ATT_SCALE).astype(BF16) for a in ds]
        dq_ref[...] = jnp.concatenate([_unstack_chunks(_bdot(a, b)) for a, b in zip(dss, kw)], axis=-1).astype(BF16)
        dk_ref[pl.ds(start, WIN), :] += jnp.concatenate([_bdot_tn(a, b) for a, b in zip(dss, q2)], axis=-1)
        dv_ref[pl.ds(start, WIN), :] += jnp.concatenate([_bdot_tn(a, b) for a, b in zip(p, do2)], axis=-1)
        for i in range(NPAIR):
            db_ref[i] += ds[i]

        @pl.when(c == ATT_STEPS - 1)
        def _():
            dko_ref[...] = dk_ref[LEFT * L:, :].astype(BF16)
            dvo_ref[...] = dv_ref[LEFT * L:, :].astype(BF16)

    qblk = pl.BlockSpec((ATT_Q * L, W), lambda b, c: (b * ATT_STEPS + c, 0))
    kblk = pl.BlockSpec((PADSEQ, W), lambda b, c: (b, 0))
    sblk = pl.BlockSpec((SEQ, W), lambda b, c: (b, 0))
    bblk = _const_spec(ATT_BIAS_SHAPE)
    return pl.pallas_call(
        body, grid=(NSEQ, ATT_STEPS), name="attention_bwd",
        in_specs=[qblk, kblk, kblk, _const_spec(ATT_WINDOW_BIAS_SHAPE), qblk],
        out_specs=[qblk, sblk, sblk, bblk],
        out_shape=[jax.ShapeDtypeStruct((T, W), BF16), jax.ShapeDtypeStruct((T, W), BF16),
                   jax.ShapeDtypeStruct((T, W), BF16), jax.ShapeDtypeStruct(ATT_BIAS_SHAPE, F32)],
        scratch_shapes=[pltpu.VMEM((PADSEQ, W), F32), pltpu.VMEM((PADSEQ, W), F32)],
        compiler_params=_cparams(("arbitrary", "arbitrary"), VMEM_BIG),
    )(q, kpad, vpad, bias, do)


NTAB = 2 * CLIP + 1
EXT = BAND + L


def _ext_onehot():
    n = _iota2((EXT, NTAB), 0)
    m = _iota2((EXT, NTAB), 1)
    return (jnp.clip(BAND - 1 - n, -CLIP, CLIP) + CLIP == m).astype(F32)


def bias_expand(table):
    def body(t_ref, o_ref):
        ext = _hdot_nt(t_ref[...], _ext_onehot())
        ext = jnp.concatenate([ext, jnp.zeros((NH, WIN - EXT), F32)], axis=-1)
        col = _iota2((NH, WIN), 1)
        for c in range(ATT_Q):
            inside = (col >= c * L) & (col < c * L + BAND)
            for i in range(L):
                shift = (c * L - (L - 1 - i)) % WIN
                o_ref[c, :, i, :] = jnp.where(inside, pltpu.roll(ext, shift, 1) if shift else ext, NEG)

    out = pl.pallas_call(body, name="bias_expand", out_shape=jax.ShapeDtypeStruct((ATT_Q, NH, L, WIN), F32))(table)
    return out.reshape(ATT_WINDOW_BIAS_SHAPE)


def bias_grad(dbias):
    def body(d_ref, o_ref):
        acc = jnp.zeros((NH, EXT), F32)
        zpad = jnp.zeros((NH, EXT - BAND), F32)
        for i in range(L):
            s = L - 1 - i
            row = jnp.concatenate([d_ref[:, i, :], zpad], axis=-1)
            acc = acc + (pltpu.roll(row, s, 1) if s else row)
        o_ref[...] = _hdot(acc, _ext_onehot())

    return pl.pallas_call(body, name="bias_grad", out_shape=jax.ShapeDtypeStruct((NH, NTAB), F32))(dbias)


def _group_cols(g):
    return slice(g * SGC, (g + 1) * SGC)


def _sg_norm(gv, lng, lnb):
    gc = gv - jnp.mean(gv, axis=-1, keepdims=True)
    rstd = lax.rsqrt(jnp.mean(gc * gc, axis=-1, keepdims=True) + LN_EPS)
    xhat = gc * rstd
    return xhat, rstd, xhat * lng + lnb


GMLP_BWD_CHUNKS = 2


def gmlp_fwd_loss(u, v, gate, lng, lnb, wm_bf, sgb_t, h, w_bf, g_final, target):
    tm = GMLP_BWD_CHUNKS * SGC

    def body(u_ref, v_ref, gt_ref, lng_ref, lnb_ref, wm_ref, sb_ref, h_ref, w_ref, g_ref, t_ref,
             dh_ref, loss_ref, dg_ref, zt_ref):
        zs = []
        for ch in range(GMLP_BWD_CHUNKS):
            rows = slice(ch * SGC, (ch + 1) * SGC)
            _, _, vln = _sg_norm(_gelu(v_ref[rows, :]), lng_ref[...], lnb_ref[...])
            vlb = vln.astype(BF16)
            zg = []
            for g in range(NG):
                cs = _group_cols(g)
                sv = jnp.dot(wm_ref[g], vlb[:, cs], preferred_element_type=F32) + sb_ref[:, g:g + 1]
                zg.append((_gelu(u_ref[rows, cs]) * sv * _silu(gt_ref[rows, cs])).astype(BF16))
            zs.append(jnp.concatenate(zg, axis=-1))
        z = jnp.concatenate(zs, axis=0)
        zt_ref[...] = z.T
        xv = h_ref[...] + jnp.dot(z, w_ref[...], preferred_element_type=F32)
        rstd = lax.rsqrt(jnp.mean(xv * xv, axis=-1, keepdims=True) + RMS_EPS)
        xhat = xv * rstd
        err = xhat * g_ref[...] - t_ref[...]
        part = 0.5 * jnp.sum(jnp.mean(err * err, axis=-1, keepdims=True), axis=0, keepdims=True)
        dout = err * (1.0 / D)

        @pl.when(pl.program_id(0) == 0)
        def _():
            loss_ref[...] = jnp.zeros_like(loss_ref)
            dg_ref[...] = jnp.zeros_like(dg_ref)

        loss_ref[...] += jnp.broadcast_to(part, loss_ref.shape)
        dg_ref[...] += jnp.sum(dout * xhat, axis=0, keepdims=True)
        dxh = dout * g_ref[...]
        dh_ref[...] = rstd * (dxh - xhat * jnp.mean(dxh * xhat, axis=-1, keepdims=True))

    return pl.pallas_call(
        body, grid=(T // tm,), name="gmlp_fwd_loss",
        in_specs=[_row_spec(tm, D)] * 3 + [_const_spec((1, D))] * 2
        + [_const_spec((NG, SGC, SGC)), _const_spec((SGC, NG)), _row_spec(tm, D), _const_spec((D, D)),
           _const_spec((1, D)), _row_spec(tm, D)],
        out_specs=[_row_spec(tm, D), _const_spec((8, 128)), _const_spec((1, D)), _col_spec(D, tm)],
        out_shape=[jax.ShapeDtypeStruct((T, D), F32), jax.ShapeDtypeStruct((8, 128), F32),
                   jax.ShapeDtypeStruct((1, D), F32), jax.ShapeDtypeStruct((D, T), BF16)],
        compiler_params=_cparams(("arbitrary",), VMEM_BIG),
    )(u, v, gate, lng, lnb, wm_bf, sgb_t, h, w_bf, g_final, target)


def gmlp_bwd(u, v, gate, lng, lnb, wm_bf, sgb_t, dh, zt_bf, w_bf):
    def body(u_ref, v_ref, gt_ref, lng_ref, lnb_ref, wm_ref, sb_ref, dh_ref, zt_ref, w_ref,
             du_ref, dv_ref, dgt_ref, dlng_ref, dlnb_ref, dwm_ref, dsb_ref, dw_ref):
        @pl.when(pl.program_id(0) == 0)
        def _():
            for ref in (dlng_ref, dlnb_ref, dwm_ref, dsb_ref):
                ref[...] = jnp.zeros_like(ref)

        dz = _out_proj_back(dh_ref, zt_ref, w_ref, dw_ref)
        sel = (_iota2((D, NG), 0) // SGC == _iota2((D, NG), 1)).astype(F32)
        for ch in range(GMLP_BWD_CHUNKS):
            rows = slice(ch * SGC, (ch + 1) * SGC)
            gv, dgv_dv = _gelu_both(v_ref[rows, :])
            xhat, rstd, vln = _sg_norm(gv, lng_ref[...], lnb_ref[...])
            vlb = vln.astype(BF16)
            dvln = []
            dsv_all = []
            for g in range(NG):
                cs = _group_cols(g)
                uu = u_ref[rows, cs]
                gg = gt_ref[rows, cs]
                dzz = dz[rows, cs]
                sv = jnp.dot(wm_ref[g], vlb[:, cs], preferred_element_type=F32) + sb_ref[:, g:g + 1]
                gu, dgu = _gelu_both(uu)
                sg, dsg = _silu_both(gg)
                dzgu = dzz * gu
                dsv = dzgu * sg
                dgt_ref[rows, cs] = (dzgu * sv * dsg).astype(BF16)
                du_ref[rows, cs] = (dzz * sv * sg * dgu).astype(BF16)
                dsb16 = dsv.astype(BF16)
                dvln.append(lax.dot_general(wm_ref[g], dsb16, (((0,), (0,)), ((), ())), preferred_element_type=F32))
                dwm_ref[g] += lax.dot_general(dsb16, vlb[:, cs], (((1,), (1,)), ((), ())),
                                              preferred_element_type=F32)
                dsv_all.append(dsv)
            dvl = jnp.concatenate(dvln, axis=-1)
            dsb_ref[...] += _hdot(jnp.concatenate(dsv_all, axis=-1), sel)
            dlng_ref[...] += jnp.sum(dvl * xhat, axis=0, keepdims=True)
            dlnb_ref[...] += jnp.sum(dvl, axis=0, keepdims=True)
            dxh = dvl * lng_ref[...]
            dgv = rstd * (dxh - jnp.mean(dxh, axis=-1, keepdims=True)
                          - xhat * jnp.mean(dxh * xhat, axis=-1, keepdims=True))
            dv_ref[rows, :] = (dgv * dgv_dv).astype(BF16)

    tm = GMLP_BWD_CHUNKS * SGC
    return pl.pallas_call(
        body, grid=(T // tm,), name="gmlp_bwd",
        in_specs=[_row_spec(tm, D)] * 3 + [_const_spec((1, D))] * 2
        + [_const_spec((NG, SGC, SGC)), _const_spec((SGC, NG)), _row_spec(tm, D), _col_spec(D, tm),
           _const_spec((D, D))],
        out_specs=[_row_spec(tm, D)] * 3 + [_const_spec((1, D))] * 2
        + [_const_spec((NG, SGC, SGC)), _const_spec((SGC, NG)), _const_spec((D, D))],
        out_shape=[jax.ShapeDtypeStruct((T, D), BF16)] * 3 + [jax.ShapeDtypeStruct((1, D), F32)] * 2
        + [jax.ShapeDtypeStruct((NG, SGC, SGC), F32), jax.ShapeDtypeStruct((SGC, NG), F32),
           jax.ShapeDtypeStruct((D, D), F32)],
        compiler_params=_cparams(("arbitrary",), VMEM_BIG),
    )(u, v, gate, lng, lnb, wm_bf, sgb_t, dh, zt_bf, w_bf)


NCHIP = 4
NDEV = 8
ANY = pl.BlockSpec(memory_space=pl.ANY)


HBM = pl.BlockSpec(memory_space=pltpu.HBM)
SEM = pl.BlockSpec(memory_space=pltpu.SEMAPHORE)
EFFECT = pltpu.SideEffectType.DATAFLOW_SIDE_EFFECTING


CHIPS, EVERY, SIBLING = "chips", "every", "sibling"
SLOTS = {CHIPS: NCHIP, EVERY: NDEV, SIBLING: 1}


def _peers(scope):
    x, y, c = lax.axis_index("x"), lax.axis_index("y"), lax.axis_index("c")
    if scope == SIBLING:
        return [((x, y, 1 - c), 0)], 0
    if scope == CHIPS:
        return [((px, py, c), 2 * px + py) for px, py in ((1 - x, y), (x, 1 - y), (1 - x, 1 - y))], 2 * x + y
    out = []
    for j in range(1, NDEV):
        px, py, pc = x ^ (j >> 2), y ^ ((j >> 1) & 1), c ^ (j & 1)
        out.append(((px, py, pc), 4 * px + 2 * py + pc))
    return out, 4 * x + 2 * y + c


def _send_copies(src, land, send, recv, scatter, scope, starting):
    peers, me = _peers(scope)
    copies = []
    for t in range(len(src)):
        for j, (dev, slot) in enumerate(peers):
            k = t * len(peers) + j
            copies.append(pltpu.make_async_remote_copy(
                src_ref=src[t].at[slot] if scatter else src[t], dst_ref=land[t].at[me if starting else slot],
                send_sem=send.at[k], recv_sem=recv.at[k], device_id=dev, device_id_type=MESH))
    return copies


def _own_copies(src, land, sems, scatter, scope):
    if scope == SIBLING:
        return []
    _, me = _peers(scope)
    return [pltpu.make_async_copy(src[t].at[me] if scatter else src[t], land[t].at[me], sems.at[t])
            for t in range(len(src))]


def send_start(srcs, scatter, scope, name, after=None):
    n = len(srcs)
    slots = SLOTS[scope]
    extra_specs, extra = _after_operand(after)
    lands = [pltpu.HBM(a.shape if scatter else (slots,) + a.shape, a.dtype) for a in srcs]
    sems = [pltpu.SemaphoreType.DMA((n * max(slots - 1, 1),))] * 2 + ([] if scope == SIBLING else
                                                                     [pltpu.SemaphoreType.DMA((n,))])
    k = len(sems)

    def body(*refs):
        first_out = n + len(extra)
        src, land = refs[:n], refs[first_out + k + n:first_out + k + 2 * n]
        for cp in _send_copies(src, land, refs[first_out], refs[first_out + 1], scatter, scope, True):
            cp.start()
        for cp in _own_copies(src, land, refs[first_out + k - 1], scatter, scope):
            cp.start()
        refs[-1][...] = jnp.zeros_like(refs[-1])

    out = pl.pallas_call(
        body, name=name,
        out_shape=(*sems, *[pltpu.HBM(a.shape, a.dtype) for a in srcs], *lands, jax.ShapeDtypeStruct((8, 128), F32)),
        in_specs=[HBM] * n + extra_specs,
        out_specs=(*[SEM] * k, *[HBM] * (2 * n), pl.BlockSpec(memory_space=pltpu.VMEM)),
        input_output_aliases={i: k + i for i in range(n)},
        compiler_params=pltpu.CompilerParams(has_side_effects=EFFECT),
    )(*[pltpu.with_memory_space_constraint(a, pltpu.HBM) for a in srcs], *extra)
    return list(out[:k]), list(out[k:k + n]), list(out[k + n:k + 2 * n]), out[-1]


def send_wait(started, after, scatter, scope, name, with_sources=False):
    sems, srcs, lands, _ = started
    n, k = len(srcs), len(sems)

    def body(*refs):
        src, land = refs[:n], refs[n:2 * n]
        for cp in _own_copies(src, land, refs[2 * n + k - 1], scatter, scope):
            cp.wait()
        for cp in _send_copies(src, land, refs[2 * n], refs[2 * n + 1], scatter, scope, False):
            cp.wait_send()
            cp.wait_recv()

    arrs = list(srcs) + list(lands)
    out = pl.pallas_call(
        body, name=name, out_shape=tuple(pltpu.HBM(a.shape, a.dtype) for a in arrs),
        in_specs=[HBM] * (2 * n) + [SEM] * k + [ANY], out_specs=tuple([HBM] * (2 * n)),
        input_output_aliases={i: i for i in range(2 * n)},
        compiler_params=pltpu.CompilerParams(has_side_effects=EFFECT),
    )(*arrs, *sems, after)
    return (list(out[:n]), list(out[n:])) if with_sources else list(out[n:])


def gather_weights(arrs, split):
    n = len(arrs)

    def body(*refs):
        ins, outs = refs[:n], refs[n:2 * n]
        send1, recv1, send2, recv2, loc_in, loc_out = refs[2 * n:2 * n + 6]
        staged = refs[2 * n + 6:]
        x, y, c = lax.axis_index("x"), lax.axis_index("y"), lax.axis_index("c")
        me = 2 * x + y
        sibling = (x, y, 1 - c)
        peers = [(1 - x, y), (x, 1 - y), (1 - x, 1 - y)]

        def rows_of(t, core):
            half = arrs[t].shape[0] // 2
            return pl.ds(core * half, half)

        def part(ref, t, core):
            return ref.at[rows_of(t, core)] if split[t] else ref

        load = [pltpu.make_async_copy(ins[t], staged[t], loc_in.at[t]) for t in range(n)]
        store = [pltpu.make_async_copy(staged[t], outs[t].at[me], loc_out.at[t]) for t in range(n)]
        for cp in load:
            cp.start()
        first = []
        for t in range(n):
            for j, (px, py) in enumerate(peers):
                first.append(pltpu.make_async_remote_copy(
                    src_ref=part(ins[t], t, c), dst_ref=part(outs[t].at[me], t, c), send_sem=send1.at[t, j],
                    recv_sem=recv1.at[t, j], device_id=(px, py, c), device_id_type=MESH))
        for cp in first:
            cp.start()
        for cp_in, cp_out in zip(load, store):
            cp_in.wait()
            cp_out.start()
        passed = []
        for t in range(n):
            for j, (px, py) in enumerate(peers):
                landed = part(outs[t].at[2 * px + py], t, c)
                pltpu.make_async_remote_copy(
                    src_ref=landed, dst_ref=landed, send_sem=send1.at[t, j], recv_sem=recv1.at[t, j],
                    device_id=(x, y, c), device_id_type=MESH).wait_recv()
                if split[t]:
                    cp = pltpu.make_async_remote_copy(
                        src_ref=landed, dst_ref=landed, send_sem=send2.at[t, j], recv_sem=recv2.at[t, j],
                        device_id=sibling, device_id_type=MESH)
                    cp.start()
                    passed.append(cp)
        for t in range(n):
            for j, (px, py) in enumerate(peers):
                if split[t]:
                    other = part(outs[t].at[2 * px + py], t, 1 - c)
                    pltpu.make_async_remote_copy(
                        src_ref=other, dst_ref=other, send_sem=send2.at[t, j], recv_sem=recv2.at[t, j],
                        device_id=(x, y, c), device_id_type=MESH).wait_recv()
        for cp in first + passed:
            cp.wait_send()
        for cp in store:
            cp.wait()

    return pl.pallas_call(
        body, name="gather_weights", in_specs=[ANY] * n, out_specs=[ANY] * n,
        out_shape=[jax.ShapeDtypeStruct((NCHIP,) + a.shape, a.dtype) for a in arrs],
        scratch_shapes=[pltpu.SemaphoreType.DMA((n, 3))] * 4 + [pltpu.SemaphoreType.DMA((n,))] * 2
        + [pltpu.VMEM(a.shape, a.dtype) for a in arrs],
    )(*arrs)


def _adam_math(g, w, m, v):
    m = ADAM_B1 * m + (1.0 - ADAM_B1) * g
    v = ADAM_B2 * v + (1.0 - ADAM_B2) * (g * g)
    m_hat = m / (1.0 - ADAM_B1 ** ADAM_STEP)
    v_hat = v / (1.0 - ADAM_B2 ** ADAM_STEP)
    delta = -ADAM_LR * (m_hat / (jnp.sqrt(v_hat) + ADAM_EPS) + ADAM_WD * w)
    return delta, m, v


def _rows_tile(rows):
    return rows if rows <= 256 else 256


def sum_chips(parts, name):
    _, rows, cols = parts.shape
    tr = _rows_tile(rows)

    def body(p_ref, o_ref):
        acc = p_ref[0].astype(F32)
        for s in range(1, NCHIP):
            acc = acc + p_ref[s].astype(F32)
        o_ref[...] = acc

    return pl.pallas_call(
        body, grid=(rows // tr,), name=name,
        in_specs=[pl.BlockSpec((NCHIP, tr, cols), lambda i: (0, i, 0))],
        out_specs=pl.BlockSpec((tr, cols), lambda i: (i, 0)),
        out_shape=jax.ShapeDtypeStruct((rows, cols), F32),
        compiler_params=_cparams(("parallel",)),
    )(parts)


def sum_chips_small(parts, name):
    n = len(parts)

    def body(*refs):
        for p_ref, o_ref in zip(refs[:n], refs[n:]):
            acc = p_ref[0]
            for s in range(1, NCHIP):
                acc = acc + p_ref[s]
            o_ref[...] = acc

    return pl.pallas_call(body, name=name, out_shape=[jax.ShapeDtypeStruct(p.shape[1:], F32) for p in parts])(*parts)


def adam_shard_small(items, name):
    n = len(items)

    def body(*refs):
        for t in range(n):
            a_ref, b_ref, w_ref, m_ref, v_ref = refs[5 * t:5 * t + 5]
            g_ref, d_ref, mo_ref, vo_ref = refs[5 * n + 4 * t:5 * n + 4 * t + 4]
            g = (a_ref[...] + b_ref[...]).reshape(w_ref.shape)
            g_ref[...] = g
            d_ref[...], mo_ref[...], vo_ref[...] = _adam_math(g, w_ref[...], m_ref[...], v_ref[...])

    out = pl.pallas_call(
        body, name=name, out_shape=[jax.ShapeDtypeStruct(it[2].shape, F32) for it in items for _ in range(4)],
    )(*[a for it in items for a in it])
    return [out[4 * t:4 * t + 4] for t in range(n)]


def adam_shard(p_mine, p_sib, w, m, v, name):
    rows, cols = p_mine.shape
    tr = _rows_tile(rows)
    lead = w.ndim == 3

    def body(a_ref, b_ref, w_ref, m_ref, v_ref, g_ref, d_ref, mo_ref, vo_ref):
        g = a_ref[...] + b_ref[...]
        g = g[None] if lead else g
        g_ref[...] = g
        d_ref[...], mo_ref[...], vo_ref[...] = _adam_math(g, w_ref[...], m_ref[...], v_ref[...])

    flat = pl.BlockSpec((tr, cols), lambda i: (i, 0))
    spec = pl.BlockSpec((1, tr, cols), lambda i: (0, i, 0)) if lead else flat
    return pl.pallas_call(
        body, grid=(rows // tr,), name=name, in_specs=[flat] * 2 + [spec] * 3, out_specs=[spec] * 4,
        out_shape=[jax.ShapeDtypeStruct(w.shape, F32)] * 4,
        compiler_params=_cparams(("parallel",)),
    )(p_mine, p_sib, w, m, v)


def adam_shard_halves_t(r_mine, r_sib, wt, mt, vt, name):
    hrows, cols = r_mine.shape
    tr = _rows_tile(hrows)
    per_half = hrows // tr

    def body(a_ref, b_ref, w_ref, m_ref, v_ref, g_ref, d_ref, mo_ref, vo_ref):
        mine = pl.program_id(0) == lax.axis_index("c")
        g = jnp.where(mine, a_ref[...], b_ref[...]).T[None]
        g_ref[...] = g
        d_ref[...], mo_ref[...], vo_ref[...] = _adam_math(g, w_ref[...], m_ref[...], v_ref[...])

    flat = pl.BlockSpec((tr, cols), lambda h, i: (i, 0))
    spec = pl.BlockSpec((1, cols, tr), lambda h, i: (0, 0, h * per_half + i))
    return pl.pallas_call(
        body, grid=(2, per_half), name=name, in_specs=[flat] * 2 + [spec] * 3, out_specs=[spec] * 4,
        out_shape=[jax.ShapeDtypeStruct(wt.shape, F32)] * 4,
        compiler_params=_cparams(("parallel", "parallel")),
    )(r_mine, r_sib, wt, mt, vt)


def adam_replicated(gathered, params, name):
    flat = []
    for i, p in enumerate(params):
        if isinstance(p, list):
            off = 0
            for wmv in p:
                n = gathered[i].shape[-1] - off if wmv[0] is None else wmv[0].shape[-1]
                flat.append((i, (off, n), wmv))
                off += n
        else:
            flat.append((i, None, p))
    ins = [a for _, _, wmv in flat for a in wmv if a is not None]
    ng = len(gathered)

    def body(*refs):
        g_refs = refs[:ng]
        in_refs = list(refs[ng:ng + len(ins)])
        out_refs = list(refs[ng + len(ins):])
        sums = []
        for r in g_refs:
            g = r[0]
            for d in range(1, NDEV):
                g = g + r[d]
            sums.append(g)
        for i, lanes, wmv in flat:
            g = sums[i] if lanes is None else sums[i][:, lanes[0]:lanes[0] + lanes[1]]
            out_refs.pop(0)[...] = g
            if wmv[0] is not None:
                w_ref, m_ref, v_ref = in_refs.pop(0), in_refs.pop(0), in_refs.pop(0)
                d_ref, mo_ref, vo_ref = out_refs.pop(0), out_refs.pop(0), out_refs.pop(0)
                d_ref[...], mo_ref[...], vo_ref[...] = _adam_math(g, w_ref[...], m_ref[...], v_ref[...])

    out_shape = []
    for i, lanes, wmv in flat:
        shape = gathered[i].shape[1:] if lanes is None else (1, lanes[1])
        out_shape += [jax.ShapeDtypeStruct(shape, F32)] * (4 if wmv[0] is not None else 1)
    outs = list(pl.pallas_call(body, name=name, out_shape=out_shape)(*gathered, *ins))
    return [[outs.pop(0) for _ in range(4 if wmv[0] is not None else 1)] for _, _, wmv in flat]


EVEN_SPLITS = (SHIFT, W, W, W, W, W)
ODD_SPLITS = (D, D, D)


def _cols_to_chips(a):
    rows, cols = a.shape
    return a.reshape(rows, NCHIP, cols // NCHIP).transpose(1, 0, 2)


def _chips_to_cols(a):
    _, rows, n = a.shape
    return a.transpose(1, 0, 2).reshape(rows, NCHIP * n)


def kernel(x, norm_g, w_in_e, shift_mu, rw_w0, rw_w2, rw_a0, rw_a2, rw_kk, rw_ka, rw_rk, rw_lnx_g, rw_lnx_b, att_bias, w_out_e, w_in_o, sg_ln_g, sg_ln_b, sg_w, sg_b, w_out_o, final_g, loss_target, m_norm_g, m_w_in_e, m_shift_mu, m_rw_w0, m_rw_w2, m_rw_a0, m_rw_a2, m_rw_kk, m_rw_ka, m_rw_rk, m_rw_lnx_g, m_rw_lnx_b, m_att_bias, m_w_out_e, m_w_in_o, m_sg_ln_g, m_sg_ln_b, m_sg_w, m_sg_b, m_w_out_o, m_final_g, v_norm_g, v_w_in_e, v_shift_mu, v_rw_w0, v_rw_w2, v_rw_a0, v_rw_a2, v_rw_kk, v_rw_ka, v_rw_rk, v_rw_lnx_g, v_rw_lnx_b, v_att_bias, v_w_out_e, v_w_in_o, v_sg_ln_g, v_sg_ln_b, v_sg_w, v_sg_b, v_w_out_o, v_final_g):
    x2 = x.reshape(T, D)
    tgt = loss_target.reshape(T, D)

    gathered = gather_weights(
        [jnp.swapaxes(w_in_e[0], 0, 1).astype(BF16), jnp.concatenate([rw_w2[0], rw_a2[0]], axis=0),
         jnp.concatenate([sg_ln_g, sg_ln_b], axis=0)], [True, True, False])
    wie = gathered[0].reshape(EVEN_IN, D)
    w2 = _chips_to_cols(gathered[1][:, :LORA])
    a2 = _chips_to_cols(gathered[1][:, LORA:])
    sglg = _chips_to_cols(gathered[2][:, 0:1])
    sglb = _chips_to_cols(gathered[2][:, 1:2])

    late = [w_out_e[0].astype(BF16), w_in_o[0].astype(BF16), w_out_o[0].astype(BF16)]
    late_started = send_start(late, False, CHIPS, "late_weights_start", after=gathered[0])

    def late_weights(after):
        woe, wio, woo = send_wait(late_started, after, False, CHIPS, "late_weights_wait")
        return woe.reshape(D, D), _chips_to_cols(wio), woo.reshape(D, D)

    def scatter_start(grads, name):
        return send_start([g_.astype(BF16) if g_.shape[-1] >= W else g_ for g_ in grads], True, CHIPS, name)

    started = {}

    def on_odd_grads(d_woo, d_wio):
        started["odd"] = scatter_start([d_woo.reshape(NCHIP, D // NCHIP, D), d_wio], "odd_grads_start")
        return started["odd"][-1]

    def on_even_grads(big_g):
        d_wie_half, d_woe, _, _, d_w2, d_a2, d_sglg, d_sglb = big_g
        blocks = [d_wie_half, d_woe.reshape(NCHIP, D // NCHIP, D), _cols_to_chips(d_w2), _cols_to_chips(d_a2),
                  _cols_to_chips(d_sglg), _cols_to_chips(d_sglb)]
        started["even"] = scatter_start(blocks, "even_grads_start")
        return started["even"][-1]

    def on_small_grads(layer, grads):
        if layer == "odd":
            d_sg_w, d_sg_b, d_final, d_g1 = grads
            mine = [d_sg_w.reshape(NG * SGC, SGC), d_sg_b, jnp.concatenate([d_final, d_g1], axis=1)]
        else:
            mine = [grads[-2], jnp.concatenate(grads[:-2] + grads[-1:], axis=1)]
        started[layer + "_small"] = send_start(mine, False, EVERY, layer + "_small_grads_start")
        return started[layer + "_small"][-1]

    loss_part, dx, _, _ = _local_step(
        x2, tgt, wie, late_weights, w2, a2, sglg, sglb, norm_g, shift_mu, rw_w0, rw_a0, rw_kk, rw_ka, rw_rk,
        rw_lnx_g, rw_lnx_b, att_bias, sg_w, sg_b, final_g, first_after=late_started[-1], on_odd_grads=on_odd_grads,
        on_even_grads=on_even_grads, on_small_grads=on_small_grads)
    wmv = {"w_in_e": tuple(jnp.swapaxes(a, 1, 2) for a in (w_in_e, m_w_in_e, v_w_in_e)),
           "w_out_e": (w_out_e, m_w_out_e, v_w_out_e),
           "w_in_o": (w_in_o, m_w_in_o, v_w_in_o), "w_out_o": (w_out_o, m_w_out_o, v_w_out_o),
           "rw_w2": (rw_w2, m_rw_w2, v_rw_w2), "rw_a2": (rw_a2, m_rw_a2, v_rw_a2),
           "sg_ln_g": (sg_ln_g, m_sg_ln_g, v_sg_ln_g), "sg_ln_b": (sg_ln_b, m_sg_ln_b, v_sg_ln_b)}
    sharded = {}

    def sum_and_swap(names, landed, tag):
        nbig = sum(p_.dtype == BF16 for p_ in landed)
        partial = [sum_chips(p_, "sum_" + nm) for p_, nm in zip(landed[:nbig], names)]
        if nbig < len(names):
            partial += sum_chips_small(landed[nbig:], "sum_small_" + tag)
        return send_start(partial, False, SIBLING, "swap_partials_" + tag + "_start")

    def update(names, swap_started, after, tag):
        partial, landed = send_wait(swap_started, after, False, SIBLING, "swap_partials_" + tag + "_wait", True)
        from_sibling = [a[0] for a in landed]
        nbig = sum(p_.shape[-1] >= W for p_ in partial)
        for nm, mine, sib in zip(names[:nbig], partial, from_sibling):
            if nm == "w_in_e":
                res = adam_shard_halves_t(mine, sib, *wmv[nm], "adam_" + nm)
                sharded[nm] = [jnp.swapaxes(a, 1, 2) for a in res]
            else:
                sharded[nm] = adam_shard(mine, sib, *wmv[nm], "adam_" + nm)
        if nbig < len(names):
            items = [(mine, sib, *wmv[nm]) for nm, mine, sib in zip(names, partial, from_sibling)][nbig:]
            for nm, res in zip(names[nbig:], adam_shard_small(items, "adam_small_" + tag)):
                sharded[nm] = res

    odd_names = ["w_out_o", "w_in_o"]
    even_names = ["w_in_e", "w_out_e", "rw_w2", "rw_a2", "sg_ln_g", "sg_ln_b"]
    odd_landed = send_wait(started["odd"], started["even_small"][-1], True, CHIPS, "odd_grads_wait")
    odd_swap = sum_and_swap(odd_names, odd_landed, "odd")
    done = odd_swap[-1]

    def wmv_of(*arrs, view=lambda a: a):
        return tuple(view(a) for a in arrs)

    vec = lambda a: a.reshape(1, -1)
    groups = {
        "odd": (["sg_w", "sg_b", "final_g", "norm_g1"],
                [wmv_of(sg_w, m_sg_w, v_sg_w, view=lambda a: a.reshape(NG * SGC, SGC)),
                 wmv_of(sg_b, m_sg_b, v_sg_b, view=lambda a: a[0]),
                 [wmv_of(final_g, m_final_g, v_final_g, view=vec),
                  wmv_of(norm_g, m_norm_g, v_norm_g, view=lambda a: a[1:2])]]),
        "even": (["att_bias", "norm_g0", "shift_mu", "rw_w0", "rw_a0", "rw_kk", "rw_ka", "rw_rk", "rw_lnx_g",
                  "rw_lnx_b", "loss"],
                 [wmv_of(att_bias, m_att_bias, v_att_bias, view=lambda a: a[0]),
                  [wmv_of(norm_g, m_norm_g, v_norm_g, view=lambda a: a[0:1]),
                   wmv_of(shift_mu, m_shift_mu, v_shift_mu), wmv_of(rw_w0, m_rw_w0, v_rw_w0),
                   wmv_of(rw_a0, m_rw_a0, v_rw_a0), wmv_of(rw_kk, m_rw_kk, v_rw_kk), wmv_of(rw_ka, m_rw_ka, v_rw_ka),
                   wmv_of(rw_rk, m_rw_rk, v_rw_rk, view=vec), wmv_of(rw_lnx_g, m_rw_lnx_g, v_rw_lnx_g),
                   wmv_of(rw_lnx_b, m_rw_lnx_b, v_rw_lnx_b), (None, None, None)]]),
    }
    rep = {}
    for layer in ("odd", "even"):
        nms, params = groups[layer]
        gathered_g = send_wait(started[layer + "_small"], done, False, EVERY, layer + "_small_grads_wait")
        for nm, res in zip(nms, adam_replicated(gathered_g, params, "adam_" + layer + "_small")):
            rep[nm] = res
        done = rep[nms[0]][0]
    native = {"sg_w": sg_w.shape, "sg_b": sg_b.shape, "final_g": final_g.shape, "rw_rk": rw_rk.shape,
              "att_bias": att_bias.shape}
    for nm, shape in native.items():
        rep[nm] = [a.reshape(shape) for a in rep[nm]]
    rep["norm_g"] = [jnp.concatenate([a, b], axis=0) for a, b in zip(rep["norm_g0"], rep["norm_g1"])]
    even_landed = send_wait(started["even"], done, True, CHIPS, "even_grads_wait")
    even_swap = sum_and_swap(even_names, even_landed, "even")
    update(odd_names, odd_swap, even_swap[-1], "odd")
    update(even_names, even_swap, sharded["w_in_o"][0], "even")

    order = ["norm_g", "w_in_e", "shift_mu", "rw_w0", "rw_w2", "rw_a0", "rw_a2", "rw_kk", "rw_ka", "rw_rk",
             "rw_lnx_g", "rw_lnx_b", "att_bias", "w_out_e", "w_in_o", "sg_ln_g", "sg_ln_b", "sg_w", "sg_b",
             "w_out_o", "final_g"]
    results = {**sharded, **rep}
    outs = [rep["loss"][0][0, 0], dx.reshape(NSEQ, SEQ, D)]
    for kind in range(4):
        outs += [results[nm][kind] for nm in order]
    return tuple(outs)


def _local_step(x2, tgt, wie_t, late_weights, w2, a2, sglg, sglb, norm_g, shift_mu, rw_w0, rw_a0, rw_kk, rw_ka, rw_rk,
                rw_lnx_g, rw_lnx_b, att_bias, sg_w, sg_b, final_g, first_after=None, on_odd_grads=None,
                on_even_grads=None, on_small_grads=None):
    zl = jnp.zeros((LORA, W), F32)
    w2x = jnp.concatenate([w2, zl], axis=0)
    a2x = jnp.concatenate([zl, a2], axis=0)
    rk = rw_rk.reshape(1, W)
    pos = np.arange(SGC)
    sg_mask = jnp.asarray(((pos[None, :] // L) <= (pos[:, None] // L)).astype(np.float32))
    wm = (sg_w[0] * sg_mask[None]).astype(BF16)
    sgb_t = sg_b[0].T

    xn0, ps, ga, q, kb, vb, gb = ln_in_proj(x2, norm_g[0:1], wie_t, EVEN_SPLITS, "in_proj_even", after=first_after,
                                            w_t=True, bf16_pieces=(2, 3, 4))
    r, lw, k2, v, aa, bb = even_prep(ps, shift_mu, rw_w0, w2x, rw_a0, a2x, rw_kk, rw_ka)
    y, rw_saved = rwkv_fwd(r, lw, k2, v, aa, bb)
    bias = bias_expand(att_bias[0])

    def padded(a):
        return jnp.pad(a.astype(BF16).reshape(NSEQ, SEQ, W), ((0, 0), (LEFT * L, 0), (0, 0))).reshape(NSEQ * PADSEQ, W)

    kpad, vpad = padded(kb), padded(vb)
    o = attention_fwd(q, kpad, vpad, bias)
    woe, wio, woo = late_weights(o)
    h1, zt = even_post(y, r, k2, v, ga, o, gb, rw_lnx_g, rw_lnx_b, rk, x2, woe)
    xn1, u, vv, gt = ln_in_proj(h1, norm_g[1:2], wio, ODD_SPLITS, "in_proj_odd")
    dh2, loss_part, d_final_g, z2t = gmlp_fwd_loss(u, vv, gt, sglg, sglb, wm, sgb_t, h1, woo, final_g[None], tgt)

    du, dvv, dgt, d_sglg, d_sglb, d_wm, d_sgb_t, d_woo = gmlp_bwd(u, vv, gt, sglg, sglb, wm, sgb_t, dh2, z2t, woo)
    dp_odd = [du, dvv, dgt]
    d_wio = matmul_acc_chips(xn1, dp_odd, "in_proj_odd_dw")
    token = on_odd_grads(d_woo, d_wio) if on_odd_grads else None
    dh1, d_g1 = in_proj_bwd_x(h1, norm_g[1:2], wio, dp_odd, dh2, "in_proj_odd_bwd", after=token)
    odd_small = [d_wm * sg_mask[None], d_sgb_t.T, d_final_g, d_g1]
    token = on_small_grads("odd", odd_small) if on_small_grads else None
    dy, dr2, dk22, dv2, dga, do, dgb, d_lng, d_lnb, d_rk, d_woe = even_post_bwd(
        y, r, k2, v, ga, o, gb, rw_lnx_g, rw_lnx_b, rk, dh1, zt, woe, after=token)
    dq, dkb, dvb, dbias = attention_bwd(q, kpad, vpad, bias, do)
    dbias = sum(dbias[:, i * 2 * L:(i + 1) * 2 * L, i * L:i * L + BAND] for i in range(ATT_Q))
    d_att_bias = bias_grad(dbias.reshape(NH, L, BAND))
    dr, dlw, dk2, dv, daa, dbb = rwkv_bwd(r, lw, k2, aa, bb, rw_saved, dy)
    dps, d_mu, d_w0, d_w2x, d_a0, d_a2x, d_kk, d_ka = even_prep_bwd(
        ps, shift_mu, rw_w0, w2x, rw_a0, a2x, rw_kk, rw_ka, dr, dlw, dk2, dv, daa, dbb, dr2, dk22, dv2)
    dp_even = [dps, dga, dq, dkb, dvb, dgb]
    d_wie = matmul_acc_chips(xn0, dp_even, "in_proj_even_dw", add_cores=on_even_grads is not None)
    big_g = (d_wie, d_woe, d_wio, d_woo, d_w2x[:LORA], d_a2x[LORA:], d_sglg, d_sglb)
    token = on_even_grads(big_g) if on_even_grads else None
    dx, d_g0 = in_proj_bwd_x(x2, norm_g[0:1], wie_t, dp_even, dh1, "in_proj_even_bwd", after=token, w_t=True)
    even_small = [d_g0, d_mu, d_w0, d_a0, d_kk, d_ka, d_rk, d_lng, d_lnb, d_att_bias]
    if on_small_grads:
        on_small_grads("even", even_small + [loss_part[0:1, :]])
    rep_g = [jnp.concatenate([d_g0, d_g1], axis=0)] + even_small[1:] + odd_small[:3]
    return loss_part[0, 0], dx, big_g, rep_g
```

```python
import functools
import math

import jax
import jax.numpy as jnp
import numpy as np
from jax import lax
from jax.experimental import pallas as pl
from jax.experimental.pallas import tpu as pltpu

F32 = jnp.float32
BF16 = jnp.bfloat16
HI = lax.Precision.HIGHEST

D = 1024
SEQ = 2048
NSEQ = 2
T = NSEQ * SEQ
HD = 64
NH = 8
W = 512
SHIFT = 1664
LORA = 64
EVEN_IN = 4224
ODD_IN = 3072
L = 64
NC = SEQ // L
LEFT = 8
BAND = (LEFT + 1) * L
CLIP = 128
SGC = 128
NG = 8
RMS_EPS = 1e-6
LN_EPS = 1e-5
GN_EPS = 64e-5
NEG = -1e30
VMEM_BIG = 56 * 1024 * 1024

ADAM_LR = 0.001
ADAM_B1 = 0.9
ADAM_B2 = 0.999
ADAM_EPS = 1e-08
ADAM_WD = 0.01
ADAM_STEP = 10

MESH = pl.DeviceIdType.MESH


def _bdot(a, b):
    return jnp.dot(a.astype(BF16), b.astype(BF16), preferred_element_type=F32)


def _bdot_nt(a, b):
    return lax.dot_general(a.astype(BF16), b.astype(BF16), (((1,), (1,)), ((), ())), preferred_element_type=F32)


def _bdot_tn(a, b):
    return lax.dot_general(a.astype(BF16), b.astype(BF16), (((0,), (0,)), ((), ())), preferred_element_type=F32)


def _hdot(a, b):
    return jnp.dot(a, b, precision=HI, preferred_element_type=F32)


def _hdot_nt(a, b):
    return lax.dot_general(a, b, (((1,), (1,)), ((), ())), precision=HI, preferred_element_type=F32)


def _hdot_tn(a, b):
    return lax.dot_general(a, b, (((0,), (0,)), ((), ())), precision=HI, preferred_element_type=F32)


def _iota2(shape, dim):
    return lax.broadcasted_iota(jnp.int32, shape, dim)


def _head_blockdiag():
    r = _iota2((2 * HD, 2 * HD), 0) // HD
    c = _iota2((2 * HD, 2 * HD), 1) // HD
    return (r == c).astype(BF16)


def _headsum_impl(x, bd):
    hi = x.astype(BF16)
    mid = (x - hi.astype(F32)).astype(BF16)
    n = bd.shape[0]
    out = [jnp.dot(hi[:, i:i + n], bd, preferred_element_type=F32) + jnp.dot(mid[:, i:i + n], bd, preferred_element_type=F32)
           for i in range(0, x.shape[1], n)]
    return jnp.concatenate(out, axis=-1)


@jax.custom_vjp
def _headsum(x, bd):
    return _headsum_impl(x, bd)


def _headsum_fwd(x, bd):
    return _headsum_impl(x, bd), bd


def _headsum_bwd(bd, ct):
    return _headsum_impl(ct, bd), None


_headsum.defvjp(_headsum_fwd, _headsum_bwd)


def _silu(x):
    return x * jax.nn.sigmoid(x)


_GELU_C = math.sqrt(2.0 / math.pi)


def _gelu(x):
    return 0.5 * x * (1.0 + jnp.tanh(_GELU_C * (x + 0.044715 * (x * x * x))))


def _silu_both(x):
    s = jax.nn.sigmoid(x)
    xs = x * s
    return xs, s + xs * (1.0 - s)


def _gelu_both(x):
    x2 = x * x
    t = jnp.tanh(_GELU_C * (x + 0.044715 * (x2 * x)))
    half = 0.5 * (1.0 + t)
    return x * half, half + 0.5 * x * (1.0 - t * t) * _GELU_C * (1.0 + 3.0 * 0.044715 * x2)


def _softplus(x):
    return jnp.maximum(x, 0.0) + jnp.log(1.0 + jnp.exp(-jnp.abs(x)))


def _cparams(sem, vmem=None):
    return pltpu.CompilerParams(dimension_semantics=sem, vmem_limit_bytes=vmem)


def _row_spec(tm, width):
    return pl.BlockSpec((tm, width), lambda i: (i, 0))


def _col_spec(height, tm):
    return pl.BlockSpec((height, tm), lambda i: (0, i))


def _const_spec(shape):
    nd = len(shape)
    return pl.BlockSpec(shape, lambda *_: (0,) * nd)


def _weight_dims(w_bf, w_t):
    return (((1,), (1,)), ((), ())) if w_t else (((1,), (0,)), ((), ())), w_bf.shape[0 if w_t else 1]


def ln_in_proj(x, g, w_bf, splits, name, after=None, w_t=False, bf16_pieces=()):
    dims, n = _weight_dims(w_bf, w_t)
    dtypes = [BF16 if i in bf16_pieces else F32 for i in range(len(splits))]
    tm = 512 if n <= ODD_IN else 256
    spans = []
    o = 0
    for s in splits:
        spans.append((o, o + s))
        o += s
    assert o == n
    extra_specs, extra = _after_operand(after)

    def body(x_ref, g_ref, w_ref, *rest):
        xn_ref, outs = rest[len(extra)], rest[len(extra) + 1:]
        xv = x_ref[...]
        rstd = lax.rsqrt(jnp.mean(xv * xv, axis=-1, keepdims=True) + RMS_EPS)
        xn = (xv * rstd * g_ref[...]).astype(BF16)
        xn_ref[...] = xn.T
        p = lax.dot_general(xn, w_ref[...], dims, preferred_element_type=F32)
        for o_ref, (a, b) in zip(outs, spans):
            o_ref[...] = p[:, a:b].astype(o_ref.dtype)

    return pl.pallas_call(
        body, grid=(T // tm,), name=name,
        in_specs=[_row_spec(tm, D), _const_spec((1, D)), _const_spec(w_bf.shape)] + extra_specs,
        out_specs=[_col_spec(D, tm)] + [_row_spec(tm, s) for s in splits],
        out_shape=[jax.ShapeDtypeStruct((D, T), BF16)]
        + [jax.ShapeDtypeStruct((T, s), dt) for s, dt in zip(splits, dtypes)],
        compiler_params=_cparams(("parallel",), VMEM_BIG),
    )(x, g, w_bf, *extra)


def in_proj_bwd_x(x, g, w_bf, dps, dres, name, after=None, w_t=False):
    tm = 512
    back = (((1,), (0,)), ((), ())) if w_t else (((1,), (1,)), ((), ()))
    widths = [d.shape[1] for d in dps]
    extra_specs, extra = _after_operand(after)

    def body(x_ref, g_ref, w_ref, dres_ref, *rest):
        dp_refs = rest[:len(widths)]
        dx_ref, dg_ref = rest[-2:]
        dp = jnp.concatenate([r[...] for r in dp_refs], axis=-1)
        dxn = lax.dot_general(dp, w_ref[...], back, preferred_element_type=F32)
        xv = x_ref[...]
        rstd = lax.rsqrt(jnp.mean(xv * xv, axis=-1, keepdims=True) + RMS_EPS)
        xhat = xv * rstd
        dgp = jnp.sum(dxn * xhat, axis=0, keepdims=True)

        @pl.when(pl.program_id(0) == 0)
        def _():
            dg_ref[...] = jnp.zeros_like(dg_ref)

        dg_ref[...] += dgp
        dxh = dxn * g_ref[...]
        dx_ref[...] = dres_ref[...] + rstd * (dxh - xhat * jnp.mean(dxh * xhat, axis=-1, keepdims=True))

    return pl.pallas_call(
        body, grid=(T // tm,), name=name,
        in_specs=[_row_spec(tm, D), _const_spec((1, D)), _const_spec(w_bf.shape), _row_spec(tm, D)]
        + [_row_spec(tm, s) for s in widths] + extra_specs,
        out_specs=[_row_spec(tm, D), _const_spec((1, D))],
        out_shape=[jax.ShapeDtypeStruct((T, D), F32), jax.ShapeDtypeStruct((1, D), F32)],
        compiler_params=_cparams(("arbitrary",), VMEM_BIG),
    )(x, g, w_bf, dres, *dps, *extra)


def _after_operand(after):
    return ([ANY], [after]) if after is not None else ([], [])


def matmul_acc_chips(at_bf, pieces, name, after=None, add_cores=False):
    k = at_bf.shape[0]
    widths = [p.shape[1] for p in pieces]
    nb = sum(widths) // NCHIP
    tm = 512
    steps = T // tm
    half = k // 2
    extra_specs, extra = _after_operand(after)

    def body(a_ref, *rest):
        o_ref, acc = rest[len(widths) + len(extra):][:2]

        @pl.when(pl.program_id(0) == 0)
        def _():
            acc[...] = jnp.zeros_like(acc)

        a = a_ref[...]
        b = jnp.concatenate([r[...] for r in rest[:len(widths)]], axis=-1)
        for s in range(NCHIP):
            acc[s] += jnp.dot(a, b[:, s * nb:(s + 1) * nb], preferred_element_type=F32)

        @pl.when(pl.program_id(0) == steps - 1)
        def _():
            if not add_cores:
                o_ref[...] = acc[...].astype(BF16)
            else:
                give, got, send, recv = rest[-4:]
                x, y, c = lax.axis_index("x"), lax.axis_index("y"), lax.axis_index("c")
                theirs = pl.multiple_of((1 - c) * half, half)
                mine = pl.multiple_of(c * half, half)
                give[...] = acc[:, pl.ds(theirs, half), :].astype(BF16)
                cp = pltpu.make_async_remote_copy(src_ref=give, dst_ref=got, send_sem=send, recv_sem=recv,
                                                  device_id=(x, y, 1 - c), device_id_type=MESH)
                cp.start()
                cp.wait()
                o_ref[...] = (acc[:, pl.ds(mine, half), :] + got[...].astype(F32)).astype(BF16)

    out_rows = half if add_cores else k
    exchange = [pltpu.VMEM((NCHIP, half, nb), BF16)] * 2 + [pltpu.SemaphoreType.DMA] * 2 if add_cores else []
    return pl.pallas_call(
        body, grid=(steps,), name=name,
        in_specs=[_col_spec(k, tm)] + [_row_spec(tm, w_) for w_ in widths] + extra_specs,
        out_specs=_const_spec((NCHIP, out_rows, nb)),
        out_shape=jax.ShapeDtypeStruct((NCHIP, out_rows, nb), BF16),
        scratch_shapes=[pltpu.VMEM((NCHIP, k, nb), F32)] + exchange,
        compiler_params=_cparams(("arbitrary",), VMEM_BIG),
    )(at_bf, *pieces, *extra)


def _out_proj_back(dh_ref, zt_ref, w_ref, dw_ref, acc_ref):
    dhb = dh_ref[...].astype(BF16)

    @pl.when(pl.program_id(0) == 0)
    def _():
        acc_ref[...] = jnp.zeros_like(acc_ref)

    acc_ref[...] += jnp.dot(zt_ref[...], dhb, preferred_element_type=F32)

    @pl.when(pl.program_id(0) == pl.num_programs(0) - 1)
    def _():
        dw_ref[...] = acc_ref[...].astype(dw_ref.dtype)

    return lax.dot_general(dhb, w_ref[...], (((1,), (1,)), ((), ())), preferred_element_type=F32)


PREP_TM = 512
PREP_NB = SEQ // PREP_TM


def _prep_elem(k, wl, apre, kkw, kaw, bd):
    wraw = -_softplus(-wl) - 0.5
    lw = -jnp.exp(wraw)
    asig = jax.nn.sigmoid(apre)
    kkr = k * kkw
    nrm = jnp.maximum(jnp.sqrt(_headsum(kkr * kkr, bd)), 1e-12)
    kk = kkr / nrm
    k2 = k * (1.0 + (asig - 1.0) * kaw)
    return lw, k2, -kk, kk * asig


def _prep_elem_bwd(k, wl, apre, kkw, kaw, bd, dlw, dk2, daa, dbb):
    s = -wl
    sp = _softplus(s)
    dwl = dlw * (-jnp.exp(-sp - 0.5)) * jnp.exp(s - sp)
    asig = jax.nn.sigmoid(apre)
    kkr = k * kkw
    root = jnp.sqrt(_headsum(kkr * kkr, bd))
    inv = 1.0 / jnp.maximum(root, 1e-12)
    kk = kkr * inv
    dkk = dbb * asig - daa
    dap = (dbb * kk + dk2 * k * kaw) * asig * (1.0 - asig)
    through_norm = jnp.where(root > 1e-12, kk * _headsum(dkk * kkr, bd) * inv, 0.0)
    dkkr = inv * (dkk - through_norm)
    gain = 1.0 + (asig - 1.0) * kaw
    dk = dkkr * kkw + dk2 * gain
    dkkw = jnp.sum(dkkr * k, axis=0, keepdims=True)
    dkaw = jnp.sum(dk2 * k * (asig - 1.0), axis=0, keepdims=True)
    return dk, dwl, dap, dkkw, dkaw


def _shifted(ps_ref, prev_ref, mu, blk):
    p = ps_ref[...]
    first = (blk % PREP_NB) == 0
    prev_row = jnp.where(first, 0.0, prev_ref[7:8, :])
    rolled = pltpu.roll(p, 1, 0)
    p_prev = jnp.where(_iota2(p.shape, 0) == 0, prev_row, rolled)
    return p, p_prev, p + (p_prev - p) * mu


def _prev_spec(width, blk_of):
    return pl.BlockSpec((8, width), lambda i: (jnp.maximum(blk_of(i) * (PREP_TM // 8) - 1, 0), 0))


def even_prep(ps, mu, w0, w2x, a0, a2x, kkw, kaw):
    tm = PREP_TM

    def body(ps_ref, prev_ref, mu_ref, w0_ref, w2_ref, a0_ref, a2_ref, kk_ref, ka_ref,
             r_ref, lw_ref, k2_ref, v_ref, aa_ref, bb_ref):
        _, _, s = _shifted(ps_ref, prev_ref, mu_ref[...], pl.program_id(0))
        wa = s[:, 3 * W:]
        wl = w0_ref[...] + _bdot(jnp.tanh(wa), w2_ref[...])
        apre = a0_ref[...] + _bdot(wa, a2_ref[...])
        lw, k2, aa, bb = _prep_elem(s[:, W:2 * W], wl, apre, kk_ref[...], ka_ref[...], _head_blockdiag())
        r_ref[...] = s[:, 0:W]
        v_ref[...] = s[:, 2 * W:3 * W]
        lw_ref[...] = lw
        k2_ref[...] = k2
        aa_ref[...] = aa
        bb_ref[...] = bb

    vec = _const_spec((1, W))
    return pl.pallas_call(
        body, grid=(T // tm,), name="even_prep",
        in_specs=[_row_spec(tm, SHIFT), _prev_spec(SHIFT, lambda i: i), _const_spec((1, SHIFT)), vec,
                  _const_spec((2 * LORA, W)), vec, _const_spec((2 * LORA, W)), vec, vec],
        out_specs=[_row_spec(tm, W)] * 6,
        out_shape=[jax.ShapeDtypeStruct((T, W), F32)] * 6,
        compiler_params=_cparams(("parallel",), VMEM_BIG),
    )(ps, ps, mu, w0, w2x, a0, a2x, kkw, kaw)


def even_prep_bwd(ps, mu, w0, w2x, a0, a2x, kkw, kaw, dr, dlw, dk2, dv, daa, dbb, dr2, dk22, dv2):
    tm = PREP_TM
    nb = T // tm
    rev = lambda i: nb - 1 - i

    def body(ps_ref, prev_ref, mu_ref, w0_ref, w2_ref, a0_ref, a2_ref, kk_ref, ka_ref,
             dr_ref, dlw_ref, dk2_ref, dv_ref, daa_ref, dbb_ref, dr2_ref, dk22_ref, dv2_ref,
             dps_ref, dmu_ref, dw0_ref, dw2_ref, da0_ref, da2_ref, dkk_ref, dka_ref, carry):
        i = pl.program_id(0)
        blk = rev(i)
        mu_v = mu_ref[...]
        p, p_prev, s = _shifted(ps_ref, prev_ref, mu_v, blk)
        wa = s[:, 3 * W:]
        th = jnp.tanh(wa)
        wl = w0_ref[...] + _bdot(th, w2_ref[...])
        apre = a0_ref[...] + _bdot(wa, a2_ref[...])
        bd = _head_blockdiag()
        k = s[:, W:2 * W]
        dk, dwl, dap, dkkw, dkaw = _prep_elem_bwd(k, wl, apre, kk_ref[...], ka_ref[...], bd, dlw_ref[...],
                                                  dk2_ref[...] + dk22_ref[...], daa_ref[...], dbb_ref[...])
        dwa = _bdot_nt(dwl, w2_ref[...]) * (1.0 - th * th) + _bdot_nt(dap, a2_ref[...])
        ds = jnp.concatenate([dr_ref[...] + dr2_ref[...], dk, dv_ref[...] + dv2_ref[...], dwa], axis=-1)

        @pl.when(i == 0)
        def _():
            for ref in (dmu_ref, dw0_ref, dw2_ref, da0_ref, da2_ref, dkk_ref, dka_ref, carry):
                ref[...] = jnp.zeros_like(ref)

        dmu_ref[...] += jnp.sum(ds * (p_prev - p), axis=0, keepdims=True)
        dw0_ref[...] += jnp.sum(dwl, axis=0, keepdims=True)
        da0_ref[...] += jnp.sum(dap, axis=0, keepdims=True)
        dw2_ref[...] += _bdot_tn(th, dwl)
        da2_ref[...] += _bdot_tn(wa, dap)
        dkk_ref[...] += dkkw
        dka_ref[...] += dkaw
        dsm = ds * mu_v
        last = (blk % PREP_NB) == PREP_NB - 1
        nxt = jnp.where(last, 0.0, carry[0:1, :])
        up = pltpu.roll(dsm, tm - 1, 0)
        up = jnp.where(_iota2(up.shape, 0) == tm - 1, nxt, up)
        dps_ref[...] = (ds - dsm + up).astype(BF16)
        carry[0:1, :] = dsm[0:1, :]

    vec = _const_spec((1, W))
    rrow = lambda width: pl.BlockSpec((tm, width), lambda i: (rev(i), 0))
    return pl.pallas_call(
        body, grid=(nb,), name="even_prep_bwd",
        in_specs=[rrow(SHIFT), _prev_spec(SHIFT, rev), _const_spec((1, SHIFT)), vec,
                  _const_spec((2 * LORA, W)), vec, _const_spec((2 * LORA, W)), vec, vec] + [rrow(W)] * 9,
        out_specs=[rrow(SHIFT), _const_spec((1, SHIFT)), vec, _const_spec((2 * LORA, W)), vec,
                   _const_spec((2 * LORA, W)), vec, vec],
        out_shape=[jax.ShapeDtypeStruct((T, SHIFT), BF16), jax.ShapeDtypeStruct((1, SHIFT), F32),
                   jax.ShapeDtypeStruct((1, W), F32), jax.ShapeDtypeStruct((2 * LORA, W), F32),
                   jax.ShapeDtypeStruct((1, W), F32), jax.ShapeDtypeStruct((2 * LORA, W), F32),
                   jax.ShapeDtypeStruct((1, W), F32), jax.ShapeDtypeStruct((1, W), F32)],
        scratch_shapes=[pltpu.VMEM((8, SHIFT), F32)],
        compiler_params=_cparams(("arbitrary",), VMEM_BIG),
    )(ps, ps, mu, w0, w2x, a0, a2x, kkw, kaw, dr, dlw, dk2, dv, daa, dbb, dr2, dk22, dv2)


NPAIR = NH // 2
PW = 2 * HD


def _pair_cols(p):
    return slice(p * PW, (p + 1) * PW)


def _pairs(a):
    return [a[:, _pair_cols(p)] for p in range(NPAIR)]


def _stack_pair(a):
    first = _iota2(a.shape, 1) < HD
    zero = jnp.zeros_like(a)
    return jnp.concatenate([jnp.where(first, a, zero), jnp.where(first, zero, a)], axis=0)


def _unstack_pair(a):
    n = a.shape[0] // 2
    return jnp.where(_iota2((n, PW), 1) < HD, a[:n], a[n:])


def _fold_pair(a):
    n = a.shape[0] // 2
    return a[:n] + a[n:]


def _chunk_masks():
    n = 4 * L
    row = _iota2((n, n), 0)
    col = _iota2((n, n), 1)
    same = ((row // L) & 1) == ((col // L) & 1)
    ri = row & (L - 1)
    ci = col & (L - 1)
    keep = same & (((row < 2 * L) & (ri > ci)) | ((row >= 2 * L) & (ri >= ci)))
    r1 = _iota2((L, L), 0)
    c1 = _iota2((L, L), 1)
    r2 = _iota2((2 * L, 2 * L), 0)
    c2 = _iota2((2 * L, 2 * L), 1)
    return keep.astype(F32), (r1 >= c1).astype(F32), (r2 == c2).astype(F32)


def _scaled(r, lw, k2, aa, bb, tri):
    g = _hdot(tri, lw)
    eg = jnp.exp(g)
    eng = jnp.exp(-g)
    egp = jnp.exp(g - lw)
    return eg, eng, egp, aa * egp, r * eg, bb * eng, k2 * eng


def _head_cols(h):
    return slice(h * HD, (h + 1) * HD)


def _per_head(a):
    return [a[:, _head_cols(h)] for h in range(NH)]


def _pairs_operands(at, rt, bt, kt):
    x = [jnp.concatenate([_stack_pair(a), _stack_pair(r)], axis=0).astype(BF16) for a, r in zip(_pairs(at), _pairs(rt))]
    yk = [jnp.concatenate([_stack_pair(b), _stack_pair(k)], axis=0).astype(BF16) for b, k in zip(_pairs(bt), _pairs(kt))]
    return x, yk


def _pairs_matrices(x, yk, keep, eye):
    m = [_bdot_nt(a, b) * keep for a, b in zip(x, yk)]
    p = [a[:2 * L, :2 * L] for a in m]
    tinv = [eye + a for a in p]
    for _ in range(5):
        p = [_bdot(a, a) for a in p]
        tinv = [t + _bdot(t, a) for t, a in zip(tinv, p)]
    return [a.astype(BF16) for a in m], [a.astype(BF16) for a in tinv]


def _pairs_fwd(x, yk, m, tinv, vw, s0, egl):
    xh = [_bdot_nt(a, s) for a, s in zip(x, s0)]
    u = [_bdot(t, h[:2 * L] + _bdot(a[:2 * L, 2 * L:], w)) for t, h, a, w in zip(tinv, xh, m, vw)]
    uv = [jnp.concatenate([a, w], axis=0).astype(BF16) for a, w in zip(u, vw)]
    y = [h[2 * L:] + _bdot(a[2 * L:], w) for h, a, w in zip(xh, m, uv)]
    sn = [e * (s + _bdot_tn(w, b)) for e, s, w, b in zip(egl, s0, uv, yk)]
    return y, sn, uv


def _pairs_bwd(x, yk, m, tinv, uv, s0, sn, egl, dyw, dsn, keep):
    dzs = [d * e for d, e in zip(dsn, egl)]
    dgl = [jnp.sum(d * s, axis=0, keepdims=True) for d, s in zip(dsn, sn)]
    dyb = [a.astype(BF16) for a in dyw]
    t1 = [_bdot_tn(a[2 * L:], d) for a, d in zip(m, dyb)]
    t2 = [_bdot_nt(b, d) for b, d in zip(yk, dzs)]
    drhs = [_bdot_tn(t, a[:2 * L] + b[:2 * L]) for t, a, b in zip(tinv, t1, t2)]
    dv = [a[2 * L:] + b[2 * L:] + _bdot_tn(c[:2 * L, 2 * L:], d) for a, b, c, d in zip(t1, t2, m, drhs)]
    gg = [jnp.concatenate([a, b], axis=0).astype(BF16) for a, b in zip(drhs, dyw)]
    ds0 = [d + _bdot_tn(g, a) for d, g, a in zip(dzs, gg, x)]
    dm = [_bdot_nt(g, w) * keep for g, w in zip(gg, uv)]
    dx = [_bdot(g, s) + _bdot(d, b) for g, s, d, b in zip(gg, s0, dm, yk)]
    dyk = [_bdot_tn(d, a) + _bdot(w, z) for d, a, w, z in zip(dm, x, uv, dzs)]
    return dx, dyk, dv, dgl, ds0


STATE_SHAPE = (NPAIR * PW, PW)
M_SHAPE = (4 * L, NPAIR * 4 * L)
TINV_SHAPE = (2 * L, NPAIR * 2 * L)


def _rows_of(a, n):
    return [a[i * n:(i + 1) * n, :] for i in range(NPAIR)]


def _both(f):
    out = []
    for s in range(NSEQ):
        out += f(s)
    return out


def _seq_view(a):
    return a.reshape(NSEQ, SEQ, a.shape[-1])


UV_SHAPE = (4 * L, NPAIR * PW)
RW_CHUNKS = 2


def rwkv_fwd(r, lw, k2, v, aa, bb):
    def body(r_ref, lw_ref, k2_ref, v_ref, aa_ref, bb_ref, y_ref, hs_ref, hn_ref, m_ref, t_ref, uv_ref, state):
        @pl.when(pl.program_id(0) == 0)
        def _():
            state[...] = jnp.zeros_like(state)

        keep, tri, eye = _chunk_masks()
        where = [(j, s) for j in range(RW_CHUNKS) for s in range(NSEQ)]
        rows = lambda j: slice(j * L, (j + 1) * L)
        sc = [_scaled(r_ref[s, rows(j)], lw_ref[s, rows(j)], k2_ref[s, rows(j)], aa_ref[s, rows(j)],
                      bb_ref[s, rows(j)], tri) for j, s in where]
        ops = [_pairs_operands(*a[3:]) for a in sc]
        m, tinv = _pairs_matrices([a for o in ops for a in o[0]], [a for o in ops for a in o[1]], keep, eye)
        s_cur = [state[s] for s in range(NSEQ)]
        for j in range(RW_CHUNKS):
            mine = slice(j * NSEQ * NPAIR, (j + 1) * NSEQ * NPAIR)
            x = [a for o in ops[j * NSEQ:(j + 1) * NSEQ] for a in o[0]]
            yk = [a for o in ops[j * NSEQ:(j + 1) * NSEQ] for a in o[1]]
            vw = _both(lambda s: [_stack_pair(a) for a in _pairs(v_ref[s, rows(j)])])
            egl = _both(lambda s: _pairs(sc[j * NSEQ + s][0][L - 1:L, :]))
            y, sn, uv = _pairs_fwd(x, yk, m[mine], tinv[mine], vw, _both(lambda s: _rows_of(s_cur[s], PW)), egl)
            for s in range(NSEQ):
                ps = slice(s * NPAIR, (s + 1) * NPAIR)
                hs_ref[j, s] = s_cur[s]
                y_ref[s, rows(j)] = jnp.concatenate([_fold_pair(a) for a in y[ps]], axis=-1)
                m_ref[j, s] = jnp.concatenate(m[mine][ps], axis=-1)
                t_ref[j, s] = jnp.concatenate(tinv[mine][ps], axis=-1)
                uv_ref[j, s] = jnp.concatenate(uv[ps], axis=-1)
                s_cur[s] = jnp.concatenate(sn[ps], axis=0)
                hn_ref[j, s] = s_cur[s]
        for s in range(NSEQ):
            state[s] = s_cur[s]

    blk = pl.BlockSpec((NSEQ, RW_CHUNKS * L, W), lambda c: (0, c, 0))
    per_chunk = lambda shape: pl.BlockSpec((RW_CHUNKS, NSEQ) + shape, lambda c: (c, 0, 0, 0))
    saved_shapes = [(STATE_SHAPE, F32), (STATE_SHAPE, F32), (M_SHAPE, BF16), (TINV_SHAPE, BF16), (UV_SHAPE, BF16)]
    y, *saved = pl.pallas_call(
        body, grid=(NC // RW_CHUNKS,), name="rwkv_fwd",
        in_specs=[blk] * 6,
        out_specs=[blk] + [per_chunk(shape) for shape, _ in saved_shapes],
        out_shape=[jax.ShapeDtypeStruct((NSEQ, SEQ, W), F32)]
        + [jax.ShapeDtypeStruct((NC, NSEQ) + shape, dt) for shape, dt in saved_shapes],
        scratch_shapes=[pltpu.VMEM((NSEQ,) + STATE_SHAPE, F32)],
        compiler_params=_cparams(("arbitrary",), VMEM_BIG),
    )(*[_seq_view(a) for a in (r, lw, k2, v, aa, bb)])
    return y.reshape(T, W), saved


def rwkv_bwd(r, lw, k2, aa, bb, saved, dy):
    def body(r_ref, lw_ref, k2_ref, aa_ref, bb_ref, hs_ref, hn_ref, m_ref, t_ref, uv_ref, dy_ref,
             dr_ref, dlw_ref, dk2_ref, dv_ref, daa_ref, dbb_ref, dstate):
        @pl.when(pl.program_id(0) == 0)
        def _():
            dstate[...] = jnp.zeros_like(dstate)

        keep, tri, _ = _chunk_masks()
        sc = [_scaled(r_ref[s], lw_ref[s], k2_ref[s], aa_ref[s], bb_ref[s], tri) for s in range(NSEQ)]
        ops = [_pairs_operands(*sc[s][3:]) for s in range(NSEQ)]
        x, yk = _both(lambda s: ops[s][0]), _both(lambda s: ops[s][1])
        m = _both(lambda s: [m_ref[0, s][:, i * 4 * L:(i + 1) * 4 * L] for i in range(NPAIR)])
        tinv = _both(lambda s: [t_ref[0, s][:, i * 2 * L:(i + 1) * 2 * L] for i in range(NPAIR)])
        uv = _both(lambda s: _pairs(uv_ref[0, s]))
        dyw = _both(lambda s: [_stack_pair(a) for a in _pairs(dy_ref[s])])
        s0 = _both(lambda s: _rows_of(hs_ref[0, s], PW))
        sn = _both(lambda s: _rows_of(hn_ref[0, s], PW))
        dsn = _both(lambda s: _rows_of(dstate[s], PW))
        egl = _both(lambda s: _pairs(sc[s][0][L - 1:L, :]))
        dx, dyk, dvw, dgl, ds0 = _pairs_bwd(x, yk, m, tinv, uv, s0, sn, egl, dyw, dsn, keep)
        for s in range(NSEQ):
            mine = slice(s * NPAIR, (s + 1) * NPAIR)
            eg, eng, egp, at, rt, bt, kt = sc[s]
            dstate[s] = jnp.concatenate(ds0[mine], axis=0)
            dv_ref[s] = jnp.concatenate([_fold_pair(a) for a in dvw[mine]], axis=-1)
            dat = jnp.concatenate([_fold_pair(a[:2 * L]) for a in dx[mine]], axis=-1)
            drt = jnp.concatenate([_fold_pair(a[2 * L:]) for a in dx[mine]], axis=-1)
            dbt = jnp.concatenate([_fold_pair(a[:2 * L]) for a in dyk[mine]], axis=-1)
            dkt = jnp.concatenate([_fold_pair(a[2 * L:]) for a in dyk[mine]], axis=-1)
            dg = drt * rt - dbt * bt - dkt * kt
            dg = dg + jnp.where(_iota2(dg.shape, 0) == L - 1, jnp.concatenate(dgl[mine], axis=-1), 0.0)
            dgp = dat * at
            dlw_ref[s] = _hdot_tn(tri, dg + dgp) - dgp
            dr_ref[s] = drt * eg
            daa_ref[s] = dat * egp
            dbb_ref[s] = dbt * eng
            dk2_ref[s] = dkt * eng

    blk = pl.BlockSpec((NSEQ, L, W), lambda c: (0, NC - 1 - c, 0))
    per_chunk = lambda shape: pl.BlockSpec((1, NSEQ) + shape, lambda c: (NC - 1 - c, 0, 0, 0))
    outs = pl.pallas_call(
        body, grid=(NC,), name="rwkv_bwd",
        in_specs=[blk] * 5 + [per_chunk(a.shape[2:]) for a in saved] + [blk],
        out_specs=[blk] * 6,
        out_shape=[jax.ShapeDtypeStruct((NSEQ, SEQ, W), F32)] * 6,
        scratch_shapes=[pltpu.VMEM((NSEQ,) + STATE_SHAPE, F32)],
        compiler_params=_cparams(("arbitrary",)),
    )(*[_seq_view(a) for a in (r, lw, k2, aa, bb)], *saved, _seq_view(dy))
    return [a.reshape(T, W) for a in outs]


def _post_math(y, r, k2, v, ga, o, gb, lng, lnb, rk, bd):
    mu = _headsum(y, bd) * (1.0 / HD)
    yc = y - mu
    var = _headsum(yc * yc, bd) * (1.0 / HD)
    yn = yc * lax.rsqrt(var + GN_EPS) * lng + lnb
    bonus = _headsum(r * k2 * rk, bd) * v
    return (yn + bonus) * _silu(ga), o * _silu(gb)


def even_post(y, r, k2, v, ga, o, gb, lng, lnb, rk, h, w_bf):
    tm = 512

    def body(y_ref, r_ref, k2_ref, v_ref, ga_ref, o_ref, gb_ref, lng_ref, lnb_ref, rk_ref, h_ref, w_ref,
             ho_ref, zt_ref):
        ya, yb = _post_math(y_ref[...], r_ref[...], k2_ref[...], v_ref[...], ga_ref[...], o_ref[...], gb_ref[...],
                            lng_ref[...], lnb_ref[...], rk_ref[...], _head_blockdiag())
        z = jnp.concatenate([ya.astype(BF16), yb.astype(BF16)], axis=-1)
        zt_ref[...] = z.T
        ho_ref[...] = h_ref[...] + jnp.dot(z, w_ref[...], preferred_element_type=F32)

    vec = _const_spec((1, W))
    return pl.pallas_call(
        body, grid=(T // tm,), name="even_post",
        in_specs=[_row_spec(tm, W)] * 7 + [vec] * 3 + [_row_spec(tm, D), _const_spec((D, D))],
        out_specs=[_row_spec(tm, D), _col_spec(D, tm)],
        out_shape=[jax.ShapeDtypeStruct((T, D), F32), jax.ShapeDtypeStruct((D, T), BF16)],
        compiler_params=_cparams(("parallel",), VMEM_BIG),
    )(y, r, k2, v, ga, o, gb, lng, lnb, rk, h, w_bf)


def even_post_bwd(y, r, k2, v, ga, o, gb, lng, lnb, rk, dh, zt_bf, w_bf, after=None):
    tm = 512
    extra_specs, extra = _after_operand(after)

    def body(y_ref, r_ref, k2_ref, v_ref, ga_ref, o_ref, gb_ref, lng_ref, lnb_ref, rk_ref, dh_ref, zt_ref, w_ref,
             *rest):
        (dy_ref, dr_ref, dk2_ref, dv_ref, dga_ref, do_ref, dgb_ref, dlng_ref, dlnb_ref, drk_ref, dw_ref,
         acc_ref) = rest[-12:]
        dzv = _out_proj_back(dh_ref, zt_ref, w_ref, dw_ref, acc_ref)
        bd = _head_blockdiag()
        _, vjp = jax.vjp(lambda *a: _post_math(*a, bd), y_ref[...], r_ref[...], k2_ref[...], v_ref[...], ga_ref[...],
                         o_ref[...], gb_ref[...], lng_ref[...], lnb_ref[...], rk_ref[...])
        dy, dr, dk2, dv, dga, do, dgb, dlng, dlnb, drk = vjp((dzv[:, 0:W], dzv[:, W:2 * W]))
        for ref, val in ((dy_ref, dy), (dr_ref, dr), (dk2_ref, dk2), (dv_ref, dv), (dga_ref, dga), (do_ref, do),
                         (dgb_ref, dgb)):
            ref[...] = val.astype(ref.dtype)

        @pl.when(pl.program_id(0) == 0)
        def _():
            for ref in (dlng_ref, dlnb_ref, drk_ref):
                ref[...] = jnp.zeros_like(ref)

        dlng_ref[...] += dlng
        dlnb_ref[...] += dlnb
        drk_ref[...] += drk

    vec = _const_spec((1, W))
    return pl.pallas_call(
        body, grid=(T // tm,), name="even_post_bwd",
        in_specs=[_row_spec(tm, W)] * 7 + [vec] * 3 + [_row_spec(tm, D), _col_spec(D, tm), _const_spec((D, D))]
        + extra_specs,
        out_specs=[_row_spec(tm, W)] * 7 + [vec] * 3 + [_const_spec((D, D))],
        out_shape=[jax.ShapeDtypeStruct((T, W), dt) for dt in (F32, F32, F32, F32, BF16, F32, BF16)]
        + [jax.ShapeDtypeStruct((1, W), F32)] * 3 + [jax.ShapeDtypeStruct((D, D), BF16)],
        scratch_shapes=[pltpu.VMEM((D, D), F32)],
        compiler_params=_cparams(("arbitrary",), VMEM_BIG),
    )(y, r, k2, v, ga, o, gb, lng, lnb, rk, dh, zt_bf, w_bf, *extra)


PADSEQ = SEQ + LEFT * L
ATT_SCALE = 1.0 / math.sqrt(HD)
ATT_Q = 4
WIN = BAND + (ATT_Q - 1) * L
ATT_STEPS = NC // ATT_Q
ATT_BIAS_SHAPE = (NPAIR, ATT_Q * 2 * L, WIN)
ATT_WINDOW_BIAS_SHAPE = (ATT_Q, NPAIR, 2 * L, WIN)


def _stack_chunks(a):
    return jnp.concatenate([_stack_pair(a[i * L:(i + 1) * L]) for i in range(ATT_Q)], axis=0)


def _unstack_chunks(a):
    return jnp.concatenate([_unstack_pair(a[i * 2 * L:(i + 1) * 2 * L]) for i in range(ATT_Q)], axis=0)


def _window_bias(b_ref):
    return [jnp.concatenate([b_ref[c, p] for c in range(ATT_Q)], axis=0) for p in range(NPAIR)]


def _att_probs(q2, kw, bias, step):
    valid = _iota2((1, WIN), 1) >= (LEFT - step * ATT_Q) * L
    s = [jnp.where(valid, _bdot_nt(a, b) * ATT_SCALE + bias[p], NEG) for p, (a, b) in enumerate(zip(q2, kw))]
    e = [jnp.exp(a - jnp.max(a, axis=-1, keepdims=True)) for a in s]
    return [a / jnp.sum(a, axis=-1, keepdims=True) for a in e]


def attention_fwd(q, kpad, vpad, bias):
    def body(q_ref, k_ref, v_ref, b_ref, o_ref):
        step = pl.program_id(1)
        start = pl.multiple_of(step * (ATT_Q * L), L)
        kw = _pairs(k_ref[pl.ds(start, WIN), :])
        vw = _pairs(v_ref[pl.ds(start, WIN), :])
        q2 = [_stack_chunks(a) for a in _pairs(q_ref[...])]
        p = _att_probs(q2, kw, _window_bias(b_ref), step)
        o_ref[...] = jnp.concatenate([_unstack_chunks(_bdot(a, b)) for a, b in zip(p, vw)], axis=-1)

    qblk = pl.BlockSpec((ATT_Q * L, W), lambda b, c: (b * ATT_STEPS + c, 0))
    kblk = pl.BlockSpec((PADSEQ, W), lambda b, c: (b, 0))
    return pl.pallas_call(
        body, grid=(NSEQ, ATT_STEPS), name="attention_fwd",
        in_specs=[qblk, kblk, kblk, _const_spec(ATT_WINDOW_BIAS_SHAPE)],
        out_specs=qblk, out_shape=jax.ShapeDtypeStruct((T, W), F32),
        compiler_params=_cparams(("parallel", "arbitrary")),
    )(q, kpad, vpad, bias)


def attention_bwd(q, kpad, vpad, bias, do):
    def body(q_ref, k_ref, v_ref, b_ref, do_ref, dq_ref, dko_ref, dvo_ref, db_ref, dk_ref, dv_ref):
        b = pl.program_id(0)
        c = pl.program_id(1)

        @pl.when(c == 0)
        def _():
            dk_ref[...] = jnp.zeros_like(dk_ref)
            dv_ref[...] = jnp.zeros_like(dv_ref)

        @pl.when((c == 0) & (b == 0))
        def _():
            db_ref[...] = jnp.zeros_like(db_ref)

        start = pl.multiple_of(c * (ATT_Q * L), L)
        kw = _pairs(k_ref[pl.ds(start, WIN), :])
        vw = _pairs(v_ref[pl.ds(start, WIN), :])
        q2 = [_stack_chunks(a) for a in _pairs(q_ref[...])]
        do2 = [_stack_chunks(a) for a in _pairs(do_ref[...].astype(BF16))]
        p = _att_probs(q2, kw, _window_bias(b_ref), c)
        dp = [_bdot_nt(a, b) for a, b in zip(do2, vw)]
        ds = [a * (d - jnp.sum(d * a, axis=-1, keepdims=True)) for a, d in zip(p, dp)]
        dss = [(a * ATT_SCALE).astype(BF16) for a in ds]
        dq_ref[...] = jnp.concatenate([_unstack_chunks(_bdot(a, b)) for a, b in zip(dss, kw)], axis=-1).astype(BF16)
        dk_ref[pl.ds(start, WIN), :] += jnp.concatenate([_bdot_tn(a, b) for a, b in zip(dss, q2)], axis=-1)
        dv_ref[pl.ds(start, WIN), :] += jnp.concatenate([_bdot_tn(a, b) for a, b in zip(p, do2)], axis=-1)
        for i in range(NPAIR):
            db_ref[i] += ds[i]

        @pl.when(c == ATT_STEPS - 1)
        def _():
            dko_ref[...] = dk_ref[LEFT * L:, :].astype(BF16)
            dvo_ref[...] = dv_ref[LEFT * L:, :].astype(BF16)

    qblk = pl.BlockSpec((ATT_Q * L, W), lambda b, c: (b * ATT_STEPS + c, 0))
    kblk = pl.BlockSpec((PADSEQ, W), lambda b, c: (b, 0))
    sblk = pl.BlockSpec((SEQ, W), lambda b, c: (b, 0))
    bblk = _const_spec(ATT_BIAS_SHAPE)
    return pl.pallas_call(
        body, grid=(NSEQ, ATT_STEPS), name="attention_bwd",
        in_specs=[qblk, kblk, kblk, _const_spec(ATT_WINDOW_BIAS_SHAPE), qblk],
        out_specs=[qblk, sblk, sblk, bblk],
        out_shape=[jax.ShapeDtypeStruct((T, W), BF16), jax.ShapeDtypeStruct((T, W), BF16),
                   jax.ShapeDtypeStruct((T, W), BF16), jax.ShapeDtypeStruct(ATT_BIAS_SHAPE, F32)],
        scratch_shapes=[pltpu.VMEM((PADSEQ, W), F32), pltpu.VMEM((PADSEQ, W), F32)],
        compiler_params=_cparams(("arbitrary", "arbitrary"), VMEM_BIG),
    )(q, kpad, vpad, bias, do)


NTAB = 2 * CLIP + 1
EXT = BAND + L


def _ext_onehot():
    n = _iota2((EXT, NTAB), 0)
    m = _iota2((EXT, NTAB), 1)
    return (jnp.clip(BAND - 1 - n, -CLIP, CLIP) + CLIP == m).astype(F32)


def bias_expand(table):
    def body(t_ref, o_ref):
        ext = _hdot_nt(t_ref[...], _ext_onehot())
        ext = jnp.concatenate([ext, jnp.zeros((NH, WIN - EXT), F32)], axis=-1)
        col = _iota2((NH, WIN), 1)
        for c in range(ATT_Q):
            inside = (col >= c * L) & (col < c * L + BAND)
            for i in range(L):
                shift = (c * L - (L - 1 - i)) % WIN
                o_ref[c, :, i, :] = jnp.where(inside, pltpu.roll(ext, shift, 1) if shift else ext, NEG)

    out = pl.pallas_call(body, name="bias_expand", out_shape=jax.ShapeDtypeStruct((ATT_Q, NH, L, WIN), F32))(table)
    return out.reshape(ATT_WINDOW_BIAS_SHAPE)


def bias_grad(dbias):
    def body(d_ref, o_ref):
        acc = jnp.zeros((NH, EXT), F32)
        zpad = jnp.zeros((NH, EXT - BAND), F32)
        for i in range(L):
            s = L - 1 - i
            row = jnp.concatenate([d_ref[:, i, :], zpad], axis=-1)
            acc = acc + (pltpu.roll(row, s, 1) if s else row)
        o_ref[...] = _hdot(acc, _ext_onehot())

    return pl.pallas_call(body, name="bias_grad", out_shape=jax.ShapeDtypeStruct((NH, NTAB), F32))(dbias)


def _group_cols(g):
    return slice(g * SGC, (g + 1) * SGC)


def _sg_norm(gv, lng, lnb):
    gc = gv - jnp.mean(gv, axis=-1, keepdims=True)
    rstd = lax.rsqrt(jnp.mean(gc * gc, axis=-1, keepdims=True) + LN_EPS)
    xhat = gc * rstd
    return xhat, rstd, xhat * lng + lnb


GMLP_BWD_CHUNKS = 2


def gmlp_fwd_loss(u, v, gate, lng, lnb, wm_bf, sgb_t, h, w_bf, g_final, target):
    tm = GMLP_BWD_CHUNKS * SGC

    def body(u_ref, v_ref, gt_ref, lng_ref, lnb_ref, wm_ref, sb_ref, h_ref, w_ref, g_ref, t_ref,
             dh_ref, loss_ref, dg_ref, zt_ref):
        zs = []
        for ch in range(GMLP_BWD_CHUNKS):
            rows = slice(ch * SGC, (ch + 1) * SGC)
            _, _, vln = _sg_norm(_gelu(v_ref[rows, :]), lng_ref[...], lnb_ref[...])
            vlb = vln.astype(BF16)
            zg = []
            for g in range(NG):
                cs = _group_cols(g)
                sv = jnp.dot(wm_ref[g], vlb[:, cs], preferred_element_type=F32) + sb_ref[:, g:g + 1]
                zg.append((_gelu(u_ref[rows, cs]) * sv * _silu(gt_ref[rows, cs])).astype(BF16))
            zs.append(jnp.concatenate(zg, axis=-1))
        z = jnp.concatenate(zs, axis=0)
        zt_ref[...] = z.T
        xv = h_ref[...] + jnp.dot(z, w_ref[...], preferred_element_type=F32)
        rstd = lax.rsqrt(jnp.mean(xv * xv, axis=-1, keepdims=True) + RMS_EPS)
        xhat = xv * rstd
        err = xhat * g_ref[...] - t_ref[...]
        part = 0.5 * jnp.sum(jnp.mean(err * err, axis=-1, keepdims=True), axis=0, keepdims=True)
        dout = err * (1.0 / D)

        @pl.when(pl.program_id(0) == 0)
        def _():
            loss_ref[...] = jnp.zeros_like(loss_ref)
            dg_ref[...] = jnp.zeros_like(dg_ref)

        loss_ref[...] += jnp.broadcast_to(part, loss_ref.shape)
        dg_ref[...] += jnp.sum(dout * xhat, axis=0, keepdims=True)
        dxh = dout * g_ref[...]
        dh_ref[...] = rstd * (dxh - xhat * jnp.mean(dxh * xhat, axis=-1, keepdims=True))

    return pl.pallas_call(
        body, grid=(T // tm,), name="gmlp_fwd_loss",
        in_specs=[_row_spec(tm, D)] * 3 + [_const_spec((1, D))] * 2
        + [_const_spec((NG, SGC, SGC)), _const_spec((SGC, NG)), _row_spec(tm, D), _const_spec((D, D)),
           _const_spec((1, D)), _row_spec(tm, D)],
        out_specs=[_row_spec(tm, D), _const_spec((8, 128)), _const_spec((1, D)), _col_spec(D, tm)],
        out_shape=[jax.ShapeDtypeStruct((T, D), F32), jax.ShapeDtypeStruct((8, 128), F32),
                   jax.ShapeDtypeStruct((1, D), F32), jax.ShapeDtypeStruct((D, T), BF16)],
        compiler_params=_cparams(("arbitrary",), VMEM_BIG),
    )(u, v, gate, lng, lnb, wm_bf, sgb_t, h, w_bf, g_final, target)


def gmlp_bwd(u, v, gate, lng, lnb, wm_bf, sgb_t, dh, zt_bf, w_bf):
    def body(u_ref, v_ref, gt_ref, lng_ref, lnb_ref, wm_ref, sb_ref, dh_ref, zt_ref, w_ref,
             du_ref, dv_ref, dgt_ref, dlng_ref, dlnb_ref, dwm_ref, dsb_ref, dw_ref, acc_ref):
        @pl.when(pl.program_id(0) == 0)
        def _():
            for ref in (dlng_ref, dlnb_ref, dwm_ref, dsb_ref):
                ref[...] = jnp.zeros_like(ref)

        dz = _out_proj_back(dh_ref, zt_ref, w_ref, dw_ref, acc_ref)
        sel = (_iota2((D, NG), 0) // SGC == _iota2((D, NG), 1)).astype(F32)
        for ch in range(GMLP_BWD_CHUNKS):
            rows = slice(ch * SGC, (ch + 1) * SGC)
            gv, dgv_dv = _gelu_both(v_ref[rows, :])
            xhat, rstd, vln = _sg_norm(gv, lng_ref[...], lnb_ref[...])
            vlb = vln.astype(BF16)
            dvln = []
            dsv_all = []
            for g in range(NG):
                cs = _group_cols(g)
                uu = u_ref[rows, cs]
                gg = gt_ref[rows, cs]
                dzz = dz[rows, cs]
                sv = jnp.dot(wm_ref[g], vlb[:, cs], preferred_element_type=F32) + sb_ref[:, g:g + 1]
                gu, dgu = _gelu_both(uu)
                sg, dsg = _silu_both(gg)
                dzgu = dzz * gu
                dsv = dzgu * sg
                dgt_ref[rows, cs] = (dzgu * sv * dsg).astype(BF16)
                du_ref[rows, cs] = (dzz * sv * sg * dgu).astype(BF16)
                dsb16 = dsv.astype(BF16)
                dvln.append(lax.dot_general(wm_ref[g], dsb16, (((0,), (0,)), ((), ())), preferred_element_type=F32))
                dwm_ref[g] += lax.dot_general(dsb16, vlb[:, cs], (((1,), (1,)), ((), ())),
                                              preferred_element_type=F32)
                dsv_all.append(dsv)
            dvl = jnp.concatenate(dvln, axis=-1)
            dsb_ref[...] += _hdot(jnp.concatenate(dsv_all, axis=-1), sel)
            dlng_ref[...] += jnp.sum(dvl * xhat, axis=0, keepdims=True)
            dlnb_ref[...] += jnp.sum(dvl, axis=0, keepdims=True)
            dxh = dvl * lng_ref[...]
            dgv = rstd * (dxh - jnp.mean(dxh, axis=-1, keepdims=True)
                          - xhat * jnp.mean(dxh * xhat, axis=-1, keepdims=True))
            dv_ref[rows, :] = (dgv * dgv_dv).astype(BF16)

    tm = GMLP_BWD_CHUNKS * SGC
    return pl.pallas_call(
        body, grid=(T // tm,), name="gmlp_bwd",
        in_specs=[_row_spec(tm, D)] * 3 + [_const_spec((1, D))] * 2
        + [_const_spec((NG, SGC, SGC)), _const_spec((SGC, NG)), _row_spec(tm, D), _col_spec(D, tm),
           _const_spec((D, D))],
        out_specs=[_row_spec(tm, D)] * 3 + [_const_spec((1, D))] * 2
        + [_const_spec((NG, SGC, SGC)), _const_spec((SGC, NG)), _const_spec((D, D))],
        out_shape=[jax.ShapeDtypeStruct((T, D), BF16)] * 3 + [jax.ShapeDtypeStruct((1, D), F32)] * 2
        + [jax.ShapeDtypeStruct((NG, SGC, SGC), F32), jax.ShapeDtypeStruct((SGC, NG), F32),
           jax.ShapeDtypeStruct((D, D), BF16)],
        scratch_shapes=[pltpu.VMEM((D, D), F32)],
        compiler_params=_cparams(("arbitrary",), VMEM_BIG),
    )(u, v, gate, lng, lnb, wm_bf, sgb_t, dh, zt_bf, w_bf)


NCHIP = 4
NDEV = 8
ANY = pl.BlockSpec(memory_space=pl.ANY)


HBM = pl.BlockSpec(memory_space=pltpu.HBM)
SEM = pl.BlockSpec(memory_space=pltpu.SEMAPHORE)
EFFECT = pltpu.SideEffectType.DATAFLOW_SIDE_EFFECTING


CHIPS, EVERY, SIBLING = "chips", "every", "sibling"
SLOTS = {CHIPS: NCHIP, EVERY: NDEV, SIBLING: 1}


def _peers(scope):
    x, y, c = lax.axis_index("x"), lax.axis_index("y"), lax.axis_index("c")
    if scope == SIBLING:
        return [((x, y, 1 - c), 0)], 0
    if scope == CHIPS:
        return [((px, py, c), 2 * px + py) for px, py in ((1 - x, y), (x, 1 - y), (1 - x, 1 - y))], 2 * x + y
    out = []
    for j in range(1, NDEV):
        px, py, pc = x ^ (j >> 2), y ^ ((j >> 1) & 1), c ^ (j & 1)
        out.append(((px, py, pc), 4 * px + 2 * py + pc))
    return out, 4 * x + 2 * y + c


def _send_copies(src, land, send, recv, scatter, scope, starting):
    peers, me = _peers(scope)
    copies = []
    for t in range(len(src)):
        for j, (dev, slot) in enumerate(peers):
            k = t * len(peers) + j
            copies.append(pltpu.make_async_remote_copy(
                src_ref=src[t].at[slot] if scatter else src[t], dst_ref=land[t].at[me if starting else slot],
                send_sem=send.at[k], recv_sem=recv.at[k], device_id=dev, device_id_type=MESH))
    return copies


def _own_copies(src, land, sems, scatter, scope):
    if scope == SIBLING:
        return []
    _, me = _peers(scope)
    return [pltpu.make_async_copy(src[t].at[me] if scatter else src[t], land[t].at[me], sems.at[t])
            for t in range(len(src))]


def send_start(srcs, scatter, scope, name, after=None):
    n = len(srcs)
    slots = SLOTS[scope]
    extra_specs, extra = _after_operand(after)
    lands = [pltpu.HBM(a.shape if scatter else (slots,) + a.shape, a.dtype) for a in srcs]
    sems = [pltpu.SemaphoreType.DMA((n * max(slots - 1, 1),))] * 2 + ([] if scope == SIBLING else
                                                                     [pltpu.SemaphoreType.DMA((n,))])
    k = len(sems)

    def body(*refs):
        first_out = n + len(extra)
        src, land = refs[:n], refs[first_out + k + n:first_out + k + 2 * n]
        for cp in _send_copies(src, land, refs[first_out], refs[first_out + 1], scatter, scope, True):
            cp.start()
        for cp in _own_copies(src, land, refs[first_out + k - 1], scatter, scope):
            cp.start()
        refs[-1][...] = jnp.zeros_like(refs[-1])

    out = pl.pallas_call(
        body, name=name,
        out_shape=(*sems, *[pltpu.HBM(a.shape, a.dtype) for a in srcs], *lands, jax.ShapeDtypeStruct((8, 128), F32)),
        in_specs=[HBM] * n + extra_specs,
        out_specs=(*[SEM] * k, *[HBM] * (2 * n), pl.BlockSpec(memory_space=pltpu.VMEM)),
        input_output_aliases={i: k + i for i in range(n)},
        compiler_params=pltpu.CompilerParams(has_side_effects=EFFECT),
    )(*[pltpu.with_memory_space_constraint(a, pltpu.HBM) for a in srcs], *extra)
    return list(out[:k]), list(out[k:k + n]), list(out[k + n:k + 2 * n]), out[-1]


def send_wait(started, after, scatter, scope, name, with_sources=False):
    sems, srcs, lands, _ = started
    n, k = len(srcs), len(sems)

    def body(*refs):
        src, land = refs[:n], refs[n:2 * n]
        for cp in _own_copies(src, land, refs[2 * n + k - 1], scatter, scope):
            cp.wait()
        for cp in _send_copies(src, land, refs[2 * n], refs[2 * n + 1], scatter, scope, False):
            cp.wait_send()
            cp.wait_recv()

    arrs = list(srcs) + list(lands)
    out = pl.pallas_call(
        body, name=name, out_shape=tuple(pltpu.HBM(a.shape, a.dtype) for a in arrs),
        in_specs=[HBM] * (2 * n) + [SEM] * k + [ANY], out_specs=tuple([HBM] * (2 * n)),
        input_output_aliases={i: i for i in range(2 * n)},
        compiler_params=pltpu.CompilerParams(has_side_effects=EFFECT),
    )(*arrs, *sems, after)
    return (list(out[:n]), list(out[n:])) if with_sources else list(out[n:])


def gather_weights(arrs, split):
    n = len(arrs)

    def body(*refs):
        ins, outs = refs[:n], refs[n:2 * n]
        send1, recv1, send2, recv2, loc_in, loc_out = refs[2 * n:2 * n + 6]
        staged = refs[2 * n + 6:]
        x, y, c = lax.axis_index("x"), lax.axis_index("y"), lax.axis_index("c")
        me = 2 * x + y
        sibling = (x, y, 1 - c)
        peers = [(1 - x, y), (x, 1 - y), (1 - x, 1 - y)]

        def rows_of(t, core):
            half = arrs[t].shape[0] // 2
            return pl.ds(core * half, half)

        def part(ref, t, core):
            return ref.at[rows_of(t, core)] if split[t] else ref

        load = [pltpu.make_async_copy(ins[t], staged[t], loc_in.at[t]) for t in range(n)]
        store = [pltpu.make_async_copy(staged[t], outs[t].at[me], loc_out.at[t]) for t in range(n)]
        for cp in load:
            cp.start()
        first = []
        for t in range(n):
            for j, (px, py) in enumerate(peers):
                first.append(pltpu.make_async_remote_copy(
                    src_ref=part(ins[t], t, c), dst_ref=part(outs[t].at[me], t, c), send_sem=send1.at[t, j],
                    recv_sem=recv1.at[t, j], device_id=(px, py, c), device_id_type=MESH))
        for cp in first:
            cp.start()
        for cp_in, cp_out in zip(load, store):
            cp_in.wait()
            cp_out.start()
        passed = []
        for t in range(n):
            for j, (px, py) in enumerate(peers):
                landed = part(outs[t].at[2 * px + py], t, c)
                pltpu.make_async_remote_copy(
                    src_ref=landed, dst_ref=landed, send_sem=send1.at[t, j], recv_sem=recv1.at[t, j],
                    device_id=(x, y, c), device_id_type=MESH).wait_recv()
                if split[t]:
                    cp = pltpu.make_async_remote_copy(
                        src_ref=landed, dst_ref=landed, send_sem=send2.at[t, j], recv_sem=recv2.at[t, j],
                        device_id=sibling, device_id_type=MESH)
                    cp.start()
                    passed.append(cp)
        for t in range(n):
            for j, (px, py) in enumerate(peers):
                if split[t]:
                    other = part(outs[t].at[2 * px + py], t, 1 - c)
                    pltpu.make_async_remote_copy(
                        src_ref=other, dst_ref=other, send_sem=send2.at[t, j], recv_sem=recv2.at[t, j],
                        device_id=(x, y, c), device_id_type=MESH).wait_recv()
        for cp in first + passed:
            cp.wait_send()
        for cp in store:
            cp.wait()

    return pl.pallas_call(
        body, name="gather_weights", in_specs=[ANY] * n, out_specs=[ANY] * n,
        out_shape=[jax.ShapeDtypeStruct((NCHIP,) + a.shape, a.dtype) for a in arrs],
        scratch_shapes=[pltpu.SemaphoreType.DMA((n, 3))] * 4 + [pltpu.SemaphoreType.DMA((n,))] * 2
        + [pltpu.VMEM(a.shape, a.dtype) for a in arrs],
    )(*arrs)


def _adam_math(g, w, m, v):
    m = ADAM_B1 * m + (1.0 - ADAM_B1) * g
    v = ADAM_B2 * v + (1.0 - ADAM_B2) * (g * g)
    m_hat = m / (1.0 - ADAM_B1 ** ADAM_STEP)
    v_hat = v / (1.0 - ADAM_B2 ** ADAM_STEP)
    delta = -ADAM_LR * (m_hat / (jnp.sqrt(v_hat) + ADAM_EPS) + ADAM_WD * w)
    return delta, m, v


def _rows_tile(rows):
    return rows if rows <= 256 else 256


def sum_chips(parts, name):
    _, rows, cols = parts.shape
    tr = _rows_tile(rows)

    def body(p_ref, o_ref):
        acc = p_ref[0].astype(F32)
        for s in range(1, NCHIP):
            acc = acc + p_ref[s].astype(F32)
        o_ref[...] = acc

    return pl.pallas_call(
        body, grid=(rows // tr,), name=name,
        in_specs=[pl.BlockSpec((NCHIP, tr, cols), lambda i: (0, i, 0))],
        out_specs=pl.BlockSpec((tr, cols), lambda i: (i, 0)),
        out_shape=jax.ShapeDtypeStruct((rows, cols), F32),
        compiler_params=_cparams(("parallel",)),
    )(parts)


def sum_chips_small(parts, name):
    n = len(parts)

    def body(*refs):
        for p_ref, o_ref in zip(refs[:n], refs[n:]):
            acc = p_ref[0]
            for s in range(1, NCHIP):
                acc = acc + p_ref[s]
            o_ref[...] = acc

    return pl.pallas_call(body, name=name, out_shape=[jax.ShapeDtypeStruct(p.shape[1:], F32) for p in parts])(*parts)


def adam_shard_small(items, name):
    n = len(items)

    def body(*refs):
        for t in range(n):
            a_ref, b_ref, w_ref, m_ref, v_ref = refs[5 * t:5 * t + 5]
            g_ref, d_ref, mo_ref, vo_ref = refs[5 * n + 4 * t:5 * n + 4 * t + 4]
            g = (a_ref[...] + b_ref[...]).reshape(w_ref.shape)
            g_ref[...] = g
            d_ref[...], mo_ref[...], vo_ref[...] = _adam_math(g, w_ref[...], m_ref[...], v_ref[...])

    out = pl.pallas_call(
        body, name=name, out_shape=[jax.ShapeDtypeStruct(it[2].shape, F32) for it in items for _ in range(4)],
    )(*[a for it in items for a in it])
    return [out[4 * t:4 * t + 4] for t in range(n)]


def adam_shard(p_mine, p_sib, w, m, v, name):
    rows, cols = p_mine.shape
    tr = _rows_tile(rows)
    lead = w.ndim == 3

    def body(a_ref, b_ref, w_ref, m_ref, v_ref, g_ref, d_ref, mo_ref, vo_ref):
        g = a_ref[...] + b_ref[...]
        g = g[None] if lead else g
        g_ref[...] = g
        d_ref[...], mo_ref[...], vo_ref[...] = _adam_math(g, w_ref[...], m_ref[...], v_ref[...])

    flat = pl.BlockSpec((tr, cols), lambda i: (i, 0))
    spec = pl.BlockSpec((1, tr, cols), lambda i: (0, i, 0)) if lead else flat
    return pl.pallas_call(
        body, grid=(rows // tr,), name=name, in_specs=[flat] * 2 + [spec] * 3, out_specs=[spec] * 4,
        out_shape=[jax.ShapeDtypeStruct(w.shape, F32)] * 4,
        compiler_params=_cparams(("parallel",)),
    )(p_mine, p_sib, w, m, v)


def adam_shard_halves_t(r_mine, r_sib, wt, mt, vt, name):
    hrows, cols = r_mine.shape
    tr = _rows_tile(hrows)
    per_half = hrows // tr

    def body(a_ref, b_ref, w_ref, m_ref, v_ref, g_ref, d_ref, mo_ref, vo_ref):
        mine = pl.program_id(0) == lax.axis_index("c")
        g = jnp.where(mine, a_ref[...], b_ref[...]).T[None]
        g_ref[...] = g
        d_ref[...], mo_ref[...], vo_ref[...] = _adam_math(g, w_ref[...], m_ref[...], v_ref[...])

    flat = pl.BlockSpec((tr, cols), lambda h, i: (i, 0))
    spec = pl.BlockSpec((1, cols, tr), lambda h, i: (0, 0, h * per_half + i))
    return pl.pallas_call(
        body, grid=(2, per_half), name=name, in_specs=[flat] * 2 + [spec] * 3, out_specs=[spec] * 4,
        out_shape=[jax.ShapeDtypeStruct(wt.shape, F32)] * 4,
        compiler_params=_cparams(("parallel", "parallel")),
    )(r_mine, r_sib, wt, mt, vt)


def adam_replicated(gathered, params, name):
    flat = []
    for i, p in enumerate(params):
        if isinstance(p, list):
            off = 0
            for wmv in p:
                n = gathered[i].shape[-1] - off if wmv[0] is None else wmv[0].shape[-1]
                flat.append((i, (off, n), wmv))
                off += n
        else:
            flat.append((i, None, p))
    ins = [a for _, _, wmv in flat for a in wmv if a is not None]
    ng = len(gathered)

    def body(*refs):
        g_refs = refs[:ng]
        in_refs = list(refs[ng:ng + len(ins)])
        out_refs = list(refs[ng + len(ins):])
        sums = []
        for r in g_refs:
            g = r[0]
            for d in range(1, NDEV):
                g = g + r[d]
            sums.append(g)
        for i, lanes, wmv in flat:
            g = sums[i] if lanes is None else sums[i][:, lanes[0]:lanes[0] + lanes[1]]
            out_refs.pop(0)[...] = g
            if wmv[0] is not None:
                w_ref, m_ref, v_ref = in_refs.pop(0), in_refs.pop(0), in_refs.pop(0)
                d_ref, mo_ref, vo_ref = out_refs.pop(0), out_refs.pop(0), out_refs.pop(0)
                d_ref[...], mo_ref[...], vo_ref[...] = _adam_math(g, w_ref[...], m_ref[...], v_ref[...])

    out_shape = []
    for i, lanes, wmv in flat:
        shape = gathered[i].shape[1:] if lanes is None else (1, lanes[1])
        out_shape += [jax.ShapeDtypeStruct(shape, F32)] * (4 if wmv[0] is not None else 1)
    outs = list(pl.pallas_call(body, name=name, out_shape=out_shape)(*gathered, *ins))
    return [[outs.pop(0) for _ in range(4 if wmv[0] is not None else 1)] for _, _, wmv in flat]


EVEN_SPLITS = (SHIFT, W, W, W, W, W)
ODD_SPLITS = (D, D, D)


def _cols_to_chips(a):
    rows, cols = a.shape
    return a.reshape(rows, NCHIP, cols // NCHIP).transpose(1, 0, 2)


def _chips_to_cols(a):
    _, rows, n = a.shape
    return a.transpose(1, 0, 2).reshape(rows, NCHIP * n)


def kernel(x, norm_g, w_in_e, shift_mu, rw_w0, rw_w2, rw_a0, rw_a2, rw_kk, rw_ka, rw_rk, rw_lnx_g, rw_lnx_b, att_bias, w_out_e, w_in_o, sg_ln_g, sg_ln_b, sg_w, sg_b, w_out_o, final_g, loss_target, m_norm_g, m_w_in_e, m_shift_mu, m_rw_w0, m_rw_w2, m_rw_a0, m_rw_a2, m_rw_kk, m_rw_ka, m_rw_rk, m_rw_lnx_g, m_rw_lnx_b, m_att_bias, m_w_out_e, m_w_in_o, m_sg_ln_g, m_sg_ln_b, m_sg_w, m_sg_b, m_w_out_o, m_final_g, v_norm_g, v_w_in_e, v_shift_mu, v_rw_w0, v_rw_w2, v_rw_a0, v_rw_a2, v_rw_kk, v_rw_ka, v_rw_rk, v_rw_lnx_g, v_rw_lnx_b, v_att_bias, v_w_out_e, v_w_in_o, v_sg_ln_g, v_sg_ln_b, v_sg_w, v_sg_b, v_w_out_o, v_final_g):
    x2 = x.reshape(T, D)
    tgt = loss_target.reshape(T, D)

    gathered = gather_weights(
        [jnp.swapaxes(w_in_e[0], 0, 1).astype(BF16), jnp.concatenate([rw_w2[0], rw_a2[0]], axis=0),
         jnp.concatenate([sg_ln_g, sg_ln_b], axis=0)], [True, True, False])
    wie = gathered[0].reshape(EVEN_IN, D)
    w2 = _chips_to_cols(gathered[1][:, :LORA])
    a2 = _chips_to_cols(gathered[1][:, LORA:])
    sglg = _chips_to_cols(gathered[2][:, 0:1])
    sglb = _chips_to_cols(gathered[2][:, 1:2])

    late = [w_out_e[0].astype(BF16), w_in_o[0].astype(BF16), w_out_o[0].astype(BF16)]
    late_started = send_start(late, False, CHIPS, "late_weights_start", after=gathered[0])

    def late_weights(after):
        woe, wio, woo = send_wait(late_started, after, False, CHIPS, "late_weights_wait")
        return woe.reshape(D, D), _chips_to_cols(wio), woo.reshape(D, D)

    def scatter_start(grads, name):
        return send_start([g_.astype(BF16) if g_.shape[-1] >= W else g_ for g_ in grads], True, CHIPS, name)

    started = {}

    def on_odd_grads(d_woo, d_wio):
        started["odd"] = scatter_start([d_woo.reshape(NCHIP, D // NCHIP, D), d_wio], "odd_grads_start")
        return started["odd"][-1]

    def on_even_grads(big_g):
        d_wie_half, d_woe, _, _, d_w2, d_a2, d_sglg, d_sglb = big_g
        blocks = [d_wie_half, d_woe.reshape(NCHIP, D // NCHIP, D), _cols_to_chips(d_w2), _cols_to_chips(d_a2),
                  _cols_to_chips(d_sglg), _cols_to_chips(d_sglb)]
        started["even"] = scatter_start(blocks, "even_grads_start")
        return started["even"][-1]

    def on_small_grads(layer, grads):
        if layer == "odd":
            d_sg_w, d_sg_b, d_final, d_g1 = grads
            mine = [d_sg_w.reshape(NG * SGC, SGC), d_sg_b, jnp.concatenate([d_final, d_g1], axis=1)]
        else:
            mine = [grads[-2], jnp.concatenate(grads[:-2] + grads[-1:], axis=1)]
        started[layer + "_small"] = send_start(mine, False, EVERY, layer + "_small_grads_start")
        return started[layer + "_small"][-1]

    loss_part, dx, _, _ = _local_step(
        x2, tgt, wie, late_weights, w2, a2, sglg, sglb, norm_g, shift_mu, rw_w0, rw_a0, rw_kk, rw_ka, rw_rk,
        rw_lnx_g, rw_lnx_b, att_bias, sg_w, sg_b, final_g, first_after=late_started[-1], on_odd_grads=on_odd_grads,
        on_even_grads=on_even_grads, on_small_grads=on_small_grads)
    wmv = {"w_in_e": tuple(jnp.swapaxes(a, 1, 2) for a in (w_in_e, m_w_in_e, v_w_in_e)),
           "w_out_e": (w_out_e, m_w_out_e, v_w_out_e),
           "w_in_o": (w_in_o, m_w_in_o, v_w_in_o), "w_out_o": (w_out_o, m_w_out_o, v_w_out_o),
           "rw_w2": (rw_w2, m_rw_w2, v_rw_w2), "rw_a2": (rw_a2, m_rw_a2, v_rw_a2),
           "sg_ln_g": (sg_ln_g, m_sg_ln_g, v_sg_ln_g), "sg_ln_b": (sg_ln_b, m_sg_ln_b, v_sg_ln_b)}
    sharded = {}

    def sum_and_swap(names, landed, tag):
        nbig = sum(p_.dtype == BF16 for p_ in landed)
        partial = [sum_chips(p_, "sum_" + nm) for p_, nm in zip(landed[:nbig], names)]
        if nbig < len(names):
            partial += sum_chips_small(landed[nbig:], "sum_small_" + tag)
        return send_start(partial, False, SIBLING, "swap_partials_" + tag + "_start")

    def update(names, swap_started, after, tag):
        partial, landed = send_wait(swap_started, after, False, SIBLING, "swap_partials_" + tag + "_wait", True)
        from_sibling = [a[0] for a in landed]
        nbig = sum(p_.shape[-1] >= W for p_ in partial)
        for nm, mine, sib in zip(names[:nbig], partial, from_sibling):
            if nm == "w_in_e":
                res = adam_shard_halves_t(mine, sib, *wmv[nm], "adam_" + nm)
                sharded[nm] = [jnp.swapaxes(a, 1, 2) for a in res]
            else:
                sharded[nm] = adam_shard(mine, sib, *wmv[nm], "adam_" + nm)
        if nbig < len(names):
            items = [(mine, sib, *wmv[nm]) for nm, mine, sib in zip(names, partial, from_sibling)][nbig:]
            for nm, res in zip(names[nbig:], adam_shard_small(items, "adam_small_" + tag)):
                sharded[nm] = res

    odd_names = ["w_out_o", "w_in_o"]
    even_names = ["w_in_e", "w_out_e", "rw_w2", "rw_a2", "sg_ln_g", "sg_ln_b"]
    odd_landed = send_wait(started["odd"], started["even_small"][-1], True, CHIPS, "odd_grads_wait")
    odd_swap = sum_and_swap(odd_names, odd_landed, "odd")
    done = odd_swap[-1]

    def wmv_of(*arrs, view=lambda a: a):
        return tuple(view(a) for a in arrs)

    vec = lambda a: a.reshape(1, -1)
    groups = {
        "odd": (["sg_w", "sg_b", "final_g", "norm_g1"],
                [wmv_of(sg_w, m_sg_w, v_sg_w, view=lambda a: a.reshape(NG * SGC, SGC)),
                 wmv_of(sg_b, m_sg_b, v_sg_b, view=lambda a: a[0]),
                 [wmv_of(final_g, m_final_g, v_final_g, view=vec),
                  wmv_of(norm_g, m_norm_g, v_norm_g, view=lambda a: a[1:2])]]),
        "even": (["att_bias", "norm_g0", "shift_mu", "rw_w0", "rw_a0", "rw_kk", "rw_ka", "rw_rk", "rw_lnx_g",
                  "rw_lnx_b", "loss"],
                 [wmv_of(att_bias, m_att_bias, v_att_bias, view=lambda a: a[0]),
                  [wmv_of(norm_g, m_norm_g, v_norm_g, view=lambda a: a[0:1]),
                   wmv_of(shift_mu, m_shift_mu, v_shift_mu), wmv_of(rw_w0, m_rw_w0, v_rw_w0),
                   wmv_of(rw_a0, m_rw_a0, v_rw_a0), wmv_of(rw_kk, m_rw_kk, v_rw_kk), wmv_of(rw_ka, m_rw_ka, v_rw_ka),
                   wmv_of(rw_rk, m_rw_rk, v_rw_rk, view=vec), wmv_of(rw_lnx_g, m_rw_lnx_g, v_rw_lnx_g),
                   wmv_of(rw_lnx_b, m_rw_lnx_b, v_rw_lnx_b), (None, None, None)]]),
    }
    rep = {}
    for layer in ("odd", "even"):
        nms, params = groups[layer]
        gathered_g = send_wait(started[layer + "_small"], done, False, EVERY, layer + "_small_grads_wait")
        for nm, res in zip(nms, adam_replicated(gathered_g, params, "adam_" + layer + "_small")):
            rep[nm] = res
        done = rep[nms[0]][0]
    native = {"sg_w": sg_w.shape, "sg_b": sg_b.shape, "final_g": final_g.shape, "rw_rk": rw_rk.shape,
              "att_bias": att_bias.shape}
    for nm, shape in native.items():
        rep[nm] = [a.reshape(shape) for a in rep[nm]]
    rep["norm_g"] = [jnp.concatenate([a, b], axis=0) for a, b in zip(rep["norm_g0"], rep["norm_g1"])]
    even_landed = send_wait(started["even"], done, True, CHIPS, "even_grads_wait")
    even_swap = sum_and_swap(even_names, even_landed, "even")
    update(odd_names, odd_swap, even_swap[-1], "odd")
    update(even_names, even_swap, sharded["w_in_o"][0], "even")

    order = ["norm_g", "w_in_e", "shift_mu", "rw_w0", "rw_w2", "rw_a0", "rw_a2", "rw_kk", "rw_ka", "rw_rk",
             "rw_lnx_g", "rw_lnx_b", "att_bias", "w_out_e", "w_in_o", "sg_ln_g", "sg_ln_b", "sg_w", "sg_b",
             "w_out_o", "final_g"]
    results = {**sharded, **rep}
    outs = [rep["loss"][0][0, 0], dx.reshape(NSEQ, SEQ, D)]
    for kind in range(4):
        outs += [results[nm][kind] for nm in order]
    return tuple(outs)


def _local_step(x2, tgt, wie_t, late_weights, w2, a2, sglg, sglb, norm_g, shift_mu, rw_w0, rw_a0, rw_kk, rw_ka, rw_rk,
                rw_lnx_g, rw_lnx_b, att_bias, sg_w, sg_b, final_g, first_after=None, on_odd_grads=None,
                on_even_grads=None, on_small_grads=None):
    zl = jnp.zeros((LORA, W), F32)
    w2x = jnp.concatenate([w2, zl], axis=0)
    a2x = jnp.concatenate([zl, a2], axis=0)
    rk = rw_rk.reshape(1, W)
    pos = np.arange(SGC)
    sg_mask = jnp.asarray(((pos[None, :] // L) <= (pos[:, None] // L)).astype(np.float32))
    wm = (sg_w[0] * sg_mask[None]).astype(BF16)
    sgb_t = sg_b[0].T

    xn0, ps, ga, q, kb, vb, gb = ln_in_proj(x2, norm_g[0:1], wie_t, EVEN_SPLITS, "in_proj_even", after=first_after,
                                            w_t=True, bf16_pieces=(2, 3, 4))
    r, lw, k2, v, aa, bb = even_prep(ps, shift_mu, rw_w0, w2x, rw_a0, a2x, rw_kk, rw_ka)
    y, rw_saved = rwkv_fwd(r, lw, k2, v, aa, bb)
    bias = bias_expand(att_bias[0])

    def padded(a):
        return jnp.pad(a.astype(BF16).reshape(NSEQ, SEQ, W), ((0, 0), (LEFT * L, 0), (0, 0))).reshape(NSEQ * PADSEQ, W)

    kpad, vpad = padded(kb), padded(vb)
    o = attention_fwd(q, kpad, vpad, bias)
    woe, wio, woo = late_weights(o)
    h1, zt = even_post(y, r, k2, v, ga, o, gb, rw_lnx_g, rw_lnx_b, rk, x2, woe)
    xn1, u, vv, gt = ln_in_proj(h1, norm_g[1:2], wio, ODD_SPLITS, "in_proj_odd")
    dh2, loss_part, d_final_g, z2t = gmlp_fwd_loss(u, vv, gt, sglg, sglb, wm, sgb_t, h1, woo, final_g[None], tgt)

    du, dvv, dgt, d_sglg, d_sglb, d_wm, d_sgb_t, d_woo = gmlp_bwd(u, vv, gt, sglg, sglb, wm, sgb_t, dh2, z2t, woo)
    dp_odd = [du, dvv, dgt]
    d_wio = matmul_acc_chips(xn1, dp_odd, "in_proj_odd_dw")
    token = on_odd_grads(d_woo, d_wio) if on_odd_grads else None
    dh1, d_g1 = in_proj_bwd_x(h1, norm_g[1:2], wio, dp_odd, dh2, "in_proj_odd_bwd", after=token)
    odd_small = [d_wm * sg_mask[None], d_sgb_t.T, d_final_g, d_g1]
    token = on_small_grads("odd", odd_small) if on_small_grads else None
    dy, dr2, dk22, dv2, dga, do, dgb, d_lng, d_lnb, d_rk, d_woe = even_post_bwd(
        y, r, k2, v, ga, o, gb, rw_lnx_g, rw_lnx_b, rk, dh1, zt, woe, after=token)
    dq, dkb, dvb, dbias = attention_bwd(q, kpad, vpad, bias, do)
    dbias = sum(dbias[:, i * 2 * L:(i + 1) * 2 * L, i * L:i * L + BAND] for i in range(ATT_Q))
    d_att_bias = bias_grad(dbias.reshape(NH, L, BAND))
    dr, dlw, dk2, dv, daa, dbb = rwkv_bwd(r, lw, k2, aa, bb, rw_saved, dy)
    dps, d_mu, d_w0, d_w2x, d_a0, d_a2x, d_kk, d_ka = even_prep_bwd(
        ps, shift_mu, rw_w0, w2x, rw_a0, a2x, rw_kk, rw_ka, dr, dlw, dk2, dv, daa, dbb, dr2, dk22, dv2)
    dp_even = [dps, dga, dq, dkb, dvb, dgb]
    d_wie = matmul_acc_chips(xn0, dp_even, "in_proj_even_dw", add_cores=on_even_grads is not None)
    big_g = (d_wie, d_woe, d_wio, d_woo, d_w2x[:LORA], d_a2x[LORA:], d_sglg, d_sglb)
    token = on_even_grads(big_g) if on_even_grads else None
    dx, d_g0 = in_proj_bwd_x(x2, norm_g[0:1], wie_t, dp_even, dh1, "in_proj_even_bwd", after=token, w_t=True)
    even_small = [d_g0, d_mu, d_w0, d_a0, d_kk, d_ka, d_rk, d_lng, d_lnb, d_att_bias]
    if on_small_grads:
        on_small_grads("even", even_small + [loss_part[0:1, :]])
    rep_g = [jnp.concatenate([d_g0, d_g1], axis=0)] + even_small[1:] + odd_small[:3]
    return loss_part[0, 0], dx, big_g, rep_g
```

```python
import functools
import math

import jax
import jax.numpy as jnp
import numpy as np
from jax import lax
from jax.experimental import pallas as pl
from jax.experimental.pallas import tpu as pltpu

F32 = jnp.float32
BF16 = jnp.bfloat16
HI = lax.Precision.HIGHEST

D = 1024
SEQ = 2048
NSEQ = 2
T = NSEQ * SEQ
HD = 64
NH = 8
W = 512
SHIFT = 1664
LORA = 64
EVEN_IN = 4224
ODD_IN = 3072
L = 64
NC = SEQ // L
LEFT = 8
BAND = (LEFT + 1) * L
CLIP = 128
SGC = 128
NG = 8
RMS_EPS = 1e-6
LN_EPS = 1e-5
GN_EPS = 64e-5
NEG = -1e30
VMEM_BIG = 56 * 1024 * 1024

ADAM_LR = 0.001
ADAM_B1 = 0.9
ADAM_B2 = 0.999
ADAM_EPS = 1e-08
ADAM_WD = 0.01
ADAM_STEP = 10

MESH = pl.DeviceIdType.MESH


def _bdot(a, b):
    return jnp.dot(a.astype(BF16), b.astype(BF16), preferred_element_type=F32)


def _bdot_nt(a, b):
    return lax.dot_general(a.astype(BF16), b.astype(BF16), (((1,), (1,)), ((), ())), preferred_element_type=F32)


def _bdot_tn(a, b):
    return lax.dot_general(a.astype(BF16), b.astype(BF16), (((0,), (0,)), ((), ())), preferred_element_type=F32)


def _hdot(a, b):
    return jnp.dot(a, b, precision=HI, preferred_element_type=F32)


def _hdot_nt(a, b):
    return lax.dot_general(a, b, (((1,), (1,)), ((), ())), precision=HI, preferred_element_type=F32)


def _hdot_tn(a, b):
    return lax.dot_general(a, b, (((0,), (0,)), ((), ())), precision=HI, preferred_element_type=F32)


def _iota2(shape, dim):
    return lax.broadcasted_iota(jnp.int32, shape, dim)


def _head_blockdiag():
    r = _iota2((2 * HD, 2 * HD), 0) // HD
    c = _iota2((2 * HD, 2 * HD), 1) // HD
    return (r == c).astype(BF16)


def _headsum_impl(x, bd):
    hi = x.astype(BF16)
    mid = (x - hi.astype(F32)).astype(BF16)
    n = bd.shape[0]
    out = [jnp.dot(hi[:, i:i + n], bd, preferred_element_type=F32) + jnp.dot(mid[:, i:i + n], bd, preferred_element_type=F32)
           for i in range(0, x.shape[1], n)]
    return jnp.concatenate(out, axis=-1)


@jax.custom_vjp
def _headsum(x, bd):
    return _headsum_impl(x, bd)


def _headsum_fwd(x, bd):
    return _headsum_impl(x, bd), bd


def _headsum_bwd(bd, ct):
    return _headsum_impl(ct, bd), None


_headsum.defvjp(_headsum_fwd, _headsum_bwd)


def _silu(x):
    return x * jax.nn.sigmoid(x)


_GELU_C = math.sqrt(2.0 / math.pi)


def _gelu(x):
    return 0.5 * x * (1.0 + jnp.tanh(_GELU_C * (x + 0.044715 * (x * x * x))))


def _silu_both(x):
    s = jax.nn.sigmoid(x)
    xs = x * s
    return xs, s + xs * (1.0 - s)


def _gelu_both(x):
    x2 = x * x
    t = jnp.tanh(_GELU_C * (x + 0.044715 * (x2 * x)))
    half = 0.5 * (1.0 + t)
    return x * half, half + 0.5 * x * (1.0 - t * t) * _GELU_C * (1.0 + 3.0 * 0.044715 * x2)


def _softplus(x):
    return jnp.maximum(x, 0.0) + jnp.log(1.0 + jnp.exp(-jnp.abs(x)))


def _cparams(sem, vmem=None):
    return pltpu.CompilerParams(dimension_semantics=sem, vmem_limit_bytes=vmem)


def _row_spec(tm, width):
    return pl.BlockSpec((tm, width), lambda i: (i, 0))


def _col_spec(height, tm):
    return pl.BlockSpec((height, tm), lambda i: (0, i))


def _const_spec(shape):
    nd = len(shape)
    return pl.BlockSpec(shape, lambda *_: (0,) * nd)


def _weight_dims(w_bf, w_t):
    return (((1,), (1,)), ((), ())) if w_t else (((1,), (0,)), ((), ())), w_bf.shape[0 if w_t else 1]


def ln_in_proj(x, g, w_bf, splits, name, after=None, w_t=False, bf16_pieces=()):
    dims, n = _weight_dims(w_bf, w_t)
    dtypes = [BF16 if i in bf16_pieces else F32 for i in range(len(splits))]
    tm = 512 if n <= ODD_IN else 256
    spans = []
    o = 0
    for s in splits:
        spans.append((o, o + s))
        o += s
    assert o == n
    extra_specs, extra = _after_operand(after)

    def body(x_ref, g_ref, w_ref, *rest):
        xn_ref, outs = rest[len(extra)], rest[len(extra) + 1:]
        xv = x_ref[...]
        rstd = lax.rsqrt(jnp.mean(xv * xv, axis=-1, keepdims=True) + RMS_EPS)
        xn = (xv * rstd * g_ref[...]).astype(BF16)
        xn_ref[...] = xn.T
        p = lax.dot_general(xn, w_ref[...], dims, preferred_element_type=F32)
        for o_ref, (a, b) in zip(outs, spans):
            o_ref[...] = p[:, a:b].astype(o_ref.dtype)

    return pl.pallas_call(
        body, grid=(T // tm,), name=name,
        in_specs=[_row_spec(tm, D), _const_spec((1, D)), _const_spec(w_bf.shape)] + extra_specs,
        out_specs=[_col_spec(D, tm)] + [_row_spec(tm, s) for s in splits],
        out_shape=[jax.ShapeDtypeStruct((D, T), BF16)]
        + [jax.ShapeDtypeStruct((T, s), dt) for s, dt in zip(splits, dtypes)],
        compiler_params=_cparams(("parallel",), VMEM_BIG),
    )(x, g, w_bf, *extra)


def in_proj_bwd_x(x, g, w_bf, dps, dres, name, after=None, w_t=False):
    tm = 512
    back = (((1,), (0,)), ((), ())) if w_t else (((1,), (1,)), ((), ()))
    widths = [d.shape[1] for d in dps]
    extra_specs, extra = _after_operand(after)

    def body(x_ref, g_ref, w_ref, dres_ref, *rest):
        dp_refs = rest[:len(widths)]
        dx_ref, dg_ref = rest[-2:]
        dp = jnp.concatenate([r[...] for r in dp_refs], axis=-1)
        dxn = lax.dot_general(dp, w_ref[...], back, preferred_element_type=F32)
        xv = x_ref[...]
        rstd = lax.rsqrt(jnp.mean(xv * xv, axis=-1, keepdims=True) + RMS_EPS)
        xhat = xv * rstd
        dgp = jnp.sum(dxn * xhat, axis=0, keepdims=True)

        @pl.when(pl.program_id(0) == 0)
        def _():
            dg_ref[...] = jnp.zeros_like(dg_ref)

        dg_ref[...] += dgp
        dxh = dxn * g_ref[...]
        dx_ref[...] = dres_ref[...] + rstd * (dxh - xhat * jnp.mean(dxh * xhat, axis=-1, keepdims=True))

    return pl.pallas_call(
        body, grid=(T // tm,), name=name,
        in_specs=[_row_spec(tm, D), _const_spec((1, D)), _const_spec(w_bf.shape), _row_spec(tm, D)]
        + [_row_spec(tm, s) for s in widths] + extra_specs,
        out_specs=[_row_spec(tm, D), _const_spec((1, D))],
        out_shape=[jax.ShapeDtypeStruct((T, D), F32), jax.ShapeDtypeStruct((1, D), F32)],
        compiler_params=_cparams(("arbitrary",), VMEM_BIG),
    )(x, g, w_bf, dres, *dps, *extra)


def _after_operand(after):
    return ([ANY], [after]) if after is not None else ([], [])


def matmul_acc_chips(at_bf, pieces, name, after=None, add_cores=False):
    k = at_bf.shape[0]
    widths = [p.shape[1] for p in pieces]
    nb = sum(widths) // NCHIP
    tm = 512
    steps = T // tm
    half = k // 2
    extra_specs, extra = _after_operand(after)

    def body(a_ref, *rest):
        o_ref, acc = rest[len(widths) + len(extra):][:2]

        @pl.when(pl.program_id(0) == 0)
        def _():
            acc[...] = jnp.zeros_like(acc)

        a = a_ref[...]
        b = jnp.concatenate([r[...] for r in rest[:len(widths)]], axis=-1)
        for s in range(NCHIP):
            acc[s] += jnp.dot(a, b[:, s * nb:(s + 1) * nb], preferred_element_type=F32)

        @pl.when(pl.program_id(0) == steps - 1)
        def _():
            if not add_cores:
                o_ref[...] = acc[...].astype(BF16)
            else:
                give, got, send, recv = rest[-4:]
                x, y, c = lax.axis_index("x"), lax.axis_index("y"), lax.axis_index("c")
                theirs = pl.multiple_of((1 - c) * half, half)
                mine = pl.multiple_of(c * half, half)
                give[...] = acc[:, pl.ds(theirs, half), :].astype(BF16)
                cp = pltpu.make_async_remote_copy(src_ref=give, dst_ref=got, send_sem=send, recv_sem=recv,
                                                  device_id=(x, y, 1 - c), device_id_type=MESH)
                cp.start()
                cp.wait()
                o_ref[...] = (acc[:, pl.ds(mine, half), :] + got[...].astype(F32)).astype(BF16)

    out_rows = half if add_cores else k
    exchange = [pltpu.VMEM((NCHIP, half, nb), BF16)] * 2 + [pltpu.SemaphoreType.DMA] * 2 if add_cores else []
    return pl.pallas_call(
        body, grid=(steps,), name=name,
        in_specs=[_col_spec(k, tm)] + [_row_spec(tm, w_) for w_ in widths] + extra_specs,
        out_specs=_const_spec((NCHIP, out_rows, nb)),
        out_shape=jax.ShapeDtypeStruct((NCHIP, out_rows, nb), BF16),
        scratch_shapes=[pltpu.VMEM((NCHIP, k, nb), F32)] + exchange,
        compiler_params=_cparams(("arbitrary",), VMEM_BIG),
    )(at_bf, *pieces, *extra)


def _out_proj_back(dh_ref, zt_ref, w_ref, dw_ref, acc_ref):
    dhb = dh_ref[...].astype(BF16)

    @pl.when(pl.program_id(0) == 0)
    def _():
        acc_ref[...] = jnp.zeros_like(acc_ref)

    acc_ref[...] += jnp.dot(zt_ref[...], dhb, preferred_element_type=F32)

    @pl.when(pl.program_id(0) == pl.num_programs(0) - 1)
    def _():
        dw_ref[...] = acc_ref[...].astype(dw_ref.dtype)

    return lax.dot_general(dhb, w_ref[...], (((1,), (1,)), ((), ())), preferred_element_type=F32)


PREP_TM = 512
PREP_NB = SEQ // PREP_TM


def _prep_elem(k, wl, apre, kkw, kaw, bd):
    wraw = -_softplus(-wl) - 0.5
    lw = -jnp.exp(wraw)
    asig = jax.nn.sigmoid(apre)
    kkr = k * kkw
    nrm = jnp.maximum(jnp.sqrt(_headsum(kkr * kkr, bd)), 1e-12)
    kk = kkr / nrm
    k2 = k * (1.0 + (asig - 1.0) * kaw)
    return lw, k2, -kk, kk * asig


def _prep_elem_bwd(k, wl, apre, kkw, kaw, bd, dlw, dk2, daa, dbb):
    s = -wl
    sp = _softplus(s)
    dwl = dlw * (-jnp.exp(-sp - 0.5)) * jnp.exp(s - sp)
    asig = jax.nn.sigmoid(apre)
    kkr = k * kkw
    root = jnp.sqrt(_headsum(kkr * kkr, bd))
    inv = 1.0 / jnp.maximum(root, 1e-12)
    kk = kkr * inv
    dkk = dbb * asig - daa
    dap = (dbb * kk + dk2 * k * kaw) * asig * (1.0 - asig)
    through_norm = jnp.where(root > 1e-12, kk * _headsum(dkk * kkr, bd) * inv, 0.0)
    dkkr = inv * (dkk - through_norm)
    gain = 1.0 + (asig - 1.0) * kaw
    dk = dkkr * kkw + dk2 * gain
    dkkw = jnp.sum(dkkr * k, axis=0, keepdims=True)
    dkaw = jnp.sum(dk2 * k * (asig - 1.0), axis=0, keepdims=True)
    return dk, dwl, dap, dkkw, dkaw


def _shifted(ps_ref, prev_ref, mu, blk):
    p = ps_ref[...]
    first = (blk % PREP_NB) == 0
    prev_row = jnp.where(first, 0.0, prev_ref[7:8, :])
    rolled = pltpu.roll(p, 1, 0)
    p_prev = jnp.where(_iota2(p.shape, 0) == 0, prev_row, rolled)
    return p, p_prev, p + (p_prev - p) * mu


def _prev_spec(width, blk_of):
    return pl.BlockSpec((8, width), lambda i: (jnp.maximum(blk_of(i) * (PREP_TM // 8) - 1, 0), 0))


def even_prep(ps, mu, w0, w2x, a0, a2x, kkw, kaw):
    tm = PREP_TM

    def body(ps_ref, prev_ref, mu_ref, w0_ref, w2_ref, a0_ref, a2_ref, kk_ref, ka_ref,
             r_ref, lw_ref, k2_ref, v_ref, aa_ref, bb_ref):
        _, _, s = _shifted(ps_ref, prev_ref, mu_ref[...], pl.program_id(0))
        wa = s[:, 3 * W:]
        wl = w0_ref[...] + _bdot(jnp.tanh(wa), w2_ref[...])
        apre = a0_ref[...] + _bdot(wa, a2_ref[...])
        lw, k2, aa, bb = _prep_elem(s[:, W:2 * W], wl, apre, kk_ref[...], ka_ref[...], _head_blockdiag())
        r_ref[...] = s[:, 0:W]
        v_ref[...] = s[:, 2 * W:3 * W]
        lw_ref[...] = lw
        k2_ref[...] = k2
        aa_ref[...] = aa
        bb_ref[...] = bb

    vec = _const_spec((1, W))
    return pl.pallas_call(
        body, grid=(T // tm,), name="even_prep",
        in_specs=[_row_spec(tm, SHIFT), _prev_spec(SHIFT, lambda i: i), _const_spec((1, SHIFT)), vec,
                  _const_spec((2 * LORA, W)), vec, _const_spec((2 * LORA, W)), vec, vec],
        out_specs=[_row_spec(tm, W)] * 6,
        out_shape=[jax.ShapeDtypeStruct((T, W), F32)] * 6,
        compiler_params=_cparams(("parallel",), VMEM_BIG),
    )(ps, ps, mu, w0, w2x, a0, a2x, kkw, kaw)


def even_prep_bwd(ps, mu, w0, w2x, a0, a2x, kkw, kaw, dr, dlw, dk2, dv, daa, dbb, dr2, dk22, dv2):
    tm = PREP_TM
    nb = T // tm
    rev = lambda i: nb - 1 - i

    def body(ps_ref, prev_ref, mu_ref, w0_ref, w2_ref, a0_ref, a2_ref, kk_ref, ka_ref,
             dr_ref, dlw_ref, dk2_ref, dv_ref, daa_ref, dbb_ref, dr2_ref, dk22_ref, dv2_ref,
             dps_ref, dmu_ref, dw0_ref, dw2_ref, da0_ref, da2_ref, dkk_ref, dka_ref, carry):
        i = pl.program_id(0)
        blk = rev(i)
        mu_v = mu_ref[...]
        p, p_prev, s = _shifted(ps_ref, prev_ref, mu_v, blk)
        wa = s[:, 3 * W:]
        th = jnp.tanh(wa)
        wl = w0_ref[...] + _bdot(th, w2_ref[...])
        apre = a0_ref[...] + _bdot(wa, a2_ref[...])
        bd = _head_blockdiag()
        k = s[:, W:2 * W]
        dk, dwl, dap, dkkw, dkaw = _prep_elem_bwd(k, wl, apre, kk_ref[...], ka_ref[...], bd, dlw_ref[...],
                                                  dk2_ref[...] + dk22_ref[...], daa_ref[...], dbb_ref[...])
        dwa = _bdot_nt(dwl, w2_ref[...]) * (1.0 - th * th) + _bdot_nt(dap, a2_ref[...])
        ds = jnp.concatenate([dr_ref[...] + dr2_ref[...], dk, dv_ref[...] + dv2_ref[...], dwa], axis=-1)

        @pl.when(i == 0)
        def _():
            for ref in (dmu_ref, dw0_ref, dw2_ref, da0_ref, da2_ref, dkk_ref, dka_ref, carry):
                ref[...] = jnp.zeros_like(ref)

        dmu_ref[...] += jnp.sum(ds * (p_prev - p), axis=0, keepdims=True)
        dw0_ref[...] += jnp.sum(dwl, axis=0, keepdims=True)
        da0_ref[...] += jnp.sum(dap, axis=0, keepdims=True)
        dw2_ref[...] += _bdot_tn(th, dwl)
        da2_ref[...] += _bdot_tn(wa, dap)
        dkk_ref[...] += dkkw
        dka_ref[...] += dkaw
        dsm = ds * mu_v
        last = (blk % PREP_NB) == PREP_NB - 1
        nxt = jnp.where(last, 0.0, carry[0:1, :])
        up = pltpu.roll(dsm, tm - 1, 0)
        up = jnp.where(_iota2(up.shape, 0) == tm - 1, nxt, up)
        dps_ref[...] = (ds - dsm + up).astype(BF16)
        carry[0:1, :] = dsm[0:1, :]

    vec = _const_spec((1, W))
    rrow = lambda width: pl.BlockSpec((tm, width), lambda i: (rev(i), 0))
    return pl.pallas_call(
        body, grid=(nb,), name="even_prep_bwd",
        in_specs=[rrow(SHIFT), _prev_spec(SHIFT, rev), _const_spec((1, SHIFT)), vec,
                  _const_spec((2 * LORA, W)), vec, _const_spec((2 * LORA, W)), vec, vec] + [rrow(W)] * 9,
        out_specs=[rrow(SHIFT), _const_spec((1, SHIFT)), vec, _const_spec((2 * LORA, W)), vec,
                   _const_spec((2 * LORA, W)), vec, vec],
        out_shape=[jax.ShapeDtypeStruct((T, SHIFT), BF16), jax.ShapeDtypeStruct((1, SHIFT), F32),
                   jax.ShapeDtypeStruct((1, W), F32), jax.ShapeDtypeStruct((2 * LORA, W), F32),
                   jax.ShapeDtypeStruct((1, W), F32), jax.ShapeDtypeStruct((2 * LORA, W), F32),
                   jax.ShapeDtypeStruct((1, W), F32), jax.ShapeDtypeStruct((1, W), F32)],
        scratch_shapes=[pltpu.VMEM((8, SHIFT), F32)],
        compiler_params=_cparams(("arbitrary",), VMEM_BIG),
    )(ps, ps, mu, w0, w2x, a0, a2x, kkw, kaw, dr, dlw, dk2, dv, daa, dbb, dr2, dk22, dv2)


NPAIR = NH // 2
PW = 2 * HD


def _pair_cols(p):
    return slice(p * PW, (p + 1) * PW)


def _pairs(a):
    return [a[:, _pair_cols(p)] for p in range(NPAIR)]


def _stack_pair(a):
    first = _iota2(a.shape, 1) < HD
    zero = jnp.zeros_like(a)
    return jnp.concatenate([jnp.where(first, a, zero), jnp.where(first, zero, a)], axis=0)


def _unstack_pair(a):
    n = a.shape[0] // 2
    return jnp.where(_iota2((n, PW), 1) < HD, a[:n], a[n:])


def _fold_pair(a):
    n = a.shape[0] // 2
    return a[:n] + a[n:]


def _chunk_masks():
    n = 4 * L
    row = _iota2((n, n), 0)
    col = _iota2((n, n), 1)
    same = ((row // L) & 1) == ((col // L) & 1)
    ri = row & (L - 1)
    ci = col & (L - 1)
    keep = same & (((row < 2 * L) & (ri > ci)) | ((row >= 2 * L) & (ri >= ci)))
    r1 = _iota2((L, L), 0)
    c1 = _iota2((L, L), 1)
    r2 = _iota2((2 * L, 2 * L), 0)
    c2 = _iota2((2 * L, 2 * L), 1)
    return keep.astype(F32), (r1 >= c1).astype(F32), (r2 == c2).astype(F32)


def _scaled(r, lw, k2, aa, bb, tri):
    g = _hdot(tri, lw)
    eg = jnp.exp(g)
    eng = jnp.exp(-g)
    egp = jnp.exp(g - lw)
    return eg, eng, egp, aa * egp, r * eg, bb * eng, k2 * eng


def _head_cols(h):
    return slice(h * HD, (h + 1) * HD)


def _per_head(a):
    return [a[:, _head_cols(h)] for h in range(NH)]


def _pairs_operands(at, rt, bt, kt):
    x = [jnp.concatenate([_stack_pair(a), _stack_pair(r)], axis=0).astype(BF16) for a, r in zip(_pairs(at), _pairs(rt))]
    yk = [jnp.concatenate([_stack_pair(b), _stack_pair(k)], axis=0).astype(BF16) for b, k in zip(_pairs(bt), _pairs(kt))]
    return x, yk


def _pairs_matrices(x, yk, keep, eye):
    m = [_bdot_nt(a, b) * keep for a, b in zip(x, yk)]
    p = [a[:2 * L, :2 * L] for a in m]
    tinv = [eye + a for a in p]
    for _ in range(5):
        p = [_bdot(a, a) for a in p]
        tinv = [t + _bdot(t, a) for t, a in zip(tinv, p)]
    return [a.astype(BF16) for a in m], [a.astype(BF16) for a in tinv]


def _pairs_fwd(x, yk, m, tinv, vw, s0, egl):
    xh = [_bdot_nt(a, s) for a, s in zip(x, s0)]
    u = [_bdot(t, h[:2 * L] + _bdot(a[:2 * L, 2 * L:], w)) for t, h, a, w in zip(tinv, xh, m, vw)]
    uv = [jnp.concatenate([a, w], axis=0).astype(BF16) for a, w in zip(u, vw)]
    y = [h[2 * L:] + _bdot(a[2 * L:], w) for h, a, w in zip(xh, m, uv)]
    sn = [e * (s + _bdot_tn(w, b)) for e, s, w, b in zip(egl, s0, uv, yk)]
    return y, sn, uv


def _pairs_bwd(x, yk, m, tinv, uv, s0, sn, egl, dyw, dsn, keep):
    dzs = [d * e for d, e in zip(dsn, egl)]
    dgl = [jnp.sum(d * s, axis=0, keepdims=True) for d, s in zip(dsn, sn)]
    dyb = [a.astype(BF16) for a in dyw]
    t1 = [_bdot_tn(a[2 * L:], d) for a, d in zip(m, dyb)]
    t2 = [_bdot_nt(b, d) for b, d in zip(yk, dzs)]
    drhs = [_bdot_tn(t, a[:2 * L] + b[:2 * L]) for t, a, b in zip(tinv, t1, t2)]
    dv = [a[2 * L:] + b[2 * L:] + _bdot_tn(c[:2 * L, 2 * L:], d) for a, b, c, d in zip(t1, t2, m, drhs)]
    gg = [jnp.concatenate([a, b], axis=0).astype(BF16) for a, b in zip(drhs, dyw)]
    ds0 = [d + _bdot_tn(g, a) for d, g, a in zip(dzs, gg, x)]
    dm = [_bdot_nt(g, w) * keep for g, w in zip(gg, uv)]
    dx = [_bdot(g, s) + _bdot(d, b) for g, s, d, b in zip(gg, s0, dm, yk)]
    dyk = [_bdot_tn(d, a) + _bdot(w, z) for d, a, w, z in zip(dm, x, uv, dzs)]
    return dx, dyk, dv, dgl, ds0


STATE_SHAPE = (NPAIR * PW, PW)
M_SHAPE = (4 * L, NPAIR * 4 * L)
TINV_SHAPE = (2 * L, NPAIR * 2 * L)


def _rows_of(a, n):
    return [a[i * n:(i + 1) * n, :] for i in range(NPAIR)]


def _both(f):
    out = []
    for s in range(NSEQ):
        out += f(s)
    return out


def _seq_view(a):
    return a.reshape(NSEQ, SEQ, a.shape[-1])


UV_SHAPE = (4 * L, NPAIR * PW)
RW_CHUNKS = 2


def rwkv_fwd(r, lw, k2, v, aa, bb):
    def body(r_ref, lw_ref, k2_ref, v_ref, aa_ref, bb_ref, y_ref, hs_ref, hn_ref, m_ref, t_ref, uv_ref, state):
        @pl.when(pl.program_id(0) == 0)
        def _():
            state[...] = jnp.zeros_like(state)

        keep, tri, eye = _chunk_masks()
        where = [(j, s) for j in range(RW_CHUNKS) for s in range(NSEQ)]
        rows = lambda j: slice(j * L, (j + 1) * L)
        sc = [_scaled(r_ref[s, rows(j)], lw_ref[s, rows(j)], k2_ref[s, rows(j)], aa_ref[s, rows(j)],
                      bb_ref[s, rows(j)], tri) for j, s in where]
        ops = [_pairs_operands(*a[3:]) for a in sc]
        m, tinv = _pairs_matrices([a for o in ops for a in o[0]], [a for o in ops for a in o[1]], keep, eye)
        s_cur = [state[s] for s in range(NSEQ)]
        for j in range(RW_CHUNKS):
            mine = slice(j * NSEQ * NPAIR, (j + 1) * NSEQ * NPAIR)
            x = [a for o in ops[j * NSEQ:(j + 1) * NSEQ] for a in o[0]]
            yk = [a for o in ops[j * NSEQ:(j + 1) * NSEQ] for a in o[1]]
            vw = _both(lambda s: [_stack_pair(a) for a in _pairs(v_ref[s, rows(j)])])
            egl = _both(lambda s: _pairs(sc[j * NSEQ + s][0][L - 1:L, :]))
            y, sn, uv = _pairs_fwd(x, yk, m[mine], tinv[mine], vw, _both(lambda s: _rows_of(s_cur[s], PW)), egl)
            for s in range(NSEQ):
                ps = slice(s * NPAIR, (s + 1) * NPAIR)
                hs_ref[j, s] = s_cur[s]
                y_ref[s, rows(j)] = jnp.concatenate([_fold_pair(a) for a in y[ps]], axis=-1)
                m_ref[j, s] = jnp.concatenate(m[mine][ps], axis=-1)
                t_ref[j, s] = jnp.concatenate(tinv[mine][ps], axis=-1)
                uv_ref[j, s] = jnp.concatenate(uv[ps], axis=-1)
                s_cur[s] = jnp.concatenate(sn[ps], axis=0)
                hn_ref[j, s] = s_cur[s]
        for s in range(NSEQ):
            state[s] = s_cur[s]

    blk = pl.BlockSpec((NSEQ, RW_CHUNKS * L, W), lambda c: (0, c, 0))
    per_chunk = lambda shape: pl.BlockSpec((RW_CHUNKS, NSEQ) + shape, lambda c: (c, 0, 0, 0))
    saved_shapes = [(STATE_SHAPE, F32), (STATE_SHAPE, F32), (M_SHAPE, BF16), (TINV_SHAPE, BF16), (UV_SHAPE, BF16)]
    y, *saved = pl.pallas_call(
        body, grid=(NC // RW_CHUNKS,), name="rwkv_fwd",
        in_specs=[blk] * 6,
        out_specs=[blk] + [per_chunk(shape) for shape, _ in saved_shapes],
        out_shape=[jax.ShapeDtypeStruct((NSEQ, SEQ, W), F32)]
        + [jax.ShapeDtypeStruct((NC, NSEQ) + shape, dt) for shape, dt in saved_shapes],
        scratch_shapes=[pltpu.VMEM((NSEQ,) + STATE_SHAPE, F32)],
        compiler_params=_cparams(("arbitrary",), VMEM_BIG),
    )(*[_seq_view(a) for a in (r, lw, k2, v, aa, bb)])
    return y.reshape(T, W), saved


def rwkv_bwd(r, lw, k2, aa, bb, saved, dy):
    def body(r_ref, lw_ref, k2_ref, aa_ref, bb_ref, hs_ref, hn_ref, m_ref, t_ref, uv_ref, dy_ref,
             dr_ref, dlw_ref, dk2_ref, dv_ref, daa_ref, dbb_ref, dstate):
        @pl.when(pl.program_id(0) == 0)
        def _():
            dstate[...] = jnp.zeros_like(dstate)

        keep, tri, _ = _chunk_masks()
        sc = [_scaled(r_ref[s], lw_ref[s], k2_ref[s], aa_ref[s], bb_ref[s], tri) for s in range(NSEQ)]
        ops = [_pairs_operands(*sc[s][3:]) for s in range(NSEQ)]
        x, yk = _both(lambda s: ops[s][0]), _both(lambda s: ops[s][1])
        m = _both(lambda s: [m_ref[0, s][:, i * 4 * L:(i + 1) * 4 * L] for i in range(NPAIR)])
        tinv = _both(lambda s: [t_ref[0, s][:, i * 2 * L:(i + 1) * 2 * L] for i in range(NPAIR)])
        uv = _both(lambda s: _pairs(uv_ref[0, s]))
        dyw = _both(lambda s: [_stack_pair(a) for a in _pairs(dy_ref[s])])
        s0 = _both(lambda s: _rows_of(hs_ref[0, s], PW))
        sn = _both(lambda s: _rows_of(hn_ref[0, s], PW))
        dsn = _both(lambda s: _rows_of(dstate[s], PW))
        egl = _both(lambda s: _pairs(sc[s][0][L - 1:L, :]))
        dx, dyk, dvw, dgl, ds0 = _pairs_bwd(x, yk, m, tinv, uv, s0, sn, egl, dyw, dsn, keep)
        for s in range(NSEQ):
            mine = slice(s * NPAIR, (s + 1) * NPAIR)
            eg, eng, egp, at, rt, bt, kt = sc[s]
            dstate[s] = jnp.concatenate(ds0[mine], axis=0)
            dv_ref[s] = jnp.concatenate([_fold_pair(a) for a in dvw[mine]], axis=-1)
            dat = jnp.concatenate([_fold_pair(a[:2 * L]) for a in dx[mine]], axis=-1)
            drt = jnp.concatenate([_fold_pair(a[2 * L:]) for a in dx[mine]], axis=-1)
            dbt = jnp.concatenate([_fold_pair(a[:2 * L]) for a in dyk[mine]], axis=-1)
            dkt = jnp.concatenate([_fold_pair(a[2 * L:]) for a in dyk[mine]], axis=-1)
            dg = drt * rt - dbt * bt - dkt * kt
            dg = dg + jnp.where(_iota2(dg.shape, 0) == L - 1, jnp.concatenate(dgl[mine], axis=-1), 0.0)
            dgp = dat * at
            dlw_ref[s] = _hdot_tn(tri, dg + dgp) - dgp
            dr_ref[s] = drt * eg
            daa_ref[s] = dat * egp
            dbb_ref[s] = dbt * eng
            dk2_ref[s] = dkt * eng

    blk = pl.BlockSpec((NSEQ, L, W), lambda c: (0, NC - 1 - c, 0))
    per_chunk = lambda shape: pl.BlockSpec((1, NSEQ) + shape, lambda c: (NC - 1 - c, 0, 0, 0))
    outs = pl.pallas_call(
        body, grid=(NC,), name="rwkv_bwd",
        in_specs=[blk] * 5 + [per_chunk(a.shape[2:]) for a in saved] + [blk],
        out_specs=[blk] * 6,
        out_shape=[jax.ShapeDtypeStruct((NSEQ, SEQ, W), F32)] * 6,
        scratch_shapes=[pltpu.VMEM((NSEQ,) + STATE_SHAPE, F32)],
        compiler_params=_cparams(("arbitrary",)),
    )(*[_seq_view(a) for a in (r, lw, k2, aa, bb)], *saved, _seq_view(dy))
    return [a.reshape(T, W) for a in outs]


def _post_math(y, r, k2, v, ga, o, gb, lng, lnb, rk, bd):
    mu = _headsum(y, bd) * (1.0 / HD)
    yc = y - mu
    var = _headsum(yc * yc, bd) * (1.0 / HD)
    yn = yc * lax.rsqrt(var + GN_EPS) * lng + lnb
    bonus = _headsum(r * k2 * rk, bd) * v
    return (yn + bonus) * _silu(ga), o * _silu(gb)


def even_post(y, r, k2, v, ga, o, gb, lng, lnb, rk, h, w_bf):
    tm = 512

    def body(y_ref, r_ref, k2_ref, v_ref, ga_ref, o_ref, gb_ref, lng_ref, lnb_ref, rk_ref, h_ref, w_ref,
             ho_ref, zt_ref):
        ya, yb = _post_math(y_ref[...], r_ref[...], k2_ref[...], v_ref[...], ga_ref[...], o_ref[...], gb_ref[...],
                            lng_ref[...], lnb_ref[...], rk_ref[...], _head_blockdiag())
        z = jnp.concatenate([ya.astype(BF16), yb.astype(BF16)], axis=-1)
        zt_ref[...] = z.T
        ho_ref[...] = h_ref[...] + jnp.dot(z, w_ref[...], preferred_element_type=F32)

    vec = _const_spec((1, W))
    return pl.pallas_call(
        body, grid=(T // tm,), name="even_post",
        in_specs=[_row_spec(tm, W)] * 7 + [vec] * 3 + [_row_spec(tm, D), _const_spec((D, D))],
        out_specs=[_row_spec(tm, D), _col_spec(D, tm)],
        out_shape=[jax.ShapeDtypeStruct((T, D), F32), jax.ShapeDtypeStruct((D, T), BF16)],
        compiler_params=_cparams(("parallel",), VMEM_BIG),
    )(y, r, k2, v, ga, o, gb, lng, lnb, rk, h, w_bf)


def even_post_bwd(y, r, k2, v, ga, o, gb, lng, lnb, rk, dh, zt_bf, w_bf, after=None):
    tm = 512
    extra_specs, extra = _after_operand(after)

    def body(y_ref, r_ref, k2_ref, v_ref, ga_ref, o_ref, gb_ref, lng_ref, lnb_ref, rk_ref, dh_ref, zt_ref, w_ref,
             *rest):
        (dy_ref, dr_ref, dk2_ref, dv_ref, dga_ref, do_ref, dgb_ref, dlng_ref, dlnb_ref, drk_ref, dw_ref,
         acc_ref) = rest[-12:]
        dzv = _out_proj_back(dh_ref, zt_ref, w_ref, dw_ref, acc_ref)
        bd = _head_blockdiag()
        _, vjp = jax.vjp(lambda *a: _post_math(*a, bd), y_ref[...], r_ref[...], k2_ref[...], v_ref[...], ga_ref[...],
                         o_ref[...], gb_ref[...], lng_ref[...], lnb_ref[...], rk_ref[...])
        dy, dr, dk2, dv, dga, do, dgb, dlng, dlnb, drk = vjp((dzv[:, 0:W], dzv[:, W:2 * W]))
        for ref, val in ((dy_ref, dy), (dr_ref, dr), (dk2_ref, dk2), (dv_ref, dv), (dga_ref, dga), (do_ref, do),
                         (dgb_ref, dgb)):
            ref[...] = val.astype(ref.dtype)

        @pl.when(pl.program_id(0) == 0)
        def _():
            for ref in (dlng_ref, dlnb_ref, drk_ref):
                ref[...] = jnp.zeros_like(ref)

        dlng_ref[...] += dlng
        dlnb_ref[...] += dlnb
        drk_ref[...] += drk

    vec = _const_spec((1, W))
    return pl.pallas_call(
        body, grid=(T // tm,), name="even_post_bwd",
        in_specs=[_row_spec(tm, W)] * 7 + [vec] * 3 + [_row_spec(tm, D), _col_spec(D, tm), _const_spec((D, D))]
        + extra_specs,
        out_specs=[_row_spec(tm, W)] * 7 + [vec] * 3 + [_const_spec((D, D))],
        out_shape=[jax.ShapeDtypeStruct((T, W), dt) for dt in (F32, F32, F32, F32, BF16, F32, BF16)]
        + [jax.ShapeDtypeStruct((1, W), F32)] * 3 + [jax.ShapeDtypeStruct((D, D), BF16)],
        scratch_shapes=[pltpu.VMEM((D, D), F32)],
        compiler_params=_cparams(("arbitrary",), VMEM_BIG),
    )(y, r, k2, v, ga, o, gb, lng, lnb, rk, dh, zt_bf, w_bf, *extra)


PADSEQ = SEQ + LEFT * L
ATT_SCALE = 1.0 / math.sqrt(HD)
ATT_Q = 4
WIN = BAND + (ATT_Q - 1) * L
ATT_STEPS = NC // ATT_Q
ATT_BIAS_SHAPE = (NPAIR, ATT_Q * 2 * L, WIN)
ATT_WINDOW_BIAS_SHAPE = (ATT_Q, NPAIR, 2 * L, WIN)


def _stack_chunks(a):
    return jnp.concatenate([_stack_pair(a[i * L:(i + 1) * L]) for i in range(ATT_Q)], axis=0)


def _unstack_chunks(a):
    return jnp.concatenate([_unstack_pair(a[i * 2 * L:(i + 1) * 2 * L]) for i in range(ATT_Q)], axis=0)


def _window_bias(b_ref):
    return [jnp.concatenate([b_ref[c, p] for c in range(ATT_Q)], axis=0) for p in range(NPAIR)]


def _key_window(ref, step):
    start = step * (ATT_Q * L) - LEFT * L
    rows = ref[pl.ds(pl.multiple_of(jnp.maximum(start, 0), L), WIN), :]
    window = rows
    for lead in range(ATT_Q * L, LEFT * L + 1, ATT_Q * L):
        moved = jnp.concatenate([rows[WIN - lead:], rows[:WIN - lead]], axis=0)
        window = jnp.where(start == -lead, moved, window)
    return window


def _att_probs(q2, kw, bias, step):
    valid = _iota2((1, WIN), 1) >= (LEFT - step * ATT_Q) * L
    s = [jnp.where(valid, _bdot_nt(a, b) * ATT_SCALE + bias[p], NEG) for p, (a, b) in enumerate(zip(q2, kw))]
    e = [jnp.exp(a - jnp.max(a, axis=-1, keepdims=True)) for a in s]
    return [a / jnp.sum(a, axis=-1, keepdims=True) for a in e]


def attention_fwd(q, k, v, bias):
    def body(q_ref, k_ref, v_ref, b_ref, o_ref):
        step = pl.program_id(1)
        kw = _pairs(_key_window(k_ref, step))
        vw = _pairs(_key_window(v_ref, step))
        q2 = [_stack_chunks(a) for a in _pairs(q_ref[...])]
        p = _att_probs(q2, kw, _window_bias(b_ref), step)
        o_ref[...] = jnp.concatenate([_unstack_chunks(_bdot(a, b)) for a, b in zip(p, vw)], axis=-1)

    qblk = pl.BlockSpec((ATT_Q * L, W), lambda b, c: (b * ATT_STEPS + c, 0))
    kblk = pl.BlockSpec((SEQ, W), lambda b, c: (b, 0))
    return pl.pallas_call(
        body, grid=(NSEQ, ATT_STEPS), name="attention_fwd",
        in_specs=[qblk, kblk, kblk, _const_spec(ATT_WINDOW_BIAS_SHAPE)],
        out_specs=qblk, out_shape=jax.ShapeDtypeStruct((T, W), F32),
        compiler_params=_cparams(("parallel", "arbitrary")),
    )(q, k, v, bias)


def attention_bwd(q, k, v, bias, do):
    def body(q_ref, k_ref, v_ref, b_ref, do_ref, dq_ref, dko_ref, dvo_ref, db_ref, dk_ref, dv_ref):
        b = pl.program_id(0)
        c = pl.program_id(1)

        @pl.when(c == 0)
        def _():
            dk_ref[...] = jnp.zeros_like(dk_ref)
            dv_ref[...] = jnp.zeros_like(dv_ref)

        @pl.when((c == 0) & (b == 0))
        def _():
            db_ref[...] = jnp.zeros_like(db_ref)

        start = pl.multiple_of(c * (ATT_Q * L), L)
        kw = _pairs(_key_window(k_ref, c))
        vw = _pairs(_key_window(v_ref, c))
        q2 = [_stack_chunks(a) for a in _pairs(q_ref[...])]
        do2 = [_stack_chunks(a) for a in _pairs(do_ref[...].astype(BF16))]
        p = _att_probs(q2, kw, _window_bias(b_ref), c)
        dp = [_bdot_nt(a, b) for a, b in zip(do2, vw)]
        ds = [a * (d - jnp.sum(d * a, axis=-1, keepdims=True)) for a, d in zip(p, dp)]
        dss = [(a * ATT_SCALE).astype(BF16) for a in ds]
        dq_ref[...] = jnp.concatenate([_unstack_chunks(_bdot(a, b)) for a, b in zip(dss, kw)], axis=-1).astype(BF16)
        dk_ref[pl.ds(start, WIN), :] += jnp.concatenate([_bdot_tn(a, b) for a, b in zip(dss, q2)], axis=-1)
        dv_ref[pl.ds(start, WIN), :] += jnp.concatenate([_bdot_tn(a, b) for a, b in zip(p, do2)], axis=-1)
        for i in range(NPAIR):
            db_ref[i] += ds[i]

        @pl.when(c == ATT_STEPS - 1)
        def _():
            dko_ref[...] = dk_ref[LEFT * L:, :].astype(BF16)
            dvo_ref[...] = dv_ref[LEFT * L:, :].astype(BF16)

    qblk = pl.BlockSpec((ATT_Q * L, W), lambda b, c: (b * ATT_STEPS + c, 0))
    sblk = pl.BlockSpec((SEQ, W), lambda b, c: (b, 0))
    bblk = _const_spec(ATT_BIAS_SHAPE)
    return pl.pallas_call(
        body, grid=(NSEQ, ATT_STEPS), name="attention_bwd",
        in_specs=[qblk, sblk, sblk, _const_spec(ATT_WINDOW_BIAS_SHAPE), qblk],
        out_specs=[qblk, sblk, sblk, bblk],
        out_shape=[jax.ShapeDtypeStruct((T, W), BF16), jax.ShapeDtypeStruct((T, W), BF16),
                   jax.ShapeDtypeStruct((T, W), BF16), jax.ShapeDtypeStruct(ATT_BIAS_SHAPE, F32)],
        scratch_shapes=[pltpu.VMEM((PADSEQ, W), F32), pltpu.VMEM((PADSEQ, W), F32)],
        compiler_params=_cparams(("arbitrary", "arbitrary"), VMEM_BIG),
    )(q, k, v, bias, do)


NTAB = 2 * CLIP + 1
EXT = BAND + L


def _ext_onehot():
    n = _iota2((EXT, NTAB), 0)
    m = _iota2((EXT, NTAB), 1)
    return (jnp.clip(BAND - 1 - n, -CLIP, CLIP) + CLIP == m).astype(F32)


def bias_expand(table):
    def body(t_ref, o_ref):
        ext = _hdot_nt(t_ref[...], _ext_onehot())
        ext = jnp.concatenate([ext, jnp.zeros((NH, WIN - EXT), F32)], axis=-1)
        col = _iota2((NH, WIN), 1)
        for c in range(ATT_Q):
            inside = (col >= c * L) & (col < c * L + BAND)
            for i in range(L):
                shift = (c * L - (L - 1 - i)) % WIN
                o_ref[c, :, i, :] = jnp.where(inside, pltpu.roll(ext, shift, 1) if shift else ext, NEG)

    out = pl.pallas_call(body, name="bias_expand", out_shape=jax.ShapeDtypeStruct((ATT_Q, NH, L, WIN), F32))(table)
    return out.reshape(ATT_WINDOW_BIAS_SHAPE)


def bias_grad(dbias):
    def body(d_ref, o_ref):
        acc = jnp.zeros((NH, EXT), F32)
        zpad = jnp.zeros((NH, EXT - BAND), F32)
        for i in range(L):
            s = L - 1 - i
            row = jnp.concatenate([d_ref[:, i, :], zpad], axis=-1)
            acc = acc + (pltpu.roll(row, s, 1) if s else row)
        o_ref[...] = _hdot(acc, _ext_onehot())

    return pl.pallas_call(body, name="bias_grad", out_shape=jax.ShapeDtypeStruct((NH, NTAB), F32))(dbias)


def _group_cols(g):
    return slice(g * SGC, (g + 1) * SGC)


def _sg_norm(gv, lng, lnb):
    gc = gv - jnp.mean(gv, axis=-1, keepdims=True)
    rstd = lax.rsqrt(jnp.mean(gc * gc, axis=-1, keepdims=True) + LN_EPS)
    xhat = gc * rstd
    return xhat, rstd, xhat * lng + lnb


GMLP_BWD_CHUNKS = 2


def gmlp_fwd_loss(u, v, gate, lng, lnb, wm_bf, sgb_t, h, w_bf, g_final, target):
    tm = GMLP_BWD_CHUNKS * SGC

    def body(u_ref, v_ref, gt_ref, lng_ref, lnb_ref, wm_ref, sb_ref, h_ref, w_ref, g_ref, t_ref,
             dh_ref, loss_ref, dg_ref, zt_ref):
        zs = []
        for ch in range(GMLP_BWD_CHUNKS):
            rows = slice(ch * SGC, (ch + 1) * SGC)
            _, _, vln = _sg_norm(_gelu(v_ref[rows, :]), lng_ref[...], lnb_ref[...])
            vlb = vln.astype(BF16)
            zg = []
            for g in range(NG):
                cs = _group_cols(g)
                sv = jnp.dot(wm_ref[g], vlb[:, cs], preferred_element_type=F32) + sb_ref[:, g:g + 1]
                zg.append((_gelu(u_ref[rows, cs]) * sv * _silu(gt_ref[rows, cs])).astype(BF16))
            zs.append(jnp.concatenate(zg, axis=-1))
        z = jnp.concatenate(zs, axis=0)
        zt_ref[...] = z.T
        xv = h_ref[...] + jnp.dot(z, w_ref[...], preferred_element_type=F32)
        rstd = lax.rsqrt(jnp.mean(xv * xv, axis=-1, keepdims=True) + RMS_EPS)
        xhat = xv * rstd
        err = xhat * g_ref[...] - t_ref[...]
        part = 0.5 * jnp.sum(jnp.mean(err * err, axis=-1, keepdims=True), axis=0, keepdims=True)
        dout = err * (1.0 / D)

        @pl.when(pl.program_id(0) == 0)
        def _():
            loss_ref[...] = jnp.zeros_like(loss_ref)
            dg_ref[...] = jnp.zeros_like(dg_ref)

        loss_ref[...] += jnp.broadcast_to(part, loss_ref.shape)
        dg_ref[...] += jnp.sum(dout * xhat, axis=0, keepdims=True)
        dxh = dout * g_ref[...]
        dh_ref[...] = rstd * (dxh - xhat * jnp.mean(dxh * xhat, axis=-1, keepdims=True))

    return pl.pallas_call(
        body, grid=(T // tm,), name="gmlp_fwd_loss",
        in_specs=[_row_spec(tm, D)] * 3 + [_const_spec((1, D))] * 2
        + [_const_spec((NG, SGC, SGC)), _const_spec((SGC, NG)), _row_spec(tm, D), _const_spec((D, D)),
           _const_spec((1, D)), _row_spec(tm, D)],
        out_specs=[_row_spec(tm, D), _const_spec((8, 128)), _const_spec((1, D)), _col_spec(D, tm)],
        out_shape=[jax.ShapeDtypeStruct((T, D), F32), jax.ShapeDtypeStruct((8, 128), F32),
                   jax.ShapeDtypeStruct((1, D), F32), jax.ShapeDtypeStruct((D, T), BF16)],
        compiler_params=_cparams(("arbitrary",), VMEM_BIG),
    )(u, v, gate, lng, lnb, wm_bf, sgb_t, h, w_bf, g_final, target)


def gmlp_bwd(u, v, gate, lng, lnb, wm_bf, sgb_t, dh, zt_bf, w_bf):
    def body(u_ref, v_ref, gt_ref, lng_ref, lnb_ref, wm_ref, sb_ref, dh_ref, zt_ref, w_ref,
             du_ref, dv_ref, dgt_ref, dlng_ref, dlnb_ref, dwm_ref, dsb_ref, dw_ref, acc_ref):
        @pl.when(pl.program_id(0) == 0)
        def _():
            for ref in (dlng_ref, dlnb_ref, dwm_ref, dsb_ref):
                ref[...] = jnp.zeros_like(ref)

        dz = _out_proj_back(dh_ref, zt_ref, w_ref, dw_ref, acc_ref)
        sel = (_iota2((D, NG), 0) // SGC == _iota2((D, NG), 1)).astype(F32)
        for ch in range(GMLP_BWD_CHUNKS):
            rows = slice(ch * SGC, (ch + 1) * SGC)
            gv, dgv_dv = _gelu_both(v_ref[rows, :])
            xhat, rstd, vln = _sg_norm(gv, lng_ref[...], lnb_ref[...])
            vlb = vln.astype(BF16)
            dvln = []
            dsv_all = []
            for g in range(NG):
                cs = _group_cols(g)
                uu = u_ref[rows, cs]
                gg = gt_ref[rows, cs]
                dzz = dz[rows, cs]
                sv = jnp.dot(wm_ref[g], vlb[:, cs], preferred_element_type=F32) + sb_ref[:, g:g + 1]
                gu, dgu = _gelu_both(uu)
                sg, dsg = _silu_both(gg)
                dzgu = dzz * gu
                dsv = dzgu * sg
                dgt_ref[rows, cs] = (dzgu * sv * dsg).astype(BF16)
                du_ref[rows, cs] = (dzz * sv * sg * dgu).astype(BF16)
                dsb16 = dsv.astype(BF16)
                dvln.append(lax.dot_general(wm_ref[g], dsb16, (((0,), (0,)), ((), ())), preferred_element_type=F32))
                dwm_ref[g] += lax.dot_general(dsb16, vlb[:, cs], (((1,), (1,)), ((), ())),
                                              preferred_element_type=F32)
                dsv_all.append(dsv)
            dvl = jnp.concatenate(dvln, axis=-1)
            dsb_ref[...] += _hdot(jnp.concatenate(dsv_all, axis=-1), sel)
            dlng_ref[...] += jnp.sum(dvl * xhat, axis=0, keepdims=True)
            dlnb_ref[...] += jnp.sum(dvl, axis=0, keepdims=True)
            dxh = dvl * lng_ref[...]
            dgv = rstd * (dxh - jnp.mean(dxh, axis=-1, keepdims=True)
                          - xhat * jnp.mean(dxh * xhat, axis=-1, keepdims=True))
            dv_ref[rows, :] = (dgv * dgv_dv).astype(BF16)

    tm = GMLP_BWD_CHUNKS * SGC
    return pl.pallas_call(
        body, grid=(T // tm,), name="gmlp_bwd",
        in_specs=[_row_spec(tm, D)] * 3 + [_const_spec((1, D))] * 2
        + [_const_spec((NG, SGC, SGC)), _const_spec((SGC, NG)), _row_spec(tm, D), _col_spec(D, tm),
           _const_spec((D, D))],
        out_specs=[_row_spec(tm, D)] * 3 + [_const_spec((1, D))] * 2
        + [_const_spec((NG, SGC, SGC)), _const_spec((SGC, NG)), _const_spec((D, D))],
        out_shape=[jax.ShapeDtypeStruct((T, D), BF16)] * 3 + [jax.ShapeDtypeStruct((1, D), F32)] * 2
        + [jax.ShapeDtypeStruct((NG, SGC, SGC), F32), jax.ShapeDtypeStruct((SGC, NG), F32),
           jax.ShapeDtypeStruct((D, D), BF16)],
        scratch_shapes=[pltpu.VMEM((D, D), F32)],
        compiler_params=_cparams(("arbitrary",), VMEM_BIG),
    )(u, v, gate, lng, lnb, wm_bf, sgb_t, dh, zt_bf, w_bf)


NCHIP = 4
NDEV = 8
ANY = pl.BlockSpec(memory_space=pl.ANY)


HBM = pl.BlockSpec(memory_space=pltpu.HBM)
SEM = pl.BlockSpec(memory_space=pltpu.SEMAPHORE)
EFFECT = pltpu.SideEffectType.DATAFLOW_SIDE_EFFECTING


CHIPS, EVERY, SIBLING = "chips", "every", "sibling"
SLOTS = {CHIPS: NCHIP, EVERY: NDEV, SIBLING: 1}


def _peers(scope):
    x, y, c = lax.axis_index("x"), lax.axis_index("y"), lax.axis_index("c")
    if scope == SIBLING:
        return [((x, y, 1 - c), 0)], 0
    if scope == CHIPS:
        return [((px, py, c), 2 * px + py) for px, py in ((1 - x, y), (x, 1 - y), (1 - x, 1 - y))], 2 * x + y
    out = []
    for j in range(1, NDEV):
        px, py, pc = x ^ (j >> 2), y ^ ((j >> 1) & 1), c ^ (j & 1)
        out.append(((px, py, pc), 4 * px + 2 * py + pc))
    return out, 4 * x + 2 * y + c


def _send_copies(src, land, send, recv, scatter, scope, starting):
    peers, me = _peers(scope)
    copies = []
    for t in range(len(src)):
        for j, (dev, slot) in enumerate(peers):
            k = t * len(peers) + j
            copies.append(pltpu.make_async_remote_copy(
                src_ref=src[t].at[slot] if scatter else src[t], dst_ref=land[t].at[me if starting else slot],
                send_sem=send.at[k], recv_sem=recv.at[k], device_id=dev, device_id_type=MESH))
    return copies


def _own_copies(src, land, sems, scatter, scope):
    if scope == SIBLING:
        return []
    _, me = _peers(scope)
    return [pltpu.make_async_copy(src[t].at[me] if scatter else src[t], land[t].at[me], sems.at[t])
            for t in range(len(src))]


def send_start(srcs, scatter, scope, name, after=None):
    n = len(srcs)
    slots = SLOTS[scope]
    extra_specs, extra = _after_operand(after)
    lands = [pltpu.HBM(a.shape if scatter else (slots,) + a.shape, a.dtype) for a in srcs]
    sems = [pltpu.SemaphoreType.DMA((n * max(slots - 1, 1),))] * 2 + ([] if scope == SIBLING else
                                                                     [pltpu.SemaphoreType.DMA((n,))])
    k = len(sems)

    def body(*refs):
        first_out = n + len(extra)
        src, land = refs[:n], refs[first_out + k + n:first_out + k + 2 * n]
        for cp in _send_copies(src, land, refs[first_out], refs[first_out + 1], scatter, scope, True):
            cp.start()
        for cp in _own_copies(src, land, refs[first_out + k - 1], scatter, scope):
            cp.start()
        refs[-1][...] = jnp.zeros_like(refs[-1])

    out = pl.pallas_call(
        body, name=name,
        out_shape=(*sems, *[pltpu.HBM(a.shape, a.dtype) for a in srcs], *lands, jax.ShapeDtypeStruct((8, 128), F32)),
        in_specs=[HBM] * n + extra_specs,
        out_specs=(*[SEM] * k, *[HBM] * (2 * n), pl.BlockSpec(memory_space=pltpu.VMEM)),
        input_output_aliases={i: k + i for i in range(n)},
        compiler_params=pltpu.CompilerParams(has_side_effects=EFFECT),
    )(*[pltpu.with_memory_space_constraint(a, pltpu.HBM) for a in srcs], *extra)
    return list(out[:k]), list(out[k:k + n]), list(out[k + n:k + 2 * n]), out[-1]


def send_wait(started, after, scatter, scope, name, with_sources=False):
    sems, srcs, lands, _ = started
    n, k = len(srcs), len(sems)

    def body(*refs):
        src, land = refs[:n], refs[n:2 * n]
        for cp in _own_copies(src, land, refs[2 * n + k - 1], scatter, scope):
            cp.wait()
        for cp in _send_copies(src, land, refs[2 * n], refs[2 * n + 1], scatter, scope, False):
            cp.wait_send()
            cp.wait_recv()

    arrs = list(srcs) + list(lands)
    out = pl.pallas_call(
        body, name=name, out_shape=tuple(pltpu.HBM(a.shape, a.dtype) for a in arrs),
        in_specs=[HBM] * (2 * n) + [SEM] * k + [ANY], out_specs=tuple([HBM] * (2 * n)),
        input_output_aliases={i: i for i in range(2 * n)},
        compiler_params=pltpu.CompilerParams(has_side_effects=EFFECT),
    )(*arrs, *sems, after)
    return (list(out[:n]), list(out[n:])) if with_sources else list(out[n:])


def gather_weights(arrs, split):
    n = len(arrs)

    def body(*refs):
        ins, outs = refs[:n], refs[n:2 * n]
        send1, recv1, send2, recv2, loc_in, loc_out = refs[2 * n:2 * n + 6]
        staged = refs[2 * n + 6:]
        x, y, c = lax.axis_index("x"), lax.axis_index("y"), lax.axis_index("c")
        me = 2 * x + y
        sibling = (x, y, 1 - c)
        peers = [(1 - x, y), (x, 1 - y), (1 - x, 1 - y)]

        def rows_of(t, core):
            half = arrs[t].shape[0] // 2
            return pl.ds(core * half, half)

        def part(ref, t, core):
            return ref.at[rows_of(t, core)] if split[t] else ref

        load = [pltpu.make_async_copy(ins[t], staged[t], loc_in.at[t]) for t in range(n)]
        store = [pltpu.make_async_copy(staged[t], outs[t].at[me], loc_out.at[t]) for t in range(n)]
        for cp in load:
            cp.start()
        first = []
        for t in range(n):
            for j, (px, py) in enumerate(peers):
                first.append(pltpu.make_async_remote_copy(
                    src_ref=part(ins[t], t, c), dst_ref=part(outs[t].at[me], t, c), send_sem=send1.at[t, j],
                    recv_sem=recv1.at[t, j], device_id=(px, py, c), device_id_type=MESH))
        for cp in first:
            cp.start()
        for cp_in, cp_out in zip(load, store):
            cp_in.wait()
            cp_out.start()
        passed = []
        for t in range(n):
            for j, (px, py) in enumerate(peers):
                landed = part(outs[t].at[2 * px + py], t, c)
                pltpu.make_async_remote_copy(
                    src_ref=landed, dst_ref=landed, send_sem=send1.at[t, j], recv_sem=recv1.at[t, j],
                    device_id=(x, y, c), device_id_type=MESH).wait_recv()
                if split[t]:
                    cp = pltpu.make_async_remote_copy(
                        src_ref=landed, dst_ref=landed, send_sem=send2.at[t, j], recv_sem=recv2.at[t, j],
                        device_id=sibling, device_id_type=MESH)
                    cp.start()
                    passed.append(cp)
        for t in range(n):
            for j, (px, py) in enumerate(peers):
                if split[t]:
                    other = part(outs[t].at[2 * px + py], t, 1 - c)
                    pltpu.make_async_remote_copy(
                        src_ref=other, dst_ref=other, send_sem=send2.at[t, j], recv_sem=recv2.at[t, j],
                        device_id=(x, y, c), device_id_type=MESH).wait_recv()
        for cp in first + passed:
            cp.wait_send()
        for cp in store:
            cp.wait()

    return pl.pallas_call(
        body, name="gather_weights", in_specs=[ANY] * n, out_specs=[ANY] * n,
        out_shape=[jax.ShapeDtypeStruct((NCHIP,) + a.shape, a.dtype) for a in arrs],
        scratch_shapes=[pltpu.SemaphoreType.DMA((n, 3))] * 4 + [pltpu.SemaphoreType.DMA((n,))] * 2
        + [pltpu.VMEM(a.shape, a.dtype) for a in arrs],
    )(*arrs)


def _adam_math(g, w, m, v):
    m = ADAM_B1 * m + (1.0 - ADAM_B1) * g
    v = ADAM_B2 * v + (1.0 - ADAM_B2) * (g * g)
    m_hat = m / (1.0 - ADAM_B1 ** ADAM_STEP)
    v_hat = v / (1.0 - ADAM_B2 ** ADAM_STEP)
    delta = -ADAM_LR * (m_hat / (jnp.sqrt(v_hat) + ADAM_EPS) + ADAM_WD * w)
    return delta, m, v


def _rows_tile(rows):
    return rows if rows <= 256 else 256


def sum_chips(parts, name):
    _, rows, cols = parts.shape
    tr = _rows_tile(rows)

    def body(p_ref, o_ref):
        acc = p_ref[0].astype(F32)
        for s in range(1, NCHIP):
            acc = acc + p_ref[s].astype(F32)
        o_ref[...] = acc

    return pl.pallas_call(
        body, grid=(rows // tr,), name=name,
        in_specs=[pl.BlockSpec((NCHIP, tr, cols), lambda i: (0, i, 0))],
        out_specs=pl.BlockSpec((tr, cols), lambda i: (i, 0)),
        out_shape=jax.ShapeDtypeStruct((rows, cols), F32),
        compiler_params=_cparams(("parallel",)),
    )(parts)


def sum_chips_small(parts, name):
    n = len(parts)

    def body(*refs):
        for p_ref, o_ref in zip(refs[:n], refs[n:]):
            acc = p_ref[0]
            for s in range(1, NCHIP):
                acc = acc + p_ref[s]
            o_ref[...] = acc

    return pl.pallas_call(body, name=name, out_shape=[jax.ShapeDtypeStruct(p.shape[1:], F32) for p in parts])(*parts)


def adam_shard_small(items, name):
    n = len(items)

    def body(*refs):
        for t in range(n):
            a_ref, b_ref, w_ref, m_ref, v_ref = refs[5 * t:5 * t + 5]
            g_ref, d_ref, mo_ref, vo_ref = refs[5 * n + 4 * t:5 * n + 4 * t + 4]
            g = (a_ref[...] + b_ref[...]).reshape(w_ref.shape)
            g_ref[...] = g
            d_ref[...], mo_ref[...], vo_ref[...] = _adam_math(g, w_ref[...], m_ref[...], v_ref[...])

    out = pl.pallas_call(
        body, name=name, out_shape=[jax.ShapeDtypeStruct(it[2].shape, F32) for it in items for _ in range(4)],
    )(*[a for it in items for a in it])
    return [out[4 * t:4 * t + 4] for t in range(n)]


def adam_shard(p_mine, p_sib, w, m, v, name):
    rows, cols = p_mine.shape
    tr = _rows_tile(rows)
    lead = w.ndim == 3

    def body(a_ref, b_ref, w_ref, m_ref, v_ref, g_ref, d_ref, mo_ref, vo_ref):
        g = a_ref[...] + b_ref[...]
        g = g[None] if lead else g
        g_ref[...] = g
        d_ref[...], mo_ref[...], vo_ref[...] = _adam_math(g, w_ref[...], m_ref[...], v_ref[...])

    flat = pl.BlockSpec((tr, cols), lambda i: (i, 0))
    spec = pl.BlockSpec((1, tr, cols), lambda i: (0, i, 0)) if lead else flat
    return pl.pallas_call(
        body, grid=(rows // tr,), name=name, in_specs=[flat] * 2 + [spec] * 3, out_specs=[spec] * 4,
        out_shape=[jax.ShapeDtypeStruct(w.shape, F32)] * 4,
        compiler_params=_cparams(("parallel",)),
    )(p_mine, p_sib, w, m, v)


def adam_shard_halves_t(r_mine, r_sib, wt, mt, vt, name):
    hrows, cols = r_mine.shape
    tr = _rows_tile(hrows)
    per_half = hrows // tr

    def body(a_ref, b_ref, w_ref, m_ref, v_ref, g_ref, d_ref, mo_ref, vo_ref):
        mine = pl.program_id(0) == lax.axis_index("c")
        g = jnp.where(mine, a_ref[...], b_ref[...]).T[None]
        g_ref[...] = g
        d_ref[...], mo_ref[...], vo_ref[...] = _adam_math(g, w_ref[...], m_ref[...], v_ref[...])

    flat = pl.BlockSpec((tr, cols), lambda h, i: (i, 0))
    spec = pl.BlockSpec((1, cols, tr), lambda h, i: (0, 0, h * per_half + i))
    return pl.pallas_call(
        body, grid=(2, per_half), name=name, in_specs=[flat] * 2 + [spec] * 3, out_specs=[spec] * 4,
        out_shape=[jax.ShapeDtypeStruct(wt.shape, F32)] * 4,
        compiler_params=_cparams(("parallel", "parallel")),
    )(r_mine, r_sib, wt, mt, vt)


def adam_replicated(gathered, params, name):
    flat = []
    for i, p in enumerate(params):
        if isinstance(p, list):
            off = 0
            for wmv in p:
                n = gathered[i].shape[-1] - off if wmv[0] is None else wmv[0].shape[-1]
                flat.append((i, (off, n), wmv))
                off += n
        else:
            flat.append((i, None, p))
    ins = [a for _, _, wmv in flat for a in wmv if a is not None]
    ng = len(gathered)

    def body(*refs):
        g_refs = refs[:ng]
        in_refs = list(refs[ng:ng + len(ins)])
        out_refs = list(refs[ng + len(ins):])
        sums = []
        for r in g_refs:
            g = r[0]
            for d in range(1, NDEV):
                g = g + r[d]
            sums.append(g)
        for i, lanes, wmv in flat:
            g = sums[i] if lanes is None else sums[i][:, lanes[0]:lanes[0] + lanes[1]]
            out_refs.pop(0)[...] = g
            if wmv[0] is not None:
                w_ref, m_ref, v_ref = in_refs.pop(0), in_refs.pop(0), in_refs.pop(0)
                d_ref, mo_ref, vo_ref = out_refs.pop(0), out_refs.pop(0), out_refs.pop(0)
                d_ref[...], mo_ref[...], vo_ref[...] = _adam_math(g, w_ref[...], m_ref[...], v_ref[...])

    out_shape = []
    for i, lanes, wmv in flat:
        shape = gathered[i].shape[1:] if lanes is None else (1, lanes[1])
        out_shape += [jax.ShapeDtypeStruct(shape, F32)] * (4 if wmv[0] is not None else 1)
    outs = list(pl.pallas_call(body, name=name, out_shape=out_shape)(*gathered, *ins))
    return [[outs.pop(0) for _ in range(4 if wmv[0] is not None else 1)] for _, _, wmv in flat]


EVEN_SPLITS = (SHIFT, W, W, W, W, W)
ODD_SPLITS = (D, D, D)


def _cols_to_chips(a):
    rows, cols = a.shape
    return a.reshape(rows, NCHIP, cols // NCHIP).transpose(1, 0, 2)


def _chips_to_cols(a):
    _, rows, n = a.shape
    return a.transpose(1, 0, 2).reshape(rows, NCHIP * n)


def kernel(x, norm_g, w_in_e, shift_mu, rw_w0, rw_w2, rw_a0, rw_a2, rw_kk, rw_ka, rw_rk, rw_lnx_g, rw_lnx_b, att_bias, w_out_e, w_in_o, sg_ln_g, sg_ln_b, sg_w, sg_b, w_out_o, final_g, loss_target, m_norm_g, m_w_in_e, m_shift_mu, m_rw_w0, m_rw_w2, m_rw_a0, m_rw_a2, m_rw_kk, m_rw_ka, m_rw_rk, m_rw_lnx_g, m_rw_lnx_b, m_att_bias, m_w_out_e, m_w_in_o, m_sg_ln_g, m_sg_ln_b, m_sg_w, m_sg_b, m_w_out_o, m_final_g, v_norm_g, v_w_in_e, v_shift_mu, v_rw_w0, v_rw_w2, v_rw_a0, v_rw_a2, v_rw_kk, v_rw_ka, v_rw_rk, v_rw_lnx_g, v_rw_lnx_b, v_att_bias, v_w_out_e, v_w_in_o, v_sg_ln_g, v_sg_ln_b, v_sg_w, v_sg_b, v_w_out_o, v_final_g):
    x2 = x.reshape(T, D)
    tgt = loss_target.reshape(T, D)

    gathered = gather_weights(
        [jnp.swapaxes(w_in_e[0], 0, 1).astype(BF16), jnp.concatenate([rw_w2[0], rw_a2[0]], axis=0),
         jnp.concatenate([sg_ln_g, sg_ln_b], axis=0)], [True, True, False])
    wie = gathered[0].reshape(EVEN_IN, D)
    w2 = _chips_to_cols(gathered[1][:, :LORA])
    a2 = _chips_to_cols(gathered[1][:, LORA:])
    sglg = _chips_to_cols(gathered[2][:, 0:1])
    sglb = _chips_to_cols(gathered[2][:, 1:2])

    late = [w_out_e[0].astype(BF16), w_in_o[0].astype(BF16), w_out_o[0].astype(BF16)]
    late_started = send_start(late, False, CHIPS, "late_weights_start", after=gathered[0])

    def late_weights(after):
        woe, wio, woo = send_wait(late_started, after, False, CHIPS, "late_weights_wait")
        return woe.reshape(D, D), _chips_to_cols(wio), woo.reshape(D, D)

    def scatter_start(grads, name):
        return send_start([g_.astype(BF16) if g_.shape[-1] >= W else g_ for g_ in grads], True, CHIPS, name)

    started = {}

    def on_odd_grads(d_woo, d_wio):
        started["odd"] = scatter_start([d_woo.reshape(NCHIP, D // NCHIP, D), d_wio], "odd_grads_start")
        return started["odd"][-1]

    def on_even_grads(big_g):
        d_wie_half, d_woe, _, _, d_w2, d_a2, d_sglg, d_sglb = big_g
        blocks = [d_wie_half, d_woe.reshape(NCHIP, D // NCHIP, D), _cols_to_chips(d_w2), _cols_to_chips(d_a2),
                  _cols_to_chips(d_sglg), _cols_to_chips(d_sglb)]
        started["even"] = scatter_start(blocks, "even_grads_start")
        return started["even"][-1]

    def on_small_grads(layer, grads):
        if layer == "odd":
            d_sg_w, d_sg_b, d_final, d_g1 = grads
            mine = [d_sg_w.reshape(NG * SGC, SGC), d_sg_b, jnp.concatenate([d_final, d_g1], axis=1)]
        else:
            mine = [grads[-2], jnp.concatenate(grads[:-2] + grads[-1:], axis=1)]
        started[layer + "_small"] = send_start(mine, False, EVERY, layer + "_small_grads_start")
        return started[layer + "_small"][-1]

    loss_part, dx, _, _ = _local_step(
        x2, tgt, wie, late_weights, w2, a2, sglg, sglb, norm_g, shift_mu, rw_w0, rw_a0, rw_kk, rw_ka, rw_rk,
        rw_lnx_g, rw_lnx_b, att_bias, sg_w, sg_b, final_g, first_after=late_started[-1], on_odd_grads=on_odd_grads,
        on_even_grads=on_even_grads, on_small_grads=on_small_grads)
    wmv = {"w_in_e": tuple(jnp.swapaxes(a, 1, 2) for a in (w_in_e, m_w_in_e, v_w_in_e)),
           "w_out_e": (w_out_e, m_w_out_e, v_w_out_e),
           "w_in_o": (w_in_o, m_w_in_o, v_w_in_o), "w_out_o": (w_out_o, m_w_out_o, v_w_out_o),
           "rw_w2": (rw_w2, m_rw_w2, v_rw_w2), "rw_a2": (rw_a2, m_rw_a2, v_rw_a2),
           "sg_ln_g": (sg_ln_g, m_sg_ln_g, v_sg_ln_g), "sg_ln_b": (sg_ln_b, m_sg_ln_b, v_sg_ln_b)}
    sharded = {}

    def sum_and_swap(names, landed, tag):
        nbig = sum(p_.dtype == BF16 for p_ in landed)
        partial = [sum_chips(p_, "sum_" + nm) for p_, nm in zip(landed[:nbig], names)]
        if nbig < len(names):
            partial += sum_chips_small(landed[nbig:], "sum_small_" + tag)
        return send_start(partial, False, SIBLING, "swap_partials_" + tag + "_start")

    def update(names, swap_started, after, tag):
        partial, landed = send_wait(swap_started, after, False, SIBLING, "swap_partials_" + tag + "_wait", True)
        from_sibling = [a[0] for a in landed]
        nbig = sum(p_.shape[-1] >= W for p_ in partial)
        for nm, mine, sib in zip(names[:nbig], partial, from_sibling):
            if nm == "w_in_e":
                res = adam_shard_halves_t(mine, sib, *wmv[nm], "adam_" + nm)
                sharded[nm] = [jnp.swapaxes(a, 1, 2) for a in res]
            else:
                sharded[nm] = adam_shard(mine, sib, *wmv[nm], "adam_" + nm)
        if nbig < len(names):
            items = [(mine, sib, *wmv[nm]) for nm, mine, sib in zip(names, partial, from_sibling)][nbig:]
            for nm, res in zip(names[nbig:], adam_shard_small(items, "adam_small_" + tag)):
                sharded[nm] = res

    odd_names = ["w_out_o", "w_in_o"]
    even_names = ["w_in_e", "w_out_e", "rw_w2", "rw_a2", "sg_ln_g", "sg_ln_b"]
    odd_landed = send_wait(started["odd"], started["even_small"][-1], True, CHIPS, "odd_grads_wait")
    odd_swap = sum_and_swap(odd_names, odd_landed, "odd")
    done = odd_swap[-1]

    def wmv_of(*arrs, view=lambda a: a):
        return tuple(view(a) for a in arrs)

    vec = lambda a: a.reshape(1, -1)
    groups = {
        "odd": (["sg_w", "sg_b", "final_g", "norm_g1"],
                [wmv_of(sg_w, m_sg_w, v_sg_w, view=lambda a: a.reshape(NG * SGC, SGC)),
                 wmv_of(sg_b, m_sg_b, v_sg_b, view=lambda a: a[0]),
                 [wmv_of(final_g, m_final_g, v_final_g, view=vec),
                  wmv_of(norm_g, m_norm_g, v_norm_g, view=lambda a: a[1:2])]]),
        "even": (["att_bias", "norm_g0", "shift_mu", "rw_w0", "rw_a0", "rw_kk", "rw_ka", "rw_rk", "rw_lnx_g",
                  "rw_lnx_b", "loss"],
                 [wmv_of(att_bias, m_att_bias, v_att_bias, view=lambda a: a[0]),
                  [wmv_of(norm_g, m_norm_g, v_norm_g, view=lambda a: a[0:1]),
                   wmv_of(shift_mu, m_shift_mu, v_shift_mu), wmv_of(rw_w0, m_rw_w0, v_rw_w0),
                   wmv_of(rw_a0, m_rw_a0, v_rw_a0), wmv_of(rw_kk, m_rw_kk, v_rw_kk), wmv_of(rw_ka, m_rw_ka, v_rw_ka),
                   wmv_of(rw_rk, m_rw_rk, v_rw_rk, view=vec), wmv_of(rw_lnx_g, m_rw_lnx_g, v_rw_lnx_g),
                   wmv_of(rw_lnx_b, m_rw_lnx_b, v_rw_lnx_b), (None, None, None)]]),
    }
    rep = {}
    for layer in ("odd", "even"):
        nms, params = groups[layer]
        gathered_g = send_wait(started[layer + "_small"], done, False, EVERY, layer + "_small_grads_wait")
        for nm, res in zip(nms, adam_replicated(gathered_g, params, "adam_" + layer + "_small")):
            rep[nm] = res
        done = rep[nms[0]][0]
    native = {"sg_w": sg_w.shape, "sg_b": sg_b.shape, "final_g": final_g.shape, "rw_rk": rw_rk.shape,
              "att_bias": att_bias.shape}
    for nm, shape in native.items():
        rep[nm] = [a.reshape(shape) for a in rep[nm]]
    rep["norm_g"] = [jnp.concatenate([a, b], axis=0) for a, b in zip(rep["norm_g0"], rep["norm_g1"])]
    even_landed = send_wait(started["even"], done, True, CHIPS, "even_grads_wait")
    even_swap = sum_and_swap(even_names, even_landed, "even")
    update(odd_names, odd_swap, even_swap[-1], "odd")
    update(even_names, even_swap, sharded["w_in_o"][0], "even")

    order = ["norm_g", "w_in_e", "shift_mu", "rw_w0", "rw_w2", "rw_a0", "rw_a2", "rw_kk", "rw_ka", "rw_rk",
             "rw_lnx_g", "rw_lnx_b", "att_bias", "w_out_e", "w_in_o", "sg_ln_g", "sg_ln_b", "sg_w", "sg_b",
             "w_out_o", "final_g"]
    results = {**sharded, **rep}
    outs = [rep["loss"][0][0, 0], dx.reshape(NSEQ, SEQ, D)]
    for kind in range(4):
        outs += [results[nm][kind] for nm in order]
    return tuple(outs)


def _local_step(x2, tgt, wie_t, late_weights, w2, a2, sglg, sglb, norm_g, shift_mu, rw_w0, rw_a0, rw_kk, rw_ka, rw_rk,
                rw_lnx_g, rw_lnx_b, att_bias, sg_w, sg_b, final_g, first_after=None, on_odd_grads=None,
                on_even_grads=None, on_small_grads=None):
    zl = jnp.zeros((LORA, W), F32)
    w2x = jnp.concatenate([w2, zl], axis=0)
    a2x = jnp.concatenate([zl, a2], axis=0)
    rk = rw_rk.reshape(1, W)
    pos = np.arange(SGC)
    sg_mask = jnp.asarray(((pos[None, :] // L) <= (pos[:, None] // L)).astype(np.float32))
    wm = (sg_w[0] * sg_mask[None]).astype(BF16)
    sgb_t = sg_b[0].T

    xn0, ps, ga, q, kb, vb, gb = ln_in_proj(x2, norm_g[0:1], wie_t, EVEN_SPLITS, "in_proj_even", after=first_after,
                                            w_t=True, bf16_pieces=(2, 3, 4))
    r, lw, k2, v, aa, bb = even_prep(ps, shift_mu, rw_w0, w2x, rw_a0, a2x, rw_kk, rw_ka)
    y, rw_saved = rwkv_fwd(r, lw, k2, v, aa, bb)
    bias = bias_expand(att_bias[0])

    o = attention_fwd(q, kb, vb, bias)
    woe, wio, woo = late_weights(o)
    h1, zt = even_post(y, r, k2, v, ga, o, gb, rw_lnx_g, rw_lnx_b, rk, x2, woe)
    xn1, u, vv, gt = ln_in_proj(h1, norm_g[1:2], wio, ODD_SPLITS, "in_proj_odd")
    dh2, loss_part, d_final_g, z2t = gmlp_fwd_loss(u, vv, gt, sglg, sglb, wm, sgb_t, h1, woo, final_g[None], tgt)

    du, dvv, dgt, d_sglg, d_sglb, d_wm, d_sgb_t, d_woo = gmlp_bwd(u, vv, gt, sglg, sglb, wm, sgb_t, dh2, z2t, woo)
    dp_odd = [du, dvv, dgt]
    d_wio = matmul_acc_chips(xn1, dp_odd, "in_proj_odd_dw")
    token = on_odd_grads(d_woo, d_wio) if on_odd_grads else None
    dh1, d_g1 = in_proj_bwd_x(h1, norm_g[1:2], wio, dp_odd, dh2, "in_proj_odd_bwd", after=token)
    odd_small = [d_wm * sg_mask[None], d_sgb_t.T, d_final_g, d_g1]
    token = on_small_grads("odd", odd_small) if on_small_grads else None
    dy, dr2, dk22, dv2, dga, do, dgb, d_lng, d_lnb, d_rk, d_woe = even_post_bwd(
        y, r, k2, v, ga, o, gb, rw_lnx_g, rw_lnx_b, rk, dh1, zt, woe, after=token)
    dq, dkb, dvb, dbias = attention_bwd(q, kb, vb, bias, do)
    dbias = sum(dbias[:, i * 2 * L:(i + 1) * 2 * L, i * L:i * L + BAND] for i in range(ATT_Q))
    d_att_bias = bias_grad(dbias.reshape(NH, L, BAND))
    dr, dlw, dk2, dv, daa, dbb = rwkv_bwd(r, lw, k2, aa, bb, rw_saved, dy)
    dps, d_mu, d_w0, d_w2x, d_a0, d_a2x, d_kk, d_ka = even_prep_bwd(
        ps, shift_mu, rw_w0, w2x, rw_a0, a2x, rw_kk, rw_ka, dr, dlw, dk2, dv, daa, dbb, dr2, dk22, dv2)
    dp_even = [dps, dga, dq, dkb, dvb, dgb]
    d_wie = matmul_acc_chips(xn0, dp_even, "in_proj_even_dw", add_cores=on_even_grads is not None)
    big_g = (d_wie, d_woe, d_wio, d_woo, d_w2x[:LORA], d_a2x[LORA:], d_sglg, d_sglb)
    token = on_even_grads(big_g) if on_even_grads else None
    dx, d_g0 = in_proj_bwd_x(x2, norm_g[0:1], wie_t, dp_even, dh1, "in_proj_even_bwd", after=token, w_t=True)
    even_small = [d_g0, d_mu, d_w0, d_a0, d_kk, d_ka, d_rk, d_lng, d_lnb, d_att_bias]
    if on_small_grads:
        on_small_grads("even", even_small + [loss_part[0:1, :]])
    rep_g = [jnp.concatenate([d_g0, d_g1], axis=0)] + even_small[1:] + odd_small[:3]
    return loss_part[0, 0], dx, big_g, rep_g
```

```python
import functools
import math

import jax
import jax.numpy as jnp
import numpy as np
from jax import lax
from jax.experimental import pallas as pl
from jax.experimental.pallas import tpu as pltpu

F32 = jnp.float32
BF16 = jnp.bfloat16
HI = lax.Precision.HIGHEST

D = 1024
SEQ = 2048
NSEQ = 2
T = NSEQ * SEQ
HD = 64
NH = 8
W = 512
SHIFT = 1664
LORA = 64
EVEN_IN = 4224
ODD_IN = 3072
L = 64
NC = SEQ // L
LEFT = 8
BAND = (LEFT + 1) * L
CLIP = 128
SGC = 128
NG = 8
RMS_EPS = 1e-6
LN_EPS = 1e-5
GN_EPS = 64e-5
NEG = -1e30
VMEM_BIG = 56 * 1024 * 1024

ADAM_LR = 0.001
ADAM_B1 = 0.9
ADAM_B2 = 0.999
ADAM_EPS = 1e-08
ADAM_WD = 0.01
ADAM_STEP = 10

MESH = pl.DeviceIdType.MESH


def _bdot(a, b):
    return jnp.dot(a.astype(BF16), b.astype(BF16), preferred_element_type=F32)


def _bdot_nt(a, b):
    return lax.dot_general(a.astype(BF16), b.astype(BF16), (((1,), (1,)), ((), ())), preferred_element_type=F32)


def _bdot_tn(a, b):
    return lax.dot_general(a.astype(BF16), b.astype(BF16), (((0,), (0,)), ((), ())), preferred_element_type=F32)


def _hdot(a, b):
    return jnp.dot(a, b, precision=HI, preferred_element_type=F32)


def _hdot_nt(a, b):
    return lax.dot_general(a, b, (((1,), (1,)), ((), ())), precision=HI, preferred_element_type=F32)


def _hdot_tn(a, b):
    return lax.dot_general(a, b, (((0,), (0,)), ((), ())), precision=HI, preferred_element_type=F32)


def _iota2(shape, dim):
    return lax.broadcasted_iota(jnp.int32, shape, dim)


def _head_blockdiag():
    r = _iota2((2 * HD, 2 * HD), 0) // HD
    c = _iota2((2 * HD, 2 * HD), 1) // HD
    return (r == c).astype(BF16)


def _headsum_impl(x, bd):
    hi = x.astype(BF16)
    mid = (x - hi.astype(F32)).astype(BF16)
    n = bd.shape[0]
    out = [jnp.dot(hi[:, i:i + n], bd, preferred_element_type=F32) + jnp.dot(mid[:, i:i + n], bd, preferred_element_type=F32)
           for i in range(0, x.shape[1], n)]
    return jnp.concatenate(out, axis=-1)


@jax.custom_vjp
def _headsum(x, bd):
    return _headsum_impl(x, bd)


def _headsum_fwd(x, bd):
    return _headsum_impl(x, bd), bd


def _headsum_bwd(bd, ct):
    return _headsum_impl(ct, bd), None


_headsum.defvjp(_headsum_fwd, _headsum_bwd)


def _silu(x):
    return x * jax.nn.sigmoid(x)


_GELU_C = math.sqrt(2.0 / math.pi)


def _gelu(x):
    return 0.5 * x * (1.0 + jnp.tanh(_GELU_C * (x + 0.044715 * (x * x * x))))


def _silu_both(x):
    s = jax.nn.sigmoid(x)
    xs = x * s
    return xs, s + xs * (1.0 - s)


def _gelu_both(x):
    x2 = x * x
    t = jnp.tanh(_GELU_C * (x + 0.044715 * (x2 * x)))
    half = 0.5 * (1.0 + t)
    return x * half, half + 0.5 * x * (1.0 - t * t) * _GELU_C * (1.0 + 3.0 * 0.044715 * x2)


def _softplus(x):
    return jnp.maximum(x, 0.0) + jnp.log(1.0 + jnp.exp(-jnp.abs(x)))


def _cparams(sem, vmem=None):
    return pltpu.CompilerParams(dimension_semantics=sem, vmem_limit_bytes=vmem)


def _row_spec(tm, width):
    return pl.BlockSpec((tm, width), lambda i: (i, 0))


def _col_spec(height, tm):
    return pl.BlockSpec((height, tm), lambda i: (0, i))


def _const_spec(shape):
    nd = len(shape)
    return pl.BlockSpec(shape, lambda *_: (0,) * nd)


def _weight_dims(w_bf, w_t):
    return (((1,), (1,)), ((), ())) if w_t else (((1,), (0,)), ((), ())), w_bf.shape[0 if w_t else 1]


def ln_in_proj(x, g, w_bf, splits, name, after=None, w_t=False, bf16_pieces=()):
    dims, n = _weight_dims(w_bf, w_t)
    dtypes = [BF16 if i in bf16_pieces else F32 for i in range(len(splits))]
    tm = 512 if n <= ODD_IN else 256
    spans = []
    o = 0
    for s in splits:
        spans.append((o, o + s))
        o += s
    assert o == n
    extra_specs, extra = _after_operand(after)

    def body(x_ref, g_ref, w_ref, *rest):
        xn_ref, outs = rest[len(extra)], rest[len(extra) + 1:]
        xv = x_ref[...]
        rstd = lax.rsqrt(jnp.mean(xv * xv, axis=-1, keepdims=True) + RMS_EPS)
        xn = (xv * rstd * g_ref[...]).astype(BF16)
        xn_ref[...] = xn.T
        p = lax.dot_general(xn, w_ref[...], dims, preferred_element_type=F32)
        for o_ref, (a, b) in zip(outs, spans):
            o_ref[...] = p[:, a:b].astype(o_ref.dtype)

    return pl.pallas_call(
        body, grid=(T // tm,), name=name,
        in_specs=[_row_spec(tm, D), _const_spec((1, D)), _const_spec(w_bf.shape)] + extra_specs,
        out_specs=[_col_spec(D, tm)] + [_row_spec(tm, s) for s in splits],
        out_shape=[jax.ShapeDtypeStruct((D, T), BF16)]
        + [jax.ShapeDtypeStruct((T, s), dt) for s, dt in zip(splits, dtypes)],
        compiler_params=_cparams(("parallel",), VMEM_BIG),
    )(x, g, w_bf, *extra)


def in_proj_bwd_x(x, g, w_bf, dps, dres, name, after=None, w_t=False):
    tm = 512
    back = (((1,), (0,)), ((), ())) if w_t else (((1,), (1,)), ((), ()))
    widths = [d.shape[1] for d in dps]
    extra_specs, extra = _after_operand(after)

    def body(x_ref, g_ref, w_ref, dres_ref, *rest):
        dp_refs = rest[:len(widths)]
        dx_ref, dg_ref = rest[-2:]
        dp = jnp.concatenate([r[...] for r in dp_refs], axis=-1)
        dxn = lax.dot_general(dp, w_ref[...], back, preferred_element_type=F32)
        xv = x_ref[...]
        rstd = lax.rsqrt(jnp.mean(xv * xv, axis=-1, keepdims=True) + RMS_EPS)
        xhat = xv * rstd
        dgp = jnp.sum(dxn * xhat, axis=0, keepdims=True)

        @pl.when(pl.program_id(0) == 0)
        def _():
            dg_ref[...] = jnp.zeros_like(dg_ref)

        dg_ref[...] += dgp
        dxh = dxn * g_ref[...]
        dx_ref[...] = dres_ref[...] + rstd * (dxh - xhat * jnp.mean(dxh * xhat, axis=-1, keepdims=True))

    return pl.pallas_call(
        body, grid=(T // tm,), name=name,
        in_specs=[_row_spec(tm, D), _const_spec((1, D)), _const_spec(w_bf.shape), _row_spec(tm, D)]
        + [_row_spec(tm, s) for s in widths] + extra_specs,
        out_specs=[_row_spec(tm, D), _const_spec((1, D))],
        out_shape=[jax.ShapeDtypeStruct((T, D), F32), jax.ShapeDtypeStruct((1, D), F32)],
        compiler_params=_cparams(("arbitrary",), VMEM_BIG),
    )(x, g, w_bf, dres, *dps, *extra)


def _after_operand(after):
    return ([ANY], [after]) if after is not None else ([], [])


def matmul_acc_chips(at_bf, pieces, name, after=None, add_cores=False):
    k = at_bf.shape[0]
    widths = [p.shape[1] for p in pieces]
    nb = sum(widths) // NCHIP
    tm = 512
    steps = T // tm
    half = k // 2
    extra_specs, extra = _after_operand(after)

    def body(a_ref, *rest):
        o_ref, acc = rest[len(widths) + len(extra):][:2]

        @pl.when(pl.program_id(0) == 0)
        def _():
            acc[...] = jnp.zeros_like(acc)

        a = a_ref[...]
        b = jnp.concatenate([r[...] for r in rest[:len(widths)]], axis=-1)
        for s in range(NCHIP):
            acc[s] += jnp.dot(a, b[:, s * nb:(s + 1) * nb], preferred_element_type=F32)

        @pl.when(pl.program_id(0) == steps - 1)
        def _():
            if not add_cores:
                o_ref[...] = acc[...].astype(BF16)
            else:
                give, got, send, recv = rest[-4:]
                x, y, c = lax.axis_index("x"), lax.axis_index("y"), lax.axis_index("c")
                theirs = pl.multiple_of((1 - c) * half, half)
                mine = pl.multiple_of(c * half, half)
                give[...] = acc[:, pl.ds(theirs, half), :].astype(BF16)
                cp = pltpu.make_async_remote_copy(src_ref=give, dst_ref=got, send_sem=send, recv_sem=recv,
                                                  device_id=(x, y, 1 - c), device_id_type=MESH)
                cp.start()
                cp.wait()
                o_ref[...] = (acc[:, pl.ds(mine, half), :] + got[...].astype(F32)).astype(BF16)

    out_rows = half if add_cores else k
    exchange = [pltpu.VMEM((NCHIP, half, nb), BF16)] * 2 + [pltpu.SemaphoreType.DMA] * 2 if add_cores else []
    return pl.pallas_call(
        body, grid=(steps,), name=name,
        in_specs=[_col_spec(k, tm)] + [_row_spec(tm, w_) for w_ in widths] + extra_specs,
        out_specs=_const_spec((NCHIP, out_rows, nb)),
        out_shape=jax.ShapeDtypeStruct((NCHIP, out_rows, nb), BF16),
        scratch_shapes=[pltpu.VMEM((NCHIP, k, nb), F32)] + exchange,
        compiler_params=_cparams(("arbitrary",), VMEM_BIG),
    )(at_bf, *pieces, *extra)


def _out_proj_back(dh_ref, zt_ref, w_ref, dw_ref, acc_ref):
    dhb = dh_ref[...].astype(BF16)

    @pl.when(pl.program_id(0) == 0)
    def _():
        acc_ref[...] = jnp.zeros_like(acc_ref)

    acc_ref[...] += jnp.dot(zt_ref[...], dhb, preferred_element_type=F32)

    @pl.when(pl.program_id(0) == pl.num_programs(0) - 1)
    def _():
        dw_ref[...] = acc_ref[...].astype(dw_ref.dtype)

    return lax.dot_general(dhb, w_ref[...], (((1,), (1,)), ((), ())), preferred_element_type=F32)


PREP_TM = 512
PREP_NB = SEQ // PREP_TM


def _prep_elem(k, wl, apre, kkw, kaw, bd):
    wraw = -_softplus(-wl) - 0.5
    lw = -jnp.exp(wraw)
    asig = jax.nn.sigmoid(apre)
    kkr = k * kkw
    nrm = jnp.maximum(jnp.sqrt(_headsum(kkr * kkr, bd)), 1e-12)
    kk = kkr / nrm
    k2 = k * (1.0 + (asig - 1.0) * kaw)
    return lw, k2, -kk, kk * asig


def _prep_elem_bwd(k, wl, apre, kkw, kaw, bd, dlw, dk2, daa, dbb):
    s = -wl
    sp = _softplus(s)
    dwl = dlw * (-jnp.exp(-sp - 0.5)) * jnp.exp(s - sp)
    asig = jax.nn.sigmoid(apre)
    kkr = k * kkw
    root = jnp.sqrt(_headsum(kkr * kkr, bd))
    inv = 1.0 / jnp.maximum(root, 1e-12)
    kk = kkr * inv
    dkk = dbb * asig - daa
    dap = (dbb * kk + dk2 * k * kaw) * asig * (1.0 - asig)
    through_norm = jnp.where(root > 1e-12, kk * _headsum(dkk * kkr, bd) * inv, 0.0)
    dkkr = inv * (dkk - through_norm)
    gain = 1.0 + (asig - 1.0) * kaw
    dk = dkkr * kkw + dk2 * gain
    dkkw = jnp.sum(dkkr * k, axis=0, keepdims=True)
    dkaw = jnp.sum(dk2 * k * (asig - 1.0), axis=0, keepdims=True)
    return dk, dwl, dap, dkkw, dkaw


def _shifted(ps_ref, prev_ref, mu, blk):
    p = ps_ref[...]
    first = (blk % PREP_NB) == 0
    prev_row = jnp.where(first, 0.0, prev_ref[7:8, :])
    rolled = pltpu.roll(p, 1, 0)
    p_prev = jnp.where(_iota2(p.shape, 0) == 0, prev_row, rolled)
    return p, p_prev, p + (p_prev - p) * mu


def _prev_spec(width, blk_of):
    return pl.BlockSpec((8, width), lambda i: (jnp.maximum(blk_of(i) * (PREP_TM // 8) - 1, 0), 0))


def even_prep(ps, mu, w0, w2x, a0, a2x, kkw, kaw):
    tm = PREP_TM

    def body(ps_ref, prev_ref, mu_ref, w0_ref, w2_ref, a0_ref, a2_ref, kk_ref, ka_ref,
             r_ref, lw_ref, k2_ref, v_ref, aa_ref, bb_ref):
        _, _, s = _shifted(ps_ref, prev_ref, mu_ref[...], pl.program_id(0))
        wa = s[:, 3 * W:]
        wl = w0_ref[...] + _bdot(jnp.tanh(wa), w2_ref[...])
        apre = a0_ref[...] + _bdot(wa, a2_ref[...])
        lw, k2, aa, bb = _prep_elem(s[:, W:2 * W], wl, apre, kk_ref[...], ka_ref[...], _head_blockdiag())
        r_ref[...] = s[:, 0:W]
        v_ref[...] = s[:, 2 * W:3 * W]
        lw_ref[...] = lw
        k2_ref[...] = k2
        aa_ref[...] = aa
        bb_ref[...] = bb

    vec = _const_spec((1, W))
    return pl.pallas_call(
        body, grid=(T // tm,), name="even_prep",
        in_specs=[_row_spec(tm, SHIFT), _prev_spec(SHIFT, lambda i: i), _const_spec((1, SHIFT)), vec,
                  _const_spec((2 * LORA, W)), vec, _const_spec((2 * LORA, W)), vec, vec],
        out_specs=[_row_spec(tm, W)] * 6,
        out_shape=[jax.ShapeDtypeStruct((T, W), F32)] * 6,
        compiler_params=_cparams(("parallel",), VMEM_BIG),
    )(ps, ps, mu, w0, w2x, a0, a2x, kkw, kaw)


def even_prep_bwd(ps, mu, w0, w2x, a0, a2x, kkw, kaw, dr, dlw, dk2, dv, daa, dbb, dr2, dk22, dv2):
    tm = PREP_TM
    nb = T // tm
    rev = lambda i: nb - 1 - i

    def body(ps_ref, prev_ref, mu_ref, w0_ref, w2_ref, a0_ref, a2_ref, kk_ref, ka_ref,
             dr_ref, dlw_ref, dk2_ref, dv_ref, daa_ref, dbb_ref, dr2_ref, dk22_ref, dv2_ref,
             dps_ref, dmu_ref, dw0_ref, dw2_ref, da0_ref, da2_ref, dkk_ref, dka_ref, carry):
        i = pl.program_id(0)
        blk = rev(i)
        mu_v = mu_ref[...]
        p, p_prev, s = _shifted(ps_ref, prev_ref, mu_v, blk)
        wa = s[:, 3 * W:]
        th = jnp.tanh(wa)
        wl = w0_ref[...] + _bdot(th, w2_ref[...])
        apre = a0_ref[...] + _bdot(wa, a2_ref[...])
        bd = _head_blockdiag()
        k = s[:, W:2 * W]
        dk, dwl, dap, dkkw, dkaw = _prep_elem_bwd(k, wl, apre, kk_ref[...], ka_ref[...], bd, dlw_ref[...],
                                                  dk2_ref[...] + dk22_ref[...], daa_ref[...], dbb_ref[...])
        dwa = _bdot_nt(dwl, w2_ref[...]) * (1.0 - th * th) + _bdot_nt(dap, a2_ref[...])
        ds = jnp.concatenate([dr_ref[...] + dr2_ref[...], dk, dv_ref[...] + dv2_ref[...], dwa], axis=-1)

        @pl.when(i == 0)
        def _():
            for ref in (dmu_ref, dw0_ref, dw2_ref, da0_ref, da2_ref, dkk_ref, dka_ref, carry):
                ref[...] = jnp.zeros_like(ref)

        dmu_ref[...] += jnp.sum(ds * (p_prev - p), axis=0, keepdims=True)
        dw0_ref[...] += jnp.sum(dwl, axis=0, keepdims=True)
        da0_ref[...] += jnp.sum(dap, axis=0, keepdims=True)
        dw2_ref[...] += _bdot_tn(th, dwl)
        da2_ref[...] += _bdot_tn(wa, dap)
        dkk_ref[...] += dkkw
        dka_ref[...] += dkaw
        dsm = ds * mu_v
        last = (blk % PREP_NB) == PREP_NB - 1
        nxt = jnp.where(last, 0.0, carry[0:1, :])
        up = pltpu.roll(dsm, tm - 1, 0)
        up = jnp.where(_iota2(up.shape, 0) == tm - 1, nxt, up)
        dps_ref[...] = (ds - dsm + up).astype(BF16)
        carry[0:1, :] = dsm[0:1, :]

    vec = _const_spec((1, W))
    rrow = lambda width: pl.BlockSpec((tm, width), lambda i: (rev(i), 0))
    return pl.pallas_call(
        body, grid=(nb,), name="even_prep_bwd",
        in_specs=[rrow(SHIFT), _prev_spec(SHIFT, rev), _const_spec((1, SHIFT)), vec,
                  _const_spec((2 * LORA, W)), vec, _const_spec((2 * LORA, W)), vec, vec] + [rrow(W)] * 9,
        out_specs=[rrow(SHIFT), _const_spec((1, SHIFT)), vec, _const_spec((2 * LORA, W)), vec,
                   _const_spec((2 * LORA, W)), vec, vec],
        out_shape=[jax.ShapeDtypeStruct((T, SHIFT), BF16), jax.ShapeDtypeStruct((1, SHIFT), F32),
                   jax.ShapeDtypeStruct((1, W), F32), jax.ShapeDtypeStruct((2 * LORA, W), F32),
                   jax.ShapeDtypeStruct((1, W), F32), jax.ShapeDtypeStruct((2 * LORA, W), F32),
                   jax.ShapeDtypeStruct((1, W), F32), jax.ShapeDtypeStruct((1, W), F32)],
        scratch_shapes=[pltpu.VMEM((8, SHIFT), F32)],
        compiler_params=_cparams(("arbitrary",), VMEM_BIG),
    )(ps, ps, mu, w0, w2x, a0, a2x, kkw, kaw, dr, dlw, dk2, dv, daa, dbb, dr2, dk22, dv2)


NPAIR = NH // 2
PW = 2 * HD


def _pair_cols(p):
    return slice(p * PW, (p + 1) * PW)


def _pairs(a):
    return [a[:, _pair_cols(p)] for p in range(NPAIR)]


def _stack_pair(a):
    first = _iota2(a.shape, 1) < HD
    zero = jnp.zeros_like(a)
    return jnp.concatenate([jnp.where(first, a, zero), jnp.where(first, zero, a)], axis=0)


def _unstack_pair(a):
    n = a.shape[0] // 2
    return jnp.where(_iota2((n, PW), 1) < HD, a[:n], a[n:])


def _fold_pair(a):
    n = a.shape[0] // 2
    return a[:n] + a[n:]


def _chunk_masks():
    n = 4 * L
    row = _iota2((n, n), 0)
    col = _iota2((n, n), 1)
    same = ((row // L) & 1) == ((col // L) & 1)
    ri = row & (L - 1)
    ci = col & (L - 1)
    keep = same & (((row < 2 * L) & (ri > ci)) | ((row >= 2 * L) & (ri >= ci)))
    r1 = _iota2((L, L), 0)
    c1 = _iota2((L, L), 1)
    r2 = _iota2((2 * L, 2 * L), 0)
    c2 = _iota2((2 * L, 2 * L), 1)
    return keep.astype(F32), (r1 >= c1).astype(F32), (r2 == c2).astype(F32)


def _scaled(r, lw, k2, aa, bb, tri):
    g = _hdot(tri, lw)
    eg = jnp.exp(g)
    eng = jnp.exp(-g)
    egp = jnp.exp(g - lw)
    return eg, eng, egp, aa * egp, r * eg, bb * eng, k2 * eng


def _head_cols(h):
    return slice(h * HD, (h + 1) * HD)


def _per_head(a):
    return [a[:, _head_cols(h)] for h in range(NH)]


def _pairs_operands(at, rt, bt, kt):
    x = [jnp.concatenate([_stack_pair(a), _stack_pair(r)], axis=0).astype(BF16) for a, r in zip(_pairs(at), _pairs(rt))]
    yk = [jnp.concatenate([_stack_pair(b), _stack_pair(k)], axis=0).astype(BF16) for b, k in zip(_pairs(bt), _pairs(kt))]
    return x, yk


def _pairs_matrices(x, yk, keep, eye):
    m = [_bdot_nt(a, b) * keep for a, b in zip(x, yk)]
    p = [a[:2 * L, :2 * L] for a in m]
    tinv = [eye + a for a in p]
    for _ in range(5):
        p = [_bdot(a, a) for a in p]
        tinv = [t + _bdot(t, a) for t, a in zip(tinv, p)]
    return [a.astype(BF16) for a in m], [a.astype(BF16) for a in tinv]


def _pairs_fwd(x, yk, m, tinv, vw, s0, egl):
    xh = [_bdot_nt(a, s) for a, s in zip(x, s0)]
    u = [_bdot(t, h[:2 * L] + _bdot(a[:2 * L, 2 * L:], w)) for t, h, a, w in zip(tinv, xh, m, vw)]
    uv = [jnp.concatenate([a, w], axis=0).astype(BF16) for a, w in zip(u, vw)]
    y = [h[2 * L:] + _bdot(a[2 * L:], w) for h, a, w in zip(xh, m, uv)]
    sn = [e * (s + _bdot_tn(w, b)) for e, s, w, b in zip(egl, s0, uv, yk)]
    return y, sn, uv


def _pairs_bwd(x, yk, m, tinv, uv, s0, sn, egl, dyw, dsn, keep):
    dzs = [d * e for d, e in zip(dsn, egl)]
    dgl = [jnp.sum(d * s, axis=0, keepdims=True) for d, s in zip(dsn, sn)]
    dyb = [a.astype(BF16) for a in dyw]
    t1 = [_bdot_tn(a[2 * L:], d) for a, d in zip(m, dyb)]
    t2 = [_bdot_nt(b, d) for b, d in zip(yk, dzs)]
    drhs = [_bdot_tn(t, a[:2 * L] + b[:2 * L]) for t, a, b in zip(tinv, t1, t2)]
    dv = [a[2 * L:] + b[2 * L:] + _bdot_tn(c[:2 * L, 2 * L:], d) for a, b, c, d in zip(t1, t2, m, drhs)]
    gg = [jnp.concatenate([a, b], axis=0).astype(BF16) for a, b in zip(drhs, dyw)]
    ds0 = [d + _bdot_tn(g, a) for d, g, a in zip(dzs, gg, x)]
    dm = [_bdot_nt(g, w) * keep for g, w in zip(gg, uv)]
    dx = [_bdot(g, s) + _bdot(d, b) for g, s, d, b in zip(gg, s0, dm, yk)]
    dyk = [_bdot_tn(d, a) + _bdot(w, z) for d, a, w, z in zip(dm, x, uv, dzs)]
    return dx, dyk, dv, dgl, ds0


STATE_SHAPE = (NPAIR * PW, PW)
M_SHAPE = (4 * L, NPAIR * 4 * L)
TINV_SHAPE = (2 * L, NPAIR * 2 * L)


def _rows_of(a, n):
    return [a[i * n:(i + 1) * n, :] for i in range(NPAIR)]


def _both(f):
    out = []
    for s in range(NSEQ):
        out += f(s)
    return out


def _seq_view(a):
    return a.reshape(NSEQ, SEQ, a.shape[-1])


UV_SHAPE = (4 * L, NPAIR * PW)
RW_CHUNKS = 2


def rwkv_fwd(r, lw, k2, v, aa, bb):
    def body(r_ref, lw_ref, k2_ref, v_ref, aa_ref, bb_ref, y_ref, hs_ref, hn_ref, m_ref, t_ref, uv_ref, state):
        @pl.when(pl.program_id(0) == 0)
        def _():
            state[...] = jnp.zeros_like(state)

        keep, tri, eye = _chunk_masks()
        where = [(j, s) for j in range(RW_CHUNKS) for s in range(NSEQ)]
        rows = lambda j: slice(j * L, (j + 1) * L)
        sc = [_scaled(r_ref[s, rows(j)], lw_ref[s, rows(j)], k2_ref[s, rows(j)], aa_ref[s, rows(j)],
                      bb_ref[s, rows(j)], tri) for j, s in where]
        ops = [_pairs_operands(*a[3:]) for a in sc]
        m, tinv = _pairs_matrices([a for o in ops for a in o[0]], [a for o in ops for a in o[1]], keep, eye)
        s_cur = [state[s] for s in range(NSEQ)]
        for j in range(RW_CHUNKS):
            mine = slice(j * NSEQ * NPAIR, (j + 1) * NSEQ * NPAIR)
            x = [a for o in ops[j * NSEQ:(j + 1) * NSEQ] for a in o[0]]
            yk = [a for o in ops[j * NSEQ:(j + 1) * NSEQ] for a in o[1]]
            vw = _both(lambda s: [_stack_pair(a) for a in _pairs(v_ref[s, rows(j)])])
            egl = _both(lambda s: _pairs(sc[j * NSEQ + s][0][L - 1:L, :]))
            y, sn, uv = _pairs_fwd(x, yk, m[mine], tinv[mine], vw, _both(lambda s: _rows_of(s_cur[s], PW)), egl)
            for s in range(NSEQ):
                ps = slice(s * NPAIR, (s + 1) * NPAIR)
                hs_ref[j, s] = s_cur[s]
                y_ref[s, rows(j)] = jnp.concatenate([_fold_pair(a) for a in y[ps]], axis=-1)
                m_ref[j, s] = jnp.concatenate(m[mine][ps], axis=-1)
                t_ref[j, s] = jnp.concatenate(tinv[mine][ps], axis=-1)
                uv_ref[j, s] = jnp.concatenate(uv[ps], axis=-1)
                s_cur[s] = jnp.concatenate(sn[ps], axis=0)
                hn_ref[j, s] = s_cur[s]
        for s in range(NSEQ):
            state[s] = s_cur[s]

    blk = pl.BlockSpec((NSEQ, RW_CHUNKS * L, W), lambda c: (0, c, 0))
    per_chunk = lambda shape: pl.BlockSpec((RW_CHUNKS, NSEQ) + shape, lambda c: (c, 0, 0, 0))
    saved_shapes = [(STATE_SHAPE, F32), (STATE_SHAPE, F32), (M_SHAPE, BF16), (TINV_SHAPE, BF16), (UV_SHAPE, BF16)]
    y, *saved = pl.pallas_call(
        body, grid=(NC // RW_CHUNKS,), name="rwkv_fwd",
        in_specs=[blk] * 6,
        out_specs=[blk] + [per_chunk(shape) for shape, _ in saved_shapes],
        out_shape=[jax.ShapeDtypeStruct((NSEQ, SEQ, W), F32)]
        + [jax.ShapeDtypeStruct((NC, NSEQ) + shape, dt) for shape, dt in saved_shapes],
        scratch_shapes=[pltpu.VMEM((NSEQ,) + STATE_SHAPE, F32)],
        compiler_params=_cparams(("arbitrary",), VMEM_BIG),
    )(*[_seq_view(a) for a in (r, lw, k2, v, aa, bb)])
    return y.reshape(T, W), saved


def rwkv_bwd(r, lw, k2, aa, bb, saved, dy):
    def body(r_ref, lw_ref, k2_ref, aa_ref, bb_ref, hs_ref, hn_ref, m_ref, t_ref, uv_ref, dy_ref,
             dr_ref, dlw_ref, dk2_ref, dv_ref, daa_ref, dbb_ref, dstate):
        @pl.when(pl.program_id(0) == 0)
        def _():
            dstate[...] = jnp.zeros_like(dstate)

        keep, tri, _ = _chunk_masks()
        sc = [_scaled(r_ref[s], lw_ref[s], k2_ref[s], aa_ref[s], bb_ref[s], tri) for s in range(NSEQ)]
        ops = [_pairs_operands(*sc[s][3:]) for s in range(NSEQ)]
        x, yk = _both(lambda s: ops[s][0]), _both(lambda s: ops[s][1])
        m = _both(lambda s: [m_ref[0, s][:, i * 4 * L:(i + 1) * 4 * L] for i in range(NPAIR)])
        tinv = _both(lambda s: [t_ref[0, s][:, i * 2 * L:(i + 1) * 2 * L] for i in range(NPAIR)])
        uv = _both(lambda s: _pairs(uv_ref[0, s]))
        dyw = _both(lambda s: [_stack_pair(a) for a in _pairs(dy_ref[s])])
        s0 = _both(lambda s: _rows_of(hs_ref[0, s], PW))
        sn = _both(lambda s: _rows_of(hn_ref[0, s], PW))
        dsn = _both(lambda s: _rows_of(dstate[s], PW))
        egl = _both(lambda s: _pairs(sc[s][0][L - 1:L, :]))
        dx, dyk, dvw, dgl, ds0 = _pairs_bwd(x, yk, m, tinv, uv, s0, sn, egl, dyw, dsn, keep)
        for s in range(NSEQ):
            mine = slice(s * NPAIR, (s + 1) * NPAIR)
            eg, eng, egp, at, rt, bt, kt = sc[s]
            dstate[s] = jnp.concatenate(ds0[mine], axis=0)
            dv_ref[s] = jnp.concatenate([_fold_pair(a) for a in dvw[mine]], axis=-1)
            dat = jnp.concatenate([_fold_pair(a[:2 * L]) for a in dx[mine]], axis=-1)
            drt = jnp.concatenate([_fold_pair(a[2 * L:]) for a in dx[mine]], axis=-1)
            dbt = jnp.concatenate([_fold_pair(a[:2 * L]) for a in dyk[mine]], axis=-1)
            dkt = jnp.concatenate([_fold_pair(a[2 * L:]) for a in dyk[mine]], axis=-1)
            dg = drt * rt - dbt * bt - dkt * kt
            dg = dg + jnp.where(_iota2(dg.shape, 0) == L - 1, jnp.concatenate(dgl[mine], axis=-1), 0.0)
            dgp = dat * at
            dlw_ref[s] = _hdot_tn(tri, dg + dgp) - dgp
            dr_ref[s] = drt * eg
            daa_ref[s] = dat * egp
            dbb_ref[s] = dbt * eng
            dk2_ref[s] = dkt * eng

    blk = pl.BlockSpec((NSEQ, L, W), lambda c: (0, NC - 1 - c, 0))
    per_chunk = lambda shape: pl.BlockSpec((1, NSEQ) + shape, lambda c: (NC - 1 - c, 0, 0, 0))
    outs = pl.pallas_call(
        body, grid=(NC,), name="rwkv_bwd",
        in_specs=[blk] * 5 + [per_chunk(a.shape[2:]) for a in saved] + [blk],
        out_specs=[blk] * 6,
        out_shape=[jax.ShapeDtypeStruct((NSEQ, SEQ, W), F32)] * 6,
        scratch_shapes=[pltpu.VMEM((NSEQ,) + STATE_SHAPE, F32)],
        compiler_params=_cparams(("arbitrary",)),
    )(*[_seq_view(a) for a in (r, lw, k2, aa, bb)], *saved, _seq_view(dy))
    return [a.reshape(T, W) for a in outs]


def _post_math(y, r, k2, v, ga, o, gb, lng, lnb, rk, bd):
    mu = _headsum(y, bd) * (1.0 / HD)
    yc = y - mu
    var = _headsum(yc * yc, bd) * (1.0 / HD)
    yn = yc * lax.rsqrt(var + GN_EPS) * lng + lnb
    bonus = _headsum(r * k2 * rk, bd) * v
    return (yn + bonus) * _silu(ga), o * _silu(gb)


def even_post(y, r, k2, v, ga, o, gb, lng, lnb, rk, h, w_bf):
    tm = 512

    def body(y_ref, r_ref, k2_ref, v_ref, ga_ref, o_ref, gb_ref, lng_ref, lnb_ref, rk_ref, h_ref, w_ref,
             ho_ref, zt_ref):
        ya, yb = _post_math(y_ref[...], r_ref[...], k2_ref[...], v_ref[...], ga_ref[...], o_ref[...], gb_ref[...],
                            lng_ref[...], lnb_ref[...], rk_ref[...], _head_blockdiag())
        z = jnp.concatenate([ya.astype(BF16), yb.astype(BF16)], axis=-1)
        zt_ref[...] = z.T
        ho_ref[...] = h_ref[...] + jnp.dot(z, w_ref[...], preferred_element_type=F32)

    vec = _const_spec((1, W))
    return pl.pallas_call(
        body, grid=(T // tm,), name="even_post",
        in_specs=[_row_spec(tm, W)] * 7 + [vec] * 3 + [_row_spec(tm, D), _const_spec((D, D))],
        out_specs=[_row_spec(tm, D), _col_spec(D, tm)],
        out_shape=[jax.ShapeDtypeStruct((T, D), F32), jax.ShapeDtypeStruct((D, T), BF16)],
        compiler_params=_cparams(("parallel",), VMEM_BIG),
    )(y, r, k2, v, ga, o, gb, lng, lnb, rk, h, w_bf)


def even_post_bwd(y, r, k2, v, ga, o, gb, lng, lnb, rk, dh, zt_bf, w_bf, after=None):
    tm = 512
    extra_specs, extra = _after_operand(after)

    def body(y_ref, r_ref, k2_ref, v_ref, ga_ref, o_ref, gb_ref, lng_ref, lnb_ref, rk_ref, dh_ref, zt_ref, w_ref,
             *rest):
        (dy_ref, dr_ref, dk2_ref, dv_ref, dga_ref, do_ref, dgb_ref, dlng_ref, dlnb_ref, drk_ref, dw_ref,
         acc_ref) = rest[-12:]
        dzv = _out_proj_back(dh_ref, zt_ref, w_ref, dw_ref, acc_ref)
        bd = _head_blockdiag()
        _, vjp = jax.vjp(lambda *a: _post_math(*a, bd), y_ref[...], r_ref[...], k2_ref[...], v_ref[...], ga_ref[...],
                         o_ref[...], gb_ref[...], lng_ref[...], lnb_ref[...], rk_ref[...])
        dy, dr, dk2, dv, dga, do, dgb, dlng, dlnb, drk = vjp((dzv[:, 0:W], dzv[:, W:2 * W]))
        for ref, val in ((dy_ref, dy), (dr_ref, dr), (dk2_ref, dk2), (dv_ref, dv), (dga_ref, dga), (do_ref, do),
                         (dgb_ref, dgb)):
            ref[...] = val.astype(ref.dtype)

        @pl.when(pl.program_id(0) == 0)
        def _():
            for ref in (dlng_ref, dlnb_ref, drk_ref):
                ref[...] = jnp.zeros_like(ref)

        dlng_ref[...] += dlng
        dlnb_ref[...] += dlnb
        drk_ref[...] += drk

    vec = _const_spec((1, W))
    return pl.pallas_call(
        body, grid=(T // tm,), name="even_post_bwd",
        in_specs=[_row_spec(tm, W)] * 7 + [vec] * 3 + [_row_spec(tm, D), _col_spec(D, tm), _const_spec((D, D))]
        + extra_specs,
        out_specs=[_row_spec(tm, W)] * 7 + [vec] * 3 + [_const_spec((D, D))],
        out_shape=[jax.ShapeDtypeStruct((T, W), dt) for dt in (F32, F32, F32, F32, BF16, F32, BF16)]
        + [jax.ShapeDtypeStruct((1, W), F32)] * 3 + [jax.ShapeDtypeStruct((D, D), BF16)],
        scratch_shapes=[pltpu.VMEM((D, D), F32)],
        compiler_params=_cparams(("arbitrary",), VMEM_BIG),
    )(y, r, k2, v, ga, o, gb, lng, lnb, rk, dh, zt_bf, w_bf, *extra)


PADSEQ = SEQ + LEFT * L
ATT_SCALE = 1.0 / math.sqrt(HD)
ATT_Q = 4
WIN = BAND + (ATT_Q - 1) * L
ATT_STEPS = NC // ATT_Q
ATT_BIAS_SHAPE = (NPAIR, ATT_Q * 2 * L, WIN)
ATT_WINDOW_BIAS_SHAPE = (ATT_Q, NPAIR, 2 * L, WIN)


def _stack_chunks(a):
    return jnp.concatenate([_stack_pair(a[i * L:(i + 1) * L]) for i in range(ATT_Q)], axis=0)


def _unstack_chunks(a):
    return jnp.concatenate([_unstack_pair(a[i * 2 * L:(i + 1) * 2 * L]) for i in range(ATT_Q)], axis=0)


def _window_bias(b_ref):
    return [jnp.concatenate([b_ref[c, p] for c in range(ATT_Q)], axis=0) for p in range(NPAIR)]


def _key_window(ref, step):
    start = step * (ATT_Q * L) - LEFT * L
    rows = ref[pl.ds(pl.multiple_of(jnp.maximum(start, 0), L), WIN), :]
    window = rows
    for lead in range(ATT_Q * L, LEFT * L + 1, ATT_Q * L):
        moved = jnp.concatenate([rows[WIN - lead:], rows[:WIN - lead]], axis=0)
        window = jnp.where(start == -lead, moved, window)
    return window


def _att_probs(q2, kw, bias, step):
    valid = _iota2((1, WIN), 1) >= (LEFT - step * ATT_Q) * L
    s = [jnp.where(valid, _bdot_nt(a, b) * ATT_SCALE + bias[p], NEG) for p, (a, b) in enumerate(zip(q2, kw))]
    e = [jnp.exp(a - jnp.max(a, axis=-1, keepdims=True)) for a in s]
    return [a / jnp.sum(a, axis=-1, keepdims=True) for a in e]


def attention_fwd(q, k, v, bias):
    def body(q_ref, k_ref, v_ref, b_ref, o_ref):
        step = pl.program_id(1)
        kw = _pairs(_key_window(k_ref, step))
        vw = _pairs(_key_window(v_ref, step))
        q2 = [_stack_chunks(a) for a in _pairs(q_ref[...])]
        p = _att_probs(q2, kw, _window_bias(b_ref), step)
        o_ref[...] = jnp.concatenate([_unstack_chunks(_bdot(a, b)) for a, b in zip(p, vw)], axis=-1)

    qblk = pl.BlockSpec((ATT_Q * L, W), lambda b, c: (b * ATT_STEPS + c, 0))
    kblk = pl.BlockSpec((SEQ, W), lambda b, c: (b, 0))
    return pl.pallas_call(
        body, grid=(NSEQ, ATT_STEPS), name="attention_fwd",
        in_specs=[qblk, kblk, kblk, _const_spec(ATT_WINDOW_BIAS_SHAPE)],
        out_specs=qblk, out_shape=jax.ShapeDtypeStruct((T, W), F32),
        compiler_params=_cparams(("parallel", "arbitrary")),
    )(q, k, v, bias)


def attention_bwd(q, k, v, bias, do):
    def body(q_ref, k_ref, v_ref, b_ref, do_ref, dq_ref, dko_ref, dvo_ref, db_ref, dk_ref, dv_ref):
        b = pl.program_id(0)
        c = pl.program_id(1)

        @pl.when(c == 0)
        def _():
            dk_ref[...] = jnp.zeros_like(dk_ref)
            dv_ref[...] = jnp.zeros_like(dv_ref)

        @pl.when((c == 0) & (b == 0))
        def _():
            db_ref[...] = jnp.zeros_like(db_ref)

        start = pl.multiple_of(c * (ATT_Q * L), L)
        kw = _pairs(_key_window(k_ref, c))
        vw = _pairs(_key_window(v_ref, c))
        q2 = [_stack_chunks(a) for a in _pairs(q_ref[...])]
        do2 = [_stack_chunks(a) for a in _pairs(do_ref[...].astype(BF16))]
        p = _att_probs(q2, kw, _window_bias(b_ref), c)
        dp = [_bdot_nt(a, b) for a, b in zip(do2, vw)]
        ds = [a * (d - jnp.sum(d * a, axis=-1, keepdims=True)) for a, d in zip(p, dp)]
        dss = [(a * ATT_SCALE).astype(BF16) for a in ds]
        dq_ref[...] = jnp.concatenate([_unstack_chunks(_bdot(a, b)) for a, b in zip(dss, kw)], axis=-1).astype(BF16)
        dk_ref[pl.ds(start, WIN), :] += jnp.concatenate([_bdot_tn(a, b) for a, b in zip(dss, q2)], axis=-1)
        dv_ref[pl.ds(start, WIN), :] += jnp.concatenate([_bdot_tn(a, b) for a, b in zip(p, do2)], axis=-1)
        for i in range(NPAIR):
            db_ref[i] += ds[i]

        @pl.when(c == ATT_STEPS - 1)
        def _():
            dko_ref[...] = dk_ref[LEFT * L:, :].astype(BF16)
            dvo_ref[...] = dv_ref[LEFT * L:, :].astype(BF16)

    qblk = pl.BlockSpec((ATT_Q * L, W), lambda b, c: (b * ATT_STEPS + c, 0))
    sblk = pl.BlockSpec((SEQ, W), lambda b, c: (b, 0))
    bblk = _const_spec(ATT_BIAS_SHAPE)
    return pl.pallas_call(
        body, grid=(NSEQ, ATT_STEPS), name="attention_bwd",
        in_specs=[qblk, sblk, sblk, _const_spec(ATT_WINDOW_BIAS_SHAPE), qblk],
        out_specs=[qblk, sblk, sblk, bblk],
        out_shape=[jax.ShapeDtypeStruct((T, W), BF16), jax.ShapeDtypeStruct((T, W), BF16),
                   jax.ShapeDtypeStruct((T, W), BF16), jax.ShapeDtypeStruct(ATT_BIAS_SHAPE, F32)],
        scratch_shapes=[pltpu.VMEM((PADSEQ, W), F32), pltpu.VMEM((PADSEQ, W), F32)],
        compiler_params=_cparams(("arbitrary", "arbitrary"), VMEM_BIG),
    )(q, k, v, bias, do)


NTAB = 2 * CLIP + 1
EXT = BAND + L


def _ext_onehot():
    n = _iota2((EXT, NTAB), 0)
    m = _iota2((EXT, NTAB), 1)
    return (jnp.clip(BAND - 1 - n, -CLIP, CLIP) + CLIP == m).astype(F32)


def bias_expand(table):
    def body(t_ref, o_ref):
        ext = _hdot_nt(t_ref[...], _ext_onehot())
        ext = jnp.concatenate([ext, jnp.zeros((NH, WIN - EXT), F32)], axis=-1)
        col = _iota2((NH, WIN), 1)
        for c in range(ATT_Q):
            inside = (col >= c * L) & (col < c * L + BAND)
            for i in range(L):
                shift = (c * L - (L - 1 - i)) % WIN
                o_ref[c, :, i, :] = jnp.where(inside, pltpu.roll(ext, shift, 1) if shift else ext, NEG)

    out = pl.pallas_call(body, name="bias_expand", out_shape=jax.ShapeDtypeStruct((ATT_Q, NH, L, WIN), F32))(table)
    return out.reshape(ATT_WINDOW_BIAS_SHAPE)


def bias_grad(dbias):
    def body(d_ref, o_ref):
        acc = jnp.zeros((NH, EXT), F32)
        zpad = jnp.zeros((NH, EXT - BAND), F32)
        for i in range(L):
            s = L - 1 - i
            row = jnp.concatenate([d_ref[:, i, :], zpad], axis=-1)
            acc = acc + (pltpu.roll(row, s, 1) if s else row)
        o_ref[...] = _hdot(acc, _ext_onehot())

    return pl.pallas_call(body, name="bias_grad", out_shape=jax.ShapeDtypeStruct((NH, NTAB), F32))(dbias)


def _group_cols(g):
    return slice(g * SGC, (g + 1) * SGC)


def _sg_norm(gv, lng, lnb):
    gc = gv - jnp.mean(gv, axis=-1, keepdims=True)
    rstd = lax.rsqrt(jnp.mean(gc * gc, axis=-1, keepdims=True) + LN_EPS)
    xhat = gc * rstd
    return xhat, rstd, xhat * lng + lnb


GMLP_BWD_CHUNKS = 2


def gmlp_fwd_loss(u, v, gate, lng, lnb, wm_bf, sgb_t, h, w_bf, g_final, target):
    tm = GMLP_BWD_CHUNKS * SGC

    def body(u_ref, v_ref, gt_ref, lng_ref, lnb_ref, wm_ref, sb_ref, h_ref, w_ref, g_ref, t_ref,
             dh_ref, loss_ref, dg_ref, zt_ref):
        zs = []
        for ch in range(GMLP_BWD_CHUNKS):
            rows = slice(ch * SGC, (ch + 1) * SGC)
            _, _, vln = _sg_norm(_gelu(v_ref[rows, :]), lng_ref[...], lnb_ref[...])
            vlb = vln.astype(BF16)
            zg = []
            for g in range(NG):
                cs = _group_cols(g)
                sv = jnp.dot(wm_ref[g], vlb[:, cs], preferred_element_type=F32) + sb_ref[:, g:g + 1]
                zg.append((_gelu(u_ref[rows, cs]) * sv * _silu(gt_ref[rows, cs])).astype(BF16))
            zs.append(jnp.concatenate(zg, axis=-1))
        z = jnp.concatenate(zs, axis=0)
        zt_ref[...] = z.T
        xv = h_ref[...] + jnp.dot(z, w_ref[...], preferred_element_type=F32)
        rstd = lax.rsqrt(jnp.mean(xv * xv, axis=-1, keepdims=True) + RMS_EPS)
        xhat = xv * rstd
        err = xhat * g_ref[...] - t_ref[...]
        part = 0.5 * jnp.sum(jnp.mean(err * err, axis=-1, keepdims=True), axis=0, keepdims=True)
        dout = err * (1.0 / D)

        @pl.when(pl.program_id(0) == 0)
        def _():
            loss_ref[...] = jnp.zeros_like(loss_ref)
            dg_ref[...] = jnp.zeros_like(dg_ref)

        loss_ref[...] += jnp.broadcast_to(part, loss_ref.shape)
        dg_ref[...] += jnp.sum(dout * xhat, axis=0, keepdims=True)
        dxh = dout * g_ref[...]
        dh_ref[...] = rstd * (dxh - xhat * jnp.mean(dxh * xhat, axis=-1, keepdims=True))

    return pl.pallas_call(
        body, grid=(T // tm,), name="gmlp_fwd_loss",
        in_specs=[_row_spec(tm, D)] * 3 + [_const_spec((1, D))] * 2
        + [_const_spec((NG, SGC, SGC)), _const_spec((SGC, NG)), _row_spec(tm, D), _const_spec((D, D)),
           _const_spec((1, D)), _row_spec(tm, D)],
        out_specs=[_row_spec(tm, D), _const_spec((8, 128)), _const_spec((1, D)), _col_spec(D, tm)],
        out_shape=[jax.ShapeDtypeStruct((T, D), F32), jax.ShapeDtypeStruct((8, 128), F32),
                   jax.ShapeDtypeStruct((1, D), F32), jax.ShapeDtypeStruct((D, T), BF16)],
        compiler_params=_cparams(("arbitrary",), VMEM_BIG),
    )(u, v, gate, lng, lnb, wm_bf, sgb_t, h, w_bf, g_final, target)


def gmlp_bwd(u, v, gate, lng, lnb, wm_bf, sgb_t, dh, zt_bf, w_bf):
    def body(u_ref, v_ref, gt_ref, lng_ref, lnb_ref, wm_ref, sb_ref, dh_ref, zt_ref, w_ref,
             du_ref, dv_ref, dgt_ref, dlng_ref, dlnb_ref, dwm_ref, dsb_ref, dw_ref, acc_ref):
        @pl.when(pl.program_id(0) == 0)
        def _():
            for ref in (dlng_ref, dlnb_ref, dwm_ref, dsb_ref):
                ref[...] = jnp.zeros_like(ref)

        dz = _out_proj_back(dh_ref, zt_ref, w_ref, dw_ref, acc_ref)
        sel = (_iota2((D, NG), 0) // SGC == _iota2((D, NG), 1)).astype(F32)
        for ch in range(GMLP_BWD_CHUNKS):
            rows = slice(ch * SGC, (ch + 1) * SGC)
            gv, dgv_dv = _gelu_both(v_ref[rows, :])
            xhat, rstd, vln = _sg_norm(gv, lng_ref[...], lnb_ref[...])
            vlb = vln.astype(BF16)
            dvln = []
            dsv_all = []
            for g in range(NG):
                cs = _group_cols(g)
                uu = u_ref[rows, cs]
                gg = gt_ref[rows, cs]
                dzz = dz[rows, cs]
                sv = jnp.dot(wm_ref[g], vlb[:, cs], preferred_element_type=F32) + sb_ref[:, g:g + 1]
                gu, dgu = _gelu_both(uu)
                sg, dsg = _silu_both(gg)
                dzgu = dzz * gu
                dsv = dzgu * sg
                dgt_ref[rows, cs] = (dzgu * sv * dsg).astype(BF16)
                du_ref[rows, cs] = (dzz * sv * sg * dgu).astype(BF16)
                dsb16 = dsv.astype(BF16)
                dvln.append(lax.dot_general(wm_ref[g], dsb16, (((0,), (0,)), ((), ())), preferred_element_type=F32))
                dwm_ref[g] += lax.dot_general(dsb16, vlb[:, cs], (((1,), (1,)), ((), ())),
                                              preferred_element_type=F32)
                dsv_all.append(dsv)
            dvl = jnp.concatenate(dvln, axis=-1)
            dsb_ref[...] += _hdot(jnp.concatenate(dsv_all, axis=-1), sel)
            dlng_ref[...] += jnp.sum(dvl * xhat, axis=0, keepdims=True)
            dlnb_ref[...] += jnp.sum(dvl, axis=0, keepdims=True)
            dxh = dvl * lng_ref[...]
            dgv = rstd * (dxh - jnp.mean(dxh, axis=-1, keepdims=True)
                          - xhat * jnp.mean(dxh * xhat, axis=-1, keepdims=True))
            dv_ref[rows, :] = (dgv * dgv_dv).astype(BF16)

    tm = GMLP_BWD_CHUNKS * SGC
    return pl.pallas_call(
        body, grid=(T // tm,), name="gmlp_bwd",
        in_specs=[_row_spec(tm, D)] * 3 + [_const_spec((1, D))] * 2
        + [_const_spec((NG, SGC, SGC)), _const_spec((SGC, NG)), _row_spec(tm, D), _col_spec(D, tm),
           _const_spec((D, D))],
        out_specs=[_row_spec(tm, D)] * 3 + [_const_spec((1, D))] * 2
        + [_const_spec((NG, SGC, SGC)), _const_spec((SGC, NG)), _const_spec((D, D))],
        out_shape=[jax.ShapeDtypeStruct((T, D), BF16)] * 3 + [jax.ShapeDtypeStruct((1, D), F32)] * 2
        + [jax.ShapeDtypeStruct((NG, SGC, SGC), F32), jax.ShapeDtypeStruct((SGC, NG), F32),
           jax.ShapeDtypeStruct((D, D), BF16)],
        scratch_shapes=[pltpu.VMEM((D, D), F32)],
        compiler_params=_cparams(("arbitrary",), VMEM_BIG),
    )(u, v, gate, lng, lnb, wm_bf, sgb_t, dh, zt_bf, w_bf)


NCHIP = 4
NDEV = 8
ANY = pl.BlockSpec(memory_space=pl.ANY)


HBM = pl.BlockSpec(memory_space=pltpu.HBM)
SEM = pl.BlockSpec(memory_space=pltpu.SEMAPHORE)
EFFECT = pltpu.SideEffectType.DATAFLOW_SIDE_EFFECTING


CHIPS, EVERY, SIBLING = "chips", "every", "sibling"
SLOTS = {CHIPS: NCHIP, EVERY: NDEV, SIBLING: 1}


def _peers(scope):
    x, y, c = lax.axis_index("x"), lax.axis_index("y"), lax.axis_index("c")
    if scope == SIBLING:
        return [((x, y, 1 - c), 0)], 0
    if scope == CHIPS:
        return [((px, py, c), 2 * px + py) for px, py in ((1 - x, y), (x, 1 - y), (1 - x, 1 - y))], 2 * x + y
    out = []
    for j in range(1, NDEV):
        px, py, pc = x ^ (j >> 2), y ^ ((j >> 1) & 1), c ^ (j & 1)
        out.append(((px, py, pc), 4 * px + 2 * py + pc))
    return out, 4 * x + 2 * y + c


def _send_copies(src, land, send, recv, scatter, scope, starting):
    peers, me = _peers(scope)
    copies = []
    for t in range(len(src)):
        for j, (dev, slot) in enumerate(peers):
            k = t * len(peers) + j
            copies.append(pltpu.make_async_remote_copy(
                src_ref=src[t].at[slot] if scatter else src[t], dst_ref=land[t].at[me if starting else slot],
                send_sem=send.at[k], recv_sem=recv.at[k], device_id=dev, device_id_type=MESH))
    return copies


def _own_copies(src, land, sems, scatter, scope):
    if scope == SIBLING:
        return []
    _, me = _peers(scope)
    return [pltpu.make_async_copy(src[t].at[me] if scatter else src[t], land[t].at[me], sems.at[t])
            for t in range(len(src))]


def send_start(srcs, scatter, scope, name, after=None):
    n = len(srcs)
    slots = SLOTS[scope]
    extra_specs, extra = _after_operand(after)
    lands = [pltpu.HBM(a.shape if scatter else (slots,) + a.shape, a.dtype) for a in srcs]
    sems = [pltpu.SemaphoreType.DMA((n * max(slots - 1, 1),))] * 2 + ([] if scope == SIBLING else
                                                                     [pltpu.SemaphoreType.DMA((n,))])
    k = len(sems)

    def body(*refs):
        first_out = n + len(extra)
        src, land = refs[:n], refs[first_out + k + n:first_out + k + 2 * n]
        for cp in _send_copies(src, land, refs[first_out], refs[first_out + 1], scatter, scope, True):
            cp.start()
        for cp in _own_copies(src, land, refs[first_out + k - 1], scatter, scope):
            cp.start()
        refs[-1][...] = jnp.zeros_like(refs[-1])

    out = pl.pallas_call(
        body, name=name,
        out_shape=(*sems, *[pltpu.HBM(a.shape, a.dtype) for a in srcs], *lands, jax.ShapeDtypeStruct((8, 128), F32)),
        in_specs=[HBM] * n + extra_specs,
        out_specs=(*[SEM] * k, *[HBM] * (2 * n), pl.BlockSpec(memory_space=pltpu.VMEM)),
        input_output_aliases={i: k + i for i in range(n)},
        compiler_params=pltpu.CompilerParams(has_side_effects=EFFECT),
    )(*[pltpu.with_memory_space_constraint(a, pltpu.HBM) for a in srcs], *extra)
    return list(out[:k]), list(out[k:k + n]), list(out[k + n:k + 2 * n]), out[-1]


def send_wait(started, after, scatter, scope, name, with_sources=False, only=None):
    sems, srcs, lands, _ = started
    n, k = len(srcs), len(sems)
    wanted = range(n) if only is None else only

    def body(*refs):
        src, land = refs[:n], refs[n:2 * n]
        for t, cp in enumerate(_own_copies(src, land, refs[2 * n + k - 1], scatter, scope)):
            if t in wanted:
                cp.wait()
        copies = _send_copies(src, land, refs[2 * n], refs[2 * n + 1], scatter, scope, False)
        for i, cp in enumerate(copies):
            if i // (len(copies) // n) in wanted:
                cp.wait_send()
                cp.wait_recv()

    arrs = list(srcs) + list(lands)
    out = pl.pallas_call(
        body, name=name, out_shape=tuple(pltpu.HBM(a.shape, a.dtype) for a in arrs),
        in_specs=[HBM] * (2 * n) + [SEM] * k + [ANY], out_specs=tuple([HBM] * (2 * n)),
        input_output_aliases={i: i for i in range(2 * n)},
        compiler_params=pltpu.CompilerParams(has_side_effects=EFFECT),
    )(*arrs, *sems, after)
    return (list(out[:n]), list(out[n:])) if with_sources else list(out[n:])


def gather_weights(arrs, split):
    n = len(arrs)

    def body(*refs):
        ins, outs = refs[:n], refs[n:2 * n]
        send1, recv1, send2, recv2, loc_in, loc_out = refs[2 * n:2 * n + 6]
        staged = refs[2 * n + 6:]
        x, y, c = lax.axis_index("x"), lax.axis_index("y"), lax.axis_index("c")
        me = 2 * x + y
        sibling = (x, y, 1 - c)
        peers = [(1 - x, y), (x, 1 - y), (1 - x, 1 - y)]

        def rows_of(t, core):
            half = arrs[t].shape[0] // 2
            return pl.ds(core * half, half)

        def part(ref, t, core):
            return ref.at[rows_of(t, core)] if split[t] else ref

        load = [pltpu.make_async_copy(ins[t], staged[t], loc_in.at[t]) for t in range(n)]
        store = [pltpu.make_async_copy(staged[t], outs[t].at[me], loc_out.at[t]) for t in range(n)]
        for cp in load:
            cp.start()
        first = []
        for t in range(n):
            for j, (px, py) in enumerate(peers):
                first.append(pltpu.make_async_remote_copy(
                    src_ref=part(ins[t], t, c), dst_ref=part(outs[t].at[me], t, c), send_sem=send1.at[t, j],
                    recv_sem=recv1.at[t, j], device_id=(px, py, c), device_id_type=MESH))
        for cp in first:
            cp.start()
        for cp_in, cp_out in zip(load, store):
            cp_in.wait()
            cp_out.start()
        passed = []
        for t in range(n):
            for j, (px, py) in enumerate(peers):
                landed = part(outs[t].at[2 * px + py], t, c)
                pltpu.make_async_remote_copy(
                    src_ref=landed, dst_ref=landed, send_sem=send1.at[t, j], recv_sem=recv1.at[t, j],
                    device_id=(x, y, c), device_id_type=MESH).wait_recv()
                if split[t]:
                    cp = pltpu.make_async_remote_copy(
                        src_ref=landed, dst_ref=landed, send_sem=send2.at[t, j], recv_sem=recv2.at[t, j],
                        device_id=sibling, device_id_type=MESH)
                    cp.start()
                    passed.append(cp)
        for t in range(n):
            for j, (px, py) in enumerate(peers):
                if split[t]:
                    other = part(outs[t].at[2 * px + py], t, 1 - c)
                    pltpu.make_async_remote_copy(
                        src_ref=other, dst_ref=other, send_sem=send2.at[t, j], recv_sem=recv2.at[t, j],
                        device_id=(x, y, c), device_id_type=MESH).wait_recv()
        for cp in first + passed:
            cp.wait_send()
        for cp in store:
            cp.wait()

    return pl.pallas_call(
        body, name="gather_weights", in_specs=[ANY] * n, out_specs=[ANY] * n,
        out_shape=[jax.ShapeDtypeStruct((NCHIP,) + a.shape, a.dtype) for a in arrs],
        scratch_shapes=[pltpu.SemaphoreType.DMA((n, 3))] * 4 + [pltpu.SemaphoreType.DMA((n,))] * 2
        + [pltpu.VMEM(a.shape, a.dtype) for a in arrs],
    )(*arrs)


def _adam_math(g, w, m, v):
    m = ADAM_B1 * m + (1.0 - ADAM_B1) * g
    v = ADAM_B2 * v + (1.0 - ADAM_B2) * (g * g)
    m_hat = m / (1.0 - ADAM_B1 ** ADAM_STEP)
    v_hat = v / (1.0 - ADAM_B2 ** ADAM_STEP)
    delta = -ADAM_LR * (m_hat / (jnp.sqrt(v_hat) + ADAM_EPS) + ADAM_WD * w)
    return delta, m, v


def _rows_tile(rows):
    return rows if rows <= 256 else 256


def sum_chips(parts, name):
    _, rows, cols = parts.shape
    tr = _rows_tile(rows)

    def body(p_ref, o_ref):
        acc = p_ref[0].astype(F32)
        for s in range(1, NCHIP):
            acc = acc + p_ref[s].astype(F32)
        o_ref[...] = acc

    return pl.pallas_call(
        body, grid=(rows // tr,), name=name,
        in_specs=[pl.BlockSpec((NCHIP, tr, cols), lambda i: (0, i, 0))],
        out_specs=pl.BlockSpec((tr, cols), lambda i: (i, 0)),
        out_shape=jax.ShapeDtypeStruct((rows, cols), F32),
        compiler_params=_cparams(("parallel",)),
    )(parts)


def sum_chips_small(parts, name):
    n = len(parts)

    def body(*refs):
        for p_ref, o_ref in zip(refs[:n], refs[n:]):
            acc = p_ref[0]
            for s in range(1, NCHIP):
                acc = acc + p_ref[s]
            o_ref[...] = acc

    return pl.pallas_call(body, name=name, out_shape=[jax.ShapeDtypeStruct(p.shape[1:], F32) for p in parts])(*parts)


def adam_shard_small(items, name):
    n = len(items)

    def body(*refs):
        for t in range(n):
            a_ref, b_ref, w_ref, m_ref, v_ref = refs[5 * t:5 * t + 5]
            g_ref, d_ref, mo_ref, vo_ref = refs[5 * n + 4 * t:5 * n + 4 * t + 4]
            g = (a_ref[...] + b_ref[...]).reshape(w_ref.shape)
            g_ref[...] = g
            d_ref[...], mo_ref[...], vo_ref[...] = _adam_math(g, w_ref[...], m_ref[...], v_ref[...])

    out = pl.pallas_call(
        body, name=name, out_shape=[jax.ShapeDtypeStruct(it[2].shape, F32) for it in items for _ in range(4)],
    )(*[a for it in items for a in it])
    return [out[4 * t:4 * t + 4] for t in range(n)]


def adam_shard(p_mine, p_sib, w, m, v, name):
    rows, cols = p_mine.shape
    tr = _rows_tile(rows)
    lead = w.ndim == 3

    def body(a_ref, b_ref, w_ref, m_ref, v_ref, g_ref, d_ref, mo_ref, vo_ref):
        g = a_ref[...] + b_ref[...]
        g = g[None] if lead else g
        g_ref[...] = g
        d_ref[...], mo_ref[...], vo_ref[...] = _adam_math(g, w_ref[...], m_ref[...], v_ref[...])

    flat = pl.BlockSpec((tr, cols), lambda i: (i, 0))
    spec = pl.BlockSpec((1, tr, cols), lambda i: (0, i, 0)) if lead else flat
    return pl.pallas_call(
        body, grid=(rows // tr,), name=name, in_specs=[flat] * 2 + [spec] * 3, out_specs=[spec] * 4,
        out_shape=[jax.ShapeDtypeStruct(w.shape, F32)] * 4,
        compiler_params=_cparams(("parallel",)),
    )(p_mine, p_sib, w, m, v)


def adam_shard_halves_t(r_mine, r_sib, wt, mt, vt, name):
    hrows, cols = r_mine.shape
    tr = _rows_tile(hrows)
    per_half = hrows // tr

    def body(a_ref, b_ref, w_ref, m_ref, v_ref, g_ref, d_ref, mo_ref, vo_ref):
        mine = pl.program_id(0) == lax.axis_index("c")
        g = jnp.where(mine, a_ref[...], b_ref[...]).T[None]
        g_ref[...] = g
        d_ref[...], mo_ref[...], vo_ref[...] = _adam_math(g, w_ref[...], m_ref[...], v_ref[...])

    flat = pl.BlockSpec((tr, cols), lambda h, i: (i, 0))
    spec = pl.BlockSpec((1, cols, tr), lambda h, i: (0, 0, h * per_half + i))
    return pl.pallas_call(
        body, grid=(2, per_half), name=name, in_specs=[flat] * 2 + [spec] * 3, out_specs=[spec] * 4,
        out_shape=[jax.ShapeDtypeStruct(wt.shape, F32)] * 4,
        compiler_params=_cparams(("parallel", "parallel")),
    )(r_mine, r_sib, wt, mt, vt)


def adam_replicated(gathered, params, name):
    flat = []
    for i, p in enumerate(params):
        if isinstance(p, list):
            off = 0
            for wmv in p:
                n = gathered[i].shape[-1] - off if wmv[0] is None else wmv[0].shape[-1]
                flat.append((i, (off, n), wmv))
                off += n
        else:
            flat.append((i, None, p))
    ins = [a for _, _, wmv in flat for a in wmv if a is not None]
    ng = len(gathered)

    def body(*refs):
        g_refs = refs[:ng]
        in_refs = list(refs[ng:ng + len(ins)])
        out_refs = list(refs[ng + len(ins):])
        sums = []
        for r in g_refs:
            g = r[0]
            for d in range(1, NDEV):
                g = g + r[d]
            sums.append(g)
        for i, lanes, wmv in flat:
            g = sums[i] if lanes is None else sums[i][:, lanes[0]:lanes[0] + lanes[1]]
            out_refs.pop(0)[...] = g
            if wmv[0] is not None:
                w_ref, m_ref, v_ref = in_refs.pop(0), in_refs.pop(0), in_refs.pop(0)
                d_ref, mo_ref, vo_ref = out_refs.pop(0), out_refs.pop(0), out_refs.pop(0)
                d_ref[...], mo_ref[...], vo_ref[...] = _adam_math(g, w_ref[...], m_ref[...], v_ref[...])

    out_shape = []
    for i, lanes, wmv in flat:
        shape = gathered[i].shape[1:] if lanes is None else (1, lanes[1])
        out_shape += [jax.ShapeDtypeStruct(shape, F32)] * (4 if wmv[0] is not None else 1)
    outs = list(pl.pallas_call(body, name=name, out_shape=out_shape)(*gathered, *ins))
    return [[outs.pop(0) for _ in range(4 if wmv[0] is not None else 1)] for _, _, wmv in flat]


EVEN_SPLITS = (SHIFT, W, W, W, W, W)
ODD_SPLITS = (D, D, D)


def _cols_to_chips(a):
    rows, cols = a.shape
    return a.reshape(rows, NCHIP, cols // NCHIP).transpose(1, 0, 2)


def _chips_to_cols(a):
    _, rows, n = a.shape
    return a.transpose(1, 0, 2).reshape(rows, NCHIP * n)


def kernel(x, norm_g, w_in_e, shift_mu, rw_w0, rw_w2, rw_a0, rw_a2, rw_kk, rw_ka, rw_rk, rw_lnx_g, rw_lnx_b, att_bias, w_out_e, w_in_o, sg_ln_g, sg_ln_b, sg_w, sg_b, w_out_o, final_g, loss_target, m_norm_g, m_w_in_e, m_shift_mu, m_rw_w0, m_rw_w2, m_rw_a0, m_rw_a2, m_rw_kk, m_rw_ka, m_rw_rk, m_rw_lnx_g, m_rw_lnx_b, m_att_bias, m_w_out_e, m_w_in_o, m_sg_ln_g, m_sg_ln_b, m_sg_w, m_sg_b, m_w_out_o, m_final_g, v_norm_g, v_w_in_e, v_shift_mu, v_rw_w0, v_rw_w2, v_rw_a0, v_rw_a2, v_rw_kk, v_rw_ka, v_rw_rk, v_rw_lnx_g, v_rw_lnx_b, v_att_bias, v_w_out_e, v_w_in_o, v_sg_ln_g, v_sg_ln_b, v_sg_w, v_sg_b, v_w_out_o, v_final_g):
    x2 = x.reshape(T, D)
    tgt = loss_target.reshape(T, D)

    gathered = gather_weights(
        [jnp.swapaxes(w_in_e[0], 0, 1).astype(BF16), jnp.concatenate([rw_w2[0], rw_a2[0]], axis=0),
         jnp.concatenate([sg_ln_g, sg_ln_b], axis=0)], [True, True, False])
    wie = gathered[0].reshape(EVEN_IN, D)
    w2 = _chips_to_cols(gathered[1][:, :LORA])
    a2 = _chips_to_cols(gathered[1][:, LORA:])
    sglg = _chips_to_cols(gathered[2][:, 0:1])
    sglb = _chips_to_cols(gathered[2][:, 1:2])

    late = [w_out_e[0].astype(BF16), w_in_o[0].astype(BF16), w_out_o[0].astype(BF16)]
    late_started = send_start(late, False, CHIPS, "late_weights_start", after=gathered[0])

    late_state = {}

    def late_weights(layer, after):
        if layer == "even":
            srcs, lands = send_wait(late_started, after, False, CHIPS, "late_w_out_e_wait", True, only=(0,))
            late_state["rest"] = (late_started[0], srcs, lands, None)
            return lands[0].reshape(D, D)
        _, wio, woo = send_wait(late_state["rest"], after, False, CHIPS, "late_weights_wait", only=(1, 2))
        return _chips_to_cols(wio), woo.reshape(D, D)

    def scatter_start(grads, name):
        return send_start([g_.astype(BF16) if g_.shape[-1] >= W else g_ for g_ in grads], True, CHIPS, name)

    started = {}

    def on_odd_grads(d_woo, d_wio):
        started["odd"] = scatter_start([d_woo.reshape(NCHIP, D // NCHIP, D), d_wio], "odd_grads_start")
        return started["odd"][-1]

    def on_even_grads(big_g):
        d_wie_half, d_woe, _, _, d_w2, d_a2, d_sglg, d_sglb = big_g
        blocks = [d_wie_half, d_woe.reshape(NCHIP, D // NCHIP, D), _cols_to_chips(d_w2), _cols_to_chips(d_a2),
                  _cols_to_chips(d_sglg), _cols_to_chips(d_sglb)]
        started["even"] = scatter_start(blocks, "even_grads_start")
        return started["even"][-1]

    def on_small_grads(layer, grads):
        if layer == "odd":
            d_sg_w, d_sg_b, d_final, d_g1 = grads
            mine = [d_sg_w.reshape(NG * SGC, SGC), d_sg_b, jnp.concatenate([d_final, d_g1], axis=1)]
        else:
            mine = [grads[-2], jnp.concatenate(grads[:-2] + grads[-1:], axis=1)]
        started[layer + "_small"] = send_start(mine, False, EVERY, layer + "_small_grads_start")
        return started[layer + "_small"][-1]

    loss_part, dx, _, _ = _local_step(
        x2, tgt, wie, late_weights, w2, a2, sglg, sglb, norm_g, shift_mu, rw_w0, rw_a0, rw_kk, rw_ka, rw_rk,
        rw_lnx_g, rw_lnx_b, att_bias, sg_w, sg_b, final_g, first_after=late_started[-1], on_odd_grads=on_odd_grads,
        on_even_grads=on_even_grads, on_small_grads=on_small_grads)
    wmv = {"w_in_e": tuple(jnp.swapaxes(a, 1, 2) for a in (w_in_e, m_w_in_e, v_w_in_e)),
           "w_out_e": (w_out_e, m_w_out_e, v_w_out_e),
           "w_in_o": (w_in_o, m_w_in_o, v_w_in_o), "w_out_o": (w_out_o, m_w_out_o, v_w_out_o),
           "rw_w2": (rw_w2, m_rw_w2, v_rw_w2), "rw_a2": (rw_a2, m_rw_a2, v_rw_a2),
           "sg_ln_g": (sg_ln_g, m_sg_ln_g, v_sg_ln_g), "sg_ln_b": (sg_ln_b, m_sg_ln_b, v_sg_ln_b)}
    sharded = {}

    def sum_and_swap(names, landed, tag):
        nbig = sum(p_.dtype == BF16 for p_ in landed)
        partial = [sum_chips(p_, "sum_" + nm) for p_, nm in zip(landed[:nbig], names)]
        if nbig < len(names):
            partial += sum_chips_small(landed[nbig:], "sum_small_" + tag)
        return send_start(partial, False, SIBLING, "swap_partials_" + tag + "_start")

    def update(names, swap_started, after, tag):
        partial, landed = send_wait(swap_started, after, False, SIBLING, "swap_partials_" + tag + "_wait", True)
        from_sibling = [a[0] for a in landed]
        nbig = sum(p_.shape[-1] >= W for p_ in partial)
        for nm, mine, sib in zip(names[:nbig], partial, from_sibling):
            if nm == "w_in_e":
                res = adam_shard_halves_t(mine, sib, *wmv[nm], "adam_" + nm)
                sharded[nm] = [jnp.swapaxes(a, 1, 2) for a in res]
            else:
                sharded[nm] = adam_shard(mine, sib, *wmv[nm], "adam_" + nm)
        if nbig < len(names):
            items = [(mine, sib, *wmv[nm]) for nm, mine, sib in zip(names, partial, from_sibling)][nbig:]
            for nm, res in zip(names[nbig:], adam_shard_small(items, "adam_small_" + tag)):
                sharded[nm] = res

    odd_names = ["w_out_o", "w_in_o"]
    even_names = ["w_in_e", "w_out_e", "rw_w2", "rw_a2", "sg_ln_g", "sg_ln_b"]
    odd_landed = send_wait(started["odd"], started["even_small"][-1], True, CHIPS, "odd_grads_wait")
    odd_swap = sum_and_swap(odd_names, odd_landed, "odd")
    done = odd_swap[-1]

    def wmv_of(*arrs, view=lambda a: a):
        return tuple(view(a) for a in arrs)

    vec = lambda a: a.reshape(1, -1)
    groups = {
        "odd": (["sg_w", "sg_b", "final_g", "norm_g1"],
                [wmv_of(sg_w, m_sg_w, v_sg_w, view=lambda a: a.reshape(NG * SGC, SGC)),
                 wmv_of(sg_b, m_sg_b, v_sg_b, view=lambda a: a[0]),
                 [wmv_of(final_g, m_final_g, v_final_g, view=vec),
                  wmv_of(norm_g, m_norm_g, v_norm_g, view=lambda a: a[1:2])]]),
        "even": (["att_bias", "norm_g0", "shift_mu", "rw_w0", "rw_a0", "rw_kk", "rw_ka", "rw_rk", "rw_lnx_g",
                  "rw_lnx_b", "loss"],
                 [wmv_of(att_bias, m_att_bias, v_att_bias, view=lambda a: a[0]),
                  [wmv_of(norm_g, m_norm_g, v_norm_g, view=lambda a: a[0:1]),
                   wmv_of(shift_mu, m_shift_mu, v_shift_mu), wmv_of(rw_w0, m_rw_w0, v_rw_w0),
                   wmv_of(rw_a0, m_rw_a0, v_rw_a0), wmv_of(rw_kk, m_rw_kk, v_rw_kk), wmv_of(rw_ka, m_rw_ka, v_rw_ka),
                   wmv_of(rw_rk, m_rw_rk, v_rw_rk, view=vec), wmv_of(rw_lnx_g, m_rw_lnx_g, v_rw_lnx_g),
                   wmv_of(rw_lnx_b, m_rw_lnx_b, v_rw_lnx_b), (None, None, None)]]),
    }
    rep = {}
    for layer in ("odd", "even"):
        nms, params = groups[layer]
        gathered_g = send_wait(started[layer + "_small"], done, False, EVERY, layer + "_small_grads_wait")
        for nm, res in zip(nms, adam_replicated(gathered_g, params, "adam_" + layer + "_small")):
            rep[nm] = res
        done = rep[nms[0]][0]
    native = {"sg_w": sg_w.shape, "sg_b": sg_b.shape, "final_g": final_g.shape, "rw_rk": rw_rk.shape,
              "att_bias": att_bias.shape}
    for nm, shape in native.items():
        rep[nm] = [a.reshape(shape) for a in rep[nm]]
    rep["norm_g"] = [jnp.concatenate([a, b], axis=0) for a, b in zip(rep["norm_g0"], rep["norm_g1"])]
    even_landed = send_wait(started["even"], done, True, CHIPS, "even_grads_wait")
    even_swap = sum_and_swap(even_names, even_landed, "even")
    update(odd_names, odd_swap, even_swap[-1], "odd")
    update(even_names, even_swap, sharded["w_in_o"][0], "even")

    order = ["norm_g", "w_in_e", "shift_mu", "rw_w0", "rw_w2", "rw_a0", "rw_a2", "rw_kk", "rw_ka", "rw_rk",
             "rw_lnx_g", "rw_lnx_b", "att_bias", "w_out_e", "w_in_o", "sg_ln_g", "sg_ln_b", "sg_w", "sg_b",
             "w_out_o", "final_g"]
    results = {**sharded, **rep}
    outs = [rep["loss"][0][0, 0], dx.reshape(NSEQ, SEQ, D)]
    for kind in range(4):
        outs += [results[nm][kind] for nm in order]
    return tuple(outs)


def _local_step(x2, tgt, wie_t, late_weights, w2, a2, sglg, sglb, norm_g, shift_mu, rw_w0, rw_a0, rw_kk, rw_ka, rw_rk,
                rw_lnx_g, rw_lnx_b, att_bias, sg_w, sg_b, final_g, first_after=None, on_odd_grads=None,
                on_even_grads=None, on_small_grads=None):
    zl = jnp.zeros((LORA, W), F32)
    w2x = jnp.concatenate([w2, zl], axis=0)
    a2x = jnp.concatenate([zl, a2], axis=0)
    rk = rw_rk.reshape(1, W)
    pos = np.arange(SGC)
    sg_mask = jnp.asarray(((pos[None, :] // L) <= (pos[:, None] // L)).astype(np.float32))
    wm = (sg_w[0] * sg_mask[None]).astype(BF16)
    sgb_t = sg_b[0].T

    xn0, ps, ga, q, kb, vb, gb = ln_in_proj(x2, norm_g[0:1], wie_t, EVEN_SPLITS, "in_proj_even", after=first_after,
                                            w_t=True, bf16_pieces=(2, 3, 4))
    r, lw, k2, v, aa, bb = even_prep(ps, shift_mu, rw_w0, w2x, rw_a0, a2x, rw_kk, rw_ka)
    y, rw_saved = rwkv_fwd(r, lw, k2, v, aa, bb)
    bias = bias_expand(att_bias[0])

    o = attention_fwd(q, kb, vb, bias)
    woe = late_weights("even", o)
    h1, zt = even_post(y, r, k2, v, ga, o, gb, rw_lnx_g, rw_lnx_b, rk, x2, woe)
    wio, woo = late_weights("odd", h1)
    xn1, u, vv, gt = ln_in_proj(h1, norm_g[1:2], wio, ODD_SPLITS, "in_proj_odd")
    dh2, loss_part, d_final_g, z2t = gmlp_fwd_loss(u, vv, gt, sglg, sglb, wm, sgb_t, h1, woo, final_g[None], tgt)

    du, dvv, dgt, d_sglg, d_sglb, d_wm, d_sgb_t, d_woo = gmlp_bwd(u, vv, gt, sglg, sglb, wm, sgb_t, dh2, z2t, woo)
    dp_odd = [du, dvv, dgt]
    d_wio = matmul_acc_chips(xn1, dp_odd, "in_proj_odd_dw")
    token = on_odd_grads(d_woo, d_wio) if on_odd_grads else None
    dh1, d_g1 = in_proj_bwd_x(h1, norm_g[1:2], wio, dp_odd, dh2, "in_proj_odd_bwd", after=token)
    odd_small = [d_wm * sg_mask[None], d_sgb_t.T, d_final_g, d_g1]
    token = on_small_grads("odd", odd_small) if on_small_grads else None
    dy, dr2, dk22, dv2, dga, do, dgb, d_lng, d_lnb, d_rk, d_woe = even_post_bwd(
        y, r, k2, v, ga, o, gb, rw_lnx_g, rw_lnx_b, rk, dh1, zt, woe, after=token)
    dq, dkb, dvb, dbias = attention_bwd(q, kb, vb, bias, do)
    dbias = sum(dbias[:, i * 2 * L:(i + 1) * 2 * L, i * L:i * L + BAND] for i in range(ATT_Q))
    d_att_bias = bias_grad(dbias.reshape(NH, L, BAND))
    dr, dlw, dk2, dv, daa, dbb = rwkv_bwd(r, lw, k2, aa, bb, rw_saved, dy)
    dps, d_mu, d_w0, d_w2x, d_a0, d_a2x, d_kk, d_ka = even_prep_bwd(
        ps, shift_mu, rw_w0, w2x, rw_a0, a2x, rw_kk, rw_ka, dr, dlw, dk2, dv, daa, dbb, dr2, dk22, dv2)
    dp_even = [dps, dga, dq, dkb, dvb, dgb]
    d_wie = matmul_acc_chips(xn0, dp_even, "in_proj_even_dw", add_cores=on_even_grads is not None)
    big_g = (d_wie, d_woe, d_wio, d_woo, d_w2x[:LORA], d_a2x[LORA:], d_sglg, d_sglb)
    token = on_even_grads(big_g) if on_even_grads else None
    dx, d_g0 = in_proj_bwd_x(x2, norm_g[0:1], wie_t, dp_even, dh1, "in_proj_even_bwd", after=token, w_t=True)
    even_small = [d_g0, d_mu, d_w0, d_a0, d_kk, d_ka, d_rk, d_lng, d_lnb, d_att_bias]
    if on_small_grads:
        on_small_grads("even", even_small + [loss_part[0:1, :]])
    rep_g = [jnp.concatenate([d_g0, d_g1], axis=0)] + even_small[1:] + odd_small[:3]
    return loss_part[0, 0], dx, big_g, rep_g
```

```python
import functools
import math

import jax
import jax.numpy as jnp
import numpy as np
from jax import lax
from jax.experimental import pallas as pl
from jax.experimental.pallas import tpu as pltpu

F32 = jnp.float32
BF16 = jnp.bfloat16
HI = lax.Precision.HIGHEST

D = 1024
SEQ = 2048
NSEQ = 2
T = NSEQ * SEQ
HD = 64
NH = 8
W = 512
SHIFT = 1664
LORA = 64
EVEN_IN = 4224
ODD_IN = 3072
L = 64
NC = SEQ // L
LEFT = 8
BAND = (LEFT + 1) * L
CLIP = 128
SGC = 128
NG = 8
RMS_EPS = 1e-6
LN_EPS = 1e-5
GN_EPS = 64e-5
NEG = -1e30
VMEM_BIG = 56 * 1024 * 1024

ADAM_LR = 0.001
ADAM_B1 = 0.9
ADAM_B2 = 0.999
ADAM_EPS = 1e-08
ADAM_WD = 0.01
ADAM_STEP = 10

MESH = pl.DeviceIdType.MESH


def _bdot(a, b):
    return jnp.dot(a.astype(BF16), b.astype(BF16), preferred_element_type=F32)


def _bdot_nt(a, b):
    return lax.dot_general(a.astype(BF16), b.astype(BF16), (((1,), (1,)), ((), ())), preferred_element_type=F32)


def _bdot_tn(a, b):
    return lax.dot_general(a.astype(BF16), b.astype(BF16), (((0,), (0,)), ((), ())), preferred_element_type=F32)


def _hdot(a, b):
    return jnp.dot(a, b, precision=HI, preferred_element_type=F32)


def _hdot_nt(a, b):
    return lax.dot_general(a, b, (((1,), (1,)), ((), ())), precision=HI, preferred_element_type=F32)


def _hdot_tn(a, b):
    return lax.dot_general(a, b, (((0,), (0,)), ((), ())), precision=HI, preferred_element_type=F32)


def _iota2(shape, dim):
    return lax.broadcasted_iota(jnp.int32, shape, dim)


def _head_blockdiag():
    r = _iota2((2 * HD, 2 * HD), 0) // HD
    c = _iota2((2 * HD, 2 * HD), 1) // HD
    return (r == c).astype(BF16)


def _headsum_impl(x, bd):
    hi = x.astype(BF16)
    mid = (x - hi.astype(F32)).astype(BF16)
    n = bd.shape[0]
    out = [jnp.dot(hi[:, i:i + n], bd, preferred_element_type=F32) + jnp.dot(mid[:, i:i + n], bd, preferred_element_type=F32)
           for i in range(0, x.shape[1], n)]
    return jnp.concatenate(out, axis=-1)


@jax.custom_vjp
def _headsum(x, bd):
    return _headsum_impl(x, bd)


def _headsum_fwd(x, bd):
    return _headsum_impl(x, bd), bd


def _headsum_bwd(bd, ct):
    return _headsum_impl(ct, bd), None


_headsum.defvjp(_headsum_fwd, _headsum_bwd)


def _silu(x):
    return x * jax.nn.sigmoid(x)


_GELU_C = math.sqrt(2.0 / math.pi)


def _gelu(x):
    return 0.5 * x * (1.0 + jnp.tanh(_GELU_C * (x + 0.044715 * (x * x * x))))


def _silu_both(x):
    s = jax.nn.sigmoid(x)
    xs = x * s
    return xs, s + xs * (1.0 - s)


def _gelu_both(x):
    x2 = x * x
    t = jnp.tanh(_GELU_C * (x + 0.044715 * (x2 * x)))
    half = 0.5 * (1.0 + t)
    return x * half, half + 0.5 * x * (1.0 - t * t) * _GELU_C * (1.0 + 3.0 * 0.044715 * x2)


def _softplus(x):
    return jnp.maximum(x, 0.0) + jnp.log(1.0 + jnp.exp(-jnp.abs(x)))


def _cparams(sem, vmem=None):
    return pltpu.CompilerParams(dimension_semantics=sem, vmem_limit_bytes=vmem)


def _row_spec(tm, width):
    return pl.BlockSpec((tm, width), lambda i: (i, 0))


def _col_spec(height, tm):
    return pl.BlockSpec((height, tm), lambda i: (0, i))


def _const_spec(shape):
    nd = len(shape)
    return pl.BlockSpec(shape, lambda *_: (0,) * nd)


def _weight_dims(w_bf, w_t):
    if w_bf.ndim == 3:
        return None, w_bf.shape[0] * w_bf.shape[2]
    return (((1,), (1,)), ((), ())) if w_t else (((1,), (0,)), ((), ())), w_bf.shape[0 if w_t else 1]


def _proj(xn, w_ref, dims):
    if dims is None:
        return jnp.concatenate([jnp.dot(xn, w_ref[s], preferred_element_type=F32) for s in range(w_ref.shape[0])],
                               axis=-1)
    return lax.dot_general(xn, w_ref[...], dims, preferred_element_type=F32)


def _proj_back(dp, w_ref, w_t):
    nt = (((1,), (1,)), ((), ()))
    if len(w_ref.shape) == 3:
        nb = w_ref.shape[2]
        parts = [lax.dot_general(dp[:, s * nb:(s + 1) * nb], w_ref[s], nt, preferred_element_type=F32)
                 for s in range(w_ref.shape[0])]
        return sum(parts[1:], parts[0])
    return lax.dot_general(dp, w_ref[...], (((1,), (0,)), ((), ())) if w_t else nt, preferred_element_type=F32)


def ln_in_proj(x, g, w_bf, splits, name, after=None, w_t=False, bf16_pieces=()):
    dims, n = _weight_dims(w_bf, w_t)
    dtypes = [BF16 if i in bf16_pieces else F32 for i in range(len(splits))]
    tm = 512 if n <= ODD_IN else 256
    spans = []
    o = 0
    for s in splits:
        spans.append((o, o + s))
        o += s
    assert o == n
    extra_specs, extra = _after_operand(after)

    def body(x_ref, g_ref, w_ref, *rest):
        xn_ref, outs = rest[len(extra)], rest[len(extra) + 1:]
        xv = x_ref[...]
        rstd = lax.rsqrt(jnp.mean(xv * xv, axis=-1, keepdims=True) + RMS_EPS)
        xn = (xv * rstd * g_ref[...]).astype(BF16)
        xn_ref[...] = xn.T
        p = _proj(xn, w_ref, dims)
        for o_ref, (a, b) in zip(outs, spans):
            o_ref[...] = p[:, a:b].astype(o_ref.dtype)

    return pl.pallas_call(
        body, grid=(T // tm,), name=name,
        in_specs=[_row_spec(tm, D), _const_spec((1, D)), _const_spec(w_bf.shape)] + extra_specs,
        out_specs=[_col_spec(D, tm)] + [_row_spec(tm, s) for s in splits],
        out_shape=[jax.ShapeDtypeStruct((D, T), BF16)]
        + [jax.ShapeDtypeStruct((T, s), dt) for s, dt in zip(splits, dtypes)],
        compiler_params=_cparams(("parallel",), VMEM_BIG),
    )(x, g, w_bf, *extra)


def in_proj_bwd_x(x, g, w_bf, dps, dres, name, after=None, w_t=False):
    tm = 512
    widths = [d.shape[1] for d in dps]
    extra_specs, extra = _after_operand(after)

    def body(x_ref, g_ref, w_ref, dres_ref, *rest):
        dp_refs = rest[:len(widths)]
        dx_ref, dg_ref = rest[-2:]
        dp = jnp.concatenate([r[...] for r in dp_refs], axis=-1)
        dxn = _proj_back(dp, w_ref, w_t)
        xv = x_ref[...]
        rstd = lax.rsqrt(jnp.mean(xv * xv, axis=-1, keepdims=True) + RMS_EPS)
        xhat = xv * rstd
        dgp = jnp.sum(dxn * xhat, axis=0, keepdims=True)

        @pl.when(pl.program_id(0) == 0)
        def _():
            dg_ref[...] = jnp.zeros_like(dg_ref)

        dg_ref[...] += dgp
        dxh = dxn * g_ref[...]
        dx_ref[...] = dres_ref[...] + rstd * (dxh - xhat * jnp.mean(dxh * xhat, axis=-1, keepdims=True))

    return pl.pallas_call(
        body, grid=(T // tm,), name=name,
        in_specs=[_row_spec(tm, D), _const_spec((1, D)), _const_spec(w_bf.shape), _row_spec(tm, D)]
        + [_row_spec(tm, s) for s in widths] + extra_specs,
        out_specs=[_row_spec(tm, D), _const_spec((1, D))],
        out_shape=[jax.ShapeDtypeStruct((T, D), F32), jax.ShapeDtypeStruct((1, D), F32)],
        compiler_params=_cparams(("arbitrary",), VMEM_BIG),
    )(x, g, w_bf, dres, *dps, *extra)


def _after_operand(after):
    return ([ANY], [after]) if after is not None else ([], [])


def matmul_acc_chips(at_bf, pieces, name, after=None, add_cores=False):
    k = at_bf.shape[0]
    widths = [p.shape[1] for p in pieces]
    nb = sum(widths) // NCHIP
    tm = 512
    steps = T // tm
    half = k // 2
    extra_specs, extra = _after_operand(after)

    def body(a_ref, *rest):
        o_ref, acc = rest[len(widths) + len(extra):][:2]

        @pl.when(pl.program_id(0) == 0)
        def _():
            acc[...] = jnp.zeros_like(acc)

        a = a_ref[...]
        b = jnp.concatenate([r[...] for r in rest[:len(widths)]], axis=-1)
        for s in range(NCHIP):
            acc[s] += jnp.dot(a, b[:, s * nb:(s + 1) * nb], preferred_element_type=F32)

        @pl.when(pl.program_id(0) == steps - 1)
        def _():
            if not add_cores:
                o_ref[...] = acc[...].astype(BF16)
            else:
                give, got, send, recv = rest[-4:]
                x, y, c = lax.axis_index("x"), lax.axis_index("y"), lax.axis_index("c")
                theirs = pl.multiple_of((1 - c) * half, half)
                mine = pl.multiple_of(c * half, half)
                give[...] = acc[:, pl.ds(theirs, half), :].astype(BF16)
                cp = pltpu.make_async_remote_copy(src_ref=give, dst_ref=got, send_sem=send, recv_sem=recv,
                                                  device_id=(x, y, 1 - c), device_id_type=MESH)
                cp.start()
                cp.wait()
                o_ref[...] = (acc[:, pl.ds(mine, half), :] + got[...].astype(F32)).astype(BF16)

    out_rows = half if add_cores else k
    exchange = [pltpu.VMEM((NCHIP, half, nb), BF16)] * 2 + [pltpu.SemaphoreType.DMA] * 2 if add_cores else []
    return pl.pallas_call(
        body, grid=(steps,), name=name,
        in_specs=[_col_spec(k, tm)] + [_row_spec(tm, w_) for w_ in widths] + extra_specs,
        out_specs=_const_spec((NCHIP, out_rows, nb)),
        out_shape=jax.ShapeDtypeStruct((NCHIP, out_rows, nb), BF16),
        scratch_shapes=[pltpu.VMEM((NCHIP, k, nb), F32)] + exchange,
        compiler_params=_cparams(("arbitrary",), VMEM_BIG),
    )(at_bf, *pieces, *extra)


def _out_proj_back(dh_ref, zt_ref, w_ref, dw_ref, acc_ref):
    dhb = dh_ref[...].astype(BF16)

    @pl.when(pl.program_id(0) == 0)
    def _():
        acc_ref[...] = jnp.zeros_like(acc_ref)

    acc_ref[...] += jnp.dot(zt_ref[...], dhb, preferred_element_type=F32)

    @pl.when(pl.program_id(0) == pl.num_programs(0) - 1)
    def _():
        dw_ref[...] = acc_ref[...].astype(dw_ref.dtype)

    return lax.dot_general(dhb, w_ref[...], (((1,), (1,)), ((), ())), preferred_element_type=F32)


PREP_TM = 512
PREP_NB = SEQ // PREP_TM


def _prep_elem(k, wl, apre, kkw, kaw, bd):
    wraw = -_softplus(-wl) - 0.5
    lw = -jnp.exp(wraw)
    asig = jax.nn.sigmoid(apre)
    kkr = k * kkw
    nrm = jnp.maximum(jnp.sqrt(_headsum(kkr * kkr, bd)), 1e-12)
    kk = kkr / nrm
    k2 = k * (1.0 + (asig - 1.0) * kaw)
    return lw, k2, -kk, kk * asig


def _prep_elem_bwd(k, wl, apre, kkw, kaw, bd, dlw, dk2, daa, dbb):
    s = -wl
    sp = _softplus(s)
    dwl = dlw * (-jnp.exp(-sp - 0.5)) * jnp.exp(s - sp)
    asig = jax.nn.sigmoid(apre)
    kkr = k * kkw
    root = jnp.sqrt(_headsum(kkr * kkr, bd))
    inv = 1.0 / jnp.maximum(root, 1e-12)
    kk = kkr * inv
    dkk = dbb * asig - daa
    dap = (dbb * kk + dk2 * k * kaw) * asig * (1.0 - asig)
    through_norm = jnp.where(root > 1e-12, kk * _headsum(dkk * kkr, bd) * inv, 0.0)
    dkkr = inv * (dkk - through_norm)
    gain = 1.0 + (asig - 1.0) * kaw
    dk = dkkr * kkw + dk2 * gain
    dkkw = jnp.sum(dkkr * k, axis=0, keepdims=True)
    dkaw = jnp.sum(dk2 * k * (asig - 1.0), axis=0, keepdims=True)
    return dk, dwl, dap, dkkw, dkaw


def _shifted(ps_ref, prev_ref, mu, blk):
    p = ps_ref[...]
    first = (blk % PREP_NB) == 0
    prev_row = jnp.where(first, 0.0, prev_ref[7:8, :])
    rolled = pltpu.roll(p, 1, 0)
    p_prev = jnp.where(_iota2(p.shape, 0) == 0, prev_row, rolled)
    return p, p_prev, p + (p_prev - p) * mu


def _prev_spec(width, blk_of):
    return pl.BlockSpec((8, width), lambda i: (jnp.maximum(blk_of(i) * (PREP_TM // 8) - 1, 0), 0))


def even_prep(ps, mu, w0, w2x, a0, a2x, kkw, kaw):
    tm = PREP_TM

    def body(ps_ref, prev_ref, mu_ref, w0_ref, w2_ref, a0_ref, a2_ref, kk_ref, ka_ref,
             r_ref, lw_ref, k2_ref, v_ref, aa_ref, bb_ref):
        _, _, s = _shifted(ps_ref, prev_ref, mu_ref[...], pl.program_id(0))
        wa = s[:, 3 * W:]
        wl = w0_ref[...] + _bdot(jnp.tanh(wa), w2_ref[...])
        apre = a0_ref[...] + _bdot(wa, a2_ref[...])
        lw, k2, aa, bb = _prep_elem(s[:, W:2 * W], wl, apre, kk_ref[...], ka_ref[...], _head_blockdiag())
        r_ref[...] = s[:, 0:W]
        v_ref[...] = s[:, 2 * W:3 * W]
        lw_ref[...] = lw
        k2_ref[...] = k2
        aa_ref[...] = aa
        bb_ref[...] = bb

    vec = _const_spec((1, W))
    return pl.pallas_call(
        body, grid=(T // tm,), name="even_prep",
        in_specs=[_row_spec(tm, SHIFT), _prev_spec(SHIFT, lambda i: i), _const_spec((1, SHIFT)), vec,
                  _const_spec((2 * LORA, W)), vec, _const_spec((2 * LORA, W)), vec, vec],
        out_specs=[_row_spec(tm, W)] * 6,
        out_shape=[jax.ShapeDtypeStruct((T, W), F32)] * 6,
        compiler_params=_cparams(("parallel",), VMEM_BIG),
    )(ps, ps, mu, w0, w2x, a0, a2x, kkw, kaw)


def even_prep_bwd(ps, mu, w0, w2x, a0, a2x, kkw, kaw, dr, dlw, dk2, dv, daa, dbb, dr2, dk22, dv2):
    tm = PREP_TM
    nb = T // tm
    rev = lambda i: nb - 1 - i

    def body(ps_ref, prev_ref, mu_ref, w0_ref, w2_ref, a0_ref, a2_ref, kk_ref, ka_ref,
             dr_ref, dlw_ref, dk2_ref, dv_ref, daa_ref, dbb_ref, dr2_ref, dk22_ref, dv2_ref,
             dps_ref, dmu_ref, dw0_ref, dw2_ref, da0_ref, da2_ref, dkk_ref, dka_ref, carry):
        i = pl.program_id(0)
        blk = rev(i)
        mu_v = mu_ref[...]
        p, p_prev, s = _shifted(ps_ref, prev_ref, mu_v, blk)
        wa = s[:, 3 * W:]
        th = jnp.tanh(wa)
        wl = w0_ref[...] + _bdot(th, w2_ref[...])
        apre = a0_ref[...] + _bdot(wa, a2_ref[...])
        bd = _head_blockdiag()
        k = s[:, W:2 * W]
        dk, dwl, dap, dkkw, dkaw = _prep_elem_bwd(k, wl, apre, kk_ref[...], ka_ref[...], bd, dlw_ref[...],
                                                  dk2_ref[...] + dk22_ref[...], daa_ref[...], dbb_ref[...])
        dwa = _bdot_nt(dwl, w2_ref[...]) * (1.0 - th * th) + _bdot_nt(dap, a2_ref[...])
        ds = jnp.concatenate([dr_ref[...] + dr2_ref[...], dk, dv_ref[...] + dv2_ref[...], dwa], axis=-1)

        @pl.when(i == 0)
        def _():
            for ref in (dmu_ref, dw0_ref, dw2_ref, da0_ref, da2_ref, dkk_ref, dka_ref, carry):
                ref[...] = jnp.zeros_like(ref)

        dmu_ref[...] += jnp.sum(ds * (p_prev - p), axis=0, keepdims=True)
        dw0_ref[...] += jnp.sum(dwl, axis=0, keepdims=True)
        da0_ref[...] += jnp.sum(dap, axis=0, keepdims=True)
        dw2_ref[...] += _bdot_tn(th, dwl)
        da2_ref[...] += _bdot_tn(wa, dap)
        dkk_ref[...] += dkkw
        dka_ref[...] += dkaw
        dsm = ds * mu_v
        last = (blk % PREP_NB) == PREP_NB - 1
        nxt = jnp.where(last, 0.0, carry[0:1, :])
        up = pltpu.roll(dsm, tm - 1, 0)
        up = jnp.where(_iota2(up.shape, 0) == tm - 1, nxt, up)
        dps_ref[...] = (ds - dsm + up).astype(BF16)
        carry[0:1, :] = dsm[0:1, :]

    vec = _const_spec((1, W))
    rrow = lambda width: pl.BlockSpec((tm, width), lambda i: (rev(i), 0))
    return pl.pallas_call(
        body, grid=(nb,), name="even_prep_bwd",
        in_specs=[rrow(SHIFT), _prev_spec(SHIFT, rev), _const_spec((1, SHIFT)), vec,
                  _const_spec((2 * LORA, W)), vec, _const_spec((2 * LORA, W)), vec, vec] + [rrow(W)] * 9,
        out_specs=[rrow(SHIFT), _const_spec((1, SHIFT)), vec, _const_spec((2 * LORA, W)), vec,
                   _const_spec((2 * LORA, W)), vec, vec],
        out_shape=[jax.ShapeDtypeStruct((T, SHIFT), BF16), jax.ShapeDtypeStruct((1, SHIFT), F32),
                   jax.ShapeDtypeStruct((1, W), F32), jax.ShapeDtypeStruct((2 * LORA, W), F32),
                   jax.ShapeDtypeStruct((1, W), F32), jax.ShapeDtypeStruct((2 * LORA, W), F32),
                   jax.ShapeDtypeStruct((1, W), F32), jax.ShapeDtypeStruct((1, W), F32)],
        scratch_shapes=[pltpu.VMEM((8, SHIFT), F32)],
        compiler_params=_cparams(("arbitrary",), VMEM_BIG),
    )(ps, ps, mu, w0, w2x, a0, a2x, kkw, kaw, dr, dlw, dk2, dv, daa, dbb, dr2, dk22, dv2)


NPAIR = NH // 2
PW = 2 * HD


def _pair_cols(p):
    return slice(p * PW, (p + 1) * PW)


def _pairs(a):
    return [a[:, _pair_cols(p)] for p in range(NPAIR)]


def _stack_pair(a):
    first = _iota2(a.shape, 1) < HD
    zero = jnp.zeros_like(a)
    return jnp.concatenate([jnp.where(first, a, zero), jnp.where(first, zero, a)], axis=0)


def _unstack_pair(a):
    n = a.shape[0] // 2
    return jnp.where(_iota2((n, PW), 1) < HD, a[:n], a[n:])


def _fold_pair(a):
    n = a.shape[0] // 2
    return a[:n] + a[n:]


def _chunk_masks():
    n = 4 * L
    row = _iota2((n, n), 0)
    col = _iota2((n, n), 1)
    same = ((row // L) & 1) == ((col // L) & 1)
    ri = row & (L - 1)
    ci = col & (L - 1)
    keep = same & (((row < 2 * L) & (ri > ci)) | ((row >= 2 * L) & (ri >= ci)))
    r1 = _iota2((L, L), 0)
    c1 = _iota2((L, L), 1)
    r2 = _iota2((2 * L, 2 * L), 0)
    c2 = _iota2((2 * L, 2 * L), 1)
    return keep.astype(F32), (r1 >= c1).astype(F32), (r2 == c2).astype(F32)


def _scaled(r, lw, k2, aa, bb, tri):
    g = _hdot(tri, lw)
    eg = jnp.exp(g)
    eng = jnp.exp(-g)
    egp = jnp.exp(g - lw)
    return eg, eng, egp, aa * egp, r * eg, bb * eng, k2 * eng


def _head_cols(h):
    return slice(h * HD, (h + 1) * HD)


def _per_head(a):
    return [a[:, _head_cols(h)] for h in range(NH)]


def _pairs_operands(at, rt, bt, kt):
    x = [jnp.concatenate([_stack_pair(a), _stack_pair(r)], axis=0).astype(BF16) for a, r in zip(_pairs(at), _pairs(rt))]
    yk = [jnp.concatenate([_stack_pair(b), _stack_pair(k)], axis=0).astype(BF16) for b, k in zip(_pairs(bt), _pairs(kt))]
    return x, yk


def _pairs_matrices(x, yk, keep, eye):
    m = [_bdot_nt(a, b) * keep for a, b in zip(x, yk)]
    p = [a[:2 * L, :2 * L] for a in m]
    tinv = [eye + a for a in p]
    for _ in range(5):
        p = [_bdot(a, a) for a in p]
        tinv = [t + _bdot(t, a) for t, a in zip(tinv, p)]
    return [a.astype(BF16) for a in m], [a.astype(BF16) for a in tinv]


def _pairs_fwd(x, yk, m, tinv, vw, s0, egl):
    xh = [_bdot_nt(a, s) for a, s in zip(x, s0)]
    u = [_bdot(t, h[:2 * L] + _bdot(a[:2 * L, 2 * L:], w)) for t, h, a, w in zip(tinv, xh, m, vw)]
    uv = [jnp.concatenate([a, w], axis=0).astype(BF16) for a, w in zip(u, vw)]
    y = [h[2 * L:] + _bdot(a[2 * L:], w) for h, a, w in zip(xh, m, uv)]
    sn = [e * (s + _bdot_tn(w, b)) for e, s, w, b in zip(egl, s0, uv, yk)]
    return y, sn, uv


def _pairs_bwd(x, yk, m, tinv, uv, s0, sn, egl, dyw, dsn, keep):
    dzs = [d * e for d, e in zip(dsn, egl)]
    dgl = [jnp.sum(d * s, axis=0, keepdims=True) for d, s in zip(dsn, sn)]
    dyb = [a.astype(BF16) for a in dyw]
    t1 = [_bdot_tn(a[2 * L:], d) for a, d in zip(m, dyb)]
    t2 = [_bdot_nt(b, d) for b, d in zip(yk, dzs)]
    drhs = [_bdot_tn(t, a[:2 * L] + b[:2 * L]) for t, a, b in zip(tinv, t1, t2)]
    dv = [a[2 * L:] + b[2 * L:] + _bdot_tn(c[:2 * L, 2 * L:], d) for a, b, c, d in zip(t1, t2, m, drhs)]
    gg = [jnp.concatenate([a, b], axis=0).astype(BF16) for a, b in zip(drhs, dyw)]
    ds0 = [d + _bdot_tn(g, a) for d, g, a in zip(dzs, gg, x)]
    dm = [_bdot_nt(g, w) * keep for g, w in zip(gg, uv)]
    dx = [_bdot(g, s) + _bdot(d, b) for g, s, d, b in zip(gg, s0, dm, yk)]
    dyk = [_bdot_tn(d, a) + _bdot(w, z) for d, a, w, z in zip(dm, x, uv, dzs)]
    return dx, dyk, dv, dgl, ds0


STATE_SHAPE = (NPAIR * PW, PW)
M_SHAPE = (4 * L, NPAIR * 4 * L)
TINV_SHAPE = (2 * L, NPAIR * 2 * L)


def _rows_of(a, n):
    return [a[i * n:(i + 1) * n, :] for i in range(NPAIR)]


def _both(f):
    out = []
    for s in range(NSEQ):
        out += f(s)
    return out


def _seq_view(a):
    return a.reshape(NSEQ, SEQ, a.shape[-1])


UV_SHAPE = (4 * L, NPAIR * PW)
RW_CHUNKS = 2


def rwkv_fwd(r, lw, k2, v, aa, bb):
    def body(r_ref, lw_ref, k2_ref, v_ref, aa_ref, bb_ref, y_ref, hs_ref, hn_ref, m_ref, t_ref, uv_ref, state):
        @pl.when(pl.program_id(0) == 0)
        def _():
            state[...] = jnp.zeros_like(state)

        keep, tri, eye = _chunk_masks()
        where = [(j, s) for j in range(RW_CHUNKS) for s in range(NSEQ)]
        rows = lambda j: slice(j * L, (j + 1) * L)
        sc = [_scaled(r_ref[s, rows(j)], lw_ref[s, rows(j)], k2_ref[s, rows(j)], aa_ref[s, rows(j)],
                      bb_ref[s, rows(j)], tri) for j, s in where]
        ops = [_pairs_operands(*a[3:]) for a in sc]
        m, tinv = _pairs_matrices([a for o in ops for a in o[0]], [a for o in ops for a in o[1]], keep, eye)
        s_cur = [state[s] for s in range(NSEQ)]
        for j in range(RW_CHUNKS):
            mine = slice(j * NSEQ * NPAIR, (j + 1) * NSEQ * NPAIR)
            x = [a for o in ops[j * NSEQ:(j + 1) * NSEQ] for a in o[0]]
            yk = [a for o in ops[j * NSEQ:(j + 1) * NSEQ] for a in o[1]]
            vw = _both(lambda s: [_stack_pair(a) for a in _pairs(v_ref[s, rows(j)])])
            egl = _both(lambda s: _pairs(sc[j * NSEQ + s][0][L - 1:L, :]))
            y, sn, uv = _pairs_fwd(x, yk, m[mine], tinv[mine], vw, _both(lambda s: _rows_of(s_cur[s], PW)), egl)
            for s in range(NSEQ):
                ps = slice(s * NPAIR, (s + 1) * NPAIR)
                hs_ref[j, s] = s_cur[s]
                y_ref[s, rows(j)] = jnp.concatenate([_fold_pair(a) for a in y[ps]], axis=-1)
                m_ref[j, s] = jnp.concatenate(m[mine][ps], axis=-1)
                t_ref[j, s] = jnp.concatenate(tinv[mine][ps], axis=-1)
                uv_ref[j, s] = jnp.concatenate(uv[ps], axis=-1)
                s_cur[s] = jnp.concatenate(sn[ps], axis=0)
                hn_ref[j, s] = s_cur[s]
        for s in range(NSEQ):
            state[s] = s_cur[s]

    blk = pl.BlockSpec((NSEQ, RW_CHUNKS * L, W), lambda c: (0, c, 0))
    per_chunk = lambda shape: pl.BlockSpec((RW_CHUNKS, NSEQ) + shape, lambda c: (c, 0, 0, 0))
    saved_shapes = [(STATE_SHAPE, F32), (STATE_SHAPE, F32), (M_SHAPE, BF16), (TINV_SHAPE, BF16), (UV_SHAPE, BF16)]
    y, *saved = pl.pallas_call(
        body, grid=(NC // RW_CHUNKS,), name="rwkv_fwd",
        in_specs=[blk] * 6,
        out_specs=[blk] + [per_chunk(shape) for shape, _ in saved_shapes],
        out_shape=[jax.ShapeDtypeStruct((NSEQ, SEQ, W), F32)]
        + [jax.ShapeDtypeStruct((NC, NSEQ) + shape, dt) for shape, dt in saved_shapes],
        scratch_shapes=[pltpu.VMEM((NSEQ,) + STATE_SHAPE, F32)],
        compiler_params=_cparams(("arbitrary",), VMEM_BIG),
    )(*[_seq_view(a) for a in (r, lw, k2, v, aa, bb)])
    return y.reshape(T, W), saved


def rwkv_bwd(r, lw, k2, aa, bb, saved, dy):
    def body(r_ref, lw_ref, k2_ref, aa_ref, bb_ref, hs_ref, hn_ref, m_ref, t_ref, uv_ref, dy_ref,
             dr_ref, dlw_ref, dk2_ref, dv_ref, daa_ref, dbb_ref, dstate):
        @pl.when(pl.program_id(0) == 0)
        def _():
            dstate[...] = jnp.zeros_like(dstate)

        keep, tri, _ = _chunk_masks()
        sc = [_scaled(r_ref[s], lw_ref[s], k2_ref[s], aa_ref[s], bb_ref[s], tri) for s in range(NSEQ)]
        ops = [_pairs_operands(*sc[s][3:]) for s in range(NSEQ)]
        x, yk = _both(lambda s: ops[s][0]), _both(lambda s: ops[s][1])
        m = _both(lambda s: [m_ref[0, s][:, i * 4 * L:(i + 1) * 4 * L] for i in range(NPAIR)])
        tinv = _both(lambda s: [t_ref[0, s][:, i * 2 * L:(i + 1) * 2 * L] for i in range(NPAIR)])
        uv = _both(lambda s: _pairs(uv_ref[0, s]))
        dyw = _both(lambda s: [_stack_pair(a) for a in _pairs(dy_ref[s])])
        s0 = _both(lambda s: _rows_of(hs_ref[0, s], PW))
        sn = _both(lambda s: _rows_of(hn_ref[0, s], PW))
        dsn = _both(lambda s: _rows_of(dstate[s], PW))
        egl = _both(lambda s: _pairs(sc[s][0][L - 1:L, :]))
        dx, dyk, dvw, dgl, ds0 = _pairs_bwd(x, yk, m, tinv, uv, s0, sn, egl, dyw, dsn, keep)
        for s in range(NSEQ):
            mine = slice(s * NPAIR, (s + 1) * NPAIR)
            eg, eng, egp, at, rt, bt, kt = sc[s]
            dstate[s] = jnp.concatenate(ds0[mine], axis=0)
            dv_ref[s] = jnp.concatenate([_fold_pair(a) for a in dvw[mine]], axis=-1)
            dat = jnp.concatenate([_fold_pair(a[:2 * L]) for a in dx[mine]], axis=-1)
            drt = jnp.concatenate([_fold_pair(a[2 * L:]) for a in dx[mine]], axis=-1)
            dbt = jnp.concatenate([_fold_pair(a[:2 * L]) for a in dyk[mine]], axis=-1)
            dkt = jnp.concatenate([_fold_pair(a[2 * L:]) for a in dyk[mine]], axis=-1)
            dg = drt * rt - dbt * bt - dkt * kt
            dg = dg + jnp.where(_iota2(dg.shape, 0) == L - 1, jnp.concatenate(dgl[mine], axis=-1), 0.0)
            dgp = dat * at
            dlw_ref[s] = _hdot_tn(tri, dg + dgp) - dgp
            dr_ref[s] = drt * eg
            daa_ref[s] = dat * egp
            dbb_ref[s] = dbt * eng
            dk2_ref[s] = dkt * eng

    blk = pl.BlockSpec((NSEQ, L, W), lambda c: (0, NC - 1 - c, 0))
    per_chunk = lambda shape: pl.BlockSpec((1, NSEQ) + shape, lambda c: (NC - 1 - c, 0, 0, 0))
    outs = pl.pallas_call(
        body, grid=(NC,), name="rwkv_bwd",
        in_specs=[blk] * 5 + [per_chunk(a.shape[2:]) for a in saved] + [blk],
        out_specs=[blk] * 6,
        out_shape=[jax.ShapeDtypeStruct((NSEQ, SEQ, W), F32)] * 6,
        scratch_shapes=[pltpu.VMEM((NSEQ,) + STATE_SHAPE, F32)],
        compiler_params=_cparams(("arbitrary",)),
    )(*[_seq_view(a) for a in (r, lw, k2, aa, bb)], *saved, _seq_view(dy))
    return [a.reshape(T, W) for a in outs]


def _post_math(y, r, k2, v, ga, o, gb, lng, lnb, rk, bd):
    mu = _headsum(y, bd) * (1.0 / HD)
    yc = y - mu
    var = _headsum(yc * yc, bd) * (1.0 / HD)
    yn = yc * lax.rsqrt(var + GN_EPS) * lng + lnb
    bonus = _headsum(r * k2 * rk, bd) * v
    return (yn + bonus) * _silu(ga), o * _silu(gb)


def even_post(y, r, k2, v, ga, o, gb, lng, lnb, rk, h, w_bf):
    tm = 512

    def body(y_ref, r_ref, k2_ref, v_ref, ga_ref, o_ref, gb_ref, lng_ref, lnb_ref, rk_ref, h_ref, w_ref,
             ho_ref, zt_ref):
        ya, yb = _post_math(y_ref[...], r_ref[...], k2_ref[...], v_ref[...], ga_ref[...], o_ref[...], gb_ref[...],
                            lng_ref[...], lnb_ref[...], rk_ref[...], _head_blockdiag())
        z = jnp.concatenate([ya.astype(BF16), yb.astype(BF16)], axis=-1)
        zt_ref[...] = z.T
        ho_ref[...] = h_ref[...] + jnp.dot(z, w_ref[...], preferred_element_type=F32)

    vec = _const_spec((1, W))
    return pl.pallas_call(
        body, grid=(T // tm,), name="even_post",
        in_specs=[_row_spec(tm, W)] * 7 + [vec] * 3 + [_row_spec(tm, D), _const_spec((D, D))],
        out_specs=[_row_spec(tm, D), _col_spec(D, tm)],
        out_shape=[jax.ShapeDtypeStruct((T, D), F32), jax.ShapeDtypeStruct((D, T), BF16)],
        compiler_params=_cparams(("parallel",), VMEM_BIG),
    )(y, r, k2, v, ga, o, gb, lng, lnb, rk, h, w_bf)


def even_post_bwd(y, r, k2, v, ga, o, gb, lng, lnb, rk, dh, zt_bf, w_bf, after=None):
    tm = 512
    extra_specs, extra = _after_operand(after)

    def body(y_ref, r_ref, k2_ref, v_ref, ga_ref, o_ref, gb_ref, lng_ref, lnb_ref, rk_ref, dh_ref, zt_ref, w_ref,
             *rest):
        (dy_ref, dr_ref, dk2_ref, dv_ref, dga_ref, do_ref, dgb_ref, dlng_ref, dlnb_ref, drk_ref, dw_ref,
         acc_ref) = rest[-12:]
        dzv = _out_proj_back(dh_ref, zt_ref, w_ref, dw_ref, acc_ref)
        bd = _head_blockdiag()
        _, vjp = jax.vjp(lambda *a: _post_math(*a, bd), y_ref[...], r_ref[...], k2_ref[...], v_ref[...], ga_ref[...],
                         o_ref[...], gb_ref[...], lng_ref[...], lnb_ref[...], rk_ref[...])
        dy, dr, dk2, dv, dga, do, dgb, dlng, dlnb, drk = vjp((dzv[:, 0:W], dzv[:, W:2 * W]))
        for ref, val in ((dy_ref, dy), (dr_ref, dr), (dk2_ref, dk2), (dv_ref, dv), (dga_ref, dga), (do_ref, do),
                         (dgb_ref, dgb)):
            ref[...] = val.astype(ref.dtype)

        @pl.when(pl.program_id(0) == 0)
        def _():
            for ref in (dlng_ref, dlnb_ref, drk_ref):
                ref[...] = jnp.zeros_like(ref)

        dlng_ref[...] += dlng
        dlnb_ref[...] += dlnb
        drk_ref[...] += drk

    vec = _const_spec((1, W))
    return pl.pallas_call(
        body, grid=(T // tm,), name="even_post_bwd",
        in_specs=[_row_spec(tm, W)] * 7 + [vec] * 3 + [_row_spec(tm, D), _col_spec(D, tm), _const_spec((D, D))]
        + extra_specs,
        out_specs=[_row_spec(tm, W)] * 7 + [vec] * 3 + [_const_spec((D, D))],
        out_shape=[jax.ShapeDtypeStruct((T, W), dt) for dt in (F32, F32, F32, F32, BF16, F32, BF16)]
        + [jax.ShapeDtypeStruct((1, W), F32)] * 3 + [jax.ShapeDtypeStruct((D, D), BF16)],
        scratch_shapes=[pltpu.VMEM((D, D), F32)],
        compiler_params=_cparams(("arbitrary",), VMEM_BIG),
    )(y, r, k2, v, ga, o, gb, lng, lnb, rk, dh, zt_bf, w_bf, *extra)


PADSEQ = SEQ + LEFT * L
ATT_SCALE = 1.0 / math.sqrt(HD)
ATT_Q = 4
WIN = BAND + (ATT_Q - 1) * L
ATT_STEPS = NC // ATT_Q
ATT_BIAS_SHAPE = (NPAIR, ATT_Q * 2 * L, WIN)
ATT_WINDOW_BIAS_SHAPE = (ATT_Q, NPAIR, 2 * L, WIN)


def _stack_chunks(a):
    return jnp.concatenate([_stack_pair(a[i * L:(i + 1) * L]) for i in range(ATT_Q)], axis=0)


def _unstack_chunks(a):
    return jnp.concatenate([_unstack_pair(a[i * 2 * L:(i + 1) * 2 * L]) for i in range(ATT_Q)], axis=0)


def _window_bias(b_ref):
    return [jnp.concatenate([b_ref[c, p] for c in range(ATT_Q)], axis=0) for p in range(NPAIR)]


def _key_window(ref, step):
    start = step * (ATT_Q * L) - LEFT * L
    rows = ref[pl.ds(pl.multiple_of(jnp.maximum(start, 0), L), WIN), :]
    window = rows
    for lead in range(ATT_Q * L, LEFT * L + 1, ATT_Q * L):
        moved = jnp.concatenate([rows[WIN - lead:], rows[:WIN - lead]], axis=0)
        window = jnp.where(start == -lead, moved, window)
    return window


def _att_probs(q2, kw, bias, step):
    valid = _iota2((1, WIN), 1) >= (LEFT - step * ATT_Q) * L
    s = [jnp.where(valid, _bdot_nt(a, b) * ATT_SCALE + bias[p], NEG) for p, (a, b) in enumerate(zip(q2, kw))]
    e = [jnp.exp(a - jnp.max(a, axis=-1, keepdims=True)) for a in s]
    return [a / jnp.sum(a, axis=-1, keepdims=True) for a in e]


def attention_fwd(q, k, v, bias):
    def body(q_ref, k_ref, v_ref, b_ref, o_ref):
        step = pl.program_id(1)
        kw = _pairs(_key_window(k_ref, step))
        vw = _pairs(_key_window(v_ref, step))
        q2 = [_stack_chunks(a) for a in _pairs(q_ref[...])]
        p = _att_probs(q2, kw, _window_bias(b_ref), step)
        o_ref[...] = jnp.concatenate([_unstack_chunks(_bdot(a, b)) for a, b in zip(p, vw)], axis=-1)

    qblk = pl.BlockSpec((ATT_Q * L, W), lambda b, c: (b * ATT_STEPS + c, 0))
    kblk = pl.BlockSpec((SEQ, W), lambda b, c: (b, 0))
    return pl.pallas_call(
        body, grid=(NSEQ, ATT_STEPS), name="attention_fwd",
        in_specs=[qblk, kblk, kblk, _const_spec(ATT_WINDOW_BIAS_SHAPE)],
        out_specs=qblk, out_shape=jax.ShapeDtypeStruct((T, W), F32),
        compiler_params=_cparams(("parallel", "arbitrary")),
    )(q, k, v, bias)


def attention_bwd(q, k, v, bias, do):
    def body(q_ref, k_ref, v_ref, b_ref, do_ref, dq_ref, dko_ref, dvo_ref, db_ref, dk_ref, dv_ref):
        b = pl.program_id(0)
        c = pl.program_id(1)

        @pl.when(c == 0)
        def _():
            dk_ref[...] = jnp.zeros_like(dk_ref)
            dv_ref[...] = jnp.zeros_like(dv_ref)

        @pl.when((c == 0) & (b == 0))
        def _():
            db_ref[...] = jnp.zeros_like(db_ref)

        start = pl.multiple_of(c * (ATT_Q * L), L)
        kw = _pairs(_key_window(k_ref, c))
        vw = _pairs(_key_window(v_ref, c))
        q2 = [_stack_chunks(a) for a in _pairs(q_ref[...])]
        do2 = [_stack_chunks(a) for a in _pairs(do_ref[...].astype(BF16))]
        p = _att_probs(q2, kw, _window_bias(b_ref), c)
        dp = [_bdot_nt(a, b) for a, b in zip(do2, vw)]
        ds = [a * (d - jnp.sum(d * a, axis=-1, keepdims=True)) for a, d in zip(p, dp)]
        dss = [(a * ATT_SCALE).astype(BF16) for a in ds]
        dq_ref[...] = jnp.concatenate([_unstack_chunks(_bdot(a, b)) for a, b in zip(dss, kw)], axis=-1).astype(BF16)
        dk_ref[pl.ds(start, WIN), :] += jnp.concatenate([_bdot_tn(a, b) for a, b in zip(dss, q2)], axis=-1)
        dv_ref[pl.ds(start, WIN), :] += jnp.concatenate([_bdot_tn(a, b) for a, b in zip(p, do2)], axis=-1)
        for i in range(NPAIR):
            db_ref[i] += ds[i]

        @pl.when(c == ATT_STEPS - 1)
        def _():
            dko_ref[...] = dk_ref[LEFT * L:, :].astype(BF16)
            dvo_ref[...] = dv_ref[LEFT * L:, :].astype(BF16)

    qblk = pl.BlockSpec((ATT_Q * L, W), lambda b, c: (b * ATT_STEPS + c, 0))
    sblk = pl.BlockSpec((SEQ, W), lambda b, c: (b, 0))
    bblk = _const_spec(ATT_BIAS_SHAPE)
    return pl.pallas_call(
        body, grid=(NSEQ, ATT_STEPS), name="attention_bwd",
        in_specs=[qblk, sblk, sblk, _const_spec(ATT_WINDOW_BIAS_SHAPE), qblk],
        out_specs=[qblk, sblk, sblk, bblk],
        out_shape=[jax.ShapeDtypeStruct((T, W), BF16), jax.ShapeDtypeStruct((T, W), BF16),
                   jax.ShapeDtypeStruct((T, W), BF16), jax.ShapeDtypeStruct(ATT_BIAS_SHAPE, F32)],
        scratch_shapes=[pltpu.VMEM((PADSEQ, W), F32), pltpu.VMEM((PADSEQ, W), F32)],
        compiler_params=_cparams(("arbitrary", "arbitrary"), VMEM_BIG),
    )(q, k, v, bias, do)


NTAB = 2 * CLIP + 1
EXT = BAND + L


def _ext_onehot():
    n = _iota2((EXT, NTAB), 0)
    m = _iota2((EXT, NTAB), 1)
    return (jnp.clip(BAND - 1 - n, -CLIP, CLIP) + CLIP == m).astype(F32)


def bias_expand(table):
    def body(t_ref, o_ref):
        ext = _hdot_nt(t_ref[...], _ext_onehot())
        ext = jnp.concatenate([ext, jnp.zeros((NH, WIN - EXT), F32)], axis=-1)
        col = _iota2((NH, WIN), 1)
        for c in range(ATT_Q):
            inside = (col >= c * L) & (col < c * L + BAND)
            for i in range(L):
                shift = (c * L - (L - 1 - i)) % WIN
                o_ref[c, :, i, :] = jnp.where(inside, pltpu.roll(ext, shift, 1) if shift else ext, NEG)

    out = pl.pallas_call(body, name="bias_expand", out_shape=jax.ShapeDtypeStruct((ATT_Q, NH, L, WIN), F32))(table)
    return out.reshape(ATT_WINDOW_BIAS_SHAPE)


def bias_grad(dbias):
    def body(d_ref, o_ref):
        acc = jnp.zeros((NH, EXT), F32)
        zpad = jnp.zeros((NH, EXT - BAND), F32)
        for i in range(L):
            s = L - 1 - i
            row = jnp.concatenate([d_ref[:, i, :], zpad], axis=-1)
            acc = acc + (pltpu.roll(row, s, 1) if s else row)
        o_ref[...] = _hdot(acc, _ext_onehot())

    return pl.pallas_call(body, name="bias_grad", out_shape=jax.ShapeDtypeStruct((NH, NTAB), F32))(dbias)


def _group_cols(g):
    return slice(g * SGC, (g + 1) * SGC)


def _sg_norm(gv, lng, lnb):
    gc = gv - jnp.mean(gv, axis=-1, keepdims=True)
    rstd = lax.rsqrt(jnp.mean(gc * gc, axis=-1, keepdims=True) + LN_EPS)
    xhat = gc * rstd
    return xhat, rstd, xhat * lng + lnb


GMLP_BWD_CHUNKS = 2


def gmlp_fwd_loss(u, v, gate, lng, lnb, wm_bf, sgb_t, h, w_bf, g_final, target):
    tm = GMLP_BWD_CHUNKS * SGC

    def body(u_ref, v_ref, gt_ref, lng_ref, lnb_ref, wm_ref, sb_ref, h_ref, w_ref, g_ref, t_ref,
             dh_ref, loss_ref, dg_ref, zt_ref):
        zs = []
        for ch in range(GMLP_BWD_CHUNKS):
            rows = slice(ch * SGC, (ch + 1) * SGC)
            _, _, vln = _sg_norm(_gelu(v_ref[rows, :]), lng_ref[...], lnb_ref[...])
            vlb = vln.astype(BF16)
            zg = []
            for g in range(NG):
                cs = _group_cols(g)
                sv = jnp.dot(wm_ref[g], vlb[:, cs], preferred_element_type=F32) + sb_ref[:, g:g + 1]
                zg.append((_gelu(u_ref[rows, cs]) * sv * _silu(gt_ref[rows, cs])).astype(BF16))
            zs.append(jnp.concatenate(zg, axis=-1))
        z = jnp.concatenate(zs, axis=0)
        zt_ref[...] = z.T
        xv = h_ref[...] + jnp.dot(z, w_ref[...], preferred_element_type=F32)
        rstd = lax.rsqrt(jnp.mean(xv * xv, axis=-1, keepdims=True) + RMS_EPS)
        xhat = xv * rstd
        err = xhat * g_ref[...] - t_ref[...]
        part = 0.5 * jnp.sum(jnp.mean(err * err, axis=-1, keepdims=True), axis=0, keepdims=True)
        dout = err * (1.0 / D)

        @pl.when(pl.program_id(0) == 0)
        def _():
            loss_ref[...] = jnp.zeros_like(loss_ref)
            dg_ref[...] = jnp.zeros_like(dg_ref)

        loss_ref[...] += jnp.broadcast_to(part, loss_ref.shape)
        dg_ref[...] += jnp.sum(dout * xhat, axis=0, keepdims=True)
        dxh = dout * g_ref[...]
        dh_ref[...] = rstd * (dxh - xhat * jnp.mean(dxh * xhat, axis=-1, keepdims=True))

    return pl.pallas_call(
        body, grid=(T // tm,), name="gmlp_fwd_loss",
        in_specs=[_row_spec(tm, D)] * 3 + [_const_spec((1, D))] * 2
        + [_const_spec((NG, SGC, SGC)), _const_spec((SGC, NG)), _row_spec(tm, D), _const_spec((D, D)),
           _const_spec((1, D)), _row_spec(tm, D)],
        out_specs=[_row_spec(tm, D), _const_spec((8, 128)), _const_spec((1, D)), _col_spec(D, tm)],
        out_shape=[jax.ShapeDtypeStruct((T, D), F32), jax.ShapeDtypeStruct((8, 128), F32),
                   jax.ShapeDtypeStruct((1, D), F32), jax.ShapeDtypeStruct((D, T), BF16)],
        compiler_params=_cparams(("arbitrary",), VMEM_BIG),
    )(u, v, gate, lng, lnb, wm_bf, sgb_t, h, w_bf, g_final, target)


def gmlp_bwd(u, v, gate, lng, lnb, wm_bf, sgb_t, dh, zt_bf, w_bf):
    def body(u_ref, v_ref, gt_ref, lng_ref, lnb_ref, wm_ref, sb_ref, dh_ref, zt_ref, w_ref,
             du_ref, dv_ref, dgt_ref, dlng_ref, dlnb_ref, dwm_ref, dsb_ref, dw_ref, acc_ref):
        @pl.when(pl.program_id(0) == 0)
        def _():
            for ref in (dlng_ref, dlnb_ref, dwm_ref, dsb_ref):
                ref[...] = jnp.zeros_like(ref)

        dz = _out_proj_back(dh_ref, zt_ref, w_ref, dw_ref, acc_ref)
        sel = (_iota2((D, NG), 0) // SGC == _iota2((D, NG), 1)).astype(F32)
        for ch in range(GMLP_BWD_CHUNKS):
            rows = slice(ch * SGC, (ch + 1) * SGC)
            gv, dgv_dv = _gelu_both(v_ref[rows, :])
            xhat, rstd, vln = _sg_norm(gv, lng_ref[...], lnb_ref[...])
            vlb = vln.astype(BF16)
            dvln = []
            dsv_all = []
            for g in range(NG):
                cs = _group_cols(g)
                uu = u_ref[rows, cs]
                gg = gt_ref[rows, cs]
                dzz = dz[rows, cs]
                sv = jnp.dot(wm_ref[g], vlb[:, cs], preferred_element_type=F32) + sb_ref[:, g:g + 1]
                gu, dgu = _gelu_both(uu)
                sg, dsg = _silu_both(gg)
                dzgu = dzz * gu
                dsv = dzgu * sg
                dgt_ref[rows, cs] = (dzgu * sv * dsg).astype(BF16)
                du_ref[rows, cs] = (dzz * sv * sg * dgu).astype(BF16)
                dsb16 = dsv.astype(BF16)
                dvln.append(lax.dot_general(wm_ref[g], dsb16, (((0,), (0,)), ((), ())), preferred_element_type=F32))
                dwm_ref[g] += lax.dot_general(dsb16, vlb[:, cs], (((1,), (1,)), ((), ())),
                                              preferred_element_type=F32)
                dsv_all.append(dsv)
            dvl = jnp.concatenate(dvln, axis=-1)
            dsb_ref[...] += _hdot(jnp.concatenate(dsv_all, axis=-1), sel)
            dlng_ref[...] += jnp.sum(dvl * xhat, axis=0, keepdims=True)
            dlnb_ref[...] += jnp.sum(dvl, axis=0, keepdims=True)
            dxh = dvl * lng_ref[...]
            dgv = rstd * (dxh - jnp.mean(dxh, axis=-1, keepdims=True)
                          - xhat * jnp.mean(dxh * xhat, axis=-1, keepdims=True))
            dv_ref[rows, :] = (dgv * dgv_dv).astype(BF16)

    tm = GMLP_BWD_CHUNKS * SGC
    return pl.pallas_call(
        body, grid=(T // tm,), name="gmlp_bwd",
        in_specs=[_row_spec(tm, D)] * 3 + [_const_spec((1, D))] * 2
        + [_const_spec((NG, SGC, SGC)), _const_spec((SGC, NG)), _row_spec(tm, D), _col_spec(D, tm),
           _const_spec((D, D))],
        out_specs=[_row_spec(tm, D)] * 3 + [_const_spec((1, D))] * 2
        + [_const_spec((NG, SGC, SGC)), _const_spec((SGC, NG)), _const_spec((D, D))],
        out_shape=[jax.ShapeDtypeStruct((T, D), BF16)] * 3 + [jax.ShapeDtypeStruct((1, D), F32)] * 2
        + [jax.ShapeDtypeStruct((NG, SGC, SGC), F32), jax.ShapeDtypeStruct((SGC, NG), F32),
           jax.ShapeDtypeStruct((D, D), BF16)],
        scratch_shapes=[pltpu.VMEM((D, D), F32)],
        compiler_params=_cparams(("arbitrary",), VMEM_BIG),
    )(u, v, gate, lng, lnb, wm_bf, sgb_t, dh, zt_bf, w_bf)


NCHIP = 4
NDEV = 8
ANY = pl.BlockSpec(memory_space=pl.ANY)


HBM = pl.BlockSpec(memory_space=pltpu.HBM)
SEM = pl.BlockSpec(memory_space=pltpu.SEMAPHORE)
EFFECT = pltpu.SideEffectType.DATAFLOW_SIDE_EFFECTING


CHIPS, EVERY, SIBLING = "chips", "every", "sibling"
SLOTS = {CHIPS: NCHIP, EVERY: NDEV, SIBLING: 1}


def _peers(scope):
    x, y, c = lax.axis_index("x"), lax.axis_index("y"), lax.axis_index("c")
    if scope == SIBLING:
        return [((x, y, 1 - c), 0)], 0
    if scope == CHIPS:
        return [((px, py, c), 2 * px + py) for px, py in ((1 - x, y), (x, 1 - y), (1 - x, 1 - y))], 2 * x + y
    out = []
    for j in range(1, NDEV):
        px, py, pc = x ^ (j >> 2), y ^ ((j >> 1) & 1), c ^ (j & 1)
        out.append(((px, py, pc), 4 * px + 2 * py + pc))
    return out, 4 * x + 2 * y + c


def _send_copies(src, land, send, recv, scatter, scope, starting):
    peers, me = _peers(scope)
    copies = []
    for t in range(len(src)):
        for j, (dev, slot) in enumerate(peers):
            k = t * len(peers) + j
            copies.append(pltpu.make_async_remote_copy(
                src_ref=src[t].at[slot] if scatter else src[t], dst_ref=land[t].at[me if starting else slot],
                send_sem=send.at[k], recv_sem=recv.at[k], device_id=dev, device_id_type=MESH))
    return copies


def _own_copies(src, land, sems, scatter, scope):
    if scope == SIBLING:
        return []
    _, me = _peers(scope)
    return [pltpu.make_async_copy(src[t].at[me] if scatter else src[t], land[t].at[me], sems.at[t])
            for t in range(len(src))]


def send_start(srcs, scatter, scope, name, after=None):
    n = len(srcs)
    slots = SLOTS[scope]
    extra_specs, extra = _after_operand(after)
    lands = [pltpu.HBM(a.shape if scatter else (slots,) + a.shape, a.dtype) for a in srcs]
    sems = [pltpu.SemaphoreType.DMA((n * max(slots - 1, 1),))] * 2 + ([] if scope == SIBLING else
                                                                     [pltpu.SemaphoreType.DMA((n,))])
    k = len(sems)

    def body(*refs):
        first_out = n + len(extra)
        src, land = refs[:n], refs[first_out + k + n:first_out + k + 2 * n]
        for cp in _send_copies(src, land, refs[first_out], refs[first_out + 1], scatter, scope, True):
            cp.start()
        for cp in _own_copies(src, land, refs[first_out + k - 1], scatter, scope):
            cp.start()
        refs[-1][...] = jnp.zeros_like(refs[-1])

    out = pl.pallas_call(
        body, name=name,
        out_shape=(*sems, *[pltpu.HBM(a.shape, a.dtype) for a in srcs], *lands, jax.ShapeDtypeStruct((8, 128), F32)),
        in_specs=[HBM] * n + extra_specs,
        out_specs=(*[SEM] * k, *[HBM] * (2 * n), pl.BlockSpec(memory_space=pltpu.VMEM)),
        input_output_aliases={i: k + i for i in range(n)},
        compiler_params=pltpu.CompilerParams(has_side_effects=EFFECT),
    )(*[pltpu.with_memory_space_constraint(a, pltpu.HBM) for a in srcs], *extra)
    return list(out[:k]), list(out[k:k + n]), list(out[k + n:k + 2 * n]), out[-1]


def send_wait(started, after, scatter, scope, name, with_sources=False, only=None):
    sems, srcs, lands, _ = started
    n, k = len(srcs), len(sems)
    wanted = range(n) if only is None else only

    def body(*refs):
        src, land = refs[:n], refs[n:2 * n]
        for t, cp in enumerate(_own_copies(src, land, refs[2 * n + k - 1], scatter, scope)):
            if t in wanted:
                cp.wait()
        copies = _send_copies(src, land, refs[2 * n], refs[2 * n + 1], scatter, scope, False)
        for i, cp in enumerate(copies):
            if i // (len(copies) // n) in wanted:
                cp.wait_send()
                cp.wait_recv()

    arrs = list(srcs) + list(lands)
    out = pl.pallas_call(
        body, name=name, out_shape=tuple(pltpu.HBM(a.shape, a.dtype) for a in arrs),
        in_specs=[HBM] * (2 * n) + [SEM] * k + [ANY], out_specs=tuple([HBM] * (2 * n)),
        input_output_aliases={i: i for i in range(2 * n)},
        compiler_params=pltpu.CompilerParams(has_side_effects=EFFECT),
    )(*arrs, *sems, after)
    return (list(out[:n]), list(out[n:])) if with_sources else list(out[n:])


def gather_weights(arrs, split):
    n = len(arrs)

    def body(*refs):
        ins, outs = refs[:n], refs[n:2 * n]
        send1, recv1, send2, recv2, loc_in, loc_out = refs[2 * n:2 * n + 6]
        staged = refs[2 * n + 6:]
        x, y, c = lax.axis_index("x"), lax.axis_index("y"), lax.axis_index("c")
        me = 2 * x + y
        sibling = (x, y, 1 - c)
        peers = [(1 - x, y), (x, 1 - y), (1 - x, 1 - y)]

        def rows_of(t, core):
            half = arrs[t].shape[0] // 2
            return pl.ds(core * half, half)

        def part(ref, t, core):
            return ref.at[rows_of(t, core)] if split[t] else ref

        load = [pltpu.make_async_copy(ins[t], staged[t], loc_in.at[t]) for t in range(n)]
        store = [pltpu.make_async_copy(staged[t], outs[t].at[me], loc_out.at[t]) for t in range(n)]
        for cp in load:
            cp.start()
        first = []
        for t in range(n):
            for j, (px, py) in enumerate(peers):
                first.append(pltpu.make_async_remote_copy(
                    src_ref=part(ins[t], t, c), dst_ref=part(outs[t].at[me], t, c), send_sem=send1.at[t, j],
                    recv_sem=recv1.at[t, j], device_id=(px, py, c), device_id_type=MESH))
        for cp in first:
            cp.start()
        for cp_in, cp_out in zip(load, store):
            cp_in.wait()
            cp_out.start()
        passed = []
        for t in range(n):
            for j, (px, py) in enumerate(peers):
                landed = part(outs[t].at[2 * px + py], t, c)
                pltpu.make_async_remote_copy(
                    src_ref=landed, dst_ref=landed, send_sem=send1.at[t, j], recv_sem=recv1.at[t, j],
                    device_id=(x, y, c), device_id_type=MESH).wait_recv()
                if split[t]:
                    cp = pltpu.make_async_remote_copy(
                        src_ref=landed, dst_ref=landed, send_sem=send2.at[t, j], recv_sem=recv2.at[t, j],
                        device_id=sibling, device_id_type=MESH)
                    cp.start()
                    passed.append(cp)
        for t in range(n):
            for j, (px, py) in enumerate(peers):
                if split[t]:
                    other = part(outs[t].at[2 * px + py], t, 1 - c)
                    pltpu.make_async_remote_copy(
                        src_ref=other, dst_ref=other, send_sem=send2.at[t, j], recv_sem=recv2.at[t, j],
                        device_id=(x, y, c), device_id_type=MESH).wait_recv()
        for cp in first + passed:
            cp.wait_send()
        for cp in store:
            cp.wait()

    return pl.pallas_call(
        body, name="gather_weights", in_specs=[ANY] * n, out_specs=[ANY] * n,
        out_shape=[jax.ShapeDtypeStruct((NCHIP,) + a.shape, a.dtype) for a in arrs],
        scratch_shapes=[pltpu.SemaphoreType.DMA((n, 3))] * 4 + [pltpu.SemaphoreType.DMA((n,))] * 2
        + [pltpu.VMEM(a.shape, a.dtype) for a in arrs],
    )(*arrs)


def _adam_math(g, w, m, v):
    m = ADAM_B1 * m + (1.0 - ADAM_B1) * g
    v = ADAM_B2 * v + (1.0 - ADAM_B2) * (g * g)
    m_hat = m / (1.0 - ADAM_B1 ** ADAM_STEP)
    v_hat = v / (1.0 - ADAM_B2 ** ADAM_STEP)
    delta = -ADAM_LR * (m_hat / (jnp.sqrt(v_hat) + ADAM_EPS) + ADAM_WD * w)
    return delta, m, v


def _rows_tile(rows):
    return rows if rows <= 256 else 256


def sum_chips(parts, name):
    _, rows, cols = parts.shape
    tr = _rows_tile(rows)

    def body(p_ref, o_ref):
        acc = p_ref[0].astype(F32)
        for s in range(1, NCHIP):
            acc = acc + p_ref[s].astype(F32)
        o_ref[...] = acc

    return pl.pallas_call(
        body, grid=(rows // tr,), name=name,
        in_specs=[pl.BlockSpec((NCHIP, tr, cols), lambda i: (0, i, 0))],
        out_specs=pl.BlockSpec((tr, cols), lambda i: (i, 0)),
        out_shape=jax.ShapeDtypeStruct((rows, cols), F32),
        compiler_params=_cparams(("parallel",)),
    )(parts)


def sum_chips_small(parts, name):
    n = len(parts)

    def body(*refs):
        for p_ref, o_ref in zip(refs[:n], refs[n:]):
            acc = p_ref[0]
            for s in range(1, NCHIP):
                acc = acc + p_ref[s]
            o_ref[...] = acc

    return pl.pallas_call(body, name=name, out_shape=[jax.ShapeDtypeStruct(p.shape[1:], F32) for p in parts])(*parts)


def adam_shard_small(items, name):
    n = len(items)

    def body(*refs):
        for t in range(n):
            a_ref, b_ref, w_ref, m_ref, v_ref = refs[5 * t:5 * t + 5]
            g_ref, d_ref, mo_ref, vo_ref = refs[5 * n + 4 * t:5 * n + 4 * t + 4]
            g = (a_ref[...] + b_ref[...]).reshape(w_ref.shape)
            g_ref[...] = g
            d_ref[...], mo_ref[...], vo_ref[...] = _adam_math(g, w_ref[...], m_ref[...], v_ref[...])

    out = pl.pallas_call(
        body, name=name, out_shape=[jax.ShapeDtypeStruct(it[2].shape, F32) for it in items for _ in range(4)],
    )(*[a for it in items for a in it])
    return [out[4 * t:4 * t + 4] for t in range(n)]


def adam_shard(p_mine, p_sib, w, m, v, name):
    rows, cols = p_mine.shape
    tr = _rows_tile(rows)
    lead = w.ndim == 3

    def body(a_ref, b_ref, w_ref, m_ref, v_ref, g_ref, d_ref, mo_ref, vo_ref):
        g = a_ref[...] + b_ref[...]
        g = g[None] if lead else g
        g_ref[...] = g
        d_ref[...], mo_ref[...], vo_ref[...] = _adam_math(g, w_ref[...], m_ref[...], v_ref[...])

    flat = pl.BlockSpec((tr, cols), lambda i: (i, 0))
    spec = pl.BlockSpec((1, tr, cols), lambda i: (0, i, 0)) if lead else flat
    return pl.pallas_call(
        body, grid=(rows // tr,), name=name, in_specs=[flat] * 2 + [spec] * 3, out_specs=[spec] * 4,
        out_shape=[jax.ShapeDtypeStruct(w.shape, F32)] * 4,
        compiler_params=_cparams(("parallel",)),
    )(p_mine, p_sib, w, m, v)


def adam_shard_halves_t(r_mine, r_sib, wt, mt, vt, name):
    hrows, cols = r_mine.shape
    tr = _rows_tile(hrows)
    per_half = hrows // tr

    def body(a_ref, b_ref, w_ref, m_ref, v_ref, g_ref, d_ref, mo_ref, vo_ref):
        mine = pl.program_id(0) == lax.axis_index("c")
        g = jnp.where(mine, a_ref[...], b_ref[...]).T[None]
        g_ref[...] = g
        d_ref[...], mo_ref[...], vo_ref[...] = _adam_math(g, w_ref[...], m_ref[...], v_ref[...])

    flat = pl.BlockSpec((tr, cols), lambda h, i: (i, 0))
    spec = pl.BlockSpec((1, cols, tr), lambda h, i: (0, 0, h * per_half + i))
    return pl.pallas_call(
        body, grid=(2, per_half), name=name, in_specs=[flat] * 2 + [spec] * 3, out_specs=[spec] * 4,
        out_shape=[jax.ShapeDtypeStruct(wt.shape, F32)] * 4,
        compiler_params=_cparams(("parallel", "parallel")),
    )(r_mine, r_sib, wt, mt, vt)


def adam_replicated(gathered, params, name):
    flat = []
    for i, p in enumerate(params):
        if isinstance(p, list):
            off = 0
            for wmv in p:
                n = gathered[i].shape[-1] - off if wmv[0] is None else wmv[0].shape[-1]
                flat.append((i, (off, n), wmv))
                off += n
        else:
            flat.append((i, None, p))
    ins = [a for _, _, wmv in flat for a in wmv if a is not None]
    ng = len(gathered)

    def body(*refs):
        g_refs = refs[:ng]
        in_refs = list(refs[ng:ng + len(ins)])
        out_refs = list(refs[ng + len(ins):])
        sums = []
        for r in g_refs:
            g = r[0]
            for d in range(1, NDEV):
                g = g + r[d]
            sums.append(g)
        for i, lanes, wmv in flat:
            g = sums[i] if lanes is None else sums[i][:, lanes[0]:lanes[0] + lanes[1]]
            out_refs.pop(0)[...] = g
            if wmv[0] is not None:
                w_ref, m_ref, v_ref = in_refs.pop(0), in_refs.pop(0), in_refs.pop(0)
                d_ref, mo_ref, vo_ref = out_refs.pop(0), out_refs.pop(0), out_refs.pop(0)
                d_ref[...], mo_ref[...], vo_ref[...] = _adam_math(g, w_ref[...], m_ref[...], v_ref[...])

    out_shape = []
    for i, lanes, wmv in flat:
        shape = gathered[i].shape[1:] if lanes is None else (1, lanes[1])
        out_shape += [jax.ShapeDtypeStruct(shape, F32)] * (4 if wmv[0] is not None else 1)
    outs = list(pl.pallas_call(body, name=name, out_shape=out_shape)(*gathered, *ins))
    return [[outs.pop(0) for _ in range(4 if wmv[0] is not None else 1)] for _, _, wmv in flat]


EVEN_SPLITS = (SHIFT, W, W, W, W, W)
ODD_SPLITS = (D, D, D)


def _cols_to_chips(a):
    rows, cols = a.shape
    return a.reshape(rows, NCHIP, cols // NCHIP).transpose(1, 0, 2)


def _chips_to_cols(a):
    _, rows, n = a.shape
    return a.transpose(1, 0, 2).reshape(rows, NCHIP * n)


def kernel(x, norm_g, w_in_e, shift_mu, rw_w0, rw_w2, rw_a0, rw_a2, rw_kk, rw_ka, rw_rk, rw_lnx_g, rw_lnx_b, att_bias, w_out_e, w_in_o, sg_ln_g, sg_ln_b, sg_w, sg_b, w_out_o, final_g, loss_target, m_norm_g, m_w_in_e, m_shift_mu, m_rw_w0, m_rw_w2, m_rw_a0, m_rw_a2, m_rw_kk, m_rw_ka, m_rw_rk, m_rw_lnx_g, m_rw_lnx_b, m_att_bias, m_w_out_e, m_w_in_o, m_sg_ln_g, m_sg_ln_b, m_sg_w, m_sg_b, m_w_out_o, m_final_g, v_norm_g, v_w_in_e, v_shift_mu, v_rw_w0, v_rw_w2, v_rw_a0, v_rw_a2, v_rw_kk, v_rw_ka, v_rw_rk, v_rw_lnx_g, v_rw_lnx_b, v_att_bias, v_w_out_e, v_w_in_o, v_sg_ln_g, v_sg_ln_b, v_sg_w, v_sg_b, v_w_out_o, v_final_g):
    x2 = x.reshape(T, D)
    tgt = loss_target.reshape(T, D)

    gathered = gather_weights(
        [jnp.swapaxes(w_in_e[0], 0, 1).astype(BF16), jnp.concatenate([rw_w2[0], rw_a2[0]], axis=0),
         jnp.concatenate([sg_ln_g, sg_ln_b], axis=0)], [True, True, False])
    wie = gathered[0].reshape(EVEN_IN, D)
    w2 = _chips_to_cols(gathered[1][:, :LORA])
    a2 = _chips_to_cols(gathered[1][:, LORA:])
    sglg = _chips_to_cols(gathered[2][:, 0:1])
    sglb = _chips_to_cols(gathered[2][:, 1:2])

    late = [w_out_e[0].astype(BF16), w_in_o[0].astype(BF16), w_out_o[0].astype(BF16)]
    late_started = send_start(late, False, CHIPS, "late_weights_start", after=gathered[0])

    late_state = {}

    def late_weights(layer, after):
        if layer == "even":
            srcs, lands = send_wait(late_started, after, False, CHIPS, "late_w_out_e_wait", True, only=(0,))
            late_state["rest"] = (late_started[0], srcs, lands, None)
            return lands[0].reshape(D, D)
        woe, wio, woo = send_wait(late_state["rest"], after, False, CHIPS, "late_weights_wait", only=(1, 2))
        return woe.reshape(D, D), wio, woo.reshape(D, D)

    def scatter_start(grads, name):
        return send_start([g_.astype(BF16) if g_.shape[-1] >= W else g_ for g_ in grads], True, CHIPS, name)

    started = {}

    def on_odd_grads(d_woo, d_wio):
        started["odd"] = scatter_start([d_woo.reshape(NCHIP, D // NCHIP, D), d_wio], "odd_grads_start")
        return started["odd"][-1]

    def on_even_grads(big_g):
        d_wie_half, d_woe, _, _, d_w2, d_a2, d_sglg, d_sglb = big_g
        blocks = [d_wie_half, d_woe.reshape(NCHIP, D // NCHIP, D), _cols_to_chips(d_w2), _cols_to_chips(d_a2),
                  _cols_to_chips(d_sglg), _cols_to_chips(d_sglb)]
        started["even"] = scatter_start(blocks, "even_grads_start")
        return started["even"][-1]

    def on_small_grads(layer, grads):
        if layer == "odd":
            d_sg_w, d_sg_b, d_final, d_g1 = grads
            mine = [d_sg_w.reshape(NG * SGC, SGC), d_sg_b, jnp.concatenate([d_final, d_g1], axis=1)]
        else:
            mine = [grads[-2], jnp.concatenate(grads[:-2] + grads[-1:], axis=1)]
        started[layer + "_small"] = send_start(mine, False, EVERY, layer + "_small_grads_start")
        return started[layer + "_small"][-1]

    loss_part, dx, _, _ = _local_step(
        x2, tgt, wie, late_weights, w2, a2, sglg, sglb, norm_g, shift_mu, rw_w0, rw_a0, rw_kk, rw_ka, rw_rk,
        rw_lnx_g, rw_lnx_b, att_bias, sg_w, sg_b, final_g, first_after=late_started[-1], on_odd_grads=on_odd_grads,
        on_even_grads=on_even_grads, on_small_grads=on_small_grads)
    wmv = {"w_in_e": tuple(jnp.swapaxes(a, 1, 2) for a in (w_in_e, m_w_in_e, v_w_in_e)),
           "w_out_e": (w_out_e, m_w_out_e, v_w_out_e),
           "w_in_o": (w_in_o, m_w_in_o, v_w_in_o), "w_out_o": (w_out_o, m_w_out_o, v_w_out_o),
           "rw_w2": (rw_w2, m_rw_w2, v_rw_w2), "rw_a2": (rw_a2, m_rw_a2, v_rw_a2),
           "sg_ln_g": (sg_ln_g, m_sg_ln_g, v_sg_ln_g), "sg_ln_b": (sg_ln_b, m_sg_ln_b, v_sg_ln_b)}
    sharded = {}

    def sum_and_swap(names, landed, tag):
        nbig = sum(p_.dtype == BF16 for p_ in landed)
        partial = [sum_chips(p_, "sum_" + nm) for p_, nm in zip(landed[:nbig], names)]
        if nbig < len(names):
            partial += sum_chips_small(landed[nbig:], "sum_small_" + tag)
        return send_start(partial, False, SIBLING, "swap_partials_" + tag + "_start")

    def update(names, swap_started, after, tag):
        partial, landed = send_wait(swap_started, after, False, SIBLING, "swap_partials_" + tag + "_wait", True)
        from_sibling = [a[0] for a in landed]
        nbig = sum(p_.shape[-1] >= W for p_ in partial)
        for nm, mine, sib in zip(names[:nbig], partial, from_sibling):
            if nm == "w_in_e":
                res = adam_shard_halves_t(mine, sib, *wmv[nm], "adam_" + nm)
                sharded[nm] = [jnp.swapaxes(a, 1, 2) for a in res]
            else:
                sharded[nm] = adam_shard(mine, sib, *wmv[nm], "adam_" + nm)
        if nbig < len(names):
            items = [(mine, sib, *wmv[nm]) for nm, mine, sib in zip(names, partial, from_sibling)][nbig:]
            for nm, res in zip(names[nbig:], adam_shard_small(items, "adam_small_" + tag)):
                sharded[nm] = res

    odd_names = ["w_out_o", "w_in_o"]
    even_names = ["w_in_e", "w_out_e", "rw_w2", "rw_a2", "sg_ln_g", "sg_ln_b"]
    odd_landed = send_wait(started["odd"], started["even_small"][-1], True, CHIPS, "odd_grads_wait")
    odd_swap = sum_and_swap(odd_names, odd_landed, "odd")
    done = odd_swap[-1]

    def wmv_of(*arrs, view=lambda a: a):
        return tuple(view(a) for a in arrs)

    vec = lambda a: a.reshape(1, -1)
    groups = {
        "odd": (["sg_w", "sg_b", "final_g", "norm_g1"],
                [wmv_of(sg_w, m_sg_w, v_sg_w, view=lambda a: a.reshape(NG * SGC, SGC)),
                 wmv_of(sg_b, m_sg_b, v_sg_b, view=lambda a: a[0]),
                 [wmv_of(final_g, m_final_g, v_final_g, view=vec),
                  wmv_of(norm_g, m_norm_g, v_norm_g, view=lambda a: a[1:2])]]),
        "even": (["att_bias", "norm_g0", "shift_mu", "rw_w0", "rw_a0", "rw_kk", "rw_ka", "rw_rk", "rw_lnx_g",
                  "rw_lnx_b", "loss"],
                 [wmv_of(att_bias, m_att_bias, v_att_bias, view=lambda a: a[0]),
                  [wmv_of(norm_g, m_norm_g, v_norm_g, view=lambda a: a[0:1]),
                   wmv_of(shift_mu, m_shift_mu, v_shift_mu), wmv_of(rw_w0, m_rw_w0, v_rw_w0),
                   wmv_of(rw_a0, m_rw_a0, v_rw_a0), wmv_of(rw_kk, m_rw_kk, v_rw_kk), wmv_of(rw_ka, m_rw_ka, v_rw_ka),
                   wmv_of(rw_rk, m_rw_rk, v_rw_rk, view=vec), wmv_of(rw_lnx_g, m_rw_lnx_g, v_rw_lnx_g),
                   wmv_of(rw_lnx_b, m_rw_lnx_b, v_rw_lnx_b), (None, None, None)]]),
    }
    rep = {}
    for layer in ("odd", "even"):
        nms, params = groups[layer]
        gathered_g = send_wait(started[layer + "_small"], done, False, EVERY, layer + "_small_grads_wait")
        for nm, res in zip(nms, adam_replicated(gathered_g, params, "adam_" + layer + "_small")):
            rep[nm] = res
        done = rep[nms[0]][0]
    native = {"sg_w": sg_w.shape, "sg_b": sg_b.shape, "final_g": final_g.shape, "rw_rk": rw_rk.shape,
              "att_bias": att_bias.shape}
    for nm, shape in native.items():
        rep[nm] = [a.reshape(shape) for a in rep[nm]]
    rep["norm_g"] = [jnp.concatenate([a, b], axis=0) for a, b in zip(rep["norm_g0"], rep["norm_g1"])]
    even_landed = send_wait(started["even"], done, True, CHIPS, "even_grads_wait")
    even_swap = sum_and_swap(even_names, even_landed, "even")
    update(odd_names, odd_swap, even_swap[-1], "odd")
    update(even_names, even_swap, sharded["w_in_o"][0], "even")

    order = ["norm_g", "w_in_e", "shift_mu", "rw_w0", "rw_w2", "rw_a0", "rw_a2", "rw_kk", "rw_ka", "rw_rk",
             "rw_lnx_g", "rw_lnx_b", "att_bias", "w_out_e", "w_in_o", "sg_ln_g", "sg_ln_b", "sg_w", "sg_b",
             "w_out_o", "final_g"]
    results = {**sharded, **rep}
    outs = [rep["loss"][0][0, 0], dx.reshape(NSEQ, SEQ, D)]
    for kind in range(4):
        outs += [results[nm][kind] for nm in order]
    return tuple(outs)


def _local_step(x2, tgt, wie_t, late_weights, w2, a2, sglg, sglb, norm_g, shift_mu, rw_w0, rw_a0, rw_kk, rw_ka, rw_rk,
                rw_lnx_g, rw_lnx_b, att_bias, sg_w, sg_b, final_g, first_after=None, on_odd_grads=None,
                on_even_grads=None, on_small_grads=None):
    zl = jnp.zeros((LORA, W), F32)
    w2x = jnp.concatenate([w2, zl], axis=0)
    a2x = jnp.concatenate([zl, a2], axis=0)
    rk = rw_rk.reshape(1, W)
    pos = np.arange(SGC)
    sg_mask = jnp.asarray(((pos[None, :] // L) <= (pos[:, None] // L)).astype(np.float32))
    wm = (sg_w[0] * sg_mask[None]).astype(BF16)
    sgb_t = sg_b[0].T

    xn0, ps, ga, q, kb, vb, gb = ln_in_proj(x2, norm_g[0:1], wie_t, EVEN_SPLITS, "in_proj_even", after=first_after,
                                            w_t=True, bf16_pieces=(2, 3, 4))
    r, lw, k2, v, aa, bb = even_prep(ps, shift_mu, rw_w0, w2x, rw_a0, a2x, rw_kk, rw_ka)
    y, rw_saved = rwkv_fwd(r, lw, k2, v, aa, bb)
    bias = bias_expand(att_bias[0])

    o = attention_fwd(q, kb, vb, bias)
    woe = late_weights("even", o)
    h1, zt = even_post(y, r, k2, v, ga, o, gb, rw_lnx_g, rw_lnx_b, rk, x2, woe)
    woe, wio, woo = late_weights("odd", h1)
    xn1, u, vv, gt = ln_in_proj(h1, norm_g[1:2], wio, ODD_SPLITS, "in_proj_odd")
    dh2, loss_part, d_final_g, z2t = gmlp_fwd_loss(u, vv, gt, sglg, sglb, wm, sgb_t, h1, woo, final_g[None], tgt)

    du, dvv, dgt, d_sglg, d_sglb, d_wm, d_sgb_t, d_woo = gmlp_bwd(u, vv, gt, sglg, sglb, wm, sgb_t, dh2, z2t, woo)
    dp_odd = [du, dvv, dgt]
    d_wio = matmul_acc_chips(xn1, dp_odd, "in_proj_odd_dw")
    token = on_odd_grads(d_woo, d_wio) if on_odd_grads else None
    dh1, d_g1 = in_proj_bwd_x(h1, norm_g[1:2], wio, dp_odd, dh2, "in_proj_odd_bwd", after=token)
    odd_small = [d_wm * sg_mask[None], d_sgb_t.T, d_final_g, d_g1]
    token = on_small_grads("odd", odd_small) if on_small_grads else None
    dy, dr2, dk22, dv2, dga, do, dgb, d_lng, d_lnb, d_rk, d_woe = even_post_bwd(
        y, r, k2, v, ga, o, gb, rw_lnx_g, rw_lnx_b, rk, dh1, zt, woe, after=token)
    dq, dkb, dvb, dbias = attention_bwd(q, kb, vb, bias, do)
    dbias = sum(dbias[:, i * 2 * L:(i + 1) * 2 * L, i * L:i * L + BAND] for i in range(ATT_Q))
    d_att_bias = bias_grad(dbias.reshape(NH, L, BAND))
    dr, dlw, dk2, dv, daa, dbb = rwkv_bwd(r, lw, k2, aa, bb, rw_saved, dy)
    dps, d_mu, d_w0, d_w2x, d_a0, d_a2x, d_kk, d_ka = even_prep_bwd(
        ps, shift_mu, rw_w0, w2x, rw_a0, a2x, rw_kk, rw_ka, dr, dlw, dk2, dv, daa, dbb, dr2, dk22, dv2)
    dp_even = [dps, dga, dq, dkb, dvb, dgb]
    d_wie = matmul_acc_chips(xn0, dp_even, "in_proj_even_dw", add_cores=on_even_grads is not None)
    big_g = (d_wie, d_woe, d_wio, d_woo, d_w2x[:LORA], d_a2x[LORA:], d_sglg, d_sglb)
    token = on_even_grads(big_g) if on_even_grads else None
    dx, d_g0 = in_proj_bwd_x(x2, norm_g[0:1], wie_t, dp_even, dh1, "in_proj_even_bwd", after=token, w_t=True)
    even_small = [d_g0, d_mu, d_w0, d_a0, d_kk, d_ka, d_rk, d_lng, d_lnb, d_att_bias]
    if on_small_grads:
        on_small_grads("even", even_small + [loss_part[0:1, :]])
    rep_g = [jnp.concatenate([d_g0, d_g1], axis=0)] + even_small[1:] + odd_small[:3]
    return loss_part[0, 0], dx, big_g, rep_g
```

```python
import functools
import math

import jax
import jax.numpy as jnp
import numpy as np
from jax import lax
from jax.experimental import pallas as pl
from jax.experimental.pallas import tpu as pltpu

F32 = jnp.float32
BF16 = jnp.bfloat16
HI = lax.Precision.HIGHEST

D = 1024
SEQ = 2048
NSEQ = 2
T = NSEQ * SEQ
HD = 64
NH = 8
W = 512
SHIFT = 1664
LORA = 64
EVEN_IN = 4224
ODD_IN = 3072
L = 64
NC = SEQ // L
LEFT = 8
BAND = (LEFT + 1) * L
CLIP = 128
SGC = 128
NG = 8
RMS_EPS = 1e-6
LN_EPS = 1e-5
GN_EPS = 64e-5
NEG = -1e30
VMEM_BIG = 56 * 1024 * 1024

ADAM_LR = 0.001
ADAM_B1 = 0.9
ADAM_B2 = 0.999
ADAM_EPS = 1e-08
ADAM_WD = 0.01
ADAM_STEP = 10

MESH = pl.DeviceIdType.MESH


def _bdot(a, b):
    return jnp.dot(a.astype(BF16), b.astype(BF16), preferred_element_type=F32)


def _bdot_nt(a, b):
    return lax.dot_general(a.astype(BF16), b.astype(BF16), (((1,), (1,)), ((), ())), preferred_element_type=F32)


def _bdot_tn(a, b):
    return lax.dot_general(a.astype(BF16), b.astype(BF16), (((0,), (0,)), ((), ())), preferred_element_type=F32)


def _hdot(a, b):
    return jnp.dot(a, b, precision=HI, preferred_element_type=F32)


def _hdot_nt(a, b):
    return lax.dot_general(a, b, (((1,), (1,)), ((), ())), precision=HI, preferred_element_type=F32)


def _hdot_tn(a, b):
    return lax.dot_general(a, b, (((0,), (0,)), ((), ())), precision=HI, preferred_element_type=F32)


def _iota2(shape, dim):
    return lax.broadcasted_iota(jnp.int32, shape, dim)


def _head_blockdiag():
    r = _iota2((2 * HD, 2 * HD), 0) // HD
    c = _iota2((2 * HD, 2 * HD), 1) // HD
    return (r == c).astype(BF16)


def _headsum_impl(x, bd):
    hi = x.astype(BF16)
    mid = (x - hi.astype(F32)).astype(BF16)
    n = bd.shape[0]
    out = [jnp.dot(hi[:, i:i + n], bd, preferred_element_type=F32) + jnp.dot(mid[:, i:i + n], bd, preferred_element_type=F32)
           for i in range(0, x.shape[1], n)]
    return jnp.concatenate(out, axis=-1)


@jax.custom_vjp
def _headsum(x, bd):
    return _headsum_impl(x, bd)


def _headsum_fwd(x, bd):
    return _headsum_impl(x, bd), bd


def _headsum_bwd(bd, ct):
    return _headsum_impl(ct, bd), None


_headsum.defvjp(_headsum_fwd, _headsum_bwd)


def _silu(x):
    return x * jax.nn.sigmoid(x)


_GELU_C = math.sqrt(2.0 / math.pi)


def _gelu(x):
    return 0.5 * x * (1.0 + jnp.tanh(_GELU_C * (x + 0.044715 * (x * x * x))))


def _silu_both(x):
    s = jax.nn.sigmoid(x)
    xs = x * s
    return xs, s + xs * (1.0 - s)


def _gelu_both(x):
    x2 = x * x
    t = jnp.tanh(_GELU_C * (x + 0.044715 * (x2 * x)))
    half = 0.5 * (1.0 + t)
    return x * half, half + 0.5 * x * (1.0 - t * t) * _GELU_C * (1.0 + 3.0 * 0.044715 * x2)


def _softplus(x):
    return jnp.maximum(x, 0.0) + jnp.log(1.0 + jnp.exp(-jnp.abs(x)))


def _cparams(sem, vmem=None):
    return pltpu.CompilerParams(dimension_semantics=sem, vmem_limit_bytes=vmem)


def _row_spec(tm, width):
    return pl.BlockSpec((tm, width), lambda i: (i, 0))


def _col_spec(height, tm):
    return pl.BlockSpec((height, tm), lambda i: (0, i))


def _const_spec(shape):
    nd = len(shape)
    return pl.BlockSpec(shape, lambda *_: (0,) * nd)


def _weight_dims(w_bf, w_t):
    if w_bf.ndim == 3:
        return None, w_bf.shape[0] * w_bf.shape[2]
    return (((1,), (1,)), ((), ())) if w_t else (((1,), (0,)), ((), ())), w_bf.shape[0 if w_t else 1]


def _proj(xn, w_ref, dims):
    if dims is None:
        return jnp.concatenate([jnp.dot(xn, w_ref[s], preferred_element_type=F32) for s in range(w_ref.shape[0])],
                               axis=-1)
    return lax.dot_general(xn, w_ref[...], dims, preferred_element_type=F32)


def _proj_back(dp, w_ref, w_t):
    nt = (((1,), (1,)), ((), ()))
    if len(w_ref.shape) == 3:
        nb = w_ref.shape[2]
        parts = [lax.dot_general(dp[:, s * nb:(s + 1) * nb], w_ref[s], nt, preferred_element_type=F32)
                 for s in range(w_ref.shape[0])]
        return sum(parts[1:], parts[0])
    return lax.dot_general(dp, w_ref[...], (((1,), (0,)), ((), ())) if w_t else nt, preferred_element_type=F32)


def ln_in_proj(x, g, w_bf, splits, name, after=None, w_t=False, bf16_pieces=()):
    dims, n = _weight_dims(w_bf, w_t)
    dtypes = [BF16 if i in bf16_pieces else F32 for i in range(len(splits))]
    tm = 512
    spans = []
    o = 0
    for s in splits:
        spans.append((o, o + s))
        o += s
    assert o == n
    extra_specs, extra = _after_operand(after)

    def body(x_ref, g_ref, w_ref, *rest):
        xn_ref, outs = rest[len(extra)], rest[len(extra) + 1:]
        xv = x_ref[...]
        rstd = lax.rsqrt(jnp.mean(xv * xv, axis=-1, keepdims=True) + RMS_EPS)
        xn = (xv * rstd * g_ref[...]).astype(BF16)
        xn_ref[...] = xn.T
        p = _proj(xn, w_ref, dims)
        for o_ref, (a, b) in zip(outs, spans):
            o_ref[...] = p[:, a:b].astype(o_ref.dtype)

    return pl.pallas_call(
        body, grid=(T // tm,), name=name,
        in_specs=[_row_spec(tm, D), _const_spec((1, D)), _const_spec(w_bf.shape)] + extra_specs,
        out_specs=[_col_spec(D, tm)] + [_row_spec(tm, s) for s in splits],
        out_shape=[jax.ShapeDtypeStruct((D, T), BF16)]
        + [jax.ShapeDtypeStruct((T, s), dt) for s, dt in zip(splits, dtypes)],
        compiler_params=_cparams(("parallel",), VMEM_BIG),
    )(x, g, w_bf, *extra)


def in_proj_bwd_x(x, g, w_bf, dps, dres, name, after=None, w_t=False):
    tm = 512
    widths = [d.shape[1] for d in dps]
    extra_specs, extra = _after_operand(after)

    def body(x_ref, g_ref, w_ref, dres_ref, *rest):
        dp_refs = rest[:len(widths)]
        dx_ref, dg_ref = rest[-2:]
        dp = jnp.concatenate([r[...] for r in dp_refs], axis=-1)
        dxn = _proj_back(dp, w_ref, w_t)
        xv = x_ref[...]
        rstd = lax.rsqrt(jnp.mean(xv * xv, axis=-1, keepdims=True) + RMS_EPS)
        xhat = xv * rstd
        dgp = jnp.sum(dxn * xhat, axis=0, keepdims=True)

        @pl.when(pl.program_id(0) == 0)
        def _():
            dg_ref[...] = jnp.zeros_like(dg_ref)

        dg_ref[...] += dgp
        dxh = dxn * g_ref[...]
        dx_ref[...] = dres_ref[...] + rstd * (dxh - xhat * jnp.mean(dxh * xhat, axis=-1, keepdims=True))

    return pl.pallas_call(
        body, grid=(T // tm,), name=name,
        in_specs=[_row_spec(tm, D), _const_spec((1, D)), _const_spec(w_bf.shape), _row_spec(tm, D)]
        + [_row_spec(tm, s) for s in widths] + extra_specs,
        out_specs=[_row_spec(tm, D), _const_spec((1, D))],
        out_shape=[jax.ShapeDtypeStruct((T, D), F32), jax.ShapeDtypeStruct((1, D), F32)],
        compiler_params=_cparams(("arbitrary",), VMEM_BIG),
    )(x, g, w_bf, dres, *dps, *extra)


def _after_operand(after):
    return ([ANY], [after]) if after is not None else ([], [])


def matmul_acc_chips(at_bf, pieces, name, after=None, add_cores=False):
    k = at_bf.shape[0]
    widths = [p.shape[1] for p in pieces]
    nb = sum(widths) // NCHIP
    tm = 512
    steps = T // tm
    half = k // 2
    extra_specs, extra = _after_operand(after)

    def body(a_ref, *rest):
        o_ref, acc = rest[len(widths) + len(extra):][:2]

        @pl.when(pl.program_id(0) == 0)
        def _():
            acc[...] = jnp.zeros_like(acc)

        a = a_ref[...]
        b = jnp.concatenate([r[...] for r in rest[:len(widths)]], axis=-1)
        for s in range(NCHIP):
            acc[s] += jnp.dot(a, b[:, s * nb:(s + 1) * nb], preferred_element_type=F32)

        @pl.when(pl.program_id(0) == steps - 1)
        def _():
            if not add_cores:
                o_ref[...] = acc[...].astype(BF16)
            else:
                give, got, send, recv = rest[-4:]
                x, y, c = lax.axis_index("x"), lax.axis_index("y"), lax.axis_index("c")
                theirs = pl.multiple_of((1 - c) * half, half)
                mine = pl.multiple_of(c * half, half)
                give[...] = acc[:, pl.ds(theirs, half), :].astype(BF16)
                cp = pltpu.make_async_remote_copy(src_ref=give, dst_ref=got, send_sem=send, recv_sem=recv,
                                                  device_id=(x, y, 1 - c), device_id_type=MESH)
                cp.start()
                cp.wait()
                o_ref[...] = (acc[:, pl.ds(mine, half), :] + got[...].astype(F32)).astype(BF16)

    out_rows = half if add_cores else k
    exchange = [pltpu.VMEM((NCHIP, half, nb), BF16)] * 2 + [pltpu.SemaphoreType.DMA] * 2 if add_cores else []
    return pl.pallas_call(
        body, grid=(steps,), name=name,
        in_specs=[_col_spec(k, tm)] + [_row_spec(tm, w_) for w_ in widths] + extra_specs,
        out_specs=_const_spec((NCHIP, out_rows, nb)),
        out_shape=jax.ShapeDtypeStruct((NCHIP, out_rows, nb), BF16),
        scratch_shapes=[pltpu.VMEM((NCHIP, k, nb), F32)] + exchange,
        compiler_params=_cparams(("arbitrary",), VMEM_BIG),
    )(at_bf, *pieces, *extra)


def _out_proj_back(dh_ref, zt_ref, w_ref, dw_ref, acc_ref):
    dhb = dh_ref[...].astype(BF16)

    @pl.when(pl.program_id(0) == 0)
    def _():
        acc_ref[...] = jnp.zeros_like(acc_ref)

    acc_ref[...] += jnp.dot(zt_ref[...], dhb, preferred_element_type=F32)

    @pl.when(pl.program_id(0) == pl.num_programs(0) - 1)
    def _():
        dw_ref[...] = acc_ref[...].astype(dw_ref.dtype)

    return lax.dot_general(dhb, w_ref[...], (((1,), (1,)), ((), ())), preferred_element_type=F32)


PREP_TM = 512
PREP_NB = SEQ // PREP_TM


def _prep_elem(k, wl, apre, kkw, kaw, bd):
    wraw = -_softplus(-wl) - 0.5
    lw = -jnp.exp(wraw)
    asig = jax.nn.sigmoid(apre)
    kkr = k * kkw
    nrm = jnp.maximum(jnp.sqrt(_headsum(kkr * kkr, bd)), 1e-12)
    kk = kkr / nrm
    k2 = k * (1.0 + (asig - 1.0) * kaw)
    return lw, k2, -kk, kk * asig


def _prep_elem_bwd(k, wl, apre, kkw, kaw, bd, dlw, dk2, daa, dbb):
    s = -wl
    sp = _softplus(s)
    dwl = dlw * (-jnp.exp(-sp - 0.5)) * jnp.exp(s - sp)
    asig = jax.nn.sigmoid(apre)
    kkr = k * kkw
    root = jnp.sqrt(_headsum(kkr * kkr, bd))
    inv = 1.0 / jnp.maximum(root, 1e-12)
    kk = kkr * inv
    dkk = dbb * asig - daa
    dap = (dbb * kk + dk2 * k * kaw) * asig * (1.0 - asig)
    through_norm = jnp.where(root > 1e-12, kk * _headsum(dkk * kkr, bd) * inv, 0.0)
    dkkr = inv * (dkk - through_norm)
    gain = 1.0 + (asig - 1.0) * kaw
    dk = dkkr * kkw + dk2 * gain
    dkkw = jnp.sum(dkkr * k, axis=0, keepdims=True)
    dkaw = jnp.sum(dk2 * k * (asig - 1.0), axis=0, keepdims=True)
    return dk, dwl, dap, dkkw, dkaw


def _shifted(ps_ref, prev_ref, mu, blk):
    p = ps_ref[...]
    first = (blk % PREP_NB) == 0
    prev_row = jnp.where(first, 0.0, prev_ref[7:8, :])
    rolled = pltpu.roll(p, 1, 0)
    p_prev = jnp.where(_iota2(p.shape, 0) == 0, prev_row, rolled)
    return p, p_prev, p + (p_prev - p) * mu


def _prev_spec(width, blk_of):
    return pl.BlockSpec((8, width), lambda i: (jnp.maximum(blk_of(i) * (PREP_TM // 8) - 1, 0), 0))


def even_prep(ps, mu, w0, w2x, a0, a2x, kkw, kaw):
    tm = PREP_TM

    def body(ps_ref, prev_ref, mu_ref, w0_ref, w2_ref, a0_ref, a2_ref, kk_ref, ka_ref,
             r_ref, lw_ref, k2_ref, v_ref, aa_ref, bb_ref):
        _, _, s = _shifted(ps_ref, prev_ref, mu_ref[...], pl.program_id(0))
        wa = s[:, 3 * W:]
        wl = w0_ref[...] + _bdot(jnp.tanh(wa), w2_ref[...])
        apre = a0_ref[...] + _bdot(wa, a2_ref[...])
        lw, k2, aa, bb = _prep_elem(s[:, W:2 * W], wl, apre, kk_ref[...], ka_ref[...], _head_blockdiag())
        r_ref[...] = s[:, 0:W]
        v_ref[...] = s[:, 2 * W:3 * W]
        lw_ref[...] = lw
        k2_ref[...] = k2
        aa_ref[...] = aa
        bb_ref[...] = bb

    vec = _const_spec((1, W))
    return pl.pallas_call(
        body, grid=(T // tm,), name="even_prep",
        in_specs=[_row_spec(tm, SHIFT), _prev_spec(SHIFT, lambda i: i), _const_spec((1, SHIFT)), vec,
                  _const_spec((2 * LORA, W)), vec, _const_spec((2 * LORA, W)), vec, vec],
        out_specs=[_row_spec(tm, W)] * 6,
        out_shape=[jax.ShapeDtypeStruct((T, W), F32)] * 6,
        compiler_params=_cparams(("parallel",), VMEM_BIG),
    )(ps, ps, mu, w0, w2x, a0, a2x, kkw, kaw)


def even_prep_bwd(ps, mu, w0, w2x, a0, a2x, kkw, kaw, dr, dlw, dk2, dv, daa, dbb, dr2, dk22, dv2):
    tm = PREP_TM
    nb = T // tm
    rev = lambda i: nb - 1 - i

    def body(ps_ref, prev_ref, mu_ref, w0_ref, w2_ref, a0_ref, a2_ref, kk_ref, ka_ref,
             dr_ref, dlw_ref, dk2_ref, dv_ref, daa_ref, dbb_ref, dr2_ref, dk22_ref, dv2_ref,
             dps_ref, dmu_ref, dw0_ref, dw2_ref, da0_ref, da2_ref, dkk_ref, dka_ref, carry):
        i = pl.program_id(0)
        blk = rev(i)
        mu_v = mu_ref[...]
        p, p_prev, s = _shifted(ps_ref, prev_ref, mu_v, blk)
        wa = s[:, 3 * W:]
        th = jnp.tanh(wa)
        wl = w0_ref[...] + _bdot(th, w2_ref[...])
        apre = a0_ref[...] + _bdot(wa, a2_ref[...])
        bd = _head_blockdiag()
        k = s[:, W:2 * W]
        dk, dwl, dap, dkkw, dkaw = _prep_elem_bwd(k, wl, apre, kk_ref[...], ka_ref[...], bd, dlw_ref[...],
                                                  dk2_ref[...] + dk22_ref[...], daa_ref[...], dbb_ref[...])
        dwa = _bdot_nt(dwl, w2_ref[...]) * (1.0 - th * th) + _bdot_nt(dap, a2_ref[...])
        ds = jnp.concatenate([dr_ref[...] + dr2_ref[...], dk, dv_ref[...] + dv2_ref[...], dwa], axis=-1)

        @pl.when(i == 0)
        def _():
            for ref in (dmu_ref, dw0_ref, dw2_ref, da0_ref, da2_ref, dkk_ref, dka_ref, carry):
                ref[...] = jnp.zeros_like(ref)

        dmu_ref[...] += jnp.sum(ds * (p_prev - p), axis=0, keepdims=True)
        dw0_ref[...] += jnp.sum(dwl, axis=0, keepdims=True)
        da0_ref[...] += jnp.sum(dap, axis=0, keepdims=True)
        dw2_ref[...] += _bdot_tn(th, dwl)
        da2_ref[...] += _bdot_tn(wa, dap)
        dkk_ref[...] += dkkw
        dka_ref[...] += dkaw
        dsm = ds * mu_v
        last = (blk % PREP_NB) == PREP_NB - 1
        nxt = jnp.where(last, 0.0, carry[0:1, :])
        up = pltpu.roll(dsm, tm - 1, 0)
        up = jnp.where(_iota2(up.shape, 0) == tm - 1, nxt, up)
        dps_ref[...] = (ds - dsm + up).astype(BF16)
        carry[0:1, :] = dsm[0:1, :]

    vec = _const_spec((1, W))
    rrow = lambda width: pl.BlockSpec((tm, width), lambda i: (rev(i), 0))
    return pl.pallas_call(
        body, grid=(nb,), name="even_prep_bwd",
        in_specs=[rrow(SHIFT), _prev_spec(SHIFT, rev), _const_spec((1, SHIFT)), vec,
                  _const_spec((2 * LORA, W)), vec, _const_spec((2 * LORA, W)), vec, vec] + [rrow(W)] * 9,
        out_specs=[rrow(SHIFT), _const_spec((1, SHIFT)), vec, _const_spec((2 * LORA, W)), vec,
                   _const_spec((2 * LORA, W)), vec, vec],
        out_shape=[jax.ShapeDtypeStruct((T, SHIFT), BF16), jax.ShapeDtypeStruct((1, SHIFT), F32),
                   jax.ShapeDtypeStruct((1, W), F32), jax.ShapeDtypeStruct((2 * LORA, W), F32),
                   jax.ShapeDtypeStruct((1, W), F32), jax.ShapeDtypeStruct((2 * LORA, W), F32),
                   jax.ShapeDtypeStruct((1, W), F32), jax.ShapeDtypeStruct((1, W), F32)],
        scratch_shapes=[pltpu.VMEM((8, SHIFT), F32)],
        compiler_params=_cparams(("arbitrary",), VMEM_BIG),
    )(ps, ps, mu, w0, w2x, a0, a2x, kkw, kaw, dr, dlw, dk2, dv, daa, dbb, dr2, dk22, dv2)


NPAIR = NH // 2
PW = 2 * HD


def _pair_cols(p):
    return slice(p * PW, (p + 1) * PW)


def _pairs(a):
    return [a[:, _pair_cols(p)] for p in range(NPAIR)]


def _stack_pair(a):
    first = _iota2(a.shape, 1) < HD
    zero = jnp.zeros_like(a)
    return jnp.concatenate([jnp.where(first, a, zero), jnp.where(first, zero, a)], axis=0)


def _unstack_pair(a):
    n = a.shape[0] // 2
    return jnp.where(_iota2((n, PW), 1) < HD, a[:n], a[n:])


def _fold_pair(a):
    n = a.shape[0] // 2
    return a[:n] + a[n:]


def _chunk_masks():
    n = 4 * L
    row = _iota2((n, n), 0)
    col = _iota2((n, n), 1)
    same = ((row // L) & 1) == ((col // L) & 1)
    ri = row & (L - 1)
    ci = col & (L - 1)
    keep = same & (((row < 2 * L) & (ri > ci)) | ((row >= 2 * L) & (ri >= ci)))
    r1 = _iota2((L, L), 0)
    c1 = _iota2((L, L), 1)
    r2 = _iota2((2 * L, 2 * L), 0)
    c2 = _iota2((2 * L, 2 * L), 1)
    return keep.astype(F32), (r1 >= c1).astype(F32), (r2 == c2).astype(F32)


def _scaled(r, lw, k2, aa, bb, tri):
    g = _hdot(tri, lw)
    eg = jnp.exp(g)
    eng = jnp.exp(-g)
    egp = jnp.exp(g - lw)
    return eg, eng, egp, aa * egp, r * eg, bb * eng, k2 * eng


def _head_cols(h):
    return slice(h * HD, (h + 1) * HD)


def _per_head(a):
    return [a[:, _head_cols(h)] for h in range(NH)]


def _pairs_operands(at, rt, bt, kt):
    x = [jnp.concatenate([_stack_pair(a), _stack_pair(r)], axis=0).astype(BF16) for a, r in zip(_pairs(at), _pairs(rt))]
    yk = [jnp.concatenate([_stack_pair(b), _stack_pair(k)], axis=0).astype(BF16) for b, k in zip(_pairs(bt), _pairs(kt))]
    return x, yk


def _pairs_matrices(x, yk, keep, eye):
    m = [_bdot_nt(a, b) * keep for a, b in zip(x, yk)]
    p = [a[:2 * L, :2 * L] for a in m]
    tinv = [eye + a for a in p]
    for _ in range(5):
        p = [_bdot(a, a) for a in p]
        tinv = [t + _bdot(t, a) for t, a in zip(tinv, p)]
    return [a.astype(BF16) for a in m], [a.astype(BF16) for a in tinv]


def _pairs_fwd(x, yk, m, tinv, vw, s0, egl):
    xh = [_bdot_nt(a, s) for a, s in zip(x, s0)]
    u = [_bdot(t, h[:2 * L] + _bdot(a[:2 * L, 2 * L:], w)) for t, h, a, w in zip(tinv, xh, m, vw)]
    uv = [jnp.concatenate([a, w], axis=0).astype(BF16) for a, w in zip(u, vw)]
    y = [h[2 * L:] + _bdot(a[2 * L:], w) for h, a, w in zip(xh, m, uv)]
    sn = [e * (s + _bdot_tn(w, b)) for e, s, w, b in zip(egl, s0, uv, yk)]
    return y, sn, uv


def _pairs_bwd(x, yk, m, tinv, uv, s0, sn, egl, dyw, dsn, keep):
    dzs = [d * e for d, e in zip(dsn, egl)]
    dgl = [jnp.sum(d * s, axis=0, keepdims=True) for d, s in zip(dsn, sn)]
    dyb = [a.astype(BF16) for a in dyw]
    t1 = [_bdot_tn(a[2 * L:], d) for a, d in zip(m, dyb)]
    t2 = [_bdot_nt(b, d) for b, d in zip(yk, dzs)]
    drhs = [_bdot_tn(t, a[:2 * L] + b[:2 * L]) for t, a, b in zip(tinv, t1, t2)]
    dv = [a[2 * L:] + b[2 * L:] + _bdot_tn(c[:2 * L, 2 * L:], d) for a, b, c, d in zip(t1, t2, m, drhs)]
    gg = [jnp.concatenate([a, b], axis=0).astype(BF16) for a, b in zip(drhs, dyw)]
    ds0 = [d + _bdot_tn(g, a) for d, g, a in zip(dzs, gg, x)]
    dm = [_bdot_nt(g, w) * keep for g, w in zip(gg, uv)]
    dx = [_bdot(g, s) + _bdot(d, b) for g, s, d, b in zip(gg, s0, dm, yk)]
    dyk = [_bdot_tn(d, a) + _bdot(w, z) for d, a, w, z in zip(dm, x, uv, dzs)]
    return dx, dyk, dv, dgl, ds0


STATE_SHAPE = (NPAIR * PW, PW)
M_SHAPE = (4 * L, NPAIR * 4 * L)
TINV_SHAPE = (2 * L, NPAIR * 2 * L)


def _rows_of(a, n):
    return [a[i * n:(i + 1) * n, :] for i in range(NPAIR)]


def _both(f):
    out = []
    for s in range(NSEQ):
        out += f(s)
    return out


def _seq_view(a):
    return a.reshape(NSEQ, SEQ, a.shape[-1])


UV_SHAPE = (4 * L, NPAIR * PW)
RW_CHUNKS = 2


def rwkv_fwd(r, lw, k2, v, aa, bb):
    def body(r_ref, lw_ref, k2_ref, v_ref, aa_ref, bb_ref, y_ref, hs_ref, hn_ref, m_ref, t_ref, uv_ref, state):
        @pl.when(pl.program_id(0) == 0)
        def _():
            state[...] = jnp.zeros_like(state)

        keep, tri, eye = _chunk_masks()
        where = [(j, s) for j in range(RW_CHUNKS) for s in range(NSEQ)]
        rows = lambda j: slice(j * L, (j + 1) * L)
        sc = [_scaled(r_ref[s, rows(j)], lw_ref[s, rows(j)], k2_ref[s, rows(j)], aa_ref[s, rows(j)],
                      bb_ref[s, rows(j)], tri) for j, s in where]
        ops = [_pairs_operands(*a[3:]) for a in sc]
        m, tinv = _pairs_matrices([a for o in ops for a in o[0]], [a for o in ops for a in o[1]], keep, eye)
        s_cur = [state[s] for s in range(NSEQ)]
        for j in range(RW_CHUNKS):
            mine = slice(j * NSEQ * NPAIR, (j + 1) * NSEQ * NPAIR)
            x = [a for o in ops[j * NSEQ:(j + 1) * NSEQ] for a in o[0]]
            yk = [a for o in ops[j * NSEQ:(j + 1) * NSEQ] for a in o[1]]
            vw = _both(lambda s: [_stack_pair(a) for a in _pairs(v_ref[s, rows(j)])])
            egl = _both(lambda s: _pairs(sc[j * NSEQ + s][0][L - 1:L, :]))
            y, sn, uv = _pairs_fwd(x, yk, m[mine], tinv[mine], vw, _both(lambda s: _rows_of(s_cur[s], PW)), egl)
            for s in range(NSEQ):
                ps = slice(s * NPAIR, (s + 1) * NPAIR)
                hs_ref[j, s] = s_cur[s]
                y_ref[s, rows(j)] = jnp.concatenate([_fold_pair(a) for a in y[ps]], axis=-1)
                m_ref[j, s] = jnp.concatenate(m[mine][ps], axis=-1)
                t_ref[j, s] = jnp.concatenate(tinv[mine][ps], axis=-1)
                uv_ref[j, s] = jnp.concatenate(uv[ps], axis=-1)
                s_cur[s] = jnp.concatenate(sn[ps], axis=0)
                hn_ref[j, s] = s_cur[s]
        for s in range(NSEQ):
            state[s] = s_cur[s]

    blk = pl.BlockSpec((NSEQ, RW_CHUNKS * L, W), lambda c: (0, c, 0))
    per_chunk = lambda shape: pl.BlockSpec((RW_CHUNKS, NSEQ) + shape, lambda c: (c, 0, 0, 0))
    saved_shapes = [(STATE_SHAPE, F32), (STATE_SHAPE, F32), (M_SHAPE, BF16), (TINV_SHAPE, BF16), (UV_SHAPE, BF16)]
    y, *saved = pl.pallas_call(
        body, grid=(NC // RW_CHUNKS,), name="rwkv_fwd",
        in_specs=[blk] * 6,
        out_specs=[blk] + [per_chunk(shape) for shape, _ in saved_shapes],
        out_shape=[jax.ShapeDtypeStruct((NSEQ, SEQ, W), F32)]
        + [jax.ShapeDtypeStruct((NC, NSEQ) + shape, dt) for shape, dt in saved_shapes],
        scratch_shapes=[pltpu.VMEM((NSEQ,) + STATE_SHAPE, F32)],
        compiler_params=_cparams(("arbitrary",), VMEM_BIG),
    )(*[_seq_view(a) for a in (r, lw, k2, v, aa, bb)])
    return y.reshape(T, W), saved


def rwkv_bwd(r, lw, k2, aa, bb, saved, dy):
    def body(r_ref, lw_ref, k2_ref, aa_ref, bb_ref, hs_ref, hn_ref, m_ref, t_ref, uv_ref, dy_ref,
             dr_ref, dlw_ref, dk2_ref, dv_ref, daa_ref, dbb_ref, dstate):
        @pl.when(pl.program_id(0) == 0)
        def _():
            dstate[...] = jnp.zeros_like(dstate)

        keep, tri, _ = _chunk_masks()
        sc = [_scaled(r_ref[s], lw_ref[s], k2_ref[s], aa_ref[s], bb_ref[s], tri) for s in range(NSEQ)]
        ops = [_pairs_operands(*sc[s][3:]) for s in range(NSEQ)]
        x, yk = _both(lambda s: ops[s][0]), _both(lambda s: ops[s][1])
        m = _both(lambda s: [m_ref[0, s][:, i * 4 * L:(i + 1) * 4 * L] for i in range(NPAIR)])
        tinv = _both(lambda s: [t_ref[0, s][:, i * 2 * L:(i + 1) * 2 * L] for i in range(NPAIR)])
        uv = _both(lambda s: _pairs(uv_ref[0, s]))
        dyw = _both(lambda s: [_stack_pair(a) for a in _pairs(dy_ref[s])])
        s0 = _both(lambda s: _rows_of(hs_ref[0, s], PW))
        sn = _both(lambda s: _rows_of(hn_ref[0, s], PW))
        dsn = _both(lambda s: _rows_of(dstate[s], PW))
        egl = _both(lambda s: _pairs(sc[s][0][L - 1:L, :]))
        dx, dyk, dvw, dgl, ds0 = _pairs_bwd(x, yk, m, tinv, uv, s0, sn, egl, dyw, dsn, keep)
        for s in range(NSEQ):
            mine = slice(s * NPAIR, (s + 1) * NPAIR)
            eg, eng, egp, at, rt, bt, kt = sc[s]
            dstate[s] = jnp.concatenate(ds0[mine], axis=0)
            dv_ref[s] = jnp.concatenate([_fold_pair(a) for a in dvw[mine]], axis=-1)
            dat = jnp.concatenate([_fold_pair(a[:2 * L]) for a in dx[mine]], axis=-1)
            drt = jnp.concatenate([_fold_pair(a[2 * L:]) for a in dx[mine]], axis=-1)
            dbt = jnp.concatenate([_fold_pair(a[:2 * L]) for a in dyk[mine]], axis=-1)
            dkt = jnp.concatenate([_fold_pair(a[2 * L:]) for a in dyk[mine]], axis=-1)
            dg = drt * rt - dbt * bt - dkt * kt
            dg = dg + jnp.where(_iota2(dg.shape, 0) == L - 1, jnp.concatenate(dgl[mine], axis=-1), 0.0)
            dgp = dat * at
            dlw_ref[s] = _hdot_tn(tri, dg + dgp) - dgp
            dr_ref[s] = drt * eg
            daa_ref[s] = dat * egp
            dbb_ref[s] = dbt * eng
            dk2_ref[s] = dkt * eng

    blk = pl.BlockSpec((NSEQ, L, W), lambda c: (0, NC - 1 - c, 0))
    per_chunk = lambda shape: pl.BlockSpec((1, NSEQ) + shape, lambda c: (NC - 1 - c, 0, 0, 0))
    outs = pl.pallas_call(
        body, grid=(NC,), name="rwkv_bwd",
        in_specs=[blk] * 5 + [per_chunk(a.shape[2:]) for a in saved] + [blk],
        out_specs=[blk] * 6,
        out_shape=[jax.ShapeDtypeStruct((NSEQ, SEQ, W), F32)] * 6,
        scratch_shapes=[pltpu.VMEM((NSEQ,) + STATE_SHAPE, F32)],
        compiler_params=_cparams(("arbitrary",)),
    )(*[_seq_view(a) for a in (r, lw, k2, aa, bb)], *saved, _seq_view(dy))
    return [a.reshape(T, W) for a in outs]


def _post_math(y, r, k2, v, ga, o, gb, lng, lnb, rk, bd):
    mu = _headsum(y, bd) * (1.0 / HD)
    yc = y - mu
    var = _headsum(yc * yc, bd) * (1.0 / HD)
    yn = yc * lax.rsqrt(var + GN_EPS) * lng + lnb
    bonus = _headsum(r * k2 * rk, bd) * v
    return (yn + bonus) * _silu(ga), o * _silu(gb)


def even_post(y, r, k2, v, ga, o, gb, lng, lnb, rk, h, w_bf):
    tm = 512

    def body(y_ref, r_ref, k2_ref, v_ref, ga_ref, o_ref, gb_ref, lng_ref, lnb_ref, rk_ref, h_ref, w_ref,
             ho_ref, zt_ref):
        ya, yb = _post_math(y_ref[...], r_ref[...], k2_ref[...], v_ref[...], ga_ref[...], o_ref[...], gb_ref[...],
                            lng_ref[...], lnb_ref[...], rk_ref[...], _head_blockdiag())
        z = jnp.concatenate([ya.astype(BF16), yb.astype(BF16)], axis=-1)
        zt_ref[...] = z.T
        ho_ref[...] = h_ref[...] + jnp.dot(z, w_ref[...], preferred_element_type=F32)

    vec = _const_spec((1, W))
    return pl.pallas_call(
        body, grid=(T // tm,), name="even_post",
        in_specs=[_row_spec(tm, W)] * 7 + [vec] * 3 + [_row_spec(tm, D), _const_spec((D, D))],
        out_specs=[_row_spec(tm, D), _col_spec(D, tm)],
        out_shape=[jax.ShapeDtypeStruct((T, D), F32), jax.ShapeDtypeStruct((D, T), BF16)],
        compiler_params=_cparams(("parallel",), VMEM_BIG),
    )(y, r, k2, v, ga, o, gb, lng, lnb, rk, h, w_bf)


def even_post_bwd(y, r, k2, v, ga, o, gb, lng, lnb, rk, dh, zt_bf, w_bf, after=None):
    tm = 512
    extra_specs, extra = _after_operand(after)

    def body(y_ref, r_ref, k2_ref, v_ref, ga_ref, o_ref, gb_ref, lng_ref, lnb_ref, rk_ref, dh_ref, zt_ref, w_ref,
             *rest):
        (dy_ref, dr_ref, dk2_ref, dv_ref, dga_ref, do_ref, dgb_ref, dlng_ref, dlnb_ref, drk_ref, dw_ref,
         acc_ref) = rest[-12:]
        dzv = _out_proj_back(dh_ref, zt_ref, w_ref, dw_ref, acc_ref)
        bd = _head_blockdiag()
        _, vjp = jax.vjp(lambda *a: _post_math(*a, bd), y_ref[...], r_ref[...], k2_ref[...], v_ref[...], ga_ref[...],
                         o_ref[...], gb_ref[...], lng_ref[...], lnb_ref[...], rk_ref[...])
        dy, dr, dk2, dv, dga, do, dgb, dlng, dlnb, drk = vjp((dzv[:, 0:W], dzv[:, W:2 * W]))
        for ref, val in ((dy_ref, dy), (dr_ref, dr), (dk2_ref, dk2), (dv_ref, dv), (dga_ref, dga), (do_ref, do),
                         (dgb_ref, dgb)):
            ref[...] = val.astype(ref.dtype)

        @pl.when(pl.program_id(0) == 0)
        def _():
            for ref in (dlng_ref, dlnb_ref, drk_ref):
                ref[...] = jnp.zeros_like(ref)

        dlng_ref[...] += dlng
        dlnb_ref[...] += dlnb
        drk_ref[...] += drk

    vec = _const_spec((1, W))
    return pl.pallas_call(
        body, grid=(T // tm,), name="even_post_bwd",
        in_specs=[_row_spec(tm, W)] * 7 + [vec] * 3 + [_row_spec(tm, D), _col_spec(D, tm), _const_spec((D, D))]
        + extra_specs,
        out_specs=[_row_spec(tm, W)] * 7 + [vec] * 3 + [_const_spec((D, D))],
        out_shape=[jax.ShapeDtypeStruct((T, W), dt) for dt in (F32, F32, F32, F32, BF16, F32, BF16)]
        + [jax.ShapeDtypeStruct((1, W), F32)] * 3 + [jax.ShapeDtypeStruct((D, D), BF16)],
        scratch_shapes=[pltpu.VMEM((D, D), F32)],
        compiler_params=_cparams(("arbitrary",), VMEM_BIG),
    )(y, r, k2, v, ga, o, gb, lng, lnb, rk, dh, zt_bf, w_bf, *extra)


PADSEQ = SEQ + LEFT * L
ATT_SCALE = 1.0 / math.sqrt(HD)
ATT_Q = 4
WIN = BAND + (ATT_Q - 1) * L
ATT_STEPS = NC // ATT_Q
ATT_BIAS_SHAPE = (NPAIR, ATT_Q * 2 * L, WIN)
ATT_WINDOW_BIAS_SHAPE = (ATT_Q, NPAIR, 2 * L, WIN)


def _stack_chunks(a):
    return jnp.concatenate([_stack_pair(a[i * L:(i + 1) * L]) for i in range(ATT_Q)], axis=0)


def _unstack_chunks(a):
    return jnp.concatenate([_unstack_pair(a[i * 2 * L:(i + 1) * 2 * L]) for i in range(ATT_Q)], axis=0)


def _window_bias(b_ref):
    return [jnp.concatenate([b_ref[c, p] for c in range(ATT_Q)], axis=0) for p in range(NPAIR)]


def _key_window(ref, step):
    start = step * (ATT_Q * L) - LEFT * L
    rows = ref[pl.ds(pl.multiple_of(jnp.maximum(start, 0), L), WIN), :]
    window = rows
    for lead in range(ATT_Q * L, LEFT * L + 1, ATT_Q * L):
        moved = jnp.concatenate([rows[WIN - lead:], rows[:WIN - lead]], axis=0)
        window = jnp.where(start == -lead, moved, window)
    return window


def _att_probs(q2, kw, bias, step):
    valid = _iota2((1, WIN), 1) >= (LEFT - step * ATT_Q) * L
    s = [jnp.where(valid, _bdot_nt(a, b) * ATT_SCALE + bias[p], NEG) for p, (a, b) in enumerate(zip(q2, kw))]
    e = [jnp.exp(a - jnp.max(a, axis=-1, keepdims=True)) for a in s]
    return [a / jnp.sum(a, axis=-1, keepdims=True) for a in e]


def attention_fwd(q, k, v, bias):
    def body(q_ref, k_ref, v_ref, b_ref, o_ref):
        step = pl.program_id(1)
        kw = _pairs(_key_window(k_ref, step))
        vw = _pairs(_key_window(v_ref, step))
        q2 = [_stack_chunks(a) for a in _pairs(q_ref[...])]
        p = _att_probs(q2, kw, _window_bias(b_ref), step)
        o_ref[...] = jnp.concatenate([_unstack_chunks(_bdot(a, b)) for a, b in zip(p, vw)], axis=-1)

    qblk = pl.BlockSpec((ATT_Q * L, W), lambda b, c: (b * ATT_STEPS + c, 0))
    kblk = pl.BlockSpec((SEQ, W), lambda b, c: (b, 0))
    return pl.pallas_call(
        body, grid=(NSEQ, ATT_STEPS), name="attention_fwd",
        in_specs=[qblk, kblk, kblk, _const_spec(ATT_WINDOW_BIAS_SHAPE)],
        out_specs=qblk, out_shape=jax.ShapeDtypeStruct((T, W), F32),
        compiler_params=_cparams(("parallel", "arbitrary")),
    )(q, k, v, bias)


def attention_bwd(q, k, v, bias, do):
    def body(q_ref, k_ref, v_ref, b_ref, do_ref, dq_ref, dko_ref, dvo_ref, db_ref, dk_ref, dv_ref):
        b = pl.program_id(0)
        c = pl.program_id(1)

        @pl.when(c == 0)
        def _():
            dk_ref[...] = jnp.zeros_like(dk_ref)
            dv_ref[...] = jnp.zeros_like(dv_ref)

        @pl.when((c == 0) & (b == 0))
        def _():
            db_ref[...] = jnp.zeros_like(db_ref)

        start = pl.multiple_of(c * (ATT_Q * L), L)
        kw = _pairs(_key_window(k_ref, c))
        vw = _pairs(_key_window(v_ref, c))
        q2 = [_stack_chunks(a) for a in _pairs(q_ref[...])]
        do2 = [_stack_chunks(a) for a in _pairs(do_ref[...].astype(BF16))]
        p = _att_probs(q2, kw, _window_bias(b_ref), c)
        dp = [_bdot_nt(a, b) for a, b in zip(do2, vw)]
        ds = [a * (d - jnp.sum(d * a, axis=-1, keepdims=True)) for a, d in zip(p, dp)]
        dss = [(a * ATT_SCALE).astype(BF16) for a in ds]
        dq_ref[...] = jnp.concatenate([_unstack_chunks(_bdot(a, b)) for a, b in zip(dss, kw)], axis=-1).astype(BF16)
        dk_ref[pl.ds(start, WIN), :] += jnp.concatenate([_bdot_tn(a, b) for a, b in zip(dss, q2)], axis=-1)
        dv_ref[pl.ds(start, WIN), :] += jnp.concatenate([_bdot_tn(a, b) for a, b in zip(p, do2)], axis=-1)
        for i in range(NPAIR):
            db_ref[i] += ds[i]

        @pl.when(c == ATT_STEPS - 1)
        def _():
            dko_ref[...] = dk_ref[LEFT * L:, :].astype(BF16)
            dvo_ref[...] = dv_ref[LEFT * L:, :].astype(BF16)

    qblk = pl.BlockSpec((ATT_Q * L, W), lambda b, c: (b * ATT_STEPS + c, 0))
    sblk = pl.BlockSpec((SEQ, W), lambda b, c: (b, 0))
    bblk = _const_spec(ATT_BIAS_SHAPE)
    return pl.pallas_call(
        body, grid=(NSEQ, ATT_STEPS), name="attention_bwd",
        in_specs=[qblk, sblk, sblk, _const_spec(ATT_WINDOW_BIAS_SHAPE), qblk],
        out_specs=[qblk, sblk, sblk, bblk],
        out_shape=[jax.ShapeDtypeStruct((T, W), BF16), jax.ShapeDtypeStruct((T, W), BF16),
                   jax.ShapeDtypeStruct((T, W), BF16), jax.ShapeDtypeStruct(ATT_BIAS_SHAPE, F32)],
        scratch_shapes=[pltpu.VMEM((PADSEQ, W), F32), pltpu.VMEM((PADSEQ, W), F32)],
        compiler_params=_cparams(("arbitrary", "arbitrary"), VMEM_BIG),
    )(q, k, v, bias, do)


NTAB = 2 * CLIP + 1
EXT = BAND + L


def _ext_onehot():
    n = _iota2((EXT, NTAB), 0)
    m = _iota2((EXT, NTAB), 1)
    return (jnp.clip(BAND - 1 - n, -CLIP, CLIP) + CLIP == m).astype(F32)


def bias_expand(table):
    def body(t_ref, o_ref):
        ext = _hdot_nt(t_ref[...], _ext_onehot())
        ext = jnp.concatenate([ext, jnp.zeros((NH, WIN - EXT), F32)], axis=-1)
        col = _iota2((NH, WIN), 1)
        for c in range(ATT_Q):
            inside = (col >= c * L) & (col < c * L + BAND)
            for i in range(L):
                shift = (c * L - (L - 1 - i)) % WIN
                o_ref[c, :, i, :] = jnp.where(inside, pltpu.roll(ext, shift, 1) if shift else ext, NEG)

    out = pl.pallas_call(body, name="bias_expand", out_shape=jax.ShapeDtypeStruct((ATT_Q, NH, L, WIN), F32))(table)
    return out.reshape(ATT_WINDOW_BIAS_SHAPE)


def bias_grad(dbias):
    def body(d_ref, o_ref):
        acc = jnp.zeros((NH, EXT), F32)
        zpad = jnp.zeros((NH, EXT - BAND), F32)
        for i in range(L):
            s = L - 1 - i
            row = jnp.concatenate([d_ref[:, i, :], zpad], axis=-1)
            acc = acc + (pltpu.roll(row, s, 1) if s else row)
        o_ref[...] = _hdot(acc, _ext_onehot())

    return pl.pallas_call(body, name="bias_grad", out_shape=jax.ShapeDtypeStruct((NH, NTAB), F32))(dbias)


def _group_cols(g):
    return slice(g * SGC, (g + 1) * SGC)


def _sg_norm(gv, lng, lnb):
    gc = gv - jnp.mean(gv, axis=-1, keepdims=True)
    rstd = lax.rsqrt(jnp.mean(gc * gc, axis=-1, keepdims=True) + LN_EPS)
    xhat = gc * rstd
    return xhat, rstd, xhat * lng + lnb


GMLP_BWD_CHUNKS = 2


def gmlp_fwd_loss(u, v, gate, lng, lnb, wm_bf, sgb_t, h, w_bf, g_final, target):
    tm = GMLP_BWD_CHUNKS * SGC

    def body(u_ref, v_ref, gt_ref, lng_ref, lnb_ref, wm_ref, sb_ref, h_ref, w_ref, g_ref, t_ref,
             dh_ref, loss_ref, dg_ref, zt_ref):
        zs = []
        for ch in range(GMLP_BWD_CHUNKS):
            rows = slice(ch * SGC, (ch + 1) * SGC)
            _, _, vln = _sg_norm(_gelu(v_ref[rows, :]), lng_ref[...], lnb_ref[...])
            vlb = vln.astype(BF16)
            zg = []
            for g in range(NG):
                cs = _group_cols(g)
                sv = jnp.dot(wm_ref[g], vlb[:, cs], preferred_element_type=F32) + sb_ref[:, g:g + 1]
                zg.append((_gelu(u_ref[rows, cs]) * sv * _silu(gt_ref[rows, cs])).astype(BF16))
            zs.append(jnp.concatenate(zg, axis=-1))
        z = jnp.concatenate(zs, axis=0)
        zt_ref[...] = z.T
        xv = h_ref[...] + jnp.dot(z, w_ref[...], preferred_element_type=F32)
        rstd = lax.rsqrt(jnp.mean(xv * xv, axis=-1, keepdims=True) + RMS_EPS)
        xhat = xv * rstd
        err = xhat * g_ref[...] - t_ref[...]
        part = 0.5 * jnp.sum(jnp.mean(err * err, axis=-1, keepdims=True), axis=0, keepdims=True)
        dout = err * (1.0 / D)

        @pl.when(pl.program_id(0) == 0)
        def _():
            loss_ref[...] = jnp.zeros_like(loss_ref)
            dg_ref[...] = jnp.zeros_like(dg_ref)

        loss_ref[...] += jnp.broadcast_to(part, loss_ref.shape)
        dg_ref[...] += jnp.sum(dout * xhat, axis=0, keepdims=True)
        dxh = dout * g_ref[...]
        dh_ref[...] = rstd * (dxh - xhat * jnp.mean(dxh * xhat, axis=-1, keepdims=True))

    return pl.pallas_call(
        body, grid=(T // tm,), name="gmlp_fwd_loss",
        in_specs=[_row_spec(tm, D)] * 3 + [_const_spec((1, D))] * 2
        + [_const_spec((NG, SGC, SGC)), _const_spec((SGC, NG)), _row_spec(tm, D), _const_spec((D, D)),
           _const_spec((1, D)), _row_spec(tm, D)],
        out_specs=[_row_spec(tm, D), _const_spec((8, 128)), _const_spec((1, D)), _col_spec(D, tm)],
        out_shape=[jax.ShapeDtypeStruct((T, D), F32), jax.ShapeDtypeStruct((8, 128), F32),
                   jax.ShapeDtypeStruct((1, D), F32), jax.ShapeDtypeStruct((D, T), BF16)],
        compiler_params=_cparams(("arbitrary",), VMEM_BIG),
    )(u, v, gate, lng, lnb, wm_bf, sgb_t, h, w_bf, g_final, target)


def gmlp_bwd(u, v, gate, lng, lnb, wm_bf, sgb_t, dh, zt_bf, w_bf):
    def body(u_ref, v_ref, gt_ref, lng_ref, lnb_ref, wm_ref, sb_ref, dh_ref, zt_ref, w_ref,
             du_ref, dv_ref, dgt_ref, dlng_ref, dlnb_ref, dwm_ref, dsb_ref, dw_ref, acc_ref):
        @pl.when(pl.program_id(0) == 0)
        def _():
            for ref in (dlng_ref, dlnb_ref, dwm_ref, dsb_ref):
                ref[...] = jnp.zeros_like(ref)

        dz = _out_proj_back(dh_ref, zt_ref, w_ref, dw_ref, acc_ref)
        sel = (_iota2((D, NG), 0) // SGC == _iota2((D, NG), 1)).astype(F32)
        for ch in range(GMLP_BWD_CHUNKS):
            rows = slice(ch * SGC, (ch + 1) * SGC)
            gv, dgv_dv = _gelu_both(v_ref[rows, :])
            xhat, rstd, vln = _sg_norm(gv, lng_ref[...], lnb_ref[...])
            vlb = vln.astype(BF16)
            dvln = []
            dsv_all = []
            for g in range(NG):
                cs = _group_cols(g)
                uu = u_ref[rows, cs]
                gg = gt_ref[rows, cs]
                dzz = dz[rows, cs]
                sv = jnp.dot(wm_ref[g], vlb[:, cs], preferred_element_type=F32) + sb_ref[:, g:g + 1]
                gu, dgu = _gelu_both(uu)
                sg, dsg = _silu_both(gg)
                dzgu = dzz * gu
                dsv = dzgu * sg
                dgt_ref[rows, cs] = (dzgu * sv * dsg).astype(BF16)
                du_ref[rows, cs] = (dzz * sv * sg * dgu).astype(BF16)
                dsb16 = dsv.astype(BF16)
                dvln.append(lax.dot_general(wm_ref[g], dsb16, (((0,), (0,)), ((), ())), preferred_element_type=F32))
                dwm_ref[g] += lax.dot_general(dsb16, vlb[:, cs], (((1,), (1,)), ((), ())),
                                              preferred_element_type=F32)
                dsv_all.append(dsv)
            dvl = jnp.concatenate(dvln, axis=-1)
            dsb_ref[...] += _hdot(jnp.concatenate(dsv_all, axis=-1), sel)
            dlng_ref[...] += jnp.sum(dvl * xhat, axis=0, keepdims=True)
            dlnb_ref[...] += jnp.sum(dvl, axis=0, keepdims=True)
            dxh = dvl * lng_ref[...]
            dgv = rstd * (dxh - jnp.mean(dxh, axis=-1, keepdims=True)
                          - xhat * jnp.mean(dxh * xhat, axis=-1, keepdims=True))
            dv_ref[rows, :] = (dgv * dgv_dv).astype(BF16)

    tm = GMLP_BWD_CHUNKS * SGC
    return pl.pallas_call(
        body, grid=(T // tm,), name="gmlp_bwd",
        in_specs=[_row_spec(tm, D)] * 3 + [_const_spec((1, D))] * 2
        + [_const_spec((NG, SGC, SGC)), _const_spec((SGC, NG)), _row_spec(tm, D), _col_spec(D, tm),
           _const_spec((D, D))],
        out_specs=[_row_spec(tm, D)] * 3 + [_const_spec((1, D))] * 2
        + [_const_spec((NG, SGC, SGC)), _const_spec((SGC, NG)), _const_spec((D, D))],
        out_shape=[jax.ShapeDtypeStruct((T, D), BF16)] * 3 + [jax.ShapeDtypeStruct((1, D), F32)] * 2
        + [jax.ShapeDtypeStruct((NG, SGC, SGC), F32), jax.ShapeDtypeStruct((SGC, NG), F32),
           jax.ShapeDtypeStruct((D, D), BF16)],
        scratch_shapes=[pltpu.VMEM((D, D), F32)],
        compiler_params=_cparams(("arbitrary",), VMEM_BIG),
    )(u, v, gate, lng, lnb, wm_bf, sgb_t, dh, zt_bf, w_bf)


NCHIP = 4
NDEV = 8
ANY = pl.BlockSpec(memory_space=pl.ANY)


HBM = pl.BlockSpec(memory_space=pltpu.HBM)
SEM = pl.BlockSpec(memory_space=pltpu.SEMAPHORE)
EFFECT = pltpu.SideEffectType.DATAFLOW_SIDE_EFFECTING


CHIPS, EVERY, SIBLING = "chips", "every", "sibling"
SLOTS = {CHIPS: NCHIP, EVERY: NDEV, SIBLING: 1}


def _peers(scope):
    x, y, c = lax.axis_index("x"), lax.axis_index("y"), lax.axis_index("c")
    if scope == SIBLING:
        return [((x, y, 1 - c), 0)], 0
    if scope == CHIPS:
        return [((px, py, c), 2 * px + py) for px, py in ((1 - x, y), (x, 1 - y), (1 - x, 1 - y))], 2 * x + y
    out = []
    for j in range(1, NDEV):
        px, py, pc = x ^ (j >> 2), y ^ ((j >> 1) & 1), c ^ (j & 1)
        out.append(((px, py, pc), 4 * px + 2 * py + pc))
    return out, 4 * x + 2 * y + c


def _send_copies(src, land, send, recv, scatter, scope, starting):
    peers, me = _peers(scope)
    copies = []
    for t in range(len(src)):
        for j, (dev, slot) in enumerate(peers):
            k = t * len(peers) + j
            copies.append(pltpu.make_async_remote_copy(
                src_ref=src[t].at[slot] if scatter else src[t], dst_ref=land[t].at[me if starting else slot],
                send_sem=send.at[k], recv_sem=recv.at[k], device_id=dev, device_id_type=MESH))
    return copies


def _own_copies(src, land, sems, scatter, scope):
    if scope == SIBLING:
        return []
    _, me = _peers(scope)
    return [pltpu.make_async_copy(src[t].at[me] if scatter else src[t], land[t].at[me], sems.at[t])
            for t in range(len(src))]


def send_start(srcs, scatter, scope, name, after=None):
    n = len(srcs)
    slots = SLOTS[scope]
    extra_specs, extra = _after_operand(after)
    lands = [pltpu.HBM(a.shape if scatter else (slots,) + a.shape, a.dtype) for a in srcs]
    sems = [pltpu.SemaphoreType.DMA((n * max(slots - 1, 1),))] * 2 + ([] if scope == SIBLING else
                                                                     [pltpu.SemaphoreType.DMA((n,))])
    k = len(sems)

    def body(*refs):
        first_out = n + len(extra)
        src, land = refs[:n], refs[first_out + k + n:first_out + k + 2 * n]
        for cp in _send_copies(src, land, refs[first_out], refs[first_out + 1], scatter, scope, True):
            cp.start()
        for cp in _own_copies(src, land, refs[first_out + k - 1], scatter, scope):
            cp.start()
        refs[-1][...] = jnp.zeros_like(refs[-1])

    out = pl.pallas_call(
        body, name=name,
        out_shape=(*sems, *[pltpu.HBM(a.shape, a.dtype) for a in srcs], *lands, jax.ShapeDtypeStruct((8, 128), F32)),
        in_specs=[HBM] * n + extra_specs,
        out_specs=(*[SEM] * k, *[HBM] * (2 * n), pl.BlockSpec(memory_space=pltpu.VMEM)),
        input_output_aliases={i: k + i for i in range(n)},
        compiler_params=pltpu.CompilerParams(has_side_effects=EFFECT),
    )(*[pltpu.with_memory_space_constraint(a, pltpu.HBM) for a in srcs], *extra)
    return list(out[:k]), list(out[k:k + n]), list(out[k + n:k + 2 * n]), out[-1]


def send_wait(started, after, scatter, scope, name, with_sources=False, only=None):
    sems, srcs, lands, _ = started
    n, k = len(srcs), len(sems)
    wanted = range(n) if only is None else only

    def body(*refs):
        src, land = refs[:n], refs[n:2 * n]
        for t, cp in enumerate(_own_copies(src, land, refs[2 * n + k - 1], scatter, scope)):
            if t in wanted:
                cp.wait()
        copies = _send_copies(src, land, refs[2 * n], refs[2 * n + 1], scatter, scope, False)
        for i, cp in enumerate(copies):
            if i // (len(copies) // n) in wanted:
                cp.wait_send()
                cp.wait_recv()

    arrs = list(srcs) + list(lands)
    out = pl.pallas_call(
        body, name=name, out_shape=tuple(pltpu.HBM(a.shape, a.dtype) for a in arrs),
        in_specs=[HBM] * (2 * n) + [SEM] * k + [ANY], out_specs=tuple([HBM] * (2 * n)),
        input_output_aliases={i: i for i in range(2 * n)},
        compiler_params=pltpu.CompilerParams(has_side_effects=EFFECT),
    )(*arrs, *sems, after)
    return (list(out[:n]), list(out[n:])) if with_sources else list(out[n:])


def gather_weights(arrs, split):
    n = len(arrs)

    def body(*refs):
        ins, outs = refs[:n], refs[n:2 * n]
        send1, recv1, send2, recv2, loc_in, loc_out = refs[2 * n:2 * n + 6]
        staged = refs[2 * n + 6:]
        x, y, c = lax.axis_index("x"), lax.axis_index("y"), lax.axis_index("c")
        me = 2 * x + y
        sibling = (x, y, 1 - c)
        peers = [(1 - x, y), (x, 1 - y), (1 - x, 1 - y)]

        def rows_of(t, core):
            half = arrs[t].shape[0] // 2
            return pl.ds(core * half, half)

        def part(ref, t, core):
            return ref.at[rows_of(t, core)] if split[t] else ref

        load = [pltpu.make_async_copy(ins[t], staged[t], loc_in.at[t]) for t in range(n)]
        store = [pltpu.make_async_copy(staged[t], outs[t].at[me], loc_out.at[t]) for t in range(n)]
        for cp in load:
            cp.start()
        first = []
        for t in range(n):
            for j, (px, py) in enumerate(peers):
                first.append(pltpu.make_async_remote_copy(
                    src_ref=part(ins[t], t, c), dst_ref=part(outs[t].at[me], t, c), send_sem=send1.at[t, j],
                    recv_sem=recv1.at[t, j], device_id=(px, py, c), device_id_type=MESH))
        for cp in first:
            cp.start()
        for cp_in, cp_out in zip(load, store):
            cp_in.wait()
            cp_out.start()
        passed = []
        for t in range(n):
            for j, (px, py) in enumerate(peers):
                landed = part(outs[t].at[2 * px + py], t, c)
                pltpu.make_async_remote_copy(
                    src_ref=landed, dst_ref=landed, send_sem=send1.at[t, j], recv_sem=recv1.at[t, j],
                    device_id=(x, y, c), device_id_type=MESH).wait_recv()
                if split[t]:
                    cp = pltpu.make_async_remote_copy(
                        src_ref=landed, dst_ref=landed, send_sem=send2.at[t, j], recv_sem=recv2.at[t, j],
                        device_id=sibling, device_id_type=MESH)
                    cp.start()
                    passed.append(cp)
        for t in range(n):
            for j, (px, py) in enumerate(peers):
                if split[t]:
                    other = part(outs[t].at[2 * px + py], t, 1 - c)
                    pltpu.make_async_remote_copy(
                        src_ref=other, dst_ref=other, send_sem=send2.at[t, j], recv_sem=recv2.at[t, j],
                        device_id=(x, y, c), device_id_type=MESH).wait_recv()
        for cp in first + passed:
            cp.wait_send()
        for cp in store:
            cp.wait()

    return pl.pallas_call(
        body, name="gather_weights", in_specs=[ANY] * n, out_specs=[ANY] * n,
        out_shape=[jax.ShapeDtypeStruct((NCHIP,) + a.shape, a.dtype) for a in arrs],
        scratch_shapes=[pltpu.SemaphoreType.DMA((n, 3))] * 4 + [pltpu.SemaphoreType.DMA((n,))] * 2
        + [pltpu.VMEM(a.shape, a.dtype) for a in arrs],
    )(*arrs)


def _adam_math(g, w, m, v):
    m = ADAM_B1 * m + (1.0 - ADAM_B1) * g
    v = ADAM_B2 * v + (1.0 - ADAM_B2) * (g * g)
    m_hat = m / (1.0 - ADAM_B1 ** ADAM_STEP)
    v_hat = v / (1.0 - ADAM_B2 ** ADAM_STEP)
    delta = -ADAM_LR * (m_hat / (jnp.sqrt(v_hat) + ADAM_EPS) + ADAM_WD * w)
    return delta, m, v


def _rows_tile(rows):
    return rows if rows <= 256 else 256


def sum_chips(parts, name):
    _, rows, cols = parts.shape
    tr = _rows_tile(rows)

    def body(p_ref, o_ref):
        acc = p_ref[0].astype(F32)
        for s in range(1, NCHIP):
            acc = acc + p_ref[s].astype(F32)
        o_ref[...] = acc

    return pl.pallas_call(
        body, grid=(rows // tr,), name=name,
        in_specs=[pl.BlockSpec((NCHIP, tr, cols), lambda i: (0, i, 0))],
        out_specs=pl.BlockSpec((tr, cols), lambda i: (i, 0)),
        out_shape=jax.ShapeDtypeStruct((rows, cols), F32),
        compiler_params=_cparams(("parallel",)),
    )(parts)


def sum_chips_small(parts, name):
    n = len(parts)

    def body(*refs):
        for p_ref, o_ref in zip(refs[:n], refs[n:]):
            acc = p_ref[0]
            for s in range(1, NCHIP):
                acc = acc + p_ref[s]
            o_ref[...] = acc

    return pl.pallas_call(body, name=name, out_shape=[jax.ShapeDtypeStruct(p.shape[1:], F32) for p in parts])(*parts)


def adam_shard_small(items, name):
    n = len(items)

    def body(*refs):
        for t in range(n):
            a_ref, b_ref, w_ref, m_ref, v_ref = refs[5 * t:5 * t + 5]
            g_ref, d_ref, mo_ref, vo_ref = refs[5 * n + 4 * t:5 * n + 4 * t + 4]
            g = (a_ref[...] + b_ref[...]).reshape(w_ref.shape)
            g_ref[...] = g
            d_ref[...], mo_ref[...], vo_ref[...] = _adam_math(g, w_ref[...], m_ref[...], v_ref[...])

    out = pl.pallas_call(
        body, name=name, out_shape=[jax.ShapeDtypeStruct(it[2].shape, F32) for it in items for _ in range(4)],
    )(*[a for it in items for a in it])
    return [out[4 * t:4 * t + 4] for t in range(n)]


def adam_shard(p_mine, p_sib, w, m, v, name):
    rows, cols = p_mine.shape
    tr = _rows_tile(rows)
    lead = w.ndim == 3

    def body(a_ref, b_ref, w_ref, m_ref, v_ref, g_ref, d_ref, mo_ref, vo_ref):
        g = a_ref[...] + b_ref[...]
        g = g[None] if lead else g
        g_ref[...] = g
        d_ref[...], mo_ref[...], vo_ref[...] = _adam_math(g, w_ref[...], m_ref[...], v_ref[...])

    flat = pl.BlockSpec((tr, cols), lambda i: (i, 0))
    spec = pl.BlockSpec((1, tr, cols), lambda i: (0, i, 0)) if lead else flat
    return pl.pallas_call(
        body, grid=(rows // tr,), name=name, in_specs=[flat] * 2 + [spec] * 3, out_specs=[spec] * 4,
        out_shape=[jax.ShapeDtypeStruct(w.shape, F32)] * 4,
        compiler_params=_cparams(("parallel",)),
    )(p_mine, p_sib, w, m, v)


def adam_shard_halves_t(r_mine, r_sib, wt, mt, vt, name):
    hrows, cols = r_mine.shape
    tr = _rows_tile(hrows)
    per_half = hrows // tr

    def body(a_ref, b_ref, w_ref, m_ref, v_ref, g_ref, d_ref, mo_ref, vo_ref):
        mine = pl.program_id(0) == lax.axis_index("c")
        g = jnp.where(mine, a_ref[...], b_ref[...]).T[None]
        g_ref[...] = g
        d_ref[...], mo_ref[...], vo_ref[...] = _adam_math(g, w_ref[...], m_ref[...], v_ref[...])

    flat = pl.BlockSpec((tr, cols), lambda h, i: (i, 0))
    spec = pl.BlockSpec((1, cols, tr), lambda h, i: (0, 0, h * per_half + i))
    return pl.pallas_call(
        body, grid=(2, per_half), name=name, in_specs=[flat] * 2 + [spec] * 3, out_specs=[spec] * 4,
        out_shape=[jax.ShapeDtypeStruct(wt.shape, F32)] * 4,
        compiler_params=_cparams(("parallel", "parallel")),
    )(r_mine, r_sib, wt, mt, vt)


def adam_replicated(gathered, params, name):
    flat = []
    for i, p in enumerate(params):
        if isinstance(p, list):
            off = 0
            for wmv in p:
                n = gathered[i].shape[-1] - off if wmv[0] is None else wmv[0].shape[-1]
                flat.append((i, (off, n), wmv))
                off += n
        else:
            flat.append((i, None, p))
    ins = [a for _, _, wmv in flat for a in wmv if a is not None]
    ng = len(gathered)

    def body(*refs):
        g_refs = refs[:ng]
        in_refs = list(refs[ng:ng + len(ins)])
        out_refs = list(refs[ng + len(ins):])
        sums = []
        for r in g_refs:
            g = r[0]
            for d in range(1, NDEV):
                g = g + r[d]
            sums.append(g)
        for i, lanes, wmv in flat:
            g = sums[i] if lanes is None else sums[i][:, lanes[0]:lanes[0] + lanes[1]]
            out_refs.pop(0)[...] = g
            if wmv[0] is not None:
                w_ref, m_ref, v_ref = in_refs.pop(0), in_refs.pop(0), in_refs.pop(0)
                d_ref, mo_ref, vo_ref = out_refs.pop(0), out_refs.pop(0), out_refs.pop(0)
                d_ref[...], mo_ref[...], vo_ref[...] = _adam_math(g, w_ref[...], m_ref[...], v_ref[...])

    out_shape = []
    for i, lanes, wmv in flat:
        shape = gathered[i].shape[1:] if lanes is None else (1, lanes[1])
        out_shape += [jax.ShapeDtypeStruct(shape, F32)] * (4 if wmv[0] is not None else 1)
    outs = list(pl.pallas_call(body, name=name, out_shape=out_shape)(*gathered, *ins))
    return [[outs.pop(0) for _ in range(4 if wmv[0] is not None else 1)] for _, _, wmv in flat]


EVEN_SPLITS = (SHIFT, W, W, W, W, W)
ODD_SPLITS = (D, D, D)


def _cols_to_chips(a):
    rows, cols = a.shape
    return a.reshape(rows, NCHIP, cols // NCHIP).transpose(1, 0, 2)


def _chips_to_cols(a):
    _, rows, n = a.shape
    return a.transpose(1, 0, 2).reshape(rows, NCHIP * n)


def kernel(x, norm_g, w_in_e, shift_mu, rw_w0, rw_w2, rw_a0, rw_a2, rw_kk, rw_ka, rw_rk, rw_lnx_g, rw_lnx_b, att_bias, w_out_e, w_in_o, sg_ln_g, sg_ln_b, sg_w, sg_b, w_out_o, final_g, loss_target, m_norm_g, m_w_in_e, m_shift_mu, m_rw_w0, m_rw_w2, m_rw_a0, m_rw_a2, m_rw_kk, m_rw_ka, m_rw_rk, m_rw_lnx_g, m_rw_lnx_b, m_att_bias, m_w_out_e, m_w_in_o, m_sg_ln_g, m_sg_ln_b, m_sg_w, m_sg_b, m_w_out_o, m_final_g, v_norm_g, v_w_in_e, v_shift_mu, v_rw_w0, v_rw_w2, v_rw_a0, v_rw_a2, v_rw_kk, v_rw_ka, v_rw_rk, v_rw_lnx_g, v_rw_lnx_b, v_att_bias, v_w_out_e, v_w_in_o, v_sg_ln_g, v_sg_ln_b, v_sg_w, v_sg_b, v_w_out_o, v_final_g):
    x2 = x.reshape(T, D)
    tgt = loss_target.reshape(T, D)

    gathered = gather_weights(
        [jnp.swapaxes(w_in_e[0], 0, 1).astype(BF16), jnp.concatenate([rw_w2[0], rw_a2[0]], axis=0),
         jnp.concatenate([sg_ln_g, sg_ln_b], axis=0)], [True, True, False])
    wie = gathered[0].reshape(EVEN_IN, D)
    w2 = _chips_to_cols(gathered[1][:, :LORA])
    a2 = _chips_to_cols(gathered[1][:, LORA:])
    sglg = _chips_to_cols(gathered[2][:, 0:1])
    sglb = _chips_to_cols(gathered[2][:, 1:2])

    late = [w_out_e[0].astype(BF16), w_in_o[0].astype(BF16), w_out_o[0].astype(BF16)]
    late_started = send_start(late, False, CHIPS, "late_weights_start", after=gathered[0])

    late_state = {}

    def late_weights(layer, after):
        if layer == "even":
            srcs, lands = send_wait(late_started, after, False, CHIPS, "late_w_out_e_wait", True, only=(0,))
            late_state["rest"] = (late_started[0], srcs, lands, None)
            return lands[0].reshape(D, D)
        woe, wio, woo = send_wait(late_state["rest"], after, False, CHIPS, "late_weights_wait", only=(1, 2))
        return woe.reshape(D, D), wio, woo.reshape(D, D)

    def scatter_start(grads, name):
        return send_start([g_.astype(BF16) if g_.shape[-1] >= W else g_ for g_ in grads], True, CHIPS, name)

    started = {}

    def on_odd_grads(d_woo, d_wio):
        started["odd"] = scatter_start([d_woo.reshape(NCHIP, D // NCHIP, D), d_wio], "odd_grads_start")
        return started["odd"][-1]

    def on_even_grads(big_g):
        d_wie_half, d_woe, _, _, d_w2, d_a2, d_sglg, d_sglb = big_g
        blocks = [d_wie_half, d_woe.reshape(NCHIP, D // NCHIP, D), _cols_to_chips(d_w2), _cols_to_chips(d_a2),
                  _cols_to_chips(d_sglg), _cols_to_chips(d_sglb)]
        started["even"] = scatter_start(blocks, "even_grads_start")
        return started["even"][-1]

    def on_small_grads(layer, grads):
        if layer == "odd":
            d_sg_w, d_sg_b, d_final, d_g1 = grads
            mine = [d_sg_w.reshape(NG * SGC, SGC), d_sg_b, jnp.concatenate([d_final, d_g1], axis=1)]
        else:
            mine = [grads[-2], jnp.concatenate(grads[:-2] + grads[-1:], axis=1)]
        started[layer + "_small"] = send_start(mine, False, EVERY, layer + "_small_grads_start")
        return started[layer + "_small"][-1]

    loss_part, dx, _, _ = _local_step(
        x2, tgt, wie, late_weights, w2, a2, sglg, sglb, norm_g, shift_mu, rw_w0, rw_a0, rw_kk, rw_ka, rw_rk,
        rw_lnx_g, rw_lnx_b, att_bias, sg_w, sg_b, final_g, first_after=late_started[-1], on_odd_grads=on_odd_grads,
        on_even_grads=on_even_grads, on_small_grads=on_small_grads)
    wmv = {"w_in_e": tuple(jnp.swapaxes(a, 1, 2) for a in (w_in_e, m_w_in_e, v_w_in_e)),
           "w_out_e": (w_out_e, m_w_out_e, v_w_out_e),
           "w_in_o": (w_in_o, m_w_in_o, v_w_in_o), "w_out_o": (w_out_o, m_w_out_o, v_w_out_o),
           "rw_w2": (rw_w2, m_rw_w2, v_rw_w2), "rw_a2": (rw_a2, m_rw_a2, v_rw_a2),
           "sg_ln_g": (sg_ln_g, m_sg_ln_g, v_sg_ln_g), "sg_ln_b": (sg_ln_b, m_sg_ln_b, v_sg_ln_b)}
    sharded = {}

    def sum_and_swap(names, landed, tag):
        nbig = sum(p_.dtype == BF16 for p_ in landed)
        partial = [sum_chips(p_, "sum_" + nm) for p_, nm in zip(landed[:nbig], names)]
        if nbig < len(names):
            partial += sum_chips_small(landed[nbig:], "sum_small_" + tag)
        return send_start(partial, False, SIBLING, "swap_partials_" + tag + "_start")

    def update(names, swap_started, after, tag):
        partial, landed = send_wait(swap_started, after, False, SIBLING, "swap_partials_" + tag + "_wait", True)
        from_sibling = [a[0] for a in landed]
        nbig = sum(p_.shape[-1] >= W for p_ in partial)
        for nm, mine, sib in zip(names[:nbig], partial, from_sibling):
            if nm == "w_in_e":
                res = adam_shard_halves_t(mine, sib, *wmv[nm], "adam_" + nm)
                sharded[nm] = [jnp.swapaxes(a, 1, 2) for a in res]
            else:
                sharded[nm] = adam_shard(mine, sib, *wmv[nm], "adam_" + nm)
        if nbig < len(names):
            items = [(mine, sib, *wmv[nm]) for nm, mine, sib in zip(names, partial, from_sibling)][nbig:]
            for nm, res in zip(names[nbig:], adam_shard_small(items, "adam_small_" + tag)):
                sharded[nm] = res

    odd_names = ["w_out_o", "w_in_o"]
    even_names = ["w_in_e", "w_out_e", "rw_w2", "rw_a2", "sg_ln_g", "sg_ln_b"]
    odd_landed = send_wait(started["odd"], started["even_small"][-1], True, CHIPS, "odd_grads_wait")
    odd_swap = sum_and_swap(odd_names, odd_landed, "odd")
    done = odd_swap[-1]

    def wmv_of(*arrs, view=lambda a: a):
        return tuple(view(a) for a in arrs)

    vec = lambda a: a.reshape(1, -1)
    groups = {
        "odd": (["sg_w", "sg_b", "final_g", "norm_g1"],
                [wmv_of(sg_w, m_sg_w, v_sg_w, view=lambda a: a.reshape(NG * SGC, SGC)),
                 wmv_of(sg_b, m_sg_b, v_sg_b, view=lambda a: a[0]),
                 [wmv_of(final_g, m_final_g, v_final_g, view=vec),
                  wmv_of(norm_g, m_norm_g, v_norm_g, view=lambda a: a[1:2])]]),
        "even": (["att_bias", "norm_g0", "shift_mu", "rw_w0", "rw_a0", "rw_kk", "rw_ka", "rw_rk", "rw_lnx_g",
                  "rw_lnx_b", "loss"],
                 [wmv_of(att_bias, m_att_bias, v_att_bias, view=lambda a: a[0]),
                  [wmv_of(norm_g, m_norm_g, v_norm_g, view=lambda a: a[0:1]),
                   wmv_of(shift_mu, m_shift_mu, v_shift_mu), wmv_of(rw_w0, m_rw_w0, v_rw_w0),
                   wmv_of(rw_a0, m_rw_a0, v_rw_a0), wmv_of(rw_kk, m_rw_kk, v_rw_kk), wmv_of(rw_ka, m_rw_ka, v_rw_ka),
                   wmv_of(rw_rk, m_rw_rk, v_rw_rk, view=vec), wmv_of(rw_lnx_g, m_rw_lnx_g, v_rw_lnx_g),
                   wmv_of(rw_lnx_b, m_rw_lnx_b, v_rw_lnx_b), (None, None, None)]]),
    }
    rep = {}
    for layer in ("odd", "even"):
        nms, params = groups[layer]
        gathered_g = send_wait(started[layer + "_small"], done, False, EVERY, layer + "_small_grads_wait")
        for nm, res in zip(nms, adam_replicated(gathered_g, params, "adam_" + layer + "_small")):
            rep[nm] = res
        done = rep[nms[0]][0]
    native = {"sg_w": sg_w.shape, "sg_b": sg_b.shape, "final_g": final_g.shape, "rw_rk": rw_rk.shape,
              "att_bias": att_bias.shape}
    for nm, shape in native.items():
        rep[nm] = [a.reshape(shape) for a in rep[nm]]
    rep["norm_g"] = [jnp.concatenate([a, b], axis=0) for a, b in zip(rep["norm_g0"], rep["norm_g1"])]
    even_landed = send_wait(started["even"], done, True, CHIPS, "even_grads_wait")
    even_swap = sum_and_swap(even_names, even_landed, "even")
    update(odd_names, odd_swap, even_swap[-1], "odd")
    update(even_names, even_swap, sharded["w_in_o"][0], "even")

    order = ["norm_g", "w_in_e", "shift_mu", "rw_w0", "rw_w2", "rw_a0", "rw_a2", "rw_kk", "rw_ka", "rw_rk",
             "rw_lnx_g", "rw_lnx_b", "att_bias", "w_out_e", "w_in_o", "sg_ln_g", "sg_ln_b", "sg_w", "sg_b",
             "w_out_o", "final_g"]
    results = {**sharded, **rep}
    outs = [rep["loss"][0][0, 0], dx.reshape(NSEQ, SEQ, D)]
    for kind in range(4):
        outs += [results[nm][kind] for nm in order]
    return tuple(outs)


def _local_step(x2, tgt, wie_t, late_weights, w2, a2, sglg, sglb, norm_g, shift_mu, rw_w0, rw_a0, rw_kk, rw_ka, rw_rk,
                rw_lnx_g, rw_lnx_b, att_bias, sg_w, sg_b, final_g, first_after=None, on_odd_grads=None,
                on_even_grads=None, on_small_grads=None):
    zl = jnp.zeros((LORA, W), F32)
    w2x = jnp.concatenate([w2, zl], axis=0)
    a2x = jnp.concatenate([zl, a2], axis=0)
    rk = rw_rk.reshape(1, W)
    pos = np.arange(SGC)
    sg_mask = jnp.asarray(((pos[None, :] // L) <= (pos[:, None] // L)).astype(np.float32))
    wm = (sg_w[0] * sg_mask[None]).astype(BF16)
    sgb_t = sg_b[0].T

    xn0, ps, ga, q, kb, vb, gb = ln_in_proj(x2, norm_g[0:1], wie_t, EVEN_SPLITS, "in_proj_even", after=first_after,
                                            w_t=True, bf16_pieces=(2, 3, 4))
    r, lw, k2, v, aa, bb = even_prep(ps, shift_mu, rw_w0, w2x, rw_a0, a2x, rw_kk, rw_ka)
    y, rw_saved = rwkv_fwd(r, lw, k2, v, aa, bb)
    bias = bias_expand(att_bias[0])

    o = attention_fwd(q, kb, vb, bias)
    woe = late_weights("even", o)
    h1, zt = even_post(y, r, k2, v, ga, o, gb, rw_lnx_g, rw_lnx_b, rk, x2, woe)
    woe, wio, woo = late_weights("odd", h1)
    xn1, u, vv, gt = ln_in_proj(h1, norm_g[1:2], wio, ODD_SPLITS, "in_proj_odd")
    dh2, loss_part, d_final_g, z2t = gmlp_fwd_loss(u, vv, gt, sglg, sglb, wm, sgb_t, h1, woo, final_g[None], tgt)

    du, dvv, dgt, d_sglg, d_sglb, d_wm, d_sgb_t, d_woo = gmlp_bwd(u, vv, gt, sglg, sglb, wm, sgb_t, dh2, z2t, woo)
    dp_odd = [du, dvv, dgt]
    d_wio = matmul_acc_chips(xn1, dp_odd, "in_proj_odd_dw")
    token = on_odd_grads(d_woo, d_wio) if on_odd_grads else None
    dh1, d_g1 = in_proj_bwd_x(h1, norm_g[1:2], wio, dp_odd, dh2, "in_proj_odd_bwd", after=token)
    odd_small = [d_wm * sg_mask[None], d_sgb_t.T, d_final_g, d_g1]
    token = on_small_grads("odd", odd_small) if on_small_grads else None
    dy, dr2, dk22, dv2, dga, do, dgb, d_lng, d_lnb, d_rk, d_woe = even_post_bwd(
        y, r, k2, v, ga, o, gb, rw_lnx_g, rw_lnx_b, rk, dh1, zt, woe, after=token)
    dq, dkb, dvb, dbias = attention_bwd(q, kb, vb, bias, do)
    dbias = sum(dbias[:, i * 2 * L:(i + 1) * 2 * L, i * L:i * L + BAND] for i in range(ATT_Q))
    d_att_bias = bias_grad(dbias.reshape(NH, L, BAND))
    dr, dlw, dk2, dv, daa, dbb = rwkv_bwd(r, lw, k2, aa, bb, rw_saved, dy)
    dps, d_mu, d_w0, d_w2x, d_a0, d_a2x, d_kk, d_ka = even_prep_bwd(
        ps, shift_mu, rw_w0, w2x, rw_a0, a2x, rw_kk, rw_ka, dr, dlw, dk2, dv, daa, dbb, dr2, dk22, dv2)
    dp_even = [dps, dga, dq, dkb, dvb, dgb]
    d_wie = matmul_acc_chips(xn0, dp_even, "in_proj_even_dw", add_cores=on_even_grads is not None)
    big_g = (d_wie, d_woe, d_wio, d_woo, d_w2x[:LORA], d_a2x[LORA:], d_sglg, d_sglb)
    token = on_even_grads(big_g) if on_even_grads else None
    dx, d_g0 = in_proj_bwd_x(x2, norm_g[0:1], wie_t, dp_even, dh1, "in_proj_even_bwd", after=token, w_t=True)
    even_small = [d_g0, d_mu, d_w0, d_a0, d_kk, d_ka, d_rk, d_lng, d_lnb, d_att_bias]
    if on_small_grads:
        on_small_grads("even", even_small + [loss_part[0:1, :]])
    rep_g = [jnp.concatenate([d_g0, d_g1], axis=0)] + even_small[1:] + odd_small[:3]
    return loss_part[0, 0], dx, big_g, rep_g
```

```python
import functools
import math

import jax
import jax.numpy as jnp
import numpy as np
from jax import lax
from jax.experimental import pallas as pl
from jax.experimental.pallas import tpu as pltpu

F32 = jnp.float32
BF16 = jnp.bfloat16
HI = lax.Precision.HIGHEST

D = 1024
SEQ = 2048
NSEQ = 2
T = NSEQ * SEQ
HD = 64
NH = 8
W = 512
SHIFT = 1664
LORA = 64
EVEN_IN = 4224
ODD_IN = 3072
L = 64
NC = SEQ // L
LEFT = 8
BAND = (LEFT + 1) * L
CLIP = 128
SGC = 128
NG = 8
RMS_EPS = 1e-6
LN_EPS = 1e-5
GN_EPS = 64e-5
NEG = -1e30
VMEM_BIG = 56 * 1024 * 1024

ADAM_LR = 0.001
ADAM_B1 = 0.9
ADAM_B2 = 0.999
ADAM_EPS = 1e-08
ADAM_WD = 0.01
ADAM_STEP = 10

MESH = pl.DeviceIdType.MESH


def _bdot(a, b):
    return jnp.dot(a.astype(BF16), b.astype(BF16), preferred_element_type=F32)


def _bdot_nt(a, b):
    return lax.dot_general(a.astype(BF16), b.astype(BF16), (((1,), (1,)), ((), ())), preferred_element_type=F32)


def _bdot_tn(a, b):
    return lax.dot_general(a.astype(BF16), b.astype(BF16), (((0,), (0,)), ((), ())), preferred_element_type=F32)


def _hdot(a, b):
    return jnp.dot(a, b, precision=HI, preferred_element_type=F32)


def _hdot_nt(a, b):
    return lax.dot_general(a, b, (((1,), (1,)), ((), ())), precision=HI, preferred_element_type=F32)


def _hdot_tn(a, b):
    return lax.dot_general(a, b, (((0,), (0,)), ((), ())), precision=HI, preferred_element_type=F32)


def _iota2(shape, dim):
    return lax.broadcasted_iota(jnp.int32, shape, dim)


def _head_blockdiag():
    r = _iota2((2 * HD, 2 * HD), 0) // HD
    c = _iota2((2 * HD, 2 * HD), 1) // HD
    return (r == c).astype(BF16)


def _headsum_impl(x, bd):
    hi = x.astype(BF16)
    mid = (x - hi.astype(F32)).astype(BF16)
    n = bd.shape[0]
    out = [jnp.dot(hi[:, i:i + n], bd, preferred_element_type=F32) + jnp.dot(mid[:, i:i + n], bd, preferred_element_type=F32)
           for i in range(0, x.shape[1], n)]
    return jnp.concatenate(out, axis=-1)


@jax.custom_vjp
def _headsum(x, bd):
    return _headsum_impl(x, bd)


def _headsum_fwd(x, bd):
    return _headsum_impl(x, bd), bd


def _headsum_bwd(bd, ct):
    return _headsum_impl(ct, bd), None


_headsum.defvjp(_headsum_fwd, _headsum_bwd)


def _silu(x):
    return x * jax.nn.sigmoid(x)


_GELU_C = math.sqrt(2.0 / math.pi)


def _gelu(x):
    return 0.5 * x * (1.0 + jnp.tanh(_GELU_C * (x + 0.044715 * (x * x * x))))


def _silu_both(x):
    s = jax.nn.sigmoid(x)
    xs = x * s
    return xs, s + xs * (1.0 - s)


def _gelu_both(x):
    x2 = x * x
    t = jnp.tanh(_GELU_C * (x + 0.044715 * (x2 * x)))
    half = 0.5 * (1.0 + t)
    return x * half, half + 0.5 * x * (1.0 - t * t) * _GELU_C * (1.0 + 3.0 * 0.044715 * x2)


def _softplus(x):
    return jnp.maximum(x, 0.0) + jnp.log(1.0 + jnp.exp(-jnp.abs(x)))


def _cparams(sem, vmem=None):
    return pltpu.CompilerParams(dimension_semantics=sem, vmem_limit_bytes=vmem)


def _row_spec(tm, width):
    return pl.BlockSpec((tm, width), lambda i: (i, 0))


def _col_spec(height, tm):
    return pl.BlockSpec((height, tm), lambda i: (0, i))


def _const_spec(shape):
    nd = len(shape)
    return pl.BlockSpec(shape, lambda *_: (0,) * nd)


def _weight_dims(w_bf, w_t):
    if w_bf.ndim == 3:
        return None, w_bf.shape[0] * w_bf.shape[2]
    return (((1,), (1,)), ((), ())) if w_t else (((1,), (0,)), ((), ())), w_bf.shape[0 if w_t else 1]


def _proj(xn, w_ref, dims):
    if dims is None:
        return jnp.concatenate([jnp.dot(xn, w_ref[s], preferred_element_type=F32) for s in range(w_ref.shape[0])],
                               axis=-1)
    return lax.dot_general(xn, w_ref[...], dims, preferred_element_type=F32)


def _proj_back(dp, w_ref, w_t):
    nt = (((1,), (1,)), ((), ()))
    if len(w_ref.shape) == 3:
        nb = w_ref.shape[2]
        parts = [lax.dot_general(dp[:, s * nb:(s + 1) * nb], w_ref[s], nt, preferred_element_type=F32)
                 for s in range(w_ref.shape[0])]
        return sum(parts[1:], parts[0])
    return lax.dot_general(dp, w_ref[...], (((1,), (0,)), ((), ())) if w_t else nt, preferred_element_type=F32)


def ln_in_proj(x, g, w_bf, splits, name, after=None, w_t=False, bf16_pieces=()):
    dims, n = _weight_dims(w_bf, w_t)
    dtypes = [BF16 if i in bf16_pieces else F32 for i in range(len(splits))]
    tm = 512 if n <= ODD_IN else 256
    spans = []
    o = 0
    for s in splits:
        spans.append((o, o + s))
        o += s
    assert o == n
    extra_specs, extra = _after_operand(after)

    def body(x_ref, g_ref, w_ref, *rest):
        xn_ref, outs = rest[len(extra)], rest[len(extra) + 1:]
        xv = x_ref[...]
        rstd = lax.rsqrt(jnp.mean(xv * xv, axis=-1, keepdims=True) + RMS_EPS)
        xn = (xv * rstd * g_ref[...]).astype(BF16)
        xn_ref[...] = xn.T
        p = _proj(xn, w_ref, dims)
        for o_ref, (a, b) in zip(outs, spans):
            o_ref[...] = p[:, a:b].astype(o_ref.dtype)

    return pl.pallas_call(
        body, grid=(T // tm,), name=name,
        in_specs=[_row_spec(tm, D), _const_spec((1, D)), _const_spec(w_bf.shape)] + extra_specs,
        out_specs=[_col_spec(D, tm)] + [_row_spec(tm, s) for s in splits],
        out_shape=[jax.ShapeDtypeStruct((D, T), BF16)]
        + [jax.ShapeDtypeStruct((T, s), dt) for s, dt in zip(splits, dtypes)],
        compiler_params=_cparams(("parallel",), VMEM_BIG),
    )(x, g, w_bf, *extra)


def in_proj_bwd_x(x, g, w_bf, dps, dres, name, after=None, w_t=False):
    tm = 512
    widths = [d.shape[1] for d in dps]
    extra_specs, extra = _after_operand(after)

    def body(x_ref, g_ref, w_ref, dres_ref, *rest):
        dp_refs = rest[:len(widths)]
        dx_ref, dg_ref = rest[-2:]
        dp = jnp.concatenate([r[...] for r in dp_refs], axis=-1)
        dxn = _proj_back(dp, w_ref, w_t)
        xv = x_ref[...]
        rstd = lax.rsqrt(jnp.mean(xv * xv, axis=-1, keepdims=True) + RMS_EPS)
        xhat = xv * rstd
        dgp = jnp.sum(dxn * xhat, axis=0, keepdims=True)

        @pl.when(pl.program_id(0) == 0)
        def _():
            dg_ref[...] = jnp.zeros_like(dg_ref)

        dg_ref[...] += dgp
        dxh = dxn * g_ref[...]
        dx_ref[...] = dres_ref[...] + rstd * (dxh - xhat * jnp.mean(dxh * xhat, axis=-1, keepdims=True))

    return pl.pallas_call(
        body, grid=(T // tm,), name=name,
        in_specs=[_row_spec(tm, D), _const_spec((1, D)), _const_spec(w_bf.shape), _row_spec(tm, D)]
        + [_row_spec(tm, s) for s in widths] + extra_specs,
        out_specs=[_row_spec(tm, D), _const_spec((1, D))],
        out_shape=[jax.ShapeDtypeStruct((T, D), F32), jax.ShapeDtypeStruct((1, D), F32)],
        compiler_params=_cparams(("arbitrary",), VMEM_BIG),
    )(x, g, w_bf, dres, *dps, *extra)


def _after_operand(after):
    return ([ANY], [after]) if after is not None else ([], [])


def matmul_acc_chips(at_bf, pieces, name, after=None, add_cores=False):
    k = at_bf.shape[0]
    widths = [p.shape[1] for p in pieces]
    nb = sum(widths) // NCHIP
    tm = 512
    steps = T // tm
    half = k // 2
    extra_specs, extra = _after_operand(after)

    def body(a_ref, *rest):
        o_ref, acc = rest[len(widths) + len(extra):][:2]

        @pl.when(pl.program_id(0) == 0)
        def _():
            acc[...] = jnp.zeros_like(acc)

        a = a_ref[...]
        b = jnp.concatenate([r[...] for r in rest[:len(widths)]], axis=-1)
        for s in range(NCHIP):
            acc[s] += jnp.dot(a, b[:, s * nb:(s + 1) * nb], preferred_element_type=F32)

        @pl.when(pl.program_id(0) == steps - 1)
        def _():
            if not add_cores:
                o_ref[...] = acc[...].astype(BF16)
            else:
                give, got, send, recv = rest[-4:]
                x, y, c = lax.axis_index("x"), lax.axis_index("y"), lax.axis_index("c")
                theirs = pl.multiple_of((1 - c) * half, half)
                mine = pl.multiple_of(c * half, half)
                give[...] = acc[:, pl.ds(theirs, half), :].astype(BF16)
                cp = pltpu.make_async_remote_copy(src_ref=give, dst_ref=got, send_sem=send, recv_sem=recv,
                                                  device_id=(x, y, 1 - c), device_id_type=MESH)
                cp.start()
                cp.wait()
                o_ref[...] = (acc[:, pl.ds(mine, half), :] + got[...].astype(F32)).astype(BF16)

    out_rows = half if add_cores else k
    exchange = [pltpu.VMEM((NCHIP, half, nb), BF16)] * 2 + [pltpu.SemaphoreType.DMA] * 2 if add_cores else []
    return pl.pallas_call(
        body, grid=(steps,), name=name,
        in_specs=[_col_spec(k, tm)] + [_row_spec(tm, w_) for w_ in widths] + extra_specs,
        out_specs=_const_spec((NCHIP, out_rows, nb)),
        out_shape=jax.ShapeDtypeStruct((NCHIP, out_rows, nb), BF16),
        scratch_shapes=[pltpu.VMEM((NCHIP, k, nb), F32)] + exchange,
        compiler_params=_cparams(("arbitrary",), VMEM_BIG),
    )(at_bf, *pieces, *extra)


def _out_proj_back(dh_ref, zt_ref, w_ref, dw_ref, acc_ref):
    dhb = dh_ref[...].astype(BF16)

    @pl.when(pl.program_id(0) == 0)
    def _():
        acc_ref[...] = jnp.zeros_like(acc_ref)

    acc_ref[...] += jnp.dot(zt_ref[...], dhb, preferred_element_type=F32)

    @pl.when(pl.program_id(0) == pl.num_programs(0) - 1)
    def _():
        dw_ref[...] = acc_ref[...].astype(dw_ref.dtype)

    return lax.dot_general(dhb, w_ref[...], (((1,), (1,)), ((), ())), preferred_element_type=F32)


PREP_TM = 512
PREP_NB = SEQ // PREP_TM


def _prep_elem(k, wl, apre, kkw, kaw, bd):
    wraw = -_softplus(-wl) - 0.5
    lw = -jnp.exp(wraw)
    asig = jax.nn.sigmoid(apre)
    kkr = k * kkw
    nrm = jnp.maximum(jnp.sqrt(_headsum(kkr * kkr, bd)), 1e-12)
    kk = kkr / nrm
    k2 = k * (1.0 + (asig - 1.0) * kaw)
    return lw, k2, -kk, kk * asig


def _prep_elem_bwd(k, wl, apre, kkw, kaw, bd, dlw, dk2, daa, dbb):
    s = -wl
    sp = _softplus(s)
    dwl = dlw * (-jnp.exp(-sp - 0.5)) * jnp.exp(s - sp)
    asig = jax.nn.sigmoid(apre)
    kkr = k * kkw
    root = jnp.sqrt(_headsum(kkr * kkr, bd))
    inv = 1.0 / jnp.maximum(root, 1e-12)
    kk = kkr * inv
    dkk = dbb * asig - daa
    dap = (dbb * kk + dk2 * k * kaw) * asig * (1.0 - asig)
    through_norm = jnp.where(root > 1e-12, kk * _headsum(dkk * kkr, bd) * inv, 0.0)
    dkkr = inv * (dkk - through_norm)
    gain = 1.0 + (asig - 1.0) * kaw
    dk = dkkr * kkw + dk2 * gain
    dkkw = jnp.sum(dkkr * k, axis=0, keepdims=True)
    dkaw = jnp.sum(dk2 * k * (asig - 1.0), axis=0, keepdims=True)
    return dk, dwl, dap, dkkw, dkaw


def _shifted(ps_ref, prev_ref, mu, blk):
    p = ps_ref[...]
    first = (blk % PREP_NB) == 0
    prev_row = jnp.where(first, 0.0, prev_ref[7:8, :])
    rolled = pltpu.roll(p, 1, 0)
    p_prev = jnp.where(_iota2(p.shape, 0) == 0, prev_row, rolled)
    return p, p_prev, p + (p_prev - p) * mu


def _prev_spec(width, blk_of):
    return pl.BlockSpec((8, width), lambda i: (jnp.maximum(blk_of(i) * (PREP_TM // 8) - 1, 0), 0))


def even_prep(ps, mu, w0, w2x, a0, a2x, kkw, kaw):
    tm = PREP_TM

    def body(ps_ref, prev_ref, mu_ref, w0_ref, w2_ref, a0_ref, a2_ref, kk_ref, ka_ref,
             r_ref, lw_ref, k2_ref, v_ref, aa_ref, bb_ref):
        _, _, s = _shifted(ps_ref, prev_ref, mu_ref[...], pl.program_id(0))
        wa = s[:, 3 * W:]
        wl = w0_ref[...] + _bdot(jnp.tanh(wa), w2_ref[...])
        apre = a0_ref[...] + _bdot(wa, a2_ref[...])
        lw, k2, aa, bb = _prep_elem(s[:, W:2 * W], wl, apre, kk_ref[...], ka_ref[...], _head_blockdiag())
        r_ref[...] = s[:, 0:W]
        v_ref[...] = s[:, 2 * W:3 * W]
        lw_ref[...] = lw
        k2_ref[...] = k2
        aa_ref[...] = aa
        bb_ref[...] = bb

    vec = _const_spec((1, W))
    return pl.pallas_call(
        body, grid=(T // tm,), name="even_prep",
        in_specs=[_row_spec(tm, SHIFT), _prev_spec(SHIFT, lambda i: i), _const_spec((1, SHIFT)), vec,
                  _const_spec((2 * LORA, W)), vec, _const_spec((2 * LORA, W)), vec, vec],
        out_specs=[_row_spec(tm, W)] * 6,
        out_shape=[jax.ShapeDtypeStruct((T, W), F32)] * 6,
        compiler_params=_cparams(("parallel",), VMEM_BIG),
    )(ps, ps, mu, w0, w2x, a0, a2x, kkw, kaw)


def even_prep_bwd(ps, mu, w0, w2x, a0, a2x, kkw, kaw, dr, dlw, dk2, dv, daa, dbb, dr2, dk22, dv2):
    tm = PREP_TM
    nb = T // tm
    rev = lambda i: nb - 1 - i

    def body(ps_ref, prev_ref, mu_ref, w0_ref, w2_ref, a0_ref, a2_ref, kk_ref, ka_ref,
             dr_ref, dlw_ref, dk2_ref, dv_ref, daa_ref, dbb_ref, dr2_ref, dk22_ref, dv2_ref,
             dps_ref, dmu_ref, dw0_ref, dw2_ref, da0_ref, da2_ref, dkk_ref, dka_ref, carry):
        i = pl.program_id(0)
        blk = rev(i)
        mu_v = mu_ref[...]
        p, p_prev, s = _shifted(ps_ref, prev_ref, mu_v, blk)
        wa = s[:, 3 * W:]
        th = jnp.tanh(wa)
        wl = w0_ref[...] + _bdot(th, w2_ref[...])
        apre = a0_ref[...] + _bdot(wa, a2_ref[...])
        bd = _head_blockdiag()
        k = s[:, W:2 * W]
        dk, dwl, dap, dkkw, dkaw = _prep_elem_bwd(k, wl, apre, kk_ref[...], ka_ref[...], bd, dlw_ref[...],
                                                  dk2_ref[...] + dk22_ref[...], daa_ref[...], dbb_ref[...])
        dwa = _bdot_nt(dwl, w2_ref[...]) * (1.0 - th * th) + _bdot_nt(dap, a2_ref[...])
        ds = jnp.concatenate([dr_ref[...] + dr2_ref[...], dk, dv_ref[...] + dv2_ref[...], dwa], axis=-1)

        @pl.when(i == 0)
        def _():
            for ref in (dmu_ref, dw0_ref, dw2_ref, da0_ref, da2_ref, dkk_ref, dka_ref, carry):
                ref[...] = jnp.zeros_like(ref)

        dmu_ref[...] += jnp.sum(ds * (p_prev - p), axis=0, keepdims=True)
        dw0_ref[...] += jnp.sum(dwl, axis=0, keepdims=True)
        da0_ref[...] += jnp.sum(dap, axis=0, keepdims=True)
        dw2_ref[...] += _bdot_tn(th, dwl)
        da2_ref[...] += _bdot_tn(wa, dap)
        dkk_ref[...] += dkkw
        dka_ref[...] += dkaw
        dsm = ds * mu_v
        last = (blk % PREP_NB) == PREP_NB - 1
        nxt = jnp.where(last, 0.0, carry[0:1, :])
        up = pltpu.roll(dsm, tm - 1, 0)
        up = jnp.where(_iota2(up.shape, 0) == tm - 1, nxt, up)
        dps_ref[...] = (ds - dsm + up).astype(BF16)
        carry[0:1, :] = dsm[0:1, :]

    vec = _const_spec((1, W))
    rrow = lambda width: pl.BlockSpec((tm, width), lambda i: (rev(i), 0))
    return pl.pallas_call(
        body, grid=(nb,), name="even_prep_bwd",
        in_specs=[rrow(SHIFT), _prev_spec(SHIFT, rev), _const_spec((1, SHIFT)), vec,
                  _const_spec((2 * LORA, W)), vec, _const_spec((2 * LORA, W)), vec, vec] + [rrow(W)] * 9,
        out_specs=[rrow(SHIFT), _const_spec((1, SHIFT)), vec, _const_spec((2 * LORA, W)), vec,
                   _const_spec((2 * LORA, W)), vec, vec],
        out_shape=[jax.ShapeDtypeStruct((T, SHIFT), BF16), jax.ShapeDtypeStruct((1, SHIFT), F32),
                   jax.ShapeDtypeStruct((1, W), F32), jax.ShapeDtypeStruct((2 * LORA, W), F32),
                   jax.ShapeDtypeStruct((1, W), F32), jax.ShapeDtypeStruct((2 * LORA, W), F32),
                   jax.ShapeDtypeStruct((1, W), F32), jax.ShapeDtypeStruct((1, W), F32)],
        scratch_shapes=[pltpu.VMEM((8, SHIFT), F32)],
        compiler_params=_cparams(("arbitrary",), VMEM_BIG),
    )(ps, ps, mu, w0, w2x, a0, a2x, kkw, kaw, dr, dlw, dk2, dv, daa, dbb, dr2, dk22, dv2)


NPAIR = NH // 2
PW = 2 * HD


def _pair_cols(p):
    return slice(p * PW, (p + 1) * PW)


def _pairs(a):
    return [a[:, _pair_cols(p)] for p in range(NPAIR)]


def _stack_pair(a):
    first = _iota2(a.shape, 1) < HD
    zero = jnp.zeros_like(a)
    return jnp.concatenate([jnp.where(first, a, zero), jnp.where(first, zero, a)], axis=0)


def _unstack_pair(a):
    n = a.shape[0] // 2
    return jnp.where(_iota2((n, PW), 1) < HD, a[:n], a[n:])


def _fold_pair(a):
    n = a.shape[0] // 2
    return a[:n] + a[n:]


def _chunk_masks():
    n = 4 * L
    row = _iota2((n, n), 0)
    col = _iota2((n, n), 1)
    same = ((row // L) & 1) == ((col // L) & 1)
    ri = row & (L - 1)
    ci = col & (L - 1)
    keep = same & (((row < 2 * L) & (ri > ci)) | ((row >= 2 * L) & (ri >= ci)))
    r1 = _iota2((L, L), 0)
    c1 = _iota2((L, L), 1)
    r2 = _iota2((2 * L, 2 * L), 0)
    c2 = _iota2((2 * L, 2 * L), 1)
    return keep.astype(F32), (r1 >= c1).astype(F32), (r2 == c2).astype(F32)


def _scaled(r, lw, k2, aa, bb, tri):
    g = _hdot(tri, lw)
    eg = jnp.exp(g)
    eng = jnp.exp(-g)
    egp = jnp.exp(g - lw)
    return eg, eng, egp, aa * egp, r * eg, bb * eng, k2 * eng


def _head_cols(h):
    return slice(h * HD, (h + 1) * HD)


def _per_head(a):
    return [a[:, _head_cols(h)] for h in range(NH)]


def _pairs_operands(at, rt, bt, kt):
    x = [jnp.concatenate([_stack_pair(a), _stack_pair(r)], axis=0).astype(BF16) for a, r in zip(_pairs(at), _pairs(rt))]
    yk = [jnp.concatenate([_stack_pair(b), _stack_pair(k)], axis=0).astype(BF16) for b, k in zip(_pairs(bt), _pairs(kt))]
    return x, yk


def _pairs_matrices(x, yk, keep, eye):
    m = [_bdot_nt(a, b) * keep for a, b in zip(x, yk)]
    p = [a[:2 * L, :2 * L] for a in m]
    tinv = [eye + a for a in p]
    for _ in range(5):
        p = [_bdot(a, a) for a in p]
        tinv = [t + _bdot(t, a) for t, a in zip(tinv, p)]
    return [a.astype(BF16) for a in m], [a.astype(BF16) for a in tinv]


def _pairs_fwd(x, yk, m, tinv, vw, s0, egl):
    xh = [_bdot_nt(a, s) for a, s in zip(x, s0)]
    u = [_bdot(t, h[:2 * L] + _bdot(a[:2 * L, 2 * L:], w)) for t, h, a, w in zip(tinv, xh, m, vw)]
    uv = [jnp.concatenate([a, w], axis=0).astype(BF16) for a, w in zip(u, vw)]
    y = [h[2 * L:] + _bdot(a[2 * L:], w) for h, a, w in zip(xh, m, uv)]
    sn = [e * (s + _bdot_tn(w, b)) for e, s, w, b in zip(egl, s0, uv, yk)]
    return y, sn, uv


def _pairs_bwd(x, yk, m, tinv, uv, s0, sn, egl, dyw, dsn, keep):
    dzs = [d * e for d, e in zip(dsn, egl)]
    dgl = [jnp.sum(d * s, axis=0, keepdims=True) for d, s in zip(dsn, sn)]
    dyb = [a.astype(BF16) for a in dyw]
    t1 = [_bdot_tn(a[2 * L:], d) for a, d in zip(m, dyb)]
    t2 = [_bdot_nt(b, d) for b, d in zip(yk, dzs)]
    drhs = [_bdot_tn(t, a[:2 * L] + b[:2 * L]) for t, a, b in zip(tinv, t1, t2)]
    dv = [a[2 * L:] + b[2 * L:] + _bdot_tn(c[:2 * L, 2 * L:], d) for a, b, c, d in zip(t1, t2, m, drhs)]
    gg = [jnp.concatenate([a, b], axis=0).astype(BF16) for a, b in zip(drhs, dyw)]
    ds0 = [d + _bdot_tn(g, a) for d, g, a in zip(dzs, gg, x)]
    dm = [_bdot_nt(g, w) * keep for g, w in zip(gg, uv)]
    dx = [_bdot(g, s) + _bdot(d, b) for g, s, d, b in zip(gg, s0, dm, yk)]
    dyk = [_bdot_tn(d, a) + _bdot(w, z) for d, a, w, z in zip(dm, x, uv, dzs)]
    return dx, dyk, dv, dgl, ds0


STATE_SHAPE = (NPAIR * PW, PW)
M_SHAPE = (4 * L, NPAIR * 4 * L)
TINV_SHAPE = (2 * L, NPAIR * 2 * L)


def _rows_of(a, n):
    return [a[i * n:(i + 1) * n, :] for i in range(NPAIR)]


def _both(f):
    out = []
    for s in range(NSEQ):
        out += f(s)
    return out


def _seq_view(a):
    return a.reshape(NSEQ, SEQ, a.shape[-1])


UV_SHAPE = (4 * L, NPAIR * PW)
RW_CHUNKS = 2


def rwkv_fwd(r, lw, k2, v, aa, bb):
    def body(r_ref, lw_ref, k2_ref, v_ref, aa_ref, bb_ref, y_ref, hs_ref, hn_ref, m_ref, t_ref, uv_ref, state):
        @pl.when(pl.program_id(0) == 0)
        def _():
            state[...] = jnp.zeros_like(state)

        keep, tri, eye = _chunk_masks()
        where = [(j, s) for j in range(RW_CHUNKS) for s in range(NSEQ)]
        rows = lambda j: slice(j * L, (j + 1) * L)
        sc = [_scaled(r_ref[s, rows(j)], lw_ref[s, rows(j)], k2_ref[s, rows(j)], aa_ref[s, rows(j)],
                      bb_ref[s, rows(j)], tri) for j, s in where]
        ops = [_pairs_operands(*a[3:]) for a in sc]
        m, tinv = _pairs_matrices([a for o in ops for a in o[0]], [a for o in ops for a in o[1]], keep, eye)
        s_cur = [state[s] for s in range(NSEQ)]
        for j in range(RW_CHUNKS):
            mine = slice(j * NSEQ * NPAIR, (j + 1) * NSEQ * NPAIR)
            x = [a for o in ops[j * NSEQ:(j + 1) * NSEQ] for a in o[0]]
            yk = [a for o in ops[j * NSEQ:(j + 1) * NSEQ] for a in o[1]]
            vw = _both(lambda s: [_stack_pair(a) for a in _pairs(v_ref[s, rows(j)])])
            egl = _both(lambda s: _pairs(sc[j * NSEQ + s][0][L - 1:L, :]))
            y, sn, uv = _pairs_fwd(x, yk, m[mine], tinv[mine], vw, _both(lambda s: _rows_of(s_cur[s], PW)), egl)
            for s in range(NSEQ):
                ps = slice(s * NPAIR, (s + 1) * NPAIR)
                hs_ref[j, s] = s_cur[s]
                y_ref[s, rows(j)] = jnp.concatenate([_fold_pair(a) for a in y[ps]], axis=-1)
                m_ref[j, s] = jnp.concatenate(m[mine][ps], axis=-1)
                t_ref[j, s] = jnp.concatenate(tinv[mine][ps], axis=-1)
                uv_ref[j, s] = jnp.concatenate(uv[ps], axis=-1)
                s_cur[s] = jnp.concatenate(sn[ps], axis=0)
                hn_ref[j, s] = s_cur[s]
        for s in range(NSEQ):
            state[s] = s_cur[s]

    blk = pl.BlockSpec((NSEQ, RW_CHUNKS * L, W), lambda c: (0, c, 0))
    per_chunk = lambda shape: pl.BlockSpec((RW_CHUNKS, NSEQ) + shape, lambda c: (c, 0, 0, 0))
    saved_shapes = [(STATE_SHAPE, F32), (STATE_SHAPE, F32), (M_SHAPE, BF16), (TINV_SHAPE, BF16), (UV_SHAPE, BF16)]
    y, *saved = pl.pallas_call(
        body, grid=(NC // RW_CHUNKS,), name="rwkv_fwd",
        in_specs=[blk] * 6,
        out_specs=[blk] + [per_chunk(shape) for shape, _ in saved_shapes],
        out_shape=[jax.ShapeDtypeStruct((NSEQ, SEQ, W), F32)]
        + [jax.ShapeDtypeStruct((NC, NSEQ) + shape, dt) for shape, dt in saved_shapes],
        scratch_shapes=[pltpu.VMEM((NSEQ,) + STATE_SHAPE, F32)],
        compiler_params=_cparams(("arbitrary",), VMEM_BIG),
    )(*[_seq_view(a) for a in (r, lw, k2, v, aa, bb)])
    return y.reshape(T, W), saved


def rwkv_bwd(r, lw, k2, aa, bb, saved, dy):
    def body(r_ref, lw_ref, k2_ref, aa_ref, bb_ref, hs_ref, hn_ref, m_ref, t_ref, uv_ref, dy_ref,
             dr_ref, dlw_ref, dk2_ref, dv_ref, daa_ref, dbb_ref, dstate):
        @pl.when(pl.program_id(0) == 0)
        def _():
            dstate[...] = jnp.zeros_like(dstate)

        keep, tri, _ = _chunk_masks()
        sc = [_scaled(r_ref[s], lw_ref[s], k2_ref[s], aa_ref[s], bb_ref[s], tri) for s in range(NSEQ)]
        ops = [_pairs_operands(*sc[s][3:]) for s in range(NSEQ)]
        x, yk = _both(lambda s: ops[s][0]), _both(lambda s: ops[s][1])
        m = _both(lambda s: [m_ref[0, s][:, i * 4 * L:(i + 1) * 4 * L] for i in range(NPAIR)])
        tinv = _both(lambda s: [t_ref[0, s][:, i * 2 * L:(i + 1) * 2 * L] for i in range(NPAIR)])
        uv = _both(lambda s: _pairs(uv_ref[0, s]))
        dyw = _both(lambda s: [_stack_pair(a) for a in _pairs(dy_ref[s])])
        s0 = _both(lambda s: _rows_of(hs_ref[0, s], PW))
        sn = _both(lambda s: _rows_of(hn_ref[0, s], PW))
        dsn = _both(lambda s: _rows_of(dstate[s], PW))
        egl = _both(lambda s: _pairs(sc[s][0][L - 1:L, :]))
        dx, dyk, dvw, dgl, ds0 = _pairs_bwd(x, yk, m, tinv, uv, s0, sn, egl, dyw, dsn, keep)
        for s in range(NSEQ):
            mine = slice(s * NPAIR, (s + 1) * NPAIR)
            eg, eng, egp, at, rt, bt, kt = sc[s]
            dstate[s] = jnp.concatenate(ds0[mine], axis=0)
            dv_ref[s] = jnp.concatenate([_fold_pair(a) for a in dvw[mine]], axis=-1)
            dat = jnp.concatenate([_fold_pair(a[:2 * L]) for a in dx[mine]], axis=-1)
            drt = jnp.concatenate([_fold_pair(a[2 * L:]) for a in dx[mine]], axis=-1)
            dbt = jnp.concatenate([_fold_pair(a[:2 * L]) for a in dyk[mine]], axis=-1)
            dkt = jnp.concatenate([_fold_pair(a[2 * L:]) for a in dyk[mine]], axis=-1)
            dg = drt * rt - dbt * bt - dkt * kt
            dg = dg + jnp.where(_iota2(dg.shape, 0) == L - 1, jnp.concatenate(dgl[mine], axis=-1), 0.0)
            dgp = dat * at
            dlw_ref[s] = _hdot_tn(tri, dg + dgp) - dgp
            dr_ref[s] = drt * eg
            daa_ref[s] = dat * egp
            dbb_ref[s] = dbt * eng
            dk2_ref[s] = dkt * eng

    blk = pl.BlockSpec((NSEQ, L, W), lambda c: (0, NC - 1 - c, 0))
    per_chunk = lambda shape: pl.BlockSpec((1, NSEQ) + shape, lambda c: (NC - 1 - c, 0, 0, 0))
    outs = pl.pallas_call(
        body, grid=(NC,), name="rwkv_bwd",
        in_specs=[blk] * 5 + [per_chunk(a.shape[2:]) for a in saved] + [blk],
        out_specs=[blk] * 6,
        out_shape=[jax.ShapeDtypeStruct((NSEQ, SEQ, W), F32)] * 6,
        scratch_shapes=[pltpu.VMEM((NSEQ,) + STATE_SHAPE, F32)],
        compiler_params=_cparams(("arbitrary",)),
    )(*[_seq_view(a) for a in (r, lw, k2, aa, bb)], *saved, _seq_view(dy))
    return [a.reshape(T, W) for a in outs]


def _post_math(y, r, k2, v, ga, o, gb, lng, lnb, rk, bd):
    mu = _headsum(y, bd) * (1.0 / HD)
    yc = y - mu
    var = _headsum(yc * yc, bd) * (1.0 / HD)
    yn = yc * lax.rsqrt(var + GN_EPS) * lng + lnb
    bonus = _headsum(r * k2 * rk, bd) * v
    return (yn + bonus) * _silu(ga), o * _silu(gb)


def even_post(y, r, k2, v, ga, o, gb, lng, lnb, rk, h, w_bf):
    tm = 512

    def body(y_ref, r_ref, k2_ref, v_ref, ga_ref, o_ref, gb_ref, lng_ref, lnb_ref, rk_ref, h_ref, w_ref,
             ho_ref, zt_ref):
        ya, yb = _post_math(y_ref[...], r_ref[...], k2_ref[...], v_ref[...], ga_ref[...], o_ref[...], gb_ref[...],
                            lng_ref[...], lnb_ref[...], rk_ref[...], _head_blockdiag())
        z = jnp.concatenate([ya.astype(BF16), yb.astype(BF16)], axis=-1)
        zt_ref[...] = z.T
        ho_ref[...] = h_ref[...] + jnp.dot(z, w_ref[...], preferred_element_type=F32)

    vec = _const_spec((1, W))
    return pl.pallas_call(
        body, grid=(T // tm,), name="even_post",
        in_specs=[_row_spec(tm, W)] * 7 + [vec] * 3 + [_row_spec(tm, D), _const_spec((D, D))],
        out_specs=[_row_spec(tm, D), _col_spec(D, tm)],
        out_shape=[jax.ShapeDtypeStruct((T, D), F32), jax.ShapeDtypeStruct((D, T), BF16)],
        compiler_params=_cparams(("parallel",), VMEM_BIG),
    )(y, r, k2, v, ga, o, gb, lng, lnb, rk, h, w_bf)


def even_post_bwd(y, r, k2, v, ga, o, gb, lng, lnb, rk, dh, zt_bf, w_bf, after=None):
    tm = 512
    extra_specs, extra = _after_operand(after)

    def body(y_ref, r_ref, k2_ref, v_ref, ga_ref, o_ref, gb_ref, lng_ref, lnb_ref, rk_ref, dh_ref, zt_ref, w_ref,
             *rest):
        (dy_ref, dr_ref, dk2_ref, dv_ref, dga_ref, do_ref, dgb_ref, dlng_ref, dlnb_ref, drk_ref, dw_ref,
         acc_ref) = rest[-12:]
        dzv = _out_proj_back(dh_ref, zt_ref, w_ref, dw_ref, acc_ref)
        bd = _head_blockdiag()
        _, vjp = jax.vjp(lambda *a: _post_math(*a, bd), y_ref[...], r_ref[...], k2_ref[...], v_ref[...], ga_ref[...],
                         o_ref[...], gb_ref[...], lng_ref[...], lnb_ref[...], rk_ref[...])
        dy, dr, dk2, dv, dga, do, dgb, dlng, dlnb, drk = vjp((dzv[:, 0:W], dzv[:, W:2 * W]))
        for ref, val in ((dy_ref, dy), (dr_ref, dr), (dk2_ref, dk2), (dv_ref, dv), (dga_ref, dga), (do_ref, do),
                         (dgb_ref, dgb)):
            ref[...] = val.astype(ref.dtype)

        @pl.when(pl.program_id(0) == 0)
        def _():
            for ref in (dlng_ref, dlnb_ref, drk_ref):
                ref[...] = jnp.zeros_like(ref)

        dlng_ref[...] += dlng
        dlnb_ref[...] += dlnb
        drk_ref[...] += drk

    vec = _const_spec((1, W))
    return pl.pallas_call(
        body, grid=(T // tm,), name="even_post_bwd",
        in_specs=[_row_spec(tm, W)] * 7 + [vec] * 3 + [_row_spec(tm, D), _col_spec(D, tm), _const_spec((D, D))]
        + extra_specs,
        out_specs=[_row_spec(tm, W)] * 7 + [vec] * 3 + [_const_spec((D, D))],
        out_shape=[jax.ShapeDtypeStruct((T, W), dt) for dt in (F32, F32, F32, F32, BF16, F32, BF16)]
        + [jax.ShapeDtypeStruct((1, W), F32)] * 3 + [jax.ShapeDtypeStruct((D, D), BF16)],
        scratch_shapes=[pltpu.VMEM((D, D), F32)],
        compiler_params=_cparams(("arbitrary",), VMEM_BIG),
    )(y, r, k2, v, ga, o, gb, lng, lnb, rk, dh, zt_bf, w_bf, *extra)


PADSEQ = SEQ + LEFT * L
ATT_SCALE = 1.0 / math.sqrt(HD)
ATT_Q = 4
WIN = BAND + (ATT_Q - 1) * L
ATT_STEPS = NC // ATT_Q
ATT_BIAS_SHAPE = (NPAIR, ATT_Q * 2 * L, WIN)
ATT_WINDOW_BIAS_SHAPE = (ATT_Q, NPAIR, 2 * L, WIN)


def _stack_chunks(a):
    return jnp.concatenate([_stack_pair(a[i * L:(i + 1) * L]) for i in range(ATT_Q)], axis=0)


def _unstack_chunks(a):
    return jnp.concatenate([_unstack_pair(a[i * 2 * L:(i + 1) * 2 * L]) for i in range(ATT_Q)], axis=0)


def _window_bias(b_ref):
    return [jnp.concatenate([b_ref[c, p] for c in range(ATT_Q)], axis=0) for p in range(NPAIR)]


def _key_window(ref, step):
    start = step * (ATT_Q * L) - LEFT * L
    rows = ref[pl.ds(pl.multiple_of(jnp.maximum(start, 0), L), WIN), :]
    window = rows
    for lead in range(ATT_Q * L, LEFT * L + 1, ATT_Q * L):
        moved = jnp.concatenate([rows[WIN - lead:], rows[:WIN - lead]], axis=0)
        window = jnp.where(start == -lead, moved, window)
    return window


def _att_probs(q2, kw, bias, step):
    valid = _iota2((1, WIN), 1) >= (LEFT - step * ATT_Q) * L
    s = [jnp.where(valid, _bdot_nt(a, b) * ATT_SCALE + bias[p], NEG) for p, (a, b) in enumerate(zip(q2, kw))]
    e = [jnp.exp(a - jnp.max(a, axis=-1, keepdims=True)) for a in s]
    return [a / jnp.sum(a, axis=-1, keepdims=True) for a in e]


def attention_fwd(q, k, v, bias):
    def body(q_ref, k_ref, v_ref, b_ref, o_ref):
        step = pl.program_id(1)
        kw = _pairs(_key_window(k_ref, step))
        vw = _pairs(_key_window(v_ref, step))
        q2 = [_stack_chunks(a) for a in _pairs(q_ref[...])]
        p = _att_probs(q2, kw, _window_bias(b_ref), step)
        o_ref[...] = jnp.concatenate([_unstack_chunks(_bdot(a, b)) for a, b in zip(p, vw)], axis=-1)

    qblk = pl.BlockSpec((ATT_Q * L, W), lambda b, c: (b * ATT_STEPS + c, 0))
    kblk = pl.BlockSpec((SEQ, W), lambda b, c: (b, 0))
    return pl.pallas_call(
        body, grid=(NSEQ, ATT_STEPS), name="attention_fwd",
        in_specs=[qblk, kblk, kblk, _const_spec(ATT_WINDOW_BIAS_SHAPE)],
        out_specs=qblk, out_shape=jax.ShapeDtypeStruct((T, W), F32),
        compiler_params=_cparams(("parallel", "arbitrary")),
    )(q, k, v, bias)


def attention_bwd(q, k, v, bias, do):
    def body(q_ref, k_ref, v_ref, b_ref, do_ref, dq_ref, dko_ref, dvo_ref, db_ref, dk_ref, dv_ref):
        b = pl.program_id(0)
        c = pl.program_id(1)

        @pl.when(c == 0)
        def _():
            dk_ref[...] = jnp.zeros_like(dk_ref)
            dv_ref[...] = jnp.zeros_like(dv_ref)

        @pl.when((c == 0) & (b == 0))
        def _():
            db_ref[...] = jnp.zeros_like(db_ref)

        start = pl.multiple_of(c * (ATT_Q * L), L)
        kw = _pairs(_key_window(k_ref, c))
        vw = _pairs(_key_window(v_ref, c))
        q2 = [_stack_chunks(a) for a in _pairs(q_ref[...])]
        do2 = [_stack_chunks(a) for a in _pairs(do_ref[...].astype(BF16))]
        p = _att_probs(q2, kw, _window_bias(b_ref), c)
        dp = [_bdot_nt(a, b) for a, b in zip(do2, vw)]
        ds = [a * (d - jnp.sum(d * a, axis=-1, keepdims=True)) for a, d in zip(p, dp)]
        dss = [(a * ATT_SCALE).astype(BF16) for a in ds]
        dq_ref[...] = jnp.concatenate([_unstack_chunks(_bdot(a, b)) for a, b in zip(dss, kw)], axis=-1).astype(BF16)
        dk_ref[pl.ds(start, WIN), :] += jnp.concatenate([_bdot_tn(a, b) for a, b in zip(dss, q2)], axis=-1)
        dv_ref[pl.ds(start, WIN), :] += jnp.concatenate([_bdot_tn(a, b) for a, b in zip(p, do2)], axis=-1)
        for i in range(NPAIR):
            db_ref[i] += ds[i]

        @pl.when(c == ATT_STEPS - 1)
        def _():
            dko_ref[...] = dk_ref[LEFT * L:, :].astype(BF16)
            dvo_ref[...] = dv_ref[LEFT * L:, :].astype(BF16)

    qblk = pl.BlockSpec((ATT_Q * L, W), lambda b, c: (b * ATT_STEPS + c, 0))
    sblk = pl.BlockSpec((SEQ, W), lambda b, c: (b, 0))
    bblk = _const_spec(ATT_BIAS_SHAPE)
    return pl.pallas_call(
        body, grid=(NSEQ, ATT_STEPS), name="attention_bwd",
        in_specs=[qblk, sblk, sblk, _const_spec(ATT_WINDOW_BIAS_SHAPE), qblk],
        out_specs=[qblk, sblk, sblk, bblk],
        out_shape=[jax.ShapeDtypeStruct((T, W), BF16), jax.ShapeDtypeStruct((T, W), BF16),
                   jax.ShapeDtypeStruct((T, W), BF16), jax.ShapeDtypeStruct(ATT_BIAS_SHAPE, F32)],
        scratch_shapes=[pltpu.VMEM((PADSEQ, W), F32), pltpu.VMEM((PADSEQ, W), F32)],
        compiler_params=_cparams(("arbitrary", "arbitrary"), VMEM_BIG),
    )(q, k, v, bias, do)


NTAB = 2 * CLIP + 1
EXT = BAND + L


def _ext_onehot():
    n = _iota2((EXT, NTAB), 0)
    m = _iota2((EXT, NTAB), 1)
    return (jnp.clip(BAND - 1 - n, -CLIP, CLIP) + CLIP == m).astype(F32)


def bias_expand(table):
    def body(t_ref, o_ref):
        ext = _hdot_nt(t_ref[...], _ext_onehot())
        ext = jnp.concatenate([ext, jnp.zeros((NH, WIN - EXT), F32)], axis=-1)
        col = _iota2((NH, WIN), 1)
        for c in range(ATT_Q):
            inside = (col >= c * L) & (col < c * L + BAND)
            for i in range(L):
                shift = (c * L - (L - 1 - i)) % WIN
                o_ref[c, :, i, :] = jnp.where(inside, pltpu.roll(ext, shift, 1) if shift else ext, NEG)

    out = pl.pallas_call(body, name="bias_expand", out_shape=jax.ShapeDtypeStruct((ATT_Q, NH, L, WIN), F32))(table)
    return out.reshape(ATT_WINDOW_BIAS_SHAPE)


def bias_grad(dbias):
    def body(d_ref, o_ref):
        acc = jnp.zeros((NH, EXT), F32)
        zpad = jnp.zeros((NH, EXT - BAND), F32)
        for i in range(L):
            s = L - 1 - i
            row = jnp.concatenate([d_ref[:, i, :], zpad], axis=-1)
            acc = acc + (pltpu.roll(row, s, 1) if s else row)
        o_ref[...] = _hdot(acc, _ext_onehot())

    return pl.pallas_call(body, name="bias_grad", out_shape=jax.ShapeDtypeStruct((NH, NTAB), F32))(dbias)


def _group_cols(g):
    return slice(g * SGC, (g + 1) * SGC)


def _sg_norm(gv, lng, lnb):
    gc = gv - jnp.mean(gv, axis=-1, keepdims=True)
    rstd = lax.rsqrt(jnp.mean(gc * gc, axis=-1, keepdims=True) + LN_EPS)
    xhat = gc * rstd
    return xhat, rstd, xhat * lng + lnb


GMLP_BWD_CHUNKS = 2


def gmlp_fwd_loss(u, v, gate, lng, lnb, wm_bf, sgb_t, h, w_bf, g_final, target):
    tm = GMLP_BWD_CHUNKS * SGC

    def body(u_ref, v_ref, gt_ref, lng_ref, lnb_ref, wm_ref, sb_ref, h_ref, w_ref, g_ref, t_ref,
             dh_ref, loss_ref, dg_ref, zt_ref):
        zs = []
        for ch in range(GMLP_BWD_CHUNKS):
            rows = slice(ch * SGC, (ch + 1) * SGC)
            _, _, vln = _sg_norm(_gelu(v_ref[rows, :]), lng_ref[...], lnb_ref[...])
            vlb = vln.astype(BF16)
            zg = []
            for g in range(NG):
                cs = _group_cols(g)
                sv = jnp.dot(wm_ref[g], vlb[:, cs], preferred_element_type=F32) + sb_ref[:, g:g + 1]
                zg.append((_gelu(u_ref[rows, cs]) * sv * _silu(gt_ref[rows, cs])).astype(BF16))
            zs.append(jnp.concatenate(zg, axis=-1))
        z = jnp.concatenate(zs, axis=0)
        zt_ref[...] = z.T
        xv = h_ref[...] + jnp.dot(z, w_ref[...], preferred_element_type=F32)
        rstd = lax.rsqrt(jnp.mean(xv * xv, axis=-1, keepdims=True) + RMS_EPS)
        xhat = xv * rstd
        err = xhat * g_ref[...] - t_ref[...]
        part = 0.5 * jnp.sum(jnp.mean(err * err, axis=-1, keepdims=True), axis=0, keepdims=True)
        dout = err * (1.0 / D)

        @pl.when(pl.program_id(0) == 0)
        def _():
            loss_ref[...] = jnp.zeros_like(loss_ref)
            dg_ref[...] = jnp.zeros_like(dg_ref)

        loss_ref[...] += jnp.broadcast_to(part, loss_ref.shape)
        dg_ref[...] += jnp.sum(dout * xhat, axis=0, keepdims=True)
        dxh = dout * g_ref[...]
        dh_ref[...] = rstd * (dxh - xhat * jnp.mean(dxh * xhat, axis=-1, keepdims=True))

    return pl.pallas_call(
        body, grid=(T // tm,), name="gmlp_fwd_loss",
        in_specs=[_row_spec(tm, D)] * 3 + [_const_spec((1, D))] * 2
        + [_const_spec((NG, SGC, SGC)), _const_spec((SGC, NG)), _row_spec(tm, D), _const_spec((D, D)),
           _const_spec((1, D)), _row_spec(tm, D)],
        out_specs=[_row_spec(tm, D), _const_spec((8, 128)), _const_spec((1, D)), _col_spec(D, tm)],
        out_shape=[jax.ShapeDtypeStruct((T, D), F32), jax.ShapeDtypeStruct((8, 128), F32),
                   jax.ShapeDtypeStruct((1, D), F32), jax.ShapeDtypeStruct((D, T), BF16)],
        compiler_params=_cparams(("arbitrary",), VMEM_BIG),
    )(u, v, gate, lng, lnb, wm_bf, sgb_t, h, w_bf, g_final, target)


def gmlp_bwd(u, v, gate, lng, lnb, wm_bf, sgb_t, dh, zt_bf, w_bf):
    def body(u_ref, v_ref, gt_ref, lng_ref, lnb_ref, wm_ref, sb_ref, dh_ref, zt_ref, w_ref,
             du_ref, dv_ref, dgt_ref, dlng_ref, dlnb_ref, dwm_ref, dsb_ref, dw_ref, acc_ref):
        @pl.when(pl.program_id(0) == 0)
        def _():
            for ref in (dlng_ref, dlnb_ref, dwm_ref, dsb_ref):
                ref[...] = jnp.zeros_like(ref)

        dz = _out_proj_back(dh_ref, zt_ref, w_ref, dw_ref, acc_ref)
        sel = (_iota2((D, NG), 0) // SGC == _iota2((D, NG), 1)).astype(F32)
        for ch in range(GMLP_BWD_CHUNKS):
            rows = slice(ch * SGC, (ch + 1) * SGC)
            gv, dgv_dv = _gelu_both(v_ref[rows, :])
            xhat, rstd, vln = _sg_norm(gv, lng_ref[...], lnb_ref[...])
            vlb = vln.astype(BF16)
            dvln = []
            dsv_all = []
            for g in range(NG):
                cs = _group_cols(g)
                uu = u_ref[rows, cs]
                gg = gt_ref[rows, cs]
                dzz = dz[rows, cs]
                sv = jnp.dot(wm_ref[g], vlb[:, cs], preferred_element_type=F32) + sb_ref[:, g:g + 1]
                gu, dgu = _gelu_both(uu)
                sg, dsg = _silu_both(gg)
                dzgu = dzz * gu
                dsv = dzgu * sg
                dgt_ref[rows, cs] = (dzgu * sv * dsg).astype(BF16)
                du_ref[rows, cs] = (dzz * sv * sg * dgu).astype(BF16)
                dsb16 = dsv.astype(BF16)
                dvln.append(lax.dot_general(wm_ref[g], dsb16, (((0,), (0,)), ((), ())), preferred_element_type=F32))
                dwm_ref[g] += lax.dot_general(dsb16, vlb[:, cs], (((1,), (1,)), ((), ())),
                                              preferred_element_type=F32)
                dsv_all.append(dsv)
            dvl = jnp.concatenate(dvln, axis=-1)
            dsb_ref[...] += _hdot(jnp.concatenate(dsv_all, axis=-1), sel)
            dlng_ref[...] += jnp.sum(dvl * xhat, axis=0, keepdims=True)
            dlnb_ref[...] += jnp.sum(dvl, axis=0, keepdims=True)
            dxh = dvl * lng_ref[...]
            dgv = rstd * (dxh - jnp.mean(dxh, axis=-1, keepdims=True)
                          - xhat * jnp.mean(dxh * xhat, axis=-1, keepdims=True))
            dv_ref[rows, :] = (dgv * dgv_dv).astype(BF16)

    tm = GMLP_BWD_CHUNKS * SGC
    return pl.pallas_call(
        body, grid=(T // tm,), name="gmlp_bwd",
        in_specs=[_row_spec(tm, D)] * 3 + [_const_spec((1, D))] * 2
        + [_const_spec((NG, SGC, SGC)), _const_spec((SGC, NG)), _row_spec(tm, D), _col_spec(D, tm),
           _const_spec((D, D))],
        out_specs=[_row_spec(tm, D)] * 3 + [_const_spec((1, D))] * 2
        + [_const_spec((NG, SGC, SGC)), _const_spec((SGC, NG)), _const_spec((D, D))],
        out_shape=[jax.ShapeDtypeStruct((T, D), BF16)] * 3 + [jax.ShapeDtypeStruct((1, D), F32)] * 2
        + [jax.ShapeDtypeStruct((NG, SGC, SGC), F32), jax.ShapeDtypeStruct((SGC, NG), F32),
           jax.ShapeDtypeStruct((D, D), BF16)],
        scratch_shapes=[pltpu.VMEM((D, D), F32)],
        compiler_params=_cparams(("arbitrary",), VMEM_BIG),
    )(u, v, gate, lng, lnb, wm_bf, sgb_t, dh, zt_bf, w_bf)


NCHIP = 4
NDEV = 8
ANY = pl.BlockSpec(memory_space=pl.ANY)


HBM = pl.BlockSpec(memory_space=pltpu.HBM)
SEM = pl.BlockSpec(memory_space=pltpu.SEMAPHORE)
EFFECT = pltpu.SideEffectType.DATAFLOW_SIDE_EFFECTING


CHIPS, EVERY, SIBLING = "chips", "every", "sibling"
SLOTS = {CHIPS: NCHIP, EVERY: NDEV, SIBLING: 1}


def _peers(scope):
    x, y, c = lax.axis_index("x"), lax.axis_index("y"), lax.axis_index("c")
    if scope == SIBLING:
        return [((x, y, 1 - c), 0)], 0
    if scope == CHIPS:
        return [((px, py, c), 2 * px + py) for px, py in ((1 - x, y), (x, 1 - y), (1 - x, 1 - y))], 2 * x + y
    out = []
    for j in range(1, NDEV):
        px, py, pc = x ^ (j >> 2), y ^ ((j >> 1) & 1), c ^ (j & 1)
        out.append(((px, py, pc), 4 * px + 2 * py + pc))
    return out, 4 * x + 2 * y + c


def _send_copies(src, land, send, recv, scatter, scope, starting):
    peers, me = _peers(scope)
    copies = []
    for t in range(len(src)):
        for j, (dev, slot) in enumerate(peers):
            k = t * len(peers) + j
            copies.append(pltpu.make_async_remote_copy(
                src_ref=src[t].at[slot] if scatter else src[t], dst_ref=land[t].at[me if starting else slot],
                send_sem=send.at[k], recv_sem=recv.at[k], device_id=dev, device_id_type=MESH))
    return copies


def _own_copies(src, land, sems, scatter, scope):
    if scope == SIBLING:
        return []
    _, me = _peers(scope)
    return [pltpu.make_async_copy(src[t].at[me] if scatter else src[t], land[t].at[me], sems.at[t])
            for t in range(len(src))]


def send_start(srcs, scatter, scope, name, after=None):
    n = len(srcs)
    slots = SLOTS[scope]
    extra_specs, extra = _after_operand(after)
    lands = [pltpu.HBM(a.shape if scatter else (slots,) + a.shape, a.dtype) for a in srcs]
    sems = [pltpu.SemaphoreType.DMA((n * max(slots - 1, 1),))] * 2 + ([] if scope == SIBLING else
                                                                     [pltpu.SemaphoreType.DMA((n,))])
    k = len(sems)

    def body(*refs):
        first_out = n + len(extra)
        src, land = refs[:n], refs[first_out + k + n:first_out + k + 2 * n]
        for cp in _send_copies(src, land, refs[first_out], refs[first_out + 1], scatter, scope, True):
            cp.start()
        for cp in _own_copies(src, land, refs[first_out + k - 1], scatter, scope):
            cp.start()
        refs[-1][...] = jnp.zeros_like(refs[-1])

    out = pl.pallas_call(
        body, name=name,
        out_shape=(*sems, *[pltpu.HBM(a.shape, a.dtype) for a in srcs], *lands, jax.ShapeDtypeStruct((8, 128), F32)),
        in_specs=[HBM] * n + extra_specs,
        out_specs=(*[SEM] * k, *[HBM] * (2 * n), pl.BlockSpec(memory_space=pltpu.VMEM)),
        input_output_aliases={i: k + i for i in range(n)},
        compiler_params=pltpu.CompilerParams(has_side_effects=EFFECT),
    )(*[pltpu.with_memory_space_constraint(a, pltpu.HBM) for a in srcs], *extra)
    return list(out[:k]), list(out[k:k + n]), list(out[k + n:k + 2 * n]), out[-1]


def send_wait(started, after, scatter, scope, name, with_sources=False, only=None):
    sems, srcs, lands, _ = started
    n, k = len(srcs), len(sems)
    wanted = range(n) if only is None else only

    def body(*refs):
        src, land = refs[:n], refs[n:2 * n]
        for t, cp in enumerate(_own_copies(src, land, refs[2 * n + k - 1], scatter, scope)):
            if t in wanted:
                cp.wait()
        copies = _send_copies(src, land, refs[2 * n], refs[2 * n + 1], scatter, scope, False)
        for i, cp in enumerate(copies):
            if i // (len(copies) // n) in wanted:
                cp.wait_send()
                cp.wait_recv()

    arrs = list(srcs) + list(lands)
    out = pl.pallas_call(
        body, name=name, out_shape=tuple(pltpu.HBM(a.shape, a.dtype) for a in arrs),
        in_specs=[HBM] * (2 * n) + [SEM] * k + [ANY], out_specs=tuple([HBM] * (2 * n)),
        input_output_aliases={i: i for i in range(2 * n)},
        compiler_params=pltpu.CompilerParams(has_side_effects=EFFECT),
    )(*arrs, *sems, after)
    return (list(out[:n]), list(out[n:])) if with_sources else list(out[n:])


def gather_weights(arrs, split):
    n = len(arrs)

    def body(*refs):
        ins, outs = refs[:n], refs[n:2 * n]
        send1, recv1, send2, recv2, loc_in, loc_out = refs[2 * n:2 * n + 6]
        staged = refs[2 * n + 6:]
        x, y, c = lax.axis_index("x"), lax.axis_index("y"), lax.axis_index("c")
        me = 2 * x + y
        sibling = (x, y, 1 - c)
        peers = [(1 - x, y), (x, 1 - y), (1 - x, 1 - y)]

        def rows_of(t, core):
            half = arrs[t].shape[0] // 2
            return pl.ds(core * half, half)

        def part(ref, t, core):
            return ref.at[rows_of(t, core)] if split[t] else ref

        load = [pltpu.make_async_copy(ins[t], staged[t], loc_in.at[t]) for t in range(n)]
        store = [pltpu.make_async_copy(staged[t], outs[t].at[me], loc_out.at[t]) for t in range(n)]
        for cp in load:
            cp.start()
        first = []
        for t in range(n):
            for j, (px, py) in enumerate(peers):
                first.append(pltpu.make_async_remote_copy(
                    src_ref=part(ins[t], t, c), dst_ref=part(outs[t].at[me], t, c), send_sem=send1.at[t, j],
                    recv_sem=recv1.at[t, j], device_id=(px, py, c), device_id_type=MESH))
        for cp in first:
            cp.start()
        for cp_in, cp_out in zip(load, store):
            cp_in.wait()
            cp_out.start()
        passed = []
        for t in range(n):
            for j, (px, py) in enumerate(peers):
                landed = part(outs[t].at[2 * px + py], t, c)
                pltpu.make_async_remote_copy(
                    src_ref=landed, dst_ref=landed, send_sem=send1.at[t, j], recv_sem=recv1.at[t, j],
                    device_id=(x, y, c), device_id_type=MESH).wait_recv()
                if split[t]:
                    cp = pltpu.make_async_remote_copy(
                        src_ref=landed, dst_ref=landed, send_sem=send2.at[t, j], recv_sem=recv2.at[t, j],
                        device_id=sibling, device_id_type=MESH)
                    cp.start()
                    passed.append(cp)
        for t in range(n):
            for j, (px, py) in enumerate(peers):
                if split[t]:
                    other = part(outs[t].at[2 * px + py], t, 1 - c)
                    pltpu.make_async_remote_copy(
                        src_ref=other, dst_ref=other, send_sem=send2.at[t, j], recv_sem=recv2.at[t, j],
                        device_id=(x, y, c), device_id_type=MESH).wait_recv()
        for cp in first + passed:
            cp.wait_send()
        for cp in store:
            cp.wait()

    return pl.pallas_call(
        body, name="gather_weights", in_specs=[ANY] * n, out_specs=[ANY] * n,
        out_shape=[jax.ShapeDtypeStruct((NCHIP,) + a.shape, a.dtype) for a in arrs],
        scratch_shapes=[pltpu.SemaphoreType.DMA((n, 3))] * 4 + [pltpu.SemaphoreType.DMA((n,))] * 2
        + [pltpu.VMEM(a.shape, a.dtype) for a in arrs],
    )(*arrs)


def _adam_math(g, w, m, v):
    m = ADAM_B1 * m + (1.0 - ADAM_B1) * g
    v = ADAM_B2 * v + (1.0 - ADAM_B2) * (g * g)
    m_hat = m / (1.0 - ADAM_B1 ** ADAM_STEP)
    v_hat = v / (1.0 - ADAM_B2 ** ADAM_STEP)
    delta = -ADAM_LR * (m_hat / (jnp.sqrt(v_hat) + ADAM_EPS) + ADAM_WD * w)
    return delta, m, v


def _rows_tile(rows):
    return rows // 4


def sum_chips(parts, name):
    _, rows, cols = parts.shape
    tr = _rows_tile(rows)

    def body(p_ref, o_ref):
        acc = p_ref[0].astype(F32)
        for s in range(1, NCHIP):
            acc = acc + p_ref[s].astype(F32)
        o_ref[...] = acc

    return pl.pallas_call(
        body, grid=(rows // tr,), name=name,
        in_specs=[pl.BlockSpec((NCHIP, tr, cols), lambda i: (0, i, 0))],
        out_specs=pl.BlockSpec((tr, cols), lambda i: (i, 0)),
        out_shape=jax.ShapeDtypeStruct((rows, cols), F32),
        compiler_params=_cparams(("parallel",)),
    )(parts)


def sum_chips_small(parts, name):
    n = len(parts)

    def body(*refs):
        for p_ref, o_ref in zip(refs[:n], refs[n:]):
            acc = p_ref[0]
            for s in range(1, NCHIP):
                acc = acc + p_ref[s]
            o_ref[...] = acc

    return pl.pallas_call(body, name=name, out_shape=[jax.ShapeDtypeStruct(p.shape[1:], F32) for p in parts])(*parts)


def adam_shard_small(items, name):
    n = len(items)

    def body(*refs):
        for t in range(n):
            a_ref, b_ref, w_ref, m_ref, v_ref = refs[5 * t:5 * t + 5]
            g_ref, d_ref, mo_ref, vo_ref = refs[5 * n + 4 * t:5 * n + 4 * t + 4]
            g = (a_ref[...] + b_ref[...]).reshape(w_ref.shape)
            g_ref[...] = g
            d_ref[...], mo_ref[...], vo_ref[...] = _adam_math(g, w_ref[...], m_ref[...], v_ref[...])

    out = pl.pallas_call(
        body, name=name, out_shape=[jax.ShapeDtypeStruct(it[2].shape, F32) for it in items for _ in range(4)],
    )(*[a for it in items for a in it])
    return [out[4 * t:4 * t + 4] for t in range(n)]


def adam_shard(p_mine, p_sib, w, m, v, name):
    rows, cols = p_mine.shape
    tr = _rows_tile(rows)
    lead = w.ndim == 3

    def body(a_ref, b_ref, w_ref, m_ref, v_ref, g_ref, d_ref, mo_ref, vo_ref):
        g = a_ref[...] + b_ref[...]
        g = g[None] if lead else g
        g_ref[...] = g
        d_ref[...], mo_ref[...], vo_ref[...] = _adam_math(g, w_ref[...], m_ref[...], v_ref[...])

    flat = pl.BlockSpec((tr, cols), lambda i: (i, 0))
    spec = pl.BlockSpec((1, tr, cols), lambda i: (0, i, 0)) if lead else flat
    return pl.pallas_call(
        body, grid=(rows // tr,), name=name, in_specs=[flat] * 2 + [spec] * 3, out_specs=[spec] * 4,
        out_shape=[jax.ShapeDtypeStruct(w.shape, F32)] * 4,
        compiler_params=_cparams(("parallel",)),
    )(p_mine, p_sib, w, m, v)


def adam_shard_halves_t(r_mine, r_sib, wt, mt, vt, name):
    hrows, cols = r_mine.shape
    tr = _rows_tile(hrows)
    per_half = hrows // tr

    def body(a_ref, b_ref, w_ref, m_ref, v_ref, g_ref, d_ref, mo_ref, vo_ref):
        mine = pl.program_id(0) == lax.axis_index("c")
        g = jnp.where(mine, a_ref[...], b_ref[...]).T[None]
        g_ref[...] = g
        d_ref[...], mo_ref[...], vo_ref[...] = _adam_math(g, w_ref[...], m_ref[...], v_ref[...])

    flat = pl.BlockSpec((tr, cols), lambda h, i: (i, 0))
    spec = pl.BlockSpec((1, cols, tr), lambda h, i: (0, 0, h * per_half + i))
    return pl.pallas_call(
        body, grid=(2, per_half), name=name, in_specs=[flat] * 2 + [spec] * 3, out_specs=[spec] * 4,
        out_shape=[jax.ShapeDtypeStruct(wt.shape, F32)] * 4,
        compiler_params=_cparams(("parallel", "parallel")),
    )(r_mine, r_sib, wt, mt, vt)


def adam_replicated(gathered, params, name):
    flat = []
    for i, p in enumerate(params):
        if isinstance(p, list):
            off = 0
            for wmv in p:
                n = gathered[i].shape[-1] - off if wmv[0] is None else wmv[0].shape[-1]
                flat.append((i, (off, n), wmv))
                off += n
        else:
            flat.append((i, None, p))
    ins = [a for _, _, wmv in flat for a in wmv if a is not None]
    ng = len(gathered)

    def body(*refs):
        g_refs = refs[:ng]
        in_refs = list(refs[ng:ng + len(ins)])
        out_refs = list(refs[ng + len(ins):])
        sums = []
        for r in g_refs:
            g = r[0]
            for d in range(1, NDEV):
                g = g + r[d]
            sums.append(g)
        for i, lanes, wmv in flat:
            g = sums[i] if lanes is None else sums[i][:, lanes[0]:lanes[0] + lanes[1]]
            out_refs.pop(0)[...] = g
            if wmv[0] is not None:
                w_ref, m_ref, v_ref = in_refs.pop(0), in_refs.pop(0), in_refs.pop(0)
                d_ref, mo_ref, vo_ref = out_refs.pop(0), out_refs.pop(0), out_refs.pop(0)
                d_ref[...], mo_ref[...], vo_ref[...] = _adam_math(g, w_ref[...], m_ref[...], v_ref[...])

    out_shape = []
    for i, lanes, wmv in flat:
        shape = gathered[i].shape[1:] if lanes is None else (1, lanes[1])
        out_shape += [jax.ShapeDtypeStruct(shape, F32)] * (4 if wmv[0] is not None else 1)
    outs = list(pl.pallas_call(body, name=name, out_shape=out_shape)(*gathered, *ins))
    return [[outs.pop(0) for _ in range(4 if wmv[0] is not None else 1)] for _, _, wmv in flat]


EVEN_SPLITS = (SHIFT, W, W, W, W, W)
ODD_SPLITS = (D, D, D)


def _cols_to_chips(a):
    rows, cols = a.shape
    return a.reshape(rows, NCHIP, cols // NCHIP).transpose(1, 0, 2)


def _chips_to_cols(a):
    _, rows, n = a.shape
    return a.transpose(1, 0, 2).reshape(rows, NCHIP * n)


def kernel(x, norm_g, w_in_e, shift_mu, rw_w0, rw_w2, rw_a0, rw_a2, rw_kk, rw_ka, rw_rk, rw_lnx_g, rw_lnx_b, att_bias, w_out_e, w_in_o, sg_ln_g, sg_ln_b, sg_w, sg_b, w_out_o, final_g, loss_target, m_norm_g, m_w_in_e, m_shift_mu, m_rw_w0, m_rw_w2, m_rw_a0, m_rw_a2, m_rw_kk, m_rw_ka, m_rw_rk, m_rw_lnx_g, m_rw_lnx_b, m_att_bias, m_w_out_e, m_w_in_o, m_sg_ln_g, m_sg_ln_b, m_sg_w, m_sg_b, m_w_out_o, m_final_g, v_norm_g, v_w_in_e, v_shift_mu, v_rw_w0, v_rw_w2, v_rw_a0, v_rw_a2, v_rw_kk, v_rw_ka, v_rw_rk, v_rw_lnx_g, v_rw_lnx_b, v_att_bias, v_w_out_e, v_w_in_o, v_sg_ln_g, v_sg_ln_b, v_sg_w, v_sg_b, v_w_out_o, v_final_g):
    x2 = x.reshape(T, D)
    tgt = loss_target.reshape(T, D)

    gathered = gather_weights(
        [jnp.swapaxes(w_in_e[0], 0, 1).astype(BF16), jnp.concatenate([rw_w2[0], rw_a2[0]], axis=0),
         jnp.concatenate([sg_ln_g, sg_ln_b], axis=0)], [True, True, False])
    wie = gathered[0].reshape(EVEN_IN, D)
    w2 = _chips_to_cols(gathered[1][:, :LORA])
    a2 = _chips_to_cols(gathered[1][:, LORA:])
    sglg = _chips_to_cols(gathered[2][:, 0:1])
    sglb = _chips_to_cols(gathered[2][:, 1:2])

    late = [w_out_e[0].astype(BF16), w_in_o[0].astype(BF16), w_out_o[0].astype(BF16)]
    late_started = send_start(late, False, CHIPS, "late_weights_start", after=gathered[0])

    late_state = {}

    def late_weights(layer, after):
        if layer == "even":
            srcs, lands = send_wait(late_started, after, False, CHIPS, "late_w_out_e_wait", True, only=(0,))
            late_state["rest"] = (late_started[0], srcs, lands, None)
            return lands[0].reshape(D, D)
        woe, wio, woo = send_wait(late_state["rest"], after, False, CHIPS, "late_weights_wait", only=(1, 2))
        return woe.reshape(D, D), wio, woo.reshape(D, D)

    def scatter_start(grads, name):
        return send_start([g_.astype(BF16) if g_.shape[-1] >= W else g_ for g_ in grads], True, CHIPS, name)

    started = {}

    def on_odd_grads(d_woo, d_wio):
        started["odd"] = scatter_start([d_woo.reshape(NCHIP, D // NCHIP, D), d_wio], "odd_grads_start")
        return started["odd"][-1]

    def on_even_grads(big_g):
        d_wie_half, d_woe, _, _, d_w2, d_a2, d_sglg, d_sglb = big_g
        blocks = [d_wie_half, d_woe.reshape(NCHIP, D // NCHIP, D), _cols_to_chips(d_w2), _cols_to_chips(d_a2),
                  _cols_to_chips(d_sglg), _cols_to_chips(d_sglb)]
        started["even"] = scatter_start(blocks, "even_grads_start")
        return started["even"][-1]

    def on_small_grads(layer, grads):
        if layer == "odd":
            d_sg_w, d_sg_b, d_final, d_g1 = grads
            mine = [d_sg_w.reshape(NG * SGC, SGC), d_sg_b, jnp.concatenate([d_final, d_g1], axis=1)]
        else:
            mine = [grads[-2], jnp.concatenate(grads[:-2] + grads[-1:], axis=1)]
        started[layer + "_small"] = send_start(mine, False, EVERY, layer + "_small_grads_start")
        return started[layer + "_small"][-1]

    loss_part, dx, _, _ = _local_step(
        x2, tgt, wie, late_weights, w2, a2, sglg, sglb, norm_g, shift_mu, rw_w0, rw_a0, rw_kk, rw_ka, rw_rk,
        rw_lnx_g, rw_lnx_b, att_bias, sg_w, sg_b, final_g, first_after=late_started[-1], on_odd_grads=on_odd_grads,
        on_even_grads=on_even_grads, on_small_grads=on_small_grads)
    wmv = {"w_in_e": tuple(jnp.swapaxes(a, 1, 2) for a in (w_in_e, m_w_in_e, v_w_in_e)),
           "w_out_e": (w_out_e, m_w_out_e, v_w_out_e),
           "w_in_o": (w_in_o, m_w_in_o, v_w_in_o), "w_out_o": (w_out_o, m_w_out_o, v_w_out_o),
           "rw_w2": (rw_w2, m_rw_w2, v_rw_w2), "rw_a2": (rw_a2, m_rw_a2, v_rw_a2),
           "sg_ln_g": (sg_ln_g, m_sg_ln_g, v_sg_ln_g), "sg_ln_b": (sg_ln_b, m_sg_ln_b, v_sg_ln_b)}
    sharded = {}

    def sum_and_swap(names, landed, tag):
        nbig = sum(p_.dtype == BF16 for p_ in landed)
        partial = [sum_chips(p_, "sum_" + nm) for p_, nm in zip(landed[:nbig], names)]
        if nbig < len(names):
            partial += sum_chips_small(landed[nbig:], "sum_small_" + tag)
        return send_start(partial, False, SIBLING, "swap_partials_" + tag + "_start")

    def update(names, swap_started, after, tag):
        partial, landed = send_wait(swap_started, after, False, SIBLING, "swap_partials_" + tag + "_wait", True)
        from_sibling = [a[0] for a in landed]
        nbig = sum(p_.shape[-1] >= W for p_ in partial)
        for nm, mine, sib in zip(names[:nbig], partial, from_sibling):
            if nm == "w_in_e":
                res = adam_shard_halves_t(mine, sib, *wmv[nm], "adam_" + nm)
                sharded[nm] = [jnp.swapaxes(a, 1, 2) for a in res]
            else:
                sharded[nm] = adam_shard(mine, sib, *wmv[nm], "adam_" + nm)
        if nbig < len(names):
            items = [(mine, sib, *wmv[nm]) for nm, mine, sib in zip(names, partial, from_sibling)][nbig:]
            for nm, res in zip(names[nbig:], adam_shard_small(items, "adam_small_" + tag)):
                sharded[nm] = res

    odd_names = ["w_out_o", "w_in_o"]
    even_names = ["w_in_e", "w_out_e", "rw_w2", "rw_a2", "sg_ln_g", "sg_ln_b"]
    odd_landed = send_wait(started["odd"], started["even_small"][-1], True, CHIPS, "odd_grads_wait")
    odd_swap = sum_and_swap(odd_names, odd_landed, "odd")
    done = odd_swap[-1]

    def wmv_of(*arrs, view=lambda a: a):
        return tuple(view(a) for a in arrs)

    vec = lambda a: a.reshape(1, -1)
    groups = {
        "odd": (["sg_w", "sg_b", "final_g", "norm_g1"],
                [wmv_of(sg_w, m_sg_w, v_sg_w, view=lambda a: a.reshape(NG * SGC, SGC)),
                 wmv_of(sg_b, m_sg_b, v_sg_b, view=lambda a: a[0]),
                 [wmv_of(final_g, m_final_g, v_final_g, view=vec),
                  wmv_of(norm_g, m_norm_g, v_norm_g, view=lambda a: a[1:2])]]),
        "even": (["att_bias", "norm_g0", "shift_mu", "rw_w0", "rw_a0", "rw_kk", "rw_ka", "rw_rk", "rw_lnx_g",
                  "rw_lnx_b", "loss"],
                 [wmv_of(att_bias, m_att_bias, v_att_bias, view=lambda a: a[0]),
                  [wmv_of(norm_g, m_norm_g, v_norm_g, view=lambda a: a[0:1]),
                   wmv_of(shift_mu, m_shift_mu, v_shift_mu), wmv_of(rw_w0, m_rw_w0, v_rw_w0),
                   wmv_of(rw_a0, m_rw_a0, v_rw_a0), wmv_of(rw_kk, m_rw_kk, v_rw_kk), wmv_of(rw_ka, m_rw_ka, v_rw_ka),
                   wmv_of(rw_rk, m_rw_rk, v_rw_rk, view=vec), wmv_of(rw_lnx_g, m_rw_lnx_g, v_rw_lnx_g),
                   wmv_of(rw_lnx_b, m_rw_lnx_b, v_rw_lnx_b), (None, None, None)]]),
    }
    rep = {}
    for layer in ("odd", "even"):
        nms, params = groups[layer]
        gathered_g = send_wait(started[layer + "_small"], done, False, EVERY, layer + "_small_grads_wait")
        for nm, res in zip(nms, adam_replicated(gathered_g, params, "adam_" + layer + "_small")):
            rep[nm] = res
        done = rep[nms[0]][0]
    native = {"sg_w": sg_w.shape, "sg_b": sg_b.shape, "final_g": final_g.shape, "rw_rk": rw_rk.shape,
              "att_bias": att_bias.shape}
    for nm, shape in native.items():
        rep[nm] = [a.reshape(shape) for a in rep[nm]]
    rep["norm_g"] = [jnp.concatenate([a, b], axis=0) for a, b in zip(rep["norm_g0"], rep["norm_g1"])]
    even_landed = send_wait(started["even"], done, True, CHIPS, "even_grads_wait")
    even_swap = sum_and_swap(even_names, even_landed, "even")
    update(odd_names, odd_swap, even_swap[-1], "odd")
    update(even_names, even_swap, sharded["w_in_o"][0], "even")

    order = ["norm_g", "w_in_e", "shift_mu", "rw_w0", "rw_w2", "rw_a0", "rw_a2", "rw_kk", "rw_ka", "rw_rk",
             "rw_lnx_g", "rw_lnx_b", "att_bias", "w_out_e", "w_in_o", "sg_ln_g", "sg_ln_b", "sg_w", "sg_b",
             "w_out_o", "final_g"]
    results = {**sharded, **rep}
    outs = [rep["loss"][0][0, 0], dx.reshape(NSEQ, SEQ, D)]
    for kind in range(4):
        outs += [results[nm][kind] for nm in order]
    return tuple(outs)


def _local_step(x2, tgt, wie_t, late_weights, w2, a2, sglg, sglb, norm_g, shift_mu, rw_w0, rw_a0, rw_kk, rw_ka, rw_rk,
                rw_lnx_g, rw_lnx_b, att_bias, sg_w, sg_b, final_g, first_after=None, on_odd_grads=None,
                on_even_grads=None, on_small_grads=None):
    zl = jnp.zeros((LORA, W), F32)
    w2x = jnp.concatenate([w2, zl], axis=0)
    a2x = jnp.concatenate([zl, a2], axis=0)
    rk = rw_rk.reshape(1, W)
    pos = np.arange(SGC)
    sg_mask = jnp.asarray(((pos[None, :] // L) <= (pos[:, None] // L)).astype(np.float32))
    wm = (sg_w[0] * sg_mask[None]).astype(BF16)
    sgb_t = sg_b[0].T

    xn0, ps, ga, q, kb, vb, gb = ln_in_proj(x2, norm_g[0:1], wie_t, EVEN_SPLITS, "in_proj_even", after=first_after,
                                            w_t=True, bf16_pieces=(2, 3, 4))
    r, lw, k2, v, aa, bb = even_prep(ps, shift_mu, rw_w0, w2x, rw_a0, a2x, rw_kk, rw_ka)
    y, rw_saved = rwkv_fwd(r, lw, k2, v, aa, bb)
    bias = bias_expand(att_bias[0])

    o = attention_fwd(q, kb, vb, bias)
    woe = late_weights("even", o)
    h1, zt = even_post(y, r, k2, v, ga, o, gb, rw_lnx_g, rw_lnx_b, rk, x2, woe)
    woe, wio, woo = late_weights("odd", h1)
    xn1, u, vv, gt = ln_in_proj(h1, norm_g[1:2], wio, ODD_SPLITS, "in_proj_odd")
    dh2, loss_part, d_final_g, z2t = gmlp_fwd_loss(u, vv, gt, sglg, sglb, wm, sgb_t, h1, woo, final_g[None], tgt)

    du, dvv, dgt, d_sglg, d_sglb, d_wm, d_sgb_t, d_woo = gmlp_bwd(u, vv, gt, sglg, sglb, wm, sgb_t, dh2, z2t, woo)
    dp_odd = [du, dvv, dgt]
    d_wio = matmul_acc_chips(xn1, dp_odd, "in_proj_odd_dw")
    token = on_odd_grads(d_woo, d_wio) if on_odd_grads else None
    dh1, d_g1 = in_proj_bwd_x(h1, norm_g[1:2], wio, dp_odd, dh2, "in_proj_odd_bwd", after=token)
    odd_small = [d_wm * sg_mask[None], d_sgb_t.T, d_final_g, d_g1]
    token = on_small_grads("odd", odd_small) if on_small_grads else None
    dy, dr2, dk22, dv2, dga, do, dgb, d_lng, d_lnb, d_rk, d_woe = even_post_bwd(
        y, r, k2, v, ga, o, gb, rw_lnx_g, rw_lnx_b, rk, dh1, zt, woe, after=token)
    dq, dkb, dvb, dbias = attention_bwd(q, kb, vb, bias, do)
    dbias = sum(dbias[:, i * 2 * L:(i + 1) * 2 * L, i * L:i * L + BAND] for i in range(ATT_Q))
    d_att_bias = bias_grad(dbias.reshape(NH, L, BAND))
    dr, dlw, dk2, dv, daa, dbb = rwkv_bwd(r, lw, k2, aa, bb, rw_saved, dy)
    dps, d_mu, d_w0, d_w2x, d_a0, d_a2x, d_kk, d_ka = even_prep_bwd(
        ps, shift_mu, rw_w0, w2x, rw_a0, a2x, rw_kk, rw_ka, dr, dlw, dk2, dv, daa, dbb, dr2, dk22, dv2)
    dp_even = [dps, dga, dq, dkb, dvb, dgb]
    d_wie = matmul_acc_chips(xn0, dp_even, "in_proj_even_dw", add_cores=on_even_grads is not None)
    big_g = (d_wie, d_woe, d_wio, d_woo, d_w2x[:LORA], d_a2x[LORA:], d_sglg, d_sglb)
    token = on_even_grads(big_g) if on_even_grads else None
    dx, d_g0 = in_proj_bwd_x(x2, norm_g[0:1], wie_t, dp_even, dh1, "in_proj_even_bwd", after=token, w_t=True)
    even_small = [d_g0, d_mu, d_w0, d_a0, d_kk, d_ka, d_rk, d_lng, d_lnb, d_att_bias]
    if on_small_grads:
        on_small_grads("even", even_small + [loss_part[0:1, :]])
    rep_g = [jnp.concatenate([d_g0, d_g1], axis=0)] + even_small[1:] + odd_small[:3]
    return loss_part[0, 0], dx, big_g, rep_g
```

```python
import functools
import math

import jax
import jax.numpy as jnp
import numpy as np
from jax import lax
from jax.experimental import pallas as pl
from jax.experimental.pallas import tpu as pltpu

F32 = jnp.float32
BF16 = jnp.bfloat16
HI = lax.Precision.HIGHEST

D = 1024
SEQ = 2048
NSEQ = 2
T = NSEQ * SEQ
HD = 64
NH = 8
W = 512
SHIFT = 1664
LORA = 64
EVEN_IN = 4224
ODD_IN = 3072
L = 64
NC = SEQ // L
LEFT = 8
BAND = (LEFT + 1) * L
CLIP = 128
SGC = 128
NG = 8
RMS_EPS = 1e-6
LN_EPS = 1e-5
GN_EPS = 64e-5
NEG = -1e30
VMEM_BIG = 56 * 1024 * 1024

ADAM_LR = 0.001
ADAM_B1 = 0.9
ADAM_B2 = 0.999
ADAM_EPS = 1e-08
ADAM_WD = 0.01
ADAM_STEP = 10

MESH = pl.DeviceIdType.MESH


def _bdot(a, b):
    return jnp.dot(a.astype(BF16), b.astype(BF16), preferred_element_type=F32)


def _bdot_nt(a, b):
    return lax.dot_general(a.astype(BF16), b.astype(BF16), (((1,), (1,)), ((), ())), preferred_element_type=F32)


def _bdot_tn(a, b):
    return lax.dot_general(a.astype(BF16), b.astype(BF16), (((0,), (0,)), ((), ())), preferred_element_type=F32)


def _hdot(a, b):
    return jnp.dot(a, b, precision=HI, preferred_element_type=F32)


def _hdot_nt(a, b):
    return lax.dot_general(a, b, (((1,), (1,)), ((), ())), precision=HI, preferred_element_type=F32)


def _hdot_tn(a, b):
    return lax.dot_general(a, b, (((0,), (0,)), ((), ())), precision=HI, preferred_element_type=F32)


def _iota2(shape, dim):
    return lax.broadcasted_iota(jnp.int32, shape, dim)


def _head_blockdiag():
    r = _iota2((2 * HD, 2 * HD), 0) // HD
    c = _iota2((2 * HD, 2 * HD), 1) // HD
    return (r == c).astype(BF16)


def _headsum_impl(x, bd):
    hi = x.astype(BF16)
    mid = (x - hi.astype(F32)).astype(BF16)
    n = bd.shape[0]
    out = [jnp.dot(hi[:, i:i + n], bd, preferred_element_type=F32) + jnp.dot(mid[:, i:i + n], bd, preferred_element_type=F32)
           for i in range(0, x.shape[1], n)]
    return jnp.concatenate(out, axis=-1)


@jax.custom_vjp
def _headsum(x, bd):
    return _headsum_impl(x, bd)


def _headsum_fwd(x, bd):
    return _headsum_impl(x, bd), bd


def _headsum_bwd(bd, ct):
    return _headsum_impl(ct, bd), None


_headsum.defvjp(_headsum_fwd, _headsum_bwd)


def _silu(x):
    return x * jax.nn.sigmoid(x)


_GELU_C = math.sqrt(2.0 / math.pi)


def _gelu(x):
    return 0.5 * x * (1.0 + jnp.tanh(_GELU_C * (x + 0.044715 * (x * x * x))))


def _silu_both(x):
    s = jax.nn.sigmoid(x)
    xs = x * s
    return xs, s + xs * (1.0 - s)


def _gelu_both(x):
    x2 = x * x
    t = jnp.tanh(_GELU_C * (x + 0.044715 * (x2 * x)))
    half = 0.5 * (1.0 + t)
    return x * half, half + 0.5 * x * (1.0 - t * t) * _GELU_C * (1.0 + 3.0 * 0.044715 * x2)


def _softplus(x):
    return jnp.maximum(x, 0.0) + jnp.log(1.0 + jnp.exp(-jnp.abs(x)))


def _cparams(sem, vmem=None):
    return pltpu.CompilerParams(dimension_semantics=sem, vmem_limit_bytes=vmem)


def _row_spec(tm, width):
    return pl.BlockSpec((tm, width), lambda i: (i, 0))


def _col_spec(height, tm):
    return pl.BlockSpec((height, tm), lambda i: (0, i))


def _const_spec(shape):
    nd = len(shape)
    return pl.BlockSpec(shape, lambda *_: (0,) * nd)


def _weight_dims(w_bf, w_t):
    if w_bf.ndim == 3:
        return None, w_bf.shape[0] * w_bf.shape[2]
    return (((1,), (1,)), ((), ())) if w_t else (((1,), (0,)), ((), ())), w_bf.shape[0 if w_t else 1]


def _proj(xn, w_ref, dims):
    if dims is None:
        return jnp.concatenate([jnp.dot(xn, w_ref[s], preferred_element_type=F32) for s in range(w_ref.shape[0])],
                               axis=-1)
    return lax.dot_general(xn, w_ref[...], dims, preferred_element_type=F32)


def _proj_back(dp, w_ref, w_t):
    nt = (((1,), (1,)), ((), ()))
    if len(w_ref.shape) == 3:
        nb = w_ref.shape[2]
        parts = [lax.dot_general(dp[:, s * nb:(s + 1) * nb], w_ref[s], nt, preferred_element_type=F32)
                 for s in range(w_ref.shape[0])]
        return sum(parts[1:], parts[0])
    return lax.dot_general(dp, w_ref[...], (((1,), (0,)), ((), ())) if w_t else nt, preferred_element_type=F32)


def ln_in_proj(x, g, w_bf, splits, name, after=None, w_t=False, bf16_pieces=()):
    dims, n = _weight_dims(w_bf, w_t)
    dtypes = [BF16 if i in bf16_pieces else F32 for i in range(len(splits))]
    tm = 512 if n <= ODD_IN else 256
    spans = []
    o = 0
    for s in splits:
        spans.append((o, o + s))
        o += s
    assert o == n
    extra_specs, extra = _after_operand(after)

    def body(x_ref, g_ref, w_ref, *rest):
        xn_ref, outs = rest[len(extra)], rest[len(extra) + 1:]
        xv = x_ref[...]
        rstd = lax.rsqrt(jnp.mean(xv * xv, axis=-1, keepdims=True) + RMS_EPS)
        xn = (xv * rstd * g_ref[...]).astype(BF16)
        xn_ref[...] = xn.T
        p = _proj(xn, w_ref, dims)
        for o_ref, (a, b) in zip(outs, spans):
            o_ref[...] = p[:, a:b].astype(o_ref.dtype)

    return pl.pallas_call(
        body, grid=(T // tm,), name=name,
        in_specs=[_row_spec(tm, D), _const_spec((1, D)), _const_spec(w_bf.shape)] + extra_specs,
        out_specs=[_col_spec(D, tm)] + [_row_spec(tm, s) for s in splits],
        out_shape=[jax.ShapeDtypeStruct((D, T), BF16)]
        + [jax.ShapeDtypeStruct((T, s), dt) for s, dt in zip(splits, dtypes)],
        compiler_params=_cparams(("parallel",), VMEM_BIG),
    )(x, g, w_bf, *extra)


def in_proj_bwd_x(x, g, w_bf, dps, dres, name, after=None, w_t=False):
    tm = 512
    widths = [d.shape[1] for d in dps]
    extra_specs, extra = _after_operand(after)

    def body(x_ref, g_ref, w_ref, dres_ref, *rest):
        dp_refs = rest[:len(widths)]
        dx_ref, dg_ref = rest[-2:]
        dp = jnp.concatenate([r[...] for r in dp_refs], axis=-1)
        dxn = _proj_back(dp, w_ref, w_t)
        xv = x_ref[...]
        rstd = lax.rsqrt(jnp.mean(xv * xv, axis=-1, keepdims=True) + RMS_EPS)
        xhat = xv * rstd
        dgp = jnp.sum(dxn * xhat, axis=0, keepdims=True)

        @pl.when(pl.program_id(0) == 0)
        def _():
            dg_ref[...] = jnp.zeros_like(dg_ref)

        dg_ref[...] += dgp
        dxh = dxn * g_ref[...]
        dx_ref[...] = dres_ref[...] + rstd * (dxh - xhat * jnp.mean(dxh * xhat, axis=-1, keepdims=True))

    return pl.pallas_call(
        body, grid=(T // tm,), name=name,
        in_specs=[_row_spec(tm, D), _const_spec((1, D)), _const_spec(w_bf.shape), _row_spec(tm, D)]
        + [_row_spec(tm, s) for s in widths] + extra_specs,
        out_specs=[_row_spec(tm, D), _const_spec((1, D))],
        out_shape=[jax.ShapeDtypeStruct((T, D), F32), jax.ShapeDtypeStruct((1, D), F32)],
        compiler_params=_cparams(("arbitrary",), VMEM_BIG),
    )(x, g, w_bf, dres, *dps, *extra)


def _after_operand(after):
    return ([ANY], [after]) if after is not None else ([], [])


def matmul_acc_chips(at_bf, pieces, name, after=None, add_cores=False):
    k = at_bf.shape[0]
    widths = [p.shape[1] for p in pieces]
    nb = sum(widths) // NCHIP
    tm = 512
    steps = T // tm
    half = k // 2
    extra_specs, extra = _after_operand(after)

    def body(a_ref, *rest):
        o_ref, acc = rest[len(widths) + len(extra):][:2]

        @pl.when(pl.program_id(0) == 0)
        def _():
            acc[...] = jnp.zeros_like(acc)

        a = a_ref[...]
        b = jnp.concatenate([r[...] for r in rest[:len(widths)]], axis=-1)
        for s in range(NCHIP):
            acc[s] += jnp.dot(a, b[:, s * nb:(s + 1) * nb], preferred_element_type=F32)

        @pl.when(pl.program_id(0) == steps - 1)
        def _():
            if not add_cores:
                o_ref[...] = acc[...].astype(BF16)
            else:
                give, got, send, recv = rest[-4:]
                x, y, c = lax.axis_index("x"), lax.axis_index("y"), lax.axis_index("c")
                theirs = pl.multiple_of((1 - c) * half, half)
                mine = pl.multiple_of(c * half, half)
                give[...] = acc[:, pl.ds(theirs, half), :].astype(BF16)
                cp = pltpu.make_async_remote_copy(src_ref=give, dst_ref=got, send_sem=send, recv_sem=recv,
                                                  device_id=(x, y, 1 - c), device_id_type=MESH)
                cp.start()
                cp.wait()
                o_ref[...] = (acc[:, pl.ds(mine, half), :] + got[...].astype(F32)).astype(BF16)

    out_rows = half if add_cores else k
    exchange = [pltpu.VMEM((NCHIP, half, nb), BF16)] * 2 + [pltpu.SemaphoreType.DMA] * 2 if add_cores else []
    return pl.pallas_call(
        body, grid=(steps,), name=name,
        in_specs=[_col_spec(k, tm)] + [_row_spec(tm, w_) for w_ in widths] + extra_specs,
        out_specs=_const_spec((NCHIP, out_rows, nb)),
        out_shape=jax.ShapeDtypeStruct((NCHIP, out_rows, nb), BF16),
        scratch_shapes=[pltpu.VMEM((NCHIP, k, nb), F32)] + exchange,
        compiler_params=_cparams(("arbitrary",), VMEM_BIG),
    )(at_bf, *pieces, *extra)


def _out_proj_back(dh_ref, zt_ref, w_ref, dw_ref, acc_ref):
    dhb = dh_ref[...].astype(BF16)

    @pl.when(pl.program_id(0) == 0)
    def _():
        acc_ref[...] = jnp.zeros_like(acc_ref)

    acc_ref[...] += jnp.dot(zt_ref[...], dhb, preferred_element_type=F32)

    @pl.when(pl.program_id(0) == pl.num_programs(0) - 1)
    def _():
        dw_ref[...] = acc_ref[...].astype(dw_ref.dtype)

    return lax.dot_general(dhb, w_ref[...], (((1,), (1,)), ((), ())), preferred_element_type=F32)


PREP_TM = 512
PREP_NB = SEQ // PREP_TM


def _prep_elem(k, wl, apre, kkw, kaw, bd):
    wraw = -_softplus(-wl) - 0.5
    lw = -jnp.exp(wraw)
    asig = jax.nn.sigmoid(apre)
    kkr = k * kkw
    nrm = jnp.maximum(jnp.sqrt(_headsum(kkr * kkr, bd)), 1e-12)
    kk = kkr / nrm
    k2 = k * (1.0 + (asig - 1.0) * kaw)
    return lw, k2, -kk, kk * asig


def _prep_elem_bwd(k, wl, apre, kkw, kaw, bd, dlw, dk2, daa, dbb):
    s = -wl
    sp = _softplus(s)
    dwl = dlw * (-jnp.exp(-sp - 0.5)) * jnp.exp(s - sp)
    asig = jax.nn.sigmoid(apre)
    kkr = k * kkw
    root = jnp.sqrt(_headsum(kkr * kkr, bd))
    inv = 1.0 / jnp.maximum(root, 1e-12)
    kk = kkr * inv
    dkk = dbb * asig - daa
    dap = (dbb * kk + dk2 * k * kaw) * asig * (1.0 - asig)
    through_norm = jnp.where(root > 1e-12, kk * _headsum(dkk * kkr, bd) * inv, 0.0)
    dkkr = inv * (dkk - through_norm)
    gain = 1.0 + (asig - 1.0) * kaw
    dk = dkkr * kkw + dk2 * gain
    dkkw = jnp.sum(dkkr * k, axis=0, keepdims=True)
    dkaw = jnp.sum(dk2 * k * (asig - 1.0), axis=0, keepdims=True)
    return dk, dwl, dap, dkkw, dkaw


def _shifted(ps_ref, prev_ref, mu, blk):
    p = ps_ref[...]
    first = (blk % PREP_NB) == 0
    prev_row = jnp.where(first, 0.0, prev_ref[7:8, :])
    rolled = pltpu.roll(p, 1, 0)
    p_prev = jnp.where(_iota2(p.shape, 0) == 0, prev_row, rolled)
    return p, p_prev, p + (p_prev - p) * mu


def _prev_spec(width, blk_of):
    return pl.BlockSpec((8, width), lambda i: (jnp.maximum(blk_of(i) * (PREP_TM // 8) - 1, 0), 0))


def even_prep(ps, mu, w0, w2x, a0, a2x, kkw, kaw):
    tm = PREP_TM

    def body(ps_ref, prev_ref, mu_ref, w0_ref, w2_ref, a0_ref, a2_ref, kk_ref, ka_ref,
             r_ref, lw_ref, k2_ref, v_ref, aa_ref, bb_ref):
        _, _, s = _shifted(ps_ref, prev_ref, mu_ref[...], pl.program_id(0))
        wa = s[:, 3 * W:]
        wl = w0_ref[...] + _bdot(jnp.tanh(wa), w2_ref[...])
        apre = a0_ref[...] + _bdot(wa, a2_ref[...])
        lw, k2, aa, bb = _prep_elem(s[:, W:2 * W], wl, apre, kk_ref[...], ka_ref[...], _head_blockdiag())
        r_ref[...] = s[:, 0:W]
        v_ref[...] = s[:, 2 * W:3 * W]
        lw_ref[...] = lw
        k2_ref[...] = k2
        aa_ref[...] = aa
        bb_ref[...] = bb

    vec = _const_spec((1, W))
    return pl.pallas_call(
        body, grid=(T // tm,), name="even_prep",
        in_specs=[_row_spec(tm, SHIFT), _prev_spec(SHIFT, lambda i: i), _const_spec((1, SHIFT)), vec,
                  _const_spec((2 * LORA, W)), vec, _const_spec((2 * LORA, W)), vec, vec],
        out_specs=[_row_spec(tm, W)] * 6,
        out_shape=[jax.ShapeDtypeStruct((T, W), F32)] * 6,
        compiler_params=_cparams(("parallel",), VMEM_BIG),
    )(ps, ps, mu, w0, w2x, a0, a2x, kkw, kaw)


def even_prep_bwd(ps, mu, w0, w2x, a0, a2x, kkw, kaw, dr, dlw, dk2, dv, daa, dbb, dr2, dk22, dv2):
    tm = PREP_TM
    nb = T // tm
    rev = lambda i: nb - 1 - i

    def body(ps_ref, prev_ref, mu_ref, w0_ref, w2_ref, a0_ref, a2_ref, kk_ref, ka_ref,
             dr_ref, dlw_ref, dk2_ref, dv_ref, daa_ref, dbb_ref, dr2_ref, dk22_ref, dv2_ref,
             dps_ref, dmu_ref, dw0_ref, dw2_ref, da0_ref, da2_ref, dkk_ref, dka_ref, carry):
        i = pl.program_id(0)
        blk = rev(i)
        mu_v = mu_ref[...]
        p, p_prev, s = _shifted(ps_ref, prev_ref, mu_v, blk)
        wa = s[:, 3 * W:]
        th = jnp.tanh(wa)
        wl = w0_ref[...] + _bdot(th, w2_ref[...])
        apre = a0_ref[...] + _bdot(wa, a2_ref[...])
        bd = _head_blockdiag()
        k = s[:, W:2 * W]
        dk, dwl, dap, dkkw, dkaw = _prep_elem_bwd(k, wl, apre, kk_ref[...], ka_ref[...], bd, dlw_ref[...],
                                                  dk2_ref[...] + dk22_ref[...], daa_ref[...], dbb_ref[...])
        dwa = _bdot_nt(dwl, w2_ref[...]) * (1.0 - th * th) + _bdot_nt(dap, a2_ref[...])
        ds = jnp.concatenate([dr_ref[...] + dr2_ref[...], dk, dv_ref[...] + dv2_ref[...], dwa], axis=-1)

        @pl.when(i == 0)
        def _():
            for ref in (dmu_ref, dw0_ref, dw2_ref, da0_ref, da2_ref, dkk_ref, dka_ref, carry):
                ref[...] = jnp.zeros_like(ref)

        dmu_ref[...] += jnp.sum(ds * (p_prev - p), axis=0, keepdims=True)
        dw0_ref[...] += jnp.sum(dwl, axis=0, keepdims=True)
        da0_ref[...] += jnp.sum(dap, axis=0, keepdims=True)
        dw2_ref[...] += _bdot_tn(th, dwl)
        da2_ref[...] += _bdot_tn(wa, dap)
        dkk_ref[...] += dkkw
        dka_ref[...] += dkaw
        dsm = ds * mu_v
        last = (blk % PREP_NB) == PREP_NB - 1
        nxt = jnp.where(last, 0.0, carry[0:1, :])
        up = pltpu.roll(dsm, tm - 1, 0)
        up = jnp.where(_iota2(up.shape, 0) == tm - 1, nxt, up)
        dps_ref[...] = (ds - dsm + up).astype(BF16)
        carry[0:1, :] = dsm[0:1, :]

    vec = _const_spec((1, W))
    rrow = lambda width: pl.BlockSpec((tm, width), lambda i: (rev(i), 0))
    return pl.pallas_call(
        body, grid=(nb,), name="even_prep_bwd",
        in_specs=[rrow(SHIFT), _prev_spec(SHIFT, rev), _const_spec((1, SHIFT)), vec,
                  _const_spec((2 * LORA, W)), vec, _const_spec((2 * LORA, W)), vec, vec] + [rrow(W)] * 9,
        out_specs=[rrow(SHIFT), _const_spec((1, SHIFT)), vec, _const_spec((2 * LORA, W)), vec,
                   _const_spec((2 * LORA, W)), vec, vec],
        out_shape=[jax.ShapeDtypeStruct((T, SHIFT), BF16), jax.ShapeDtypeStruct((1, SHIFT), F32),
                   jax.ShapeDtypeStruct((1, W), F32), jax.ShapeDtypeStruct((2 * LORA, W), F32),
                   jax.ShapeDtypeStruct((1, W), F32), jax.ShapeDtypeStruct((2 * LORA, W), F32),
                   jax.ShapeDtypeStruct((1, W), F32), jax.ShapeDtypeStruct((1, W), F32)],
        scratch_shapes=[pltpu.VMEM((8, SHIFT), F32)],
        compiler_params=_cparams(("arbitrary",), VMEM_BIG),
    )(ps, ps, mu, w0, w2x, a0, a2x, kkw, kaw, dr, dlw, dk2, dv, daa, dbb, dr2, dk22, dv2)


NPAIR = NH // 2
PW = 2 * HD


def _pair_cols(p):
    return slice(p * PW, (p + 1) * PW)


def _pairs(a):
    return [a[:, _pair_cols(p)] for p in range(NPAIR)]


def _stack_pair(a):
    first = _iota2(a.shape, 1) < HD
    zero = jnp.zeros_like(a)
    return jnp.concatenate([jnp.where(first, a, zero), jnp.where(first, zero, a)], axis=0)


def _unstack_pair(a):
    n = a.shape[0] // 2
    return jnp.where(_iota2((n, PW), 1) < HD, a[:n], a[n:])


def _fold_pair(a):
    n = a.shape[0] // 2
    return a[:n] + a[n:]


def _chunk_masks():
    n = 4 * L
    row = _iota2((n, n), 0)
    col = _iota2((n, n), 1)
    same = ((row // L) & 1) == ((col // L) & 1)
    ri = row & (L - 1)
    ci = col & (L - 1)
    keep = same & (((row < 2 * L) & (ri > ci)) | ((row >= 2 * L) & (ri >= ci)))
    r1 = _iota2((L, L), 0)
    c1 = _iota2((L, L), 1)
    r2 = _iota2((2 * L, 2 * L), 0)
    c2 = _iota2((2 * L, 2 * L), 1)
    return keep.astype(F32), (r1 >= c1).astype(F32), (r2 == c2).astype(F32)


def _scaled(r, lw, k2, aa, bb, tri):
    g = _hdot(tri, lw)
    eg = jnp.exp(g)
    eng = jnp.exp(-g)
    egp = jnp.exp(g - lw)
    return eg, eng, egp, aa * egp, r * eg, bb * eng, k2 * eng


def _head_cols(h):
    return slice(h * HD, (h + 1) * HD)


def _per_head(a):
    return [a[:, _head_cols(h)] for h in range(NH)]


def _pairs_operands(at, rt, bt, kt):
    x = [jnp.concatenate([_stack_pair(a), _stack_pair(r)], axis=0).astype(BF16) for a, r in zip(_pairs(at), _pairs(rt))]
    yk = [jnp.concatenate([_stack_pair(b), _stack_pair(k)], axis=0).astype(BF16) for b, k in zip(_pairs(bt), _pairs(kt))]
    return x, yk


def _pairs_matrices(x, yk, keep, eye):
    m = [_bdot_nt(a, b) * keep for a, b in zip(x, yk)]
    p = [a[:2 * L, :2 * L] for a in m]
    tinv = [eye + a for a in p]
    for _ in range(5):
        p = [_bdot(a, a) for a in p]
        tinv = [t + _bdot(t, a) for t, a in zip(tinv, p)]
    return [a.astype(BF16) for a in m], [a.astype(BF16) for a in tinv]


def _pairs_fwd(x, yk, m, tinv, vw, s0, egl):
    xh = [_bdot_nt(a, s) for a, s in zip(x, s0)]
    u = [_bdot(t, h[:2 * L] + _bdot(a[:2 * L, 2 * L:], w)) for t, h, a, w in zip(tinv, xh, m, vw)]
    uv = [jnp.concatenate([a, w], axis=0).astype(BF16) for a, w in zip(u, vw)]
    y = [h[2 * L:] + _bdot(a[2 * L:], w) for h, a, w in zip(xh, m, uv)]
    sn = [e * (s + _bdot_tn(w, b)) for e, s, w, b in zip(egl, s0, uv, yk)]
    return y, sn, uv


def _pairs_bwd(x, yk, m, tinv, uv, s0, sn, egl, dyw, dsn, keep):
    dzs = [d * e for d, e in zip(dsn, egl)]
    dgl = [jnp.sum(d * s, axis=0, keepdims=True) for d, s in zip(dsn, sn)]
    dyb = [a.astype(BF16) for a in dyw]
    t1 = [_bdot_tn(a[2 * L:], d) for a, d in zip(m, dyb)]
    t2 = [_bdot_nt(b, d) for b, d in zip(yk, dzs)]
    drhs = [_bdot_tn(t, a[:2 * L] + b[:2 * L]) for t, a, b in zip(tinv, t1, t2)]
    dv = [a[2 * L:] + b[2 * L:] + _bdot_tn(c[:2 * L, 2 * L:], d) for a, b, c, d in zip(t1, t2, m, drhs)]
    gg = [jnp.concatenate([a, b], axis=0).astype(BF16) for a, b in zip(drhs, dyw)]
    ds0 = [d + _bdot_tn(g, a) for d, g, a in zip(dzs, gg, x)]
    dm = [_bdot_nt(g, w) * keep for g, w in zip(gg, uv)]
    dx = [_bdot(g, s) + _bdot(d, b) for g, s, d, b in zip(gg, s0, dm, yk)]
    dyk = [_bdot_tn(d, a) + _bdot(w, z) for d, a, w, z in zip(dm, x, uv, dzs)]
    return dx, dyk, dv, dgl, ds0


STATE_SHAPE = (NPAIR * PW, PW)
M_SHAPE = (4 * L, NPAIR * 4 * L)
TINV_SHAPE = (2 * L, NPAIR * 2 * L)


def _rows_of(a, n):
    return [a[i * n:(i + 1) * n, :] for i in range(NPAIR)]


def _both(f):
    out = []
    for s in range(NSEQ):
        out += f(s)
    return out


def _seq_view(a):
    return a.reshape(NSEQ, SEQ, a.shape[-1])


UV_SHAPE = (4 * L, NPAIR * PW)
RW_CHUNKS = 2


def rwkv_fwd(r, lw, k2, v, aa, bb):
    def body(r_ref, lw_ref, k2_ref, v_ref, aa_ref, bb_ref, y_ref, hs_ref, hn_ref, m_ref, t_ref, uv_ref, state):
        @pl.when(pl.program_id(0) == 0)
        def _():
            state[...] = jnp.zeros_like(state)

        keep, tri, eye = _chunk_masks()
        where = [(j, s) for j in range(RW_CHUNKS) for s in range(NSEQ)]
        rows = lambda j: slice(j * L, (j + 1) * L)
        sc = [_scaled(r_ref[s, rows(j)], lw_ref[s, rows(j)], k2_ref[s, rows(j)], aa_ref[s, rows(j)],
                      bb_ref[s, rows(j)], tri) for j, s in where]
        ops = [_pairs_operands(*a[3:]) for a in sc]
        m, tinv = _pairs_matrices([a for o in ops for a in o[0]], [a for o in ops for a in o[1]], keep, eye)
        s_cur = [state[s] for s in range(NSEQ)]
        for j in range(RW_CHUNKS):
            mine = slice(j * NSEQ * NPAIR, (j + 1) * NSEQ * NPAIR)
            x = [a for o in ops[j * NSEQ:(j + 1) * NSEQ] for a in o[0]]
            yk = [a for o in ops[j * NSEQ:(j + 1) * NSEQ] for a in o[1]]
            vw = _both(lambda s: [_stack_pair(a) for a in _pairs(v_ref[s, rows(j)])])
            egl = _both(lambda s: _pairs(sc[j * NSEQ + s][0][L - 1:L, :]))
            y, sn, uv = _pairs_fwd(x, yk, m[mine], tinv[mine], vw, _both(lambda s: _rows_of(s_cur[s], PW)), egl)
            for s in range(NSEQ):
                ps = slice(s * NPAIR, (s + 1) * NPAIR)
                hs_ref[j, s] = s_cur[s]
                y_ref[s, rows(j)] = jnp.concatenate([_fold_pair(a) for a in y[ps]], axis=-1)
                m_ref[j, s] = jnp.concatenate(m[mine][ps], axis=-1)
                t_ref[j, s] = jnp.concatenate(tinv[mine][ps], axis=-1)
                uv_ref[j, s] = jnp.concatenate(uv[ps], axis=-1)
                s_cur[s] = jnp.concatenate(sn[ps], axis=0)
                hn_ref[j, s] = s_cur[s]
        for s in range(NSEQ):
            state[s] = s_cur[s]

    blk = pl.BlockSpec((NSEQ, RW_CHUNKS * L, W), lambda c: (0, c, 0))
    per_chunk = lambda shape: pl.BlockSpec((RW_CHUNKS, NSEQ) + shape, lambda c: (c, 0, 0, 0))
    saved_shapes = [(STATE_SHAPE, F32), (STATE_SHAPE, F32), (M_SHAPE, BF16), (TINV_SHAPE, BF16), (UV_SHAPE, BF16)]
    y, *saved = pl.pallas_call(
        body, grid=(NC // RW_CHUNKS,), name="rwkv_fwd",
        in_specs=[blk] * 6,
        out_specs=[blk] + [per_chunk(shape) for shape, _ in saved_shapes],
        out_shape=[jax.ShapeDtypeStruct((NSEQ, SEQ, W), F32)]
        + [jax.ShapeDtypeStruct((NC, NSEQ) + shape, dt) for shape, dt in saved_shapes],
        scratch_shapes=[pltpu.VMEM((NSEQ,) + STATE_SHAPE, F32)],
        compiler_params=_cparams(("arbitrary",), VMEM_BIG),
    )(*[_seq_view(a) for a in (r, lw, k2, v, aa, bb)])
    return y.reshape(T, W), saved


def rwkv_bwd(r, lw, k2, aa, bb, saved, dy):
    def body(r_ref, lw_ref, k2_ref, aa_ref, bb_ref, hs_ref, hn_ref, m_ref, t_ref, uv_ref, dy_ref,
             dr_ref, dlw_ref, dk2_ref, dv_ref, daa_ref, dbb_ref, dstate):
        @pl.when(pl.program_id(0) == 0)
        def _():
            dstate[...] = jnp.zeros_like(dstate)

        keep, tri, _ = _chunk_masks()
        sc = [_scaled(r_ref[s], lw_ref[s], k2_ref[s], aa_ref[s], bb_ref[s], tri) for s in range(NSEQ)]
        ops = [_pairs_operands(*sc[s][3:]) for s in range(NSEQ)]
        x, yk = _both(lambda s: ops[s][0]), _both(lambda s: ops[s][1])
        m = _both(lambda s: [m_ref[0, s][:, i * 4 * L:(i + 1) * 4 * L] for i in range(NPAIR)])
        tinv = _both(lambda s: [t_ref[0, s][:, i * 2 * L:(i + 1) * 2 * L] for i in range(NPAIR)])
        uv = _both(lambda s: _pairs(uv_ref[0, s]))
        dyw = _both(lambda s: [_stack_pair(a) for a in _pairs(dy_ref[s])])
        s0 = _both(lambda s: _rows_of(hs_ref[0, s], PW))
        sn = _both(lambda s: _rows_of(hn_ref[0, s], PW))
        dsn = _both(lambda s: _rows_of(dstate[s], PW))
        egl = _both(lambda s: _pairs(sc[s][0][L - 1:L, :]))
        dx, dyk, dvw, dgl, ds0 = _pairs_bwd(x, yk, m, tinv, uv, s0, sn, egl, dyw, dsn, keep)
        for s in range(NSEQ):
            mine = slice(s * NPAIR, (s + 1) * NPAIR)
            eg, eng, egp, at, rt, bt, kt = sc[s]
            dstate[s] = jnp.concatenate(ds0[mine], axis=0)
            dv_ref[s] = jnp.concatenate([_fold_pair(a) for a in dvw[mine]], axis=-1)
            dat = jnp.concatenate([_fold_pair(a[:2 * L]) for a in dx[mine]], axis=-1)
            drt = jnp.concatenate([_fold_pair(a[2 * L:]) for a in dx[mine]], axis=-1)
            dbt = jnp.concatenate([_fold_pair(a[:2 * L]) for a in dyk[mine]], axis=-1)
            dkt = jnp.concatenate([_fold_pair(a[2 * L:]) for a in dyk[mine]], axis=-1)
            dg = drt * rt - dbt * bt - dkt * kt
            dg = dg + jnp.where(_iota2(dg.shape, 0) == L - 1, jnp.concatenate(dgl[mine], axis=-1), 0.0)
            dgp = dat * at
            dlw_ref[s] = _hdot_tn(tri, dg + dgp) - dgp
            dr_ref[s] = drt * eg
            daa_ref[s] = dat * egp
            dbb_ref[s] = dbt * eng
            dk2_ref[s] = dkt * eng

    blk = pl.BlockSpec((NSEQ, L, W), lambda c: (0, NC - 1 - c, 0))
    per_chunk = lambda shape: pl.BlockSpec((1, NSEQ) + shape, lambda c: (NC - 1 - c, 0, 0, 0))
    outs = pl.pallas_call(
        body, grid=(NC,), name="rwkv_bwd",
        in_specs=[blk] * 5 + [per_chunk(a.shape[2:]) for a in saved] + [blk],
        out_specs=[blk] * 6,
        out_shape=[jax.ShapeDtypeStruct((NSEQ, SEQ, W), F32)] * 6,
        scratch_shapes=[pltpu.VMEM((NSEQ,) + STATE_SHAPE, F32)],
        compiler_params=_cparams(("arbitrary",)),
    )(*[_seq_view(a) for a in (r, lw, k2, aa, bb)], *saved, _seq_view(dy))
    return [a.reshape(T, W) for a in outs]


def _post_math(y, r, k2, v, ga, o, gb, lng, lnb, rk, bd):
    mu = _headsum(y, bd) * (1.0 / HD)
    yc = y - mu
    var = _headsum(yc * yc, bd) * (1.0 / HD)
    yn = yc * lax.rsqrt(var + GN_EPS) * lng + lnb
    bonus = _headsum(r * k2 * rk, bd) * v
    return (yn + bonus) * _silu(ga), o * _silu(gb)


def even_post(y, r, k2, v, ga, o, gb, lng, lnb, rk, h, w_bf):
    tm = 512

    def body(y_ref, r_ref, k2_ref, v_ref, ga_ref, o_ref, gb_ref, lng_ref, lnb_ref, rk_ref, h_ref, w_ref,
             ho_ref, zt_ref):
        ya, yb = _post_math(y_ref[...], r_ref[...], k2_ref[...], v_ref[...], ga_ref[...], o_ref[...], gb_ref[...],
                            lng_ref[...], lnb_ref[...], rk_ref[...], _head_blockdiag())
        z = jnp.concatenate([ya.astype(BF16), yb.astype(BF16)], axis=-1)
        zt_ref[...] = z.T
        ho_ref[...] = h_ref[...] + jnp.dot(z, w_ref[...], preferred_element_type=F32)

    vec = _const_spec((1, W))
    return pl.pallas_call(
        body, grid=(T // tm,), name="even_post",
        in_specs=[_row_spec(tm, W)] * 7 + [vec] * 3 + [_row_spec(tm, D), _const_spec((D, D))],
        out_specs=[_row_spec(tm, D), _col_spec(D, tm)],
        out_shape=[jax.ShapeDtypeStruct((T, D), F32), jax.ShapeDtypeStruct((D, T), BF16)],
        compiler_params=_cparams(("parallel",), VMEM_BIG),
    )(y, r, k2, v, ga, o, gb, lng, lnb, rk, h, w_bf)


def even_post_bwd(y, r, k2, v, ga, o, gb, lng, lnb, rk, dh, zt_bf, w_bf, after=None):
    tm = 512
    extra_specs, extra = _after_operand(after)

    def body(y_ref, r_ref, k2_ref, v_ref, ga_ref, o_ref, gb_ref, lng_ref, lnb_ref, rk_ref, dh_ref, zt_ref, w_ref,
             *rest):
        (dy_ref, dr_ref, dk2_ref, dv_ref, dga_ref, do_ref, dgb_ref, dlng_ref, dlnb_ref, drk_ref, dw_ref,
         acc_ref) = rest[-12:]
        dzv = _out_proj_back(dh_ref, zt_ref, w_ref, dw_ref, acc_ref)
        bd = _head_blockdiag()
        _, vjp = jax.vjp(lambda *a: _post_math(*a, bd), y_ref[...], r_ref[...], k2_ref[...], v_ref[...], ga_ref[...],
                         o_ref[...], gb_ref[...], lng_ref[...], lnb_ref[...], rk_ref[...])
        dy, dr, dk2, dv, dga, do, dgb, dlng, dlnb, drk = vjp((dzv[:, 0:W], dzv[:, W:2 * W]))
        for ref, val in ((dy_ref, dy), (dr_ref, dr), (dk2_ref, dk2), (dv_ref, dv), (dga_ref, dga), (do_ref, do),
                         (dgb_ref, dgb)):
            ref[...] = val.astype(ref.dtype)

        @pl.when(pl.program_id(0) == 0)
        def _():
            for ref in (dlng_ref, dlnb_ref, drk_ref):
                ref[...] = jnp.zeros_like(ref)

        dlng_ref[...] += dlng
        dlnb_ref[...] += dlnb
        drk_ref[...] += drk

    vec = _const_spec((1, W))
    return pl.pallas_call(
        body, grid=(T // tm,), name="even_post_bwd",
        in_specs=[_row_spec(tm, W)] * 7 + [vec] * 3 + [_row_spec(tm, D), _col_spec(D, tm), _const_spec((D, D))]
        + extra_specs,
        out_specs=[_row_spec(tm, W)] * 7 + [vec] * 3 + [_const_spec((D, D))],
        out_shape=[jax.ShapeDtypeStruct((T, W), dt) for dt in (F32, F32, F32, F32, BF16, F32, BF16)]
        + [jax.ShapeDtypeStruct((1, W), F32)] * 3 + [jax.ShapeDtypeStruct((D, D), BF16)],
        scratch_shapes=[pltpu.VMEM((D, D), F32)],
        compiler_params=_cparams(("arbitrary",), VMEM_BIG),
    )(y, r, k2, v, ga, o, gb, lng, lnb, rk, dh, zt_bf, w_bf, *extra)


PADSEQ = SEQ + LEFT * L
ATT_SCALE = 1.0 / math.sqrt(HD)
ATT_Q = 4
WIN = BAND + (ATT_Q - 1) * L
ATT_STEPS = NC // ATT_Q
ATT_BIAS_SHAPE = (NPAIR, ATT_Q * 2 * L, WIN)
ATT_WINDOW_BIAS_SHAPE = (ATT_Q, NPAIR, 2 * L, WIN)


def _stack_chunks(a):
    return jnp.concatenate([_stack_pair(a[i * L:(i + 1) * L]) for i in range(ATT_Q)], axis=0)


def _unstack_chunks(a):
    return jnp.concatenate([_unstack_pair(a[i * 2 * L:(i + 1) * 2 * L]) for i in range(ATT_Q)], axis=0)


def _window_bias(b_ref):
    return [jnp.concatenate([b_ref[c, p] for c in range(ATT_Q)], axis=0) for p in range(NPAIR)]


def _key_window(ref, step):
    start = step * (ATT_Q * L) - LEFT * L
    rows = ref[pl.ds(pl.multiple_of(jnp.maximum(start, 0), L), WIN), :]
    window = rows
    for lead in range(ATT_Q * L, LEFT * L + 1, ATT_Q * L):
        moved = jnp.concatenate([rows[WIN - lead:], rows[:WIN - lead]], axis=0)
        window = jnp.where(start == -lead, moved, window)
    return window


def _att_probs(q2, kw, bias, step):
    valid = _iota2((1, WIN), 1) >= (LEFT - step * ATT_Q) * L
    s = [jnp.where(valid, _bdot_nt(a, b) * ATT_SCALE + bias[p], NEG) for p, (a, b) in enumerate(zip(q2, kw))]
    e = [jnp.exp(a - jnp.max(a, axis=-1, keepdims=True)) for a in s]
    return [a / jnp.sum(a, axis=-1, keepdims=True) for a in e]


def attention_fwd(q, k, v, bias):
    def body(q_ref, k_ref, v_ref, b_ref, o_ref):
        step = pl.program_id(1)
        kw = _pairs(_key_window(k_ref, step))
        vw = _pairs(_key_window(v_ref, step))
        q2 = [_stack_chunks(a) for a in _pairs(q_ref[...])]
        p = _att_probs(q2, kw, _window_bias(b_ref), step)
        o_ref[...] = jnp.concatenate([_unstack_chunks(_bdot(a, b)) for a, b in zip(p, vw)], axis=-1)

    qblk = pl.BlockSpec((ATT_Q * L, W), lambda b, c: (b * ATT_STEPS + c, 0))
    kblk = pl.BlockSpec((SEQ, W), lambda b, c: (b, 0))
    return pl.pallas_call(
        body, grid=(NSEQ, ATT_STEPS), name="attention_fwd",
        in_specs=[qblk, kblk, kblk, _const_spec(ATT_WINDOW_BIAS_SHAPE)],
        out_specs=qblk, out_shape=jax.ShapeDtypeStruct((T, W), F32),
        compiler_params=_cparams(("parallel", "arbitrary")),
    )(q, k, v, bias)


def attention_bwd(q, k, v, bias, do):
    def body(q_ref, k_ref, v_ref, b_ref, do_ref, dq_ref, dko_ref, dvo_ref, db_ref, dk_ref, dv_ref):
        b = pl.program_id(0)
        c = pl.program_id(1)

        @pl.when(c == 0)
        def _():
            dk_ref[...] = jnp.zeros_like(dk_ref)
            dv_ref[...] = jnp.zeros_like(dv_ref)

        @pl.when((c == 0) & (b == 0))
        def _():
            db_ref[...] = jnp.zeros_like(db_ref)

        start = pl.multiple_of(c * (ATT_Q * L), L)
        kw = _pairs(_key_window(k_ref, c))
        vw = _pairs(_key_window(v_ref, c))
        q2 = [_stack_chunks(a) for a in _pairs(q_ref[...])]
        do2 = [_stack_chunks(a) for a in _pairs(do_ref[...].astype(BF16))]
        p = _att_probs(q2, kw, _window_bias(b_ref), c)
        dp = [_bdot_nt(a, b) for a, b in zip(do2, vw)]
        ds = [a * (d - jnp.sum(d * a, axis=-1, keepdims=True)) for a, d in zip(p, dp)]
        dss = [(a * ATT_SCALE).astype(BF16) for a in ds]
        dq_ref[...] = jnp.concatenate([_unstack_chunks(_bdot(a, b)) for a, b in zip(dss, kw)], axis=-1).astype(BF16)
        dk_ref[pl.ds(start, WIN), :] += jnp.concatenate([_bdot_tn(a, b) for a, b in zip(dss, q2)], axis=-1)
        dv_ref[pl.ds(start, WIN), :] += jnp.concatenate([_bdot_tn(a, b) for a, b in zip(p, do2)], axis=-1)
        for i in range(NPAIR):
            db_ref[i] += ds[i]

        @pl.when(c == ATT_STEPS - 1)
        def _():
            dko_ref[...] = dk_ref[LEFT * L:, :].astype(BF16)
            dvo_ref[...] = dv_ref[LEFT * L:, :].astype(BF16)

    qblk = pl.BlockSpec((ATT_Q * L, W), lambda b, c: (b * ATT_STEPS + c, 0))
    sblk = pl.BlockSpec((SEQ, W), lambda b, c: (b, 0))
    bblk = _const_spec(ATT_BIAS_SHAPE)
    return pl.pallas_call(
        body, grid=(NSEQ, ATT_STEPS), name="attention_bwd",
        in_specs=[qblk, sblk, sblk, _const_spec(ATT_WINDOW_BIAS_SHAPE), qblk],
        out_specs=[qblk, sblk, sblk, bblk],
        out_shape=[jax.ShapeDtypeStruct((T, W), BF16), jax.ShapeDtypeStruct((T, W), BF16),
                   jax.ShapeDtypeStruct((T, W), BF16), jax.ShapeDtypeStruct(ATT_BIAS_SHAPE, F32)],
        scratch_shapes=[pltpu.VMEM((PADSEQ, W), F32), pltpu.VMEM((PADSEQ, W), F32)],
        compiler_params=_cparams(("arbitrary", "arbitrary"), VMEM_BIG),
    )(q, k, v, bias, do)


NTAB = 2 * CLIP + 1
EXT = BAND + L


def _ext_onehot():
    n = _iota2((EXT, NTAB), 0)
    m = _iota2((EXT, NTAB), 1)
    return (jnp.clip(BAND - 1 - n, -CLIP, CLIP) + CLIP == m).astype(F32)


def bias_expand(table):
    def body(t_ref, o_ref):
        ext = _hdot_nt(t_ref[...], _ext_onehot())
        ext = jnp.concatenate([ext, jnp.zeros((NH, WIN - EXT), F32)], axis=-1)
        col = _iota2((L, WIN), 1)
        for h in range(NH):
            rows = jnp.broadcast_to(ext[h:h + 1], (L, WIN))
            for c in range(ATT_Q):
                inside = (col >= c * L) & (col < c * L + BAND)
                plane = pltpu.roll(rows, (c * L - (L - 1)) % WIN, 1, stride=1, stride_axis=0)
                o_ref[c, h] = jnp.where(inside, plane, NEG)

    out = pl.pallas_call(body, name="bias_expand", out_shape=jax.ShapeDtypeStruct((ATT_Q, NH, L, WIN), F32))(table)
    return out.reshape(ATT_WINDOW_BIAS_SHAPE)


def bias_grad(dbias):
    def body(d_ref, o_ref):
        acc = jnp.zeros((NH, EXT), F32)
        zpad = jnp.zeros((NH, EXT - BAND), F32)
        for i in range(L):
            s = L - 1 - i
            row = jnp.concatenate([d_ref[:, i, :], zpad], axis=-1)
            acc = acc + (pltpu.roll(row, s, 1) if s else row)
        o_ref[...] = _hdot(acc, _ext_onehot())

    return pl.pallas_call(body, name="bias_grad", out_shape=jax.ShapeDtypeStruct((NH, NTAB), F32))(dbias)


def _group_cols(g):
    return slice(g * SGC, (g + 1) * SGC)


def _sg_norm(gv, lng, lnb):
    gc = gv - jnp.mean(gv, axis=-1, keepdims=True)
    rstd = lax.rsqrt(jnp.mean(gc * gc, axis=-1, keepdims=True) + LN_EPS)
    xhat = gc * rstd
    return xhat, rstd, xhat * lng + lnb


GMLP_BWD_CHUNKS = 2


def gmlp_fwd_loss(u, v, gate, lng, lnb, wm_bf, sgb_t, h, w_bf, g_final, target):
    tm = GMLP_BWD_CHUNKS * SGC

    def body(u_ref, v_ref, gt_ref, lng_ref, lnb_ref, wm_ref, sb_ref, h_ref, w_ref, g_ref, t_ref,
             dh_ref, loss_ref, dg_ref, zt_ref):
        zs = []
        for ch in range(GMLP_BWD_CHUNKS):
            rows = slice(ch * SGC, (ch + 1) * SGC)
            _, _, vln = _sg_norm(_gelu(v_ref[rows, :]), lng_ref[...], lnb_ref[...])
            vlb = vln.astype(BF16)
            zg = []
            for g in range(NG):
                cs = _group_cols(g)
                sv = jnp.dot(wm_ref[g], vlb[:, cs], preferred_element_type=F32) + sb_ref[:, g:g + 1]
                zg.append((_gelu(u_ref[rows, cs]) * sv * _silu(gt_ref[rows, cs])).astype(BF16))
            zs.append(jnp.concatenate(zg, axis=-1))
        z = jnp.concatenate(zs, axis=0)
        zt_ref[...] = z.T
        xv = h_ref[...] + jnp.dot(z, w_ref[...], preferred_element_type=F32)
        rstd = lax.rsqrt(jnp.mean(xv * xv, axis=-1, keepdims=True) + RMS_EPS)
        xhat = xv * rstd
        err = xhat * g_ref[...] - t_ref[...]
        part = 0.5 * jnp.sum(jnp.mean(err * err, axis=-1, keepdims=True), axis=0, keepdims=True)
        dout = err * (1.0 / D)

        @pl.when(pl.program_id(0) == 0)
        def _():
            loss_ref[...] = jnp.zeros_like(loss_ref)
            dg_ref[...] = jnp.zeros_like(dg_ref)

        loss_ref[...] += jnp.broadcast_to(part, loss_ref.shape)
        dg_ref[...] += jnp.sum(dout * xhat, axis=0, keepdims=True)
        dxh = dout * g_ref[...]
        dh_ref[...] = rstd * (dxh - xhat * jnp.mean(dxh * xhat, axis=-1, keepdims=True))

    return pl.pallas_call(
        body, grid=(T // tm,), name="gmlp_fwd_loss",
        in_specs=[_row_spec(tm, D)] * 3 + [_const_spec((1, D))] * 2
        + [_const_spec((NG, SGC, SGC)), _const_spec((SGC, NG)), _row_spec(tm, D), _const_spec((D, D)),
           _const_spec((1, D)), _row_spec(tm, D)],
        out_specs=[_row_spec(tm, D), _const_spec((8, 128)), _const_spec((1, D)), _col_spec(D, tm)],
        out_shape=[jax.ShapeDtypeStruct((T, D), F32), jax.ShapeDtypeStruct((8, 128), F32),
                   jax.ShapeDtypeStruct((1, D), F32), jax.ShapeDtypeStruct((D, T), BF16)],
        compiler_params=_cparams(("arbitrary",), VMEM_BIG),
    )(u, v, gate, lng, lnb, wm_bf, sgb_t, h, w_bf, g_final, target)


def gmlp_bwd(u, v, gate, lng, lnb, wm_bf, sgb_t, dh, zt_bf, w_bf):
    def body(u_ref, v_ref, gt_ref, lng_ref, lnb_ref, wm_ref, sb_ref, dh_ref, zt_ref, w_ref,
             du_ref, dv_ref, dgt_ref, dlng_ref, dlnb_ref, dwm_ref, dsb_ref, dw_ref, acc_ref):
        @pl.when(pl.program_id(0) == 0)
        def _():
            for ref in (dlng_ref, dlnb_ref, dwm_ref, dsb_ref):
                ref[...] = jnp.zeros_like(ref)

        dz = _out_proj_back(dh_ref, zt_ref, w_ref, dw_ref, acc_ref)
        sel = (_iota2((D, NG), 0) // SGC == _iota2((D, NG), 1)).astype(F32)
        for ch in range(GMLP_BWD_CHUNKS):
            rows = slice(ch * SGC, (ch + 1) * SGC)
            gv, dgv_dv = _gelu_both(v_ref[rows, :])
            xhat, rstd, vln = _sg_norm(gv, lng_ref[...], lnb_ref[...])
            vlb = vln.astype(BF16)
            dvln = []
            dsv_all = []
            for g in range(NG):
                cs = _group_cols(g)
                uu = u_ref[rows, cs]
                gg = gt_ref[rows, cs]
                dzz = dz[rows, cs]
                sv = jnp.dot(wm_ref[g], vlb[:, cs], preferred_element_type=F32) + sb_ref[:, g:g + 1]
                gu, dgu = _gelu_both(uu)
                sg, dsg = _silu_both(gg)
                dzgu = dzz * gu
                dsv = dzgu * sg
                dgt_ref[rows, cs] = (dzgu * sv * dsg).astype(BF16)
                du_ref[rows, cs] = (dzz * sv * sg * dgu).astype(BF16)
                dsb16 = dsv.astype(BF16)
                dvln.append(lax.dot_general(wm_ref[g], dsb16, (((0,), (0,)), ((), ())), preferred_element_type=F32))
                dwm_ref[g] += lax.dot_general(dsb16, vlb[:, cs], (((1,), (1,)), ((), ())),
                                              preferred_element_type=F32)
                dsv_all.append(dsv)
            dvl = jnp.concatenate(dvln, axis=-1)
            dsb_ref[...] += _hdot(jnp.concatenate(dsv_all, axis=-1), sel)
            dlng_ref[...] += jnp.sum(dvl * xhat, axis=0, keepdims=True)
            dlnb_ref[...] += jnp.sum(dvl, axis=0, keepdims=True)
            dxh = dvl * lng_ref[...]
            dgv = rstd * (dxh - jnp.mean(dxh, axis=-1, keepdims=True)
                          - xhat * jnp.mean(dxh * xhat, axis=-1, keepdims=True))
            dv_ref[rows, :] = (dgv * dgv_dv).astype(BF16)

    tm = GMLP_BWD_CHUNKS * SGC
    return pl.pallas_call(
        body, grid=(T // tm,), name="gmlp_bwd",
        in_specs=[_row_spec(tm, D)] * 3 + [_const_spec((1, D))] * 2
        + [_const_spec((NG, SGC, SGC)), _const_spec((SGC, NG)), _row_spec(tm, D), _col_spec(D, tm),
           _const_spec((D, D))],
        out_specs=[_row_spec(tm, D)] * 3 + [_const_spec((1, D))] * 2
        + [_const_spec((NG, SGC, SGC)), _const_spec((SGC, NG)), _const_spec((D, D))],
        out_shape=[jax.ShapeDtypeStruct((T, D), BF16)] * 3 + [jax.ShapeDtypeStruct((1, D), F32)] * 2
        + [jax.ShapeDtypeStruct((NG, SGC, SGC), F32), jax.ShapeDtypeStruct((SGC, NG), F32),
           jax.ShapeDtypeStruct((D, D), BF16)],
        scratch_shapes=[pltpu.VMEM((D, D), F32)],
        compiler_params=_cparams(("arbitrary",), VMEM_BIG),
    )(u, v, gate, lng, lnb, wm_bf, sgb_t, dh, zt_bf, w_bf)


NCHIP = 4
NDEV = 8
ANY = pl.BlockSpec(memory_space=pl.ANY)


HBM = pl.BlockSpec(memory_space=pltpu.HBM)
SEM = pl.BlockSpec(memory_space=pltpu.SEMAPHORE)
EFFECT = pltpu.SideEffectType.DATAFLOW_SIDE_EFFECTING


CHIPS, EVERY, SIBLING = "chips", "every", "sibling"
SLOTS = {CHIPS: NCHIP, EVERY: NDEV, SIBLING: 1}


def _peers(scope):
    x, y, c = lax.axis_index("x"), lax.axis_index("y"), lax.axis_index("c")
    if scope == SIBLING:
        return [((x, y, 1 - c), 0)], 0
    if scope == CHIPS:
        return [((px, py, c), 2 * px + py) for px, py in ((1 - x, y), (x, 1 - y), (1 - x, 1 - y))], 2 * x + y
    out = []
    for j in range(1, NDEV):
        px, py, pc = x ^ (j >> 2), y ^ ((j >> 1) & 1), c ^ (j & 1)
        out.append(((px, py, pc), 4 * px + 2 * py + pc))
    return out, 4 * x + 2 * y + c


def _send_copies(src, land, send, recv, scatter, scope, starting):
    peers, me = _peers(scope)
    copies = []
    for t in range(len(src)):
        for j, (dev, slot) in enumerate(peers):
            k = t * len(peers) + j
            copies.append(pltpu.make_async_remote_copy(
                src_ref=src[t].at[slot] if scatter else src[t], dst_ref=land[t].at[me if starting else slot],
                send_sem=send.at[k], recv_sem=recv.at[k], device_id=dev, device_id_type=MESH))
    return copies


def _own_copies(src, land, sems, scatter, scope):
    if scope == SIBLING:
        return []
    _, me = _peers(scope)
    return [pltpu.make_async_copy(src[t].at[me] if scatter else src[t], land[t].at[me], sems.at[t])
            for t in range(len(src))]


def send_start(srcs, scatter, scope, name, after=None):
    n = len(srcs)
    slots = SLOTS[scope]
    extra_specs, extra = _after_operand(after)
    lands = [pltpu.HBM(a.shape if scatter else (slots,) + a.shape, a.dtype) for a in srcs]
    sems = [pltpu.SemaphoreType.DMA((n * max(slots - 1, 1),))] * 2 + ([] if scope == SIBLING else
                                                                     [pltpu.SemaphoreType.DMA((n,))])
    k = len(sems)

    def body(*refs):
        first_out = n + len(extra)
        src, land = refs[:n], refs[first_out + k + n:first_out + k + 2 * n]
        for cp in _send_copies(src, land, refs[first_out], refs[first_out + 1], scatter, scope, True):
            cp.start()
        for cp in _own_copies(src, land, refs[first_out + k - 1], scatter, scope):
            cp.start()
        refs[-1][...] = jnp.zeros_like(refs[-1])

    out = pl.pallas_call(
        body, name=name,
        out_shape=(*sems, *[pltpu.HBM(a.shape, a.dtype) for a in srcs], *lands, jax.ShapeDtypeStruct((8, 128), F32)),
        in_specs=[HBM] * n + extra_specs,
        out_specs=(*[SEM] * k, *[HBM] * (2 * n), pl.BlockSpec(memory_space=pltpu.VMEM)),
        input_output_aliases={i: k + i for i in range(n)},
        compiler_params=pltpu.CompilerParams(has_side_effects=EFFECT),
    )(*[pltpu.with_memory_space_constraint(a, pltpu.HBM) for a in srcs], *extra)
    return list(out[:k]), list(out[k:k + n]), list(out[k + n:k + 2 * n]), out[-1]


def send_wait(started, after, scatter, scope, name, with_sources=False, only=None):
    sems, srcs, lands, _ = started
    n, k = len(srcs), len(sems)
    wanted = range(n) if only is None else only

    def body(*refs):
        src, land = refs[:n], refs[n:2 * n]
        for t, cp in enumerate(_own_copies(src, land, refs[2 * n + k - 1], scatter, scope)):
            if t in wanted:
                cp.wait()
        copies = _send_copies(src, land, refs[2 * n], refs[2 * n + 1], scatter, scope, False)
        for i, cp in enumerate(copies):
            if i // (len(copies) // n) in wanted:
                cp.wait_send()
                cp.wait_recv()

    arrs = list(srcs) + list(lands)
    out = pl.pallas_call(
        body, name=name, out_shape=tuple(pltpu.HBM(a.shape, a.dtype) for a in arrs),
        in_specs=[HBM] * (2 * n) + [SEM] * k + [ANY], out_specs=tuple([HBM] * (2 * n)),
        input_output_aliases={i: i for i in range(2 * n)},
        compiler_params=pltpu.CompilerParams(has_side_effects=EFFECT),
    )(*arrs, *sems, after)
    return (list(out[:n]), list(out[n:])) if with_sources else list(out[n:])


def gather_weights(arrs, split):
    n = len(arrs)

    def body(*refs):
        ins, outs = refs[:n], refs[n:2 * n]
        send1, recv1, send2, recv2, loc_in, loc_out = refs[2 * n:2 * n + 6]
        staged = refs[2 * n + 6:]
        x, y, c = lax.axis_index("x"), lax.axis_index("y"), lax.axis_index("c")
        me = 2 * x + y
        sibling = (x, y, 1 - c)
        peers = [(1 - x, y), (x, 1 - y), (1 - x, 1 - y)]

        def rows_of(t, core):
            half = arrs[t].shape[0] // 2
            return pl.ds(core * half, half)

        def part(ref, t, core):
            return ref.at[rows_of(t, core)] if split[t] else ref

        load = [pltpu.make_async_copy(ins[t], staged[t], loc_in.at[t]) for t in range(n)]
        store = [pltpu.make_async_copy(staged[t], outs[t].at[me], loc_out.at[t]) for t in range(n)]
        for cp in load:
            cp.start()
        first = []
        for t in range(n):
            for j, (px, py) in enumerate(peers):
                first.append(pltpu.make_async_remote_copy(
                    src_ref=part(ins[t], t, c), dst_ref=part(outs[t].at[me], t, c), send_sem=send1.at[t, j],
                    recv_sem=recv1.at[t, j], device_id=(px, py, c), device_id_type=MESH))
        for cp in first:
            cp.start()
        for cp_in, cp_out in zip(load, store):
            cp_in.wait()
            cp_out.start()
        passed = []
        for t in range(n):
            for j, (px, py) in enumerate(peers):
                landed = part(outs[t].at[2 * px + py], t, c)
                pltpu.make_async_remote_copy(
                    src_ref=landed, dst_ref=landed, send_sem=send1.at[t, j], recv_sem=recv1.at[t, j],
                    device_id=(x, y, c), device_id_type=MESH).wait_recv()
                if split[t]:
                    cp = pltpu.make_async_remote_copy(
                        src_ref=landed, dst_ref=landed, send_sem=send2.at[t, j], recv_sem=recv2.at[t, j],
                        device_id=sibling, device_id_type=MESH)
                    cp.start()
                    passed.append(cp)
        for t in range(n):
            for j, (px, py) in enumerate(peers):
                if split[t]:
                    other = part(outs[t].at[2 * px + py], t, 1 - c)
                    pltpu.make_async_remote_copy(
                        src_ref=other, dst_ref=other, send_sem=send2.at[t, j], recv_sem=recv2.at[t, j],
                        device_id=(x, y, c), device_id_type=MESH).wait_recv()
        for cp in first + passed:
            cp.wait_send()
        for cp in store:
            cp.wait()

    return pl.pallas_call(
        body, name="gather_weights", in_specs=[ANY] * n, out_specs=[ANY] * n,
        out_shape=[jax.ShapeDtypeStruct((NCHIP,) + a.shape, a.dtype) for a in arrs],
        scratch_shapes=[pltpu.SemaphoreType.DMA((n, 3))] * 4 + [pltpu.SemaphoreType.DMA((n,))] * 2
        + [pltpu.VMEM(a.shape, a.dtype) for a in arrs],
    )(*arrs)


def _adam_math(g, w, m, v):
    m = ADAM_B1 * m + (1.0 - ADAM_B1) * g
    v = ADAM_B2 * v + (1.0 - ADAM_B2) * (g * g)
    m_hat = m / (1.0 - ADAM_B1 ** ADAM_STEP)
    v_hat = v / (1.0 - ADAM_B2 ** ADAM_STEP)
    delta = -ADAM_LR * (m_hat / (jnp.sqrt(v_hat) + ADAM_EPS) + ADAM_WD * w)
    return delta, m, v


def _rows_tile(rows):
    return rows if rows <= 256 else 256


def sum_chips(parts, name):
    _, rows, cols = parts.shape
    tr = _rows_tile(rows)

    def body(p_ref, o_ref):
        acc = p_ref[0].astype(F32)
        for s in range(1, NCHIP):
            acc = acc + p_ref[s].astype(F32)
        o_ref[...] = acc

    return pl.pallas_call(
        body, grid=(rows // tr,), name=name,
        in_specs=[pl.BlockSpec((NCHIP, tr, cols), lambda i: (0, i, 0))],
        out_specs=pl.BlockSpec((tr, cols), lambda i: (i, 0)),
        out_shape=jax.ShapeDtypeStruct((rows, cols), F32),
        compiler_params=_cparams(("parallel",)),
    )(parts)


def sum_chips_small(parts, name):
    n = len(parts)

    def body(*refs):
        for p_ref, o_ref in zip(refs[:n], refs[n:]):
            acc = p_ref[0]
            for s in range(1, NCHIP):
                acc = acc + p_ref[s]
            o_ref[...] = acc

    return pl.pallas_call(body, name=name, out_shape=[jax.ShapeDtypeStruct(p.shape[1:], F32) for p in parts])(*parts)


def adam_shard_small(items, name):
    n = len(items)

    def body(*refs):
        for t in range(n):
            a_ref, b_ref, w_ref, m_ref, v_ref = refs[5 * t:5 * t + 5]
            g_ref, d_ref, mo_ref, vo_ref = refs[5 * n + 4 * t:5 * n + 4 * t + 4]
            g = (a_ref[...] + b_ref[...]).reshape(w_ref.shape)
            g_ref[...] = g
            d_ref[...], mo_ref[...], vo_ref[...] = _adam_math(g, w_ref[...], m_ref[...], v_ref[...])

    out = pl.pallas_call(
        body, name=name, out_shape=[jax.ShapeDtypeStruct(it[2].shape, F32) for it in items for _ in range(4)],
    )(*[a for it in items for a in it])
    return [out[4 * t:4 * t + 4] for t in range(n)]


def adam_shard(p_mine, p_sib, w, m, v, name):
    rows, cols = p_mine.shape
    tr = _rows_tile(rows)
    lead = w.ndim == 3

    def body(a_ref, b_ref, w_ref, m_ref, v_ref, g_ref, d_ref, mo_ref, vo_ref):
        g = a_ref[...] + b_ref[...]
        g = g[None] if lead else g
        g_ref[...] = g
        d_ref[...], mo_ref[...], vo_ref[...] = _adam_math(g, w_ref[...], m_ref[...], v_ref[...])

    flat = pl.BlockSpec((tr, cols), lambda i: (i, 0))
    spec = pl.BlockSpec((1, tr, cols), lambda i: (0, i, 0)) if lead else flat
    return pl.pallas_call(
        body, grid=(rows // tr,), name=name, in_specs=[flat] * 2 + [spec] * 3, out_specs=[spec] * 4,
        out_shape=[jax.ShapeDtypeStruct(w.shape, F32)] * 4,
        compiler_params=_cparams(("parallel",)),
    )(p_mine, p_sib, w, m, v)


def adam_shard_halves_t(r_mine, r_sib, wt, mt, vt, name):
    hrows, cols = r_mine.shape
    tr = _rows_tile(hrows)
    per_half = hrows // tr

    def body(a_ref, b_ref, w_ref, m_ref, v_ref, g_ref, d_ref, mo_ref, vo_ref):
        mine = pl.program_id(0) == lax.axis_index("c")
        g = jnp.where(mine, a_ref[...], b_ref[...]).T[None]
        g_ref[...] = g
        d_ref[...], mo_ref[...], vo_ref[...] = _adam_math(g, w_ref[...], m_ref[...], v_ref[...])

    flat = pl.BlockSpec((tr, cols), lambda h, i: (i, 0))
    spec = pl.BlockSpec((1, cols, tr), lambda h, i: (0, 0, h * per_half + i))
    return pl.pallas_call(
        body, grid=(2, per_half), name=name, in_specs=[flat] * 2 + [spec] * 3, out_specs=[spec] * 4,
        out_shape=[jax.ShapeDtypeStruct(wt.shape, F32)] * 4,
        compiler_params=_cparams(("parallel", "parallel")),
    )(r_mine, r_sib, wt, mt, vt)


def adam_replicated(gathered, params, name):
    flat = []
    for i, p in enumerate(params):
        if isinstance(p, list):
            off = 0
            for wmv in p:
                n = gathered[i].shape[-1] - off if wmv[0] is None else wmv[0].shape[-1]
                flat.append((i, (off, n), wmv))
                off += n
        else:
            flat.append((i, None, p))
    ins = [a for _, _, wmv in flat for a in wmv if a is not None]
    ng = len(gathered)

    def body(*refs):
        g_refs = refs[:ng]
        in_refs = list(refs[ng:ng + len(ins)])
        out_refs = list(refs[ng + len(ins):])
        sums = []
        for r in g_refs:
            g = r[0]
            for d in range(1, NDEV):
                g = g + r[d]
            sums.append(g)
        for i, lanes, wmv in flat:
            g = sums[i] if lanes is None else sums[i][:, lanes[0]:lanes[0] + lanes[1]]
            out_refs.pop(0)[...] = g
            if wmv[0] is not None:
                w_ref, m_ref, v_ref = in_refs.pop(0), in_refs.pop(0), in_refs.pop(0)
                d_ref, mo_ref, vo_ref = out_refs.pop(0), out_refs.pop(0), out_refs.pop(0)
                d_ref[...], mo_ref[...], vo_ref[...] = _adam_math(g, w_ref[...], m_ref[...], v_ref[...])

    out_shape = []
    for i, lanes, wmv in flat:
        shape = gathered[i].shape[1:] if lanes is None else (1, lanes[1])
        out_shape += [jax.ShapeDtypeStruct(shape, F32)] * (4 if wmv[0] is not None else 1)
    outs = list(pl.pallas_call(body, name=name, out_shape=out_shape)(*gathered, *ins))
    return [[outs.pop(0) for _ in range(4 if wmv[0] is not None else 1)] for _, _, wmv in flat]


EVEN_SPLITS = (SHIFT, W, W, W, W, W)
ODD_SPLITS = (D, D, D)


def _cols_to_chips(a):
    rows, cols = a.shape
    return a.reshape(rows, NCHIP, cols // NCHIP).transpose(1, 0, 2)


def _chips_to_cols(a):
    _, rows, n = a.shape
    return a.transpose(1, 0, 2).reshape(rows, NCHIP * n)


def kernel(x, norm_g, w_in_e, shift_mu, rw_w0, rw_w2, rw_a0, rw_a2, rw_kk, rw_ka, rw_rk, rw_lnx_g, rw_lnx_b, att_bias, w_out_e, w_in_o, sg_ln_g, sg_ln_b, sg_w, sg_b, w_out_o, final_g, loss_target, m_norm_g, m_w_in_e, m_shift_mu, m_rw_w0, m_rw_w2, m_rw_a0, m_rw_a2, m_rw_kk, m_rw_ka, m_rw_rk, m_rw_lnx_g, m_rw_lnx_b, m_att_bias, m_w_out_e, m_w_in_o, m_sg_ln_g, m_sg_ln_b, m_sg_w, m_sg_b, m_w_out_o, m_final_g, v_norm_g, v_w_in_e, v_shift_mu, v_rw_w0, v_rw_w2, v_rw_a0, v_rw_a2, v_rw_kk, v_rw_ka, v_rw_rk, v_rw_lnx_g, v_rw_lnx_b, v_att_bias, v_w_out_e, v_w_in_o, v_sg_ln_g, v_sg_ln_b, v_sg_w, v_sg_b, v_w_out_o, v_final_g):
    x2 = x.reshape(T, D)
    tgt = loss_target.reshape(T, D)

    gathered = gather_weights(
        [jnp.swapaxes(w_in_e[0], 0, 1).astype(BF16), jnp.concatenate([rw_w2[0], rw_a2[0]], axis=0),
         jnp.concatenate([sg_ln_g, sg_ln_b], axis=0)], [True, True, False])
    wie = gathered[0].reshape(EVEN_IN, D)
    w2 = _chips_to_cols(gathered[1][:, :LORA])
    a2 = _chips_to_cols(gathered[1][:, LORA:])
    sglg = _chips_to_cols(gathered[2][:, 0:1])
    sglb = _chips_to_cols(gathered[2][:, 1:2])

    late = [w_out_e[0].astype(BF16), w_in_o[0].astype(BF16), w_out_o[0].astype(BF16)]
    late_started = send_start(late, False, CHIPS, "late_weights_start", after=gathered[0])

    late_state = {}

    def late_weights(layer, after):
        if layer == "even":
            srcs, lands = send_wait(late_started, after, False, CHIPS, "late_w_out_e_wait", True, only=(0,))
            late_state["rest"] = (late_started[0], srcs, lands, None)
            return lands[0].reshape(D, D)
        woe, wio, woo = send_wait(late_state["rest"], after, False, CHIPS, "late_weights_wait", only=(1, 2))
        return woe.reshape(D, D), wio, woo.reshape(D, D)

    def scatter_start(grads, name):
        return send_start([g_.astype(BF16) if g_.shape[-1] >= W else g_ for g_ in grads], True, CHIPS, name)

    started = {}

    def on_odd_grads(d_woo, d_wio):
        started["odd"] = scatter_start([d_woo.reshape(NCHIP, D // NCHIP, D), d_wio], "odd_grads_start")
        return started["odd"][-1]

    def on_even_grads(big_g):
        d_wie_half, d_woe, _, _, d_w2, d_a2, d_sglg, d_sglb = big_g
        blocks = [d_wie_half, d_woe.reshape(NCHIP, D // NCHIP, D), _cols_to_chips(d_w2), _cols_to_chips(d_a2),
                  _cols_to_chips(d_sglg), _cols_to_chips(d_sglb)]
        started["even"] = scatter_start(blocks, "even_grads_start")
        return started["even"][-1]

    def on_small_grads(layer, grads):
        if layer == "odd":
            d_sg_w, d_sg_b, d_final, d_g1 = grads
            mine = [d_sg_w.reshape(NG * SGC, SGC), d_sg_b, jnp.concatenate([d_final, d_g1], axis=1)]
        else:
            mine = [grads[-2], jnp.concatenate(grads[:-2] + grads[-1:], axis=1)]
        started[layer + "_small"] = send_start(mine, False, EVERY, layer + "_small_grads_start")
        return started[layer + "_small"][-1]

    loss_part, dx, _, _ = _local_step(
        x2, tgt, wie, late_weights, w2, a2, sglg, sglb, norm_g, shift_mu, rw_w0, rw_a0, rw_kk, rw_ka, rw_rk,
        rw_lnx_g, rw_lnx_b, att_bias, sg_w, sg_b, final_g, first_after=late_started[-1], on_odd_grads=on_odd_grads,
        on_even_grads=on_even_grads, on_small_grads=on_small_grads)
    wmv = {"w_in_e": tuple(jnp.swapaxes(a, 1, 2) for a in (w_in_e, m_w_in_e, v_w_in_e)),
           "w_out_e": (w_out_e, m_w_out_e, v_w_out_e),
           "w_in_o": (w_in_o, m_w_in_o, v_w_in_o), "w_out_o": (w_out_o, m_w_out_o, v_w_out_o),
           "rw_w2": (rw_w2, m_rw_w2, v_rw_w2), "rw_a2": (rw_a2, m_rw_a2, v_rw_a2),
           "sg_ln_g": (sg_ln_g, m_sg_ln_g, v_sg_ln_g), "sg_ln_b": (sg_ln_b, m_sg_ln_b, v_sg_ln_b)}
    sharded = {}

    def sum_and_swap(names, landed, tag):
        nbig = sum(p_.dtype == BF16 for p_ in landed)
        partial = [sum_chips(p_, "sum_" + nm) for p_, nm in zip(landed[:nbig], names)]
        if nbig < len(names):
            partial += sum_chips_small(landed[nbig:], "sum_small_" + tag)
        return send_start(partial, False, SIBLING, "swap_partials_" + tag + "_start")

    def update(names, swap_started, after, tag):
        partial, landed = send_wait(swap_started, after, False, SIBLING, "swap_partials_" + tag + "_wait", True)
        from_sibling = [a[0] for a in landed]
        nbig = sum(p_.shape[-1] >= W for p_ in partial)
        for nm, mine, sib in zip(names[:nbig], partial, from_sibling):
            if nm == "w_in_e":
                res = adam_shard_halves_t(mine, sib, *wmv[nm], "adam_" + nm)
                sharded[nm] = [jnp.swapaxes(a, 1, 2) for a in res]
            else:
                sharded[nm] = adam_shard(mine, sib, *wmv[nm], "adam_" + nm)
        if nbig < len(names):
            items = [(mine, sib, *wmv[nm]) for nm, mine, sib in zip(names, partial, from_sibling)][nbig:]
            for nm, res in zip(names[nbig:], adam_shard_small(items, "adam_small_" + tag)):
                sharded[nm] = res

    odd_names = ["w_out_o", "w_in_o"]
    even_names = ["w_in_e", "w_out_e", "rw_w2", "rw_a2", "sg_ln_g", "sg_ln_b"]
    odd_landed = send_wait(started["odd"], started["even_small"][-1], True, CHIPS, "odd_grads_wait")
    odd_swap = sum_and_swap(odd_names, odd_landed, "odd")
    done = odd_swap[-1]

    def wmv_of(*arrs, view=lambda a: a):
        return tuple(view(a) for a in arrs)

    vec = lambda a: a.reshape(1, -1)
    groups = {
        "odd": (["sg_w", "sg_b", "final_g", "norm_g1"],
                [wmv_of(sg_w, m_sg_w, v_sg_w, view=lambda a: a.reshape(NG * SGC, SGC)),
                 wmv_of(sg_b, m_sg_b, v_sg_b, view=lambda a: a[0]),
                 [wmv_of(final_g, m_final_g, v_final_g, view=vec),
                  wmv_of(norm_g, m_norm_g, v_norm_g, view=lambda a: a[1:2])]]),
        "even": (["att_bias", "norm_g0", "shift_mu", "rw_w0", "rw_a0", "rw_kk", "rw_ka", "rw_rk", "rw_lnx_g",
                  "rw_lnx_b", "loss"],
                 [wmv_of(att_bias, m_att_bias, v_att_bias, view=lambda a: a[0]),
                  [wmv_of(norm_g, m_norm_g, v_norm_g, view=lambda a: a[0:1]),
                   wmv_of(shift_mu, m_shift_mu, v_shift_mu), wmv_of(rw_w0, m_rw_w0, v_rw_w0),
                   wmv_of(rw_a0, m_rw_a0, v_rw_a0), wmv_of(rw_kk, m_rw_kk, v_rw_kk), wmv_of(rw_ka, m_rw_ka, v_rw_ka),
                   wmv_of(rw_rk, m_rw_rk, v_rw_rk, view=vec), wmv_of(rw_lnx_g, m_rw_lnx_g, v_rw_lnx_g),
                   wmv_of(rw_lnx_b, m_rw_lnx_b, v_rw_lnx_b), (None, None, None)]]),
    }
    rep = {}
    for layer in ("odd", "even"):
        nms, params = groups[layer]
        gathered_g = send_wait(started[layer + "_small"], done, False, EVERY, layer + "_small_grads_wait")
        for nm, res in zip(nms, adam_replicated(gathered_g, params, "adam_" + layer + "_small")):
            rep[nm] = res
        done = rep[nms[0]][0]
    native = {"sg_w": sg_w.shape, "sg_b": sg_b.shape, "final_g": final_g.shape, "rw_rk": rw_rk.shape,
              "att_bias": att_bias.shape}
    for nm, shape in native.items():
        rep[nm] = [a.reshape(shape) for a in rep[nm]]
    rep["norm_g"] = [jnp.concatenate([a, b], axis=0) for a, b in zip(rep["norm_g0"], rep["norm_g1"])]
    even_landed = send_wait(started["even"], done, True, CHIPS, "even_grads_wait")
    even_swap = sum_and_swap(even_names, even_landed, "even")
    update(odd_names, odd_swap, even_swap[-1], "odd")
    update(even_names, even_swap, sharded["w_in_o"][0], "even")

    order = ["norm_g", "w_in_e", "shift_mu", "rw_w0", "rw_w2", "rw_a0", "rw_a2", "rw_kk", "rw_ka", "rw_rk",
             "rw_lnx_g", "rw_lnx_b", "att_bias", "w_out_e", "w_in_o", "sg_ln_g", "sg_ln_b", "sg_w", "sg_b",
             "w_out_o", "final_g"]
    results = {**sharded, **rep}
    outs = [rep["loss"][0][0, 0], dx.reshape(NSEQ, SEQ, D)]
    for kind in range(4):
        outs += [results[nm][kind] for nm in order]
    return tuple(outs)


def _local_step(x2, tgt, wie_t, late_weights, w2, a2, sglg, sglb, norm_g, shift_mu, rw_w0, rw_a0, rw_kk, rw_ka, rw_rk,
                rw_lnx_g, rw_lnx_b, att_bias, sg_w, sg_b, final_g, first_after=None, on_odd_grads=None,
                on_even_grads=None, on_small_grads=None):
    zl = jnp.zeros((LORA, W), F32)
    w2x = jnp.concatenate([w2, zl], axis=0)
    a2x = jnp.concatenate([zl, a2], axis=0)
    rk = rw_rk.reshape(1, W)
    pos = np.arange(SGC)
    sg_mask = jnp.asarray(((pos[None, :] // L) <= (pos[:, None] // L)).astype(np.float32))
    wm = (sg_w[0] * sg_mask[None]).astype(BF16)
    sgb_t = sg_b[0].T

    xn0, ps, ga, q, kb, vb, gb = ln_in_proj(x2, norm_g[0:1], wie_t, EVEN_SPLITS, "in_proj_even", after=first_after,
                                            w_t=True, bf16_pieces=(2, 3, 4))
    r, lw, k2, v, aa, bb = even_prep(ps, shift_mu, rw_w0, w2x, rw_a0, a2x, rw_kk, rw_ka)
    y, rw_saved = rwkv_fwd(r, lw, k2, v, aa, bb)
    bias = bias_expand(att_bias[0])

    o = attention_fwd(q, kb, vb, bias)
    woe = late_weights("even", o)
    h1, zt = even_post(y, r, k2, v, ga, o, gb, rw_lnx_g, rw_lnx_b, rk, x2, woe)
    woe, wio, woo = late_weights("odd", h1)
    xn1, u, vv, gt = ln_in_proj(h1, norm_g[1:2], wio, ODD_SPLITS, "in_proj_odd")
    dh2, loss_part, d_final_g, z2t = gmlp_fwd_loss(u, vv, gt, sglg, sglb, wm, sgb_t, h1, woo, final_g[None], tgt)

    du, dvv, dgt, d_sglg, d_sglb, d_wm, d_sgb_t, d_woo = gmlp_bwd(u, vv, gt, sglg, sglb, wm, sgb_t, dh2, z2t, woo)
    dp_odd = [du, dvv, dgt]
    d_wio = matmul_acc_chips(xn1, dp_odd, "in_proj_odd_dw")
    token = on_odd_grads(d_woo, d_wio) if on_odd_grads else None
    dh1, d_g1 = in_proj_bwd_x(h1, norm_g[1:2], wio, dp_odd, dh2, "in_proj_odd_bwd", after=token)
    odd_small = [d_wm * sg_mask[None], d_sgb_t.T, d_final_g, d_g1]
    token = on_small_grads("odd", odd_small) if on_small_grads else None
    dy, dr2, dk22, dv2, dga, do, dgb, d_lng, d_lnb, d_rk, d_woe = even_post_bwd(
        y, r, k2, v, ga, o, gb, rw_lnx_g, rw_lnx_b, rk, dh1, zt, woe, after=token)
    dq, dkb, dvb, dbias = attention_bwd(q, kb, vb, bias, do)
    dbias = sum(dbias[:, i * 2 * L:(i + 1) * 2 * L, i * L:i * L + BAND] for i in range(ATT_Q))
    d_att_bias = bias_grad(dbias.reshape(NH, L, BAND))
    dr, dlw, dk2, dv, daa, dbb = rwkv_bwd(r, lw, k2, aa, bb, rw_saved, dy)
    dps, d_mu, d_w0, d_w2x, d_a0, d_a2x, d_kk, d_ka = even_prep_bwd(
        ps, shift_mu, rw_w0, w2x, rw_a0, a2x, rw_kk, rw_ka, dr, dlw, dk2, dv, daa, dbb, dr2, dk22, dv2)
    dp_even = [dps, dga, dq, dkb, dvb, dgb]
    d_wie = matmul_acc_chips(xn0, dp_even, "in_proj_even_dw", add_cores=on_even_grads is not None)
    big_g = (d_wie, d_woe, d_wio, d_woo, d_w2x[:LORA], d_a2x[LORA:], d_sglg, d_sglb)
    token = on_even_grads(big_g) if on_even_grads else None
    dx, d_g0 = in_proj_bwd_x(x2, norm_g[0:1], wie_t, dp_even, dh1, "in_proj_even_bwd", after=token, w_t=True)
    even_small = [d_g0, d_mu, d_w0, d_a0, d_kk, d_ka, d_rk, d_lng, d_lnb, d_att_bias]
    if on_small_grads:
        on_small_grads("even", even_small + [loss_part[0:1, :]])
    rep_g = [jnp.concatenate([d_g0, d_g1], axis=0)] + even_small[1:] + odd_small[:3]
    return loss_part[0, 0], dx, big_g, rep_g
```

```python
import functools
import math

import jax
import jax.numpy as jnp
import numpy as np
from jax import lax
from jax.experimental import pallas as pl
from jax.experimental.pallas import tpu as pltpu

F32 = jnp.float32
BF16 = jnp.bfloat16
HI = lax.Precision.HIGHEST

D = 1024
SEQ = 2048
NSEQ = 2
T = NSEQ * SEQ
HD = 64
NH = 8
W = 512
SHIFT = 1664
LORA = 64
EVEN_IN = 4224
ODD_IN = 3072
L = 64
NC = SEQ // L
LEFT = 8
BAND = (LEFT + 1) * L
CLIP = 128
SGC = 128
NG = 8
RMS_EPS = 1e-6
LN_EPS = 1e-5
GN_EPS = 64e-5
NEG = -1e30
VMEM_BIG = 56 * 1024 * 1024

ADAM_LR = 0.001
ADAM_B1 = 0.9
ADAM_B2 = 0.999
ADAM_EPS = 1e-08
ADAM_WD = 0.01
ADAM_STEP = 10

MESH = pl.DeviceIdType.MESH


def _bdot(a, b):
    return jnp.dot(a.astype(BF16), b.astype(BF16), preferred_element_type=F32)


def _bdot_nt(a, b):
    return lax.dot_general(a.astype(BF16), b.astype(BF16), (((1,), (1,)), ((), ())), preferred_element_type=F32)


def _bdot_tn(a, b):
    return lax.dot_general(a.astype(BF16), b.astype(BF16), (((0,), (0,)), ((), ())), preferred_element_type=F32)


def _hdot(a, b):
    return jnp.dot(a, b, precision=HI, preferred_element_type=F32)


def _hdot_nt(a, b):
    return lax.dot_general(a, b, (((1,), (1,)), ((), ())), precision=HI, preferred_element_type=F32)


def _hdot_tn(a, b):
    return lax.dot_general(a, b, (((0,), (0,)), ((), ())), precision=HI, preferred_element_type=F32)


def _iota2(shape, dim):
    return lax.broadcasted_iota(jnp.int32, shape, dim)


def _head_blockdiag():
    r = _iota2((2 * HD, 2 * HD), 0) // HD
    c = _iota2((2 * HD, 2 * HD), 1) // HD
    return (r == c).astype(BF16)


def _headsum_impl(x, bd):
    hi = x.astype(BF16)
    mid = (x - hi.astype(F32)).astype(BF16)
    n = bd.shape[0]
    out = [jnp.dot(hi[:, i:i + n], bd, preferred_element_type=F32) + jnp.dot(mid[:, i:i + n], bd, preferred_element_type=F32)
           for i in range(0, x.shape[1], n)]
    return jnp.concatenate(out, axis=-1)


@jax.custom_vjp
def _headsum(x, bd):
    return _headsum_impl(x, bd)


def _headsum_fwd(x, bd):
    return _headsum_impl(x, bd), bd


def _headsum_bwd(bd, ct):
    return _headsum_impl(ct, bd), None


_headsum.defvjp(_headsum_fwd, _headsum_bwd)


def _silu(x):
    return x * jax.nn.sigmoid(x)


_GELU_C = math.sqrt(2.0 / math.pi)


def _gelu(x):
    return 0.5 * x * (1.0 + jnp.tanh(_GELU_C * (x + 0.044715 * (x * x * x))))


def _silu_both(x):
    s = jax.nn.sigmoid(x)
    xs = x * s
    return xs, s + xs * (1.0 - s)


def _gelu_both(x):
    x2 = x * x
    t = jnp.tanh(_GELU_C * (x + 0.044715 * (x2 * x)))
    half = 0.5 * (1.0 + t)
    return x * half, half + 0.5 * x * (1.0 - t * t) * _GELU_C * (1.0 + 3.0 * 0.044715 * x2)


def _softplus(x):
    return jnp.maximum(x, 0.0) + jnp.log(1.0 + jnp.exp(-jnp.abs(x)))


def _cparams(sem, vmem=None):
    return pltpu.CompilerParams(dimension_semantics=sem, vmem_limit_bytes=vmem)


def _row_spec(tm, width):
    return pl.BlockSpec((tm, width), lambda i: (i, 0))


def _col_spec(height, tm):
    return pl.BlockSpec((height, tm), lambda i: (0, i))


def _const_spec(shape):
    nd = len(shape)
    return pl.BlockSpec(shape, lambda *_: (0,) * nd)


def _weight_dims(w_bf, w_t):
    if w_bf.ndim == 3:
        return None, w_bf.shape[0] * w_bf.shape[2]
    return (((1,), (1,)), ((), ())) if w_t else (((1,), (0,)), ((), ())), w_bf.shape[0 if w_t else 1]


def _proj(xn, w_ref, dims):
    if dims is None:
        return jnp.concatenate([jnp.dot(xn, w_ref[s], preferred_element_type=F32) for s in range(w_ref.shape[0])],
                               axis=-1)
    return lax.dot_general(xn, w_ref[...], dims, preferred_element_type=F32)


def _proj_back(dp, w_ref, w_t):
    nt = (((1,), (1,)), ((), ()))
    if len(w_ref.shape) == 3:
        nb = w_ref.shape[2]
        parts = [lax.dot_general(dp[:, s * nb:(s + 1) * nb], w_ref[s], nt, preferred_element_type=F32)
                 for s in range(w_ref.shape[0])]
        return sum(parts[1:], parts[0])
    return lax.dot_general(dp, w_ref[...], (((1,), (0,)), ((), ())) if w_t else nt, preferred_element_type=F32)


def ln_in_proj(x, g, w_bf, splits, name, after=None, w_t=False, bf16_pieces=()):
    dims, n = _weight_dims(w_bf, w_t)
    dtypes = [BF16 if i in bf16_pieces else F32 for i in range(len(splits))]
    tm = 512 if n <= ODD_IN else 256
    spans = []
    o = 0
    for s in splits:
        spans.append((o, o + s))
        o += s
    assert o == n
    extra_specs, extra = _after_operand(after)

    def body(x_ref, g_ref, w_ref, *rest):
        xn_ref, outs = rest[len(extra)], rest[len(extra) + 1:]
        xv = x_ref[...]
        rstd = lax.rsqrt(jnp.mean(xv * xv, axis=-1, keepdims=True) + RMS_EPS)
        xn = (xv * rstd * g_ref[...]).astype(BF16)
        xn_ref[...] = xn.T
        p = _proj(xn, w_ref, dims)
        for o_ref, (a, b) in zip(outs, spans):
            o_ref[...] = p[:, a:b].astype(o_ref.dtype)

    return pl.pallas_call(
        body, grid=(T // tm,), name=name,
        in_specs=[_row_spec(tm, D), _const_spec((1, D)), _const_spec(w_bf.shape)] + extra_specs,
        out_specs=[_col_spec(D, tm)] + [_row_spec(tm, s) for s in splits],
        out_shape=[jax.ShapeDtypeStruct((D, T), BF16)]
        + [jax.ShapeDtypeStruct((T, s), dt) for s, dt in zip(splits, dtypes)],
        compiler_params=_cparams(("parallel",), VMEM_BIG),
    )(x, g, w_bf, *extra)


def in_proj_bwd_x(x, g, w_bf, dps, dres, name, after=None, w_t=False):
    tm = 512
    widths = [d.shape[1] for d in dps]
    extra_specs, extra = _after_operand(after)

    def body(x_ref, g_ref, w_ref, dres_ref, *rest):
        dp_refs = rest[:len(widths)]
        dx_ref, dg_ref = rest[-2:]
        dp = jnp.concatenate([r[...] for r in dp_refs], axis=-1)
        dxn = _proj_back(dp, w_ref, w_t)
        xv = x_ref[...]
        rstd = lax.rsqrt(jnp.mean(xv * xv, axis=-1, keepdims=True) + RMS_EPS)
        xhat = xv * rstd
        dgp = jnp.sum(dxn * xhat, axis=0, keepdims=True)

        @pl.when(pl.program_id(0) == 0)
        def _():
            dg_ref[...] = jnp.zeros_like(dg_ref)

        dg_ref[...] += dgp
        dxh = dxn * g_ref[...]
        dx_ref[...] = dres_ref[...] + rstd * (dxh - xhat * jnp.mean(dxh * xhat, axis=-1, keepdims=True))

    return pl.pallas_call(
        body, grid=(T // tm,), name=name,
        in_specs=[_row_spec(tm, D), _const_spec((1, D)), _const_spec(w_bf.shape), _row_spec(tm, D)]
        + [_row_spec(tm, s) for s in widths] + extra_specs,
        out_specs=[_row_spec(tm, D), _const_spec((1, D))],
        out_shape=[jax.ShapeDtypeStruct((T, D), F32), jax.ShapeDtypeStruct((1, D), F32)],
        compiler_params=_cparams(("arbitrary",), VMEM_BIG),
    )(x, g, w_bf, dres, *dps, *extra)


def _after_operand(after):
    return ([ANY], [after]) if after is not None else ([], [])


def matmul_acc_chips(at_bf, pieces, name, after=None, add_cores=False):
    k = at_bf.shape[0]
    widths = [p.shape[1] for p in pieces]
    nb = sum(widths) // NCHIP
    tm = 512
    steps = T // tm
    half = k // 2
    extra_specs, extra = _after_operand(after)

    def body(a_ref, *rest):
        o_ref, acc = rest[len(widths) + len(extra):][:2]

        @pl.when(pl.program_id(0) == 0)
        def _():
            acc[...] = jnp.zeros_like(acc)

        a = a_ref[...]
        b = jnp.concatenate([r[...] for r in rest[:len(widths)]], axis=-1)
        for s in range(NCHIP):
            acc[s] += jnp.dot(a, b[:, s * nb:(s + 1) * nb], preferred_element_type=F32)

        @pl.when(pl.program_id(0) == steps - 1)
        def _():
            if not add_cores:
                o_ref[...] = acc[...].astype(BF16)
            else:
                give, got, send, recv = rest[-4:]
                x, y, c = lax.axis_index("x"), lax.axis_index("y"), lax.axis_index("c")
                theirs = pl.multiple_of((1 - c) * half, half)
                mine = pl.multiple_of(c * half, half)
                copies = []
                for s in range(NCHIP):
                    give[s] = acc[s, pl.ds(theirs, half), :].astype(BF16)
                    copies.append(pltpu.make_async_remote_copy(
                        src_ref=give.at[s], dst_ref=got.at[s], send_sem=send.at[s], recv_sem=recv.at[s],
                        device_id=(x, y, 1 - c), device_id_type=MESH))
                    copies[s].start()
                for s in range(NCHIP):
                    copies[s].wait()
                    o_ref[s] = (acc[s, pl.ds(mine, half), :] + got[s].astype(F32)).astype(BF16)

    out_rows = half if add_cores else k
    exchange = ([pltpu.VMEM((NCHIP, half, nb), BF16)] * 2 + [pltpu.SemaphoreType.DMA((NCHIP,))] * 2 if add_cores
                else [])
    return pl.pallas_call(
        body, grid=(steps,), name=name,
        in_specs=[_col_spec(k, tm)] + [_row_spec(tm, w_) for w_ in widths] + extra_specs,
        out_specs=_const_spec((NCHIP, out_rows, nb)),
        out_shape=jax.ShapeDtypeStruct((NCHIP, out_rows, nb), BF16),
        scratch_shapes=[pltpu.VMEM((NCHIP, k, nb), F32)] + exchange,
        compiler_params=_cparams(("arbitrary",), VMEM_BIG),
    )(at_bf, *pieces, *extra)


def _out_proj_back(dh_ref, zt_ref, w_ref, dw_ref, acc_ref):
    dhb = dh_ref[...].astype(BF16)

    @pl.when(pl.program_id(0) == 0)
    def _():
        acc_ref[...] = jnp.zeros_like(acc_ref)

    acc_ref[...] += jnp.dot(zt_ref[...], dhb, preferred_element_type=F32)

    @pl.when(pl.program_id(0) == pl.num_programs(0) - 1)
    def _():
        dw_ref[...] = acc_ref[...].astype(dw_ref.dtype)

    return lax.dot_general(dhb, w_ref[...], (((1,), (1,)), ((), ())), preferred_element_type=F32)


PREP_TM = 512
PREP_NB = SEQ // PREP_TM


def _prep_elem(k, wl, apre, kkw, kaw, bd):
    wraw = -_softplus(-wl) - 0.5
    lw = -jnp.exp(wraw)
    asig = jax.nn.sigmoid(apre)
    kkr = k * kkw
    nrm = jnp.maximum(jnp.sqrt(_headsum(kkr * kkr, bd)), 1e-12)
    kk = kkr / nrm
    k2 = k * (1.0 + (asig - 1.0) * kaw)
    return lw, k2, -kk, kk * asig


def _prep_elem_bwd(k, wl, apre, kkw, kaw, bd, dlw, dk2, daa, dbb):
    s = -wl
    sp = _softplus(s)
    dwl = dlw * (-jnp.exp(-sp - 0.5)) * jnp.exp(s - sp)
    asig = jax.nn.sigmoid(apre)
    kkr = k * kkw
    root = jnp.sqrt(_headsum(kkr * kkr, bd))
    inv = 1.0 / jnp.maximum(root, 1e-12)
    kk = kkr * inv
    dkk = dbb * asig - daa
    dap = (dbb * kk + dk2 * k * kaw) * asig * (1.0 - asig)
    through_norm = jnp.where(root > 1e-12, kk * _headsum(dkk * kkr, bd) * inv, 0.0)
    dkkr = inv * (dkk - through_norm)
    gain = 1.0 + (asig - 1.0) * kaw
    dk = dkkr * kkw + dk2 * gain
    dkkw = jnp.sum(dkkr * k, axis=0, keepdims=True)
    dkaw = jnp.sum(dk2 * k * (asig - 1.0), axis=0, keepdims=True)
    return dk, dwl, dap, dkkw, dkaw


def _shifted(ps_ref, prev_ref, mu, blk):
    p = ps_ref[...]
    first = (blk % PREP_NB) == 0
    prev_row = jnp.where(first, 0.0, prev_ref[7:8, :])
    rolled = pltpu.roll(p, 1, 0)
    p_prev = jnp.where(_iota2(p.shape, 0) == 0, prev_row, rolled)
    return p, p_prev, p + (p_prev - p) * mu


def _prev_spec(width, blk_of):
    return pl.BlockSpec((8, width), lambda i: (jnp.maximum(blk_of(i) * (PREP_TM // 8) - 1, 0), 0))


def even_prep(ps, mu, w0, w2x, a0, a2x, kkw, kaw):
    tm = PREP_TM

    def body(ps_ref, prev_ref, mu_ref, w0_ref, w2_ref, a0_ref, a2_ref, kk_ref, ka_ref,
             r_ref, lw_ref, k2_ref, v_ref, aa_ref, bb_ref):
        _, _, s = _shifted(ps_ref, prev_ref, mu_ref[...], pl.program_id(0))
        wa = s[:, 3 * W:]
        wl = w0_ref[...] + _bdot(jnp.tanh(wa), w2_ref[...])
        apre = a0_ref[...] + _bdot(wa, a2_ref[...])
        lw, k2, aa, bb = _prep_elem(s[:, W:2 * W], wl, apre, kk_ref[...], ka_ref[...], _head_blockdiag())
        r_ref[...] = s[:, 0:W]
        v_ref[...] = s[:, 2 * W:3 * W]
        lw_ref[...] = lw
        k2_ref[...] = k2
        aa_ref[...] = aa
        bb_ref[...] = bb

    vec = _const_spec((1, W))
    return pl.pallas_call(
        body, grid=(T // tm,), name="even_prep",
        in_specs=[_row_spec(tm, SHIFT), _prev_spec(SHIFT, lambda i: i), _const_spec((1, SHIFT)), vec,
                  _const_spec((2 * LORA, W)), vec, _const_spec((2 * LORA, W)), vec, vec],
        out_specs=[_row_spec(tm, W)] * 6,
        out_shape=[jax.ShapeDtypeStruct((T, W), F32)] * 6,
        compiler_params=_cparams(("parallel",), VMEM_BIG),
    )(ps, ps, mu, w0, w2x, a0, a2x, kkw, kaw)


def even_prep_bwd(ps, mu, w0, w2x, a0, a2x, kkw, kaw, dr, dlw, dk2, dv, daa, dbb, dr2, dk22, dv2):
    tm = PREP_TM
    nb = T // tm
    rev = lambda i: nb - 1 - i

    def body(ps_ref, prev_ref, mu_ref, w0_ref, w2_ref, a0_ref, a2_ref, kk_ref, ka_ref,
             dr_ref, dlw_ref, dk2_ref, dv_ref, daa_ref, dbb_ref, dr2_ref, dk22_ref, dv2_ref,
             dps_ref, dmu_ref, dw0_ref, dw2_ref, da0_ref, da2_ref, dkk_ref, dka_ref, carry):
        i = pl.program_id(0)
        blk = rev(i)
        mu_v = mu_ref[...]
        p, p_prev, s = _shifted(ps_ref, prev_ref, mu_v, blk)
        wa = s[:, 3 * W:]
        th = jnp.tanh(wa)
        wl = w0_ref[...] + _bdot(th, w2_ref[...])
        apre = a0_ref[...] + _bdot(wa, a2_ref[...])
        bd = _head_blockdiag()
        k = s[:, W:2 * W]
        dk, dwl, dap, dkkw, dkaw = _prep_elem_bwd(k, wl, apre, kk_ref[...], ka_ref[...], bd, dlw_ref[...],
                                                  dk2_ref[...] + dk22_ref[...], daa_ref[...], dbb_ref[...])
        dwa = _bdot_nt(dwl, w2_ref[...]) * (1.0 - th * th) + _bdot_nt(dap, a2_ref[...])
        ds = jnp.concatenate([dr_ref[...] + dr2_ref[...], dk, dv_ref[...] + dv2_ref[...], dwa], axis=-1)

        @pl.when(i == 0)
        def _():
            for ref in (dmu_ref, dw0_ref, dw2_ref, da0_ref, da2_ref, dkk_ref, dka_ref, carry):
                ref[...] = jnp.zeros_like(ref)

        dmu_ref[...] += jnp.sum(ds * (p_prev - p), axis=0, keepdims=True)
        dw0_ref[...] += jnp.sum(dwl, axis=0, keepdims=True)
        da0_ref[...] += jnp.sum(dap, axis=0, keepdims=True)
        dw2_ref[...] += _bdot_tn(th, dwl)
        da2_ref[...] += _bdot_tn(wa, dap)
        dkk_ref[...] += dkkw
        dka_ref[...] += dkaw
        dsm = ds * mu_v
        last = (blk % PREP_NB) == PREP_NB - 1
        nxt = jnp.where(last, 0.0, carry[0:1, :])
        up = pltpu.roll(dsm, tm - 1, 0)
        up = jnp.where(_iota2(up.shape, 0) == tm - 1, nxt, up)
        dps_ref[...] = (ds - dsm + up).astype(BF16)
        carry[0:1, :] = dsm[0:1, :]

    vec = _const_spec((1, W))
    rrow = lambda width: pl.BlockSpec((tm, width), lambda i: (rev(i), 0))
    return pl.pallas_call(
        body, grid=(nb,), name="even_prep_bwd",
        in_specs=[rrow(SHIFT), _prev_spec(SHIFT, rev), _const_spec((1, SHIFT)), vec,
                  _const_spec((2 * LORA, W)), vec, _const_spec((2 * LORA, W)), vec, vec] + [rrow(W)] * 9,
        out_specs=[rrow(SHIFT), _const_spec((1, SHIFT)), vec, _const_spec((2 * LORA, W)), vec,
                   _const_spec((2 * LORA, W)), vec, vec],
        out_shape=[jax.ShapeDtypeStruct((T, SHIFT), BF16), jax.ShapeDtypeStruct((1, SHIFT), F32),
                   jax.ShapeDtypeStruct((1, W), F32), jax.ShapeDtypeStruct((2 * LORA, W), F32),
                   jax.ShapeDtypeStruct((1, W), F32), jax.ShapeDtypeStruct((2 * LORA, W), F32),
                   jax.ShapeDtypeStruct((1, W), F32), jax.ShapeDtypeStruct((1, W), F32)],
        scratch_shapes=[pltpu.VMEM((8, SHIFT), F32)],
        compiler_params=_cparams(("arbitrary",), VMEM_BIG),
    )(ps, ps, mu, w0, w2x, a0, a2x, kkw, kaw, dr, dlw, dk2, dv, daa, dbb, dr2, dk22, dv2)


NPAIR = NH // 2
PW = 2 * HD


def _pair_cols(p):
    return slice(p * PW, (p + 1) * PW)


def _pairs(a):
    return [a[:, _pair_cols(p)] for p in range(NPAIR)]


def _stack_pair(a):
    first = _iota2(a.shape, 1) < HD
    zero = jnp.zeros_like(a)
    return jnp.concatenate([jnp.where(first, a, zero), jnp.where(first, zero, a)], axis=0)


def _unstack_pair(a):
    n = a.shape[0] // 2
    return jnp.where(_iota2((n, PW), 1) < HD, a[:n], a[n:])


def _fold_pair(a):
    n = a.shape[0] // 2
    return a[:n] + a[n:]


def _chunk_masks():
    n = 4 * L
    row = _iota2((n, n), 0)
    col = _iota2((n, n), 1)
    same = ((row // L) & 1) == ((col // L) & 1)
    ri = row & (L - 1)
    ci = col & (L - 1)
    keep = same & (((row < 2 * L) & (ri > ci)) | ((row >= 2 * L) & (ri >= ci)))
    r1 = _iota2((L, L), 0)
    c1 = _iota2((L, L), 1)
    r2 = _iota2((2 * L, 2 * L), 0)
    c2 = _iota2((2 * L, 2 * L), 1)
    return keep.astype(F32), (r1 >= c1).astype(F32), (r2 == c2).astype(F32)


def _scaled(r, lw, k2, aa, bb, tri):
    g = _hdot(tri, lw)
    eg = jnp.exp(g)
    eng = jnp.exp(-g)
    egp = jnp.exp(g - lw)
    return eg, eng, egp, aa * egp, r * eg, bb * eng, k2 * eng


def _head_cols(h):
    return slice(h * HD, (h + 1) * HD)


def _per_head(a):
    return [a[:, _head_cols(h)] for h in range(NH)]


def _pairs_operands(at, rt, bt, kt):
    x = [jnp.concatenate([_stack_pair(a), _stack_pair(r)], axis=0).astype(BF16) for a, r in zip(_pairs(at), _pairs(rt))]
    yk = [jnp.concatenate([_stack_pair(b), _stack_pair(k)], axis=0).astype(BF16) for b, k in zip(_pairs(bt), _pairs(kt))]
    return x, yk


def _pairs_matrices(x, yk, keep, eye):
    m = [_bdot_nt(a, b) * keep for a, b in zip(x, yk)]
    p = [a[:2 * L, :2 * L] for a in m]
    tinv = [eye + a for a in p]
    for _ in range(5):
        p = [_bdot(a, a) for a in p]
        tinv = [t + _bdot(t, a) for t, a in zip(tinv, p)]
    return [a.astype(BF16) for a in m], [a.astype(BF16) for a in tinv]


def _pairs_fwd(x, yk, m, tinv, vw, s0, egl):
    xh = [_bdot_nt(a, s) for a, s in zip(x, s0)]
    u = [_bdot(t, h[:2 * L] + _bdot(a[:2 * L, 2 * L:], w)) for t, h, a, w in zip(tinv, xh, m, vw)]
    uv = [jnp.concatenate([a, w], axis=0).astype(BF16) for a, w in zip(u, vw)]
    y = [h[2 * L:] + _bdot(a[2 * L:], w) for h, a, w in zip(xh, m, uv)]
    sn = [e * (s + _bdot_tn(w, b)) for e, s, w, b in zip(egl, s0, uv, yk)]
    return y, sn, uv


def _pairs_bwd(x, yk, m, tinv, uv, s0, sn, egl, dyw, dsn, keep):
    dzs = [d * e for d, e in zip(dsn, egl)]
    dgl = [jnp.sum(d * s, axis=0, keepdims=True) for d, s in zip(dsn, sn)]
    dyb = [a.astype(BF16) for a in dyw]
    t1 = [_bdot_tn(a[2 * L:], d) for a, d in zip(m, dyb)]
    t2 = [_bdot_nt(b, d) for b, d in zip(yk, dzs)]
    drhs = [_bdot_tn(t, a[:2 * L] + b[:2 * L]) for t, a, b in zip(tinv, t1, t2)]
    dv = [a[2 * L:] + b[2 * L:] + _bdot_tn(c[:2 * L, 2 * L:], d) for a, b, c, d in zip(t1, t2, m, drhs)]
    gg = [jnp.concatenate([a, b], axis=0).astype(BF16) for a, b in zip(drhs, dyw)]
    ds0 = [d + _bdot_tn(g, a) for d, g, a in zip(dzs, gg, x)]
    dm = [_bdot_nt(g, w) * keep for g, w in zip(gg, uv)]
    dx = [_bdot(g, s) + _bdot(d, b) for g, s, d, b in zip(gg, s0, dm, yk)]
    dyk = [_bdot_tn(d, a) + _bdot(w, z) for d, a, w, z in zip(dm, x, uv, dzs)]
    return dx, dyk, dv, dgl, ds0


STATE_SHAPE = (NPAIR * PW, PW)
M_SHAPE = (4 * L, NPAIR * 4 * L)
TINV_SHAPE = (2 * L, NPAIR * 2 * L)


def _rows_of(a, n):
    return [a[i * n:(i + 1) * n, :] for i in range(NPAIR)]


def _both(f):
    out = []
    for s in range(NSEQ):
        out += f(s)
    return out


def _seq_view(a):
    return a.reshape(NSEQ, SEQ, a.shape[-1])


UV_SHAPE = (4 * L, NPAIR * PW)
RW_CHUNKS = 2


def rwkv_fwd(r, lw, k2, v, aa, bb):
    def body(r_ref, lw_ref, k2_ref, v_ref, aa_ref, bb_ref, y_ref, hs_ref, hn_ref, m_ref, t_ref, uv_ref, state):
        @pl.when(pl.program_id(0) == 0)
        def _():
            state[...] = jnp.zeros_like(state)

        keep, tri, eye = _chunk_masks()
        where = [(j, s) for j in range(RW_CHUNKS) for s in range(NSEQ)]
        rows = lambda j: slice(j * L, (j + 1) * L)
        sc = [_scaled(r_ref[s, rows(j)], lw_ref[s, rows(j)], k2_ref[s, rows(j)], aa_ref[s, rows(j)],
                      bb_ref[s, rows(j)], tri) for j, s in where]
        ops = [_pairs_operands(*a[3:]) for a in sc]
        m, tinv = _pairs_matrices([a for o in ops for a in o[0]], [a for o in ops for a in o[1]], keep, eye)
        s_cur = [state[s] for s in range(NSEQ)]
        for j in range(RW_CHUNKS):
            mine = slice(j * NSEQ * NPAIR, (j + 1) * NSEQ * NPAIR)
            x = [a for o in ops[j * NSEQ:(j + 1) * NSEQ] for a in o[0]]
            yk = [a for o in ops[j * NSEQ:(j + 1) * NSEQ] for a in o[1]]
            vw = _both(lambda s: [_stack_pair(a) for a in _pairs(v_ref[s, rows(j)])])
            egl = _both(lambda s: _pairs(sc[j * NSEQ + s][0][L - 1:L, :]))
            y, sn, uv = _pairs_fwd(x, yk, m[mine], tinv[mine], vw, _both(lambda s: _rows_of(s_cur[s], PW)), egl)
            for s in range(NSEQ):
                ps = slice(s * NPAIR, (s + 1) * NPAIR)
                hs_ref[j, s] = s_cur[s]
                y_ref[s, rows(j)] = jnp.concatenate([_fold_pair(a) for a in y[ps]], axis=-1)
                m_ref[j, s] = jnp.concatenate(m[mine][ps], axis=-1)
                t_ref[j, s] = jnp.concatenate(tinv[mine][ps], axis=-1)
                uv_ref[j, s] = jnp.concatenate(uv[ps], axis=-1)
                s_cur[s] = jnp.concatenate(sn[ps], axis=0)
                hn_ref[j, s] = s_cur[s]
        for s in range(NSEQ):
            state[s] = s_cur[s]

    blk = pl.BlockSpec((NSEQ, RW_CHUNKS * L, W), lambda c: (0, c, 0))
    per_chunk = lambda shape: pl.BlockSpec((RW_CHUNKS, NSEQ) + shape, lambda c: (c, 0, 0, 0))
    saved_shapes = [(STATE_SHAPE, F32), (STATE_SHAPE, F32), (M_SHAPE, BF16), (TINV_SHAPE, BF16), (UV_SHAPE, BF16)]
    y, *saved = pl.pallas_call(
        body, grid=(NC // RW_CHUNKS,), name="rwkv_fwd",
        in_specs=[blk] * 6,
        out_specs=[blk] + [per_chunk(shape) for shape, _ in saved_shapes],
        out_shape=[jax.ShapeDtypeStruct((NSEQ, SEQ, W), F32)]
        + [jax.ShapeDtypeStruct((NC, NSEQ) + shape, dt) for shape, dt in saved_shapes],
        scratch_shapes=[pltpu.VMEM((NSEQ,) + STATE_SHAPE, F32)],
        compiler_params=_cparams(("arbitrary",), VMEM_BIG),
    )(*[_seq_view(a) for a in (r, lw, k2, v, aa, bb)])
    return y.reshape(T, W), saved


def rwkv_bwd(r, lw, k2, aa, bb, saved, dy):
    def body(r_ref, lw_ref, k2_ref, aa_ref, bb_ref, hs_ref, hn_ref, m_ref, t_ref, uv_ref, dy_ref,
             dr_ref, dlw_ref, dk2_ref, dv_ref, daa_ref, dbb_ref, dstate):
        @pl.when(pl.program_id(0) == 0)
        def _():
            dstate[...] = jnp.zeros_like(dstate)

        keep, tri, _ = _chunk_masks()
        sc = [_scaled(r_ref[s], lw_ref[s], k2_ref[s], aa_ref[s], bb_ref[s], tri) for s in range(NSEQ)]
        ops = [_pairs_operands(*sc[s][3:]) for s in range(NSEQ)]
        x, yk = _both(lambda s: ops[s][0]), _both(lambda s: ops[s][1])
        m = _both(lambda s: [m_ref[0, s][:, i * 4 * L:(i + 1) * 4 * L] for i in range(NPAIR)])
        tinv = _both(lambda s: [t_ref[0, s][:, i * 2 * L:(i + 1) * 2 * L] for i in range(NPAIR)])
        uv = _both(lambda s: _pairs(uv_ref[0, s]))
        dyw = _both(lambda s: [_stack_pair(a) for a in _pairs(dy_ref[s])])
        s0 = _both(lambda s: _rows_of(hs_ref[0, s], PW))
        sn = _both(lambda s: _rows_of(hn_ref[0, s], PW))
        dsn = _both(lambda s: _rows_of(dstate[s], PW))
        egl = _both(lambda s: _pairs(sc[s][0][L - 1:L, :]))
        dx, dyk, dvw, dgl, ds0 = _pairs_bwd(x, yk, m, tinv, uv, s0, sn, egl, dyw, dsn, keep)
        for s in range(NSEQ):
            mine = slice(s * NPAIR, (s + 1) * NPAIR)
            eg, eng, egp, at, rt, bt, kt = sc[s]
            dstate[s] = jnp.concatenate(ds0[mine], axis=0)
            dv_ref[s] = jnp.concatenate([_fold_pair(a) for a in dvw[mine]], axis=-1)
            dat = jnp.concatenate([_fold_pair(a[:2 * L]) for a in dx[mine]], axis=-1)
            drt = jnp.concatenate([_fold_pair(a[2 * L:]) for a in dx[mine]], axis=-1)
            dbt = jnp.concatenate([_fold_pair(a[:2 * L]) for a in dyk[mine]], axis=-1)
            dkt = jnp.concatenate([_fold_pair(a[2 * L:]) for a in dyk[mine]], axis=-1)
            dg = drt * rt - dbt * bt - dkt * kt
            dg = dg + jnp.where(_iota2(dg.shape, 0) == L - 1, jnp.concatenate(dgl[mine], axis=-1), 0.0)
            dgp = dat * at
            dlw_ref[s] = _hdot_tn(tri, dg + dgp) - dgp
            dr_ref[s] = drt * eg
            daa_ref[s] = dat * egp
            dbb_ref[s] = dbt * eng
            dk2_ref[s] = dkt * eng

    blk = pl.BlockSpec((NSEQ, L, W), lambda c: (0, NC - 1 - c, 0))
    per_chunk = lambda shape: pl.BlockSpec((1, NSEQ) + shape, lambda c: (NC - 1 - c, 0, 0, 0))
    outs = pl.pallas_call(
        body, grid=(NC,), name="rwkv_bwd",
        in_specs=[blk] * 5 + [per_chunk(a.shape[2:]) for a in saved] + [blk],
        out_specs=[blk] * 6,
        out_shape=[jax.ShapeDtypeStruct((NSEQ, SEQ, W), F32)] * 6,
        scratch_shapes=[pltpu.VMEM((NSEQ,) + STATE_SHAPE, F32)],
        compiler_params=_cparams(("arbitrary",)),
    )(*[_seq_view(a) for a in (r, lw, k2, aa, bb)], *saved, _seq_view(dy))
    return [a.reshape(T, W) for a in outs]


def _post_math(y, r, k2, v, ga, o, gb, lng, lnb, rk, bd):
    mu = _headsum(y, bd) * (1.0 / HD)
    yc = y - mu
    var = _headsum(yc * yc, bd) * (1.0 / HD)
    yn = yc * lax.rsqrt(var + GN_EPS) * lng + lnb
    bonus = _headsum(r * k2 * rk, bd) * v
    return (yn + bonus) * _silu(ga), o * _silu(gb)


def even_post(y, r, k2, v, ga, o, gb, lng, lnb, rk, h, w_bf):
    tm = 512

    def body(y_ref, r_ref, k2_ref, v_ref, ga_ref, o_ref, gb_ref, lng_ref, lnb_ref, rk_ref, h_ref, w_ref,
             ho_ref, zt_ref):
        ya, yb = _post_math(y_ref[...], r_ref[...], k2_ref[...], v_ref[...], ga_ref[...], o_ref[...], gb_ref[...],
                            lng_ref[...], lnb_ref[...], rk_ref[...], _head_blockdiag())
        z = jnp.concatenate([ya.astype(BF16), yb.astype(BF16)], axis=-1)
        zt_ref[...] = z.T
        ho_ref[...] = h_ref[...] + jnp.dot(z, w_ref[...], preferred_element_type=F32)

    vec = _const_spec((1, W))
    return pl.pallas_call(
        body, grid=(T // tm,), name="even_post",
        in_specs=[_row_spec(tm, W)] * 7 + [vec] * 3 + [_row_spec(tm, D), _const_spec((D, D))],
        out_specs=[_row_spec(tm, D), _col_spec(D, tm)],
        out_shape=[jax.ShapeDtypeStruct((T, D), F32), jax.ShapeDtypeStruct((D, T), BF16)],
        compiler_params=_cparams(("parallel",), VMEM_BIG),
    )(y, r, k2, v, ga, o, gb, lng, lnb, rk, h, w_bf)


def even_post_bwd(y, r, k2, v, ga, o, gb, lng, lnb, rk, dh, zt_bf, w_bf, after=None):
    tm = 512
    extra_specs, extra = _after_operand(after)

    def body(y_ref, r_ref, k2_ref, v_ref, ga_ref, o_ref, gb_ref, lng_ref, lnb_ref, rk_ref, dh_ref, zt_ref, w_ref,
             *rest):
        (dy_ref, dr_ref, dk2_ref, dv_ref, dga_ref, do_ref, dgb_ref, dlng_ref, dlnb_ref, drk_ref, dw_ref,
         acc_ref) = rest[-12:]
        dzv = _out_proj_back(dh_ref, zt_ref, w_ref, dw_ref, acc_ref)
        bd = _head_blockdiag()
        _, vjp = jax.vjp(lambda *a: _post_math(*a, bd), y_ref[...], r_ref[...], k2_ref[...], v_ref[...], ga_ref[...],
                         o_ref[...], gb_ref[...], lng_ref[...], lnb_ref[...], rk_ref[...])
        dy, dr, dk2, dv, dga, do, dgb, dlng, dlnb, drk = vjp((dzv[:, 0:W], dzv[:, W:2 * W]))
        for ref, val in ((dy_ref, dy), (dr_ref, dr), (dk2_ref, dk2), (dv_ref, dv), (dga_ref, dga), (do_ref, do),
                         (dgb_ref, dgb)):
            ref[...] = val.astype(ref.dtype)

        @pl.when(pl.program_id(0) == 0)
        def _():
            for ref in (dlng_ref, dlnb_ref, drk_ref):
                ref[...] = jnp.zeros_like(ref)

        dlng_ref[...] += dlng
        dlnb_ref[...] += dlnb
        drk_ref[...] += drk

    vec = _const_spec((1, W))
    return pl.pallas_call(
        body, grid=(T // tm,), name="even_post_bwd",
        in_specs=[_row_spec(tm, W)] * 7 + [vec] * 3 + [_row_spec(tm, D), _col_spec(D, tm), _const_spec((D, D))]
        + extra_specs,
        out_specs=[_row_spec(tm, W)] * 7 + [vec] * 3 + [_const_spec((D, D))],
        out_shape=[jax.ShapeDtypeStruct((T, W), dt) for dt in (F32, F32, F32, F32, BF16, F32, BF16)]
        + [jax.ShapeDtypeStruct((1, W), F32)] * 3 + [jax.ShapeDtypeStruct((D, D), BF16)],
        scratch_shapes=[pltpu.VMEM((D, D), F32)],
        compiler_params=_cparams(("arbitrary",), VMEM_BIG),
    )(y, r, k2, v, ga, o, gb, lng, lnb, rk, dh, zt_bf, w_bf, *extra)


PADSEQ = SEQ + LEFT * L
ATT_SCALE = 1.0 / math.sqrt(HD)
ATT_Q = 4
WIN = BAND + (ATT_Q - 1) * L
ATT_STEPS = NC // ATT_Q
ATT_BIAS_SHAPE = (NPAIR, ATT_Q * 2 * L, WIN)
ATT_WINDOW_BIAS_SHAPE = (ATT_Q, NPAIR, 2 * L, WIN)


def _stack_chunks(a):
    return jnp.concatenate([_stack_pair(a[i * L:(i + 1) * L]) for i in range(ATT_Q)], axis=0)


def _unstack_chunks(a):
    return jnp.concatenate([_unstack_pair(a[i * 2 * L:(i + 1) * 2 * L]) for i in range(ATT_Q)], axis=0)


def _window_bias(b_ref):
    return [jnp.concatenate([b_ref[c, p] for c in range(ATT_Q)], axis=0) for p in range(NPAIR)]


def _key_window(ref, step):
    start = step * (ATT_Q * L) - LEFT * L
    rows = ref[pl.ds(pl.multiple_of(jnp.maximum(start, 0), L), WIN), :]
    window = rows
    for lead in range(ATT_Q * L, LEFT * L + 1, ATT_Q * L):
        moved = jnp.concatenate([rows[WIN - lead:], rows[:WIN - lead]], axis=0)
        window = jnp.where(start == -lead, moved, window)
    return window


def _att_probs(q2, kw, bias, step):
    valid = _iota2((1, WIN), 1) >= (LEFT - step * ATT_Q) * L
    s = [jnp.where(valid, _bdot_nt(a, b) * ATT_SCALE + bias[p], NEG) for p, (a, b) in enumerate(zip(q2, kw))]
    e = [jnp.exp(a - jnp.max(a, axis=-1, keepdims=True)) for a in s]
    return [a / jnp.sum(a, axis=-1, keepdims=True) for a in e]


def attention_fwd(q, k, v, bias):
    def body(q_ref, k_ref, v_ref, b_ref, o_ref):
        step = pl.program_id(1)
        kw = _pairs(_key_window(k_ref, step))
        vw = _pairs(_key_window(v_ref, step))
        q2 = [_stack_chunks(a) for a in _pairs(q_ref[...])]
        p = _att_probs(q2, kw, _window_bias(b_ref), step)
        o_ref[...] = jnp.concatenate([_unstack_chunks(_bdot(a, b)) for a, b in zip(p, vw)], axis=-1)

    qblk = pl.BlockSpec((ATT_Q * L, W), lambda b, c: (b * ATT_STEPS + c, 0))
    kblk = pl.BlockSpec((SEQ, W), lambda b, c: (b, 0))
    return pl.pallas_call(
        body, grid=(NSEQ, ATT_STEPS), name="attention_fwd",
        in_specs=[qblk, kblk, kblk, _const_spec(ATT_WINDOW_BIAS_SHAPE)],
        out_specs=qblk, out_shape=jax.ShapeDtypeStruct((T, W), F32),
        compiler_params=_cparams(("parallel", "arbitrary")),
    )(q, k, v, bias)


def attention_bwd(q, k, v, bias, do):
    def body(q_ref, k_ref, v_ref, b_ref, do_ref, dq_ref, dko_ref, dvo_ref, db_ref, dk_ref, dv_ref):
        b = pl.program_id(0)
        c = pl.program_id(1)

        @pl.when(c == 0)
        def _():
            dk_ref[...] = jnp.zeros_like(dk_ref)
            dv_ref[...] = jnp.zeros_like(dv_ref)

        @pl.when((c == 0) & (b == 0))
        def _():
            db_ref[...] = jnp.zeros_like(db_ref)

        start = pl.multiple_of(c * (ATT_Q * L), L)
        kw = _pairs(_key_window(k_ref, c))
        vw = _pairs(_key_window(v_ref, c))
        q2 = [_stack_chunks(a) for a in _pairs(q_ref[...])]
        do2 = [_stack_chunks(a) for a in _pairs(do_ref[...].astype(BF16))]
        p = _att_probs(q2, kw, _window_bias(b_ref), c)
        dp = [_bdot_nt(a, b) for a, b in zip(do2, vw)]
        ds = [a * (d - jnp.sum(d * a, axis=-1, keepdims=True)) for a, d in zip(p, dp)]
        dss = [(a * ATT_SCALE).astype(BF16) for a in ds]
        dq_ref[...] = jnp.concatenate([_unstack_chunks(_bdot(a, b)) for a, b in zip(dss, kw)], axis=-1).astype(BF16)
        dk_ref[pl.ds(start, WIN), :] += jnp.concatenate([_bdot_tn(a, b) for a, b in zip(dss, q2)], axis=-1)
        dv_ref[pl.ds(start, WIN), :] += jnp.concatenate([_bdot_tn(a, b) for a, b in zip(p, do2)], axis=-1)
        for i in range(NPAIR):
            db_ref[i] += ds[i]

        @pl.when(c == ATT_STEPS - 1)
        def _():
            dko_ref[...] = dk_ref[LEFT * L:, :].astype(BF16)
            dvo_ref[...] = dv_ref[LEFT * L:, :].astype(BF16)

    qblk = pl.BlockSpec((ATT_Q * L, W), lambda b, c: (b * ATT_STEPS + c, 0))
    sblk = pl.BlockSpec((SEQ, W), lambda b, c: (b, 0))
    bblk = _const_spec(ATT_BIAS_SHAPE)
    return pl.pallas_call(
        body, grid=(NSEQ, ATT_STEPS), name="attention_bwd",
        in_specs=[qblk, sblk, sblk, _const_spec(ATT_WINDOW_BIAS_SHAPE), qblk],
        out_specs=[qblk, sblk, sblk, bblk],
        out_shape=[jax.ShapeDtypeStruct((T, W), BF16), jax.ShapeDtypeStruct((T, W), BF16),
                   jax.ShapeDtypeStruct((T, W), BF16), jax.ShapeDtypeStruct(ATT_BIAS_SHAPE, F32)],
        scratch_shapes=[pltpu.VMEM((PADSEQ, W), F32), pltpu.VMEM((PADSEQ, W), F32)],
        compiler_params=_cparams(("arbitrary", "arbitrary"), VMEM_BIG),
    )(q, k, v, bias, do)


NTAB = 2 * CLIP + 1
EXT = BAND + L


def _ext_onehot():
    n = _iota2((EXT, NTAB), 0)
    m = _iota2((EXT, NTAB), 1)
    return (jnp.clip(BAND - 1 - n, -CLIP, CLIP) + CLIP == m).astype(F32)


def bias_expand(table):
    def body(t_ref, o_ref):
        ext = _hdot_nt(t_ref[...], _ext_onehot())
        ext = jnp.concatenate([ext, jnp.zeros((NH, WIN - EXT), F32)], axis=-1)
        col = _iota2((L, WIN), 1)
        for h in range(NH):
            rows = jnp.broadcast_to(ext[h:h + 1], (L, WIN))
            for c in range(ATT_Q):
                inside = (col >= c * L) & (col < c * L + BAND)
                plane = pltpu.roll(rows, (c * L - (L - 1)) % WIN, 1, stride=1, stride_axis=0)
                o_ref[c, h] = jnp.where(inside, plane, NEG)

    out = pl.pallas_call(body, name="bias_expand", out_shape=jax.ShapeDtypeStruct((ATT_Q, NH, L, WIN), F32))(table)
    return out.reshape(ATT_WINDOW_BIAS_SHAPE)


def bias_grad(dbias):
    def body(d_ref, o_ref):
        acc = jnp.zeros((NH, EXT), F32)
        zpad = jnp.zeros((NH, EXT - BAND), F32)
        for i in range(L):
            s = L - 1 - i
            row = jnp.concatenate([d_ref[:, i, :], zpad], axis=-1)
            acc = acc + (pltpu.roll(row, s, 1) if s else row)
        o_ref[...] = _hdot(acc, _ext_onehot())

    return pl.pallas_call(body, name="bias_grad", out_shape=jax.ShapeDtypeStruct((NH, NTAB), F32))(dbias)


def _group_cols(g):
    return slice(g * SGC, (g + 1) * SGC)


def _sg_norm(gv, lng, lnb):
    gc = gv - jnp.mean(gv, axis=-1, keepdims=True)
    rstd = lax.rsqrt(jnp.mean(gc * gc, axis=-1, keepdims=True) + LN_EPS)
    xhat = gc * rstd
    return xhat, rstd, xhat * lng + lnb


GMLP_BWD_CHUNKS = 2


def gmlp_fwd_loss(u, v, gate, lng, lnb, wm_bf, sgb_t, h, w_bf, g_final, target):
    tm = GMLP_BWD_CHUNKS * SGC

    def body(u_ref, v_ref, gt_ref, lng_ref, lnb_ref, wm_ref, sb_ref, h_ref, w_ref, g_ref, t_ref,
             dh_ref, loss_ref, dg_ref, zt_ref):
        zs = []
        for ch in range(GMLP_BWD_CHUNKS):
            rows = slice(ch * SGC, (ch + 1) * SGC)
            _, _, vln = _sg_norm(_gelu(v_ref[rows, :]), lng_ref[...], lnb_ref[...])
            vlb = vln.astype(BF16)
            zg = []
            for g in range(NG):
                cs = _group_cols(g)
                sv = jnp.dot(wm_ref[g], vlb[:, cs], preferred_element_type=F32) + sb_ref[:, g:g + 1]
                zg.append((_gelu(u_ref[rows, cs]) * sv * _silu(gt_ref[rows, cs])).astype(BF16))
            zs.append(jnp.concatenate(zg, axis=-1))
        z = jnp.concatenate(zs, axis=0)
        zt_ref[...] = z.T
        xv = h_ref[...] + jnp.dot(z, w_ref[...], preferred_element_type=F32)
        rstd = lax.rsqrt(jnp.mean(xv * xv, axis=-1, keepdims=True) + RMS_EPS)
        xhat = xv * rstd
        err = xhat * g_ref[...] - t_ref[...]
        part = 0.5 * jnp.sum(jnp.mean(err * err, axis=-1, keepdims=True), axis=0, keepdims=True)
        dout = err * (1.0 / D)

        @pl.when(pl.program_id(0) == 0)
        def _():
            loss_ref[...] = jnp.zeros_like(loss_ref)
            dg_ref[...] = jnp.zeros_like(dg_ref)

        loss_ref[...] += jnp.broadcast_to(part, loss_ref.shape)
        dg_ref[...] += jnp.sum(dout * xhat, axis=0, keepdims=True)
        dxh = dout * g_ref[...]
        dh_ref[...] = rstd * (dxh - xhat * jnp.mean(dxh * xhat, axis=-1, keepdims=True))

    return pl.pallas_call(
        body, grid=(T // tm,), name="gmlp_fwd_loss",
        in_specs=[_row_spec(tm, D)] * 3 + [_const_spec((1, D))] * 2
        + [_const_spec((NG, SGC, SGC)), _const_spec((SGC, NG)), _row_spec(tm, D), _const_spec((D, D)),
           _const_spec((1, D)), _row_spec(tm, D)],
        out_specs=[_row_spec(tm, D), _const_spec((8, 128)), _const_spec((1, D)), _col_spec(D, tm)],
        out_shape=[jax.ShapeDtypeStruct((T, D), F32), jax.ShapeDtypeStruct((8, 128), F32),
                   jax.ShapeDtypeStruct((1, D), F32), jax.ShapeDtypeStruct((D, T), BF16)],
        compiler_params=_cparams(("arbitrary",), VMEM_BIG),
    )(u, v, gate, lng, lnb, wm_bf, sgb_t, h, w_bf, g_final, target)


def gmlp_bwd(u, v, gate, lng, lnb, wm_bf, sgb_t, dh, zt_bf, w_bf):
    def body(u_ref, v_ref, gt_ref, lng_ref, lnb_ref, wm_ref, sb_ref, dh_ref, zt_ref, w_ref,
             du_ref, dv_ref, dgt_ref, dlng_ref, dlnb_ref, dwm_ref, dsb_ref, dw_ref, acc_ref):
        @pl.when(pl.program_id(0) == 0)
        def _():
            for ref in (dlng_ref, dlnb_ref, dwm_ref, dsb_ref):
                ref[...] = jnp.zeros_like(ref)

        dz = _out_proj_back(dh_ref, zt_ref, w_ref, dw_ref, acc_ref)
        sel = (_iota2((D, NG), 0) // SGC == _iota2((D, NG), 1)).astype(F32)
        for ch in range(GMLP_BWD_CHUNKS):
            rows = slice(ch * SGC, (ch + 1) * SGC)
            gv, dgv_dv = _gelu_both(v_ref[rows, :])
            xhat, rstd, vln = _sg_norm(gv, lng_ref[...], lnb_ref[...])
            vlb = vln.astype(BF16)
            dvln = []
            dsv_all = []
            for g in range(NG):
                cs = _group_cols(g)
                uu = u_ref[rows, cs]
                gg = gt_ref[rows, cs]
                dzz = dz[rows, cs]
                sv = jnp.dot(wm_ref[g], vlb[:, cs], preferred_element_type=F32) + sb_ref[:, g:g + 1]
                gu, dgu = _gelu_both(uu)
                sg, dsg = _silu_both(gg)
                dzgu = dzz * gu
                dsv = dzgu * sg
                dgt_ref[rows, cs] = (dzgu * sv * dsg).astype(BF16)
                du_ref[rows, cs] = (dzz * sv * sg * dgu).astype(BF16)
                dsb16 = dsv.astype(BF16)
                dvln.append(lax.dot_general(wm_ref[g], dsb16, (((0,), (0,)), ((), ())), preferred_element_type=F32))
                dwm_ref[g] += lax.dot_general(dsb16, vlb[:, cs], (((1,), (1,)), ((), ())),
                                              preferred_element_type=F32)
                dsv_all.append(dsv)
            dvl = jnp.concatenate(dvln, axis=-1)
            dsb_ref[...] += _hdot(jnp.concatenate(dsv_all, axis=-1), sel)
            dlng_ref[...] += jnp.sum(dvl * xhat, axis=0, keepdims=True)
            dlnb_ref[...] += jnp.sum(dvl, axis=0, keepdims=True)
            dxh = dvl * lng_ref[...]
            dgv = rstd * (dxh - jnp.mean(dxh, axis=-1, keepdims=True)
                          - xhat * jnp.mean(dxh * xhat, axis=-1, keepdims=True))
            dv_ref[rows, :] = (dgv * dgv_dv).astype(BF16)

    tm = GMLP_BWD_CHUNKS * SGC
    return pl.pallas_call(
        body, grid=(T // tm,), name="gmlp_bwd",
        in_specs=[_row_spec(tm, D)] * 3 + [_const_spec((1, D))] * 2
        + [_const_spec((NG, SGC, SGC)), _const_spec((SGC, NG)), _row_spec(tm, D), _col_spec(D, tm),
           _const_spec((D, D))],
        out_specs=[_row_spec(tm, D)] * 3 + [_const_spec((1, D))] * 2
        + [_const_spec((NG, SGC, SGC)), _const_spec((SGC, NG)), _const_spec((D, D))],
        out_shape=[jax.ShapeDtypeStruct((T, D), BF16)] * 3 + [jax.ShapeDtypeStruct((1, D), F32)] * 2
        + [jax.ShapeDtypeStruct((NG, SGC, SGC), F32), jax.ShapeDtypeStruct((SGC, NG), F32),
           jax.ShapeDtypeStruct((D, D), BF16)],
        scratch_shapes=[pltpu.VMEM((D, D), F32)],
        compiler_params=_cparams(("arbitrary",), VMEM_BIG),
    )(u, v, gate, lng, lnb, wm_bf, sgb_t, dh, zt_bf, w_bf)


NCHIP = 4
NDEV = 8
ANY = pl.BlockSpec(memory_space=pl.ANY)


HBM = pl.BlockSpec(memory_space=pltpu.HBM)
SEM = pl.BlockSpec(memory_space=pltpu.SEMAPHORE)
EFFECT = pltpu.SideEffectType.DATAFLOW_SIDE_EFFECTING


CHIPS, EVERY, SIBLING = "chips", "every", "sibling"
SLOTS = {CHIPS: NCHIP, EVERY: NDEV, SIBLING: 1}


def _peers(scope):
    x, y, c = lax.axis_index("x"), lax.axis_index("y"), lax.axis_index("c")
    if scope == SIBLING:
        return [((x, y, 1 - c), 0)], 0
    if scope == CHIPS:
        return [((px, py, c), 2 * px + py) for px, py in ((1 - x, y), (x, 1 - y), (1 - x, 1 - y))], 2 * x + y
    out = []
    for j in range(1, NDEV):
        px, py, pc = x ^ (j >> 2), y ^ ((j >> 1) & 1), c ^ (j & 1)
        out.append(((px, py, pc), 4 * px + 2 * py + pc))
    return out, 4 * x + 2 * y + c


def _send_copies(src, land, send, recv, scatter, scope, starting):
    peers, me = _peers(scope)
    copies = []
    for t in range(len(src)):
        for j, (dev, slot) in enumerate(peers):
            k = t * len(peers) + j
            copies.append(pltpu.make_async_remote_copy(
                src_ref=src[t].at[slot] if scatter else src[t], dst_ref=land[t].at[me if starting else slot],
                send_sem=send.at[k], recv_sem=recv.at[k], device_id=dev, device_id_type=MESH))
    return copies


def _own_copies(src, land, sems, scatter, scope):
    if scope == SIBLING:
        return []
    _, me = _peers(scope)
    return [pltpu.make_async_copy(src[t].at[me] if scatter else src[t], land[t].at[me], sems.at[t])
            for t in range(len(src))]


def send_start(srcs, scatter, scope, name, after=None):
    n = len(srcs)
    slots = SLOTS[scope]
    extra_specs, extra = _after_operand(after)
    lands = [pltpu.HBM(a.shape if scatter else (slots,) + a.shape, a.dtype) for a in srcs]
    sems = [pltpu.SemaphoreType.DMA((n * max(slots - 1, 1),))] * 2 + ([] if scope == SIBLING else
                                                                     [pltpu.SemaphoreType.DMA((n,))])
    k = len(sems)

    def body(*refs):
        first_out = n + len(extra)
        src, land = refs[:n], refs[first_out + k + n:first_out + k + 2 * n]
        for cp in _send_copies(src, land, refs[first_out], refs[first_out + 1], scatter, scope, True):
            cp.start()
        for cp in _own_copies(src, land, refs[first_out + k - 1], scatter, scope):
            cp.start()
        refs[-1][...] = jnp.zeros_like(refs[-1])

    out = pl.pallas_call(
        body, name=name,
        out_shape=(*sems, *[pltpu.HBM(a.shape, a.dtype) for a in srcs], *lands, jax.ShapeDtypeStruct((8, 128), F32)),
        in_specs=[HBM] * n + extra_specs,
        out_specs=(*[SEM] * k, *[HBM] * (2 * n), pl.BlockSpec(memory_space=pltpu.VMEM)),
        input_output_aliases={i: k + i for i in range(n)},
        compiler_params=pltpu.CompilerParams(has_side_effects=EFFECT),
    )(*[pltpu.with_memory_space_constraint(a, pltpu.HBM) for a in srcs], *extra)
    return list(out[:k]), list(out[k:k + n]), list(out[k + n:k + 2 * n]), out[-1]


def send_wait(started, after, scatter, scope, name, with_sources=False, only=None):
    sems, srcs, lands, _ = started
    n, k = len(srcs), len(sems)
    wanted = range(n) if only is None else only

    def body(*refs):
        src, land = refs[:n], refs[n:2 * n]
        for t, cp in enumerate(_own_copies(src, land, refs[2 * n + k - 1], scatter, scope)):
            if t in wanted:
                cp.wait()
        copies = _send_copies(src, land, refs[2 * n], refs[2 * n + 1], scatter, scope, False)
        for i, cp in enumerate(copies):
            if i // (len(copies) // n) in wanted:
                cp.wait_send()
                cp.wait_recv()

    arrs = list(srcs) + list(lands)
    out = pl.pallas_call(
        body, name=name, out_shape=tuple(pltpu.HBM(a.shape, a.dtype) for a in arrs),
        in_specs=[HBM] * (2 * n) + [SEM] * k + [ANY], out_specs=tuple([HBM] * (2 * n)),
        input_output_aliases={i: i for i in range(2 * n)},
        compiler_params=pltpu.CompilerParams(has_side_effects=EFFECT),
    )(*arrs, *sems, after)
    return (list(out[:n]), list(out[n:])) if with_sources else list(out[n:])


def gather_weights(arrs, split):
    n = len(arrs)

    def body(*refs):
        ins, outs = refs[:n], refs[n:2 * n]
        send1, recv1, send2, recv2, loc_in, loc_out = refs[2 * n:2 * n + 6]
        staged = refs[2 * n + 6:]
        x, y, c = lax.axis_index("x"), lax.axis_index("y"), lax.axis_index("c")
        me = 2 * x + y
        sibling = (x, y, 1 - c)
        peers = [(1 - x, y), (x, 1 - y), (1 - x, 1 - y)]

        def rows_of(t, core):
            half = arrs[t].shape[0] // 2
            return pl.ds(core * half, half)

        def part(ref, t, core):
            return ref.at[rows_of(t, core)] if split[t] else ref

        load = [pltpu.make_async_copy(ins[t], staged[t], loc_in.at[t]) for t in range(n)]
        store = [pltpu.make_async_copy(staged[t], outs[t].at[me], loc_out.at[t]) for t in range(n)]
        for cp in load:
            cp.start()
        first = []
        for t in range(n):
            for j, (px, py) in enumerate(peers):
                first.append(pltpu.make_async_remote_copy(
                    src_ref=part(ins[t], t, c), dst_ref=part(outs[t].at[me], t, c), send_sem=send1.at[t, j],
                    recv_sem=recv1.at[t, j], device_id=(px, py, c), device_id_type=MESH))
        for cp in first:
            cp.start()
        for cp_in, cp_out in zip(load, store):
            cp_in.wait()
            cp_out.start()
        passed = []
        for t in range(n):
            for j, (px, py) in enumerate(peers):
                landed = part(outs[t].at[2 * px + py], t, c)
                pltpu.make_async_remote_copy(
                    src_ref=landed, dst_ref=landed, send_sem=send1.at[t, j], recv_sem=recv1.at[t, j],
                    device_id=(x, y, c), device_id_type=MESH).wait_recv()
                if split[t]:
                    cp = pltpu.make_async_remote_copy(
                        src_ref=landed, dst_ref=landed, send_sem=send2.at[t, j], recv_sem=recv2.at[t, j],
                        device_id=sibling, device_id_type=MESH)
                    cp.start()
                    passed.append(cp)
        for t in range(n):
            for j, (px, py) in enumerate(peers):
                if split[t]:
                    other = part(outs[t].at[2 * px + py], t, 1 - c)
                    pltpu.make_async_remote_copy(
                        src_ref=other, dst_ref=other, send_sem=send2.at[t, j], recv_sem=recv2.at[t, j],
                        device_id=(x, y, c), device_id_type=MESH).wait_recv()
        for cp in first + passed:
            cp.wait_send()
        for cp in store:
            cp.wait()

    return pl.pallas_call(
        body, name="gather_weights", in_specs=[ANY] * n, out_specs=[ANY] * n,
        out_shape=[jax.ShapeDtypeStruct((NCHIP,) + a.shape, a.dtype) for a in arrs],
        scratch_shapes=[pltpu.SemaphoreType.DMA((n, 3))] * 4 + [pltpu.SemaphoreType.DMA((n,))] * 2
        + [pltpu.VMEM(a.shape, a.dtype) for a in arrs],
    )(*arrs)


def _adam_math(g, w, m, v):
    m = ADAM_B1 * m + (1.0 - ADAM_B1) * g
    v = ADAM_B2 * v + (1.0 - ADAM_B2) * (g * g)
    m_hat = m / (1.0 - ADAM_B1 ** ADAM_STEP)
    v_hat = v / (1.0 - ADAM_B2 ** ADAM_STEP)
    delta = -ADAM_LR * (m_hat / (jnp.sqrt(v_hat) + ADAM_EPS) + ADAM_WD * w)
    return delta, m, v


def _rows_tile(rows):
    return rows if rows <= 256 else 256


def sum_chips(parts, name):
    _, rows, cols = parts.shape
    tr = _rows_tile(rows)

    def body(p_ref, o_ref):
        acc = p_ref[0].astype(F32)
        for s in range(1, NCHIP):
            acc = acc + p_ref[s].astype(F32)
        o_ref[...] = acc

    return pl.pallas_call(
        body, grid=(rows // tr,), name=name,
        in_specs=[pl.BlockSpec((NCHIP, tr, cols), lambda i: (0, i, 0))],
        out_specs=pl.BlockSpec((tr, cols), lambda i: (i, 0)),
        out_shape=jax.ShapeDtypeStruct((rows, cols), F32),
        compiler_params=_cparams(("parallel",)),
    )(parts)


def sum_chips_small(parts, name):
    n = len(parts)

    def body(*refs):
        for p_ref, o_ref in zip(refs[:n], refs[n:]):
            acc = p_ref[0]
            for s in range(1, NCHIP):
                acc = acc + p_ref[s]
            o_ref[...] = acc

    return pl.pallas_call(body, name=name, out_shape=[jax.ShapeDtypeStruct(p.shape[1:], F32) for p in parts])(*parts)


def adam_shard_small(items, name):
    n = len(items)

    def body(*refs):
        for t in range(n):
            a_ref, b_ref, w_ref, m_ref, v_ref = refs[5 * t:5 * t + 5]
            g_ref, d_ref, mo_ref, vo_ref = refs[5 * n + 4 * t:5 * n + 4 * t + 4]
            g = (a_ref[...] + b_ref[...]).reshape(w_ref.shape)
            g_ref[...] = g
            d_ref[...], mo_ref[...], vo_ref[...] = _adam_math(g, w_ref[...], m_ref[...], v_ref[...])

    out = pl.pallas_call(
        body, name=name, out_shape=[jax.ShapeDtypeStruct(it[2].shape, F32) for it in items for _ in range(4)],
    )(*[a for it in items for a in it])
    return [out[4 * t:4 * t + 4] for t in range(n)]


def adam_shard(p_mine, p_sib, w, m, v, name):
    rows, cols = p_mine.shape
    tr = _rows_tile(rows)
    lead = w.ndim == 3

    def body(a_ref, b_ref, w_ref, m_ref, v_ref, g_ref, d_ref, mo_ref, vo_ref):
        g = a_ref[...] + b_ref[...]
        g = g[None] if lead else g
        g_ref[...] = g
        d_ref[...], mo_ref[...], vo_ref[...] = _adam_math(g, w_ref[...], m_ref[...], v_ref[...])

    flat = pl.BlockSpec((tr, cols), lambda i: (i, 0))
    spec = pl.BlockSpec((1, tr, cols), lambda i: (0, i, 0)) if lead else flat
    return pl.pallas_call(
        body, grid=(rows // tr,), name=name, in_specs=[flat] * 2 + [spec] * 3, out_specs=[spec] * 4,
        out_shape=[jax.ShapeDtypeStruct(w.shape, F32)] * 4,
        compiler_params=_cparams(("parallel",)),
    )(p_mine, p_sib, w, m, v)


def adam_shard_halves_t(r_mine, r_sib, wt, mt, vt, name):
    hrows, cols = r_mine.shape
    tr = _rows_tile(hrows)
    per_half = hrows // tr

    def body(a_ref, b_ref, w_ref, m_ref, v_ref, g_ref, d_ref, mo_ref, vo_ref):
        mine = pl.program_id(0) == lax.axis_index("c")
        g = jnp.where(mine, a_ref[...], b_ref[...]).T[None]
        g_ref[...] = g
        d_ref[...], mo_ref[...], vo_ref[...] = _adam_math(g, w_ref[...], m_ref[...], v_ref[...])

    flat = pl.BlockSpec((tr, cols), lambda h, i: (i, 0))
    spec = pl.BlockSpec((1, cols, tr), lambda h, i: (0, 0, h * per_half + i))
    return pl.pallas_call(
        body, grid=(2, per_half), name=name, in_specs=[flat] * 2 + [spec] * 3, out_specs=[spec] * 4,
        out_shape=[jax.ShapeDtypeStruct(wt.shape, F32)] * 4,
        compiler_params=_cparams(("parallel", "parallel")),
    )(r_mine, r_sib, wt, mt, vt)


def adam_replicated(gathered, params, name):
    flat = []
    for i, p in enumerate(params):
        if isinstance(p, list):
            off = 0
            for wmv in p:
                n = gathered[i].shape[-1] - off if wmv[0] is None else wmv[0].shape[-1]
                flat.append((i, (off, n), wmv))
                off += n
        else:
            flat.append((i, None, p))
    ins = [a for _, _, wmv in flat for a in wmv if a is not None]
    ng = len(gathered)

    def body(*refs):
        g_refs = refs[:ng]
        in_refs = list(refs[ng:ng + len(ins)])
        out_refs = list(refs[ng + len(ins):])
        sums = []
        for r in g_refs:
            g = r[0]
            for d in range(1, NDEV):
                g = g + r[d]
            sums.append(g)
        for i, lanes, wmv in flat:
            g = sums[i] if lanes is None else sums[i][:, lanes[0]:lanes[0] + lanes[1]]
            out_refs.pop(0)[...] = g
            if wmv[0] is not None:
                w_ref, m_ref, v_ref = in_refs.pop(0), in_refs.pop(0), in_refs.pop(0)
                d_ref, mo_ref, vo_ref = out_refs.pop(0), out_refs.pop(0), out_refs.pop(0)
                d_ref[...], mo_ref[...], vo_ref[...] = _adam_math(g, w_ref[...], m_ref[...], v_ref[...])

    out_shape = []
    for i, lanes, wmv in flat:
        shape = gathered[i].shape[1:] if lanes is None else (1, lanes[1])
        out_shape += [jax.ShapeDtypeStruct(shape, F32)] * (4 if wmv[0] is not None else 1)
    outs = list(pl.pallas_call(body, name=name, out_shape=out_shape)(*gathered, *ins))
    return [[outs.pop(0) for _ in range(4 if wmv[0] is not None else 1)] for _, _, wmv in flat]


EVEN_SPLITS = (SHIFT, W, W, W, W, W)
ODD_SPLITS = (D, D, D)


def _cols_to_chips(a):
    rows, cols = a.shape
    return a.reshape(rows, NCHIP, cols // NCHIP).transpose(1, 0, 2)


def _chips_to_cols(a):
    _, rows, n = a.shape
    return a.transpose(1, 0, 2).reshape(rows, NCHIP * n)


def kernel(x, norm_g, w_in_e, shift_mu, rw_w0, rw_w2, rw_a0, rw_a2, rw_kk, rw_ka, rw_rk, rw_lnx_g, rw_lnx_b, att_bias, w_out_e, w_in_o, sg_ln_g, sg_ln_b, sg_w, sg_b, w_out_o, final_g, loss_target, m_norm_g, m_w_in_e, m_shift_mu, m_rw_w0, m_rw_w2, m_rw_a0, m_rw_a2, m_rw_kk, m_rw_ka, m_rw_rk, m_rw_lnx_g, m_rw_lnx_b, m_att_bias, m_w_out_e, m_w_in_o, m_sg_ln_g, m_sg_ln_b, m_sg_w, m_sg_b, m_w_out_o, m_final_g, v_norm_g, v_w_in_e, v_shift_mu, v_rw_w0, v_rw_w2, v_rw_a0, v_rw_a2, v_rw_kk, v_rw_ka, v_rw_rk, v_rw_lnx_g, v_rw_lnx_b, v_att_bias, v_w_out_e, v_w_in_o, v_sg_ln_g, v_sg_ln_b, v_sg_w, v_sg_b, v_w_out_o, v_final_g):
    x2 = x.reshape(T, D)
    tgt = loss_target.reshape(T, D)

    gathered = gather_weights(
        [jnp.swapaxes(w_in_e[0], 0, 1).astype(BF16), jnp.concatenate([rw_w2[0], rw_a2[0]], axis=0),
         jnp.concatenate([sg_ln_g, sg_ln_b], axis=0)], [True, True, False])
    wie = gathered[0].reshape(EVEN_IN, D)
    w2 = _chips_to_cols(gathered[1][:, :LORA])
    a2 = _chips_to_cols(gathered[1][:, LORA:])
    sglg = _chips_to_cols(gathered[2][:, 0:1])
    sglb = _chips_to_cols(gathered[2][:, 1:2])

    late = [w_out_e[0].astype(BF16), w_in_o[0].astype(BF16), w_out_o[0].astype(BF16)]
    late_started = send_start(late, False, CHIPS, "late_weights_start", after=gathered[0])

    late_state = {}

    def late_weights(layer, after):
        if layer == "even":
            srcs, lands = send_wait(late_started, after, False, CHIPS, "late_w_out_e_wait", True, only=(0,))
            late_state["rest"] = (late_started[0], srcs, lands, None)
            return lands[0].reshape(D, D)
        woe, wio, woo = send_wait(late_state["rest"], after, False, CHIPS, "late_weights_wait", only=(1, 2))
        return woe.reshape(D, D), wio, woo.reshape(D, D)

    def scatter_start(grads, name):
        return send_start([g_.astype(BF16) if g_.shape[-1] >= W else g_ for g_ in grads], True, CHIPS, name)

    started = {}

    def on_odd_grads(d_woo, d_wio):
        started["odd"] = scatter_start([d_woo.reshape(NCHIP, D // NCHIP, D), d_wio], "odd_grads_start")
        return started["odd"][-1]

    def on_even_grads(big_g):
        d_wie_half, d_woe, _, _, d_w2, d_a2, d_sglg, d_sglb = big_g
        blocks = [d_wie_half, d_woe.reshape(NCHIP, D // NCHIP, D), _cols_to_chips(d_w2), _cols_to_chips(d_a2),
                  _cols_to_chips(d_sglg), _cols_to_chips(d_sglb)]
        started["even"] = scatter_start(blocks, "even_grads_start")
        return started["even"][-1]

    def on_small_grads(layer, grads):
        if layer == "odd":
            d_sg_w, d_sg_b, d_final, d_g1 = grads
            mine = [d_sg_w.reshape(NG * SGC, SGC), d_sg_b, jnp.concatenate([d_final, d_g1], axis=1)]
        else:
            mine = [grads[-2], jnp.concatenate(grads[:-2] + grads[-1:], axis=1)]
        started[layer + "_small"] = send_start(mine, False, EVERY, layer + "_small_grads_start")
        return started[layer + "_small"][-1]

    loss_part, dx, _, _ = _local_step(
        x2, tgt, wie, late_weights, w2, a2, sglg, sglb, norm_g, shift_mu, rw_w0, rw_a0, rw_kk, rw_ka, rw_rk,
        rw_lnx_g, rw_lnx_b, att_bias, sg_w, sg_b, final_g, first_after=late_started[-1], on_odd_grads=on_odd_grads,
        on_even_grads=on_even_grads, on_small_grads=on_small_grads)
    wmv = {"w_in_e": tuple(jnp.swapaxes(a, 1, 2) for a in (w_in_e, m_w_in_e, v_w_in_e)),
           "w_out_e": (w_out_e, m_w_out_e, v_w_out_e),
           "w_in_o": (w_in_o, m_w_in_o, v_w_in_o), "w_out_o": (w_out_o, m_w_out_o, v_w_out_o),
           "rw_w2": (rw_w2, m_rw_w2, v_rw_w2), "rw_a2": (rw_a2, m_rw_a2, v_rw_a2),
           "sg_ln_g": (sg_ln_g, m_sg_ln_g, v_sg_ln_g), "sg_ln_b": (sg_ln_b, m_sg_ln_b, v_sg_ln_b)}
    sharded = {}

    def sum_and_swap(names, landed, tag):
        nbig = sum(p_.dtype == BF16 for p_ in landed)
        partial = [sum_chips(p_, "sum_" + nm) for p_, nm in zip(landed[:nbig], names)]
        if nbig < len(names):
            partial += sum_chips_small(landed[nbig:], "sum_small_" + tag)
        return send_start(partial, False, SIBLING, "swap_partials_" + tag + "_start")

    def update(names, swap_started, after, tag):
        partial, landed = send_wait(swap_started, after, False, SIBLING, "swap_partials_" + tag + "_wait", True)
        from_sibling = [a[0] for a in landed]
        nbig = sum(p_.shape[-1] >= W for p_ in partial)
        for nm, mine, sib in zip(names[:nbig], partial, from_sibling):
            if nm == "w_in_e":
                res = adam_shard_halves_t(mine, sib, *wmv[nm], "adam_" + nm)
                sharded[nm] = [jnp.swapaxes(a, 1, 2) for a in res]
            else:
                sharded[nm] = adam_shard(mine, sib, *wmv[nm], "adam_" + nm)
        if nbig < len(names):
            items = [(mine, sib, *wmv[nm]) for nm, mine, sib in zip(names, partial, from_sibling)][nbig:]
            for nm, res in zip(names[nbig:], adam_shard_small(items, "adam_small_" + tag)):
                sharded[nm] = res

    odd_names = ["w_out_o", "w_in_o"]
    even_names = ["w_in_e", "w_out_e", "rw_w2", "rw_a2", "sg_ln_g", "sg_ln_b"]
    odd_landed = send_wait(started["odd"], started["even_small"][-1], True, CHIPS, "odd_grads_wait")
    odd_swap = sum_and_swap(odd_names, odd_landed, "odd")
    done = odd_swap[-1]

    def wmv_of(*arrs, view=lambda a: a):
        return tuple(view(a) for a in arrs)

    vec = lambda a: a.reshape(1, -1)
    groups = {
        "odd": (["sg_w", "sg_b", "final_g", "norm_g1"],
                [wmv_of(sg_w, m_sg_w, v_sg_w, view=lambda a: a.reshape(NG * SGC, SGC)),
                 wmv_of(sg_b, m_sg_b, v_sg_b, view=lambda a: a[0]),
                 [wmv_of(final_g, m_final_g, v_final_g, view=vec),
                  wmv_of(norm_g, m_norm_g, v_norm_g, view=lambda a: a[1:2])]]),
        "even": (["att_bias", "norm_g0", "shift_mu", "rw_w0", "rw_a0", "rw_kk", "rw_ka", "rw_rk", "rw_lnx_g",
                  "rw_lnx_b", "loss"],
                 [wmv_of(att_bias, m_att_bias, v_att_bias, view=lambda a: a[0]),
                  [wmv_of(norm_g, m_norm_g, v_norm_g, view=lambda a: a[0:1]),
                   wmv_of(shift_mu, m_shift_mu, v_shift_mu), wmv_of(rw_w0, m_rw_w0, v_rw_w0),
                   wmv_of(rw_a0, m_rw_a0, v_rw_a0), wmv_of(rw_kk, m_rw_kk, v_rw_kk), wmv_of(rw_ka, m_rw_ka, v_rw_ka),
                   wmv_of(rw_rk, m_rw_rk, v_rw_rk, view=vec), wmv_of(rw_lnx_g, m_rw_lnx_g, v_rw_lnx_g),
                   wmv_of(rw_lnx_b, m_rw_lnx_b, v_rw_lnx_b), (None, None, None)]]),
    }
    rep = {}
    for layer in ("odd", "even"):
        nms, params = groups[layer]
        gathered_g = send_wait(started[layer + "_small"], done, False, EVERY, layer + "_small_grads_wait")
        for nm, res in zip(nms, adam_replicated(gathered_g, params, "adam_" + layer + "_small")):
            rep[nm] = res
        done = rep[nms[0]][0]
    native = {"sg_w": sg_w.shape, "sg_b": sg_b.shape, "final_g": final_g.shape, "rw_rk": rw_rk.shape,
              "att_bias": att_bias.shape}
    for nm, shape in native.items():
        rep[nm] = [a.reshape(shape) for a in rep[nm]]
    rep["norm_g"] = [jnp.concatenate([a, b], axis=0) for a, b in zip(rep["norm_g0"], rep["norm_g1"])]
    even_landed = send_wait(started["even"], done, True, CHIPS, "even_grads_wait")
    even_swap = sum_and_swap(even_names, even_landed, "even")
    update(odd_names, odd_swap, even_swap[-1], "odd")
    update(even_names, even_swap, sharded["w_in_o"][0], "even")

    order = ["norm_g", "w_in_e", "shift_mu", "rw_w0", "rw_w2", "rw_a0", "rw_a2", "rw_kk", "rw_ka", "rw_rk",
             "rw_lnx_g", "rw_lnx_b", "att_bias", "w_out_e", "w_in_o", "sg_ln_g", "sg_ln_b", "sg_w", "sg_b",
             "w_out_o", "final_g"]
    results = {**sharded, **rep}
    outs = [rep["loss"][0][0, 0], dx.reshape(NSEQ, SEQ, D)]
    for kind in range(4):
        outs += [results[nm][kind] for nm in order]
    return tuple(outs)


def _local_step(x2, tgt, wie_t, late_weights, w2, a2, sglg, sglb, norm_g, shift_mu, rw_w0, rw_a0, rw_kk, rw_ka, rw_rk,
                rw_lnx_g, rw_lnx_b, att_bias, sg_w, sg_b, final_g, first_after=None, on_odd_grads=None,
                on_even_grads=None, on_small_grads=None):
    zl = jnp.zeros((LORA, W), F32)
    w2x = jnp.concatenate([w2, zl], axis=0)
    a2x = jnp.concatenate([zl, a2], axis=0)
    rk = rw_rk.reshape(1, W)
    pos = np.arange(SGC)
    sg_mask = jnp.asarray(((pos[None, :] // L) <= (pos[:, None] // L)).astype(np.float32))
    wm = (sg_w[0] * sg_mask[None]).astype(BF16)
    sgb_t = sg_b[0].T

    xn0, ps, ga, q, kb, vb, gb = ln_in_proj(x2, norm_g[0:1], wie_t, EVEN_SPLITS, "in_proj_even", after=first_after,
                                            w_t=True, bf16_pieces=(2, 3, 4))
    r, lw, k2, v, aa, bb = even_prep(ps, shift_mu, rw_w0, w2x, rw_a0, a2x, rw_kk, rw_ka)
    y, rw_saved = rwkv_fwd(r, lw, k2, v, aa, bb)
    bias = bias_expand(att_bias[0])

    o = attention_fwd(q, kb, vb, bias)
    woe = late_weights("even", o)
    h1, zt = even_post(y, r, k2, v, ga, o, gb, rw_lnx_g, rw_lnx_b, rk, x2, woe)
    woe, wio, woo = late_weights("odd", h1)
    xn1, u, vv, gt = ln_in_proj(h1, norm_g[1:2], wio, ODD_SPLITS, "in_proj_odd")
    dh2, loss_part, d_final_g, z2t = gmlp_fwd_loss(u, vv, gt, sglg, sglb, wm, sgb_t, h1, woo, final_g[None], tgt)

    du, dvv, dgt, d_sglg, d_sglb, d_wm, d_sgb_t, d_woo = gmlp_bwd(u, vv, gt, sglg, sglb, wm, sgb_t, dh2, z2t, woo)
    dp_odd = [du, dvv, dgt]
    d_wio = matmul_acc_chips(xn1, dp_odd, "in_proj_odd_dw")
    token = on_odd_grads(d_woo, d_wio) if on_odd_grads else None
    dh1, d_g1 = in_proj_bwd_x(h1, norm_g[1:2], wio, dp_odd, dh2, "in_proj_odd_bwd", after=token)
    odd_small = [d_wm * sg_mask[None], d_sgb_t.T, d_final_g, d_g1]
    token = on_small_grads("odd", odd_small) if on_small_grads else None
    dy, dr2, dk22, dv2, dga, do, dgb, d_lng, d_lnb, d_rk, d_woe = even_post_bwd(
        y, r, k2, v, ga, o, gb, rw_lnx_g, rw_lnx_b, rk, dh1, zt, woe, after=token)
    dq, dkb, dvb, dbias = attention_bwd(q, kb, vb, bias, do)
    dbias = sum(dbias[:, i * 2 * L:(i + 1) * 2 * L, i * L:i * L + BAND] for i in range(ATT_Q))
    d_att_bias = bias_grad(dbias.reshape(NH, L, BAND))
    dr, dlw, dk2, dv, daa, dbb = rwkv_bwd(r, lw, k2, aa, bb, rw_saved, dy)
    dps, d_mu, d_w0, d_w2x, d_a0, d_a2x, d_kk, d_ka = even_prep_bwd(
        ps, shift_mu, rw_w0, w2x, rw_a0, a2x, rw_kk, rw_ka, dr, dlw, dk2, dv, daa, dbb, dr2, dk22, dv2)
    dp_even = [dps, dga, dq, dkb, dvb, dgb]
    d_wie = matmul_acc_chips(xn0, dp_even, "in_proj_even_dw", add_cores=on_even_grads is not None)
    big_g = (d_wie, d_woe, d_wio, d_woo, d_w2x[:LORA], d_a2x[LORA:], d_sglg, d_sglb)
    token = on_even_grads(big_g) if on_even_grads else None
    dx, d_g0 = in_proj_bwd_x(x2, norm_g[0:1], wie_t, dp_even, dh1, "in_proj_even_bwd", after=token, w_t=True)
    even_small = [d_g0, d_mu, d_w0, d_a0, d_kk, d_ka, d_rk, d_lng, d_lnb, d_att_bias]
    if on_small_grads:
        on_small_grads("even", even_small + [loss_part[0:1, :]])
    rep_g = [jnp.concatenate([d_g0, d_g1], axis=0)] + even_small[1:] + odd_small[:3]
    return loss_part[0, 0], dx, big_g, rep_g
```

```python
import functools
import math

import jax
import jax.numpy as jnp
import numpy as np
from jax import lax
from jax.experimental import pallas as pl
from jax.experimental.pallas import tpu as pltpu

F32 = jnp.float32
BF16 = jnp.bfloat16
HI = lax.Precision.HIGHEST

D = 1024
SEQ = 2048
NSEQ = 2
T = NSEQ * SEQ
HD = 64
NH = 8
W = 512
SHIFT = 1664
LORA = 64
EVEN_IN = 4224
ODD_IN = 3072
L = 64
NC = SEQ // L
LEFT = 8
BAND = (LEFT + 1) * L
CLIP = 128
SGC = 128
NG = 8
RMS_EPS = 1e-6
LN_EPS = 1e-5
GN_EPS = 64e-5
NEG = -1e30
VMEM_BIG = 56 * 1024 * 1024

ADAM_LR = 0.001
ADAM_B1 = 0.9
ADAM_B2 = 0.999
ADAM_EPS = 1e-08
ADAM_WD = 0.01
ADAM_STEP = 10

MESH = pl.DeviceIdType.MESH


def _bdot(a, b):
    return jnp.dot(a.astype(BF16), b.astype(BF16), preferred_element_type=F32)


def _bdot_nt(a, b):
    return lax.dot_general(a.astype(BF16), b.astype(BF16), (((1,), (1,)), ((), ())), preferred_element_type=F32)


def _bdot_tn(a, b):
    return lax.dot_general(a.astype(BF16), b.astype(BF16), (((0,), (0,)), ((), ())), preferred_element_type=F32)


def _hdot(a, b):
    return jnp.dot(a, b, precision=HI, preferred_element_type=F32)


def _hdot_nt(a, b):
    return lax.dot_general(a, b, (((1,), (1,)), ((), ())), precision=HI, preferred_element_type=F32)


def _hdot_tn(a, b):
    return lax.dot_general(a, b, (((0,), (0,)), ((), ())), precision=HI, preferred_element_type=F32)


def _iota2(shape, dim):
    return lax.broadcasted_iota(jnp.int32, shape, dim)


def _head_blockdiag():
    r = _iota2((2 * HD, 2 * HD), 0) // HD
    c = _iota2((2 * HD, 2 * HD), 1) // HD
    return (r == c).astype(BF16)


def _headsum_impl(x, bd):
    hi = x.astype(BF16)
    mid = (x - hi.astype(F32)).astype(BF16)
    n = bd.shape[0]
    out = [jnp.dot(hi[:, i:i + n], bd, preferred_element_type=F32) + jnp.dot(mid[:, i:i + n], bd, preferred_element_type=F32)
           for i in range(0, x.shape[1], n)]
    return jnp.concatenate(out, axis=-1)


@jax.custom_vjp
def _headsum(x, bd):
    return _headsum_impl(x, bd)


def _headsum_fwd(x, bd):
    return _headsum_impl(x, bd), bd


def _headsum_bwd(bd, ct):
    return _headsum_impl(ct, bd), None


_headsum.defvjp(_headsum_fwd, _headsum_bwd)


def _silu(x):
    return x * jax.nn.sigmoid(x)


_GELU_C = math.sqrt(2.0 / math.pi)


def _gelu(x):
    return 0.5 * x * (1.0 + jnp.tanh(_GELU_C * (x + 0.044715 * (x * x * x))))


def _silu_both(x):
    s = jax.nn.sigmoid(x)
    xs = x * s
    return xs, s + xs * (1.0 - s)


def _gelu_both(x):
    x2 = x * x
    t = jnp.tanh(_GELU_C * (x + 0.044715 * (x2 * x)))
    half = 0.5 * (1.0 + t)
    return x * half, half + 0.5 * x * (1.0 - t * t) * _GELU_C * (1.0 + 3.0 * 0.044715 * x2)


def _softplus(x):
    return jnp.maximum(x, 0.0) + jnp.log(1.0 + jnp.exp(-jnp.abs(x)))


def _cparams(sem, vmem=None):
    return pltpu.CompilerParams(dimension_semantics=sem, vmem_limit_bytes=vmem)


def _row_spec(tm, width):
    return pl.BlockSpec((tm, width), lambda i: (i, 0))


def _col_spec(height, tm):
    return pl.BlockSpec((height, tm), lambda i: (0, i))


def _const_spec(shape):
    nd = len(shape)
    return pl.BlockSpec(shape, lambda *_: (0,) * nd)


def _weight_dims(w_bf, w_t):
    if w_bf.ndim == 3:
        return None, w_bf.shape[0] * w_bf.shape[2]
    return (((1,), (1,)), ((), ())) if w_t else (((1,), (0,)), ((), ())), w_bf.shape[0 if w_t else 1]


def _proj(xn, w_ref, dims):
    if dims is None:
        return jnp.concatenate([jnp.dot(xn, w_ref[s], preferred_element_type=F32) for s in range(w_ref.shape[0])],
                               axis=-1)
    return lax.dot_general(xn, w_ref[...], dims, preferred_element_type=F32)


def _proj_back(dp, w_ref, w_t):
    nt = (((1,), (1,)), ((), ()))
    if len(w_ref.shape) == 3:
        nb = w_ref.shape[2]
        parts = [lax.dot_general(dp[:, s * nb:(s + 1) * nb], w_ref[s], nt, preferred_element_type=F32)
                 for s in range(w_ref.shape[0])]
        return sum(parts[1:], parts[0])
    return lax.dot_general(dp, w_ref[...], (((1,), (0,)), ((), ())) if w_t else nt, preferred_element_type=F32)


WEIGHT_CHUNKS = 4
LANES = 128


def _streams(w_bf, w_t):
    return w_t and w_bf.ndim == 2


def _stream_weight(w_hbm, w_vmem, sems):
    n = w_hbm.shape[0]
    size = n // LANES // WEIGHT_CHUNKS * LANES
    bounds = [s * size for s in range(WEIGHT_CHUNKS)] + [n]
    chunks = list(zip(bounds[:-1], bounds[1:]))
    copies = [pltpu.make_async_copy(w_hbm.at[pl.ds(a, b - a)], w_vmem.at[pl.ds(a, b - a)], sems.at[s])
              for s, (a, b) in enumerate(chunks)]
    first = pl.program_id(0) == 0

    @pl.when(first)
    def _():
        for cp in copies:
            cp.start()

    def arrived(s):
        @pl.when(first)
        def _():
            copies[s].wait()

        return w_vmem[chunks[s][0]:chunks[s][1]]

    return chunks, arrived


def _stream_scratch(w_bf):
    return [pltpu.VMEM(w_bf.shape, w_bf.dtype), pltpu.SemaphoreType.DMA((WEIGHT_CHUNKS,))]


def ln_in_proj(x, g, w_bf, splits, name, after=None, w_t=False, bf16_pieces=()):
    dims, n = _weight_dims(w_bf, w_t)
    stream = _streams(w_bf, w_t)
    dtypes = [BF16 if i in bf16_pieces else F32 for i in range(len(splits))]
    tm = 512 if n <= ODD_IN else 256
    spans = []
    o = 0
    for s in splits:
        spans.append((o, o + s))
        o += s
    assert o == n
    extra_specs, extra = _after_operand(after)

    def body(x_ref, g_ref, w_ref, *rest):
        xn_ref, outs = rest[len(extra)], rest[len(extra) + 1:len(extra) + 1 + len(splits)]
        if stream:
            chunks, arrived = _stream_weight(w_ref, rest[-2], rest[-1])
        xv = x_ref[...]
        rstd = lax.rsqrt(jnp.mean(xv * xv, axis=-1, keepdims=True) + RMS_EPS)
        xn = (xv * rstd * g_ref[...]).astype(BF16)
        xn_ref[...] = xn.T
        if stream:
            p = jnp.concatenate([lax.dot_general(xn, arrived(s), dims, preferred_element_type=F32)
                                 for s in range(len(chunks))], axis=-1)
        else:
            p = _proj(xn, w_ref, dims)
        for o_ref, (a, b) in zip(outs, spans):
            o_ref[...] = p[:, a:b].astype(o_ref.dtype)

    return pl.pallas_call(
        body, grid=(T // tm,), name=name,
        in_specs=[_row_spec(tm, D), _const_spec((1, D)), ANY if stream else _const_spec(w_bf.shape)] + extra_specs,
        out_specs=[_col_spec(D, tm)] + [_row_spec(tm, s) for s in splits],
        out_shape=[jax.ShapeDtypeStruct((D, T), BF16)]
        + [jax.ShapeDtypeStruct((T, s), dt) for s, dt in zip(splits, dtypes)],
        scratch_shapes=_stream_scratch(w_bf) if stream else [],
        compiler_params=_cparams(("arbitrary" if stream else "parallel",), VMEM_BIG),
    )(x, g, w_bf, *extra)


def in_proj_bwd_x(x, g, w_bf, dps, dres, name, after=None, w_t=False):
    tm = 512
    widths = [d.shape[1] for d in dps]
    extra_specs, extra = _after_operand(after)
    stream = _streams(w_bf, w_t)

    def body(x_ref, g_ref, w_ref, dres_ref, *rest):
        dp_refs = rest[:len(widths)]
        dx_ref, dg_ref = rest[len(widths) + len(extra):][:2]
        dp = jnp.concatenate([r[...] for r in dp_refs], axis=-1)
        if stream:
            chunks, arrived = _stream_weight(w_ref, rest[-2], rest[-1])
            parts = [jnp.dot(dp[:, a:b], arrived(s), preferred_element_type=F32) for s, (a, b) in enumerate(chunks)]
            dxn = sum(parts[1:], parts[0])
        else:
            dxn = _proj_back(dp, w_ref, w_t)
        xv = x_ref[...]
        rstd = lax.rsqrt(jnp.mean(xv * xv, axis=-1, keepdims=True) + RMS_EPS)
        xhat = xv * rstd
        dgp = jnp.sum(dxn * xhat, axis=0, keepdims=True)

        @pl.when(pl.program_id(0) == 0)
        def _():
            dg_ref[...] = jnp.zeros_like(dg_ref)

        dg_ref[...] += dgp
        dxh = dxn * g_ref[...]
        dx_ref[...] = dres_ref[...] + rstd * (dxh - xhat * jnp.mean(dxh * xhat, axis=-1, keepdims=True))

    return pl.pallas_call(
        body, grid=(T // tm,), name=name,
        in_specs=[_row_spec(tm, D), _const_spec((1, D)), ANY if stream else _const_spec(w_bf.shape), _row_spec(tm, D)]
        + [_row_spec(tm, s) for s in widths] + extra_specs,
        out_specs=[_row_spec(tm, D), _const_spec((1, D))],
        out_shape=[jax.ShapeDtypeStruct((T, D), F32), jax.ShapeDtypeStruct((1, D), F32)],
        scratch_shapes=_stream_scratch(w_bf) if stream else [],
        compiler_params=_cparams(("arbitrary",), VMEM_BIG),
    )(x, g, w_bf, dres, *dps, *extra)


def _after_operand(after):
    return ([ANY], [after]) if after is not None else ([], [])


def matmul_acc_chips(at_bf, pieces, name, after=None, add_cores=False):
    k = at_bf.shape[0]
    widths = [p.shape[1] for p in pieces]
    nb = sum(widths) // NCHIP
    tm = 512
    steps = T // tm
    half = k // 2
    extra_specs, extra = _after_operand(after)

    def body(a_ref, *rest):
        o_ref, acc = rest[len(widths) + len(extra):][:2]

        @pl.when(pl.program_id(0) == 0)
        def _():
            acc[...] = jnp.zeros_like(acc)

        a = a_ref[...]
        b = jnp.concatenate([r[...] for r in rest[:len(widths)]], axis=-1)
        for s in range(NCHIP):
            acc[s] += jnp.dot(a, b[:, s * nb:(s + 1) * nb], preferred_element_type=F32)

        @pl.when(pl.program_id(0) == steps - 1)
        def _():
            if not add_cores:
                o_ref[...] = acc[...].astype(BF16)
            else:
                give, got, send, recv = rest[-4:]
                x, y, c = lax.axis_index("x"), lax.axis_index("y"), lax.axis_index("c")
                theirs = pl.multiple_of((1 - c) * half, half)
                mine = pl.multiple_of(c * half, half)
                copies = []
                for s in range(NCHIP):
                    give[s] = acc[s, pl.ds(theirs, half), :].astype(BF16)
                    copies.append(pltpu.make_async_remote_copy(
                        src_ref=give.at[s], dst_ref=got.at[s], send_sem=send.at[s], recv_sem=recv.at[s],
                        device_id=(x, y, 1 - c), device_id_type=MESH))
                    copies[s].start()
                for s in range(NCHIP):
                    copies[s].wait()
                    o_ref[s] = (acc[s, pl.ds(mine, half), :] + got[s].astype(F32)).astype(BF16)

    out_rows = half if add_cores else k
    exchange = ([pltpu.VMEM((NCHIP, half, nb), BF16)] * 2 + [pltpu.SemaphoreType.DMA((NCHIP,))] * 2 if add_cores
                else [])
    return pl.pallas_call(
        body, grid=(steps,), name=name,
        in_specs=[_col_spec(k, tm)] + [_row_spec(tm, w_) for w_ in widths] + extra_specs,
        out_specs=_const_spec((NCHIP, out_rows, nb)),
        out_shape=jax.ShapeDtypeStruct((NCHIP, out_rows, nb), BF16),
        scratch_shapes=[pltpu.VMEM((NCHIP, k, nb), F32)] + exchange,
        compiler_params=_cparams(("arbitrary",), VMEM_BIG),
    )(at_bf, *pieces, *extra)


def _out_proj_back(dh_ref, zt_ref, w_ref, dw_ref, acc_ref):
    dhb = dh_ref[...].astype(BF16)

    @pl.when(pl.program_id(0) == 0)
    def _():
        acc_ref[...] = jnp.zeros_like(acc_ref)

    acc_ref[...] += jnp.dot(zt_ref[...], dhb, preferred_element_type=F32)

    @pl.when(pl.program_id(0) == pl.num_programs(0) - 1)
    def _():
        dw_ref[...] = acc_ref[...].astype(dw_ref.dtype)

    return lax.dot_general(dhb, w_ref[...], (((1,), (1,)), ((), ())), preferred_element_type=F32)


PREP_TM = 512
PREP_NB = SEQ // PREP_TM


def _prep_elem(k, wl, apre, kkw, kaw, bd):
    wraw = -_softplus(-wl) - 0.5
    lw = -jnp.exp(wraw)
    asig = jax.nn.sigmoid(apre)
    kkr = k * kkw
    nrm = jnp.maximum(jnp.sqrt(_headsum(kkr * kkr, bd)), 1e-12)
    kk = kkr / nrm
    k2 = k * (1.0 + (asig - 1.0) * kaw)
    return lw, k2, -kk, kk * asig


def _prep_elem_bwd(k, wl, apre, kkw, kaw, bd, dlw, dk2, daa, dbb):
    s = -wl
    sp = _softplus(s)
    dwl = dlw * (-jnp.exp(-sp - 0.5)) * jnp.exp(s - sp)
    asig = jax.nn.sigmoid(apre)
    kkr = k * kkw
    root = jnp.sqrt(_headsum(kkr * kkr, bd))
    inv = 1.0 / jnp.maximum(root, 1e-12)
    kk = kkr * inv
    dkk = dbb * asig - daa
    dap = (dbb * kk + dk2 * k * kaw) * asig * (1.0 - asig)
    through_norm = jnp.where(root > 1e-12, kk * _headsum(dkk * kkr, bd) * inv, 0.0)
    dkkr = inv * (dkk - through_norm)
    gain = 1.0 + (asig - 1.0) * kaw
    dk = dkkr * kkw + dk2 * gain
    dkkw = jnp.sum(dkkr * k, axis=0, keepdims=True)
    dkaw = jnp.sum(dk2 * k * (asig - 1.0), axis=0, keepdims=True)
    return dk, dwl, dap, dkkw, dkaw


def _shifted(ps_ref, prev_ref, mu, blk):
    p = ps_ref[...]
    first = (blk % PREP_NB) == 0
    prev_row = jnp.where(first, 0.0, prev_ref[7:8, :])
    rolled = pltpu.roll(p, 1, 0)
    p_prev = jnp.where(_iota2(p.shape, 0) == 0, prev_row, rolled)
    return p, p_prev, p + (p_prev - p) * mu


def _prev_spec(width, blk_of):
    return pl.BlockSpec((8, width), lambda i: (jnp.maximum(blk_of(i) * (PREP_TM // 8) - 1, 0), 0))


def even_prep(ps, mu, w0, w2x, a0, a2x, kkw, kaw):
    tm = PREP_TM

    def body(ps_ref, prev_ref, mu_ref, w0_ref, w2_ref, a0_ref, a2_ref, kk_ref, ka_ref,
             r_ref, lw_ref, k2_ref, v_ref, aa_ref, bb_ref):
        _, _, s = _shifted(ps_ref, prev_ref, mu_ref[...], pl.program_id(0))
        wa = s[:, 3 * W:]
        wl = w0_ref[...] + _bdot(jnp.tanh(wa), w2_ref[...])
        apre = a0_ref[...] + _bdot(wa, a2_ref[...])
        lw, k2, aa, bb = _prep_elem(s[:, W:2 * W], wl, apre, kk_ref[...], ka_ref[...], _head_blockdiag())
        r_ref[...] = s[:, 0:W]
        v_ref[...] = s[:, 2 * W:3 * W]
        lw_ref[...] = lw
        k2_ref[...] = k2
        aa_ref[...] = aa
        bb_ref[...] = bb

    vec = _const_spec((1, W))
    return pl.pallas_call(
        body, grid=(T // tm,), name="even_prep",
        in_specs=[_row_spec(tm, SHIFT), _prev_spec(SHIFT, lambda i: i), _const_spec((1, SHIFT)), vec,
                  _const_spec((2 * LORA, W)), vec, _const_spec((2 * LORA, W)), vec, vec],
        out_specs=[_row_spec(tm, W)] * 6,
        out_shape=[jax.ShapeDtypeStruct((T, W), F32)] * 6,
        compiler_params=_cparams(("parallel",), VMEM_BIG),
    )(ps, ps, mu, w0, w2x, a0, a2x, kkw, kaw)


def even_prep_bwd(ps, mu, w0, w2x, a0, a2x, kkw, kaw, dr, dlw, dk2, dv, daa, dbb, dr2, dk22, dv2):
    tm = PREP_TM
    nb = T // tm
    rev = lambda i: nb - 1 - i

    def body(ps_ref, prev_ref, mu_ref, w0_ref, w2_ref, a0_ref, a2_ref, kk_ref, ka_ref,
             dr_ref, dlw_ref, dk2_ref, dv_ref, daa_ref, dbb_ref, dr2_ref, dk22_ref, dv2_ref,
             dps_ref, dmu_ref, dw0_ref, dw2_ref, da0_ref, da2_ref, dkk_ref, dka_ref, carry):
        i = pl.program_id(0)
        blk = rev(i)
        mu_v = mu_ref[...]
        p, p_prev, s = _shifted(ps_ref, prev_ref, mu_v, blk)
        wa = s[:, 3 * W:]
        th = jnp.tanh(wa)
        wl = w0_ref[...] + _bdot(th, w2_ref[...])
        apre = a0_ref[...] + _bdot(wa, a2_ref[...])
        bd = _head_blockdiag()
        k = s[:, W:2 * W]
        dk, dwl, dap, dkkw, dkaw = _prep_elem_bwd(k, wl, apre, kk_ref[...], ka_ref[...], bd, dlw_ref[...],
                                                  dk2_ref[...] + dk22_ref[...], daa_ref[...], dbb_ref[...])
        dwa = _bdot_nt(dwl, w2_ref[...]) * (1.0 - th * th) + _bdot_nt(dap, a2_ref[...])
        ds = jnp.concatenate([dr_ref[...] + dr2_ref[...], dk, dv_ref[...] + dv2_ref[...], dwa], axis=-1)

        @pl.when(i == 0)
        def _():
            for ref in (dmu_ref, dw0_ref, dw2_ref, da0_ref, da2_ref, dkk_ref, dka_ref, carry):
                ref[...] = jnp.zeros_like(ref)

        dmu_ref[...] += jnp.sum(ds * (p_prev - p), axis=0, keepdims=True)
        dw0_ref[...] += jnp.sum(dwl, axis=0, keepdims=True)
        da0_ref[...] += jnp.sum(dap, axis=0, keepdims=True)
        dw2_ref[...] += _bdot_tn(th, dwl)
        da2_ref[...] += _bdot_tn(wa, dap)
        dkk_ref[...] += dkkw
        dka_ref[...] += dkaw
        dsm = ds * mu_v
        last = (blk % PREP_NB) == PREP_NB - 1
        nxt = jnp.where(last, 0.0, carry[0:1, :])
        up = pltpu.roll(dsm, tm - 1, 0)
        up = jnp.where(_iota2(up.shape, 0) == tm - 1, nxt, up)
        dps_ref[...] = (ds - dsm + up).astype(BF16)
        carry[0:1, :] = dsm[0:1, :]

    vec = _const_spec((1, W))
    rrow = lambda width: pl.BlockSpec((tm, width), lambda i: (rev(i), 0))
    return pl.pallas_call(
        body, grid=(nb,), name="even_prep_bwd",
        in_specs=[rrow(SHIFT), _prev_spec(SHIFT, rev), _const_spec((1, SHIFT)), vec,
                  _const_spec((2 * LORA, W)), vec, _const_spec((2 * LORA, W)), vec, vec] + [rrow(W)] * 9,
        out_specs=[rrow(SHIFT), _const_spec((1, SHIFT)), vec, _const_spec((2 * LORA, W)), vec,
                   _const_spec((2 * LORA, W)), vec, vec],
        out_shape=[jax.ShapeDtypeStruct((T, SHIFT), BF16), jax.ShapeDtypeStruct((1, SHIFT), F32),
                   jax.ShapeDtypeStruct((1, W), F32), jax.ShapeDtypeStruct((2 * LORA, W), F32),
                   jax.ShapeDtypeStruct((1, W), F32), jax.ShapeDtypeStruct((2 * LORA, W), F32),
                   jax.ShapeDtypeStruct((1, W), F32), jax.ShapeDtypeStruct((1, W), F32)],
        scratch_shapes=[pltpu.VMEM((8, SHIFT), F32)],
        compiler_params=_cparams(("arbitrary",), VMEM_BIG),
    )(ps, ps, mu, w0, w2x, a0, a2x, kkw, kaw, dr, dlw, dk2, dv, daa, dbb, dr2, dk22, dv2)


NPAIR = NH // 2
PW = 2 * HD


def _pair_cols(p):
    return slice(p * PW, (p + 1) * PW)


def _pairs(a):
    return [a[:, _pair_cols(p)] for p in range(NPAIR)]


def _stack_pair(a):
    first = _iota2(a.shape, 1) < HD
    zero = jnp.zeros_like(a)
    return jnp.concatenate([jnp.where(first, a, zero), jnp.where(first, zero, a)], axis=0)


def _unstack_pair(a):
    n = a.shape[0] // 2
    return jnp.where(_iota2((n, PW), 1) < HD, a[:n], a[n:])


def _fold_pair(a):
    n = a.shape[0] // 2
    return a[:n] + a[n:]


def _chunk_masks():
    n = 4 * L
    row = _iota2((n, n), 0)
    col = _iota2((n, n), 1)
    same = ((row // L) & 1) == ((col // L) & 1)
    ri = row & (L - 1)
    ci = col & (L - 1)
    keep = same & (((row < 2 * L) & (ri > ci)) | ((row >= 2 * L) & (ri >= ci)))
    r1 = _iota2((L, L), 0)
    c1 = _iota2((L, L), 1)
    r2 = _iota2((2 * L, 2 * L), 0)
    c2 = _iota2((2 * L, 2 * L), 1)
    return keep.astype(F32), (r1 >= c1).astype(F32), (r2 == c2).astype(F32)


def _scaled(r, lw, k2, aa, bb, tri):
    g = _hdot(tri, lw)
    eg = jnp.exp(g)
    eng = jnp.exp(-g)
    egp = jnp.exp(g - lw)
    return eg, eng, egp, aa * egp, r * eg, bb * eng, k2 * eng


def _head_cols(h):
    return slice(h * HD, (h + 1) * HD)


def _per_head(a):
    return [a[:, _head_cols(h)] for h in range(NH)]


def _pairs_operands(at, rt, bt, kt):
    x = [jnp.concatenate([_stack_pair(a), _stack_pair(r)], axis=0).astype(BF16) for a, r in zip(_pairs(at), _pairs(rt))]
    yk = [jnp.concatenate([_stack_pair(b), _stack_pair(k)], axis=0).astype(BF16) for b, k in zip(_pairs(bt), _pairs(kt))]
    return x, yk


def _pairs_matrices(x, yk, keep, eye):
    m = [_bdot_nt(a, b) * keep for a, b in zip(x, yk)]
    p = [a[:2 * L, :2 * L] for a in m]
    tinv = [eye + a for a in p]
    for _ in range(5):
        p = [_bdot(a, a) for a in p]
        tinv = [t + _bdot(t, a) for t, a in zip(tinv, p)]
    return [a.astype(BF16) for a in m], [a.astype(BF16) for a in tinv]


def _pairs_fwd(x, yk, m, tinv, vw, s0, egl):
    xh = [_bdot_nt(a, s) for a, s in zip(x, s0)]
    u = [_bdot(t, h[:2 * L] + _bdot(a[:2 * L, 2 * L:], w)) for t, h, a, w in zip(tinv, xh, m, vw)]
    uv = [jnp.concatenate([a, w], axis=0).astype(BF16) for a, w in zip(u, vw)]
    y = [h[2 * L:] + _bdot(a[2 * L:], w) for h, a, w in zip(xh, m, uv)]
    sn = [e * (s + _bdot_tn(w, b)) for e, s, w, b in zip(egl, s0, uv, yk)]
    return y, sn, uv


def _pairs_bwd(x, yk, m, tinv, uv, s0, sn, egl, dyw, dsn, keep):
    dzs = [d * e for d, e in zip(dsn, egl)]
    dgl = [jnp.sum(d * s, axis=0, keepdims=True) for d, s in zip(dsn, sn)]
    dyb = [a.astype(BF16) for a in dyw]
    t1 = [_bdot_tn(a[2 * L:], d) for a, d in zip(m, dyb)]
    t2 = [_bdot_nt(b, d) for b, d in zip(yk, dzs)]
    drhs = [_bdot_tn(t, a[:2 * L] + b[:2 * L]) for t, a, b in zip(tinv, t1, t2)]
    dv = [a[2 * L:] + b[2 * L:] + _bdot_tn(c[:2 * L, 2 * L:], d) for a, b, c, d in zip(t1, t2, m, drhs)]
    gg = [jnp.concatenate([a, b], axis=0).astype(BF16) for a, b in zip(drhs, dyw)]
    ds0 = [d + _bdot_tn(g, a) for d, g, a in zip(dzs, gg, x)]
    dm = [_bdot_nt(g, w) * keep for g, w in zip(gg, uv)]
    dx = [_bdot(g, s) + _bdot(d, b) for g, s, d, b in zip(gg, s0, dm, yk)]
    dyk = [_bdot_tn(d, a) + _bdot(w, z) for d, a, w, z in zip(dm, x, uv, dzs)]
    return dx, dyk, dv, dgl, ds0


STATE_SHAPE = (NPAIR * PW, PW)
M_SHAPE = (4 * L, NPAIR * 4 * L)
TINV_SHAPE = (2 * L, NPAIR * 2 * L)


def _rows_of(a, n):
    return [a[i * n:(i + 1) * n, :] for i in range(NPAIR)]


def _both(f):
    out = []
    for s in range(NSEQ):
        out += f(s)
    return out


def _seq_view(a):
    return a.reshape(NSEQ, SEQ, a.shape[-1])


UV_SHAPE = (4 * L, NPAIR * PW)
RW_CHUNKS = 2


def rwkv_fwd(r, lw, k2, v, aa, bb):
    def body(r_ref, lw_ref, k2_ref, v_ref, aa_ref, bb_ref, y_ref, hs_ref, hn_ref, m_ref, t_ref, uv_ref, state):
        @pl.when(pl.program_id(0) == 0)
        def _():
            state[...] = jnp.zeros_like(state)

        keep, tri, eye = _chunk_masks()
        where = [(j, s) for j in range(RW_CHUNKS) for s in range(NSEQ)]
        rows = lambda j: slice(j * L, (j + 1) * L)
        sc = [_scaled(r_ref[s, rows(j)], lw_ref[s, rows(j)], k2_ref[s, rows(j)], aa_ref[s, rows(j)],
                      bb_ref[s, rows(j)], tri) for j, s in where]
        ops = [_pairs_operands(*a[3:]) for a in sc]
        m, tinv = _pairs_matrices([a for o in ops for a in o[0]], [a for o in ops for a in o[1]], keep, eye)
        s_cur = [state[s] for s in range(NSEQ)]
        for j in range(RW_CHUNKS):
            mine = slice(j * NSEQ * NPAIR, (j + 1) * NSEQ * NPAIR)
            x = [a for o in ops[j * NSEQ:(j + 1) * NSEQ] for a in o[0]]
            yk = [a for o in ops[j * NSEQ:(j + 1) * NSEQ] for a in o[1]]
            vw = _both(lambda s: [_stack_pair(a) for a in _pairs(v_ref[s, rows(j)])])
            egl = _both(lambda s: _pairs(sc[j * NSEQ + s][0][L - 1:L, :]))
            y, sn, uv = _pairs_fwd(x, yk, m[mine], tinv[mine], vw, _both(lambda s: _rows_of(s_cur[s], PW)), egl)
            for s in range(NSEQ):
                ps = slice(s * NPAIR, (s + 1) * NPAIR)
                hs_ref[j, s] = s_cur[s]
                y_ref[s, rows(j)] = jnp.concatenate([_fold_pair(a) for a in y[ps]], axis=-1)
                m_ref[j, s] = jnp.concatenate(m[mine][ps], axis=-1)
                t_ref[j, s] = jnp.concatenate(tinv[mine][ps], axis=-1)
                uv_ref[j, s] = jnp.concatenate(uv[ps], axis=-1)
                s_cur[s] = jnp.concatenate(sn[ps], axis=0)
                hn_ref[j, s] = s_cur[s]
        for s in range(NSEQ):
            state[s] = s_cur[s]

    blk = pl.BlockSpec((NSEQ, RW_CHUNKS * L, W), lambda c: (0, c, 0))
    per_chunk = lambda shape: pl.BlockSpec((RW_CHUNKS, NSEQ) + shape, lambda c: (c, 0, 0, 0))
    saved_shapes = [(STATE_SHAPE, F32), (STATE_SHAPE, F32), (M_SHAPE, BF16), (TINV_SHAPE, BF16), (UV_SHAPE, BF16)]
    y, *saved = pl.pallas_call(
        body, grid=(NC // RW_CHUNKS,), name="rwkv_fwd",
        in_specs=[blk] * 6,
        out_specs=[blk] + [per_chunk(shape) for shape, _ in saved_shapes],
        out_shape=[jax.ShapeDtypeStruct((NSEQ, SEQ, W), F32)]
        + [jax.ShapeDtypeStruct((NC, NSEQ) + shape, dt) for shape, dt in saved_shapes],
        scratch_shapes=[pltpu.VMEM((NSEQ,) + STATE_SHAPE, F32)],
        compiler_params=_cparams(("arbitrary",), VMEM_BIG),
    )(*[_seq_view(a) for a in (r, lw, k2, v, aa, bb)])
    return y.reshape(T, W), saved


def rwkv_bwd(r, lw, k2, aa, bb, saved, dy):
    def body(r_ref, lw_ref, k2_ref, aa_ref, bb_ref, hs_ref, hn_ref, m_ref, t_ref, uv_ref, dy_ref,
             dr_ref, dlw_ref, dk2_ref, dv_ref, daa_ref, dbb_ref, dstate):
        @pl.when(pl.program_id(0) == 0)
        def _():
            dstate[...] = jnp.zeros_like(dstate)

        keep, tri, _ = _chunk_masks()
        sc = [_scaled(r_ref[s], lw_ref[s], k2_ref[s], aa_ref[s], bb_ref[s], tri) for s in range(NSEQ)]
        ops = [_pairs_operands(*sc[s][3:]) for s in range(NSEQ)]
        x, yk = _both(lambda s: ops[s][0]), _both(lambda s: ops[s][1])
        m = _both(lambda s: [m_ref[0, s][:, i * 4 * L:(i + 1) * 4 * L] for i in range(NPAIR)])
        tinv = _both(lambda s: [t_ref[0, s][:, i * 2 * L:(i + 1) * 2 * L] for i in range(NPAIR)])
        uv = _both(lambda s: _pairs(uv_ref[0, s]))
        dyw = _both(lambda s: [_stack_pair(a) for a in _pairs(dy_ref[s])])
        s0 = _both(lambda s: _rows_of(hs_ref[0, s], PW))
        sn = _both(lambda s: _rows_of(hn_ref[0, s], PW))
        dsn = _both(lambda s: _rows_of(dstate[s], PW))
        egl = _both(lambda s: _pairs(sc[s][0][L - 1:L, :]))
        dx, dyk, dvw, dgl, ds0 = _pairs_bwd(x, yk, m, tinv, uv, s0, sn, egl, dyw, dsn, keep)
        for s in range(NSEQ):
            mine = slice(s * NPAIR, (s + 1) * NPAIR)
            eg, eng, egp, at, rt, bt, kt = sc[s]
            dstate[s] = jnp.concatenate(ds0[mine], axis=0)
            dv_ref[s] = jnp.concatenate([_fold_pair(a) for a in dvw[mine]], axis=-1)
            dat = jnp.concatenate([_fold_pair(a[:2 * L]) for a in dx[mine]], axis=-1)
            drt = jnp.concatenate([_fold_pair(a[2 * L:]) for a in dx[mine]], axis=-1)
            dbt = jnp.concatenate([_fold_pair(a[:2 * L]) for a in dyk[mine]], axis=-1)
            dkt = jnp.concatenate([_fold_pair(a[2 * L:]) for a in dyk[mine]], axis=-1)
            dg = drt * rt - dbt * bt - dkt * kt
            dg = dg + jnp.where(_iota2(dg.shape, 0) == L - 1, jnp.concatenate(dgl[mine], axis=-1), 0.0)
            dgp = dat * at
            dlw_ref[s] = _hdot_tn(tri, dg + dgp) - dgp
            dr_ref[s] = drt * eg
            daa_ref[s] = dat * egp
            dbb_ref[s] = dbt * eng
            dk2_ref[s] = dkt * eng

    blk = pl.BlockSpec((NSEQ, L, W), lambda c: (0, NC - 1 - c, 0))
    per_chunk = lambda shape: pl.BlockSpec((1, NSEQ) + shape, lambda c: (NC - 1 - c, 0, 0, 0))
    outs = pl.pallas_call(
        body, grid=(NC,), name="rwkv_bwd",
        in_specs=[blk] * 5 + [per_chunk(a.shape[2:]) for a in saved] + [blk],
        out_specs=[blk] * 6,
        out_shape=[jax.ShapeDtypeStruct((NSEQ, SEQ, W), F32)] * 6,
        scratch_shapes=[pltpu.VMEM((NSEQ,) + STATE_SHAPE, F32)],
        compiler_params=_cparams(("arbitrary",)),
    )(*[_seq_view(a) for a in (r, lw, k2, aa, bb)], *saved, _seq_view(dy))
    return [a.reshape(T, W) for a in outs]


def _post_math(y, r, k2, v, ga, o, gb, lng, lnb, rk, bd):
    mu = _headsum(y, bd) * (1.0 / HD)
    yc = y - mu
    var = _headsum(yc * yc, bd) * (1.0 / HD)
    yn = yc * lax.rsqrt(var + GN_EPS) * lng + lnb
    bonus = _headsum(r * k2 * rk, bd) * v
    return (yn + bonus) * _silu(ga), o * _silu(gb)


def even_post(y, r, k2, v, ga, o, gb, lng, lnb, rk, h, w_bf):
    tm = 512

    def body(y_ref, r_ref, k2_ref, v_ref, ga_ref, o_ref, gb_ref, lng_ref, lnb_ref, rk_ref, h_ref, w_ref,
             ho_ref, zt_ref):
        ya, yb = _post_math(y_ref[...], r_ref[...], k2_ref[...], v_ref[...], ga_ref[...], o_ref[...], gb_ref[...],
                            lng_ref[...], lnb_ref[...], rk_ref[...], _head_blockdiag())
        z = jnp.concatenate([ya.astype(BF16), yb.astype(BF16)], axis=-1)
        zt_ref[...] = z.T
        ho_ref[...] = h_ref[...] + jnp.dot(z, w_ref[...], preferred_element_type=F32)

    vec = _const_spec((1, W))
    return pl.pallas_call(
        body, grid=(T // tm,), name="even_post",
        in_specs=[_row_spec(tm, W)] * 7 + [vec] * 3 + [_row_spec(tm, D), _const_spec((D, D))],
        out_specs=[_row_spec(tm, D), _col_spec(D, tm)],
        out_shape=[jax.ShapeDtypeStruct((T, D), F32), jax.ShapeDtypeStruct((D, T), BF16)],
        compiler_params=_cparams(("parallel",), VMEM_BIG),
    )(y, r, k2, v, ga, o, gb, lng, lnb, rk, h, w_bf)


def even_post_bwd(y, r, k2, v, ga, o, gb, lng, lnb, rk, dh, zt_bf, w_bf, after=None):
    tm = 512
    extra_specs, extra = _after_operand(after)

    def body(y_ref, r_ref, k2_ref, v_ref, ga_ref, o_ref, gb_ref, lng_ref, lnb_ref, rk_ref, dh_ref, zt_ref, w_ref,
             *rest):
        (dy_ref, dr_ref, dk2_ref, dv_ref, dga_ref, do_ref, dgb_ref, dlng_ref, dlnb_ref, drk_ref, dw_ref,
         acc_ref) = rest[-12:]
        dzv = _out_proj_back(dh_ref, zt_ref, w_ref, dw_ref, acc_ref)
        bd = _head_blockdiag()
        _, vjp = jax.vjp(lambda *a: _post_math(*a, bd), y_ref[...], r_ref[...], k2_ref[...], v_ref[...], ga_ref[...],
                         o_ref[...], gb_ref[...], lng_ref[...], lnb_ref[...], rk_ref[...])
        dy, dr, dk2, dv, dga, do, dgb, dlng, dlnb, drk = vjp((dzv[:, 0:W], dzv[:, W:2 * W]))
        for ref, val in ((dy_ref, dy), (dr_ref, dr), (dk2_ref, dk2), (dv_ref, dv), (dga_ref, dga), (do_ref, do),
                         (dgb_ref, dgb)):
            ref[...] = val.astype(ref.dtype)

        @pl.when(pl.program_id(0) == 0)
        def _():
            for ref in (dlng_ref, dlnb_ref, drk_ref):
                ref[...] = jnp.zeros_like(ref)

        dlng_ref[...] += dlng
        dlnb_ref[...] += dlnb
        drk_ref[...] += drk

    vec = _const_spec((1, W))
    return pl.pallas_call(
        body, grid=(T // tm,), name="even_post_bwd",
        in_specs=[_row_spec(tm, W)] * 7 + [vec] * 3 + [_row_spec(tm, D), _col_spec(D, tm), _const_spec((D, D))]
        + extra_specs,
        out_specs=[_row_spec(tm, W)] * 7 + [vec] * 3 + [_const_spec((D, D))],
        out_shape=[jax.ShapeDtypeStruct((T, W), dt) for dt in (F32, F32, F32, F32, BF16, F32, BF16)]
        + [jax.ShapeDtypeStruct((1, W), F32)] * 3 + [jax.ShapeDtypeStruct((D, D), BF16)],
        scratch_shapes=[pltpu.VMEM((D, D), F32)],
        compiler_params=_cparams(("arbitrary",), VMEM_BIG),
    )(y, r, k2, v, ga, o, gb, lng, lnb, rk, dh, zt_bf, w_bf, *extra)


PADSEQ = SEQ + LEFT * L
ATT_SCALE = 1.0 / math.sqrt(HD)
ATT_Q = 4
WIN = BAND + (ATT_Q - 1) * L
ATT_STEPS = NC // ATT_Q
ATT_BIAS_SHAPE = (NPAIR, ATT_Q * 2 * L, WIN)
ATT_WINDOW_BIAS_SHAPE = (ATT_Q, NPAIR, 2 * L, WIN)


def _stack_chunks(a):
    return jnp.concatenate([_stack_pair(a[i * L:(i + 1) * L]) for i in range(ATT_Q)], axis=0)


def _unstack_chunks(a):
    return jnp.concatenate([_unstack_pair(a[i * 2 * L:(i + 1) * 2 * L]) for i in range(ATT_Q)], axis=0)


def _window_bias(b_ref):
    return [jnp.concatenate([b_ref[c, p] for c in range(ATT_Q)], axis=0) for p in range(NPAIR)]


def _key_window(ref, step):
    start = step * (ATT_Q * L) - LEFT * L
    rows = ref[pl.ds(pl.multiple_of(jnp.maximum(start, 0), L), WIN), :]
    window = rows
    for lead in range(ATT_Q * L, LEFT * L + 1, ATT_Q * L):
        moved = jnp.concatenate([rows[WIN - lead:], rows[:WIN - lead]], axis=0)
        window = jnp.where(start == -lead, moved, window)
    return window


def _att_probs(q2, kw, bias, step):
    valid = _iota2((1, WIN), 1) >= (LEFT - step * ATT_Q) * L
    s = [jnp.where(valid, _bdot_nt(a, b) * ATT_SCALE + bias[p], NEG) for p, (a, b) in enumerate(zip(q2, kw))]
    e = [jnp.exp(a - jnp.max(a, axis=-1, keepdims=True)) for a in s]
    return [a / jnp.sum(a, axis=-1, keepdims=True) for a in e]


def attention_fwd(q, k, v, bias):
    def body(q_ref, k_ref, v_ref, b_ref, o_ref):
        step = pl.program_id(1)
        kw = _pairs(_key_window(k_ref, step))
        vw = _pairs(_key_window(v_ref, step))
        q2 = [_stack_chunks(a) for a in _pairs(q_ref[...])]
        p = _att_probs(q2, kw, _window_bias(b_ref), step)
        o_ref[...] = jnp.concatenate([_unstack_chunks(_bdot(a, b)) for a, b in zip(p, vw)], axis=-1)

    qblk = pl.BlockSpec((ATT_Q * L, W), lambda b, c: (b * ATT_STEPS + c, 0))
    kblk = pl.BlockSpec((SEQ, W), lambda b, c: (b, 0))
    return pl.pallas_call(
        body, grid=(NSEQ, ATT_STEPS), name="attention_fwd",
        in_specs=[qblk, kblk, kblk, _const_spec(ATT_WINDOW_BIAS_SHAPE)],
        out_specs=qblk, out_shape=jax.ShapeDtypeStruct((T, W), F32),
        compiler_params=_cparams(("parallel", "arbitrary")),
    )(q, k, v, bias)


def attention_bwd(q, k, v, bias, do):
    def body(q_ref, k_ref, v_ref, b_ref, do_ref, dq_ref, dko_ref, dvo_ref, db_ref, dk_ref, dv_ref):
        b = pl.program_id(0)
        c = pl.program_id(1)

        @pl.when(c == 0)
        def _():
            dk_ref[...] = jnp.zeros_like(dk_ref)
            dv_ref[...] = jnp.zeros_like(dv_ref)

        @pl.when((c == 0) & (b == 0))
        def _():
            db_ref[...] = jnp.zeros_like(db_ref)

        start = pl.multiple_of(c * (ATT_Q * L), L)
        kw = _pairs(_key_window(k_ref, c))
        vw = _pairs(_key_window(v_ref, c))
        q2 = [_stack_chunks(a) for a in _pairs(q_ref[...])]
        do2 = [_stack_chunks(a) for a in _pairs(do_ref[...].astype(BF16))]
        p = _att_probs(q2, kw, _window_bias(b_ref), c)
        dp = [_bdot_nt(a, b) for a, b in zip(do2, vw)]
        ds = [a * (d - jnp.sum(d * a, axis=-1, keepdims=True)) for a, d in zip(p, dp)]
        dss = [(a * ATT_SCALE).astype(BF16) for a in ds]
        dq_ref[...] = jnp.concatenate([_unstack_chunks(_bdot(a, b)) for a, b in zip(dss, kw)], axis=-1).astype(BF16)
        dk_ref[pl.ds(start, WIN), :] += jnp.concatenate([_bdot_tn(a, b) for a, b in zip(dss, q2)], axis=-1)
        dv_ref[pl.ds(start, WIN), :] += jnp.concatenate([_bdot_tn(a, b) for a, b in zip(p, do2)], axis=-1)
        for i in range(NPAIR):
            db_ref[i] += ds[i]

        @pl.when(c == ATT_STEPS - 1)
        def _():
            dko_ref[...] = dk_ref[LEFT * L:, :].astype(BF16)
            dvo_ref[...] = dv_ref[LEFT * L:, :].astype(BF16)

    qblk = pl.BlockSpec((ATT_Q * L, W), lambda b, c: (b * ATT_STEPS + c, 0))
    sblk = pl.BlockSpec((SEQ, W), lambda b, c: (b, 0))
    bblk = _const_spec(ATT_BIAS_SHAPE)
    return pl.pallas_call(
        body, grid=(NSEQ, ATT_STEPS), name="attention_bwd",
        in_specs=[qblk, sblk, sblk, _const_spec(ATT_WINDOW_BIAS_SHAPE), qblk],
        out_specs=[qblk, sblk, sblk, bblk],
        out_shape=[jax.ShapeDtypeStruct((T, W), BF16), jax.ShapeDtypeStruct((T, W), BF16),
                   jax.ShapeDtypeStruct((T, W), BF16), jax.ShapeDtypeStruct(ATT_BIAS_SHAPE, F32)],
        scratch_shapes=[pltpu.VMEM((PADSEQ, W), F32), pltpu.VMEM((PADSEQ, W), F32)],
        compiler_params=_cparams(("arbitrary", "arbitrary"), VMEM_BIG),
    )(q, k, v, bias, do)


NTAB = 2 * CLIP + 1
EXT = BAND + L


def _ext_onehot():
    n = _iota2((EXT, NTAB), 0)
    m = _iota2((EXT, NTAB), 1)
    return (jnp.clip(BAND - 1 - n, -CLIP, CLIP) + CLIP == m).astype(F32)


def bias_expand(table):
    def body(t_ref, o_ref):
        ext = _hdot_nt(t_ref[...], _ext_onehot())
        ext = jnp.concatenate([ext, jnp.zeros((NH, WIN - EXT), F32)], axis=-1)
        col = _iota2((L, WIN), 1)
        for h in range(NH):
            rows = jnp.broadcast_to(ext[h:h + 1], (L, WIN))
            for c in range(ATT_Q):
                inside = (col >= c * L) & (col < c * L + BAND)
                plane = pltpu.roll(rows, (c * L - (L - 1)) % WIN, 1, stride=1, stride_axis=0)
                o_ref[c, h] = jnp.where(inside, plane, NEG)

    out = pl.pallas_call(body, name="bias_expand", out_shape=jax.ShapeDtypeStruct((ATT_Q, NH, L, WIN), F32))(table)
    return out.reshape(ATT_WINDOW_BIAS_SHAPE)


def bias_grad(dbias):
    def body(d_ref, o_ref):
        acc = jnp.zeros((NH, EXT), F32)
        zpad = jnp.zeros((NH, EXT - BAND), F32)
        for i in range(L):
            s = L - 1 - i
            row = jnp.concatenate([d_ref[:, i, :], zpad], axis=-1)
            acc = acc + (pltpu.roll(row, s, 1) if s else row)
        o_ref[...] = _hdot(acc, _ext_onehot())

    return pl.pallas_call(body, name="bias_grad", out_shape=jax.ShapeDtypeStruct((NH, NTAB), F32))(dbias)


def _group_cols(g):
    return slice(g * SGC, (g + 1) * SGC)


def _sg_norm(gv, lng, lnb):
    gc = gv - jnp.mean(gv, axis=-1, keepdims=True)
    rstd = lax.rsqrt(jnp.mean(gc * gc, axis=-1, keepdims=True) + LN_EPS)
    xhat = gc * rstd
    return xhat, rstd, xhat * lng + lnb


GMLP_BWD_CHUNKS = 2


def gmlp_fwd_loss(u, v, gate, lng, lnb, wm_bf, sgb_t, h, w_bf, g_final, target):
    tm = GMLP_BWD_CHUNKS * SGC

    def body(u_ref, v_ref, gt_ref, lng_ref, lnb_ref, wm_ref, sb_ref, h_ref, w_ref, g_ref, t_ref,
             dh_ref, loss_ref, dg_ref, zt_ref):
        zs = []
        for ch in range(GMLP_BWD_CHUNKS):
            rows = slice(ch * SGC, (ch + 1) * SGC)
            _, _, vln = _sg_norm(_gelu(v_ref[rows, :]), lng_ref[...], lnb_ref[...])
            vlb = vln.astype(BF16)
            zg = []
            for g in range(NG):
                cs = _group_cols(g)
                sv = jnp.dot(wm_ref[g], vlb[:, cs], preferred_element_type=F32) + sb_ref[:, g:g + 1]
                zg.append((_gelu(u_ref[rows, cs]) * sv * _silu(gt_ref[rows, cs])).astype(BF16))
            zs.append(jnp.concatenate(zg, axis=-1))
        z = jnp.concatenate(zs, axis=0)
        zt_ref[...] = z.T
        xv = h_ref[...] + jnp.dot(z, w_ref[...], preferred_element_type=F32)
        rstd = lax.rsqrt(jnp.mean(xv * xv, axis=-1, keepdims=True) + RMS_EPS)
        xhat = xv * rstd
        err = xhat * g_ref[...] - t_ref[...]
        part = 0.5 * jnp.sum(jnp.mean(err * err, axis=-1, keepdims=True), axis=0, keepdims=True)
        dout = err * (1.0 / D)

        @pl.when(pl.program_id(0) == 0)
        def _():
            loss_ref[...] = jnp.zeros_like(loss_ref)
            dg_ref[...] = jnp.zeros_like(dg_ref)

        loss_ref[...] += jnp.broadcast_to(part, loss_ref.shape)
        dg_ref[...] += jnp.sum(dout * xhat, axis=0, keepdims=True)
        dxh = dout * g_ref[...]
        dh_ref[...] = rstd * (dxh - xhat * jnp.mean(dxh * xhat, axis=-1, keepdims=True))

    return pl.pallas_call(
        body, grid=(T // tm,), name="gmlp_fwd_loss",
        in_specs=[_row_spec(tm, D)] * 3 + [_const_spec((1, D))] * 2
        + [_const_spec((NG, SGC, SGC)), _const_spec((SGC, NG)), _row_spec(tm, D), _const_spec((D, D)),
           _const_spec((1, D)), _row_spec(tm, D)],
        out_specs=[_row_spec(tm, D), _const_spec((8, 128)), _const_spec((1, D)), _col_spec(D, tm)],
        out_shape=[jax.ShapeDtypeStruct((T, D), F32), jax.ShapeDtypeStruct((8, 128), F32),
                   jax.ShapeDtypeStruct((1, D), F32), jax.ShapeDtypeStruct((D, T), BF16)],
        compiler_params=_cparams(("arbitrary",), VMEM_BIG),
    )(u, v, gate, lng, lnb, wm_bf, sgb_t, h, w_bf, g_final, target)


def gmlp_bwd(u, v, gate, lng, lnb, wm_bf, sgb_t, dh, zt_bf, w_bf):
    def body(u_ref, v_ref, gt_ref, lng_ref, lnb_ref, wm_ref, sb_ref, dh_ref, zt_ref, w_ref,
             du_ref, dv_ref, dgt_ref, dlng_ref, dlnb_ref, dwm_ref, dsb_ref, dw_ref, acc_ref):
        @pl.when(pl.program_id(0) == 0)
        def _():
            for ref in (dlng_ref, dlnb_ref, dwm_ref, dsb_ref):
                ref[...] = jnp.zeros_like(ref)

        dz = _out_proj_back(dh_ref, zt_ref, w_ref, dw_ref, acc_ref)
        sel = (_iota2((D, NG), 0) // SGC == _iota2((D, NG), 1)).astype(F32)
        for ch in range(GMLP_BWD_CHUNKS):
            rows = slice(ch * SGC, (ch + 1) * SGC)
            gv, dgv_dv = _gelu_both(v_ref[rows, :])
            xhat, rstd, vln = _sg_norm(gv, lng_ref[...], lnb_ref[...])
            vlb = vln.astype(BF16)
            dvln = []
            dsv_all = []
            for g in range(NG):
                cs = _group_cols(g)
                uu = u_ref[rows, cs]
                gg = gt_ref[rows, cs]
                dzz = dz[rows, cs]
                sv = jnp.dot(wm_ref[g], vlb[:, cs], preferred_element_type=F32) + sb_ref[:, g:g + 1]
                gu, dgu = _gelu_both(uu)
                sg, dsg = _silu_both(gg)
                dzgu = dzz * gu
                dsv = dzgu * sg
                dgt_ref[rows, cs] = (dzgu * sv * dsg).astype(BF16)
                du_ref[rows, cs] = (dzz * sv * sg * dgu).astype(BF16)
                dsb16 = dsv.astype(BF16)
                dvln.append(lax.dot_general(wm_ref[g], dsb16, (((0,), (0,)), ((), ())), preferred_element_type=F32))
                dwm_ref[g] += lax.dot_general(dsb16, vlb[:, cs], (((1,), (1,)), ((), ())),
                                              preferred_element_type=F32)
                dsv_all.append(dsv)
            dvl = jnp.concatenate(dvln, axis=-1)
            dsb_ref[...] += _hdot(jnp.concatenate(dsv_all, axis=-1), sel)
            dlng_ref[...] += jnp.sum(dvl * xhat, axis=0, keepdims=True)
            dlnb_ref[...] += jnp.sum(dvl, axis=0, keepdims=True)
            dxh = dvl * lng_ref[...]
            dgv = rstd * (dxh - jnp.mean(dxh, axis=-1, keepdims=True)
                          - xhat * jnp.mean(dxh * xhat, axis=-1, keepdims=True))
            dv_ref[rows, :] = (dgv * dgv_dv).astype(BF16)

    tm = GMLP_BWD_CHUNKS * SGC
    return pl.pallas_call(
        body, grid=(T // tm,), name="gmlp_bwd",
        in_specs=[_row_spec(tm, D)] * 3 + [_const_spec((1, D))] * 2
        + [_const_spec((NG, SGC, SGC)), _const_spec((SGC, NG)), _row_spec(tm, D), _col_spec(D, tm),
           _const_spec((D, D))],
        out_specs=[_row_spec(tm, D)] * 3 + [_const_spec((1, D))] * 2
        + [_const_spec((NG, SGC, SGC)), _const_spec((SGC, NG)), _const_spec((D, D))],
        out_shape=[jax.ShapeDtypeStruct((T, D), BF16)] * 3 + [jax.ShapeDtypeStruct((1, D), F32)] * 2
        + [jax.ShapeDtypeStruct((NG, SGC, SGC), F32), jax.ShapeDtypeStruct((SGC, NG), F32),
           jax.ShapeDtypeStruct((D, D), BF16)],
        scratch_shapes=[pltpu.VMEM((D, D), F32)],
        compiler_params=_cparams(("arbitrary",), VMEM_BIG),
    )(u, v, gate, lng, lnb, wm_bf, sgb_t, dh, zt_bf, w_bf)


NCHIP = 4
NDEV = 8
ANY = pl.BlockSpec(memory_space=pl.ANY)


HBM = pl.BlockSpec(memory_space=pltpu.HBM)
SEM = pl.BlockSpec(memory_space=pltpu.SEMAPHORE)
EFFECT = pltpu.SideEffectType.DATAFLOW_SIDE_EFFECTING


CHIPS, EVERY, SIBLING = "chips", "every", "sibling"
SLOTS = {CHIPS: NCHIP, EVERY: NDEV, SIBLING: 1}


def _peers(scope):
    x, y, c = lax.axis_index("x"), lax.axis_index("y"), lax.axis_index("c")
    if scope == SIBLING:
        return [((x, y, 1 - c), 0)], 0
    if scope == CHIPS:
        return [((px, py, c), 2 * px + py) for px, py in ((1 - x, y), (x, 1 - y), (1 - x, 1 - y))], 2 * x + y
    out = []
    for j in range(1, NDEV):
        px, py, pc = x ^ (j >> 2), y ^ ((j >> 1) & 1), c ^ (j & 1)
        out.append(((px, py, pc), 4 * px + 2 * py + pc))
    return out, 4 * x + 2 * y + c


def _send_copies(src, land, send, recv, scatter, scope, starting):
    peers, me = _peers(scope)
    copies = []
    for t in range(len(src)):
        for j, (dev, slot) in enumerate(peers):
            k = t * len(peers) + j
            copies.append(pltpu.make_async_remote_copy(
                src_ref=src[t].at[slot] if scatter else src[t], dst_ref=land[t].at[me if starting else slot],
                send_sem=send.at[k], recv_sem=recv.at[k], device_id=dev, device_id_type=MESH))
    return copies


def _own_copies(src, land, sems, scatter, scope):
    if scope == SIBLING:
        return []
    _, me = _peers(scope)
    return [pltpu.make_async_copy(src[t].at[me] if scatter else src[t], land[t].at[me], sems.at[t])
            for t in range(len(src))]


def send_start(srcs, scatter, scope, name, after=None):
    n = len(srcs)
    slots = SLOTS[scope]
    extra_specs, extra = _after_operand(after)
    lands = [pltpu.HBM(a.shape if scatter else (slots,) + a.shape, a.dtype) for a in srcs]
    sems = [pltpu.SemaphoreType.DMA((n * max(slots - 1, 1),))] * 2 + ([] if scope == SIBLING else
                                                                     [pltpu.SemaphoreType.DMA((n,))])
    k = len(sems)

    def body(*refs):
        first_out = n + len(extra)
        src, land = refs[:n], refs[first_out + k + n:first_out + k + 2 * n]
        for cp in _send_copies(src, land, refs[first_out], refs[first_out + 1], scatter, scope, True):
            cp.start()
        for cp in _own_copies(src, land, refs[first_out + k - 1], scatter, scope):
            cp.start()
        refs[-1][...] = jnp.zeros_like(refs[-1])

    out = pl.pallas_call(
        body, name=name,
        out_shape=(*sems, *[pltpu.HBM(a.shape, a.dtype) for a in srcs], *lands, jax.ShapeDtypeStruct((8, 128), F32)),
        in_specs=[HBM] * n + extra_specs,
        out_specs=(*[SEM] * k, *[HBM] * (2 * n), pl.BlockSpec(memory_space=pltpu.VMEM)),
        input_output_aliases={i: k + i for i in range(n)},
        compiler_params=pltpu.CompilerParams(has_side_effects=EFFECT),
    )(*[pltpu.with_memory_space_constraint(a, pltpu.HBM) for a in srcs], *extra)
    return list(out[:k]), list(out[k:k + n]), list(out[k + n:k + 2 * n]), out[-1]


def send_wait(started, after, scatter, scope, name, with_sources=False, only=None):
    sems, srcs, lands, _ = started
    n, k = len(srcs), len(sems)
    wanted = range(n) if only is None else only

    def body(*refs):
        src, land = refs[:n], refs[n:2 * n]
        for t, cp in enumerate(_own_copies(src, land, refs[2 * n + k - 1], scatter, scope)):
            if t in wanted:
                cp.wait()
        copies = _send_copies(src, land, refs[2 * n], refs[2 * n + 1], scatter, scope, False)
        for i, cp in enumerate(copies):
            if i // (len(copies) // n) in wanted:
                cp.wait_send()
                cp.wait_recv()

    arrs = list(srcs) + list(lands)
    out = pl.pallas_call(
        body, name=name, out_shape=tuple(pltpu.HBM(a.shape, a.dtype) for a in arrs),
        in_specs=[HBM] * (2 * n) + [SEM] * k + [ANY], out_specs=tuple([HBM] * (2 * n)),
        input_output_aliases={i: i for i in range(2 * n)},
        compiler_params=pltpu.CompilerParams(has_side_effects=EFFECT),
    )(*arrs, *sems, after)
    return (list(out[:n]), list(out[n:])) if with_sources else list(out[n:])


def gather_weights(arrs, split):
    n = len(arrs)

    def body(*refs):
        ins, outs = refs[:n], refs[n:2 * n]
        send1, recv1, send2, recv2, loc_in, loc_out = refs[2 * n:2 * n + 6]
        staged = refs[2 * n + 6:]
        x, y, c = lax.axis_index("x"), lax.axis_index("y"), lax.axis_index("c")
        me = 2 * x + y
        sibling = (x, y, 1 - c)
        peers = [(1 - x, y), (x, 1 - y), (1 - x, 1 - y)]

        def rows_of(t, core):
            half = arrs[t].shape[0] // 2
            return pl.ds(core * half, half)

        def part(ref, t, core):
            return ref.at[rows_of(t, core)] if split[t] else ref

        load = [pltpu.make_async_copy(ins[t], staged[t], loc_in.at[t]) for t in range(n)]
        store = [pltpu.make_async_copy(staged[t], outs[t].at[me], loc_out.at[t]) for t in range(n)]
        for cp in load:
            cp.start()
        first = []
        for t in range(n):
            for j, (px, py) in enumerate(peers):
                first.append(pltpu.make_async_remote_copy(
                    src_ref=part(ins[t], t, c), dst_ref=part(outs[t].at[me], t, c), send_sem=send1.at[t, j],
                    recv_sem=recv1.at[t, j], device_id=(px, py, c), device_id_type=MESH))
        for cp in first:
            cp.start()
        for cp_in, cp_out in zip(load, store):
            cp_in.wait()
            cp_out.start()
        passed = []
        for t in range(n):
            for j, (px, py) in enumerate(peers):
                landed = part(outs[t].at[2 * px + py], t, c)
                pltpu.make_async_remote_copy(
                    src_ref=landed, dst_ref=landed, send_sem=send1.at[t, j], recv_sem=recv1.at[t, j],
                    device_id=(x, y, c), device_id_type=MESH).wait_recv()
                if split[t]:
                    cp = pltpu.make_async_remote_copy(
                        src_ref=landed, dst_ref=landed, send_sem=send2.at[t, j], recv_sem=recv2.at[t, j],
                        device_id=sibling, device_id_type=MESH)
                    cp.start()
                    passed.append(cp)
        for t in range(n):
            for j, (px, py) in enumerate(peers):
                if split[t]:
                    other = part(outs[t].at[2 * px + py], t, 1 - c)
                    pltpu.make_async_remote_copy(
                        src_ref=other, dst_ref=other, send_sem=send2.at[t, j], recv_sem=recv2.at[t, j],
                        device_id=(x, y, c), device_id_type=MESH).wait_recv()
        for cp in first + passed:
            cp.wait_send()
        for cp in store:
            cp.wait()

    return pl.pallas_call(
        body, name="gather_weights", in_specs=[ANY] * n, out_specs=[ANY] * n,
        out_shape=[jax.ShapeDtypeStruct((NCHIP,) + a.shape, a.dtype) for a in arrs],
        scratch_shapes=[pltpu.SemaphoreType.DMA((n, 3))] * 4 + [pltpu.SemaphoreType.DMA((n,))] * 2
        + [pltpu.VMEM(a.shape, a.dtype) for a in arrs],
    )(*arrs)


def _adam_math(g, w, m, v):
    m = ADAM_B1 * m + (1.0 - ADAM_B1) * g
    v = ADAM_B2 * v + (1.0 - ADAM_B2) * (g * g)
    m_hat = m / (1.0 - ADAM_B1 ** ADAM_STEP)
    v_hat = v / (1.0 - ADAM_B2 ** ADAM_STEP)
    delta = -ADAM_LR * (m_hat / (jnp.sqrt(v_hat) + ADAM_EPS) + ADAM_WD * w)
    return delta, m, v


def _rows_tile(rows):
    return rows if rows <= 256 else 256


def sum_chips(parts, name):
    _, rows, cols = parts.shape
    tr = _rows_tile(rows)

    def body(p_ref, o_ref):
        acc = p_ref[0].astype(F32)
        for s in range(1, NCHIP):
            acc = acc + p_ref[s].astype(F32)
        o_ref[...] = acc

    return pl.pallas_call(
        body, grid=(rows // tr,), name=name,
        in_specs=[pl.BlockSpec((NCHIP, tr, cols), lambda i: (0, i, 0))],
        out_specs=pl.BlockSpec((tr, cols), lambda i: (i, 0)),
        out_shape=jax.ShapeDtypeStruct((rows, cols), F32),
        compiler_params=_cparams(("parallel",)),
    )(parts)


def sum_chips_small(parts, name):
    n = len(parts)

    def body(*refs):
        for p_ref, o_ref in zip(refs[:n], refs[n:]):
            acc = p_ref[0]
            for s in range(1, NCHIP):
                acc = acc + p_ref[s]
            o_ref[...] = acc

    return pl.pallas_call(body, name=name, out_shape=[jax.ShapeDtypeStruct(p.shape[1:], F32) for p in parts])(*parts)


def adam_shard_small(items, name):
    n = len(items)

    def body(*refs):
        for t in range(n):
            a_ref, b_ref, w_ref, m_ref, v_ref = refs[5 * t:5 * t + 5]
            g_ref, d_ref, mo_ref, vo_ref = refs[5 * n + 4 * t:5 * n + 4 * t + 4]
            g = (a_ref[...] + b_ref[...]).reshape(w_ref.shape)
            g_ref[...] = g
            d_ref[...], mo_ref[...], vo_ref[...] = _adam_math(g, w_ref[...], m_ref[...], v_ref[...])

    out = pl.pallas_call(
        body, name=name, out_shape=[jax.ShapeDtypeStruct(it[2].shape, F32) for it in items for _ in range(4)],
    )(*[a for it in items for a in it])
    return [out[4 * t:4 * t + 4] for t in range(n)]


def adam_shard(p_mine, p_sib, w, m, v, name):
    rows, cols = p_mine.shape
    tr = _rows_tile(rows)
    lead = w.ndim == 3

    def body(a_ref, b_ref, w_ref, m_ref, v_ref, g_ref, d_ref, mo_ref, vo_ref):
        g = a_ref[...] + b_ref[...]
        g = g[None] if lead else g
        g_ref[...] = g
        d_ref[...], mo_ref[...], vo_ref[...] = _adam_math(g, w_ref[...], m_ref[...], v_ref[...])

    flat = pl.BlockSpec((tr, cols), lambda i: (i, 0))
    spec = pl.BlockSpec((1, tr, cols), lambda i: (0, i, 0)) if lead else flat
    return pl.pallas_call(
        body, grid=(rows // tr,), name=name, in_specs=[flat] * 2 + [spec] * 3, out_specs=[spec] * 4,
        out_shape=[jax.ShapeDtypeStruct(w.shape, F32)] * 4,
        compiler_params=_cparams(("parallel",)),
    )(p_mine, p_sib, w, m, v)


def adam_shard_halves_t(r_mine, r_sib, wt, mt, vt, name):
    hrows, cols = r_mine.shape
    tr = _rows_tile(hrows)
    per_half = hrows // tr

    def body(a_ref, b_ref, w_ref, m_ref, v_ref, g_ref, d_ref, mo_ref, vo_ref):
        mine = pl.program_id(0) == lax.axis_index("c")
        g = jnp.where(mine, a_ref[...], b_ref[...]).T[None]
        g_ref[...] = g
        d_ref[...], mo_ref[...], vo_ref[...] = _adam_math(g, w_ref[...], m_ref[...], v_ref[...])

    flat = pl.BlockSpec((tr, cols), lambda h, i: (i, 0))
    spec = pl.BlockSpec((1, cols, tr), lambda h, i: (0, 0, h * per_half + i))
    return pl.pallas_call(
        body, grid=(2, per_half), name=name, in_specs=[flat] * 2 + [spec] * 3, out_specs=[spec] * 4,
        out_shape=[jax.ShapeDtypeStruct(wt.shape, F32)] * 4,
        compiler_params=_cparams(("parallel", "parallel")),
    )(r_mine, r_sib, wt, mt, vt)


def adam_replicated(gathered, params, name):
    flat = []
    for i, p in enumerate(params):
        if isinstance(p, list):
            off = 0
            for wmv in p:
                n = gathered[i].shape[-1] - off if wmv[0] is None else wmv[0].shape[-1]
                flat.append((i, (off, n), wmv))
                off += n
        else:
            flat.append((i, None, p))
    ins = [a for _, _, wmv in flat for a in wmv if a is not None]
    ng = len(gathered)

    def body(*refs):
        g_refs = refs[:ng]
        in_refs = list(refs[ng:ng + len(ins)])
        out_refs = list(refs[ng + len(ins):])
        sums = []
        for r in g_refs:
            g = r[0]
            for d in range(1, NDEV):
                g = g + r[d]
            sums.append(g)
        for i, lanes, wmv in flat:
            g = sums[i] if lanes is None else sums[i][:, lanes[0]:lanes[0] + lanes[1]]
            out_refs.pop(0)[...] = g
            if wmv[0] is not None:
                w_ref, m_ref, v_ref = in_refs.pop(0), in_refs.pop(0), in_refs.pop(0)
                d_ref, mo_ref, vo_ref = out_refs.pop(0), out_refs.pop(0), out_refs.pop(0)
                d_ref[...], mo_ref[...], vo_ref[...] = _adam_math(g, w_ref[...], m_ref[...], v_ref[...])

    out_shape = []
    for i, lanes, wmv in flat:
        shape = gathered[i].shape[1:] if lanes is None else (1, lanes[1])
        out_shape += [jax.ShapeDtypeStruct(shape, F32)] * (4 if wmv[0] is not None else 1)
    outs = list(pl.pallas_call(body, name=name, out_shape=out_shape)(*gathered, *ins))
    return [[outs.pop(0) for _ in range(4 if wmv[0] is not None else 1)] for _, _, wmv in flat]


EVEN_SPLITS = (SHIFT, W, W, W, W, W)
ODD_SPLITS = (D, D, D)


def _cols_to_chips(a):
    rows, cols = a.shape
    return a.reshape(rows, NCHIP, cols // NCHIP).transpose(1, 0, 2)


def _chips_to_cols(a):
    _, rows, n = a.shape
    return a.transpose(1, 0, 2).reshape(rows, NCHIP * n)


def kernel(x, norm_g, w_in_e, shift_mu, rw_w0, rw_w2, rw_a0, rw_a2, rw_kk, rw_ka, rw_rk, rw_lnx_g, rw_lnx_b, att_bias, w_out_e, w_in_o, sg_ln_g, sg_ln_b, sg_w, sg_b, w_out_o, final_g, loss_target, m_norm_g, m_w_in_e, m_shift_mu, m_rw_w0, m_rw_w2, m_rw_a0, m_rw_a2, m_rw_kk, m_rw_ka, m_rw_rk, m_rw_lnx_g, m_rw_lnx_b, m_att_bias, m_w_out_e, m_w_in_o, m_sg_ln_g, m_sg_ln_b, m_sg_w, m_sg_b, m_w_out_o, m_final_g, v_norm_g, v_w_in_e, v_shift_mu, v_rw_w0, v_rw_w2, v_rw_a0, v_rw_a2, v_rw_kk, v_rw_ka, v_rw_rk, v_rw_lnx_g, v_rw_lnx_b, v_att_bias, v_w_out_e, v_w_in_o, v_sg_ln_g, v_sg_ln_b, v_sg_w, v_sg_b, v_w_out_o, v_final_g):
    x2 = x.reshape(T, D)
    tgt = loss_target.reshape(T, D)

    gathered = gather_weights(
        [jnp.swapaxes(w_in_e[0], 0, 1).astype(BF16), jnp.concatenate([rw_w2[0], rw_a2[0]], axis=0),
         jnp.concatenate([sg_ln_g, sg_ln_b], axis=0)], [True, True, False])
    wie = gathered[0].reshape(EVEN_IN, D)
    w2 = _chips_to_cols(gathered[1][:, :LORA])
    a2 = _chips_to_cols(gathered[1][:, LORA:])
    sglg = _chips_to_cols(gathered[2][:, 0:1])
    sglb = _chips_to_cols(gathered[2][:, 1:2])

    late = [w_out_e[0].astype(BF16), w_in_o[0].astype(BF16), w_out_o[0].astype(BF16)]
    late_started = send_start(late, False, CHIPS, "late_weights_start", after=gathered[0])

    late_state = {}

    def late_weights(layer, after):
        if layer == "even":
            srcs, lands = send_wait(late_started, after, False, CHIPS, "late_w_out_e_wait", True, only=(0,))
            late_state["rest"] = (late_started[0], srcs, lands, None)
            return lands[0].reshape(D, D)
        woe, wio, woo = send_wait(late_state["rest"], after, False, CHIPS, "late_weights_wait", only=(1, 2))
        return woe.reshape(D, D), wio, woo.reshape(D, D)

    def scatter_start(grads, name):
        return send_start([g_.astype(BF16) if g_.shape[-1] >= W else g_ for g_ in grads], True, CHIPS, name)

    started = {}

    def on_odd_grads(d_woo, d_wio):
        started["odd"] = scatter_start([d_woo.reshape(NCHIP, D // NCHIP, D), d_wio], "odd_grads_start")
        return started["odd"][-1]

    def on_even_grads(big_g):
        d_wie_half, d_woe, _, _, d_w2, d_a2, d_sglg, d_sglb = big_g
        blocks = [d_wie_half, d_woe.reshape(NCHIP, D // NCHIP, D), _cols_to_chips(d_w2), _cols_to_chips(d_a2),
                  _cols_to_chips(d_sglg), _cols_to_chips(d_sglb)]
        started["even"] = scatter_start(blocks, "even_grads_start")
        return started["even"][-1]

    def on_small_grads(layer, grads):
        if layer == "odd":
            d_sg_w, d_sg_b, d_final, d_g1 = grads
            mine = [d_sg_w.reshape(NG * SGC, SGC), d_sg_b, jnp.concatenate([d_final, d_g1], axis=1)]
        else:
            mine = [grads[-2], jnp.concatenate(grads[:-2] + grads[-1:], axis=1)]
        started[layer + "_small"] = send_start(mine, False, EVERY, layer + "_small_grads_start")
        return started[layer + "_small"][-1]

    loss_part, dx, _, _ = _local_step(
        x2, tgt, wie, late_weights, w2, a2, sglg, sglb, norm_g, shift_mu, rw_w0, rw_a0, rw_kk, rw_ka, rw_rk,
        rw_lnx_g, rw_lnx_b, att_bias, sg_w, sg_b, final_g, first_after=late_started[-1], on_odd_grads=on_odd_grads,
        on_even_grads=on_even_grads, on_small_grads=on_small_grads)
    wmv = {"w_in_e": tuple(jnp.swapaxes(a, 1, 2) for a in (w_in_e, m_w_in_e, v_w_in_e)),
           "w_out_e": (w_out_e, m_w_out_e, v_w_out_e),
           "w_in_o": (w_in_o, m_w_in_o, v_w_in_o), "w_out_o": (w_out_o, m_w_out_o, v_w_out_o),
           "rw_w2": (rw_w2, m_rw_w2, v_rw_w2), "rw_a2": (rw_a2, m_rw_a2, v_rw_a2),
           "sg_ln_g": (sg_ln_g, m_sg_ln_g, v_sg_ln_g), "sg_ln_b": (sg_ln_b, m_sg_ln_b, v_sg_ln_b)}
    sharded = {}

    def sum_and_swap(names, landed, tag):
        nbig = sum(p_.dtype == BF16 for p_ in landed)
        partial = [sum_chips(p_, "sum_" + nm) for p_, nm in zip(landed[:nbig], names)]
        if nbig < len(names):
            partial += sum_chips_small(landed[nbig:], "sum_small_" + tag)
        return send_start(partial, False, SIBLING, "swap_partials_" + tag + "_start")

    def update(names, swap_started, after, tag):
        partial, landed = send_wait(swap_started, after, False, SIBLING, "swap_partials_" + tag + "_wait", True)
        from_sibling = [a[0] for a in landed]
        nbig = sum(p_.shape[-1] >= W for p_ in partial)
        for nm, mine, sib in zip(names[:nbig], partial, from_sibling):
            if nm == "w_in_e":
                res = adam_shard_halves_t(mine, sib, *wmv[nm], "adam_" + nm)
                sharded[nm] = [jnp.swapaxes(a, 1, 2) for a in res]
            else:
                sharded[nm] = adam_shard(mine, sib, *wmv[nm], "adam_" + nm)
        if nbig < len(names):
            items = [(mine, sib, *wmv[nm]) for nm, mine, sib in zip(names, partial, from_sibling)][nbig:]
            for nm, res in zip(names[nbig:], adam_shard_small(items, "adam_small_" + tag)):
                sharded[nm] = res

    odd_names = ["w_out_o", "w_in_o"]
    even_names = ["w_in_e", "w_out_e", "rw_w2", "rw_a2", "sg_ln_g", "sg_ln_b"]
    odd_landed = send_wait(started["odd"], started["even_small"][-1], True, CHIPS, "odd_grads_wait")
    odd_swap = sum_and_swap(odd_names, odd_landed, "odd")
    done = odd_swap[-1]

    def wmv_of(*arrs, view=lambda a: a):
        return tuple(view(a) for a in arrs)

    vec = lambda a: a.reshape(1, -1)
    groups = {
        "odd": (["sg_w", "sg_b", "final_g", "norm_g1"],
                [wmv_of(sg_w, m_sg_w, v_sg_w, view=lambda a: a.reshape(NG * SGC, SGC)),
                 wmv_of(sg_b, m_sg_b, v_sg_b, view=lambda a: a[0]),
                 [wmv_of(final_g, m_final_g, v_final_g, view=vec),
                  wmv_of(norm_g, m_norm_g, v_norm_g, view=lambda a: a[1:2])]]),
        "even": (["att_bias", "norm_g0", "shift_mu", "rw_w0", "rw_a0", "rw_kk", "rw_ka", "rw_rk", "rw_lnx_g",
                  "rw_lnx_b", "loss"],
                 [wmv_of(att_bias, m_att_bias, v_att_bias, view=lambda a: a[0]),
                  [wmv_of(norm_g, m_norm_g, v_norm_g, view=lambda a: a[0:1]),
                   wmv_of(shift_mu, m_shift_mu, v_shift_mu), wmv_of(rw_w0, m_rw_w0, v_rw_w0),
                   wmv_of(rw_a0, m_rw_a0, v_rw_a0), wmv_of(rw_kk, m_rw_kk, v_rw_kk), wmv_of(rw_ka, m_rw_ka, v_rw_ka),
                   wmv_of(rw_rk, m_rw_rk, v_rw_rk, view=vec), wmv_of(rw_lnx_g, m_rw_lnx_g, v_rw_lnx_g),
                   wmv_of(rw_lnx_b, m_rw_lnx_b, v_rw_lnx_b), (None, None, None)]]),
    }
    rep = {}
    for layer in ("odd", "even"):
        nms, params = groups[layer]
        gathered_g = send_wait(started[layer + "_small"], done, False, EVERY, layer + "_small_grads_wait")
        for nm, res in zip(nms, adam_replicated(gathered_g, params, "adam_" + layer + "_small")):
            rep[nm] = res
        done = rep[nms[0]][0]
    native = {"sg_w": sg_w.shape, "sg_b": sg_b.shape, "final_g": final_g.shape, "rw_rk": rw_rk.shape,
              "att_bias": att_bias.shape}
    for nm, shape in native.items():
        rep[nm] = [a.reshape(shape) for a in rep[nm]]
    rep["norm_g"] = [jnp.concatenate([a, b], axis=0) for a, b in zip(rep["norm_g0"], rep["norm_g1"])]
    even_landed = send_wait(started["even"], done, True, CHIPS, "even_grads_wait")
    even_swap = sum_and_swap(even_names, even_landed, "even")
    update(odd_names, odd_swap, even_swap[-1], "odd")
    update(even_names, even_swap, sharded["w_in_o"][0], "even")

    order = ["norm_g", "w_in_e", "shift_mu", "rw_w0", "rw_w2", "rw_a0", "rw_a2", "rw_kk", "rw_ka", "rw_rk",
             "rw_lnx_g", "rw_lnx_b", "att_bias", "w_out_e", "w_in_o", "sg_ln_g", "sg_ln_b", "sg_w", "sg_b",
             "w_out_o", "final_g"]
    results = {**sharded, **rep}
    outs = [rep["loss"][0][0, 0], dx.reshape(NSEQ, SEQ, D)]
    for kind in range(4):
        outs += [results[nm][kind] for nm in order]
    return tuple(outs)


def _local_step(x2, tgt, wie_t, late_weights, w2, a2, sglg, sglb, norm_g, shift_mu, rw_w0, rw_a0, rw_kk, rw_ka, rw_rk,
                rw_lnx_g, rw_lnx_b, att_bias, sg_w, sg_b, final_g, first_after=None, on_odd_grads=None,
                on_even_grads=None, on_small_grads=None):
    zl = jnp.zeros((LORA, W), F32)
    w2x = jnp.concatenate([w2, zl], axis=0)
    a2x = jnp.concatenate([zl, a2], axis=0)
    rk = rw_rk.reshape(1, W)
    pos = np.arange(SGC)
    sg_mask = jnp.asarray(((pos[None, :] // L) <= (pos[:, None] // L)).astype(np.float32))
    wm = (sg_w[0] * sg_mask[None]).astype(BF16)
    sgb_t = sg_b[0].T

    xn0, ps, ga, q, kb, vb, gb = ln_in_proj(x2, norm_g[0:1], wie_t, EVEN_SPLITS, "in_proj_even", after=first_after,
                                            w_t=True, bf16_pieces=(2, 3, 4))
    r, lw, k2, v, aa, bb = even_prep(ps, shift_mu, rw_w0, w2x, rw_a0, a2x, rw_kk, rw_ka)
    y, rw_saved = rwkv_fwd(r, lw, k2, v, aa, bb)
    bias = bias_expand(att_bias[0])

    o = attention_fwd(q, kb, vb, bias)
    woe = late_weights("even", o)
    h1, zt = even_post(y, r, k2, v, ga, o, gb, rw_lnx_g, rw_lnx_b, rk, x2, woe)
    woe, wio, woo = late_weights("odd", h1)
    xn1, u, vv, gt = ln_in_proj(h1, norm_g[1:2], wio, ODD_SPLITS, "in_proj_odd")
    dh2, loss_part, d_final_g, z2t = gmlp_fwd_loss(u, vv, gt, sglg, sglb, wm, sgb_t, h1, woo, final_g[None], tgt)

    du, dvv, dgt, d_sglg, d_sglb, d_wm, d_sgb_t, d_woo = gmlp_bwd(u, vv, gt, sglg, sglb, wm, sgb_t, dh2, z2t, woo)
    dp_odd = [du, dvv, dgt]
    d_wio = matmul_acc_chips(xn1, dp_odd, "in_proj_odd_dw")
    token = on_odd_grads(d_woo, d_wio) if on_odd_grads else None
    dh1, d_g1 = in_proj_bwd_x(h1, norm_g[1:2], wio, dp_odd, dh2, "in_proj_odd_bwd", after=token)
    odd_small = [d_wm * sg_mask[None], d_sgb_t.T, d_final_g, d_g1]
    token = on_small_grads("odd", odd_small) if on_small_grads else None
    dy, dr2, dk22, dv2, dga, do, dgb, d_lng, d_lnb, d_rk, d_woe = even_post_bwd(
        y, r, k2, v, ga, o, gb, rw_lnx_g, rw_lnx_b, rk, dh1, zt, woe, after=token)
    dq, dkb, dvb, dbias = attention_bwd(q, kb, vb, bias, do)
    dbias = sum(dbias[:, i * 2 * L:(i + 1) * 2 * L, i * L:i * L + BAND] for i in range(ATT_Q))
    d_att_bias = bias_grad(dbias.reshape(NH, L, BAND))
    dr, dlw, dk2, dv, daa, dbb = rwkv_bwd(r, lw, k2, aa, bb, rw_saved, dy)
    dps, d_mu, d_w0, d_w2x, d_a0, d_a2x, d_kk, d_ka = even_prep_bwd(
        ps, shift_mu, rw_w0, w2x, rw_a0, a2x, rw_kk, rw_ka, dr, dlw, dk2, dv, daa, dbb, dr2, dk22, dv2)
    dp_even = [dps, dga, dq, dkb, dvb, dgb]
    d_wie = matmul_acc_chips(xn0, dp_even, "in_proj_even_dw", add_cores=on_even_grads is not None)
    big_g = (d_wie, d_woe, d_wio, d_woo, d_w2x[:LORA], d_a2x[LORA:], d_sglg, d_sglb)
    token = on_even_grads(big_g) if on_even_grads else None
    dx, d_g0 = in_proj_bwd_x(x2, norm_g[0:1], wie_t, dp_even, dh1, "in_proj_even_bwd", after=token, w_t=True)
    even_small = [d_g0, d_mu, d_w0, d_a0, d_kk, d_ka, d_rk, d_lng, d_lnb, d_att_bias]
    if on_small_grads:
        on_small_grads("even", even_small + [loss_part[0:1, :]])
    rep_g = [jnp.concatenate([d_g0, d_g1], axis=0)] + even_small[1:] + odd_small[:3]
    return loss_part[0, 0], dx, big_g, rep_g
```

```python
import functools
import math

import jax
import jax.numpy as jnp
import numpy as np
from jax import lax
from jax.experimental import pallas as pl
from jax.experimental.pallas import tpu as pltpu

F32 = jnp.float32
BF16 = jnp.bfloat16
HI = lax.Precision.HIGHEST

D = 1024
SEQ = 2048
NSEQ = 2
T = NSEQ * SEQ
HD = 64
NH = 8
W = 512
SHIFT = 1664
LORA = 64
EVEN_IN = 4224
ODD_IN = 3072
L = 64
NC = SEQ // L
LEFT = 8
BAND = (LEFT + 1) * L
CLIP = 128
SGC = 128
NG = 8
RMS_EPS = 1e-6
LN_EPS = 1e-5
GN_EPS = 64e-5
NEG = -1e30
VMEM_BIG = 56 * 1024 * 1024

ADAM_LR = 0.001
ADAM_B1 = 0.9
ADAM_B2 = 0.999
ADAM_EPS = 1e-08
ADAM_WD = 0.01
ADAM_STEP = 10

MESH = pl.DeviceIdType.MESH


def _bdot(a, b):
    return jnp.dot(a.astype(BF16), b.astype(BF16), preferred_element_type=F32)


def _bdot_nt(a, b):
    return lax.dot_general(a.astype(BF16), b.astype(BF16), (((1,), (1,)), ((), ())), preferred_element_type=F32)


def _bdot_tn(a, b):
    return lax.dot_general(a.astype(BF16), b.astype(BF16), (((0,), (0,)), ((), ())), preferred_element_type=F32)


def _hdot(a, b):
    return jnp.dot(a, b, precision=HI, preferred_element_type=F32)


def _hdot_nt(a, b):
    return lax.dot_general(a, b, (((1,), (1,)), ((), ())), precision=HI, preferred_element_type=F32)


def _hdot_tn(a, b):
    return lax.dot_general(a, b, (((0,), (0,)), ((), ())), precision=HI, preferred_element_type=F32)


def _iota2(shape, dim):
    return lax.broadcasted_iota(jnp.int32, shape, dim)


def _head_blockdiag():
    r = _iota2((2 * HD, 2 * HD), 0) // HD
    c = _iota2((2 * HD, 2 * HD), 1) // HD
    return (r == c).astype(BF16)


def _headsum_impl(x, bd):
    hi = x.astype(BF16)
    mid = (x - hi.astype(F32)).astype(BF16)
    n = bd.shape[0]
    out = [jnp.dot(hi[:, i:i + n], bd, preferred_element_type=F32) + jnp.dot(mid[:, i:i + n], bd, preferred_element_type=F32)
           for i in range(0, x.shape[1], n)]
    return jnp.concatenate(out, axis=-1)


@jax.custom_vjp
def _headsum(x, bd):
    return _headsum_impl(x, bd)


def _headsum_fwd(x, bd):
    return _headsum_impl(x, bd), bd


def _headsum_bwd(bd, ct):
    return _headsum_impl(ct, bd), None


_headsum.defvjp(_headsum_fwd, _headsum_bwd)


def _silu(x):
    return x * jax.nn.sigmoid(x)


_GELU_C = math.sqrt(2.0 / math.pi)


def _gelu(x):
    return 0.5 * x * (1.0 + jnp.tanh(_GELU_C * (x + 0.044715 * (x * x * x))))


def _silu_both(x):
    s = jax.nn.sigmoid(x)
    xs = x * s
    return xs, s + xs * (1.0 - s)


def _gelu_both(x):
    x2 = x * x
    t = jnp.tanh(_GELU_C * (x + 0.044715 * (x2 * x)))
    half = 0.5 * (1.0 + t)
    return x * half, half + 0.5 * x * (1.0 - t * t) * _GELU_C * (1.0 + 3.0 * 0.044715 * x2)


def _softplus(x):
    return jnp.maximum(x, 0.0) + jnp.log(1.0 + jnp.exp(-jnp.abs(x)))


def _cparams(sem, vmem=None):
    return pltpu.CompilerParams(dimension_semantics=sem, vmem_limit_bytes=vmem)


def _row_spec(tm, width):
    return pl.BlockSpec((tm, width), lambda i: (i, 0))


def _col_spec(height, tm):
    return pl.BlockSpec((height, tm), lambda i: (0, i))


def _const_spec(shape):
    nd = len(shape)
    return pl.BlockSpec(shape, lambda *_: (0,) * nd)


def _weight_dims(w_bf, w_t):
    if w_bf.ndim == 3:
        return None, w_bf.shape[0] * w_bf.shape[2]
    return (((1,), (1,)), ((), ())) if w_t else (((1,), (0,)), ((), ())), w_bf.shape[0 if w_t else 1]


def _proj(xn, w_ref, dims):
    if dims is None:
        return jnp.concatenate([jnp.dot(xn, w_ref[s], preferred_element_type=F32) for s in range(w_ref.shape[0])],
                               axis=-1)
    return lax.dot_general(xn, w_ref[...], dims, preferred_element_type=F32)


def _proj_back(dp, w_ref, w_t):
    nt = (((1,), (1,)), ((), ()))
    if len(w_ref.shape) == 3:
        nb = w_ref.shape[2]
        parts = [lax.dot_general(dp[:, s * nb:(s + 1) * nb], w_ref[s], nt, preferred_element_type=F32)
                 for s in range(w_ref.shape[0])]
        return sum(parts[1:], parts[0])
    return lax.dot_general(dp, w_ref[...], (((1,), (0,)), ((), ())) if w_t else nt, preferred_element_type=F32)


def ln_in_proj(x, g, w_bf, splits, name, after=None, w_t=False, bf16_pieces=()):
    dims, n = _weight_dims(w_bf, w_t)
    dtypes = [BF16 if i in bf16_pieces else F32 for i in range(len(splits))]
    tm = 512 if n <= ODD_IN else 256
    spans = []
    o = 0
    for s in splits:
        spans.append((o, o + s))
        o += s
    assert o == n
    extra_specs, extra = _after_operand(after)

    def body(x_ref, g_ref, w_ref, *rest):
        xn_ref, outs = rest[len(extra)], rest[len(extra) + 1:]
        xv = x_ref[...]
        rstd = lax.rsqrt(jnp.mean(xv * xv, axis=-1, keepdims=True) + RMS_EPS)
        xn = (xv * rstd * g_ref[...]).astype(BF16)
        xn_ref[...] = xn.T
        p = _proj(xn, w_ref, dims)
        for o_ref, (a, b) in zip(outs, spans):
            o_ref[...] = p[:, a:b].astype(o_ref.dtype)

    return pl.pallas_call(
        body, grid=(T // tm,), name=name,
        in_specs=[_row_spec(tm, D), _const_spec((1, D)), _const_spec(w_bf.shape)] + extra_specs,
        out_specs=[_col_spec(D, tm)] + [_row_spec(tm, s) for s in splits],
        out_shape=[jax.ShapeDtypeStruct((D, T), BF16)]
        + [jax.ShapeDtypeStruct((T, s), dt) for s, dt in zip(splits, dtypes)],
        compiler_params=_cparams(("parallel",), VMEM_BIG),
    )(x, g, w_bf, *extra)


def in_proj_bwd_x(x, g, w_bf, dps, dres, name, after=None, w_t=False):
    tm = 512
    widths = [d.shape[1] for d in dps]
    extra_specs, extra = _after_operand(after)

    def body(x_ref, g_ref, w_ref, dres_ref, *rest):
        dp_refs = rest[:len(widths)]
        dx_ref, dg_ref = rest[-2:]
        dp = jnp.concatenate([r[...] for r in dp_refs], axis=-1)
        dxn = _proj_back(dp, w_ref, w_t)
        xv = x_ref[...]
        rstd = lax.rsqrt(jnp.mean(xv * xv, axis=-1, keepdims=True) + RMS_EPS)
        xhat = xv * rstd
        dgp = jnp.sum(dxn * xhat, axis=0, keepdims=True)

        @pl.when(pl.program_id(0) == 0)
        def _():
            dg_ref[...] = jnp.zeros_like(dg_ref)

        dg_ref[...] += dgp
        dxh = dxn * g_ref[...]
        dx_ref[...] = dres_ref[...] + rstd * (dxh - xhat * jnp.mean(dxh * xhat, axis=-1, keepdims=True))

    return pl.pallas_call(
        body, grid=(T // tm,), name=name,
        in_specs=[_row_spec(tm, D), _const_spec((1, D)), _const_spec(w_bf.shape), _row_spec(tm, D)]
        + [_row_spec(tm, s) for s in widths] + extra_specs,
        out_specs=[_row_spec(tm, D), _const_spec((1, D))],
        out_shape=[jax.ShapeDtypeStruct((T, D), F32), jax.ShapeDtypeStruct((1, D), F32)],
        compiler_params=_cparams(("arbitrary",), VMEM_BIG),
    )(x, g, w_bf, dres, *dps, *extra)


def _after_operand(after):
    return ([ANY], [after]) if after is not None else ([], [])


def matmul_acc_chips(at_bf, pieces, name, after=None, add_cores=False):
    k = at_bf.shape[0]
    widths = [p.shape[1] for p in pieces]
    nb = sum(widths) // NCHIP
    tm = 512
    steps = T // tm
    half = k // 2
    extra_specs, extra = _after_operand(after)

    def body(a_ref, *rest):
        o_ref, acc = rest[len(widths) + len(extra):][:2]

        @pl.when(pl.program_id(0) == 0)
        def _():
            acc[...] = jnp.zeros_like(acc)

        a = a_ref[...]
        b = jnp.concatenate([r[...] for r in rest[:len(widths)]], axis=-1)
        for s in range(NCHIP):
            acc[s] += jnp.dot(a, b[:, s * nb:(s + 1) * nb], preferred_element_type=F32)

        @pl.when(pl.program_id(0) == steps - 1)
        def _():
            if not add_cores:
                o_ref[...] = acc[...].astype(BF16)
            else:
                give, got, send, recv = rest[-4:]
                x, y, c = lax.axis_index("x"), lax.axis_index("y"), lax.axis_index("c")
                theirs = pl.multiple_of((1 - c) * half, half)
                mine = pl.multiple_of(c * half, half)
                copies = []
                for s in range(NCHIP):
                    give[s] = acc[s, pl.ds(theirs, half), :].astype(BF16)
                    copies.append(pltpu.make_async_remote_copy(
                        src_ref=give.at[s], dst_ref=got.at[s], send_sem=send.at[s], recv_sem=recv.at[s],
                        device_id=(x, y, 1 - c), device_id_type=MESH))
                    copies[s].start()
                for s in range(NCHIP):
                    copies[s].wait()
                    o_ref[s] = (acc[s, pl.ds(mine, half), :] + got[s].astype(F32)).astype(BF16)

    out_rows = half if add_cores else k
    exchange = ([pltpu.VMEM((NCHIP, half, nb), BF16)] * 2 + [pltpu.SemaphoreType.DMA((NCHIP,))] * 2 if add_cores
                else [])
    return pl.pallas_call(
        body, grid=(steps,), name=name,
        in_specs=[_col_spec(k, tm)] + [_row_spec(tm, w_) for w_ in widths] + extra_specs,
        out_specs=_const_spec((NCHIP, out_rows, nb)),
        out_shape=jax.ShapeDtypeStruct((NCHIP, out_rows, nb), BF16),
        scratch_shapes=[pltpu.VMEM((NCHIP, k, nb), F32)] + exchange,
        compiler_params=_cparams(("arbitrary",), VMEM_BIG),
    )(at_bf, *pieces, *extra)


def _out_proj_back(dh_ref, zt_ref, w_ref, dw_ref, acc_ref):
    dhb = dh_ref[...].astype(BF16)

    @pl.when(pl.program_id(0) == 0)
    def _():
        acc_ref[...] = jnp.zeros_like(acc_ref)

    acc_ref[...] += jnp.dot(zt_ref[...], dhb, preferred_element_type=F32)

    @pl.when(pl.program_id(0) == pl.num_programs(0) - 1)
    def _():
        dw_ref[...] = acc_ref[...].astype(dw_ref.dtype)

    return lax.dot_general(dhb, w_ref[...], (((1,), (1,)), ((), ())), preferred_element_type=F32)


PREP_TM = 512
PREP_NB = SEQ // PREP_TM


def _prep_elem(k, wl, apre, kkw, kaw, bd):
    wraw = -_softplus(-wl) - 0.5
    lw = -jnp.exp(wraw)
    asig = jax.nn.sigmoid(apre)
    kkr = k * kkw
    nrm = jnp.maximum(jnp.sqrt(_headsum(kkr * kkr, bd)), 1e-12)
    kk = kkr / nrm
    k2 = k * (1.0 + (asig - 1.0) * kaw)
    return lw, k2, -kk, kk * asig


def _prep_elem_bwd(k, wl, apre, kkw, kaw, bd, dlw, dk2, daa, dbb):
    s = -wl
    sp = _softplus(s)
    dwl = dlw * (-jnp.exp(-sp - 0.5)) * jnp.exp(s - sp)
    asig = jax.nn.sigmoid(apre)
    kkr = k * kkw
    root = jnp.sqrt(_headsum(kkr * kkr, bd))
    inv = 1.0 / jnp.maximum(root, 1e-12)
    kk = kkr * inv
    dkk = dbb * asig - daa
    dap = (dbb * kk + dk2 * k * kaw) * asig * (1.0 - asig)
    through_norm = jnp.where(root > 1e-12, kk * _headsum(dkk * kkr, bd) * inv, 0.0)
    dkkr = inv * (dkk - through_norm)
    gain = 1.0 + (asig - 1.0) * kaw
    dk = dkkr * kkw + dk2 * gain
    dkkw = jnp.sum(dkkr * k, axis=0, keepdims=True)
    dkaw = jnp.sum(dk2 * k * (asig - 1.0), axis=0, keepdims=True)
    return dk, dwl, dap, dkkw, dkaw


def _shifted(ps_ref, prev_ref, mu, blk):
    p = ps_ref[...]
    first = (blk % PREP_NB) == 0
    prev_row = jnp.where(first, 0.0, prev_ref[7:8, :])
    rolled = pltpu.roll(p, 1, 0)
    p_prev = jnp.where(_iota2(p.shape, 0) == 0, prev_row, rolled)
    return p, p_prev, p + (p_prev - p) * mu


def _prev_spec(width, blk_of):
    return pl.BlockSpec((8, width), lambda i: (jnp.maximum(blk_of(i) * (PREP_TM // 8) - 1, 0), 0))


def even_prep(ps, mu, w0, w2x, a0, a2x, kkw, kaw):
    tm = PREP_TM

    def body(ps_ref, prev_ref, mu_ref, w0_ref, w2_ref, a0_ref, a2_ref, kk_ref, ka_ref,
             r_ref, lw_ref, k2_ref, v_ref, aa_ref, bb_ref):
        _, _, s = _shifted(ps_ref, prev_ref, mu_ref[...], pl.program_id(0))
        wa = s[:, 3 * W:]
        wl = w0_ref[...] + _bdot(jnp.tanh(wa), w2_ref[...])
        apre = a0_ref[...] + _bdot(wa, a2_ref[...])
        lw, k2, aa, bb = _prep_elem(s[:, W:2 * W], wl, apre, kk_ref[...], ka_ref[...], _head_blockdiag())
        r_ref[...] = s[:, 0:W]
        v_ref[...] = s[:, 2 * W:3 * W]
        lw_ref[...] = lw
        k2_ref[...] = k2
        aa_ref[...] = aa
        bb_ref[...] = bb

    vec = _const_spec((1, W))
    return pl.pallas_call(
        body, grid=(T // tm,), name="even_prep",
        in_specs=[_row_spec(tm, SHIFT), _prev_spec(SHIFT, lambda i: i), _const_spec((1, SHIFT)), vec,
                  _const_spec((2 * LORA, W)), vec, _const_spec((2 * LORA, W)), vec, vec],
        out_specs=[_row_spec(tm, W)] * 6,
        out_shape=[jax.ShapeDtypeStruct((T, W), F32)] * 6,
        compiler_params=_cparams(("parallel",), VMEM_BIG),
    )(ps, ps, mu, w0, w2x, a0, a2x, kkw, kaw)


def even_prep_bwd(ps, mu, w0, w2x, a0, a2x, kkw, kaw, dr, dlw, dk2, dv, daa, dbb, dr2, dk22, dv2):
    tm = PREP_TM
    nb = T // tm
    rev = lambda i: nb - 1 - i

    def body(ps_ref, prev_ref, mu_ref, w0_ref, w2_ref, a0_ref, a2_ref, kk_ref, ka_ref,
             dr_ref, dlw_ref, dk2_ref, dv_ref, daa_ref, dbb_ref, dr2_ref, dk22_ref, dv2_ref,
             dps_ref, dmu_ref, dw0_ref, dw2_ref, da0_ref, da2_ref, dkk_ref, dka_ref, carry):
        i = pl.program_id(0)
        blk = rev(i)
        mu_v = mu_ref[...]
        p, p_prev, s = _shifted(ps_ref, prev_ref, mu_v, blk)
        wa = s[:, 3 * W:]
        th = jnp.tanh(wa)
        wl = w0_ref[...] + _bdot(th, w2_ref[...])
        apre = a0_ref[...] + _bdot(wa, a2_ref[...])
        bd = _head_blockdiag()
        k = s[:, W:2 * W]
        dk, dwl, dap, dkkw, dkaw = _prep_elem_bwd(k, wl, apre, kk_ref[...], ka_ref[...], bd, dlw_ref[...],
                                                  dk2_ref[...] + dk22_ref[...], daa_ref[...], dbb_ref[...])
        dwa = _bdot_nt(dwl, w2_ref[...]) * (1.0 - th * th) + _bdot_nt(dap, a2_ref[...])
        ds = jnp.concatenate([dr_ref[...] + dr2_ref[...], dk, dv_ref[...] + dv2_ref[...], dwa], axis=-1)

        @pl.when(i == 0)
        def _():
            for ref in (dmu_ref, dw0_ref, dw2_ref, da0_ref, da2_ref, dkk_ref, dka_ref, carry):
                ref[...] = jnp.zeros_like(ref)

        dmu_ref[...] += jnp.sum(ds * (p_prev - p), axis=0, keepdims=True)
        dw0_ref[...] += jnp.sum(dwl, axis=0, keepdims=True)
        da0_ref[...] += jnp.sum(dap, axis=0, keepdims=True)
        dw2_ref[...] += _bdot_tn(th, dwl)
        da2_ref[...] += _bdot_tn(wa, dap)
        dkk_ref[...] += dkkw
        dka_ref[...] += dkaw
        dsm = ds * mu_v
        last = (blk % PREP_NB) == PREP_NB - 1
        nxt = jnp.where(last, 0.0, carry[0:1, :])
        up = pltpu.roll(dsm, tm - 1, 0)
        up = jnp.where(_iota2(up.shape, 0) == tm - 1, nxt, up)
        dps_ref[...] = (ds - dsm + up).astype(BF16)
        carry[0:1, :] = dsm[0:1, :]

    vec = _const_spec((1, W))
    rrow = lambda width: pl.BlockSpec((tm, width), lambda i: (rev(i), 0))
    return pl.pallas_call(
        body, grid=(nb,), name="even_prep_bwd",
        in_specs=[rrow(SHIFT), _prev_spec(SHIFT, rev), _const_spec((1, SHIFT)), vec,
                  _const_spec((2 * LORA, W)), vec, _const_spec((2 * LORA, W)), vec, vec] + [rrow(W)] * 9,
        out_specs=[rrow(SHIFT), _const_spec((1, SHIFT)), vec, _const_spec((2 * LORA, W)), vec,
                   _const_spec((2 * LORA, W)), vec, vec],
        out_shape=[jax.ShapeDtypeStruct((T, SHIFT), BF16), jax.ShapeDtypeStruct((1, SHIFT), F32),
                   jax.ShapeDtypeStruct((1, W), F32), jax.ShapeDtypeStruct((2 * LORA, W), F32),
                   jax.ShapeDtypeStruct((1, W), F32), jax.ShapeDtypeStruct((2 * LORA, W), F32),
                   jax.ShapeDtypeStruct((1, W), F32), jax.ShapeDtypeStruct((1, W), F32)],
        scratch_shapes=[pltpu.VMEM((8, SHIFT), F32)],
        compiler_params=_cparams(("arbitrary",), VMEM_BIG),
    )(ps, ps, mu, w0, w2x, a0, a2x, kkw, kaw, dr, dlw, dk2, dv, daa, dbb, dr2, dk22, dv2)


NPAIR = NH // 2
PW = 2 * HD


def _pair_cols(p):
    return slice(p * PW, (p + 1) * PW)


def _pairs(a):
    return [a[:, _pair_cols(p)] for p in range(NPAIR)]


def _stack_pair(a):
    first = _iota2(a.shape, 1) < HD
    zero = jnp.zeros_like(a)
    return jnp.concatenate([jnp.where(first, a, zero), jnp.where(first, zero, a)], axis=0)


def _unstack_pair(a):
    n = a.shape[0] // 2
    return jnp.where(_iota2((n, PW), 1) < HD, a[:n], a[n:])


def _fold_pair(a):
    n = a.shape[0] // 2
    return a[:n] + a[n:]


def _chunk_masks():
    n = 4 * L
    row = _iota2((n, n), 0)
    col = _iota2((n, n), 1)
    same = ((row // L) & 1) == ((col // L) & 1)
    ri = row & (L - 1)
    ci = col & (L - 1)
    keep = same & (((row < 2 * L) & (ri > ci)) | ((row >= 2 * L) & (ri >= ci)))
    r1 = _iota2((L, L), 0)
    c1 = _iota2((L, L), 1)
    r2 = _iota2((2 * L, 2 * L), 0)
    c2 = _iota2((2 * L, 2 * L), 1)
    return keep.astype(F32), (r1 >= c1).astype(F32), (r2 == c2).astype(F32)


def _scaled(r, lw, k2, aa, bb, tri):
    g = _hdot(tri, lw)
    eg = jnp.exp(g)
    eng = jnp.exp(-g)
    egp = jnp.exp(g - lw)
    return eg, eng, egp, aa * egp, r * eg, bb * eng, k2 * eng


def _head_cols(h):
    return slice(h * HD, (h + 1) * HD)


def _per_head(a):
    return [a[:, _head_cols(h)] for h in range(NH)]


def _pairs_operands(at, rt, bt, kt):
    x = [jnp.concatenate([_stack_pair(a), _stack_pair(r)], axis=0).astype(BF16) for a, r in zip(_pairs(at), _pairs(rt))]
    yk = [jnp.concatenate([_stack_pair(b), _stack_pair(k)], axis=0).astype(BF16) for b, k in zip(_pairs(bt), _pairs(kt))]
    return x, yk


def _pairs_matrices(x, yk, keep, eye):
    m = [_bdot_nt(a, b) * keep for a, b in zip(x, yk)]
    p = [a[:2 * L, :2 * L] for a in m]
    tinv = [eye + a for a in p]
    for _ in range(5):
        p = [_bdot(a, a) for a in p]
        tinv = [t + _bdot(t, a) for t, a in zip(tinv, p)]
    return [a.astype(BF16) for a in m], [a.astype(BF16) for a in tinv]


def _pairs_fwd(x, yk, m, tinv, vw, s0, egl):
    xh = [_bdot_nt(a, s) for a, s in zip(x, s0)]
    u = [_bdot(t, h[:2 * L] + _bdot(a[:2 * L, 2 * L:], w)) for t, h, a, w in zip(tinv, xh, m, vw)]
    uv = [jnp.concatenate([a, w], axis=0).astype(BF16) for a, w in zip(u, vw)]
    y = [h[2 * L:] + _bdot(a[2 * L:], w) for h, a, w in zip(xh, m, uv)]
    sn = [e * (s + _bdot_tn(w, b)) for e, s, w, b in zip(egl, s0, uv, yk)]
    return y, sn, uv


def _pairs_bwd(x, yk, m, tinv, uv, s0, sn, egl, dyw, dsn, keep):
    dzs = [d * e for d, e in zip(dsn, egl)]
    dgl = [jnp.sum(d * s, axis=0, keepdims=True) for d, s in zip(dsn, sn)]
    dyb = [a.astype(BF16) for a in dyw]
    t1 = [_bdot_tn(a[2 * L:], d) for a, d in zip(m, dyb)]
    t2 = [_bdot_nt(b, d) for b, d in zip(yk, dzs)]
    drhs = [_bdot_tn(t, a[:2 * L] + b[:2 * L]) for t, a, b in zip(tinv, t1, t2)]
    dv = [a[2 * L:] + b[2 * L:] + _bdot_tn(c[:2 * L, 2 * L:], d) for a, b, c, d in zip(t1, t2, m, drhs)]
    gg = [jnp.concatenate([a, b], axis=0).astype(BF16) for a, b in zip(drhs, dyw)]
    ds0 = [d + _bdot_tn(g, a) for d, g, a in zip(dzs, gg, x)]
    dm = [_bdot_nt(g, w) * keep for g, w in zip(gg, uv)]
    dx = [_bdot(g, s) + _bdot(d, b) for g, s, d, b in zip(gg, s0, dm, yk)]
    dyk = [_bdot_tn(d, a) + _bdot(w, z) for d, a, w, z in zip(dm, x, uv, dzs)]
    return dx, dyk, dv, dgl, ds0


STATE_SHAPE = (NPAIR * PW, PW)
M_SHAPE = (4 * L, NPAIR * 4 * L)
TINV_SHAPE = (2 * L, NPAIR * 2 * L)


def _rows_of(a, n):
    return [a[i * n:(i + 1) * n, :] for i in range(NPAIR)]


def _both(f):
    out = []
    for s in range(NSEQ):
        out += f(s)
    return out


def _seq_view(a):
    return a.reshape(NSEQ, SEQ, a.shape[-1])


UV_SHAPE = (4 * L, NPAIR * PW)
RW_CHUNKS = 2


def rwkv_fwd(r, lw, k2, v, aa, bb):
    def body(r_ref, lw_ref, k2_ref, v_ref, aa_ref, bb_ref, y_ref, hs_ref, hn_ref, m_ref, t_ref, uv_ref, state):
        @pl.when(pl.program_id(0) == 0)
        def _():
            state[...] = jnp.zeros_like(state)

        keep, tri, eye = _chunk_masks()
        where = [(j, s) for j in range(RW_CHUNKS) for s in range(NSEQ)]
        rows = lambda j: slice(j * L, (j + 1) * L)
        sc = [_scaled(r_ref[s, rows(j)], lw_ref[s, rows(j)], k2_ref[s, rows(j)], aa_ref[s, rows(j)],
                      bb_ref[s, rows(j)], tri) for j, s in where]
        ops = [_pairs_operands(*a[3:]) for a in sc]
        m, tinv = _pairs_matrices([a for o in ops for a in o[0]], [a for o in ops for a in o[1]], keep, eye)
        s_cur = [state[s] for s in range(NSEQ)]
        for j in range(RW_CHUNKS):
            mine = slice(j * NSEQ * NPAIR, (j + 1) * NSEQ * NPAIR)
            x = [a for o in ops[j * NSEQ:(j + 1) * NSEQ] for a in o[0]]
            yk = [a for o in ops[j * NSEQ:(j + 1) * NSEQ] for a in o[1]]
            vw = _both(lambda s: [_stack_pair(a) for a in _pairs(v_ref[s, rows(j)])])
            egl = _both(lambda s: _pairs(sc[j * NSEQ + s][0][L - 1:L, :]))
            y, sn, uv = _pairs_fwd(x, yk, m[mine], tinv[mine], vw, _both(lambda s: _rows_of(s_cur[s], PW)), egl)
            for s in range(NSEQ):
                ps = slice(s * NPAIR, (s + 1) * NPAIR)
                hs_ref[j, s] = s_cur[s]
                y_ref[s, rows(j)] = jnp.concatenate([_fold_pair(a) for a in y[ps]], axis=-1)
                m_ref[j, s] = jnp.concatenate(m[mine][ps], axis=-1)
                t_ref[j, s] = jnp.concatenate(tinv[mine][ps], axis=-1)
                uv_ref[j, s] = jnp.concatenate(uv[ps], axis=-1)
                s_cur[s] = jnp.concatenate(sn[ps], axis=0)
                hn_ref[j, s] = s_cur[s]
        for s in range(NSEQ):
            state[s] = s_cur[s]

    blk = pl.BlockSpec((NSEQ, RW_CHUNKS * L, W), lambda c: (0, c, 0))
    per_chunk = lambda shape: pl.BlockSpec((RW_CHUNKS, NSEQ) + shape, lambda c: (c, 0, 0, 0))
    saved_shapes = [(STATE_SHAPE, F32), (STATE_SHAPE, F32), (M_SHAPE, BF16), (TINV_SHAPE, BF16), (UV_SHAPE, BF16)]
    y, *saved = pl.pallas_call(
        body, grid=(NC // RW_CHUNKS,), name="rwkv_fwd",
        in_specs=[blk] * 6,
        out_specs=[blk] + [per_chunk(shape) for shape, _ in saved_shapes],
        out_shape=[jax.ShapeDtypeStruct((NSEQ, SEQ, W), F32)]
        + [jax.ShapeDtypeStruct((NC, NSEQ) + shape, dt) for shape, dt in saved_shapes],
        scratch_shapes=[pltpu.VMEM((NSEQ,) + STATE_SHAPE, F32)],
        compiler_params=_cparams(("arbitrary",), VMEM_BIG),
    )(*[_seq_view(a) for a in (r, lw, k2, v, aa, bb)])
    return y.reshape(T, W), saved


def rwkv_bwd(r, lw, k2, aa, bb, saved, dy):
    def body(r_ref, lw_ref, k2_ref, aa_ref, bb_ref, hs_ref, hn_ref, m_ref, t_ref, uv_ref, dy_ref,
             dr_ref, dlw_ref, dk2_ref, dv_ref, daa_ref, dbb_ref, dstate):
        @pl.when(pl.program_id(0) == 0)
        def _():
            dstate[...] = jnp.zeros_like(dstate)

        keep, tri, _ = _chunk_masks()
        sc = [_scaled(r_ref[s], lw_ref[s], k2_ref[s], aa_ref[s], bb_ref[s], tri) for s in range(NSEQ)]
        ops = [_pairs_operands(*sc[s][3:]) for s in range(NSEQ)]
        x, yk = _both(lambda s: ops[s][0]), _both(lambda s: ops[s][1])
        m = _both(lambda s: [m_ref[0, s][:, i * 4 * L:(i + 1) * 4 * L] for i in range(NPAIR)])
        tinv = _both(lambda s: [t_ref[0, s][:, i * 2 * L:(i + 1) * 2 * L] for i in range(NPAIR)])
        uv = _both(lambda s: _pairs(uv_ref[0, s]))
        dyw = _both(lambda s: [_stack_pair(a) for a in _pairs(dy_ref[s])])
        s0 = _both(lambda s: _rows_of(hs_ref[0, s], PW))
        sn = _both(lambda s: _rows_of(hn_ref[0, s], PW))
        dsn = _both(lambda s: _rows_of(dstate[s], PW))
        egl = _both(lambda s: _pairs(sc[s][0][L - 1:L, :]))
        dx, dyk, dvw, dgl, ds0 = _pairs_bwd(x, yk, m, tinv, uv, s0, sn, egl, dyw, dsn, keep)
        for s in range(NSEQ):
            mine = slice(s * NPAIR, (s + 1) * NPAIR)
            eg, eng, egp, at, rt, bt, kt = sc[s]
            dstate[s] = jnp.concatenate(ds0[mine], axis=0)
            dv_ref[s] = jnp.concatenate([_fold_pair(a) for a in dvw[mine]], axis=-1)
            dat = jnp.concatenate([_fold_pair(a[:2 * L]) for a in dx[mine]], axis=-1)
            drt = jnp.concatenate([_fold_pair(a[2 * L:]) for a in dx[mine]], axis=-1)
            dbt = jnp.concatenate([_fold_pair(a[:2 * L]) for a in dyk[mine]], axis=-1)
            dkt = jnp.concatenate([_fold_pair(a[2 * L:]) for a in dyk[mine]], axis=-1)
            dg = drt * rt - dbt * bt - dkt * kt
            dg = dg + jnp.where(_iota2(dg.shape, 0) == L - 1, jnp.concatenate(dgl[mine], axis=-1), 0.0)
            dgp = dat * at
            dlw_ref[s] = _hdot_tn(tri, dg + dgp) - dgp
            dr_ref[s] = drt * eg
            daa_ref[s] = dat * egp
            dbb_ref[s] = dbt * eng
            dk2_ref[s] = dkt * eng

    blk = pl.BlockSpec((NSEQ, L, W), lambda c: (0, NC - 1 - c, 0))
    per_chunk = lambda shape: pl.BlockSpec((1, NSEQ) + shape, lambda c: (NC - 1 - c, 0, 0, 0))
    outs = pl.pallas_call(
        body, grid=(NC,), name="rwkv_bwd",
        in_specs=[blk] * 5 + [per_chunk(a.shape[2:]) for a in saved] + [blk],
        out_specs=[blk] * 6,
        out_shape=[jax.ShapeDtypeStruct((NSEQ, SEQ, W), F32)] * 6,
        scratch_shapes=[pltpu.VMEM((NSEQ,) + STATE_SHAPE, F32)],
        compiler_params=_cparams(("arbitrary",)),
    )(*[_seq_view(a) for a in (r, lw, k2, aa, bb)], *saved, _seq_view(dy))
    return [a.reshape(T, W) for a in outs]


def _post_math(y, r, k2, v, ga, o, gb, lng, lnb, rk, bd):
    mu = _headsum(y, bd) * (1.0 / HD)
    yc = y - mu
    var = _headsum(yc * yc, bd) * (1.0 / HD)
    yn = yc * lax.rsqrt(var + GN_EPS) * lng + lnb
    bonus = _headsum(r * k2 * rk, bd) * v
    return (yn + bonus) * _silu(ga), o * _silu(gb)


def even_post(y, r, k2, v, ga, o, gb, lng, lnb, rk, h, w_bf):
    tm = 512

    def body(y_ref, r_ref, k2_ref, v_ref, ga_ref, o_ref, gb_ref, lng_ref, lnb_ref, rk_ref, h_ref, w_ref,
             ho_ref, zt_ref):
        ya, yb = _post_math(y_ref[...], r_ref[...], k2_ref[...], v_ref[...], ga_ref[...], o_ref[...], gb_ref[...],
                            lng_ref[...], lnb_ref[...], rk_ref[...], _head_blockdiag())
        z = jnp.concatenate([ya.astype(BF16), yb.astype(BF16)], axis=-1)
        zt_ref[...] = z.T
        ho_ref[...] = h_ref[...] + jnp.dot(z, w_ref[...], preferred_element_type=F32)

    vec = _const_spec((1, W))
    return pl.pallas_call(
        body, grid=(T // tm,), name="even_post",
        in_specs=[_row_spec(tm, W)] * 7 + [vec] * 3 + [_row_spec(tm, D), _const_spec((D, D))],
        out_specs=[_row_spec(tm, D), _col_spec(D, tm)],
        out_shape=[jax.ShapeDtypeStruct((T, D), F32), jax.ShapeDtypeStruct((D, T), BF16)],
        compiler_params=_cparams(("parallel",), VMEM_BIG),
    )(y, r, k2, v, ga, o, gb, lng, lnb, rk, h, w_bf)


def even_post_bwd(y, r, k2, v, ga, o, gb, lng, lnb, rk, dh, zt_bf, w_bf, after=None):
    tm = 512
    extra_specs, extra = _after_operand(after)

    def body(y_ref, r_ref, k2_ref, v_ref, ga_ref, o_ref, gb_ref, lng_ref, lnb_ref, rk_ref, dh_ref, zt_ref, w_ref,
             *rest):
        (dy_ref, dr_ref, dk2_ref, dv_ref, dga_ref, do_ref, dgb_ref, dlng_ref, dlnb_ref, drk_ref, dw_ref,
         acc_ref) = rest[-12:]
        dzv = _out_proj_back(dh_ref, zt_ref, w_ref, dw_ref, acc_ref)
        bd = _head_blockdiag()
        _, vjp = jax.vjp(lambda *a: _post_math(*a, bd), y_ref[...], r_ref[...], k2_ref[...], v_ref[...], ga_ref[...],
                         o_ref[...], gb_ref[...], lng_ref[...], lnb_ref[...], rk_ref[...])
        dy, dr, dk2, dv, dga, do, dgb, dlng, dlnb, drk = vjp((dzv[:, 0:W], dzv[:, W:2 * W]))
        for ref, val in ((dy_ref, dy), (dr_ref, dr), (dk2_ref, dk2), (dv_ref, dv), (dga_ref, dga), (do_ref, do),
                         (dgb_ref, dgb)):
            ref[...] = val.astype(ref.dtype)

        @pl.when(pl.program_id(0) == 0)
        def _():
            for ref in (dlng_ref, dlnb_ref, drk_ref):
                ref[...] = jnp.zeros_like(ref)

        dlng_ref[...] += dlng
        dlnb_ref[...] += dlnb
        drk_ref[...] += drk

    vec = _const_spec((1, W))
    return pl.pallas_call(
        body, grid=(T // tm,), name="even_post_bwd",
        in_specs=[_row_spec(tm, W)] * 7 + [vec] * 3 + [_row_spec(tm, D), _col_spec(D, tm), _const_spec((D, D))]
        + extra_specs,
        out_specs=[_row_spec(tm, W)] * 7 + [vec] * 3 + [_const_spec((D, D))],
        out_shape=[jax.ShapeDtypeStruct((T, W), dt) for dt in (F32, F32, F32, F32, BF16, F32, BF16)]
        + [jax.ShapeDtypeStruct((1, W), F32)] * 3 + [jax.ShapeDtypeStruct((D, D), BF16)],
        scratch_shapes=[pltpu.VMEM((D, D), F32)],
        compiler_params=_cparams(("arbitrary",), VMEM_BIG),
    )(y, r, k2, v, ga, o, gb, lng, lnb, rk, dh, zt_bf, w_bf, *extra)


PADSEQ = SEQ + LEFT * L
ATT_SCALE = 1.0 / math.sqrt(HD)
ATT_Q = 4
WIN = BAND + (ATT_Q - 1) * L
ATT_STEPS = NC // ATT_Q
ATT_BIAS_SHAPE = (NPAIR, ATT_Q * 2 * L, WIN)
ATT_WINDOW_BIAS_SHAPE = (ATT_Q, NPAIR, 2 * L, WIN)


def _stack_chunks(a):
    return jnp.concatenate([_stack_pair(a[i * L:(i + 1) * L]) for i in range(ATT_Q)], axis=0)


def _unstack_chunks(a):
    return jnp.concatenate([_unstack_pair(a[i * 2 * L:(i + 1) * 2 * L]) for i in range(ATT_Q)], axis=0)


def _window_bias(b_ref):
    return [jnp.concatenate([b_ref[c, p] for c in range(ATT_Q)], axis=0) for p in range(NPAIR)]


def _key_window(ref, step):
    start = step * (ATT_Q * L) - LEFT * L
    rows = ref[pl.ds(pl.multiple_of(jnp.maximum(start, 0), L), WIN), :]
    window = rows
    for lead in range(ATT_Q * L, LEFT * L + 1, ATT_Q * L):
        moved = jnp.concatenate([rows[WIN - lead:], rows[:WIN - lead]], axis=0)
        window = jnp.where(start == -lead, moved, window)
    return window


def _att_probs(q2, kw, bias, step):
    valid = _iota2((1, WIN), 1) >= (LEFT - step * ATT_Q) * L
    s = [jnp.where(valid, _bdot_nt(a, b) * ATT_SCALE + bias[p], NEG) for p, (a, b) in enumerate(zip(q2, kw))]
    e = [jnp.exp(a - jnp.max(a, axis=-1, keepdims=True)) for a in s]
    return [a / jnp.sum(a, axis=-1, keepdims=True) for a in e]


def attention_fwd(q, k, v, bias):
    def body(q_ref, k_ref, v_ref, b_ref, o_ref):
        step = pl.program_id(1)
        kw = _pairs(_key_window(k_ref, step))
        vw = _pairs(_key_window(v_ref, step))
        q2 = [_stack_chunks(a) for a in _pairs(q_ref[...])]
        p = _att_probs(q2, kw, _window_bias(b_ref), step)
        o_ref[...] = jnp.concatenate([_unstack_chunks(_bdot(a, b)) for a, b in zip(p, vw)], axis=-1)

    qblk = pl.BlockSpec((ATT_Q * L, W), lambda b, c: (b * ATT_STEPS + c, 0))
    kblk = pl.BlockSpec((SEQ, W), lambda b, c: (b, 0))
    return pl.pallas_call(
        body, grid=(NSEQ, ATT_STEPS), name="attention_fwd",
        in_specs=[qblk, kblk, kblk, _const_spec(ATT_WINDOW_BIAS_SHAPE)],
        out_specs=qblk, out_shape=jax.ShapeDtypeStruct((T, W), F32),
        compiler_params=_cparams(("parallel", "arbitrary")),
    )(q, k, v, bias)


def attention_bwd(q, k, v, bias, do):
    def body(q_ref, k_ref, v_ref, b_ref, do_ref, dq_ref, dko_ref, dvo_ref, db_ref, dk_ref, dv_ref):
        b = pl.program_id(0)
        c = pl.program_id(1)

        @pl.when(c == 0)
        def _():
            dk_ref[...] = jnp.zeros_like(dk_ref)
            dv_ref[...] = jnp.zeros_like(dv_ref)

        @pl.when((c == 0) & (b == 0))
        def _():
            db_ref[...] = jnp.zeros_like(db_ref)

        start = pl.multiple_of(c * (ATT_Q * L), L)
        kw = _pairs(_key_window(k_ref, c))
        vw = _pairs(_key_window(v_ref, c))
        q2 = [_stack_chunks(a) for a in _pairs(q_ref[...])]
        do2 = [_stack_chunks(a) for a in _pairs(do_ref[...].astype(BF16))]
        p = _att_probs(q2, kw, _window_bias(b_ref), c)
        dp = [_bdot_nt(a, b) for a, b in zip(do2, vw)]
        ds = [a * (d - jnp.sum(d * a, axis=-1, keepdims=True)) for a, d in zip(p, dp)]
        dss = [(a * ATT_SCALE).astype(BF16) for a in ds]
        dq_ref[...] = jnp.concatenate([_unstack_chunks(_bdot(a, b)) for a, b in zip(dss, kw)], axis=-1).astype(BF16)
        dk_ref[pl.ds(start, WIN), :] += jnp.concatenate([_bdot_tn(a, b) for a, b in zip(dss, q2)], axis=-1)
        dv_ref[pl.ds(start, WIN), :] += jnp.concatenate([_bdot_tn(a, b) for a, b in zip(p, do2)], axis=-1)
        for i in range(NPAIR):
            db_ref[i] += ds[i]

        @pl.when(c == ATT_STEPS - 1)
        def _():
            dko_ref[...] = dk_ref[LEFT * L:, :].astype(BF16)
            dvo_ref[...] = dv_ref[LEFT * L:, :].astype(BF16)

    qblk = pl.BlockSpec((ATT_Q * L, W), lambda b, c: (b * ATT_STEPS + c, 0))
    sblk = pl.BlockSpec((SEQ, W), lambda b, c: (b, 0))
    bblk = _const_spec(ATT_BIAS_SHAPE)
    return pl.pallas_call(
        body, grid=(NSEQ, ATT_STEPS), name="attention_bwd",
        in_specs=[qblk, sblk, sblk, _const_spec(ATT_WINDOW_BIAS_SHAPE), qblk],
        out_specs=[qblk, sblk, sblk, bblk],
        out_shape=[jax.ShapeDtypeStruct((T, W), BF16), jax.ShapeDtypeStruct((T, W), BF16),
                   jax.ShapeDtypeStruct((T, W), BF16), jax.ShapeDtypeStruct(ATT_BIAS_SHAPE, F32)],
        scratch_shapes=[pltpu.VMEM((PADSEQ, W), F32), pltpu.VMEM((PADSEQ, W), F32)],
        compiler_params=_cparams(("arbitrary", "arbitrary"), VMEM_BIG),
    )(q, k, v, bias, do)


NTAB = 2 * CLIP + 1
EXT = BAND + L


def _ext_onehot():
    n = _iota2((EXT, NTAB), 0)
    m = _iota2((EXT, NTAB), 1)
    return (jnp.clip(BAND - 1 - n, -CLIP, CLIP) + CLIP == m).astype(F32)


def bias_expand(table):
    def body(t_ref, o_ref):
        ext = _hdot_nt(t_ref[...], _ext_onehot())
        ext = jnp.concatenate([ext, jnp.zeros((NH, WIN - EXT), F32)], axis=-1)
        col = _iota2((L, WIN), 1)
        for h in range(NH):
            rows = jnp.broadcast_to(ext[h:h + 1], (L, WIN))
            for c in range(ATT_Q):
                inside = (col >= c * L) & (col < c * L + BAND)
                plane = pltpu.roll(rows, (c * L - (L - 1)) % WIN, 1, stride=1, stride_axis=0)
                o_ref[c, h] = jnp.where(inside, plane, NEG)

    out = pl.pallas_call(body, name="bias_expand", out_shape=jax.ShapeDtypeStruct((ATT_Q, NH, L, WIN), F32))(table)
    return out.reshape(ATT_WINDOW_BIAS_SHAPE)


def bias_grad(dbias):
    def body(d_ref, o_ref):
        acc = jnp.zeros((NH, EXT), F32)
        zpad = jnp.zeros((NH, EXT - BAND), F32)
        for i in range(L):
            s = L - 1 - i
            row = jnp.concatenate([d_ref[:, i, :], zpad], axis=-1)
            acc = acc + (pltpu.roll(row, s, 1) if s else row)
        o_ref[...] = _hdot(acc, _ext_onehot())

    return pl.pallas_call(body, name="bias_grad", out_shape=jax.ShapeDtypeStruct((NH, NTAB), F32))(dbias)


def _group_cols(g):
    return slice(g * SGC, (g + 1) * SGC)


def _sg_norm(gv, lng, lnb):
    gc = gv - jnp.mean(gv, axis=-1, keepdims=True)
    rstd = lax.rsqrt(jnp.mean(gc * gc, axis=-1, keepdims=True) + LN_EPS)
    xhat = gc * rstd
    return xhat, rstd, xhat * lng + lnb


GMLP_BWD_CHUNKS = 2


def gmlp_fwd_loss(u, v, gate, lng, lnb, wm_bf, sgb_t, h, w_bf, g_final, target):
    tm = GMLP_BWD_CHUNKS * SGC

    def body(u_ref, v_ref, gt_ref, lng_ref, lnb_ref, wm_ref, sb_ref, h_ref, w_ref, g_ref, t_ref,
             dh_ref, loss_ref, dg_ref, zt_ref):
        zs = []
        for ch in range(GMLP_BWD_CHUNKS):
            rows = slice(ch * SGC, (ch + 1) * SGC)
            _, _, vln = _sg_norm(_gelu(v_ref[rows, :]), lng_ref[...], lnb_ref[...])
            vlb = vln.astype(BF16)
            zg = []
            for g in range(NG):
                cs = _group_cols(g)
                sv = jnp.dot(wm_ref[g], vlb[:, cs], preferred_element_type=F32) + sb_ref[:, g:g + 1]
                zg.append((_gelu(u_ref[rows, cs]) * sv * _silu(gt_ref[rows, cs])).astype(BF16))
            zs.append(jnp.concatenate(zg, axis=-1))
        z = jnp.concatenate(zs, axis=0)
        zt_ref[...] = z.T
        xv = h_ref[...] + jnp.dot(z, w_ref[...], preferred_element_type=F32)
        rstd = lax.rsqrt(jnp.mean(xv * xv, axis=-1, keepdims=True) + RMS_EPS)
        xhat = xv * rstd
        err = xhat * g_ref[...] - t_ref[...]
        part = 0.5 * jnp.sum(jnp.mean(err * err, axis=-1, keepdims=True), axis=0, keepdims=True)
        dout = err * (1.0 / D)

        @pl.when(pl.program_id(0) == 0)
        def _():
            loss_ref[...] = jnp.zeros_like(loss_ref)
            dg_ref[...] = jnp.zeros_like(dg_ref)

        loss_ref[...] += jnp.broadcast_to(part, loss_ref.shape)
        dg_ref[...] += jnp.sum(dout * xhat, axis=0, keepdims=True)
        dxh = dout * g_ref[...]
        dh_ref[...] = rstd * (dxh - xhat * jnp.mean(dxh * xhat, axis=-1, keepdims=True))

    return pl.pallas_call(
        body, grid=(T // tm,), name="gmlp_fwd_loss",
        in_specs=[_row_spec(tm, D)] * 3 + [_const_spec((1, D))] * 2
        + [_const_spec((NG, SGC, SGC)), _const_spec((SGC, NG)), _row_spec(tm, D), _const_spec((D, D)),
           _const_spec((1, D)), _row_spec(tm, D)],
        out_specs=[_row_spec(tm, D), _const_spec((8, 128)), _const_spec((1, D)), _col_spec(D, tm)],
        out_shape=[jax.ShapeDtypeStruct((T, D), F32), jax.ShapeDtypeStruct((8, 128), F32),
                   jax.ShapeDtypeStruct((1, D), F32), jax.ShapeDtypeStruct((D, T), BF16)],
        compiler_params=_cparams(("arbitrary",), VMEM_BIG),
    )(u, v, gate, lng, lnb, wm_bf, sgb_t, h, w_bf, g_final, target)


def gmlp_bwd(u, v, gate, lng, lnb, wm_bf, sgb_t, dh, zt_bf, w_bf):
    def body(u_ref, v_ref, gt_ref, lng_ref, lnb_ref, wm_ref, sb_ref, dh_ref, zt_ref, w_ref,
             du_ref, dv_ref, dgt_ref, dlng_ref, dlnb_ref, dwm_ref, dsb_ref, dw_ref, acc_ref):
        @pl.when(pl.program_id(0) == 0)
        def _():
            for ref in (dlng_ref, dlnb_ref, dwm_ref, dsb_ref):
                ref[...] = jnp.zeros_like(ref)

        dz = _out_proj_back(dh_ref, zt_ref, w_ref, dw_ref, acc_ref)
        sel = (_iota2((D, NG), 0) // SGC == _iota2((D, NG), 1)).astype(F32)
        for ch in range(GMLP_BWD_CHUNKS):
            rows = slice(ch * SGC, (ch + 1) * SGC)
            gv, dgv_dv = _gelu_both(v_ref[rows, :])
            xhat, rstd, vln = _sg_norm(gv, lng_ref[...], lnb_ref[...])
            vlb = vln.astype(BF16)
            dvln = []
            dsv_all = []
            for g in range(NG):
                cs = _group_cols(g)
                uu = u_ref[rows, cs]
                gg = gt_ref[rows, cs]
                dzz = dz[rows, cs]
                sv = jnp.dot(wm_ref[g], vlb[:, cs], preferred_element_type=F32) + sb_ref[:, g:g + 1]
                gu, dgu = _gelu_both(uu)
                sg, dsg = _silu_both(gg)
                dzgu = dzz * gu
                dsv = dzgu * sg
                dgt_ref[rows, cs] = (dzgu * sv * dsg).astype(BF16)
                du_ref[rows, cs] = (dzz * sv * sg * dgu).astype(BF16)
                dsb16 = dsv.astype(BF16)
                dvln.append(lax.dot_general(wm_ref[g], dsb16, (((0,), (0,)), ((), ())), preferred_element_type=F32))
                dwm_ref[g] += lax.dot_general(dsb16, vlb[:, cs], (((1,), (1,)), ((), ())),
                                              preferred_element_type=F32)
                dsv_all.append(dsv)
            dvl = jnp.concatenate(dvln, axis=-1)
            dsb_ref[...] += _hdot(jnp.concatenate(dsv_all, axis=-1), sel)
            dlng_ref[...] += jnp.sum(dvl * xhat, axis=0, keepdims=True)
            dlnb_ref[...] += jnp.sum(dvl, axis=0, keepdims=True)
            dxh = dvl * lng_ref[...]
            dgv = rstd * (dxh - jnp.mean(dxh, axis=-1, keepdims=True)
                          - xhat * jnp.mean(dxh * xhat, axis=-1, keepdims=True))
            dv_ref[rows, :] = (dgv * dgv_dv).astype(BF16)

    tm = GMLP_BWD_CHUNKS * SGC
    return pl.pallas_call(
        body, grid=(T // tm,), name="gmlp_bwd",
        in_specs=[_row_spec(tm, D)] * 3 + [_const_spec((1, D))] * 2
        + [_const_spec((NG, SGC, SGC)), _const_spec((SGC, NG)), _row_spec(tm, D), _col_spec(D, tm),
           _const_spec((D, D))],
        out_specs=[_row_spec(tm, D)] * 3 + [_const_spec((1, D))] * 2
        + [_const_spec((NG, SGC, SGC)), _const_spec((SGC, NG)), _const_spec((D, D))],
        out_shape=[jax.ShapeDtypeStruct((T, D), BF16)] * 3 + [jax.ShapeDtypeStruct((1, D), F32)] * 2
        + [jax.ShapeDtypeStruct((NG, SGC, SGC), F32), jax.ShapeDtypeStruct((SGC, NG), F32),
           jax.ShapeDtypeStruct((D, D), BF16)],
        scratch_shapes=[pltpu.VMEM((D, D), F32)],
        compiler_params=_cparams(("arbitrary",), VMEM_BIG),
    )(u, v, gate, lng, lnb, wm_bf, sgb_t, dh, zt_bf, w_bf)


NCHIP = 4
NDEV = 8
ANY = pl.BlockSpec(memory_space=pl.ANY)


HBM = pl.BlockSpec(memory_space=pltpu.HBM)
SEM = pl.BlockSpec(memory_space=pltpu.SEMAPHORE)
EFFECT = pltpu.SideEffectType.DATAFLOW_SIDE_EFFECTING


CHIPS, EVERY, SIBLING = "chips", "every", "sibling"
SLOTS = {CHIPS: NCHIP, EVERY: NDEV, SIBLING: 1}


def _peers(scope):
    x, y, c = lax.axis_index("x"), lax.axis_index("y"), lax.axis_index("c")
    if scope == SIBLING:
        return [((x, y, 1 - c), 0)], 0
    if scope == CHIPS:
        return [((px, py, c), 2 * px + py) for px, py in ((1 - x, y), (x, 1 - y), (1 - x, 1 - y))], 2 * x + y
    out = []
    for j in range(1, NDEV):
        px, py, pc = x ^ (j >> 2), y ^ ((j >> 1) & 1), c ^ (j & 1)
        out.append(((px, py, pc), 4 * px + 2 * py + pc))
    return out, 4 * x + 2 * y + c


def _send_copies(src, land, send, recv, scatter, scope, starting):
    peers, me = _peers(scope)
    copies = []
    for t in range(len(src)):
        for j, (dev, slot) in enumerate(peers):
            k = t * len(peers) + j
            copies.append(pltpu.make_async_remote_copy(
                src_ref=src[t].at[slot] if scatter else src[t], dst_ref=land[t].at[me if starting else slot],
                send_sem=send.at[k], recv_sem=recv.at[k], device_id=dev, device_id_type=MESH))
    return copies


def _own_copies(src, land, sems, scatter, scope):
    if scope == SIBLING:
        return []
    _, me = _peers(scope)
    return [pltpu.make_async_copy(src[t].at[me] if scatter else src[t], land[t].at[me], sems.at[t])
            for t in range(len(src))]


def send_start(srcs, scatter, scope, name, after=None):
    n = len(srcs)
    slots = SLOTS[scope]
    extra_specs, extra = _after_operand(after)
    lands = [pltpu.HBM(a.shape if scatter else (slots,) + a.shape, a.dtype) for a in srcs]
    sems = [pltpu.SemaphoreType.DMA((n * max(slots - 1, 1),))] * 2 + ([] if scope == SIBLING else
                                                                     [pltpu.SemaphoreType.DMA((n,))])
    k = len(sems)

    def body(*refs):
        first_out = n + len(extra)
        src, land = refs[:n], refs[first_out + k + n:first_out + k + 2 * n]
        for cp in _send_copies(src, land, refs[first_out], refs[first_out + 1], scatter, scope, True):
            cp.start()
        for cp in _own_copies(src, land, refs[first_out + k - 1], scatter, scope):
            cp.start()
        refs[-1][...] = jnp.zeros_like(refs[-1])

    out = pl.pallas_call(
        body, name=name,
        out_shape=(*sems, *[pltpu.HBM(a.shape, a.dtype) for a in srcs], *lands, jax.ShapeDtypeStruct((8, 128), F32)),
        in_specs=[HBM] * n + extra_specs,
        out_specs=(*[SEM] * k, *[HBM] * (2 * n), pl.BlockSpec(memory_space=pltpu.VMEM)),
        input_output_aliases={i: k + i for i in range(n)},
        compiler_params=pltpu.CompilerParams(has_side_effects=EFFECT),
    )(*[pltpu.with_memory_space_constraint(a, pltpu.HBM) for a in srcs], *extra)
    return list(out[:k]), list(out[k:k + n]), list(out[k + n:k + 2 * n]), out[-1]


def send_wait(started, after, scatter, scope, name, with_sources=False, only=None):
    sems, srcs, lands, _ = started
    n, k = len(srcs), len(sems)
    wanted = range(n) if only is None else only

    def body(*refs):
        src, land = refs[:n], refs[n:2 * n]
        for t, cp in enumerate(_own_copies(src, land, refs[2 * n + k - 1], scatter, scope)):
            if t in wanted:
                cp.wait()
        copies = _send_copies(src, land, refs[2 * n], refs[2 * n + 1], scatter, scope, False)
        for i, cp in enumerate(copies):
            if i // (len(copies) // n) in wanted:
                cp.wait_send()
                cp.wait_recv()

    arrs = list(srcs) + list(lands)
    out = pl.pallas_call(
        body, name=name, out_shape=tuple(pltpu.HBM(a.shape, a.dtype) for a in arrs),
        in_specs=[HBM] * (2 * n) + [SEM] * k + [ANY], out_specs=tuple([HBM] * (2 * n)),
        input_output_aliases={i: i for i in range(2 * n)},
        compiler_params=pltpu.CompilerParams(has_side_effects=EFFECT),
    )(*arrs, *sems, after)
    return (list(out[:n]), list(out[n:])) if with_sources else list(out[n:])


def gather_weights(arrs, split):
    n = len(arrs)

    def body(*refs):
        ins, outs = refs[:n], refs[n:2 * n]
        send1, recv1, send2, recv2, loc_in, loc_out = refs[2 * n:2 * n + 6]
        staged = refs[2 * n + 6:]
        x, y, c = lax.axis_index("x"), lax.axis_index("y"), lax.axis_index("c")
        me = 2 * x + y
        sibling = (x, y, 1 - c)
        peers = [(1 - x, y), (x, 1 - y), (1 - x, 1 - y)]

        def rows_of(t, core):
            half = arrs[t].shape[0] // 2
            return pl.ds(core * half, half)

        def part(ref, t, core):
            return ref.at[rows_of(t, core)] if split[t] else ref

        load = [pltpu.make_async_copy(ins[t], staged[t], loc_in.at[t]) for t in range(n)]
        store = [pltpu.make_async_copy(staged[t], outs[t].at[me], loc_out.at[t]) for t in range(n)]
        for cp in load:
            cp.start(priority=1)
        first = []
        for t in range(n):
            for j, (px, py) in enumerate(peers):
                first.append(pltpu.make_async_remote_copy(
                    src_ref=part(ins[t], t, c), dst_ref=part(outs[t].at[me], t, c), send_sem=send1.at[t, j],
                    recv_sem=recv1.at[t, j], device_id=(px, py, c), device_id_type=MESH))
        for cp in first:
            cp.start()
        for cp_in, cp_out in zip(load, store):
            cp_in.wait()
            cp_out.start(priority=1)
        passed = []
        for t in range(n):
            for j, (px, py) in enumerate(peers):
                landed = part(outs[t].at[2 * px + py], t, c)
                pltpu.make_async_remote_copy(
                    src_ref=landed, dst_ref=landed, send_sem=send1.at[t, j], recv_sem=recv1.at[t, j],
                    device_id=(x, y, c), device_id_type=MESH).wait_recv()
                if split[t]:
                    cp = pltpu.make_async_remote_copy(
                        src_ref=landed, dst_ref=landed, send_sem=send2.at[t, j], recv_sem=recv2.at[t, j],
                        device_id=sibling, device_id_type=MESH)
                    cp.start()
                    passed.append(cp)
        for t in range(n):
            for j, (px, py) in enumerate(peers):
                if split[t]:
                    other = part(outs[t].at[2 * px + py], t, 1 - c)
                    pltpu.make_async_remote_copy(
                        src_ref=other, dst_ref=other, send_sem=send2.at[t, j], recv_sem=recv2.at[t, j],
                        device_id=(x, y, c), device_id_type=MESH).wait_recv()
        for cp in first + passed:
            cp.wait_send()
        for cp in store:
            cp.wait()

    return pl.pallas_call(
        body, name="gather_weights", in_specs=[ANY] * n, out_specs=[ANY] * n,
        out_shape=[jax.ShapeDtypeStruct((NCHIP,) + a.shape, a.dtype) for a in arrs],
        scratch_shapes=[pltpu.SemaphoreType.DMA((n, 3))] * 4 + [pltpu.SemaphoreType.DMA((n,))] * 2
        + [pltpu.VMEM(a.shape, a.dtype) for a in arrs],
    )(*arrs)


def _adam_math(g, w, m, v):
    m = ADAM_B1 * m + (1.0 - ADAM_B1) * g
    v = ADAM_B2 * v + (1.0 - ADAM_B2) * (g * g)
    m_hat = m / (1.0 - ADAM_B1 ** ADAM_STEP)
    v_hat = v / (1.0 - ADAM_B2 ** ADAM_STEP)
    delta = -ADAM_LR * (m_hat / (jnp.sqrt(v_hat) + ADAM_EPS) + ADAM_WD * w)
    return delta, m, v


def _rows_tile(rows):
    return rows if rows <= 256 else 256


def sum_chips(parts, name):
    _, rows, cols = parts.shape
    tr = _rows_tile(rows)

    def body(p_ref, o_ref):
        acc = p_ref[0].astype(F32)
        for s in range(1, NCHIP):
            acc = acc + p_ref[s].astype(F32)
        o_ref[...] = acc

    return pl.pallas_call(
        body, grid=(rows // tr,), name=name,
        in_specs=[pl.BlockSpec((NCHIP, tr, cols), lambda i: (0, i, 0))],
        out_specs=pl.BlockSpec((tr, cols), lambda i: (i, 0)),
        out_shape=jax.ShapeDtypeStruct((rows, cols), F32),
        compiler_params=_cparams(("parallel",)),
    )(parts)


def sum_chips_small(parts, name):
    n = len(parts)

    def body(*refs):
        for p_ref, o_ref in zip(refs[:n], refs[n:]):
            acc = p_ref[0]
            for s in range(1, NCHIP):
                acc = acc + p_ref[s]
            o_ref[...] = acc

    return pl.pallas_call(body, name=name, out_shape=[jax.ShapeDtypeStruct(p.shape[1:], F32) for p in parts])(*parts)


def adam_shard_small(items, name):
    n = len(items)

    def body(*refs):
        for t in range(n):
            a_ref, b_ref, w_ref, m_ref, v_ref = refs[5 * t:5 * t + 5]
            g_ref, d_ref, mo_ref, vo_ref = refs[5 * n + 4 * t:5 * n + 4 * t + 4]
            g = (a_ref[...] + b_ref[...]).reshape(w_ref.shape)
            g_ref[...] = g
            d_ref[...], mo_ref[...], vo_ref[...] = _adam_math(g, w_ref[...], m_ref[...], v_ref[...])

    out = pl.pallas_call(
        body, name=name, out_shape=[jax.ShapeDtypeStruct(it[2].shape, F32) for it in items for _ in range(4)],
    )(*[a for it in items for a in it])
    return [out[4 * t:4 * t + 4] for t in range(n)]


def adam_shard(p_mine, p_sib, w, m, v, name):
    rows, cols = p_mine.shape
    tr = _rows_tile(rows)
    lead = w.ndim == 3

    def body(a_ref, b_ref, w_ref, m_ref, v_ref, g_ref, d_ref, mo_ref, vo_ref):
        g = a_ref[...] + b_ref[...]
        g = g[None] if lead else g
        g_ref[...] = g
        d_ref[...], mo_ref[...], vo_ref[...] = _adam_math(g, w_ref[...], m_ref[...], v_ref[...])

    flat = pl.BlockSpec((tr, cols), lambda i: (i, 0))
    spec = pl.BlockSpec((1, tr, cols), lambda i: (0, i, 0)) if lead else flat
    return pl.pallas_call(
        body, grid=(rows // tr,), name=name, in_specs=[flat] * 2 + [spec] * 3, out_specs=[spec] * 4,
        out_shape=[jax.ShapeDtypeStruct(w.shape, F32)] * 4,
        compiler_params=_cparams(("parallel",)),
    )(p_mine, p_sib, w, m, v)


def adam_shard_halves_t(r_mine, r_sib, wt, mt, vt, name):
    hrows, cols = r_mine.shape
    tr = _rows_tile(hrows)
    per_half = hrows // tr

    def body(a_ref, b_ref, w_ref, m_ref, v_ref, g_ref, d_ref, mo_ref, vo_ref):
        mine = pl.program_id(0) == lax.axis_index("c")
        g = jnp.where(mine, a_ref[...], b_ref[...]).T[None]
        g_ref[...] = g
        d_ref[...], mo_ref[...], vo_ref[...] = _adam_math(g, w_ref[...], m_ref[...], v_ref[...])

    flat = pl.BlockSpec((tr, cols), lambda h, i: (i, 0))
    spec = pl.BlockSpec((1, cols, tr), lambda h, i: (0, 0, h * per_half + i))
    return pl.pallas_call(
        body, grid=(2, per_half), name=name, in_specs=[flat] * 2 + [spec] * 3, out_specs=[spec] * 4,
        out_shape=[jax.ShapeDtypeStruct(wt.shape, F32)] * 4,
        compiler_params=_cparams(("parallel", "parallel")),
    )(r_mine, r_sib, wt, mt, vt)


def adam_replicated(gathered, params, name):
    flat = []
    for i, p in enumerate(params):
        if isinstance(p, list):
            off = 0
            for wmv in p:
                n = gathered[i].shape[-1] - off if wmv[0] is None else wmv[0].shape[-1]
                flat.append((i, (off, n), wmv))
                off += n
        else:
            flat.append((i, None, p))
    ins = [a for _, _, wmv in flat for a in wmv if a is not None]
    ng = len(gathered)

    def body(*refs):
        g_refs = refs[:ng]
        in_refs = list(refs[ng:ng + len(ins)])
        out_refs = list(refs[ng + len(ins):])
        sums = []
        for r in g_refs:
            g = r[0]
            for d in range(1, NDEV):
                g = g + r[d]
            sums.append(g)
        for i, lanes, wmv in flat:
            g = sums[i] if lanes is None else sums[i][:, lanes[0]:lanes[0] + lanes[1]]
            out_refs.pop(0)[...] = g
            if wmv[0] is not None:
                w_ref, m_ref, v_ref = in_refs.pop(0), in_refs.pop(0), in_refs.pop(0)
                d_ref, mo_ref, vo_ref = out_refs.pop(0), out_refs.pop(0), out_refs.pop(0)
                d_ref[...], mo_ref[...], vo_ref[...] = _adam_math(g, w_ref[...], m_ref[...], v_ref[...])

    out_shape = []
    for i, lanes, wmv in flat:
        shape = gathered[i].shape[1:] if lanes is None else (1, lanes[1])
        out_shape += [jax.ShapeDtypeStruct(shape, F32)] * (4 if wmv[0] is not None else 1)
    outs = list(pl.pallas_call(body, name=name, out_shape=out_shape)(*gathered, *ins))
    return [[outs.pop(0) for _ in range(4 if wmv[0] is not None else 1)] for _, _, wmv in flat]


EVEN_SPLITS = (SHIFT, W, W, W, W, W)
ODD_SPLITS = (D, D, D)


def _cols_to_chips(a):
    rows, cols = a.shape
    return a.reshape(rows, NCHIP, cols // NCHIP).transpose(1, 0, 2)


def _chips_to_cols(a):
    _, rows, n = a.shape
    return a.transpose(1, 0, 2).reshape(rows, NCHIP * n)


def kernel(x, norm_g, w_in_e, shift_mu, rw_w0, rw_w2, rw_a0, rw_a2, rw_kk, rw_ka, rw_rk, rw_lnx_g, rw_lnx_b, att_bias, w_out_e, w_in_o, sg_ln_g, sg_ln_b, sg_w, sg_b, w_out_o, final_g, loss_target, m_norm_g, m_w_in_e, m_shift_mu, m_rw_w0, m_rw_w2, m_rw_a0, m_rw_a2, m_rw_kk, m_rw_ka, m_rw_rk, m_rw_lnx_g, m_rw_lnx_b, m_att_bias, m_w_out_e, m_w_in_o, m_sg_ln_g, m_sg_ln_b, m_sg_w, m_sg_b, m_w_out_o, m_final_g, v_norm_g, v_w_in_e, v_shift_mu, v_rw_w0, v_rw_w2, v_rw_a0, v_rw_a2, v_rw_kk, v_rw_ka, v_rw_rk, v_rw_lnx_g, v_rw_lnx_b, v_att_bias, v_w_out_e, v_w_in_o, v_sg_ln_g, v_sg_ln_b, v_sg_w, v_sg_b, v_w_out_o, v_final_g):
    x2 = x.reshape(T, D)
    tgt = loss_target.reshape(T, D)

    gathered = gather_weights(
        [jnp.swapaxes(w_in_e[0], 0, 1).astype(BF16), jnp.concatenate([rw_w2[0], rw_a2[0]], axis=0),
         jnp.concatenate([sg_ln_g, sg_ln_b], axis=0)], [True, True, False])
    wie = gathered[0].reshape(EVEN_IN, D)
    w2 = _chips_to_cols(gathered[1][:, :LORA])
    a2 = _chips_to_cols(gathered[1][:, LORA:])
    sglg = _chips_to_cols(gathered[2][:, 0:1])
    sglb = _chips_to_cols(gathered[2][:, 1:2])

    late = [w_out_e[0].astype(BF16), w_in_o[0].astype(BF16), w_out_o[0].astype(BF16)]
    late_started = send_start(late, False, CHIPS, "late_weights_start", after=gathered[0])

    late_state = {}

    def late_weights(layer, after):
        if layer == "even":
            srcs, lands = send_wait(late_started, after, False, CHIPS, "late_w_out_e_wait", True, only=(0,))
            late_state["rest"] = (late_started[0], srcs, lands, None)
            return lands[0].reshape(D, D)
        woe, wio, woo = send_wait(late_state["rest"], after, False, CHIPS, "late_weights_wait", only=(1, 2))
        return woe.reshape(D, D), wio, woo.reshape(D, D)

    def scatter_start(grads, name):
        return send_start([g_.astype(BF16) if g_.shape[-1] >= W else g_ for g_ in grads], True, CHIPS, name)

    started = {}

    def on_odd_grads(d_woo, d_wio):
        started["odd"] = scatter_start([d_woo.reshape(NCHIP, D // NCHIP, D), d_wio], "odd_grads_start")
        return started["odd"][-1]

    def on_even_grads(big_g):
        d_wie_half, d_woe, _, _, d_w2, d_a2, d_sglg, d_sglb = big_g
        blocks = [d_wie_half, d_woe.reshape(NCHIP, D // NCHIP, D), _cols_to_chips(d_w2), _cols_to_chips(d_a2),
                  _cols_to_chips(d_sglg), _cols_to_chips(d_sglb)]
        started["even"] = scatter_start(blocks, "even_grads_start")
        return started["even"][-1]

    def on_small_grads(layer, grads):
        if layer == "odd":
            d_sg_w, d_sg_b, d_final, d_g1 = grads
            mine = [d_sg_w.reshape(NG * SGC, SGC), d_sg_b, jnp.concatenate([d_final, d_g1], axis=1)]
        else:
            mine = [grads[-2], jnp.concatenate(grads[:-2] + grads[-1:], axis=1)]
        started[layer + "_small"] = send_start(mine, False, EVERY, layer + "_small_grads_start")
        return started[layer + "_small"][-1]

    loss_part, dx, _, _ = _local_step(
        x2, tgt, wie, late_weights, w2, a2, sglg, sglb, norm_g, shift_mu, rw_w0, rw_a0, rw_kk, rw_ka, rw_rk,
        rw_lnx_g, rw_lnx_b, att_bias, sg_w, sg_b, final_g, first_after=late_started[-1], on_odd_grads=on_odd_grads,
        on_even_grads=on_even_grads, on_small_grads=on_small_grads)
    wmv = {"w_in_e": tuple(jnp.swapaxes(a, 1, 2) for a in (w_in_e, m_w_in_e, v_w_in_e)),
           "w_out_e": (w_out_e, m_w_out_e, v_w_out_e),
           "w_in_o": (w_in_o, m_w_in_o, v_w_in_o), "w_out_o": (w_out_o, m_w_out_o, v_w_out_o),
           "rw_w2": (rw_w2, m_rw_w2, v_rw_w2), "rw_a2": (rw_a2, m_rw_a2, v_rw_a2),
           "sg_ln_g": (sg_ln_g, m_sg_ln_g, v_sg_ln_g), "sg_ln_b": (sg_ln_b, m_sg_ln_b, v_sg_ln_b)}
    sharded = {}

    def sum_and_swap(names, landed, tag):
        nbig = sum(p_.dtype == BF16 for p_ in landed)
        partial = [sum_chips(p_, "sum_" + nm) for p_, nm in zip(landed[:nbig], names)]
        if nbig < len(names):
            partial += sum_chips_small(landed[nbig:], "sum_small_" + tag)
        return send_start(partial, False, SIBLING, "swap_partials_" + tag + "_start")

    def update(names, swap_started, after, tag):
        partial, landed = send_wait(swap_started, after, False, SIBLING, "swap_partials_" + tag + "_wait", True)
        from_sibling = [a[0] for a in landed]
        nbig = sum(p_.shape[-1] >= W for p_ in partial)
        for nm, mine, sib in zip(names[:nbig], partial, from_sibling):
            if nm == "w_in_e":
                res = adam_shard_halves_t(mine, sib, *wmv[nm], "adam_" + nm)
                sharded[nm] = [jnp.swapaxes(a, 1, 2) for a in res]
            else:
                sharded[nm] = adam_shard(mine, sib, *wmv[nm], "adam_" + nm)
        if nbig < len(names):
            items = [(mine, sib, *wmv[nm]) for nm, mine, sib in zip(names, partial, from_sibling)][nbig:]
            for nm, res in zip(names[nbig:], adam_shard_small(items, "adam_small_" + tag)):
                sharded[nm] = res

    odd_names = ["w_out_o", "w_in_o"]
    even_names = ["w_in_e", "w_out_e", "rw_w2", "rw_a2", "sg_ln_g", "sg_ln_b"]
    odd_landed = send_wait(started["odd"], started["even_small"][-1], True, CHIPS, "odd_grads_wait")
    odd_swap = sum_and_swap(odd_names, odd_landed, "odd")
    done = odd_swap[-1]

    def wmv_of(*arrs, view=lambda a: a):
        return tuple(view(a) for a in arrs)

    vec = lambda a: a.reshape(1, -1)
    groups = {
        "odd": (["sg_w", "sg_b", "final_g", "norm_g1"],
                [wmv_of(sg_w, m_sg_w, v_sg_w, view=lambda a: a.reshape(NG * SGC, SGC)),
                 wmv_of(sg_b, m_sg_b, v_sg_b, view=lambda a: a[0]),
                 [wmv_of(final_g, m_final_g, v_final_g, view=vec),
                  wmv_of(norm_g, m_norm_g, v_norm_g, view=lambda a: a[1:2])]]),
        "even": (["att_bias", "norm_g0", "shift_mu", "rw_w0", "rw_a0", "rw_kk", "rw_ka", "rw_rk", "rw_lnx_g",
                  "rw_lnx_b", "loss"],
                 [wmv_of(att_bias, m_att_bias, v_att_bias, view=lambda a: a[0]),
                  [wmv_of(norm_g, m_norm_g, v_norm_g, view=lambda a: a[0:1]),
                   wmv_of(shift_mu, m_shift_mu, v_shift_mu), wmv_of(rw_w0, m_rw_w0, v_rw_w0),
                   wmv_of(rw_a0, m_rw_a0, v_rw_a0), wmv_of(rw_kk, m_rw_kk, v_rw_kk), wmv_of(rw_ka, m_rw_ka, v_rw_ka),
                   wmv_of(rw_rk, m_rw_rk, v_rw_rk, view=vec), wmv_of(rw_lnx_g, m_rw_lnx_g, v_rw_lnx_g),
                   wmv_of(rw_lnx_b, m_rw_lnx_b, v_rw_lnx_b), (None, None, None)]]),
    }
    rep = {}
    for layer in ("odd", "even"):
        nms, params = groups[layer]
        gathered_g = send_wait(started[layer + "_small"], done, False, EVERY, layer + "_small_grads_wait")
        for nm, res in zip(nms, adam_replicated(gathered_g, params, "adam_" + layer + "_small")):
            rep[nm] = res
        done = rep[nms[0]][0]
    native = {"sg_w": sg_w.shape, "sg_b": sg_b.shape, "final_g": final_g.shape, "rw_rk": rw_rk.shape,
              "att_bias": att_bias.shape}
    for nm, shape in native.items():
        rep[nm] = [a.reshape(shape) for a in rep[nm]]
    rep["norm_g"] = [jnp.concatenate([a, b], axis=0) for a, b in zip(rep["norm_g0"], rep["norm_g1"])]
    even_landed = send_wait(started["even"], done, True, CHIPS, "even_grads_wait")
    even_swap = sum_and_swap(even_names, even_landed, "even")
    update(odd_names, odd_swap, even_swap[-1], "odd")
    update(even_names, even_swap, sharded["w_in_o"][0], "even")

    order = ["norm_g", "w_in_e", "shift_mu", "rw_w0", "rw_w2", "rw_a0", "rw_a2", "rw_kk", "rw_ka", "rw_rk",
             "rw_lnx_g", "rw_lnx_b", "att_bias", "w_out_e", "w_in_o", "sg_ln_g", "sg_ln_b", "sg_w", "sg_b",
             "w_out_o", "final_g"]
    results = {**sharded, **rep}
    outs = [rep["loss"][0][0, 0], dx.reshape(NSEQ, SEQ, D)]
    for kind in range(4):
        outs += [results[nm][kind] for nm in order]
    return tuple(outs)


def _local_step(x2, tgt, wie_t, late_weights, w2, a2, sglg, sglb, norm_g, shift_mu, rw_w0, rw_a0, rw_kk, rw_ka, rw_rk,
                rw_lnx_g, rw_lnx_b, att_bias, sg_w, sg_b, final_g, first_after=None, on_odd_grads=None,
                on_even_grads=None, on_small_grads=None):
    zl = jnp.zeros((LORA, W), F32)
    w2x = jnp.concatenate([w2, zl], axis=0)
    a2x = jnp.concatenate([zl, a2], axis=0)
    rk = rw_rk.reshape(1, W)
    pos = np.arange(SGC)
    sg_mask = jnp.asarray(((pos[None, :] // L) <= (pos[:, None] // L)).astype(np.float32))
    wm = (sg_w[0] * sg_mask[None]).astype(BF16)
    sgb_t = sg_b[0].T

    xn0, ps, ga, q, kb, vb, gb = ln_in_proj(x2, norm_g[0:1], wie_t, EVEN_SPLITS, "in_proj_even", after=first_after,
                                            w_t=True, bf16_pieces=(2, 3, 4))
    r, lw, k2, v, aa, bb = even_prep(ps, shift_mu, rw_w0, w2x, rw_a0, a2x, rw_kk, rw_ka)
    y, rw_saved = rwkv_fwd(r, lw, k2, v, aa, bb)
    bias = bias_expand(att_bias[0])

    o = attention_fwd(q, kb, vb, bias)
    woe = late_weights("even", o)
    h1, zt = even_post(y, r, k2, v, ga, o, gb, rw_lnx_g, rw_lnx_b, rk, x2, woe)
    woe, wio, woo = late_weights("odd", h1)
    xn1, u, vv, gt = ln_in_proj(h1, norm_g[1:2], wio, ODD_SPLITS, "in_proj_odd")
    dh2, loss_part, d_final_g, z2t = gmlp_fwd_loss(u, vv, gt, sglg, sglb, wm, sgb_t, h1, woo, final_g[None], tgt)

    du, dvv, dgt, d_sglg, d_sglb, d_wm, d_sgb_t, d_woo = gmlp_bwd(u, vv, gt, sglg, sglb, wm, sgb_t, dh2, z2t, woo)
    dp_odd = [du, dvv, dgt]
    d_wio = matmul_acc_chips(xn1, dp_odd, "in_proj_odd_dw")
    token = on_odd_grads(d_woo, d_wio) if on_odd_grads else None
    dh1, d_g1 = in_proj_bwd_x(h1, norm_g[1:2], wio, dp_odd, dh2, "in_proj_odd_bwd", after=token)
    odd_small = [d_wm * sg_mask[None], d_sgb_t.T, d_final_g, d_g1]
    token = on_small_grads("odd", odd_small) if on_small_grads else None
    dy, dr2, dk22, dv2, dga, do, dgb, d_lng, d_lnb, d_rk, d_woe = even_post_bwd(
        y, r, k2, v, ga, o, gb, rw_lnx_g, rw_lnx_b, rk, dh1, zt, woe, after=token)
    dq, dkb, dvb, dbias = attention_bwd(q, kb, vb, bias, do)
    dbias = sum(dbias[:, i * 2 * L:(i + 1) * 2 * L, i * L:i * L + BAND] for i in range(ATT_Q))
    d_att_bias = bias_grad(dbias.reshape(NH, L, BAND))
    dr, dlw, dk2, dv, daa, dbb = rwkv_bwd(r, lw, k2, aa, bb, rw_saved, dy)
    dps, d_mu, d_w0, d_w2x, d_a0, d_a2x, d_kk, d_ka = even_prep_bwd(
        ps, shift_mu, rw_w0, w2x, rw_a0, a2x, rw_kk, rw_ka, dr, dlw, dk2, dv, daa, dbb, dr2, dk22, dv2)
    dp_even = [dps, dga, dq, dkb, dvb, dgb]
    d_wie = matmul_acc_chips(xn0, dp_even, "in_proj_even_dw", add_cores=on_even_grads is not None)
    big_g = (d_wie, d_woe, d_wio, d_woo, d_w2x[:LORA], d_a2x[LORA:], d_sglg, d_sglb)
    token = on_even_grads(big_g) if on_even_grads else None
    dx, d_g0 = in_proj_bwd_x(x2, norm_g[0:1], wie_t, dp_even, dh1, "in_proj_even_bwd", after=token, w_t=True)
    even_small = [d_g0, d_mu, d_w0, d_a0, d_kk, d_ka, d_rk, d_lng, d_lnb, d_att_bias]
    if on_small_grads:
        on_small_grads("even", even_small + [loss_part[0:1, :]])
    rep_g = [jnp.concatenate([d_g0, d_g1], axis=0)] + even_small[1:] + odd_small[:3]
    return loss_part[0, 0], dx, big_g, rep_g
```

```python
import functools
import math

import jax
import jax.numpy as jnp
import numpy as np
from jax import lax
from jax.experimental import pallas as pl
from jax.experimental.pallas import tpu as pltpu

F32 = jnp.float32
BF16 = jnp.bfloat16
HI = lax.Precision.HIGHEST

D = 1024
SEQ = 2048
NSEQ = 2
T = NSEQ * SEQ
HD = 64
NH = 8
W = 512
SHIFT = 1664
LORA = 64
EVEN_IN = 4224
ODD_IN = 3072
L = 64
NC = SEQ // L
LEFT = 8
BAND = (LEFT + 1) * L
CLIP = 128
SGC = 128
NG = 8
RMS_EPS = 1e-6
LN_EPS = 1e-5
GN_EPS = 64e-5
NEG = -1e30
VMEM_BIG = 56 * 1024 * 1024

ADAM_LR = 0.001
ADAM_B1 = 0.9
ADAM_B2 = 0.999
ADAM_EPS = 1e-08
ADAM_WD = 0.01
ADAM_STEP = 10

MESH = pl.DeviceIdType.MESH


def _bdot(a, b):
    return jnp.dot(a.astype(BF16), b.astype(BF16), preferred_element_type=F32)


def _bdot_nt(a, b):
    return lax.dot_general(a.astype(BF16), b.astype(BF16), (((1,), (1,)), ((), ())), preferred_element_type=F32)


def _bdot_tn(a, b):
    return lax.dot_general(a.astype(BF16), b.astype(BF16), (((0,), (0,)), ((), ())), preferred_element_type=F32)


def _hdot(a, b):
    return jnp.dot(a, b, precision=HI, preferred_element_type=F32)


def _hdot_nt(a, b):
    return lax.dot_general(a, b, (((1,), (1,)), ((), ())), precision=HI, preferred_element_type=F32)


def _hdot_tn(a, b):
    return lax.dot_general(a, b, (((0,), (0,)), ((), ())), precision=HI, preferred_element_type=F32)


def _iota2(shape, dim):
    return lax.broadcasted_iota(jnp.int32, shape, dim)


def _head_blockdiag():
    r = _iota2((2 * HD, 2 * HD), 0) // HD
    c = _iota2((2 * HD, 2 * HD), 1) // HD
    return (r == c).astype(BF16)


def _headsum_impl(x, bd):
    hi = x.astype(BF16)
    mid = (x - hi.astype(F32)).astype(BF16)
    n = bd.shape[0]
    out = [jnp.dot(hi[:, i:i + n], bd, preferred_element_type=F32) + jnp.dot(mid[:, i:i + n], bd, preferred_element_type=F32)
           for i in range(0, x.shape[1], n)]
    return jnp.concatenate(out, axis=-1)


@jax.custom_vjp
def _headsum(x, bd):
    return _headsum_impl(x, bd)


def _headsum_fwd(x, bd):
    return _headsum_impl(x, bd), bd


def _headsum_bwd(bd, ct):
    return _headsum_impl(ct, bd), None


_headsum.defvjp(_headsum_fwd, _headsum_bwd)


def _silu(x):
    return x * jax.nn.sigmoid(x)


_GELU_C = math.sqrt(2.0 / math.pi)


def _gelu(x):
    return 0.5 * x * (1.0 + jnp.tanh(_GELU_C * (x + 0.044715 * (x * x * x))))


def _silu_both(x):
    s = jax.nn.sigmoid(x)
    xs = x * s
    return xs, s + xs * (1.0 - s)


def _gelu_both(x):
    x2 = x * x
    t = jnp.tanh(_GELU_C * (x + 0.044715 * (x2 * x)))
    half = 0.5 * (1.0 + t)
    return x * half, half + 0.5 * x * (1.0 - t * t) * _GELU_C * (1.0 + 3.0 * 0.044715 * x2)


def _softplus(x):
    return jnp.maximum(x, 0.0) + jnp.log(1.0 + jnp.exp(-jnp.abs(x)))


def _cparams(sem, vmem=None):
    return pltpu.CompilerParams(dimension_semantics=sem, vmem_limit_bytes=vmem)


def _row_spec(tm, width):
    return pl.BlockSpec((tm, width), lambda i: (i, 0))


def _col_spec(height, tm):
    return pl.BlockSpec((height, tm), lambda i: (0, i))


def _const_spec(shape):
    nd = len(shape)
    return pl.BlockSpec(shape, lambda *_: (0,) * nd)


def _weight_dims(w_bf, w_t):
    if w_bf.ndim == 3:
        return None, w_bf.shape[0] * w_bf.shape[2]
    return (((1,), (1,)), ((), ())) if w_t else (((1,), (0,)), ((), ())), w_bf.shape[0 if w_t else 1]


def _proj(xn, w_ref, dims):
    if dims is None:
        return jnp.concatenate([jnp.dot(xn, w_ref[s], preferred_element_type=F32) for s in range(w_ref.shape[0])],
                               axis=-1)
    return lax.dot_general(xn, w_ref[...], dims, preferred_element_type=F32)


def _proj_back(dp, w_ref, w_t):
    nt = (((1,), (1,)), ((), ()))
    if len(w_ref.shape) == 3:
        nb = w_ref.shape[2]
        parts = [lax.dot_general(dp[:, s * nb:(s + 1) * nb], w_ref[s], nt, preferred_element_type=F32)
                 for s in range(w_ref.shape[0])]
        return sum(parts[1:], parts[0])
    return lax.dot_general(dp, w_ref[...], (((1,), (0,)), ((), ())) if w_t else nt, preferred_element_type=F32)


def ln_in_proj(x, g, w_bf, splits, name, after=None, w_t=False, bf16_pieces=()):
    dims, n = _weight_dims(w_bf, w_t)
    dtypes = [BF16 if i in bf16_pieces else F32 for i in range(len(splits))]
    tm = 512 if n <= ODD_IN else 256
    spans = []
    o = 0
    for s in splits:
        spans.append((o, o + s))
        o += s
    assert o == n
    extra_specs, extra = _after_operand(after)

    def body(x_ref, g_ref, w_ref, *rest):
        xn_ref, outs = rest[len(extra)], rest[len(extra) + 1:]
        xv = x_ref[...]
        rstd = lax.rsqrt(jnp.mean(xv * xv, axis=-1, keepdims=True) + RMS_EPS)
        xn = (xv * rstd * g_ref[...]).astype(BF16)
        xn_ref[...] = xn.T
        p = _proj(xn, w_ref, dims)
        for o_ref, (a, b) in zip(outs, spans):
            o_ref[...] = p[:, a:b].astype(o_ref.dtype)

    return pl.pallas_call(
        body, grid=(T // tm,), name=name,
        in_specs=[_row_spec(tm, D), _const_spec((1, D)), _const_spec(w_bf.shape)] + extra_specs,
        out_specs=[_col_spec(D, tm)] + [_row_spec(tm, s) for s in splits],
        out_shape=[jax.ShapeDtypeStruct((D, T), BF16)]
        + [jax.ShapeDtypeStruct((T, s), dt) for s, dt in zip(splits, dtypes)],
        compiler_params=_cparams(("parallel",), VMEM_BIG),
    )(x, g, w_bf, *extra)


def in_proj_bwd_x(x, g, w_bf, dps, dres, name, after=None, w_t=False):
    tm = 512
    widths = [d.shape[1] for d in dps]
    extra_specs, extra = _after_operand(after)

    def body(x_ref, g_ref, w_ref, dres_ref, *rest):
        dp_refs = rest[:len(widths)]
        dx_ref, dg_ref = rest[-2:]
        dp = jnp.concatenate([r[...] for r in dp_refs], axis=-1)
        dxn = _proj_back(dp, w_ref, w_t)
        xv = x_ref[...]
        rstd = lax.rsqrt(jnp.mean(xv * xv, axis=-1, keepdims=True) + RMS_EPS)
        xhat = xv * rstd
        dgp = jnp.sum(dxn * xhat, axis=0, keepdims=True)

        @pl.when(pl.program_id(0) == 0)
        def _():
            dg_ref[...] = jnp.zeros_like(dg_ref)

        dg_ref[...] += dgp
        dxh = dxn * g_ref[...]
        dx_ref[...] = dres_ref[...] + rstd * (dxh - xhat * jnp.mean(dxh * xhat, axis=-1, keepdims=True))

    return pl.pallas_call(
        body, grid=(T // tm,), name=name,
        in_specs=[_row_spec(tm, D), _const_spec((1, D)), _const_spec(w_bf.shape), _row_spec(tm, D)]
        + [_row_spec(tm, s) for s in widths] + extra_specs,
        out_specs=[_row_spec(tm, D), _const_spec((1, D))],
        out_shape=[jax.ShapeDtypeStruct((T, D), F32), jax.ShapeDtypeStruct((1, D), F32)],
        compiler_params=_cparams(("arbitrary",), VMEM_BIG),
    )(x, g, w_bf, dres, *dps, *extra)


def _after_operand(after):
    return ([ANY], [after]) if after is not None else ([], [])


def matmul_acc_chips(at_bf, pieces, name, after=None, add_cores=False):
    k = at_bf.shape[0]
    widths = [p.shape[1] for p in pieces]
    nb = sum(widths) // NCHIP
    tm = 512
    steps = T // tm
    half = k // 2
    extra_specs, extra = _after_operand(after)

    def body(a_ref, *rest):
        o_ref, acc = rest[len(widths) + len(extra):][:2]

        @pl.when(pl.program_id(0) == 0)
        def _():
            acc[...] = jnp.zeros_like(acc)

        a = a_ref[...]
        b = jnp.concatenate([r[...] for r in rest[:len(widths)]], axis=-1)
        for s in range(NCHIP):
            acc[s] += jnp.dot(a, b[:, s * nb:(s + 1) * nb], preferred_element_type=F32)

        @pl.when(pl.program_id(0) == steps - 1)
        def _():
            if not add_cores:
                o_ref[...] = acc[...].astype(BF16)
            else:
                give, got, send, recv = rest[-4:]
                x, y, c = lax.axis_index("x"), lax.axis_index("y"), lax.axis_index("c")
                theirs = pl.multiple_of((1 - c) * half, half)
                mine = pl.multiple_of(c * half, half)
                copies = []
                for s in range(NCHIP):
                    give[s] = acc[s, pl.ds(theirs, half), :].astype(BF16)
                    copies.append(pltpu.make_async_remote_copy(
                        src_ref=give.at[s], dst_ref=got.at[s], send_sem=send.at[s], recv_sem=recv.at[s],
                        device_id=(x, y, 1 - c), device_id_type=MESH))
                    copies[s].start()
                for s in range(NCHIP):
                    copies[s].wait()
                    o_ref[s] = (acc[s, pl.ds(mine, half), :] + got[s].astype(F32)).astype(BF16)

    out_rows = half if add_cores else k
    exchange = ([pltpu.VMEM((NCHIP, half, nb), BF16)] * 2 + [pltpu.SemaphoreType.DMA((NCHIP,))] * 2 if add_cores
                else [])
    return pl.pallas_call(
        body, grid=(steps,), name=name,
        in_specs=[_col_spec(k, tm)] + [_row_spec(tm, w_) for w_ in widths] + extra_specs,
        out_specs=_const_spec((NCHIP, out_rows, nb)),
        out_shape=jax.ShapeDtypeStruct((NCHIP, out_rows, nb), BF16),
        scratch_shapes=[pltpu.VMEM((NCHIP, k, nb), F32)] + exchange,
        compiler_params=_cparams(("arbitrary",), VMEM_BIG),
    )(at_bf, *pieces, *extra)


def _out_proj_back(dh_ref, zt_ref, w_ref, dw_ref, acc_ref):
    dhb = dh_ref[...].astype(BF16)

    @pl.when(pl.program_id(0) == 0)
    def _():
        acc_ref[...] = jnp.zeros_like(acc_ref)

    acc_ref[...] += jnp.dot(zt_ref[...], dhb, preferred_element_type=F32)

    @pl.when(pl.program_id(0) == pl.num_programs(0) - 1)
    def _():
        dw_ref[...] = acc_ref[...].astype(dw_ref.dtype)

    return lax.dot_general(dhb, w_ref[...], (((1,), (1,)), ((), ())), preferred_element_type=F32)


PREP_TM = 512
PREP_NB = SEQ // PREP_TM


def _prep_elem(k, wl, apre, kkw, kaw, bd):
    wraw = -_softplus(-wl) - 0.5
    lw = -jnp.exp(wraw)
    asig = jax.nn.sigmoid(apre)
    kkr = k * kkw
    nrm = jnp.maximum(jnp.sqrt(_headsum(kkr * kkr, bd)), 1e-12)
    kk = kkr / nrm
    k2 = k * (1.0 + (asig - 1.0) * kaw)
    return lw, k2, -kk, kk * asig


def _prep_elem_bwd(k, wl, apre, kkw, kaw, bd, dlw, dk2, daa, dbb):
    s = -wl
    sp = _softplus(s)
    dwl = dlw * (-jnp.exp(-sp - 0.5)) * jnp.exp(s - sp)
    asig = jax.nn.sigmoid(apre)
    kkr = k * kkw
    root = jnp.sqrt(_headsum(kkr * kkr, bd))
    inv = 1.0 / jnp.maximum(root, 1e-12)
    kk = kkr * inv
    dkk = dbb * asig - daa
    dap = (dbb * kk + dk2 * k * kaw) * asig * (1.0 - asig)
    through_norm = jnp.where(root > 1e-12, kk * _headsum(dkk * kkr, bd) * inv, 0.0)
    dkkr = inv * (dkk - through_norm)
    gain = 1.0 + (asig - 1.0) * kaw
    dk = dkkr * kkw + dk2 * gain
    dkkw = jnp.sum(dkkr * k, axis=0, keepdims=True)
    dkaw = jnp.sum(dk2 * k * (asig - 1.0), axis=0, keepdims=True)
    return dk, dwl, dap, dkkw, dkaw


def _shifted(ps_ref, prev_ref, mu, blk):
    p = ps_ref[...]
    first = (blk % PREP_NB) == 0
    prev_row = jnp.where(first, 0.0, prev_ref[7:8, :])
    rolled = pltpu.roll(p, 1, 0)
    p_prev = jnp.where(_iota2(p.shape, 0) == 0, prev_row, rolled)
    return p, p_prev, p + (p_prev - p) * mu


def _prev_spec(width, blk_of):
    return pl.BlockSpec((8, width), lambda i: (jnp.maximum(blk_of(i) * (PREP_TM // 8) - 1, 0), 0))


def even_prep(ps, mu, w0, w2x, a0, a2x, kkw, kaw):
    tm = PREP_TM

    def body(ps_ref, prev_ref, mu_ref, w0_ref, w2_ref, a0_ref, a2_ref, kk_ref, ka_ref,
             r_ref, lw_ref, k2_ref, v_ref, aa_ref, bb_ref):
        _, _, s = _shifted(ps_ref, prev_ref, mu_ref[...], pl.program_id(0))
        wa = s[:, 3 * W:]
        wl = w0_ref[...] + _bdot(jnp.tanh(wa), w2_ref[...])
        apre = a0_ref[...] + _bdot(wa, a2_ref[...])
        lw, k2, aa, bb = _prep_elem(s[:, W:2 * W], wl, apre, kk_ref[...], ka_ref[...], _head_blockdiag())
        r_ref[...] = s[:, 0:W]
        v_ref[...] = s[:, 2 * W:3 * W]
        lw_ref[...] = lw
        k2_ref[...] = k2
        aa_ref[...] = aa
        bb_ref[...] = bb

    vec = _const_spec((1, W))
    return pl.pallas_call(
        body, grid=(T // tm,), name="even_prep",
        in_specs=[_row_spec(tm, SHIFT), _prev_spec(SHIFT, lambda i: i), _const_spec((1, SHIFT)), vec,
                  _const_spec((2 * LORA, W)), vec, _const_spec((2 * LORA, W)), vec, vec],
        out_specs=[_row_spec(tm, W)] * 6,
        out_shape=[jax.ShapeDtypeStruct((T, W), F32)] * 6,
        compiler_params=_cparams(("parallel",), VMEM_BIG),
    )(ps, ps, mu, w0, w2x, a0, a2x, kkw, kaw)


def even_prep_bwd(ps, mu, w0, w2x, a0, a2x, kkw, kaw, dr, dlw, dk2, dv, daa, dbb, dr2, dk22, dv2):
    tm = PREP_TM
    nb = T // tm
    rev = lambda i: nb - 1 - i

    def body(ps_ref, prev_ref, mu_ref, w0_ref, w2_ref, a0_ref, a2_ref, kk_ref, ka_ref,
             dr_ref, dlw_ref, dk2_ref, dv_ref, daa_ref, dbb_ref, dr2_ref, dk22_ref, dv2_ref,
             dps_ref, dmu_ref, dw0_ref, dw2_ref, da0_ref, da2_ref, dkk_ref, dka_ref, carry):
        i = pl.program_id(0)
        blk = rev(i)
        mu_v = mu_ref[...]
        p, p_prev, s = _shifted(ps_ref, prev_ref, mu_v, blk)
        wa = s[:, 3 * W:]
        th = jnp.tanh(wa)
        wl = w0_ref[...] + _bdot(th, w2_ref[...])
        apre = a0_ref[...] + _bdot(wa, a2_ref[...])
        bd = _head_blockdiag()
        k = s[:, W:2 * W]
        dk, dwl, dap, dkkw, dkaw = _prep_elem_bwd(k, wl, apre, kk_ref[...], ka_ref[...], bd, dlw_ref[...],
                                                  dk2_ref[...] + dk22_ref[...], daa_ref[...], dbb_ref[...])
        dwa = _bdot_nt(dwl, w2_ref[...]) * (1.0 - th * th) + _bdot_nt(dap, a2_ref[...])
        ds = jnp.concatenate([dr_ref[...] + dr2_ref[...], dk, dv_ref[...] + dv2_ref[...], dwa], axis=-1)

        @pl.when(i == 0)
        def _():
            for ref in (dmu_ref, dw0_ref, dw2_ref, da0_ref, da2_ref, dkk_ref, dka_ref, carry):
                ref[...] = jnp.zeros_like(ref)

        dmu_ref[...] += jnp.sum(ds * (p_prev - p), axis=0, keepdims=True)
        dw0_ref[...] += jnp.sum(dwl, axis=0, keepdims=True)
        da0_ref[...] += jnp.sum(dap, axis=0, keepdims=True)
        dw2_ref[...] += _bdot_tn(th, dwl)
        da2_ref[...] += _bdot_tn(wa, dap)
        dkk_ref[...] += dkkw
        dka_ref[...] += dkaw
        dsm = ds * mu_v
        last = (blk % PREP_NB) == PREP_NB - 1
        nxt = jnp.where(last, 0.0, carry[0:1, :])
        up = pltpu.roll(dsm, tm - 1, 0)
        up = jnp.where(_iota2(up.shape, 0) == tm - 1, nxt, up)
        dps_ref[...] = (ds - dsm + up).astype(BF16)
        carry[0:1, :] = dsm[0:1, :]

    vec = _const_spec((1, W))
    rrow = lambda width: pl.BlockSpec((tm, width), lambda i: (rev(i), 0))
    return pl.pallas_call(
        body, grid=(nb,), name="even_prep_bwd",
        in_specs=[rrow(SHIFT), _prev_spec(SHIFT, rev), _const_spec((1, SHIFT)), vec,
                  _const_spec((2 * LORA, W)), vec, _const_spec((2 * LORA, W)), vec, vec] + [rrow(W)] * 9,
        out_specs=[rrow(SHIFT), _const_spec((1, SHIFT)), vec, _const_spec((2 * LORA, W)), vec,
                   _const_spec((2 * LORA, W)), vec, vec],
        out_shape=[jax.ShapeDtypeStruct((T, SHIFT), BF16), jax.ShapeDtypeStruct((1, SHIFT), F32),
                   jax.ShapeDtypeStruct((1, W), F32), jax.ShapeDtypeStruct((2 * LORA, W), F32),
                   jax.ShapeDtypeStruct((1, W), F32), jax.ShapeDtypeStruct((2 * LORA, W), F32),
                   jax.ShapeDtypeStruct((1, W), F32), jax.ShapeDtypeStruct((1, W), F32)],
        scratch_shapes=[pltpu.VMEM((8, SHIFT), F32)],
        compiler_params=_cparams(("arbitrary",), VMEM_BIG),
    )(ps, ps, mu, w0, w2x, a0, a2x, kkw, kaw, dr, dlw, dk2, dv, daa, dbb, dr2, dk22, dv2)


NPAIR = NH // 2
PW = 2 * HD


def _pair_cols(p):
    return slice(p * PW, (p + 1) * PW)


def _pairs(a):
    return [a[:, _pair_cols(p)] for p in range(NPAIR)]


def _stack_pair(a):
    first = _iota2(a.shape, 1) < HD
    zero = jnp.zeros_like(a)
    return jnp.concatenate([jnp.where(first, a, zero), jnp.where(first, zero, a)], axis=0)


def _unstack_pair(a):
    n = a.shape[0] // 2
    return jnp.where(_iota2((n, PW), 1) < HD, a[:n], a[n:])


def _fold_pair(a):
    n = a.shape[0] // 2
    return a[:n] + a[n:]


def _chunk_masks():
    n = 4 * L
    row = _iota2((n, n), 0)
    col = _iota2((n, n), 1)
    same = ((row // L) & 1) == ((col // L) & 1)
    ri = row & (L - 1)
    ci = col & (L - 1)
    keep = same & (((row < 2 * L) & (ri > ci)) | ((row >= 2 * L) & (ri >= ci)))
    r1 = _iota2((L, L), 0)
    c1 = _iota2((L, L), 1)
    r2 = _iota2((2 * L, 2 * L), 0)
    c2 = _iota2((2 * L, 2 * L), 1)
    return keep.astype(F32), (r1 >= c1).astype(F32), (r2 == c2).astype(F32)


def _scaled(r, lw, k2, aa, bb, tri):
    g = _hdot(tri, lw)
    eg = jnp.exp(g)
    eng = jnp.exp(-g)
    egp = jnp.exp(g - lw)
    return eg, eng, egp, aa * egp, r * eg, bb * eng, k2 * eng


def _head_cols(h):
    return slice(h * HD, (h + 1) * HD)


def _per_head(a):
    return [a[:, _head_cols(h)] for h in range(NH)]


def _pairs_operands(at, rt, bt, kt):
    x = [jnp.concatenate([_stack_pair(a), _stack_pair(r)], axis=0).astype(BF16) for a, r in zip(_pairs(at), _pairs(rt))]
    yk = [jnp.concatenate([_stack_pair(b), _stack_pair(k)], axis=0).astype(BF16) for b, k in zip(_pairs(bt), _pairs(kt))]
    return x, yk


def _pairs_matrices(x, yk, keep, eye):
    m = [_bdot_nt(a, b) * keep for a, b in zip(x, yk)]
    p = [a[:2 * L, :2 * L] for a in m]
    tinv = [eye + a for a in p]
    for _ in range(5):
        p = [_bdot(a, a) for a in p]
        tinv = [t + _bdot(t, a) for t, a in zip(tinv, p)]
    return [a.astype(BF16) for a in m], [a.astype(BF16) for a in tinv]


def _pairs_fwd(x, yk, m, tinv, vw, s0, egl):
    xh = [_bdot_nt(a, s) for a, s in zip(x, s0)]
    u = [_bdot(t, h[:2 * L] + _bdot(a[:2 * L, 2 * L:], w)) for t, h, a, w in zip(tinv, xh, m, vw)]
    uv = [jnp.concatenate([a, w], axis=0).astype(BF16) for a, w in zip(u, vw)]
    y = [h[2 * L:] + _bdot(a[2 * L:], w) for h, a, w in zip(xh, m, uv)]
    sn = [e * (s + _bdot_tn(w, b)) for e, s, w, b in zip(egl, s0, uv, yk)]
    return y, sn, uv


def _pairs_bwd(x, yk, m, tinv, uv, s0, sn, egl, dyw, dsn, keep):
    dzs = [d * e for d, e in zip(dsn, egl)]
    dgl = [jnp.sum(d * s, axis=0, keepdims=True) for d, s in zip(dsn, sn)]
    dyb = [a.astype(BF16) for a in dyw]
    t1 = [_bdot_tn(a[2 * L:], d) for a, d in zip(m, dyb)]
    t2 = [_bdot_nt(b, d) for b, d in zip(yk, dzs)]
    drhs = [_bdot_tn(t, a[:2 * L] + b[:2 * L]) for t, a, b in zip(tinv, t1, t2)]
    dv = [a[2 * L:] + b[2 * L:] + _bdot_tn(c[:2 * L, 2 * L:], d) for a, b, c, d in zip(t1, t2, m, drhs)]
    gg = [jnp.concatenate([a, b], axis=0).astype(BF16) for a, b in zip(drhs, dyw)]
    ds0 = [d + _bdot_tn(g, a) for d, g, a in zip(dzs, gg, x)]
    dm = [_bdot_nt(g, w) * keep for g, w in zip(gg, uv)]
    dx = [_bdot(g, s) + _bdot(d, b) for g, s, d, b in zip(gg, s0, dm, yk)]
    dyk = [_bdot_tn(d, a) + _bdot(w, z) for d, a, w, z in zip(dm, x, uv, dzs)]
    return dx, dyk, dv, dgl, ds0


STATE_SHAPE = (NPAIR * PW, PW)
M_SHAPE = (4 * L, NPAIR * 4 * L)
TINV_SHAPE = (2 * L, NPAIR * 2 * L)


def _rows_of(a, n):
    return [a[i * n:(i + 1) * n, :] for i in range(NPAIR)]


def _both(f):
    out = []
    for s in range(NSEQ):
        out += f(s)
    return out


def _seq_view(a):
    return a.reshape(NSEQ, SEQ, a.shape[-1])


UV_SHAPE = (4 * L, NPAIR * PW)
RW_CHUNKS = 2


def rwkv_fwd(r, lw, k2, v, aa, bb):
    def body(r_ref, lw_ref, k2_ref, v_ref, aa_ref, bb_ref, y_ref, hs_ref, hn_ref, m_ref, t_ref, uv_ref, state):
        @pl.when(pl.program_id(0) == 0)
        def _():
            state[...] = jnp.zeros_like(state)

        keep, tri, eye = _chunk_masks()
        where = [(j, s) for j in range(RW_CHUNKS) for s in range(NSEQ)]
        rows = lambda j: slice(j * L, (j + 1) * L)
        sc = [_scaled(r_ref[s, rows(j)], lw_ref[s, rows(j)], k2_ref[s, rows(j)], aa_ref[s, rows(j)],
                      bb_ref[s, rows(j)], tri) for j, s in where]
        ops = [_pairs_operands(*a[3:]) for a in sc]
        m, tinv = _pairs_matrices([a for o in ops for a in o[0]], [a for o in ops for a in o[1]], keep, eye)
        s_cur = [state[s] for s in range(NSEQ)]
        for j in range(RW_CHUNKS):
            mine = slice(j * NSEQ * NPAIR, (j + 1) * NSEQ * NPAIR)
            x = [a for o in ops[j * NSEQ:(j + 1) * NSEQ] for a in o[0]]
            yk = [a for o in ops[j * NSEQ:(j + 1) * NSEQ] for a in o[1]]
            vw = _both(lambda s: [_stack_pair(a) for a in _pairs(v_ref[s, rows(j)])])
            egl = _both(lambda s: _pairs(sc[j * NSEQ + s][0][L - 1:L, :]))
            y, sn, uv = _pairs_fwd(x, yk, m[mine], tinv[mine], vw, _both(lambda s: _rows_of(s_cur[s], PW)), egl)
            for s in range(NSEQ):
                ps = slice(s * NPAIR, (s + 1) * NPAIR)
                hs_ref[j, s] = s_cur[s]
                y_ref[s, rows(j)] = jnp.concatenate([_fold_pair(a) for a in y[ps]], axis=-1)
                m_ref[j, s] = jnp.concatenate(m[mine][ps], axis=-1)
                t_ref[j, s] = jnp.concatenate(tinv[mine][ps], axis=-1)
                uv_ref[j, s] = jnp.concatenate(uv[ps], axis=-1)
                s_cur[s] = jnp.concatenate(sn[ps], axis=0)
                hn_ref[j, s] = s_cur[s]
        for s in range(NSEQ):
            state[s] = s_cur[s]

    blk = pl.BlockSpec((NSEQ, RW_CHUNKS * L, W), lambda c: (0, c, 0))
    per_chunk = lambda shape: pl.BlockSpec((RW_CHUNKS, NSEQ) + shape, lambda c: (c, 0, 0, 0))
    saved_shapes = [(STATE_SHAPE, F32), (STATE_SHAPE, F32), (M_SHAPE, BF16), (TINV_SHAPE, BF16), (UV_SHAPE, BF16)]
    y, *saved = pl.pallas_call(
        body, grid=(NC // RW_CHUNKS,), name="rwkv_fwd",
        in_specs=[blk] * 6,
        out_specs=[blk] + [per_chunk(shape) for shape, _ in saved_shapes],
        out_shape=[jax.ShapeDtypeStruct((NSEQ, SEQ, W), F32)]
        + [jax.ShapeDtypeStruct((NC, NSEQ) + shape, dt) for shape, dt in saved_shapes],
        scratch_shapes=[pltpu.VMEM((NSEQ,) + STATE_SHAPE, F32)],
        compiler_params=_cparams(("arbitrary",), VMEM_BIG),
    )(*[_seq_view(a) for a in (r, lw, k2, v, aa, bb)])
    return y.reshape(T, W), saved


def rwkv_bwd(r, lw, k2, aa, bb, saved, dy):
    def body(r_ref, lw_ref, k2_ref, aa_ref, bb_ref, hs_ref, hn_ref, m_ref, t_ref, uv_ref, dy_ref,
             dr_ref, dlw_ref, dk2_ref, dv_ref, daa_ref, dbb_ref, dstate):
        @pl.when(pl.program_id(0) == 0)
        def _():
            dstate[...] = jnp.zeros_like(dstate)

        keep, tri, _ = _chunk_masks()
        sc = [_scaled(r_ref[s], lw_ref[s], k2_ref[s], aa_ref[s], bb_ref[s], tri) for s in range(NSEQ)]
        ops = [_pairs_operands(*sc[s][3:]) for s in range(NSEQ)]
        x, yk = _both(lambda s: ops[s][0]), _both(lambda s: ops[s][1])
        m = _both(lambda s: [m_ref[0, s][:, i * 4 * L:(i + 1) * 4 * L] for i in range(NPAIR)])
        tinv = _both(lambda s: [t_ref[0, s][:, i * 2 * L:(i + 1) * 2 * L] for i in range(NPAIR)])
        uv = _both(lambda s: _pairs(uv_ref[0, s]))
        dyw = _both(lambda s: [_stack_pair(a) for a in _pairs(dy_ref[s])])
        s0 = _both(lambda s: _rows_of(hs_ref[0, s], PW))
        sn = _both(lambda s: _rows_of(hn_ref[0, s], PW))
        dsn = _both(lambda s: _rows_of(dstate[s], PW))
        egl = _both(lambda s: _pairs(sc[s][0][L - 1:L, :]))
        dx, dyk, dvw, dgl, ds0 = _pairs_bwd(x, yk, m, tinv, uv, s0, sn, egl, dyw, dsn, keep)
        for s in range(NSEQ):
            mine = slice(s * NPAIR, (s + 1) * NPAIR)
            eg, eng, egp, at, rt, bt, kt = sc[s]
            dstate[s] = jnp.concatenate(ds0[mine], axis=0)
            dv_ref[s] = jnp.concatenate([_fold_pair(a) for a in dvw[mine]], axis=-1)
            dat = jnp.concatenate([_fold_pair(a[:2 * L]) for a in dx[mine]], axis=-1)
            drt = jnp.concatenate([_fold_pair(a[2 * L:]) for a in dx[mine]], axis=-1)
            dbt = jnp.concatenate([_fold_pair(a[:2 * L]) for a in dyk[mine]], axis=-1)
            dkt = jnp.concatenate([_fold_pair(a[2 * L:]) for a in dyk[mine]], axis=-1)
            dg = drt * rt - dbt * bt - dkt * kt
            dg = dg + jnp.where(_iota2(dg.shape, 0) == L - 1, jnp.concatenate(dgl[mine], axis=-1), 0.0)
            dgp = dat * at
            dlw_ref[s] = _hdot_tn(tri, dg + dgp) - dgp
            dr_ref[s] = drt * eg
            daa_ref[s] = dat * egp
            dbb_ref[s] = dbt * eng
            dk2_ref[s] = dkt * eng

    blk = pl.BlockSpec((NSEQ, L, W), lambda c: (0, NC - 1 - c, 0))
    per_chunk = lambda shape: pl.BlockSpec((1, NSEQ) + shape, lambda c: (NC - 1 - c, 0, 0, 0))
    outs = pl.pallas_call(
        body, grid=(NC,), name="rwkv_bwd",
        in_specs=[blk] * 5 + [per_chunk(a.shape[2:]) for a in saved] + [blk],
        out_specs=[blk] * 6,
        out_shape=[jax.ShapeDtypeStruct((NSEQ, SEQ, W), F32)] * 6,
        scratch_shapes=[pltpu.VMEM((NSEQ,) + STATE_SHAPE, F32)],
        compiler_params=_cparams(("arbitrary",)),
    )(*[_seq_view(a) for a in (r, lw, k2, aa, bb)], *saved, _seq_view(dy))
    return [a.reshape(T, W) for a in outs]


def _post_math(y, r, k2, v, ga, o, gb, lng, lnb, rk, bd):
    mu = _headsum(y, bd) * (1.0 / HD)
    yc = y - mu
    var = _headsum(yc * yc, bd) * (1.0 / HD)
    yn = yc * lax.rsqrt(var + GN_EPS) * lng + lnb
    bonus = _headsum(r * k2 * rk, bd) * v
    return (yn + bonus) * _silu(ga), o * _silu(gb)


def even_post(y, r, k2, v, ga, o, gb, lng, lnb, rk, h, w_bf):
    tm = 512

    def body(y_ref, r_ref, k2_ref, v_ref, ga_ref, o_ref, gb_ref, lng_ref, lnb_ref, rk_ref, h_ref, w_ref,
             ho_ref, zt_ref):
        ya, yb = _post_math(y_ref[...], r_ref[...], k2_ref[...], v_ref[...], ga_ref[...], o_ref[...], gb_ref[...],
                            lng_ref[...], lnb_ref[...], rk_ref[...], _head_blockdiag())
        z = jnp.concatenate([ya.astype(BF16), yb.astype(BF16)], axis=-1)
        zt_ref[...] = z.T
        ho_ref[...] = h_ref[...] + jnp.dot(z, w_ref[...], preferred_element_type=F32)

    vec = _const_spec((1, W))
    return pl.pallas_call(
        body, grid=(T // tm,), name="even_post",
        in_specs=[_row_spec(tm, W)] * 7 + [vec] * 3 + [_row_spec(tm, D), _const_spec((D, D))],
        out_specs=[_row_spec(tm, D), _col_spec(D, tm)],
        out_shape=[jax.ShapeDtypeStruct((T, D), F32), jax.ShapeDtypeStruct((D, T), BF16)],
        compiler_params=_cparams(("parallel",), VMEM_BIG),
    )(y, r, k2, v, ga, o, gb, lng, lnb, rk, h, w_bf)


def even_post_bwd(y, r, k2, v, ga, o, gb, lng, lnb, rk, dh, zt_bf, w_bf, after=None):
    tm = 512
    extra_specs, extra = _after_operand(after)

    def body(y_ref, r_ref, k2_ref, v_ref, ga_ref, o_ref, gb_ref, lng_ref, lnb_ref, rk_ref, dh_ref, zt_ref, w_ref,
             *rest):
        (dy_ref, dr_ref, dk2_ref, dv_ref, dga_ref, do_ref, dgb_ref, dlng_ref, dlnb_ref, drk_ref, dw_ref,
         acc_ref) = rest[-12:]
        dzv = _out_proj_back(dh_ref, zt_ref, w_ref, dw_ref, acc_ref)
        bd = _head_blockdiag()
        _, vjp = jax.vjp(lambda *a: _post_math(*a, bd), y_ref[...], r_ref[...], k2_ref[...], v_ref[...], ga_ref[...],
                         o_ref[...], gb_ref[...], lng_ref[...], lnb_ref[...], rk_ref[...])
        dy, dr, dk2, dv, dga, do, dgb, dlng, dlnb, drk = vjp((dzv[:, 0:W], dzv[:, W:2 * W]))
        for ref, val in ((dy_ref, dy), (dr_ref, dr), (dk2_ref, dk2), (dv_ref, dv), (dga_ref, dga), (do_ref, do),
                         (dgb_ref, dgb)):
            ref[...] = val.astype(ref.dtype)

        @pl.when(pl.program_id(0) == 0)
        def _():
            for ref in (dlng_ref, dlnb_ref, drk_ref):
                ref[...] = jnp.zeros_like(ref)

        dlng_ref[...] += dlng
        dlnb_ref[...] += dlnb
        drk_ref[...] += drk

    vec = _const_spec((1, W))
    return pl.pallas_call(
        body, grid=(T // tm,), name="even_post_bwd",
        in_specs=[_row_spec(tm, W)] * 7 + [vec] * 3 + [_row_spec(tm, D), _col_spec(D, tm), _const_spec((D, D))]
        + extra_specs,
        out_specs=[_row_spec(tm, W)] * 7 + [vec] * 3 + [_const_spec((D, D))],
        out_shape=[jax.ShapeDtypeStruct((T, W), dt) for dt in (F32, F32, F32, F32, BF16, F32, BF16)]
        + [jax.ShapeDtypeStruct((1, W), F32)] * 3 + [jax.ShapeDtypeStruct((D, D), BF16)],
        scratch_shapes=[pltpu.VMEM((D, D), F32)],
        compiler_params=_cparams(("arbitrary",), VMEM_BIG),
    )(y, r, k2, v, ga, o, gb, lng, lnb, rk, dh, zt_bf, w_bf, *extra)


PADSEQ = SEQ + LEFT * L
ATT_SCALE = 1.0 / math.sqrt(HD)
ATT_Q = 4
WIN = BAND + (ATT_Q - 1) * L
ATT_STEPS = NC // ATT_Q
ATT_BIAS_SHAPE = (NPAIR, ATT_Q * 2 * L, WIN)
ATT_WINDOW_BIAS_SHAPE = (ATT_Q, NPAIR, 2 * L, WIN)


def _stack_chunks(a):
    return jnp.concatenate([_stack_pair(a[i * L:(i + 1) * L]) for i in range(ATT_Q)], axis=0)


def _unstack_chunks(a):
    return jnp.concatenate([_unstack_pair(a[i * 2 * L:(i + 1) * 2 * L]) for i in range(ATT_Q)], axis=0)


def _window_bias(b_ref):
    return [jnp.concatenate([b_ref[c, p] for c in range(ATT_Q)], axis=0) for p in range(NPAIR)]


def _key_window(ref, step):
    start = step * (ATT_Q * L) - LEFT * L
    rows = ref[pl.ds(pl.multiple_of(jnp.maximum(start, 0), L), WIN), :]
    window = rows
    for lead in range(ATT_Q * L, LEFT * L + 1, ATT_Q * L):
        moved = jnp.concatenate([rows[WIN - lead:], rows[:WIN - lead]], axis=0)
        window = jnp.where(start == -lead, moved, window)
    return window


def _att_probs(q2, kw, bias, step):
    valid = _iota2((1, WIN), 1) >= (LEFT - step * ATT_Q) * L
    s = [jnp.where(valid, _bdot_nt(a, b) * ATT_SCALE + bias[p], NEG) for p, (a, b) in enumerate(zip(q2, kw))]
    e = [jnp.exp(a - jnp.max(a, axis=-1, keepdims=True)) for a in s]
    return [a / jnp.sum(a, axis=-1, keepdims=True) for a in e]


def attention_fwd(q, k, v, bias):
    def body(q_ref, k_ref, v_ref, b_ref, o_ref):
        step = pl.program_id(1)
        kw = _pairs(_key_window(k_ref, step))
        vw = _pairs(_key_window(v_ref, step))
        q2 = [_stack_chunks(a) for a in _pairs(q_ref[...])]
        p = _att_probs(q2, kw, _window_bias(b_ref), step)
        o_ref[...] = jnp.concatenate([_unstack_chunks(_bdot(a, b)) for a, b in zip(p, vw)], axis=-1)

    qblk = pl.BlockSpec((ATT_Q * L, W), lambda b, c: (b * ATT_STEPS + c, 0))
    kblk = pl.BlockSpec((SEQ, W), lambda b, c: (b, 0))
    return pl.pallas_call(
        body, grid=(NSEQ, ATT_STEPS), name="attention_fwd",
        in_specs=[qblk, kblk, kblk, _const_spec(ATT_WINDOW_BIAS_SHAPE)],
        out_specs=qblk, out_shape=jax.ShapeDtypeStruct((T, W), F32),
        compiler_params=_cparams(("parallel", "arbitrary")),
    )(q, k, v, bias)


def attention_bwd(q, k, v, bias, do):
    def body(q_ref, k_ref, v_ref, b_ref, do_ref, dq_ref, dko_ref, dvo_ref, db_ref, dk_ref, dv_ref):
        b = pl.program_id(0)
        c = pl.program_id(1)

        @pl.when(c == 0)
        def _():
            dk_ref[...] = jnp.zeros_like(dk_ref)
            dv_ref[...] = jnp.zeros_like(dv_ref)

        @pl.when((c == 0) & (b == 0))
        def _():
            db_ref[...] = jnp.zeros_like(db_ref)

        start = pl.multiple_of(c * (ATT_Q * L), L)
        kw = _pairs(_key_window(k_ref, c))
        vw = _pairs(_key_window(v_ref, c))
        q2 = [_stack_chunks(a) for a in _pairs(q_ref[...])]
        do2 = [_stack_chunks(a) for a in _pairs(do_ref[...].astype(BF16))]
        p = _att_probs(q2, kw, _window_bias(b_ref), c)
        dp = [_bdot_nt(a, b) for a, b in zip(do2, vw)]
        ds = [a * (d - jnp.sum(d * a, axis=-1, keepdims=True)) for a, d in zip(p, dp)]
        dss = [(a * ATT_SCALE).astype(BF16) for a in ds]
        dq_ref[...] = jnp.concatenate([_unstack_chunks(_bdot(a, b)) for a, b in zip(dss, kw)], axis=-1).astype(BF16)
        dk_ref[pl.ds(start, WIN), :] += jnp.concatenate([_bdot_tn(a, b) for a, b in zip(dss, q2)], axis=-1)
        dv_ref[pl.ds(start, WIN), :] += jnp.concatenate([_bdot_tn(a, b) for a, b in zip(p, do2)], axis=-1)
        for i in range(NPAIR):
            db_ref[i] += ds[i]

        @pl.when(c == ATT_STEPS - 1)
        def _():
            dko_ref[...] = dk_ref[LEFT * L:, :].astype(BF16)
            dvo_ref[...] = dv_ref[LEFT * L:, :].astype(BF16)

    qblk = pl.BlockSpec((ATT_Q * L, W), lambda b, c: (b * ATT_STEPS + c, 0))
    sblk = pl.BlockSpec((SEQ, W), lambda b, c: (b, 0))
    bblk = _const_spec(ATT_BIAS_SHAPE)
    return pl.pallas_call(
        body, grid=(NSEQ, ATT_STEPS), name="attention_bwd",
        in_specs=[qblk, sblk, sblk, _const_spec(ATT_WINDOW_BIAS_SHAPE), qblk],
        out_specs=[qblk, sblk, sblk, bblk],
        out_shape=[jax.ShapeDtypeStruct((T, W), BF16), jax.ShapeDtypeStruct((T, W), BF16),
                   jax.ShapeDtypeStruct((T, W), BF16), jax.ShapeDtypeStruct(ATT_BIAS_SHAPE, F32)],
        scratch_shapes=[pltpu.VMEM((PADSEQ, W), F32), pltpu.VMEM((PADSEQ, W), F32)],
        compiler_params=_cparams(("arbitrary", "arbitrary"), VMEM_BIG),
    )(q, k, v, bias, do)


NTAB = 2 * CLIP + 1
EXT = BAND + L


def _ext_onehot():
    n = _iota2((EXT, NTAB), 0)
    m = _iota2((EXT, NTAB), 1)
    return (jnp.clip(BAND - 1 - n, -CLIP, CLIP) + CLIP == m).astype(F32)


def bias_expand(table):
    def body(t_ref, o_ref):
        ext = _hdot_nt(t_ref[...], _ext_onehot())
        ext = jnp.concatenate([ext, jnp.zeros((NH, WIN - EXT), F32)], axis=-1)
        col = _iota2((L, WIN), 1)
        for h in range(NH):
            rows = jnp.broadcast_to(ext[h:h + 1], (L, WIN))
            for c in range(ATT_Q):
                inside = (col >= c * L) & (col < c * L + BAND)
                plane = pltpu.roll(rows, (c * L - (L - 1)) % WIN, 1, stride=1, stride_axis=0)
                o_ref[c, h] = jnp.where(inside, plane, NEG)

    out = pl.pallas_call(body, name="bias_expand", out_shape=jax.ShapeDtypeStruct((ATT_Q, NH, L, WIN), F32))(table)
    return out.reshape(ATT_WINDOW_BIAS_SHAPE)


def bias_grad(dbias):
    def body(d_ref, o_ref):
        acc = jnp.zeros((NH, EXT), F32)
        zpad = jnp.zeros((NH, EXT - BAND), F32)
        for i in range(L):
            s = L - 1 - i
            row = jnp.concatenate([d_ref[:, i, :], zpad], axis=-1)
            acc = acc + (pltpu.roll(row, s, 1) if s else row)
        o_ref[...] = _hdot(acc, _ext_onehot())

    return pl.pallas_call(body, name="bias_grad", out_shape=jax.ShapeDtypeStruct((NH, NTAB), F32))(dbias)


def _group_cols(g):
    return slice(g * SGC, (g + 1) * SGC)


def _sg_norm(gv, lng, lnb):
    gc = gv - jnp.mean(gv, axis=-1, keepdims=True)
    rstd = lax.rsqrt(jnp.mean(gc * gc, axis=-1, keepdims=True) + LN_EPS)
    xhat = gc * rstd
    return xhat, rstd, xhat * lng + lnb


GMLP_BWD_CHUNKS = 2


def gmlp_fwd_loss(u, v, gate, lng, lnb, wm_bf, sgb_t, h, w_bf, g_final, target):
    tm = GMLP_BWD_CHUNKS * SGC

    def body(u_ref, v_ref, gt_ref, lng_ref, lnb_ref, wm_ref, sb_ref, h_ref, w_ref, g_ref, t_ref,
             dh_ref, loss_ref, dg_ref, zt_ref):
        zs = []
        for ch in range(GMLP_BWD_CHUNKS):
            rows = slice(ch * SGC, (ch + 1) * SGC)
            _, _, vln = _sg_norm(_gelu(v_ref[rows, :]), lng_ref[...], lnb_ref[...])
            vlb = vln.astype(BF16)
            zg = []
            for g in range(NG):
                cs = _group_cols(g)
                sv = jnp.dot(wm_ref[g], vlb[:, cs], preferred_element_type=F32) + sb_ref[:, g:g + 1]
                zg.append((_gelu(u_ref[rows, cs]) * sv * _silu(gt_ref[rows, cs])).astype(BF16))
            zs.append(jnp.concatenate(zg, axis=-1))
        z = jnp.concatenate(zs, axis=0)
        zt_ref[...] = z.T
        xv = h_ref[...] + jnp.dot(z, w_ref[...], preferred_element_type=F32)
        rstd = lax.rsqrt(jnp.mean(xv * xv, axis=-1, keepdims=True) + RMS_EPS)
        xhat = xv * rstd
        err = xhat * g_ref[...] - t_ref[...]
        part = 0.5 * jnp.sum(jnp.mean(err * err, axis=-1, keepdims=True), axis=0, keepdims=True)
        dout = err * (1.0 / D)

        @pl.when(pl.program_id(0) == 0)
        def _():
            loss_ref[...] = jnp.zeros_like(loss_ref)
            dg_ref[...] = jnp.zeros_like(dg_ref)

        loss_ref[...] += jnp.broadcast_to(part, loss_ref.shape)
        dg_ref[...] += jnp.sum(dout * xhat, axis=0, keepdims=True)
        dxh = dout * g_ref[...]
        dh_ref[...] = rstd * (dxh - xhat * jnp.mean(dxh * xhat, axis=-1, keepdims=True))

    return pl.pallas_call(
        body, grid=(T // tm,), name="gmlp_fwd_loss",
        in_specs=[_row_spec(tm, D)] * 3 + [_const_spec((1, D))] * 2
        + [_const_spec((NG, SGC, SGC)), _const_spec((SGC, NG)), _row_spec(tm, D), _const_spec((D, D)),
           _const_spec((1, D)), _row_spec(tm, D)],
        out_specs=[_row_spec(tm, D), _const_spec((8, 128)), _const_spec((1, D)), _col_spec(D, tm)],
        out_shape=[jax.ShapeDtypeStruct((T, D), F32), jax.ShapeDtypeStruct((8, 128), F32),
                   jax.ShapeDtypeStruct((1, D), F32), jax.ShapeDtypeStruct((D, T), BF16)],
        compiler_params=_cparams(("arbitrary",), VMEM_BIG),
    )(u, v, gate, lng, lnb, wm_bf, sgb_t, h, w_bf, g_final, target)


def gmlp_bwd(u, v, gate, lng, lnb, wm_bf, sgb_t, dh, zt_bf, w_bf):
    def body(u_ref, v_ref, gt_ref, lng_ref, lnb_ref, wm_ref, sb_ref, dh_ref, zt_ref, w_ref,
             du_ref, dv_ref, dgt_ref, dlng_ref, dlnb_ref, dwm_ref, dsb_ref, dw_ref, acc_ref):
        @pl.when(pl.program_id(0) == 0)
        def _():
            for ref in (dlng_ref, dlnb_ref, dwm_ref, dsb_ref):
                ref[...] = jnp.zeros_like(ref)

        dz = _out_proj_back(dh_ref, zt_ref, w_ref, dw_ref, acc_ref)
        sel = (_iota2((D, NG), 0) // SGC == _iota2((D, NG), 1)).astype(F32)
        for ch in range(GMLP_BWD_CHUNKS):
            rows = slice(ch * SGC, (ch + 1) * SGC)
            gv, dgv_dv = _gelu_both(v_ref[rows, :])
            xhat, rstd, vln = _sg_norm(gv, lng_ref[...], lnb_ref[...])
            vlb = vln.astype(BF16)
            dvln = []
            dsv_all = []
            for g in range(NG):
                cs = _group_cols(g)
                uu = u_ref[rows, cs]
                gg = gt_ref[rows, cs]
                dzz = dz[rows, cs]
                sv = jnp.dot(wm_ref[g], vlb[:, cs], preferred_element_type=F32) + sb_ref[:, g:g + 1]
                gu, dgu = _gelu_both(uu)
                sg, dsg = _silu_both(gg)
                dzgu = dzz * gu
                dsv = dzgu * sg
                dgt_ref[rows, cs] = (dzgu * sv * dsg).astype(BF16)
                du_ref[rows, cs] = (dzz * sv * sg * dgu).astype(BF16)
                dsb16 = dsv.astype(BF16)
                dvln.append(lax.dot_general(wm_ref[g], dsb16, (((0,), (0,)), ((), ())), preferred_element_type=F32))
                dwm_ref[g] += lax.dot_general(dsb16, vlb[:, cs], (((1,), (1,)), ((), ())),
                                              preferred_element_type=F32)
                dsv_all.append(dsv)
            dvl = jnp.concatenate(dvln, axis=-1)
            dsb_ref[...] += _hdot(jnp.concatenate(dsv_all, axis=-1), sel)
            dlng_ref[...] += jnp.sum(dvl * xhat, axis=0, keepdims=True)
            dlnb_ref[...] += jnp.sum(dvl, axis=0, keepdims=True)
            dxh = dvl * lng_ref[...]
            dgv = rstd * (dxh - jnp.mean(dxh, axis=-1, keepdims=True)
                          - xhat * jnp.mean(dxh * xhat, axis=-1, keepdims=True))
            dv_ref[rows, :] = (dgv * dgv_dv).astype(BF16)

    tm = GMLP_BWD_CHUNKS * SGC
    return pl.pallas_call(
        body, grid=(T // tm,), name="gmlp_bwd",
        in_specs=[_row_spec(tm, D)] * 3 + [_const_spec((1, D))] * 2
        + [_const_spec((NG, SGC, SGC)), _const_spec((SGC, NG)), _row_spec(tm, D), _col_spec(D, tm),
           _const_spec((D, D))],
        out_specs=[_row_spec(tm, D)] * 3 + [_const_spec((1, D))] * 2
        + [_const_spec((NG, SGC, SGC)), _const_spec((SGC, NG)), _const_spec((D, D))],
        out_shape=[jax.ShapeDtypeStruct((T, D), BF16)] * 3 + [jax.ShapeDtypeStruct((1, D), F32)] * 2
        + [jax.ShapeDtypeStruct((NG, SGC, SGC), F32), jax.ShapeDtypeStruct((SGC, NG), F32),
           jax.ShapeDtypeStruct((D, D), BF16)],
        scratch_shapes=[pltpu.VMEM((D, D), F32)],
        compiler_params=_cparams(("arbitrary",), VMEM_BIG),
    )(u, v, gate, lng, lnb, wm_bf, sgb_t, dh, zt_bf, w_bf)


NCHIP = 4
NDEV = 8
ANY = pl.BlockSpec(memory_space=pl.ANY)


HBM = pl.BlockSpec(memory_space=pltpu.HBM)
SEM = pl.BlockSpec(memory_space=pltpu.SEMAPHORE)
EFFECT = pltpu.SideEffectType.DATAFLOW_SIDE_EFFECTING


CHIPS, EVERY, SIBLING = "chips", "every", "sibling"
SLOTS = {CHIPS: NCHIP, EVERY: NDEV, SIBLING: 1}


def _peers(scope):
    x, y, c = lax.axis_index("x"), lax.axis_index("y"), lax.axis_index("c")
    if scope == SIBLING:
        return [((x, y, 1 - c), 0)], 0
    if scope == CHIPS:
        return [((px, py, c), 2 * px + py) for px, py in ((1 - x, y), (x, 1 - y), (1 - x, 1 - y))], 2 * x + y
    out = []
    for j in range(1, NDEV):
        px, py, pc = x ^ (j >> 2), y ^ ((j >> 1) & 1), c ^ (j & 1)
        out.append(((px, py, pc), 4 * px + 2 * py + pc))
    return out, 4 * x + 2 * y + c


def _send_copies(src, land, send, recv, scatter, scope, starting):
    peers, me = _peers(scope)
    copies = []
    for t in range(len(src)):
        for j, (dev, slot) in enumerate(peers):
            k = t * len(peers) + j
            copies.append(pltpu.make_async_remote_copy(
                src_ref=src[t].at[slot] if scatter else src[t], dst_ref=land[t].at[me if starting else slot],
                send_sem=send.at[k], recv_sem=recv.at[k], device_id=dev, device_id_type=MESH))
    return copies


def _own_copies(src, land, sems, scatter, scope):
    if scope == SIBLING:
        return []
    _, me = _peers(scope)
    return [pltpu.make_async_copy(src[t].at[me] if scatter else src[t], land[t].at[me], sems.at[t])
            for t in range(len(src))]


def send_start(srcs, scatter, scope, name, after=None):
    n = len(srcs)
    slots = SLOTS[scope]
    extra_specs, extra = _after_operand(after)
    lands = [pltpu.HBM(a.shape if scatter else (slots,) + a.shape, a.dtype) for a in srcs]
    sems = [pltpu.SemaphoreType.DMA((n * max(slots - 1, 1),))] * 2 + ([] if scope == SIBLING else
                                                                     [pltpu.SemaphoreType.DMA((n,))])
    k = len(sems)

    def body(*refs):
        first_out = n + len(extra)
        src, land = refs[:n], refs[first_out + k + n:first_out + k + 2 * n]
        for cp in _send_copies(src, land, refs[first_out], refs[first_out + 1], scatter, scope, True):
            cp.start()
        for cp in _own_copies(src, land, refs[first_out + k - 1], scatter, scope):
            cp.start(priority=1)
        refs[-1][...] = jnp.zeros_like(refs[-1])

    out = pl.pallas_call(
        body, name=name,
        out_shape=(*sems, *[pltpu.HBM(a.shape, a.dtype) for a in srcs], *lands, jax.ShapeDtypeStruct((8, 128), F32)),
        in_specs=[HBM] * n + extra_specs,
        out_specs=(*[SEM] * k, *[HBM] * (2 * n), pl.BlockSpec(memory_space=pltpu.VMEM)),
        input_output_aliases={i: k + i for i in range(n)},
        compiler_params=pltpu.CompilerParams(has_side_effects=EFFECT),
    )(*[pltpu.with_memory_space_constraint(a, pltpu.HBM) for a in srcs], *extra)
    return list(out[:k]), list(out[k:k + n]), list(out[k + n:k + 2 * n]), out[-1]


def send_wait(started, after, scatter, scope, name, with_sources=False, only=None):
    sems, srcs, lands, _ = started
    n, k = len(srcs), len(sems)
    wanted = range(n) if only is None else only

    def body(*refs):
        src, land = refs[:n], refs[n:2 * n]
        for t, cp in enumerate(_own_copies(src, land, refs[2 * n + k - 1], scatter, scope)):
            if t in wanted:
                cp.wait()
        copies = _send_copies(src, land, refs[2 * n], refs[2 * n + 1], scatter, scope, False)
        for i, cp in enumerate(copies):
            if i // (len(copies) // n) in wanted:
                cp.wait_send()
                cp.wait_recv()

    arrs = list(srcs) + list(lands)
    out = pl.pallas_call(
        body, name=name, out_shape=tuple(pltpu.HBM(a.shape, a.dtype) for a in arrs),
        in_specs=[HBM] * (2 * n) + [SEM] * k + [ANY], out_specs=tuple([HBM] * (2 * n)),
        input_output_aliases={i: i for i in range(2 * n)},
        compiler_params=pltpu.CompilerParams(has_side_effects=EFFECT),
    )(*arrs, *sems, after)
    return (list(out[:n]), list(out[n:])) if with_sources else list(out[n:])


def gather_weights(arrs, split):
    n = len(arrs)

    def body(*refs):
        ins, outs = refs[:n], refs[n:2 * n]
        send1, recv1, send2, recv2, loc_in, loc_out = refs[2 * n:2 * n + 6]
        staged = refs[2 * n + 6:]
        x, y, c = lax.axis_index("x"), lax.axis_index("y"), lax.axis_index("c")
        me = 2 * x + y
        sibling = (x, y, 1 - c)
        peers = [(1 - x, y), (x, 1 - y), (1 - x, 1 - y)]

        def rows_of(t, core):
            half = arrs[t].shape[0] // 2
            return pl.ds(core * half, half)

        def part(ref, t, core):
            return ref.at[rows_of(t, core)] if split[t] else ref

        load = [pltpu.make_async_copy(ins[t], staged[t], loc_in.at[t]) for t in range(n)]
        store = [pltpu.make_async_copy(staged[t], outs[t].at[me], loc_out.at[t]) for t in range(n)]
        for cp in load:
            cp.start(priority=1)
        first = []
        for t in range(n):
            for j, (px, py) in enumerate(peers):
                first.append(pltpu.make_async_remote_copy(
                    src_ref=part(ins[t], t, c), dst_ref=part(outs[t].at[me], t, c), send_sem=send1.at[t, j],
                    recv_sem=recv1.at[t, j], device_id=(px, py, c), device_id_type=MESH))
        for cp in first:
            cp.start()
        for cp_in, cp_out in zip(load, store):
            cp_in.wait()
            cp_out.start(priority=1)
        passed = []
        for t in range(n):
            for j, (px, py) in enumerate(peers):
                landed = part(outs[t].at[2 * px + py], t, c)
                pltpu.make_async_remote_copy(
                    src_ref=landed, dst_ref=landed, send_sem=send1.at[t, j], recv_sem=recv1.at[t, j],
                    device_id=(x, y, c), device_id_type=MESH).wait_recv()
                if split[t]:
                    cp = pltpu.make_async_remote_copy(
                        src_ref=landed, dst_ref=landed, send_sem=send2.at[t, j], recv_sem=recv2.at[t, j],
                        device_id=sibling, device_id_type=MESH)
                    cp.start()
                    passed.append(cp)
        for t in range(n):
            for j, (px, py) in enumerate(peers):
                if split[t]:
                    other = part(outs[t].at[2 * px + py], t, 1 - c)
                    pltpu.make_async_remote_copy(
                        src_ref=other, dst_ref=other, send_sem=send2.at[t, j], recv_sem=recv2.at[t, j],
                        device_id=(x, y, c), device_id_type=MESH).wait_recv()
        for cp in first + passed:
            cp.wait_send()
        for cp in store:
            cp.wait()

    return pl.pallas_call(
        body, name="gather_weights", in_specs=[ANY] * n, out_specs=[ANY] * n,
        out_shape=[jax.ShapeDtypeStruct((NCHIP,) + a.shape, a.dtype) for a in arrs],
        scratch_shapes=[pltpu.SemaphoreType.DMA((n, 3))] * 4 + [pltpu.SemaphoreType.DMA((n,))] * 2
        + [pltpu.VMEM(a.shape, a.dtype) for a in arrs],
    )(*arrs)


def _adam_math(g, w, m, v):
    m = ADAM_B1 * m + (1.0 - ADAM_B1) * g
    v = ADAM_B2 * v + (1.0 - ADAM_B2) * (g * g)
    m_hat = m / (1.0 - ADAM_B1 ** ADAM_STEP)
    v_hat = v / (1.0 - ADAM_B2 ** ADAM_STEP)
    delta = -ADAM_LR * (m_hat / (jnp.sqrt(v_hat) + ADAM_EPS) + ADAM_WD * w)
    return delta, m, v


def _rows_tile(rows):
    return rows if rows <= 256 else 256


def sum_chips(parts, name):
    _, rows, cols = parts.shape
    tr = _rows_tile(rows)

    def body(p_ref, o_ref):
        acc = p_ref[0].astype(F32)
        for s in range(1, NCHIP):
            acc = acc + p_ref[s].astype(F32)
        o_ref[...] = acc

    return pl.pallas_call(
        body, grid=(rows // tr,), name=name,
        in_specs=[pl.BlockSpec((NCHIP, tr, cols), lambda i: (0, i, 0))],
        out_specs=pl.BlockSpec((tr, cols), lambda i: (i, 0)),
        out_shape=jax.ShapeDtypeStruct((rows, cols), F32),
        compiler_params=_cparams(("parallel",)),
    )(parts)


def sum_chips_small(parts, name):
    n = len(parts)

    def body(*refs):
        for p_ref, o_ref in zip(refs[:n], refs[n:]):
            acc = p_ref[0]
            for s in range(1, NCHIP):
                acc = acc + p_ref[s]
            o_ref[...] = acc

    return pl.pallas_call(body, name=name, out_shape=[jax.ShapeDtypeStruct(p.shape[1:], F32) for p in parts])(*parts)


def adam_shard_small(items, name):
    n = len(items)

    def body(*refs):
        for t in range(n):
            a_ref, b_ref, w_ref, m_ref, v_ref = refs[5 * t:5 * t + 5]
            g_ref, d_ref, mo_ref, vo_ref = refs[5 * n + 4 * t:5 * n + 4 * t + 4]
            g = (a_ref[...] + b_ref[...]).reshape(w_ref.shape)
            g_ref[...] = g
            d_ref[...], mo_ref[...], vo_ref[...] = _adam_math(g, w_ref[...], m_ref[...], v_ref[...])

    out = pl.pallas_call(
        body, name=name, out_shape=[jax.ShapeDtypeStruct(it[2].shape, F32) for it in items for _ in range(4)],
    )(*[a for it in items for a in it])
    return [out[4 * t:4 * t + 4] for t in range(n)]


def adam_shard(p_mine, p_sib, w, m, v, name):
    rows, cols = p_mine.shape
    tr = _rows_tile(rows)
    lead = w.ndim == 3

    def body(a_ref, b_ref, w_ref, m_ref, v_ref, g_ref, d_ref, mo_ref, vo_ref):
        g = a_ref[...] + b_ref[...]
        g = g[None] if lead else g
        g_ref[...] = g
        d_ref[...], mo_ref[...], vo_ref[...] = _adam_math(g, w_ref[...], m_ref[...], v_ref[...])

    flat = pl.BlockSpec((tr, cols), lambda i: (i, 0))
    spec = pl.BlockSpec((1, tr, cols), lambda i: (0, i, 0)) if lead else flat
    return pl.pallas_call(
        body, grid=(rows // tr,), name=name, in_specs=[flat] * 2 + [spec] * 3, out_specs=[spec] * 4,
        out_shape=[jax.ShapeDtypeStruct(w.shape, F32)] * 4,
        compiler_params=_cparams(("parallel",)),
    )(p_mine, p_sib, w, m, v)


def adam_shard_halves_t(r_mine, r_sib, wt, mt, vt, name):
    hrows, cols = r_mine.shape
    tr = _rows_tile(hrows)
    per_half = hrows // tr

    def body(a_ref, b_ref, w_ref, m_ref, v_ref, g_ref, d_ref, mo_ref, vo_ref):
        mine = pl.program_id(0) == lax.axis_index("c")
        g = jnp.where(mine, a_ref[...], b_ref[...]).T[None]
        g_ref[...] = g
        d_ref[...], mo_ref[...], vo_ref[...] = _adam_math(g, w_ref[...], m_ref[...], v_ref[...])

    flat = pl.BlockSpec((tr, cols), lambda h, i: (i, 0))
    spec = pl.BlockSpec((1, cols, tr), lambda h, i: (0, 0, h * per_half + i))
    return pl.pallas_call(
        body, grid=(2, per_half), name=name, in_specs=[flat] * 2 + [spec] * 3, out_specs=[spec] * 4,
        out_shape=[jax.ShapeDtypeStruct(wt.shape, F32)] * 4,
        compiler_params=_cparams(("parallel", "parallel")),
    )(r_mine, r_sib, wt, mt, vt)


def adam_replicated(gathered, params, name):
    flat = []
    for i, p in enumerate(params):
        if isinstance(p, list):
            off = 0
            for wmv in p:
                n = gathered[i].shape[-1] - off if wmv[0] is None else wmv[0].shape[-1]
                flat.append((i, (off, n), wmv))
                off += n
        else:
            flat.append((i, None, p))
    ins = [a for _, _, wmv in flat for a in wmv if a is not None]
    ng = len(gathered)

    def body(*refs):
        g_refs = refs[:ng]
        in_refs = list(refs[ng:ng + len(ins)])
        out_refs = list(refs[ng + len(ins):])
        sums = []
        for r in g_refs:
            g = r[0]
            for d in range(1, NDEV):
                g = g + r[d]
            sums.append(g)
        for i, lanes, wmv in flat:
            g = sums[i] if lanes is None else sums[i][:, lanes[0]:lanes[0] + lanes[1]]
            out_refs.pop(0)[...] = g
            if wmv[0] is not None:
                w_ref, m_ref, v_ref = in_refs.pop(0), in_refs.pop(0), in_refs.pop(0)
                d_ref, mo_ref, vo_ref = out_refs.pop(0), out_refs.pop(0), out_refs.pop(0)
                d_ref[...], mo_ref[...], vo_ref[...] = _adam_math(g, w_ref[...], m_ref[...], v_ref[...])

    out_shape = []
    for i, lanes, wmv in flat:
        shape = gathered[i].shape[1:] if lanes is None else (1, lanes[1])
        out_shape += [jax.ShapeDtypeStruct(shape, F32)] * (4 if wmv[0] is not None else 1)
    outs = list(pl.pallas_call(body, name=name, out_shape=out_shape)(*gathered, *ins))
    return [[outs.pop(0) for _ in range(4 if wmv[0] is not None else 1)] for _, _, wmv in flat]


EVEN_SPLITS = (SHIFT, W, W, W, W, W)
ODD_SPLITS = (D, D, D)


def _cols_to_chips(a):
    rows, cols = a.shape
    return a.reshape(rows, NCHIP, cols // NCHIP).transpose(1, 0, 2)


def _chips_to_cols(a):
    _, rows, n = a.shape
    return a.transpose(1, 0, 2).reshape(rows, NCHIP * n)


def kernel(x, norm_g, w_in_e, shift_mu, rw_w0, rw_w2, rw_a0, rw_a2, rw_kk, rw_ka, rw_rk, rw_lnx_g, rw_lnx_b, att_bias, w_out_e, w_in_o, sg_ln_g, sg_ln_b, sg_w, sg_b, w_out_o, final_g, loss_target, m_norm_g, m_w_in_e, m_shift_mu, m_rw_w0, m_rw_w2, m_rw_a0, m_rw_a2, m_rw_kk, m_rw_ka, m_rw_rk, m_rw_lnx_g, m_rw_lnx_b, m_att_bias, m_w_out_e, m_w_in_o, m_sg_ln_g, m_sg_ln_b, m_sg_w, m_sg_b, m_w_out_o, m_final_g, v_norm_g, v_w_in_e, v_shift_mu, v_rw_w0, v_rw_w2, v_rw_a0, v_rw_a2, v_rw_kk, v_rw_ka, v_rw_rk, v_rw_lnx_g, v_rw_lnx_b, v_att_bias, v_w_out_e, v_w_in_o, v_sg_ln_g, v_sg_ln_b, v_sg_w, v_sg_b, v_w_out_o, v_final_g):
    x2 = x.reshape(T, D)
    tgt = loss_target.reshape(T, D)

    gathered = gather_weights(
        [jnp.swapaxes(w_in_e[0], 0, 1).astype(BF16), jnp.concatenate([rw_w2[0], rw_a2[0]], axis=0),
         jnp.concatenate([sg_ln_g, sg_ln_b], axis=0)], [True, True, False])
    wie = gathered[0].reshape(EVEN_IN, D)
    w2 = _chips_to_cols(gathered[1][:, :LORA])
    a2 = _chips_to_cols(gathered[1][:, LORA:])
    sglg = _chips_to_cols(gathered[2][:, 0:1])
    sglb = _chips_to_cols(gathered[2][:, 1:2])

    late = [w_out_e[0].astype(BF16), w_in_o[0].astype(BF16), w_out_o[0].astype(BF16)]
    late_started = send_start(late, False, CHIPS, "late_weights_start", after=gathered[0])

    late_state = {}

    def late_weights(layer, after):
        if layer == "even":
            srcs, lands = send_wait(late_started, after, False, CHIPS, "late_w_out_e_wait", True, only=(0,))
            late_state["rest"] = (late_started[0], srcs, lands, None)
            return lands[0].reshape(D, D)
        woe, wio, woo = send_wait(late_state["rest"], after, False, CHIPS, "late_weights_wait", only=(1, 2))
        return woe.reshape(D, D), wio, woo.reshape(D, D)

    def scatter_start(grads, name):
        return send_start([g_.astype(BF16) if g_.shape[-1] >= W else g_ for g_ in grads], True, CHIPS, name)

    started = {}

    def on_odd_grads(d_woo, d_wio):
        started["odd"] = scatter_start([d_woo.reshape(NCHIP, D // NCHIP, D), d_wio], "odd_grads_start")
        return started["odd"][-1]

    def on_even_grads(big_g):
        d_wie_half, d_woe, _, _, d_w2, d_a2, d_sglg, d_sglb = big_g
        blocks = [d_wie_half, d_woe.reshape(NCHIP, D // NCHIP, D), _cols_to_chips(d_w2), _cols_to_chips(d_a2),
                  _cols_to_chips(d_sglg), _cols_to_chips(d_sglb)]
        started["even"] = scatter_start(blocks, "even_grads_start")
        return started["even"][-1]

    def on_small_grads(layer, grads):
        if layer == "odd":
            d_sg_w, d_sg_b, d_final, d_g1 = grads
            mine = [d_sg_w.reshape(NG * SGC, SGC), d_sg_b, jnp.concatenate([d_final, d_g1], axis=1)]
        else:
            mine = [grads[-2], jnp.concatenate(grads[:-2] + grads[-1:], axis=1)]
        started[layer + "_small"] = send_start(mine, False, EVERY, layer + "_small_grads_start")
        return started[layer + "_small"][-1]

    loss_part, dx, _, _ = _local_step(
        x2, tgt, wie, late_weights, w2, a2, sglg, sglb, norm_g, shift_mu, rw_w0, rw_a0, rw_kk, rw_ka, rw_rk,
        rw_lnx_g, rw_lnx_b, att_bias, sg_w, sg_b, final_g, first_after=late_started[-1], on_odd_grads=on_odd_grads,
        on_even_grads=on_even_grads, on_small_grads=on_small_grads)
    wmv = {"w_in_e": tuple(jnp.swapaxes(a, 1, 2) for a in (w_in_e, m_w_in_e, v_w_in_e)),
           "w_out_e": (w_out_e, m_w_out_e, v_w_out_e),
           "w_in_o": (w_in_o, m_w_in_o, v_w_in_o), "w_out_o": (w_out_o, m_w_out_o, v_w_out_o),
           "rw_w2": (rw_w2, m_rw_w2, v_rw_w2), "rw_a2": (rw_a2, m_rw_a2, v_rw_a2),
           "sg_ln_g": (sg_ln_g, m_sg_ln_g, v_sg_ln_g), "sg_ln_b": (sg_ln_b, m_sg_ln_b, v_sg_ln_b)}
    sharded = {}

    def sum_and_swap(names, landed, tag):
        nbig = sum(p_.dtype == BF16 for p_ in landed)
        partial = [sum_chips(p_, "sum_" + nm) for p_, nm in zip(landed[:nbig], names)]
        if nbig < len(names):
            partial += sum_chips_small(landed[nbig:], "sum_small_" + tag)
        return send_start(partial, False, SIBLING, "swap_partials_" + tag + "_start")

    def update(names, swap_started, after, tag):
        partial, landed = send_wait(swap_started, after, False, SIBLING, "swap_partials_" + tag + "_wait", True)
        from_sibling = [a[0] for a in landed]
        nbig = sum(p_.shape[-1] >= W for p_ in partial)
        for nm, mine, sib in zip(names[:nbig], partial, from_sibling):
            if nm == "w_in_e":
                res = adam_shard_halves_t(mine, sib, *wmv[nm], "adam_" + nm)
                sharded[nm] = [jnp.swapaxes(a, 1, 2) for a in res]
            else:
                sharded[nm] = adam_shard(mine, sib, *wmv[nm], "adam_" + nm)
        if nbig < len(names):
            items = [(mine, sib, *wmv[nm]) for nm, mine, sib in zip(names, partial, from_sibling)][nbig:]
            for nm, res in zip(names[nbig:], adam_shard_small(items, "adam_small_" + tag)):
                sharded[nm] = res

    odd_names = ["w_out_o", "w_in_o"]
    even_names = ["w_in_e", "w_out_e", "rw_w2", "rw_a2", "sg_ln_g", "sg_ln_b"]
    odd_landed = send_wait(started["odd"], started["even_small"][-1], True, CHIPS, "odd_grads_wait")
    odd_swap = sum_and_swap(odd_names, odd_landed, "odd")
    done = odd_swap[-1]

    def wmv_of(*arrs, view=lambda a: a):
        return tuple(view(a) for a in arrs)

    vec = lambda a: a.reshape(1, -1)
    groups = {
        "odd": (["sg_w", "sg_b", "final_g", "norm_g1"],
                [wmv_of(sg_w, m_sg_w, v_sg_w, view=lambda a: a.reshape(NG * SGC, SGC)),
                 wmv_of(sg_b, m_sg_b, v_sg_b, view=lambda a: a[0]),
                 [wmv_of(final_g, m_final_g, v_final_g, view=vec),
                  wmv_of(norm_g, m_norm_g, v_norm_g, view=lambda a: a[1:2])]]),
        "even": (["att_bias", "norm_g0", "shift_mu", "rw_w0", "rw_a0", "rw_kk", "rw_ka", "rw_rk", "rw_lnx_g",
                  "rw_lnx_b", "loss"],
                 [wmv_of(att_bias, m_att_bias, v_att_bias, view=lambda a: a[0]),
                  [wmv_of(norm_g, m_norm_g, v_norm_g, view=lambda a: a[0:1]),
                   wmv_of(shift_mu, m_shift_mu, v_shift_mu), wmv_of(rw_w0, m_rw_w0, v_rw_w0),
                   wmv_of(rw_a0, m_rw_a0, v_rw_a0), wmv_of(rw_kk, m_rw_kk, v_rw_kk), wmv_of(rw_ka, m_rw_ka, v_rw_ka),
                   wmv_of(rw_rk, m_rw_rk, v_rw_rk, view=vec), wmv_of(rw_lnx_g, m_rw_lnx_g, v_rw_lnx_g),
                   wmv_of(rw_lnx_b, m_rw_lnx_b, v_rw_lnx_b), (None, None, None)]]),
    }
    rep = {}
    for layer in ("odd", "even"):
        nms, params = groups[layer]
        gathered_g = send_wait(started[layer + "_small"], done, False, EVERY, layer + "_small_grads_wait")
        for nm, res in zip(nms, adam_replicated(gathered_g, params, "adam_" + layer + "_small")):
            rep[nm] = res
        done = rep[nms[0]][0]
    native = {"sg_w": sg_w.shape, "sg_b": sg_b.shape, "final_g": final_g.shape, "rw_rk": rw_rk.shape,
              "att_bias": att_bias.shape}
    for nm, shape in native.items():
        rep[nm] = [a.reshape(shape) for a in rep[nm]]
    rep["norm_g"] = [jnp.concatenate([a, b], axis=0) for a, b in zip(rep["norm_g0"], rep["norm_g1"])]
    even_landed = send_wait(started["even"], done, True, CHIPS, "even_grads_wait")
    even_swap = sum_and_swap(even_names, even_landed, "even")
    update(odd_names, odd_swap, even_swap[-1], "odd")
    update(even_names, even_swap, sharded["w_in_o"][0], "even")

    order = ["norm_g", "w_in_e", "shift_mu", "rw_w0", "rw_w2", "rw_a0", "rw_a2", "rw_kk", "rw_ka", "rw_rk",
             "rw_lnx_g", "rw_lnx_b", "att_bias", "w_out_e", "w_in_o", "sg_ln_g", "sg_ln_b", "sg_w", "sg_b",
             "w_out_o", "final_g"]
    results = {**sharded, **rep}
    outs = [rep["loss"][0][0, 0], dx.reshape(NSEQ, SEQ, D)]
    for kind in range(4):
        outs += [results[nm][kind] for nm in order]
    return tuple(outs)


def _local_step(x2, tgt, wie_t, late_weights, w2, a2, sglg, sglb, norm_g, shift_mu, rw_w0, rw_a0, rw_kk, rw_ka, rw_rk,
                rw_lnx_g, rw_lnx_b, att_bias, sg_w, sg_b, final_g, first_after=None, on_odd_grads=None,
                on_even_grads=None, on_small_grads=None):
    zl = jnp.zeros((LORA, W), F32)
    w2x = jnp.concatenate([w2, zl], axis=0)
    a2x = jnp.concatenate([zl, a2], axis=0)
    rk = rw_rk.reshape(1, W)
    pos = np.arange(SGC)
    sg_mask = jnp.asarray(((pos[None, :] // L) <= (pos[:, None] // L)).astype(np.float32))
    wm = (sg_w[0] * sg_mask[None]).astype(BF16)
    sgb_t = sg_b[0].T

    xn0, ps, ga, q, kb, vb, gb = ln_in_proj(x2, norm_g[0:1], wie_t, EVEN_SPLITS, "in_proj_even", after=first_after,
                                            w_t=True, bf16_pieces=(2, 3, 4))
    r, lw, k2, v, aa, bb = even_prep(ps, shift_mu, rw_w0, w2x, rw_a0, a2x, rw_kk, rw_ka)
    y, rw_saved = rwkv_fwd(r, lw, k2, v, aa, bb)
    bias = bias_expand(att_bias[0])

    o = attention_fwd(q, kb, vb, bias)
    woe = late_weights("even", o)
    h1, zt = even_post(y, r, k2, v, ga, o, gb, rw_lnx_g, rw_lnx_b, rk, x2, woe)
    woe, wio, woo = late_weights("odd", h1)
    xn1, u, vv, gt = ln_in_proj(h1, norm_g[1:2], wio, ODD_SPLITS, "in_proj_odd")
    dh2, loss_part, d_final_g, z2t = gmlp_fwd_loss(u, vv, gt, sglg, sglb, wm, sgb_t, h1, woo, final_g[None], tgt)

    du, dvv, dgt, d_sglg, d_sglb, d_wm, d_sgb_t, d_woo = gmlp_bwd(u, vv, gt, sglg, sglb, wm, sgb_t, dh2, z2t, woo)
    dp_odd = [du, dvv, dgt]
    d_wio = matmul_acc_chips(xn1, dp_odd, "in_proj_odd_dw")
    token = on_odd_grads(d_woo, d_wio) if on_odd_grads else None
    dh1, d_g1 = in_proj_bwd_x(h1, norm_g[1:2], wio, dp_odd, dh2, "in_proj_odd_bwd", after=token)
    odd_small = [d_wm * sg_mask[None], d_sgb_t.T, d_final_g, d_g1]
    token = on_small_grads("odd", odd_small) if on_small_grads else None
    dy, dr2, dk22, dv2, dga, do, dgb, d_lng, d_lnb, d_rk, d_woe = even_post_bwd(
        y, r, k2, v, ga, o, gb, rw_lnx_g, rw_lnx_b, rk, dh1, zt, woe, after=token)
    dq, dkb, dvb, dbias = attention_bwd(q, kb, vb, bias, do)
    dbias = sum(dbias[:, i * 2 * L:(i + 1) * 2 * L, i * L:i * L + BAND] for i in range(ATT_Q))
    d_att_bias = bias_grad(dbias.reshape(NH, L, BAND))
    dr, dlw, dk2, dv, daa, dbb = rwkv_bwd(r, lw, k2, aa, bb, rw_saved, dy)
    dps, d_mu, d_w0, d_w2x, d_a0, d_a2x, d_kk, d_ka = even_prep_bwd(
        ps, shift_mu, rw_w0, w2x, rw_a0, a2x, rw_kk, rw_ka, dr, dlw, dk2, dv, daa, dbb, dr2, dk22, dv2)
    dp_even = [dps, dga, dq, dkb, dvb, dgb]
    d_wie = matmul_acc_chips(xn0, dp_even, "in_proj_even_dw", add_cores=on_even_grads is not None)
    big_g = (d_wie, d_woe, d_wio, d_woo, d_w2x[:LORA], d_a2x[LORA:], d_sglg, d_sglb)
    token = on_even_grads(big_g) if on_even_grads else None
    dx, d_g0 = in_proj_bwd_x(x2, norm_g[0:1], wie_t, dp_even, dh1, "in_proj_even_bwd", after=token, w_t=True)
    even_small = [d_g0, d_mu, d_w0, d_a0, d_kk, d_ka, d_rk, d_lng, d_lnb, d_att_bias]
    if on_small_grads:
        on_small_grads("even", even_small + [loss_part[0:1, :]])
    rep_g = [jnp.concatenate([d_g0, d_g1], axis=0)] + even_small[1:] + odd_small[:3]
    return loss_part[0, 0], dx, big_g, rep_g
```
